```python
import jax, jax.numpy as jnp
from jax import lax
import numpy as np

D_MODEL = 2048
BATCH = 8
SEQ = 2048
DEPTH = 1

FOX_HEAD_DIM = 128
FOX_WIDTH = D_MODEL // 2
FOX_HEADS = FOX_WIDTH // FOX_HEAD_DIM
RWKV_HEAD_DIM = 64
RWKV_WIDTH = D_MODEL // 2
RWKV_HEADS = RWKV_WIDTH // RWKV_HEAD_DIM
DECAY_LORA = max(32, int(round(D_MODEL ** 0.5 * 1.8 / 32)) * 32)
AAA_LORA = max(32, int(round(D_MODEL ** 0.5 * 1.8 / 32)) * 32)
Q_BLOCK = 128
RMS_EPS = 1e-6
GN_EPS = 64e-5
L2_EPS = 1e-12

FOX_SIZES = (FOX_WIDTH, FOX_WIDTH, FOX_WIDTH, FOX_WIDTH, FOX_HEADS)
RWKV_SIZES = (RWKV_WIDTH, RWKV_WIDTH, RWKV_WIDTH, RWKV_WIDTH, DECAY_LORA, AAA_LORA)
FOX_COLS = sum(FOX_SIZES)
RWKV_COLS = sum(RWKV_SIZES)
IN_COLS = FOX_COLS + RWKV_COLS + 2 * D_MODEL

kernel_name = "fox_rwkv7_gated_parallel_hybrid"


def _split(u, sizes):
    idx = [int(i) for i in np.cumsum(sizes)[:-1]]
    return jnp.split(u, idx, axis=-1)


def _rmsnorm(x, g):
    xf = x.astype(jnp.float32)
    y = xf * lax.rsqrt(jnp.mean(xf * xf, axis=-1, keepdims=True) + RMS_EPS)
    return (y * g.astype(jnp.float32)).astype(x.dtype)


def _token_shift(u, mu):
    prev = jnp.pad(u, ((0, 0), (1, 0), (0, 0)))[:, :-1]
    return u + (prev - u) * mu


def _fox_attention(q, k, v, log_f):
    c = jnp.cumsum(log_f, axis=-1)
    T = q.shape[2]
    scale = FOX_HEAD_DIM ** -0.5
    outs = []
    for i in range(T // Q_BLOCK):
        s0, e = i * Q_BLOCK, (i + 1) * Q_BLOCK
        qb, kb, vb = q[:, :, s0:e], k[:, :, :e], v[:, :, :e]
        logits = (jnp.einsum('bhqd,bhkd->bhqk', qb, kb) * scale
                  + c[:, :, s0:e, None] - c[:, :, None, :e])
        causal = jnp.arange(e)[None, :] <= jnp.arange(s0, e)[:, None]
        logits = jnp.where(causal, logits, -jnp.inf)
        p = jax.nn.softmax(logits, axis=-1)
        outs.append(jnp.einsum('bhqk,bhkd->bhqd', p, vb))
    return jnp.concatenate(outs, axis=2)


def _rwkv7_scan(r, decay, k, v, kk, b):
    B, T, H, N = r.shape
    S0 = jnp.zeros((B, H, N, N), jnp.float32)
    xs = tuple(jnp.moveaxis(t, 1, 0) for t in (r, decay, k, v, kk, b))

    def step(S, inp):
        r_t, w_t, k_t, v_t, kk_t, b_t = inp
        sa = jnp.einsum('bhvk,bhk->bhv', S, -kk_t)
        S = (S * w_t[:, :, None, :] + sa[..., None] * b_t[:, :, None, :]
             + v_t[..., None] * k_t[:, :, None, :])
        y = jnp.einsum('bhvk,bhk->bhv', S, r_t)
        return S, y

    _, ys = lax.scan(step, S0, xs)
    return jnp.moveaxis(ys, 0, 1)


def _hybrid_layer(x, norm_gain, w_in, fox_forget_bias, rwkv_shift_mix, rwkv_w0, rwkv_w2,
                  rwkv_a0, rwkv_a2, rwkv_k_k, rwkv_k_a, rwkv_r_k, rwkv_ln_w, rwkv_ln_b,
                  w_proj_fox, w_proj_rwkv, w_out):
    B, T, _ = x.shape
    f32 = jnp.float32
    h = _rmsnorm(x, norm_gain)
    u = h @ w_in
    u_fox, u_rwkv, g_a, g_b = _split(u, (FOX_COLS, RWKV_COLS, D_MODEL, D_MODEL))

    q, k, v, z_a, f_logit = _split(u_fox.astype(f32), FOX_SIZES)
    to_heads = lambda t: t.reshape(B, T, FOX_HEADS, FOX_HEAD_DIM).transpose(0, 2, 1, 3)
    log_f = jax.nn.log_sigmoid(f_logit + fox_forget_bias.astype(f32)).transpose(0, 2, 1)
    o_a = _fox_attention(to_heads(q), to_heads(k), to_heads(v), log_f)
    o_a = o_a.transpose(0, 2, 1, 3).reshape(B, T, FOX_WIDTH) * jax.nn.silu(z_a)

    u_rwkv = _token_shift(u_rwkv.astype(f32), rwkv_shift_mix.astype(f32))
    r, kr, vr, z_b, w_down, a_down = _split(u_rwkv, RWKV_SIZES)
    w = -jax.nn.softplus(-(rwkv_w0.astype(f32) + jnp.tanh(w_down) @ rwkv_w2.astype(f32))) - 0.5
    decay = jnp.exp(-jnp.exp(w))
    a = jax.nn.sigmoid(rwkv_a0.astype(f32) + a_down @ rwkv_a2.astype(f32))
    heads = lambda t: t.reshape(B, T, RWKV_HEADS, RWKV_HEAD_DIM)
    kk = heads(kr * rwkv_k_k.astype(f32))
    kk = kk / jnp.maximum(jnp.sqrt(jnp.sum(kk * kk, axis=-1, keepdims=True)), L2_EPS)
    kr = kr * (1.0 + (a - 1.0) * rwkv_k_a.astype(f32))
    rh, kh, vh, ah = heads(r), heads(kr), heads(vr), heads(a)
    y = _rwkv7_scan(rh, heads(decay), kh, vh, kk, kk * ah)
    mu = jnp.mean(y, axis=-1, keepdims=True)
    var = jnp.mean(jnp.square(y - mu), axis=-1, keepdims=True)
    y = ((y - mu) * lax.rsqrt(var + GN_EPS)).reshape(B, T, RWKV_WIDTH)
    y = y * rwkv_ln_w.astype(f32) + rwkv_ln_b.astype(f32)
    bonus = jnp.sum(rh * kh * rwkv_r_k.astype(f32), axis=-1, keepdims=True) * vh
    o_b = (y + bonus.reshape(B, T, RWKV_WIDTH)) * jax.nn.silu(z_b)

    o_a = o_a.astype(x.dtype) @ w_proj_fox
    o_b = o_b.astype(x.dtype) @ w_proj_rwkv
    m = jax.nn.sigmoid(g_a) * o_a + jax.nn.sigmoid(g_b) * o_b
    return x + m @ w_out


def _fwd_setup_inputs(seed: int = 0) -> dict:
    key = jax.random.key(seed)
    ks = jax.random.split(key, 20)
    n = jax.random.normal
    L, D = DEPTH, D_MODEL
    return {
        "x": n(ks[0], (BATCH, SEQ, D), jnp.float32),
        "norm_gain": 1.0 + 0.05 * n(ks[1], (L, D), jnp.float32),
        "w_in": n(ks[2], (L, D, IN_COLS), jnp.float32) * D ** -0.5,
        "fox_forget_bias": 3.0 + 0.5 * n(ks[3], (L, FOX_HEADS), jnp.float32),
        "rwkv_shift_mix": jax.random.uniform(ks[4], (L, RWKV_COLS), jnp.float32),
        "rwkv_w0": jax.random.uniform(ks[5], (L, RWKV_WIDTH), jnp.float32, -6.5, -1.5),
        "rwkv_w2": n(ks[6], (L, DECAY_LORA, RWKV_WIDTH), jnp.float32) * 0.5 * DECAY_LORA ** -0.5,
        "rwkv_a0": 0.1 * n(ks[7], (L, RWKV_WIDTH), jnp.float32),
        "rwkv_a2": n(ks[8], (L, AAA_LORA, RWKV_WIDTH), jnp.float32) * AAA_LORA ** -0.5,
        "rwkv_k_k": 0.85 + 0.05 * n(ks[9], (L, RWKV_WIDTH), jnp.float32),
        "rwkv_k_a": 1.0 + 0.05 * n(ks[10], (L, RWKV_WIDTH), jnp.float32),
        "rwkv_r_k": -0.04 + 0.02 * n(ks[11], (L, RWKV_HEADS, RWKV_HEAD_DIM), jnp.float32),
        "rwkv_ln_w": 1.0 + 0.05 * n(ks[12], (L, RWKV_WIDTH), jnp.float32),
        "rwkv_ln_b": 0.02 * n(ks[13], (L, RWKV_WIDTH), jnp.float32),
        "w_proj_fox": n(ks[14], (L, FOX_WIDTH, D), jnp.float32) * FOX_WIDTH ** -0.5,
        "w_proj_rwkv": n(ks[15], (L, RWKV_WIDTH, D), jnp.float32) * RWKV_WIDTH ** -0.5,
        "w_out": n(ks[16], (L, D, D), jnp.float32) * D ** -0.5,
        "final_norm_gain": 1.0 + 0.05 * n(ks[17], (D,), jnp.float32),
    }


def _fwd_reference(x, norm_gain, w_in, fox_forget_bias, rwkv_shift_mix, rwkv_w0, rwkv_w2,
              rwkv_a0, rwkv_a2, rwkv_k_k, rwkv_k_a, rwkv_r_k, rwkv_ln_w, rwkv_ln_b,
              w_proj_fox, w_proj_rwkv, w_out, final_norm_gain):
    h = x
    for l in range(DEPTH):
        h = _hybrid_layer(h, norm_gain[l], w_in[l], fox_forget_bias[l], rwkv_shift_mix[l],
                          rwkv_w0[l], rwkv_w2[l], rwkv_a0[l], rwkv_a2[l], rwkv_k_k[l],
                          rwkv_k_a[l], rwkv_r_k[l], rwkv_ln_w[l], rwkv_ln_b[l],
                          w_proj_fox[l], w_proj_rwkv[l], w_out[l])
    return _rmsnorm(h, final_norm_gain)


import jax as _jax
import jax.numpy as _jnp

TWIN_FORMAT = 'train_step'
FWD_PARAMS = ['x', 'norm_gain', 'w_in', 'fox_forget_bias', 'rwkv_shift_mix', 'rwkv_w0', 'rwkv_w2', 'rwkv_a0', 'rwkv_a2', 'rwkv_k_k', 'rwkv_k_a', 'rwkv_r_k', 'rwkv_ln_w', 'rwkv_ln_b', 'w_proj_fox', 'w_proj_rwkv', 'w_out', 'final_norm_gain']
TWIN_WEIGHTS = ['norm_gain', 'w_in', 'fox_forget_bias', 'rwkv_shift_mix', 'rwkv_w0', 'rwkv_w2', 'rwkv_a0', 'rwkv_a2', 'rwkv_k_k', 'rwkv_k_a', 'rwkv_r_k', 'rwkv_ln_w', 'rwkv_ln_b', 'w_proj_fox', 'w_proj_rwkv', 'w_out', 'final_norm_gain']
TWIN_DIFF_INPUT = 'x'
TWIN_INPUTS = ['x', 'norm_gain', 'w_in', 'fox_forget_bias', 'rwkv_shift_mix', 'rwkv_w0', 'rwkv_w2', 'rwkv_a0', 'rwkv_a2', 'rwkv_k_k', 'rwkv_k_a', 'rwkv_r_k', 'rwkv_ln_w', 'rwkv_ln_b', 'w_proj_fox', 'w_proj_rwkv', 'w_out', 'final_norm_gain', 'loss_target', 'm_norm_gain', 'm_w_in', 'm_fox_forget_bias', 'm_rwkv_shift_mix', 'm_rwkv_w0', 'm_rwkv_w2', 'm_rwkv_a0', 'm_rwkv_a2', 'm_rwkv_k_k', 'm_rwkv_k_a', 'm_rwkv_r_k', 'm_rwkv_ln_w', 'm_rwkv_ln_b', 'm_w_proj_fox', 'm_w_proj_rwkv', 'm_w_out', 'm_final_norm_gain', 'v_norm_gain', 'v_w_in', 'v_fox_forget_bias', 'v_rwkv_shift_mix', 'v_rwkv_w0', 'v_rwkv_w2', 'v_rwkv_a0', 'v_rwkv_a2', 'v_rwkv_k_k', 'v_rwkv_k_a', 'v_rwkv_r_k', 'v_rwkv_ln_w', 'v_rwkv_ln_b', 'v_w_proj_fox', 'v_w_proj_rwkv', 'v_w_out', 'v_final_norm_gain']
TWIN_OUTPUTS = ['loss', 'grad_x', 'grad_norm_gain', 'grad_w_in', 'grad_fox_forget_bias', 'grad_rwkv_shift_mix', 'grad_rwkv_w0', 'grad_rwkv_w2', 'grad_rwkv_a0', 'grad_rwkv_a2', 'grad_rwkv_k_k', 'grad_rwkv_k_a', 'grad_rwkv_r_k', 'grad_rwkv_ln_w', 'grad_rwkv_ln_b', 'grad_w_proj_fox', 'grad_w_proj_rwkv', 'grad_w_out', 'grad_final_norm_gain', 'delta_norm_gain', 'delta_w_in', 'delta_fox_forget_bias', 'delta_rwkv_shift_mix', 'delta_rwkv_w0', 'delta_rwkv_w2', 'delta_rwkv_a0', 'delta_rwkv_a2', 'delta_rwkv_k_k', 'delta_rwkv_k_a', 'delta_rwkv_r_k', 'delta_rwkv_ln_w', 'delta_rwkv_ln_b', 'delta_w_proj_fox', 'delta_w_proj_rwkv', 'delta_w_out', 'delta_final_norm_gain', 'new_m_norm_gain', 'new_m_w_in', 'new_m_fox_forget_bias', 'new_m_rwkv_shift_mix', 'new_m_rwkv_w0', 'new_m_rwkv_w2', 'new_m_rwkv_a0', 'new_m_rwkv_a2', 'new_m_rwkv_k_k', 'new_m_rwkv_k_a', 'new_m_rwkv_r_k', 'new_m_rwkv_ln_w', 'new_m_rwkv_ln_b', 'new_m_w_proj_fox', 'new_m_w_proj_rwkv', 'new_m_w_out', 'new_m_final_norm_gain', 'new_v_norm_gain', 'new_v_w_in', 'new_v_fox_forget_bias', 'new_v_rwkv_shift_mix', 'new_v_rwkv_w0', 'new_v_rwkv_w2', 'new_v_rwkv_a0', 'new_v_rwkv_a2', 'new_v_rwkv_k_k', 'new_v_rwkv_k_a', 'new_v_rwkv_r_k', 'new_v_rwkv_ln_w', 'new_v_rwkv_ln_b', 'new_v_w_proj_fox', 'new_v_w_proj_rwkv', 'new_v_w_out', 'new_v_final_norm_gain']
TWIN_LEAF_KINDS = {'loss': 'loss', 'grad_x': 'grad_x', 'grad_norm_gain': 'grad_w', 'grad_w_in': 'grad_w', 'grad_fox_forget_bias': 'grad_w', 'grad_rwkv_shift_mix': 'grad_w', 'grad_rwkv_w0': 'grad_w', 'grad_rwkv_w2': 'grad_w', 'grad_rwkv_a0': 'grad_w', 'grad_rwkv_a2': 'grad_w', 'grad_rwkv_k_k': 'grad_w', 'grad_rwkv_k_a': 'grad_w', 'grad_rwkv_r_k': 'grad_w', 'grad_rwkv_ln_w': 'grad_w', 'grad_rwkv_ln_b': 'grad_w', 'grad_w_proj_fox': 'grad_w', 'grad_w_proj_rwkv': 'grad_w', 'grad_w_out': 'grad_w', 'grad_final_norm_gain': 'grad_w', 'delta_norm_gain': 'delta_w', 'delta_w_in': 'delta_w', 'delta_fox_forget_bias': 'delta_w', 'delta_rwkv_shift_mix': 'delta_w', 'delta_rwkv_w0': 'delta_w', 'delta_rwkv_w2': 'delta_w', 'delta_rwkv_a0': 'delta_w', 'delta_rwkv_a2': 'delta_w', 'delta_rwkv_k_k': 'delta_w', 'delta_rwkv_k_a': 'delta_w', 'delta_rwkv_r_k': 'delta_w', 'delta_rwkv_ln_w': 'delta_w', 'delta_rwkv_ln_b': 'delta_w', 'delta_w_proj_fox': 'delta_w', 'delta_w_proj_rwkv': 'delta_w', 'delta_w_out': 'delta_w', 'delta_final_norm_gain': 'delta_w', 'new_m_norm_gain': 'new_m', 'new_m_w_in': 'new_m', 'new_m_fox_forget_bias': 'new_m', 'new_m_rwkv_shift_mix': 'new_m', 'new_m_rwkv_w0': 'new_m', 'new_m_rwkv_w2': 'new_m', 'new_m_rwkv_a0': 'new_m', 'new_m_rwkv_a2': 'new_m', 'new_m_rwkv_k_k': 'new_m', 'new_m_rwkv_k_a': 'new_m', 'new_m_rwkv_r_k': 'new_m', 'new_m_rwkv_ln_w': 'new_m', 'new_m_rwkv_ln_b': 'new_m', 'new_m_w_proj_fox': 'new_m', 'new_m_w_proj_rwkv': 'new_m', 'new_m_w_out': 'new_m', 'new_m_final_norm_gain': 'new_m', 'new_v_norm_gain': 'new_v', 'new_v_w_in': 'new_v', 'new_v_fox_forget_bias': 'new_v', 'new_v_rwkv_shift_mix': 'new_v', 'new_v_rwkv_w0': 'new_v', 'new_v_rwkv_w2': 'new_v', 'new_v_rwkv_a0': 'new_v', 'new_v_rwkv_a2': 'new_v', 'new_v_rwkv_k_k': 'new_v', 'new_v_rwkv_k_a': 'new_v', 'new_v_rwkv_r_k': 'new_v', 'new_v_rwkv_ln_w': 'new_v', 'new_v_rwkv_ln_b': 'new_v', 'new_v_w_proj_fox': 'new_v', 'new_v_w_proj_rwkv': 'new_v', 'new_v_w_out': 'new_v', 'new_v_final_norm_gain': 'new_v'}


def _forward(args):
    return _fwd_reference(*[args[k] for k in FWD_PARAMS])


def _output_shape():
    out = _jax.eval_shape(lambda: _forward(_fwd_setup_inputs(0)))
    return out.shape, out.dtype

N_MICROBATCH = 1
ADAM_LR = 0.001
ADAM_B1 = 0.9
ADAM_B2 = 0.999
ADAM_EPS = 1e-08
ADAM_WD = 0.01
ADAM_STEP = 10
PER_EXAMPLE_BATCH_AXIS = {'x': 0, 'loss_target': 0}
SHARED_INPUTS = []
_WEIGHT_DTYPES = {'norm_gain': _jnp.float32, 'w_in': _jnp.float32, 'fox_forget_bias': _jnp.float32, 'rwkv_shift_mix': _jnp.float32, 'rwkv_w0': _jnp.float32, 'rwkv_w2': _jnp.float32, 'rwkv_a0': _jnp.float32, 'rwkv_a2': _jnp.float32, 'rwkv_k_k': _jnp.float32, 'rwkv_k_a': _jnp.float32, 'rwkv_r_k': _jnp.float32, 'rwkv_ln_w': _jnp.float32, 'rwkv_ln_b': _jnp.float32, 'w_proj_fox': _jnp.float32, 'w_proj_rwkv': _jnp.float32, 'w_out': _jnp.float32, 'final_norm_gain': _jnp.float32}
MOMENT_SCALE = {'norm_gain': 3.657253e-02, 'w_in': 1.456442e-02, 'fox_forget_bias': 6.849553e-02, 'rwkv_shift_mix': 3.924915e-02, 'rwkv_w0': 8.597842e-03, 'rwkv_w2': 9.111380e-04, 'rwkv_a0': 8.276543e-03, 'rwkv_a2': 7.855691e-03, 'rwkv_k_k': 3.328331e-02, 'rwkv_k_a': 2.472990e-02, 'rwkv_r_k': 5.066459e-02, 'rwkv_ln_w': 2.174106e-02, 'rwkv_ln_b': 2.262094e-02, 'w_proj_fox': 6.398890e-03, 'w_proj_rwkv': 1.546982e-02, 'w_out': 1.673914e-02, 'final_norm_gain': 8.012635e+00}


def _to_microbatches(a, axis):
    t = _jnp.moveaxis(a, axis, 0)
    t = t.reshape((N_MICROBATCH, t.shape[0] // N_MICROBATCH) + t.shape[1:])
    return _jnp.moveaxis(t, 1, axis + 1)


def setup_inputs(seed: int = 0) -> dict:
    inp = _fwd_setup_inputs(seed)
    key = _jax.random.fold_in(_jax.random.key(seed), 7919)
    shape, _ = _output_shape()
    out = dict(inp)
    out["loss_target"] = _jax.random.normal(_jax.random.fold_in(key, 0), shape, _jnp.float32)
    for i, name in enumerate(TWIN_WEIGHTS):
        w = inp[name].astype(_jnp.float32)
        if MOMENT_SCALE is None:
            s = _jnp.sqrt(_jnp.mean(_jnp.square(w)) + 1e-30)
        else:
            s = MOMENT_SCALE[name]
        km, kv = _jax.random.split(_jax.random.fold_in(key, i + 1))
        out[name] = w
        out["m_" + name] = s * _jax.random.normal(km, w.shape, _jnp.float32)
        out["v_" + name] = (s * s) * _jax.random.uniform(kv, w.shape, _jnp.float32, 0.5, 1.5)
    if N_MICROBATCH > 1:
        for name, axis in PER_EXAMPLE_BATCH_AXIS.items():
            out[name] = _to_microbatches(out[name], axis)
    return {'x': out['x'], 'norm_gain': out['norm_gain'], 'w_in': out['w_in'], 'fox_forget_bias': out['fox_forget_bias'], 'rwkv_shift_mix': out['rwkv_shift_mix'], 'rwkv_w0': out['rwkv_w0'], 'rwkv_w2': out['rwkv_w2'], 'rwkv_a0': out['rwkv_a0'], 'rwkv_a2': out['rwkv_a2'], 'rwkv_k_k': out['rwkv_k_k'], 'rwkv_k_a': out['rwkv_k_a'], 'rwkv_r_k': out['rwkv_r_k'], 'rwkv_ln_w': out['rwkv_ln_w'], 'rwkv_ln_b': out['rwkv_ln_b'], 'w_proj_fox': out['w_proj_fox'], 'w_proj_rwkv': out['w_proj_rwkv'], 'w_out': out['w_out'], 'final_norm_gain': out['final_norm_gain'], 'loss_target': out['loss_target'], 'm_norm_gain': out['m_norm_gain'], 'm_w_in': out['m_w_in'], 'm_fox_forget_bias': out['m_fox_forget_bias'], 'm_rwkv_shift_mix': out['m_rwkv_shift_mix'], 'm_rwkv_w0': out['m_rwkv_w0'], 'm_rwkv_w2': out['m_rwkv_w2'], 'm_rwkv_a0': out['m_rwkv_a0'], 'm_rwkv_a2': out['m_rwkv_a2'], 'm_rwkv_k_k': out['m_rwkv_k_k'], 'm_rwkv_k_a': out['m_rwkv_k_a'], 'm_rwkv_r_k': out['m_rwkv_r_k'], 'm_rwkv_ln_w': out['m_rwkv_ln_w'], 'm_rwkv_ln_b': out['m_rwkv_ln_b'], 'm_w_proj_fox': out['m_w_proj_fox'], 'm_w_proj_rwkv': out['m_w_proj_rwkv'], 'm_w_out': out['m_w_out'], 'm_final_norm_gain': out['m_final_norm_gain'], 'v_norm_gain': out['v_norm_gain'], 'v_w_in': out['v_w_in'], 'v_fox_forget_bias': out['v_fox_forget_bias'], 'v_rwkv_shift_mix': out['v_rwkv_shift_mix'], 'v_rwkv_w0': out['v_rwkv_w0'], 'v_rwkv_w2': out['v_rwkv_w2'], 'v_rwkv_a0': out['v_rwkv_a0'], 'v_rwkv_a2': out['v_rwkv_a2'], 'v_rwkv_k_k': out['v_rwkv_k_k'], 'v_rwkv_k_a': out['v_rwkv_k_a'], 'v_rwkv_r_k': out['v_rwkv_r_k'], 'v_rwkv_ln_w': out['v_rwkv_ln_w'], 'v_rwkv_ln_b': out['v_rwkv_ln_b'], 'v_w_proj_fox': out['v_w_proj_fox'], 'v_w_proj_rwkv': out['v_w_proj_rwkv'], 'v_w_out': out['v_w_out'], 'v_final_norm_gain': out['v_final_norm_gain']}


def _loss(weights, diff, rest, loss_target):
    with _jax.named_scope("forward"):
        args = {**rest, TWIN_DIFF_INPUT: diff, **{k: w.astype(_WEIGHT_DTYPES[k]) for k, w in weights.items()}}
        y = _forward(args)
    with _jax.named_scope("loss_head"):
        err = _jnp.square(y.astype(_jnp.float32) - loss_target)
        return 0.5 * _jnp.sum(_jnp.mean(err, axis=-1)) if err.ndim else 0.5 * err


def _adamw(w, g, m, v):
    m = ADAM_B1 * m + (1.0 - ADAM_B1) * g
    v = ADAM_B2 * v + (1.0 - ADAM_B2) * _jnp.square(g)
    m_hat = m / (1.0 - ADAM_B1 ** ADAM_STEP)
    v_hat = v / (1.0 - ADAM_B2 ** ADAM_STEP)
    delta = -ADAM_LR * (m_hat / (_jnp.sqrt(v_hat) + ADAM_EPS) + ADAM_WD * w)
    return delta, m, v


def reference(x, norm_gain, w_in, fox_forget_bias, rwkv_shift_mix, rwkv_w0, rwkv_w2, rwkv_a0, rwkv_a2, rwkv_k_k, rwkv_k_a, rwkv_r_k, rwkv_ln_w, rwkv_ln_b, w_proj_fox, w_proj_rwkv, w_out, final_norm_gain, loss_target, m_norm_gain, m_w_in, m_fox_forget_bias, m_rwkv_shift_mix, m_rwkv_w0, m_rwkv_w2, m_rwkv_a0, m_rwkv_a2, m_rwkv_k_k, m_rwkv_k_a, m_rwkv_r_k, m_rwkv_ln_w, m_rwkv_ln_b, m_w_proj_fox, m_w_proj_rwkv, m_w_out, m_final_norm_gain, v_norm_gain, v_w_in, v_fox_forget_bias, v_rwkv_shift_mix, v_rwkv_w0, v_rwkv_w2, v_rwkv_a0, v_rwkv_a2, v_rwkv_k_k, v_rwkv_k_a, v_rwkv_r_k, v_rwkv_ln_w, v_rwkv_ln_b, v_w_proj_fox, v_w_proj_rwkv, v_w_out, v_final_norm_gain):
    given = dict(x=x, norm_gain=norm_gain, w_in=w_in, fox_forget_bias=fox_forget_bias, rwkv_shift_mix=rwkv_shift_mix, rwkv_w0=rwkv_w0, rwkv_w2=rwkv_w2, rwkv_a0=rwkv_a0, rwkv_a2=rwkv_a2, rwkv_k_k=rwkv_k_k, rwkv_k_a=rwkv_k_a, rwkv_r_k=rwkv_r_k, rwkv_ln_w=rwkv_ln_w, rwkv_ln_b=rwkv_ln_b, w_proj_fox=w_proj_fox, w_proj_rwkv=w_proj_rwkv, w_out=w_out, final_norm_gain=final_norm_gain, loss_target=loss_target, m_norm_gain=m_norm_gain, m_w_in=m_w_in, m_fox_forget_bias=m_fox_forget_bias, m_rwkv_shift_mix=m_rwkv_shift_mix, m_rwkv_w0=m_rwkv_w0, m_rwkv_w2=m_rwkv_w2, m_rwkv_a0=m_rwkv_a0, m_rwkv_a2=m_rwkv_a2, m_rwkv_k_k=m_rwkv_k_k, m_rwkv_k_a=m_rwkv_k_a, m_rwkv_r_k=m_rwkv_r_k, m_rwkv_ln_w=m_rwkv_ln_w, m_rwkv_ln_b=m_rwkv_ln_b, m_w_proj_fox=m_w_proj_fox, m_w_proj_rwkv=m_w_proj_rwkv, m_w_out=m_w_out, m_final_norm_gain=m_final_norm_gain, v_norm_gain=v_norm_gain, v_w_in=v_w_in, v_fox_forget_bias=v_fox_forget_bias, v_rwkv_shift_mix=v_rwkv_shift_mix, v_rwkv_w0=v_rwkv_w0, v_rwkv_w2=v_rwkv_w2, v_rwkv_a0=v_rwkv_a0, v_rwkv_a2=v_rwkv_a2, v_rwkv_k_k=v_rwkv_k_k, v_rwkv_k_a=v_rwkv_k_a, v_rwkv_r_k=v_rwkv_r_k, v_rwkv_ln_w=v_rwkv_ln_w, v_rwkv_ln_b=v_rwkv_ln_b, v_w_proj_fox=v_w_proj_fox, v_w_proj_rwkv=v_w_proj_rwkv, v_w_out=v_w_out, v_final_norm_gain=v_final_norm_gain)
    weights = {n: given[n] for n in TWIN_WEIGHTS}
    shared = {n: given[n] for n in SHARED_INPUTS}
    per_example = {n: given[n] for n in ['x']}
    grad_fn = _jax.value_and_grad(_loss, argnums=(0, 1))

    def one_microbatch(ex, loss_target):
        ex = dict(ex)
        diff = ex.pop(TWIN_DIFF_INPUT)
        return grad_fn(weights, diff, {**shared, **ex}, loss_target)

    if N_MICROBATCH == 1:
        loss, (grad_w, grad_x) = one_microbatch(per_example, given["loss_target"])
    else:
        def body(carry, xs):
            loss_sum, grad_sum = carry
            l_k, (gw_k, gx_k) = one_microbatch(xs[0], xs[1])
            with _jax.named_scope("update"):
                return (loss_sum + l_k, _jax.tree.map(_jnp.add, grad_sum, gw_k)), gx_k

        init = (_jnp.zeros((), _jnp.float32), _jax.tree.map(_jnp.zeros_like, weights))
        (loss, grad_w), grad_x = _jax.lax.scan(body, init, (per_example, given["loss_target"]))
    with _jax.named_scope("update"):
        delta_w, new_m, new_v = {}, {}, {}
        for n in TWIN_WEIGHTS:
            delta_w[n], new_m[n], new_v[n] = _adamw(weights[n], grad_w[n], given["m_" + n], given["v_" + n])
    return (loss, grad_x, *[grad_w[n] for n in TWIN_WEIGHTS], *[delta_w[n] for n in TWIN_WEIGHTS],
            *[new_m[n] for n in TWIN_WEIGHTS], *[new_v[n] for n in TWIN_WEIGHTS])
```

```python
import functools

import jax
import jax.numpy as jnp
from jax import lax
from jax.experimental import pallas as pl
from jax.experimental.pallas import tpu as pltpu

F32 = jnp.float32
BF16 = jnp.bfloat16
HI = lax.Precision.HIGHEST
MESH = pl.DeviceIdType.MESH

FOX_HD = 128
RW_HD = 64
RMS_EPS = 1e-6
GN_EPS = 64e-5
L2_EPS = 1e-12
ADAM_LR = 0.001
ADAM_B1 = 0.9
ADAM_B2 = 0.999
ADAM_EPS = 1e-08
ADAM_WD = 0.01
ADAM_STEP = 10

LANE = 128
SUB = 8
VMEM_LIMIT = 56 * 1024 * 1024
N_DEV = 8
CHUNK = 64
NEG = -1e30


def _cp(sem=None):
    return pltpu.CompilerParams(dimension_semantics=sem, vmem_limit_bytes=VMEM_LIMIT)


def _sigmoid(x):
    return jax.nn.sigmoid(x)


def _softplus(x):
    return jnp.maximum(x, 0.0) + jnp.log(1.0 + jnp.exp(-jnp.abs(x)))


def _nn(a, b, prec=None):
    return lax.dot_general(a, b, (((1,), (0,)), ((), ())), precision=prec, preferred_element_type=F32)


def _nt(a, b, prec=None):
    return lax.dot_general(a, b, (((1,), (1,)), ((), ())), precision=prec, preferred_element_type=F32)


def _tn(a, b, prec=None):
    return lax.dot_general(a, b, (((0,), (0,)), ((), ())), precision=prec, preferred_element_type=F32)


def _iota2(shape, dim):
    return lax.broadcasted_iota(jnp.int32, shape, dim)


def _seg_sum(x):
    r = _iota2((LANE, LANE), 0) // RW_HD
    c = _iota2((LANE, LANE), 1) // RW_HD
    bd = (r == c).astype(F32)
    parts = [_nn(x[:, j * LANE:(j + 1) * LANE], bd, HI) for j in range(x.shape[1] // LANE)]
    return parts[0] if len(parts) == 1 else jnp.concatenate(parts, axis=1)


def _mm(a, b, *, tb=False, out_dtype=F32, tm=512, tn=512, tk=None, name):
    M, K = a.shape
    N = b.shape[0] if tb else b.shape[1]
    tm, tn = min(tm, M), min(tn, N)
    tk = K if tk is None else tk
    nk = K // tk
    assert M % tm == 0 and N % tn == 0 and K % tk == 0
    b_spec = pl.BlockSpec((tn, tk), lambda i, j, k: (j, k)) if tb else pl.BlockSpec((tk, tn), lambda i, j, k: (k, j))

    def body(a_ref, b_ref, o_ref, *scr):
        av = a_ref[...].astype(BF16)
        bv = b_ref[...].astype(BF16)
        p = _nt(av, bv) if tb else _nn(av, bv)
        if nk == 1:
            o_ref[...] = p.astype(out_dtype)
        else:
            acc_ref = scr[0]
            k = pl.program_id(2)

            @pl.when(k == 0)
            def _():
                acc_ref[...] = p

            @pl.when(k > 0)
            def _():
                acc_ref[...] += p

            @pl.when(k == nk - 1)
            def _():
                o_ref[...] = acc_ref[...].astype(out_dtype)

    return pl.pallas_call(
        body, name=name,
        out_shape=jax.ShapeDtypeStruct((M, N), out_dtype),
        grid=(M // tm, N // tn, nk),
        in_specs=[pl.BlockSpec((tm, tk), lambda i, j, k: (i, k)), b_spec],
        out_specs=pl.BlockSpec((tm, tn), lambda i, j, k: (i, j)),
        scratch_shapes=[] if nk == 1 else [pltpu.VMEM((tm, tn), F32)],
        compiler_params=_cp(("parallel", "parallel", "arbitrary")),
    )(a, b)


def _rows(arr, tb, w, cb=0):
    return (arr, (tb, w), lambda i: (i, cb))


def _whole(arr):
    nd = arr.ndim
    return (arr, arr.shape, lambda i: (0,) * nd)


def _rowcall(name, T, tb, ins, body, outs, accs=()):
    n_in, n_out = len(ins), len(outs)

    def kern(*refs):
        i = pl.program_id(0)
        vals = [r[...] for r in refs[:n_in]]
        ro, ao = body(i, *vals)
        for r, v in zip(refs[n_in:n_in + n_out], ro):
            if isinstance(v, (list, tuple)):
                off = 0
                for piece in v:
                    w = piece.shape[1]
                    r[:, off:off + w] = piece.astype(r.dtype)
                    off += w
            else:
                r[...] = v.astype(r.dtype)
        if accs:
            acc_refs = refs[n_in + n_out:]

            @pl.when(i == 0)
            def _():
                for r in acc_refs:
                    r[...] = jnp.zeros(r.shape, F32)

            for r, v in zip(acc_refs, ao):
                r[...] += v

    out_shape = [jax.ShapeDtypeStruct((T, w), dt) for (w, dt) in outs] + [jax.ShapeDtypeStruct(s, F32) for s in accs]
    out_specs = [pl.BlockSpec((tb, w), lambda i: (i, 0)) for (w, dt) in outs] + [pl.BlockSpec(s, lambda i: (0, 0)) for s in accs]
    res = pl.pallas_call(
        kern, name=name,
        out_shape=out_shape,
        grid=(T // tb,),
        in_specs=[pl.BlockSpec(bs, im) for (_, bs, im) in ins],
        out_specs=out_specs,
        compiler_params=_cp(("arbitrary",)),
    )(*[a for (a, _, _) in ins])
    return res


def _rms_math(x, g):
    r = lax.rsqrt(jnp.mean(x * x, axis=-1, keepdims=True) + RMS_EPS)
    return x * r * g


def _merge_math(ga, gb, pa, pb):
    return _sigmoid(ga) * pa + _sigmoid(gb) * pb


def _prep_math(xk, xwd, xad, w0, a0, kk_w, ka_w, w2p, a2p):
    z = w0 + _nn(jnp.tanh(xwd), w2p, HI)
    w = -_softplus(-z) - 0.5
    lw = -jnp.exp(w)
    ag = _sigmoid(a0 + _nn(xad, a2p, HI))
    p = xk * kk_w
    n = jnp.maximum(jnp.sqrt(_seg_sum(p * p)), L2_EPS)
    kk = p / n
    kp = xk * (1.0 + (ag - 1.0) * ka_w)
    return lw, kp, -kk, kk * ag


def _post_math(y, r, kp, v, z, lnw, lnb, rk):
    inv = 1.0 / RW_HD
    mu = _seg_sum(y) * inv
    d = y - mu
    var = _seg_sum(d * d) * inv
    yn = d * lax.rsqrt(var + GN_EPS) * lnw + lnb
    bonus = _seg_sum(r * kp * rk) * v
    return (yn + bonus) * (z * _sigmoid(z))


def _scan_chunk(s0, r, lw, k, v, a, b):
    C = r.shape[0]
    row = _iota2((C, C), 0)
    col = _iota2((C, C), 1)
    incl = row >= col
    strict = row > col
    g = _nn(incl.astype(F32), lw, HI)
    g_end = jnp.sum(lw, axis=0, keepdims=True)
    at = a * jnp.exp(g - lw)
    rt = r * jnp.exp(g)
    en = jnp.exp(-g)
    bt = b * en
    kt = k * en
    ec = jnp.exp(g_end - g)
    m0 = _iota2((1, LANE), 1) < RW_HD
    eye = (row == col).astype(F32)

    def head(mask):
        atm = at * mask
        rtm = rt * mask
        mab = jnp.where(strict, _nt(atm, bt, HI), 0.0)
        mak = jnp.where(strict, _nt(atm, kt, HI), 0.0)
        rb = jnp.where(incl, _nt(rtm, bt, HI), 0.0)
        rk = jnp.where(incl, _nt(rtm, kt, HI), 0.0)
        tinv = eye + mab
        mp = mab
        n = 2
        while n < C:
            mp = _nn(mp, mp, HI)
            tinv = tinv + _nn(tinv, mp, HI)
            n *= 2
        return mak, rb, rk, tinv

    mask0 = m0.astype(F32)
    mak0, rb0, rk0, ti0 = head(mask0)
    mak1, rb1, rk1, ti1 = head(1.0 - mask0)
    x = _nt(at, s0, HI) + jnp.where(m0, _nn(mak0, v, HI), _nn(mak1, v, HI))
    u = jnp.where(m0, _nn(ti0, x, HI), _nn(ti1, x, HI))
    y = _nt(rt, s0, HI) + jnp.where(m0, _nn(rb0, u, HI) + _nn(rk0, v, HI), _nn(rb1, u, HI) + _nn(rk1, v, HI))
    bd = (_iota2((LANE, LANE), 0) // RW_HD) == (_iota2((LANE, LANE), 1) // RW_HD)
    s1 = s0 * jnp.exp(g_end) + jnp.where(bd, _tn(u, b * ec, HI) + _tn(v, k * ec, HI), 0.0)
    return y, s1


def _scan_fwd(xs, lw, kp, an, bb, T):
    C = min(CHUNK, T)
    nc = T // C
    npair = 1024 // LANE

    def kern(r_ref, lw_ref, k_ref, v_ref, a_ref, b_ref, y_ref, st_ref, s_scr):
        n = pl.program_id(1)

        @pl.when(n == 0)
        def _():
            s_scr[...] = jnp.zeros(s_scr.shape, F32)

        s0 = s_scr[...]
        st_ref[0, 0] = s0
        y, s1 = _scan_chunk(s0, r_ref[...], lw_ref[...], k_ref[...], v_ref[...], a_ref[...], b_ref[...])
        y_ref[...] = y
        s_scr[...] = s1

    def col(off):
        return pl.BlockSpec((C, LANE), lambda p, n: (n, off + p))

    return pl.pallas_call(
        kern, name="rwkv_scan_fwd",
        out_shape=[jax.ShapeDtypeStruct((T, 1024), F32), jax.ShapeDtypeStruct((nc, npair, LANE, LANE), F32)],
        grid=(npair, nc),
        in_specs=[col(0), col(0), col(0), col(16), col(0), col(0)],
        out_specs=[col(0), pl.BlockSpec((1, 1, LANE, LANE), lambda p, n: (n, p, 0, 0))],
        scratch_shapes=[pltpu.VMEM((LANE, LANE), F32)],
        compiler_params=_cp(("parallel", "arbitrary")),
    )(xs, lw, kp, xs, an, bb)


def _scan_bwd(xs, lw, kp, an, bb, states, dy, T):
    C = min(CHUNK, T)
    nc = T // C
    npair = 1024 // LANE

    def kern(r_ref, lw_ref, k_ref, v_ref, a_ref, b_ref, st_ref, dy_ref, dr_ref, dlw_ref, dk_ref, dv_ref, da_ref, db_ref, ds_scr):
        n = pl.program_id(1)

        @pl.when(n == 0)
        def _():
            ds_scr[...] = jnp.zeros(ds_scr.shape, F32)

        _, vjp = jax.vjp(_scan_chunk, st_ref[0, 0], r_ref[...], lw_ref[...], k_ref[...], v_ref[...], a_ref[...], b_ref[...])
        ds0, dr, dlw, dk, dv, da, db = vjp((dy_ref[...], ds_scr[...]))
        ds_scr[...] = ds0
        dr_ref[...] = dr
        dlw_ref[...] = dlw
        dk_ref[...] = dk
        dv_ref[...] = dv
        da_ref[...] = da
        db_ref[...] = db

    def col(off):
        return pl.BlockSpec((C, LANE), lambda p, n: (nc - 1 - n, off + p))

    return pl.pallas_call(
        kern, name="rwkv_scan_bwd",
        out_shape=[jax.ShapeDtypeStruct((T, 1024), F32)] * 6,
        grid=(npair, nc),
        in_specs=[col(0), col(0), col(0), col(16), col(0), col(0),
                  pl.BlockSpec((1, 1, LANE, LANE), lambda p, n: (nc - 1 - n, p, 0, 0)), col(0)],
        out_specs=[col(0)] * 6,
        scratch_shapes=[pltpu.VMEM((LANE, LANE), F32)],
        compiler_params=_cp(("parallel", "arbitrary")),
    )(xs, lw, kp, xs, an, bb, states, dy)


def _gates_fwd(u, bias_pad, T, f_cb):
    nb = T // LANE

    def kern(f_ref, b_ref, c_ref):
        x = f_ref[...] + b_ref[...]
        lf = jnp.minimum(x, 0.0) - jnp.log(1.0 + jnp.exp(-jnp.abs(x)))
        lft = lf.T
        ut = (_iota2((LANE, LANE), 0) <= _iota2((LANE, LANE), 1)).astype(F32)
        carry = jnp.zeros((LANE, 1), F32)
        for blk in range(nb):
            seg = lft[:, blk * LANE:(blk + 1) * LANE]
            cs = _nn(seg, ut, HI) + carry
            c_ref[:, blk * LANE:(blk + 1) * LANE] = cs[:SUB, :]
            carry = carry + jnp.sum(seg, axis=1, keepdims=True)

    return pl.pallas_call(
        kern, name="fox_gates_fwd",
        out_shape=jax.ShapeDtypeStruct((SUB, T), F32),
        grid=(1,),
        in_specs=[pl.BlockSpec((T, LANE), lambda i: (0, f_cb)), pl.BlockSpec((1, LANE), lambda i: (0, 0))],
        out_specs=pl.BlockSpec((SUB, T), lambda i: (0, 0)),
        compiler_params=_cp(("arbitrary",)),
    )(u, bias_pad)


def _gates_bwd(dc, u, bias_pad, T, f_cb):
    nb = T // LANE

    def kern(dc_ref, f_ref, b_ref, dfl_ref, db_ref):
        dcv = jnp.concatenate([dc_ref[...], jnp.zeros((LANE - SUB, T), F32)], axis=0)
        lt = (_iota2((LANE, LANE), 0) >= _iota2((LANE, LANE), 1)).astype(F32)
        carry = jnp.zeros((LANE, 1), F32)
        pieces = [None] * nb
        for blk in range(nb - 1, -1, -1):
            seg = dcv[:, blk * LANE:(blk + 1) * LANE]
            pieces[blk] = _nn(seg, lt, HI) + carry
            carry = carry + jnp.sum(seg, axis=1, keepdims=True)
        dlf = (pieces[0] if nb == 1 else jnp.concatenate(pieces, axis=1)).T
        x = f_ref[...] + b_ref[...]
        dfl = dlf * _sigmoid(-x)
        dfl_ref[...] = dfl
        db_ref[...] = jnp.sum(dfl, axis=0, keepdims=True)

    return pl.pallas_call(
        kern, name="fox_gates_bwd",
        out_shape=[jax.ShapeDtypeStruct((T, LANE), F32), jax.ShapeDtypeStruct((1, LANE), F32)],
        grid=(1,),
        in_specs=[pl.BlockSpec((SUB, T), lambda i: (0, 0)), pl.BlockSpec((T, LANE), lambda i: (0, f_cb)),
                  pl.BlockSpec((1, LANE), lambda i: (0, 0))],
        out_specs=[pl.BlockSpec((T, LANE), lambda i: (0, 0)), pl.BlockSpec((1, LANE), lambda i: (0, 0))],
        compiler_params=_cp(("arbitrary",)),
    )(dc, u, bias_pad)


def _attn_block(T):
    return 256 if T % 256 == 0 and T >= 512 else 128


def _attn_fwd(u, c3, T):
    H = 8
    bq = _attn_block(T)
    nq = T // bq
    scale = FOX_HD ** -0.5

    def kern(q_ref, k_ref, v_ref, z_ref, cq_ref, ck_ref, o_ref, oa_ref, lse_ref):
        i = pl.program_id(1)
        q = (q_ref[...] * scale).astype(BF16)
        c0 = cq_ref[0][:, 0:1]
        row = _iota2((bq, bq), 0) + i * bq

        def step(j, carry):
            m, l, acc = carry
            off = pl.multiple_of(j * bq, bq)
            kj = k_ref[pl.ds(off, bq), :].astype(BF16)
            vj = v_ref[pl.ds(off, bq), :].astype(BF16)
            s = _nt(q, kj) + (c0 - ck_ref[0, :, pl.ds(off, bq)])
            s = jnp.where(_iota2((bq, bq), 1) + j * bq <= row, s, NEG)
            m_new = jnp.maximum(m, jnp.max(s, axis=1, keepdims=True))
            p = jnp.exp(s - m_new)
            alpha = jnp.exp(m - m_new)
            l = alpha * l + jnp.sum(p, axis=1, keepdims=True)
            acc = alpha * acc + _nn(p.astype(BF16), vj)
            return m_new, l, acc

        init = (jnp.full((bq, 1), NEG, F32), jnp.zeros((bq, 1), F32), jnp.zeros((bq, FOX_HD), F32))
        m, l, acc = lax.fori_loop(0, i + 1, step, init)
        o = acc / l
        z = z_ref[...]
        o_ref[...] = o
        oa_ref[...] = (o * z * _sigmoid(z)).astype(BF16)
        lse_ref[0] = m + jnp.log(l)

    return pl.pallas_call(
        kern, name="fox_attn_fwd",
        out_shape=[jax.ShapeDtypeStruct((T, 1024), F32), jax.ShapeDtypeStruct((T, 1024), BF16),
                   jax.ShapeDtypeStruct((H, T, 1), F32)],
        grid=(H, nq),
        in_specs=[pl.BlockSpec((bq, LANE), lambda h, i: (i, h)),
                  pl.BlockSpec((T, LANE), lambda h, i: (0, 8 + h)),
                  pl.BlockSpec((T, LANE), lambda h, i: (0, 16 + h)),
                  pl.BlockSpec((bq, LANE), lambda h, i: (i, 24 + h)),
                  pl.BlockSpec((1, 1, bq), lambda h, i: (h, 0, i)),
                  pl.BlockSpec((1, 1, T), lambda h, i: (h, 0, 0))],
        out_specs=[pl.BlockSpec((bq, LANE), lambda h, i: (i, h)),
                   pl.BlockSpec((bq, LANE), lambda h, i: (i, h)),
                   pl.BlockSpec((1, bq, 1), lambda h, i: (h, i, 0))],
        compiler_params=_cp(("parallel", "arbitrary")),
    )(u, u, u, u, c3, c3)


def _attn_tile(qs, kj, vj, dob, cdiff, lse_i, keep):
    s = _nt(qs, kj) + cdiff
    p = jnp.where(keep, jnp.exp(jnp.where(keep, s, NEG) - lse_i), 0.0)
    return p, _nt(dob, vj)


def _attn_delta(u, c3, o, lse, doa, T):
    H = 8
    bq = _attn_block(T)
    nq = T // bq
    scale = FOX_HD ** -0.5

    def kern(q_ref, k_ref, v_ref, z_ref, c_ref, o_ref, lse_ref, doa_ref, do_ref, dz_ref, dl_ref):
        i = pl.program_id(1)
        z = z_ref[...]
        sg = _sigmoid(z)
        dov = doa_ref[...]
        dob = (dov * z * sg).astype(BF16)
        do_ref[...] = dob
        dz_ref[...] = dov * o_ref[...] * (sg * (1.0 + z * (1.0 - sg)))
        qs = (q_ref[...] * scale).astype(BF16)
        ioff = pl.multiple_of(i * bq, bq)
        c0 = c_ref[0, :, pl.ds(ioff, bq)][:, 0:1]
        lse_i = lse_ref[0]
        row = _iota2((bq, bq), 0) + i * bq

        def step(j, acc):
            off = pl.multiple_of(j * bq, bq)
            kj = k_ref[pl.ds(off, bq), :].astype(BF16)
            vj = v_ref[pl.ds(off, bq), :].astype(BF16)
            keep = _iota2((bq, bq), 1) + j * bq <= row
            p, dp = _attn_tile(qs, kj, vj, dob, c0 - c_ref[0, :, pl.ds(off, bq)], lse_i, keep)
            return acc + jnp.sum(p * dp, axis=1, keepdims=True)

        dl_ref[0] = lax.fori_loop(0, i + 1, step, jnp.zeros((bq, 1), F32))

    full = lambda cb: pl.BlockSpec((T, LANE), lambda h, i: (0, cb + h))
    blk = lambda cb: pl.BlockSpec((bq, LANE), lambda h, i: (i, cb + h))
    return pl.pallas_call(
        kern, name="fox_attn_delta",
        out_shape=[jax.ShapeDtypeStruct((T, 1024), BF16), jax.ShapeDtypeStruct((T, 1024), F32), jax.ShapeDtypeStruct((H, T, 1), F32)],
        grid=(H, nq),
        in_specs=[blk(0), full(8), full(16), blk(24),
                  pl.BlockSpec((1, 1, T), lambda h, i: (h, 0, 0)),
                  blk(0),
                  pl.BlockSpec((1, bq, 1), lambda h, i: (h, i, 0)),
                  blk(0)],
        out_specs=[blk(0), blk(0), pl.BlockSpec((1, bq, 1), lambda h, i: (h, i, 0))],
        compiler_params=_cp(("parallel", "arbitrary")),
    )(u, u, u, u, c3, o, lse, doa)


def _attn_bwd(u, c3, lse, do, delta, T):
    H = 8
    bq = _attn_block(T)
    nq = T // bq
    scale = FOX_HD ** -0.5

    def kern(q_ref, k_ref, v_ref, c_ref, lse_ref, do_ref, dl_ref, dq_ref, dk_ref, dv_ref, dc_ref):
        j = pl.program_id(1)

        @pl.when(j == 0)
        def _():
            dq_ref[...] = jnp.zeros(dq_ref.shape, F32)

        kj = k_ref[...].astype(BF16)
        vj = v_ref[...].astype(BF16)
        joff = pl.multiple_of(j * bq, bq)
        ck = c_ref[0, :, pl.ds(joff, bq)]
        col = _iota2((bq, bq), 1) + j * bq

        def step(i, carry):
            dk, dv, dc = carry
            off = pl.multiple_of(i * bq, bq)
            qs = (q_ref[pl.ds(off, bq), :] * scale).astype(BF16)
            dob = do_ref[pl.ds(off, bq), :]
            c0 = c_ref[0, :, pl.ds(off, bq)][:, 0:1]
            keep = col <= _iota2((bq, bq), 0) + i * bq
            p, dp = _attn_tile(qs, kj, vj, dob, c0 - ck, lse_ref[0, pl.ds(off, bq), :], keep)
            ds = p * (dp - dl_ref[0, pl.ds(off, bq), :])
            dsb = ds.astype(BF16)
            dv = dv + _tn(p.astype(BF16), dob)
            dk = dk + _tn(dsb, qs)
            dq_ref[pl.ds(off, bq), :] += _nn(dsb, kj) * scale
            dc = dc - jnp.sum(ds, axis=0, keepdims=True)
            return dk, dv, dc

        init = (jnp.zeros((bq, FOX_HD), F32), jnp.zeros((bq, FOX_HD), F32), jnp.zeros((1, bq), F32))
        dk, dv, dc = lax.fori_loop(j, nq, step, init)
        dk_ref[...] = dk
        dv_ref[...] = dv
        dc_ref[0] = dc

    full = lambda cb: pl.BlockSpec((T, LANE), lambda h, j: (0, cb + h))
    blk = lambda cb: pl.BlockSpec((bq, LANE), lambda h, j: (j, cb + h))
    col1 = pl.BlockSpec((1, T, 1), lambda h, j: (h, 0, 0))
    return pl.pallas_call(
        kern, name="fox_attn_bwd",
        out_shape=[jax.ShapeDtypeStruct((T, 1024), F32)] * 3 + [jax.ShapeDtypeStruct((H, 1, T), F32)],
        grid=(H, nq),
        in_specs=[full(0), blk(8), blk(16), pl.BlockSpec((1, 1, T), lambda h, j: (h, 0, 0)), col1, full(0), col1],
        out_specs=[full(0), blk(0), blk(0), pl.BlockSpec((1, 1, bq), lambda h, j: (h, 0, j))],
        compiler_params=_cp(("parallel", "arbitrary")),
    )(u, u, u, c3, lse, do, delta)


def _place():
    return lax.axis_index("x"), lax.axis_index("y"), lax.axis_index("c")


def _slot(p):
    return 4 * p[0] + 2 * p[1] + p[2]


def _allgather_weights(a, b):
    def body(a_ref, b_ref, ao_ref, bo_ref, send_sems, recv_sems, local_sems):
        x, y, c = _place()
        me, sibling = (x, y, c), (x, y, 1 - c)
        chips = [(1 - x, y), (x, 1 - y), (1 - x, 1 - y)]
        srcs, dsts = (a_ref, b_ref), (ao_ref, bo_ref)

        def copy(t, k, block, to, from_input=False):
            dst = dsts[t].at[_slot(block)]
            return pltpu.make_async_remote_copy(
                src_ref=srcs[t] if from_input else dst, dst_ref=dst,
                send_sem=send_sems.at[t, k], recv_sem=recv_sems.at[t, k],
                device_id=to, device_id_type=MESH)

        mine = [pltpu.make_async_copy(srcs[t], dsts[t].at[_slot(me)], local_sems.at[t]) for t in range(2)]
        for cp in mine:
            cp.start()
        first = []
        for t in range(2):
            first.append(copy(t, 0, me, sibling, True))
            first += [copy(t, 1 + j, me, (*chip, c), True) for j, chip in enumerate(chips)]
        for cp in first:
            cp.start()
        passed = []
        for j, chip in enumerate(chips):
            for t in range(2):
                copy(t, 1 + j, (*chip, c), me).wait_recv()
                fwd = copy(t, 4 + j, (*chip, c), sibling)
                fwd.start()
                passed.append(fwd)
        for t in range(2):
            copy(t, 0, sibling, me).wait_recv()
            for j, chip in enumerate(chips):
                copy(t, 4 + j, (*chip, 1 - c), me).wait_recv()
        for cp in first + passed:
            cp.wait_send()
        for cp in mine:
            cp.wait()

    any_spec = pl.BlockSpec(memory_space=pl.ANY)
    return pl.pallas_call(
        body, name="allgather_weights",
        out_shape=[jax.ShapeDtypeStruct((N_DEV,) + a.shape, a.dtype), jax.ShapeDtypeStruct((N_DEV,) + b.shape, b.dtype)],
        in_specs=[any_spec, any_spec],
        out_specs=[any_spec, any_spec],
        scratch_shapes=[pltpu.SemaphoreType.DMA((2, 7)), pltpu.SemaphoreType.DMA((2, 7)), pltpu.SemaphoreType.DMA((2,))],
    )(a, b)


def _exchange_grads(ga, gb, small):
    def body(ga_ref, gb_ref, sm_ref, ra_ref, rb_ref, rs_ref, send_sems, recv_sems, local_sems):
        x, y, c = _place()
        me = _slot((x, y, c))

        def peer(k):
            return (1 - x if k & 4 else x, 1 - y if k & 2 else y, 1 - c if k & 1 else c)

        def copies(k):
            p = peer(k)
            ps = _slot(p)
            srcs = (ga_ref.at[ps], gb_ref.at[ps], sm_ref)
            out = []
            for t, (src, dst) in enumerate(zip(srcs, (ra_ref, rb_ref, rs_ref))):
                out.append(pltpu.make_async_remote_copy(
                    src_ref=src, dst_ref=dst.at[me], send_sem=send_sems.at[t, k - 1], recv_sem=recv_sems.at[t, k - 1],
                    device_id=p, device_id_type=MESH))
            return out

        def arrivals(k):
            ps = _slot(peer(k))
            out = []
            for t, dst in enumerate((ra_ref, rb_ref, rs_ref)):
                out.append(pltpu.make_async_remote_copy(
                    src_ref=dst.at[ps], dst_ref=dst.at[ps], send_sem=send_sems.at[t, k - 1], recv_sem=recv_sems.at[t, k - 1],
                    device_id=peer(k), device_id_type=MESH))
            return out

        mine = [pltpu.make_async_copy(src, dst.at[me], local_sems.at[t])
                for t, (src, dst) in enumerate(zip((ga_ref.at[me], gb_ref.at[me], sm_ref), (ra_ref, rb_ref, rs_ref)))]
        for cp in mine:
            cp.start()
        sent = []
        for k in range(1, N_DEV):
            for cp in copies(k):
                cp.start()
                sent.append(cp)
        for k in range(1, N_DEV):
            for cp in arrivals(k):
                cp.wait_recv()
        for cp in sent:
            cp.wait_send()
        for cp in mine:
            cp.wait()

    any_spec = pl.BlockSpec(memory_space=pl.ANY)
    return pl.pallas_call(
        body, name="exchange_grads",
        out_shape=[jax.ShapeDtypeStruct(ga.shape, ga.dtype), jax.ShapeDtypeStruct(gb.shape, gb.dtype),
                   jax.ShapeDtypeStruct((N_DEV,) + small.shape, small.dtype)],
        in_specs=[any_spec] * 3,
        out_specs=[any_spec] * 3,
        scratch_shapes=[pltpu.SemaphoreType.DMA((3, 7)), pltpu.SemaphoreType.DMA((3, 7)), pltpu.SemaphoreType.DMA((3,))],
    )(ga, gb, small)


def _adamw(name, w, m, v, g_own, recv, tb):
    R, Cc = w.shape
    assert R % tb == 0

    def kern(*refs):
        if g_own is None:
            w_ref, m_ref, v_ref, r_ref = refs[:4]
            outs = refs[4:]
            g = r_ref[0].astype(F32)
            for s in range(1, N_DEV):
                g = g + r_ref[s].astype(F32)
        else:
            w_ref, m_ref, v_ref, g_ref, r_ref = refs[:5]
            outs = refs[5:]
            me = _slot(_place())
            g = g_ref[...]
            for s in range(N_DEV):
                g = g + jnp.where(me == s, 0.0, r_ref[s].astype(F32))
        g_out, d_out, m_out, v_out = outs
        mn = ADAM_B1 * m_ref[...] + (1.0 - ADAM_B1) * g
        vn = ADAM_B2 * v_ref[...] + (1.0 - ADAM_B2) * (g * g)
        m_hat = mn / (1.0 - ADAM_B1 ** ADAM_STEP)
        v_hat = vn / (1.0 - ADAM_B2 ** ADAM_STEP)
        g_out[...] = g
        d_out[...] = -ADAM_LR * (m_hat / (jnp.sqrt(v_hat) + ADAM_EPS) + ADAM_WD * w_ref[...])
        m_out[...] = mn
        v_out[...] = vn

    blk = pl.BlockSpec((tb, Cc), lambda i: (i, 0))
    rblk = pl.BlockSpec((N_DEV, tb, Cc), lambda i: (0, i, 0))
    ins = [w, m, v] + ([] if g_own is None else [g_own]) + [recv]
    return pl.pallas_call(
        kern, name=name,
        out_shape=[jax.ShapeDtypeStruct((R, Cc), F32)] * 4,
        grid=(R // tb,),
        in_specs=[blk] * (len(ins) - 1) + [rblk],
        out_specs=[blk] * 4,
        compiler_params=_cp(("arbitrary",)),
    )(*ins)


def _pad_cols(a, w):
    return jnp.pad(a, ((0, 0), (0, w - a.shape[1])))


def _pad_rows(a, r):
    return jnp.pad(a, ((0, r - a.shape[0]), (0, 0)))


def _pack_b(pf, pr, wo, w2, a2, rows):
    body = jnp.concatenate([pf, pr, wo.reshape(2048, 256), jnp.concatenate([w2, a2], axis=1)], axis=0)
    return _pad_rows(body, rows)


def kernel(x, norm_gain, w_in, fox_forget_bias, rwkv_shift_mix, rwkv_w0, rwkv_w2, rwkv_a0, rwkv_a2, rwkv_k_k, rwkv_k_a, rwkv_r_k, rwkv_ln_w, rwkv_ln_b, w_proj_fox, w_proj_rwkv, w_out, final_norm_gain, loss_target, m_norm_gain, m_w_in, m_fox_forget_bias, m_rwkv_shift_mix, m_rwkv_w0, m_rwkv_w2, m_rwkv_a0, m_rwkv_a2, m_rwkv_k_k, m_rwkv_k_a, m_rwkv_r_k, m_rwkv_ln_w, m_rwkv_ln_b, m_w_proj_fox, m_w_proj_rwkv, m_w_out, m_final_norm_gain, v_norm_gain, v_w_in, v_fox_forget_bias, v_rwkv_shift_mix, v_rwkv_w0, v_rwkv_w2, v_rwkv_a0, v_rwkv_a2, v_rwkv_k_k, v_rwkv_k_a, v_rwkv_r_k, v_rwkv_ln_w, v_rwkv_ln_b, v_w_proj_fox, v_w_proj_rwkv, v_w_out, v_final_norm_gain):
    T, D = x.shape[1], x.shape[2]
    assert D == 2048 and T % LANE == 0
    NI = w_in.shape[2]
    IN = N_DEV * NI
    RB = 4224
    x2 = x[0]
    lt2 = loss_target[0]
    me = _slot(_place())

    wa, wb = _allgather_weights(
        w_in[0].astype(BF16),
        _pack_b(w_proj_fox[0], w_proj_rwkv[0], w_out[0], rwkv_w2[0], rwkv_a2[0], RB).astype(BF16))
    wf = wa.transpose(1, 0, 2).reshape(D, IN)
    zc = lambda n: jnp.zeros((D, n), BF16)
    w_pad = jnp.concatenate([wf[:, 0:4096], wf[:, 4104:8200], wf[:, 8392:12488], wf[:, 4096:4104], zc(120), zc(128),
                             wf[:, 8200:8296], zc(32), wf[:, 8296:8392], zc(32)], axis=1)
    NP = w_pad.shape[1]
    F_CB, LORA_CB = 96, 49
    wpf = wb[:, 0:1024, :].transpose(1, 0, 2).reshape(1024, D)
    wpr = wb[:, 1024:2048, :].transpose(1, 0, 2).reshape(1024, D)
    wo = wb[:, 2048:4096, :].reshape(N_DEV * 256, D)
    w2p = _pad_rows(wb[:, 4096:4192, 0:128].transpose(1, 0, 2).reshape(96, 1024).astype(F32), LANE)
    a2p = _pad_rows(wb[:, 4096:4192, 128:256].transpose(1, 0, 2).reshape(96, 1024).astype(F32), LANE)

    mu = rwkv_shift_mix
    mu_main = mu[:, 0:4096]
    mu_lora = jnp.concatenate([_pad_cols(mu[:, 4096:4192], LANE), _pad_cols(mu[:, 4192:4288], LANE)], axis=1)
    bias_pad = _pad_cols(fox_forget_bias, LANE)
    rk_flat = rwkv_r_k.reshape(1, 1024)
    gf = final_norm_gain.reshape(1, D)

    tb = min(256, T)
    tbh = min(128, T)
    (h,) = _rowcall("rms_fwd", T, tb, [_rows(x2, tb, D), _whole(norm_gain)],
                    lambda i, xv, g: ([_rms_math(xv, g)], []), [(D, BF16)])
    u = _mm(h, w_pad, name="mm_in")

    c8 = _gates_fwd(u, bias_pad, T, F_CB)
    c3 = c8.reshape(8, 1, T)
    o_raw, o_a, lse = _attn_fwd(u, c3, T)

    def shift_body(i, um, hm, ul, hl, mm_, ml):
        outs = []
        for uv, hv, mv in ((um, hm, mm_), (ul, hl, ml)):
            hv = jnp.where(i == 0, 0.0, hv)
            prev = pltpu.roll(jnp.concatenate([hv, uv], axis=0), 1, 0)[SUB:]
            outs.append(uv + (prev - uv) * mv)
        return outs, []

    def halo_prev(arr, w, cb):
        return (arr, (SUB, w), lambda i: (jnp.maximum(i * (tbh // SUB) - 1, 0), cb))

    xs, xl = _rowcall("rwkv_shift_fwd", T, tbh,
                      [_rows(u, tbh, 4096, 1), halo_prev(u, 4096, 1), _rows(u, tbh, 256, LORA_CB), halo_prev(u, 256, LORA_CB),
                       _whole(mu_main), _whole(mu_lora)],
                      shift_body, [(4096, F32), (256, F32)])

    prep_par = [_whole(rwkv_w0), _whole(rwkv_a0), _whole(rwkv_k_k), _whole(rwkv_k_a), _whole(w2p), _whole(a2p)]
    prep_rows = [_rows(xs, tbh, 1024, 1), _rows(xl, tbh, LANE, 0), _rows(xl, tbh, LANE, 1)]
    lw, kp, an, bb = _rowcall("rwkv_prep_fwd", T, tbh, prep_rows + prep_par,
                              lambda i, *a: (list(_prep_math(*a)), []), [(1024, F32)] * 4)
    y, states = _scan_fwd(xs, lw, kp, an, bb, T)
    post_rows = [_rows(y, tbh, 1024), _rows(xs, tbh, 1024, 0), _rows(kp, tbh, 1024), _rows(xs, tbh, 1024, 2), _rows(xs, tbh, 1024, 3)]
    post_par = [_whole(rwkv_ln_w), _whole(rwkv_ln_b), _whole(rk_flat)]
    (o_b,) = _rowcall("rwkv_post_fwd", T, tbh, post_rows + post_par,
                      lambda i, *a: ([_post_math(*a)], []), [(1024, BF16)])

    pa = _mm(o_a, wpf, name="mm_proj_fox")
    pb = _mm(o_b, wpr, name="mm_proj_rwkv")
    merge_rows = [_rows(u, tb, D, 4), _rows(u, tb, D, 5), _rows(pa, tb, D), _rows(pb, tb, D)]
    (mg,) = _rowcall("merge_fwd", T, tb, merge_rows, lambda i, *a: ([_merge_math(*a)], []), [(D, BF16)])
    mo = _mm(mg, wo, name="mm_out")

    def head_body(i, xv, mov, ltv, g):
        out = xv + mov
        r = lax.rsqrt(jnp.mean(out * out, axis=-1, keepdims=True) + RMS_EPS)
        yn = out * r
        err = yn * g - ltv
        loss = 0.5 * jnp.sum(jnp.sum(err * err, axis=-1, keepdims=True), axis=0, keepdims=True) / D
        dyv = err / D
        dyn = dyv * g
        dout = r * (dyn - yn * jnp.mean(dyn * yn, axis=-1, keepdims=True))
        return [dout], [loss, jnp.sum(dyv * yn, axis=0, keepdims=True)]

    dout, loss_p, dgf_p = _rowcall("loss_head", T, tb, [_rows(x2, tb, D), _rows(mo, tb, D), _rows(lt2, tb, D), _whole(gf)],
                                   head_body, [(D, F32)], [(1, 1), (1, D)])

    dm = _mm(dout, wo, tb=True, name="mm_out_dx")
    dwo = _mm(mg.T, dout, name="mm_out_dw")

    def merge_bwd_body(i, ga, gb, pav, pbv, dmv):
        _, vjp = jax.vjp(_merge_math, ga, gb, pav, pbv)
        dga, dgb, dpa, dpb = vjp(dmv)
        return [dga, dgb, dpa, dpb], []

    dga, dgb, dpa, dpb = _rowcall("merge_bwd", T, tb, merge_rows + [_rows(dm, tb, D)], merge_bwd_body,
                                  [(D, BF16), (D, BF16), (D, BF16), (D, BF16)])
    doa = _mm(dpa, wpf, tb=True, name="mm_proj_fox_dx")
    dwpf = _mm(o_a.T, dpa, name="mm_proj_fox_dw")
    dob = _mm(dpb, wpr, tb=True, name="mm_proj_rwkv_dx")
    dwpr = _mm(o_b.T, dpb, name="mm_proj_rwkv_dw")

    do_b, dza, delta = _attn_delta(u, c3, o_raw, lse, doa, T)
    dq, dk, dv, dc3 = _attn_bwd(u, c3, lse, do_b, delta, T)
    dfl, dbias_p = _gates_bwd(dc3.reshape(8, T), u, bias_pad, T, F_CB)

    def post_bwd_body(i, yv, rv, kpv, vv, zv, lnw, lnb, rkv, dobv):
        _, vjp = jax.vjp(_post_math, yv, rv, kpv, vv, zv, lnw, lnb, rkv)
        dy_, dr_, dkp_, dv_, dz_, dlnw, dlnb, drk = vjp(dobv)
        return [dy_, dr_, dkp_, dv_, dz_], [dlnw, dlnb, drk]

    dy_s, dr_p, dkp_p, dv_p, dzb, dlnw_p, dlnb_p, drk_p = _rowcall(
        "rwkv_post_bwd", T, tbh, post_rows + post_par + [_rows(dob, tbh, 1024)], post_bwd_body,
        [(1024, F32)] * 5, [(1, 1024)] * 3)
    dr_s, dlw, dkp_s, dv_s, dan, dbb = _scan_bwd(xs, lw, kp, an, bb, states, dy_s, T)

    def prep_bwd_body(i, xk, xwd, xad, w0, a0, kkw, kaw, w2v, a2v, dlw_, dkp1, dkp2, dan_, dbb_, dr1, dr2, dv1, dv2, dz_):
        _, vjp = jax.vjp(_prep_math, xk, xwd, xad, w0, a0, kkw, kaw, w2v, a2v)
        dxk, dxwd, dxad, dw0, da0, dkk, dka, dw2, da2 = vjp((dlw_, dkp1 + dkp2, dan_, dbb_))
        return [[dr1 + dr2, dxk, dv1 + dv2, dz_], [dxwd, dxad]], [dw0, da0, dkk, dka, dw2, da2]

    cots = [dlw, dkp_s, dkp_p, dan, dbb, dr_s, dr_p, dv_s, dv_p, dzb]
    dxs, dxl, dw0_p, da0_p, dkk_p, dka_p, dw2_p, da2_p = _rowcall(
        "rwkv_prep_bwd", T, tbh, prep_rows + prep_par + [_rows(c_, tbh, 1024) for c_ in cots], prep_bwd_body,
        [(4096, F32), (256, F32)], [(1, 1024)] * 4 + [(LANE, 1024)] * 2)

    def shift_bwd_body(i, dm_, hm, dl_, hl, um, pm, ul, pl_, mm_, ml):
        last = i == T // tbh - 1
        outs, accs = [], []
        for dv_, hv, uv, pv, mv in ((dm_, hm, um, pm, mm_), (dl_, hl, ul, pl_, ml)):
            hv = jnp.where(last, 0.0, hv)
            nxt = pltpu.roll(jnp.concatenate([dv_, hv], axis=0), tbh + SUB - 1, 0)[:tbh]
            pv = jnp.where(i == 0, 0.0, pv)
            prev = pltpu.roll(jnp.concatenate([pv, uv], axis=0), 1, 0)[SUB:]
            outs.append(dv_ * (1.0 - mv) + nxt * mv)
            accs.append(jnp.sum(dv_ * (prev - uv), axis=0, keepdims=True))
        return outs, accs

    def halo_next(arr, w, cb):
        last_blk = T // SUB - 1
        return (arr, (SUB, w), lambda i: (jnp.minimum((i + 1) * (tbh // SUB), last_blk), cb))

    du_b, du_l, dmu_main_p, dmu_lora_p = _rowcall(
        "rwkv_shift_bwd", T, tbh,
        [_rows(dxs, tbh, 4096), halo_next(dxs, 4096, 0), _rows(dxl, tbh, 256), halo_next(dxl, 256, 0),
         _rows(u, tbh, 4096, 1), halo_prev(u, 4096, 1), _rows(u, tbh, 256, LORA_CB), halo_prev(u, 256, LORA_CB),
         _whole(mu_main), _whole(mu_lora)],
        shift_bwd_body, [(4096, BF16), (256, BF16)], [(1, 4096), (1, 256)])

    du = jnp.concatenate([dq.astype(BF16), dk.astype(BF16), dv.astype(BF16), dza.astype(BF16), du_b, dga, dgb,
                          dfl.astype(BF16), jnp.zeros((T, LANE), BF16), du_l], axis=1)
    dh = _mm(du, w_pad, tb=True, tk=NP // 10, name="mm_in_dx")
    dw_pad = _mm(h.T, du, name="mm_in_dw")

    def rms_bwd_body(i, xv, g, dhv, doutv):
        _, vjp = jax.vjp(_rms_math, xv, g)
        dx_, dg_ = vjp(dhv)
        return [dx_ + doutv], [dg_]

    grad_x2, dng_p = _rowcall("rms_bwd", T, tb, [_rows(x2, tb, D), _whole(norm_gain), _rows(dh, tb, D), _rows(dout, tb, D)],
                              rms_bwd_body, [(D, F32)], [(1, D)])

    dw_full = jnp.concatenate([dw_pad[:, 0:4096], dw_pad[:, 12288:12296], dw_pad[:, 4096:8192], dw_pad[:, 12544:12640],
                               dw_pad[:, 12672:12768], dw_pad[:, 8192:12288]], axis=1)
    ga = dw_full.reshape(D, N_DEV, NI).transpose(1, 0, 2)
    ga_own = lax.dynamic_index_in_dim(ga, me, 0, keepdims=False)
    lora_g = jnp.concatenate([dw2_p[:96].reshape(96, N_DEV, 128).transpose(1, 0, 2),
                              da2_p[:96].reshape(96, N_DEV, 128).transpose(1, 0, 2)], axis=2)
    gb = jnp.concatenate([dwpf.reshape(1024, N_DEV, 256).transpose(1, 0, 2),
                          dwpr.reshape(1024, N_DEV, 256).transpose(1, 0, 2),
                          dwo.reshape(N_DEV, 2048, 256), lora_g, jnp.zeros((N_DEV, RB - 4192, 256), F32)], axis=1)
    gb_own = lax.dynamic_index_in_dim(gb, me, 0, keepdims=False)

    dmu = jnp.concatenate([dmu_main_p, dmu_lora_p[:, 0:96], dmu_lora_p[:, 128:224]], axis=1)
    small_parts = [dng_p, dbias_p[:, 0:8], dmu, dw0_p, da0_p, dkk_p, dka_p, drk_p, dlnw_p, dlnb_p, dgf_p, loss_p]
    SR = 128
    small = _pad_cols(jnp.concatenate(small_parts, axis=1), SR * LANE).reshape(SR, LANE)

    ra, rb, rs = _exchange_grads(ga.astype(BF16), gb.astype(BF16), small)

    g_in, d_in, m_in, v_in = _adamw("adamw_w_in", w_in[0], m_w_in[0], v_w_in[0], ga_own, ra, min(128, D))
    pk = lambda pf, pr, wo_, w2_, a2_: _pack_b(pf[0], pr[0], wo_[0], w2_[0], a2_[0], RB)
    outs_b = _adamw("adamw_packed", pk(w_proj_fox, w_proj_rwkv, w_out, rwkv_w2, rwkv_a2),
                    pk(m_w_proj_fox, m_w_proj_rwkv, m_w_out, m_rwkv_w2, m_rwkv_a2),
                    pk(v_w_proj_fox, v_w_proj_rwkv, v_w_out, v_rwkv_w2, v_rwkv_a2), gb_own, rb, 384)

    def pack_small(ng, fb, sm, w0, a0, kk_, ka_, rk_, lnw, lnb, fg):
        parts = [ng, fb, sm, w0, a0, kk_, ka_, rk_.reshape(1, 1024), lnw, lnb, fg.reshape(1, D), jnp.zeros((1, 1), F32)]
        return _pad_cols(jnp.concatenate(parts, axis=1), SR * LANE).reshape(SR, LANE)

    outs_s = _adamw("adamw_small",
                    pack_small(norm_gain, fox_forget_bias, rwkv_shift_mix, rwkv_w0, rwkv_a0, rwkv_k_k, rwkv_k_a, rwkv_r_k,
                               rwkv_ln_w, rwkv_ln_b, final_norm_gain),
                    pack_small(m_norm_gain, m_fox_forget_bias, m_rwkv_shift_mix, m_rwkv_w0, m_rwkv_a0, m_rwkv_k_k, m_rwkv_k_a,
                               m_rwkv_r_k, m_rwkv_ln_w, m_rwkv_ln_b, m_final_norm_gain),
                    pack_small(v_norm_gain, v_fox_forget_bias, v_rwkv_shift_mix, v_rwkv_w0, v_rwkv_a0, v_rwkv_k_k, v_rwkv_k_a,
                               v_rwkv_r_k, v_rwkv_ln_w, v_rwkv_ln_b, v_final_norm_gain),
                    None, rs, SR)

    def unpack_b(pkd):
        return dict(w_proj_fox=pkd[0:1024][None], w_proj_rwkv=pkd[1024:2048][None], w_out=pkd[2048:4096].reshape(1, 256, D),
                    rwkv_w2=pkd[4096:4192, 0:128][None], rwkv_a2=pkd[4096:4192, 128:256][None])

    def unpack_s(pkd):
        flat = pkd.reshape(1, SR * LANE)
        names = [("norm_gain", D), ("fox_forget_bias", 8), ("rwkv_shift_mix", 4288), ("rwkv_w0", 1024), ("rwkv_a0", 1024),
                 ("rwkv_k_k", 1024), ("rwkv_k_a", 1024), ("rwkv_r_k", 1024), ("rwkv_ln_w", 1024), ("rwkv_ln_b", 1024),
                 ("final_norm_gain", D), ("loss", 1)]
        out, off = {}, 0
        for nm, n in names:
            out[nm] = flat[:, off:off + n]
            off += n
        out["rwkv_r_k"] = out["rwkv_r_k"].reshape(1, 16, 64)
        out["final_norm_gain"] = out["final_norm_gain"].reshape(D)
        return out

    order = ["norm_gain", "w_in", "fox_forget_bias", "rwkv_shift_mix", "rwkv_w0", "rwkv_w2", "rwkv_a0", "rwkv_a2", "rwkv_k_k",
             "rwkv_k_a", "rwkv_r_k", "rwkv_ln_w", "rwkv_ln_b", "w_proj_fox", "w_proj_rwkv", "w_out", "final_norm_gain"]
    result = []
    loss = None
    for kind, big in enumerate((g_in, d_in, m_in, v_in)):
        d = {**unpack_b(outs_b[kind]), **unpack_s(outs_s[kind]), "w_in": big[None]}
        if kind == 0:
            loss = d["loss"].reshape(())
        result += [d[n] for n in order]
    return (loss, grad_x2[None], *result)
```

```python
import functools

import jax
import jax.numpy as jnp
from jax import lax
from jax.experimental import pallas as pl
from jax.experimental.pallas import tpu as pltpu

F32 = jnp.float32
BF16 = jnp.bfloat16
HI = lax.Precision.HIGHEST
MESH = pl.DeviceIdType.MESH

FOX_HD = 128
RW_HD = 64
RMS_EPS = 1e-6
GN_EPS = 64e-5
L2_EPS = 1e-12
ADAM_LR = 0.001
ADAM_B1 = 0.9
ADAM_B2 = 0.999
ADAM_EPS = 1e-08
ADAM_WD = 0.01
ADAM_STEP = 10

LANE = 128
SUB = 8
VMEM_LIMIT = 56 * 1024 * 1024
N_DEV = 8
CHUNK = 64
SCAN_GROUP = 4
PS = None
NEG = -1e30


def _scan_shape(T):
    c = min(CHUNK, T)
    return c, min(SCAN_GROUP, T // c)


def _cp(sem=None):
    return pltpu.CompilerParams(dimension_semantics=sem, vmem_limit_bytes=VMEM_LIMIT)


def _sigmoid(x):
    return jax.nn.sigmoid(x)


def _softplus(x):
    return jnp.maximum(x, 0.0) + jnp.log(1.0 + jnp.exp(-jnp.abs(x)))


def _nn(a, b, prec=None):
    return lax.dot_general(a, b, (((1,), (0,)), ((), ())), precision=prec, preferred_element_type=F32)


def _nt(a, b, prec=None):
    return lax.dot_general(a, b, (((1,), (1,)), ((), ())), precision=prec, preferred_element_type=F32)


def _tn(a, b, prec=None):
    return lax.dot_general(a, b, (((0,), (0,)), ((), ())), precision=prec, preferred_element_type=F32)


def _iota2(shape, dim):
    return lax.broadcasted_iota(jnp.int32, shape, dim)


def _seg_sum(x):
    r = _iota2((LANE, LANE), 0) // RW_HD
    c = _iota2((LANE, LANE), 1) // RW_HD
    bd = (r == c).astype(F32)
    parts = [_nn(x[:, j * LANE:(j + 1) * LANE], bd, HI) for j in range(x.shape[1] // LANE)]
    return parts[0] if len(parts) == 1 else jnp.concatenate(parts, axis=1)


def _mm(a, b, *, tb=False, out_dtype=F32, tm=512, tn=512, tk=None, name):
    M, K = a.shape
    N = b.shape[0] if tb else b.shape[1]
    tm, tn = min(tm, M), min(tn, N)
    tk = K if tk is None else tk
    nk = K // tk
    assert M % tm == 0 and N % tn == 0 and K % tk == 0
    b_spec = pl.BlockSpec((tn, tk), lambda i, j, k: (j, k)) if tb else pl.BlockSpec((tk, tn), lambda i, j, k: (k, j))

    def body(a_ref, b_ref, o_ref, *scr):
        av = a_ref[...].astype(BF16)
        bv = b_ref[...].astype(BF16)
        p = _nt(av, bv) if tb else _nn(av, bv)
        if nk == 1:
            o_ref[...] = p.astype(out_dtype)
        else:
            acc_ref = scr[0]
            k = pl.program_id(2)

            @pl.when(k == 0)
            def _():
                acc_ref[...] = p

            @pl.when(k > 0)
            def _():
                acc_ref[...] += p

            @pl.when(k == nk - 1)
            def _():
                o_ref[...] = acc_ref[...].astype(out_dtype)

    return pl.pallas_call(
        body, name=name,
        out_shape=jax.ShapeDtypeStruct((M, N), out_dtype),
        grid=(M // tm, N // tn, nk),
        in_specs=[pl.BlockSpec((tm, tk), lambda i, j, k: (i, k)), b_spec],
        out_specs=pl.BlockSpec((tm, tn), lambda i, j, k: (i, j)),
        scratch_shapes=[] if nk == 1 else [pltpu.VMEM((tm, tn), F32)],
        compiler_params=_cp(("parallel", "parallel", "arbitrary")),
    )(a, b)


def _rows(arr, tb, w, cb=0):
    return (arr, (tb, w), lambda i: (i, cb))


def _whole(arr):
    nd = arr.ndim
    return (arr, arr.shape, lambda i: (0,) * nd)


def _rowcall(name, T, tb, ins, body, outs, accs=()):
    n_in, n_out = len(ins), len(outs)

    def kern(*refs):
        i = pl.program_id(0)
        vals = [r[...] for r in refs[:n_in]]
        ro, ao = body(i, *vals)
        for r, v in zip(refs[n_in:n_in + n_out], ro):
            if isinstance(v, (list, tuple)):
                off = 0
                for piece in v:
                    w = piece.shape[1]
                    r[:, off:off + w] = piece.astype(r.dtype)
                    off += w
            else:
                r[...] = v.astype(r.dtype)
        if accs:
            acc_refs = refs[n_in + n_out:]

            @pl.when(i == 0)
            def _():
                for r in acc_refs:
                    r[...] = jnp.zeros(r.shape, F32)

            for r, v in zip(acc_refs, ao):
                r[...] += v

    out_shape = [jax.ShapeDtypeStruct((T, w), dt) for (w, dt) in outs] + [jax.ShapeDtypeStruct(s, F32) for s in accs]
    out_specs = [pl.BlockSpec((tb, w), lambda i: (i, 0)) for (w, dt) in outs] + [pl.BlockSpec(s, lambda i: (0, 0)) for s in accs]
    res = pl.pallas_call(
        kern, name=name,
        out_shape=out_shape,
        grid=(T // tb,),
        in_specs=[pl.BlockSpec(bs, im) for (_, bs, im) in ins],
        out_specs=out_specs,
        compiler_params=_cp(("arbitrary",)),
    )(*[a for (a, _, _) in ins])
    return res


def _rms_math(x, g):
    r = lax.rsqrt(jnp.mean(x * x, axis=-1, keepdims=True) + RMS_EPS)
    return x * r * g


def _merge_math(ga, gb, pa, pb):
    return _sigmoid(ga) * pa + _sigmoid(gb) * pb


def _prep_math(xk, xwd, xad, w0, a0, kk_w, ka_w, w2p, a2p):
    z = w0 + _nn(jnp.tanh(xwd), w2p, HI)
    w = -_softplus(-z) - 0.5
    lw = -jnp.exp(w)
    ag = _sigmoid(a0 + _nn(xad, a2p, HI))
    p = xk * kk_w
    n = jnp.maximum(jnp.sqrt(_seg_sum(p * p)), L2_EPS)
    kk = p / n
    kp = xk * (1.0 + (ag - 1.0) * ka_w)
    return lw, kp, -kk, kk * ag


def _post_math(y, r, kp, v, z, lnw, lnb, rk):
    inv = 1.0 / RW_HD
    mu = _seg_sum(y) * inv
    d = y - mu
    var = _seg_sum(d * d) * inv
    yn = d * lax.rsqrt(var + GN_EPS) * lnw + lnb
    bonus = _seg_sum(r * kp * rk) * v
    return (yn + bonus) * (z * _sigmoid(z))


def _scan_group(s0, *flat):
    G = len(flat) // 6
    ch = [flat[6 * i:6 * i + 6] for i in range(G)]
    C = ch[0][0].shape[0]
    C2 = 2 * C
    cat = jnp.concatenate
    m0 = _iota2((1, LANE), 1) < RW_HD
    mask0 = m0.astype(F32)
    mask1 = 1.0 - mask0
    r2 = _iota2((C2, C2), 0)
    c2 = _iota2((C2, C2), 1)
    dist = r2 - c2
    in_head = dist <= r2 % C
    eye = (r2 == c2).astype(F32)
    lower = (_iota2((C, C), 0) >= _iota2((C, C), 1)).astype(F32)
    bd = (_iota2((LANE, LANE), 0) // RW_HD) == (_iota2((LANE, LANE), 1) // RW_HD)

    def tri(m, strict):
        return jnp.where(dist > 0 if strict else dist >= 0, jnp.where(in_head, m, 0.0), 0.0)

    def sel(z):
        return jnp.where(m0, z[:C], z[C:])

    gs = [_nn(lower, c[1], HI) for c in ch]
    pre = []
    for (r, lw, k, v, a, b), g in zip(ch, gs):
        g_end = jnp.sum(lw, axis=0, keepdims=True)
        en = jnp.exp(-g)
        ec = jnp.exp(g_end - g)
        pre.append(dict(at=a * jnp.exp(g - lw), rt=r * jnp.exp(g), bt=b * en, kt=k * en, bh=b * ec, kh=k * ec,
                        dec=jnp.exp(g_end), v=v))
    grams = [_nt(cat([p["at"] * mask0, p["at"] * mask1, p["rt"] * mask0, p["rt"] * mask1], axis=0),
                 cat([p["bt"], p["bt"], p["kt"], p["kt"]], axis=0), PS) for p in pre]
    mab = [tri(gm[:C2, :C2], True) for gm in grams]
    mp = [_nn(m, m, PS) for m in mab]
    tinv = [eye + m for m in mab]
    n = 2
    while n < C:
        last = 2 * n >= C
        for i in range(G):
            if last:
                tinv[i] = tinv[i] + _nn(mp[i], tinv[i], PS)
            else:
                z = _nn(mp[i], cat([mp[i], tinv[i]], axis=1), PS)
                mp[i], tinv[i] = z[:, :C2], tinv[i] + z[:, C2:]
        n *= 2
    xv = [sel(_nn(tri(gm[:C2, C2:], True), cat([p["v"], p["v"]], axis=0), PS)) for gm, p in zip(grams, pre)]
    ys, s = [], s0
    for i in range(G):
        p, gm = pre[i], grams[i]
        sx = _nt(cat([p["at"], p["rt"]], axis=0), s, PS)
        x = sx[:C] + xv[i]
        u = sel(_nn(tinv[i], cat([x, x], axis=0), PS))
        v = p["v"]
        ys.append(sx[C:] + sel(_nn(cat([tri(gm[C2:, :C2], False), tri(gm[C2:, C2:], False)], axis=1), cat([u, u, v, v], axis=0), PS)))
        s = s * p["dec"] + jnp.where(bd, _tn(cat([u, v], axis=0), cat([p["bh"], p["kh"]], axis=0), PS), 0.0)
    return tuple(ys), s


def _scan_fwd(xs, lw, kp, an, bb, T):
    C, G = _scan_shape(T)
    nc = T // (C * G)
    npair = 1024 // LANE

    def kern(r_ref, lw_ref, k_ref, v_ref, a_ref, b_ref, y_ref, st_ref, s_scr):
        n = pl.program_id(1)

        @pl.when(n == 0)
        def _():
            s_scr[...] = jnp.zeros(s_scr.shape, F32)

        s0 = s_scr[...]
        st_ref[0, 0] = s0
        ins = (r_ref, lw_ref, k_ref, v_ref, a_ref, b_ref)
        ys, s1 = _scan_group(s0, *[ref[i * C:(i + 1) * C, :] for i in range(G) for ref in ins])
        for i in range(G):
            y_ref[i * C:(i + 1) * C, :] = ys[i]
        s_scr[...] = s1

    def col(off):
        return pl.BlockSpec((C * G, LANE), lambda p, n: (n, off + p))

    return pl.pallas_call(
        kern, name="rwkv_scan_fwd",
        out_shape=[jax.ShapeDtypeStruct((T, 1024), F32), jax.ShapeDtypeStruct((nc, npair, LANE, LANE), F32)],
        grid=(npair, nc),
        in_specs=[col(0), col(0), col(0), col(16), col(0), col(0)],
        out_specs=[col(0), pl.BlockSpec((1, 1, LANE, LANE), lambda p, n: (n, p, 0, 0))],
        scratch_shapes=[pltpu.VMEM((LANE, LANE), F32)],
        compiler_params=_cp(("parallel", "arbitrary")),
    )(xs, lw, kp, xs, an, bb)


def _scan_bwd(xs, lw, kp, an, bb, states, dy, T):
    C, G = _scan_shape(T)
    nc = T // (C * G)
    npair = 1024 // LANE

    def kern(r_ref, lw_ref, k_ref, v_ref, a_ref, b_ref, st_ref, dy_ref, dr_ref, dlw_ref, dk_ref, dv_ref, da_ref, db_ref, ds_scr):
        n = pl.program_id(1)

        @pl.when(n == 0)
        def _():
            ds_scr[...] = jnp.zeros(ds_scr.shape, F32)

        ins = (r_ref, lw_ref, k_ref, v_ref, a_ref, b_ref)
        _, vjp = jax.vjp(_scan_group, st_ref[0, 0], *[ref[i * C:(i + 1) * C, :] for i in range(G) for ref in ins])
        grads = vjp((tuple(dy_ref[i * C:(i + 1) * C, :] for i in range(G)), ds_scr[...]))
        ds_scr[...] = grads[0]
        outs = (dr_ref, dlw_ref, dk_ref, dv_ref, da_ref, db_ref)
        for i in range(G):
            for t, ref in enumerate(outs):
                ref[i * C:(i + 1) * C, :] = grads[1 + 6 * i + t]

    def col(off):
        return pl.BlockSpec((C * G, LANE), lambda p, n: (nc - 1 - n, off + p))

    return pl.pallas_call(
        kern, name="rwkv_scan_bwd",
        out_shape=[jax.ShapeDtypeStruct((T, 1024), F32)] * 6,
        grid=(npair, nc),
        in_specs=[col(0), col(0), col(0), col(16), col(0), col(0),
                  pl.BlockSpec((1, 1, LANE, LANE), lambda p, n: (nc - 1 - n, p, 0, 0)), col(0)],
        out_specs=[col(0)] * 6,
        scratch_shapes=[pltpu.VMEM((LANE, LANE), F32)],
        compiler_params=_cp(("parallel", "arbitrary")),
    )(xs, lw, kp, xs, an, bb, states, dy)


def _gates_fwd(u, bias_pad, T, f_cb):
    nb = T // LANE

    def kern(f_ref, b_ref, c_ref):
        x = f_ref[...] + b_ref[...]
        lf = jnp.minimum(x, 0.0) - jnp.log(1.0 + jnp.exp(-jnp.abs(x)))
        lft = lf.T
        ut = (_iota2((LANE, LANE), 0) <= _iota2((LANE, LANE), 1)).astype(F32)
        carry = jnp.zeros((LANE, 1), F32)
        for blk in range(nb):
            seg = lft[:, blk * LANE:(blk + 1) * LANE]
            cs = _nn(seg, ut, HI) + carry
            c_ref[:, blk * LANE:(blk + 1) * LANE] = cs[:SUB, :]
            carry = carry + jnp.sum(seg, axis=1, keepdims=True)

    return pl.pallas_call(
        kern, name="fox_gates_fwd",
        out_shape=jax.ShapeDtypeStruct((SUB, T), F32),
        grid=(1,),
        in_specs=[pl.BlockSpec((T, LANE), lambda i: (0, f_cb)), pl.BlockSpec((1, LANE), lambda i: (0, 0))],
        out_specs=pl.BlockSpec((SUB, T), lambda i: (0, 0)),
        compiler_params=_cp(("arbitrary",)),
    )(u, bias_pad)


def _gates_bwd(dc, u, bias_pad, T, f_cb):
    nb = T // LANE

    def kern(dc_ref, f_ref, b_ref, dfl_ref, db_ref):
        dcv = jnp.concatenate([dc_ref[...], jnp.zeros((LANE - SUB, T), F32)], axis=0)
        lt = (_iota2((LANE, LANE), 0) >= _iota2((LANE, LANE), 1)).astype(F32)
        carry = jnp.zeros((LANE, 1), F32)
        pieces = [None] * nb
        for blk in range(nb - 1, -1, -1):
            seg = dcv[:, blk * LANE:(blk + 1) * LANE]
            pieces[blk] = _nn(seg, lt, HI) + carry
            carry = carry + jnp.sum(seg, axis=1, keepdims=True)
        dlf = (pieces[0] if nb == 1 else jnp.concatenate(pieces, axis=1)).T
        x = f_ref[...] + b_ref[...]
        dfl = dlf * _sigmoid(-x)
        dfl_ref[...] = dfl
        db_ref[...] = jnp.sum(dfl, axis=0, keepdims=True)

    return pl.pallas_call(
        kern, name="fox_gates_bwd",
        out_shape=[jax.ShapeDtypeStruct((T, LANE), F32), jax.ShapeDtypeStruct((1, LANE), F32)],
        grid=(1,),
        in_specs=[pl.BlockSpec((SUB, T), lambda i: (0, 0)), pl.BlockSpec((T, LANE), lambda i: (0, f_cb)),
                  pl.BlockSpec((1, LANE), lambda i: (0, 0))],
        out_specs=[pl.BlockSpec((T, LANE), lambda i: (0, 0)), pl.BlockSpec((1, LANE), lambda i: (0, 0))],
        compiler_params=_cp(("arbitrary",)),
    )(dc, u, bias_pad)


def _attn_block(T):
    return 256 if T % 256 == 0 and T >= 512 else 128


def _attn_fwd(u, c3, T):
    H = 8
    bq = _attn_block(T)
    nq = T // bq
    scale = FOX_HD ** -0.5

    def kern(q_ref, k_ref, v_ref, z_ref, cq_ref, ck_ref, o_ref, oa_ref, lse_ref):
        i = pl.program_id(1)
        q = (q_ref[...] * scale).astype(BF16)
        c0 = cq_ref[0][:, 0:1]
        row = _iota2((bq, bq), 0) + i * bq

        def step(j, carry):
            m, l, acc = carry
            off = pl.multiple_of(j * bq, bq)
            kj = k_ref[pl.ds(off, bq), :].astype(BF16)
            vj = v_ref[pl.ds(off, bq), :].astype(BF16)
            s = _nt(q, kj) + (c0 - ck_ref[0, :, pl.ds(off, bq)])
            s = jnp.where(_iota2((bq, bq), 1) + j * bq <= row, s, NEG)
            m_new = jnp.maximum(m, jnp.max(s, axis=1, keepdims=True))
            p = jnp.exp(s - m_new)
            alpha = jnp.exp(m - m_new)
            l = alpha * l + jnp.sum(p, axis=1, keepdims=True)
            acc = alpha * acc + _nn(p.astype(BF16), vj)
            return m_new, l, acc

        init = (jnp.full((bq, 1), NEG, F32), jnp.zeros((bq, 1), F32), jnp.zeros((bq, FOX_HD), F32))
        m, l, acc = lax.fori_loop(0, i + 1, step, init)
        o = acc / l
        z = z_ref[...]
        o_ref[...] = o
        oa_ref[...] = (o * z * _sigmoid(z)).astype(BF16)
        lse_ref[0] = m + jnp.log(l)

    return pl.pallas_call(
        kern, name="fox_attn_fwd",
        out_shape=[jax.ShapeDtypeStruct((T, 1024), F32), jax.ShapeDtypeStruct((T, 1024), BF16),
                   jax.ShapeDtypeStruct((H, T, 1), F32)],
        grid=(H, nq),
        in_specs=[pl.BlockSpec((bq, LANE), lambda h, i: (i, h)),
                  pl.BlockSpec((T, LANE), lambda h, i: (0, 8 + h)),
                  pl.BlockSpec((T, LANE), lambda h, i: (0, 16 + h)),
                  pl.BlockSpec((bq, LANE), lambda h, i: (i, 24 + h)),
                  pl.BlockSpec((1, 1, bq), lambda h, i: (h, 0, i)),
                  pl.BlockSpec((1, 1, T), lambda h, i: (h, 0, 0))],
        out_specs=[pl.BlockSpec((bq, LANE), lambda h, i: (i, h)),
                   pl.BlockSpec((bq, LANE), lambda h, i: (i, h)),
                   pl.BlockSpec((1, bq, 1), lambda h, i: (h, i, 0))],
        compiler_params=_cp(("parallel", "arbitrary")),
    )(u, u, u, u, c3, c3)


def _attn_tile(qs, kj, vj, dob, cdiff, lse_i, keep):
    s = _nt(qs, kj) + cdiff
    p = jnp.where(keep, jnp.exp(jnp.where(keep, s, NEG) - lse_i), 0.0)
    return p, _nt(dob, vj)


def _attn_delta(u, c3, o, lse, doa, T):
    H = 8
    bq = _attn_block(T)
    nq = T // bq
    scale = FOX_HD ** -0.5

    def kern(q_ref, k_ref, v_ref, z_ref, c_ref, o_ref, lse_ref, doa_ref, do_ref, dz_ref, dl_ref):
        i = pl.program_id(1)
        z = z_ref[...]
        sg = _sigmoid(z)
        dov = doa_ref[...]
        dob = (dov * z * sg).astype(BF16)
        do_ref[...] = dob
        dz_ref[...] = dov * o_ref[...] * (sg * (1.0 + z * (1.0 - sg)))
        qs = (q_ref[...] * scale).astype(BF16)
        ioff = pl.multiple_of(i * bq, bq)
        c0 = c_ref[0, :, pl.ds(ioff, bq)][:, 0:1]
        lse_i = lse_ref[0]
        row = _iota2((bq, bq), 0) + i * bq

        def step(j, acc):
            off = pl.multiple_of(j * bq, bq)
            kj = k_ref[pl.ds(off, bq), :].astype(BF16)
            vj = v_ref[pl.ds(off, bq), :].astype(BF16)
            keep = _iota2((bq, bq), 1) + j * bq <= row
            p, dp = _attn_tile(qs, kj, vj, dob, c0 - c_ref[0, :, pl.ds(off, bq)], lse_i, keep)
            return acc + jnp.sum(p * dp, axis=1, keepdims=True)

        dl_ref[0] = lax.fori_loop(0, i + 1, step, jnp.zeros((bq, 1), F32))

    full = lambda cb: pl.BlockSpec((T, LANE), lambda h, i: (0, cb + h))
    blk = lambda cb: pl.BlockSpec((bq, LANE), lambda h, i: (i, cb + h))
    return pl.pallas_call(
        kern, name="fox_attn_delta",
        out_shape=[jax.ShapeDtypeStruct((T, 1024), BF16), jax.ShapeDtypeStruct((T, 1024), F32), jax.ShapeDtypeStruct((H, T, 1), F32)],
        grid=(H, nq),
        in_specs=[blk(0), full(8), full(16), blk(24),
                  pl.BlockSpec((1, 1, T), lambda h, i: (h, 0, 0)),
                  blk(0),
                  pl.BlockSpec((1, bq, 1), lambda h, i: (h, i, 0)),
                  blk(0)],
        out_specs=[blk(0), blk(0), pl.BlockSpec((1, bq, 1), lambda h, i: (h, i, 0))],
        compiler_params=_cp(("parallel", "arbitrary")),
    )(u, u, u, u, c3, o, lse, doa)


def _attn_bwd(u, c3, lse, do, delta, T):
    H = 8
    bq = _attn_block(T)
    nq = T // bq
    scale = FOX_HD ** -0.5

    def kern(q_ref, k_ref, v_ref, c_ref, lse_ref, do_ref, dl_ref, dq_ref, dk_ref, dv_ref, dc_ref):
        j = pl.program_id(1)

        @pl.when(j == 0)
        def _():
            dq_ref[...] = jnp.zeros(dq_ref.shape, F32)

        kj = k_ref[...].astype(BF16)
        vj = v_ref[...].astype(BF16)
        joff = pl.multiple_of(j * bq, bq)
        ck = c_ref[0, :, pl.ds(joff, bq)]
        col = _iota2((bq, bq), 1) + j * bq

        def step(i, carry):
            dk, dv, dc = carry
            off = pl.multiple_of(i * bq, bq)
            qs = (q_ref[pl.ds(off, bq), :] * scale).astype(BF16)
            dob = do_ref[pl.ds(off, bq), :]
            c0 = c_ref[0, :, pl.ds(off, bq)][:, 0:1]
            keep = col <= _iota2((bq, bq), 0) + i * bq
            p, dp = _attn_tile(qs, kj, vj, dob, c0 - ck, lse_ref[0, pl.ds(off, bq), :], keep)
            ds = p * (dp - dl_ref[0, pl.ds(off, bq), :])
            dsb = ds.astype(BF16)
            dv = dv + _tn(p.astype(BF16), dob)
            dk = dk + _tn(dsb, qs)
            dq_ref[pl.ds(off, bq), :] += _nn(dsb, kj) * scale
            dc = dc - jnp.sum(ds, axis=0, keepdims=True)
            return dk, dv, dc

        init = (jnp.zeros((bq, FOX_HD), F32), jnp.zeros((bq, FOX_HD), F32), jnp.zeros((1, bq), F32))
        dk, dv, dc = lax.fori_loop(j, nq, step, init)
        dk_ref[...] = dk
        dv_ref[...] = dv
        dc_ref[0] = dc

    full = lambda cb: pl.BlockSpec((T, LANE), lambda h, j: (0, cb + h))
    blk = lambda cb: pl.BlockSpec((bq, LANE), lambda h, j: (j, cb + h))
    col1 = pl.BlockSpec((1, T, 1), lambda h, j: (h, 0, 0))
    return pl.pallas_call(
        kern, name="fox_attn_bwd",
        out_shape=[jax.ShapeDtypeStruct((T, 1024), F32)] * 3 + [jax.ShapeDtypeStruct((H, 1, T), F32)],
        grid=(H, nq),
        in_specs=[full(0), blk(8), blk(16), pl.BlockSpec((1, 1, T), lambda h, j: (h, 0, 0)), col1, full(0), col1],
        out_specs=[full(0), blk(0), blk(0), pl.BlockSpec((1, 1, bq), lambda h, j: (h, 0, j))],
        compiler_params=_cp(("parallel", "arbitrary")),
    )(u, u, u, c3, lse, do, delta)


def _place():
    return lax.axis_index("x"), lax.axis_index("y"), lax.axis_index("c")


def _slot(p):
    return 4 * p[0] + 2 * p[1] + p[2]


def _allgather_weights(a, b):
    def body(a_ref, b_ref, ao_ref, bo_ref, send_sems, recv_sems, local_sems):
        x, y, c = _place()
        me, sibling = (x, y, c), (x, y, 1 - c)
        chips = [(1 - x, y), (x, 1 - y), (1 - x, 1 - y)]
        srcs, dsts = (a_ref, b_ref), (ao_ref, bo_ref)

        def copy(t, k, block, to, from_input=False):
            dst = dsts[t].at[_slot(block)]
            return pltpu.make_async_remote_copy(
                src_ref=srcs[t] if from_input else dst, dst_ref=dst,
                send_sem=send_sems.at[t, k], recv_sem=recv_sems.at[t, k],
                device_id=to, device_id_type=MESH)

        mine = [pltpu.make_async_copy(srcs[t], dsts[t].at[_slot(me)], local_sems.at[t]) for t in range(2)]
        for cp in mine:
            cp.start()
        first = []
        for t in range(2):
            first.append(copy(t, 0, me, sibling, True))
            first += [copy(t, 1 + j, me, (*chip, c), True) for j, chip in enumerate(chips)]
        for cp in first:
            cp.start()
        passed = []
        for j, chip in enumerate(chips):
            for t in range(2):
                copy(t, 1 + j, (*chip, c), me).wait_recv()
                fwd = copy(t, 4 + j, (*chip, c), sibling)
                fwd.start()
                passed.append(fwd)
        for t in range(2):
            copy(t, 0, sibling, me).wait_recv()
            for j, chip in enumerate(chips):
                copy(t, 4 + j, (*chip, 1 - c), me).wait_recv()
        for cp in first + passed:
            cp.wait_send()
        for cp in mine:
            cp.wait()

    any_spec = pl.BlockSpec(memory_space=pl.ANY)
    return pl.pallas_call(
        body, name="allgather_weights",
        out_shape=[jax.ShapeDtypeStruct((N_DEV,) + a.shape, a.dtype), jax.ShapeDtypeStruct((N_DEV,) + b.shape, b.dtype)],
        in_specs=[any_spec, any_spec],
        out_specs=[any_spec, any_spec],
        scratch_shapes=[pltpu.SemaphoreType.DMA((2, 7)), pltpu.SemaphoreType.DMA((2, 7)), pltpu.SemaphoreType.DMA((2,))],
    )(a, b)


def _exchange_grads(ga, gb, small):
    def body(ga_ref, gb_ref, sm_ref, ra_ref, rb_ref, rs_ref, send_sems, recv_sems, local_sems):
        x, y, c = _place()
        me = _slot((x, y, c))

        def peer(k):
            return (1 - x if k & 4 else x, 1 - y if k & 2 else y, 1 - c if k & 1 else c)

        def copies(k):
            p = peer(k)
            ps = _slot(p)
            srcs = (ga_ref.at[ps], gb_ref.at[ps], sm_ref)
            out = []
            for t, (src, dst) in enumerate(zip(srcs, (ra_ref, rb_ref, rs_ref))):
                out.append(pltpu.make_async_remote_copy(
                    src_ref=src, dst_ref=dst.at[me], send_sem=send_sems.at[t, k - 1], recv_sem=recv_sems.at[t, k - 1],
                    device_id=p, device_id_type=MESH))
            return out

        def arrivals(k):
            ps = _slot(peer(k))
            out = []
            for t, dst in enumerate((ra_ref, rb_ref, rs_ref)):
                out.append(pltpu.make_async_remote_copy(
                    src_ref=dst.at[ps], dst_ref=dst.at[ps], send_sem=send_sems.at[t, k - 1], recv_sem=recv_sems.at[t, k - 1],
                    device_id=peer(k), device_id_type=MESH))
            return out

        mine = [pltpu.make_async_copy(src, dst.at[me], local_sems.at[t])
                for t, (src, dst) in enumerate(zip((ga_ref.at[me], gb_ref.at[me], sm_ref), (ra_ref, rb_ref, rs_ref)))]
        for cp in mine:
            cp.start()
        sent = []
        for k in range(1, N_DEV):
            for cp in copies(k):
                cp.start()
                sent.append(cp)
        for k in range(1, N_DEV):
            for cp in arrivals(k):
                cp.wait_recv()
        for cp in sent:
            cp.wait_send()
        for cp in mine:
            cp.wait()

    any_spec = pl.BlockSpec(memory_space=pl.ANY)
    return pl.pallas_call(
        body, name="exchange_grads",
        out_shape=[jax.ShapeDtypeStruct(ga.shape, ga.dtype), jax.ShapeDtypeStruct(gb.shape, gb.dtype),
                   jax.ShapeDtypeStruct((N_DEV,) + small.shape, small.dtype)],
        in_specs=[any_spec] * 3,
        out_specs=[any_spec] * 3,
        scratch_shapes=[pltpu.SemaphoreType.DMA((3, 7)), pltpu.SemaphoreType.DMA((3, 7)), pltpu.SemaphoreType.DMA((3,))],
    )(ga, gb, small)


def _adamw(name, w, m, v, g_own, recv, tb):
    R, Cc = w.shape
    assert R % tb == 0

    def kern(*refs):
        if g_own is None:
            w_ref, m_ref, v_ref, r_ref = refs[:4]
            outs = refs[4:]
            g = r_ref[0].astype(F32)
            for s in range(1, N_DEV):
                g = g + r_ref[s].astype(F32)
        else:
            w_ref, m_ref, v_ref, g_ref, r_ref = refs[:5]
            outs = refs[5:]
            me = _slot(_place())
            g = g_ref[...]
            for s in range(N_DEV):
                g = g + jnp.where(me == s, 0.0, r_ref[s].astype(F32))
        g_out, d_out, m_out, v_out = outs
        mn = ADAM_B1 * m_ref[...] + (1.0 - ADAM_B1) * g
        vn = ADAM_B2 * v_ref[...] + (1.0 - ADAM_B2) * (g * g)
        m_hat = mn / (1.0 - ADAM_B1 ** ADAM_STEP)
        v_hat = vn / (1.0 - ADAM_B2 ** ADAM_STEP)
        g_out[...] = g
        d_out[...] = -ADAM_LR * (m_hat / (jnp.sqrt(v_hat) + ADAM_EPS) + ADAM_WD * w_ref[...])
        m_out[...] = mn
        v_out[...] = vn

    blk = pl.BlockSpec((tb, Cc), lambda i: (i, 0))
    rblk = pl.BlockSpec((N_DEV, tb, Cc), lambda i: (0, i, 0))
    ins = [w, m, v] + ([] if g_own is None else [g_own]) + [recv]
    return pl.pallas_call(
        kern, name=name,
        out_shape=[jax.ShapeDtypeStruct((R, Cc), F32)] * 4,
        grid=(R // tb,),
        in_specs=[blk] * (len(ins) - 1) + [rblk],
        out_specs=[blk] * 4,
        compiler_params=_cp(("arbitrary",)),
    )(*ins)


def _pad_cols(a, w):
    return jnp.pad(a, ((0, 0), (0, w - a.shape[1])))


def _pad_rows(a, r):
    return jnp.pad(a, ((0, r - a.shape[0]), (0, 0)))


def _pack_b(pf, pr, wo, w2, a2, rows):
    body = jnp.concatenate([pf, pr, wo.reshape(2048, 256), jnp.concatenate([w2, a2], axis=1)], axis=0)
    return _pad_rows(body, rows)


def kernel(x, norm_gain, w_in, fox_forget_bias, rwkv_shift_mix, rwkv_w0, rwkv_w2, rwkv_a0, rwkv_a2, rwkv_k_k, rwkv_k_a, rwkv_r_k, rwkv_ln_w, rwkv_ln_b, w_proj_fox, w_proj_rwkv, w_out, final_norm_gain, loss_target, m_norm_gain, m_w_in, m_fox_forget_bias, m_rwkv_shift_mix, m_rwkv_w0, m_rwkv_w2, m_rwkv_a0, m_rwkv_a2, m_rwkv_k_k, m_rwkv_k_a, m_rwkv_r_k, m_rwkv_ln_w, m_rwkv_ln_b, m_w_proj_fox, m_w_proj_rwkv, m_w_out, m_final_norm_gain, v_norm_gain, v_w_in, v_fox_forget_bias, v_rwkv_shift_mix, v_rwkv_w0, v_rwkv_w2, v_rwkv_a0, v_rwkv_a2, v_rwkv_k_k, v_rwkv_k_a, v_rwkv_r_k, v_rwkv_ln_w, v_rwkv_ln_b, v_w_proj_fox, v_w_proj_rwkv, v_w_out, v_final_norm_gain):
    T, D = x.shape[1], x.shape[2]
    assert D == 2048 and T % LANE == 0
    NI = w_in.shape[2]
    IN = N_DEV * NI
    RB = 4224
    x2 = x[0]
    lt2 = loss_target[0]
    me = _slot(_place())

    wa, wb = _allgather_weights(
        w_in[0].astype(BF16),
        _pack_b(w_proj_fox[0], w_proj_rwkv[0], w_out[0], rwkv_w2[0], rwkv_a2[0], RB).astype(BF16))
    wf = wa.transpose(1, 0, 2).reshape(D, IN)
    zc = lambda n: jnp.zeros((D, n), BF16)
    w_pad = jnp.concatenate([wf[:, 0:4096], wf[:, 4104:8200], wf[:, 8392:12488], wf[:, 4096:4104], zc(120), zc(128),
                             wf[:, 8200:8296], zc(32), wf[:, 8296:8392], zc(32)], axis=1)
    NP = w_pad.shape[1]
    F_CB, LORA_CB = 96, 49
    wpf = wb[:, 0:1024, :].transpose(1, 0, 2).reshape(1024, D)
    wpr = wb[:, 1024:2048, :].transpose(1, 0, 2).reshape(1024, D)
    wo = wb[:, 2048:4096, :].reshape(N_DEV * 256, D)
    w2p = _pad_rows(wb[:, 4096:4192, 0:128].transpose(1, 0, 2).reshape(96, 1024).astype(F32), LANE)
    a2p = _pad_rows(wb[:, 4096:4192, 128:256].transpose(1, 0, 2).reshape(96, 1024).astype(F32), LANE)

    mu = rwkv_shift_mix
    mu_main = mu[:, 0:4096]
    mu_lora = jnp.concatenate([_pad_cols(mu[:, 4096:4192], LANE), _pad_cols(mu[:, 4192:4288], LANE)], axis=1)
    bias_pad = _pad_cols(fox_forget_bias, LANE)
    rk_flat = rwkv_r_k.reshape(1, 1024)
    gf = final_norm_gain.reshape(1, D)

    tb = min(256, T)
    tbh = min(128, T)
    (h,) = _rowcall("rms_fwd", T, tb, [_rows(x2, tb, D), _whole(norm_gain)],
                    lambda i, xv, g: ([_rms_math(xv, g)], []), [(D, BF16)])
    u = _mm(h, w_pad, name="mm_in")

    c8 = _gates_fwd(u, bias_pad, T, F_CB)
    c3 = c8.reshape(8, 1, T)
    o_raw, o_a, lse = _attn_fwd(u, c3, T)

    def shift_body(i, um, hm, ul, hl, mm_, ml):
        outs = []
        for uv, hv, mv in ((um, hm, mm_), (ul, hl, ml)):
            hv = jnp.where(i == 0, 0.0, hv)
            prev = pltpu.roll(jnp.concatenate([hv, uv], axis=0), 1, 0)[SUB:]
            outs.append(uv + (prev - uv) * mv)
        return outs, []

    def halo_prev(arr, w, cb):
        return (arr, (SUB, w), lambda i: (jnp.maximum(i * (tbh // SUB) - 1, 0), cb))

    xs, xl = _rowcall("rwkv_shift_fwd", T, tbh,
                      [_rows(u, tbh, 4096, 1), halo_prev(u, 4096, 1), _rows(u, tbh, 256, LORA_CB), halo_prev(u, 256, LORA_CB),
                       _whole(mu_main), _whole(mu_lora)],
                      shift_body, [(4096, F32), (256, F32)])

    prep_par = [_whole(rwkv_w0), _whole(rwkv_a0), _whole(rwkv_k_k), _whole(rwkv_k_a), _whole(w2p), _whole(a2p)]
    prep_rows = [_rows(xs, tbh, 1024, 1), _rows(xl, tbh, LANE, 0), _rows(xl, tbh, LANE, 1)]
    lw, kp, an, bb = _rowcall("rwkv_prep_fwd", T, tbh, prep_rows + prep_par,
                              lambda i, *a: (list(_prep_math(*a)), []), [(1024, F32)] * 4)
    y, states = _scan_fwd(xs, lw, kp, an, bb, T)
    post_rows = [_rows(y, tbh, 1024), _rows(xs, tbh, 1024, 0), _rows(kp, tbh, 1024), _rows(xs, tbh, 1024, 2), _rows(xs, tbh, 1024, 3)]
    post_par = [_whole(rwkv_ln_w), _whole(rwkv_ln_b), _whole(rk_flat)]
    (o_b,) = _rowcall("rwkv_post_fwd", T, tbh, post_rows + post_par,
                      lambda i, *a: ([_post_math(*a)], []), [(1024, BF16)])

    pa = _mm(o_a, wpf, name="mm_proj_fox")
    pb = _mm(o_b, wpr, name="mm_proj_rwkv")
    merge_rows = [_rows(u, tb, D, 4), _rows(u, tb, D, 5), _rows(pa, tb, D), _rows(pb, tb, D)]
    (mg,) = _rowcall("merge_fwd", T, tb, merge_rows, lambda i, *a: ([_merge_math(*a)], []), [(D, BF16)])
    mo = _mm(mg, wo, name="mm_out")

    def head_body(i, xv, mov, ltv, g):
        out = xv + mov
        r = lax.rsqrt(jnp.mean(out * out, axis=-1, keepdims=True) + RMS_EPS)
        yn = out * r
        err = yn * g - ltv
        loss = 0.5 * jnp.sum(jnp.sum(err * err, axis=-1, keepdims=True), axis=0, keepdims=True) / D
        dyv = err / D
        dyn = dyv * g
        dout = r * (dyn - yn * jnp.mean(dyn * yn, axis=-1, keepdims=True))
        return [dout], [loss, jnp.sum(dyv * yn, axis=0, keepdims=True)]

    dout, loss_p, dgf_p = _rowcall("loss_head", T, tb, [_rows(x2, tb, D), _rows(mo, tb, D), _rows(lt2, tb, D), _whole(gf)],
                                   head_body, [(D, F32)], [(1, 1), (1, D)])

    dm = _mm(dout, wo, tb=True, name="mm_out_dx")
    dwo = _mm(mg.T, dout, name="mm_out_dw")

    def merge_bwd_body(i, ga, gb, pav, pbv, dmv):
        _, vjp = jax.vjp(_merge_math, ga, gb, pav, pbv)
        dga, dgb, dpa, dpb = vjp(dmv)
        return [dga, dgb, dpa, dpb], []

    dga, dgb, dpa, dpb = _rowcall("merge_bwd", T, tb, merge_rows + [_rows(dm, tb, D)], merge_bwd_body,
                                  [(D, BF16), (D, BF16), (D, BF16), (D, BF16)])
    doa = _mm(dpa, wpf, tb=True, name="mm_proj_fox_dx")
    dwpf = _mm(o_a.T, dpa, name="mm_proj_fox_dw")
    dob = _mm(dpb, wpr, tb=True, name="mm_proj_rwkv_dx")
    dwpr = _mm(o_b.T, dpb, name="mm_proj_rwkv_dw")

    do_b, dza, delta = _attn_delta(u, c3, o_raw, lse, doa, T)
    dq, dk, dv, dc3 = _attn_bwd(u, c3, lse, do_b, delta, T)
    dfl, dbias_p = _gates_bwd(dc3.reshape(8, T), u, bias_pad, T, F_CB)

    def post_bwd_body(i, yv, rv, kpv, vv, zv, lnw, lnb, rkv, dobv):
        _, vjp = jax.vjp(_post_math, yv, rv, kpv, vv, zv, lnw, lnb, rkv)
        dy_, dr_, dkp_, dv_, dz_, dlnw, dlnb, drk = vjp(dobv)
        return [dy_, dr_, dkp_, dv_, dz_], [dlnw, dlnb, drk]

    dy_s, dr_p, dkp_p, dv_p, dzb, dlnw_p, dlnb_p, drk_p = _rowcall(
        "rwkv_post_bwd", T, tbh, post_rows + post_par + [_rows(dob, tbh, 1024)], post_bwd_body,
        [(1024, F32)] * 5, [(1, 1024)] * 3)
    dr_s, dlw, dkp_s, dv_s, dan, dbb = _scan_bwd(xs, lw, kp, an, bb, states, dy_s, T)

    def prep_bwd_body(i, xk, xwd, xad, w0, a0, kkw, kaw, w2v, a2v, dlw_, dkp1, dkp2, dan_, dbb_, dr1, dr2, dv1, dv2, dz_):
        _, vjp = jax.vjp(_prep_math, xk, xwd, xad, w0, a0, kkw, kaw, w2v, a2v)
        dxk, dxwd, dxad, dw0, da0, dkk, dka, dw2, da2 = vjp((dlw_, dkp1 + dkp2, dan_, dbb_))
        return [[dr1 + dr2, dxk, dv1 + dv2, dz_], [dxwd, dxad]], [dw0, da0, dkk, dka, dw2, da2]

    cots = [dlw, dkp_s, dkp_p, dan, dbb, dr_s, dr_p, dv_s, dv_p, dzb]
    dxs, dxl, dw0_p, da0_p, dkk_p, dka_p, dw2_p, da2_p = _rowcall(
        "rwkv_prep_bwd", T, tbh, prep_rows + prep_par + [_rows(c_, tbh, 1024) for c_ in cots], prep_bwd_body,
        [(4096, F32), (256, F32)], [(1, 1024)] * 4 + [(LANE, 1024)] * 2)

    def shift_bwd_body(i, dm_, hm, dl_, hl, um, pm, ul, pl_, mm_, ml):
        last = i == T // tbh - 1
        outs, accs = [], []
        for dv_, hv, uv, pv, mv in ((dm_, hm, um, pm, mm_), (dl_, hl, ul, pl_, ml)):
            hv = jnp.where(last, 0.0, hv)
            nxt = pltpu.roll(jnp.concatenate([dv_, hv], axis=0), tbh + SUB - 1, 0)[:tbh]
            pv = jnp.where(i == 0, 0.0, pv)
            prev = pltpu.roll(jnp.concatenate([pv, uv], axis=0), 1, 0)[SUB:]
            outs.append(dv_ * (1.0 - mv) + nxt * mv)
            accs.append(jnp.sum(dv_ * (prev - uv), axis=0, keepdims=True))
        return outs, accs

    def halo_next(arr, w, cb):
        last_blk = T // SUB - 1
        return (arr, (SUB, w), lambda i: (jnp.minimum((i + 1) * (tbh // SUB), last_blk), cb))

    du_b, du_l, dmu_main_p, dmu_lora_p = _rowcall(
        "rwkv_shift_bwd", T, tbh,
        [_rows(dxs, tbh, 4096), halo_next(dxs, 4096, 0), _rows(dxl, tbh, 256), halo_next(dxl, 256, 0),
         _rows(u, tbh, 4096, 1), halo_prev(u, 4096, 1), _rows(u, tbh, 256, LORA_CB), halo_prev(u, 256, LORA_CB),
         _whole(mu_main), _whole(mu_lora)],
        shift_bwd_body, [(4096, BF16), (256, BF16)], [(1, 4096), (1, 256)])

    du = jnp.concatenate([dq.astype(BF16), dk.astype(BF16), dv.astype(BF16), dza.astype(BF16), du_b, dga, dgb,
                          dfl.astype(BF16), jnp.zeros((T, LANE), BF16), du_l], axis=1)
    dh = _mm(du, w_pad, tb=True, tk=NP // 10, name="mm_in_dx")
    dw_pad = _mm(h.T, du, name="mm_in_dw")

    def rms_bwd_body(i, xv, g, dhv, doutv):
        _, vjp = jax.vjp(_rms_math, xv, g)
        dx_, dg_ = vjp(dhv)
        return [dx_ + doutv], [dg_]

    grad_x2, dng_p = _rowcall("rms_bwd", T, tb, [_rows(x2, tb, D), _whole(norm_gain), _rows(dh, tb, D), _rows(dout, tb, D)],
                              rms_bwd_body, [(D, F32)], [(1, D)])

    dw_full = jnp.concatenate([dw_pad[:, 0:4096], dw_pad[:, 12288:12296], dw_pad[:, 4096:8192], dw_pad[:, 12544:12640],
                               dw_pad[:, 12672:12768], dw_pad[:, 8192:12288]], axis=1)
    ga = dw_full.reshape(D, N_DEV, NI).transpose(1, 0, 2)
    ga_own = lax.dynamic_index_in_dim(ga, me, 0, keepdims=False)
    lora_g = jnp.concatenate([dw2_p[:96].reshape(96, N_DEV, 128).transpose(1, 0, 2),
                              da2_p[:96].reshape(96, N_DEV, 128).transpose(1, 0, 2)], axis=2)
    gb = jnp.concatenate([dwpf.reshape(1024, N_DEV, 256).transpose(1, 0, 2),
                          dwpr.reshape(1024, N_DEV, 256).transpose(1, 0, 2),
                          dwo.reshape(N_DEV, 2048, 256), lora_g, jnp.zeros((N_DEV, RB - 4192, 256), F32)], axis=1)
    gb_own = lax.dynamic_index_in_dim(gb, me, 0, keepdims=False)

    dmu = jnp.concatenate([dmu_main_p, dmu_lora_p[:, 0:96], dmu_lora_p[:, 128:224]], axis=1)
    small_parts = [dng_p, dbias_p[:, 0:8], dmu, dw0_p, da0_p, dkk_p, dka_p, drk_p, dlnw_p, dlnb_p, dgf_p, loss_p]
    SR = 128
    small = _pad_cols(jnp.concatenate(small_parts, axis=1), SR * LANE).reshape(SR, LANE)

    ra, rb, rs = _exchange_grads(ga.astype(BF16), gb.astype(BF16), small)

    g_in, d_in, m_in, v_in = _adamw("adamw_w_in", w_in[0], m_w_in[0], v_w_in[0], ga_own, ra, min(128, D))
    pk = lambda pf, pr, wo_, w2_, a2_: _pack_b(pf[0], pr[0], wo_[0], w2_[0], a2_[0], RB)
    outs_b = _adamw("adamw_packed", pk(w_proj_fox, w_proj_rwkv, w_out, rwkv_w2, rwkv_a2),
                    pk(m_w_proj_fox, m_w_proj_rwkv, m_w_out, m_rwkv_w2, m_rwkv_a2),
                    pk(v_w_proj_fox, v_w_proj_rwkv, v_w_out, v_rwkv_w2, v_rwkv_a2), gb_own, rb, 384)

    def pack_small(ng, fb, sm, w0, a0, kk_, ka_, rk_, lnw, lnb, fg):
        parts = [ng, fb, sm, w0, a0, kk_, ka_, rk_.reshape(1, 1024), lnw, lnb, fg.reshape(1, D), jnp.zeros((1, 1), F32)]
        return _pad_cols(jnp.concatenate(parts, axis=1), SR * LANE).reshape(SR, LANE)

    outs_s = _adamw("adamw_small",
                    pack_small(norm_gain, fox_forget_bias, rwkv_shift_mix, rwkv_w0, rwkv_a0, rwkv_k_k, rwkv_k_a, rwkv_r_k,
                               rwkv_ln_w, rwkv_ln_b, final_norm_gain),
                    pack_small(m_norm_gain, m_fox_forget_bias, m_rwkv_shift_mix, m_rwkv_w0, m_rwkv_a0, m_rwkv_k_k, m_rwkv_k_a,
                               m_rwkv_r_k, m_rwkv_ln_w, m_rwkv_ln_b, m_final_norm_gain),
                    pack_small(v_norm_gain, v_fox_forget_bias, v_rwkv_shift_mix, v_rwkv_w0, v_rwkv_a0, v_rwkv_k_k, v_rwkv_k_a,
                               v_rwkv_r_k, v_rwkv_ln_w, v_rwkv_ln_b, v_final_norm_gain),
                    None, rs, SR)

    def unpack_b(pkd):
        return dict(w_proj_fox=pkd[0:1024][None], w_proj_rwkv=pkd[1024:2048][None], w_out=pkd[2048:4096].reshape(1, 256, D),
                    rwkv_w2=pkd[4096:4192, 0:128][None], rwkv_a2=pkd[4096:4192, 128:256][None])

    def unpack_s(pkd):
        flat = pkd.reshape(1, SR * LANE)
        names = [("norm_gain", D), ("fox_forget_bias", 8), ("rwkv_shift_mix", 4288), ("rwkv_w0", 1024), ("rwkv_a0", 1024),
                 ("rwkv_k_k", 1024), ("rwkv_k_a", 1024), ("rwkv_r_k", 1024), ("rwkv_ln_w", 1024), ("rwkv_ln_b", 1024),
                 ("final_norm_gain", D), ("loss", 1)]
        out, off = {}, 0
        for nm, n in names:
            out[nm] = flat[:, off:off + n]
            off += n
        out["rwkv_r_k"] = out["rwkv_r_k"].reshape(1, 16, 64)
        out["final_norm_gain"] = out["final_norm_gain"].reshape(D)
        return out

    order = ["norm_gain", "w_in", "fox_forget_bias", "rwkv_shift_mix", "rwkv_w0", "rwkv_w2", "rwkv_a0", "rwkv_a2", "rwkv_k_k",
             "rwkv_k_a", "rwkv_r_k", "rwkv_ln_w", "rwkv_ln_b", "w_proj_fox", "w_proj_rwkv", "w_out", "final_norm_gain"]
    result = []
    loss = None
    for kind, big in enumerate((g_in, d_in, m_in, v_in)):
        d = {**unpack_b(outs_b[kind]), **unpack_s(outs_s[kind]), "w_in": big[None]}
        if kind == 0:
            loss = d["loss"].reshape(())
        result += [d[n] for n in order]
    return (loss, grad_x2[None], *result)
```

```python
import functools

import jax
import jax.numpy as jnp
from jax import lax
from jax.experimental import pallas as pl
from jax.experimental.pallas import tpu as pltpu

F32 = jnp.float32
BF16 = jnp.bfloat16
HI = lax.Precision.HIGHEST
MESH = pl.DeviceIdType.MESH

FOX_HD = 128
RW_HD = 64
RMS_EPS = 1e-6
GN_EPS = 64e-5
L2_EPS = 1e-12
ADAM_LR = 0.001
ADAM_B1 = 0.9
ADAM_B2 = 0.999
ADAM_EPS = 1e-08
ADAM_WD = 0.01
ADAM_STEP = 10

LANE = 128
SUB = 8
VMEM_LIMIT = 56 * 1024 * 1024
N_DEV = 8
CHUNK = 64
SCAN_GROUP = 4
PS = None
NEG = -1e30


def _scan_shape(T):
    c = min(CHUNK, T)
    return c, min(SCAN_GROUP, T // c)


def _cp(sem=None):
    return pltpu.CompilerParams(dimension_semantics=sem, vmem_limit_bytes=VMEM_LIMIT)


def _sigmoid(x):
    return jax.nn.sigmoid(x)


def _softplus(x):
    return jnp.maximum(x, 0.0) + jnp.log(1.0 + jnp.exp(-jnp.abs(x)))


def _nn(a, b, prec=None):
    return lax.dot_general(a, b, (((1,), (0,)), ((), ())), precision=prec, preferred_element_type=F32)


def _nt(a, b, prec=None):
    return lax.dot_general(a, b, (((1,), (1,)), ((), ())), precision=prec, preferred_element_type=F32)


def _tn(a, b, prec=None):
    return lax.dot_general(a, b, (((0,), (0,)), ((), ())), precision=prec, preferred_element_type=F32)


def _iota2(shape, dim):
    return lax.broadcasted_iota(jnp.int32, shape, dim)


def _seg_sum(x):
    r = _iota2((LANE, LANE), 0) // RW_HD
    c = _iota2((LANE, LANE), 1) // RW_HD
    bd = (r == c).astype(F32)
    parts = [_nn(x[:, j * LANE:(j + 1) * LANE], bd, HI) for j in range(x.shape[1] // LANE)]
    return parts[0] if len(parts) == 1 else jnp.concatenate(parts, axis=1)


def _mm(a, b, *, tb=False, out_dtype=F32, tm=512, tn=512, tk=None, name):
    M, K = a.shape
    N = b.shape[0] if tb else b.shape[1]
    tm, tn = min(tm, M), min(tn, N)
    tk = K if tk is None else tk
    nk = K // tk
    assert M % tm == 0 and N % tn == 0 and K % tk == 0
    b_spec = pl.BlockSpec((tn, tk), lambda i, j, k: (j, k)) if tb else pl.BlockSpec((tk, tn), lambda i, j, k: (k, j))

    def body(a_ref, b_ref, o_ref, *scr):
        av = a_ref[...].astype(BF16)
        bv = b_ref[...].astype(BF16)
        p = _nt(av, bv) if tb else _nn(av, bv)
        if nk == 1:
            o_ref[...] = p.astype(out_dtype)
        else:
            acc_ref = scr[0]
            k = pl.program_id(2)

            @pl.when(k == 0)
            def _():
                acc_ref[...] = p

            @pl.when(k > 0)
            def _():
                acc_ref[...] += p

            @pl.when(k == nk - 1)
            def _():
                o_ref[...] = acc_ref[...].astype(out_dtype)

    return pl.pallas_call(
        body, name=name,
        out_shape=jax.ShapeDtypeStruct((M, N), out_dtype),
        grid=(M // tm, N // tn, nk),
        in_specs=[pl.BlockSpec((tm, tk), lambda i, j, k: (i, k)), b_spec],
        out_specs=pl.BlockSpec((tm, tn), lambda i, j, k: (i, j)),
        scratch_shapes=[] if nk == 1 else [pltpu.VMEM((tm, tn), F32)],
        compiler_params=_cp(("parallel", "parallel", "arbitrary")),
    )(a, b)


def _rows(arr, tb, w, cb=0):
    return (arr, (tb, w), lambda i: (i, cb))


def _whole(arr):
    nd = arr.ndim
    return (arr, arr.shape, lambda i: (0,) * nd)


def _rowcall(name, T, tb, ins, body, outs, accs=()):
    n_in, n_out = len(ins), len(outs)

    def kern(*refs):
        i = pl.program_id(0)
        vals = [r[...] for r in refs[:n_in]]
        ro, ao = body(i, *vals)
        for r, v in zip(refs[n_in:n_in + n_out], ro):
            if isinstance(v, (list, tuple)):
                off = 0
                for piece in v:
                    w = piece.shape[1]
                    r[:, off:off + w] = piece.astype(r.dtype)
                    off += w
            else:
                r[...] = v.astype(r.dtype)
        if accs:
            acc_refs = refs[n_in + n_out:]

            @pl.when(i == 0)
            def _():
                for r in acc_refs:
                    r[...] = jnp.zeros(r.shape, F32)

            for r, v in zip(acc_refs, ao):
                r[...] += v

    out_shape = [jax.ShapeDtypeStruct((T, w), dt) for (w, dt) in outs] + [jax.ShapeDtypeStruct(s, F32) for s in accs]
    out_specs = [pl.BlockSpec((tb, w), lambda i: (i, 0)) for (w, dt) in outs] + [pl.BlockSpec(s, lambda i: (0, 0)) for s in accs]
    res = pl.pallas_call(
        kern, name=name,
        out_shape=out_shape,
        grid=(T // tb,),
        in_specs=[pl.BlockSpec(bs, im) for (_, bs, im) in ins],
        out_specs=out_specs,
        compiler_params=_cp(("arbitrary",)),
    )(*[a for (a, _, _) in ins])
    return res


def _rms_math(x, g):
    r = lax.rsqrt(jnp.mean(x * x, axis=-1, keepdims=True) + RMS_EPS)
    return x * r * g


def _merge_math(ga, gb, pa, pb):
    return _sigmoid(ga) * pa + _sigmoid(gb) * pb


def _prep_math(xk, xwd, xad, w0, a0, kk_w, ka_w, w2p, a2p):
    z = w0 + _nn(jnp.tanh(xwd), w2p, HI)
    w = -_softplus(-z) - 0.5
    lw = -jnp.exp(w)
    ag = _sigmoid(a0 + _nn(xad, a2p, HI))
    p = xk * kk_w
    n = jnp.maximum(jnp.sqrt(_seg_sum(p * p)), L2_EPS)
    kk = p / n
    kp = xk * (1.0 + (ag - 1.0) * ka_w)
    return lw, kp, -kk, kk * ag


def _post_math(y, r, kp, v, z, lnw, lnb, rk):
    inv = 1.0 / RW_HD
    mu = _seg_sum(y) * inv
    d = y - mu
    var = _seg_sum(d * d) * inv
    yn = d * lax.rsqrt(var + GN_EPS) * lnw + lnb
    bonus = _seg_sum(r * kp * rk) * v
    return (yn + bonus) * (z * _sigmoid(z))


def _scan_group(s0, *flat):
    G = len(flat) // 6
    ch = [flat[6 * i:6 * i + 6] for i in range(G)]
    C = ch[0][0].shape[0]
    C2 = 2 * C
    cat = jnp.concatenate
    m0 = _iota2((1, LANE), 1) < RW_HD
    mask0 = m0.astype(F32)
    mask1 = 1.0 - mask0
    r2 = _iota2((C2, C2), 0)
    c2 = _iota2((C2, C2), 1)
    dist = r2 - c2
    in_head = dist <= r2 % C
    eye = (r2 == c2).astype(F32)
    lower = (_iota2((C, C), 0) >= _iota2((C, C), 1)).astype(F32)
    bd = (_iota2((LANE, LANE), 0) // RW_HD) == (_iota2((LANE, LANE), 1) // RW_HD)

    def tri(m, strict):
        return jnp.where(dist > 0 if strict else dist >= 0, jnp.where(in_head, m, 0.0), 0.0)

    def sel(z):
        return jnp.where(m0, z[:C], z[C:])

    gs = [_nn(lower, c[1], HI) for c in ch]
    pre = []
    for (r, lw, k, v, a, b), g in zip(ch, gs):
        g_end = jnp.sum(lw, axis=0, keepdims=True)
        en = jnp.exp(-g)
        ec = jnp.exp(g_end - g)
        pre.append(dict(at=a * jnp.exp(g - lw), rt=r * jnp.exp(g), bt=b * en, kt=k * en, bh=b * ec, kh=k * ec,
                        dec=jnp.exp(g_end), v=v))
    grams = [_nt(cat([p["at"] * mask0, p["at"] * mask1, p["rt"] * mask0, p["rt"] * mask1], axis=0),
                 cat([p["bt"], p["bt"], p["kt"], p["kt"]], axis=0), PS) for p in pre]
    mab = [tri(gm[:C2, :C2], True) for gm in grams]
    mp = [_nn(m, m, PS) for m in mab]
    tinv = [eye + m for m in mab]
    n = 2
    while n < C:
        last = 2 * n >= C
        for i in range(G):
            if last:
                tinv[i] = tinv[i] + _nn(mp[i], tinv[i], PS)
            else:
                z = _nn(mp[i], cat([mp[i], tinv[i]], axis=1), PS)
                mp[i], tinv[i] = z[:, :C2], tinv[i] + z[:, C2:]
        n *= 2
    xv = [sel(_nn(tri(gm[:C2, C2:], True), cat([p["v"], p["v"]], axis=0), PS)) for gm, p in zip(grams, pre)]
    ys, s = [], s0
    for i in range(G):
        p, gm = pre[i], grams[i]
        sx = _nt(cat([p["at"], p["rt"]], axis=0), s, PS)
        x = sx[:C] + xv[i]
        u = sel(_nn(tinv[i], cat([x, x], axis=0), PS))
        v = p["v"]
        ys.append(sx[C:] + sel(_nn(cat([tri(gm[C2:, :C2], False), tri(gm[C2:, C2:], False)], axis=1), cat([u, u, v, v], axis=0), PS)))
        s = s * p["dec"] + jnp.where(bd, _tn(cat([u, v], axis=0), cat([p["bh"], p["kh"]], axis=0), PS), 0.0)
    return tuple(ys), s


def _scan_fwd(xs, lw, kp, an, bb, T):
    C, G = _scan_shape(T)
    nc = T // (C * G)
    npair = 1024 // LANE

    def kern(r_ref, lw_ref, k_ref, v_ref, a_ref, b_ref, y_ref, st_ref, s_scr):
        n = pl.program_id(1)

        @pl.when(n == 0)
        def _():
            s_scr[...] = jnp.zeros(s_scr.shape, F32)

        s0 = s_scr[...]
        st_ref[0, 0] = s0
        ins = (r_ref, lw_ref, k_ref, v_ref, a_ref, b_ref)
        ys, s1 = _scan_group(s0, *[ref[i * C:(i + 1) * C, :] for i in range(G) for ref in ins])
        for i in range(G):
            y_ref[i * C:(i + 1) * C, :] = ys[i]
        s_scr[...] = s1

    def col(off):
        return pl.BlockSpec((C * G, LANE), lambda p, n: (n, off + p))

    return pl.pallas_call(
        kern, name="rwkv_scan_fwd",
        out_shape=[jax.ShapeDtypeStruct((T, 1024), F32), jax.ShapeDtypeStruct((nc, npair, LANE, LANE), F32)],
        grid=(npair, nc),
        in_specs=[col(0), col(0), col(0), col(16), col(0), col(0)],
        out_specs=[col(0), pl.BlockSpec((1, 1, LANE, LANE), lambda p, n: (n, p, 0, 0))],
        scratch_shapes=[pltpu.VMEM((LANE, LANE), F32)],
        compiler_params=_cp(("parallel", "arbitrary")),
    )(xs, lw, kp, xs, an, bb)


def _scan_bwd(xs, lw, kp, an, bb, states, dy, T):
    C, G = _scan_shape(T)
    nc = T // (C * G)
    npair = 1024 // LANE

    def kern(r_ref, lw_ref, k_ref, v_ref, a_ref, b_ref, st_ref, dy_ref, dr_ref, dlw_ref, dk_ref, dv_ref, da_ref, db_ref, ds_scr):
        n = pl.program_id(1)

        @pl.when(n == 0)
        def _():
            ds_scr[...] = jnp.zeros(ds_scr.shape, F32)

        ins = (r_ref, lw_ref, k_ref, v_ref, a_ref, b_ref)
        _, vjp = jax.vjp(_scan_group, st_ref[0, 0], *[ref[i * C:(i + 1) * C, :] for i in range(G) for ref in ins])
        grads = vjp((tuple(dy_ref[i * C:(i + 1) * C, :] for i in range(G)), ds_scr[...]))
        ds_scr[...] = grads[0]
        outs = (dr_ref, dlw_ref, dk_ref, dv_ref, da_ref, db_ref)
        for i in range(G):
            for t, ref in enumerate(outs):
                ref[i * C:(i + 1) * C, :] = grads[1 + 6 * i + t]

    def col(off):
        return pl.BlockSpec((C * G, LANE), lambda p, n: (nc - 1 - n, off + p))

    return pl.pallas_call(
        kern, name="rwkv_scan_bwd",
        out_shape=[jax.ShapeDtypeStruct((T, 1024), F32)] * 6,
        grid=(npair, nc),
        in_specs=[col(0), col(0), col(0), col(16), col(0), col(0),
                  pl.BlockSpec((1, 1, LANE, LANE), lambda p, n: (nc - 1 - n, p, 0, 0)), col(0)],
        out_specs=[col(0)] * 6,
        scratch_shapes=[pltpu.VMEM((LANE, LANE), F32)],
        compiler_params=_cp(("parallel", "arbitrary")),
    )(xs, lw, kp, xs, an, bb, states, dy)


def _gates_fwd(u, bias_pad, T, f_cb):
    nb = T // LANE

    def kern(f_ref, b_ref, c_ref):
        x = f_ref[...] + b_ref[...]
        lf = jnp.minimum(x, 0.0) - jnp.log(1.0 + jnp.exp(-jnp.abs(x)))
        lft = lf.T
        ut = (_iota2((LANE, LANE), 0) <= _iota2((LANE, LANE), 1)).astype(F32)
        carry = jnp.zeros((LANE, 1), F32)
        for blk in range(nb):
            seg = lft[:, blk * LANE:(blk + 1) * LANE]
            cs = _nn(seg, ut, HI) + carry
            c_ref[:, blk * LANE:(blk + 1) * LANE] = cs[:SUB, :]
            carry = carry + jnp.sum(seg, axis=1, keepdims=True)

    return pl.pallas_call(
        kern, name="fox_gates_fwd",
        out_shape=jax.ShapeDtypeStruct((SUB, T), F32),
        grid=(1,),
        in_specs=[pl.BlockSpec((T, LANE), lambda i: (0, f_cb)), pl.BlockSpec((1, LANE), lambda i: (0, 0))],
        out_specs=pl.BlockSpec((SUB, T), lambda i: (0, 0)),
        compiler_params=_cp(("arbitrary",)),
    )(u, bias_pad)


def _gates_bwd(dc, u, bias_pad, T, f_cb):
    nb = T // LANE

    def kern(dc_ref, f_ref, b_ref, dfl_ref, db_ref):
        dcv = jnp.concatenate([dc_ref[...], jnp.zeros((LANE - SUB, T), F32)], axis=0)
        lt = (_iota2((LANE, LANE), 0) >= _iota2((LANE, LANE), 1)).astype(F32)
        carry = jnp.zeros((LANE, 1), F32)
        pieces = [None] * nb
        for blk in range(nb - 1, -1, -1):
            seg = dcv[:, blk * LANE:(blk + 1) * LANE]
            pieces[blk] = _nn(seg, lt, HI) + carry
            carry = carry + jnp.sum(seg, axis=1, keepdims=True)
        dlf = (pieces[0] if nb == 1 else jnp.concatenate(pieces, axis=1)).T
        x = f_ref[...] + b_ref[...]
        dfl = dlf * _sigmoid(-x)
        dfl_ref[...] = dfl
        db_ref[...] = jnp.sum(dfl, axis=0, keepdims=True)

    return pl.pallas_call(
        kern, name="fox_gates_bwd",
        out_shape=[jax.ShapeDtypeStruct((T, LANE), F32), jax.ShapeDtypeStruct((1, LANE), F32)],
        grid=(1,),
        in_specs=[pl.BlockSpec((SUB, T), lambda i: (0, 0)), pl.BlockSpec((T, LANE), lambda i: (0, f_cb)),
                  pl.BlockSpec((1, LANE), lambda i: (0, 0))],
        out_specs=[pl.BlockSpec((T, LANE), lambda i: (0, 0)), pl.BlockSpec((1, LANE), lambda i: (0, 0))],
        compiler_params=_cp(("arbitrary",)),
    )(dc, u, bias_pad)


def _attn_block(T):
    return 256 if T % 256 == 0 and T >= 512 else 128


def _attn_fwd(u, c3, T):
    H = 8
    bq = _attn_block(T)
    nq = T // bq
    scale = FOX_HD ** -0.5

    def kern(q_ref, k_ref, v_ref, z_ref, cq_ref, ck_ref, o_ref, oa_ref, lse_ref):
        i = pl.program_id(1)
        q = (q_ref[...] * scale).astype(BF16)
        c0 = cq_ref[0][:, 0:1]
        row = _iota2((bq, bq), 0) + i * bq

        def step(j, carry):
            m, l, acc = carry
            off = pl.multiple_of(j * bq, bq)
            kj = k_ref[pl.ds(off, bq), :].astype(BF16)
            vj = v_ref[pl.ds(off, bq), :].astype(BF16)
            s = _nt(q, kj) + (c0 - ck_ref[0, :, pl.ds(off, bq)])
            s = jnp.where(_iota2((bq, bq), 1) + j * bq <= row, s, NEG)
            m_new = jnp.maximum(m, jnp.max(s, axis=1, keepdims=True))
            p = jnp.exp(s - m_new)
            alpha = jnp.exp(m - m_new)
            l = alpha * l + jnp.sum(p, axis=1, keepdims=True)
            acc = alpha * acc + _nn(p.astype(BF16), vj)
            return m_new, l, acc

        init = (jnp.full((bq, 1), NEG, F32), jnp.zeros((bq, 1), F32), jnp.zeros((bq, FOX_HD), F32))
        m, l, acc = lax.fori_loop(0, i + 1, step, init)
        o = acc / l
        z = z_ref[...]
        o_ref[...] = o
        oa_ref[...] = (o * z * _sigmoid(z)).astype(BF16)
        lse_ref[0] = m + jnp.log(l)

    return pl.pallas_call(
        kern, name="fox_attn_fwd",
        out_shape=[jax.ShapeDtypeStruct((T, 1024), F32), jax.ShapeDtypeStruct((T, 1024), BF16),
                   jax.ShapeDtypeStruct((H, T, 1), F32)],
        grid=(H, nq),
        in_specs=[pl.BlockSpec((bq, LANE), lambda h, i: (i, h)),
                  pl.BlockSpec((T, LANE), lambda h, i: (0, 8 + h)),
                  pl.BlockSpec((T, LANE), lambda h, i: (0, 16 + h)),
                  pl.BlockSpec((bq, LANE), lambda h, i: (i, 24 + h)),
                  pl.BlockSpec((1, 1, bq), lambda h, i: (h, 0, i)),
                  pl.BlockSpec((1, 1, T), lambda h, i: (h, 0, 0))],
        out_specs=[pl.BlockSpec((bq, LANE), lambda h, i: (i, h)),
                   pl.BlockSpec((bq, LANE), lambda h, i: (i, h)),
                   pl.BlockSpec((1, bq, 1), lambda h, i: (h, i, 0))],
        compiler_params=_cp(("parallel", "arbitrary")),
    )(u, u, u, u, c3, c3)


def _attn_tile(qs, kj, vj, dob, cdiff, lse_i, keep):
    s = _nt(qs, kj) + cdiff
    p = jnp.where(keep, jnp.exp(jnp.where(keep, s, NEG) - lse_i), 0.0)
    return p, _nt(dob, vj)


def _attn_delta(u, c3, o, lse, doa, T):
    H = 8
    bq = _attn_block(T)
    nq = T // bq
    scale = FOX_HD ** -0.5

    def kern(q_ref, k_ref, v_ref, z_ref, c_ref, o_ref, lse_ref, doa_ref, do_ref, dz_ref, dl_ref):
        i = pl.program_id(1)
        z = z_ref[...]
        sg = _sigmoid(z)
        dov = doa_ref[...]
        dob = (dov * z * sg).astype(BF16)
        do_ref[...] = dob
        dz_ref[...] = dov * o_ref[...] * (sg * (1.0 + z * (1.0 - sg)))
        qs = (q_ref[...] * scale).astype(BF16)
        ioff = pl.multiple_of(i * bq, bq)
        c0 = c_ref[0, :, pl.ds(ioff, bq)][:, 0:1]
        lse_i = lse_ref[0]
        row = _iota2((bq, bq), 0) + i * bq

        def step(j, acc):
            off = pl.multiple_of(j * bq, bq)
            kj = k_ref[pl.ds(off, bq), :].astype(BF16)
            vj = v_ref[pl.ds(off, bq), :].astype(BF16)
            keep = _iota2((bq, bq), 1) + j * bq <= row
            p, dp = _attn_tile(qs, kj, vj, dob, c0 - c_ref[0, :, pl.ds(off, bq)], lse_i, keep)
            return acc + jnp.sum(p * dp, axis=1, keepdims=True)

        dl_ref[0] = lax.fori_loop(0, i + 1, step, jnp.zeros((bq, 1), F32))

    full = lambda cb: pl.BlockSpec((T, LANE), lambda h, i: (0, cb + h))
    blk = lambda cb: pl.BlockSpec((bq, LANE), lambda h, i: (i, cb + h))
    return pl.pallas_call(
        kern, name="fox_attn_delta",
        out_shape=[jax.ShapeDtypeStruct((T, 1024), BF16), jax.ShapeDtypeStruct((T, 1024), F32), jax.ShapeDtypeStruct((H, T, 1), F32)],
        grid=(H, nq),
        in_specs=[blk(0), full(8), full(16), blk(24),
                  pl.BlockSpec((1, 1, T), lambda h, i: (h, 0, 0)),
                  blk(0),
                  pl.BlockSpec((1, bq, 1), lambda h, i: (h, i, 0)),
                  blk(0)],
        out_specs=[blk(0), blk(0), pl.BlockSpec((1, bq, 1), lambda h, i: (h, i, 0))],
        compiler_params=_cp(("parallel", "arbitrary")),
    )(u, u, u, u, c3, o, lse, doa)


def _attn_bwd(u, c3, lse, do, delta, T):
    H = 8
    bq = _attn_block(T)
    nq = T // bq
    scale = FOX_HD ** -0.5

    def kern(q_ref, k_ref, v_ref, c_ref, lse_ref, do_ref, dl_ref, dq_ref, dk_ref, dv_ref, dc_ref):
        j = pl.program_id(1)

        @pl.when(j == 0)
        def _():
            dq_ref[...] = jnp.zeros(dq_ref.shape, F32)

        kj = k_ref[...].astype(BF16)
        vj = v_ref[...].astype(BF16)
        joff = pl.multiple_of(j * bq, bq)
        ck = c_ref[0, :, pl.ds(joff, bq)]
        col = _iota2((bq, bq), 1) + j * bq

        def step(i, carry):
            dk, dv, dc = carry
            off = pl.multiple_of(i * bq, bq)
            qs = (q_ref[pl.ds(off, bq), :] * scale).astype(BF16)
            dob = do_ref[pl.ds(off, bq), :]
            c0 = c_ref[0, :, pl.ds(off, bq)][:, 0:1]
            keep = col <= _iota2((bq, bq), 0) + i * bq
            p, dp = _attn_tile(qs, kj, vj, dob, c0 - ck, lse_ref[0, pl.ds(off, bq), :], keep)
            ds = p * (dp - dl_ref[0, pl.ds(off, bq), :])
            dsb = ds.astype(BF16)
            dv = dv + _tn(p.astype(BF16), dob)
            dk = dk + _tn(dsb, qs)
            dq_ref[pl.ds(off, bq), :] += _nn(dsb, kj) * scale
            dc = dc - jnp.sum(ds, axis=0, keepdims=True)
            return dk, dv, dc

        init = (jnp.zeros((bq, FOX_HD), F32), jnp.zeros((bq, FOX_HD), F32), jnp.zeros((1, bq), F32))
        dk, dv, dc = lax.fori_loop(j, nq, step, init)
        dk_ref[...] = dk
        dv_ref[...] = dv
        dc_ref[0] = dc

    full = lambda cb: pl.BlockSpec((T, LANE), lambda h, j: (0, cb + h))
    blk = lambda cb: pl.BlockSpec((bq, LANE), lambda h, j: (j, cb + h))
    col1 = pl.BlockSpec((1, T, 1), lambda h, j: (h, 0, 0))
    return pl.pallas_call(
        kern, name="fox_attn_bwd",
        out_shape=[jax.ShapeDtypeStruct((T, 1024), F32)] * 3 + [jax.ShapeDtypeStruct((H, 1, T), F32)],
        grid=(H, nq),
        in_specs=[full(0), blk(8), blk(16), pl.BlockSpec((1, 1, T), lambda h, j: (h, 0, 0)), col1, full(0), col1],
        out_specs=[full(0), blk(0), blk(0), pl.BlockSpec((1, 1, bq), lambda h, j: (h, 0, j))],
        compiler_params=_cp(("parallel", "arbitrary")),
    )(u, u, u, c3, lse, do, delta)


def _place():
    return lax.axis_index("x"), lax.axis_index("y"), lax.axis_index("c")


def _slot(p):
    return 4 * p[0] + 2 * p[1] + p[2]


def _allgather_weights(a, b):
    def body(a_ref, b_ref, ao_ref, bo_ref, send_sems, recv_sems, local_sems):
        x, y, c = _place()
        me, sibling = (x, y, c), (x, y, 1 - c)
        chips = [(1 - x, y), (x, 1 - y), (1 - x, 1 - y)]
        srcs, dsts = (a_ref, b_ref), (ao_ref, bo_ref)

        def copy(t, k, block, to, from_input=False):
            dst = dsts[t].at[_slot(block)]
            return pltpu.make_async_remote_copy(
                src_ref=srcs[t] if from_input else dst, dst_ref=dst,
                send_sem=send_sems.at[t, k], recv_sem=recv_sems.at[t, k],
                device_id=to, device_id_type=MESH)

        mine = [pltpu.make_async_copy(srcs[t], dsts[t].at[_slot(me)], local_sems.at[t]) for t in range(2)]
        for cp in mine:
            cp.start()
        first = []
        for t in range(2):
            first.append(copy(t, 0, me, sibling, True))
            first += [copy(t, 1 + j, me, (*chip, c), True) for j, chip in enumerate(chips)]
        for cp in first:
            cp.start()
        passed = []
        for j, chip in enumerate(chips):
            for t in range(2):
                copy(t, 1 + j, (*chip, c), me).wait_recv()
                fwd = copy(t, 4 + j, (*chip, c), sibling)
                fwd.start()
                passed.append(fwd)
        for t in range(2):
            copy(t, 0, sibling, me).wait_recv()
            for j, chip in enumerate(chips):
                copy(t, 4 + j, (*chip, 1 - c), me).wait_recv()
        for cp in first + passed:
            cp.wait_send()
        for cp in mine:
            cp.wait()

    any_spec = pl.BlockSpec(memory_space=pl.ANY)
    return pl.pallas_call(
        body, name="allgather_weights",
        out_shape=[jax.ShapeDtypeStruct((N_DEV,) + a.shape, a.dtype), jax.ShapeDtypeStruct((N_DEV,) + b.shape, b.dtype)],
        in_specs=[any_spec, any_spec],
        out_specs=[any_spec, any_spec],
        scratch_shapes=[pltpu.SemaphoreType.DMA((2, 7)), pltpu.SemaphoreType.DMA((2, 7)), pltpu.SemaphoreType.DMA((2,))],
    )(a, b)


def _other_chips(x, y):
    return [(1 - x, y), (x, 1 - y), (1 - x, 1 - y)]


def _exchange_pair(ga, gb):
    def body(ga_ref, gb_ref, ra_ref, rb_ref, send_sems, recv_sems):
        x, y, c = _place()
        sibling = (x, y, 1 - c)
        slots = [_slot(sibling)] + [_slot((*chip, 1 - c)) for chip in _other_chips(x, y)]
        cps = []
        for t, (src, dst) in enumerate(((ga_ref, ra_ref), (gb_ref, rb_ref))):
            for k, ps in enumerate(slots):
                cps.append(pltpu.make_async_remote_copy(
                    src_ref=src.at[ps], dst_ref=dst.at[k], send_sem=send_sems.at[t, k], recv_sem=recv_sems.at[t, k],
                    device_id=sibling, device_id_type=MESH))
        for cp in cps:
            cp.start()
        for cp in cps:
            cp.wait()

    any_spec = pl.BlockSpec(memory_space=pl.ANY)
    return pl.pallas_call(
        body, name="exchange_pair",
        out_shape=[jax.ShapeDtypeStruct((4,) + ga.shape[1:], ga.dtype), jax.ShapeDtypeStruct((4,) + gb.shape[1:], gb.dtype)],
        in_specs=[any_spec] * 2,
        out_specs=[any_spec] * 2,
        scratch_shapes=[pltpu.SemaphoreType.DMA((2, 4)), pltpu.SemaphoreType.DMA((2, 4))],
    )(ga, gb)


def _pair_add(name, g, r1, slots, tb):
    _, R, Cc = g.shape
    assert R % tb == 0

    def kern(s_ref, a_ref, b_ref, o_ref):
        o_ref[...] = (a_ref[...].astype(F32) + b_ref[...].astype(F32)).astype(o_ref.dtype)

    return pl.pallas_call(
        kern, name=name,
        out_shape=jax.ShapeDtypeStruct((3, R, Cc), BF16),
        grid_spec=pltpu.PrefetchScalarGridSpec(
            num_scalar_prefetch=1, grid=(3, R // tb),
            in_specs=[pl.BlockSpec((1, tb, Cc), lambda j, i, s: (s[j], i, 0)),
                      pl.BlockSpec((1, tb, Cc), lambda j, i, s: (1 + j, i, 0))],
            out_specs=pl.BlockSpec((1, tb, Cc), lambda j, i, s: (j, i, 0))),
        compiler_params=_cp(("arbitrary", "arbitrary")),
    )(slots, g, r1)


def _exchange_ici(sa, sb, small):
    def body(sa_ref, sb_ref, sm_ref, ra_ref, rb_ref, rs_ref, send_sems, recv_sems, small_send, small_recv, local_sem):
        x, y, c = _place()
        me = _slot((x, y, c))
        chips = _other_chips(x, y)

        def peer(k):
            return (1 - x if k & 4 else x, 1 - y if k & 2 else y, 1 - c if k & 1 else c)

        mine = pltpu.make_async_copy(sm_ref, rs_ref.at[me], local_sem)
        mine.start()
        cps = []
        for j, chip in enumerate(chips):
            for t, (src, dst) in enumerate(((sa_ref, ra_ref), (sb_ref, rb_ref))):
                cps.append(pltpu.make_async_remote_copy(
                    src_ref=src.at[j], dst_ref=dst.at[j], send_sem=send_sems.at[t, j], recv_sem=recv_sems.at[t, j],
                    device_id=(*chip, c), device_id_type=MESH))
        for cp in cps:
            cp.start()
        smalls = [pltpu.make_async_remote_copy(
            src_ref=sm_ref, dst_ref=rs_ref.at[me], send_sem=small_send.at[k - 1], recv_sem=small_recv.at[k - 1],
            device_id=peer(k), device_id_type=MESH) for k in range(1, N_DEV)]
        for cp in smalls:
            cp.start()
        for cp in cps:
            cp.wait()
        for k in range(1, N_DEV):
            arrived = rs_ref.at[_slot(peer(k))]
            pltpu.make_async_remote_copy(
                src_ref=arrived, dst_ref=arrived, send_sem=small_send.at[k - 1], recv_sem=small_recv.at[k - 1],
                device_id=peer(k), device_id_type=MESH).wait_recv()
        for cp in smalls:
            cp.wait_send()
        mine.wait()

    any_spec = pl.BlockSpec(memory_space=pl.ANY)
    return pl.pallas_call(
        body, name="exchange_ici",
        out_shape=[jax.ShapeDtypeStruct(sa.shape, sa.dtype), jax.ShapeDtypeStruct(sb.shape, sb.dtype),
                   jax.ShapeDtypeStruct((N_DEV,) + small.shape, small.dtype)],
        in_specs=[any_spec] * 3,
        out_specs=[any_spec] * 3,
        scratch_shapes=[pltpu.SemaphoreType.DMA((2, 3)), pltpu.SemaphoreType.DMA((2, 3)),
                        pltpu.SemaphoreType.DMA((7,)), pltpu.SemaphoreType.DMA((7,)), pltpu.SemaphoreType.DMA],
    )(sa, sb, small)


def _adamw(name, w, m, v, parts, tb):
    R, Cc = w.shape
    assert R % tb == 0
    n_parts = len(parts)

    def kern(*refs):
        w_ref, m_ref, v_ref = refs[:3]
        g = None
        for r_ref, (_, n) in zip(refs[3:3 + n_parts], parts):
            for s in range(n):
                term = r_ref[s].astype(F32)
                g = term if g is None else g + term
        g_out, d_out, m_out, v_out = refs[3 + n_parts:]
        mn = ADAM_B1 * m_ref[...] + (1.0 - ADAM_B1) * g
        vn = ADAM_B2 * v_ref[...] + (1.0 - ADAM_B2) * (g * g)
        m_hat = mn / (1.0 - ADAM_B1 ** ADAM_STEP)
        v_hat = vn / (1.0 - ADAM_B2 ** ADAM_STEP)
        g_out[...] = g
        d_out[...] = -ADAM_LR * (m_hat / (jnp.sqrt(v_hat) + ADAM_EPS) + ADAM_WD * w_ref[...])
        m_out[...] = mn
        v_out[...] = vn

    blk = pl.BlockSpec((tb, Cc), lambda i: (i, 0))
    return pl.pallas_call(
        kern, name=name,
        out_shape=[jax.ShapeDtypeStruct((R, Cc), F32)] * 4,
        grid=(R // tb,),
        in_specs=[blk] * 3 + [pl.BlockSpec((n, tb, Cc), lambda i: (0, i, 0)) for (_, n) in parts],
        out_specs=[blk] * 4,
        compiler_params=_cp(("arbitrary",)),
    )(w, m, v, *[a for (a, _) in parts])


def _pad_cols(a, w):
    return jnp.pad(a, ((0, 0), (0, w - a.shape[1])))


def _pad_rows(a, r):
    return jnp.pad(a, ((0, r - a.shape[0]), (0, 0)))


def _pack_b(pf, pr, wo, w2, a2, rows):
    body = jnp.concatenate([pf, pr, wo.reshape(2048, 256), jnp.concatenate([w2, a2], axis=1)], axis=0)
    return _pad_rows(body, rows)


def kernel(x, norm_gain, w_in, fox_forget_bias, rwkv_shift_mix, rwkv_w0, rwkv_w2, rwkv_a0, rwkv_a2, rwkv_k_k, rwkv_k_a, rwkv_r_k, rwkv_ln_w, rwkv_ln_b, w_proj_fox, w_proj_rwkv, w_out, final_norm_gain, loss_target, m_norm_gain, m_w_in, m_fox_forget_bias, m_rwkv_shift_mix, m_rwkv_w0, m_rwkv_w2, m_rwkv_a0, m_rwkv_a2, m_rwkv_k_k, m_rwkv_k_a, m_rwkv_r_k, m_rwkv_ln_w, m_rwkv_ln_b, m_w_proj_fox, m_w_proj_rwkv, m_w_out, m_final_norm_gain, v_norm_gain, v_w_in, v_fox_forget_bias, v_rwkv_shift_mix, v_rwkv_w0, v_rwkv_w2, v_rwkv_a0, v_rwkv_a2, v_rwkv_k_k, v_rwkv_k_a, v_rwkv_r_k, v_rwkv_ln_w, v_rwkv_ln_b, v_w_proj_fox, v_w_proj_rwkv, v_w_out, v_final_norm_gain):
    T, D = x.shape[1], x.shape[2]
    assert D == 2048 and T % LANE == 0
    NI = w_in.shape[2]
    IN = N_DEV * NI
    RB = 4224
    x2 = x[0]
    lt2 = loss_target[0]
    me = _slot(_place())

    wa, wb = _allgather_weights(
        w_in[0].astype(BF16),
        _pack_b(w_proj_fox[0], w_proj_rwkv[0], w_out[0], rwkv_w2[0], rwkv_a2[0], RB).astype(BF16))
    sections = [(0, 4096, 0), (4104, 4096, 4096), (8392, 4096, 8192), (4096, 8, 12288), (8200, 96, 12544), (8296, 96, 12672)]
    NP = 12800
    pieces, at_col = [], 0
    for lo, width, pad_lo in sections:
        if pad_lo > at_col:
            pieces.append(jnp.zeros((D, pad_lo - at_col), BF16))
        col = lo
        while col < lo + width:
            d = col // NI
            stop = min(lo + width, (d + 1) * NI)
            pieces.append(wa[d, :, col - d * NI:stop - d * NI])
            col = stop
        at_col = pad_lo + width
    pieces.append(jnp.zeros((D, NP - at_col), BF16))
    w_pad = jnp.concatenate(pieces, axis=1)
    F_CB, LORA_CB = 96, 49
    wpf = wb[:, 0:1024, :].transpose(1, 0, 2).reshape(1024, D)
    wpr = wb[:, 1024:2048, :].transpose(1, 0, 2).reshape(1024, D)
    wo = wb[:, 2048:4096, :].reshape(N_DEV * 256, D)
    w2p = _pad_rows(wb[:, 4096:4192, 0:128].transpose(1, 0, 2).reshape(96, 1024).astype(F32), LANE)
    a2p = _pad_rows(wb[:, 4096:4192, 128:256].transpose(1, 0, 2).reshape(96, 1024).astype(F32), LANE)

    mu = rwkv_shift_mix
    mu_main = mu[:, 0:4096]
    mu_lora = jnp.concatenate([_pad_cols(mu[:, 4096:4192], LANE), _pad_cols(mu[:, 4192:4288], LANE)], axis=1)
    bias_pad = _pad_cols(fox_forget_bias, LANE)
    rk_flat = rwkv_r_k.reshape(1, 1024)
    gf = final_norm_gain.reshape(1, D)

    tb = min(256, T)
    tbh = min(128, T)
    (h,) = _rowcall("rms_fwd", T, tb, [_rows(x2, tb, D), _whole(norm_gain)],
                    lambda i, xv, g: ([_rms_math(xv, g)], []), [(D, BF16)])
    u = _mm(h, w_pad, name="mm_in")

    c8 = _gates_fwd(u, bias_pad, T, F_CB)
    c3 = c8.reshape(8, 1, T)
    o_raw, o_a, lse = _attn_fwd(u, c3, T)

    def shift_body(i, um, hm, ul, hl, mm_, ml):
        outs = []
        for uv, hv, mv in ((um, hm, mm_), (ul, hl, ml)):
            hv = jnp.where(i == 0, 0.0, hv)
            prev = pltpu.roll(jnp.concatenate([hv, uv], axis=0), 1, 0)[SUB:]
            outs.append(uv + (prev - uv) * mv)
        return outs, []

    def halo_prev(arr, w, cb):
        return (arr, (SUB, w), lambda i: (jnp.maximum(i * (tbh // SUB) - 1, 0), cb))

    xs, xl = _rowcall("rwkv_shift_fwd", T, tbh,
                      [_rows(u, tbh, 4096, 1), halo_prev(u, 4096, 1), _rows(u, tbh, 256, LORA_CB), halo_prev(u, 256, LORA_CB),
                       _whole(mu_main), _whole(mu_lora)],
                      shift_body, [(4096, F32), (256, F32)])

    prep_par = [_whole(rwkv_w0), _whole(rwkv_a0), _whole(rwkv_k_k), _whole(rwkv_k_a), _whole(w2p), _whole(a2p)]
    prep_rows = [_rows(xs, tbh, 1024, 1), _rows(xl, tbh, LANE, 0), _rows(xl, tbh, LANE, 1)]
    lw, kp, an, bb = _rowcall("rwkv_prep_fwd", T, tbh, prep_rows + prep_par,
                              lambda i, *a: (list(_prep_math(*a)), []), [(1024, F32)] * 4)
    y, states = _scan_fwd(xs, lw, kp, an, bb, T)
    post_rows = [_rows(y, tbh, 1024), _rows(xs, tbh, 1024, 0), _rows(kp, tbh, 1024), _rows(xs, tbh, 1024, 2), _rows(xs, tbh, 1024, 3)]
    post_par = [_whole(rwkv_ln_w), _whole(rwkv_ln_b), _whole(rk_flat)]
    (o_b,) = _rowcall("rwkv_post_fwd", T, tbh, post_rows + post_par,
                      lambda i, *a: ([_post_math(*a)], []), [(1024, BF16)])

    pa = _mm(o_a, wpf, name="mm_proj_fox")
    pb = _mm(o_b, wpr, name="mm_proj_rwkv")
    merge_rows = [_rows(u, tb, D, 4), _rows(u, tb, D, 5), _rows(pa, tb, D), _rows(pb, tb, D)]
    (mg,) = _rowcall("merge_fwd", T, tb, merge_rows, lambda i, *a: ([_merge_math(*a)], []), [(D, BF16)])
    mo = _mm(mg, wo, name="mm_out")

    def head_body(i, xv, mov, ltv, g):
        out = xv + mov
        r = lax.rsqrt(jnp.mean(out * out, axis=-1, keepdims=True) + RMS_EPS)
        yn = out * r
        err = yn * g - ltv
        loss = 0.5 * jnp.sum(jnp.sum(err * err, axis=-1, keepdims=True), axis=0, keepdims=True) / D
        dyv = err / D
        dyn = dyv * g
        dout = r * (dyn - yn * jnp.mean(dyn * yn, axis=-1, keepdims=True))
        return [dout], [loss, jnp.sum(dyv * yn, axis=0, keepdims=True)]

    dout, loss_p, dgf_p = _rowcall("loss_head", T, tb, [_rows(x2, tb, D), _rows(mo, tb, D), _rows(lt2, tb, D), _whole(gf)],
                                   head_body, [(D, F32)], [(1, 1), (1, D)])

    dm = _mm(dout, wo, tb=True, name="mm_out_dx")
    dwo = _mm(mg.T, dout, out_dtype=BF16, name="mm_out_dw")

    def merge_bwd_body(i, ga, gb, pav, pbv, dmv):
        _, vjp = jax.vjp(_merge_math, ga, gb, pav, pbv)
        dga, dgb, dpa, dpb = vjp(dmv)
        return [dga, dgb, dpa, dpb], []

    dga, dgb, dpa, dpb = _rowcall("merge_bwd", T, tb, merge_rows + [_rows(dm, tb, D)], merge_bwd_body,
                                  [(D, BF16), (D, BF16), (D, BF16), (D, BF16)])
    doa = _mm(dpa, wpf, tb=True, name="mm_proj_fox_dx")
    dwpf = _mm(o_a.T, dpa, out_dtype=BF16, name="mm_proj_fox_dw")
    dob = _mm(dpb, wpr, tb=True, name="mm_proj_rwkv_dx")
    dwpr = _mm(o_b.T, dpb, out_dtype=BF16, name="mm_proj_rwkv_dw")

    do_b, dza, delta = _attn_delta(u, c3, o_raw, lse, doa, T)
    dq, dk, dv, dc3 = _attn_bwd(u, c3, lse, do_b, delta, T)
    dfl, dbias_p = _gates_bwd(dc3.reshape(8, T), u, bias_pad, T, F_CB)

    def post_bwd_body(i, yv, rv, kpv, vv, zv, lnw, lnb, rkv, dobv):
        _, vjp = jax.vjp(_post_math, yv, rv, kpv, vv, zv, lnw, lnb, rkv)
        dy_, dr_, dkp_, dv_, dz_, dlnw, dlnb, drk = vjp(dobv)
        return [dy_, dr_, dkp_, dv_, dz_], [dlnw, dlnb, drk]

    dy_s, dr_p, dkp_p, dv_p, dzb, dlnw_p, dlnb_p, drk_p = _rowcall(
        "rwkv_post_bwd", T, tbh, post_rows + post_par + [_rows(dob, tbh, 1024)], post_bwd_body,
        [(1024, F32)] * 5, [(1, 1024)] * 3)
    dr_s, dlw, dkp_s, dv_s, dan, dbb = _scan_bwd(xs, lw, kp, an, bb, states, dy_s, T)

    def prep_bwd_body(i, xk, xwd, xad, w0, a0, kkw, kaw, w2v, a2v, dlw_, dkp1, dkp2, dan_, dbb_, dr1, dr2, dv1, dv2, dz_):
        _, vjp = jax.vjp(_prep_math, xk, xwd, xad, w0, a0, kkw, kaw, w2v, a2v)
        dxk, dxwd, dxad, dw0, da0, dkk, dka, dw2, da2 = vjp((dlw_, dkp1 + dkp2, dan_, dbb_))
        return [[dr1 + dr2, dxk, dv1 + dv2, dz_], [dxwd, dxad]], [dw0, da0, dkk, dka, dw2, da2]

    cots = [dlw, dkp_s, dkp_p, dan, dbb, dr_s, dr_p, dv_s, dv_p, dzb]
    dxs, dxl, dw0_p, da0_p, dkk_p, dka_p, dw2_p, da2_p = _rowcall(
        "rwkv_prep_bwd", T, tbh, prep_rows + prep_par + [_rows(c_, tbh, 1024) for c_ in cots], prep_bwd_body,
        [(4096, F32), (256, F32)], [(1, 1024)] * 4 + [(LANE, 1024)] * 2)

    def shift_bwd_body(i, dm_, hm, dl_, hl, um, pm, ul, pl_, mm_, ml):
        last = i == T // tbh - 1
        outs, accs = [], []
        for dv_, hv, uv, pv, mv in ((dm_, hm, um, pm, mm_), (dl_, hl, ul, pl_, ml)):
            hv = jnp.where(last, 0.0, hv)
            nxt = pltpu.roll(jnp.concatenate([dv_, hv], axis=0), tbh + SUB - 1, 0)[:tbh]
            pv = jnp.where(i == 0, 0.0, pv)
            prev = pltpu.roll(jnp.concatenate([pv, uv], axis=0), 1, 0)[SUB:]
            outs.append(dv_ * (1.0 - mv) + nxt * mv)
            accs.append(jnp.sum(dv_ * (prev - uv), axis=0, keepdims=True))
        return outs, accs

    def halo_next(arr, w, cb):
        last_blk = T // SUB - 1
        return (arr, (SUB, w), lambda i: (jnp.minimum((i + 1) * (tbh // SUB), last_blk), cb))

    du_b, du_l, dmu_main_p, dmu_lora_p = _rowcall(
        "rwkv_shift_bwd", T, tbh,
        [_rows(dxs, tbh, 4096), halo_next(dxs, 4096, 0), _rows(dxl, tbh, 256), halo_next(dxl, 256, 0),
         _rows(u, tbh, 4096, 1), halo_prev(u, 4096, 1), _rows(u, tbh, 256, LORA_CB), halo_prev(u, 256, LORA_CB),
         _whole(mu_main), _whole(mu_lora)],
        shift_bwd_body, [(4096, BF16), (256, BF16)], [(1, 4096), (1, 256)])

    du = jnp.concatenate([dq.astype(BF16), dk.astype(BF16), dv.astype(BF16), dza.astype(BF16), du_b, dga, dgb,
                          dfl.astype(BF16), jnp.zeros((T, LANE), BF16), du_l], axis=1)
    dh = _mm(du, w_pad, tb=True, tm=1024, tn=1024, tk=NP // 10, name="mm_in_dx")
    dw_pad = _mm(h.T, du, out_dtype=BF16, name="mm_in_dw")

    def rms_bwd_body(i, xv, g, dhv, doutv):
        _, vjp = jax.vjp(_rms_math, xv, g)
        dx_, dg_ = vjp(dhv)
        return [dx_ + doutv], [dg_]

    grad_x2, dng_p = _rowcall("rms_bwd", T, tb, [_rows(x2, tb, D), _whole(norm_gain), _rows(dh, tb, D), _rows(dout, tb, D)],
                              rms_bwd_body, [(D, F32)], [(1, D)])

    by_col = sorted(sections)
    blocks = []
    for d in range(N_DEV):
        parts = []
        for lo, width, pad_lo in by_col:
            a, b = max(lo, d * NI), min(lo + width, (d + 1) * NI)
            if a < b:
                parts.append(dw_pad[:, pad_lo + a - lo:pad_lo + b - lo])
        blocks.append(jnp.concatenate(parts, axis=1)[None])
    ga = jnp.concatenate(blocks, axis=0)
    lora_g = jnp.concatenate([dw2_p[:96].reshape(96, N_DEV, 128).transpose(1, 0, 2),
                              da2_p[:96].reshape(96, N_DEV, 128).transpose(1, 0, 2)], axis=2).astype(BF16)
    gb = jnp.concatenate([dwpf.reshape(1024, N_DEV, 256).transpose(1, 0, 2),
                          dwpr.reshape(1024, N_DEV, 256).transpose(1, 0, 2),
                          dwo.reshape(N_DEV, 2048, 256), lora_g, jnp.zeros((N_DEV, RB - 4192, 256), BF16)], axis=1)

    dmu = jnp.concatenate([dmu_main_p, dmu_lora_p[:, 0:96], dmu_lora_p[:, 128:224]], axis=1)
    small_parts = [dng_p, dbias_p[:, 0:8], dmu, dw0_p, da0_p, dkk_p, dka_p, drk_p, dlnw_p, dlnb_p, dgf_p, loss_p]
    SR = 128
    small = _pad_cols(jnp.concatenate(small_parts, axis=1), SR * LANE).reshape(SR, LANE)

    xx, yy, cc = _place()
    chip_slots = jnp.stack([_slot((*chip, cc)) for chip in _other_chips(xx, yy)]).astype(jnp.int32)
    r1a, r1b = _exchange_pair(ga, gb)
    sa = _pair_add("pair_add_w_in", ga, r1a, chip_slots, 256)
    sb = _pair_add("pair_add_packed", gb, r1b, chip_slots, 384)
    r2a, r2b, rs = _exchange_ici(sa, sb, small)
    ga_own = lax.dynamic_index_in_dim(ga, me, 0, keepdims=True)
    gb_own = lax.dynamic_index_in_dim(gb, me, 0, keepdims=True)

    g_in, d_in, m_in, v_in = _adamw("adamw_w_in", w_in[0], m_w_in[0], v_w_in[0], [(ga_own, 1), (r1a, 1), (r2a, 3)], min(128, D))
    pk = lambda pf, pr, wo_, w2_, a2_: _pack_b(pf[0], pr[0], wo_[0], w2_[0], a2_[0], RB)
    outs_b = _adamw("adamw_packed", pk(w_proj_fox, w_proj_rwkv, w_out, rwkv_w2, rwkv_a2),
                    pk(m_w_proj_fox, m_w_proj_rwkv, m_w_out, m_rwkv_w2, m_rwkv_a2),
                    pk(v_w_proj_fox, v_w_proj_rwkv, v_w_out, v_rwkv_w2, v_rwkv_a2), [(gb_own, 1), (r1b, 1), (r2b, 3)], 384)

    def pack_small(ng, fb, sm, w0, a0, kk_, ka_, rk_, lnw, lnb, fg):
        parts = [ng, fb, sm, w0, a0, kk_, ka_, rk_.reshape(1, 1024), lnw, lnb, fg.reshape(1, D), jnp.zeros((1, 1), F32)]
        return _pad_cols(jnp.concatenate(parts, axis=1), SR * LANE).reshape(SR, LANE)

    outs_s = _adamw("adamw_small",
                    pack_small(norm_gain, fox_forget_bias, rwkv_shift_mix, rwkv_w0, rwkv_a0, rwkv_k_k, rwkv_k_a, rwkv_r_k,
                               rwkv_ln_w, rwkv_ln_b, final_norm_gain),
                    pack_small(m_norm_gain, m_fox_forget_bias, m_rwkv_shift_mix, m_rwkv_w0, m_rwkv_a0, m_rwkv_k_k, m_rwkv_k_a,
                               m_rwkv_r_k, m_rwkv_ln_w, m_rwkv_ln_b, m_final_norm_gain),
                    pack_small(v_norm_gain, v_fox_forget_bias, v_rwkv_shift_mix, v_rwkv_w0, v_rwkv_a0, v_rwkv_k_k, v_rwkv_k_a,
                               v_rwkv_r_k, v_rwkv_ln_w, v_rwkv_ln_b, v_final_norm_gain),
                    [(rs, N_DEV)], SR)

    def unpack_b(pkd):
        return dict(w_proj_fox=pkd[0:1024][None], w_proj_rwkv=pkd[1024:2048][None], w_out=pkd[2048:4096].reshape(1, 256, D),
                    rwkv_w2=pkd[4096:4192, 0:128][None], rwkv_a2=pkd[4096:4192, 128:256][None])

    def unpack_s(pkd):
        flat = pkd.reshape(1, SR * LANE)
        names = [("norm_gain", D), ("fox_forget_bias", 8), ("rwkv_shift_mix", 4288), ("rwkv_w0", 1024), ("rwkv_a0", 1024),
                 ("rwkv_k_k", 1024), ("rwkv_k_a", 1024), ("rwkv_r_k", 1024), ("rwkv_ln_w", 1024), ("rwkv_ln_b", 1024),
                 ("final_norm_gain", D), ("loss", 1)]
        out, off = {}, 0
        for nm, n in names:
            out[nm] = flat[:, off:off + n]
            off += n
        out["rwkv_r_k"] = out["rwkv_r_k"].reshape(1, 16, 64)
        out["final_norm_gain"] = out["final_norm_gain"].reshape(D)
        return out

    order = ["norm_gain", "w_in", "fox_forget_bias", "rwkv_shift_mix", "rwkv_w0", "rwkv_w2", "rwkv_a0", "rwkv_a2", "rwkv_k_k",
             "rwkv_k_a", "rwkv_r_k", "rwkv_ln_w", "rwkv_ln_b", "w_proj_fox", "w_proj_rwkv", "w_out", "final_norm_gain"]
    result = []
    loss = None
    for kind, big in enumerate((g_in, d_in, m_in, v_in)):
        d = {**unpack_b(outs_b[kind]), **unpack_s(outs_s[kind]), "w_in": big[None]}
        if kind == 0:
            loss = d["loss"].reshape(())
        result += [d[n] for n in order]
    return (loss, grad_x2[None], *result)
```

```python
import functools

import jax
import jax.numpy as jnp
from jax import lax
from jax.experimental import pallas as pl
from jax.experimental.pallas import tpu as pltpu

F32 = jnp.float32
BF16 = jnp.bfloat16
HI = lax.Precision.HIGHEST
MESH = pl.DeviceIdType.MESH

FOX_HD = 128
RW_HD = 64
RMS_EPS = 1e-6
GN_EPS = 64e-5
L2_EPS = 1e-12
ADAM_LR = 0.001
ADAM_B1 = 0.9
ADAM_B2 = 0.999
ADAM_EPS = 1e-08
ADAM_WD = 0.01
ADAM_STEP = 10

LANE = 128
SUB = 8
VMEM_LIMIT = 56 * 1024 * 1024
N_DEV = 8
CHUNK = 64
SCAN_GROUP = 4
PS = None
NEG = -1e30


def _scan_shape(T):
    c = min(CHUNK, T)
    return c, min(SCAN_GROUP, T // c)


def _cp(sem=None):
    return pltpu.CompilerParams(dimension_semantics=sem, vmem_limit_bytes=VMEM_LIMIT)


def _sigmoid(x):
    return jax.nn.sigmoid(x)


def _softplus(x):
    return jnp.maximum(x, 0.0) + jnp.log(1.0 + jnp.exp(-jnp.abs(x)))


def _nn(a, b, prec=None):
    return lax.dot_general(a, b, (((1,), (0,)), ((), ())), precision=prec, preferred_element_type=F32)


def _nt(a, b, prec=None):
    return lax.dot_general(a, b, (((1,), (1,)), ((), ())), precision=prec, preferred_element_type=F32)


def _tn(a, b, prec=None):
    return lax.dot_general(a, b, (((0,), (0,)), ((), ())), precision=prec, preferred_element_type=F32)


def _iota2(shape, dim):
    return lax.broadcasted_iota(jnp.int32, shape, dim)


def _seg_sum(x):
    r = _iota2((LANE, LANE), 0) // RW_HD
    c = _iota2((LANE, LANE), 1) // RW_HD
    bd = (r == c).astype(F32)
    parts = [_nn(x[:, j * LANE:(j + 1) * LANE], bd, HI) for j in range(x.shape[1] // LANE)]
    return parts[0] if len(parts) == 1 else jnp.concatenate(parts, axis=1)


def _mm(a, b, *, tb=False, out_dtype=F32, tm=512, tn=512, tk=None, name, side=None):
    M, K = a.shape
    N = b.shape[0] if tb else b.shape[1]
    tm, tn = min(tm, M), min(tn, N)
    tk = K if tk is None else tk
    nk = K // tk
    assert M % tm == 0 and N % tn == 0 and K % tk == 0
    gi, gj = M // tm, N // tn
    b_spec = pl.BlockSpec((tn, tk), lambda i, j, k: (j, k)) if tb else pl.BlockSpec((tk, tn), lambda i, j, k: (k, j))
    side_ins, side_outs, side_scr, side_make = side if side is not None else ((), (), (), None)
    n_si, n_so = len(side_ins), len(side_outs)
    n_acc = 0 if nk == 1 else 1

    def body(*refs):
        a_ref, b_ref = refs[:2]
        o_ref = refs[2 + n_si]
        scr = refs[3 + n_si + n_so:]
        k = pl.program_id(2)
        if side_make is not None:
            start, finish = side_make(refs[2:2 + n_si], refs[3 + n_si:3 + n_si + n_so], scr[n_acc:])
            first = jnp.logical_and(jnp.logical_and(pl.program_id(0) == 0, pl.program_id(1) == 0), k == 0)
            last = jnp.logical_and(jnp.logical_and(pl.program_id(0) == gi - 1, pl.program_id(1) == gj - 1), k == nk - 1)
            pl.when(first)(start)
        av = a_ref[...].astype(BF16)
        bv = b_ref[...].astype(BF16)
        p = _nt(av, bv) if tb else _nn(av, bv)
        if nk == 1:
            o_ref[...] = p.astype(out_dtype)
        else:
            acc_ref = scr[0]

            @pl.when(k == 0)
            def _():
                acc_ref[...] = p

            @pl.when(k > 0)
            def _():
                acc_ref[...] += p

            @pl.when(k == nk - 1)
            def _():
                o_ref[...] = acc_ref[...].astype(out_dtype)
        if side_make is not None:
            pl.when(last)(finish)

    any_spec = pl.BlockSpec(memory_space=pl.ANY)
    res = pl.pallas_call(
        body, name=name,
        out_shape=[jax.ShapeDtypeStruct((M, N), out_dtype)] + list(side_outs),
        grid=(gi, gj, nk),
        in_specs=[pl.BlockSpec((tm, tk), lambda i, j, k: (i, k)), b_spec] + [any_spec] * n_si,
        out_specs=[pl.BlockSpec((tm, tn), lambda i, j, k: (i, j))] + [any_spec] * n_so,
        scratch_shapes=([] if nk == 1 else [pltpu.VMEM((tm, tn), F32)]) + list(side_scr),
        compiler_params=_cp(("arbitrary",) * 3 if side is not None else ("parallel", "parallel", "arbitrary")),
    )(a, b, *side_ins)
    return res if side is not None else res[0]


def _rows(arr, tb, w, cb=0):
    return (arr, (tb, w), lambda i: (i, cb))


def _whole(arr):
    nd = arr.ndim
    return (arr, arr.shape, lambda i: (0,) * nd)


def _rowcall(name, T, tb, ins, body, outs, accs=()):
    n_in, n_out = len(ins), len(outs)

    def kern(*refs):
        i = pl.program_id(0)
        vals = [r[...] for r in refs[:n_in]]
        ro, ao = body(i, *vals)
        for r, v in zip(refs[n_in:n_in + n_out], ro):
            if isinstance(v, (list, tuple)):
                off = 0
                for piece in v:
                    w = piece.shape[1]
                    r[:, off:off + w] = piece.astype(r.dtype)
                    off += w
            else:
                r[...] = v.astype(r.dtype)
        if accs:
            acc_refs = refs[n_in + n_out:]

            @pl.when(i == 0)
            def _():
                for r in acc_refs:
                    r[...] = jnp.zeros(r.shape, F32)

            for r, v in zip(acc_refs, ao):
                r[...] += v

    out_shape = [jax.ShapeDtypeStruct((T, w), dt) for (w, dt) in outs] + [jax.ShapeDtypeStruct(s, F32) for s in accs]
    out_specs = [pl.BlockSpec((tb, w), lambda i: (i, 0)) for (w, dt) in outs] + [pl.BlockSpec(s, lambda i: (0, 0)) for s in accs]
    res = pl.pallas_call(
        kern, name=name,
        out_shape=out_shape,
        grid=(T // tb,),
        in_specs=[pl.BlockSpec(bs, im) for (_, bs, im) in ins],
        out_specs=out_specs,
        compiler_params=_cp(("arbitrary",)),
    )(*[a for (a, _, _) in ins])
    return res


def _rms_math(x, g):
    r = lax.rsqrt(jnp.mean(x * x, axis=-1, keepdims=True) + RMS_EPS)
    return x * r * g


def _merge_math(ga, gb, pa, pb):
    return _sigmoid(ga) * pa + _sigmoid(gb) * pb


def _prep_math(xk, xwd, xad, w0, a0, kk_w, ka_w, w2p, a2p):
    z = w0 + _nn(jnp.tanh(xwd), w2p, HI)
    w = -_softplus(-z) - 0.5
    lw = -jnp.exp(w)
    ag = _sigmoid(a0 + _nn(xad, a2p, HI))
    p = xk * kk_w
    n = jnp.maximum(jnp.sqrt(_seg_sum(p * p)), L2_EPS)
    kk = p / n
    kp = xk * (1.0 + (ag - 1.0) * ka_w)
    return lw, kp, -kk, kk * ag


def _post_math(y, r, kp, v, z, lnw, lnb, rk):
    inv = 1.0 / RW_HD
    mu = _seg_sum(y) * inv
    d = y - mu
    var = _seg_sum(d * d) * inv
    yn = d * lax.rsqrt(var + GN_EPS) * lnw + lnb
    bonus = _seg_sum(r * kp * rk) * v
    return (yn + bonus) * (z * _sigmoid(z))


def _scan_group(s0, *flat):
    G = len(flat) // 6
    ch = [flat[6 * i:6 * i + 6] for i in range(G)]
    C = ch[0][0].shape[0]
    C2 = 2 * C
    cat = jnp.concatenate
    m0 = _iota2((1, LANE), 1) < RW_HD
    mask0 = m0.astype(F32)
    mask1 = 1.0 - mask0
    r2 = _iota2((C2, C2), 0)
    c2 = _iota2((C2, C2), 1)
    dist = r2 - c2
    in_head = dist <= r2 % C
    eye = (r2 == c2).astype(F32)
    lower = (_iota2((C, C), 0) >= _iota2((C, C), 1)).astype(F32)
    bd = (_iota2((LANE, LANE), 0) // RW_HD) == (_iota2((LANE, LANE), 1) // RW_HD)

    def tri(m, strict):
        return jnp.where(dist > 0 if strict else dist >= 0, jnp.where(in_head, m, 0.0), 0.0)

    def sel(z):
        return jnp.where(m0, z[:C], z[C:])

    gs = [_nn(lower, c[1], HI) for c in ch]
    pre = []
    for (r, lw, k, v, a, b), g in zip(ch, gs):
        g_end = jnp.sum(lw, axis=0, keepdims=True)
        en = jnp.exp(-g)
        ec = jnp.exp(g_end - g)
        pre.append(dict(at=a * jnp.exp(g - lw), rt=r * jnp.exp(g), bt=b * en, kt=k * en, bh=b * ec, kh=k * ec,
                        dec=jnp.exp(g_end), v=v))
    grams = [_nt(cat([p["at"] * mask0, p["at"] * mask1, p["rt"] * mask0, p["rt"] * mask1], axis=0),
                 cat([p["bt"], p["bt"], p["kt"], p["kt"]], axis=0), PS) for p in pre]
    mab = [tri(gm[:C2, :C2], True) for gm in grams]
    mp = [_nn(m, m, PS) for m in mab]
    tinv = [eye + m for m in mab]
    n = 2
    while n < C:
        last = 2 * n >= C
        for i in range(G):
            if last:
                tinv[i] = tinv[i] + _nn(mp[i], tinv[i], PS)
            else:
                z = _nn(mp[i], cat([mp[i], tinv[i]], axis=1), PS)
                mp[i], tinv[i] = z[:, :C2], tinv[i] + z[:, C2:]
        n *= 2
    xv = [sel(_nn(tri(gm[:C2, C2:], True), cat([p["v"], p["v"]], axis=0), PS)) for gm, p in zip(grams, pre)]
    ys, s = [], s0
    for i in range(G):
        p, gm = pre[i], grams[i]
        sx = _nt(cat([p["at"], p["rt"]], axis=0), s, PS)
        x = sx[:C] + xv[i]
        u = sel(_nn(tinv[i], cat([x, x], axis=0), PS))
        v = p["v"]
        ys.append(sx[C:] + sel(_nn(cat([tri(gm[C2:, :C2], False), tri(gm[C2:, C2:], False)], axis=1), cat([u, u, v, v], axis=0), PS)))
        s = s * p["dec"] + jnp.where(bd, _tn(cat([u, v], axis=0), cat([p["bh"], p["kh"]], axis=0), PS), 0.0)
    return tuple(ys), s


def _scan_fwd(xs, lw, kp, an, bb, T):
    C, G = _scan_shape(T)
    nc = T // (C * G)
    npair = 1024 // LANE

    def kern(r_ref, lw_ref, k_ref, v_ref, a_ref, b_ref, y_ref, st_ref, s_scr):
        n = pl.program_id(1)

        @pl.when(n == 0)
        def _():
            s_scr[...] = jnp.zeros(s_scr.shape, F32)

        s0 = s_scr[...]
        st_ref[0, 0] = s0
        ins = (r_ref, lw_ref, k_ref, v_ref, a_ref, b_ref)
        ys, s1 = _scan_group(s0, *[ref[i * C:(i + 1) * C, :] for i in range(G) for ref in ins])
        for i in range(G):
            y_ref[i * C:(i + 1) * C, :] = ys[i]
        s_scr[...] = s1

    def col(off):
        return pl.BlockSpec((C * G, LANE), lambda p, n: (n, off + p))

    return pl.pallas_call(
        kern, name="rwkv_scan_fwd",
        out_shape=[jax.ShapeDtypeStruct((T, 1024), F32), jax.ShapeDtypeStruct((nc, npair, LANE, LANE), F32)],
        grid=(npair, nc),
        in_specs=[col(0), col(0), col(0), col(16), col(0), col(0)],
        out_specs=[col(0), pl.BlockSpec((1, 1, LANE, LANE), lambda p, n: (n, p, 0, 0))],
        scratch_shapes=[pltpu.VMEM((LANE, LANE), F32)],
        compiler_params=_cp(("parallel", "arbitrary")),
    )(xs, lw, kp, xs, an, bb)


def _scan_bwd(xs, lw, kp, an, bb, states, dy, T):
    C, G = _scan_shape(T)
    nc = T // (C * G)
    npair = 1024 // LANE

    def kern(r_ref, lw_ref, k_ref, v_ref, a_ref, b_ref, st_ref, dy_ref, dr_ref, dlw_ref, dk_ref, dv_ref, da_ref, db_ref, ds_scr):
        n = pl.program_id(1)

        @pl.when(n == 0)
        def _():
            ds_scr[...] = jnp.zeros(ds_scr.shape, F32)

        ins = (r_ref, lw_ref, k_ref, v_ref, a_ref, b_ref)
        _, vjp = jax.vjp(_scan_group, st_ref[0, 0], *[ref[i * C:(i + 1) * C, :] for i in range(G) for ref in ins])
        grads = vjp((tuple(dy_ref[i * C:(i + 1) * C, :] for i in range(G)), ds_scr[...]))
        ds_scr[...] = grads[0]
        outs = (dr_ref, dlw_ref, dk_ref, dv_ref, da_ref, db_ref)
        for i in range(G):
            for t, ref in enumerate(outs):
                ref[i * C:(i + 1) * C, :] = grads[1 + 6 * i + t]

    def col(off):
        return pl.BlockSpec((C * G, LANE), lambda p, n: (nc - 1 - n, off + p))

    return pl.pallas_call(
        kern, name="rwkv_scan_bwd",
        out_shape=[jax.ShapeDtypeStruct((T, 1024), F32)] * 6,
        grid=(npair, nc),
        in_specs=[col(0), col(0), col(0), col(16), col(0), col(0),
                  pl.BlockSpec((1, 1, LANE, LANE), lambda p, n: (nc - 1 - n, p, 0, 0)), col(0)],
        out_specs=[col(0)] * 6,
        scratch_shapes=[pltpu.VMEM((LANE, LANE), F32)],
        compiler_params=_cp(("parallel", "arbitrary")),
    )(xs, lw, kp, xs, an, bb, states, dy)


def _gates_fwd(u, bias_pad, T, f_cb):
    nb = T // LANE

    def kern(f_ref, b_ref, c_ref):
        x = f_ref[...] + b_ref[...]
        lf = jnp.minimum(x, 0.0) - jnp.log(1.0 + jnp.exp(-jnp.abs(x)))
        lft = lf.T
        ut = (_iota2((LANE, LANE), 0) <= _iota2((LANE, LANE), 1)).astype(F32)
        carry = jnp.zeros((LANE, 1), F32)
        for blk in range(nb):
            seg = lft[:, blk * LANE:(blk + 1) * LANE]
            cs = _nn(seg, ut, HI) + carry
            c_ref[:, blk * LANE:(blk + 1) * LANE] = cs[:SUB, :]
            carry = carry + jnp.sum(seg, axis=1, keepdims=True)

    return pl.pallas_call(
        kern, name="fox_gates_fwd",
        out_shape=jax.ShapeDtypeStruct((SUB, T), F32),
        grid=(1,),
        in_specs=[pl.BlockSpec((T, LANE), lambda i: (0, f_cb)), pl.BlockSpec((1, LANE), lambda i: (0, 0))],
        out_specs=pl.BlockSpec((SUB, T), lambda i: (0, 0)),
        compiler_params=_cp(("arbitrary",)),
    )(u, bias_pad)


def _gates_bwd(dc, u, bias_pad, T, f_cb):
    nb = T // LANE

    def kern(dc_ref, f_ref, b_ref, dfl_ref, db_ref):
        dcv = jnp.concatenate([dc_ref[...], jnp.zeros((LANE - SUB, T), F32)], axis=0)
        lt = (_iota2((LANE, LANE), 0) >= _iota2((LANE, LANE), 1)).astype(F32)
        carry = jnp.zeros((LANE, 1), F32)
        pieces = [None] * nb
        for blk in range(nb - 1, -1, -1):
            seg = dcv[:, blk * LANE:(blk + 1) * LANE]
            pieces[blk] = _nn(seg, lt, HI) + carry
            carry = carry + jnp.sum(seg, axis=1, keepdims=True)
        dlf = (pieces[0] if nb == 1 else jnp.concatenate(pieces, axis=1)).T
        x = f_ref[...] + b_ref[...]
        dfl = dlf * _sigmoid(-x)
        dfl_ref[...] = dfl
        db_ref[...] = jnp.sum(dfl, axis=0, keepdims=True)

    return pl.pallas_call(
        kern, name="fox_gates_bwd",
        out_shape=[jax.ShapeDtypeStruct((T, LANE), F32), jax.ShapeDtypeStruct((1, LANE), F32)],
        grid=(1,),
        in_specs=[pl.BlockSpec((SUB, T), lambda i: (0, 0)), pl.BlockSpec((T, LANE), lambda i: (0, f_cb)),
                  pl.BlockSpec((1, LANE), lambda i: (0, 0))],
        out_specs=[pl.BlockSpec((T, LANE), lambda i: (0, 0)), pl.BlockSpec((1, LANE), lambda i: (0, 0))],
        compiler_params=_cp(("arbitrary",)),
    )(dc, u, bias_pad)


def _attn_block(T):
    return 256 if T % 256 == 0 and T >= 512 else 128


def _attn_fwd(u, c3, T):
    H = 8
    bq = _attn_block(T)
    nq = T // bq
    scale = FOX_HD ** -0.5

    def kern(q_ref, k_ref, v_ref, z_ref, cq_ref, ck_ref, o_ref, oa_ref, lse_ref):
        i = pl.program_id(1)
        q = (q_ref[...] * scale).astype(BF16)
        c0 = cq_ref[0][:, 0:1]
        def step(j, carry, diagonal=False):
            m, l, acc = carry
            off = pl.multiple_of(j * bq, bq)
            kj = k_ref[pl.ds(off, bq), :].astype(BF16)
            vj = v_ref[pl.ds(off, bq), :].astype(BF16)
            s = _nt(q, kj) + (c0 - ck_ref[0, :, pl.ds(off, bq)])
            if diagonal:
                s = jnp.where(_iota2((bq, bq), 1) <= _iota2((bq, bq), 0), s, NEG)
            m_new = jnp.maximum(m, jnp.max(s, axis=1, keepdims=True))
            p = jnp.exp(s - m_new)
            alpha = jnp.exp(m - m_new)
            l = alpha * l + jnp.sum(p, axis=1, keepdims=True)
            acc = alpha * acc + _nn(p.astype(BF16), vj)
            return m_new, l, acc

        init = (jnp.full((bq, 1), NEG, F32), jnp.zeros((bq, 1), F32), jnp.zeros((bq, FOX_HD), F32))
        m, l, acc = step(i, lax.fori_loop(0, i, step, init), diagonal=True)
        o = acc / l
        z = z_ref[...]
        o_ref[...] = o
        oa_ref[...] = (o * z * _sigmoid(z)).astype(BF16)
        lse_ref[0] = m + jnp.log(l)

    return pl.pallas_call(
        kern, name="fox_attn_fwd",
        out_shape=[jax.ShapeDtypeStruct((T, 1024), F32), jax.ShapeDtypeStruct((T, 1024), BF16),
                   jax.ShapeDtypeStruct((H, T, 1), F32)],
        grid=(H, nq),
        in_specs=[pl.BlockSpec((bq, LANE), lambda h, i: (i, h)),
                  pl.BlockSpec((T, LANE), lambda h, i: (0, 8 + h)),
                  pl.BlockSpec((T, LANE), lambda h, i: (0, 16 + h)),
                  pl.BlockSpec((bq, LANE), lambda h, i: (i, 24 + h)),
                  pl.BlockSpec((1, 1, bq), lambda h, i: (h, 0, i)),
                  pl.BlockSpec((1, 1, T), lambda h, i: (h, 0, 0))],
        out_specs=[pl.BlockSpec((bq, LANE), lambda h, i: (i, h)),
                   pl.BlockSpec((bq, LANE), lambda h, i: (i, h)),
                   pl.BlockSpec((1, bq, 1), lambda h, i: (h, i, 0))],
        compiler_params=_cp(("parallel", "arbitrary")),
    )(u, u, u, u, c3, c3)


def _attn_tile(qs, kj, vj, dob, cdiff, lse_i, keep):
    s = _nt(qs, kj) + cdiff
    if keep is None:
        p = jnp.exp(s - lse_i)
    else:
        p = jnp.where(keep, jnp.exp(jnp.where(keep, s, NEG) - lse_i), 0.0)
    return p, _nt(dob, vj)


def _attn_delta(u, c3, o, lse, doa, T):
    H = 8
    bq = _attn_block(T)
    nq = T // bq
    scale = FOX_HD ** -0.5

    def kern(q_ref, k_ref, v_ref, z_ref, c_ref, o_ref, lse_ref, doa_ref, do_ref, dz_ref, dl_ref):
        i = pl.program_id(1)
        z = z_ref[...]
        sg = _sigmoid(z)
        dov = doa_ref[...]
        dob = (dov * z * sg).astype(BF16)
        do_ref[...] = dob
        dz_ref[...] = dov * o_ref[...] * (sg * (1.0 + z * (1.0 - sg)))
        qs = (q_ref[...] * scale).astype(BF16)
        ioff = pl.multiple_of(i * bq, bq)
        c0 = c_ref[0, :, pl.ds(ioff, bq)][:, 0:1]
        lse_i = lse_ref[0]

        def step(j, acc, diagonal=False):
            off = pl.multiple_of(j * bq, bq)
            kj = k_ref[pl.ds(off, bq), :].astype(BF16)
            vj = v_ref[pl.ds(off, bq), :].astype(BF16)
            keep = _iota2((bq, bq), 1) <= _iota2((bq, bq), 0) if diagonal else None
            p, dp = _attn_tile(qs, kj, vj, dob, c0 - c_ref[0, :, pl.ds(off, bq)], lse_i, keep)
            return acc + jnp.sum(p * dp, axis=1, keepdims=True)

        dl_ref[0] = step(i, lax.fori_loop(0, i, step, jnp.zeros((bq, 1), F32)), diagonal=True)

    full = lambda cb: pl.BlockSpec((T, LANE), lambda h, i: (0, cb + h))
    blk = lambda cb: pl.BlockSpec((bq, LANE), lambda h, i: (i, cb + h))
    return pl.pallas_call(
        kern, name="fox_attn_delta",
        out_shape=[jax.ShapeDtypeStruct((T, 1024), BF16), jax.ShapeDtypeStruct((T, 1024), F32), jax.ShapeDtypeStruct((H, T, 1), F32)],
        grid=(H, nq),
        in_specs=[blk(0), full(8), full(16), blk(24),
                  pl.BlockSpec((1, 1, T), lambda h, i: (h, 0, 0)),
                  blk(0),
                  pl.BlockSpec((1, bq, 1), lambda h, i: (h, i, 0)),
                  blk(0)],
        out_specs=[blk(0), blk(0), pl.BlockSpec((1, bq, 1), lambda h, i: (h, i, 0))],
        compiler_params=_cp(("parallel", "arbitrary")),
    )(u, u, u, u, c3, o, lse, doa)


def _attn_bwd(u, c3, lse, do, delta, T):
    H = 8
    bq = _attn_block(T)
    nq = T // bq
    scale = FOX_HD ** -0.5

    def kern(q_ref, k_ref, v_ref, c_ref, lse_ref, do_ref, dl_ref, dq_ref, dk_ref, dv_ref, dc_ref):
        j = pl.program_id(1)

        @pl.when(j == 0)
        def _():
            dq_ref[...] = jnp.zeros(dq_ref.shape, F32)

        kj = k_ref[...].astype(BF16)
        vj = v_ref[...].astype(BF16)
        joff = pl.multiple_of(j * bq, bq)
        ck = c_ref[0, :, pl.ds(joff, bq)]

        def step(i, carry, diagonal=False):
            dk, dv, dc = carry
            off = pl.multiple_of(i * bq, bq)
            qs = (q_ref[pl.ds(off, bq), :] * scale).astype(BF16)
            dob = do_ref[pl.ds(off, bq), :]
            c0 = c_ref[0, :, pl.ds(off, bq)][:, 0:1]
            keep = _iota2((bq, bq), 1) <= _iota2((bq, bq), 0) if diagonal else None
            p, dp = _attn_tile(qs, kj, vj, dob, c0 - ck, lse_ref[0, pl.ds(off, bq), :], keep)
            ds = p * (dp - dl_ref[0, pl.ds(off, bq), :])
            dsb = ds.astype(BF16)
            dv = dv + _tn(p.astype(BF16), dob)
            dk = dk + _tn(dsb, qs)
            dq_ref[pl.ds(off, bq), :] += _nn(dsb, kj) * scale
            dc = dc - jnp.sum(ds, axis=0, keepdims=True)
            return dk, dv, dc

        init = (jnp.zeros((bq, FOX_HD), F32), jnp.zeros((bq, FOX_HD), F32), jnp.zeros((1, bq), F32))
        dk, dv, dc = lax.fori_loop(j + 1, nq, step, step(j, init, diagonal=True))
        dk_ref[...] = dk
        dv_ref[...] = dv
        dc_ref[0] = dc

    full = lambda cb: pl.BlockSpec((T, LANE), lambda h, j: (0, cb + h))
    blk = lambda cb: pl.BlockSpec((bq, LANE), lambda h, j: (j, cb + h))
    col1 = pl.BlockSpec((1, T, 1), lambda h, j: (h, 0, 0))
    return pl.pallas_call(
        kern, name="fox_attn_bwd",
        out_shape=[jax.ShapeDtypeStruct((T, 1024), F32)] * 3 + [jax.ShapeDtypeStruct((H, 1, T), F32)],
        grid=(H, nq),
        in_specs=[full(0), blk(8), blk(16), pl.BlockSpec((1, 1, T), lambda h, j: (h, 0, 0)), col1, full(0), col1],
        out_specs=[full(0), blk(0), blk(0), pl.BlockSpec((1, 1, bq), lambda h, j: (h, 0, j))],
        compiler_params=_cp(("parallel", "arbitrary")),
    )(u, u, u, c3, lse, do, delta)


def _place():
    return lax.axis_index("x"), lax.axis_index("y"), lax.axis_index("c")


def _slot(p):
    return 4 * p[0] + 2 * p[1] + p[2]


def _other_chips(x, y):
    return [(1 - x, y), (x, 1 - y), (1 - x, 1 - y)]


def _allgather_steps(in_refs, out_refs, scratch):
    (src,), (dst,) = in_refs, out_refs
    send_sems, recv_sems, local_sem = scratch
    x, y, c = _place()
    me, sibling = (x, y, c), (x, y, 1 - c)
    chips = _other_chips(x, y)

    def copy(k, block, to, from_input=False):
        d = dst.at[_slot(block)]
        return pltpu.make_async_remote_copy(
            src_ref=src if from_input else d, dst_ref=d, send_sem=send_sems.at[k], recv_sem=recv_sems.at[k],
            device_id=to, device_id_type=MESH)

    def first_copies():
        return [copy(0, me, sibling, True)] + [copy(1 + j, me, (*chip, c), True) for j, chip in enumerate(chips)]

    def start():
        pltpu.make_async_copy(src, dst.at[_slot(me)], local_sem).start()
        for cp in first_copies():
            cp.start()

    def finish():
        passed = []
        for j, chip in enumerate(chips):
            copy(1 + j, (*chip, c), me).wait_recv()
            passed.append(copy(4 + j, (*chip, c), sibling))
            passed[-1].start()
        copy(0, sibling, me).wait_recv()
        for j, chip in enumerate(chips):
            copy(4 + j, (*chip, 1 - c), me).wait_recv()
        for cp in first_copies() + passed:
            cp.wait_send()
        pltpu.make_async_copy(src, dst.at[_slot(me)], local_sem).wait()

    return start, finish


def _allgather_side(a):
    return ((a,), (jax.ShapeDtypeStruct((N_DEV,) + a.shape, a.dtype),),
            (pltpu.SemaphoreType.DMA((7,)), pltpu.SemaphoreType.DMA((7,)), pltpu.SemaphoreType.DMA), _allgather_steps)


def _allgather(a, name):
    ins, outs, scratch, make = _allgather_side(a)

    def body(a_ref, o_ref, *scr):
        start, finish = make((a_ref,), (o_ref,), scr)
        start()
        finish()

    any_spec = pl.BlockSpec(memory_space=pl.ANY)
    return pl.pallas_call(body, name=name, out_shape=outs[0], in_specs=[any_spec], out_specs=any_spec,
                          scratch_shapes=list(scratch))(a)


def _exchange_pair(ga, gb):
    def body(ga_ref, gb_ref, ra_ref, rb_ref, send_sems, recv_sems):
        x, y, c = _place()
        sibling = (x, y, 1 - c)
        slots = [_slot(sibling)] + [_slot((*chip, 1 - c)) for chip in _other_chips(x, y)]
        cps = []
        for t, (src, dst) in enumerate(((ga_ref, ra_ref), (gb_ref, rb_ref))):
            for k, ps in enumerate(slots):
                cps.append(pltpu.make_async_remote_copy(
                    src_ref=src.at[ps], dst_ref=dst.at[k], send_sem=send_sems.at[t, k], recv_sem=recv_sems.at[t, k],
                    device_id=sibling, device_id_type=MESH))
        for cp in cps:
            cp.start()
        for cp in cps:
            cp.wait()

    any_spec = pl.BlockSpec(memory_space=pl.ANY)
    return pl.pallas_call(
        body, name="exchange_pair",
        out_shape=[jax.ShapeDtypeStruct((4,) + ga.shape[1:], ga.dtype), jax.ShapeDtypeStruct((4,) + gb.shape[1:], gb.dtype)],
        in_specs=[any_spec] * 2,
        out_specs=[any_spec] * 2,
        scratch_shapes=[pltpu.SemaphoreType.DMA((2, 4)), pltpu.SemaphoreType.DMA((2, 4))],
    )(ga, gb)


def _pair_add(name, g, r1, slots, tb):
    _, R, Cc = g.shape
    assert R % tb == 0

    def kern(s_ref, a_ref, b_ref, o_ref):
        o_ref[...] = (a_ref[...].astype(F32) + b_ref[...].astype(F32)).astype(o_ref.dtype)

    return pl.pallas_call(
        kern, name=name,
        out_shape=jax.ShapeDtypeStruct((3, R, Cc), BF16),
        grid_spec=pltpu.PrefetchScalarGridSpec(
            num_scalar_prefetch=1, grid=(3, R // tb),
            in_specs=[pl.BlockSpec((1, tb, Cc), lambda j, i, s: (s[j], i, 0)),
                      pl.BlockSpec((1, tb, Cc), lambda j, i, s: (1 + j, i, 0))],
            out_specs=pl.BlockSpec((1, tb, Cc), lambda j, i, s: (j, i, 0))),
        compiler_params=_cp(("arbitrary", "arbitrary")),
    )(slots, g, r1)


def _exchange_ici_steps(in_refs, out_refs, scratch):
    pairs = list(zip(in_refs, out_refs))
    send_sems, recv_sems = scratch
    x, y, c = _place()

    def copies():
        return [pltpu.make_async_remote_copy(
            src_ref=src.at[j], dst_ref=dst.at[j], send_sem=send_sems.at[t, j], recv_sem=recv_sems.at[t, j],
            device_id=(*chip, c), device_id_type=MESH)
            for j, chip in enumerate(_other_chips(x, y)) for t, (src, dst) in enumerate(pairs)]

    def start():
        for cp in copies():
            cp.start()

    def finish():
        for cp in copies():
            cp.wait()

    return start, finish


def _exchange_ici_side(sa, sb):
    return ((sa, sb), (jax.ShapeDtypeStruct(sa.shape, sa.dtype), jax.ShapeDtypeStruct(sb.shape, sb.dtype)),
            (pltpu.SemaphoreType.DMA((2, 3)), pltpu.SemaphoreType.DMA((2, 3))), _exchange_ici_steps)


def _adamw(name, w, m, v, parts, tb):
    R, Cc = w.shape
    assert R % tb == 0
    n_parts = len(parts)

    def kern(*refs):
        w_ref, m_ref, v_ref = refs[:3]
        g = None
        for r_ref, (_, n) in zip(refs[3:3 + n_parts], parts):
            for s in range(n):
                term = r_ref[s].astype(F32)
                g = term if g is None else g + term
        g_out, d_out, m_out, v_out = refs[3 + n_parts:]
        mn = ADAM_B1 * m_ref[...] + (1.0 - ADAM_B1) * g
        vn = ADAM_B2 * v_ref[...] + (1.0 - ADAM_B2) * (g * g)
        m_hat = mn / (1.0 - ADAM_B1 ** ADAM_STEP)
        v_hat = vn / (1.0 - ADAM_B2 ** ADAM_STEP)
        g_out[...] = g
        d_out[...] = -ADAM_LR * (m_hat / (jnp.sqrt(v_hat) + ADAM_EPS) + ADAM_WD * w_ref[...])
        m_out[...] = mn
        v_out[...] = vn

    blk = pl.BlockSpec((tb, Cc), lambda i: (i, 0))
    return pl.pallas_call(
        kern, name=name,
        out_shape=[jax.ShapeDtypeStruct((R, Cc), F32)] * 4,
        grid=(R // tb,),
        in_specs=[blk] * 3 + [pl.BlockSpec((n, tb, Cc), lambda i: (0, i, 0)) for (_, n) in parts],
        out_specs=[blk] * 4,
        compiler_params=_cp(("arbitrary",)),
    )(w, m, v, *[a for (a, _) in parts])


def _pad_cols(a, w):
    return jnp.pad(a, ((0, 0), (0, w - a.shape[1])))


def _pad_rows(a, r):
    return jnp.pad(a, ((0, r - a.shape[0]), (0, 0)))


def _pack_b(pf, pr, wo, w2, a2, rows):
    body = jnp.concatenate([pf, pr, wo.reshape(2048, 256), jnp.concatenate([w2, a2], axis=1)], axis=0)
    return _pad_rows(body, rows)


def kernel(x, norm_gain, w_in, fox_forget_bias, rwkv_shift_mix, rwkv_w0, rwkv_w2, rwkv_a0, rwkv_a2, rwkv_k_k, rwkv_k_a, rwkv_r_k, rwkv_ln_w, rwkv_ln_b, w_proj_fox, w_proj_rwkv, w_out, final_norm_gain, loss_target, m_norm_gain, m_w_in, m_fox_forget_bias, m_rwkv_shift_mix, m_rwkv_w0, m_rwkv_w2, m_rwkv_a0, m_rwkv_a2, m_rwkv_k_k, m_rwkv_k_a, m_rwkv_r_k, m_rwkv_ln_w, m_rwkv_ln_b, m_w_proj_fox, m_w_proj_rwkv, m_w_out, m_final_norm_gain, v_norm_gain, v_w_in, v_fox_forget_bias, v_rwkv_shift_mix, v_rwkv_w0, v_rwkv_w2, v_rwkv_a0, v_rwkv_a2, v_rwkv_k_k, v_rwkv_k_a, v_rwkv_r_k, v_rwkv_ln_w, v_rwkv_ln_b, v_w_proj_fox, v_w_proj_rwkv, v_w_out, v_final_norm_gain):
    T, D = x.shape[1], x.shape[2]
    assert D == 2048 and T % LANE == 0
    NI = w_in.shape[2]
    IN = N_DEV * NI
    RB = 4224
    x2 = x[0]
    lt2 = loss_target[0]
    me = _slot(_place())

    wa = _allgather(w_in[0].astype(BF16), "allgather_w_in")
    packed_own = _pack_b(w_proj_fox[0], w_proj_rwkv[0], w_out[0], rwkv_w2[0], rwkv_a2[0], RB).astype(BF16)
    sections = [(0, 4096, 0), (4104, 4096, 4096), (8392, 4096, 8192), (4096, 8, 12288), (8200, 96, 12544), (8296, 96, 12672)]
    NP = 12800
    pieces, at_col = [], 0
    for lo, width, pad_lo in sections:
        if pad_lo > at_col:
            pieces.append(jnp.zeros((D, pad_lo - at_col), BF16))
        col = lo
        while col < lo + width:
            d = col // NI
            stop = min(lo + width, (d + 1) * NI)
            pieces.append(wa[d, :, col - d * NI:stop - d * NI])
            col = stop
        at_col = pad_lo + width
    pieces.append(jnp.zeros((D, NP - at_col), BF16))
    w_pad = jnp.concatenate(pieces, axis=1)
    F_CB, LORA_CB = 96, 49

    mu = rwkv_shift_mix
    mu_main = mu[:, 0:4096]
    mu_lora = jnp.concatenate([_pad_cols(mu[:, 4096:4192], LANE), _pad_cols(mu[:, 4192:4288], LANE)], axis=1)
    bias_pad = _pad_cols(fox_forget_bias, LANE)
    rk_flat = rwkv_r_k.reshape(1, 1024)
    gf = final_norm_gain.reshape(1, D)

    tb = min(256, T)
    tbh = min(128, T)
    (h,) = _rowcall("rms_fwd", T, tb, [_rows(x2, tb, D), _whole(norm_gain)],
                    lambda i, xv, g: ([_rms_math(xv, g)], []), [(D, BF16)])
    u, wb = _mm(h, w_pad, tm=1024, tn=1280, name="mm_in", side=_allgather_side(packed_own))
    wpf = wb[:, 0:1024, :].transpose(1, 0, 2).reshape(1024, D)
    wpr = wb[:, 1024:2048, :].transpose(1, 0, 2).reshape(1024, D)
    wo = wb[:, 2048:4096, :].reshape(N_DEV * 256, D)
    w2p = _pad_rows(wb[:, 4096:4192, 0:128].transpose(1, 0, 2).reshape(96, 1024).astype(F32), LANE)
    a2p = _pad_rows(wb[:, 4096:4192, 128:256].transpose(1, 0, 2).reshape(96, 1024).astype(F32), LANE)

    c8 = _gates_fwd(u, bias_pad, T, F_CB)
    c3 = c8.reshape(8, 1, T)
    o_raw, o_a, lse = _attn_fwd(u, c3, T)

    def shift_body(i, um, hm, ul, hl, mm_, ml):
        outs = []
        for uv, hv, mv in ((um, hm, mm_), (ul, hl, ml)):
            hv = jnp.where(i == 0, 0.0, hv)
            prev = pltpu.roll(jnp.concatenate([hv, uv], axis=0), 1, 0)[SUB:]
            outs.append(uv + (prev - uv) * mv)
        return outs, []

    def halo_prev(arr, w, cb):
        return (arr, (SUB, w), lambda i: (jnp.maximum(i * (tbh // SUB) - 1, 0), cb))

    xs, xl = _rowcall("rwkv_shift_fwd", T, tbh,
                      [_rows(u, tbh, 4096, 1), halo_prev(u, 4096, 1), _rows(u, tbh, 256, LORA_CB), halo_prev(u, 256, LORA_CB),
                       _whole(mu_main), _whole(mu_lora)],
                      shift_body, [(4096, F32), (256, F32)])

    prep_par = [_whole(rwkv_w0), _whole(rwkv_a0), _whole(rwkv_k_k), _whole(rwkv_k_a), _whole(w2p), _whole(a2p)]
    prep_rows = [_rows(xs, tbh, 1024, 1), _rows(xl, tbh, LANE, 0), _rows(xl, tbh, LANE, 1)]
    lw, kp, an, bb = _rowcall("rwkv_prep_fwd", T, tbh, prep_rows + prep_par,
                              lambda i, *a: (list(_prep_math(*a)), []), [(1024, F32)] * 4)
    y, states = _scan_fwd(xs, lw, kp, an, bb, T)
    post_rows = [_rows(y, tbh, 1024), _rows(xs, tbh, 1024, 0), _rows(kp, tbh, 1024), _rows(xs, tbh, 1024, 2), _rows(xs, tbh, 1024, 3)]
    post_par = [_whole(rwkv_ln_w), _whole(rwkv_ln_b), _whole(rk_flat)]
    (o_b,) = _rowcall("rwkv_post_fwd", T, tbh, post_rows + post_par,
                      lambda i, *a: ([_post_math(*a)], []), [(1024, BF16)])

    pa = _mm(o_a, wpf, name="mm_proj_fox")
    pb = _mm(o_b, wpr, name="mm_proj_rwkv")
    merge_rows = [_rows(u, tb, D, 4), _rows(u, tb, D, 5), _rows(pa, tb, D), _rows(pb, tb, D)]
    (mg,) = _rowcall("merge_fwd", T, tb, merge_rows, lambda i, *a: ([_merge_math(*a)], []), [(D, BF16)])
    mo = _mm(mg, wo, name="mm_out")

    def head_body(i, xv, mov, ltv, g):
        out = xv + mov
        r = lax.rsqrt(jnp.mean(out * out, axis=-1, keepdims=True) + RMS_EPS)
        yn = out * r
        err = yn * g - ltv
        loss = 0.5 * jnp.sum(jnp.sum(err * err, axis=-1, keepdims=True), axis=0, keepdims=True) / D
        dyv = err / D
        dyn = dyv * g
        dout = r * (dyn - yn * jnp.mean(dyn * yn, axis=-1, keepdims=True))
        return [dout], [loss, jnp.sum(dyv * yn, axis=0, keepdims=True)]

    dout, loss_p, dgf_p = _rowcall("loss_head", T, tb, [_rows(x2, tb, D), _rows(mo, tb, D), _rows(lt2, tb, D), _whole(gf)],
                                   head_body, [(D, F32)], [(1, 1), (1, D)])

    dm = _mm(dout, wo, tb=True, name="mm_out_dx")
    dwo = _mm(mg.T, dout, out_dtype=BF16, name="mm_out_dw")

    def merge_bwd_body(i, ga, gb, pav, pbv, dmv):
        _, vjp = jax.vjp(_merge_math, ga, gb, pav, pbv)
        dga, dgb, dpa, dpb = vjp(dmv)
        return [dga, dgb, dpa, dpb], []

    dga, dgb, dpa, dpb = _rowcall("merge_bwd", T, tb, merge_rows + [_rows(dm, tb, D)], merge_bwd_body,
                                  [(D, BF16), (D, BF16), (D, BF16), (D, BF16)])
    doa = _mm(dpa, wpf, tb=True, name="mm_proj_fox_dx")
    dwpf = _mm(o_a.T, dpa, out_dtype=BF16, name="mm_proj_fox_dw")
    dob = _mm(dpb, wpr, tb=True, name="mm_proj_rwkv_dx")
    dwpr = _mm(o_b.T, dpb, out_dtype=BF16, name="mm_proj_rwkv_dw")

    do_b, dza, delta = _attn_delta(u, c3, o_raw, lse, doa, T)
    dq, dk, dv, dc3 = _attn_bwd(u, c3, lse, do_b, delta, T)
    dfl, dbias_p = _gates_bwd(dc3.reshape(8, T), u, bias_pad, T, F_CB)

    def post_bwd_body(i, yv, rv, kpv, vv, zv, lnw, lnb, rkv, dobv):
        _, vjp = jax.vjp(_post_math, yv, rv, kpv, vv, zv, lnw, lnb, rkv)
        dy_, dr_, dkp_, dv_, dz_, dlnw, dlnb, drk = vjp(dobv)
        return [dy_, dr_, dkp_, dv_, dz_], [dlnw, dlnb, drk]

    dy_s, dr_p, dkp_p, dv_p, dzb, dlnw_p, dlnb_p, drk_p = _rowcall(
        "rwkv_post_bwd", T, tbh, post_rows + post_par + [_rows(dob, tbh, 1024)], post_bwd_body,
        [(1024, F32)] * 5, [(1, 1024)] * 3)
    dr_s, dlw, dkp_s, dv_s, dan, dbb = _scan_bwd(xs, lw, kp, an, bb, states, dy_s, T)

    def prep_bwd_body(i, xk, xwd, xad, w0, a0, kkw, kaw, w2v, a2v, dlw_, dkp1, dkp2, dan_, dbb_, dr1, dr2, dv1, dv2, dz_):
        _, vjp = jax.vjp(_prep_math, xk, xwd, xad, w0, a0, kkw, kaw, w2v, a2v)
        dxk, dxwd, dxad, dw0, da0, dkk, dka, dw2, da2 = vjp((dlw_, dkp1 + dkp2, dan_, dbb_))
        return [[dr1 + dr2, dxk, dv1 + dv2, dz_], [dxwd, dxad]], [dw0, da0, dkk, dka, dw2, da2]

    cots = [dlw, dkp_s, dkp_p, dan, dbb, dr_s, dr_p, dv_s, dv_p, dzb]
    dxs, dxl, dw0_p, da0_p, dkk_p, dka_p, dw2_p, da2_p = _rowcall(
        "rwkv_prep_bwd", T, tbh, prep_rows + prep_par + [_rows(c_, tbh, 1024) for c_ in cots], prep_bwd_body,
        [(4096, F32), (256, F32)], [(1, 1024)] * 4 + [(LANE, 1024)] * 2)

    def shift_bwd_body(i, dm_, hm, dl_, hl, um, pm, ul, pl_, mm_, ml):
        last = i == T // tbh - 1
        outs, accs = [], []
        for dv_, hv, uv, pv, mv in ((dm_, hm, um, pm, mm_), (dl_, hl, ul, pl_, ml)):
            hv = jnp.where(last, 0.0, hv)
            nxt = pltpu.roll(jnp.concatenate([dv_, hv], axis=0), tbh + SUB - 1, 0)[:tbh]
            pv = jnp.where(i == 0, 0.0, pv)
            prev = pltpu.roll(jnp.concatenate([pv, uv], axis=0), 1, 0)[SUB:]
            outs.append(dv_ * (1.0 - mv) + nxt * mv)
            accs.append(jnp.sum(dv_ * (prev - uv), axis=0, keepdims=True))
        return outs, accs

    def halo_next(arr, w, cb):
        last_blk = T // SUB - 1
        return (arr, (SUB, w), lambda i: (jnp.minimum((i + 1) * (tbh // SUB), last_blk), cb))

    du_b, du_l, dmu_main_p, dmu_lora_p = _rowcall(
        "rwkv_shift_bwd", T, tbh,
        [_rows(dxs, tbh, 4096), halo_next(dxs, 4096, 0), _rows(dxl, tbh, 256), halo_next(dxl, 256, 0),
         _rows(u, tbh, 4096, 1), halo_prev(u, 4096, 1), _rows(u, tbh, 256, LORA_CB), halo_prev(u, 256, LORA_CB),
         _whole(mu_main), _whole(mu_lora)],
        shift_bwd_body, [(4096, BF16), (256, BF16)], [(1, 4096), (1, 256)])

    du = jnp.concatenate([dq.astype(BF16), dk.astype(BF16), dv.astype(BF16), dza.astype(BF16), du_b, dga, dgb,
                          dfl.astype(BF16), jnp.zeros((T, LANE), BF16), du_l], axis=1)
    dw_pad = _mm(h.T, du, out_dtype=BF16, tm=1024, tn=1280, name="mm_in_dw")

    by_col = sorted(sections)
    blocks = []
    for d in range(N_DEV):
        parts = []
        for lo, width, pad_lo in by_col:
            a, b = max(lo, d * NI), min(lo + width, (d + 1) * NI)
            if a < b:
                parts.append(dw_pad[:, pad_lo + a - lo:pad_lo + b - lo])
        blocks.append(jnp.concatenate(parts, axis=1)[None])
    ga = jnp.concatenate(blocks, axis=0)
    lora_g = jnp.concatenate([dw2_p[:96].reshape(96, N_DEV, 128).transpose(1, 0, 2),
                              da2_p[:96].reshape(96, N_DEV, 128).transpose(1, 0, 2)], axis=2).astype(BF16)
    gb = jnp.concatenate([dwpf.reshape(1024, N_DEV, 256).transpose(1, 0, 2),
                          dwpr.reshape(1024, N_DEV, 256).transpose(1, 0, 2),
                          dwo.reshape(N_DEV, 2048, 256), lora_g, jnp.zeros((N_DEV, RB - 4192, 256), BF16)], axis=1)

    xx, yy, cc = _place()
    chip_slots = jnp.stack([_slot((*chip, cc)) for chip in _other_chips(xx, yy)]).astype(jnp.int32)
    r1a, r1b = _exchange_pair(ga, gb)
    sa = _pair_add("pair_add_w_in", ga, r1a, chip_slots, 256)
    sb = _pair_add("pair_add_packed", gb, r1b, chip_slots, 384)
    ga_own = lax.dynamic_index_in_dim(ga, me, 0, keepdims=True)
    gb_own = lax.dynamic_index_in_dim(gb, me, 0, keepdims=True)
    dh, r2a, r2b = _mm(du, w_pad, tb=True, tm=1024, tn=1024, tk=NP // 10, name="mm_in_dx", side=_exchange_ici_side(sa, sb))

    def rms_bwd_body(i, xv, g, dhv, doutv):
        _, vjp = jax.vjp(_rms_math, xv, g)
        dx_, dg_ = vjp(dhv)
        return [dx_ + doutv], [dg_]

    grad_x2, dng_p = _rowcall("rms_bwd", T, tb, [_rows(x2, tb, D), _whole(norm_gain), _rows(dh, tb, D), _rows(dout, tb, D)],
                              rms_bwd_body, [(D, F32)], [(1, D)])

    dmu = jnp.concatenate([dmu_main_p, dmu_lora_p[:, 0:96], dmu_lora_p[:, 128:224]], axis=1)
    small_parts = [dng_p, dbias_p[:, 0:8], dmu, dw0_p, da0_p, dkk_p, dka_p, drk_p, dlnw_p, dlnb_p, dgf_p, loss_p]
    SR = 128
    small = _pad_cols(jnp.concatenate(small_parts, axis=1), SR * LANE).reshape(SR, LANE)
    rs = _allgather(small, "allgather_small")

    g_in, d_in, m_in, v_in = _adamw("adamw_w_in", w_in[0], m_w_in[0], v_w_in[0], [(ga_own, 1), (r1a, 1), (r2a, 3)], min(128, D))
    pk = lambda pf, pr, wo_, w2_, a2_: _pack_b(pf[0], pr[0], wo_[0], w2_[0], a2_[0], RB)
    outs_b = _adamw("adamw_packed", pk(w_proj_fox, w_proj_rwkv, w_out, rwkv_w2, rwkv_a2),
                    pk(m_w_proj_fox, m_w_proj_rwkv, m_w_out, m_rwkv_w2, m_rwkv_a2),
                    pk(v_w_proj_fox, v_w_proj_rwkv, v_w_out, v_rwkv_w2, v_rwkv_a2), [(gb_own, 1), (r1b, 1), (r2b, 3)], 384)

    def pack_small(ng, fb, sm, w0, a0, kk_, ka_, rk_, lnw, lnb, fg):
        parts = [ng, fb, sm, w0, a0, kk_, ka_, rk_.reshape(1, 1024), lnw, lnb, fg.reshape(1, D), jnp.zeros((1, 1), F32)]
        return _pad_cols(jnp.concatenate(parts, axis=1), SR * LANE).reshape(SR, LANE)

    outs_s = _adamw("adamw_small",
                    pack_small(norm_gain, fox_forget_bias, rwkv_shift_mix, rwkv_w0, rwkv_a0, rwkv_k_k, rwkv_k_a, rwkv_r_k,
                               rwkv_ln_w, rwkv_ln_b, final_norm_gain),
                    pack_small(m_norm_gain, m_fox_forget_bias, m_rwkv_shift_mix, m_rwkv_w0, m_rwkv_a0, m_rwkv_k_k, m_rwkv_k_a,
                               m_rwkv_r_k, m_rwkv_ln_w, m_rwkv_ln_b, m_final_norm_gain),
                    pack_small(v_norm_gain, v_fox_forget_bias, v_rwkv_shift_mix, v_rwkv_w0, v_rwkv_a0, v_rwkv_k_k, v_rwkv_k_a,
                               v_rwkv_r_k, v_rwkv_ln_w, v_rwkv_ln_b, v_final_norm_gain),
                    [(rs, N_DEV)], SR)

    def unpack_b(pkd):
        return dict(w_proj_fox=pkd[0:1024][None], w_proj_rwkv=pkd[1024:2048][None], w_out=pkd[2048:4096].reshape(1, 256, D),
                    rwkv_w2=pkd[4096:4192, 0:128][None], rwkv_a2=pkd[4096:4192, 128:256][None])

    def unpack_s(pkd):
        flat = pkd.reshape(1, SR * LANE)
        names = [("norm_gain", D), ("fox_forget_bias", 8), ("rwkv_shift_mix", 4288), ("rwkv_w0", 1024), ("rwkv_a0", 1024),
                 ("rwkv_k_k", 1024), ("rwkv_k_a", 1024), ("rwkv_r_k", 1024), ("rwkv_ln_w", 1024), ("rwkv_ln_b", 1024),
                 ("final_norm_gain", D), ("loss", 1)]
        out, off = {}, 0
        for nm, n in names:
            out[nm] = flat[:, off:off + n]
            off += n
        out["rwkv_r_k"] = out["rwkv_r_k"].reshape(1, 16, 64)
        out["final_norm_gain"] = out["final_norm_gain"].reshape(D)
        return out

    order = ["norm_gain", "w_in", "fox_forget_bias", "rwkv_shift_mix", "rwkv_w0", "rwkv_w2", "rwkv_a0", "rwkv_a2", "rwkv_k_k",
             "rwkv_k_a", "rwkv_r_k", "rwkv_ln_w", "rwkv_ln_b", "w_proj_fox", "w_proj_rwkv", "w_out", "final_norm_gain"]
    result = []
    loss = None
    for kind, big in enumerate((g_in, d_in, m_in, v_in)):
        d = {**unpack_b(outs_b[kind]), **unpack_s(outs_s[kind]), "w_in": big[None]}
        if kind == 0:
            loss = d["loss"].reshape(())
        result += [d[n] for n in order]
    return (loss, grad_x2[None], *result)
```

```python
import functools

import jax
import jax.numpy as jnp
from jax import lax
from jax.experimental import pallas as pl
from jax.experimental.pallas import tpu as pltpu

F32 = jnp.float32
BF16 = jnp.bfloat16
HI = lax.Precision.HIGHEST
MESH = pl.DeviceIdType.MESH

FOX_HD = 128
RW_HD = 64
RMS_EPS = 1e-6
GN_EPS = 64e-5
L2_EPS = 1e-12
ADAM_LR = 0.001
ADAM_B1 = 0.9
ADAM_B2 = 0.999
ADAM_EPS = 1e-08
ADAM_WD = 0.01
ADAM_STEP = 10

LANE = 128
SUB = 8
VMEM_LIMIT = 56 * 1024 * 1024
N_DEV = 8
CHUNK = 64
SCAN_GROUP = 4
SCAN_PAIRS = 2
PS = None
NEG = -1e30


def _scan_shape(T):
    c = min(CHUNK, T)
    return c, min(SCAN_GROUP, T // c)


def _cp(sem=None):
    return pltpu.CompilerParams(dimension_semantics=sem, vmem_limit_bytes=VMEM_LIMIT)


def _sigmoid(x):
    return jax.nn.sigmoid(x)


def _softplus(x):
    return jnp.maximum(x, 0.0) + jnp.log(1.0 + jnp.exp(-jnp.abs(x)))


def _nn(a, b, prec=None):
    return lax.dot_general(a, b, (((1,), (0,)), ((), ())), precision=prec, preferred_element_type=F32)


def _nt(a, b, prec=None):
    return lax.dot_general(a, b, (((1,), (1,)), ((), ())), precision=prec, preferred_element_type=F32)


def _tn(a, b, prec=None):
    return lax.dot_general(a, b, (((0,), (0,)), ((), ())), precision=prec, preferred_element_type=F32)


def _iota2(shape, dim):
    return lax.broadcasted_iota(jnp.int32, shape, dim)


def _seg_sum(x):
    r = _iota2((LANE, LANE), 0) // RW_HD
    c = _iota2((LANE, LANE), 1) // RW_HD
    bd = (r == c).astype(F32)
    parts = [_nn(x[:, j * LANE:(j + 1) * LANE], bd, HI) for j in range(x.shape[1] // LANE)]
    return parts[0] if len(parts) == 1 else jnp.concatenate(parts, axis=1)


def _mm(a, b, *, tb=False, out_dtype=F32, tm=512, tn=512, tk=None, name, side=None):
    M, K = a.shape
    N = b.shape[0] if tb else b.shape[1]
    tm, tn = min(tm, M), min(tn, N)
    tk = K if tk is None else tk
    nk = K // tk
    assert M % tm == 0 and N % tn == 0 and K % tk == 0
    gi, gj = M // tm, N // tn
    b_spec = pl.BlockSpec((tn, tk), lambda i, j, k: (j, k)) if tb else pl.BlockSpec((tk, tn), lambda i, j, k: (k, j))
    side_ins, side_outs, side_scr, side_make = side if side is not None else ((), (), (), None)
    n_si, n_so = len(side_ins), len(side_outs)
    n_acc = 0 if nk == 1 else 1

    def body(*refs):
        a_ref, b_ref = refs[:2]
        o_ref = refs[2 + n_si]
        scr = refs[3 + n_si + n_so:]
        k = pl.program_id(2)
        if side_make is not None:
            start, finish = side_make(refs[2:2 + n_si], refs[3 + n_si:3 + n_si + n_so], scr[n_acc:])
            first = jnp.logical_and(jnp.logical_and(pl.program_id(0) == 0, pl.program_id(1) == 0), k == 0)
            last = jnp.logical_and(jnp.logical_and(pl.program_id(0) == gi - 1, pl.program_id(1) == gj - 1), k == nk - 1)
            pl.when(first)(start)
        av = a_ref[...].astype(BF16)
        bv = b_ref[...].astype(BF16)
        p = _nt(av, bv) if tb else _nn(av, bv)
        if nk == 1:
            o_ref[...] = p.astype(out_dtype)
        else:
            acc_ref = scr[0]

            @pl.when(k == 0)
            def _():
                acc_ref[...] = p

            @pl.when(k > 0)
            def _():
                acc_ref[...] += p

            @pl.when(k == nk - 1)
            def _():
                o_ref[...] = acc_ref[...].astype(out_dtype)
        if side_make is not None:
            pl.when(last)(finish)

    any_spec = pl.BlockSpec(memory_space=pl.ANY)
    res = pl.pallas_call(
        body, name=name,
        out_shape=[jax.ShapeDtypeStruct((M, N), out_dtype)] + list(side_outs),
        grid=(gi, gj, nk),
        in_specs=[pl.BlockSpec((tm, tk), lambda i, j, k: (i, k)), b_spec] + [any_spec] * n_si,
        out_specs=[pl.BlockSpec((tm, tn), lambda i, j, k: (i, j))] + [any_spec] * n_so,
        scratch_shapes=([] if nk == 1 else [pltpu.VMEM((tm, tn), F32)]) + list(side_scr),
        compiler_params=_cp(("arbitrary",) * 3 if side is not None else ("parallel", "parallel", "arbitrary")),
    )(a, b, *side_ins)
    return res if side is not None else res[0]


def _rows(arr, tb, w, cb=0):
    return (arr, (tb, w), lambda i: (i, cb))


def _whole(arr):
    nd = arr.ndim
    return (arr, arr.shape, lambda i: (0,) * nd)


def _rowcall(name, T, tb, ins, body, outs, accs=()):
    n_in, n_out = len(ins), len(outs)

    def kern(*refs):
        i = pl.program_id(0)
        vals = [r[...] for r in refs[:n_in]]
        ro, ao = body(i, *vals)
        for r, v in zip(refs[n_in:n_in + n_out], ro):
            if isinstance(v, (list, tuple)):
                off = 0
                for piece in v:
                    w = piece.shape[1]
                    r[:, off:off + w] = piece.astype(r.dtype)
                    off += w
            else:
                r[...] = v.astype(r.dtype)
        if accs:
            acc_refs = refs[n_in + n_out:]

            @pl.when(i == 0)
            def _():
                for r in acc_refs:
                    r[...] = jnp.zeros(r.shape, F32)

            for r, v in zip(acc_refs, ao):
                r[...] += v

    out_shape = [jax.ShapeDtypeStruct((T, w), dt) for (w, dt) in outs] + [jax.ShapeDtypeStruct(s, F32) for s in accs]
    out_specs = [pl.BlockSpec((tb, w), lambda i: (i, 0)) for (w, dt) in outs] + [pl.BlockSpec(s, lambda i: (0, 0)) for s in accs]
    res = pl.pallas_call(
        kern, name=name,
        out_shape=out_shape,
        grid=(T // tb,),
        in_specs=[pl.BlockSpec(bs, im) for (_, bs, im) in ins],
        out_specs=out_specs,
        compiler_params=_cp(("arbitrary",)),
    )(*[a for (a, _, _) in ins])
    return res


def _rms_math(x, g):
    r = lax.rsqrt(jnp.mean(x * x, axis=-1, keepdims=True) + RMS_EPS)
    return x * r * g


def _merge_math(ga, gb, pa, pb):
    return _sigmoid(ga) * pa + _sigmoid(gb) * pb


def _prep_math(xk, xwd, xad, w0, a0, kk_w, ka_w, w2p, a2p):
    z = w0 + _nn(jnp.tanh(xwd), w2p, HI)
    w = -_softplus(-z) - 0.5
    lw = -jnp.exp(w)
    ag = _sigmoid(a0 + _nn(xad, a2p, HI))
    p = xk * kk_w
    n = jnp.maximum(jnp.sqrt(_seg_sum(p * p)), L2_EPS)
    kk = p / n
    kp = xk * (1.0 + (ag - 1.0) * ka_w)
    return lw, kp, -kk, kk * ag


def _post_math(y, r, kp, v, z, lnw, lnb, rk):
    inv = 1.0 / RW_HD
    mu = _seg_sum(y) * inv
    d = y - mu
    var = _seg_sum(d * d) * inv
    yn = d * lax.rsqrt(var + GN_EPS) * lnw + lnb
    bonus = _seg_sum(r * kp * rk) * v
    return (yn + bonus) * (z * _sigmoid(z))


def _scan_group(s0s, *flat):
    P = len(s0s)
    G = len(flat) // (6 * P)
    ch = [flat[6 * i:6 * i + 6] for i in range(P * G)]
    C = ch[0][0].shape[0]
    C2 = 2 * C
    cat = jnp.concatenate
    m0 = _iota2((1, LANE), 1) < RW_HD
    mask0 = m0.astype(F32)
    mask1 = 1.0 - mask0
    r2 = _iota2((C2, C2), 0)
    c2 = _iota2((C2, C2), 1)
    dist = r2 - c2
    in_head = dist <= r2 % C
    eye = (r2 == c2).astype(F32)
    lower = (_iota2((C, C), 0) >= _iota2((C, C), 1)).astype(F32)
    bd = (_iota2((LANE, LANE), 0) // RW_HD) == (_iota2((LANE, LANE), 1) // RW_HD)

    def tri(m, strict):
        return jnp.where(dist > 0 if strict else dist >= 0, jnp.where(in_head, m, 0.0), 0.0)

    def sel(z):
        return jnp.where(m0, z[:C], z[C:])

    gs = [_nn(lower, c[1], HI) for c in ch]
    pre = []
    for (r, lw, k, v, a, b), g in zip(ch, gs):
        g_end = jnp.sum(lw, axis=0, keepdims=True)
        en = jnp.exp(-g)
        ec = jnp.exp(g_end - g)
        pre.append(dict(at=a * jnp.exp(g - lw), rt=r * jnp.exp(g), bt=b * en, kt=k * en, bh=b * ec, kh=k * ec,
                        dec=jnp.exp(g_end), v=v))
    grams = [_nt(cat([p["at"] * mask0, p["at"] * mask1, p["rt"] * mask0, p["rt"] * mask1], axis=0),
                 cat([p["bt"], p["bt"], p["kt"], p["kt"]], axis=0), PS) for p in pre]
    mab = [tri(gm[:C2, :C2], True) for gm in grams]
    mp = [_nn(m, m, PS) for m in mab]
    tinv = [eye + m for m in mab]
    n = 2
    while n < C:
        last = 2 * n >= C
        for i in range(P * G):
            if last:
                tinv[i] = tinv[i] + _nn(mp[i], tinv[i], PS)
            else:
                z = _nn(mp[i], cat([mp[i], tinv[i]], axis=1), PS)
                mp[i], tinv[i] = z[:, :C2], tinv[i] + z[:, C2:]
        n *= 2
    xv = [sel(_nn(tri(gm[:C2, C2:], True), cat([p["v"], p["v"]], axis=0), PS)) for gm, p in zip(grams, pre)]
    ys, s = [None] * (P * G), list(s0s)
    for i in range(G):
        for q in range(P):
            n = q * G + i
            p, gm = pre[n], grams[n]
            sx = _nt(cat([p["at"], p["rt"]], axis=0), s[q], PS)
            x = sx[:C] + xv[n]
            u = sel(_nn(tinv[n], cat([x, x], axis=0), PS))
            v = p["v"]
            ys[n] = sx[C:] + sel(_nn(cat([tri(gm[C2:, :C2], False), tri(gm[C2:, C2:], False)], axis=1),
                                     cat([u, u, v, v], axis=0), PS))
            s[q] = s[q] * p["dec"] + jnp.where(bd, _tn(cat([u, v], axis=0), cat([p["bh"], p["kh"]], axis=0), PS), 0.0)
    return tuple(ys), tuple(s)


def _scan_fwd(xs, lw, kp, an, bb, T):
    C, G = _scan_shape(T)
    P = SCAN_PAIRS
    nc = T // (C * G)
    npair = 1024 // LANE

    def kern(r_ref, lw_ref, k_ref, v_ref, a_ref, b_ref, y_ref, st_ref, s_scr):
        n = pl.program_id(1)

        @pl.when(n == 0)
        def _():
            s_scr[...] = jnp.zeros(s_scr.shape, F32)

        st_ref[0] = s_scr[...]
        ins = (r_ref, lw_ref, k_ref, v_ref, a_ref, b_ref)
        ys, s1 = _scan_group(tuple(s_scr[q] for q in range(P)),
                             *[ref[i * C:(i + 1) * C, q * LANE:(q + 1) * LANE] for q in range(P) for i in range(G) for ref in ins])
        for q in range(P):
            for i in range(G):
                y_ref[i * C:(i + 1) * C, q * LANE:(q + 1) * LANE] = ys[q * G + i]
            s_scr[q] = s1[q]

    def col(off):
        return pl.BlockSpec((C * G, P * LANE), lambda p, n: (n, off // P + p))

    return pl.pallas_call(
        kern, name="rwkv_scan_fwd",
        out_shape=[jax.ShapeDtypeStruct((T, 1024), F32), jax.ShapeDtypeStruct((nc, npair, LANE, LANE), F32)],
        grid=(npair // P, nc),
        in_specs=[col(0), col(0), col(0), col(16), col(0), col(0)],
        out_specs=[col(0), pl.BlockSpec((1, P, LANE, LANE), lambda p, n: (n, p, 0, 0))],
        scratch_shapes=[pltpu.VMEM((P, LANE, LANE), F32)],
        compiler_params=_cp(("parallel", "arbitrary")),
    )(xs, lw, kp, xs, an, bb)


def _scan_bwd(xs, lw, kp, an, bb, states, dy, T):
    C, G = _scan_shape(T)
    P = SCAN_PAIRS
    nc = T // (C * G)
    npair = 1024 // LANE

    def kern(r_ref, lw_ref, k_ref, v_ref, a_ref, b_ref, st_ref, dy_ref, dr_ref, dlw_ref, dk_ref, dv_ref, da_ref, db_ref, ds_scr):
        n = pl.program_id(1)

        @pl.when(n == 0)
        def _():
            ds_scr[...] = jnp.zeros(ds_scr.shape, F32)

        ins = (r_ref, lw_ref, k_ref, v_ref, a_ref, b_ref)
        units = [(q, i) for q in range(P) for i in range(G)]
        _, vjp = jax.vjp(_scan_group, tuple(st_ref[0, q] for q in range(P)),
                         *[ref[i * C:(i + 1) * C, q * LANE:(q + 1) * LANE] for q, i in units for ref in ins])
        grads = vjp((tuple(dy_ref[i * C:(i + 1) * C, q * LANE:(q + 1) * LANE] for q, i in units),
                     tuple(ds_scr[q] for q in range(P))))
        for q in range(P):
            ds_scr[q] = grads[0][q]
        outs = (dr_ref, dlw_ref, dk_ref, dv_ref, da_ref, db_ref)
        for n_, (q, i) in enumerate(units):
            for t, ref in enumerate(outs):
                ref[i * C:(i + 1) * C, q * LANE:(q + 1) * LANE] = grads[1 + 6 * n_ + t]

    def col(off):
        return pl.BlockSpec((C * G, P * LANE), lambda p, n: (nc - 1 - n, off // P + p))

    return pl.pallas_call(
        kern, name="rwkv_scan_bwd",
        out_shape=[jax.ShapeDtypeStruct((T, 1024), F32)] * 6,
        grid=(npair // P, nc),
        in_specs=[col(0), col(0), col(0), col(16), col(0), col(0),
                  pl.BlockSpec((1, P, LANE, LANE), lambda p, n: (nc - 1 - n, p, 0, 0)), col(0)],
        out_specs=[col(0)] * 6,
        scratch_shapes=[pltpu.VMEM((P, LANE, LANE), F32)],
        compiler_params=_cp(("parallel", "arbitrary")),
    )(xs, lw, kp, xs, an, bb, states, dy)


def _gates_fwd(u, bias_pad, T, f_cb):
    nb = T // LANE

    def kern(f_ref, b_ref, c_ref):
        x = f_ref[...] + b_ref[...]
        lf = jnp.minimum(x, 0.0) - jnp.log(1.0 + jnp.exp(-jnp.abs(x)))
        lft = lf.T
        ut = (_iota2((LANE, LANE), 0) <= _iota2((LANE, LANE), 1)).astype(F32)
        carry = jnp.zeros((LANE, 1), F32)
        for blk in range(nb):
            seg = lft[:, blk * LANE:(blk + 1) * LANE]
            cs = _nn(seg, ut, HI) + carry
            c_ref[:, blk * LANE:(blk + 1) * LANE] = cs[:SUB, :]
            carry = carry + jnp.sum(seg, axis=1, keepdims=True)

    return pl.pallas_call(
        kern, name="fox_gates_fwd",
        out_shape=jax.ShapeDtypeStruct((SUB, T), F32),
        grid=(1,),
        in_specs=[pl.BlockSpec((T, LANE), lambda i: (0, f_cb)), pl.BlockSpec((1, LANE), lambda i: (0, 0))],
        out_specs=pl.BlockSpec((SUB, T), lambda i: (0, 0)),
        compiler_params=_cp(("arbitrary",)),
    )(u, bias_pad)


def _gates_bwd(dc, u, bias_pad, T, f_cb):
    nb = T // LANE

    def kern(dc_ref, f_ref, b_ref, dfl_ref, db_ref):
        dcv = jnp.concatenate([dc_ref[...], jnp.zeros((LANE - SUB, T), F32)], axis=0)
        lt = (_iota2((LANE, LANE), 0) >= _iota2((LANE, LANE), 1)).astype(F32)
        carry = jnp.zeros((LANE, 1), F32)
        pieces = [None] * nb
        for blk in range(nb - 1, -1, -1):
            seg = dcv[:, blk * LANE:(blk + 1) * LANE]
            pieces[blk] = _nn(seg, lt, HI) + carry
            carry = carry + jnp.sum(seg, axis=1, keepdims=True)
        dlf = (pieces[0] if nb == 1 else jnp.concatenate(pieces, axis=1)).T
        x = f_ref[...] + b_ref[...]
        dfl = dlf * _sigmoid(-x)
        dfl_ref[...] = dfl
        db_ref[...] = jnp.sum(dfl, axis=0, keepdims=True)

    return pl.pallas_call(
        kern, name="fox_gates_bwd",
        out_shape=[jax.ShapeDtypeStruct((T, LANE), F32), jax.ShapeDtypeStruct((1, LANE), F32)],
        grid=(1,),
        in_specs=[pl.BlockSpec((SUB, T), lambda i: (0, 0)), pl.BlockSpec((T, LANE), lambda i: (0, f_cb)),
                  pl.BlockSpec((1, LANE), lambda i: (0, 0))],
        out_specs=[pl.BlockSpec((T, LANE), lambda i: (0, 0)), pl.BlockSpec((1, LANE), lambda i: (0, 0))],
        compiler_params=_cp(("arbitrary",)),
    )(dc, u, bias_pad)


ATTN_HEADS = 4


def _attn_block(T):
    return 256 if T % 256 == 0 and T >= 512 else 128


def _attn_fwd(u, c3, T):
    H, HP = 8, ATTN_HEADS
    bq = _attn_block(T)
    nq = T // bq
    scale = FOX_HD ** -0.5
    lanes = [slice(h * LANE, (h + 1) * LANE) for h in range(HP)]

    def kern(q_ref, k_ref, v_ref, z_ref, cq_ref, ck_ref, o_ref, oa_ref, lse_ref):
        i = pl.program_id(1)
        q = [(q_ref[:, ln] * scale).astype(BF16) for ln in lanes]
        c0 = [cq_ref[h][:, 0:1] for h in range(HP)]

        def step(j, carry, diagonal=False):
            off = pl.multiple_of(j * bq, bq)
            s = [_nt(q[h], k_ref[pl.ds(off, bq), lanes[h]].astype(BF16)) + (c0[h] - ck_ref[h, :, pl.ds(off, bq)])
                 for h in range(HP)]
            ps, out = [], []
            for h in range(HP):
                m, l, acc = carry[h]
                sh = s[h]
                if diagonal:
                    sh = jnp.where(_iota2((bq, bq), 1) <= _iota2((bq, bq), 0), sh, NEG)
                m_new = jnp.maximum(m, jnp.max(sh, axis=1, keepdims=True))
                p = jnp.exp(sh - m_new)
                alpha = jnp.exp(m - m_new)
                ps.append(p.astype(BF16))
                out.append((m_new, alpha * l + jnp.sum(p, axis=1, keepdims=True), alpha * acc))
            return tuple((m, l, acc + _nn(ps[h], v_ref[pl.ds(off, bq), lanes[h]].astype(BF16)))
                         for h, (m, l, acc) in enumerate(out))

        init = tuple((jnp.full((bq, 1), NEG, F32), jnp.zeros((bq, 1), F32), jnp.zeros((bq, FOX_HD), F32)) for _ in range(HP))
        res = step(i, lax.fori_loop(0, i, step, init), diagonal=True)
        for h, (m, l, acc) in enumerate(res):
            o = acc / l
            z = z_ref[:, lanes[h]]
            o_ref[:, lanes[h]] = o
            oa_ref[:, lanes[h]] = (o * z * _sigmoid(z)).astype(BF16)
            lse_ref[h] = m + jnp.log(l)

    W = HP * LANE
    return pl.pallas_call(
        kern, name="fox_attn_fwd",
        out_shape=[jax.ShapeDtypeStruct((T, 1024), F32), jax.ShapeDtypeStruct((T, 1024), BF16),
                   jax.ShapeDtypeStruct((H, T, 1), F32)],
        grid=(H // HP, nq),
        in_specs=[pl.BlockSpec((bq, W), lambda g, i: (i, g)),
                  pl.BlockSpec((T, W), lambda g, i: (0, 8 // HP + g)),
                  pl.BlockSpec((T, W), lambda g, i: (0, 16 // HP + g)),
                  pl.BlockSpec((bq, W), lambda g, i: (i, 24 // HP + g)),
                  pl.BlockSpec((HP, 1, bq), lambda g, i: (g, 0, i)),
                  pl.BlockSpec((HP, 1, T), lambda g, i: (g, 0, 0))],
        out_specs=[pl.BlockSpec((bq, W), lambda g, i: (i, g)),
                   pl.BlockSpec((bq, W), lambda g, i: (i, g)),
                   pl.BlockSpec((HP, bq, 1), lambda g, i: (g, i, 0))],
        compiler_params=_cp(("parallel", "arbitrary")),
    )(u, u, u, u, c3, c3)


def _attn_probs(s, lse_i, diagonal):
    if not diagonal:
        return jnp.exp(s - lse_i)
    keep = _iota2(s.shape, 1) <= _iota2(s.shape, 0)
    return jnp.where(keep, jnp.exp(jnp.where(keep, s, NEG) - lse_i), 0.0)


def _attn_delta(u, c3, o, lse, doa, T):
    H, HP = 8, ATTN_HEADS
    bq = _attn_block(T)
    nq = T // bq
    scale = FOX_HD ** -0.5
    lanes = [slice(h * LANE, (h + 1) * LANE) for h in range(HP)]

    def kern(q_ref, k_ref, v_ref, z_ref, c_ref, o_ref, lse_ref, doa_ref, do_ref, dz_ref, dl_ref):
        i = pl.program_id(1)
        z = z_ref[...]
        sg = _sigmoid(z)
        dov = doa_ref[...]
        do_all = (dov * z * sg).astype(BF16)
        do_ref[...] = do_all
        dz_ref[...] = dov * o_ref[...] * (sg * (1.0 + z * (1.0 - sg)))
        qs = [(q_ref[:, ln] * scale).astype(BF16) for ln in lanes]
        dob = [do_all[:, ln] for ln in lanes]
        ioff = pl.multiple_of(i * bq, bq)
        c0 = [c_ref[h, :, pl.ds(ioff, bq)][:, 0:1] for h in range(HP)]
        lse_i = [lse_ref[h] for h in range(HP)]

        def step(j, acc, diagonal=False):
            off = pl.multiple_of(j * bq, bq)
            s = [_nt(qs[h], k_ref[pl.ds(off, bq), lanes[h]].astype(BF16)) + (c0[h] - c_ref[h, :, pl.ds(off, bq)])
                 for h in range(HP)]
            dp = [_nt(dob[h], v_ref[pl.ds(off, bq), lanes[h]].astype(BF16)) for h in range(HP)]
            return tuple(acc[h] + jnp.sum(_attn_probs(s[h], lse_i[h], diagonal) * dp[h], axis=1, keepdims=True)
                         for h in range(HP))

        init = tuple(jnp.zeros((bq, 1), F32) for _ in range(HP))
        res = step(i, lax.fori_loop(0, i, step, init), diagonal=True)
        for h in range(HP):
            dl_ref[h] = res[h]

    W = HP * LANE
    full = lambda cb: pl.BlockSpec((T, W), lambda g, i: (0, cb // HP + g))
    blk = lambda cb: pl.BlockSpec((bq, W), lambda g, i: (i, cb // HP + g))
    return pl.pallas_call(
        kern, name="fox_attn_delta",
        out_shape=[jax.ShapeDtypeStruct((T, 1024), BF16), jax.ShapeDtypeStruct((T, 1024), F32), jax.ShapeDtypeStruct((H, T, 1), F32)],
        grid=(H // HP, nq),
        in_specs=[blk(0), full(8), full(16), blk(24),
                  pl.BlockSpec((HP, 1, T), lambda g, i: (g, 0, 0)),
                  blk(0),
                  pl.BlockSpec((HP, bq, 1), lambda g, i: (g, i, 0)),
                  blk(0)],
        out_specs=[blk(0), blk(0), pl.BlockSpec((HP, bq, 1), lambda g, i: (g, i, 0))],
        compiler_params=_cp(("parallel", "arbitrary")),
    )(u, u, u, u, c3, o, lse, doa)


def _attn_bwd(u, c3, lse, do, delta, T):
    H, HP = 8, ATTN_HEADS
    bq = _attn_block(T)
    nq = T // bq
    scale = FOX_HD ** -0.5
    lanes = [slice(h * LANE, (h + 1) * LANE) for h in range(HP)]

    def kern(q_ref, k_ref, v_ref, c_ref, lse_ref, do_ref, dl_ref, dq_ref, dk_ref, dv_ref, dc_ref):
        j = pl.program_id(1)

        @pl.when(j == 0)
        def _():
            dq_ref[...] = jnp.zeros(dq_ref.shape, F32)

        kj = [k_ref[:, ln].astype(BF16) for ln in lanes]
        vj = [v_ref[:, ln].astype(BF16) for ln in lanes]
        joff = pl.multiple_of(j * bq, bq)
        ck = [c_ref[h, :, pl.ds(joff, bq)] for h in range(HP)]

        def step(i, carry, diagonal=False):
            off = pl.multiple_of(i * bq, bq)
            qs = [(q_ref[pl.ds(off, bq), ln] * scale).astype(BF16) for ln in lanes]
            dob = [do_ref[pl.ds(off, bq), ln] for ln in lanes]
            s = [_nt(qs[h], kj[h]) + (c_ref[h, :, pl.ds(off, bq)][:, 0:1] - ck[h]) for h in range(HP)]
            dp = [_nt(dob[h], vj[h]) for h in range(HP)]
            pb, dsb, dcs = [], [], []
            for h in range(HP):
                p = _attn_probs(s[h], lse_ref[h, pl.ds(off, bq), :], diagonal)
                ds = p * (dp[h] - dl_ref[h, pl.ds(off, bq), :])
                pb.append(p.astype(BF16))
                dsb.append(ds.astype(BF16))
                dcs.append(jnp.sum(ds, axis=0, keepdims=True))
            out = []
            for h, (dk, dv, dc) in enumerate(carry):
                dq_ref[pl.ds(off, bq), lanes[h]] += _nn(dsb[h], kj[h]) * scale
                out.append((dk + _tn(dsb[h], qs[h]), dv + _tn(pb[h], dob[h]), dc - dcs[h]))
            return tuple(out)

        init = tuple((jnp.zeros((bq, FOX_HD), F32), jnp.zeros((bq, FOX_HD), F32), jnp.zeros((1, bq), F32)) for _ in range(HP))
        res = lax.fori_loop(j + 1, nq, step, step(j, init, diagonal=True))
        for h, (dk, dv, dc) in enumerate(res):
            dk_ref[:, lanes[h]] = dk
            dv_ref[:, lanes[h]] = dv
            dc_ref[h] = dc

    W = HP * LANE
    full = lambda cb: pl.BlockSpec((T, W), lambda g, j: (0, cb // HP + g))
    blk = lambda cb: pl.BlockSpec((bq, W), lambda g, j: (j, cb // HP + g))
    col1 = pl.BlockSpec((HP, T, 1), lambda g, j: (g, 0, 0))
    return pl.pallas_call(
        kern, name="fox_attn_bwd",
        out_shape=[jax.ShapeDtypeStruct((T, 1024), F32)] * 3 + [jax.ShapeDtypeStruct((H, 1, T), F32)],
        grid=(H // HP, nq),
        in_specs=[full(0), blk(8), blk(16), pl.BlockSpec((HP, 1, T), lambda g, j: (g, 0, 0)), col1, full(0), col1],
        out_specs=[full(0), blk(0), blk(0), pl.BlockSpec((HP, 1, bq), lambda g, j: (g, 0, j))],
        compiler_params=_cp(("parallel", "arbitrary")),
    )(u, u, u, c3, lse, do, delta)


def _place():
    return lax.axis_index("x"), lax.axis_index("y"), lax.axis_index("c")


def _slot(p):
    return 4 * p[0] + 2 * p[1] + p[2]


def _other_chips(x, y):
    return [(1 - x, y), (x, 1 - y), (1 - x, 1 - y)]


def _allgather_steps(in_refs, out_refs, scratch):
    (src,), (dst,) = in_refs, out_refs
    send_sems, recv_sems, local_sem = scratch
    x, y, c = _place()
    me, sibling = (x, y, c), (x, y, 1 - c)
    chips = _other_chips(x, y)

    def copy(k, block, to, from_input=False):
        d = dst.at[_slot(block)]
        return pltpu.make_async_remote_copy(
            src_ref=src if from_input else d, dst_ref=d, send_sem=send_sems.at[k], recv_sem=recv_sems.at[k],
            device_id=to, device_id_type=MESH)

    def first_copies():
        return [copy(0, me, sibling, True)] + [copy(1 + j, me, (*chip, c), True) for j, chip in enumerate(chips)]

    def start():
        pltpu.make_async_copy(src, dst.at[_slot(me)], local_sem).start()
        for cp in first_copies():
            cp.start()

    def finish():
        passed = []
        for j, chip in enumerate(chips):
            copy(1 + j, (*chip, c), me).wait_recv()
            passed.append(copy(4 + j, (*chip, c), sibling))
            passed[-1].start()
        copy(0, sibling, me).wait_recv()
        for j, chip in enumerate(chips):
            copy(4 + j, (*chip, 1 - c), me).wait_recv()
        for cp in first_copies() + passed:
            cp.wait_send()
        pltpu.make_async_copy(src, dst.at[_slot(me)], local_sem).wait()

    return start, finish


def _allgather_side(a):
    return ((a,), (jax.ShapeDtypeStruct((N_DEV,) + a.shape, a.dtype),),
            (pltpu.SemaphoreType.DMA((7,)), pltpu.SemaphoreType.DMA((7,)), pltpu.SemaphoreType.DMA), _allgather_steps)


def _allgather(a, name):
    ins, outs, scratch, make = _allgather_side(a)

    def body(a_ref, o_ref, *scr):
        start, finish = make((a_ref,), (o_ref,), scr)
        start()
        finish()

    any_spec = pl.BlockSpec(memory_space=pl.ANY)
    return pl.pallas_call(body, name=name, out_shape=outs[0], in_specs=[any_spec], out_specs=any_spec,
                          scratch_shapes=list(scratch))(a)


def _exchange_pair(ga, gb):
    def body(ga_ref, gb_ref, ra_ref, rb_ref, send_sems, recv_sems):
        x, y, c = _place()
        sibling = (x, y, 1 - c)
        slots = [_slot(sibling)] + [_slot((*chip, 1 - c)) for chip in _other_chips(x, y)]
        cps = []
        for t, (src, dst) in enumerate(((ga_ref, ra_ref), (gb_ref, rb_ref))):
            for k, ps in enumerate(slots):
                cps.append(pltpu.make_async_remote_copy(
                    src_ref=src.at[ps], dst_ref=dst.at[k], send_sem=send_sems.at[t, k], recv_sem=recv_sems.at[t, k],
                    device_id=sibling, device_id_type=MESH))
        for cp in cps:
            cp.start()
        for cp in cps:
            cp.wait()

    any_spec = pl.BlockSpec(memory_space=pl.ANY)
    return pl.pallas_call(
        body, name="exchange_pair",
        out_shape=[jax.ShapeDtypeStruct((4,) + ga.shape[1:], ga.dtype), jax.ShapeDtypeStruct((4,) + gb.shape[1:], gb.dtype)],
        in_specs=[any_spec] * 2,
        out_specs=[any_spec] * 2,
        scratch_shapes=[pltpu.SemaphoreType.DMA((2, 4)), pltpu.SemaphoreType.DMA((2, 4))],
    )(ga, gb)


def _pair_add(name, g, r1, slots, tb):
    _, R, Cc = g.shape
    assert R % tb == 0

    def kern(s_ref, a_ref, b_ref, o_ref):
        o_ref[...] = (a_ref[...].astype(F32) + b_ref[...].astype(F32)).astype(o_ref.dtype)

    return pl.pallas_call(
        kern, name=name,
        out_shape=jax.ShapeDtypeStruct((3, R, Cc), BF16),
        grid_spec=pltpu.PrefetchScalarGridSpec(
            num_scalar_prefetch=1, grid=(3, R // tb),
            in_specs=[pl.BlockSpec((1, tb, Cc), lambda j, i, s: (s[j], i, 0)),
                      pl.BlockSpec((1, tb, Cc), lambda j, i, s: (1 + j, i, 0))],
            out_specs=pl.BlockSpec((1, tb, Cc), lambda j, i, s: (j, i, 0))),
        compiler_params=_cp(("arbitrary", "arbitrary")),
    )(slots, g, r1)


def _exchange_ici_steps(in_refs, out_refs, scratch):
    pairs = list(zip(in_refs, out_refs))
    send_sems, recv_sems = scratch
    x, y, c = _place()

    def copies():
        return [pltpu.make_async_remote_copy(
            src_ref=src.at[j], dst_ref=dst.at[j], send_sem=send_sems.at[t, j], recv_sem=recv_sems.at[t, j],
            device_id=(*chip, c), device_id_type=MESH)
            for j, chip in enumerate(_other_chips(x, y)) for t, (src, dst) in enumerate(pairs)]

    def start():
        for cp in copies():
            cp.start()

    def finish():
        for cp in copies():
            cp.wait()

    return start, finish


def _exchange_ici_side(sa, sb):
    return ((sa, sb), (jax.ShapeDtypeStruct(sa.shape, sa.dtype), jax.ShapeDtypeStruct(sb.shape, sb.dtype)),
            (pltpu.SemaphoreType.DMA((2, 3)), pltpu.SemaphoreType.DMA((2, 3))), _exchange_ici_steps)


def _adamw(name, w, m, v, parts, tb):
    R, Cc = w.shape
    assert R % tb == 0
    n_parts = len(parts)

    def kern(*refs):
        w_ref, m_ref, v_ref = refs[:3]
        g = None
        for r_ref, (_, n) in zip(refs[3:3 + n_parts], parts):
            for s in range(n):
                term = r_ref[s].astype(F32)
                g = term if g is None else g + term
        g_out, d_out, m_out, v_out = refs[3 + n_parts:]
        mn = ADAM_B1 * m_ref[...] + (1.0 - ADAM_B1) * g
        vn = ADAM_B2 * v_ref[...] + (1.0 - ADAM_B2) * (g * g)
        m_hat = mn / (1.0 - ADAM_B1 ** ADAM_STEP)
        v_hat = vn / (1.0 - ADAM_B2 ** ADAM_STEP)
        g_out[...] = g
        d_out[...] = -ADAM_LR * (m_hat / (jnp.sqrt(v_hat) + ADAM_EPS) + ADAM_WD * w_ref[...])
        m_out[...] = mn
        v_out[...] = vn

    blk = pl.BlockSpec((tb, Cc), lambda i: (i, 0))
    return pl.pallas_call(
        kern, name=name,
        out_shape=[jax.ShapeDtypeStruct((R, Cc), F32)] * 4,
        grid=(R // tb,),
        in_specs=[blk] * 3 + [pl.BlockSpec((n, tb, Cc), lambda i: (0, i, 0)) for (_, n) in parts],
        out_specs=[blk] * 4,
        compiler_params=_cp(("arbitrary",)),
    )(w, m, v, *[a for (a, _) in parts])


def _pad_cols(a, w):
    return jnp.pad(a, ((0, 0), (0, w - a.shape[1])))


def _pad_rows(a, r):
    return jnp.pad(a, ((0, r - a.shape[0]), (0, 0)))


def _pack_b(pf, pr, wo, w2, a2, rows):
    body = jnp.concatenate([pf, pr, wo.reshape(2048, 256), jnp.concatenate([w2, a2], axis=1)], axis=0)
    return _pad_rows(body, rows)


def kernel(x, norm_gain, w_in, fox_forget_bias, rwkv_shift_mix, rwkv_w0, rwkv_w2, rwkv_a0, rwkv_a2, rwkv_k_k, rwkv_k_a, rwkv_r_k, rwkv_ln_w, rwkv_ln_b, w_proj_fox, w_proj_rwkv, w_out, final_norm_gain, loss_target, m_norm_gain, m_w_in, m_fox_forget_bias, m_rwkv_shift_mix, m_rwkv_w0, m_rwkv_w2, m_rwkv_a0, m_rwkv_a2, m_rwkv_k_k, m_rwkv_k_a, m_rwkv_r_k, m_rwkv_ln_w, m_rwkv_ln_b, m_w_proj_fox, m_w_proj_rwkv, m_w_out, m_final_norm_gain, v_norm_gain, v_w_in, v_fox_forget_bias, v_rwkv_shift_mix, v_rwkv_w0, v_rwkv_w2, v_rwkv_a0, v_rwkv_a2, v_rwkv_k_k, v_rwkv_k_a, v_rwkv_r_k, v_rwkv_ln_w, v_rwkv_ln_b, v_w_proj_fox, v_w_proj_rwkv, v_w_out, v_final_norm_gain):
    T, D = x.shape[1], x.shape[2]
    assert D == 2048 and T % LANE == 0
    NI = w_in.shape[2]
    IN = N_DEV * NI
    RB = 4224
    x2 = x[0]
    lt2 = loss_target[0]
    me = _slot(_place())

    wa = _allgather(w_in[0].astype(BF16), "allgather_w_in")
    packed_own = _pack_b(w_proj_fox[0], w_proj_rwkv[0], w_out[0], rwkv_w2[0], rwkv_a2[0], RB).astype(BF16)
    sections = [(0, 4096, 0), (4104, 4096, 4096), (8392, 4096, 8192), (4096, 8, 12288), (8200, 96, 12544), (8296, 96, 12672)]
    NP = 12800
    pieces, at_col = [], 0
    for lo, width, pad_lo in sections:
        if pad_lo > at_col:
            pieces.append(jnp.zeros((D, pad_lo - at_col), BF16))
        col = lo
        while col < lo + width:
            d = col // NI
            stop = min(lo + width, (d + 1) * NI)
            pieces.append(wa[d, :, col - d * NI:stop - d * NI])
            col = stop
        at_col = pad_lo + width
    pieces.append(jnp.zeros((D, NP - at_col), BF16))
    w_pad = jnp.concatenate(pieces, axis=1)
    F_CB, LORA_CB = 96, 49

    mu = rwkv_shift_mix
    mu_main = mu[:, 0:4096]
    mu_lora = jnp.concatenate([_pad_cols(mu[:, 4096:4192], LANE), _pad_cols(mu[:, 4192:4288], LANE)], axis=1)
    bias_pad = _pad_cols(fox_forget_bias, LANE)
    rk_flat = rwkv_r_k.reshape(1, 1024)
    gf = final_norm_gain.reshape(1, D)

    tb = min(256, T)
    tbh = min(128, T)
    (h,) = _rowcall("rms_fwd", T, tb, [_rows(x2, tb, D), _whole(norm_gain)],
                    lambda i, xv, g: ([_rms_math(xv, g)], []), [(D, BF16)])
    u, wb = _mm(h, w_pad, tm=1024, tn=1280, name="mm_in", side=_allgather_side(packed_own))
    wpf = wb[:, 0:1024, :].transpose(1, 0, 2).reshape(1024, D)
    wpr = wb[:, 1024:2048, :].transpose(1, 0, 2).reshape(1024, D)
    wo = wb[:, 2048:4096, :].reshape(N_DEV * 256, D)
    w2p = _pad_rows(wb[:, 4096:4192, 0:128].transpose(1, 0, 2).reshape(96, 1024).astype(F32), LANE)
    a2p = _pad_rows(wb[:, 4096:4192, 128:256].transpose(1, 0, 2).reshape(96, 1024).astype(F32), LANE)

    c8 = _gates_fwd(u, bias_pad, T, F_CB)
    c3 = c8.reshape(8, 1, T)
    o_raw, o_a, lse = _attn_fwd(u, c3, T)

    def shift_body(i, um, hm, ul, hl, mm_, ml):
        outs = []
        for uv, hv, mv in ((um, hm, mm_), (ul, hl, ml)):
            hv = jnp.where(i == 0, 0.0, hv)
            prev = pltpu.roll(jnp.concatenate([hv, uv], axis=0), 1, 0)[SUB:]
            outs.append(uv + (prev - uv) * mv)
        return outs, []

    def halo_prev(arr, w, cb):
        return (arr, (SUB, w), lambda i: (jnp.maximum(i * (tbh // SUB) - 1, 0), cb))

    xs, xl = _rowcall("rwkv_shift_fwd", T, tbh,
                      [_rows(u, tbh, 4096, 1), halo_prev(u, 4096, 1), _rows(u, tbh, 256, LORA_CB), halo_prev(u, 256, LORA_CB),
                       _whole(mu_main), _whole(mu_lora)],
                      shift_body, [(4096, F32), (256, F32)])

    prep_par = [_whole(rwkv_w0), _whole(rwkv_a0), _whole(rwkv_k_k), _whole(rwkv_k_a), _whole(w2p), _whole(a2p)]
    prep_rows = [_rows(xs, tbh, 1024, 1), _rows(xl, tbh, LANE, 0), _rows(xl, tbh, LANE, 1)]
    lw, kp, an, bb = _rowcall("rwkv_prep_fwd", T, tbh, prep_rows + prep_par,
                              lambda i, *a: (list(_prep_math(*a)), []), [(1024, F32)] * 4)
    y, states = _scan_fwd(xs, lw, kp, an, bb, T)
    post_rows = [_rows(y, tbh, 1024), _rows(xs, tbh, 1024, 0), _rows(kp, tbh, 1024), _rows(xs, tbh, 1024, 2), _rows(xs, tbh, 1024, 3)]
    post_par = [_whole(rwkv_ln_w), _whole(rwkv_ln_b), _whole(rk_flat)]
    (o_b,) = _rowcall("rwkv_post_fwd", T, tbh, post_rows + post_par,
                      lambda i, *a: ([_post_math(*a)], []), [(1024, BF16)])

    pa = _mm(o_a, wpf, name="mm_proj_fox")
    pb = _mm(o_b, wpr, name="mm_proj_rwkv")
    merge_rows = [_rows(u, tb, D, 4), _rows(u, tb, D, 5), _rows(pa, tb, D), _rows(pb, tb, D)]
    (mg,) = _rowcall("merge_fwd", T, tb, merge_rows, lambda i, *a: ([_merge_math(*a)], []), [(D, BF16)])
    mo = _mm(mg, wo, name="mm_out")

    def head_body(i, xv, mov, ltv, g):
        out = xv + mov
        r = lax.rsqrt(jnp.mean(out * out, axis=-1, keepdims=True) + RMS_EPS)
        yn = out * r
        err = yn * g - ltv
        loss = 0.5 * jnp.sum(jnp.sum(err * err, axis=-1, keepdims=True), axis=0, keepdims=True) / D
        dyv = err / D
        dyn = dyv * g
        dout = r * (dyn - yn * jnp.mean(dyn * yn, axis=-1, keepdims=True))
        return [dout], [loss, jnp.sum(dyv * yn, axis=0, keepdims=True)]

    dout, loss_p, dgf_p = _rowcall("loss_head", T, tb, [_rows(x2, tb, D), _rows(mo, tb, D), _rows(lt2, tb, D), _whole(gf)],
                                   head_body, [(D, F32)], [(1, 1), (1, D)])

    dm = _mm(dout, wo, tb=True, name="mm_out_dx")
    dwo = _mm(mg.T, dout, out_dtype=BF16, name="mm_out_dw")

    def merge_bwd_body(i, ga, gb, pav, pbv, dmv):
        _, vjp = jax.vjp(_merge_math, ga, gb, pav, pbv)
        dga, dgb, dpa, dpb = vjp(dmv)
        return [dga, dgb, dpa, dpb], []

    dga, dgb, dpa, dpb = _rowcall("merge_bwd", T, tb, merge_rows + [_rows(dm, tb, D)], merge_bwd_body,
                                  [(D, BF16), (D, BF16), (D, BF16), (D, BF16)])
    doa = _mm(dpa, wpf, tb=True, name="mm_proj_fox_dx")
    dwpf = _mm(o_a.T, dpa, out_dtype=BF16, name="mm_proj_fox_dw")
    dob = _mm(dpb, wpr, tb=True, name="mm_proj_rwkv_dx")
    dwpr = _mm(o_b.T, dpb, out_dtype=BF16, name="mm_proj_rwkv_dw")

    do_b, dza, delta = _attn_delta(u, c3, o_raw, lse, doa, T)
    dq, dk, dv, dc3 = _attn_bwd(u, c3, lse, do_b, delta, T)
    dfl, dbias_p = _gates_bwd(dc3.reshape(8, T), u, bias_pad, T, F_CB)

    def post_bwd_body(i, yv, rv, kpv, vv, zv, lnw, lnb, rkv, dobv):
        _, vjp = jax.vjp(_post_math, yv, rv, kpv, vv, zv, lnw, lnb, rkv)
        dy_, dr_, dkp_, dv_, dz_, dlnw, dlnb, drk = vjp(dobv)
        return [dy_, dr_, dkp_, dv_, dz_], [dlnw, dlnb, drk]

    dy_s, dr_p, dkp_p, dv_p, dzb, dlnw_p, dlnb_p, drk_p = _rowcall(
        "rwkv_post_bwd", T, tbh, post_rows + post_par + [_rows(dob, tbh, 1024)], post_bwd_body,
        [(1024, F32)] * 5, [(1, 1024)] * 3)
    dr_s, dlw, dkp_s, dv_s, dan, dbb = _scan_bwd(xs, lw, kp, an, bb, states, dy_s, T)

    def prep_bwd_body(i, xk, xwd, xad, w0, a0, kkw, kaw, w2v, a2v, dlw_, dkp1, dkp2, dan_, dbb_, dr1, dr2, dv1, dv2, dz_):
        _, vjp = jax.vjp(_prep_math, xk, xwd, xad, w0, a0, kkw, kaw, w2v, a2v)
        dxk, dxwd, dxad, dw0, da0, dkk, dka, dw2, da2 = vjp((dlw_, dkp1 + dkp2, dan_, dbb_))
        return [[dr1 + dr2, dxk, dv1 + dv2, dz_], [dxwd, dxad]], [dw0, da0, dkk, dka, dw2, da2]

    cots = [dlw, dkp_s, dkp_p, dan, dbb, dr_s, dr_p, dv_s, dv_p, dzb]
    dxs, dxl, dw0_p, da0_p, dkk_p, dka_p, dw2_p, da2_p = _rowcall(
        "rwkv_prep_bwd", T, tbh, prep_rows + prep_par + [_rows(c_, tbh, 1024) for c_ in cots], prep_bwd_body,
        [(4096, F32), (256, F32)], [(1, 1024)] * 4 + [(LANE, 1024)] * 2)

    def shift_bwd_body(i, dm_, hm, dl_, hl, um, pm, ul, pl_, mm_, ml):
        last = i == T // tbh - 1
        outs, accs = [], []
        for dv_, hv, uv, pv, mv in ((dm_, hm, um, pm, mm_), (dl_, hl, ul, pl_, ml)):
            hv = jnp.where(last, 0.0, hv)
            nxt = pltpu.roll(jnp.concatenate([dv_, hv], axis=0), tbh + SUB - 1, 0)[:tbh]
            pv = jnp.where(i == 0, 0.0, pv)
            prev = pltpu.roll(jnp.concatenate([pv, uv], axis=0), 1, 0)[SUB:]
            outs.append(dv_ * (1.0 - mv) + nxt * mv)
            accs.append(jnp.sum(dv_ * (prev - uv), axis=0, keepdims=True))
        return outs, accs

    def halo_next(arr, w, cb):
        last_blk = T // SUB - 1
        return (arr, (SUB, w), lambda i: (jnp.minimum((i + 1) * (tbh // SUB), last_blk), cb))

    du_b, du_l, dmu_main_p, dmu_lora_p = _rowcall(
        "rwkv_shift_bwd", T, tbh,
        [_rows(dxs, tbh, 4096), halo_next(dxs, 4096, 0), _rows(dxl, tbh, 256), halo_next(dxl, 256, 0),
         _rows(u, tbh, 4096, 1), halo_prev(u, 4096, 1), _rows(u, tbh, 256, LORA_CB), halo_prev(u, 256, LORA_CB),
         _whole(mu_main), _whole(mu_lora)],
        shift_bwd_body, [(4096, BF16), (256, BF16)], [(1, 4096), (1, 256)])

    du = jnp.concatenate([dq.astype(BF16), dk.astype(BF16), dv.astype(BF16), dza.astype(BF16), du_b, dga, dgb,
                          dfl.astype(BF16), jnp.zeros((T, LANE), BF16), du_l], axis=1)
    dw_pad = _mm(h.T, du, out_dtype=BF16, tm=1024, tn=1280, name="mm_in_dw")

    by_col = sorted(sections)
    blocks = []
    for d in range(N_DEV):
        parts = []
        for lo, width, pad_lo in by_col:
            a, b = max(lo, d * NI), min(lo + width, (d + 1) * NI)
            if a < b:
                parts.append(dw_pad[:, pad_lo + a - lo:pad_lo + b - lo])
        blocks.append(jnp.concatenate(parts, axis=1)[None])
    ga = jnp.concatenate(blocks, axis=0)
    lora_g = jnp.concatenate([dw2_p[:96].reshape(96, N_DEV, 128).transpose(1, 0, 2),
                              da2_p[:96].reshape(96, N_DEV, 128).transpose(1, 0, 2)], axis=2).astype(BF16)
    gb = jnp.concatenate([dwpf.reshape(1024, N_DEV, 256).transpose(1, 0, 2),
                          dwpr.reshape(1024, N_DEV, 256).transpose(1, 0, 2),
                          dwo.reshape(N_DEV, 2048, 256), lora_g, jnp.zeros((N_DEV, RB - 4192, 256), BF16)], axis=1)

    xx, yy, cc = _place()
    chip_slots = jnp.stack([_slot((*chip, cc)) for chip in _other_chips(xx, yy)]).astype(jnp.int32)
    r1a, r1b = _exchange_pair(ga, gb)
    sa = _pair_add("pair_add_w_in", ga, r1a, chip_slots, 256)
    sb = _pair_add("pair_add_packed", gb, r1b, chip_slots, 384)
    ga_own = lax.dynamic_index_in_dim(ga, me, 0, keepdims=True)
    gb_own = lax.dynamic_index_in_dim(gb, me, 0, keepdims=True)
    dh, r2a, r2b = _mm(du, w_pad, tb=True, tm=1024, tn=1024, tk=NP // 10, name="mm_in_dx", side=_exchange_ici_side(sa, sb))

    def rms_bwd_body(i, xv, g, dhv, doutv):
        _, vjp = jax.vjp(_rms_math, xv, g)
        dx_, dg_ = vjp(dhv)
        return [dx_ + doutv], [dg_]

    grad_x2, dng_p = _rowcall("rms_bwd", T, tb, [_rows(x2, tb, D), _whole(norm_gain), _rows(dh, tb, D), _rows(dout, tb, D)],
                              rms_bwd_body, [(D, F32)], [(1, D)])

    dmu = jnp.concatenate([dmu_main_p, dmu_lora_p[:, 0:96], dmu_lora_p[:, 128:224]], axis=1)
    small_parts = [dng_p, dbias_p[:, 0:8], dmu, dw0_p, da0_p, dkk_p, dka_p, drk_p, dlnw_p, dlnb_p, dgf_p, loss_p]
    SR = 128
    small = _pad_cols(jnp.concatenate(small_parts, axis=1), SR * LANE).reshape(SR, LANE)
    rs = _allgather(small, "allgather_small")

    g_in, d_in, m_in, v_in = _adamw("adamw_w_in", w_in[0], m_w_in[0], v_w_in[0], [(ga_own, 1), (r1a, 1), (r2a, 3)], min(128, D))
    pk = lambda pf, pr, wo_, w2_, a2_: _pack_b(pf[0], pr[0], wo_[0], w2_[0], a2_[0], RB)
    outs_b = _adamw("adamw_packed", pk(w_proj_fox, w_proj_rwkv, w_out, rwkv_w2, rwkv_a2),
                    pk(m_w_proj_fox, m_w_proj_rwkv, m_w_out, m_rwkv_w2, m_rwkv_a2),
                    pk(v_w_proj_fox, v_w_proj_rwkv, v_w_out, v_rwkv_w2, v_rwkv_a2), [(gb_own, 1), (r1b, 1), (r2b, 3)], 384)

    def pack_small(ng, fb, sm, w0, a0, kk_, ka_, rk_, lnw, lnb, fg):
        parts = [ng, fb, sm, w0, a0, kk_, ka_, rk_.reshape(1, 1024), lnw, lnb, fg.reshape(1, D), jnp.zeros((1, 1), F32)]
        return _pad_cols(jnp.concatenate(parts, axis=1), SR * LANE).reshape(SR, LANE)

    outs_s = _adamw("adamw_small",
                    pack_small(norm_gain, fox_forget_bias, rwkv_shift_mix, rwkv_w0, rwkv_a0, rwkv_k_k, rwkv_k_a, rwkv_r_k,
                               rwkv_ln_w, rwkv_ln_b, final_norm_gain),
                    pack_small(m_norm_gain, m_fox_forget_bias, m_rwkv_shift_mix, m_rwkv_w0, m_rwkv_a0, m_rwkv_k_k, m_rwkv_k_a,
                               m_rwkv_r_k, m_rwkv_ln_w, m_rwkv_ln_b, m_final_norm_gain),
                    pack_small(v_norm_gain, v_fox_forget_bias, v_rwkv_shift_mix, v_rwkv_w0, v_rwkv_a0, v_rwkv_k_k, v_rwkv_k_a,
                               v_rwkv_r_k, v_rwkv_ln_w, v_rwkv_ln_b, v_final_norm_gain),
                    [(rs, N_DEV)], SR)

    def unpack_b(pkd):
        return dict(w_proj_fox=pkd[0:1024][None], w_proj_rwkv=pkd[1024:2048][None], w_out=pkd[2048:4096].reshape(1, 256, D),
                    rwkv_w2=pkd[4096:4192, 0:128][None], rwkv_a2=pkd[4096:4192, 128:256][None])

    def unpack_s(pkd):
        flat = pkd.reshape(1, SR * LANE)
        names = [("norm_gain", D), ("fox_forget_bias", 8), ("rwkv_shift_mix", 4288), ("rwkv_w0", 1024), ("rwkv_a0", 1024),
                 ("rwkv_k_k", 1024), ("rwkv_k_a", 1024), ("rwkv_r_k", 1024), ("rwkv_ln_w", 1024), ("rwkv_ln_b", 1024),
                 ("final_norm_gain", D), ("loss", 1)]
        out, off = {}, 0
        for nm, n in names:
            out[nm] = flat[:, off:off + n]
            off += n
        out["rwkv_r_k"] = out["rwkv_r_k"].reshape(1, 16, 64)
        out["final_norm_gain"] = out["final_norm_gain"].reshape(D)
        return out

    order = ["norm_gain", "w_in", "fox_forget_bias", "rwkv_shift_mix", "rwkv_w0", "rwkv_w2", "rwkv_a0", "rwkv_a2", "rwkv_k_k",
             "rwkv_k_a", "rwkv_r_k", "rwkv_ln_w", "rwkv_ln_b", "w_proj_fox", "w_proj_rwkv", "w_out", "final_norm_gain"]
    result = []
    loss = None
    for kind, big in enumerate((g_in, d_in, m_in, v_in)):
        d = {**unpack_b(outs_b[kind]), **unpack_s(outs_s[kind]), "w_in": big[None]}
        if kind == 0:
            loss = d["loss"].reshape(())
        result += [d[n] for n in order]
    return (loss, grad_x2[None], *result)
```

```python
import functools

import jax
import jax.numpy as jnp
from jax import lax
from jax.experimental import pallas as pl
from jax.experimental.pallas import tpu as pltpu

F32 = jnp.float32
BF16 = jnp.bfloat16
HI = lax.Precision.HIGHEST
MESH = pl.DeviceIdType.MESH

FOX_HD = 128
RW_HD = 64
RMS_EPS = 1e-6
GN_EPS = 64e-5
L2_EPS = 1e-12
ADAM_LR = 0.001
ADAM_B1 = 0.9
ADAM_B2 = 0.999
ADAM_EPS = 1e-08
ADAM_WD = 0.01
ADAM_STEP = 10

LANE = 128
SUB = 8
VMEM_LIMIT = 56 * 1024 * 1024
N_DEV = 8
CHUNK = 64
SCAN_GROUP = 4
SCAN_PAIRS = 2
PS = None
NEG = -1e30


def _scan_shape(T):
    c = min(CHUNK, T)
    return c, min(SCAN_GROUP, T // c)


def _cp(sem=None):
    return pltpu.CompilerParams(dimension_semantics=sem, vmem_limit_bytes=VMEM_LIMIT)


def _sigmoid(x):
    return jax.nn.sigmoid(x)


def _softplus(x):
    return jnp.maximum(x, 0.0) + jnp.log(1.0 + jnp.exp(-jnp.abs(x)))


def _nn(a, b, prec=None):
    return lax.dot_general(a, b, (((1,), (0,)), ((), ())), precision=prec, preferred_element_type=F32)


def _nt(a, b, prec=None):
    return lax.dot_general(a, b, (((1,), (1,)), ((), ())), precision=prec, preferred_element_type=F32)


def _tn(a, b, prec=None):
    return lax.dot_general(a, b, (((0,), (0,)), ((), ())), precision=prec, preferred_element_type=F32)


def _iota2(shape, dim):
    return lax.broadcasted_iota(jnp.int32, shape, dim)


def _seg_sum(x):
    r = _iota2((LANE, LANE), 0) // RW_HD
    c = _iota2((LANE, LANE), 1) // RW_HD
    bd = (r == c).astype(F32)
    parts = [_nn(x[:, j * LANE:(j + 1) * LANE], bd, HI) for j in range(x.shape[1] // LANE)]
    return parts[0] if len(parts) == 1 else jnp.concatenate(parts, axis=1)


def _mm(a, b, *, ta=False, tb=False, out_dtype=F32, tm=512, tn=512, tk=None, name, side=None):
    assert not (ta and tb)
    K, M = a.shape if ta else a.shape[::-1]
    N = b.shape[0] if tb else b.shape[1]
    tm, tn = min(tm, M), min(tn, N)
    tk = K if tk is None else tk
    nk = K // tk
    assert M % tm == 0 and N % tn == 0 and K % tk == 0
    gi, gj = M // tm, N // tn
    a_spec = pl.BlockSpec((tk, tm), lambda i, j, k: (k, i)) if ta else pl.BlockSpec((tm, tk), lambda i, j, k: (i, k))
    b_spec = pl.BlockSpec((tn, tk), lambda i, j, k: (j, k)) if tb else pl.BlockSpec((tk, tn), lambda i, j, k: (k, j))
    side_ins, side_outs, side_scr, side_make = side if side is not None else ((), (), (), None)
    n_si, n_so = len(side_ins), len(side_outs)
    n_acc = 0 if nk == 1 else 1

    def body(*refs):
        a_ref, b_ref = refs[:2]
        o_ref = refs[2 + n_si]
        scr = refs[3 + n_si + n_so:]
        k = pl.program_id(2)
        if side_make is not None:
            start, finish = side_make(refs[2:2 + n_si], refs[3 + n_si:3 + n_si + n_so], scr[n_acc:])
            first = jnp.logical_and(jnp.logical_and(pl.program_id(0) == 0, pl.program_id(1) == 0), k == 0)
            last = jnp.logical_and(jnp.logical_and(pl.program_id(0) == gi - 1, pl.program_id(1) == gj - 1), k == nk - 1)
            pl.when(first)(start)
        av = a_ref[...].astype(BF16)
        bv = b_ref[...].astype(BF16)
        p = _tn(av, bv) if ta else _nt(av, bv) if tb else _nn(av, bv)
        if nk == 1:
            o_ref[...] = p.astype(out_dtype)
        else:
            acc_ref = scr[0]

            @pl.when(k == 0)
            def _():
                acc_ref[...] = p

            @pl.when(k > 0)
            def _():
                acc_ref[...] += p

            @pl.when(k == nk - 1)
            def _():
                o_ref[...] = acc_ref[...].astype(out_dtype)
        if side_make is not None:
            pl.when(last)(finish)

    any_spec = pl.BlockSpec(memory_space=pl.ANY)
    res = pl.pallas_call(
        body, name=name,
        out_shape=[jax.ShapeDtypeStruct((M, N), out_dtype)] + list(side_outs),
        grid=(gi, gj, nk),
        in_specs=[a_spec, b_spec] + [any_spec] * n_si,
        out_specs=[pl.BlockSpec((tm, tn), lambda i, j, k: (i, j))] + [any_spec] * n_so,
        scratch_shapes=([] if nk == 1 else [pltpu.VMEM((tm, tn), F32)]) + list(side_scr),
        compiler_params=_cp(("arbitrary",) * 3 if side is not None else ("parallel", "parallel", "arbitrary")),
    )(a, b, *side_ins)
    return res if side is not None else res[0]


def _rows(arr, tb, w, cb=0):
    return (arr, (tb, w), lambda i: (i, cb))


def _whole(arr):
    nd = arr.ndim
    return (arr, arr.shape, lambda i: (0,) * nd)


def _rowcall(name, T, tb, ins, body, outs, accs=()):
    n_in, n_out = len(ins), len(outs)

    def kern(*refs):
        i = pl.program_id(0)
        vals = [r[...] for r in refs[:n_in]]
        ro, ao = body(i, *vals)
        for r, v in zip(refs[n_in:n_in + n_out], ro):
            if isinstance(v, (list, tuple)):
                off = 0
                for piece in v:
                    w = piece.shape[1]
                    r[:, off:off + w] = piece.astype(r.dtype)
                    off += w
            else:
                r[...] = v.astype(r.dtype)
        if accs:
            acc_refs = refs[n_in + n_out:]

            @pl.when(i == 0)
            def _():
                for r in acc_refs:
                    r[...] = jnp.zeros(r.shape, F32)

            for r, v in zip(acc_refs, ao):
                r[...] += v

    out_shape = [jax.ShapeDtypeStruct((T, w), dt) for (w, dt) in outs] + [jax.ShapeDtypeStruct(s, F32) for s in accs]
    out_specs = [pl.BlockSpec((tb, w), lambda i: (i, 0)) for (w, dt) in outs] + [pl.BlockSpec(s, lambda i: (0, 0)) for s in accs]
    res = pl.pallas_call(
        kern, name=name,
        out_shape=out_shape,
        grid=(T // tb,),
        in_specs=[pl.BlockSpec(bs, im) for (_, bs, im) in ins],
        out_specs=out_specs,
        compiler_params=_cp(("arbitrary",)),
    )(*[a for (a, _, _) in ins])
    return res


def _rms_math(x, g):
    r = lax.rsqrt(jnp.mean(x * x, axis=-1, keepdims=True) + RMS_EPS)
    return x * r * g


def _merge_math(ga, gb, pa, pb):
    return _sigmoid(ga) * pa + _sigmoid(gb) * pb


def _prep_math(xk, xwd, xad, w0, a0, kk_w, ka_w, w2p, a2p):
    z = w0 + _nn(jnp.tanh(xwd), w2p, HI)
    w = -_softplus(-z) - 0.5
    lw = -jnp.exp(w)
    ag = _sigmoid(a0 + _nn(xad, a2p, HI))
    p = xk * kk_w
    n = jnp.maximum(jnp.sqrt(_seg_sum(p * p)), L2_EPS)
    kk = p / n
    kp = xk * (1.0 + (ag - 1.0) * ka_w)
    return lw, kp, -kk, kk * ag


def _post_math(y, r, kp, v, z, lnw, lnb, rk):
    inv = 1.0 / RW_HD
    mu = _seg_sum(y) * inv
    d = y - mu
    var = _seg_sum(d * d) * inv
    yn = d * lax.rsqrt(var + GN_EPS) * lnw + lnb
    bonus = _seg_sum(r * kp * rk) * v
    return (yn + bonus) * (z * _sigmoid(z))


def _scan_group(s0s, *flat):
    P = len(s0s)
    G = len(flat) // (6 * P)
    ch = [flat[6 * i:6 * i + 6] for i in range(P * G)]
    C = ch[0][0].shape[0]
    C2 = 2 * C
    cat = jnp.concatenate
    m0 = _iota2((1, LANE), 1) < RW_HD
    mask0 = m0.astype(F32)
    mask1 = 1.0 - mask0
    r2 = _iota2((C2, C2), 0)
    c2 = _iota2((C2, C2), 1)
    dist = r2 - c2
    in_head = dist <= r2 % C
    eye = (r2 == c2).astype(F32)
    lower = (_iota2((C, C), 0) >= _iota2((C, C), 1)).astype(F32)
    bd = (_iota2((LANE, LANE), 0) // RW_HD) == (_iota2((LANE, LANE), 1) // RW_HD)

    def tri(m, strict):
        return jnp.where(dist > 0 if strict else dist >= 0, jnp.where(in_head, m, 0.0), 0.0)

    def sel(z):
        return jnp.where(m0, z[:C], z[C:])

    gs = [_nn(lower, c[1], HI) for c in ch]
    pre = []
    for (r, lw, k, v, a, b), g in zip(ch, gs):
        g_end = jnp.sum(lw, axis=0, keepdims=True)
        en = jnp.exp(-g)
        ec = jnp.exp(g_end - g)
        pre.append(dict(at=a * jnp.exp(g - lw), rt=r * jnp.exp(g), bt=b * en, kt=k * en, bh=b * ec, kh=k * ec,
                        dec=jnp.exp(g_end), v=v))
    grams = [_nt(cat([p["at"] * mask0, p["at"] * mask1, p["rt"] * mask0, p["rt"] * mask1], axis=0),
                 cat([p["bt"], p["bt"], p["kt"], p["kt"]], axis=0), PS) for p in pre]
    mab = [tri(gm[:C2, :C2], True) for gm in grams]
    mp = [_nn(m, m, PS) for m in mab]
    tinv = [eye + m for m in mab]
    n = 2
    while n < C:
        last = 2 * n >= C
        for i in range(P * G):
            if last:
                tinv[i] = tinv[i] + _nn(mp[i], tinv[i], PS)
            else:
                z = _nn(mp[i], cat([mp[i], tinv[i]], axis=1), PS)
                mp[i], tinv[i] = z[:, :C2], tinv[i] + z[:, C2:]
        n *= 2
    xv = [sel(_nn(tri(gm[:C2, C2:], True), cat([p["v"], p["v"]], axis=0), PS)) for gm, p in zip(grams, pre)]
    ys, s = [None] * (P * G), list(s0s)
    for i in range(G):
        for q in range(P):
            n = q * G + i
            p, gm = pre[n], grams[n]
            sx = _nt(cat([p["at"], p["rt"]], axis=0), s[q], PS)
            x = sx[:C] + xv[n]
            u = sel(_nn(tinv[n], cat([x, x], axis=0), PS))
            v = p["v"]
            ys[n] = sx[C:] + sel(_nn(cat([tri(gm[C2:, :C2], False), tri(gm[C2:, C2:], False)], axis=1),
                                     cat([u, u, v, v], axis=0), PS))
            s[q] = s[q] * p["dec"] + jnp.where(bd, _tn(cat([u, v], axis=0), cat([p["bh"], p["kh"]], axis=0), PS), 0.0)
    return tuple(ys), tuple(s)


def _scan_fwd(xs, lw, kp, an, bb, T):
    C, G = _scan_shape(T)
    P = SCAN_PAIRS
    nc = T // (C * G)
    npair = 1024 // LANE

    def kern(r_ref, lw_ref, k_ref, v_ref, a_ref, b_ref, y_ref, st_ref, s_scr):
        n = pl.program_id(1)

        @pl.when(n == 0)
        def _():
            s_scr[...] = jnp.zeros(s_scr.shape, F32)

        st_ref[0] = s_scr[...]
        ins = (r_ref, lw_ref, k_ref, v_ref, a_ref, b_ref)
        ys, s1 = _scan_group(tuple(s_scr[q] for q in range(P)),
                             *[ref[i * C:(i + 1) * C, q * LANE:(q + 1) * LANE] for q in range(P) for i in range(G) for ref in ins])
        for q in range(P):
            for i in range(G):
                y_ref[i * C:(i + 1) * C, q * LANE:(q + 1) * LANE] = ys[q * G + i]
            s_scr[q] = s1[q]

    def col(off):
        return pl.BlockSpec((C * G, P * LANE), lambda p, n: (n, off // P + p))

    return pl.pallas_call(
        kern, name="rwkv_scan_fwd",
        out_shape=[jax.ShapeDtypeStruct((T, 1024), F32), jax.ShapeDtypeStruct((nc, npair, LANE, LANE), F32)],
        grid=(npair // P, nc),
        in_specs=[col(0), col(0), col(0), col(16), col(0), col(0)],
        out_specs=[col(0), pl.BlockSpec((1, P, LANE, LANE), lambda p, n: (n, p, 0, 0))],
        scratch_shapes=[pltpu.VMEM((P, LANE, LANE), F32)],
        compiler_params=_cp(("parallel", "arbitrary")),
    )(xs, lw, kp, xs, an, bb)


def _scan_bwd(xs, lw, kp, an, bb, states, dy, T):
    C, G = _scan_shape(T)
    P = SCAN_PAIRS
    nc = T // (C * G)
    npair = 1024 // LANE

    def kern(r_ref, lw_ref, k_ref, v_ref, a_ref, b_ref, st_ref, dy_ref, dr_ref, dlw_ref, dk_ref, dv_ref, da_ref, db_ref, ds_scr):
        n = pl.program_id(1)

        @pl.when(n == 0)
        def _():
            ds_scr[...] = jnp.zeros(ds_scr.shape, F32)

        ins = (r_ref, lw_ref, k_ref, v_ref, a_ref, b_ref)
        units = [(q, i) for q in range(P) for i in range(G)]
        _, vjp = jax.vjp(_scan_group, tuple(st_ref[0, q] for q in range(P)),
                         *[ref[i * C:(i + 1) * C, q * LANE:(q + 1) * LANE] for q, i in units for ref in ins])
        grads = vjp((tuple(dy_ref[i * C:(i + 1) * C, q * LANE:(q + 1) * LANE] for q, i in units),
                     tuple(ds_scr[q] for q in range(P))))
        for q in range(P):
            ds_scr[q] = grads[0][q]
        outs = (dr_ref, dlw_ref, dk_ref, dv_ref, da_ref, db_ref)
        for n_, (q, i) in enumerate(units):
            for t, ref in enumerate(outs):
                ref[i * C:(i + 1) * C, q * LANE:(q + 1) * LANE] = grads[1 + 6 * n_ + t]

    def col(off):
        return pl.BlockSpec((C * G, P * LANE), lambda p, n: (nc - 1 - n, off // P + p))

    return pl.pallas_call(
        kern, name="rwkv_scan_bwd",
        out_shape=[jax.ShapeDtypeStruct((T, 1024), F32)] * 6,
        grid=(npair // P, nc),
        in_specs=[col(0), col(0), col(0), col(16), col(0), col(0),
                  pl.BlockSpec((1, P, LANE, LANE), lambda p, n: (nc - 1 - n, p, 0, 0)), col(0)],
        out_specs=[col(0)] * 6,
        scratch_shapes=[pltpu.VMEM((P, LANE, LANE), F32)],
        compiler_params=_cp(("parallel", "arbitrary")),
    )(xs, lw, kp, xs, an, bb, states, dy)


def _gates_fwd(u, bias_pad, T, f_cb):
    nb = T // LANE

    def kern(f_ref, b_ref, c_ref):
        x = f_ref[...] + b_ref[...]
        lf = jnp.minimum(x, 0.0) - jnp.log(1.0 + jnp.exp(-jnp.abs(x)))
        lft = lf.T
        ut = (_iota2((LANE, LANE), 0) <= _iota2((LANE, LANE), 1)).astype(F32)
        carry = jnp.zeros((LANE, 1), F32)
        for blk in range(nb):
            seg = lft[:, blk * LANE:(blk + 1) * LANE]
            cs = _nn(seg, ut, HI) + carry
            c_ref[:, blk * LANE:(blk + 1) * LANE] = cs[:SUB, :]
            carry = carry + jnp.sum(seg, axis=1, keepdims=True)

    return pl.pallas_call(
        kern, name="fox_gates_fwd",
        out_shape=jax.ShapeDtypeStruct((SUB, T), F32),
        grid=(1,),
        in_specs=[pl.BlockSpec((T, LANE), lambda i: (0, f_cb)), pl.BlockSpec((1, LANE), lambda i: (0, 0))],
        out_specs=pl.BlockSpec((SUB, T), lambda i: (0, 0)),
        compiler_params=_cp(("arbitrary",)),
    )(u, bias_pad)


def _gates_bwd(dc, u, bias_pad, T, f_cb):
    nb = T // LANE

    def kern(dc_ref, f_ref, b_ref, dfl_ref, db_ref):
        dcv = jnp.concatenate([dc_ref[...], jnp.zeros((LANE - SUB, T), F32)], axis=0)
        lt = (_iota2((LANE, LANE), 0) >= _iota2((LANE, LANE), 1)).astype(F32)
        carry = jnp.zeros((LANE, 1), F32)
        pieces = [None] * nb
        for blk in range(nb - 1, -1, -1):
            seg = dcv[:, blk * LANE:(blk + 1) * LANE]
            pieces[blk] = _nn(seg, lt, HI) + carry
            carry = carry + jnp.sum(seg, axis=1, keepdims=True)
        dlf = (pieces[0] if nb == 1 else jnp.concatenate(pieces, axis=1)).T
        x = f_ref[...] + b_ref[...]
        dfl = dlf * _sigmoid(-x)
        dfl_ref[...] = dfl
        db_ref[...] = jnp.sum(dfl, axis=0, keepdims=True)

    return pl.pallas_call(
        kern, name="fox_gates_bwd",
        out_shape=[jax.ShapeDtypeStruct((T, LANE), F32), jax.ShapeDtypeStruct((1, LANE), F32)],
        grid=(1,),
        in_specs=[pl.BlockSpec((SUB, T), lambda i: (0, 0)), pl.BlockSpec((T, LANE), lambda i: (0, f_cb)),
                  pl.BlockSpec((1, LANE), lambda i: (0, 0))],
        out_specs=[pl.BlockSpec((T, LANE), lambda i: (0, 0)), pl.BlockSpec((1, LANE), lambda i: (0, 0))],
        compiler_params=_cp(("arbitrary",)),
    )(dc, u, bias_pad)


ATTN_HEADS = 4


def _attn_block(T):
    return 256 if T % 256 == 0 and T >= 512 else 128


def _attn_fwd(u, c3, T):
    H, HP = 8, ATTN_HEADS
    bq = _attn_block(T)
    nq = T // bq
    scale = FOX_HD ** -0.5
    lanes = [slice(h * LANE, (h + 1) * LANE) for h in range(HP)]

    def kern(q_ref, k_ref, v_ref, z_ref, cq_ref, ck_ref, o_ref, oa_ref, lse_ref):
        i = pl.program_id(1)
        q = [(q_ref[:, ln] * scale).astype(BF16) for ln in lanes]
        c0 = [cq_ref[h][:, 0:1] for h in range(HP)]

        def step(j, carry, diagonal=False):
            off = pl.multiple_of(j * bq, bq)
            s = [_nt(q[h], k_ref[pl.ds(off, bq), lanes[h]].astype(BF16)) + (c0[h] - ck_ref[h, :, pl.ds(off, bq)])
                 for h in range(HP)]
            ps, out = [], []
            for h in range(HP):
                m, l, acc = carry[h]
                sh = s[h]
                if diagonal:
                    sh = jnp.where(_iota2((bq, bq), 1) <= _iota2((bq, bq), 0), sh, NEG)
                m_new = jnp.maximum(m, jnp.max(sh, axis=1, keepdims=True))
                p = jnp.exp(sh - m_new)
                alpha = jnp.exp(m - m_new)
                ps.append(p.astype(BF16))
                out.append((m_new, alpha * l + jnp.sum(p, axis=1, keepdims=True), alpha * acc))
            return tuple((m, l, acc + _nn(ps[h], v_ref[pl.ds(off, bq), lanes[h]].astype(BF16)))
                         for h, (m, l, acc) in enumerate(out))

        init = tuple((jnp.full((bq, 1), NEG, F32), jnp.zeros((bq, 1), F32), jnp.zeros((bq, FOX_HD), F32)) for _ in range(HP))
        res = step(i, lax.fori_loop(0, i, step, init), diagonal=True)
        for h, (m, l, acc) in enumerate(res):
            o = acc / l
            z = z_ref[:, lanes[h]]
            o_ref[:, lanes[h]] = o
            oa_ref[:, lanes[h]] = (o * z * _sigmoid(z)).astype(BF16)
            lse_ref[h] = m + jnp.log(l)

    W = HP * LANE
    return pl.pallas_call(
        kern, name="fox_attn_fwd",
        out_shape=[jax.ShapeDtypeStruct((T, 1024), F32), jax.ShapeDtypeStruct((T, 1024), BF16),
                   jax.ShapeDtypeStruct((H, T, 1), F32)],
        grid=(H // HP, nq),
        in_specs=[pl.BlockSpec((bq, W), lambda g, i: (i, g)),
                  pl.BlockSpec((T, W), lambda g, i: (0, 8 // HP + g)),
                  pl.BlockSpec((T, W), lambda g, i: (0, 16 // HP + g)),
                  pl.BlockSpec((bq, W), lambda g, i: (i, 24 // HP + g)),
                  pl.BlockSpec((HP, 1, bq), lambda g, i: (g, 0, i)),
                  pl.BlockSpec((HP, 1, T), lambda g, i: (g, 0, 0))],
        out_specs=[pl.BlockSpec((bq, W), lambda g, i: (i, g)),
                   pl.BlockSpec((bq, W), lambda g, i: (i, g)),
                   pl.BlockSpec((HP, bq, 1), lambda g, i: (g, i, 0))],
        compiler_params=_cp(("parallel", "arbitrary")),
    )(u, u, u, u, c3, c3)


def _attn_probs(s, lse_i, diagonal):
    if not diagonal:
        return jnp.exp(s - lse_i)
    keep = _iota2(s.shape, 1) <= _iota2(s.shape, 0)
    return jnp.where(keep, jnp.exp(jnp.where(keep, s, NEG) - lse_i), 0.0)


def _attn_delta(u, c3, o, lse, doa, T):
    H, HP = 8, ATTN_HEADS
    bq = _attn_block(T)
    nq = T // bq
    scale = FOX_HD ** -0.5
    lanes = [slice(h * LANE, (h + 1) * LANE) for h in range(HP)]

    def kern(q_ref, k_ref, v_ref, z_ref, c_ref, o_ref, lse_ref, doa_ref, do_ref, dz_ref, dl_ref):
        i = pl.program_id(1)
        z = z_ref[...]
        sg = _sigmoid(z)
        dov = doa_ref[...]
        do_all = (dov * z * sg).astype(BF16)
        do_ref[...] = do_all
        dz_ref[...] = dov * o_ref[...] * (sg * (1.0 + z * (1.0 - sg)))
        qs = [(q_ref[:, ln] * scale).astype(BF16) for ln in lanes]
        dob = [do_all[:, ln] for ln in lanes]
        ioff = pl.multiple_of(i * bq, bq)
        c0 = [c_ref[h, :, pl.ds(ioff, bq)][:, 0:1] for h in range(HP)]
        lse_i = [lse_ref[h] for h in range(HP)]

        def step(j, acc, diagonal=False):
            off = pl.multiple_of(j * bq, bq)
            s = [_nt(qs[h], k_ref[pl.ds(off, bq), lanes[h]].astype(BF16)) + (c0[h] - c_ref[h, :, pl.ds(off, bq)])
                 for h in range(HP)]
            dp = [_nt(dob[h], v_ref[pl.ds(off, bq), lanes[h]].astype(BF16)) for h in range(HP)]
            return tuple(acc[h] + jnp.sum(_attn_probs(s[h], lse_i[h], diagonal) * dp[h], axis=1, keepdims=True)
                         for h in range(HP))

        init = tuple(jnp.zeros((bq, 1), F32) for _ in range(HP))
        res = step(i, lax.fori_loop(0, i, step, init), diagonal=True)
        for h in range(HP):
            dl_ref[h] = res[h]

    W = HP * LANE
    full = lambda cb: pl.BlockSpec((T, W), lambda g, i: (0, cb // HP + g))
    blk = lambda cb: pl.BlockSpec((bq, W), lambda g, i: (i, cb // HP + g))
    return pl.pallas_call(
        kern, name="fox_attn_delta",
        out_shape=[jax.ShapeDtypeStruct((T, 1024), BF16), jax.ShapeDtypeStruct((T, 1024), F32), jax.ShapeDtypeStruct((H, T, 1), F32)],
        grid=(H // HP, nq),
        in_specs=[blk(0), full(8), full(16), blk(24),
                  pl.BlockSpec((HP, 1, T), lambda g, i: (g, 0, 0)),
                  blk(0),
                  pl.BlockSpec((HP, bq, 1), lambda g, i: (g, i, 0)),
                  blk(0)],
        out_specs=[blk(0), blk(0), pl.BlockSpec((HP, bq, 1), lambda g, i: (g, i, 0))],
        compiler_params=_cp(("parallel", "arbitrary")),
    )(u, u, u, u, c3, o, lse, doa)


def _attn_bwd(u, c3, lse, do, delta, T):
    H, HP = 8, ATTN_HEADS
    bq = _attn_block(T)
    nq = T // bq
    scale = FOX_HD ** -0.5
    lanes = [slice(h * LANE, (h + 1) * LANE) for h in range(HP)]

    def kern(q_ref, k_ref, v_ref, c_ref, lse_ref, do_ref, dl_ref, dq_ref, dk_ref, dv_ref, dc_ref):
        j = pl.program_id(1)

        @pl.when(j == 0)
        def _():
            dq_ref[...] = jnp.zeros(dq_ref.shape, F32)

        kj = [k_ref[:, ln].astype(BF16) for ln in lanes]
        vj = [v_ref[:, ln].astype(BF16) for ln in lanes]
        joff = pl.multiple_of(j * bq, bq)
        ck = [c_ref[h, :, pl.ds(joff, bq)] for h in range(HP)]

        def step(i, carry, diagonal=False):
            off = pl.multiple_of(i * bq, bq)
            qs = [(q_ref[pl.ds(off, bq), ln] * scale).astype(BF16) for ln in lanes]
            dob = [do_ref[pl.ds(off, bq), ln] for ln in lanes]
            s = [_nt(qs[h], kj[h]) + (c_ref[h, :, pl.ds(off, bq)][:, 0:1] - ck[h]) for h in range(HP)]
            dp = [_nt(dob[h], vj[h]) for h in range(HP)]
            pb, dsb, dcs = [], [], []
            for h in range(HP):
                p = _attn_probs(s[h], lse_ref[h, pl.ds(off, bq), :], diagonal)
                ds = p * (dp[h] - dl_ref[h, pl.ds(off, bq), :])
                pb.append(p.astype(BF16))
                dsb.append(ds.astype(BF16))
                dcs.append(jnp.sum(ds, axis=0, keepdims=True))
            out = []
            for h, (dk, dv, dc) in enumerate(carry):
                dq_ref[pl.ds(off, bq), lanes[h]] += _nn(dsb[h], kj[h]) * scale
                out.append((dk + _tn(dsb[h], qs[h]), dv + _tn(pb[h], dob[h]), dc - dcs[h]))
            return tuple(out)

        init = tuple((jnp.zeros((bq, FOX_HD), F32), jnp.zeros((bq, FOX_HD), F32), jnp.zeros((1, bq), F32)) for _ in range(HP))
        res = lax.fori_loop(j + 1, nq, step, step(j, init, diagonal=True))
        for h, (dk, dv, dc) in enumerate(res):
            dk_ref[:, lanes[h]] = dk
            dv_ref[:, lanes[h]] = dv
            dc_ref[h] = dc

    W = HP * LANE
    full = lambda cb: pl.BlockSpec((T, W), lambda g, j: (0, cb // HP + g))
    blk = lambda cb: pl.BlockSpec((bq, W), lambda g, j: (j, cb // HP + g))
    col1 = pl.BlockSpec((HP, T, 1), lambda g, j: (g, 0, 0))
    return pl.pallas_call(
        kern, name="fox_attn_bwd",
        out_shape=[jax.ShapeDtypeStruct((T, 1024), F32)] * 3 + [jax.ShapeDtypeStruct((H, 1, T), F32)],
        grid=(H // HP, nq),
        in_specs=[full(0), blk(8), blk(16), pl.BlockSpec((HP, 1, T), lambda g, j: (g, 0, 0)), col1, full(0), col1],
        out_specs=[full(0), blk(0), blk(0), pl.BlockSpec((HP, 1, bq), lambda g, j: (g, 0, j))],
        compiler_params=_cp(("parallel", "arbitrary")),
    )(u, u, u, c3, lse, do, delta)


def _place():
    return lax.axis_index("x"), lax.axis_index("y"), lax.axis_index("c")


def _slot(p):
    return 4 * p[0] + 2 * p[1] + p[2]


def _other_chips(x, y):
    return [(1 - x, y), (x, 1 - y), (1 - x, 1 - y)]


def _allgather_steps(in_refs, out_refs, scratch):
    (src,), (dst,) = in_refs, out_refs
    send_sems, recv_sems, local_sem = scratch
    x, y, c = _place()
    me, sibling = (x, y, c), (x, y, 1 - c)
    chips = _other_chips(x, y)

    def copy(k, block, to, from_input=False):
        d = dst.at[_slot(block)]
        return pltpu.make_async_remote_copy(
            src_ref=src if from_input else d, dst_ref=d, send_sem=send_sems.at[k], recv_sem=recv_sems.at[k],
            device_id=to, device_id_type=MESH)

    def first_copies():
        return [copy(0, me, sibling, True)] + [copy(1 + j, me, (*chip, c), True) for j, chip in enumerate(chips)]

    def start():
        pltpu.make_async_copy(src, dst.at[_slot(me)], local_sem).start()
        for cp in first_copies():
            cp.start()

    def finish():
        passed = []
        for j, chip in enumerate(chips):
            copy(1 + j, (*chip, c), me).wait_recv()
            passed.append(copy(4 + j, (*chip, c), sibling))
            passed[-1].start()
        copy(0, sibling, me).wait_recv()
        for j, chip in enumerate(chips):
            copy(4 + j, (*chip, 1 - c), me).wait_recv()
        for cp in first_copies() + passed:
            cp.wait_send()
        pltpu.make_async_copy(src, dst.at[_slot(me)], local_sem).wait()

    return start, finish


def _allgather_side(a):
    return ((a,), (jax.ShapeDtypeStruct((N_DEV,) + a.shape, a.dtype),),
            (pltpu.SemaphoreType.DMA((7,)), pltpu.SemaphoreType.DMA((7,)), pltpu.SemaphoreType.DMA), _allgather_steps)


def _allgather(a, name):
    ins, outs, scratch, make = _allgather_side(a)

    def body(a_ref, o_ref, *scr):
        start, finish = make((a_ref,), (o_ref,), scr)
        start()
        finish()

    any_spec = pl.BlockSpec(memory_space=pl.ANY)
    return pl.pallas_call(body, name=name, out_shape=outs[0], in_specs=[any_spec], out_specs=any_spec,
                          scratch_shapes=list(scratch))(a)


def _exchange_pair(ga, gb):
    def body(ga_ref, gb_ref, ra_ref, rb_ref, send_sems, recv_sems):
        x, y, c = _place()
        sibling = (x, y, 1 - c)
        slots = [_slot(sibling)] + [_slot((*chip, 1 - c)) for chip in _other_chips(x, y)]
        cps = []
        for t, (src, dst) in enumerate(((ga_ref, ra_ref), (gb_ref, rb_ref))):
            for k, ps in enumerate(slots):
                cps.append(pltpu.make_async_remote_copy(
                    src_ref=src.at[ps], dst_ref=dst.at[k], send_sem=send_sems.at[t, k], recv_sem=recv_sems.at[t, k],
                    device_id=sibling, device_id_type=MESH))
        for cp in cps:
            cp.start()
        for cp in cps:
            cp.wait()

    any_spec = pl.BlockSpec(memory_space=pl.ANY)
    return pl.pallas_call(
        body, name="exchange_pair",
        out_shape=[jax.ShapeDtypeStruct((4,) + ga.shape[1:], ga.dtype), jax.ShapeDtypeStruct((4,) + gb.shape[1:], gb.dtype)],
        in_specs=[any_spec] * 2,
        out_specs=[any_spec] * 2,
        scratch_shapes=[pltpu.SemaphoreType.DMA((2, 4)), pltpu.SemaphoreType.DMA((2, 4))],
    )(ga, gb)


def _tiling(R, Cc, tile, by_cols):
    if by_cols:
        assert Cc % tile == 0
        return Cc // tile, (R, tile), lambda lead, i: (lead, 0, i)
    assert R % tile == 0
    return R // tile, (tile, Cc), lambda lead, i: (lead, i, 0)


def _pair_add(name, g, r1, slots, tile, by_cols=False):
    _, R, Cc = g.shape
    steps, blk, at = _tiling(R, Cc, tile, by_cols)

    def kern(s_ref, a_ref, b_ref, o_ref):
        o_ref[...] = (a_ref[...].astype(F32) + b_ref[...].astype(F32)).astype(o_ref.dtype)

    return pl.pallas_call(
        kern, name=name,
        out_shape=jax.ShapeDtypeStruct((3, R, Cc), BF16),
        grid_spec=pltpu.PrefetchScalarGridSpec(
            num_scalar_prefetch=1, grid=(3, steps),
            in_specs=[pl.BlockSpec((1,) + blk, lambda j, i, s: at(s[j], i)),
                      pl.BlockSpec((1,) + blk, lambda j, i, s: at(1 + j, i))],
            out_specs=pl.BlockSpec((1,) + blk, lambda j, i, s: at(j, i))),
        compiler_params=_cp(("arbitrary", "arbitrary")),
    )(slots, g, r1)


def _exchange_ici_steps(in_refs, out_refs, scratch):
    pairs = list(zip(in_refs, out_refs))
    send_sems, recv_sems = scratch
    x, y, c = _place()

    def copies():
        return [pltpu.make_async_remote_copy(
            src_ref=src.at[j], dst_ref=dst.at[j], send_sem=send_sems.at[t, j], recv_sem=recv_sems.at[t, j],
            device_id=(*chip, c), device_id_type=MESH)
            for j, chip in enumerate(_other_chips(x, y)) for t, (src, dst) in enumerate(pairs)]

    def start():
        for cp in copies():
            cp.start()

    def finish():
        for cp in copies():
            cp.wait()

    return start, finish


def _exchange_ici_side(sa, sb):
    return ((sa, sb), (jax.ShapeDtypeStruct(sa.shape, sa.dtype), jax.ShapeDtypeStruct(sb.shape, sb.dtype)),
            (pltpu.SemaphoreType.DMA((2, 3)), pltpu.SemaphoreType.DMA((2, 3))), _exchange_ici_steps)


def _adamw(name, w, m, v, parts, tile, by_cols=False):
    R, Cc = w.shape
    steps, blk_shape, at = _tiling(R, Cc, tile, by_cols)
    n_parts = len(parts)

    def kern(*refs):
        w_ref, m_ref, v_ref = refs[:3]
        g = None
        for r_ref, (_, n) in zip(refs[3:3 + n_parts], parts):
            for s in range(n):
                term = r_ref[s].astype(F32)
                g = term if g is None else g + term
        g_out, d_out, m_out, v_out = refs[3 + n_parts:]
        mn = ADAM_B1 * m_ref[...] + (1.0 - ADAM_B1) * g
        vn = ADAM_B2 * v_ref[...] + (1.0 - ADAM_B2) * (g * g)
        m_hat = mn / (1.0 - ADAM_B1 ** ADAM_STEP)
        v_hat = vn / (1.0 - ADAM_B2 ** ADAM_STEP)
        g_out[...] = g
        d_out[...] = -ADAM_LR * (m_hat / (jnp.sqrt(v_hat) + ADAM_EPS) + ADAM_WD * w_ref[...])
        m_out[...] = mn
        v_out[...] = vn

    blk = pl.BlockSpec(blk_shape, lambda i: at(0, i)[1:])
    return pl.pallas_call(
        kern, name=name,
        out_shape=[jax.ShapeDtypeStruct((R, Cc), F32)] * 4,
        grid=(steps,),
        in_specs=[blk] * 3 + [pl.BlockSpec((n,) + blk_shape, lambda i: at(0, i)) for (_, n) in parts],
        out_specs=[blk] * 4,
        compiler_params=_cp(("arbitrary",)),
    )(w, m, v, *[a for (a, _) in parts])


def _pad_cols(a, w):
    return jnp.pad(a, ((0, 0), (0, w - a.shape[1])))


def _pad_rows(a, r):
    return jnp.pad(a, ((0, r - a.shape[0]), (0, 0)))


def _pack_b(pf, pr, wo, w2, a2, rows):
    body = jnp.concatenate([pf, pr, wo.reshape(2048, 256), jnp.concatenate([w2, a2], axis=1)], axis=0)
    return _pad_rows(body, rows)


def kernel(x, norm_gain, w_in, fox_forget_bias, rwkv_shift_mix, rwkv_w0, rwkv_w2, rwkv_a0, rwkv_a2, rwkv_k_k, rwkv_k_a, rwkv_r_k, rwkv_ln_w, rwkv_ln_b, w_proj_fox, w_proj_rwkv, w_out, final_norm_gain, loss_target, m_norm_gain, m_w_in, m_fox_forget_bias, m_rwkv_shift_mix, m_rwkv_w0, m_rwkv_w2, m_rwkv_a0, m_rwkv_a2, m_rwkv_k_k, m_rwkv_k_a, m_rwkv_r_k, m_rwkv_ln_w, m_rwkv_ln_b, m_w_proj_fox, m_w_proj_rwkv, m_w_out, m_final_norm_gain, v_norm_gain, v_w_in, v_fox_forget_bias, v_rwkv_shift_mix, v_rwkv_w0, v_rwkv_w2, v_rwkv_a0, v_rwkv_a2, v_rwkv_k_k, v_rwkv_k_a, v_rwkv_r_k, v_rwkv_ln_w, v_rwkv_ln_b, v_w_proj_fox, v_w_proj_rwkv, v_w_out, v_final_norm_gain):
    T, D = x.shape[1], x.shape[2]
    assert D == 2048 and T % LANE == 0
    NI = w_in.shape[2]
    IN = N_DEV * NI
    RB = 4224
    x2 = x[0]
    lt2 = loss_target[0]
    me = _slot(_place())

    wa = _allgather(w_in[0].T.astype(BF16), "allgather_w_in")
    packed_own = _pack_b(w_proj_fox[0], w_proj_rwkv[0], w_out[0], rwkv_w2[0], rwkv_a2[0], RB).astype(BF16)
    sections = [(0, 4096, 0), (4104, 4096, 4096), (8392, 4096, 8192), (4096, 8, 12288), (8200, 96, 12544), (8296, 96, 12672)]
    NP = 12800
    pieces, at_col = [], 0
    for lo, width, pad_lo in sections:
        if pad_lo > at_col:
            pieces.append(jnp.zeros((pad_lo - at_col, D), BF16))
        col = lo
        while col < lo + width:
            d = col // NI
            stop = min(lo + width, (d + 1) * NI)
            pieces.append(wa[d, col - d * NI:stop - d * NI, :])
            col = stop
        at_col = pad_lo + width
    pieces.append(jnp.zeros((NP - at_col, D), BF16))
    w_pad_t = jnp.concatenate(pieces, axis=0)
    F_CB, LORA_CB = 96, 49

    mu = rwkv_shift_mix
    mu_main = mu[:, 0:4096]
    mu_lora = jnp.concatenate([_pad_cols(mu[:, 4096:4192], LANE), _pad_cols(mu[:, 4192:4288], LANE)], axis=1)
    bias_pad = _pad_cols(fox_forget_bias, LANE)
    rk_flat = rwkv_r_k.reshape(1, 1024)
    gf = final_norm_gain.reshape(1, D)

    tb = min(256, T)
    tbh = min(128, T)
    (h,) = _rowcall("rms_fwd", T, tb, [_rows(x2, tb, D), _whole(norm_gain)],
                    lambda i, xv, g: ([_rms_math(xv, g)], []), [(D, BF16)])
    u, wb = _mm(h, w_pad_t, tb=True, tm=1024, tn=1280, name="mm_in", side=_allgather_side(packed_own))
    wpf = wb[:, 0:1024, :].transpose(1, 0, 2).reshape(1024, D)
    wpr = wb[:, 1024:2048, :].transpose(1, 0, 2).reshape(1024, D)
    wo = wb[:, 2048:4096, :].reshape(N_DEV * 256, D)
    w2p = _pad_rows(wb[:, 4096:4192, 0:128].transpose(1, 0, 2).reshape(96, 1024).astype(F32), LANE)
    a2p = _pad_rows(wb[:, 4096:4192, 128:256].transpose(1, 0, 2).reshape(96, 1024).astype(F32), LANE)

    c8 = _gates_fwd(u, bias_pad, T, F_CB)
    c3 = c8.reshape(8, 1, T)
    o_raw, o_a, lse = _attn_fwd(u, c3, T)

    def shift_body(i, um, hm, ul, hl, mm_, ml):
        outs = []
        for uv, hv, mv in ((um, hm, mm_), (ul, hl, ml)):
            hv = jnp.where(i == 0, 0.0, hv)
            prev = pltpu.roll(jnp.concatenate([hv, uv], axis=0), 1, 0)[SUB:]
            outs.append(uv + (prev - uv) * mv)
        return outs, []

    def halo_prev(arr, w, cb):
        return (arr, (SUB, w), lambda i: (jnp.maximum(i * (tbh // SUB) - 1, 0), cb))

    xs, xl = _rowcall("rwkv_shift_fwd", T, tbh,
                      [_rows(u, tbh, 4096, 1), halo_prev(u, 4096, 1), _rows(u, tbh, 256, LORA_CB), halo_prev(u, 256, LORA_CB),
                       _whole(mu_main), _whole(mu_lora)],
                      shift_body, [(4096, F32), (256, F32)])

    prep_par = [_whole(rwkv_w0), _whole(rwkv_a0), _whole(rwkv_k_k), _whole(rwkv_k_a), _whole(w2p), _whole(a2p)]
    prep_rows = [_rows(xs, tbh, 1024, 1), _rows(xl, tbh, LANE, 0), _rows(xl, tbh, LANE, 1)]
    lw, kp, an, bb = _rowcall("rwkv_prep_fwd", T, tbh, prep_rows + prep_par,
                              lambda i, *a: (list(_prep_math(*a)), []), [(1024, F32)] * 4)
    y, states = _scan_fwd(xs, lw, kp, an, bb, T)
    post_rows = [_rows(y, tbh, 1024), _rows(xs, tbh, 1024, 0), _rows(kp, tbh, 1024), _rows(xs, tbh, 1024, 2), _rows(xs, tbh, 1024, 3)]
    post_par = [_whole(rwkv_ln_w), _whole(rwkv_ln_b), _whole(rk_flat)]
    (o_b,) = _rowcall("rwkv_post_fwd", T, tbh, post_rows + post_par,
                      lambda i, *a: ([_post_math(*a)], []), [(1024, BF16)])

    pa = _mm(o_a, wpf, name="mm_proj_fox")
    pb = _mm(o_b, wpr, name="mm_proj_rwkv")
    merge_rows = [_rows(u, tb, D, 4), _rows(u, tb, D, 5), _rows(pa, tb, D), _rows(pb, tb, D)]
    (mg,) = _rowcall("merge_fwd", T, tb, merge_rows, lambda i, *a: ([_merge_math(*a)], []), [(D, BF16)])
    mo = _mm(mg, wo, name="mm_out")

    def head_body(i, xv, mov, ltv, g):
        out = xv + mov
        r = lax.rsqrt(jnp.mean(out * out, axis=-1, keepdims=True) + RMS_EPS)
        yn = out * r
        err = yn * g - ltv
        loss = 0.5 * jnp.sum(jnp.sum(err * err, axis=-1, keepdims=True), axis=0, keepdims=True) / D
        dyv = err / D
        dyn = dyv * g
        dout = r * (dyn - yn * jnp.mean(dyn * yn, axis=-1, keepdims=True))
        return [dout], [loss, jnp.sum(dyv * yn, axis=0, keepdims=True)]

    dout, loss_p, dgf_p = _rowcall("loss_head", T, tb, [_rows(x2, tb, D), _rows(mo, tb, D), _rows(lt2, tb, D), _whole(gf)],
                                   head_body, [(D, F32)], [(1, 1), (1, D)])

    dm = _mm(dout, wo, tb=True, name="mm_out_dx")
    dwo = _mm(mg.T, dout, out_dtype=BF16, name="mm_out_dw")

    def merge_bwd_body(i, ga, gb, pav, pbv, dmv):
        _, vjp = jax.vjp(_merge_math, ga, gb, pav, pbv)
        dga, dgb, dpa, dpb = vjp(dmv)
        return [dga, dgb, dpa, dpb], []

    dga, dgb, dpa, dpb = _rowcall("merge_bwd", T, tb, merge_rows + [_rows(dm, tb, D)], merge_bwd_body,
                                  [(D, BF16), (D, BF16), (D, BF16), (D, BF16)])
    doa = _mm(dpa, wpf, tb=True, name="mm_proj_fox_dx")
    dwpf = _mm(o_a.T, dpa, out_dtype=BF16, name="mm_proj_fox_dw")
    dob = _mm(dpb, wpr, tb=True, name="mm_proj_rwkv_dx")
    dwpr = _mm(o_b.T, dpb, out_dtype=BF16, name="mm_proj_rwkv_dw")

    do_b, dza, delta = _attn_delta(u, c3, o_raw, lse, doa, T)
    dq, dk, dv, dc3 = _attn_bwd(u, c3, lse, do_b, delta, T)
    dfl, dbias_p = _gates_bwd(dc3.reshape(8, T), u, bias_pad, T, F_CB)

    def post_bwd_body(i, yv, rv, kpv, vv, zv, lnw, lnb, rkv, dobv):
        _, vjp = jax.vjp(_post_math, yv, rv, kpv, vv, zv, lnw, lnb, rkv)
        dy_, dr_, dkp_, dv_, dz_, dlnw, dlnb, drk = vjp(dobv)
        return [dy_, dr_, dkp_, dv_, dz_], [dlnw, dlnb, drk]

    dy_s, dr_p, dkp_p, dv_p, dzb, dlnw_p, dlnb_p, drk_p = _rowcall(
        "rwkv_post_bwd", T, tbh, post_rows + post_par + [_rows(dob, tbh, 1024)], post_bwd_body,
        [(1024, F32)] * 5, [(1, 1024)] * 3)
    dr_s, dlw, dkp_s, dv_s, dan, dbb = _scan_bwd(xs, lw, kp, an, bb, states, dy_s, T)

    def prep_bwd_body(i, xk, xwd, xad, w0, a0, kkw, kaw, w2v, a2v, dlw_, dkp1, dkp2, dan_, dbb_, dr1, dr2, dv1, dv2, dz_):
        _, vjp = jax.vjp(_prep_math, xk, xwd, xad, w0, a0, kkw, kaw, w2v, a2v)
        dxk, dxwd, dxad, dw0, da0, dkk, dka, dw2, da2 = vjp((dlw_, dkp1 + dkp2, dan_, dbb_))
        return [[dr1 + dr2, dxk, dv1 + dv2, dz_], [dxwd, dxad]], [dw0, da0, dkk, dka, dw2, da2]

    cots = [dlw, dkp_s, dkp_p, dan, dbb, dr_s, dr_p, dv_s, dv_p, dzb]
    dxs, dxl, dw0_p, da0_p, dkk_p, dka_p, dw2_p, da2_p = _rowcall(
        "rwkv_prep_bwd", T, tbh, prep_rows + prep_par + [_rows(c_, tbh, 1024) for c_ in cots], prep_bwd_body,
        [(4096, F32), (256, F32)], [(1, 1024)] * 4 + [(LANE, 1024)] * 2)

    def shift_bwd_body(i, dm_, hm, dl_, hl, um, pm, ul, pl_, mm_, ml):
        last = i == T // tbh - 1
        outs, accs = [], []
        for dv_, hv, uv, pv, mv in ((dm_, hm, um, pm, mm_), (dl_, hl, ul, pl_, ml)):
            hv = jnp.where(last, 0.0, hv)
            nxt = pltpu.roll(jnp.concatenate([dv_, hv], axis=0), tbh + SUB - 1, 0)[:tbh]
            pv = jnp.where(i == 0, 0.0, pv)
            prev = pltpu.roll(jnp.concatenate([pv, uv], axis=0), 1, 0)[SUB:]
            outs.append(dv_ * (1.0 - mv) + nxt * mv)
            accs.append(jnp.sum(dv_ * (prev - uv), axis=0, keepdims=True))
        return outs, accs

    def halo_next(arr, w, cb):
        last_blk = T // SUB - 1
        return (arr, (SUB, w), lambda i: (jnp.minimum((i + 1) * (tbh // SUB), last_blk), cb))

    du_b, du_l, dmu_main_p, dmu_lora_p = _rowcall(
        "rwkv_shift_bwd", T, tbh,
        [_rows(dxs, tbh, 4096), halo_next(dxs, 4096, 0), _rows(dxl, tbh, 256), halo_next(dxl, 256, 0),
         _rows(u, tbh, 4096, 1), halo_prev(u, 4096, 1), _rows(u, tbh, 256, LORA_CB), halo_prev(u, 256, LORA_CB),
         _whole(mu_main), _whole(mu_lora)],
        shift_bwd_body, [(4096, BF16), (256, BF16)], [(1, 4096), (1, 256)])

    du = jnp.concatenate([dq.astype(BF16), dk.astype(BF16), dv.astype(BF16), dza.astype(BF16), du_b, dga, dgb,
                          dfl.astype(BF16), jnp.zeros((T, LANE), BF16), du_l], axis=1)
    dw_pad_t = _mm(du, h, ta=True, out_dtype=BF16, tm=1280, tn=1024, name="mm_in_dw")

    by_col = sorted(sections)
    blocks = []
    for d in range(N_DEV):
        parts = []
        for lo, width, pad_lo in by_col:
            a, b = max(lo, d * NI), min(lo + width, (d + 1) * NI)
            if a < b:
                parts.append(dw_pad_t[pad_lo + a - lo:pad_lo + b - lo, :])
        blocks.append(jnp.concatenate(parts, axis=0)[None])
    ga = jnp.concatenate(blocks, axis=0)
    lora_g = jnp.concatenate([dw2_p[:96].reshape(96, N_DEV, 128).transpose(1, 0, 2),
                              da2_p[:96].reshape(96, N_DEV, 128).transpose(1, 0, 2)], axis=2).astype(BF16)
    gb = jnp.concatenate([dwpf.reshape(1024, N_DEV, 256).transpose(1, 0, 2),
                          dwpr.reshape(1024, N_DEV, 256).transpose(1, 0, 2),
                          dwo.reshape(N_DEV, 2048, 256), lora_g, jnp.zeros((N_DEV, RB - 4192, 256), BF16)], axis=1)

    xx, yy, cc = _place()
    chip_slots = jnp.stack([_slot((*chip, cc)) for chip in _other_chips(xx, yy)]).astype(jnp.int32)
    r1a, r1b = _exchange_pair(ga, gb)
    sa = _pair_add("pair_add_w_in", ga, r1a, chip_slots, 256, by_cols=True)
    sb = _pair_add("pair_add_packed", gb, r1b, chip_slots, 384)
    ga_own = lax.dynamic_index_in_dim(ga, me, 0, keepdims=True)
    gb_own = lax.dynamic_index_in_dim(gb, me, 0, keepdims=True)
    dh, r2a, r2b = _mm(du, w_pad_t, tm=1024, tn=1024, tk=NP // 10, name="mm_in_dx", side=_exchange_ici_side(sa, sb))

    def rms_bwd_body(i, xv, g, dhv, doutv):
        _, vjp = jax.vjp(_rms_math, xv, g)
        dx_, dg_ = vjp(dhv)
        return [dx_ + doutv], [dg_]

    grad_x2, dng_p = _rowcall("rms_bwd", T, tb, [_rows(x2, tb, D), _whole(norm_gain), _rows(dh, tb, D), _rows(dout, tb, D)],
                              rms_bwd_body, [(D, F32)], [(1, D)])

    dmu = jnp.concatenate([dmu_main_p, dmu_lora_p[:, 0:96], dmu_lora_p[:, 128:224]], axis=1)
    small_parts = [dng_p, dbias_p[:, 0:8], dmu, dw0_p, da0_p, dkk_p, dka_p, drk_p, dlnw_p, dlnb_p, dgf_p, loss_p]
    SR = 128
    small = _pad_cols(jnp.concatenate(small_parts, axis=1), SR * LANE).reshape(SR, LANE)
    rs = _allgather(small, "allgather_small")

    w_in_outs = _adamw("adamw_w_in", w_in[0].T, m_w_in[0].T, v_w_in[0].T, [(ga_own, 1), (r1a, 1), (r2a, 3)], 256, by_cols=True)
    g_in, d_in, m_in, v_in = [o.T for o in w_in_outs]
    pk = lambda pf, pr, wo_, w2_, a2_: _pack_b(pf[0], pr[0], wo_[0], w2_[0], a2_[0], RB)
    outs_b = _adamw("adamw_packed", pk(w_proj_fox, w_proj_rwkv, w_out, rwkv_w2, rwkv_a2),
                    pk(m_w_proj_fox, m_w_proj_rwkv, m_w_out, m_rwkv_w2, m_rwkv_a2),
                    pk(v_w_proj_fox, v_w_proj_rwkv, v_w_out, v_rwkv_w2, v_rwkv_a2), [(gb_own, 1), (r1b, 1), (r2b, 3)], 384)

    def pack_small(ng, fb, sm, w0, a0, kk_, ka_, rk_, lnw, lnb, fg):
        parts = [ng, fb, sm, w0, a0, kk_, ka_, rk_.reshape(1, 1024), lnw, lnb, fg.reshape(1, D), jnp.zeros((1, 1), F32)]
        return _pad_cols(jnp.concatenate(parts, axis=1), SR * LANE).reshape(SR, LANE)

    outs_s = _adamw("adamw_small",
                    pack_small(norm_gain, fox_forget_bias, rwkv_shift_mix, rwkv_w0, rwkv_a0, rwkv_k_k, rwkv_k_a, rwkv_r_k,
                               rwkv_ln_w, rwkv_ln_b, final_norm_gain),
                    pack_small(m_norm_gain, m_fox_forget_bias, m_rwkv_shift_mix, m_rwkv_w0, m_rwkv_a0, m_rwkv_k_k, m_rwkv_k_a,
                               m_rwkv_r_k, m_rwkv_ln_w, m_rwkv_ln_b, m_final_norm_gain),
                    pack_small(v_norm_gain, v_fox_forget_bias, v_rwkv_shift_mix, v_rwkv_w0, v_rwkv_a0, v_rwkv_k_k, v_rwkv_k_a,
                               v_rwkv_r_k, v_rwkv_ln_w, v_rwkv_ln_b, v_final_norm_gain),
                    [(rs, N_DEV)], SR)

    def unpack_b(pkd):
        return dict(w_proj_fox=pkd[0:1024][None], w_proj_rwkv=pkd[1024:2048][None], w_out=pkd[2048:4096].reshape(1, 256, D),
                    rwkv_w2=pkd[4096:4192, 0:128][None], rwkv_a2=pkd[4096:4192, 128:256][None])

    def unpack_s(pkd):
        flat = pkd.reshape(1, SR * LANE)
        names = [("norm_gain", D), ("fox_forget_bias", 8), ("rwkv_shift_mix", 4288), ("rwkv_w0", 1024), ("rwkv_a0", 1024),
                 ("rwkv_k_k", 1024), ("rwkv_k_a", 1024), ("rwkv_r_k", 1024), ("rwkv_ln_w", 1024), ("rwkv_ln_b", 1024),
                 ("final_norm_gain", D), ("loss", 1)]
        out, off = {}, 0
        for nm, n in names:
            out[nm] = flat[:, off:off + n]
            off += n
        out["rwkv_r_k"] = out["rwkv_r_k"].reshape(1, 16, 64)
        out["final_norm_gain"] = out["final_norm_gain"].reshape(D)
        return out

    order = ["norm_gain", "w_in", "fox_forget_bias", "rwkv_shift_mix", "rwkv_w0", "rwkv_w2", "rwkv_a0", "rwkv_a2", "rwkv_k_k",
             "rwkv_k_a", "rwkv_r_k", "rwkv_ln_w", "rwkv_ln_b", "w_proj_fox", "w_proj_rwkv", "w_out", "final_norm_gain"]
    result = []
    loss = None
    for kind, big in enumerate((g_in, d_in, m_in, v_in)):
        d = {**unpack_b(outs_b[kind]), **unpack_s(outs_s[kind]), "w_in": big[None]}
        if kind == 0:
            loss = d["loss"].reshape(())
        result += [d[n] for n in order]
    return (loss, grad_x2[None], *result)
```

```python
import functools

import jax
import jax.numpy as jnp
from jax import lax
from jax.experimental import pallas as pl
from jax.experimental.pallas import tpu as pltpu

F32 = jnp.float32
BF16 = jnp.bfloat16
HI = lax.Precision.HIGHEST
MESH = pl.DeviceIdType.MESH

FOX_HD = 128
RW_HD = 64
RMS_EPS = 1e-6
GN_EPS = 64e-5
L2_EPS = 1e-12
ADAM_LR = 0.001
ADAM_B1 = 0.9
ADAM_B2 = 0.999
ADAM_EPS = 1e-08
ADAM_WD = 0.01
ADAM_STEP = 10

LANE = 128
SUB = 8
VMEM_LIMIT = 56 * 1024 * 1024
N_DEV = 8
CHUNK = 64
SCAN_GROUP = 4
SCAN_PAIRS = 2
PS = None
NEG = -1e30


def _scan_shape(T):
    c = min(CHUNK, T)
    return c, min(SCAN_GROUP, T // c)


def _cp(sem=None):
    return pltpu.CompilerParams(dimension_semantics=sem, vmem_limit_bytes=VMEM_LIMIT)


def _sigmoid(x):
    return jax.nn.sigmoid(x)


def _softplus(x):
    return jnp.maximum(x, 0.0) + jnp.log(1.0 + jnp.exp(-jnp.abs(x)))


def _nn(a, b, prec=None):
    return lax.dot_general(a, b, (((1,), (0,)), ((), ())), precision=prec, preferred_element_type=F32)


def _nt(a, b, prec=None):
    return lax.dot_general(a, b, (((1,), (1,)), ((), ())), precision=prec, preferred_element_type=F32)


def _tn(a, b, prec=None):
    return lax.dot_general(a, b, (((0,), (0,)), ((), ())), precision=prec, preferred_element_type=F32)


def _iota2(shape, dim):
    return lax.broadcasted_iota(jnp.int32, shape, dim)


def _seg_sum(x):
    r = _iota2((LANE, LANE), 0) // RW_HD
    c = _iota2((LANE, LANE), 1) // RW_HD
    bd = (r == c).astype(F32)
    parts = [_nn(x[:, j * LANE:(j + 1) * LANE], bd, HI) for j in range(x.shape[1] // LANE)]
    return parts[0] if len(parts) == 1 else jnp.concatenate(parts, axis=1)


def _mm(a, b, *, ta=False, tb=False, out_dtype=F32, tm=512, tn=512, tk=None, name, side=None):
    assert not (ta and tb)
    K, M = a.shape if ta else a.shape[::-1]
    N = b.shape[0] if tb else b.shape[1]
    tm, tn = min(tm, M), min(tn, N)
    tk = K if tk is None else tk
    nk = K // tk
    assert M % tm == 0 and N % tn == 0 and K % tk == 0
    gi, gj = M // tm, N // tn
    a_spec = pl.BlockSpec((tk, tm), lambda i, j, k: (k, i)) if ta else pl.BlockSpec((tm, tk), lambda i, j, k: (i, k))
    b_spec = pl.BlockSpec((tn, tk), lambda i, j, k: (j, k)) if tb else pl.BlockSpec((tk, tn), lambda i, j, k: (k, j))
    side_ins, side_outs, side_scr, side_make = side if side is not None else ((), (), (), None)
    n_si, n_so = len(side_ins), len(side_outs)
    n_acc = 0 if nk == 1 else 1

    def body(*refs):
        a_ref, b_ref = refs[:2]
        o_ref = refs[2 + n_si]
        scr = refs[3 + n_si + n_so:]
        k = pl.program_id(2)
        if side_make is not None:
            start, finish = side_make(refs[2:2 + n_si], refs[3 + n_si:3 + n_si + n_so], scr[n_acc:])
            first = jnp.logical_and(jnp.logical_and(pl.program_id(0) == 0, pl.program_id(1) == 0), k == 0)
            last = jnp.logical_and(jnp.logical_and(pl.program_id(0) == gi - 1, pl.program_id(1) == gj - 1), k == nk - 1)
            pl.when(first)(start)
        av = a_ref[...].astype(BF16)
        bv = b_ref[...].astype(BF16)
        p = _tn(av, bv) if ta else _nt(av, bv) if tb else _nn(av, bv)
        if nk == 1:
            o_ref[...] = p.astype(out_dtype)
        else:
            acc_ref = scr[0]

            @pl.when(k == 0)
            def _():
                acc_ref[...] = p

            @pl.when(k > 0)
            def _():
                acc_ref[...] += p

            @pl.when(k == nk - 1)
            def _():
                o_ref[...] = acc_ref[...].astype(out_dtype)
        if side_make is not None:
            pl.when(last)(finish)

    any_spec = pl.BlockSpec(memory_space=pl.ANY)
    res = pl.pallas_call(
        body, name=name,
        out_shape=[jax.ShapeDtypeStruct((M, N), out_dtype)] + list(side_outs),
        grid=(gi, gj, nk),
        in_specs=[a_spec, b_spec] + [any_spec] * n_si,
        out_specs=[pl.BlockSpec((tm, tn), lambda i, j, k: (i, j))] + [any_spec] * n_so,
        scratch_shapes=([] if nk == 1 else [pltpu.VMEM((tm, tn), F32)]) + list(side_scr),
        compiler_params=_cp(("arbitrary",) * 3 if side is not None else ("parallel", "parallel", "arbitrary")),
    )(a, b, *side_ins)
    return res if side is not None else res[0]


def _rows(arr, tb, w, cb=0):
    return (arr, (tb, w), lambda i: (i, cb))


def _whole(arr):
    nd = arr.ndim
    return (arr, arr.shape, lambda i: (0,) * nd)


def _rowcall(name, T, tb, ins, body, outs, accs=()):
    n_in, n_out = len(ins), len(outs)

    def kern(*refs):
        i = pl.program_id(0)
        vals = [r[...] for r in refs[:n_in]]
        ro, ao = body(i, *vals)
        for r, v in zip(refs[n_in:n_in + n_out], ro):
            if isinstance(v, (list, tuple)):
                off = 0
                for piece in v:
                    w = piece.shape[1]
                    r[:, off:off + w] = piece.astype(r.dtype)
                    off += w
            else:
                r[...] = v.astype(r.dtype)
        if accs:
            acc_refs = refs[n_in + n_out:]

            @pl.when(i == 0)
            def _():
                for r in acc_refs:
                    r[...] = jnp.zeros(r.shape, F32)

            for r, v in zip(acc_refs, ao):
                r[...] += v

    out_shape = [jax.ShapeDtypeStruct((T, w), dt) for (w, dt) in outs] + [jax.ShapeDtypeStruct(s, F32) for s in accs]
    out_specs = [pl.BlockSpec((tb, w), lambda i: (i, 0)) for (w, dt) in outs] + [pl.BlockSpec(s, lambda i: (0, 0)) for s in accs]
    res = pl.pallas_call(
        kern, name=name,
        out_shape=out_shape,
        grid=(T // tb,),
        in_specs=[pl.BlockSpec(bs, im) for (_, bs, im) in ins],
        out_specs=out_specs,
        compiler_params=_cp(("arbitrary",)),
    )(*[a for (a, _, _) in ins])
    return res


def _rms_math(x, g):
    r = lax.rsqrt(jnp.mean(x * x, axis=-1, keepdims=True) + RMS_EPS)
    return x * r * g


def _merge_math(ga, gb, pa, pb):
    return _sigmoid(ga) * pa + _sigmoid(gb) * pb


def _prep_math(xk, xwd, xad, w0, a0, kk_w, ka_w, w2p, a2p):
    z = w0 + _nn(jnp.tanh(xwd), w2p, HI)
    w = -_softplus(-z) - 0.5
    lw = -jnp.exp(w)
    ag = _sigmoid(a0 + _nn(xad, a2p, HI))
    p = xk * kk_w
    n = jnp.maximum(jnp.sqrt(_seg_sum(p * p)), L2_EPS)
    kk = p / n
    kp = xk * (1.0 + (ag - 1.0) * ka_w)
    return lw, kp, -kk, kk * ag


def _post_math(y, r, kp, v, z, lnw, lnb, rk):
    inv = 1.0 / RW_HD
    mu = _seg_sum(y) * inv
    d = y - mu
    var = _seg_sum(d * d) * inv
    yn = d * lax.rsqrt(var + GN_EPS) * lnw + lnb
    bonus = _seg_sum(r * kp * rk) * v
    return (yn + bonus) * (z * _sigmoid(z))


def _scan_group(s0s, *flat):
    P = len(s0s)
    G = len(flat) // (6 * P)
    ch = [flat[6 * i:6 * i + 6] for i in range(P * G)]
    C = ch[0][0].shape[0]
    C2 = 2 * C
    cat = jnp.concatenate
    m0 = _iota2((1, LANE), 1) < RW_HD
    mask0 = m0.astype(F32)
    mask1 = 1.0 - mask0
    r2 = _iota2((C2, C2), 0)
    c2 = _iota2((C2, C2), 1)
    dist = r2 - c2
    in_head = dist <= r2 % C
    eye = (r2 == c2).astype(F32)
    lower = (_iota2((C, C), 0) >= _iota2((C, C), 1)).astype(F32)
    bd = (_iota2((LANE, LANE), 0) // RW_HD) == (_iota2((LANE, LANE), 1) // RW_HD)

    def tri(m, strict):
        return jnp.where(dist > 0 if strict else dist >= 0, jnp.where(in_head, m, 0.0), 0.0)

    def sel(z):
        return jnp.where(m0, z[:C], z[C:])

    gs = [_nn(lower, c[1], HI) for c in ch]
    pre = []
    for (r, lw, k, v, a, b), g in zip(ch, gs):
        g_end = jnp.sum(lw, axis=0, keepdims=True)
        en = jnp.exp(-g)
        ec = jnp.exp(g_end - g)
        pre.append(dict(at=a * jnp.exp(g - lw), rt=r * jnp.exp(g), bt=b * en, kt=k * en, bh=b * ec, kh=k * ec,
                        dec=jnp.exp(g_end), v=v))
    grams = [_nt(cat([p["at"] * mask0, p["at"] * mask1, p["rt"] * mask0, p["rt"] * mask1], axis=0),
                 cat([p["bt"], p["bt"], p["kt"], p["kt"]], axis=0), PS) for p in pre]
    mab = [tri(gm[:C2, :C2], True) for gm in grams]
    mp = [_nn(m, m, PS) for m in mab]
    tinv = [eye + m for m in mab]
    n = 2
    while n < C:
        last = 2 * n >= C
        for i in range(P * G):
            if last:
                tinv[i] = tinv[i] + _nn(mp[i], tinv[i], PS)
            else:
                z = _nn(mp[i], cat([mp[i], tinv[i]], axis=1), PS)
                mp[i], tinv[i] = z[:, :C2], tinv[i] + z[:, C2:]
        n *= 2
    xv = [sel(_nn(tri(gm[:C2, C2:], True), cat([p["v"], p["v"]], axis=0), PS)) for gm, p in zip(grams, pre)]
    ys, s = [None] * (P * G), list(s0s)
    for i in range(G):
        for q in range(P):
            n = q * G + i
            p, gm = pre[n], grams[n]
            sx = _nt(cat([p["at"], p["rt"]], axis=0), s[q], PS)
            x = sx[:C] + xv[n]
            u = sel(_nn(tinv[n], cat([x, x], axis=0), PS))
            v = p["v"]
            ys[n] = sx[C:] + sel(_nn(cat([tri(gm[C2:, :C2], False), tri(gm[C2:, C2:], False)], axis=1),
                                     cat([u, u, v, v], axis=0), PS))
            s[q] = s[q] * p["dec"] + jnp.where(bd, _tn(cat([u, v], axis=0), cat([p["bh"], p["kh"]], axis=0), PS), 0.0)
    return tuple(ys), tuple(s)


def _scan_fwd(xs, lw, kp, an, bb, T):
    C, G = _scan_shape(T)
    P = SCAN_PAIRS
    nc = T // (C * G)
    npair = 1024 // LANE

    def kern(r_ref, lw_ref, k_ref, v_ref, a_ref, b_ref, y_ref, st_ref, s_scr):
        n = pl.program_id(1)

        @pl.when(n == 0)
        def _():
            s_scr[...] = jnp.zeros(s_scr.shape, F32)

        st_ref[0] = s_scr[...]
        ins = (r_ref, lw_ref, k_ref, v_ref, a_ref, b_ref)
        ys, s1 = _scan_group(tuple(s_scr[q] for q in range(P)),
                             *[ref[i * C:(i + 1) * C, q * LANE:(q + 1) * LANE] for q in range(P) for i in range(G) for ref in ins])
        for q in range(P):
            for i in range(G):
                y_ref[i * C:(i + 1) * C, q * LANE:(q + 1) * LANE] = ys[q * G + i]
            s_scr[q] = s1[q]

    def col(off):
        return pl.BlockSpec((C * G, P * LANE), lambda p, n: (n, off // P + p))

    return pl.pallas_call(
        kern, name="rwkv_scan_fwd",
        out_shape=[jax.ShapeDtypeStruct((T, 1024), F32), jax.ShapeDtypeStruct((nc, npair, LANE, LANE), F32)],
        grid=(npair // P, nc),
        in_specs=[col(0), col(0), col(0), col(16), col(0), col(0)],
        out_specs=[col(0), pl.BlockSpec((1, P, LANE, LANE), lambda p, n: (n, p, 0, 0))],
        scratch_shapes=[pltpu.VMEM((P, LANE, LANE), F32)],
        compiler_params=_cp(("parallel", "arbitrary")),
    )(xs, lw, kp, xs, an, bb)


def _scan_bwd(xs, lw, kp, an, bb, states, dy, T):
    C, G = _scan_shape(T)
    P = SCAN_PAIRS
    nc = T // (C * G)
    npair = 1024 // LANE

    def kern(r_ref, lw_ref, k_ref, v_ref, a_ref, b_ref, st_ref, dy_ref, dr_ref, dlw_ref, dk_ref, dv_ref, da_ref, db_ref, ds_scr):
        n = pl.program_id(1)

        @pl.when(n == 0)
        def _():
            ds_scr[...] = jnp.zeros(ds_scr.shape, F32)

        ins = (r_ref, lw_ref, k_ref, v_ref, a_ref, b_ref)
        units = [(q, i) for q in range(P) for i in range(G)]
        _, vjp = jax.vjp(_scan_group, tuple(st_ref[0, q] for q in range(P)),
                         *[ref[i * C:(i + 1) * C, q * LANE:(q + 1) * LANE] for q, i in units for ref in ins])
        grads = vjp((tuple(dy_ref[i * C:(i + 1) * C, q * LANE:(q + 1) * LANE] for q, i in units),
                     tuple(ds_scr[q] for q in range(P))))
        for q in range(P):
            ds_scr[q] = grads[0][q]
        outs = (dr_ref, dlw_ref, dk_ref, dv_ref, da_ref, db_ref)
        for n_, (q, i) in enumerate(units):
            for t, ref in enumerate(outs):
                ref[i * C:(i + 1) * C, q * LANE:(q + 1) * LANE] = grads[1 + 6 * n_ + t]

    def col(off):
        return pl.BlockSpec((C * G, P * LANE), lambda p, n: (nc - 1 - n, off // P + p))

    return pl.pallas_call(
        kern, name="rwkv_scan_bwd",
        out_shape=[jax.ShapeDtypeStruct((T, 1024), F32)] * 6,
        grid=(npair // P, nc),
        in_specs=[col(0), col(0), col(0), col(16), col(0), col(0),
                  pl.BlockSpec((1, P, LANE, LANE), lambda p, n: (nc - 1 - n, p, 0, 0)), col(0)],
        out_specs=[col(0)] * 6,
        scratch_shapes=[pltpu.VMEM((P, LANE, LANE), F32)],
        compiler_params=_cp(("parallel", "arbitrary")),
    )(xs, lw, kp, xs, an, bb, states, dy)


def _gates_fwd(u, bias_pad, T, f_cb):
    nb = T // LANE

    def kern(f_ref, b_ref, c_ref):
        x = f_ref[...] + b_ref[...]
        lf = jnp.minimum(x, 0.0) - jnp.log(1.0 + jnp.exp(-jnp.abs(x)))
        lft = lf.T
        ut = (_iota2((LANE, LANE), 0) <= _iota2((LANE, LANE), 1)).astype(F32)
        carry = jnp.zeros((LANE, 1), F32)
        for blk in range(nb):
            seg = lft[:, blk * LANE:(blk + 1) * LANE]
            cs = _nn(seg, ut, HI) + carry
            c_ref[:, blk * LANE:(blk + 1) * LANE] = cs[:SUB, :]
            carry = carry + jnp.sum(seg, axis=1, keepdims=True)

    return pl.pallas_call(
        kern, name="fox_gates_fwd",
        out_shape=jax.ShapeDtypeStruct((SUB, T), F32),
        grid=(1,),
        in_specs=[pl.BlockSpec((T, LANE), lambda i: (0, f_cb)), pl.BlockSpec((1, LANE), lambda i: (0, 0))],
        out_specs=pl.BlockSpec((SUB, T), lambda i: (0, 0)),
        compiler_params=_cp(("arbitrary",)),
    )(u, bias_pad)


def _gates_bwd(dc, u, bias_pad, T, f_cb):
    nb = T // LANE

    def kern(dc_ref, f_ref, b_ref, dfl_ref, db_ref):
        dcv = jnp.concatenate([dc_ref[...], jnp.zeros((LANE - SUB, T), F32)], axis=0)
        lt = (_iota2((LANE, LANE), 0) >= _iota2((LANE, LANE), 1)).astype(F32)
        carry = jnp.zeros((LANE, 1), F32)
        pieces = [None] * nb
        for blk in range(nb - 1, -1, -1):
            seg = dcv[:, blk * LANE:(blk + 1) * LANE]
            pieces[blk] = _nn(seg, lt, HI) + carry
            carry = carry + jnp.sum(seg, axis=1, keepdims=True)
        dlf = (pieces[0] if nb == 1 else jnp.concatenate(pieces, axis=1)).T
        x = f_ref[...] + b_ref[...]
        dfl = dlf * _sigmoid(-x)
        dfl_ref[...] = dfl
        db_ref[...] = jnp.sum(dfl, axis=0, keepdims=True)

    return pl.pallas_call(
        kern, name="fox_gates_bwd",
        out_shape=[jax.ShapeDtypeStruct((T, LANE), F32), jax.ShapeDtypeStruct((1, LANE), F32)],
        grid=(1,),
        in_specs=[pl.BlockSpec((SUB, T), lambda i: (0, 0)), pl.BlockSpec((T, LANE), lambda i: (0, f_cb)),
                  pl.BlockSpec((1, LANE), lambda i: (0, 0))],
        out_specs=[pl.BlockSpec((T, LANE), lambda i: (0, 0)), pl.BlockSpec((1, LANE), lambda i: (0, 0))],
        compiler_params=_cp(("arbitrary",)),
    )(dc, u, bias_pad)


ATTN_HEADS = 4


def _attn_block(T):
    return 256 if T % 256 == 0 and T >= 512 else 128


def _attn_fwd(u, c3, T):
    H, HP = 8, ATTN_HEADS
    bq = _attn_block(T)
    nq = T // bq
    scale = FOX_HD ** -0.5
    lanes = [slice(h * LANE, (h + 1) * LANE) for h in range(HP)]

    def kern(q_ref, k_ref, v_ref, z_ref, cq_ref, ck_ref, o_ref, oa_ref, lse_ref):
        i = pl.program_id(1)
        q = [(q_ref[:, ln] * scale).astype(BF16) for ln in lanes]
        c0 = [cq_ref[h][:, 0:1] for h in range(HP)]

        def step(j, carry, diagonal=False):
            off = pl.multiple_of(j * bq, bq)
            s = [_nt(q[h], k_ref[pl.ds(off, bq), lanes[h]].astype(BF16)) + (c0[h] - ck_ref[h, :, pl.ds(off, bq)])
                 for h in range(HP)]
            ps, out = [], []
            for h in range(HP):
                m, l, acc = carry[h]
                sh = s[h]
                if diagonal:
                    sh = jnp.where(_iota2((bq, bq), 1) <= _iota2((bq, bq), 0), sh, NEG)
                m_new = jnp.maximum(m, jnp.max(sh, axis=1, keepdims=True))
                p = jnp.exp(sh - m_new)
                alpha = jnp.exp(m - m_new)
                ps.append(p.astype(BF16))
                out.append((m_new, alpha * l + jnp.sum(p, axis=1, keepdims=True), alpha * acc))
            return tuple((m, l, acc + _nn(ps[h], v_ref[pl.ds(off, bq), lanes[h]].astype(BF16)))
                         for h, (m, l, acc) in enumerate(out))

        init = tuple((jnp.full((bq, 1), NEG, F32), jnp.zeros((bq, 1), F32), jnp.zeros((bq, FOX_HD), F32)) for _ in range(HP))
        res = step(i, lax.fori_loop(0, i, step, init), diagonal=True)
        for h, (m, l, acc) in enumerate(res):
            o = acc / l
            z = z_ref[:, lanes[h]]
            o_ref[:, lanes[h]] = o
            oa_ref[:, lanes[h]] = (o * z * _sigmoid(z)).astype(BF16)
            lse_ref[h] = m + jnp.log(l)

    W = HP * LANE
    return pl.pallas_call(
        kern, name="fox_attn_fwd",
        out_shape=[jax.ShapeDtypeStruct((T, 1024), F32), jax.ShapeDtypeStruct((T, 1024), BF16),
                   jax.ShapeDtypeStruct((H, T, 1), F32)],
        grid=(H // HP, nq),
        in_specs=[pl.BlockSpec((bq, W), lambda g, i: (i, g)),
                  pl.BlockSpec((T, W), lambda g, i: (0, 8 // HP + g)),
                  pl.BlockSpec((T, W), lambda g, i: (0, 16 // HP + g)),
                  pl.BlockSpec((bq, W), lambda g, i: (i, 24 // HP + g)),
                  pl.BlockSpec((HP, 1, bq), lambda g, i: (g, 0, i)),
                  pl.BlockSpec((HP, 1, T), lambda g, i: (g, 0, 0))],
        out_specs=[pl.BlockSpec((bq, W), lambda g, i: (i, g)),
                   pl.BlockSpec((bq, W), lambda g, i: (i, g)),
                   pl.BlockSpec((HP, bq, 1), lambda g, i: (g, i, 0))],
        compiler_params=_cp(("parallel", "arbitrary")),
    )(u, u, u, u, c3, c3)


def _attn_probs(s, lse_i, diagonal):
    if not diagonal:
        return jnp.exp(s - lse_i)
    keep = _iota2(s.shape, 1) <= _iota2(s.shape, 0)
    return jnp.where(keep, jnp.exp(jnp.where(keep, s, NEG) - lse_i), 0.0)


def _attn_delta(u, c3, o, lse, doa, T):
    H, HP = 8, ATTN_HEADS
    bq = _attn_block(T)
    nq = T // bq
    scale = FOX_HD ** -0.5
    lanes = [slice(h * LANE, (h + 1) * LANE) for h in range(HP)]

    def kern(q_ref, k_ref, v_ref, z_ref, c_ref, o_ref, lse_ref, doa_ref, do_ref, dz_ref, dl_ref):
        i = pl.program_id(1)
        z = z_ref[...]
        sg = _sigmoid(z)
        dov = doa_ref[...]
        do_all = (dov * z * sg).astype(BF16)
        do_ref[...] = do_all
        dz_ref[...] = dov * o_ref[...] * (sg * (1.0 + z * (1.0 - sg)))
        qs = [(q_ref[:, ln] * scale).astype(BF16) for ln in lanes]
        dob = [do_all[:, ln] for ln in lanes]
        ioff = pl.multiple_of(i * bq, bq)
        c0 = [c_ref[h, :, pl.ds(ioff, bq)][:, 0:1] for h in range(HP)]
        lse_i = [lse_ref[h] for h in range(HP)]

        def step(j, acc, diagonal=False):
            off = pl.multiple_of(j * bq, bq)
            s = [_nt(qs[h], k_ref[pl.ds(off, bq), lanes[h]].astype(BF16)) + (c0[h] - c_ref[h, :, pl.ds(off, bq)])
                 for h in range(HP)]
            dp = [_nt(dob[h], v_ref[pl.ds(off, bq), lanes[h]].astype(BF16)) for h in range(HP)]
            return tuple(acc[h] + jnp.sum(_attn_probs(s[h], lse_i[h], diagonal) * dp[h], axis=1, keepdims=True)
                         for h in range(HP))

        init = tuple(jnp.zeros((bq, 1), F32) for _ in range(HP))
        res = step(i, lax.fori_loop(0, i, step, init), diagonal=True)
        for h in range(HP):
            dl_ref[h] = res[h]

    W = HP * LANE
    full = lambda cb: pl.BlockSpec((T, W), lambda g, i: (0, cb // HP + g))
    blk = lambda cb: pl.BlockSpec((bq, W), lambda g, i: (i, cb // HP + g))
    return pl.pallas_call(
        kern, name="fox_attn_delta",
        out_shape=[jax.ShapeDtypeStruct((T, 1024), BF16), jax.ShapeDtypeStruct((T, 1024), F32), jax.ShapeDtypeStruct((H, T, 1), F32)],
        grid=(H // HP, nq),
        in_specs=[blk(0), full(8), full(16), blk(24),
                  pl.BlockSpec((HP, 1, T), lambda g, i: (g, 0, 0)),
                  blk(0),
                  pl.BlockSpec((HP, bq, 1), lambda g, i: (g, i, 0)),
                  blk(0)],
        out_specs=[blk(0), blk(0), pl.BlockSpec((HP, bq, 1), lambda g, i: (g, i, 0))],
        compiler_params=_cp(("parallel", "arbitrary")),
    )(u, u, u, u, c3, o, lse, doa)


def _attn_bwd(u, c3, lse, do, delta, T):
    H, HP = 8, ATTN_HEADS
    bq = _attn_block(T)
    nq = T // bq
    scale = FOX_HD ** -0.5
    lanes = [slice(h * LANE, (h + 1) * LANE) for h in range(HP)]

    def kern(q_ref, k_ref, v_ref, c_ref, lse_ref, do_ref, dl_ref, dq_ref, dk_ref, dv_ref, dc_ref):
        j = pl.program_id(1)

        @pl.when(j == 0)
        def _():
            dq_ref[...] = jnp.zeros(dq_ref.shape, F32)

        kj = [k_ref[:, ln].astype(BF16) for ln in lanes]
        vj = [v_ref[:, ln].astype(BF16) for ln in lanes]
        joff = pl.multiple_of(j * bq, bq)
        ck = [c_ref[h, :, pl.ds(joff, bq)] for h in range(HP)]

        def step(i, carry, diagonal=False):
            off = pl.multiple_of(i * bq, bq)
            qs = [(q_ref[pl.ds(off, bq), ln] * scale).astype(BF16) for ln in lanes]
            dob = [do_ref[pl.ds(off, bq), ln] for ln in lanes]
            s = [_nt(qs[h], kj[h]) + (c_ref[h, :, pl.ds(off, bq)][:, 0:1] - ck[h]) for h in range(HP)]
            dp = [_nt(dob[h], vj[h]) for h in range(HP)]
            pb, dsb, dcs = [], [], []
            for h in range(HP):
                p = _attn_probs(s[h], lse_ref[h, pl.ds(off, bq), :], diagonal)
                ds = p * (dp[h] - dl_ref[h, pl.ds(off, bq), :])
                pb.append(p.astype(BF16))
                dsb.append(ds.astype(BF16))
                dcs.append(jnp.sum(ds, axis=0, keepdims=True))
            out = []
            for h, (dk, dv, dc) in enumerate(carry):
                dq_ref[pl.ds(off, bq), lanes[h]] += _nn(dsb[h], kj[h]) * scale
                out.append((dk + _tn(dsb[h], qs[h]), dv + _tn(pb[h], dob[h]), dc - dcs[h]))
            return tuple(out)

        init = tuple((jnp.zeros((bq, FOX_HD), F32), jnp.zeros((bq, FOX_HD), F32), jnp.zeros((1, bq), F32)) for _ in range(HP))
        res = lax.fori_loop(j + 1, nq, step, step(j, init, diagonal=True))
        for h, (dk, dv, dc) in enumerate(res):
            dk_ref[:, lanes[h]] = dk
            dv_ref[:, lanes[h]] = dv
            dc_ref[h] = dc

    W = HP * LANE
    full = lambda cb: pl.BlockSpec((T, W), lambda g, j: (0, cb // HP + g))
    blk = lambda cb: pl.BlockSpec((bq, W), lambda g, j: (j, cb // HP + g))
    col1 = pl.BlockSpec((HP, T, 1), lambda g, j: (g, 0, 0))
    return pl.pallas_call(
        kern, name="fox_attn_bwd",
        out_shape=[jax.ShapeDtypeStruct((T, 1024), F32)] * 3 + [jax.ShapeDtypeStruct((H, 1, T), F32)],
        grid=(H // HP, nq),
        in_specs=[full(0), blk(8), blk(16), pl.BlockSpec((HP, 1, T), lambda g, j: (g, 0, 0)), col1, full(0), col1],
        out_specs=[full(0), blk(0), blk(0), pl.BlockSpec((HP, 1, bq), lambda g, j: (g, 0, j))],
        compiler_params=_cp(("parallel", "arbitrary")),
    )(u, u, u, c3, lse, do, delta)


def _place():
    return lax.axis_index("x"), lax.axis_index("y"), lax.axis_index("c")


def _slot(p):
    return 4 * p[0] + 2 * p[1] + p[2]


def _other_chips(x, y):
    return [(1 - x, y), (x, 1 - y), (1 - x, 1 - y)]


def _allgather_steps(in_refs, out_refs, scratch):
    (src,), (dst,) = in_refs, out_refs
    send_sems, recv_sems, local_sem = scratch
    x, y, c = _place()
    me, sibling = (x, y, c), (x, y, 1 - c)
    chips = _other_chips(x, y)

    def copy(k, block, to, from_input=False):
        d = dst.at[_slot(block)]
        return pltpu.make_async_remote_copy(
            src_ref=src if from_input else d, dst_ref=d, send_sem=send_sems.at[k], recv_sem=recv_sems.at[k],
            device_id=to, device_id_type=MESH)

    def first_copies():
        return [copy(0, me, sibling, True)] + [copy(1 + j, me, (*chip, c), True) for j, chip in enumerate(chips)]

    def start():
        pltpu.make_async_copy(src, dst.at[_slot(me)], local_sem).start()
        for cp in first_copies():
            cp.start()

    def finish():
        passed = []
        for j, chip in enumerate(chips):
            copy(1 + j, (*chip, c), me).wait_recv()
            passed.append(copy(4 + j, (*chip, c), sibling))
            passed[-1].start()
        copy(0, sibling, me).wait_recv()
        for j, chip in enumerate(chips):
            copy(4 + j, (*chip, 1 - c), me).wait_recv()
        for cp in first_copies() + passed:
            cp.wait_send()
        pltpu.make_async_copy(src, dst.at[_slot(me)], local_sem).wait()

    return start, finish


def _allgather_relay_steps(in_refs, out_refs, scratch):
    (src,), (dst,) = in_refs, out_refs
    send_sems, recv_sems, local_sem = scratch
    x, y, c = _place()
    me, sibling = (x, y, c), (x, y, 1 - c)
    x_nbr, y_nbr, diag = (1 - x, y, c), (x, 1 - y, c), (1 - x, 1 - y, c)
    flip = lambda a, bit: a + bit - 2 * a * bit
    relay_from = (flip(x, 1 - c), flip(y, c), c)
    relay_to = (flip(x, c), flip(y, 1 - c), c)

    def copy(k, block, to, from_input=False):
        d = dst.at[_slot(block)]
        return pltpu.make_async_remote_copy(
            src_ref=src if from_input else d, dst_ref=d, send_sem=send_sems.at[k], recv_sem=recv_sems.at[k],
            device_id=to, device_id_type=MESH)

    def first_copies():
        return [copy(0, me, sibling, True), copy(1, me, x_nbr, True), copy(2, me, y_nbr, True)]

    def other(block):
        return block[:2] + (1 - c,)

    def start():
        pltpu.make_async_copy(src, dst.at[_slot(me)], local_sem).start()
        for cp in first_copies():
            cp.start()

    def finish():
        copy(1, x_nbr, me).wait_recv()
        copy(2, y_nbr, me).wait_recv()
        later = [copy(3, relay_from, relay_to), copy(4, x_nbr, sibling), copy(5, y_nbr, sibling)]
        for cp in later:
            cp.start()
        copy(3, diag, me).wait_recv()
        later.append(copy(6, diag, sibling))
        later[-1].start()
        copy(0, sibling, me).wait_recv()
        for k, block in ((4, x_nbr), (5, y_nbr), (6, diag)):
            copy(k, other(block), me).wait_recv()
        for cp in first_copies() + later:
            cp.wait_send()
        pltpu.make_async_copy(src, dst.at[_slot(me)], local_sem).wait()

    return start, finish


def _allgather_side(a, relay=False):
    return ((a,), (jax.ShapeDtypeStruct((N_DEV,) + a.shape, a.dtype),),
            (pltpu.SemaphoreType.DMA((7,)), pltpu.SemaphoreType.DMA((7,)), pltpu.SemaphoreType.DMA),
            _allgather_relay_steps if relay else _allgather_steps)


def _allgather(a, name, relay=False):
    ins, outs, scratch, make = _allgather_side(a, relay)

    def body(a_ref, o_ref, *scr):
        start, finish = make((a_ref,), (o_ref,), scr)
        start()
        finish()

    any_spec = pl.BlockSpec(memory_space=pl.ANY)
    return pl.pallas_call(body, name=name, out_shape=outs[0], in_specs=[any_spec], out_specs=any_spec,
                          scratch_shapes=list(scratch))(a)


def _exchange_pair(ga, gb):
    def body(ga_ref, gb_ref, ra_ref, rb_ref, send_sems, recv_sems):
        x, y, c = _place()
        sibling = (x, y, 1 - c)
        slots = [_slot(sibling)] + [_slot((*chip, 1 - c)) for chip in _other_chips(x, y)]
        cps = []
        for t, (src, dst) in enumerate(((ga_ref, ra_ref), (gb_ref, rb_ref))):
            for k, ps in enumerate(slots):
                cps.append(pltpu.make_async_remote_copy(
                    src_ref=src.at[ps], dst_ref=dst.at[k], send_sem=send_sems.at[t, k], recv_sem=recv_sems.at[t, k],
                    device_id=sibling, device_id_type=MESH))
        for cp in cps:
            cp.start()
        for cp in cps:
            cp.wait()

    any_spec = pl.BlockSpec(memory_space=pl.ANY)
    return pl.pallas_call(
        body, name="exchange_pair",
        out_shape=[jax.ShapeDtypeStruct((4,) + ga.shape[1:], ga.dtype), jax.ShapeDtypeStruct((4,) + gb.shape[1:], gb.dtype)],
        in_specs=[any_spec] * 2,
        out_specs=[any_spec] * 2,
        scratch_shapes=[pltpu.SemaphoreType.DMA((2, 4)), pltpu.SemaphoreType.DMA((2, 4))],
    )(ga, gb)


def _tiling(R, Cc, tile, by_cols):
    if by_cols:
        assert Cc % tile == 0
        return Cc // tile, (R, tile), lambda lead, i: (lead, 0, i)
    assert R % tile == 0
    return R // tile, (tile, Cc), lambda lead, i: (lead, i, 0)


def _pair_add(name, g, r1, slots, tile, by_cols=False):
    _, R, Cc = g.shape
    steps, blk, at = _tiling(R, Cc, tile, by_cols)

    def kern(s_ref, a_ref, b_ref, o_ref):
        o_ref[...] = (a_ref[...].astype(F32) + b_ref[...].astype(F32)).astype(o_ref.dtype)

    return pl.pallas_call(
        kern, name=name,
        out_shape=jax.ShapeDtypeStruct((3, R, Cc), BF16),
        grid_spec=pltpu.PrefetchScalarGridSpec(
            num_scalar_prefetch=1, grid=(3, steps),
            in_specs=[pl.BlockSpec((1,) + blk, lambda j, i, s: at(s[j], i)),
                      pl.BlockSpec((1,) + blk, lambda j, i, s: at(1 + j, i))],
            out_specs=pl.BlockSpec((1,) + blk, lambda j, i, s: at(j, i))),
        compiler_params=_cp(("arbitrary", "arbitrary")),
    )(slots, g, r1)


def _exchange_ici_steps(in_refs, out_refs, scratch):
    pairs = list(zip(in_refs, out_refs))
    send_sems, recv_sems = scratch
    x, y, c = _place()

    def copies():
        return [pltpu.make_async_remote_copy(
            src_ref=src.at[j], dst_ref=dst.at[j], send_sem=send_sems.at[t, j], recv_sem=recv_sems.at[t, j],
            device_id=(*chip, c), device_id_type=MESH)
            for j, chip in enumerate(_other_chips(x, y)) for t, (src, dst) in enumerate(pairs)]

    def start():
        for cp in copies():
            cp.start()

    def finish():
        for cp in copies():
            cp.wait()

    return start, finish


def _exchange_ici_side(sa, sb):
    return ((sa, sb), (jax.ShapeDtypeStruct(sa.shape, sa.dtype), jax.ShapeDtypeStruct(sb.shape, sb.dtype)),
            (pltpu.SemaphoreType.DMA((2, 3)), pltpu.SemaphoreType.DMA((2, 3))), _exchange_ici_steps)


def _adamw(name, w, m, v, parts, tile, by_cols=False):
    R, Cc = w.shape
    steps, blk_shape, at = _tiling(R, Cc, tile, by_cols)
    n_parts = len(parts)

    def kern(*refs):
        w_ref, m_ref, v_ref = refs[:3]
        g = None
        for r_ref, (_, n) in zip(refs[3:3 + n_parts], parts):
            for s in range(n):
                term = r_ref[s].astype(F32)
                g = term if g is None else g + term
        g_out, d_out, m_out, v_out = refs[3 + n_parts:]
        mn = ADAM_B1 * m_ref[...] + (1.0 - ADAM_B1) * g
        vn = ADAM_B2 * v_ref[...] + (1.0 - ADAM_B2) * (g * g)
        m_hat = mn / (1.0 - ADAM_B1 ** ADAM_STEP)
        v_hat = vn / (1.0 - ADAM_B2 ** ADAM_STEP)
        g_out[...] = g
        d_out[...] = -ADAM_LR * (m_hat / (jnp.sqrt(v_hat) + ADAM_EPS) + ADAM_WD * w_ref[...])
        m_out[...] = mn
        v_out[...] = vn

    blk = pl.BlockSpec(blk_shape, lambda i: at(0, i)[1:])
    return pl.pallas_call(
        kern, name=name,
        out_shape=[jax.ShapeDtypeStruct((R, Cc), F32)] * 4,
        grid=(steps,),
        in_specs=[blk] * 3 + [pl.BlockSpec((n,) + blk_shape, lambda i: at(0, i)) for (_, n) in parts],
        out_specs=[blk] * 4,
        compiler_params=_cp(("arbitrary",)),
    )(w, m, v, *[a for (a, _) in parts])


def _pad_cols(a, w):
    return jnp.pad(a, ((0, 0), (0, w - a.shape[1])))


def _pad_rows(a, r):
    return jnp.pad(a, ((0, r - a.shape[0]), (0, 0)))


def _pack_b(pf, pr, wo, w2, a2, rows):
    body = jnp.concatenate([pf, pr, wo.reshape(2048, 256), jnp.concatenate([w2, a2], axis=1)], axis=0)
    return _pad_rows(body, rows)


def kernel(x, norm_gain, w_in, fox_forget_bias, rwkv_shift_mix, rwkv_w0, rwkv_w2, rwkv_a0, rwkv_a2, rwkv_k_k, rwkv_k_a, rwkv_r_k, rwkv_ln_w, rwkv_ln_b, w_proj_fox, w_proj_rwkv, w_out, final_norm_gain, loss_target, m_norm_gain, m_w_in, m_fox_forget_bias, m_rwkv_shift_mix, m_rwkv_w0, m_rwkv_w2, m_rwkv_a0, m_rwkv_a2, m_rwkv_k_k, m_rwkv_k_a, m_rwkv_r_k, m_rwkv_ln_w, m_rwkv_ln_b, m_w_proj_fox, m_w_proj_rwkv, m_w_out, m_final_norm_gain, v_norm_gain, v_w_in, v_fox_forget_bias, v_rwkv_shift_mix, v_rwkv_w0, v_rwkv_w2, v_rwkv_a0, v_rwkv_a2, v_rwkv_k_k, v_rwkv_k_a, v_rwkv_r_k, v_rwkv_ln_w, v_rwkv_ln_b, v_w_proj_fox, v_w_proj_rwkv, v_w_out, v_final_norm_gain):
    T, D = x.shape[1], x.shape[2]
    assert D == 2048 and T % LANE == 0
    NI = w_in.shape[2]
    IN = N_DEV * NI
    RB = 4224
    x2 = x[0]
    lt2 = loss_target[0]
    me = _slot(_place())

    wa = _allgather(w_in[0].astype(BF16), "allgather_w_in", relay=True)
    packed_own = _pack_b(w_proj_fox[0], w_proj_rwkv[0], w_out[0], rwkv_w2[0], rwkv_a2[0], RB).astype(BF16)
    sections = [(0, 4096, 0), (4104, 4096, 4096), (8392, 4096, 8192), (4096, 8, 12288), (8200, 96, 12544), (8296, 96, 12672)]
    NP = 12800
    pieces, at_col = [], 0
    for lo, width, pad_lo in sections:
        if pad_lo > at_col:
            pieces.append(jnp.zeros((D, pad_lo - at_col), BF16))
        col = lo
        while col < lo + width:
            d = col // NI
            stop = min(lo + width, (d + 1) * NI)
            pieces.append(wa[d, :, col - d * NI:stop - d * NI])
            col = stop
        at_col = pad_lo + width
    pieces.append(jnp.zeros((D, NP - at_col), BF16))
    w_pad = jnp.concatenate(pieces, axis=1)
    F_CB, LORA_CB = 96, 49

    mu = rwkv_shift_mix
    mu_main = mu[:, 0:4096]
    mu_lora = jnp.concatenate([_pad_cols(mu[:, 4096:4192], LANE), _pad_cols(mu[:, 4192:4288], LANE)], axis=1)
    bias_pad = _pad_cols(fox_forget_bias, LANE)
    rk_flat = rwkv_r_k.reshape(1, 1024)
    gf = final_norm_gain.reshape(1, D)

    tb = min(256, T)
    tbh = min(128, T)
    (h,) = _rowcall("rms_fwd", T, tb, [_rows(x2, tb, D), _whole(norm_gain)],
                    lambda i, xv, g: ([_rms_math(xv, g)], []), [(D, BF16)])
    u, wb = _mm(h, w_pad, tm=1024, tn=1280, name="mm_in", side=_allgather_side(packed_own))
    wpf = wb[:, 0:1024, :].transpose(1, 0, 2).reshape(1024, D)
    wpr = wb[:, 1024:2048, :].transpose(1, 0, 2).reshape(1024, D)
    wo = wb[:, 2048:4096, :].reshape(N_DEV * 256, D)
    w2p = _pad_rows(wb[:, 4096:4192, 0:128].transpose(1, 0, 2).reshape(96, 1024).astype(F32), LANE)
    a2p = _pad_rows(wb[:, 4096:4192, 128:256].transpose(1, 0, 2).reshape(96, 1024).astype(F32), LANE)

    c8 = _gates_fwd(u, bias_pad, T, F_CB)
    c3 = c8.reshape(8, 1, T)
    o_raw, o_a, lse = _attn_fwd(u, c3, T)

    def shift_body(i, um, hm, ul, hl, mm_, ml):
        outs = []
        for uv, hv, mv in ((um, hm, mm_), (ul, hl, ml)):
            hv = jnp.where(i == 0, 0.0, hv)
            prev = pltpu.roll(jnp.concatenate([hv, uv], axis=0), 1, 0)[SUB:]
            outs.append(uv + (prev - uv) * mv)
        return outs, []

    def halo_prev(arr, w, cb):
        return (arr, (SUB, w), lambda i: (jnp.maximum(i * (tbh // SUB) - 1, 0), cb))

    xs, xl = _rowcall("rwkv_shift_fwd", T, tbh,
                      [_rows(u, tbh, 4096, 1), halo_prev(u, 4096, 1), _rows(u, tbh, 256, LORA_CB), halo_prev(u, 256, LORA_CB),
                       _whole(mu_main), _whole(mu_lora)],
                      shift_body, [(4096, F32), (256, F32)])

    prep_par = [_whole(rwkv_w0), _whole(rwkv_a0), _whole(rwkv_k_k), _whole(rwkv_k_a), _whole(w2p), _whole(a2p)]
    prep_rows = [_rows(xs, tbh, 1024, 1), _rows(xl, tbh, LANE, 0), _rows(xl, tbh, LANE, 1)]
    lw, kp, an, bb = _rowcall("rwkv_prep_fwd", T, tbh, prep_rows + prep_par,
                              lambda i, *a: (list(_prep_math(*a)), []), [(1024, F32)] * 4)
    y, states = _scan_fwd(xs, lw, kp, an, bb, T)
    post_rows = [_rows(y, tbh, 1024), _rows(xs, tbh, 1024, 0), _rows(kp, tbh, 1024), _rows(xs, tbh, 1024, 2), _rows(xs, tbh, 1024, 3)]
    post_par = [_whole(rwkv_ln_w), _whole(rwkv_ln_b), _whole(rk_flat)]
    (o_b,) = _rowcall("rwkv_post_fwd", T, tbh, post_rows + post_par,
                      lambda i, *a: ([_post_math(*a)], []), [(1024, BF16)])

    pa = _mm(o_a, wpf, name="mm_proj_fox")
    pb = _mm(o_b, wpr, name="mm_proj_rwkv")
    merge_rows = [_rows(u, tb, D, 4), _rows(u, tb, D, 5), _rows(pa, tb, D), _rows(pb, tb, D)]
    (mg,) = _rowcall("merge_fwd", T, tb, merge_rows, lambda i, *a: ([_merge_math(*a)], []), [(D, BF16)])
    mo = _mm(mg, wo, name="mm_out")

    def head_body(i, xv, mov, ltv, g):
        out = xv + mov
        r = lax.rsqrt(jnp.mean(out * out, axis=-1, keepdims=True) + RMS_EPS)
        yn = out * r
        err = yn * g - ltv
        loss = 0.5 * jnp.sum(jnp.sum(err * err, axis=-1, keepdims=True), axis=0, keepdims=True) / D
        dyv = err / D
        dyn = dyv * g
        dout = r * (dyn - yn * jnp.mean(dyn * yn, axis=-1, keepdims=True))
        return [dout], [loss, jnp.sum(dyv * yn, axis=0, keepdims=True)]

    dout, loss_p, dgf_p = _rowcall("loss_head", T, tb, [_rows(x2, tb, D), _rows(mo, tb, D), _rows(lt2, tb, D), _whole(gf)],
                                   head_body, [(D, F32)], [(1, 1), (1, D)])

    dm = _mm(dout, wo, tb=True, name="mm_out_dx")
    dwo = _mm(mg.T, dout, out_dtype=BF16, name="mm_out_dw")

    def merge_bwd_body(i, ga, gb, pav, pbv, dmv):
        _, vjp = jax.vjp(_merge_math, ga, gb, pav, pbv)
        dga, dgb, dpa, dpb = vjp(dmv)
        return [dga, dgb, dpa, dpb], []

    dga, dgb, dpa, dpb = _rowcall("merge_bwd", T, tb, merge_rows + [_rows(dm, tb, D)], merge_bwd_body,
                                  [(D, BF16), (D, BF16), (D, BF16), (D, BF16)])
    doa = _mm(dpa, wpf, tb=True, name="mm_proj_fox_dx")
    dwpf = _mm(o_a.T, dpa, out_dtype=BF16, name="mm_proj_fox_dw")
    dob = _mm(dpb, wpr, tb=True, name="mm_proj_rwkv_dx")
    dwpr = _mm(o_b.T, dpb, out_dtype=BF16, name="mm_proj_rwkv_dw")

    do_b, dza, delta = _attn_delta(u, c3, o_raw, lse, doa, T)
    dq, dk, dv, dc3 = _attn_bwd(u, c3, lse, do_b, delta, T)
    dfl, dbias_p = _gates_bwd(dc3.reshape(8, T), u, bias_pad, T, F_CB)

    def post_bwd_body(i, yv, rv, kpv, vv, zv, lnw, lnb, rkv, dobv):
        _, vjp = jax.vjp(_post_math, yv, rv, kpv, vv, zv, lnw, lnb, rkv)
        dy_, dr_, dkp_, dv_, dz_, dlnw, dlnb, drk = vjp(dobv)
        return [dy_, dr_, dkp_, dv_, dz_], [dlnw, dlnb, drk]

    dy_s, dr_p, dkp_p, dv_p, dzb, dlnw_p, dlnb_p, drk_p = _rowcall(
        "rwkv_post_bwd", T, tbh, post_rows + post_par + [_rows(dob, tbh, 1024)], post_bwd_body,
        [(1024, F32)] * 5, [(1, 1024)] * 3)
    dr_s, dlw, dkp_s, dv_s, dan, dbb = _scan_bwd(xs, lw, kp, an, bb, states, dy_s, T)

    def prep_bwd_body(i, xk, xwd, xad, w0, a0, kkw, kaw, w2v, a2v, dlw_, dkp1, dkp2, dan_, dbb_, dr1, dr2, dv1, dv2, dz_):
        _, vjp = jax.vjp(_prep_math, xk, xwd, xad, w0, a0, kkw, kaw, w2v, a2v)
        dxk, dxwd, dxad, dw0, da0, dkk, dka, dw2, da2 = vjp((dlw_, dkp1 + dkp2, dan_, dbb_))
        return [[dr1 + dr2, dxk, dv1 + dv2, dz_], [dxwd, dxad]], [dw0, da0, dkk, dka, dw2, da2]

    cots = [dlw, dkp_s, dkp_p, dan, dbb, dr_s, dr_p, dv_s, dv_p, dzb]
    dxs, dxl, dw0_p, da0_p, dkk_p, dka_p, dw2_p, da2_p = _rowcall(
        "rwkv_prep_bwd", T, tbh, prep_rows + prep_par + [_rows(c_, tbh, 1024) for c_ in cots], prep_bwd_body,
        [(4096, F32), (256, F32)], [(1, 1024)] * 4 + [(LANE, 1024)] * 2)

    def shift_bwd_body(i, dm_, hm, dl_, hl, um, pm, ul, pl_, mm_, ml):
        last = i == T // tbh - 1
        outs, accs = [], []
        for dv_, hv, uv, pv, mv in ((dm_, hm, um, pm, mm_), (dl_, hl, ul, pl_, ml)):
            hv = jnp.where(last, 0.0, hv)
            nxt = pltpu.roll(jnp.concatenate([dv_, hv], axis=0), tbh + SUB - 1, 0)[:tbh]
            pv = jnp.where(i == 0, 0.0, pv)
            prev = pltpu.roll(jnp.concatenate([pv, uv], axis=0), 1, 0)[SUB:]
            outs.append(dv_ * (1.0 - mv) + nxt * mv)
            accs.append(jnp.sum(dv_ * (prev - uv), axis=0, keepdims=True))
        return outs, accs

    def halo_next(arr, w, cb):
        last_blk = T // SUB - 1
        return (arr, (SUB, w), lambda i: (jnp.minimum((i + 1) * (tbh // SUB), last_blk), cb))

    du_b, du_l, dmu_main_p, dmu_lora_p = _rowcall(
        "rwkv_shift_bwd", T, tbh,
        [_rows(dxs, tbh, 4096), halo_next(dxs, 4096, 0), _rows(dxl, tbh, 256), halo_next(dxl, 256, 0),
         _rows(u, tbh, 4096, 1), halo_prev(u, 4096, 1), _rows(u, tbh, 256, LORA_CB), halo_prev(u, 256, LORA_CB),
         _whole(mu_main), _whole(mu_lora)],
        shift_bwd_body, [(4096, BF16), (256, BF16)], [(1, 4096), (1, 256)])

    du = jnp.concatenate([dq.astype(BF16), dk.astype(BF16), dv.astype(BF16), dza.astype(BF16), du_b, dga, dgb,
                          dfl.astype(BF16), jnp.zeros((T, LANE), BF16), du_l], axis=1)
    dw_pad_t = _mm(du, h, ta=True, out_dtype=BF16, tm=1280, tn=1024, name="mm_in_dw")

    by_col = sorted(sections)
    blocks = []
    for d in range(N_DEV):
        parts = []
        for lo, width, pad_lo in by_col:
            a, b = max(lo, d * NI), min(lo + width, (d + 1) * NI)
            if a < b:
                parts.append(dw_pad_t[pad_lo + a - lo:pad_lo + b - lo, :])
        blocks.append(jnp.concatenate(parts, axis=0)[None])
    ga = jnp.concatenate(blocks, axis=0)
    lora_g = jnp.concatenate([dw2_p[:96].reshape(96, N_DEV, 128).transpose(1, 0, 2),
                              da2_p[:96].reshape(96, N_DEV, 128).transpose(1, 0, 2)], axis=2).astype(BF16)
    gb = jnp.concatenate([dwpf.reshape(1024, N_DEV, 256).transpose(1, 0, 2),
                          dwpr.reshape(1024, N_DEV, 256).transpose(1, 0, 2),
                          dwo.reshape(N_DEV, 2048, 256), lora_g, jnp.zeros((N_DEV, RB - 4192, 256), BF16)], axis=1)

    xx, yy, cc = _place()
    chip_slots = jnp.stack([_slot((*chip, cc)) for chip in _other_chips(xx, yy)]).astype(jnp.int32)
    r1a, r1b = _exchange_pair(ga, gb)
    sa = _pair_add("pair_add_w_in", ga, r1a, chip_slots, 256, by_cols=True)
    sb = _pair_add("pair_add_packed", gb, r1b, chip_slots, 384)
    ga_own = lax.dynamic_index_in_dim(ga, me, 0, keepdims=True)
    gb_own = lax.dynamic_index_in_dim(gb, me, 0, keepdims=True)
    dh, r2a, r2b = _mm(du, w_pad, tb=True, tm=1024, tn=1024, tk=NP // 10, name="mm_in_dx", side=_exchange_ici_side(sa, sb))

    def rms_bwd_body(i, xv, g, dhv, doutv):
        _, vjp = jax.vjp(_rms_math, xv, g)
        dx_, dg_ = vjp(dhv)
        return [dx_ + doutv], [dg_]

    grad_x2, dng_p = _rowcall("rms_bwd", T, tb, [_rows(x2, tb, D), _whole(norm_gain), _rows(dh, tb, D), _rows(dout, tb, D)],
                              rms_bwd_body, [(D, F32)], [(1, D)])

    dmu = jnp.concatenate([dmu_main_p, dmu_lora_p[:, 0:96], dmu_lora_p[:, 128:224]], axis=1)
    small_parts = [dng_p, dbias_p[:, 0:8], dmu, dw0_p, da0_p, dkk_p, dka_p, drk_p, dlnw_p, dlnb_p, dgf_p, loss_p]
    SR = 128
    small = _pad_cols(jnp.concatenate(small_parts, axis=1), SR * LANE).reshape(SR, LANE)
    rs = _allgather(small, "allgather_small")

    w_in_outs = _adamw("adamw_w_in", w_in[0].T, m_w_in[0].T, v_w_in[0].T, [(ga_own, 1), (r1a, 1), (r2a, 3)], 256, by_cols=True)
    g_in, d_in, m_in, v_in = [o.T for o in w_in_outs]
    pk = lambda pf, pr, wo_, w2_, a2_: _pack_b(pf[0], pr[0], wo_[0], w2_[0], a2_[0], RB)
    outs_b = _adamw("adamw_packed", pk(w_proj_fox, w_proj_rwkv, w_out, rwkv_w2, rwkv_a2),
                    pk(m_w_proj_fox, m_w_proj_rwkv, m_w_out, m_rwkv_w2, m_rwkv_a2),
                    pk(v_w_proj_fox, v_w_proj_rwkv, v_w_out, v_rwkv_w2, v_rwkv_a2), [(gb_own, 1), (r1b, 1), (r2b, 3)], 384)

    def pack_small(ng, fb, sm, w0, a0, kk_, ka_, rk_, lnw, lnb, fg):
        parts = [ng, fb, sm, w0, a0, kk_, ka_, rk_.reshape(1, 1024), lnw, lnb, fg.reshape(1, D), jnp.zeros((1, 1), F32)]
        return _pad_cols(jnp.concatenate(parts, axis=1), SR * LANE).reshape(SR, LANE)

    outs_s = _adamw("adamw_small",
                    pack_small(norm_gain, fox_forget_bias, rwkv_shift_mix, rwkv_w0, rwkv_a0, rwkv_k_k, rwkv_k_a, rwkv_r_k,
                               rwkv_ln_w, rwkv_ln_b, final_norm_gain),
                    pack_small(m_norm_gain, m_fox_forget_bias, m_rwkv_shift_mix, m_rwkv_w0, m_rwkv_a0, m_rwkv_k_k, m_rwkv_k_a,
                               m_rwkv_r_k, m_rwkv_ln_w, m_rwkv_ln_b, m_final_norm_gain),
                    pack_small(v_norm_gain, v_fox_forget_bias, v_rwkv_shift_mix, v_rwkv_w0, v_rwkv_a0, v_rwkv_k_k, v_rwkv_k_a,
                               v_rwkv_r_k, v_rwkv_ln_w, v_rwkv_ln_b, v_final_norm_gain),
                    [(rs, N_DEV)], SR)

    def unpack_b(pkd):
        return dict(w_proj_fox=pkd[0:1024][None], w_proj_rwkv=pkd[1024:2048][None], w_out=pkd[2048:4096].reshape(1, 256, D),
                    rwkv_w2=pkd[4096:4192, 0:128][None], rwkv_a2=pkd[4096:4192, 128:256][None])

    def unpack_s(pkd):
        flat = pkd.reshape(1, SR * LANE)
        names = [("norm_gain", D), ("fox_forget_bias", 8), ("rwkv_shift_mix", 4288), ("rwkv_w0", 1024), ("rwkv_a0", 1024),
                 ("rwkv_k_k", 1024), ("rwkv_k_a", 1024), ("rwkv_r_k", 1024), ("rwkv_ln_w", 1024), ("rwkv_ln_b", 1024),
                 ("final_norm_gain", D), ("loss", 1)]
        out, off = {}, 0
        for nm, n in names:
            out[nm] = flat[:, off:off + n]
            off += n
        out["rwkv_r_k"] = out["rwkv_r_k"].reshape(1, 16, 64)
        out["final_norm_gain"] = out["final_norm_gain"].reshape(D)
        return out

    order = ["norm_gain", "w_in", "fox_forget_bias", "rwkv_shift_mix", "rwkv_w0", "rwkv_w2", "rwkv_a0", "rwkv_a2", "rwkv_k_k",
             "rwkv_k_a", "rwkv_r_k", "rwkv_ln_w", "rwkv_ln_b", "w_proj_fox", "w_proj_rwkv", "w_out", "final_norm_gain"]
    result = []
    loss = None
    for kind, big in enumerate((g_in, d_in, m_in, v_in)):
        d = {**unpack_b(outs_b[kind]), **unpack_s(outs_s[kind]), "w_in": big[None]}
        if kind == 0:
            loss = d["loss"].reshape(())
        result += [d[n] for n in order]
    return (loss, grad_x2[None], *result)
```

```python
import functools

import jax
import jax.numpy as jnp
from jax import lax
from jax.experimental import pallas as pl
from jax.experimental.pallas import tpu as pltpu

F32 = jnp.float32
BF16 = jnp.bfloat16
HI = lax.Precision.HIGHEST
H3 = lax.Precision.HIGH
MESH = pl.DeviceIdType.MESH

FOX_HD = 128
RW_HD = 64
RMS_EPS = 1e-6
GN_EPS = 64e-5
L2_EPS = 1e-12
ADAM_LR = 0.001
ADAM_B1 = 0.9
ADAM_B2 = 0.999
ADAM_EPS = 1e-08
ADAM_WD = 0.01
ADAM_STEP = 10

LANE = 128
SUB = 8
VMEM_LIMIT = 56 * 1024 * 1024
N_DEV = 8
CHUNK = 64
SCAN_GROUP = 4
SCAN_PAIRS = 2
PS = None
NEG = -1e30


def _scan_shape(T):
    c = min(CHUNK, T)
    return c, min(SCAN_GROUP, T // c)


def _cp(sem=None):
    return pltpu.CompilerParams(dimension_semantics=sem, vmem_limit_bytes=VMEM_LIMIT)


def _sigmoid(x):
    return jax.nn.sigmoid(x)


def _softplus(x):
    return jnp.maximum(x, 0.0) + jnp.log(1.0 + jnp.exp(-jnp.abs(x)))


def _nn(a, b, prec=None):
    return lax.dot_general(a, b, (((1,), (0,)), ((), ())), precision=prec, preferred_element_type=F32)


def _nt(a, b, prec=None):
    return lax.dot_general(a, b, (((1,), (1,)), ((), ())), precision=prec, preferred_element_type=F32)


def _tn(a, b, prec=None):
    return lax.dot_general(a, b, (((0,), (0,)), ((), ())), precision=prec, preferred_element_type=F32)


def _iota2(shape, dim):
    return lax.broadcasted_iota(jnp.int32, shape, dim)


def _seg_sum(x):
    r = _iota2((LANE, LANE), 0) // RW_HD
    c = _iota2((LANE, LANE), 1) // RW_HD
    bd = (r == c).astype(F32)
    parts = [_nn(x[:, j * LANE:(j + 1) * LANE], bd, H3) for j in range(x.shape[1] // LANE)]
    return parts[0] if len(parts) == 1 else jnp.concatenate(parts, axis=1)


def _mm(a, b, *, ta=False, tb=False, out_dtype=F32, tm=512, tn=512, tk=None, name, side=None):
    assert not (ta and tb)
    K, M = a.shape if ta else a.shape[::-1]
    N = b.shape[0] if tb else b.shape[1]
    tm, tn = min(tm, M), min(tn, N)
    tk = K if tk is None else tk
    nk = K // tk
    assert M % tm == 0 and N % tn == 0 and K % tk == 0
    gi, gj = M // tm, N // tn
    a_spec = pl.BlockSpec((tk, tm), lambda i, j, k: (k, i)) if ta else pl.BlockSpec((tm, tk), lambda i, j, k: (i, k))
    b_spec = pl.BlockSpec((tn, tk), lambda i, j, k: (j, k)) if tb else pl.BlockSpec((tk, tn), lambda i, j, k: (k, j))
    side_ins, side_outs, side_scr, side_make = side if side is not None else ((), (), (), None)
    n_si, n_so = len(side_ins), len(side_outs)
    n_acc = 0 if nk == 1 else 1

    def body(*refs):
        a_ref, b_ref = refs[:2]
        o_ref = refs[2 + n_si]
        scr = refs[3 + n_si + n_so:]
        k = pl.program_id(2)
        if side_make is not None:
            start, finish = side_make(refs[2:2 + n_si], refs[3 + n_si:3 + n_si + n_so], scr[n_acc:])
            first = jnp.logical_and(jnp.logical_and(pl.program_id(0) == 0, pl.program_id(1) == 0), k == 0)
            last = jnp.logical_and(jnp.logical_and(pl.program_id(0) == gi - 1, pl.program_id(1) == gj - 1), k == nk - 1)
            pl.when(first)(start)
        av = a_ref[...].astype(BF16)
        bv = b_ref[...].astype(BF16)
        p = _tn(av, bv) if ta else _nt(av, bv) if tb else _nn(av, bv)
        if nk == 1:
            o_ref[...] = p.astype(out_dtype)
        else:
            acc_ref = scr[0]

            @pl.when(k == 0)
            def _():
                acc_ref[...] = p

            @pl.when(k > 0)
            def _():
                acc_ref[...] += p

            @pl.when(k == nk - 1)
            def _():
                o_ref[...] = acc_ref[...].astype(out_dtype)
        if side_make is not None:
            pl.when(last)(finish)

    any_spec = pl.BlockSpec(memory_space=pl.ANY)
    res = pl.pallas_call(
        body, name=name,
        out_shape=[jax.ShapeDtypeStruct((M, N), out_dtype)] + list(side_outs),
        grid=(gi, gj, nk),
        in_specs=[a_spec, b_spec] + [any_spec] * n_si,
        out_specs=[pl.BlockSpec((tm, tn), lambda i, j, k: (i, j))] + [any_spec] * n_so,
        scratch_shapes=([] if nk == 1 else [pltpu.VMEM((tm, tn), F32)]) + list(side_scr),
        compiler_params=_cp(("arbitrary",) * 3 if side is not None else ("parallel", "parallel", "arbitrary")),
    )(a, b, *side_ins)
    return res if side is not None else res[0]


def _rows(arr, tb, w, cb=0):
    return (arr, (tb, w), lambda i: (i, cb))


def _whole(arr):
    nd = arr.ndim
    return (arr, arr.shape, lambda i: (0,) * nd)


def _rowcall(name, T, tb, ins, body, outs, accs=()):
    n_in, n_out = len(ins), len(outs)

    def kern(*refs):
        i = pl.program_id(0)
        vals = [r[...] for r in refs[:n_in]]
        ro, ao = body(i, *vals)
        for r, v in zip(refs[n_in:n_in + n_out], ro):
            if isinstance(v, (list, tuple)):
                off = 0
                for piece in v:
                    w = piece.shape[1]
                    r[:, off:off + w] = piece.astype(r.dtype)
                    off += w
            else:
                r[...] = v.astype(r.dtype)
        if accs:
            acc_refs = refs[n_in + n_out:]

            @pl.when(i == 0)
            def _():
                for r in acc_refs:
                    r[...] = jnp.zeros(r.shape, F32)

            for r, v in zip(acc_refs, ao):
                r[...] += v

    out_shape = [jax.ShapeDtypeStruct((T, w), dt) for (w, dt) in outs] + [jax.ShapeDtypeStruct(s, F32) for s in accs]
    out_specs = [pl.BlockSpec((tb, w), lambda i: (i, 0)) for (w, dt) in outs] + [pl.BlockSpec(s, lambda i: (0, 0)) for s in accs]
    res = pl.pallas_call(
        kern, name=name,
        out_shape=out_shape,
        grid=(T // tb,),
        in_specs=[pl.BlockSpec(bs, im) for (_, bs, im) in ins],
        out_specs=out_specs,
        compiler_params=_cp(("arbitrary",)),
    )(*[a for (a, _, _) in ins])
    return res


def _rms_math(x, g):
    r = lax.rsqrt(jnp.mean(x * x, axis=-1, keepdims=True) + RMS_EPS)
    return x * r * g


def _merge_math(ga, gb, pa, pb):
    return _sigmoid(ga) * pa + _sigmoid(gb) * pb


def _prep_math(xk, xwd, xad, w0, a0, kk_w, ka_w, w2p, a2p):
    z = w0 + _nn(jnp.tanh(xwd), w2p, H3)
    w = -_softplus(-z) - 0.5
    lw = -jnp.exp(w)
    ag = _sigmoid(a0 + _nn(xad, a2p, H3))
    p = xk * kk_w
    n = jnp.maximum(jnp.sqrt(_seg_sum(p * p)), L2_EPS)
    kk = p / n
    kp = xk * (1.0 + (ag - 1.0) * ka_w)
    return lw, kp, -kk, kk * ag


def _post_math(y, r, kp, v, z, lnw, lnb, rk):
    inv = 1.0 / RW_HD
    mu = _seg_sum(y) * inv
    d = y - mu
    var = _seg_sum(d * d) * inv
    yn = d * lax.rsqrt(var + GN_EPS) * lnw + lnb
    bonus = _seg_sum(r * kp * rk) * v
    return (yn + bonus) * (z * _sigmoid(z))


def _scan_group(s0s, *flat):
    P = len(s0s)
    G = len(flat) // (6 * P)
    ch = [flat[6 * i:6 * i + 6] for i in range(P * G)]
    C = ch[0][0].shape[0]
    C2 = 2 * C
    cat = jnp.concatenate
    m0 = _iota2((1, LANE), 1) < RW_HD
    mask0 = m0.astype(F32)
    mask1 = 1.0 - mask0
    r2 = _iota2((C2, C2), 0)
    c2 = _iota2((C2, C2), 1)
    dist = r2 - c2
    in_head = dist <= r2 % C
    eye = (r2 == c2).astype(F32)
    lower = (_iota2((C, C), 0) >= _iota2((C, C), 1)).astype(F32)
    bd = (_iota2((LANE, LANE), 0) // RW_HD) == (_iota2((LANE, LANE), 1) // RW_HD)

    def tri(m, strict):
        return jnp.where(dist > 0 if strict else dist >= 0, jnp.where(in_head, m, 0.0), 0.0)

    def sel(z):
        return jnp.where(m0, z[:C], z[C:])

    gs = [_nn(lower, c[1], HI) for c in ch]
    pre = []
    for (r, lw, k, v, a, b), g in zip(ch, gs):
        g_end = jnp.sum(lw, axis=0, keepdims=True)
        en = jnp.exp(-g)
        ec = jnp.exp(g_end - g)
        pre.append(dict(at=a * jnp.exp(g - lw), rt=r * jnp.exp(g), bt=b * en, kt=k * en, bh=b * ec, kh=k * ec,
                        dec=jnp.exp(g_end), v=v))
    grams = [_nt(cat([p["at"] * mask0, p["at"] * mask1, p["rt"] * mask0, p["rt"] * mask1], axis=0),
                 cat([p["bt"], p["bt"], p["kt"], p["kt"]], axis=0), PS) for p in pre]
    mab = [tri(gm[:C2, :C2], True) for gm in grams]
    mp = [_nn(m, m, PS) for m in mab]
    tinv = [eye + m for m in mab]
    n = 2
    while n < C:
        last = 2 * n >= C
        for i in range(P * G):
            if last:
                tinv[i] = tinv[i] + _nn(mp[i], tinv[i], PS)
            else:
                z = _nn(mp[i], cat([mp[i], tinv[i]], axis=1), PS)
                mp[i], tinv[i] = z[:, :C2], tinv[i] + z[:, C2:]
        n *= 2
    xv = [sel(_nn(tri(gm[:C2, C2:], True), cat([p["v"], p["v"]], axis=0), PS)) for gm, p in zip(grams, pre)]
    ys, s = [None] * (P * G), list(s0s)
    for i in range(G):
        for q in range(P):
            n = q * G + i
            p, gm = pre[n], grams[n]
            sx = _nt(cat([p["at"], p["rt"]], axis=0), s[q], PS)
            x = sx[:C] + xv[n]
            u = sel(_nn(tinv[n], cat([x, x], axis=0), PS))
            v = p["v"]
            ys[n] = sx[C:] + sel(_nn(cat([tri(gm[C2:, :C2], False), tri(gm[C2:, C2:], False)], axis=1),
                                     cat([u, u, v, v], axis=0), PS))
            s[q] = s[q] * p["dec"] + jnp.where(bd, _tn(cat([u, v], axis=0), cat([p["bh"], p["kh"]], axis=0), PS), 0.0)
    return tuple(ys), tuple(s)


def _scan_fwd(xs, lw, kp, an, bb, T):
    C, G = _scan_shape(T)
    P = SCAN_PAIRS
    nc = T // (C * G)
    npair = 1024 // LANE

    def kern(r_ref, lw_ref, k_ref, v_ref, a_ref, b_ref, y_ref, st_ref, s_scr):
        n = pl.program_id(1)

        @pl.when(n == 0)
        def _():
            s_scr[...] = jnp.zeros(s_scr.shape, F32)

        st_ref[0] = s_scr[...]
        ins = (r_ref, lw_ref, k_ref, v_ref, a_ref, b_ref)
        ys, s1 = _scan_group(tuple(s_scr[q] for q in range(P)),
                             *[ref[i * C:(i + 1) * C, q * LANE:(q + 1) * LANE] for q in range(P) for i in range(G) for ref in ins])
        for q in range(P):
            for i in range(G):
                y_ref[i * C:(i + 1) * C, q * LANE:(q + 1) * LANE] = ys[q * G + i]
            s_scr[q] = s1[q]

    def col(off):
        return pl.BlockSpec((C * G, P * LANE), lambda p, n: (n, off // P + p))

    return pl.pallas_call(
        kern, name="rwkv_scan_fwd",
        out_shape=[jax.ShapeDtypeStruct((T, 1024), F32), jax.ShapeDtypeStruct((nc, npair, LANE, LANE), F32)],
        grid=(npair // P, nc),
        in_specs=[col(0), col(0), col(0), col(16), col(0), col(0)],
        out_specs=[col(0), pl.BlockSpec((1, P, LANE, LANE), lambda p, n: (n, p, 0, 0))],
        scratch_shapes=[pltpu.VMEM((P, LANE, LANE), F32)],
        compiler_params=_cp(("parallel", "arbitrary")),
    )(xs, lw, kp, xs, an, bb)


def _scan_bwd(xs, lw, kp, an, bb, states, dy, T):
    C, G = _scan_shape(T)
    P = SCAN_PAIRS
    nc = T // (C * G)
    npair = 1024 // LANE

    def kern(r_ref, lw_ref, k_ref, v_ref, a_ref, b_ref, st_ref, dy_ref, dr_ref, dlw_ref, dk_ref, dv_ref, da_ref, db_ref, ds_scr):
        n = pl.program_id(1)

        @pl.when(n == 0)
        def _():
            ds_scr[...] = jnp.zeros(ds_scr.shape, F32)

        ins = (r_ref, lw_ref, k_ref, v_ref, a_ref, b_ref)
        units = [(q, i) for q in range(P) for i in range(G)]
        _, vjp = jax.vjp(_scan_group, tuple(st_ref[0, q] for q in range(P)),
                         *[ref[i * C:(i + 1) * C, q * LANE:(q + 1) * LANE] for q, i in units for ref in ins])
        grads = vjp((tuple(dy_ref[i * C:(i + 1) * C, q * LANE:(q + 1) * LANE] for q, i in units),
                     tuple(ds_scr[q] for q in range(P))))
        for q in range(P):
            ds_scr[q] = grads[0][q]
        outs = (dr_ref, dlw_ref, dk_ref, dv_ref, da_ref, db_ref)
        for n_, (q, i) in enumerate(units):
            for t, ref in enumerate(outs):
                ref[i * C:(i + 1) * C, q * LANE:(q + 1) * LANE] = grads[1 + 6 * n_ + t]

    def col(off):
        return pl.BlockSpec((C * G, P * LANE), lambda p, n: (nc - 1 - n, off // P + p))

    return pl.pallas_call(
        kern, name="rwkv_scan_bwd",
        out_shape=[jax.ShapeDtypeStruct((T, 1024), F32)] * 6,
        grid=(npair // P, nc),
        in_specs=[col(0), col(0), col(0), col(16), col(0), col(0),
                  pl.BlockSpec((1, P, LANE, LANE), lambda p, n: (nc - 1 - n, p, 0, 0)), col(0)],
        out_specs=[col(0)] * 6,
        scratch_shapes=[pltpu.VMEM((P, LANE, LANE), F32)],
        compiler_params=_cp(("parallel", "arbitrary")),
    )(xs, lw, kp, xs, an, bb, states, dy)


def _gates_fwd(u, bias_pad, T, f_cb):
    nb = T // LANE

    def kern(f_ref, b_ref, c_ref):
        x = f_ref[...] + b_ref[...]
        lf = jnp.minimum(x, 0.0) - jnp.log(1.0 + jnp.exp(-jnp.abs(x)))
        lft = lf.T
        ut = (_iota2((LANE, LANE), 0) <= _iota2((LANE, LANE), 1)).astype(F32)
        carry = jnp.zeros((LANE, 1), F32)
        for blk in range(nb):
            seg = lft[:, blk * LANE:(blk + 1) * LANE]
            cs = _nn(seg, ut, HI) + carry
            c_ref[:, blk * LANE:(blk + 1) * LANE] = cs[:SUB, :]
            carry = carry + jnp.sum(seg, axis=1, keepdims=True)

    return pl.pallas_call(
        kern, name="fox_gates_fwd",
        out_shape=jax.ShapeDtypeStruct((SUB, T), F32),
        grid=(1,),
        in_specs=[pl.BlockSpec((T, LANE), lambda i: (0, f_cb)), pl.BlockSpec((1, LANE), lambda i: (0, 0))],
        out_specs=pl.BlockSpec((SUB, T), lambda i: (0, 0)),
        compiler_params=_cp(("arbitrary",)),
    )(u, bias_pad)


def _gates_bwd(dc, u, bias_pad, T, f_cb):
    nb = T // LANE

    def kern(dc_ref, f_ref, b_ref, dfl_ref, db_ref):
        dcv = jnp.concatenate([dc_ref[...], jnp.zeros((LANE - SUB, T), F32)], axis=0)
        lt = (_iota2((LANE, LANE), 0) >= _iota2((LANE, LANE), 1)).astype(F32)
        carry = jnp.zeros((LANE, 1), F32)
        pieces = [None] * nb
        for blk in range(nb - 1, -1, -1):
            seg = dcv[:, blk * LANE:(blk + 1) * LANE]
            pieces[blk] = _nn(seg, lt, HI) + carry
            carry = carry + jnp.sum(seg, axis=1, keepdims=True)
        dlf = (pieces[0] if nb == 1 else jnp.concatenate(pieces, axis=1)).T
        x = f_ref[...] + b_ref[...]
        dfl = dlf * _sigmoid(-x)
        dfl_ref[...] = dfl
        db_ref[...] = jnp.sum(dfl, axis=0, keepdims=True)

    return pl.pallas_call(
        kern, name="fox_gates_bwd",
        out_shape=[jax.ShapeDtypeStruct((T, LANE), F32), jax.ShapeDtypeStruct((1, LANE), F32)],
        grid=(1,),
        in_specs=[pl.BlockSpec((SUB, T), lambda i: (0, 0)), pl.BlockSpec((T, LANE), lambda i: (0, f_cb)),
                  pl.BlockSpec((1, LANE), lambda i: (0, 0))],
        out_specs=[pl.BlockSpec((T, LANE), lambda i: (0, 0)), pl.BlockSpec((1, LANE), lambda i: (0, 0))],
        compiler_params=_cp(("arbitrary",)),
    )(dc, u, bias_pad)


ATTN_HEADS = 4


def _attn_block(T):
    return 256 if T % 256 == 0 and T >= 512 else 128


def _attn_fwd(u, c3, T):
    H, HP = 8, ATTN_HEADS
    bq = _attn_block(T)
    nq = T // bq
    scale = FOX_HD ** -0.5
    lanes = [slice(h * LANE, (h + 1) * LANE) for h in range(HP)]

    def kern(q_ref, k_ref, v_ref, z_ref, cq_ref, ck_ref, o_ref, oa_ref, lse_ref):
        i = pl.program_id(1)
        q = [(q_ref[:, ln] * scale).astype(BF16) for ln in lanes]
        c0 = [cq_ref[h][:, 0:1] for h in range(HP)]

        def step(j, carry, diagonal=False):
            off = pl.multiple_of(j * bq, bq)
            s = [_nt(q[h], k_ref[pl.ds(off, bq), lanes[h]].astype(BF16)) + (c0[h] - ck_ref[h, :, pl.ds(off, bq)])
                 for h in range(HP)]
            ps, out = [], []
            for h in range(HP):
                m, l, acc = carry[h]
                sh = s[h]
                if diagonal:
                    sh = jnp.where(_iota2((bq, bq), 1) <= _iota2((bq, bq), 0), sh, NEG)
                m_new = jnp.maximum(m, jnp.max(sh, axis=1, keepdims=True))
                p = jnp.exp(sh - m_new)
                alpha = jnp.exp(m - m_new)
                ps.append(p.astype(BF16))
                out.append((m_new, alpha * l + jnp.sum(p, axis=1, keepdims=True), alpha * acc))
            return tuple((m, l, acc + _nn(ps[h], v_ref[pl.ds(off, bq), lanes[h]].astype(BF16)))
                         for h, (m, l, acc) in enumerate(out))

        init = tuple((jnp.full((bq, 1), NEG, F32), jnp.zeros((bq, 1), F32), jnp.zeros((bq, FOX_HD), F32)) for _ in range(HP))
        res = step(i, lax.fori_loop(0, i, step, init), diagonal=True)
        for h, (m, l, acc) in enumerate(res):
            o = acc / l
            z = z_ref[:, lanes[h]]
            o_ref[:, lanes[h]] = o
            oa_ref[:, lanes[h]] = (o * z * _sigmoid(z)).astype(BF16)
            lse_ref[h] = m + jnp.log(l)

    W = HP * LANE
    return pl.pallas_call(
        kern, name="fox_attn_fwd",
        out_shape=[jax.ShapeDtypeStruct((T, 1024), F32), jax.ShapeDtypeStruct((T, 1024), BF16),
                   jax.ShapeDtypeStruct((H, T, 1), F32)],
        grid=(H // HP, nq),
        in_specs=[pl.BlockSpec((bq, W), lambda g, i: (i, g)),
                  pl.BlockSpec((T, W), lambda g, i: (0, 8 // HP + g)),
                  pl.BlockSpec((T, W), lambda g, i: (0, 16 // HP + g)),
                  pl.BlockSpec((bq, W), lambda g, i: (i, 24 // HP + g)),
                  pl.BlockSpec((HP, 1, bq), lambda g, i: (g, 0, i)),
                  pl.BlockSpec((HP, 1, T), lambda g, i: (g, 0, 0))],
        out_specs=[pl.BlockSpec((bq, W), lambda g, i: (i, g)),
                   pl.BlockSpec((bq, W), lambda g, i: (i, g)),
                   pl.BlockSpec((HP, bq, 1), lambda g, i: (g, i, 0))],
        compiler_params=_cp(("parallel", "arbitrary")),
    )(u, u, u, u, c3, c3)


def _attn_probs(s, lse_i, diagonal):
    if not diagonal:
        return jnp.exp(s - lse_i)
    keep = _iota2(s.shape, 1) <= _iota2(s.shape, 0)
    return jnp.where(keep, jnp.exp(jnp.where(keep, s, NEG) - lse_i), 0.0)


def _attn_delta(u, c3, o, lse, doa, T):
    H, HP = 8, ATTN_HEADS
    bq = _attn_block(T)
    nq = T // bq
    scale = FOX_HD ** -0.5
    lanes = [slice(h * LANE, (h + 1) * LANE) for h in range(HP)]

    def kern(q_ref, k_ref, v_ref, z_ref, c_ref, o_ref, lse_ref, doa_ref, do_ref, dz_ref, dl_ref):
        i = pl.program_id(1)
        z = z_ref[...]
        sg = _sigmoid(z)
        dov = doa_ref[...]
        do_all = (dov * z * sg).astype(BF16)
        do_ref[...] = do_all
        dz_ref[...] = dov * o_ref[...] * (sg * (1.0 + z * (1.0 - sg)))
        qs = [(q_ref[:, ln] * scale).astype(BF16) for ln in lanes]
        dob = [do_all[:, ln] for ln in lanes]
        ioff = pl.multiple_of(i * bq, bq)
        c0 = [c_ref[h, :, pl.ds(ioff, bq)][:, 0:1] for h in range(HP)]
        lse_i = [lse_ref[h] for h in range(HP)]

        def step(j, acc, diagonal=False):
            off = pl.multiple_of(j * bq, bq)
            s = [_nt(qs[h], k_ref[pl.ds(off, bq), lanes[h]].astype(BF16)) + (c0[h] - c_ref[h, :, pl.ds(off, bq)])
                 for h in range(HP)]
            dp = [_nt(dob[h], v_ref[pl.ds(off, bq), lanes[h]].astype(BF16)) for h in range(HP)]
            return tuple(acc[h] + jnp.sum(_attn_probs(s[h], lse_i[h], diagonal) * dp[h], axis=1, keepdims=True)
                         for h in range(HP))

        init = tuple(jnp.zeros((bq, 1), F32) for _ in range(HP))
        res = step(i, lax.fori_loop(0, i, step, init), diagonal=True)
        for h in range(HP):
            dl_ref[h] = res[h]

    W = HP * LANE
    full = lambda cb: pl.BlockSpec((T, W), lambda g, i: (0, cb // HP + g))
    blk = lambda cb: pl.BlockSpec((bq, W), lambda g, i: (i, cb // HP + g))
    return pl.pallas_call(
        kern, name="fox_attn_delta",
        out_shape=[jax.ShapeDtypeStruct((T, 1024), BF16), jax.ShapeDtypeStruct((T, 1024), F32), jax.ShapeDtypeStruct((H, T, 1), F32)],
        grid=(H // HP, nq),
        in_specs=[blk(0), full(8), full(16), blk(24),
                  pl.BlockSpec((HP, 1, T), lambda g, i: (g, 0, 0)),
                  blk(0),
                  pl.BlockSpec((HP, bq, 1), lambda g, i: (g, i, 0)),
                  blk(0)],
        out_specs=[blk(0), blk(0), pl.BlockSpec((HP, bq, 1), lambda g, i: (g, i, 0))],
        compiler_params=_cp(("parallel", "arbitrary")),
    )(u, u, u, u, c3, o, lse, doa)


def _attn_bwd(u, c3, lse, do, delta, T):
    H, HP = 8, ATTN_HEADS
    bq = _attn_block(T)
    nq = T // bq
    scale = FOX_HD ** -0.5
    lanes = [slice(h * LANE, (h + 1) * LANE) for h in range(HP)]

    def kern(q_ref, k_ref, v_ref, c_ref, lse_ref, do_ref, dl_ref, dq_ref, dk_ref, dv_ref, dc_ref):
        j = pl.program_id(1)

        @pl.when(j == 0)
        def _():
            dq_ref[...] = jnp.zeros(dq_ref.shape, F32)

        kj = [k_ref[:, ln].astype(BF16) for ln in lanes]
        vj = [v_ref[:, ln].astype(BF16) for ln in lanes]
        joff = pl.multiple_of(j * bq, bq)
        ck = [c_ref[h, :, pl.ds(joff, bq)] for h in range(HP)]

        def step(i, carry, diagonal=False):
            off = pl.multiple_of(i * bq, bq)
            qs = [(q_ref[pl.ds(off, bq), ln] * scale).astype(BF16) for ln in lanes]
            dob = [do_ref[pl.ds(off, bq), ln] for ln in lanes]
            s = [_nt(qs[h], kj[h]) + (c_ref[h, :, pl.ds(off, bq)][:, 0:1] - ck[h]) for h in range(HP)]
            dp = [_nt(dob[h], vj[h]) for h in range(HP)]
            pb, dsb, dcs = [], [], []
            for h in range(HP):
                p = _attn_probs(s[h], lse_ref[h, pl.ds(off, bq), :], diagonal)
                ds = p * (dp[h] - dl_ref[h, pl.ds(off, bq), :])
                pb.append(p.astype(BF16))
                dsb.append(ds.astype(BF16))
                dcs.append(jnp.sum(ds, axis=0, keepdims=True))
            out = []
            for h, (dk, dv, dc) in enumerate(carry):
                dq_ref[pl.ds(off, bq), lanes[h]] += _nn(dsb[h], kj[h]) * scale
                out.append((dk + _tn(dsb[h], qs[h]), dv + _tn(pb[h], dob[h]), dc - dcs[h]))
            return tuple(out)

        init = tuple((jnp.zeros((bq, FOX_HD), F32), jnp.zeros((bq, FOX_HD), F32), jnp.zeros((1, bq), F32)) for _ in range(HP))
        res = lax.fori_loop(j + 1, nq, step, step(j, init, diagonal=True))
        for h, (dk, dv, dc) in enumerate(res):
            dk_ref[:, lanes[h]] = dk
            dv_ref[:, lanes[h]] = dv
            dc_ref[h] = dc

    W = HP * LANE
    full = lambda cb: pl.BlockSpec((T, W), lambda g, j: (0, cb // HP + g))
    blk = lambda cb: pl.BlockSpec((bq, W), lambda g, j: (j, cb // HP + g))
    col1 = pl.BlockSpec((HP, T, 1), lambda g, j: (g, 0, 0))
    return pl.pallas_call(
        kern, name="fox_attn_bwd",
        out_shape=[jax.ShapeDtypeStruct((T, 1024), F32)] * 3 + [jax.ShapeDtypeStruct((H, 1, T), F32)],
        grid=(H // HP, nq),
        in_specs=[full(0), blk(8), blk(16), pl.BlockSpec((HP, 1, T), lambda g, j: (g, 0, 0)), col1, full(0), col1],
        out_specs=[full(0), blk(0), blk(0), pl.BlockSpec((HP, 1, bq), lambda g, j: (g, 0, j))],
        compiler_params=_cp(("parallel", "arbitrary")),
    )(u, u, u, c3, lse, do, delta)


def _place():
    return lax.axis_index("x"), lax.axis_index("y"), lax.axis_index("c")


def _slot(p):
    return 4 * p[0] + 2 * p[1] + p[2]


def _other_chips(x, y):
    return [(1 - x, y), (x, 1 - y), (1 - x, 1 - y)]


def _allgather_steps(in_refs, out_refs, scratch):
    (src,), (dst,) = in_refs, out_refs
    send_sems, recv_sems, local_sem = scratch
    x, y, c = _place()
    me, sibling = (x, y, c), (x, y, 1 - c)
    chips = _other_chips(x, y)

    def copy(k, block, to, from_input=False):
        d = dst.at[_slot(block)]
        return pltpu.make_async_remote_copy(
            src_ref=src if from_input else d, dst_ref=d, send_sem=send_sems.at[k], recv_sem=recv_sems.at[k],
            device_id=to, device_id_type=MESH)

    def first_copies():
        return [copy(0, me, sibling, True)] + [copy(1 + j, me, (*chip, c), True) for j, chip in enumerate(chips)]

    def start():
        pltpu.make_async_copy(src, dst.at[_slot(me)], local_sem).start()
        for cp in first_copies():
            cp.start()

    def finish():
        passed = []
        for j, chip in enumerate(chips):
            copy(1 + j, (*chip, c), me).wait_recv()
            passed.append(copy(4 + j, (*chip, c), sibling))
            passed[-1].start()
        copy(0, sibling, me).wait_recv()
        for j, chip in enumerate(chips):
            copy(4 + j, (*chip, 1 - c), me).wait_recv()
        for cp in first_copies() + passed:
            cp.wait_send()
        pltpu.make_async_copy(src, dst.at[_slot(me)], local_sem).wait()

    return start, finish


def _allgather_relay_steps(in_refs, out_refs, scratch):
    (src,), (dst,) = in_refs, out_refs
    send_sems, recv_sems, local_sem = scratch
    x, y, c = _place()
    me, sibling = (x, y, c), (x, y, 1 - c)
    x_nbr, y_nbr, diag = (1 - x, y, c), (x, 1 - y, c), (1 - x, 1 - y, c)
    flip = lambda a, bit: a + bit - 2 * a * bit
    relay_from = (flip(x, 1 - c), flip(y, c), c)
    relay_to = (flip(x, c), flip(y, 1 - c), c)

    def copy(k, block, to, from_input=False):
        d = dst.at[_slot(block)]
        return pltpu.make_async_remote_copy(
            src_ref=src if from_input else d, dst_ref=d, send_sem=send_sems.at[k], recv_sem=recv_sems.at[k],
            device_id=to, device_id_type=MESH)

    def first_copies():
        return [copy(0, me, sibling, True), copy(1, me, x_nbr, True), copy(2, me, y_nbr, True)]

    def other(block):
        return block[:2] + (1 - c,)

    def start():
        pltpu.make_async_copy(src, dst.at[_slot(me)], local_sem).start()
        for cp in first_copies():
            cp.start()

    def finish():
        copy(1, x_nbr, me).wait_recv()
        copy(2, y_nbr, me).wait_recv()
        later = [copy(3, relay_from, relay_to), copy(4, x_nbr, sibling), copy(5, y_nbr, sibling)]
        for cp in later:
            cp.start()
        copy(3, diag, me).wait_recv()
        later.append(copy(6, diag, sibling))
        later[-1].start()
        copy(0, sibling, me).wait_recv()
        for k, block in ((4, x_nbr), (5, y_nbr), (6, diag)):
            copy(k, other(block), me).wait_recv()
        for cp in first_copies() + later:
            cp.wait_send()
        pltpu.make_async_copy(src, dst.at[_slot(me)], local_sem).wait()

    return start, finish


def _allgather_side(a, relay=False):
    return ((a,), (jax.ShapeDtypeStruct((N_DEV,) + a.shape, a.dtype),),
            (pltpu.SemaphoreType.DMA((7,)), pltpu.SemaphoreType.DMA((7,)), pltpu.SemaphoreType.DMA),
            _allgather_relay_steps if relay else _allgather_steps)


def _allgather(a, name, relay=False):
    ins, outs, scratch, make = _allgather_side(a, relay)

    def body(a_ref, o_ref, *scr):
        start, finish = make((a_ref,), (o_ref,), scr)
        start()
        finish()

    any_spec = pl.BlockSpec(memory_space=pl.ANY)
    return pl.pallas_call(body, name=name, out_shape=outs[0], in_specs=[any_spec], out_specs=any_spec,
                          scratch_shapes=list(scratch))(a)


def _exchange_pair(ga, gb):
    def body(ga_ref, gb_ref, ra_ref, rb_ref, send_sems, recv_sems):
        x, y, c = _place()
        sibling = (x, y, 1 - c)
        slots = [_slot(sibling)] + [_slot((*chip, 1 - c)) for chip in _other_chips(x, y)]
        cps = []
        for t, (src, dst) in enumerate(((ga_ref, ra_ref), (gb_ref, rb_ref))):
            for k, ps in enumerate(slots):
                cps.append(pltpu.make_async_remote_copy(
                    src_ref=src.at[ps], dst_ref=dst.at[k], send_sem=send_sems.at[t, k], recv_sem=recv_sems.at[t, k],
                    device_id=sibling, device_id_type=MESH))
        for cp in cps:
            cp.start()
        for cp in cps:
            cp.wait()

    any_spec = pl.BlockSpec(memory_space=pl.ANY)
    return pl.pallas_call(
        body, name="exchange_pair",
        out_shape=[jax.ShapeDtypeStruct((4,) + ga.shape[1:], ga.dtype), jax.ShapeDtypeStruct((4,) + gb.shape[1:], gb.dtype)],
        in_specs=[any_spec] * 2,
        out_specs=[any_spec] * 2,
        scratch_shapes=[pltpu.SemaphoreType.DMA((2, 4)), pltpu.SemaphoreType.DMA((2, 4))],
    )(ga, gb)


def _tiling(R, Cc, tile, by_cols):
    if by_cols:
        assert Cc % tile == 0
        return Cc // tile, (R, tile), lambda lead, i: (lead, 0, i)
    assert R % tile == 0
    return R // tile, (tile, Cc), lambda lead, i: (lead, i, 0)


def _pair_add(name, g, r1, slots, tile, by_cols=False):
    _, R, Cc = g.shape
    steps, blk, at = _tiling(R, Cc, tile, by_cols)

    def kern(s_ref, a_ref, b_ref, o_ref):
        o_ref[...] = (a_ref[...].astype(F32) + b_ref[...].astype(F32)).astype(o_ref.dtype)

    return pl.pallas_call(
        kern, name=name,
        out_shape=jax.ShapeDtypeStruct((3, R, Cc), BF16),
        grid_spec=pltpu.PrefetchScalarGridSpec(
            num_scalar_prefetch=1, grid=(3, steps),
            in_specs=[pl.BlockSpec((1,) + blk, lambda j, i, s: at(s[j], i)),
                      pl.BlockSpec((1,) + blk, lambda j, i, s: at(1 + j, i))],
            out_specs=pl.BlockSpec((1,) + blk, lambda j, i, s: at(j, i))),
        compiler_params=_cp(("arbitrary", "arbitrary")),
    )(slots, g, r1)


def _axis_neighbours():
    x, y, c = _place()
    flip = lambda a, bit: a + bit - 2 * a * bit
    return (flip(x, c), flip(y, 1 - c), c), (flip(x, 1 - c), flip(y, c), c), c


def _exchange_first_steps(in_refs, out_refs, scratch):
    pairs = list(zip(in_refs, out_refs))
    send_sems, recv_sems = scratch
    first, _, c = _axis_neighbours()

    def copies():
        return [pltpu.make_async_remote_copy(
            src_ref=src.at[j], dst_ref=dst.at[k], send_sem=send_sems.at[t, k], recv_sem=recv_sems.at[t, k],
            device_id=first, device_id_type=MESH)
            for t, (src, dst) in enumerate(pairs) for k, j in enumerate((1 - c, 2))]

    def start():
        for cp in copies():
            cp.start()

    def finish():
        for cp in copies():
            cp.wait()

    return start, finish


def _exchange_first_side(sa, sb):
    return ((sa, sb), (jax.ShapeDtypeStruct((2,) + sa.shape[1:], sa.dtype), jax.ShapeDtypeStruct((2,) + sb.shape[1:], sb.dtype)),
            (pltpu.SemaphoreType.DMA((2, 2)), pltpu.SemaphoreType.DMA((2, 2))), _exchange_first_steps)


def _axis_add(name, s, t1, core, tile, by_cols=False):
    _, R, Cc = s.shape
    steps, blk, at = _tiling(R, Cc, tile, by_cols)

    def kern(c_ref, a_ref, b_ref, o_ref):
        o_ref[...] = (a_ref[...].astype(F32) + b_ref[...].astype(F32)).astype(o_ref.dtype)

    return pl.pallas_call(
        kern, name=name,
        out_shape=jax.ShapeDtypeStruct((1, R, Cc), BF16),
        grid_spec=pltpu.PrefetchScalarGridSpec(
            num_scalar_prefetch=1, grid=(steps,),
            in_specs=[pl.BlockSpec((1,) + blk, lambda i, cr: at(cr[0], i)),
                      pl.BlockSpec((1,) + blk, lambda i, cr: at(1, i))],
            out_specs=pl.BlockSpec((1,) + blk, lambda i, cr: at(0, i))),
        compiler_params=_cp(("arbitrary",)),
    )(core, s, t1)


def _exchange_second(pa, pb):
    def body(pa_ref, pb_ref, ta_ref, tb_ref, send_sems, recv_sems):
        _, second, _ = _axis_neighbours()
        cps = [pltpu.make_async_remote_copy(src_ref=src, dst_ref=dst, send_sem=send_sems.at[t], recv_sem=recv_sems.at[t],
                                            device_id=second, device_id_type=MESH)
               for t, (src, dst) in enumerate(((pa_ref, ta_ref), (pb_ref, tb_ref)))]
        for cp in cps:
            cp.start()
        for cp in cps:
            cp.wait()

    any_spec = pl.BlockSpec(memory_space=pl.ANY)
    return pl.pallas_call(
        body, name="exchange_second",
        out_shape=[jax.ShapeDtypeStruct(pa.shape, pa.dtype), jax.ShapeDtypeStruct(pb.shape, pb.dtype)],
        in_specs=[any_spec] * 2, out_specs=[any_spec] * 2,
        scratch_shapes=[pltpu.SemaphoreType.DMA((2,)), pltpu.SemaphoreType.DMA((2,))],
    )(pa, pb)


def _adamw(name, w, m, v, parts, tile, by_cols=False):
    R, Cc = w.shape
    steps, blk_shape, at = _tiling(R, Cc, tile, by_cols)
    n_parts = len(parts)

    def kern(*refs):
        w_ref, m_ref, v_ref = refs[:3]
        g = None
        for r_ref, (_, n) in zip(refs[3:3 + n_parts], parts):
            for s in range(n):
                term = r_ref[s].astype(F32)
                g = term if g is None else g + term
        g_out, d_out, m_out, v_out = refs[3 + n_parts:]
        mn = ADAM_B1 * m_ref[...] + (1.0 - ADAM_B1) * g
        vn = ADAM_B2 * v_ref[...] + (1.0 - ADAM_B2) * (g * g)
        m_hat = mn / (1.0 - ADAM_B1 ** ADAM_STEP)
        v_hat = vn / (1.0 - ADAM_B2 ** ADAM_STEP)
        g_out[...] = g
        d_out[...] = -ADAM_LR * (m_hat / (jnp.sqrt(v_hat) + ADAM_EPS) + ADAM_WD * w_ref[...])
        m_out[...] = mn
        v_out[...] = vn

    blk = pl.BlockSpec(blk_shape, lambda i: at(0, i)[1:])
    return pl.pallas_call(
        kern, name=name,
        out_shape=[jax.ShapeDtypeStruct((R, Cc), F32)] * 4,
        grid=(steps,),
        in_specs=[blk] * 3 + [pl.BlockSpec((n,) + blk_shape, lambda i: at(0, i)) for (_, n) in parts],
        out_specs=[blk] * 4,
        compiler_params=_cp(("arbitrary",)),
    )(w, m, v, *[a for (a, _) in parts])


def _pad_cols(a, w):
    return jnp.pad(a, ((0, 0), (0, w - a.shape[1])))


def _pad_rows(a, r):
    return jnp.pad(a, ((0, r - a.shape[0]), (0, 0)))


def _pack_b(pf, pr, wo, w2, a2, rows):
    body = jnp.concatenate([pf, pr, wo.reshape(2048, 256), jnp.concatenate([w2, a2], axis=1)], axis=0)
    return _pad_rows(body, rows)


def kernel(x, norm_gain, w_in, fox_forget_bias, rwkv_shift_mix, rwkv_w0, rwkv_w2, rwkv_a0, rwkv_a2, rwkv_k_k, rwkv_k_a, rwkv_r_k, rwkv_ln_w, rwkv_ln_b, w_proj_fox, w_proj_rwkv, w_out, final_norm_gain, loss_target, m_norm_gain, m_w_in, m_fox_forget_bias, m_rwkv_shift_mix, m_rwkv_w0, m_rwkv_w2, m_rwkv_a0, m_rwkv_a2, m_rwkv_k_k, m_rwkv_k_a, m_rwkv_r_k, m_rwkv_ln_w, m_rwkv_ln_b, m_w_proj_fox, m_w_proj_rwkv, m_w_out, m_final_norm_gain, v_norm_gain, v_w_in, v_fox_forget_bias, v_rwkv_shift_mix, v_rwkv_w0, v_rwkv_w2, v_rwkv_a0, v_rwkv_a2, v_rwkv_k_k, v_rwkv_k_a, v_rwkv_r_k, v_rwkv_ln_w, v_rwkv_ln_b, v_w_proj_fox, v_w_proj_rwkv, v_w_out, v_final_norm_gain):
    T, D = x.shape[1], x.shape[2]
    assert D == 2048 and T % LANE == 0
    NI = w_in.shape[2]
    IN = N_DEV * NI
    RB = 4224
    x2 = x[0]
    lt2 = loss_target[0]
    me = _slot(_place())

    wa = _allgather(w_in[0].astype(BF16), "allgather_w_in", relay=True)
    packed_own = _pack_b(w_proj_fox[0], w_proj_rwkv[0], w_out[0], rwkv_w2[0], rwkv_a2[0], RB).astype(BF16)
    sections = [(0, 4096, 0), (4104, 4096, 4096), (8392, 4096, 8192), (4096, 8, 12288), (8200, 96, 12544), (8296, 96, 12672)]
    NP = 12800
    pieces, at_col = [], 0
    for lo, width, pad_lo in sections:
        if pad_lo > at_col:
            pieces.append(jnp.zeros((D, pad_lo - at_col), BF16))
        col = lo
        while col < lo + width:
            d = col // NI
            stop = min(lo + width, (d + 1) * NI)
            pieces.append(wa[d, :, col - d * NI:stop - d * NI])
            col = stop
        at_col = pad_lo + width
    pieces.append(jnp.zeros((D, NP - at_col), BF16))
    w_pad = jnp.concatenate(pieces, axis=1)
    F_CB, LORA_CB = 96, 49

    mu = rwkv_shift_mix
    mu_main = mu[:, 0:4096]
    mu_lora = jnp.concatenate([_pad_cols(mu[:, 4096:4192], LANE), _pad_cols(mu[:, 4192:4288], LANE)], axis=1)
    bias_pad = _pad_cols(fox_forget_bias, LANE)
    rk_flat = rwkv_r_k.reshape(1, 1024)
    gf = final_norm_gain.reshape(1, D)

    tb = min(256, T)
    tbh = min(128, T)
    (h,) = _rowcall("rms_fwd", T, tb, [_rows(x2, tb, D), _whole(norm_gain)],
                    lambda i, xv, g: ([_rms_math(xv, g)], []), [(D, BF16)])
    u, wb = _mm(h, w_pad, tm=1024, tn=1280, name="mm_in", side=_allgather_side(packed_own))
    wpf = wb[:, 0:1024, :].transpose(1, 0, 2).reshape(1024, D)
    wpr = wb[:, 1024:2048, :].transpose(1, 0, 2).reshape(1024, D)
    wo = wb[:, 2048:4096, :].reshape(N_DEV * 256, D)
    w2p = _pad_rows(wb[:, 4096:4192, 0:128].transpose(1, 0, 2).reshape(96, 1024).astype(F32), LANE)
    a2p = _pad_rows(wb[:, 4096:4192, 128:256].transpose(1, 0, 2).reshape(96, 1024).astype(F32), LANE)

    c8 = _gates_fwd(u, bias_pad, T, F_CB)
    c3 = c8.reshape(8, 1, T)
    o_raw, o_a, lse = _attn_fwd(u, c3, T)

    def shift_body(i, um, hm, ul, hl, mm_, ml):
        outs = []
        for uv, hv, mv in ((um, hm, mm_), (ul, hl, ml)):
            hv = jnp.where(i == 0, 0.0, hv)
            prev = pltpu.roll(jnp.concatenate([hv, uv], axis=0), 1, 0)[SUB:]
            outs.append(uv + (prev - uv) * mv)
        return outs, []

    def halo_prev(arr, w, cb):
        return (arr, (SUB, w), lambda i: (jnp.maximum(i * (tbh // SUB) - 1, 0), cb))

    xs, xl = _rowcall("rwkv_shift_fwd", T, tbh,
                      [_rows(u, tbh, 4096, 1), halo_prev(u, 4096, 1), _rows(u, tbh, 256, LORA_CB), halo_prev(u, 256, LORA_CB),
                       _whole(mu_main), _whole(mu_lora)],
                      shift_body, [(4096, F32), (256, F32)])

    prep_par = [_whole(rwkv_w0), _whole(rwkv_a0), _whole(rwkv_k_k), _whole(rwkv_k_a), _whole(w2p), _whole(a2p)]
    prep_rows = [_rows(xs, tbh, 1024, 1), _rows(xl, tbh, LANE, 0), _rows(xl, tbh, LANE, 1)]
    lw, kp, an, bb = _rowcall("rwkv_prep_fwd", T, tbh, prep_rows + prep_par,
                              lambda i, *a: (list(_prep_math(*a)), []), [(1024, F32)] * 4)
    y, states = _scan_fwd(xs, lw, kp, an, bb, T)
    post_rows = [_rows(y, tbh, 1024), _rows(xs, tbh, 1024, 0), _rows(kp, tbh, 1024), _rows(xs, tbh, 1024, 2), _rows(xs, tbh, 1024, 3)]
    post_par = [_whole(rwkv_ln_w), _whole(rwkv_ln_b), _whole(rk_flat)]
    (o_b,) = _rowcall("rwkv_post_fwd", T, tbh, post_rows + post_par,
                      lambda i, *a: ([_post_math(*a)], []), [(1024, BF16)])

    pa = _mm(o_a, wpf, name="mm_proj_fox")
    pb = _mm(o_b, wpr, name="mm_proj_rwkv")
    merge_rows = [_rows(u, tb, D, 4), _rows(u, tb, D, 5), _rows(pa, tb, D), _rows(pb, tb, D)]
    (mg,) = _rowcall("merge_fwd", T, tb, merge_rows, lambda i, *a: ([_merge_math(*a)], []), [(D, BF16)])
    mo = _mm(mg, wo, name="mm_out")

    def head_body(i, xv, mov, ltv, g):
        out = xv + mov
        r = lax.rsqrt(jnp.mean(out * out, axis=-1, keepdims=True) + RMS_EPS)
        yn = out * r
        err = yn * g - ltv
        loss = 0.5 * jnp.sum(jnp.sum(err * err, axis=-1, keepdims=True), axis=0, keepdims=True) / D
        dyv = err / D
        dyn = dyv * g
        dout = r * (dyn - yn * jnp.mean(dyn * yn, axis=-1, keepdims=True))
        return [dout], [loss, jnp.sum(dyv * yn, axis=0, keepdims=True)]

    dout, loss_p, dgf_p = _rowcall("loss_head", T, tb, [_rows(x2, tb, D), _rows(mo, tb, D), _rows(lt2, tb, D), _whole(gf)],
                                   head_body, [(D, F32)], [(1, 1), (1, D)])

    dm = _mm(dout, wo, tb=True, name="mm_out_dx")
    dwo = _mm(mg, dout, ta=True, out_dtype=BF16, name="mm_out_dw")

    def merge_bwd_body(i, ga, gb, pav, pbv, dmv):
        _, vjp = jax.vjp(_merge_math, ga, gb, pav, pbv)
        dga, dgb, dpa, dpb = vjp(dmv)
        return [dga, dgb, dpa, dpb], []

    dga, dgb, dpa, dpb = _rowcall("merge_bwd", T, tb, merge_rows + [_rows(dm, tb, D)], merge_bwd_body,
                                  [(D, BF16), (D, BF16), (D, BF16), (D, BF16)])
    doa = _mm(dpa, wpf, tb=True, name="mm_proj_fox_dx")
    dwpf = _mm(o_a, dpa, ta=True, out_dtype=BF16, name="mm_proj_fox_dw")
    dob = _mm(dpb, wpr, tb=True, name="mm_proj_rwkv_dx")
    dwpr = _mm(o_b, dpb, ta=True, out_dtype=BF16, name="mm_proj_rwkv_dw")

    do_b, dza, delta = _attn_delta(u, c3, o_raw, lse, doa, T)
    dq, dk, dv, dc3 = _attn_bwd(u, c3, lse, do_b, delta, T)
    dfl, dbias_p = _gates_bwd(dc3.reshape(8, T), u, bias_pad, T, F_CB)

    def post_bwd_body(i, yv, rv, kpv, vv, zv, lnw, lnb, rkv, dobv):
        _, vjp = jax.vjp(_post_math, yv, rv, kpv, vv, zv, lnw, lnb, rkv)
        dy_, dr_, dkp_, dv_, dz_, dlnw, dlnb, drk = vjp(dobv)
        return [dy_, dr_, dkp_, dv_, dz_], [dlnw, dlnb, drk]

    dy_s, dr_p, dkp_p, dv_p, dzb, dlnw_p, dlnb_p, drk_p = _rowcall(
        "rwkv_post_bwd", T, tbh, post_rows + post_par + [_rows(dob, tbh, 1024)], post_bwd_body,
        [(1024, F32)] * 5, [(1, 1024)] * 3)
    dr_s, dlw, dkp_s, dv_s, dan, dbb = _scan_bwd(xs, lw, kp, an, bb, states, dy_s, T)

    def prep_bwd_body(i, xk, xwd, xad, w0, a0, kkw, kaw, w2v, a2v, dlw_, dkp1, dkp2, dan_, dbb_, dr1, dr2, dv1, dv2, dz_):
        _, vjp = jax.vjp(_prep_math, xk, xwd, xad, w0, a0, kkw, kaw, w2v, a2v)
        dxk, dxwd, dxad, dw0, da0, dkk, dka, dw2, da2 = vjp((dlw_, dkp1 + dkp2, dan_, dbb_))
        return [[dr1 + dr2, dxk, dv1 + dv2, dz_], [dxwd, dxad]], [dw0, da0, dkk, dka, dw2, da2]

    cots = [dlw, dkp_s, dkp_p, dan, dbb, dr_s, dr_p, dv_s, dv_p, dzb]
    dxs, dxl, dw0_p, da0_p, dkk_p, dka_p, dw2_p, da2_p = _rowcall(
        "rwkv_prep_bwd", T, tbh, prep_rows + prep_par + [_rows(c_, tbh, 1024) for c_ in cots], prep_bwd_body,
        [(4096, F32), (256, F32)], [(1, 1024)] * 4 + [(LANE, 1024)] * 2)

    def shift_bwd_body(i, dm_, hm, dl_, hl, um, pm, ul, pl_, mm_, ml):
        last = i == T // tbh - 1
        outs, accs = [], []
        for dv_, hv, uv, pv, mv in ((dm_, hm, um, pm, mm_), (dl_, hl, ul, pl_, ml)):
            hv = jnp.where(last, 0.0, hv)
            nxt = pltpu.roll(jnp.concatenate([dv_, hv], axis=0), tbh + SUB - 1, 0)[:tbh]
            pv = jnp.where(i == 0, 0.0, pv)
            prev = pltpu.roll(jnp.concatenate([pv, uv], axis=0), 1, 0)[SUB:]
            outs.append(dv_ * (1.0 - mv) + nxt * mv)
            accs.append(jnp.sum(dv_ * (prev - uv), axis=0, keepdims=True))
        return outs, accs

    def halo_next(arr, w, cb):
        last_blk = T // SUB - 1
        return (arr, (SUB, w), lambda i: (jnp.minimum((i + 1) * (tbh // SUB), last_blk), cb))

    du_b, du_l, dmu_main_p, dmu_lora_p = _rowcall(
        "rwkv_shift_bwd", T, tbh,
        [_rows(dxs, tbh, 4096), halo_next(dxs, 4096, 0), _rows(dxl, tbh, 256), halo_next(dxl, 256, 0),
         _rows(u, tbh, 4096, 1), halo_prev(u, 4096, 1), _rows(u, tbh, 256, LORA_CB), halo_prev(u, 256, LORA_CB),
         _whole(mu_main), _whole(mu_lora)],
        shift_bwd_body, [(4096, BF16), (256, BF16)], [(1, 4096), (1, 256)])

    du = jnp.concatenate([dq.astype(BF16), dk.astype(BF16), dv.astype(BF16), dza.astype(BF16), du_b, dga, dgb,
                          dfl.astype(BF16), jnp.zeros((T, LANE), BF16), du_l], axis=1)
    dw_pad_t = _mm(du, h, ta=True, out_dtype=BF16, tm=1280, tn=1024, name="mm_in_dw")

    by_col = sorted(sections)
    blocks = []
    for d in range(N_DEV):
        parts = []
        for lo, width, pad_lo in by_col:
            a, b = max(lo, d * NI), min(lo + width, (d + 1) * NI)
            if a < b:
                parts.append(dw_pad_t[pad_lo + a - lo:pad_lo + b - lo, :])
        blocks.append(jnp.concatenate(parts, axis=0)[None])
    ga = jnp.concatenate(blocks, axis=0)
    lora_g = jnp.concatenate([dw2_p[:96].reshape(96, N_DEV, 128).transpose(1, 0, 2),
                              da2_p[:96].reshape(96, N_DEV, 128).transpose(1, 0, 2)], axis=2).astype(BF16)
    gb = jnp.concatenate([dwpf.reshape(1024, N_DEV, 256).transpose(1, 0, 2),
                          dwpr.reshape(1024, N_DEV, 256).transpose(1, 0, 2),
                          dwo.reshape(N_DEV, 2048, 256), lora_g, jnp.zeros((N_DEV, RB - 4192, 256), BF16)], axis=1)

    xx, yy, cc = _place()
    chip_slots = jnp.stack([_slot((*chip, cc)) for chip in _other_chips(xx, yy)]).astype(jnp.int32)
    r1a, r1b = _exchange_pair(ga, gb)
    sa = _pair_add("pair_add_w_in", ga, r1a, chip_slots, 256, by_cols=True)
    sb = _pair_add("pair_add_packed", gb, r1b, chip_slots, 384)
    ga_own = lax.dynamic_index_in_dim(ga, me, 0, keepdims=True)
    gb_own = lax.dynamic_index_in_dim(gb, me, 0, keepdims=True)
    dh, t1a, t1b = _mm(du, w_pad, tb=True, tm=1024, tn=1024, tk=NP // 10, name="mm_in_dx", side=_exchange_first_side(sa, sb))
    core = jnp.stack([cc]).astype(jnp.int32)
    t2a, t2b = _exchange_second(_axis_add("axis_add_w_in", sa, t1a, core, 256, by_cols=True),
                                _axis_add("axis_add_packed", sb, t1b, core, 384))

    def rms_bwd_body(i, xv, g, dhv, doutv):
        _, vjp = jax.vjp(_rms_math, xv, g)
        dx_, dg_ = vjp(dhv)
        return [dx_ + doutv], [dg_]

    grad_x2, dng_p = _rowcall("rms_bwd", T, tb, [_rows(x2, tb, D), _whole(norm_gain), _rows(dh, tb, D), _rows(dout, tb, D)],
                              rms_bwd_body, [(D, F32)], [(1, D)])

    dmu = jnp.concatenate([dmu_main_p, dmu_lora_p[:, 0:96], dmu_lora_p[:, 128:224]], axis=1)
    small_parts = [dng_p, dbias_p[:, 0:8], dmu, dw0_p, da0_p, dkk_p, dka_p, drk_p, dlnw_p, dlnb_p, dgf_p, loss_p]
    SR = 128
    small = _pad_cols(jnp.concatenate(small_parts, axis=1), SR * LANE).reshape(SR, LANE)
    rs = _allgather(small, "allgather_small")

    w_in_outs = _adamw("adamw_w_in", w_in[0].T, m_w_in[0].T, v_w_in[0].T, [(ga_own, 1), (r1a, 1), (t1a, 1), (t2a, 1)], 256, by_cols=True)
    g_in, d_in, m_in, v_in = [o.T for o in w_in_outs]
    pk = lambda pf, pr, wo_, w2_, a2_: _pack_b(pf[0], pr[0], wo_[0], w2_[0], a2_[0], RB)
    outs_b = _adamw("adamw_packed", pk(w_proj_fox, w_proj_rwkv, w_out, rwkv_w2, rwkv_a2),
                    pk(m_w_proj_fox, m_w_proj_rwkv, m_w_out, m_rwkv_w2, m_rwkv_a2),
                    pk(v_w_proj_fox, v_w_proj_rwkv, v_w_out, v_rwkv_w2, v_rwkv_a2), [(gb_own, 1), (r1b, 1), (t1b, 1), (t2b, 1)], 384)

    def pack_small(ng, fb, sm, w0, a0, kk_, ka_, rk_, lnw, lnb, fg):
        parts = [ng, fb, sm, w0, a0, kk_, ka_, rk_.reshape(1, 1024), lnw, lnb, fg.reshape(1, D), jnp.zeros((1, 1), F32)]
        return _pad_cols(jnp.concatenate(parts, axis=1), SR * LANE).reshape(SR, LANE)

    outs_s = _adamw("adamw_small",
                    pack_small(norm_gain, fox_forget_bias, rwkv_shift_mix, rwkv_w0, rwkv_a0, rwkv_k_k, rwkv_k_a, rwkv_r_k,
                               rwkv_ln_w, rwkv_ln_b, final_norm_gain),
                    pack_small(m_norm_gain, m_fox_forget_bias, m_rwkv_shift_mix, m_rwkv_w0, m_rwkv_a0, m_rwkv_k_k, m_rwkv_k_a,
                               m_rwkv_r_k, m_rwkv_ln_w, m_rwkv_ln_b, m_final_norm_gain),
                    pack_small(v_norm_gain, v_fox_forget_bias, v_rwkv_shift_mix, v_rwkv_w0, v_rwkv_a0, v_rwkv_k_k, v_rwkv_k_a,
                               v_rwkv_r_k, v_rwkv_ln_w, v_rwkv_ln_b, v_final_norm_gain),
                    [(rs, N_DEV)], SR)

    def unpack_b(pkd):
        return dict(w_proj_fox=pkd[0:1024][None], w_proj_rwkv=pkd[1024:2048][None], w_out=pkd[2048:4096].reshape(1, 256, D),
                    rwkv_w2=pkd[4096:4192, 0:128][None], rwkv_a2=pkd[4096:4192, 128:256][None])

    def unpack_s(pkd):
        flat = pkd.reshape(1, SR * LANE)
        names = [("norm_gain", D), ("fox_forget_bias", 8), ("rwkv_shift_mix", 4288), ("rwkv_w0", 1024), ("rwkv_a0", 1024),
                 ("rwkv_k_k", 1024), ("rwkv_k_a", 1024), ("rwkv_r_k", 1024), ("rwkv_ln_w", 1024), ("rwkv_ln_b", 1024),
                 ("final_norm_gain", D), ("loss", 1)]
        out, off = {}, 0
        for nm, n in names:
            out[nm] = flat[:, off:off + n]
            off += n
        out["rwkv_r_k"] = out["rwkv_r_k"].reshape(1, 16, 64)
        out["final_norm_gain"] = out["final_norm_gain"].reshape(D)
        return out

    order = ["norm_gain", "w_in", "fox_forget_bias", "rwkv_shift_mix", "rwkv_w0", "rwkv_w2", "rwkv_a0", "rwkv_a2", "rwkv_k_k",
             "rwkv_k_a", "rwkv_r_k", "rwkv_ln_w", "rwkv_ln_b", "w_proj_fox", "w_proj_rwkv", "w_out", "final_norm_gain"]
    result = []
    loss = None
    for kind, big in enumerate((g_in, d_in, m_in, v_in)):
        d = {**unpack_b(outs_b[kind]), **unpack_s(outs_s[kind]), "w_in": big[None]}
        if kind == 0:
            loss = d["loss"].reshape(())
        result += [d[n] for n in order]
    return (loss, grad_x2[None], *result)
```

```python
import functools

import jax
import jax.numpy as jnp
from jax import lax
from jax.experimental import pallas as pl
from jax.experimental.pallas import tpu as pltpu

F32 = jnp.float32
BF16 = jnp.bfloat16
HI = lax.Precision.HIGHEST
H3 = lax.Precision.HIGH
MESH = pl.DeviceIdType.MESH

FOX_HD = 128
RW_HD = 64
RMS_EPS = 1e-6
GN_EPS = 64e-5
L2_EPS = 1e-12
ADAM_LR = 0.001
ADAM_B1 = 0.9
ADAM_B2 = 0.999
ADAM_EPS = 1e-08
ADAM_WD = 0.01
ADAM_STEP = 10

LANE = 128
SUB = 8
VMEM_LIMIT = 56 * 1024 * 1024
N_DEV = 8
CHUNK = 64
SCAN_GROUP = 4
SCAN_PAIRS = 2
PS = None
NEG = -1e30


def _scan_shape(T):
    c = min(CHUNK, T)
    return c, min(SCAN_GROUP, T // c)


def _cp(sem=None):
    return pltpu.CompilerParams(dimension_semantics=sem, vmem_limit_bytes=VMEM_LIMIT)


def _sigmoid(x):
    return jax.nn.sigmoid(x)


def _softplus(x):
    return jnp.maximum(x, 0.0) + jnp.log(1.0 + jnp.exp(-jnp.abs(x)))


def _nn(a, b, prec=None):
    return lax.dot_general(a, b, (((1,), (0,)), ((), ())), precision=prec, preferred_element_type=F32)


def _nt(a, b, prec=None):
    return lax.dot_general(a, b, (((1,), (1,)), ((), ())), precision=prec, preferred_element_type=F32)


def _tn(a, b, prec=None):
    return lax.dot_general(a, b, (((0,), (0,)), ((), ())), precision=prec, preferred_element_type=F32)


def _iota2(shape, dim):
    return lax.broadcasted_iota(jnp.int32, shape, dim)


def _seg_sum(x):
    r = _iota2((LANE, LANE), 0) // RW_HD
    c = _iota2((LANE, LANE), 1) // RW_HD
    bd = (r == c).astype(F32)
    parts = [_nn(x[:, j * LANE:(j + 1) * LANE], bd, H3) for j in range(x.shape[1] // LANE)]
    return parts[0] if len(parts) == 1 else jnp.concatenate(parts, axis=1)


def _mm(a, b, *, ta=False, tb=False, out_dtype=F32, tm=512, tn=512, tk=None, name, side=None):
    assert not (ta and tb)
    K, M = a.shape if ta else a.shape[::-1]
    N = b.shape[0] if tb else b.shape[1]
    tm, tn = min(tm, M), min(tn, N)
    tk = K if tk is None else tk
    nk = K // tk
    assert M % tm == 0 and N % tn == 0 and K % tk == 0
    gi, gj = M // tm, N // tn
    a_spec = pl.BlockSpec((tk, tm), lambda i, j, k: (k, i)) if ta else pl.BlockSpec((tm, tk), lambda i, j, k: (i, k))
    b_spec = pl.BlockSpec((tn, tk), lambda i, j, k: (j, k)) if tb else pl.BlockSpec((tk, tn), lambda i, j, k: (k, j))
    side_ins, side_outs, side_scr, side_make = side if side is not None else ((), (), (), None)
    n_si, n_so = len(side_ins), len(side_outs)
    n_acc = 0 if nk == 1 else 1

    def body(*refs):
        a_ref, b_ref = refs[:2]
        o_ref = refs[2 + n_si]
        scr = refs[3 + n_si + n_so:]
        k = pl.program_id(2)
        if side_make is not None:
            start, finish = side_make(refs[2:2 + n_si], refs[3 + n_si:3 + n_si + n_so], scr[n_acc:])
            first = jnp.logical_and(jnp.logical_and(pl.program_id(0) == 0, pl.program_id(1) == 0), k == 0)
            last = jnp.logical_and(jnp.logical_and(pl.program_id(0) == gi - 1, pl.program_id(1) == gj - 1), k == nk - 1)
            pl.when(first)(start)
        av = a_ref[...].astype(BF16)
        bv = b_ref[...].astype(BF16)
        p = _tn(av, bv) if ta else _nt(av, bv) if tb else _nn(av, bv)
        if nk == 1:
            o_ref[...] = p.astype(out_dtype)
        else:
            acc_ref = scr[0]

            @pl.when(k == 0)
            def _():
                acc_ref[...] = p

            @pl.when(k > 0)
            def _():
                acc_ref[...] += p

            @pl.when(k == nk - 1)
            def _():
                o_ref[...] = acc_ref[...].astype(out_dtype)
        if side_make is not None:
            pl.when(last)(finish)

    any_spec = pl.BlockSpec(memory_space=pl.ANY)
    res = pl.pallas_call(
        body, name=name,
        out_shape=[jax.ShapeDtypeStruct((M, N), out_dtype)] + list(side_outs),
        grid=(gi, gj, nk),
        in_specs=[a_spec, b_spec] + [any_spec] * n_si,
        out_specs=[pl.BlockSpec((tm, tn), lambda i, j, k: (i, j))] + [any_spec] * n_so,
        scratch_shapes=([] if nk == 1 else [pltpu.VMEM((tm, tn), F32)]) + list(side_scr),
        compiler_params=_cp(("arbitrary",) * 3 if side is not None else ("parallel", "parallel", "arbitrary")),
    )(a, b, *side_ins)
    return res if side is not None else res[0]


def _rows(arr, tb, w, cb=0):
    return (arr, (tb, w), lambda i: (i, cb))


def _whole(arr):
    nd = arr.ndim
    return (arr, arr.shape, lambda i: (0,) * nd)


def _rowcall(name, T, tb, ins, body, outs, accs=()):
    n_in, n_out = len(ins), len(outs)

    def kern(*refs):
        i = pl.program_id(0)
        vals = [r[...] for r in refs[:n_in]]
        ro, ao = body(i, *vals)
        for r, v in zip(refs[n_in:n_in + n_out], ro):
            if isinstance(v, (list, tuple)):
                off = 0
                for piece in v:
                    w = piece.shape[1]
                    r[:, off:off + w] = piece.astype(r.dtype)
                    off += w
            else:
                r[...] = v.astype(r.dtype)
        if accs:
            acc_refs = refs[n_in + n_out:]

            @pl.when(i == 0)
            def _():
                for r in acc_refs:
                    r[...] = jnp.zeros(r.shape, F32)

            for r, v in zip(acc_refs, ao):
                r[...] += v

    out_shape = [jax.ShapeDtypeStruct((T, w), dt) for (w, dt) in outs] + [jax.ShapeDtypeStruct(s, F32) for s in accs]
    out_specs = [pl.BlockSpec((tb, w), lambda i: (i, 0)) for (w, dt) in outs] + [pl.BlockSpec(s, lambda i: (0, 0)) for s in accs]
    res = pl.pallas_call(
        kern, name=name,
        out_shape=out_shape,
        grid=(T // tb,),
        in_specs=[pl.BlockSpec(bs, im) for (_, bs, im) in ins],
        out_specs=out_specs,
        compiler_params=_cp(("arbitrary",)),
    )(*[a for (a, _, _) in ins])
    return res


def _rms_math(x, g):
    r = lax.rsqrt(jnp.mean(x * x, axis=-1, keepdims=True) + RMS_EPS)
    return x * r * g


def _merge_math(ga, gb, pa, pb):
    return _sigmoid(ga) * pa + _sigmoid(gb) * pb


def _prep_math(xk, xwd, xad, w0, a0, kk_w, ka_w, w2p, a2p):
    z = w0 + _nn(jnp.tanh(xwd), w2p, H3)
    w = -_softplus(-z) - 0.5
    lw = -jnp.exp(w)
    ag = _sigmoid(a0 + _nn(xad, a2p, H3))
    p = xk * kk_w
    n = jnp.maximum(jnp.sqrt(_seg_sum(p * p)), L2_EPS)
    kk = p / n
    kp = xk * (1.0 + (ag - 1.0) * ka_w)
    return lw, kp, -kk, kk * ag


def _post_math(y, r, kp, v, z, lnw, lnb, rk):
    inv = 1.0 / RW_HD
    mu = _seg_sum(y) * inv
    d = y - mu
    var = _seg_sum(d * d) * inv
    yn = d * lax.rsqrt(var + GN_EPS) * lnw + lnb
    bonus = _seg_sum(r * kp * rk) * v
    return (yn + bonus) * (z * _sigmoid(z))


@functools.partial(jax.custom_vjp, nondiff_argnums=(1,))
def _unit_inverses(ms, depth):
    eye = (_iota2(ms[0].shape, 0) == _iota2(ms[0].shape, 1)).astype(F32)
    half = ms[0].shape[1]
    mp = [_nn(m, m, PS) for m in ms]
    inv = [eye + m for m in ms]
    n = 2
    while n < depth:
        last = 2 * n >= depth
        for i in range(len(ms)):
            if last:
                inv[i] = inv[i] + _nn(mp[i], inv[i], PS)
            else:
                z = _nn(mp[i], jnp.concatenate([mp[i], inv[i]], axis=1), PS)
                mp[i], inv[i] = z[:, :half], inv[i] + z[:, half:]
        n *= 2
    return tuple(inv)


def _unit_inverses_fwd(ms, depth):
    inv = _unit_inverses(ms, depth)
    return inv, inv


def _unit_inverses_bwd(depth, inv, cts):
    left = [_tn(t, g, PS) for t, g in zip(inv, cts)]
    return (tuple(_nt(l, t, PS) for l, t in zip(left, inv)),)


_unit_inverses.defvjp(_unit_inverses_fwd, _unit_inverses_bwd)


def _scan_group(s0s, *flat):
    P = len(s0s)
    G = len(flat) // (6 * P)
    ch = [flat[6 * i:6 * i + 6] for i in range(P * G)]
    C = ch[0][0].shape[0]
    C2 = 2 * C
    cat = jnp.concatenate
    m0 = _iota2((1, LANE), 1) < RW_HD
    mask0 = m0.astype(F32)
    mask1 = 1.0 - mask0
    r2 = _iota2((C2, C2), 0)
    c2 = _iota2((C2, C2), 1)
    dist = r2 - c2
    in_head = dist <= r2 % C
    lower = (_iota2((C, C), 0) >= _iota2((C, C), 1)).astype(F32)
    bd = (_iota2((LANE, LANE), 0) // RW_HD) == (_iota2((LANE, LANE), 1) // RW_HD)

    def tri(m, strict):
        return jnp.where(dist > 0 if strict else dist >= 0, jnp.where(in_head, m, 0.0), 0.0)

    def sel(z):
        return jnp.where(m0, z[:C], z[C:])

    gs = [_nn(lower, c[1], HI) for c in ch]
    pre = []
    for (r, lw, k, v, a, b), g in zip(ch, gs):
        g_end = jnp.sum(lw, axis=0, keepdims=True)
        en = jnp.exp(-g)
        ec = jnp.exp(g_end - g)
        pre.append(dict(at=a * jnp.exp(g - lw), rt=r * jnp.exp(g), bt=b * en, kt=k * en, bh=b * ec, kh=k * ec,
                        dec=jnp.exp(g_end), v=v))
    grams = [_nt(cat([p["at"] * mask0, p["at"] * mask1, p["rt"] * mask0, p["rt"] * mask1], axis=0),
                 cat([p["bt"], p["bt"], p["kt"], p["kt"]], axis=0), PS) for p in pre]
    tinv = _unit_inverses(tuple(tri(gm[:C2, :C2], True) for gm in grams), C)
    xv =[sel(_nn(tri(gm[:C2, C2:], True), cat([p["v"], p["v"]], axis=0), PS)) for gm, p in zip(grams, pre)]
    ys, s = [None] * (P * G), list(s0s)
    for i in range(G):
        for q in range(P):
            n = q * G + i
            p, gm = pre[n], grams[n]
            sx = _nt(cat([p["at"], p["rt"]], axis=0), s[q], PS)
            x = sx[:C] + xv[n]
            u = sel(_nn(tinv[n], cat([x, x], axis=0), PS))
            v = p["v"]
            ys[n] = sx[C:] + sel(_nn(cat([tri(gm[C2:, :C2], False), tri(gm[C2:, C2:], False)], axis=1),
                                     cat([u, u, v, v], axis=0), PS))
            s[q] = s[q] * p["dec"] + jnp.where(bd, _tn(cat([u, v], axis=0), cat([p["bh"], p["kh"]], axis=0), PS), 0.0)
    return tuple(ys), tuple(s)


def _scan_fwd(xs, lw, kp, an, bb, T):
    C, G = _scan_shape(T)
    P = SCAN_PAIRS
    nc = T // (C * G)
    npair = 1024 // LANE

    def kern(r_ref, lw_ref, k_ref, v_ref, a_ref, b_ref, y_ref, st_ref, s_scr):
        n = pl.program_id(1)

        @pl.when(n == 0)
        def _():
            s_scr[...] = jnp.zeros(s_scr.shape, F32)

        st_ref[0] = s_scr[...]
        ins = (r_ref, lw_ref, k_ref, v_ref, a_ref, b_ref)
        ys, s1 = _scan_group(tuple(s_scr[q] for q in range(P)),
                             *[ref[i * C:(i + 1) * C, q * LANE:(q + 1) * LANE] for q in range(P) for i in range(G) for ref in ins])
        for q in range(P):
            for i in range(G):
                y_ref[i * C:(i + 1) * C, q * LANE:(q + 1) * LANE] = ys[q * G + i]
            s_scr[q] = s1[q]

    def col(off):
        return pl.BlockSpec((C * G, P * LANE), lambda p, n: (n, off // P + p))

    return pl.pallas_call(
        kern, name="rwkv_scan_fwd",
        out_shape=[jax.ShapeDtypeStruct((T, 1024), F32), jax.ShapeDtypeStruct((nc, npair, LANE, LANE), F32)],
        grid=(npair // P, nc),
        in_specs=[col(0), col(0), col(0), col(16), col(0), col(0)],
        out_specs=[col(0), pl.BlockSpec((1, P, LANE, LANE), lambda p, n: (n, p, 0, 0))],
        scratch_shapes=[pltpu.VMEM((P, LANE, LANE), F32)],
        compiler_params=_cp(("parallel", "arbitrary")),
    )(xs, lw, kp, xs, an, bb)


def _scan_bwd(xs, lw, kp, an, bb, states, dy, T):
    C, G = _scan_shape(T)
    P = SCAN_PAIRS
    nc = T // (C * G)
    npair = 1024 // LANE

    def kern(r_ref, lw_ref, k_ref, v_ref, a_ref, b_ref, st_ref, dy_ref, dr_ref, dlw_ref, dk_ref, dv_ref, da_ref, db_ref, ds_scr):
        n = pl.program_id(1)

        @pl.when(n == 0)
        def _():
            ds_scr[...] = jnp.zeros(ds_scr.shape, F32)

        ins = (r_ref, lw_ref, k_ref, v_ref, a_ref, b_ref)
        units = [(q, i) for q in range(P) for i in range(G)]
        _, vjp = jax.vjp(_scan_group, tuple(st_ref[0, q] for q in range(P)),
                         *[ref[i * C:(i + 1) * C, q * LANE:(q + 1) * LANE] for q, i in units for ref in ins])
        grads = vjp((tuple(dy_ref[i * C:(i + 1) * C, q * LANE:(q + 1) * LANE] for q, i in units),
                     tuple(ds_scr[q] for q in range(P))))
        for q in range(P):
            ds_scr[q] = grads[0][q]
        outs = (dr_ref, dlw_ref, dk_ref, dv_ref, da_ref, db_ref)
        for n_, (q, i) in enumerate(units):
            for t, ref in enumerate(outs):
                ref[i * C:(i + 1) * C, q * LANE:(q + 1) * LANE] = grads[1 + 6 * n_ + t]

    def col(off):
        return pl.BlockSpec((C * G, P * LANE), lambda p, n: (nc - 1 - n, off // P + p))

    return pl.pallas_call(
        kern, name="rwkv_scan_bwd",
        out_shape=[jax.ShapeDtypeStruct((T, 1024), F32)] * 6,
        grid=(npair // P, nc),
        in_specs=[col(0), col(0), col(0), col(16), col(0), col(0),
                  pl.BlockSpec((1, P, LANE, LANE), lambda p, n: (nc - 1 - n, p, 0, 0)), col(0)],
        out_specs=[col(0)] * 6,
        scratch_shapes=[pltpu.VMEM((P, LANE, LANE), F32)],
        compiler_params=_cp(("parallel", "arbitrary")),
    )(xs, lw, kp, xs, an, bb, states, dy)


def _gates_fwd(u, bias_pad, T, f_cb):
    nb = T // LANE

    def kern(f_ref, b_ref, c_ref):
        x = f_ref[...] + b_ref[...]
        lf = jnp.minimum(x, 0.0) - jnp.log(1.0 + jnp.exp(-jnp.abs(x)))
        lft = lf.T
        ut = (_iota2((LANE, LANE), 0) <= _iota2((LANE, LANE), 1)).astype(F32)
        carry = jnp.zeros((LANE, 1), F32)
        for blk in range(nb):
            seg = lft[:, blk * LANE:(blk + 1) * LANE]
            cs = _nn(seg, ut, HI) + carry
            c_ref[:, blk * LANE:(blk + 1) * LANE] = cs[:SUB, :]
            carry = carry + jnp.sum(seg, axis=1, keepdims=True)

    return pl.pallas_call(
        kern, name="fox_gates_fwd",
        out_shape=jax.ShapeDtypeStruct((SUB, T), F32),
        grid=(1,),
        in_specs=[pl.BlockSpec((T, LANE), lambda i: (0, f_cb)), pl.BlockSpec((1, LANE), lambda i: (0, 0))],
        out_specs=pl.BlockSpec((SUB, T), lambda i: (0, 0)),
        compiler_params=_cp(("arbitrary",)),
    )(u, bias_pad)


def _gates_bwd(dc, u, bias_pad, T, f_cb):
    nb = T // LANE

    def kern(dc_ref, f_ref, b_ref, dfl_ref, db_ref):
        dcv = jnp.concatenate([dc_ref[...], jnp.zeros((LANE - SUB, T), F32)], axis=0)
        lt = (_iota2((LANE, LANE), 0) >= _iota2((LANE, LANE), 1)).astype(F32)
        carry = jnp.zeros((LANE, 1), F32)
        pieces = [None] * nb
        for blk in range(nb - 1, -1, -1):
            seg = dcv[:, blk * LANE:(blk + 1) * LANE]
            pieces[blk] = _nn(seg, lt, HI) + carry
            carry = carry + jnp.sum(seg, axis=1, keepdims=True)
        dlf = (pieces[0] if nb == 1 else jnp.concatenate(pieces, axis=1)).T
        x = f_ref[...] + b_ref[...]
        dfl = dlf * _sigmoid(-x)
        dfl_ref[...] = dfl
        db_ref[...] = jnp.sum(dfl, axis=0, keepdims=True)

    return pl.pallas_call(
        kern, name="fox_gates_bwd",
        out_shape=[jax.ShapeDtypeStruct((T, LANE), F32), jax.ShapeDtypeStruct((1, LANE), F32)],
        grid=(1,),
        in_specs=[pl.BlockSpec((SUB, T), lambda i: (0, 0)), pl.BlockSpec((T, LANE), lambda i: (0, f_cb)),
                  pl.BlockSpec((1, LANE), lambda i: (0, 0))],
        out_specs=[pl.BlockSpec((T, LANE), lambda i: (0, 0)), pl.BlockSpec((1, LANE), lambda i: (0, 0))],
        compiler_params=_cp(("arbitrary",)),
    )(dc, u, bias_pad)


ATTN_HEADS = 4


def _attn_block(T):
    return 256 if T % 256 == 0 and T >= 512 else 128


def _attn_fwd(u, c3, T):
    H, HP = 8, ATTN_HEADS
    bq = _attn_block(T)
    nq = T // bq
    scale = FOX_HD ** -0.5
    lanes = [slice(h * LANE, (h + 1) * LANE) for h in range(HP)]

    def kern(q_ref, k_ref, v_ref, z_ref, cq_ref, ck_ref, o_ref, oa_ref, lse_ref):
        i = pl.program_id(1)
        q = [(q_ref[:, ln] * scale).astype(BF16) for ln in lanes]
        c0 = [cq_ref[h][:, 0:1] for h in range(HP)]

        def step(j, carry, diagonal=False):
            off = pl.multiple_of(j * bq, bq)
            s = [_nt(q[h], k_ref[pl.ds(off, bq), lanes[h]].astype(BF16)) + (c0[h] - ck_ref[h, :, pl.ds(off, bq)])
                 for h in range(HP)]
            ps, out = [], []
            for h in range(HP):
                m, l, acc = carry[h]
                sh = s[h]
                if diagonal:
                    sh = jnp.where(_iota2((bq, bq), 1) <= _iota2((bq, bq), 0), sh, NEG)
                m_new = jnp.maximum(m, jnp.max(sh, axis=1, keepdims=True))
                p = jnp.exp(sh - m_new)
                alpha = jnp.exp(m - m_new)
                ps.append(p.astype(BF16))
                out.append((m_new, alpha * l + jnp.sum(p, axis=1, keepdims=True), alpha * acc))
            return tuple((m, l, acc + _nn(ps[h], v_ref[pl.ds(off, bq), lanes[h]].astype(BF16)))
                         for h, (m, l, acc) in enumerate(out))

        init = tuple((jnp.full((bq, 1), NEG, F32), jnp.zeros((bq, 1), F32), jnp.zeros((bq, FOX_HD), F32)) for _ in range(HP))
        res = step(i, lax.fori_loop(0, i, step, init), diagonal=True)
        for h, (m, l, acc) in enumerate(res):
            o = acc / l
            z = z_ref[:, lanes[h]]
            o_ref[:, lanes[h]] = o
            oa_ref[:, lanes[h]] = (o * z * _sigmoid(z)).astype(BF16)
            lse_ref[h] = m + jnp.log(l)

    W = HP * LANE
    return pl.pallas_call(
        kern, name="fox_attn_fwd",
        out_shape=[jax.ShapeDtypeStruct((T, 1024), F32), jax.ShapeDtypeStruct((T, 1024), BF16),
                   jax.ShapeDtypeStruct((H, T, 1), F32)],
        grid=(H // HP, nq),
        in_specs=[pl.BlockSpec((bq, W), lambda g, i: (i, g)),
                  pl.BlockSpec((T, W), lambda g, i: (0, 8 // HP + g)),
                  pl.BlockSpec((T, W), lambda g, i: (0, 16 // HP + g)),
                  pl.BlockSpec((bq, W), lambda g, i: (i, 24 // HP + g)),
                  pl.BlockSpec((HP, 1, bq), lambda g, i: (g, 0, i)),
                  pl.BlockSpec((HP, 1, T), lambda g, i: (g, 0, 0))],
        out_specs=[pl.BlockSpec((bq, W), lambda g, i: (i, g)),
                   pl.BlockSpec((bq, W), lambda g, i: (i, g)),
                   pl.BlockSpec((HP, bq, 1), lambda g, i: (g, i, 0))],
        compiler_params=_cp(("parallel", "arbitrary")),
    )(u, u, u, u, c3, c3)


def _attn_probs(s, lse_i, diagonal):
    if not diagonal:
        return jnp.exp(s - lse_i)
    keep = _iota2(s.shape, 1) <= _iota2(s.shape, 0)
    return jnp.where(keep, jnp.exp(jnp.where(keep, s, NEG) - lse_i), 0.0)


def _attn_delta(u, c3, o, lse, doa, T):
    H, HP = 8, ATTN_HEADS
    bq = _attn_block(T)
    nq = T // bq
    scale = FOX_HD ** -0.5
    lanes = [slice(h * LANE, (h + 1) * LANE) for h in range(HP)]

    def kern(q_ref, k_ref, v_ref, z_ref, c_ref, o_ref, lse_ref, doa_ref, do_ref, dz_ref, dl_ref):
        i = pl.program_id(1)
        z = z_ref[...]
        sg = _sigmoid(z)
        dov = doa_ref[...]
        do_all = (dov * z * sg).astype(BF16)
        do_ref[...] = do_all
        dz_ref[...] = dov * o_ref[...] * (sg * (1.0 + z * (1.0 - sg)))
        qs = [(q_ref[:, ln] * scale).astype(BF16) for ln in lanes]
        dob = [do_all[:, ln] for ln in lanes]
        ioff = pl.multiple_of(i * bq, bq)
        c0 = [c_ref[h, :, pl.ds(ioff, bq)][:, 0:1] for h in range(HP)]
        lse_i = [lse_ref[h] for h in range(HP)]

        def step(j, acc, diagonal=False):
            off = pl.multiple_of(j * bq, bq)
            s = [_nt(qs[h], k_ref[pl.ds(off, bq), lanes[h]].astype(BF16)) + (c0[h] - c_ref[h, :, pl.ds(off, bq)])
                 for h in range(HP)]
            dp = [_nt(dob[h], v_ref[pl.ds(off, bq), lanes[h]].astype(BF16)) for h in range(HP)]
            return tuple(acc[h] + jnp.sum(_attn_probs(s[h], lse_i[h], diagonal) * dp[h], axis=1, keepdims=True)
                         for h in range(HP))

        init = tuple(jnp.zeros((bq, 1), F32) for _ in range(HP))
        res = step(i, lax.fori_loop(0, i, step, init), diagonal=True)
        for h in range(HP):
            dl_ref[h] = res[h]

    W = HP * LANE
    full = lambda cb: pl.BlockSpec((T, W), lambda g, i: (0, cb // HP + g))
    blk = lambda cb: pl.BlockSpec((bq, W), lambda g, i: (i, cb // HP + g))
    return pl.pallas_call(
        kern, name="fox_attn_delta",
        out_shape=[jax.ShapeDtypeStruct((T, 1024), BF16), jax.ShapeDtypeStruct((T, 1024), F32), jax.ShapeDtypeStruct((H, T, 1), F32)],
        grid=(H // HP, nq),
        in_specs=[blk(0), full(8), full(16), blk(24),
                  pl.BlockSpec((HP, 1, T), lambda g, i: (g, 0, 0)),
                  blk(0),
                  pl.BlockSpec((HP, bq, 1), lambda g, i: (g, i, 0)),
                  blk(0)],
        out_specs=[blk(0), blk(0), pl.BlockSpec((HP, bq, 1), lambda g, i: (g, i, 0))],
        compiler_params=_cp(("parallel", "arbitrary")),
    )(u, u, u, u, c3, o, lse, doa)


def _attn_bwd(u, c3, lse, do, delta, T):
    H, HP = 8, ATTN_HEADS
    bq = _attn_block(T)
    nq = T // bq
    scale = FOX_HD ** -0.5
    lanes = [slice(h * LANE, (h + 1) * LANE) for h in range(HP)]

    def kern(q_ref, k_ref, v_ref, c_ref, lse_ref, do_ref, dl_ref, dq_ref, dk_ref, dv_ref, dc_ref):
        j = pl.program_id(1)

        @pl.when(j == 0)
        def _():
            dq_ref[...] = jnp.zeros(dq_ref.shape, F32)

        kj = [k_ref[:, ln].astype(BF16) for ln in lanes]
        vj = [v_ref[:, ln].astype(BF16) for ln in lanes]
        joff = pl.multiple_of(j * bq, bq)
        ck = [c_ref[h, :, pl.ds(joff, bq)] for h in range(HP)]

        def step(i, carry, diagonal=False):
            off = pl.multiple_of(i * bq, bq)
            qs = [(q_ref[pl.ds(off, bq), ln] * scale).astype(BF16) for ln in lanes]
            dob = [do_ref[pl.ds(off, bq), ln] for ln in lanes]
            s = [_nt(qs[h], kj[h]) + (c_ref[h, :, pl.ds(off, bq)][:, 0:1] - ck[h]) for h in range(HP)]
            dp = [_nt(dob[h], vj[h]) for h in range(HP)]
            pb, dsb, dcs = [], [], []
            for h in range(HP):
                p = _attn_probs(s[h], lse_ref[h, pl.ds(off, bq), :], diagonal)
                ds = p * (dp[h] - dl_ref[h, pl.ds(off, bq), :])
                pb.append(p.astype(BF16))
                dsb.append(ds.astype(BF16))
                dcs.append(jnp.sum(ds, axis=0, keepdims=True))
            out = []
            for h, (dk, dv, dc) in enumerate(carry):
                dq_ref[pl.ds(off, bq), lanes[h]] += _nn(dsb[h], kj[h]) * scale
                out.append((dk + _tn(dsb[h], qs[h]), dv + _tn(pb[h], dob[h]), dc - dcs[h]))
            return tuple(out)

        init = tuple((jnp.zeros((bq, FOX_HD), F32), jnp.zeros((bq, FOX_HD), F32), jnp.zeros((1, bq), F32)) for _ in range(HP))
        res = lax.fori_loop(j + 1, nq, step, step(j, init, diagonal=True))
        for h, (dk, dv, dc) in enumerate(res):
            dk_ref[:, lanes[h]] = dk
            dv_ref[:, lanes[h]] = dv
            dc_ref[h] = dc

    W = HP * LANE
    full = lambda cb: pl.BlockSpec((T, W), lambda g, j: (0, cb // HP + g))
    blk = lambda cb: pl.BlockSpec((bq, W), lambda g, j: (j, cb // HP + g))
    col1 = pl.BlockSpec((HP, T, 1), lambda g, j: (g, 0, 0))
    return pl.pallas_call(
        kern, name="fox_attn_bwd",
        out_shape=[jax.ShapeDtypeStruct((T, 1024), F32)] * 3 + [jax.ShapeDtypeStruct((H, 1, T), F32)],
        grid=(H // HP, nq),
        in_specs=[full(0), blk(8), blk(16), pl.BlockSpec((HP, 1, T), lambda g, j: (g, 0, 0)), col1, full(0), col1],
        out_specs=[full(0), blk(0), blk(0), pl.BlockSpec((HP, 1, bq), lambda g, j: (g, 0, j))],
        compiler_params=_cp(("parallel", "arbitrary")),
    )(u, u, u, c3, lse, do, delta)


def _place():
    return lax.axis_index("x"), lax.axis_index("y"), lax.axis_index("c")


def _slot(p):
    return 4 * p[0] + 2 * p[1] + p[2]


def _other_chips(x, y):
    return [(1 - x, y), (x, 1 - y), (1 - x, 1 - y)]


def _allgather_steps(in_refs, out_refs, scratch):
    (src,), (dst,) = in_refs, out_refs
    send_sems, recv_sems, local_sem = scratch
    x, y, c = _place()
    me, sibling = (x, y, c), (x, y, 1 - c)
    chips = _other_chips(x, y)

    def copy(k, block, to, from_input=False):
        d = dst.at[_slot(block)]
        return pltpu.make_async_remote_copy(
            src_ref=src if from_input else d, dst_ref=d, send_sem=send_sems.at[k], recv_sem=recv_sems.at[k],
            device_id=to, device_id_type=MESH)

    def first_copies():
        return [copy(0, me, sibling, True)] + [copy(1 + j, me, (*chip, c), True) for j, chip in enumerate(chips)]

    def start():
        pltpu.make_async_copy(src, dst.at[_slot(me)], local_sem).start()
        for cp in first_copies():
            cp.start()

    def finish():
        passed = []
        for j, chip in enumerate(chips):
            copy(1 + j, (*chip, c), me).wait_recv()
            passed.append(copy(4 + j, (*chip, c), sibling))
            passed[-1].start()
        copy(0, sibling, me).wait_recv()
        for j, chip in enumerate(chips):
            copy(4 + j, (*chip, 1 - c), me).wait_recv()
        for cp in first_copies() + passed:
            cp.wait_send()
        pltpu.make_async_copy(src, dst.at[_slot(me)], local_sem).wait()

    return start, finish


def _allgather_relay_steps(in_refs, out_refs, scratch):
    (src,), (dst,) = in_refs, out_refs
    send_sems, recv_sems, local_sem = scratch
    x, y, c = _place()
    me, sibling = (x, y, c), (x, y, 1 - c)
    x_nbr, y_nbr, diag = (1 - x, y, c), (x, 1 - y, c), (1 - x, 1 - y, c)
    flip = lambda a, bit: a + bit - 2 * a * bit
    relay_from = (flip(x, 1 - c), flip(y, c), c)
    relay_to = (flip(x, c), flip(y, 1 - c), c)

    def copy(k, block, to, from_input=False):
        d = dst.at[_slot(block)]
        return pltpu.make_async_remote_copy(
            src_ref=src if from_input else d, dst_ref=d, send_sem=send_sems.at[k], recv_sem=recv_sems.at[k],
            device_id=to, device_id_type=MESH)

    def first_copies():
        return [copy(0, me, sibling, True), copy(1, me, x_nbr, True), copy(2, me, y_nbr, True)]

    def other(block):
        return block[:2] + (1 - c,)

    def start():
        pltpu.make_async_copy(src, dst.at[_slot(me)], local_sem).start()
        for cp in first_copies():
            cp.start()

    def finish():
        copy(1, x_nbr, me).wait_recv()
        copy(2, y_nbr, me).wait_recv()
        later = [copy(3, relay_from, relay_to), copy(4, x_nbr, sibling), copy(5, y_nbr, sibling)]
        for cp in later:
            cp.start()
        copy(3, diag, me).wait_recv()
        later.append(copy(6, diag, sibling))
        later[-1].start()
        copy(0, sibling, me).wait_recv()
        for k, block in ((4, x_nbr), (5, y_nbr), (6, diag)):
            copy(k, other(block), me).wait_recv()
        for cp in first_copies() + later:
            cp.wait_send()
        pltpu.make_async_copy(src, dst.at[_slot(me)], local_sem).wait()

    return start, finish


def _allgather_side(a, relay=False):
    return ((a,), (jax.ShapeDtypeStruct((N_DEV,) + a.shape, a.dtype),),
            (pltpu.SemaphoreType.DMA((7,)), pltpu.SemaphoreType.DMA((7,)), pltpu.SemaphoreType.DMA),
            _allgather_relay_steps if relay else _allgather_steps)


def _allgather(a, name, relay=False):
    ins, outs, scratch, make = _allgather_side(a, relay)

    def body(a_ref, o_ref, *scr):
        start, finish = make((a_ref,), (o_ref,), scr)
        start()
        finish()

    any_spec = pl.BlockSpec(memory_space=pl.ANY)
    return pl.pallas_call(body, name=name, out_shape=outs[0], in_specs=[any_spec], out_specs=any_spec,
                          scratch_shapes=list(scratch))(a)


def _exchange_pair(ga, gb):
    def body(ga_ref, gb_ref, ra_ref, rb_ref, send_sems, recv_sems):
        x, y, c = _place()
        sibling = (x, y, 1 - c)
        slots = [_slot(sibling)] + [_slot((*chip, 1 - c)) for chip in _other_chips(x, y)]
        cps = []
        for t, (src, dst) in enumerate(((ga_ref, ra_ref), (gb_ref, rb_ref))):
            for k, ps in enumerate(slots):
                cps.append(pltpu.make_async_remote_copy(
                    src_ref=src.at[ps], dst_ref=dst.at[k], send_sem=send_sems.at[t, k], recv_sem=recv_sems.at[t, k],
                    device_id=sibling, device_id_type=MESH))
        for cp in cps:
            cp.start()
        for cp in cps:
            cp.wait()

    any_spec = pl.BlockSpec(memory_space=pl.ANY)
    return pl.pallas_call(
        body, name="exchange_pair",
        out_shape=[jax.ShapeDtypeStruct((4,) + ga.shape[1:], ga.dtype), jax.ShapeDtypeStruct((4,) + gb.shape[1:], gb.dtype)],
        in_specs=[any_spec] * 2,
        out_specs=[any_spec] * 2,
        scratch_shapes=[pltpu.SemaphoreType.DMA((2, 4)), pltpu.SemaphoreType.DMA((2, 4))],
    )(ga, gb)


def _tiling(R, Cc, tile, by_cols):
    if by_cols:
        assert Cc % tile == 0
        return Cc // tile, (R, tile), lambda lead, i: (lead, 0, i)
    assert R % tile == 0
    return R // tile, (tile, Cc), lambda lead, i: (lead, i, 0)


def _pair_add(name, g, r1, slots, tile, by_cols=False):
    _, R, Cc = g.shape
    steps, blk, at = _tiling(R, Cc, tile, by_cols)

    def kern(s_ref, a_ref, b_ref, o_ref):
        o_ref[...] = (a_ref[...].astype(F32) + b_ref[...].astype(F32)).astype(o_ref.dtype)

    return pl.pallas_call(
        kern, name=name,
        out_shape=jax.ShapeDtypeStruct((3, R, Cc), BF16),
        grid_spec=pltpu.PrefetchScalarGridSpec(
            num_scalar_prefetch=1, grid=(3, steps),
            in_specs=[pl.BlockSpec((1,) + blk, lambda j, i, s: at(s[j], i)),
                      pl.BlockSpec((1,) + blk, lambda j, i, s: at(1 + j, i))],
            out_specs=pl.BlockSpec((1,) + blk, lambda j, i, s: at(j, i))),
        compiler_params=_cp(("arbitrary", "arbitrary")),
    )(slots, g, r1)


def _axis_neighbours():
    x, y, c = _place()
    flip = lambda a, bit: a + bit - 2 * a * bit
    return (flip(x, c), flip(y, 1 - c), c), (flip(x, 1 - c), flip(y, c), c), c


def _exchange_first_steps(in_refs, out_refs, scratch):
    pairs = list(zip(in_refs, out_refs))
    send_sems, recv_sems = scratch
    first, _, c = _axis_neighbours()

    def copies():
        return [pltpu.make_async_remote_copy(
            src_ref=src.at[j], dst_ref=dst.at[k], send_sem=send_sems.at[t, k], recv_sem=recv_sems.at[t, k],
            device_id=first, device_id_type=MESH)
            for t, (src, dst) in enumerate(pairs) for k, j in enumerate((1 - c, 2))]

    def start():
        for cp in copies():
            cp.start()

    def finish():
        for cp in copies():
            cp.wait()

    return start, finish


def _exchange_first_side(sa, sb):
    return ((sa, sb), (jax.ShapeDtypeStruct((2,) + sa.shape[1:], sa.dtype), jax.ShapeDtypeStruct((2,) + sb.shape[1:], sb.dtype)),
            (pltpu.SemaphoreType.DMA((2, 2)), pltpu.SemaphoreType.DMA((2, 2))), _exchange_first_steps)


def _axis_add(name, s, t1, core, tile, by_cols=False):
    _, R, Cc = s.shape
    steps, blk, at = _tiling(R, Cc, tile, by_cols)

    def kern(c_ref, a_ref, b_ref, o_ref):
        o_ref[...] = (a_ref[...].astype(F32) + b_ref[...].astype(F32)).astype(o_ref.dtype)

    return pl.pallas_call(
        kern, name=name,
        out_shape=jax.ShapeDtypeStruct((1, R, Cc), BF16),
        grid_spec=pltpu.PrefetchScalarGridSpec(
            num_scalar_prefetch=1, grid=(steps,),
            in_specs=[pl.BlockSpec((1,) + blk, lambda i, cr: at(cr[0], i)),
                      pl.BlockSpec((1,) + blk, lambda i, cr: at(1, i))],
            out_specs=pl.BlockSpec((1,) + blk, lambda i, cr: at(0, i))),
        compiler_params=_cp(("arbitrary",)),
    )(core, s, t1)


def _exchange_second(pa, pb):
    def body(pa_ref, pb_ref, ta_ref, tb_ref, send_sems, recv_sems):
        _, second, _ = _axis_neighbours()
        cps = [pltpu.make_async_remote_copy(src_ref=src, dst_ref=dst, send_sem=send_sems.at[t], recv_sem=recv_sems.at[t],
                                            device_id=second, device_id_type=MESH)
               for t, (src, dst) in enumerate(((pa_ref, ta_ref), (pb_ref, tb_ref)))]
        for cp in cps:
            cp.start()
        for cp in cps:
            cp.wait()

    any_spec = pl.BlockSpec(memory_space=pl.ANY)
    return pl.pallas_call(
        body, name="exchange_second",
        out_shape=[jax.ShapeDtypeStruct(pa.shape, pa.dtype), jax.ShapeDtypeStruct(pb.shape, pb.dtype)],
        in_specs=[any_spec] * 2, out_specs=[any_spec] * 2,
        scratch_shapes=[pltpu.SemaphoreType.DMA((2,)), pltpu.SemaphoreType.DMA((2,))],
    )(pa, pb)


def _adamw(name, w, m, v, parts, tile, by_cols=False):
    R, Cc = w.shape
    steps, blk_shape, at = _tiling(R, Cc, tile, by_cols)
    n_parts = len(parts)

    def kern(*refs):
        w_ref, m_ref, v_ref = refs[:3]
        g = None
        for r_ref, (_, n) in zip(refs[3:3 + n_parts], parts):
            for s in range(n):
                term = r_ref[s].astype(F32)
                g = term if g is None else g + term
        g_out, d_out, m_out, v_out = refs[3 + n_parts:]
        mn = ADAM_B1 * m_ref[...] + (1.0 - ADAM_B1) * g
        vn = ADAM_B2 * v_ref[...] + (1.0 - ADAM_B2) * (g * g)
        m_hat = mn / (1.0 - ADAM_B1 ** ADAM_STEP)
        v_hat = vn / (1.0 - ADAM_B2 ** ADAM_STEP)
        g_out[...] = g
        d_out[...] = -ADAM_LR * (m_hat / (jnp.sqrt(v_hat) + ADAM_EPS) + ADAM_WD * w_ref[...])
        m_out[...] = mn
        v_out[...] = vn

    blk = pl.BlockSpec(blk_shape, lambda i: at(0, i)[1:])
    return pl.pallas_call(
        kern, name=name,
        out_shape=[jax.ShapeDtypeStruct((R, Cc), F32)] * 4,
        grid=(steps,),
        in_specs=[blk] * 3 + [pl.BlockSpec((n,) + blk_shape, lambda i: at(0, i)) for (_, n) in parts],
        out_specs=[blk] * 4,
        compiler_params=_cp(("arbitrary",)),
    )(w, m, v, *[a for (a, _) in parts])


def _pad_cols(a, w):
    return jnp.pad(a, ((0, 0), (0, w - a.shape[1])))


def _pad_rows(a, r):
    return jnp.pad(a, ((0, r - a.shape[0]), (0, 0)))


def _pack_b(pf, pr, wo, w2, a2, rows):
    body = jnp.concatenate([pf, pr, wo.reshape(2048, 256), jnp.concatenate([w2, a2], axis=1)], axis=0)
    return _pad_rows(body, rows)


def kernel(x, norm_gain, w_in, fox_forget_bias, rwkv_shift_mix, rwkv_w0, rwkv_w2, rwkv_a0, rwkv_a2, rwkv_k_k, rwkv_k_a, rwkv_r_k, rwkv_ln_w, rwkv_ln_b, w_proj_fox, w_proj_rwkv, w_out, final_norm_gain, loss_target, m_norm_gain, m_w_in, m_fox_forget_bias, m_rwkv_shift_mix, m_rwkv_w0, m_rwkv_w2, m_rwkv_a0, m_rwkv_a2, m_rwkv_k_k, m_rwkv_k_a, m_rwkv_r_k, m_rwkv_ln_w, m_rwkv_ln_b, m_w_proj_fox, m_w_proj_rwkv, m_w_out, m_final_norm_gain, v_norm_gain, v_w_in, v_fox_forget_bias, v_rwkv_shift_mix, v_rwkv_w0, v_rwkv_w2, v_rwkv_a0, v_rwkv_a2, v_rwkv_k_k, v_rwkv_k_a, v_rwkv_r_k, v_rwkv_ln_w, v_rwkv_ln_b, v_w_proj_fox, v_w_proj_rwkv, v_w_out, v_final_norm_gain):
    T, D = x.shape[1], x.shape[2]
    assert D == 2048 and T % LANE == 0
    NI = w_in.shape[2]
    IN = N_DEV * NI
    RB = 4224
    x2 = x[0]
    lt2 = loss_target[0]
    me = _slot(_place())

    wa = _allgather(w_in[0].astype(BF16), "allgather_w_in", relay=True)
    packed_own = _pack_b(w_proj_fox[0], w_proj_rwkv[0], w_out[0], rwkv_w2[0], rwkv_a2[0], RB).astype(BF16)
    sections = [(0, 4096, 0), (4104, 4096, 4096), (8392, 4096, 8192), (4096, 8, 12288), (8200, 96, 12544), (8296, 96, 12672)]
    NP = 12800
    pieces, at_col = [], 0
    for lo, width, pad_lo in sections:
        if pad_lo > at_col:
            pieces.append(jnp.zeros((D, pad_lo - at_col), BF16))
        col = lo
        while col < lo + width:
            d = col // NI
            stop = min(lo + width, (d + 1) * NI)
            pieces.append(wa[d, :, col - d * NI:stop - d * NI])
            col = stop
        at_col = pad_lo + width
    pieces.append(jnp.zeros((D, NP - at_col), BF16))
    w_pad = jnp.concatenate(pieces, axis=1)
    F_CB, LORA_CB = 96, 49

    mu = rwkv_shift_mix
    mu_main = mu[:, 0:4096]
    mu_lora = jnp.concatenate([_pad_cols(mu[:, 4096:4192], LANE), _pad_cols(mu[:, 4192:4288], LANE)], axis=1)
    bias_pad = _pad_cols(fox_forget_bias, LANE)
    rk_flat = rwkv_r_k.reshape(1, 1024)
    gf = final_norm_gain.reshape(1, D)

    tb = min(256, T)
    tbh = min(128, T)
    (h,) = _rowcall("rms_fwd", T, tb, [_rows(x2, tb, D), _whole(norm_gain)],
                    lambda i, xv, g: ([_rms_math(xv, g)], []), [(D, BF16)])
    u, wb = _mm(h, w_pad, tm=1024, tn=1280, name="mm_in", side=_allgather_side(packed_own))
    wpf = wb[:, 0:1024, :].transpose(1, 0, 2).reshape(1024, D)
    wpr = wb[:, 1024:2048, :].transpose(1, 0, 2).reshape(1024, D)
    wo = wb[:, 2048:4096, :].reshape(N_DEV * 256, D)
    w2p = _pad_rows(wb[:, 4096:4192, 0:128].transpose(1, 0, 2).reshape(96, 1024).astype(F32), LANE)
    a2p = _pad_rows(wb[:, 4096:4192, 128:256].transpose(1, 0, 2).reshape(96, 1024).astype(F32), LANE)

    c8 = _gates_fwd(u, bias_pad, T, F_CB)
    c3 = c8.reshape(8, 1, T)
    o_raw, o_a, lse = _attn_fwd(u, c3, T)

    def shift_body(i, um, hm, ul, hl, mm_, ml):
        outs = []
        for uv, hv, mv in ((um, hm, mm_), (ul, hl, ml)):
            hv = jnp.where(i == 0, 0.0, hv)
            prev = pltpu.roll(jnp.concatenate([hv, uv], axis=0), 1, 0)[SUB:]
            outs.append(uv + (prev - uv) * mv)
        return outs, []

    def halo_prev(arr, w, cb):
        return (arr, (SUB, w), lambda i: (jnp.maximum(i * (tbh // SUB) - 1, 0), cb))

    xs, xl = _rowcall("rwkv_shift_fwd", T, tbh,
                      [_rows(u, tbh, 4096, 1), halo_prev(u, 4096, 1), _rows(u, tbh, 256, LORA_CB), halo_prev(u, 256, LORA_CB),
                       _whole(mu_main), _whole(mu_lora)],
                      shift_body, [(4096, F32), (256, F32)])

    prep_par = [_whole(rwkv_w0), _whole(rwkv_a0), _whole(rwkv_k_k), _whole(rwkv_k_a), _whole(w2p), _whole(a2p)]
    prep_rows = [_rows(xs, tbh, 1024, 1), _rows(xl, tbh, LANE, 0), _rows(xl, tbh, LANE, 1)]
    lw, kp, an, bb = _rowcall("rwkv_prep_fwd", T, tbh, prep_rows + prep_par,
                              lambda i, *a: (list(_prep_math(*a)), []), [(1024, F32)] * 4)
    y, states = _scan_fwd(xs, lw, kp, an, bb, T)
    post_rows = [_rows(y, tbh, 1024), _rows(xs, tbh, 1024, 0), _rows(kp, tbh, 1024), _rows(xs, tbh, 1024, 2), _rows(xs, tbh, 1024, 3)]
    post_par = [_whole(rwkv_ln_w), _whole(rwkv_ln_b), _whole(rk_flat)]
    (o_b,) = _rowcall("rwkv_post_fwd", T, tbh, post_rows + post_par,
                      lambda i, *a: ([_post_math(*a)], []), [(1024, BF16)])

    pa = _mm(o_a, wpf, name="mm_proj_fox")
    pb = _mm(o_b, wpr, name="mm_proj_rwkv")
    merge_rows = [_rows(u, tb, D, 4), _rows(u, tb, D, 5), _rows(pa, tb, D), _rows(pb, tb, D)]
    (mg,) = _rowcall("merge_fwd", T, tb, merge_rows, lambda i, *a: ([_merge_math(*a)], []), [(D, BF16)])
    mo = _mm(mg, wo, name="mm_out")

    def head_body(i, xv, mov, ltv, g):
        out = xv + mov
        r = lax.rsqrt(jnp.mean(out * out, axis=-1, keepdims=True) + RMS_EPS)
        yn = out * r
        err = yn * g - ltv
        loss = 0.5 * jnp.sum(jnp.sum(err * err, axis=-1, keepdims=True), axis=0, keepdims=True) / D
        dyv = err / D
        dyn = dyv * g
        dout = r * (dyn - yn * jnp.mean(dyn * yn, axis=-1, keepdims=True))
        return [dout], [loss, jnp.sum(dyv * yn, axis=0, keepdims=True)]

    dout, loss_p, dgf_p = _rowcall("loss_head", T, tb, [_rows(x2, tb, D), _rows(mo, tb, D), _rows(lt2, tb, D), _whole(gf)],
                                   head_body, [(D, F32)], [(1, 1), (1, D)])

    dm = _mm(dout, wo, tb=True, name="mm_out_dx")
    dwo = _mm(mg, dout, ta=True, out_dtype=BF16, name="mm_out_dw")

    def merge_bwd_body(i, ga, gb, pav, pbv, dmv):
        _, vjp = jax.vjp(_merge_math, ga, gb, pav, pbv)
        dga, dgb, dpa, dpb = vjp(dmv)
        return [dga, dgb, dpa, dpb], []

    dga, dgb, dpa, dpb = _rowcall("merge_bwd", T, tb, merge_rows + [_rows(dm, tb, D)], merge_bwd_body,
                                  [(D, BF16), (D, BF16), (D, BF16), (D, BF16)])
    doa = _mm(dpa, wpf, tb=True, name="mm_proj_fox_dx")
    dwpf = _mm(o_a, dpa, ta=True, out_dtype=BF16, name="mm_proj_fox_dw")
    dob = _mm(dpb, wpr, tb=True, name="mm_proj_rwkv_dx")
    dwpr = _mm(o_b, dpb, ta=True, out_dtype=BF16, name="mm_proj_rwkv_dw")

    do_b, dza, delta = _attn_delta(u, c3, o_raw, lse, doa, T)
    dq, dk, dv, dc3 = _attn_bwd(u, c3, lse, do_b, delta, T)
    dfl, dbias_p = _gates_bwd(dc3.reshape(8, T), u, bias_pad, T, F_CB)

    def post_bwd_body(i, yv, rv, kpv, vv, zv, lnw, lnb, rkv, dobv):
        _, vjp = jax.vjp(_post_math, yv, rv, kpv, vv, zv, lnw, lnb, rkv)
        dy_, dr_, dkp_, dv_, dz_, dlnw, dlnb, drk = vjp(dobv)
        return [dy_, dr_, dkp_, dv_, dz_], [dlnw, dlnb, drk]

    dy_s, dr_p, dkp_p, dv_p, dzb, dlnw_p, dlnb_p, drk_p = _rowcall(
        "rwkv_post_bwd", T, tbh, post_rows + post_par + [_rows(dob, tbh, 1024)], post_bwd_body,
        [(1024, F32)] * 5, [(1, 1024)] * 3)
    dr_s, dlw, dkp_s, dv_s, dan, dbb = _scan_bwd(xs, lw, kp, an, bb, states, dy_s, T)

    def prep_bwd_body(i, xk, xwd, xad, w0, a0, kkw, kaw, w2v, a2v, dlw_, dkp1, dkp2, dan_, dbb_, dr1, dr2, dv1, dv2, dz_):
        _, vjp = jax.vjp(_prep_math, xk, xwd, xad, w0, a0, kkw, kaw, w2v, a2v)
        dxk, dxwd, dxad, dw0, da0, dkk, dka, dw2, da2 = vjp((dlw_, dkp1 + dkp2, dan_, dbb_))
        return [[dr1 + dr2, dxk, dv1 + dv2, dz_], [dxwd, dxad]], [dw0, da0, dkk, dka, dw2, da2]

    cots = [dlw, dkp_s, dkp_p, dan, dbb, dr_s, dr_p, dv_s, dv_p, dzb]
    dxs, dxl, dw0_p, da0_p, dkk_p, dka_p, dw2_p, da2_p = _rowcall(
        "rwkv_prep_bwd", T, tbh, prep_rows + prep_par + [_rows(c_, tbh, 1024) for c_ in cots], prep_bwd_body,
        [(4096, F32), (256, F32)], [(1, 1024)] * 4 + [(LANE, 1024)] * 2)

    def shift_bwd_body(i, dm_, hm, dl_, hl, um, pm, ul, pl_, mm_, ml):
        last = i == T // tbh - 1
        outs, accs = [], []
        for dv_, hv, uv, pv, mv in ((dm_, hm, um, pm, mm_), (dl_, hl, ul, pl_, ml)):
            hv = jnp.where(last, 0.0, hv)
            nxt = pltpu.roll(jnp.concatenate([dv_, hv], axis=0), tbh + SUB - 1, 0)[:tbh]
            pv = jnp.where(i == 0, 0.0, pv)
            prev = pltpu.roll(jnp.concatenate([pv, uv], axis=0), 1, 0)[SUB:]
            outs.append(dv_ * (1.0 - mv) + nxt * mv)
            accs.append(jnp.sum(dv_ * (prev - uv), axis=0, keepdims=True))
        return outs, accs

    def halo_next(arr, w, cb):
        last_blk = T // SUB - 1
        return (arr, (SUB, w), lambda i: (jnp.minimum((i + 1) * (tbh // SUB), last_blk), cb))

    du_b, du_l, dmu_main_p, dmu_lora_p = _rowcall(
        "rwkv_shift_bwd", T, tbh,
        [_rows(dxs, tbh, 4096), halo_next(dxs, 4096, 0), _rows(dxl, tbh, 256), halo_next(dxl, 256, 0),
         _rows(u, tbh, 4096, 1), halo_prev(u, 4096, 1), _rows(u, tbh, 256, LORA_CB), halo_prev(u, 256, LORA_CB),
         _whole(mu_main), _whole(mu_lora)],
        shift_bwd_body, [(4096, BF16), (256, BF16)], [(1, 4096), (1, 256)])

    du = jnp.concatenate([dq.astype(BF16), dk.astype(BF16), dv.astype(BF16), dza.astype(BF16), du_b, dga, dgb,
                          dfl.astype(BF16), jnp.zeros((T, LANE), BF16), du_l], axis=1)
    dw_pad_t = _mm(du, h, ta=True, out_dtype=BF16, tm=1280, tn=1024, name="mm_in_dw")

    by_col = sorted(sections)
    blocks = []
    for d in range(N_DEV):
        parts = []
        for lo, width, pad_lo in by_col:
            a, b = max(lo, d * NI), min(lo + width, (d + 1) * NI)
            if a < b:
                parts.append(dw_pad_t[pad_lo + a - lo:pad_lo + b - lo, :])
        blocks.append(jnp.concatenate(parts, axis=0)[None])
    ga = jnp.concatenate(blocks, axis=0)
    lora_g = jnp.concatenate([dw2_p[:96].reshape(96, N_DEV, 128).transpose(1, 0, 2),
                              da2_p[:96].reshape(96, N_DEV, 128).transpose(1, 0, 2)], axis=2).astype(BF16)
    gb = jnp.concatenate([dwpf.reshape(1024, N_DEV, 256).transpose(1, 0, 2),
                          dwpr.reshape(1024, N_DEV, 256).transpose(1, 0, 2),
                          dwo.reshape(N_DEV, 2048, 256), lora_g, jnp.zeros((N_DEV, RB - 4192, 256), BF16)], axis=1)

    xx, yy, cc = _place()
    chip_slots = jnp.stack([_slot((*chip, cc)) for chip in _other_chips(xx, yy)]).astype(jnp.int32)
    r1a, r1b = _exchange_pair(ga, gb)
    sa = _pair_add("pair_add_w_in", ga, r1a, chip_slots, 256, by_cols=True)
    sb = _pair_add("pair_add_packed", gb, r1b, chip_slots, 384)
    ga_own = lax.dynamic_index_in_dim(ga, me, 0, keepdims=True)
    gb_own = lax.dynamic_index_in_dim(gb, me, 0, keepdims=True)
    dh, t1a, t1b = _mm(du, w_pad, tb=True, tm=1024, tn=1024, tk=NP // 10, name="mm_in_dx", side=_exchange_first_side(sa, sb))
    core = jnp.stack([cc]).astype(jnp.int32)
    t2a, t2b = _exchange_second(_axis_add("axis_add_w_in", sa, t1a, core, 256, by_cols=True),
                                _axis_add("axis_add_packed", sb, t1b, core, 384))

    def rms_bwd_body(i, xv, g, dhv, doutv):
        _, vjp = jax.vjp(_rms_math, xv, g)
        dx_, dg_ = vjp(dhv)
        return [dx_ + doutv], [dg_]

    grad_x2, dng_p = _rowcall("rms_bwd", T, tb, [_rows(x2, tb, D), _whole(norm_gain), _rows(dh, tb, D), _rows(dout, tb, D)],
                              rms_bwd_body, [(D, F32)], [(1, D)])

    dmu = jnp.concatenate([dmu_main_p, dmu_lora_p[:, 0:96], dmu_lora_p[:, 128:224]], axis=1)
    small_parts = [dng_p, dbias_p[:, 0:8], dmu, dw0_p, da0_p, dkk_p, dka_p, drk_p, dlnw_p, dlnb_p, dgf_p, loss_p]
    SR = 128
    small = _pad_cols(jnp.concatenate(small_parts, axis=1), SR * LANE).reshape(SR, LANE)
    rs = _allgather(small, "allgather_small")

    w_in_outs = _adamw("adamw_w_in", w_in[0].T, m_w_in[0].T, v_w_in[0].T, [(ga_own, 1), (r1a, 1), (t1a, 1), (t2a, 1)], 256, by_cols=True)
    g_in, d_in, m_in, v_in = [o.T for o in w_in_outs]
    pk = lambda pf, pr, wo_, w2_, a2_: _pack_b(pf[0], pr[0], wo_[0], w2_[0], a2_[0], RB)
    outs_b = _adamw("adamw_packed", pk(w_proj_fox, w_proj_rwkv, w_out, rwkv_w2, rwkv_a2),
                    pk(m_w_proj_fox, m_w_proj_rwkv, m_w_out, m_rwkv_w2, m_rwkv_a2),
                    pk(v_w_proj_fox, v_w_proj_rwkv, v_w_out, v_rwkv_w2, v_rwkv_a2), [(gb_own, 1), (r1b, 1), (t1b, 1), (t2b, 1)], 384)

    def pack_small(ng, fb, sm, w0, a0, kk_, ka_, rk_, lnw, lnb, fg):
        parts = [ng, fb, sm, w0, a0, kk_, ka_, rk_.reshape(1, 1024), lnw, lnb, fg.reshape(1, D), jnp.zeros((1, 1), F32)]
        return _pad_cols(jnp.concatenate(parts, axis=1), SR * LANE).reshape(SR, LANE)

    outs_s = _adamw("adamw_small",
                    pack_small(norm_gain, fox_forget_bias, rwkv_shift_mix, rwkv_w0, rwkv_a0, rwkv_k_k, rwkv_k_a, rwkv_r_k,
                               rwkv_ln_w, rwkv_ln_b, final_norm_gain),
                    pack_small(m_norm_gain, m_fox_forget_bias, m_rwkv_shift_mix, m_rwkv_w0, m_rwkv_a0, m_rwkv_k_k, m_rwkv_k_a,
                               m_rwkv_r_k, m_rwkv_ln_w, m_rwkv_ln_b, m_final_norm_gain),
                    pack_small(v_norm_gain, v_fox_forget_bias, v_rwkv_shift_mix, v_rwkv_w0, v_rwkv_a0, v_rwkv_k_k, v_rwkv_k_a,
                               v_rwkv_r_k, v_rwkv_ln_w, v_rwkv_ln_b, v_final_norm_gain),
                    [(rs, N_DEV)], SR)

    def unpack_b(pkd):
        return dict(w_proj_fox=pkd[0:1024][None], w_proj_rwkv=pkd[1024:2048][None], w_out=pkd[2048:4096].reshape(1, 256, D),
                    rwkv_w2=pkd[4096:4192, 0:128][None], rwkv_a2=pkd[4096:4192, 128:256][None])

    def unpack_s(pkd):
        flat = pkd.reshape(1, SR * LANE)
        names = [("norm_gain", D), ("fox_forget_bias", 8), ("rwkv_shift_mix", 4288), ("rwkv_w0", 1024), ("rwkv_a0", 1024),
                 ("rwkv_k_k", 1024), ("rwkv_k_a", 1024), ("rwkv_r_k", 1024), ("rwkv_ln_w", 1024), ("rwkv_ln_b", 1024),
                 ("final_norm_gain", D), ("loss", 1)]
        out, off = {}, 0
        for nm, n in names:
            out[nm] = flat[:, off:off + n]
            off += n
        out["rwkv_r_k"] = out["rwkv_r_k"].reshape(1, 16, 64)
        out["final_norm_gain"] = out["final_norm_gain"].reshape(D)
        return out

    order = ["norm_gain", "w_in", "fox_forget_bias", "rwkv_shift_mix", "rwkv_w0", "rwkv_w2", "rwkv_a0", "rwkv_a2", "rwkv_k_k",
             "rwkv_k_a", "rwkv_r_k", "rwkv_ln_w", "rwkv_ln_b", "w_proj_fox", "w_proj_rwkv", "w_out", "final_norm_gain"]
    result = []
    loss = None
    for kind, big in enumerate((g_in, d_in, m_in, v_in)):
        d = {**unpack_b(outs_b[kind]), **unpack_s(outs_s[kind]), "w_in": big[None]}
        if kind == 0:
            loss = d["loss"].reshape(())
        result += [d[n] for n in order]
    return (loss, grad_x2[None], *result)
```

```python
import functools

import jax
import jax.numpy as jnp
from jax import lax
from jax.experimental import pallas as pl
from jax.experimental.pallas import tpu as pltpu

F32 = jnp.float32
BF16 = jnp.bfloat16
HI = lax.Precision.HIGHEST
H3 = lax.Precision.HIGH
MESH = pl.DeviceIdType.MESH

FOX_HD = 128
RW_HD = 64
RMS_EPS = 1e-6
GN_EPS = 64e-5
L2_EPS = 1e-12
ADAM_LR = 0.001
ADAM_B1 = 0.9
ADAM_B2 = 0.999
ADAM_EPS = 1e-08
ADAM_WD = 0.01
ADAM_STEP = 10

LANE = 128
SUB = 8
VMEM_LIMIT = 56 * 1024 * 1024
N_DEV = 8
CHUNK = 64
SCAN_GROUP = 4
SCAN_PAIRS = 2
PS = None
NEG = -1e30


def _scan_shape(T):
    c = min(CHUNK, T)
    return c, min(SCAN_GROUP, T // c)


def _cp(sem=None):
    return pltpu.CompilerParams(dimension_semantics=sem, vmem_limit_bytes=VMEM_LIMIT)


def _sigmoid(x):
    return jax.nn.sigmoid(x)


def _softplus(x):
    return jnp.maximum(x, 0.0) + jnp.log(1.0 + jnp.exp(-jnp.abs(x)))


def _nn(a, b, prec=None):
    return lax.dot_general(a, b, (((1,), (0,)), ((), ())), precision=prec, preferred_element_type=F32)


def _nt(a, b, prec=None):
    return lax.dot_general(a, b, (((1,), (1,)), ((), ())), precision=prec, preferred_element_type=F32)


def _tn(a, b, prec=None):
    return lax.dot_general(a, b, (((0,), (0,)), ((), ())), precision=prec, preferred_element_type=F32)


def _iota2(shape, dim):
    return lax.broadcasted_iota(jnp.int32, shape, dim)


def _seg_sum(x):
    r = _iota2((LANE, LANE), 0) // RW_HD
    c = _iota2((LANE, LANE), 1) // RW_HD
    bd = (r == c).astype(F32)
    parts = [_nn(x[:, j * LANE:(j + 1) * LANE], bd, H3) for j in range(x.shape[1] // LANE)]
    return parts[0] if len(parts) == 1 else jnp.concatenate(parts, axis=1)


def _mm(a, b, *, ta=False, tb=False, out_dtype=F32, tm=1024, tn=1024, tk=None, name, side=None):
    assert not (ta and tb)
    K, M = a.shape if ta else a.shape[::-1]
    N = b.shape[0] if tb else b.shape[1]
    tm, tn = min(tm, M), min(tn, N)
    tk = K if tk is None else tk
    nk = K // tk
    assert M % tm == 0 and N % tn == 0 and K % tk == 0
    gi, gj = M // tm, N // tn
    a_spec = pl.BlockSpec((tk, tm), lambda i, j, k: (k, i)) if ta else pl.BlockSpec((tm, tk), lambda i, j, k: (i, k))
    b_spec = pl.BlockSpec((tn, tk), lambda i, j, k: (j, k)) if tb else pl.BlockSpec((tk, tn), lambda i, j, k: (k, j))
    side_ins, side_outs, side_scr, side_make = side if side is not None else ((), (), (), None)
    n_si, n_so = len(side_ins), len(side_outs)
    n_acc = 0 if nk == 1 else 1

    def body(*refs):
        a_ref, b_ref = refs[:2]
        o_ref = refs[2 + n_si]
        scr = refs[3 + n_si + n_so:]
        k = pl.program_id(2)
        if side_make is not None:
            start, finish = side_make(refs[2:2 + n_si], refs[3 + n_si:3 + n_si + n_so], scr[n_acc:])
            first = jnp.logical_and(jnp.logical_and(pl.program_id(0) == 0, pl.program_id(1) == 0), k == 0)
            last = jnp.logical_and(jnp.logical_and(pl.program_id(0) == gi - 1, pl.program_id(1) == gj - 1), k == nk - 1)
            pl.when(first)(start)
        av = a_ref[...].astype(BF16)
        bv = b_ref[...].astype(BF16)
        p = _tn(av, bv) if ta else _nt(av, bv) if tb else _nn(av, bv)
        if nk == 1:
            o_ref[...] = p.astype(out_dtype)
        else:
            acc_ref = scr[0]

            @pl.when(k == 0)
            def _():
                acc_ref[...] = p

            @pl.when(k > 0)
            def _():
                acc_ref[...] += p

            @pl.when(k == nk - 1)
            def _():
                o_ref[...] = acc_ref[...].astype(out_dtype)
        if side_make is not None:
            pl.when(last)(finish)

    any_spec = pl.BlockSpec(memory_space=pl.ANY)
    res = pl.pallas_call(
        body, name=name,
        out_shape=[jax.ShapeDtypeStruct((M, N), out_dtype)] + list(side_outs),
        grid=(gi, gj, nk),
        in_specs=[a_spec, b_spec] + [any_spec] * n_si,
        out_specs=[pl.BlockSpec((tm, tn), lambda i, j, k: (i, j))] + [any_spec] * n_so,
        scratch_shapes=([] if nk == 1 else [pltpu.VMEM((tm, tn), F32)]) + list(side_scr),
        compiler_params=_cp(("arbitrary",) * 3 if side is not None else ("parallel", "parallel", "arbitrary")),
    )(a, b, *side_ins)
    return res if side is not None else res[0]


def _rows(arr, tb, w, cb=0):
    return (arr, (tb, w), lambda i: (i, cb))


def _whole(arr):
    nd = arr.ndim
    return (arr, arr.shape, lambda i: (0,) * nd)


def _rowcall(name, T, tb, ins, body, outs, accs=()):
    n_in, n_out = len(ins), len(outs)

    def kern(*refs):
        i = pl.program_id(0)
        vals = [r[...] for r in refs[:n_in]]
        ro, ao = body(i, *vals)
        for r, v in zip(refs[n_in:n_in + n_out], ro):
            if isinstance(v, (list, tuple)):
                off = 0
                for piece in v:
                    w = piece.shape[1]
                    r[:, off:off + w] = piece.astype(r.dtype)
                    off += w
            else:
                r[...] = v.astype(r.dtype)
        if accs:
            acc_refs = refs[n_in + n_out:]

            @pl.when(i == 0)
            def _():
                for r in acc_refs:
                    r[...] = jnp.zeros(r.shape, F32)

            for r, v in zip(acc_refs, ao):
                r[...] += v

    out_shape = [jax.ShapeDtypeStruct((T, w), dt) for (w, dt) in outs] + [jax.ShapeDtypeStruct(s, F32) for s in accs]
    out_specs = [pl.BlockSpec((tb, w), lambda i: (i, 0)) for (w, dt) in outs] + [pl.BlockSpec(s, lambda i: (0, 0)) for s in accs]
    res = pl.pallas_call(
        kern, name=name,
        out_shape=out_shape,
        grid=(T // tb,),
        in_specs=[pl.BlockSpec(bs, im) for (_, bs, im) in ins],
        out_specs=out_specs,
        compiler_params=_cp(("arbitrary",)),
    )(*[a for (a, _, _) in ins])
    return res


def _rms_math(x, g):
    r = lax.rsqrt(jnp.mean(x * x, axis=-1, keepdims=True) + RMS_EPS)
    return x * r * g


def _merge_math(ga, gb, pa, pb):
    return _sigmoid(ga) * pa + _sigmoid(gb) * pb


def _prep_math(xk, xwd, xad, w0, a0, kk_w, ka_w, w2p, a2p):
    z = w0 + _nn(jnp.tanh(xwd), w2p, H3)
    w = -_softplus(-z) - 0.5
    lw = -jnp.exp(w)
    ag = _sigmoid(a0 + _nn(xad, a2p, H3))
    p = xk * kk_w
    n = jnp.maximum(jnp.sqrt(_seg_sum(p * p)), L2_EPS)
    kk = p / n
    kp = xk * (1.0 + (ag - 1.0) * ka_w)
    return lw, kp, -kk, kk * ag


def _post_math(y, r, kp, v, z, lnw, lnb, rk):
    inv = 1.0 / RW_HD
    mu = _seg_sum(y) * inv
    d = y - mu
    var = _seg_sum(d * d) * inv
    yn = d * lax.rsqrt(var + GN_EPS) * lnw + lnb
    bonus = _seg_sum(r * kp * rk) * v
    return (yn + bonus) * (z * _sigmoid(z))


@functools.partial(jax.custom_vjp, nondiff_argnums=(1,))
def _unit_inverses(ms, depth):
    eye = (_iota2(ms[0].shape, 0) == _iota2(ms[0].shape, 1)).astype(F32)
    half = ms[0].shape[1]
    mp = [_nn(m, m, PS) for m in ms]
    inv = [eye + m for m in ms]
    n = 2
    while n < depth:
        last = 2 * n >= depth
        for i in range(len(ms)):
            if last:
                inv[i] = inv[i] + _nn(mp[i], inv[i], PS)
            else:
                z = _nn(mp[i], jnp.concatenate([mp[i], inv[i]], axis=1), PS)
                mp[i], inv[i] = z[:, :half], inv[i] + z[:, half:]
        n *= 2
    return tuple(inv)


def _unit_inverses_fwd(ms, depth):
    inv = _unit_inverses(ms, depth)
    return inv, inv


def _unit_inverses_bwd(depth, inv, cts):
    left = [_tn(t, g, PS) for t, g in zip(inv, cts)]
    return (tuple(_nt(l, t, PS) for l, t in zip(left, inv)),)


_unit_inverses.defvjp(_unit_inverses_fwd, _unit_inverses_bwd)


@jax.custom_vjp
def _known_inverses(ms, inv):
    return inv


def _known_inverses_fwd(ms, inv):
    return inv, inv


def _known_inverses_bwd(inv, cts):
    dms = _unit_inverses_bwd(None, inv, cts)[0]
    return dms, tuple(jnp.zeros_like(t) for t in inv)


_known_inverses.defvjp(_known_inverses_fwd, _known_inverses_bwd)


def _scan_group(s0s, *flat, known_inv=None, with_inv=False):
    P = len(s0s)
    G = len(flat) // (6 * P)
    ch = [flat[6 * i:6 * i + 6] for i in range(P * G)]
    C = ch[0][0].shape[0]
    C2 = 2 * C
    cat = jnp.concatenate
    m0 = _iota2((1, LANE), 1) < RW_HD
    mask0 = m0.astype(F32)
    mask1 = 1.0 - mask0
    r2 = _iota2((C2, C2), 0)
    c2 = _iota2((C2, C2), 1)
    dist = r2 - c2
    in_head = dist <= r2 % C
    lower = (_iota2((C, C), 0) >= _iota2((C, C), 1)).astype(F32)
    bd = (_iota2((LANE, LANE), 0) // RW_HD) == (_iota2((LANE, LANE), 1) // RW_HD)

    def tri(m, strict):
        return jnp.where(dist > 0 if strict else dist >= 0, jnp.where(in_head, m, 0.0), 0.0)

    def sel(z):
        return jnp.where(m0, z[:C], z[C:])

    gs = [_nn(lower, c[1], HI) for c in ch]
    pre = []
    for (r, lw, k, v, a, b), g in zip(ch, gs):
        g_end = jnp.sum(lw, axis=0, keepdims=True)
        en = jnp.exp(-g)
        ec = jnp.exp(g_end - g)
        pre.append(dict(at=a * jnp.exp(g - lw), rt=r * jnp.exp(g), bt=b * en, kt=k * en, bh=b * ec, kh=k * ec,
                        dec=jnp.exp(g_end), v=v))
    grams = [_nt(cat([p["at"] * mask0, p["at"] * mask1, p["rt"] * mask0, p["rt"] * mask1], axis=0),
                 cat([p["bt"], p["bt"], p["kt"], p["kt"]], axis=0), PS) for p in pre]
    mab = tuple(tri(gm[:C2, :C2], True) for gm in grams)
    tinv = _unit_inverses(mab, C) if known_inv is None else _known_inverses(mab, known_inv)
    xv =[sel(_nn(tri(gm[:C2, C2:], True), cat([p["v"], p["v"]], axis=0), PS)) for gm, p in zip(grams, pre)]
    ys, s = [None] * (P * G), list(s0s)
    for i in range(G):
        for q in range(P):
            n = q * G + i
            p, gm = pre[n], grams[n]
            sx = _nt(cat([p["at"], p["rt"]], axis=0), s[q], PS)
            x = sx[:C] + xv[n]
            u = sel(_nn(tinv[n], cat([x, x], axis=0), PS))
            v = p["v"]
            ys[n] = sx[C:] + sel(_nn(cat([tri(gm[C2:, :C2], False), tri(gm[C2:, C2:], False)], axis=1),
                                     cat([u, u, v, v], axis=0), PS))
            s[q] = s[q] * p["dec"] + jnp.where(bd, _tn(cat([u, v], axis=0), cat([p["bh"], p["kh"]], axis=0), PS), 0.0)
    return (tuple(ys), tuple(s), tinv) if with_inv else (tuple(ys), tuple(s))


def _scan_fwd(xs, lw, kp, an, bb, T):
    C, G = _scan_shape(T)
    P = SCAN_PAIRS
    nc = T // (C * G)
    npair = 1024 // LANE

    def kern(r_ref, lw_ref, k_ref, v_ref, a_ref, b_ref, y_ref, st_ref, inv_ref, s_scr):
        n = pl.program_id(1)

        @pl.when(n == 0)
        def _():
            s_scr[...] = jnp.zeros(s_scr.shape, F32)

        st_ref[0] = s_scr[...]
        ins = (r_ref, lw_ref, k_ref, v_ref, a_ref, b_ref)
        ys, s1, inv = _scan_group(
            tuple(s_scr[q] for q in range(P)),
            *[ref[i * C:(i + 1) * C, q * LANE:(q + 1) * LANE] for q in range(P) for i in range(G) for ref in ins], with_inv=True)
        for q in range(P):
            for i in range(G):
                y_ref[i * C:(i + 1) * C, q * LANE:(q + 1) * LANE] = ys[q * G + i]
                inv_ref[0, 0, q * G + i] = inv[q * G + i]
            s_scr[q] = s1[q]

    def col(off):
        return pl.BlockSpec((C * G, P * LANE), lambda p, n: (n, off // P + p))

    return pl.pallas_call(
        kern, name="rwkv_scan_fwd",
        out_shape=[jax.ShapeDtypeStruct((T, 1024), F32), jax.ShapeDtypeStruct((nc, npair, LANE, LANE), F32),
                   jax.ShapeDtypeStruct((nc, npair // P, P * G, 2 * C, 2 * C), F32)],
        grid=(npair // P, nc),
        in_specs=[col(0), col(0), col(0), col(16), col(0), col(0)],
        out_specs=[col(0), pl.BlockSpec((1, P, LANE, LANE), lambda p, n: (n, p, 0, 0)),
                   pl.BlockSpec((1, 1, P * G, 2 * C, 2 * C), lambda p, n: (n, p, 0, 0, 0))],
        scratch_shapes=[pltpu.VMEM((P, LANE, LANE), F32)],
        compiler_params=_cp(("parallel", "arbitrary")),
    )(xs, lw, kp, xs, an, bb)


def _scan_bwd(xs, lw, kp, an, bb, states, invs, dy, T):
    C, G = _scan_shape(T)
    P = SCAN_PAIRS
    nc = T // (C * G)
    npair = 1024 // LANE

    def kern(r_ref, lw_ref, k_ref, v_ref, a_ref, b_ref, st_ref, inv_ref, dy_ref, dr_ref, dlw_ref, dk_ref, dv_ref, da_ref, db_ref,
             ds_scr):
        n = pl.program_id(1)

        @pl.when(n == 0)
        def _():
            ds_scr[...] = jnp.zeros(ds_scr.shape, F32)

        ins = (r_ref, lw_ref, k_ref, v_ref, a_ref, b_ref)
        units = [(q, i) for q in range(P) for i in range(G)]
        known = tuple(inv_ref[0, 0, q * G + i] for q, i in units)
        _, vjp = jax.vjp(functools.partial(_scan_group, known_inv=known), tuple(st_ref[0, q] for q in range(P)),
                         *[ref[i * C:(i + 1) * C, q * LANE:(q + 1) * LANE] for q, i in units for ref in ins])
        grads = vjp((tuple(dy_ref[i * C:(i + 1) * C, q * LANE:(q + 1) * LANE] for q, i in units),
                     tuple(ds_scr[q] for q in range(P))))
        for q in range(P):
            ds_scr[q] = grads[0][q]
        outs = (dr_ref, dlw_ref, dk_ref, dv_ref, da_ref, db_ref)
        for n_, (q, i) in enumerate(units):
            for t, ref in enumerate(outs):
                ref[i * C:(i + 1) * C, q * LANE:(q + 1) * LANE] = grads[1 + 6 * n_ + t]

    def col(off):
        return pl.BlockSpec((C * G, P * LANE), lambda p, n: (nc - 1 - n, off // P + p))

    return pl.pallas_call(
        kern, name="rwkv_scan_bwd",
        out_shape=[jax.ShapeDtypeStruct((T, 1024), F32)] * 6,
        grid=(npair // P, nc),
        in_specs=[col(0), col(0), col(0), col(16), col(0), col(0),
                  pl.BlockSpec((1, P, LANE, LANE), lambda p, n: (nc - 1 - n, p, 0, 0)),
                  pl.BlockSpec((1, 1, P * G, 2 * C, 2 * C), lambda p, n: (nc - 1 - n, p, 0, 0, 0)), col(0)],
        out_specs=[col(0)] * 6,
        scratch_shapes=[pltpu.VMEM((P, LANE, LANE), F32)],
        compiler_params=_cp(("parallel", "arbitrary")),
    )(xs, lw, kp, xs, an, bb, states, invs, dy)


def _gates_fwd(u, bias_pad, T, f_cb):
    nb = T // LANE

    def kern(f_ref, b_ref, c_ref):
        x = f_ref[...] + b_ref[...]
        lf = jnp.minimum(x, 0.0) - jnp.log(1.0 + jnp.exp(-jnp.abs(x)))
        lft = lf.T
        ut = (_iota2((LANE, LANE), 0) <= _iota2((LANE, LANE), 1)).astype(F32)
        carry = jnp.zeros((LANE, 1), F32)
        for blk in range(nb):
            seg = lft[:, blk * LANE:(blk + 1) * LANE]
            cs = _nn(seg, ut, HI) + carry
            c_ref[:, blk * LANE:(blk + 1) * LANE] = cs[:SUB, :]
            carry = carry + jnp.sum(seg, axis=1, keepdims=True)

    return pl.pallas_call(
        kern, name="fox_gates_fwd",
        out_shape=jax.ShapeDtypeStruct((SUB, T), F32),
        grid=(1,),
        in_specs=[pl.BlockSpec((T, LANE), lambda i: (0, f_cb)), pl.BlockSpec((1, LANE), lambda i: (0, 0))],
        out_specs=pl.BlockSpec((SUB, T), lambda i: (0, 0)),
        compiler_params=_cp(("arbitrary",)),
    )(u, bias_pad)


def _gates_bwd(dc, u, bias_pad, T, f_cb):
    nb = T // LANE

    def kern(dc_ref, f_ref, b_ref, dfl_ref, db_ref):
        dcv = jnp.concatenate([dc_ref[...], jnp.zeros((LANE - SUB, T), F32)], axis=0)
        lt = (_iota2((LANE, LANE), 0) >= _iota2((LANE, LANE), 1)).astype(F32)
        carry = jnp.zeros((LANE, 1), F32)
        pieces = [None] * nb
        for blk in range(nb - 1, -1, -1):
            seg = dcv[:, blk * LANE:(blk + 1) * LANE]
            pieces[blk] = _nn(seg, lt, HI) + carry
            carry = carry + jnp.sum(seg, axis=1, keepdims=True)
        dlf = (pieces[0] if nb == 1 else jnp.concatenate(pieces, axis=1)).T
        x = f_ref[...] + b_ref[...]
        dfl = dlf * _sigmoid(-x)
        dfl_ref[...] = dfl
        db_ref[...] = jnp.sum(dfl, axis=0, keepdims=True)

    return pl.pallas_call(
        kern, name="fox_gates_bwd",
        out_shape=[jax.ShapeDtypeStruct((T, LANE), F32), jax.ShapeDtypeStruct((1, LANE), F32)],
        grid=(1,),
        in_specs=[pl.BlockSpec((SUB, T), lambda i: (0, 0)), pl.BlockSpec((T, LANE), lambda i: (0, f_cb)),
                  pl.BlockSpec((1, LANE), lambda i: (0, 0))],
        out_specs=[pl.BlockSpec((T, LANE), lambda i: (0, 0)), pl.BlockSpec((1, LANE), lambda i: (0, 0))],
        compiler_params=_cp(("arbitrary",)),
    )(dc, u, bias_pad)


ATTN_HEADS = 4


def _attn_block(T):
    return 256 if T % 256 == 0 and T >= 512 else 128


def _attn_fwd(u, c3, T):
    H, HP = 8, ATTN_HEADS
    bq = _attn_block(T)
    nq = T // bq
    scale = FOX_HD ** -0.5
    lanes = [slice(h * LANE, (h + 1) * LANE) for h in range(HP)]

    def kern(q_ref, k_ref, v_ref, z_ref, cq_ref, ck_ref, o_ref, oa_ref, lse_ref):
        i = pl.program_id(1)
        q = [(q_ref[:, ln] * scale).astype(BF16) for ln in lanes]
        c0 = [cq_ref[h][:, 0:1] for h in range(HP)]

        def step(j, carry, diagonal=False):
            off = pl.multiple_of(j * bq, bq)
            s = [_nt(q[h], k_ref[pl.ds(off, bq), lanes[h]].astype(BF16)) + (c0[h] - ck_ref[h, :, pl.ds(off, bq)])
                 for h in range(HP)]
            ps, out = [], []
            for h in range(HP):
                m, l, acc = carry[h]
                sh = s[h]
                if diagonal:
                    sh = jnp.where(_iota2((bq, bq), 1) <= _iota2((bq, bq), 0), sh, NEG)
                m_new = jnp.maximum(m, jnp.max(sh, axis=1, keepdims=True))
                p = jnp.exp(sh - m_new)
                alpha = jnp.exp(m - m_new)
                ps.append(p.astype(BF16))
                out.append((m_new, alpha * l + jnp.sum(p, axis=1, keepdims=True), alpha * acc))
            return tuple((m, l, acc + _nn(ps[h], v_ref[pl.ds(off, bq), lanes[h]].astype(BF16)))
                         for h, (m, l, acc) in enumerate(out))

        init = tuple((jnp.full((bq, 1), NEG, F32), jnp.zeros((bq, 1), F32), jnp.zeros((bq, FOX_HD), F32)) for _ in range(HP))
        res = step(i, lax.fori_loop(0, i, step, init), diagonal=True)
        for h, (m, l, acc) in enumerate(res):
            o = acc / l
            z = z_ref[:, lanes[h]]
            o_ref[:, lanes[h]] = o
            oa_ref[:, lanes[h]] = (o * z * _sigmoid(z)).astype(BF16)
            lse_ref[h] = m + jnp.log(l)

    W = HP * LANE
    return pl.pallas_call(
        kern, name="fox_attn_fwd",
        out_shape=[jax.ShapeDtypeStruct((T, 1024), F32), jax.ShapeDtypeStruct((T, 1024), BF16),
                   jax.ShapeDtypeStruct((H, T, 1), F32)],
        grid=(H // HP, nq),
        in_specs=[pl.BlockSpec((bq, W), lambda g, i: (i, g)),
                  pl.BlockSpec((T, W), lambda g, i: (0, 8 // HP + g)),
                  pl.BlockSpec((T, W), lambda g, i: (0, 16 // HP + g)),
                  pl.BlockSpec((bq, W), lambda g, i: (i, 24 // HP + g)),
                  pl.BlockSpec((HP, 1, bq), lambda g, i: (g, 0, i)),
                  pl.BlockSpec((HP, 1, T), lambda g, i: (g, 0, 0))],
        out_specs=[pl.BlockSpec((bq, W), lambda g, i: (i, g)),
                   pl.BlockSpec((bq, W), lambda g, i: (i, g)),
                   pl.BlockSpec((HP, bq, 1), lambda g, i: (g, i, 0))],
        compiler_params=_cp(("parallel", "arbitrary")),
    )(u, u, u, u, c3, c3)


def _attn_probs(s, lse_i, diagonal):
    if not diagonal:
        return jnp.exp(s - lse_i)
    keep = _iota2(s.shape, 1) <= _iota2(s.shape, 0)
    return jnp.where(keep, jnp.exp(jnp.where(keep, s, NEG) - lse_i), 0.0)


def _attn_delta(u, c3, o, lse, doa, T):
    H, HP = 8, ATTN_HEADS
    bq = _attn_block(T)
    nq = T // bq
    scale = FOX_HD ** -0.5
    lanes = [slice(h * LANE, (h + 1) * LANE) for h in range(HP)]

    def kern(q_ref, k_ref, v_ref, z_ref, c_ref, o_ref, lse_ref, doa_ref, do_ref, dz_ref, dl_ref):
        i = pl.program_id(1)
        z = z_ref[...]
        sg = _sigmoid(z)
        dov = doa_ref[...]
        do_all = (dov * z * sg).astype(BF16)
        do_ref[...] = do_all
        dz_ref[...] = dov * o_ref[...] * (sg * (1.0 + z * (1.0 - sg)))
        qs = [(q_ref[:, ln] * scale).astype(BF16) for ln in lanes]
        dob = [do_all[:, ln] for ln in lanes]
        ioff = pl.multiple_of(i * bq, bq)
        c0 = [c_ref[h, :, pl.ds(ioff, bq)][:, 0:1] for h in range(HP)]
        lse_i = [lse_ref[h] for h in range(HP)]

        def step(j, acc, diagonal=False):
            off = pl.multiple_of(j * bq, bq)
            s = [_nt(qs[h], k_ref[pl.ds(off, bq), lanes[h]].astype(BF16)) + (c0[h] - c_ref[h, :, pl.ds(off, bq)])
                 for h in range(HP)]
            dp = [_nt(dob[h], v_ref[pl.ds(off, bq), lanes[h]].astype(BF16)) for h in range(HP)]
            return tuple(acc[h] + jnp.sum(_attn_probs(s[h], lse_i[h], diagonal) * dp[h], axis=1, keepdims=True)
                         for h in range(HP))

        init = tuple(jnp.zeros((bq, 1), F32) for _ in range(HP))
        res = step(i, lax.fori_loop(0, i, step, init), diagonal=True)
        for h in range(HP):
            dl_ref[h] = res[h]

    W = HP * LANE
    full = lambda cb: pl.BlockSpec((T, W), lambda g, i: (0, cb // HP + g))
    blk = lambda cb: pl.BlockSpec((bq, W), lambda g, i: (i, cb // HP + g))
    return pl.pallas_call(
        kern, name="fox_attn_delta",
        out_shape=[jax.ShapeDtypeStruct((T, 1024), BF16), jax.ShapeDtypeStruct((T, 1024), F32), jax.ShapeDtypeStruct((H, T, 1), F32)],
        grid=(H // HP, nq),
        in_specs=[blk(0), full(8), full(16), blk(24),
                  pl.BlockSpec((HP, 1, T), lambda g, i: (g, 0, 0)),
                  blk(0),
                  pl.BlockSpec((HP, bq, 1), lambda g, i: (g, i, 0)),
                  blk(0)],
        out_specs=[blk(0), blk(0), pl.BlockSpec((HP, bq, 1), lambda g, i: (g, i, 0))],
        compiler_params=_cp(("parallel", "arbitrary")),
    )(u, u, u, u, c3, o, lse, doa)


def _attn_bwd(u, c3, lse, do, delta, T):
    H, HP = 8, ATTN_HEADS
    bq = _attn_block(T)
    nq = T // bq
    scale = FOX_HD ** -0.5
    lanes = [slice(h * LANE, (h + 1) * LANE) for h in range(HP)]

    def kern(q_ref, k_ref, v_ref, c_ref, lse_ref, do_ref, dl_ref, dq_ref, dk_ref, dv_ref, dc_ref):
        j = pl.program_id(1)

        @pl.when(j == 0)
        def _():
            dq_ref[...] = jnp.zeros(dq_ref.shape, F32)

        kj = [k_ref[:, ln].astype(BF16) for ln in lanes]
        vj = [v_ref[:, ln].astype(BF16) for ln in lanes]
        joff = pl.multiple_of(j * bq, bq)
        ck = [c_ref[h, :, pl.ds(joff, bq)] for h in range(HP)]

        def step(i, carry, diagonal=False):
            off = pl.multiple_of(i * bq, bq)
            qs = [(q_ref[pl.ds(off, bq), ln] * scale).astype(BF16) for ln in lanes]
            dob = [do_ref[pl.ds(off, bq), ln] for ln in lanes]
            s = [_nt(qs[h], kj[h]) + (c_ref[h, :, pl.ds(off, bq)][:, 0:1] - ck[h]) for h in range(HP)]
            dp = [_nt(dob[h], vj[h]) for h in range(HP)]
            pb, dsb, dcs = [], [], []
            for h in range(HP):
                p = _attn_probs(s[h], lse_ref[h, pl.ds(off, bq), :], diagonal)
                ds = p * (dp[h] - dl_ref[h, pl.ds(off, bq), :])
                pb.append(p.astype(BF16))
                dsb.append(ds.astype(BF16))
                dcs.append(jnp.sum(ds, axis=0, keepdims=True))
            out = []
            for h, (dk, dv, dc) in enumerate(carry):
                dq_ref[pl.ds(off, bq), lanes[h]] += _nn(dsb[h], kj[h]) * scale
                out.append((dk + _tn(dsb[h], qs[h]), dv + _tn(pb[h], dob[h]), dc - dcs[h]))
            return tuple(out)

        init = tuple((jnp.zeros((bq, FOX_HD), F32), jnp.zeros((bq, FOX_HD), F32), jnp.zeros((1, bq), F32)) for _ in range(HP))
        res = lax.fori_loop(j + 1, nq, step, step(j, init, diagonal=True))
        for h, (dk, dv, dc) in enumerate(res):
            dk_ref[:, lanes[h]] = dk
            dv_ref[:, lanes[h]] = dv
            dc_ref[h] = dc

    W = HP * LANE
    full = lambda cb: pl.BlockSpec((T, W), lambda g, j: (0, cb // HP + g))
    blk = lambda cb: pl.BlockSpec((bq, W), lambda g, j: (j, cb // HP + g))
    col1 = pl.BlockSpec((HP, T, 1), lambda g, j: (g, 0, 0))
    return pl.pallas_call(
        kern, name="fox_attn_bwd",
        out_shape=[jax.ShapeDtypeStruct((T, 1024), F32)] * 3 + [jax.ShapeDtypeStruct((H, 1, T), F32)],
        grid=(H // HP, nq),
        in_specs=[full(0), blk(8), blk(16), pl.BlockSpec((HP, 1, T), lambda g, j: (g, 0, 0)), col1, full(0), col1],
        out_specs=[full(0), blk(0), blk(0), pl.BlockSpec((HP, 1, bq), lambda g, j: (g, 0, j))],
        compiler_params=_cp(("parallel", "arbitrary")),
    )(u, u, u, c3, lse, do, delta)


def _place():
    return lax.axis_index("x"), lax.axis_index("y"), lax.axis_index("c")


def _slot(p):
    return 4 * p[0] + 2 * p[1] + p[2]


def _other_chips(x, y):
    return [(1 - x, y), (x, 1 - y), (1 - x, 1 - y)]


def _allgather_steps(in_refs, out_refs, scratch):
    (src,), (dst,) = in_refs, out_refs
    send_sems, recv_sems, local_sem = scratch
    x, y, c = _place()
    me, sibling = (x, y, c), (x, y, 1 - c)
    chips = _other_chips(x, y)

    def copy(k, block, to, from_input=False):
        d = dst.at[_slot(block)]
        return pltpu.make_async_remote_copy(
            src_ref=src if from_input else d, dst_ref=d, send_sem=send_sems.at[k], recv_sem=recv_sems.at[k],
            device_id=to, device_id_type=MESH)

    def first_copies():
        return [copy(0, me, sibling, True)] + [copy(1 + j, me, (*chip, c), True) for j, chip in enumerate(chips)]

    def start():
        pltpu.make_async_copy(src, dst.at[_slot(me)], local_sem).start()
        for cp in first_copies():
            cp.start()

    def finish():
        passed = []
        for j, chip in enumerate(chips):
            copy(1 + j, (*chip, c), me).wait_recv()
            passed.append(copy(4 + j, (*chip, c), sibling))
            passed[-1].start()
        copy(0, sibling, me).wait_recv()
        for j, chip in enumerate(chips):
            copy(4 + j, (*chip, 1 - c), me).wait_recv()
        for cp in first_copies() + passed:
            cp.wait_send()
        pltpu.make_async_copy(src, dst.at[_slot(me)], local_sem).wait()

    return start, finish


def _allgather_relay_steps(in_refs, out_refs, scratch):
    (src,), (dst,) = in_refs, out_refs
    send_sems, recv_sems, local_sem = scratch
    x, y, c = _place()
    me, sibling = (x, y, c), (x, y, 1 - c)
    x_nbr, y_nbr, diag = (1 - x, y, c), (x, 1 - y, c), (1 - x, 1 - y, c)
    flip = lambda a, bit: a + bit - 2 * a * bit
    relay_from = (flip(x, 1 - c), flip(y, c), c)
    relay_to = (flip(x, c), flip(y, 1 - c), c)

    def copy(k, block, to, from_input=False):
        d = dst.at[_slot(block)]
        return pltpu.make_async_remote_copy(
            src_ref=src if from_input else d, dst_ref=d, send_sem=send_sems.at[k], recv_sem=recv_sems.at[k],
            device_id=to, device_id_type=MESH)

    def first_copies():
        return [copy(0, me, sibling, True), copy(1, me, x_nbr, True), copy(2, me, y_nbr, True)]

    def other(block):
        return block[:2] + (1 - c,)

    def start():
        pltpu.make_async_copy(src, dst.at[_slot(me)], local_sem).start()
        for cp in first_copies():
            cp.start()

    def finish():
        copy(1, x_nbr, me).wait_recv()
        copy(2, y_nbr, me).wait_recv()
        later = [copy(3, relay_from, relay_to), copy(4, x_nbr, sibling), copy(5, y_nbr, sibling)]
        for cp in later:
            cp.start()
        copy(3, diag, me).wait_recv()
        later.append(copy(6, diag, sibling))
        later[-1].start()
        copy(0, sibling, me).wait_recv()
        for k, block in ((4, x_nbr), (5, y_nbr), (6, diag)):
            copy(k, other(block), me).wait_recv()
        for cp in first_copies() + later:
            cp.wait_send()
        pltpu.make_async_copy(src, dst.at[_slot(me)], local_sem).wait()

    return start, finish


def _allgather_side(a, relay=False):
    return ((a,), (jax.ShapeDtypeStruct((N_DEV,) + a.shape, a.dtype),),
            (pltpu.SemaphoreType.DMA((7,)), pltpu.SemaphoreType.DMA((7,)), pltpu.SemaphoreType.DMA),
            _allgather_relay_steps if relay else _allgather_steps)


def _allgather(a, name, relay=False):
    ins, outs, scratch, make = _allgather_side(a, relay)

    def body(a_ref, o_ref, *scr):
        start, finish = make((a_ref,), (o_ref,), scr)
        start()
        finish()

    any_spec = pl.BlockSpec(memory_space=pl.ANY)
    return pl.pallas_call(body, name=name, out_shape=outs[0], in_specs=[any_spec], out_specs=any_spec,
                          scratch_shapes=list(scratch))(a)


def _exchange_pair(ga, gb):
    def body(ga_ref, gb_ref, ra_ref, rb_ref, send_sems, recv_sems):
        x, y, c = _place()
        sibling = (x, y, 1 - c)
        slots = [_slot(sibling)] + [_slot((*chip, 1 - c)) for chip in _other_chips(x, y)]
        cps = []
        for t, (src, dst) in enumerate(((ga_ref, ra_ref), (gb_ref, rb_ref))):
            for k, ps in enumerate(slots):
                cps.append(pltpu.make_async_remote_copy(
                    src_ref=src.at[ps], dst_ref=dst.at[k], send_sem=send_sems.at[t, k], recv_sem=recv_sems.at[t, k],
                    device_id=sibling, device_id_type=MESH))
        for cp in cps:
            cp.start()
        for cp in cps:
            cp.wait()

    any_spec = pl.BlockSpec(memory_space=pl.ANY)
    return pl.pallas_call(
        body, name="exchange_pair",
        out_shape=[jax.ShapeDtypeStruct((4,) + ga.shape[1:], ga.dtype), jax.ShapeDtypeStruct((4,) + gb.shape[1:], gb.dtype)],
        in_specs=[any_spec] * 2,
        out_specs=[any_spec] * 2,
        scratch_shapes=[pltpu.SemaphoreType.DMA((2, 4)), pltpu.SemaphoreType.DMA((2, 4))],
    )(ga, gb)


def _tiling(R, Cc, tile, by_cols):
    if by_cols:
        assert Cc % tile == 0
        return Cc // tile, (R, tile), lambda lead, i: (lead, 0, i)
    assert R % tile == 0
    return R // tile, (tile, Cc), lambda lead, i: (lead, i, 0)


def _pair_add(name, g, r1, slots, tile, by_cols=False):
    _, R, Cc = g.shape
    steps, blk, at = _tiling(R, Cc, tile, by_cols)

    def kern(s_ref, a_ref, b_ref, o_ref):
        o_ref[...] = (a_ref[...].astype(F32) + b_ref[...].astype(F32)).astype(o_ref.dtype)

    return pl.pallas_call(
        kern, name=name,
        out_shape=jax.ShapeDtypeStruct((3, R, Cc), BF16),
        grid_spec=pltpu.PrefetchScalarGridSpec(
            num_scalar_prefetch=1, grid=(3, steps),
            in_specs=[pl.BlockSpec((1,) + blk, lambda j, i, s: at(s[j], i)),
                      pl.BlockSpec((1,) + blk, lambda j, i, s: at(1 + j, i))],
            out_specs=pl.BlockSpec((1,) + blk, lambda j, i, s: at(j, i))),
        compiler_params=_cp(("arbitrary", "arbitrary")),
    )(slots, g, r1)


def _axis_neighbours():
    x, y, c = _place()
    flip = lambda a, bit: a + bit - 2 * a * bit
    return (flip(x, c), flip(y, 1 - c), c), (flip(x, 1 - c), flip(y, c), c), c


def _exchange_first_steps(in_refs, out_refs, scratch):
    pairs = list(zip(in_refs, out_refs))
    send_sems, recv_sems = scratch
    first, _, c = _axis_neighbours()

    def copies():
        return [pltpu.make_async_remote_copy(
            src_ref=src.at[j], dst_ref=dst.at[k], send_sem=send_sems.at[t, k], recv_sem=recv_sems.at[t, k],
            device_id=first, device_id_type=MESH)
            for t, (src, dst) in enumerate(pairs) for k, j in enumerate((1 - c, 2))]

    def start():
        for cp in copies():
            cp.start()

    def finish():
        for cp in copies():
            cp.wait()

    return start, finish


def _exchange_first_side(sa, sb):
    return ((sa, sb), (jax.ShapeDtypeStruct((2,) + sa.shape[1:], sa.dtype), jax.ShapeDtypeStruct((2,) + sb.shape[1:], sb.dtype)),
            (pltpu.SemaphoreType.DMA((2, 2)), pltpu.SemaphoreType.DMA((2, 2))), _exchange_first_steps)


def _axis_add(name, s, t1, core, tile, by_cols=False):
    _, R, Cc = s.shape
    steps, blk, at = _tiling(R, Cc, tile, by_cols)

    def kern(c_ref, a_ref, b_ref, o_ref):
        o_ref[...] = (a_ref[...].astype(F32) + b_ref[...].astype(F32)).astype(o_ref.dtype)

    return pl.pallas_call(
        kern, name=name,
        out_shape=jax.ShapeDtypeStruct((1, R, Cc), BF16),
        grid_spec=pltpu.PrefetchScalarGridSpec(
            num_scalar_prefetch=1, grid=(steps,),
            in_specs=[pl.BlockSpec((1,) + blk, lambda i, cr: at(cr[0], i)),
                      pl.BlockSpec((1,) + blk, lambda i, cr: at(1, i))],
            out_specs=pl.BlockSpec((1,) + blk, lambda i, cr: at(0, i))),
        compiler_params=_cp(("arbitrary",)),
    )(core, s, t1)


def _exchange_second(pa, pb):
    def body(pa_ref, pb_ref, ta_ref, tb_ref, send_sems, recv_sems):
        _, second, _ = _axis_neighbours()
        cps = [pltpu.make_async_remote_copy(src_ref=src, dst_ref=dst, send_sem=send_sems.at[t], recv_sem=recv_sems.at[t],
                                            device_id=second, device_id_type=MESH)
               for t, (src, dst) in enumerate(((pa_ref, ta_ref), (pb_ref, tb_ref)))]
        for cp in cps:
            cp.start()
        for cp in cps:
            cp.wait()

    any_spec = pl.BlockSpec(memory_space=pl.ANY)
    return pl.pallas_call(
        body, name="exchange_second",
        out_shape=[jax.ShapeDtypeStruct(pa.shape, pa.dtype), jax.ShapeDtypeStruct(pb.shape, pb.dtype)],
        in_specs=[any_spec] * 2, out_specs=[any_spec] * 2,
        scratch_shapes=[pltpu.SemaphoreType.DMA((2,)), pltpu.SemaphoreType.DMA((2,))],
    )(pa, pb)


def _adamw(name, w, m, v, parts, tile, by_cols=False):
    R, Cc = w.shape
    steps, blk_shape, at = _tiling(R, Cc, tile, by_cols)
    n_parts = len(parts)

    def kern(*refs):
        w_ref, m_ref, v_ref = refs[:3]
        g = None
        for r_ref, (_, n) in zip(refs[3:3 + n_parts], parts):
            for s in range(n):
                term = r_ref[s].astype(F32)
                g = term if g is None else g + term
        g_out, d_out, m_out, v_out = refs[3 + n_parts:]
        mn = ADAM_B1 * m_ref[...] + (1.0 - ADAM_B1) * g
        vn = ADAM_B2 * v_ref[...] + (1.0 - ADAM_B2) * (g * g)
        m_hat = mn / (1.0 - ADAM_B1 ** ADAM_STEP)
        v_hat = vn / (1.0 - ADAM_B2 ** ADAM_STEP)
        g_out[...] = g
        d_out[...] = -ADAM_LR * (m_hat / (jnp.sqrt(v_hat) + ADAM_EPS) + ADAM_WD * w_ref[...])
        m_out[...] = mn
        v_out[...] = vn

    blk = pl.BlockSpec(blk_shape, lambda i: at(0, i)[1:])
    return pl.pallas_call(
        kern, name=name,
        out_shape=[jax.ShapeDtypeStruct((R, Cc), F32)] * 4,
        grid=(steps,),
        in_specs=[blk] * 3 + [pl.BlockSpec((n,) + blk_shape, lambda i: at(0, i)) for (_, n) in parts],
        out_specs=[blk] * 4,
        compiler_params=_cp(("arbitrary",)),
    )(w, m, v, *[a for (a, _) in parts])


def _pad_cols(a, w):
    return jnp.pad(a, ((0, 0), (0, w - a.shape[1])))


def _pad_rows(a, r):
    return jnp.pad(a, ((0, r - a.shape[0]), (0, 0)))


def _pack_b(pf, pr, wo, w2, a2, rows):
    body = jnp.concatenate([pf, pr, wo.reshape(2048, 256), jnp.concatenate([w2, a2], axis=1)], axis=0)
    return _pad_rows(body, rows)


def kernel(x, norm_gain, w_in, fox_forget_bias, rwkv_shift_mix, rwkv_w0, rwkv_w2, rwkv_a0, rwkv_a2, rwkv_k_k, rwkv_k_a, rwkv_r_k, rwkv_ln_w, rwkv_ln_b, w_proj_fox, w_proj_rwkv, w_out, final_norm_gain, loss_target, m_norm_gain, m_w_in, m_fox_forget_bias, m_rwkv_shift_mix, m_rwkv_w0, m_rwkv_w2, m_rwkv_a0, m_rwkv_a2, m_rwkv_k_k, m_rwkv_k_a, m_rwkv_r_k, m_rwkv_ln_w, m_rwkv_ln_b, m_w_proj_fox, m_w_proj_rwkv, m_w_out, m_final_norm_gain, v_norm_gain, v_w_in, v_fox_forget_bias, v_rwkv_shift_mix, v_rwkv_w0, v_rwkv_w2, v_rwkv_a0, v_rwkv_a2, v_rwkv_k_k, v_rwkv_k_a, v_rwkv_r_k, v_rwkv_ln_w, v_rwkv_ln_b, v_w_proj_fox, v_w_proj_rwkv, v_w_out, v_final_norm_gain):
    T, D = x.shape[1], x.shape[2]
    assert D == 2048 and T % LANE == 0
    NI = w_in.shape[2]
    IN = N_DEV * NI
    RB = 4224
    x2 = x[0]
    lt2 = loss_target[0]
    me = _slot(_place())

    wa = _allgather(w_in[0].astype(BF16), "allgather_w_in", relay=True)
    packed_own = _pack_b(w_proj_fox[0], w_proj_rwkv[0], w_out[0], rwkv_w2[0], rwkv_a2[0], RB).astype(BF16)
    sections = [(0, 4096, 0), (4104, 4096, 4096), (8392, 4096, 8192), (4096, 8, 12288), (8200, 96, 12544), (8296, 96, 12672)]
    NP = 12800
    pieces, at_col = [], 0
    for lo, width, pad_lo in sections:
        if pad_lo > at_col:
            pieces.append(jnp.zeros((D, pad_lo - at_col), BF16))
        col = lo
        while col < lo + width:
            d = col // NI
            stop = min(lo + width, (d + 1) * NI)
            pieces.append(wa[d, :, col - d * NI:stop - d * NI])
            col = stop
        at_col = pad_lo + width
    pieces.append(jnp.zeros((D, NP - at_col), BF16))
    w_pad = jnp.concatenate(pieces, axis=1)
    F_CB, LORA_CB = 96, 49

    mu = rwkv_shift_mix
    mu_main = mu[:, 0:4096]
    mu_lora = jnp.concatenate([_pad_cols(mu[:, 4096:4192], LANE), _pad_cols(mu[:, 4192:4288], LANE)], axis=1)
    bias_pad = _pad_cols(fox_forget_bias, LANE)
    rk_flat = rwkv_r_k.reshape(1, 1024)
    gf = final_norm_gain.reshape(1, D)

    tb = min(256, T)
    tbh = min(128, T)
    (h,) = _rowcall("rms_fwd", T, tb, [_rows(x2, tb, D), _whole(norm_gain)],
                    lambda i, xv, g: ([_rms_math(xv, g)], []), [(D, BF16)])
    u, wb = _mm(h, w_pad, tm=1024, tn=1280, name="mm_in", side=_allgather_side(packed_own))
    wpf = wb[:, 0:1024, :].transpose(1, 0, 2).reshape(1024, D)
    wpr = wb[:, 1024:2048, :].transpose(1, 0, 2).reshape(1024, D)
    wo = wb[:, 2048:4096, :].reshape(N_DEV * 256, D)
    w2p = _pad_rows(wb[:, 4096:4192, 0:128].transpose(1, 0, 2).reshape(96, 1024).astype(F32), LANE)
    a2p = _pad_rows(wb[:, 4096:4192, 128:256].transpose(1, 0, 2).reshape(96, 1024).astype(F32), LANE)

    c8 = _gates_fwd(u, bias_pad, T, F_CB)
    c3 = c8.reshape(8, 1, T)
    o_raw, o_a, lse = _attn_fwd(u, c3, T)

    def shift_body(i, um, hm, ul, hl, mm_, ml):
        outs = []
        for uv, hv, mv in ((um, hm, mm_), (ul, hl, ml)):
            hv = jnp.where(i == 0, 0.0, hv)
            prev = pltpu.roll(jnp.concatenate([hv, uv], axis=0), 1, 0)[SUB:]
            outs.append(uv + (prev - uv) * mv)
        return outs, []

    def halo_prev(arr, w, cb):
        return (arr, (SUB, w), lambda i: (jnp.maximum(i * (tbh // SUB) - 1, 0), cb))

    xs, xl = _rowcall("rwkv_shift_fwd", T, tbh,
                      [_rows(u, tbh, 4096, 1), halo_prev(u, 4096, 1), _rows(u, tbh, 256, LORA_CB), halo_prev(u, 256, LORA_CB),
                       _whole(mu_main), _whole(mu_lora)],
                      shift_body, [(4096, F32), (256, F32)])

    prep_par = [_whole(rwkv_w0), _whole(rwkv_a0), _whole(rwkv_k_k), _whole(rwkv_k_a), _whole(w2p), _whole(a2p)]
    prep_rows = [_rows(xs, tbh, 1024, 1), _rows(xl, tbh, LANE, 0), _rows(xl, tbh, LANE, 1)]
    lw, kp, an, bb = _rowcall("rwkv_prep_fwd", T, tbh, prep_rows + prep_par,
                              lambda i, *a: (list(_prep_math(*a)), []), [(1024, F32)] * 4)
    y, states, invs = _scan_fwd(xs, lw, kp, an, bb, T)
    post_rows = [_rows(y, tbh, 1024), _rows(xs, tbh, 1024, 0), _rows(kp, tbh, 1024), _rows(xs, tbh, 1024, 2), _rows(xs, tbh, 1024, 3)]
    post_par = [_whole(rwkv_ln_w), _whole(rwkv_ln_b), _whole(rk_flat)]
    (o_b,) = _rowcall("rwkv_post_fwd", T, tbh, post_rows + post_par,
                      lambda i, *a: ([_post_math(*a)], []), [(1024, BF16)])

    pa = _mm(o_a, wpf, name="mm_proj_fox")
    pb = _mm(o_b, wpr, name="mm_proj_rwkv")
    merge_rows = [_rows(u, tb, D, 4), _rows(u, tb, D, 5), _rows(pa, tb, D), _rows(pb, tb, D)]
    (mg,) = _rowcall("merge_fwd", T, tb, merge_rows, lambda i, *a: ([_merge_math(*a)], []), [(D, BF16)])
    mo = _mm(mg, wo, name="mm_out")

    def head_body(i, xv, mov, ltv, g):
        out = xv + mov
        r = lax.rsqrt(jnp.mean(out * out, axis=-1, keepdims=True) + RMS_EPS)
        yn = out * r
        err = yn * g - ltv
        loss = 0.5 * jnp.sum(jnp.sum(err * err, axis=-1, keepdims=True), axis=0, keepdims=True) / D
        dyv = err / D
        dyn = dyv * g
        dout = r * (dyn - yn * jnp.mean(dyn * yn, axis=-1, keepdims=True))
        return [dout], [loss, jnp.sum(dyv * yn, axis=0, keepdims=True)]

    dout, loss_p, dgf_p = _rowcall("loss_head", T, tb, [_rows(x2, tb, D), _rows(mo, tb, D), _rows(lt2, tb, D), _whole(gf)],
                                   head_body, [(D, F32)], [(1, 1), (1, D)])

    dm = _mm(dout, wo, tb=True, name="mm_out_dx")
    dwo = _mm(mg, dout, ta=True, out_dtype=BF16, name="mm_out_dw")

    def merge_bwd_body(i, ga, gb, pav, pbv, dmv):
        _, vjp = jax.vjp(_merge_math, ga, gb, pav, pbv)
        dga, dgb, dpa, dpb = vjp(dmv)
        return [dga, dgb, dpa, dpb], []

    dga, dgb, dpa, dpb = _rowcall("merge_bwd", T, tb, merge_rows + [_rows(dm, tb, D)], merge_bwd_body,
                                  [(D, BF16), (D, BF16), (D, BF16), (D, BF16)])
    doa = _mm(dpa, wpf, tb=True, name="mm_proj_fox_dx")
    dwpf = _mm(o_a, dpa, ta=True, out_dtype=BF16, name="mm_proj_fox_dw")
    dob = _mm(dpb, wpr, tb=True, name="mm_proj_rwkv_dx")
    dwpr = _mm(o_b, dpb, ta=True, out_dtype=BF16, name="mm_proj_rwkv_dw")

    do_b, dza, delta = _attn_delta(u, c3, o_raw, lse, doa, T)
    dq, dk, dv, dc3 = _attn_bwd(u, c3, lse, do_b, delta, T)
    dfl, dbias_p = _gates_bwd(dc3.reshape(8, T), u, bias_pad, T, F_CB)

    def post_bwd_body(i, yv, rv, kpv, vv, zv, lnw, lnb, rkv, dobv):
        _, vjp = jax.vjp(_post_math, yv, rv, kpv, vv, zv, lnw, lnb, rkv)
        dy_, dr_, dkp_, dv_, dz_, dlnw, dlnb, drk = vjp(dobv)
        return [dy_, dr_, dkp_, dv_, dz_], [dlnw, dlnb, drk]

    dy_s, dr_p, dkp_p, dv_p, dzb, dlnw_p, dlnb_p, drk_p = _rowcall(
        "rwkv_post_bwd", T, tbh, post_rows + post_par + [_rows(dob, tbh, 1024)], post_bwd_body,
        [(1024, F32)] * 5, [(1, 1024)] * 3)
    dr_s, dlw, dkp_s, dv_s, dan, dbb = _scan_bwd(xs, lw, kp, an, bb, states, invs, dy_s, T)

    def prep_bwd_body(i, xk, xwd, xad, w0, a0, kkw, kaw, w2v, a2v, dlw_, dkp1, dkp2, dan_, dbb_, dr1, dr2, dv1, dv2, dz_):
        _, vjp = jax.vjp(_prep_math, xk, xwd, xad, w0, a0, kkw, kaw, w2v, a2v)
        dxk, dxwd, dxad, dw0, da0, dkk, dka, dw2, da2 = vjp((dlw_, dkp1 + dkp2, dan_, dbb_))
        return [[dr1 + dr2, dxk, dv1 + dv2, dz_], [dxwd, dxad]], [dw0, da0, dkk, dka, dw2, da2]

    cots = [dlw, dkp_s, dkp_p, dan, dbb, dr_s, dr_p, dv_s, dv_p, dzb]
    dxs, dxl, dw0_p, da0_p, dkk_p, dka_p, dw2_p, da2_p = _rowcall(
        "rwkv_prep_bwd", T, tbh, prep_rows + prep_par + [_rows(c_, tbh, 1024) for c_ in cots], prep_bwd_body,
        [(4096, F32), (256, F32)], [(1, 1024)] * 4 + [(LANE, 1024)] * 2)

    def shift_bwd_body(i, dm_, hm, dl_, hl, um, pm, ul, pl_, mm_, ml):
        last = i == T // tbh - 1
        outs, accs = [], []
        for dv_, hv, uv, pv, mv in ((dm_, hm, um, pm, mm_), (dl_, hl, ul, pl_, ml)):
            hv = jnp.where(last, 0.0, hv)
            nxt = pltpu.roll(jnp.concatenate([dv_, hv], axis=0), tbh + SUB - 1, 0)[:tbh]
            pv = jnp.where(i == 0, 0.0, pv)
            prev = pltpu.roll(jnp.concatenate([pv, uv], axis=0), 1, 0)[SUB:]
            outs.append(dv_ * (1.0 - mv) + nxt * mv)
            accs.append(jnp.sum(dv_ * (prev - uv), axis=0, keepdims=True))
        return outs, accs

    def halo_next(arr, w, cb):
        last_blk = T // SUB - 1
        return (arr, (SUB, w), lambda i: (jnp.minimum((i + 1) * (tbh // SUB), last_blk), cb))

    du_b, du_l, dmu_main_p, dmu_lora_p = _rowcall(
        "rwkv_shift_bwd", T, tbh,
        [_rows(dxs, tbh, 4096), halo_next(dxs, 4096, 0), _rows(dxl, tbh, 256), halo_next(dxl, 256, 0),
         _rows(u, tbh, 4096, 1), halo_prev(u, 4096, 1), _rows(u, tbh, 256, LORA_CB), halo_prev(u, 256, LORA_CB),
         _whole(mu_main), _whole(mu_lora)],
        shift_bwd_body, [(4096, BF16), (256, BF16)], [(1, 4096), (1, 256)])

    du = jnp.concatenate([dq.astype(BF16), dk.astype(BF16), dv.astype(BF16), dza.astype(BF16), du_b, dga, dgb,
                          dfl.astype(BF16), jnp.zeros((T, LANE), BF16), du_l], axis=1)
    dw_pad_t = _mm(du, h, ta=True, out_dtype=BF16, tm=1280, tn=1024, name="mm_in_dw")

    by_col = sorted(sections)
    blocks = []
    for d in range(N_DEV):
        parts = []
        for lo, width, pad_lo in by_col:
            a, b = max(lo, d * NI), min(lo + width, (d + 1) * NI)
            if a < b:
                parts.append(dw_pad_t[pad_lo + a - lo:pad_lo + b - lo, :])
        blocks.append(jnp.concatenate(parts, axis=0)[None])
    ga = jnp.concatenate(blocks, axis=0)
    lora_g = jnp.concatenate([dw2_p[:96].reshape(96, N_DEV, 128).transpose(1, 0, 2),
                              da2_p[:96].reshape(96, N_DEV, 128).transpose(1, 0, 2)], axis=2).astype(BF16)
    gb = jnp.concatenate([dwpf.reshape(1024, N_DEV, 256).transpose(1, 0, 2),
                          dwpr.reshape(1024, N_DEV, 256).transpose(1, 0, 2),
                          dwo.reshape(N_DEV, 2048, 256), lora_g, jnp.zeros((N_DEV, RB - 4192, 256), BF16)], axis=1)

    xx, yy, cc = _place()
    chip_slots = jnp.stack([_slot((*chip, cc)) for chip in _other_chips(xx, yy)]).astype(jnp.int32)
    r1a, r1b = _exchange_pair(ga, gb)
    sa = _pair_add("pair_add_w_in", ga, r1a, chip_slots, 256, by_cols=True)
    sb = _pair_add("pair_add_packed", gb, r1b, chip_slots, 384)
    ga_own = lax.dynamic_index_in_dim(ga, me, 0, keepdims=True)
    gb_own = lax.dynamic_index_in_dim(gb, me, 0, keepdims=True)
    dh, t1a, t1b = _mm(du, w_pad, tb=True, tm=1024, tn=1024, tk=NP // 10, name="mm_in_dx", side=_exchange_first_side(sa, sb))
    core = jnp.stack([cc]).astype(jnp.int32)
    t2a, t2b = _exchange_second(_axis_add("axis_add_w_in", sa, t1a, core, 256, by_cols=True),
                                _axis_add("axis_add_packed", sb, t1b, core, 384))

    def rms_bwd_body(i, xv, g, dhv, doutv):
        _, vjp = jax.vjp(_rms_math, xv, g)
        dx_, dg_ = vjp(dhv)
        return [dx_ + doutv], [dg_]

    grad_x2, dng_p = _rowcall("rms_bwd", T, tb, [_rows(x2, tb, D), _whole(norm_gain), _rows(dh, tb, D), _rows(dout, tb, D)],
                              rms_bwd_body, [(D, F32)], [(1, D)])

    dmu = jnp.concatenate([dmu_main_p, dmu_lora_p[:, 0:96], dmu_lora_p[:, 128:224]], axis=1)
    small_parts = [dng_p, dbias_p[:, 0:8], dmu, dw0_p, da0_p, dkk_p, dka_p, drk_p, dlnw_p, dlnb_p, dgf_p, loss_p]
    SR = 128
    small = _pad_cols(jnp.concatenate(small_parts, axis=1), SR * LANE).reshape(SR, LANE)
    rs = _allgather(small, "allgather_small")

    w_in_outs = _adamw("adamw_w_in", w_in[0].T, m_w_in[0].T, v_w_in[0].T, [(ga_own, 1), (r1a, 1), (t1a, 1), (t2a, 1)], 256, by_cols=True)
    g_in, d_in, m_in, v_in = [o.T for o in w_in_outs]
    pk = lambda pf, pr, wo_, w2_, a2_: _pack_b(pf[0], pr[0], wo_[0], w2_[0], a2_[0], RB)
    outs_b = _adamw("adamw_packed", pk(w_proj_fox, w_proj_rwkv, w_out, rwkv_w2, rwkv_a2),
                    pk(m_w_proj_fox, m_w_proj_rwkv, m_w_out, m_rwkv_w2, m_rwkv_a2),
                    pk(v_w_proj_fox, v_w_proj_rwkv, v_w_out, v_rwkv_w2, v_rwkv_a2), [(gb_own, 1), (r1b, 1), (t1b, 1), (t2b, 1)], 384)

    def pack_small(ng, fb, sm, w0, a0, kk_, ka_, rk_, lnw, lnb, fg):
        parts = [ng, fb, sm, w0, a0, kk_, ka_, rk_.reshape(1, 1024), lnw, lnb, fg.reshape(1, D), jnp.zeros((1, 1), F32)]
        return _pad_cols(jnp.concatenate(parts, axis=1), SR * LANE).reshape(SR, LANE)

    outs_s = _adamw("adamw_small",
                    pack_small(norm_gain, fox_forget_bias, rwkv_shift_mix, rwkv_w0, rwkv_a0, rwkv_k_k, rwkv_k_a, rwkv_r_k,
                               rwkv_ln_w, rwkv_ln_b, final_norm_gain),
                    pack_small(m_norm_gain, m_fox_forget_bias, m_rwkv_shift_mix, m_rwkv_w0, m_rwkv_a0, m_rwkv_k_k, m_rwkv_k_a,
                               m_rwkv_r_k, m_rwkv_ln_w, m_rwkv_ln_b, m_final_norm_gain),
                    pack_small(v_norm_gain, v_fox_forget_bias, v_rwkv_shift_mix, v_rwkv_w0, v_rwkv_a0, v_rwkv_k_k, v_rwkv_k_a,
                               v_rwkv_r_k, v_rwkv_ln_w, v_rwkv_ln_b, v_final_norm_gain),
                    [(rs, N_DEV)], SR)

    def unpack_b(pkd):
        return dict(w_proj_fox=pkd[0:1024][None], w_proj_rwkv=pkd[1024:2048][None], w_out=pkd[2048:4096].reshape(1, 256, D),
                    rwkv_w2=pkd[4096:4192, 0:128][None], rwkv_a2=pkd[4096:4192, 128:256][None])

    def unpack_s(pkd):
        flat = pkd.reshape(1, SR * LANE)
        names = [("norm_gain", D), ("fox_forget_bias", 8), ("rwkv_shift_mix", 4288), ("rwkv_w0", 1024), ("rwkv_a0", 1024),
                 ("rwkv_k_k", 1024), ("rwkv_k_a", 1024), ("rwkv_r_k", 1024), ("rwkv_ln_w", 1024), ("rwkv_ln_b", 1024),
                 ("final_norm_gain", D), ("loss", 1)]
        out, off = {}, 0
        for nm, n in names:
            out[nm] = flat[:, off:off + n]
            off += n
        out["rwkv_r_k"] = out["rwkv_r_k"].reshape(1, 16, 64)
        out["final_norm_gain"] = out["final_norm_gain"].reshape(D)
        return out

    order = ["norm_gain", "w_in", "fox_forget_bias", "rwkv_shift_mix", "rwkv_w0", "rwkv_w2", "rwkv_a0", "rwkv_a2", "rwkv_k_k",
             "rwkv_k_a", "rwkv_r_k", "rwkv_ln_w", "rwkv_ln_b", "w_proj_fox", "w_proj_rwkv", "w_out", "final_norm_gain"]
    result = []
    loss = None
    for kind, big in enumerate((g_in, d_in, m_in, v_in)):
        d = {**unpack_b(outs_b[kind]), **unpack_s(outs_s[kind]), "w_in": big[None]}
        if kind == 0:
            loss = d["loss"].reshape(())
        result += [d[n] for n in order]
    return (loss, grad_x2[None], *result)
```

```python
import functools

import jax
import jax.numpy as jnp
from jax import lax
from jax.experimental import pallas as pl
from jax.experimental.pallas import tpu as pltpu

F32 = jnp.float32
BF16 = jnp.bfloat16
HI = lax.Precision.HIGHEST
H3 = lax.Precision.HIGH
MESH = pl.DeviceIdType.MESH

FOX_HD = 128
RW_HD = 64
RMS_EPS = 1e-6
GN_EPS = 64e-5
L2_EPS = 1e-12
ADAM_LR = 0.001
ADAM_B1 = 0.9
ADAM_B2 = 0.999
ADAM_EPS = 1e-08
ADAM_WD = 0.01
ADAM_STEP = 10

LANE = 128
SUB = 8
VMEM_LIMIT = 56 * 1024 * 1024
N_DEV = 8
CHUNK = 64
SCAN_GROUP = 4
SCAN_PAIRS = 2
PS = None
NEG = -1e30


def _scan_shape(T):
    c = min(CHUNK, T)
    return c, min(SCAN_GROUP, T // c)


def _cp(sem=None):
    return pltpu.CompilerParams(dimension_semantics=sem, vmem_limit_bytes=VMEM_LIMIT)


def _sigmoid(x):
    return jax.nn.sigmoid(x)


def _softplus(x):
    return jnp.maximum(x, 0.0) + jnp.log(1.0 + jnp.exp(-jnp.abs(x)))


def _nn(a, b, prec=None):
    return lax.dot_general(a, b, (((1,), (0,)), ((), ())), precision=prec, preferred_element_type=F32)


def _nt(a, b, prec=None):
    return lax.dot_general(a, b, (((1,), (1,)), ((), ())), precision=prec, preferred_element_type=F32)


def _tn(a, b, prec=None):
    return lax.dot_general(a, b, (((0,), (0,)), ((), ())), precision=prec, preferred_element_type=F32)


def _iota2(shape, dim):
    return lax.broadcasted_iota(jnp.int32, shape, dim)


def _seg_sum(x):
    r = _iota2((LANE, LANE), 0) // RW_HD
    c = _iota2((LANE, LANE), 1) // RW_HD
    bd = (r == c).astype(F32)
    parts = [_nn(x[:, j * LANE:(j + 1) * LANE], bd, H3) for j in range(x.shape[1] // LANE)]
    return parts[0] if len(parts) == 1 else jnp.concatenate(parts, axis=1)


def _mm(a, b, *, ta=False, tb=False, out_dtype=F32, tm=1024, tn=1024, tk=None, name, side=None):
    assert not (ta and tb)
    K, M = a.shape if ta else a.shape[::-1]
    N = b.shape[0] if tb else b.shape[1]
    tm, tn = min(tm, M), min(tn, N)
    tk = K if tk is None else tk
    nk = K // tk
    assert M % tm == 0 and N % tn == 0 and K % tk == 0
    gi, gj = M // tm, N // tn
    a_spec = pl.BlockSpec((tk, tm), lambda i, j, k: (k, i)) if ta else pl.BlockSpec((tm, tk), lambda i, j, k: (i, k))
    b_spec = pl.BlockSpec((tn, tk), lambda i, j, k: (j, k)) if tb else pl.BlockSpec((tk, tn), lambda i, j, k: (k, j))
    side_ins, side_outs, side_scr, side_make = side if side is not None else ((), (), (), None)
    n_si, n_so = len(side_ins), len(side_outs)
    n_acc = 0 if nk == 1 else 1

    def body(*refs):
        a_ref, b_ref = refs[:2]
        o_ref = refs[2 + n_si]
        scr = refs[3 + n_si + n_so:]
        k = pl.program_id(2)
        if side_make is not None:
            start, finish = side_make(refs[2:2 + n_si], refs[3 + n_si:3 + n_si + n_so], scr[n_acc:])
            first = jnp.logical_and(jnp.logical_and(pl.program_id(0) == 0, pl.program_id(1) == 0), k == 0)
            last = jnp.logical_and(jnp.logical_and(pl.program_id(0) == gi - 1, pl.program_id(1) == gj - 1), k == nk - 1)
            pl.when(first)(start)
        av = a_ref[...].astype(BF16)
        bv = b_ref[...].astype(BF16)
        p = _tn(av, bv) if ta else _nt(av, bv) if tb else _nn(av, bv)
        if nk == 1:
            o_ref[...] = p.astype(out_dtype)
        else:
            acc_ref = scr[0]

            @pl.when(k == 0)
            def _():
                acc_ref[...] = p

            @pl.when(k > 0)
            def _():
                acc_ref[...] += p

            @pl.when(k == nk - 1)
            def _():
                o_ref[...] = acc_ref[...].astype(out_dtype)
        if side_make is not None:
            pl.when(last)(finish)

    any_spec = pl.BlockSpec(memory_space=pl.ANY)
    res = pl.pallas_call(
        body, name=name,
        out_shape=[jax.ShapeDtypeStruct((M, N), out_dtype)] + list(side_outs),
        grid=(gi, gj, nk),
        in_specs=[a_spec, b_spec] + [any_spec] * n_si,
        out_specs=[pl.BlockSpec((tm, tn), lambda i, j, k: (i, j))] + [any_spec] * n_so,
        scratch_shapes=([] if nk == 1 else [pltpu.VMEM((tm, tn), F32)]) + list(side_scr),
        compiler_params=_cp(("arbitrary",) * 3 if side is not None else ("parallel", "parallel", "arbitrary")),
    )(a, b, *side_ins)
    return res if side is not None else res[0]


def _rows(arr, tb, w, cb=0):
    return (arr, (tb, w), lambda i: (i, cb))


def _whole(arr):
    nd = arr.ndim
    return (arr, arr.shape, lambda i: (0,) * nd)


def _rowcall(name, T, tb, ins, body, outs, accs=()):
    n_in, n_out = len(ins), len(outs)

    def kern(*refs):
        i = pl.program_id(0)
        vals = [r[...] for r in refs[:n_in]]
        ro, ao = body(i, *vals)
        for r, v in zip(refs[n_in:n_in + n_out], ro):
            if isinstance(v, (list, tuple)):
                off = 0
                for piece in v:
                    w = piece.shape[1]
                    r[:, off:off + w] = piece.astype(r.dtype)
                    off += w
            else:
                r[...] = v.astype(r.dtype)
        if accs:
            acc_refs = refs[n_in + n_out:]

            @pl.when(i == 0)
            def _():
                for r in acc_refs:
                    r[...] = jnp.zeros(r.shape, F32)

            for r, v in zip(acc_refs, ao):
                r[...] += v

    out_shape = [jax.ShapeDtypeStruct((T, w), dt) for (w, dt) in outs] + [jax.ShapeDtypeStruct(s, F32) for s in accs]
    out_specs = [pl.BlockSpec((tb, w), lambda i: (i, 0)) for (w, dt) in outs] + [pl.BlockSpec(s, lambda i: (0, 0)) for s in accs]
    res = pl.pallas_call(
        kern, name=name,
        out_shape=out_shape,
        grid=(T // tb,),
        in_specs=[pl.BlockSpec(bs, im) for (_, bs, im) in ins],
        out_specs=out_specs,
        compiler_params=_cp(("arbitrary",)),
    )(*[a for (a, _, _) in ins])
    return res


def _rms_math(x, g):
    r = lax.rsqrt(jnp.mean(x * x, axis=-1, keepdims=True) + RMS_EPS)
    return x * r * g


def _merge_math(ga, gb, pa, pb):
    return _sigmoid(ga) * pa + _sigmoid(gb) * pb


def _prep_math(xk, xwd, xad, w0, a0, kk_w, ka_w, w2p, a2p):
    z = w0 + _nn(jnp.tanh(xwd), w2p, H3)
    w = -_softplus(-z) - 0.5
    lw = -jnp.exp(w)
    ag = _sigmoid(a0 + _nn(xad, a2p, H3))
    p = xk * kk_w
    n = jnp.maximum(jnp.sqrt(_seg_sum(p * p)), L2_EPS)
    kk = p / n
    kp = xk * (1.0 + (ag - 1.0) * ka_w)
    return lw, kp, -kk, kk * ag


def _post_math(y, r, kp, v, z, lnw, lnb, rk):
    inv = 1.0 / RW_HD
    mu = _seg_sum(y) * inv
    d = y - mu
    var = _seg_sum(d * d) * inv
    yn = d * lax.rsqrt(var + GN_EPS) * lnw + lnb
    bonus = _seg_sum(r * kp * rk) * v
    return (yn + bonus) * (z * _sigmoid(z))


@functools.partial(jax.custom_vjp, nondiff_argnums=(1,))
def _unit_inverses(ms, depth):
    eye = (_iota2(ms[0].shape, 0) == _iota2(ms[0].shape, 1)).astype(F32)
    half = ms[0].shape[1]
    mp = [_nn(m, m, PS) for m in ms]
    inv = [eye + m for m in ms]
    n = 2
    while n < depth:
        last = 2 * n >= depth
        for i in range(len(ms)):
            if last:
                inv[i] = inv[i] + _nn(mp[i], inv[i], PS)
            else:
                z = _nn(mp[i], jnp.concatenate([mp[i], inv[i]], axis=1), PS)
                mp[i], inv[i] = z[:, :half], inv[i] + z[:, half:]
        n *= 2
    return tuple(inv)


def _unit_inverses_fwd(ms, depth):
    inv = _unit_inverses(ms, depth)
    return inv, inv


def _unit_inverses_bwd(depth, inv, cts):
    left = [_tn(t, g, PS) for t, g in zip(inv, cts)]
    return (tuple(_nt(l, t, PS) for l, t in zip(left, inv)),)


_unit_inverses.defvjp(_unit_inverses_fwd, _unit_inverses_bwd)


@jax.custom_vjp
def _known_inverses(ms, inv):
    return inv


def _known_inverses_fwd(ms, inv):
    return inv, inv


def _known_inverses_bwd(inv, cts):
    dms = _unit_inverses_bwd(None, inv, cts)[0]
    return dms, tuple(jnp.zeros_like(t) for t in inv)


_known_inverses.defvjp(_known_inverses_fwd, _known_inverses_bwd)


def _scan_group(s0s, *flat, known_inv=None, with_inv=False):
    P = len(s0s)
    G = len(flat) // (6 * P)
    ch = [flat[6 * i:6 * i + 6] for i in range(P * G)]
    C = ch[0][0].shape[0]
    C2 = 2 * C
    cat = jnp.concatenate
    m0 = _iota2((1, LANE), 1) < RW_HD
    mask0 = m0.astype(F32)
    mask1 = 1.0 - mask0
    r2 = _iota2((C2, C2), 0)
    c2 = _iota2((C2, C2), 1)
    dist = r2 - c2
    in_head = dist <= r2 % C
    lower = (_iota2((C, C), 0) >= _iota2((C, C), 1)).astype(F32)
    bd = (_iota2((LANE, LANE), 0) // RW_HD) == (_iota2((LANE, LANE), 1) // RW_HD)

    def tri(m, strict):
        return jnp.where(dist > 0 if strict else dist >= 0, jnp.where(in_head, m, 0.0), 0.0)

    def sel(z):
        return jnp.where(m0, z[:C], z[C:])

    gs = [_nn(lower, c[1], HI) for c in ch]
    pre = []
    for (r, lw, k, v, a, b), g in zip(ch, gs):
        g_end = jnp.sum(lw, axis=0, keepdims=True)
        en = jnp.exp(-g)
        ec = jnp.exp(g_end - g)
        pre.append(dict(at=a * jnp.exp(g - lw), rt=r * jnp.exp(g), bt=b * en, kt=k * en, bh=b * ec, kh=k * ec,
                        dec=jnp.exp(g_end), v=v))
    grams = [_nt(cat([p["at"] * mask0, p["at"] * mask1, p["rt"] * mask0, p["rt"] * mask1], axis=0),
                 cat([p["bt"], p["bt"], p["kt"], p["kt"]], axis=0), PS) for p in pre]
    mab = tuple(tri(gm[:C2, :C2], True) for gm in grams)
    tinv = _unit_inverses(mab, C) if known_inv is None else _known_inverses(mab, known_inv)
    xv =[sel(_nn(tri(gm[:C2, C2:], True), cat([p["v"], p["v"]], axis=0), PS)) for gm, p in zip(grams, pre)]
    ys, s = [None] * (P * G), list(s0s)
    for i in range(G):
        for q in range(P):
            n = q * G + i
            p, gm = pre[n], grams[n]
            sx = _nt(cat([p["at"], p["rt"]], axis=0), s[q], PS)
            x = sx[:C] + xv[n]
            u = sel(_nn(tinv[n], cat([x, x], axis=0), PS))
            v = p["v"]
            ys[n] = sx[C:] + sel(_nn(cat([tri(gm[C2:, :C2], False), tri(gm[C2:, C2:], False)], axis=1),
                                     cat([u, u, v, v], axis=0), PS))
            s[q] = s[q] * p["dec"] + jnp.where(bd, _tn(cat([u, v], axis=0), cat([p["bh"], p["kh"]], axis=0), PS), 0.0)
    return (tuple(ys), tuple(s), tinv) if with_inv else (tuple(ys), tuple(s))


def _scan_fwd(xs, lw, kp, an, bb, T):
    C, G = _scan_shape(T)
    P = SCAN_PAIRS
    nc = T // (C * G)
    npair = 1024 // LANE

    def kern(r_ref, lw_ref, k_ref, v_ref, a_ref, b_ref, y_ref, st_ref, inv_ref, s_scr):
        n = pl.program_id(1)

        @pl.when(n == 0)
        def _():
            s_scr[...] = jnp.zeros(s_scr.shape, F32)

        st_ref[0] = s_scr[...]
        ins = (r_ref, lw_ref, k_ref, v_ref, a_ref, b_ref)
        ys, s1, inv = _scan_group(
            tuple(s_scr[q] for q in range(P)),
            *[ref[i * C:(i + 1) * C, q * LANE:(q + 1) * LANE] for q in range(P) for i in range(G) for ref in ins], with_inv=True)
        for q in range(P):
            for i in range(G):
                y_ref[i * C:(i + 1) * C, q * LANE:(q + 1) * LANE] = ys[q * G + i]
                inv_ref[0, 0, q * G + i] = inv[q * G + i]
            s_scr[q] = s1[q]

    def col(off):
        return pl.BlockSpec((C * G, P * LANE), lambda p, n: (n, off // P + p))

    return pl.pallas_call(
        kern, name="rwkv_scan_fwd",
        out_shape=[jax.ShapeDtypeStruct((T, 1024), F32), jax.ShapeDtypeStruct((nc, npair, LANE, LANE), F32),
                   jax.ShapeDtypeStruct((nc, npair // P, P * G, 2 * C, 2 * C), F32)],
        grid=(npair // P, nc),
        in_specs=[col(0), col(0), col(0), col(16), col(0), col(0)],
        out_specs=[col(0), pl.BlockSpec((1, P, LANE, LANE), lambda p, n: (n, p, 0, 0)),
                   pl.BlockSpec((1, 1, P * G, 2 * C, 2 * C), lambda p, n: (n, p, 0, 0, 0))],
        scratch_shapes=[pltpu.VMEM((P, LANE, LANE), F32)],
        compiler_params=_cp(("parallel", "arbitrary")),
    )(xs, lw, kp, xs, an, bb)


def _scan_bwd(xs, lw, kp, an, bb, states, invs, dy, T):
    C, G = _scan_shape(T)
    P = SCAN_PAIRS
    nc = T // (C * G)
    npair = 1024 // LANE

    def kern(r_ref, lw_ref, k_ref, v_ref, a_ref, b_ref, st_ref, inv_ref, dy_ref, dr_ref, dlw_ref, dk_ref, dv_ref, da_ref, db_ref,
             ds_scr):
        n = pl.program_id(1)

        @pl.when(n == 0)
        def _():
            ds_scr[...] = jnp.zeros(ds_scr.shape, F32)

        ins = (r_ref, lw_ref, k_ref, v_ref, a_ref, b_ref)
        units = [(q, i) for q in range(P) for i in range(G)]
        known = tuple(inv_ref[0, 0, q * G + i] for q, i in units)
        _, vjp = jax.vjp(functools.partial(_scan_group, known_inv=known), tuple(st_ref[0, q] for q in range(P)),
                         *[ref[i * C:(i + 1) * C, q * LANE:(q + 1) * LANE] for q, i in units for ref in ins])
        grads = vjp((tuple(dy_ref[i * C:(i + 1) * C, q * LANE:(q + 1) * LANE] for q, i in units),
                     tuple(ds_scr[q] for q in range(P))))
        for q in range(P):
            ds_scr[q] = grads[0][q]
        outs = (dr_ref, dlw_ref, dk_ref, dv_ref, da_ref, db_ref)
        for n_, (q, i) in enumerate(units):
            for t, ref in enumerate(outs):
                ref[i * C:(i + 1) * C, q * LANE:(q + 1) * LANE] = grads[1 + 6 * n_ + t]

    def col(off):
        return pl.BlockSpec((C * G, P * LANE), lambda p, n: (nc - 1 - n, off // P + p))

    return pl.pallas_call(
        kern, name="rwkv_scan_bwd",
        out_shape=[jax.ShapeDtypeStruct((T, 1024), F32)] * 6,
        grid=(npair // P, nc),
        in_specs=[col(0), col(0), col(0), col(16), col(0), col(0),
                  pl.BlockSpec((1, P, LANE, LANE), lambda p, n: (nc - 1 - n, p, 0, 0)),
                  pl.BlockSpec((1, 1, P * G, 2 * C, 2 * C), lambda p, n: (nc - 1 - n, p, 0, 0, 0)), col(0)],
        out_specs=[col(0)] * 6,
        scratch_shapes=[pltpu.VMEM((P, LANE, LANE), F32)],
        compiler_params=_cp(("parallel", "arbitrary")),
    )(xs, lw, kp, xs, an, bb, states, invs, dy)


def _gates_fwd(u, bias_pad, T, f_cb):
    nb = T // LANE

    def kern(f_ref, b_ref, c_ref):
        x = f_ref[...] + b_ref[...]
        lf = jnp.minimum(x, 0.0) - jnp.log(1.0 + jnp.exp(-jnp.abs(x)))
        lft = lf.T
        ut = (_iota2((LANE, LANE), 0) <= _iota2((LANE, LANE), 1)).astype(F32)
        carry = jnp.zeros((LANE, 1), F32)
        for blk in range(nb):
            seg = lft[:, blk * LANE:(blk + 1) * LANE]
            cs = _nn(seg, ut, HI) + carry
            c_ref[:, blk * LANE:(blk + 1) * LANE] = cs[:SUB, :]
            carry = carry + jnp.sum(seg, axis=1, keepdims=True)

    return pl.pallas_call(
        kern, name="fox_gates_fwd",
        out_shape=jax.ShapeDtypeStruct((SUB, T), F32),
        grid=(1,),
        in_specs=[pl.BlockSpec((T, LANE), lambda i: (0, f_cb)), pl.BlockSpec((1, LANE), lambda i: (0, 0))],
        out_specs=pl.BlockSpec((SUB, T), lambda i: (0, 0)),
        compiler_params=_cp(("arbitrary",)),
    )(u, bias_pad)


def _gates_bwd(dc, u, bias_pad, T, f_cb):
    nb = T // LANE

    def kern(dc_ref, f_ref, b_ref, dfl_ref, db_ref):
        dcv = jnp.concatenate([dc_ref[...], jnp.zeros((LANE - SUB, T), F32)], axis=0)
        lt = (_iota2((LANE, LANE), 0) >= _iota2((LANE, LANE), 1)).astype(F32)
        carry = jnp.zeros((LANE, 1), F32)
        pieces = [None] * nb
        for blk in range(nb - 1, -1, -1):
            seg = dcv[:, blk * LANE:(blk + 1) * LANE]
            pieces[blk] = _nn(seg, lt, HI) + carry
            carry = carry + jnp.sum(seg, axis=1, keepdims=True)
        dlf = (pieces[0] if nb == 1 else jnp.concatenate(pieces, axis=1)).T
        x = f_ref[...] + b_ref[...]
        dfl = dlf * _sigmoid(-x)
        dfl_ref[...] = dfl
        db_ref[...] = jnp.sum(dfl, axis=0, keepdims=True)

    return pl.pallas_call(
        kern, name="fox_gates_bwd",
        out_shape=[jax.ShapeDtypeStruct((T, LANE), F32), jax.ShapeDtypeStruct((1, LANE), F32)],
        grid=(1,),
        in_specs=[pl.BlockSpec((SUB, T), lambda i: (0, 0)), pl.BlockSpec((T, LANE), lambda i: (0, f_cb)),
                  pl.BlockSpec((1, LANE), lambda i: (0, 0))],
        out_specs=[pl.BlockSpec((T, LANE), lambda i: (0, 0)), pl.BlockSpec((1, LANE), lambda i: (0, 0))],
        compiler_params=_cp(("arbitrary",)),
    )(dc, u, bias_pad)


ATTN_HEADS = 4


def _attn_block(T):
    return 256 if T % 256 == 0 and T >= 512 else 128


def _attn_fwd(u, c3, T):
    H, HP = 8, ATTN_HEADS
    bq = _attn_block(T)
    nq = T // bq
    scale = FOX_HD ** -0.5
    lanes = [slice(h * LANE, (h + 1) * LANE) for h in range(HP)]

    def kern(q_ref, k_ref, v_ref, z_ref, cq_ref, ck_ref, o_ref, oa_ref, lse_ref):
        i = pl.program_id(1)
        q = [(q_ref[:, ln] * scale).astype(BF16) for ln in lanes]
        c0 = [cq_ref[h][:, 0:1] for h in range(HP)]

        def step(j, carry, diagonal=False):
            off = pl.multiple_of(j * bq, bq)
            s = [_nt(q[h], k_ref[pl.ds(off, bq), lanes[h]].astype(BF16)) + (c0[h] - ck_ref[h, :, pl.ds(off, bq)])
                 for h in range(HP)]
            ps, out = [], []
            for h in range(HP):
                m, l, acc = carry[h]
                sh = s[h]
                if diagonal:
                    sh = jnp.where(_iota2((bq, bq), 1) <= _iota2((bq, bq), 0), sh, NEG)
                m_new = jnp.maximum(m, jnp.max(sh, axis=1, keepdims=True))
                p = jnp.exp(sh - m_new)
                alpha = jnp.exp(m - m_new)
                ps.append(p.astype(BF16))
                out.append((m_new, alpha * l + jnp.sum(p, axis=1, keepdims=True), alpha * acc))
            return tuple((m, l, acc + _nn(ps[h], v_ref[pl.ds(off, bq), lanes[h]].astype(BF16)))
                         for h, (m, l, acc) in enumerate(out))

        init = tuple((jnp.full((bq, 1), NEG, F32), jnp.zeros((bq, 1), F32), jnp.zeros((bq, FOX_HD), F32)) for _ in range(HP))
        res = step(i, lax.fori_loop(0, i, step, init), diagonal=True)
        for h, (m, l, acc) in enumerate(res):
            o = acc / l
            z = z_ref[:, lanes[h]]
            o_ref[:, lanes[h]] = o
            oa_ref[:, lanes[h]] = (o * z * _sigmoid(z)).astype(BF16)
            lse_ref[h] = m + jnp.log(l)

    W = HP * LANE
    return pl.pallas_call(
        kern, name="fox_attn_fwd",
        out_shape=[jax.ShapeDtypeStruct((T, 1024), F32), jax.ShapeDtypeStruct((T, 1024), BF16),
                   jax.ShapeDtypeStruct((H, T, 1), F32)],
        grid=(H // HP, nq),
        in_specs=[pl.BlockSpec((bq, W), lambda g, i: (i, g)),
                  pl.BlockSpec((T, W), lambda g, i: (0, 8 // HP + g)),
                  pl.BlockSpec((T, W), lambda g, i: (0, 16 // HP + g)),
                  pl.BlockSpec((bq, W), lambda g, i: (i, 24 // HP + g)),
                  pl.BlockSpec((HP, 1, bq), lambda g, i: (g, 0, i)),
                  pl.BlockSpec((HP, 1, T), lambda g, i: (g, 0, 0))],
        out_specs=[pl.BlockSpec((bq, W), lambda g, i: (i, g)),
                   pl.BlockSpec((bq, W), lambda g, i: (i, g)),
                   pl.BlockSpec((HP, bq, 1), lambda g, i: (g, i, 0))],
        compiler_params=_cp(("parallel", "arbitrary")),
    )(u, u, u, u, c3, c3)


def _attn_probs(s, lse_i, diagonal):
    if not diagonal:
        return jnp.exp(s - lse_i)
    keep = _iota2(s.shape, 1) <= _iota2(s.shape, 0)
    return jnp.where(keep, jnp.exp(jnp.where(keep, s, NEG) - lse_i), 0.0)


def _attn_delta(u, c3, o, lse, doa, T):
    H, HP = 8, ATTN_HEADS
    bq = _attn_block(T)
    nq = T // bq
    scale = FOX_HD ** -0.5
    lanes = [slice(h * LANE, (h + 1) * LANE) for h in range(HP)]

    def kern(q_ref, k_ref, v_ref, z_ref, c_ref, o_ref, lse_ref, doa_ref, do_ref, dz_ref, dl_ref):
        i = pl.program_id(1)
        z = z_ref[...]
        sg = _sigmoid(z)
        dov = doa_ref[...]
        do_all = (dov * z * sg).astype(BF16)
        do_ref[...] = do_all
        dz_ref[...] = dov * o_ref[...] * (sg * (1.0 + z * (1.0 - sg)))
        qs = [(q_ref[:, ln] * scale).astype(BF16) for ln in lanes]
        dob = [do_all[:, ln] for ln in lanes]
        ioff = pl.multiple_of(i * bq, bq)
        c0 = [c_ref[h, :, pl.ds(ioff, bq)][:, 0:1] for h in range(HP)]
        lse_i = [lse_ref[h] for h in range(HP)]

        def step(j, acc, diagonal=False):
            off = pl.multiple_of(j * bq, bq)
            s = [_nt(qs[h], k_ref[pl.ds(off, bq), lanes[h]].astype(BF16)) + (c0[h] - c_ref[h, :, pl.ds(off, bq)])
                 for h in range(HP)]
            dp = [_nt(dob[h], v_ref[pl.ds(off, bq), lanes[h]].astype(BF16)) for h in range(HP)]
            return tuple(acc[h] + jnp.sum(_attn_probs(s[h], lse_i[h], diagonal) * dp[h], axis=1, keepdims=True)
                         for h in range(HP))

        init = tuple(jnp.zeros((bq, 1), F32) for _ in range(HP))
        res = step(i, lax.fori_loop(0, i, step, init), diagonal=True)
        for h in range(HP):
            dl_ref[h] = res[h]

    W = HP * LANE
    full = lambda cb: pl.BlockSpec((T, W), lambda g, i: (0, cb // HP + g))
    blk = lambda cb: pl.BlockSpec((bq, W), lambda g, i: (i, cb // HP + g))
    return pl.pallas_call(
        kern, name="fox_attn_delta",
        out_shape=[jax.ShapeDtypeStruct((T, 1024), BF16), jax.ShapeDtypeStruct((T, 1024), F32), jax.ShapeDtypeStruct((H, T, 1), F32)],
        grid=(H // HP, nq),
        in_specs=[blk(0), full(8), full(16), blk(24),
                  pl.BlockSpec((HP, 1, T), lambda g, i: (g, 0, 0)),
                  blk(0),
                  pl.BlockSpec((HP, bq, 1), lambda g, i: (g, i, 0)),
                  blk(0)],
        out_specs=[blk(0), blk(0), pl.BlockSpec((HP, bq, 1), lambda g, i: (g, i, 0))],
        compiler_params=_cp(("parallel", "arbitrary")),
    )(u, u, u, u, c3, o, lse, doa)


def _attn_bwd(u, c3, lse, do, delta, T):
    H, HP = 8, ATTN_HEADS
    bq = _attn_block(T)
    nq = T // bq
    scale = FOX_HD ** -0.5
    lanes = [slice(h * LANE, (h + 1) * LANE) for h in range(HP)]

    def kern(q_ref, k_ref, v_ref, c_ref, lse_ref, do_ref, dl_ref, dq_ref, dk_ref, dv_ref, dc_ref):
        j = pl.program_id(1)

        @pl.when(j == 0)
        def _():
            dq_ref[...] = jnp.zeros(dq_ref.shape, F32)

        kj = [k_ref[:, ln].astype(BF16) for ln in lanes]
        vj = [v_ref[:, ln].astype(BF16) for ln in lanes]
        joff = pl.multiple_of(j * bq, bq)
        ck = [c_ref[h, :, pl.ds(joff, bq)] for h in range(HP)]

        def step(i, carry, diagonal=False):
            off = pl.multiple_of(i * bq, bq)
            qs = [(q_ref[pl.ds(off, bq), ln] * scale).astype(BF16) for ln in lanes]
            dob = [do_ref[pl.ds(off, bq), ln] for ln in lanes]
            s = [_nt(qs[h], kj[h]) + (c_ref[h, :, pl.ds(off, bq)][:, 0:1] - ck[h]) for h in range(HP)]
            dp = [_nt(dob[h], vj[h]) for h in range(HP)]
            pb, dsb, dcs = [], [], []
            for h in range(HP):
                p = _attn_probs(s[h], lse_ref[h, pl.ds(off, bq), :], diagonal)
                ds = p * (dp[h] - dl_ref[h, pl.ds(off, bq), :])
                pb.append(p.astype(BF16))
                dsb.append(ds.astype(BF16))
                dcs.append(jnp.sum(ds, axis=0, keepdims=True))
            out = []
            for h, (dk, dv, dc) in enumerate(carry):
                dq_ref[pl.ds(off, bq), lanes[h]] += _nn(dsb[h], kj[h]) * scale
                out.append((dk + _tn(dsb[h], qs[h]), dv + _tn(pb[h], dob[h]), dc - dcs[h]))
            return tuple(out)

        init = tuple((jnp.zeros((bq, FOX_HD), F32), jnp.zeros((bq, FOX_HD), F32), jnp.zeros((1, bq), F32)) for _ in range(HP))
        res = lax.fori_loop(j + 1, nq, step, step(j, init, diagonal=True))
        for h, (dk, dv, dc) in enumerate(res):
            dk_ref[:, lanes[h]] = dk
            dv_ref[:, lanes[h]] = dv
            dc_ref[h] = dc

    W = HP * LANE
    full = lambda cb: pl.BlockSpec((T, W), lambda g, j: (0, cb // HP + g))
    blk = lambda cb: pl.BlockSpec((bq, W), lambda g, j: (j, cb // HP + g))
    col1 = pl.BlockSpec((HP, T, 1), lambda g, j: (g, 0, 0))
    return pl.pallas_call(
        kern, name="fox_attn_bwd",
        out_shape=[jax.ShapeDtypeStruct((T, 1024), F32)] * 3 + [jax.ShapeDtypeStruct((H, 1, T), F32)],
        grid=(H // HP, nq),
        in_specs=[full(0), blk(8), blk(16), pl.BlockSpec((HP, 1, T), lambda g, j: (g, 0, 0)), col1, full(0), col1],
        out_specs=[full(0), blk(0), blk(0), pl.BlockSpec((HP, 1, bq), lambda g, j: (g, 0, j))],
        compiler_params=_cp(("parallel", "arbitrary")),
    )(u, u, u, c3, lse, do, delta)


def _place():
    return lax.axis_index("x"), lax.axis_index("y"), lax.axis_index("c")


def _slot(p):
    return 4 * p[0] + 2 * p[1] + p[2]


def _other_chips(x, y):
    return [(1 - x, y), (x, 1 - y), (1 - x, 1 - y)]


def _allgather_steps(in_refs, out_refs, scratch):
    (src,), (dst,) = in_refs, out_refs
    send_sems, recv_sems, local_sem = scratch
    x, y, c = _place()
    me, sibling = (x, y, c), (x, y, 1 - c)
    chips = _other_chips(x, y)

    def copy(k, block, to, from_input=False):
        d = dst.at[_slot(block)]
        return pltpu.make_async_remote_copy(
            src_ref=src if from_input else d, dst_ref=d, send_sem=send_sems.at[k], recv_sem=recv_sems.at[k],
            device_id=to, device_id_type=MESH)

    def first_copies():
        return [copy(0, me, sibling, True)] + [copy(1 + j, me, (*chip, c), True) for j, chip in enumerate(chips)]

    def start():
        pltpu.make_async_copy(src, dst.at[_slot(me)], local_sem).start()
        for cp in first_copies():
            cp.start()

    def finish():
        passed = []
        for j, chip in enumerate(chips):
            copy(1 + j, (*chip, c), me).wait_recv()
            passed.append(copy(4 + j, (*chip, c), sibling))
            passed[-1].start()
        copy(0, sibling, me).wait_recv()
        for j, chip in enumerate(chips):
            copy(4 + j, (*chip, 1 - c), me).wait_recv()
        for cp in first_copies() + passed:
            cp.wait_send()
        pltpu.make_async_copy(src, dst.at[_slot(me)], local_sem).wait()

    return start, finish


def _allgather_relay_steps(in_refs, out_refs, scratch):
    (src,), (dst,) = in_refs, out_refs
    send_sems, recv_sems, local_sem = scratch
    x, y, c = _place()
    me, sibling = (x, y, c), (x, y, 1 - c)
    x_nbr, y_nbr, diag = (1 - x, y, c), (x, 1 - y, c), (1 - x, 1 - y, c)
    flip = lambda a, bit: a + bit - 2 * a * bit
    relay_from = (flip(x, 1 - c), flip(y, c), c)
    relay_to = (flip(x, c), flip(y, 1 - c), c)

    def copy(k, block, to, from_input=False):
        d = dst.at[_slot(block)]
        return pltpu.make_async_remote_copy(
            src_ref=src if from_input else d, dst_ref=d, send_sem=send_sems.at[k], recv_sem=recv_sems.at[k],
            device_id=to, device_id_type=MESH)

    def first_copies():
        return [copy(0, me, sibling, True), copy(1, me, x_nbr, True), copy(2, me, y_nbr, True)]

    def other(block):
        return block[:2] + (1 - c,)

    def start():
        pltpu.make_async_copy(src, dst.at[_slot(me)], local_sem).start()
        for cp in first_copies():
            cp.start()

    def finish():
        copy(1, x_nbr, me).wait_recv()
        copy(2, y_nbr, me).wait_recv()
        later = [copy(3, relay_from, relay_to), copy(4, x_nbr, sibling), copy(5, y_nbr, sibling)]
        for cp in later:
            cp.start()
        copy(3, diag, me).wait_recv()
        later.append(copy(6, diag, sibling))
        later[-1].start()
        copy(0, sibling, me).wait_recv()
        for k, block in ((4, x_nbr), (5, y_nbr), (6, diag)):
            copy(k, other(block), me).wait_recv()
        for cp in first_copies() + later:
            cp.wait_send()
        pltpu.make_async_copy(src, dst.at[_slot(me)], local_sem).wait()

    return start, finish


def _allgather_side(a, relay=False):
    return ((a,), (jax.ShapeDtypeStruct((N_DEV,) + a.shape, a.dtype),),
            (pltpu.SemaphoreType.DMA((7,)), pltpu.SemaphoreType.DMA((7,)), pltpu.SemaphoreType.DMA),
            _allgather_relay_steps if relay else _allgather_steps)


def _allgather(a, name, relay=False):
    ins, outs, scratch, make = _allgather_side(a, relay)

    def body(a_ref, o_ref, *scr):
        start, finish = make((a_ref,), (o_ref,), scr)
        start()
        finish()

    any_spec = pl.BlockSpec(memory_space=pl.ANY)
    return pl.pallas_call(body, name=name, out_shape=outs[0], in_specs=[any_spec], out_specs=any_spec,
                          scratch_shapes=list(scratch))(a)


def _exchange_pair(ga, gb):
    def body(ga_ref, gb_ref, ra_ref, rb_ref, send_sems, recv_sems):
        x, y, c = _place()
        sibling = (x, y, 1 - c)
        slots = [_slot(sibling)] + [_slot((*chip, 1 - c)) for chip in _other_chips(x, y)]
        cps = []
        for t, (src, dst) in enumerate(((ga_ref, ra_ref), (gb_ref, rb_ref))):
            for k, ps in enumerate(slots):
                cps.append(pltpu.make_async_remote_copy(
                    src_ref=src.at[ps], dst_ref=dst.at[k], send_sem=send_sems.at[t, k], recv_sem=recv_sems.at[t, k],
                    device_id=sibling, device_id_type=MESH))
        for cp in cps:
            cp.start()
        for cp in cps:
            cp.wait()

    any_spec = pl.BlockSpec(memory_space=pl.ANY)
    return pl.pallas_call(
        body, name="exchange_pair",
        out_shape=[jax.ShapeDtypeStruct((4,) + ga.shape[1:], ga.dtype), jax.ShapeDtypeStruct((4,) + gb.shape[1:], gb.dtype)],
        in_specs=[any_spec] * 2,
        out_specs=[any_spec] * 2,
        scratch_shapes=[pltpu.SemaphoreType.DMA((2, 4)), pltpu.SemaphoreType.DMA((2, 4))],
    )(ga, gb)


def _tiling(R, Cc, tile, by_cols):
    if by_cols:
        assert Cc % tile == 0
        return Cc // tile, (R, tile), lambda lead, i: (lead, 0, i)
    assert R % tile == 0
    return R // tile, (tile, Cc), lambda lead, i: (lead, i, 0)


def _pair_add(name, g, r1, slots, tile, by_cols=False):
    _, R, Cc = g.shape
    steps, blk, at = _tiling(R, Cc, tile, by_cols)

    def kern(s_ref, a_ref, b_ref, o_ref):
        o_ref[...] = (a_ref[...].astype(F32) + b_ref[...].astype(F32)).astype(o_ref.dtype)

    return pl.pallas_call(
        kern, name=name,
        out_shape=jax.ShapeDtypeStruct((3, R, Cc), BF16),
        grid_spec=pltpu.PrefetchScalarGridSpec(
            num_scalar_prefetch=1, grid=(3, steps),
            in_specs=[pl.BlockSpec((1,) + blk, lambda j, i, s: at(s[j], i)),
                      pl.BlockSpec((1,) + blk, lambda j, i, s: at(1 + j, i))],
            out_specs=pl.BlockSpec((1,) + blk, lambda j, i, s: at(j, i))),
        compiler_params=_cp(("arbitrary", "arbitrary")),
    )(slots, g, r1)


def _axis_neighbours():
    x, y, c = _place()
    flip = lambda a, bit: a + bit - 2 * a * bit
    return (flip(x, c), flip(y, 1 - c), c), (flip(x, 1 - c), flip(y, c), c), c


def _exchange_first_steps(in_refs, out_refs, scratch):
    pairs = list(zip(in_refs, out_refs))
    send_sems, recv_sems = scratch
    first, _, c = _axis_neighbours()

    def copies():
        return [pltpu.make_async_remote_copy(
            src_ref=src.at[j], dst_ref=dst.at[k], send_sem=send_sems.at[t, k], recv_sem=recv_sems.at[t, k],
            device_id=first, device_id_type=MESH)
            for t, (src, dst) in enumerate(pairs) for k, j in enumerate((1 - c, 2))]

    def start():
        for cp in copies():
            cp.start()

    def finish():
        for cp in copies():
            cp.wait()

    return start, finish


def _exchange_first_side(sa, sb):
    return ((sa, sb), (jax.ShapeDtypeStruct((2,) + sa.shape[1:], sa.dtype), jax.ShapeDtypeStruct((2,) + sb.shape[1:], sb.dtype)),
            (pltpu.SemaphoreType.DMA((2, 2)), pltpu.SemaphoreType.DMA((2, 2))), _exchange_first_steps)


def _axis_add(name, s, t1, core, tile, by_cols=False):
    _, R, Cc = s.shape
    steps, blk, at = _tiling(R, Cc, tile, by_cols)

    def kern(c_ref, a_ref, b_ref, o_ref):
        o_ref[...] = (a_ref[...].astype(F32) + b_ref[...].astype(F32)).astype(o_ref.dtype)

    return pl.pallas_call(
        kern, name=name,
        out_shape=jax.ShapeDtypeStruct((1, R, Cc), BF16),
        grid_spec=pltpu.PrefetchScalarGridSpec(
            num_scalar_prefetch=1, grid=(steps,),
            in_specs=[pl.BlockSpec((1,) + blk, lambda i, cr: at(cr[0], i)),
                      pl.BlockSpec((1,) + blk, lambda i, cr: at(1, i))],
            out_specs=pl.BlockSpec((1,) + blk, lambda i, cr: at(0, i))),
        compiler_params=_cp(("arbitrary",)),
    )(core, s, t1)


def _exchange_second(pa, pb):
    def body(pa_ref, pb_ref, ta_ref, tb_ref, send_sems, recv_sems):
        _, second, _ = _axis_neighbours()
        cps = [pltpu.make_async_remote_copy(src_ref=src, dst_ref=dst, send_sem=send_sems.at[t], recv_sem=recv_sems.at[t],
                                            device_id=second, device_id_type=MESH)
               for t, (src, dst) in enumerate(((pa_ref, ta_ref), (pb_ref, tb_ref)))]
        for cp in cps:
            cp.start()
        for cp in cps:
            cp.wait()

    any_spec = pl.BlockSpec(memory_space=pl.ANY)
    return pl.pallas_call(
        body, name="exchange_second",
        out_shape=[jax.ShapeDtypeStruct(pa.shape, pa.dtype), jax.ShapeDtypeStruct(pb.shape, pb.dtype)],
        in_specs=[any_spec] * 2, out_specs=[any_spec] * 2,
        scratch_shapes=[pltpu.SemaphoreType.DMA((2,)), pltpu.SemaphoreType.DMA((2,))],
    )(pa, pb)


def _adamw(name, w, m, v, parts, tile, by_cols=False):
    R, Cc = w.shape
    steps, blk_shape, at = _tiling(R, Cc, tile, by_cols)
    n_parts = len(parts)

    def kern(*refs):
        w_ref, m_ref, v_ref = refs[:3]
        g = None
        for r_ref, (_, n) in zip(refs[3:3 + n_parts], parts):
            for s in range(n):
                term = r_ref[s].astype(F32)
                g = term if g is None else g + term
        g_out, d_out, m_out, v_out = refs[3 + n_parts:]
        mn = ADAM_B1 * m_ref[...] + (1.0 - ADAM_B1) * g
        vn = ADAM_B2 * v_ref[...] + (1.0 - ADAM_B2) * (g * g)
        m_hat = mn / (1.0 - ADAM_B1 ** ADAM_STEP)
        v_hat = vn / (1.0 - ADAM_B2 ** ADAM_STEP)
        g_out[...] = g
        d_out[...] = -ADAM_LR * (m_hat / (jnp.sqrt(v_hat) + ADAM_EPS) + ADAM_WD * w_ref[...])
        m_out[...] = mn
        v_out[...] = vn

    blk = pl.BlockSpec(blk_shape, lambda i: at(0, i)[1:])
    return pl.pallas_call(
        kern, name=name,
        out_shape=[jax.ShapeDtypeStruct((R, Cc), F32)] * 4,
        grid=(steps,),
        in_specs=[blk] * 3 + [pl.BlockSpec((n,) + blk_shape, lambda i: at(0, i)) for (_, n) in parts],
        out_specs=[blk] * 4,
        compiler_params=_cp(("arbitrary",)),
    )(w, m, v, *[a for (a, _) in parts])


def _assemble_columns(blocks, pieces, zeros, width):
    _, R, Cc = blocks.shape
    tr = min(256, R)

    def kern(b_ref, o_ref):
        for col, n in zeros:
            o_ref[:, col:col + n] = jnp.zeros((tr, n), o_ref.dtype)
        for col, d, lo, n in pieces:
            o_ref[:, col:col + n] = b_ref[d, :, lo:lo + n]

    return pl.pallas_call(
        kern, name="assemble_w_in",
        out_shape=jax.ShapeDtypeStruct((R, width), blocks.dtype),
        grid=(R // tr,),
        in_specs=[pl.BlockSpec((N_DEV, tr, Cc), lambda i: (0, i, 0))],
        out_specs=pl.BlockSpec((tr, width), lambda i: (i, 0)),
        compiler_params=_cp(("parallel",)),
    )(blocks)


def _split_rows(x, pieces, rows):
    _, Cc = x.shape
    tc = min(256, Cc)

    def kern(x_ref, o_ref):
        for d, lo, row, n in pieces:
            o_ref[d, lo:lo + n, :] = x_ref[row:row + n, :]

    return pl.pallas_call(
        kern, name="split_w_in_grad",
        out_shape=jax.ShapeDtypeStruct((N_DEV, rows, Cc), x.dtype),
        grid=(Cc // tc,),
        in_specs=[pl.BlockSpec((x.shape[0], tc), lambda i: (0, i))],
        out_specs=pl.BlockSpec((N_DEV, rows, tc), lambda i: (0, 0, i)),
        compiler_params=_cp(("parallel",)),
    )(x)


def _pad_cols(a, w):
    return jnp.pad(a, ((0, 0), (0, w - a.shape[1])))


def _pad_rows(a, r):
    return jnp.pad(a, ((0, r - a.shape[0]), (0, 0)))


def _pack_b(pf, pr, wo, w2, a2, rows):
    body = jnp.concatenate([pf, pr, wo.reshape(2048, 256), jnp.concatenate([w2, a2], axis=1)], axis=0)
    return _pad_rows(body, rows)


def kernel(x, norm_gain, w_in, fox_forget_bias, rwkv_shift_mix, rwkv_w0, rwkv_w2, rwkv_a0, rwkv_a2, rwkv_k_k, rwkv_k_a, rwkv_r_k, rwkv_ln_w, rwkv_ln_b, w_proj_fox, w_proj_rwkv, w_out, final_norm_gain, loss_target, m_norm_gain, m_w_in, m_fox_forget_bias, m_rwkv_shift_mix, m_rwkv_w0, m_rwkv_w2, m_rwkv_a0, m_rwkv_a2, m_rwkv_k_k, m_rwkv_k_a, m_rwkv_r_k, m_rwkv_ln_w, m_rwkv_ln_b, m_w_proj_fox, m_w_proj_rwkv, m_w_out, m_final_norm_gain, v_norm_gain, v_w_in, v_fox_forget_bias, v_rwkv_shift_mix, v_rwkv_w0, v_rwkv_w2, v_rwkv_a0, v_rwkv_a2, v_rwkv_k_k, v_rwkv_k_a, v_rwkv_r_k, v_rwkv_ln_w, v_rwkv_ln_b, v_w_proj_fox, v_w_proj_rwkv, v_w_out, v_final_norm_gain):
    T, D = x.shape[1], x.shape[2]
    assert D == 2048 and T % LANE == 0
    NI = w_in.shape[2]
    IN = N_DEV * NI
    RB = 4224
    x2 = x[0]
    lt2 = loss_target[0]
    me = _slot(_place())

    wa = _allgather(w_in[0].astype(BF16), "allgather_w_in", relay=True)
    packed_own = _pack_b(w_proj_fox[0], w_proj_rwkv[0], w_out[0], rwkv_w2[0], rwkv_a2[0], RB).astype(BF16)
    sections = [(0, 4096, 0), (4104, 4096, 4096), (8392, 4096, 8192), (4096, 8, 12288), (8200, 96, 12544), (8296, 96, 12672)]
    NP = 12800
    pieces, zeros, at_col = [], [], 0
    for lo, width, pad_lo in sections:
        if pad_lo > at_col:
            zeros.append((at_col, pad_lo - at_col))
        col = lo
        while col < lo + width:
            d = col // NI
            stop = min(lo + width, (d + 1) * NI)
            pieces.append((pad_lo + col - lo, d, col - d * NI, stop - col))
            col = stop
        at_col = pad_lo + width
    zeros.append((at_col, NP - at_col))
    w_pad = _assemble_columns(wa, pieces, zeros, NP)
    F_CB, LORA_CB = 96, 49

    mu = rwkv_shift_mix
    mu_main = mu[:, 0:4096]
    mu_lora = jnp.concatenate([_pad_cols(mu[:, 4096:4192], LANE), _pad_cols(mu[:, 4192:4288], LANE)], axis=1)
    bias_pad = _pad_cols(fox_forget_bias, LANE)
    rk_flat = rwkv_r_k.reshape(1, 1024)
    gf = final_norm_gain.reshape(1, D)

    tb = min(256, T)
    tbh = min(128, T)
    (h,) = _rowcall("rms_fwd", T, tb, [_rows(x2, tb, D), _whole(norm_gain)],
                    lambda i, xv, g: ([_rms_math(xv, g)], []), [(D, BF16)])
    u, wb = _mm(h, w_pad, tm=1024, tn=1280, name="mm_in", side=_allgather_side(packed_own))
    wpf = wb[:, 0:1024, :].transpose(1, 0, 2).reshape(1024, D)
    wpr = wb[:, 1024:2048, :].transpose(1, 0, 2).reshape(1024, D)
    wo = wb[:, 2048:4096, :].reshape(N_DEV * 256, D)
    w2p = _pad_rows(wb[:, 4096:4192, 0:128].transpose(1, 0, 2).reshape(96, 1024).astype(F32), LANE)
    a2p = _pad_rows(wb[:, 4096:4192, 128:256].transpose(1, 0, 2).reshape(96, 1024).astype(F32), LANE)

    c8 = _gates_fwd(u, bias_pad, T, F_CB)
    c3 = c8.reshape(8, 1, T)
    o_raw, o_a, lse = _attn_fwd(u, c3, T)

    def shift_body(i, um, hm, ul, hl, mm_, ml):
        outs = []
        for uv, hv, mv in ((um, hm, mm_), (ul, hl, ml)):
            hv = jnp.where(i == 0, 0.0, hv)
            prev = pltpu.roll(jnp.concatenate([hv, uv], axis=0), 1, 0)[SUB:]
            outs.append(uv + (prev - uv) * mv)
        return outs, []

    def halo_prev(arr, w, cb):
        return (arr, (SUB, w), lambda i: (jnp.maximum(i * (tbh // SUB) - 1, 0), cb))

    xs, xl = _rowcall("rwkv_shift_fwd", T, tbh,
                      [_rows(u, tbh, 4096, 1), halo_prev(u, 4096, 1), _rows(u, tbh, 256, LORA_CB), halo_prev(u, 256, LORA_CB),
                       _whole(mu_main), _whole(mu_lora)],
                      shift_body, [(4096, F32), (256, F32)])

    prep_par = [_whole(rwkv_w0), _whole(rwkv_a0), _whole(rwkv_k_k), _whole(rwkv_k_a), _whole(w2p), _whole(a2p)]
    prep_rows = [_rows(xs, tbh, 1024, 1), _rows(xl, tbh, LANE, 0), _rows(xl, tbh, LANE, 1)]
    lw, kp, an, bb = _rowcall("rwkv_prep_fwd", T, tbh, prep_rows + prep_par,
                              lambda i, *a: (list(_prep_math(*a)), []), [(1024, F32)] * 4)
    y, states, invs = _scan_fwd(xs, lw, kp, an, bb, T)
    post_rows = [_rows(y, tbh, 1024), _rows(xs, tbh, 1024, 0), _rows(kp, tbh, 1024), _rows(xs, tbh, 1024, 2), _rows(xs, tbh, 1024, 3)]
    post_par = [_whole(rwkv_ln_w), _whole(rwkv_ln_b), _whole(rk_flat)]
    (o_b,) = _rowcall("rwkv_post_fwd", T, tbh, post_rows + post_par,
                      lambda i, *a: ([_post_math(*a)], []), [(1024, BF16)])

    pa = _mm(o_a, wpf, name="mm_proj_fox")
    pb = _mm(o_b, wpr, name="mm_proj_rwkv")
    merge_rows = [_rows(u, tb, D, 4), _rows(u, tb, D, 5), _rows(pa, tb, D), _rows(pb, tb, D)]
    (mg,) = _rowcall("merge_fwd", T, tb, merge_rows, lambda i, *a: ([_merge_math(*a)], []), [(D, BF16)])
    mo = _mm(mg, wo, name="mm_out")

    def head_body(i, xv, mov, ltv, g):
        out = xv + mov
        r = lax.rsqrt(jnp.mean(out * out, axis=-1, keepdims=True) + RMS_EPS)
        yn = out * r
        err = yn * g - ltv
        loss = 0.5 * jnp.sum(jnp.sum(err * err, axis=-1, keepdims=True), axis=0, keepdims=True) / D
        dyv = err / D
        dyn = dyv * g
        dout = r * (dyn - yn * jnp.mean(dyn * yn, axis=-1, keepdims=True))
        return [dout], [loss, jnp.sum(dyv * yn, axis=0, keepdims=True)]

    dout, loss_p, dgf_p = _rowcall("loss_head", T, tb, [_rows(x2, tb, D), _rows(mo, tb, D), _rows(lt2, tb, D), _whole(gf)],
                                   head_body, [(D, F32)], [(1, 1), (1, D)])

    dm = _mm(dout, wo, tb=True, name="mm_out_dx")
    dwo = _mm(mg, dout, ta=True, out_dtype=BF16, name="mm_out_dw")

    def merge_bwd_body(i, ga, gb, pav, pbv, dmv):
        _, vjp = jax.vjp(_merge_math, ga, gb, pav, pbv)
        dga, dgb, dpa, dpb = vjp(dmv)
        return [dga, dgb, dpa, dpb], []

    dga, dgb, dpa, dpb = _rowcall("merge_bwd", T, tb, merge_rows + [_rows(dm, tb, D)], merge_bwd_body,
                                  [(D, BF16), (D, BF16), (D, BF16), (D, BF16)])
    doa = _mm(dpa, wpf, tb=True, name="mm_proj_fox_dx")
    dwpf = _mm(o_a, dpa, ta=True, out_dtype=BF16, name="mm_proj_fox_dw")
    dob = _mm(dpb, wpr, tb=True, name="mm_proj_rwkv_dx")
    dwpr = _mm(o_b, dpb, ta=True, out_dtype=BF16, name="mm_proj_rwkv_dw")

    do_b, dza, delta = _attn_delta(u, c3, o_raw, lse, doa, T)
    dq, dk, dv, dc3 = _attn_bwd(u, c3, lse, do_b, delta, T)
    dfl, dbias_p = _gates_bwd(dc3.reshape(8, T), u, bias_pad, T, F_CB)

    def post_bwd_body(i, yv, rv, kpv, vv, zv, lnw, lnb, rkv, dobv):
        _, vjp = jax.vjp(_post_math, yv, rv, kpv, vv, zv, lnw, lnb, rkv)
        dy_, dr_, dkp_, dv_, dz_, dlnw, dlnb, drk = vjp(dobv)
        return [dy_, dr_, dkp_, dv_, dz_], [dlnw, dlnb, drk]

    dy_s, dr_p, dkp_p, dv_p, dzb, dlnw_p, dlnb_p, drk_p = _rowcall(
        "rwkv_post_bwd", T, tbh, post_rows + post_par + [_rows(dob, tbh, 1024)], post_bwd_body,
        [(1024, F32)] * 5, [(1, 1024)] * 3)
    dr_s, dlw, dkp_s, dv_s, dan, dbb = _scan_bwd(xs, lw, kp, an, bb, states, invs, dy_s, T)

    def prep_bwd_body(i, xk, xwd, xad, w0, a0, kkw, kaw, w2v, a2v, dlw_, dkp1, dkp2, dan_, dbb_, dr1, dr2, dv1, dv2, dz_):
        _, vjp = jax.vjp(_prep_math, xk, xwd, xad, w0, a0, kkw, kaw, w2v, a2v)
        dxk, dxwd, dxad, dw0, da0, dkk, dka, dw2, da2 = vjp((dlw_, dkp1 + dkp2, dan_, dbb_))
        return [[dr1 + dr2, dxk, dv1 + dv2, dz_], [dxwd, dxad]], [dw0, da0, dkk, dka, dw2, da2]

    cots = [dlw, dkp_s, dkp_p, dan, dbb, dr_s, dr_p, dv_s, dv_p, dzb]
    dxs, dxl, dw0_p, da0_p, dkk_p, dka_p, dw2_p, da2_p = _rowcall(
        "rwkv_prep_bwd", T, tbh, prep_rows + prep_par + [_rows(c_, tbh, 1024) for c_ in cots], prep_bwd_body,
        [(4096, F32), (256, F32)], [(1, 1024)] * 4 + [(LANE, 1024)] * 2)

    def shift_bwd_body(i, dm_, hm, dl_, hl, um, pm, ul, pl_, mm_, ml):
        last = i == T // tbh - 1
        outs, accs = [], []
        for dv_, hv, uv, pv, mv in ((dm_, hm, um, pm, mm_), (dl_, hl, ul, pl_, ml)):
            hv = jnp.where(last, 0.0, hv)
            nxt = pltpu.roll(jnp.concatenate([dv_, hv], axis=0), tbh + SUB - 1, 0)[:tbh]
            pv = jnp.where(i == 0, 0.0, pv)
            prev = pltpu.roll(jnp.concatenate([pv, uv], axis=0), 1, 0)[SUB:]
            outs.append(dv_ * (1.0 - mv) + nxt * mv)
            accs.append(jnp.sum(dv_ * (prev - uv), axis=0, keepdims=True))
        return outs, accs

    def halo_next(arr, w, cb):
        last_blk = T // SUB - 1
        return (arr, (SUB, w), lambda i: (jnp.minimum((i + 1) * (tbh // SUB), last_blk), cb))

    du_b, du_l, dmu_main_p, dmu_lora_p = _rowcall(
        "rwkv_shift_bwd", T, tbh,
        [_rows(dxs, tbh, 4096), halo_next(dxs, 4096, 0), _rows(dxl, tbh, 256), halo_next(dxl, 256, 0),
         _rows(u, tbh, 4096, 1), halo_prev(u, 4096, 1), _rows(u, tbh, 256, LORA_CB), halo_prev(u, 256, LORA_CB),
         _whole(mu_main), _whole(mu_lora)],
        shift_bwd_body, [(4096, BF16), (256, BF16)], [(1, 4096), (1, 256)])

    du = jnp.concatenate([dq.astype(BF16), dk.astype(BF16), dv.astype(BF16), dza.astype(BF16), du_b, dga, dgb,
                          dfl.astype(BF16), jnp.zeros((T, LANE), BF16), du_l], axis=1)
    dw_pad_t = _mm(du, h, ta=True, out_dtype=BF16, tm=1280, tn=1024, name="mm_in_dw")

    ga = _split_rows(dw_pad_t, [(d, lo, pad_col, n) for pad_col, d, lo, n in pieces], NI)
    lora_g = jnp.concatenate([dw2_p[:96].reshape(96, N_DEV, 128).transpose(1, 0, 2),
                              da2_p[:96].reshape(96, N_DEV, 128).transpose(1, 0, 2)], axis=2).astype(BF16)
    gb = jnp.concatenate([dwpf.reshape(1024, N_DEV, 256).transpose(1, 0, 2),
                          dwpr.reshape(1024, N_DEV, 256).transpose(1, 0, 2),
                          dwo.reshape(N_DEV, 2048, 256), lora_g, jnp.zeros((N_DEV, RB - 4192, 256), BF16)], axis=1)

    xx, yy, cc = _place()
    chip_slots = jnp.stack([_slot((*chip, cc)) for chip in _other_chips(xx, yy)]).astype(jnp.int32)
    r1a, r1b = _exchange_pair(ga, gb)
    sa = _pair_add("pair_add_w_in", ga, r1a, chip_slots, 256, by_cols=True)
    sb = _pair_add("pair_add_packed", gb, r1b, chip_slots, 384)
    ga_own = lax.dynamic_index_in_dim(ga, me, 0, keepdims=True)
    gb_own = lax.dynamic_index_in_dim(gb, me, 0, keepdims=True)
    dh, t1a, t1b = _mm(du, w_pad, tb=True, tm=1024, tn=1024, tk=NP // 10, name="mm_in_dx", side=_exchange_first_side(sa, sb))
    core = jnp.stack([cc]).astype(jnp.int32)
    t2a, t2b = _exchange_second(_axis_add("axis_add_w_in", sa, t1a, core, 256, by_cols=True),
                                _axis_add("axis_add_packed", sb, t1b, core, 384))

    def rms_bwd_body(i, xv, g, dhv, doutv):
        _, vjp = jax.vjp(_rms_math, xv, g)
        dx_, dg_ = vjp(dhv)
        return [dx_ + doutv], [dg_]

    grad_x2, dng_p = _rowcall("rms_bwd", T, tb, [_rows(x2, tb, D), _whole(norm_gain), _rows(dh, tb, D), _rows(dout, tb, D)],
                              rms_bwd_body, [(D, F32)], [(1, D)])

    dmu = jnp.concatenate([dmu_main_p, dmu_lora_p[:, 0:96], dmu_lora_p[:, 128:224]], axis=1)
    small_parts = [dng_p, dbias_p[:, 0:8], dmu, dw0_p, da0_p, dkk_p, dka_p, drk_p, dlnw_p, dlnb_p, dgf_p, loss_p]
    SR = 128
    small = _pad_cols(jnp.concatenate(small_parts, axis=1), SR * LANE).reshape(SR, LANE)
    rs = _allgather(small, "allgather_small")

    w_in_outs = _adamw("adamw_w_in", w_in[0].T, m_w_in[0].T, v_w_in[0].T, [(ga_own, 1), (r1a, 1), (t1a, 1), (t2a, 1)], 256, by_cols=True)
    g_in, d_in, m_in, v_in = [o.T for o in w_in_outs]
    pk = lambda pf, pr, wo_, w2_, a2_: _pack_b(pf[0], pr[0], wo_[0], w2_[0], a2_[0], RB)
    outs_b = _adamw("adamw_packed", pk(w_proj_fox, w_proj_rwkv, w_out, rwkv_w2, rwkv_a2),
                    pk(m_w_proj_fox, m_w_proj_rwkv, m_w_out, m_rwkv_w2, m_rwkv_a2),
                    pk(v_w_proj_fox, v_w_proj_rwkv, v_w_out, v_rwkv_w2, v_rwkv_a2), [(gb_own, 1), (r1b, 1), (t1b, 1), (t2b, 1)], 384)

    def pack_small(ng, fb, sm, w0, a0, kk_, ka_, rk_, lnw, lnb, fg):
        parts = [ng, fb, sm, w0, a0, kk_, ka_, rk_.reshape(1, 1024), lnw, lnb, fg.reshape(1, D), jnp.zeros((1, 1), F32)]
        return _pad_cols(jnp.concatenate(parts, axis=1), SR * LANE).reshape(SR, LANE)

    outs_s = _adamw("adamw_small",
                    pack_small(norm_gain, fox_forget_bias, rwkv_shift_mix, rwkv_w0, rwkv_a0, rwkv_k_k, rwkv_k_a, rwkv_r_k,
                               rwkv_ln_w, rwkv_ln_b, final_norm_gain),
                    pack_small(m_norm_gain, m_fox_forget_bias, m_rwkv_shift_mix, m_rwkv_w0, m_rwkv_a0, m_rwkv_k_k, m_rwkv_k_a,
                               m_rwkv_r_k, m_rwkv_ln_w, m_rwkv_ln_b, m_final_norm_gain),
                    pack_small(v_norm_gain, v_fox_forget_bias, v_rwkv_shift_mix, v_rwkv_w0, v_rwkv_a0, v_rwkv_k_k, v_rwkv_k_a,
                               v_rwkv_r_k, v_rwkv_ln_w, v_rwkv_ln_b, v_final_norm_gain),
                    [(rs, N_DEV)], SR)

    def unpack_b(pkd):
        return dict(w_proj_fox=pkd[0:1024][None], w_proj_rwkv=pkd[1024:2048][None], w_out=pkd[2048:4096].reshape(1, 256, D),
                    rwkv_w2=pkd[4096:4192, 0:128][None], rwkv_a2=pkd[4096:4192, 128:256][None])

    def unpack_s(pkd):
        flat = pkd.reshape(1, SR * LANE)
        names = [("norm_gain", D), ("fox_forget_bias", 8), ("rwkv_shift_mix", 4288), ("rwkv_w0", 1024), ("rwkv_a0", 1024),
                 ("rwkv_k_k", 1024), ("rwkv_k_a", 1024), ("rwkv_r_k", 1024), ("rwkv_ln_w", 1024), ("rwkv_ln_b", 1024),
                 ("final_norm_gain", D), ("loss", 1)]
        out, off = {}, 0
        for nm, n in names:
            out[nm] = flat[:, off:off + n]
            off += n
        out["rwkv_r_k"] = out["rwkv_r_k"].reshape(1, 16, 64)
        out["final_norm_gain"] = out["final_norm_gain"].reshape(D)
        return out

    order = ["norm_gain", "w_in", "fox_forget_bias", "rwkv_shift_mix", "rwkv_w0", "rwkv_w2", "rwkv_a0", "rwkv_a2", "rwkv_k_k",
             "rwkv_k_a", "rwkv_r_k", "rwkv_ln_w", "rwkv_ln_b", "w_proj_fox", "w_proj_rwkv", "w_out", "final_norm_gain"]
    result = []
    loss = None
    for kind, big in enumerate((g_in, d_in, m_in, v_in)):
        d = {**unpack_b(outs_b[kind]), **unpack_s(outs_s[kind]), "w_in": big[None]}
        if kind == 0:
            loss = d["loss"].reshape(())
        result += [d[n] for n in order]
    return (loss, grad_x2[None], *result)
```

```python
import functools

import jax
import jax.numpy as jnp
from jax import lax
from jax.experimental import pallas as pl
from jax.experimental.pallas import tpu as pltpu

F32 = jnp.float32
BF16 = jnp.bfloat16
HI = lax.Precision.HIGHEST
H3 = lax.Precision.HIGH
MESH = pl.DeviceIdType.MESH

FOX_HD = 128
RW_HD = 64
RMS_EPS = 1e-6
GN_EPS = 64e-5
L2_EPS = 1e-12
ADAM_LR = 0.001
ADAM_B1 = 0.9
ADAM_B2 = 0.999
ADAM_EPS = 1e-08
ADAM_WD = 0.01
ADAM_STEP = 10

LANE = 128
SUB = 8
VMEM_LIMIT = 56 * 1024 * 1024
N_DEV = 8
CHUNK = 64
SCAN_GROUP = 4
SCAN_PAIRS = 2
PS = None
NEG = -1e30


def _scan_shape(T):
    c = min(CHUNK, T)
    return c, min(SCAN_GROUP, T // c)


def _cp(sem=None):
    return pltpu.CompilerParams(dimension_semantics=sem, vmem_limit_bytes=VMEM_LIMIT)


def _sigmoid(x):
    return jax.nn.sigmoid(x)


def _softplus(x):
    return jnp.maximum(x, 0.0) + jnp.log(1.0 + jnp.exp(-jnp.abs(x)))


def _nn(a, b, prec=None):
    return lax.dot_general(a, b, (((1,), (0,)), ((), ())), precision=prec, preferred_element_type=F32)


def _nt(a, b, prec=None):
    return lax.dot_general(a, b, (((1,), (1,)), ((), ())), precision=prec, preferred_element_type=F32)


def _tn(a, b, prec=None):
    return lax.dot_general(a, b, (((0,), (0,)), ((), ())), precision=prec, preferred_element_type=F32)


def _iota2(shape, dim):
    return lax.broadcasted_iota(jnp.int32, shape, dim)


def _seg_sum(x):
    r = _iota2((LANE, LANE), 0) // RW_HD
    c = _iota2((LANE, LANE), 1) // RW_HD
    bd = (r == c).astype(F32)
    parts = [_nn(x[:, j * LANE:(j + 1) * LANE], bd, H3) for j in range(x.shape[1] // LANE)]
    return parts[0] if len(parts) == 1 else jnp.concatenate(parts, axis=1)


def _mm(a, b, *, ta=False, tb=False, out_dtype=F32, tm=1024, tn=1024, tk=None, name, side=None):
    assert not (ta and tb)
    K, M = a.shape if ta else a.shape[::-1]
    N = b.shape[0] if tb else b.shape[1]
    tm, tn = min(tm, M), min(tn, N)
    tk = K if tk is None else tk
    nk = K // tk
    assert M % tm == 0 and N % tn == 0 and K % tk == 0
    gi, gj = M // tm, N // tn
    a_spec = pl.BlockSpec((tk, tm), lambda i, j, k: (k, i)) if ta else pl.BlockSpec((tm, tk), lambda i, j, k: (i, k))
    b_spec = pl.BlockSpec((tn, tk), lambda i, j, k: (j, k)) if tb else pl.BlockSpec((tk, tn), lambda i, j, k: (k, j))
    side_ins, side_outs, side_scr, side_make = side if side is not None else ((), (), (), None)
    n_si, n_so = len(side_ins), len(side_outs)
    n_acc = 0 if nk == 1 else 1

    def body(*refs):
        a_ref, b_ref = refs[:2]
        o_ref = refs[2 + n_si]
        scr = refs[3 + n_si + n_so:]
        k = pl.program_id(2)
        if side_make is not None:
            start, finish = side_make(refs[2:2 + n_si], refs[3 + n_si:3 + n_si + n_so], scr[n_acc:])
            first = jnp.logical_and(jnp.logical_and(pl.program_id(0) == 0, pl.program_id(1) == 0), k == 0)
            last = jnp.logical_and(jnp.logical_and(pl.program_id(0) == gi - 1, pl.program_id(1) == gj - 1), k == nk - 1)
            pl.when(first)(start)
        av = a_ref[...].astype(BF16)
        bv = b_ref[...].astype(BF16)
        p = _tn(av, bv) if ta else _nt(av, bv) if tb else _nn(av, bv)
        if nk == 1:
            o_ref[...] = p.astype(out_dtype)
        else:
            acc_ref = scr[0]

            @pl.when(k == 0)
            def _():
                acc_ref[...] = p

            @pl.when(k > 0)
            def _():
                acc_ref[...] += p

            @pl.when(k == nk - 1)
            def _():
                o_ref[...] = acc_ref[...].astype(out_dtype)
        if side_make is not None:
            pl.when(last)(finish)

    any_spec = pl.BlockSpec(memory_space=pl.ANY)
    res = pl.pallas_call(
        body, name=name,
        out_shape=[jax.ShapeDtypeStruct((M, N), out_dtype)] + list(side_outs),
        grid=(gi, gj, nk),
        in_specs=[a_spec, b_spec] + [any_spec] * n_si,
        out_specs=[pl.BlockSpec((tm, tn), lambda i, j, k: (i, j))] + [any_spec] * n_so,
        scratch_shapes=([] if nk == 1 else [pltpu.VMEM((tm, tn), F32)]) + list(side_scr),
        compiler_params=_cp(("arbitrary",) * 3 if side is not None else ("parallel", "parallel", "arbitrary")),
    )(a, b, *side_ins)
    return res if side is not None else res[0]


def _rows(arr, tb, w, cb=0):
    return (arr, (tb, w), lambda i: (i, cb))


def _whole(arr):
    nd = arr.ndim
    return (arr, arr.shape, lambda i: (0,) * nd)


def _rowcall(name, T, tb, ins, body, outs, accs=(), side=None):
    n_in, n_out, n_acc = len(ins), len(outs), len(accs)
    side_ins, side_outs, side_scr, side_make = side if side is not None else ((), (), (), None)
    n_si = len(side_ins)

    def kern(*refs):
        i = pl.program_id(0)
        if side_make is not None:
            side_refs = refs[n_in + n_si + n_out + n_acc:]
            start, finish = side_make(refs[n_in:n_in + n_si], side_refs[:len(side_outs)], side_refs[len(side_outs):])
            pl.when(i == 0)(start)
            refs = refs[:n_in] + refs[n_in + n_si:]
        vals = [r[...] for r in refs[:n_in]]
        ro, ao = body(i, *vals)
        for r, v in zip(refs[n_in:n_in + n_out], ro):
            if isinstance(v, (list, tuple)):
                off = 0
                for piece in v:
                    w = piece.shape[1]
                    r[:, off:off + w] = piece.astype(r.dtype)
                    off += w
            else:
                r[...] = v.astype(r.dtype)
        if accs:
            acc_refs = refs[n_in + n_out:n_in + n_out + n_acc]

            @pl.when(i == 0)
            def _():
                for r in acc_refs:
                    r[...] = jnp.zeros(r.shape, F32)

            for r, v in zip(acc_refs, ao):
                r[...] += v
        if side_make is not None:
            pl.when(i == T // tb - 1)(finish)

    any_spec = pl.BlockSpec(memory_space=pl.ANY)
    out_shape = [jax.ShapeDtypeStruct((T, w), dt) for (w, dt) in outs] + [jax.ShapeDtypeStruct(s, F32) for s in accs]
    out_specs = [pl.BlockSpec((tb, w), lambda i: (i, 0)) for (w, dt) in outs] + [pl.BlockSpec(s, lambda i: (0, 0)) for s in accs]
    res = pl.pallas_call(
        kern, name=name,
        out_shape=out_shape + list(side_outs),
        grid=(T // tb,),
        in_specs=[pl.BlockSpec(bs, im) for (_, bs, im) in ins] + [any_spec] * n_si,
        out_specs=out_specs + [any_spec] * len(side_outs),
        scratch_shapes=list(side_scr),
        compiler_params=_cp(("arbitrary",)),
    )(*[a for (a, _, _) in ins], *side_ins)
    return res


def _rms_math(x, g):
    r = lax.rsqrt(jnp.mean(x * x, axis=-1, keepdims=True) + RMS_EPS)
    return x * r * g


def _merge_math(ga, gb, pa, pb):
    return _sigmoid(ga) * pa + _sigmoid(gb) * pb


def _prep_math(xk, xwd, xad, w0, a0, kk_w, ka_w, w2p, a2p):
    z = w0 + _nn(jnp.tanh(xwd), w2p, H3)
    w = -_softplus(-z) - 0.5
    lw = -jnp.exp(w)
    ag = _sigmoid(a0 + _nn(xad, a2p, H3))
    p = xk * kk_w
    n = jnp.maximum(jnp.sqrt(_seg_sum(p * p)), L2_EPS)
    kk = p / n
    kp = xk * (1.0 + (ag - 1.0) * ka_w)
    return lw, kp, -kk, kk * ag


def _post_math(y, r, kp, v, z, lnw, lnb, rk):
    inv = 1.0 / RW_HD
    mu = _seg_sum(y) * inv
    d = y - mu
    var = _seg_sum(d * d) * inv
    yn = d * lax.rsqrt(var + GN_EPS) * lnw + lnb
    bonus = _seg_sum(r * kp * rk) * v
    return (yn + bonus) * (z * _sigmoid(z))


@functools.partial(jax.custom_vjp, nondiff_argnums=(1,))
def _unit_inverses(ms, depth):
    eye = (_iota2(ms[0].shape, 0) == _iota2(ms[0].shape, 1)).astype(F32)
    half = ms[0].shape[1]
    mp = [_nn(m, m, PS) for m in ms]
    inv = [eye + m for m in ms]
    n = 2
    while n < depth:
        last = 2 * n >= depth
        for i in range(len(ms)):
            if last:
                inv[i] = inv[i] + _nn(mp[i], inv[i], PS)
            else:
                z = _nn(mp[i], jnp.concatenate([mp[i], inv[i]], axis=1), PS)
                mp[i], inv[i] = z[:, :half], inv[i] + z[:, half:]
        n *= 2
    return tuple(inv)


def _unit_inverses_fwd(ms, depth):
    inv = _unit_inverses(ms, depth)
    return inv, inv


def _unit_inverses_bwd(depth, inv, cts):
    left = [_tn(t, g, PS) for t, g in zip(inv, cts)]
    return (tuple(_nt(l, t, PS) for l, t in zip(left, inv)),)


_unit_inverses.defvjp(_unit_inverses_fwd, _unit_inverses_bwd)


@jax.custom_vjp
def _known_inverses(ms, inv):
    return inv


def _known_inverses_fwd(ms, inv):
    return inv, inv


def _known_inverses_bwd(inv, cts):
    dms = _unit_inverses_bwd(None, inv, cts)[0]
    return dms, tuple(jnp.zeros_like(t) for t in inv)


_known_inverses.defvjp(_known_inverses_fwd, _known_inverses_bwd)


def _scan_group(s0s, *flat, known_inv=None, with_inv=False):
    P = len(s0s)
    G = len(flat) // (6 * P)
    ch = [flat[6 * i:6 * i + 6] for i in range(P * G)]
    C = ch[0][0].shape[0]
    C2 = 2 * C
    cat = jnp.concatenate
    m0 = _iota2((1, LANE), 1) < RW_HD
    mask0 = m0.astype(F32)
    mask1 = 1.0 - mask0
    r2 = _iota2((C2, C2), 0)
    c2 = _iota2((C2, C2), 1)
    dist = r2 - c2
    in_head = dist <= r2 % C
    lower = (_iota2((C, C), 0) >= _iota2((C, C), 1)).astype(F32)
    bd = (_iota2((LANE, LANE), 0) // RW_HD) == (_iota2((LANE, LANE), 1) // RW_HD)

    def tri(m, strict):
        return jnp.where(dist > 0 if strict else dist >= 0, jnp.where(in_head, m, 0.0), 0.0)

    def sel(z):
        return jnp.where(m0, z[:C], z[C:])

    gs = [_nn(lower, c[1], HI) for c in ch]
    pre = []
    for (r, lw, k, v, a, b), g in zip(ch, gs):
        g_end = jnp.sum(lw, axis=0, keepdims=True)
        en = jnp.exp(-g)
        ec = jnp.exp(g_end - g)
        pre.append(dict(at=a * jnp.exp(g - lw), rt=r * jnp.exp(g), bt=b * en, kt=k * en, bh=b * ec, kh=k * ec,
                        dec=jnp.exp(g_end), v=v))
    grams = [_nt(cat([p["at"] * mask0, p["at"] * mask1, p["rt"] * mask0, p["rt"] * mask1], axis=0),
                 cat([p["bt"], p["bt"], p["kt"], p["kt"]], axis=0), PS) for p in pre]
    mab = tuple(tri(gm[:C2, :C2], True) for gm in grams)
    tinv = _unit_inverses(mab, C) if known_inv is None else _known_inverses(mab, known_inv)
    xv =[sel(_nn(tri(gm[:C2, C2:], True), cat([p["v"], p["v"]], axis=0), PS)) for gm, p in zip(grams, pre)]
    ys, s = [None] * (P * G), list(s0s)
    for i in range(G):
        for q in range(P):
            n = q * G + i
            p, gm = pre[n], grams[n]
            sx = _nt(cat([p["at"], p["rt"]], axis=0), s[q], PS)
            x = sx[:C] + xv[n]
            u = sel(_nn(tinv[n], cat([x, x], axis=0), PS))
            v = p["v"]
            ys[n] = sx[C:] + sel(_nn(cat([tri(gm[C2:, :C2], False), tri(gm[C2:, C2:], False)], axis=1),
                                     cat([u, u, v, v], axis=0), PS))
            s[q] = s[q] * p["dec"] + jnp.where(bd, _tn(cat([u, v], axis=0), cat([p["bh"], p["kh"]], axis=0), PS), 0.0)
    return (tuple(ys), tuple(s), tinv) if with_inv else (tuple(ys), tuple(s))


def _scan_fwd(xs, lw, kp, an, bb, T):
    C, G = _scan_shape(T)
    P = SCAN_PAIRS
    nc = T // (C * G)
    npair = 1024 // LANE

    def kern(r_ref, lw_ref, k_ref, v_ref, a_ref, b_ref, y_ref, st_ref, inv_ref, s_scr):
        n = pl.program_id(1)

        @pl.when(n == 0)
        def _():
            s_scr[...] = jnp.zeros(s_scr.shape, F32)

        st_ref[0] = s_scr[...]
        ins = (r_ref, lw_ref, k_ref, v_ref, a_ref, b_ref)
        ys, s1, inv = _scan_group(
            tuple(s_scr[q] for q in range(P)),
            *[ref[i * C:(i + 1) * C, q * LANE:(q + 1) * LANE] for q in range(P) for i in range(G) for ref in ins], with_inv=True)
        for q in range(P):
            for i in range(G):
                y_ref[i * C:(i + 1) * C, q * LANE:(q + 1) * LANE] = ys[q * G + i]
                inv_ref[0, 0, q * G + i] = inv[q * G + i]
            s_scr[q] = s1[q]

    def col(off):
        return pl.BlockSpec((C * G, P * LANE), lambda p, n: (n, off // P + p))

    return pl.pallas_call(
        kern, name="rwkv_scan_fwd",
        out_shape=[jax.ShapeDtypeStruct((T, 1024), F32), jax.ShapeDtypeStruct((nc, npair, LANE, LANE), F32),
                   jax.ShapeDtypeStruct((nc, npair // P, P * G, 2 * C, 2 * C), F32)],
        grid=(npair // P, nc),
        in_specs=[col(0), col(0), col(0), col(16), col(0), col(0)],
        out_specs=[col(0), pl.BlockSpec((1, P, LANE, LANE), lambda p, n: (n, p, 0, 0)),
                   pl.BlockSpec((1, 1, P * G, 2 * C, 2 * C), lambda p, n: (n, p, 0, 0, 0))],
        scratch_shapes=[pltpu.VMEM((P, LANE, LANE), F32)],
        compiler_params=_cp(("parallel", "arbitrary")),
    )(xs, lw, kp, xs, an, bb)


def _scan_bwd(xs, lw, kp, an, bb, states, invs, dy, T):
    C, G = _scan_shape(T)
    P = SCAN_PAIRS
    nc = T // (C * G)
    npair = 1024 // LANE

    def kern(r_ref, lw_ref, k_ref, v_ref, a_ref, b_ref, st_ref, inv_ref, dy_ref, dr_ref, dlw_ref, dk_ref, dv_ref, da_ref, db_ref,
             ds_scr):
        n = pl.program_id(1)

        @pl.when(n == 0)
        def _():
            ds_scr[...] = jnp.zeros(ds_scr.shape, F32)

        ins = (r_ref, lw_ref, k_ref, v_ref, a_ref, b_ref)
        units = [(q, i) for q in range(P) for i in range(G)]
        known = tuple(inv_ref[0, 0, q * G + i] for q, i in units)
        _, vjp = jax.vjp(functools.partial(_scan_group, known_inv=known), tuple(st_ref[0, q] for q in range(P)),
                         *[ref[i * C:(i + 1) * C, q * LANE:(q + 1) * LANE] for q, i in units for ref in ins])
        grads = vjp((tuple(dy_ref[i * C:(i + 1) * C, q * LANE:(q + 1) * LANE] for q, i in units),
                     tuple(ds_scr[q] for q in range(P))))
        for q in range(P):
            ds_scr[q] = grads[0][q]
        outs = (dr_ref, dlw_ref, dk_ref, dv_ref, da_ref, db_ref)
        for n_, (q, i) in enumerate(units):
            for t, ref in enumerate(outs):
                ref[i * C:(i + 1) * C, q * LANE:(q + 1) * LANE] = grads[1 + 6 * n_ + t]

    def col(off):
        return pl.BlockSpec((C * G, P * LANE), lambda p, n: (nc - 1 - n, off // P + p))

    return pl.pallas_call(
        kern, name="rwkv_scan_bwd",
        out_shape=[jax.ShapeDtypeStruct((T, 1024), F32)] * 6,
        grid=(npair // P, nc),
        in_specs=[col(0), col(0), col(0), col(16), col(0), col(0),
                  pl.BlockSpec((1, P, LANE, LANE), lambda p, n: (nc - 1 - n, p, 0, 0)),
                  pl.BlockSpec((1, 1, P * G, 2 * C, 2 * C), lambda p, n: (nc - 1 - n, p, 0, 0, 0)), col(0)],
        out_specs=[col(0)] * 6,
        scratch_shapes=[pltpu.VMEM((P, LANE, LANE), F32)],
        compiler_params=_cp(("parallel", "arbitrary")),
    )(xs, lw, kp, xs, an, bb, states, invs, dy)


def _gates_fwd(u, bias_pad, T, f_cb):
    nb = T // LANE

    def kern(f_ref, b_ref, c_ref):
        x = f_ref[...] + b_ref[...]
        lf = jnp.minimum(x, 0.0) - jnp.log(1.0 + jnp.exp(-jnp.abs(x)))
        lft = lf.T
        ut = (_iota2((LANE, LANE), 0) <= _iota2((LANE, LANE), 1)).astype(F32)
        carry = jnp.zeros((LANE, 1), F32)
        for blk in range(nb):
            seg = lft[:, blk * LANE:(blk + 1) * LANE]
            cs = _nn(seg, ut, HI) + carry
            c_ref[:, blk * LANE:(blk + 1) * LANE] = cs[:SUB, :]
            carry = carry + jnp.sum(seg, axis=1, keepdims=True)

    return pl.pallas_call(
        kern, name="fox_gates_fwd",
        out_shape=jax.ShapeDtypeStruct((SUB, T), F32),
        grid=(1,),
        in_specs=[pl.BlockSpec((T, LANE), lambda i: (0, f_cb)), pl.BlockSpec((1, LANE), lambda i: (0, 0))],
        out_specs=pl.BlockSpec((SUB, T), lambda i: (0, 0)),
        compiler_params=_cp(("arbitrary",)),
    )(u, bias_pad)


def _gates_bwd(dc, u, bias_pad, T, f_cb):
    nb = T // LANE

    def kern(dc_ref, f_ref, b_ref, dfl_ref, db_ref):
        dcv = jnp.concatenate([dc_ref[...], jnp.zeros((LANE - SUB, T), F32)], axis=0)
        lt = (_iota2((LANE, LANE), 0) >= _iota2((LANE, LANE), 1)).astype(F32)
        carry = jnp.zeros((LANE, 1), F32)
        pieces = [None] * nb
        for blk in range(nb - 1, -1, -1):
            seg = dcv[:, blk * LANE:(blk + 1) * LANE]
            pieces[blk] = _nn(seg, lt, HI) + carry
            carry = carry + jnp.sum(seg, axis=1, keepdims=True)
        dlf = (pieces[0] if nb == 1 else jnp.concatenate(pieces, axis=1)).T
        x = f_ref[...] + b_ref[...]
        dfl = dlf * _sigmoid(-x)
        dfl_ref[...] = dfl
        db_ref[...] = jnp.sum(dfl, axis=0, keepdims=True)

    return pl.pallas_call(
        kern, name="fox_gates_bwd",
        out_shape=[jax.ShapeDtypeStruct((T, LANE), F32), jax.ShapeDtypeStruct((1, LANE), F32)],
        grid=(1,),
        in_specs=[pl.BlockSpec((SUB, T), lambda i: (0, 0)), pl.BlockSpec((T, LANE), lambda i: (0, f_cb)),
                  pl.BlockSpec((1, LANE), lambda i: (0, 0))],
        out_specs=[pl.BlockSpec((T, LANE), lambda i: (0, 0)), pl.BlockSpec((1, LANE), lambda i: (0, 0))],
        compiler_params=_cp(("arbitrary",)),
    )(dc, u, bias_pad)


ATTN_HEADS = 4


def _attn_block(T):
    return 256 if T % 256 == 0 and T >= 512 else 128


def _attn_fwd(u, c3, T):
    H, HP = 8, ATTN_HEADS
    bq = _attn_block(T)
    nq = T // bq
    scale = FOX_HD ** -0.5
    lanes = [slice(h * LANE, (h + 1) * LANE) for h in range(HP)]

    def kern(q_ref, k_ref, v_ref, z_ref, cq_ref, ck_ref, o_ref, oa_ref, lse_ref):
        i = pl.program_id(1)
        q = [(q_ref[:, ln] * scale).astype(BF16) for ln in lanes]
        c0 = [cq_ref[h][:, 0:1] for h in range(HP)]

        def step(j, carry, diagonal=False):
            off = pl.multiple_of(j * bq, bq)
            s = [_nt(q[h], k_ref[pl.ds(off, bq), lanes[h]].astype(BF16)) + (c0[h] - ck_ref[h, :, pl.ds(off, bq)])
                 for h in range(HP)]
            ps, out = [], []
            for h in range(HP):
                m, l, acc = carry[h]
                sh = s[h]
                if diagonal:
                    sh = jnp.where(_iota2((bq, bq), 1) <= _iota2((bq, bq), 0), sh, NEG)
                m_new = jnp.maximum(m, jnp.max(sh, axis=1, keepdims=True))
                p = jnp.exp(sh - m_new)
                alpha = jnp.exp(m - m_new)
                ps.append(p.astype(BF16))
                out.append((m_new, alpha * l + jnp.sum(p, axis=1, keepdims=True), alpha * acc))
            return tuple((m, l, acc + _nn(ps[h], v_ref[pl.ds(off, bq), lanes[h]].astype(BF16)))
                         for h, (m, l, acc) in enumerate(out))

        init = tuple((jnp.full((bq, 1), NEG, F32), jnp.zeros((bq, 1), F32), jnp.zeros((bq, FOX_HD), F32)) for _ in range(HP))
        res = step(i, lax.fori_loop(0, i, step, init), diagonal=True)
        for h, (m, l, acc) in enumerate(res):
            o = acc / l
            z = z_ref[:, lanes[h]]
            o_ref[:, lanes[h]] = o
            oa_ref[:, lanes[h]] = (o * z * _sigmoid(z)).astype(BF16)
            lse_ref[h] = m + jnp.log(l)

    W = HP * LANE
    return pl.pallas_call(
        kern, name="fox_attn_fwd",
        out_shape=[jax.ShapeDtypeStruct((T, 1024), F32), jax.ShapeDtypeStruct((T, 1024), BF16),
                   jax.ShapeDtypeStruct((H, T, 1), F32)],
        grid=(H // HP, nq),
        in_specs=[pl.BlockSpec((bq, W), lambda g, i: (i, g)),
                  pl.BlockSpec((T, W), lambda g, i: (0, 8 // HP + g)),
                  pl.BlockSpec((T, W), lambda g, i: (0, 16 // HP + g)),
                  pl.BlockSpec((bq, W), lambda g, i: (i, 24 // HP + g)),
                  pl.BlockSpec((HP, 1, bq), lambda g, i: (g, 0, i)),
                  pl.BlockSpec((HP, 1, T), lambda g, i: (g, 0, 0))],
        out_specs=[pl.BlockSpec((bq, W), lambda g, i: (i, g)),
                   pl.BlockSpec((bq, W), lambda g, i: (i, g)),
                   pl.BlockSpec((HP, bq, 1), lambda g, i: (g, i, 0))],
        compiler_params=_cp(("parallel", "arbitrary")),
    )(u, u, u, u, c3, c3)


def _attn_probs(s, lse_i, diagonal):
    if not diagonal:
        return jnp.exp(s - lse_i)
    keep = _iota2(s.shape, 1) <= _iota2(s.shape, 0)
    return jnp.where(keep, jnp.exp(jnp.where(keep, s, NEG) - lse_i), 0.0)


def _attn_delta(u, c3, o, lse, doa, T):
    H, HP = 8, ATTN_HEADS
    bq = _attn_block(T)
    nq = T // bq
    scale = FOX_HD ** -0.5
    lanes = [slice(h * LANE, (h + 1) * LANE) for h in range(HP)]

    def kern(q_ref, k_ref, v_ref, z_ref, c_ref, o_ref, lse_ref, doa_ref, do_ref, dz_ref, dl_ref):
        i = pl.program_id(1)
        z = z_ref[...]
        sg = _sigmoid(z)
        dov = doa_ref[...]
        do_all = (dov * z * sg).astype(BF16)
        do_ref[...] = do_all
        dz_ref[...] = dov * o_ref[...] * (sg * (1.0 + z * (1.0 - sg)))
        qs = [(q_ref[:, ln] * scale).astype(BF16) for ln in lanes]
        dob = [do_all[:, ln] for ln in lanes]
        ioff = pl.multiple_of(i * bq, bq)
        c0 = [c_ref[h, :, pl.ds(ioff, bq)][:, 0:1] for h in range(HP)]
        lse_i = [lse_ref[h] for h in range(HP)]

        def step(j, acc, diagonal=False):
            off = pl.multiple_of(j * bq, bq)
            s = [_nt(qs[h], k_ref[pl.ds(off, bq), lanes[h]].astype(BF16)) + (c0[h] - c_ref[h, :, pl.ds(off, bq)])
                 for h in range(HP)]
            dp = [_nt(dob[h], v_ref[pl.ds(off, bq), lanes[h]].astype(BF16)) for h in range(HP)]
            return tuple(acc[h] + jnp.sum(_attn_probs(s[h], lse_i[h], diagonal) * dp[h], axis=1, keepdims=True)
                         for h in range(HP))

        init = tuple(jnp.zeros((bq, 1), F32) for _ in range(HP))
        res = step(i, lax.fori_loop(0, i, step, init), diagonal=True)
        for h in range(HP):
            dl_ref[h] = res[h]

    W = HP * LANE
    full = lambda cb: pl.BlockSpec((T, W), lambda g, i: (0, cb // HP + g))
    blk = lambda cb: pl.BlockSpec((bq, W), lambda g, i: (i, cb // HP + g))
    return pl.pallas_call(
        kern, name="fox_attn_delta",
        out_shape=[jax.ShapeDtypeStruct((T, 1024), BF16), jax.ShapeDtypeStruct((T, 1024), F32), jax.ShapeDtypeStruct((H, T, 1), F32)],
        grid=(H // HP, nq),
        in_specs=[blk(0), full(8), full(16), blk(24),
                  pl.BlockSpec((HP, 1, T), lambda g, i: (g, 0, 0)),
                  blk(0),
                  pl.BlockSpec((HP, bq, 1), lambda g, i: (g, i, 0)),
                  blk(0)],
        out_specs=[blk(0), blk(0), pl.BlockSpec((HP, bq, 1), lambda g, i: (g, i, 0))],
        compiler_params=_cp(("parallel", "arbitrary")),
    )(u, u, u, u, c3, o, lse, doa)


def _attn_bwd(u, c3, lse, do, delta, T):
    H, HP = 8, ATTN_HEADS
    bq = _attn_block(T)
    nq = T // bq
    scale = FOX_HD ** -0.5
    lanes = [slice(h * LANE, (h + 1) * LANE) for h in range(HP)]

    def kern(q_ref, k_ref, v_ref, c_ref, lse_ref, do_ref, dl_ref, dq_ref, dk_ref, dv_ref, dc_ref):
        j = pl.program_id(1)

        @pl.when(j == 0)
        def _():
            dq_ref[...] = jnp.zeros(dq_ref.shape, F32)

        kj = [k_ref[:, ln].astype(BF16) for ln in lanes]
        vj = [v_ref[:, ln].astype(BF16) for ln in lanes]
        joff = pl.multiple_of(j * bq, bq)
        ck = [c_ref[h, :, pl.ds(joff, bq)] for h in range(HP)]

        def step(i, carry, diagonal=False):
            off = pl.multiple_of(i * bq, bq)
            qs = [(q_ref[pl.ds(off, bq), ln] * scale).astype(BF16) for ln in lanes]
            dob = [do_ref[pl.ds(off, bq), ln] for ln in lanes]
            s = [_nt(qs[h], kj[h]) + (c_ref[h, :, pl.ds(off, bq)][:, 0:1] - ck[h]) for h in range(HP)]
            dp = [_nt(dob[h], vj[h]) for h in range(HP)]
            pb, dsb, dcs = [], [], []
            for h in range(HP):
                p = _attn_probs(s[h], lse_ref[h, pl.ds(off, bq), :], diagonal)
                ds = p * (dp[h] - dl_ref[h, pl.ds(off, bq), :])
                pb.append(p.astype(BF16))
                dsb.append(ds.astype(BF16))
                dcs.append(jnp.sum(ds, axis=0, keepdims=True))
            out = []
            for h, (dk, dv, dc) in enumerate(carry):
                dq_ref[pl.ds(off, bq), lanes[h]] += _nn(dsb[h], kj[h]) * scale
                out.append((dk + _tn(dsb[h], qs[h]), dv + _tn(pb[h], dob[h]), dc - dcs[h]))
            return tuple(out)

        init = tuple((jnp.zeros((bq, FOX_HD), F32), jnp.zeros((bq, FOX_HD), F32), jnp.zeros((1, bq), F32)) for _ in range(HP))
        res = lax.fori_loop(j + 1, nq, step, step(j, init, diagonal=True))
        for h, (dk, dv, dc) in enumerate(res):
            dk_ref[:, lanes[h]] = dk
            dv_ref[:, lanes[h]] = dv
            dc_ref[h] = dc

    W = HP * LANE
    full = lambda cb: pl.BlockSpec((T, W), lambda g, j: (0, cb // HP + g))
    blk = lambda cb: pl.BlockSpec((bq, W), lambda g, j: (j, cb // HP + g))
    col1 = pl.BlockSpec((HP, T, 1), lambda g, j: (g, 0, 0))
    return pl.pallas_call(
        kern, name="fox_attn_bwd",
        out_shape=[jax.ShapeDtypeStruct((T, 1024), F32)] * 3 + [jax.ShapeDtypeStruct((H, 1, T), F32)],
        grid=(H // HP, nq),
        in_specs=[full(0), blk(8), blk(16), pl.BlockSpec((HP, 1, T), lambda g, j: (g, 0, 0)), col1, full(0), col1],
        out_specs=[full(0), blk(0), blk(0), pl.BlockSpec((HP, 1, bq), lambda g, j: (g, 0, j))],
        compiler_params=_cp(("parallel", "arbitrary")),
    )(u, u, u, c3, lse, do, delta)


def _place():
    return lax.axis_index("x"), lax.axis_index("y"), lax.axis_index("c")


def _slot(p):
    return 4 * p[0] + 2 * p[1] + p[2]


def _other_chips(x, y):
    return [(1 - x, y), (x, 1 - y), (1 - x, 1 - y)]


def _allgather_steps(in_refs, out_refs, scratch):
    (src,), (dst,) = in_refs, out_refs
    send_sems, recv_sems, local_sem = scratch
    x, y, c = _place()
    me, sibling = (x, y, c), (x, y, 1 - c)
    chips = _other_chips(x, y)

    def copy(k, block, to, from_input=False):
        d = dst.at[_slot(block)]
        return pltpu.make_async_remote_copy(
            src_ref=src if from_input else d, dst_ref=d, send_sem=send_sems.at[k], recv_sem=recv_sems.at[k],
            device_id=to, device_id_type=MESH)

    def first_copies():
        return [copy(0, me, sibling, True)] + [copy(1 + j, me, (*chip, c), True) for j, chip in enumerate(chips)]

    def start():
        pltpu.make_async_copy(src, dst.at[_slot(me)], local_sem).start()
        for cp in first_copies():
            cp.start()

    def finish():
        passed = []
        for j, chip in enumerate(chips):
            copy(1 + j, (*chip, c), me).wait_recv()
            passed.append(copy(4 + j, (*chip, c), sibling))
            passed[-1].start()
        copy(0, sibling, me).wait_recv()
        for j, chip in enumerate(chips):
            copy(4 + j, (*chip, 1 - c), me).wait_recv()
        for cp in first_copies() + passed:
            cp.wait_send()
        pltpu.make_async_copy(src, dst.at[_slot(me)], local_sem).wait()

    return start, finish


def _allgather_relay_steps(in_refs, out_refs, scratch):
    (src,), (dst,) = in_refs, out_refs
    send_sems, recv_sems, local_sem = scratch
    x, y, c = _place()
    me, sibling = (x, y, c), (x, y, 1 - c)
    x_nbr, y_nbr, diag = (1 - x, y, c), (x, 1 - y, c), (1 - x, 1 - y, c)
    flip = lambda a, bit: a + bit - 2 * a * bit
    relay_from = (flip(x, 1 - c), flip(y, c), c)
    relay_to = (flip(x, c), flip(y, 1 - c), c)

    def copy(k, block, to, from_input=False):
        d = dst.at[_slot(block)]
        return pltpu.make_async_remote_copy(
            src_ref=src if from_input else d, dst_ref=d, send_sem=send_sems.at[k], recv_sem=recv_sems.at[k],
            device_id=to, device_id_type=MESH)

    def first_copies():
        return [copy(0, me, sibling, True), copy(1, me, x_nbr, True), copy(2, me, y_nbr, True)]

    def other(block):
        return block[:2] + (1 - c,)

    def start():
        pltpu.make_async_copy(src, dst.at[_slot(me)], local_sem).start()
        for cp in first_copies():
            cp.start()

    def finish():
        copy(1, x_nbr, me).wait_recv()
        copy(2, y_nbr, me).wait_recv()
        later = [copy(3, relay_from, relay_to), copy(4, x_nbr, sibling), copy(5, y_nbr, sibling)]
        for cp in later:
            cp.start()
        copy(3, diag, me).wait_recv()
        later.append(copy(6, diag, sibling))
        later[-1].start()
        copy(0, sibling, me).wait_recv()
        for k, block in ((4, x_nbr), (5, y_nbr), (6, diag)):
            copy(k, other(block), me).wait_recv()
        for cp in first_copies() + later:
            cp.wait_send()
        pltpu.make_async_copy(src, dst.at[_slot(me)], local_sem).wait()

    return start, finish


def _allgather_side(a, relay=False):
    return ((a,), (jax.ShapeDtypeStruct((N_DEV,) + a.shape, a.dtype),),
            (pltpu.SemaphoreType.DMA((7,)), pltpu.SemaphoreType.DMA((7,)), pltpu.SemaphoreType.DMA),
            _allgather_relay_steps if relay else _allgather_steps)


def _allgather(a, name, relay=False):
    ins, outs, scratch, make = _allgather_side(a, relay)

    def body(a_ref, o_ref, *scr):
        start, finish = make((a_ref,), (o_ref,), scr)
        start()
        finish()

    any_spec = pl.BlockSpec(memory_space=pl.ANY)
    return pl.pallas_call(body, name=name, out_shape=outs[0], in_specs=[any_spec], out_specs=any_spec,
                          scratch_shapes=list(scratch))(a)


def _exchange_pair(ga, gb):
    def body(ga_ref, gb_ref, ra_ref, rb_ref, send_sems, recv_sems):
        x, y, c = _place()
        sibling = (x, y, 1 - c)
        slots = [_slot(sibling)] + [_slot((*chip, 1 - c)) for chip in _other_chips(x, y)]
        cps = []
        for t, (src, dst) in enumerate(((ga_ref, ra_ref), (gb_ref, rb_ref))):
            for k, ps in enumerate(slots):
                cps.append(pltpu.make_async_remote_copy(
                    src_ref=src.at[ps], dst_ref=dst.at[k], send_sem=send_sems.at[t, k], recv_sem=recv_sems.at[t, k],
                    device_id=sibling, device_id_type=MESH))
        for cp in cps:
            cp.start()
        for cp in cps:
            cp.wait()

    any_spec = pl.BlockSpec(memory_space=pl.ANY)
    return pl.pallas_call(
        body, name="exchange_pair",
        out_shape=[jax.ShapeDtypeStruct((4,) + ga.shape[1:], ga.dtype), jax.ShapeDtypeStruct((4,) + gb.shape[1:], gb.dtype)],
        in_specs=[any_spec] * 2,
        out_specs=[any_spec] * 2,
        scratch_shapes=[pltpu.SemaphoreType.DMA((2, 4)), pltpu.SemaphoreType.DMA((2, 4))],
    )(ga, gb)


def _tiling(R, Cc, tile, by_cols):
    if by_cols:
        assert Cc % tile == 0
        return Cc // tile, (R, tile), lambda lead, i: (lead, 0, i)
    assert R % tile == 0
    return R // tile, (tile, Cc), lambda lead, i: (lead, i, 0)


def _pair_add(name, g, r1, slots, tile, by_cols=False):
    _, R, Cc = g.shape
    steps, blk, at = _tiling(R, Cc, tile, by_cols)

    def kern(s_ref, a_ref, b_ref, o_ref):
        o_ref[...] = (a_ref[...].astype(F32) + b_ref[...].astype(F32)).astype(o_ref.dtype)

    return pl.pallas_call(
        kern, name=name,
        out_shape=jax.ShapeDtypeStruct((3, R, Cc), BF16),
        grid_spec=pltpu.PrefetchScalarGridSpec(
            num_scalar_prefetch=1, grid=(3, steps),
            in_specs=[pl.BlockSpec((1,) + blk, lambda j, i, s: at(s[j], i)),
                      pl.BlockSpec((1,) + blk, lambda j, i, s: at(1 + j, i))],
            out_specs=pl.BlockSpec((1,) + blk, lambda j, i, s: at(j, i))),
        compiler_params=_cp(("arbitrary", "arbitrary")),
    )(slots, g, r1)


def _axis_neighbours():
    x, y, c = _place()
    flip = lambda a, bit: a + bit - 2 * a * bit
    return (flip(x, c), flip(y, 1 - c), c), (flip(x, 1 - c), flip(y, c), c), c


def _exchange_first_steps(in_refs, out_refs, scratch):
    pairs = list(zip(in_refs, out_refs))
    send_sems, recv_sems = scratch
    first, _, c = _axis_neighbours()

    def copies():
        return [pltpu.make_async_remote_copy(
            src_ref=src.at[j], dst_ref=dst.at[k], send_sem=send_sems.at[t, k], recv_sem=recv_sems.at[t, k],
            device_id=first, device_id_type=MESH)
            for t, (src, dst) in enumerate(pairs) for k, j in enumerate((1 - c, 2))]

    def start():
        for cp in copies():
            cp.start()

    def finish():
        for cp in copies():
            cp.wait()

    return start, finish


def _exchange_first_side(sa, sb):
    return ((sa, sb), (jax.ShapeDtypeStruct((2,) + sa.shape[1:], sa.dtype), jax.ShapeDtypeStruct((2,) + sb.shape[1:], sb.dtype)),
            (pltpu.SemaphoreType.DMA((2, 2)), pltpu.SemaphoreType.DMA((2, 2))), _exchange_first_steps)


def _axis_add(name, s, t1, core, tile, by_cols=False):
    _, R, Cc = s.shape
    steps, blk, at = _tiling(R, Cc, tile, by_cols)

    def kern(c_ref, a_ref, b_ref, o_ref):
        o_ref[...] = (a_ref[...].astype(F32) + b_ref[...].astype(F32)).astype(o_ref.dtype)

    return pl.pallas_call(
        kern, name=name,
        out_shape=jax.ShapeDtypeStruct((1, R, Cc), BF16),
        grid_spec=pltpu.PrefetchScalarGridSpec(
            num_scalar_prefetch=1, grid=(steps,),
            in_specs=[pl.BlockSpec((1,) + blk, lambda i, cr: at(cr[0], i)),
                      pl.BlockSpec((1,) + blk, lambda i, cr: at(1, i))],
            out_specs=pl.BlockSpec((1,) + blk, lambda i, cr: at(0, i))),
        compiler_params=_cp(("arbitrary",)),
    )(core, s, t1)


def _exchange_second_steps(in_refs, out_refs, scratch):
    send_sems, recv_sems = scratch
    _, second, _ = _axis_neighbours()

    def copies():
        return [pltpu.make_async_remote_copy(src_ref=src, dst_ref=dst, send_sem=send_sems.at[t], recv_sem=recv_sems.at[t],
                                             device_id=second, device_id_type=MESH)
                for t, (src, dst) in enumerate(zip(in_refs, out_refs))]

    def start():
        for cp in copies():
            cp.start()

    def finish():
        for cp in copies():
            cp.wait()

    return start, finish


def _exchange_second_side(pa, pb):
    return ((pa, pb), (jax.ShapeDtypeStruct(pa.shape, pa.dtype), jax.ShapeDtypeStruct(pb.shape, pb.dtype)),
            (pltpu.SemaphoreType.DMA((2,)), pltpu.SemaphoreType.DMA((2,))), _exchange_second_steps)


def _adamw(name, w, m, v, parts, tile, by_cols=False):
    R, Cc = w.shape
    steps, blk_shape, at = _tiling(R, Cc, tile, by_cols)
    n_parts = len(parts)

    def kern(*refs):
        w_ref, m_ref, v_ref = refs[:3]
        g = None
        for r_ref, (_, n) in zip(refs[3:3 + n_parts], parts):
            for s in range(n):
                term = r_ref[s].astype(F32)
                g = term if g is None else g + term
        g_out, d_out, m_out, v_out = refs[3 + n_parts:]
        mn = ADAM_B1 * m_ref[...] + (1.0 - ADAM_B1) * g
        vn = ADAM_B2 * v_ref[...] + (1.0 - ADAM_B2) * (g * g)
        m_hat = mn / (1.0 - ADAM_B1 ** ADAM_STEP)
        v_hat = vn / (1.0 - ADAM_B2 ** ADAM_STEP)
        g_out[...] = g
        d_out[...] = -ADAM_LR * (m_hat / (jnp.sqrt(v_hat) + ADAM_EPS) + ADAM_WD * w_ref[...])
        m_out[...] = mn
        v_out[...] = vn

    blk = pl.BlockSpec(blk_shape, lambda i: at(0, i)[1:])
    return pl.pallas_call(
        kern, name=name,
        out_shape=[jax.ShapeDtypeStruct((R, Cc), F32)] * 4,
        grid=(steps,),
        in_specs=[blk] * 3 + [pl.BlockSpec((n,) + blk_shape, lambda i: at(0, i)) for (_, n) in parts],
        out_specs=[blk] * 4,
        compiler_params=_cp(("arbitrary",)),
    )(w, m, v, *[a for (a, _) in parts])


def _assemble_columns(blocks, pieces, zeros, width):
    _, R, Cc = blocks.shape
    tr = min(256, R)

    def kern(b_ref, o_ref):
        for col, n in zeros:
            o_ref[:, col:col + n] = jnp.zeros((tr, n), o_ref.dtype)
        for col, d, lo, n in pieces:
            o_ref[:, col:col + n] = b_ref[d, :, lo:lo + n]

    return pl.pallas_call(
        kern, name="assemble_w_in",
        out_shape=jax.ShapeDtypeStruct((R, width), blocks.dtype),
        grid=(R // tr,),
        in_specs=[pl.BlockSpec((N_DEV, tr, Cc), lambda i: (0, i, 0))],
        out_specs=pl.BlockSpec((tr, width), lambda i: (i, 0)),
        compiler_params=_cp(("parallel",)),
    )(blocks)


def _split_rows(x, pieces, rows):
    _, Cc = x.shape
    tc = min(256, Cc)

    def kern(x_ref, o_ref):
        for d, lo, row, n in pieces:
            o_ref[d, lo:lo + n, :] = x_ref[row:row + n, :]

    return pl.pallas_call(
        kern, name="split_w_in_grad",
        out_shape=jax.ShapeDtypeStruct((N_DEV, rows, Cc), x.dtype),
        grid=(Cc // tc,),
        in_specs=[pl.BlockSpec((x.shape[0], tc), lambda i: (0, i))],
        out_specs=pl.BlockSpec((N_DEV, rows, tc), lambda i: (0, 0, i)),
        compiler_params=_cp(("parallel",)),
    )(x)


def _pad_cols(a, w):
    return jnp.pad(a, ((0, 0), (0, w - a.shape[1])))


def _pad_rows(a, r):
    return jnp.pad(a, ((0, r - a.shape[0]), (0, 0)))


def _pack_b(pf, pr, wo, w2, a2, rows):
    body = jnp.concatenate([pf, pr, wo.reshape(2048, 256), jnp.concatenate([w2, a2], axis=1)], axis=0)
    return _pad_rows(body, rows)


def kernel(x, norm_gain, w_in, fox_forget_bias, rwkv_shift_mix, rwkv_w0, rwkv_w2, rwkv_a0, rwkv_a2, rwkv_k_k, rwkv_k_a, rwkv_r_k, rwkv_ln_w, rwkv_ln_b, w_proj_fox, w_proj_rwkv, w_out, final_norm_gain, loss_target, m_norm_gain, m_w_in, m_fox_forget_bias, m_rwkv_shift_mix, m_rwkv_w0, m_rwkv_w2, m_rwkv_a0, m_rwkv_a2, m_rwkv_k_k, m_rwkv_k_a, m_rwkv_r_k, m_rwkv_ln_w, m_rwkv_ln_b, m_w_proj_fox, m_w_proj_rwkv, m_w_out, m_final_norm_gain, v_norm_gain, v_w_in, v_fox_forget_bias, v_rwkv_shift_mix, v_rwkv_w0, v_rwkv_w2, v_rwkv_a0, v_rwkv_a2, v_rwkv_k_k, v_rwkv_k_a, v_rwkv_r_k, v_rwkv_ln_w, v_rwkv_ln_b, v_w_proj_fox, v_w_proj_rwkv, v_w_out, v_final_norm_gain):
    T, D = x.shape[1], x.shape[2]
    assert D == 2048 and T % LANE == 0
    NI = w_in.shape[2]
    IN = N_DEV * NI
    RB = 4224
    x2 = x[0]
    lt2 = loss_target[0]
    me = _slot(_place())

    wa = _allgather(w_in[0].astype(BF16), "allgather_w_in", relay=True)
    packed_own = _pack_b(w_proj_fox[0], w_proj_rwkv[0], w_out[0], rwkv_w2[0], rwkv_a2[0], RB).astype(BF16)
    sections = [(0, 4096, 0), (4104, 4096, 4096), (8392, 4096, 8192), (4096, 8, 12288), (8200, 96, 12544), (8296, 96, 12672)]
    NP = 12800
    pieces, zeros, at_col = [], [], 0
    for lo, width, pad_lo in sections:
        if pad_lo > at_col:
            zeros.append((at_col, pad_lo - at_col))
        col = lo
        while col < lo + width:
            d = col // NI
            stop = min(lo + width, (d + 1) * NI)
            pieces.append((pad_lo + col - lo, d, col - d * NI, stop - col))
            col = stop
        at_col = pad_lo + width
    zeros.append((at_col, NP - at_col))
    w_pad = _assemble_columns(wa, pieces, zeros, NP)
    F_CB, LORA_CB = 96, 49

    mu = rwkv_shift_mix
    mu_main = mu[:, 0:4096]
    mu_lora = jnp.concatenate([_pad_cols(mu[:, 4096:4192], LANE), _pad_cols(mu[:, 4192:4288], LANE)], axis=1)
    bias_pad = _pad_cols(fox_forget_bias, LANE)
    rk_flat = rwkv_r_k.reshape(1, 1024)
    gf = final_norm_gain.reshape(1, D)

    tb = min(256, T)
    tbh = min(128, T)
    (h,) = _rowcall("rms_fwd", T, tb, [_rows(x2, tb, D), _whole(norm_gain)],
                    lambda i, xv, g: ([_rms_math(xv, g)], []), [(D, BF16)])
    u, wb = _mm(h, w_pad, tm=1024, tn=1280, name="mm_in", side=_allgather_side(packed_own))
    wpf = wb[:, 0:1024, :].transpose(1, 0, 2).reshape(1024, D)
    wpr = wb[:, 1024:2048, :].transpose(1, 0, 2).reshape(1024, D)
    wo = wb[:, 2048:4096, :].reshape(N_DEV * 256, D)
    w2p = _pad_rows(wb[:, 4096:4192, 0:128].transpose(1, 0, 2).reshape(96, 1024).astype(F32), LANE)
    a2p = _pad_rows(wb[:, 4096:4192, 128:256].transpose(1, 0, 2).reshape(96, 1024).astype(F32), LANE)

    c8 = _gates_fwd(u, bias_pad, T, F_CB)
    c3 = c8.reshape(8, 1, T)
    o_raw, o_a, lse = _attn_fwd(u, c3, T)

    def shift_body(i, um, hm, ul, hl, mm_, ml):
        outs = []
        for uv, hv, mv in ((um, hm, mm_), (ul, hl, ml)):
            hv = jnp.where(i == 0, 0.0, hv)
            prev = pltpu.roll(jnp.concatenate([hv, uv], axis=0), 1, 0)[SUB:]
            outs.append(uv + (prev - uv) * mv)
        return outs, []

    def halo_prev(arr, w, cb):
        return (arr, (SUB, w), lambda i: (jnp.maximum(i * (tbh // SUB) - 1, 0), cb))

    xs, xl = _rowcall("rwkv_shift_fwd", T, tbh,
                      [_rows(u, tbh, 4096, 1), halo_prev(u, 4096, 1), _rows(u, tbh, 256, LORA_CB), halo_prev(u, 256, LORA_CB),
                       _whole(mu_main), _whole(mu_lora)],
                      shift_body, [(4096, F32), (256, F32)])

    prep_par = [_whole(rwkv_w0), _whole(rwkv_a0), _whole(rwkv_k_k), _whole(rwkv_k_a), _whole(w2p), _whole(a2p)]
    prep_rows = [_rows(xs, tbh, 1024, 1), _rows(xl, tbh, LANE, 0), _rows(xl, tbh, LANE, 1)]
    lw, kp, an, bb = _rowcall("rwkv_prep_fwd", T, tbh, prep_rows + prep_par,
                              lambda i, *a: (list(_prep_math(*a)), []), [(1024, F32)] * 4)
    y, states, invs = _scan_fwd(xs, lw, kp, an, bb, T)
    post_rows = [_rows(y, tbh, 1024), _rows(xs, tbh, 1024, 0), _rows(kp, tbh, 1024), _rows(xs, tbh, 1024, 2), _rows(xs, tbh, 1024, 3)]
    post_par = [_whole(rwkv_ln_w), _whole(rwkv_ln_b), _whole(rk_flat)]
    (o_b,) = _rowcall("rwkv_post_fwd", T, tbh, post_rows + post_par,
                      lambda i, *a: ([_post_math(*a)], []), [(1024, BF16)])

    pa = _mm(o_a, wpf, name="mm_proj_fox")
    pb = _mm(o_b, wpr, name="mm_proj_rwkv")
    merge_rows = [_rows(u, tb, D, 4), _rows(u, tb, D, 5), _rows(pa, tb, D), _rows(pb, tb, D)]
    (mg,) = _rowcall("merge_fwd", T, tb, merge_rows, lambda i, *a: ([_merge_math(*a)], []), [(D, BF16)])
    mo = _mm(mg, wo, name="mm_out")

    def head_body(i, xv, mov, ltv, g):
        out = xv + mov
        r = lax.rsqrt(jnp.mean(out * out, axis=-1, keepdims=True) + RMS_EPS)
        yn = out * r
        err = yn * g - ltv
        loss = 0.5 * jnp.sum(jnp.sum(err * err, axis=-1, keepdims=True), axis=0, keepdims=True) / D
        dyv = err / D
        dyn = dyv * g
        dout = r * (dyn - yn * jnp.mean(dyn * yn, axis=-1, keepdims=True))
        return [dout], [loss, jnp.sum(dyv * yn, axis=0, keepdims=True)]

    dout, loss_p, dgf_p = _rowcall("loss_head", T, tb, [_rows(x2, tb, D), _rows(mo, tb, D), _rows(lt2, tb, D), _whole(gf)],
                                   head_body, [(D, F32)], [(1, 1), (1, D)])

    dm = _mm(dout, wo, tb=True, name="mm_out_dx")
    dwo = _mm(mg, dout, ta=True, out_dtype=BF16, name="mm_out_dw")

    def merge_bwd_body(i, ga, gb, pav, pbv, dmv):
        _, vjp = jax.vjp(_merge_math, ga, gb, pav, pbv)
        dga, dgb, dpa, dpb = vjp(dmv)
        return [dga, dgb, dpa, dpb], []

    dga, dgb, dpa, dpb = _rowcall("merge_bwd", T, tb, merge_rows + [_rows(dm, tb, D)], merge_bwd_body,
                                  [(D, BF16), (D, BF16), (D, BF16), (D, BF16)])
    doa = _mm(dpa, wpf, tb=True, name="mm_proj_fox_dx")
    dwpf = _mm(o_a, dpa, ta=True, out_dtype=BF16, name="mm_proj_fox_dw")
    dob = _mm(dpb, wpr, tb=True, name="mm_proj_rwkv_dx")
    dwpr = _mm(o_b, dpb, ta=True, out_dtype=BF16, name="mm_proj_rwkv_dw")

    do_b, dza, delta = _attn_delta(u, c3, o_raw, lse, doa, T)
    dq, dk, dv, dc3 = _attn_bwd(u, c3, lse, do_b, delta, T)
    dfl, dbias_p = _gates_bwd(dc3.reshape(8, T), u, bias_pad, T, F_CB)

    def post_bwd_body(i, yv, rv, kpv, vv, zv, lnw, lnb, rkv, dobv):
        _, vjp = jax.vjp(_post_math, yv, rv, kpv, vv, zv, lnw, lnb, rkv)
        dy_, dr_, dkp_, dv_, dz_, dlnw, dlnb, drk = vjp(dobv)
        return [dy_, dr_, dkp_, dv_, dz_], [dlnw, dlnb, drk]

    dy_s, dr_p, dkp_p, dv_p, dzb, dlnw_p, dlnb_p, drk_p = _rowcall(
        "rwkv_post_bwd", T, tbh, post_rows + post_par + [_rows(dob, tbh, 1024)], post_bwd_body,
        [(1024, F32)] * 5, [(1, 1024)] * 3)
    dr_s, dlw, dkp_s, dv_s, dan, dbb = _scan_bwd(xs, lw, kp, an, bb, states, invs, dy_s, T)

    def prep_bwd_body(i, xk, xwd, xad, w0, a0, kkw, kaw, w2v, a2v, dlw_, dkp1, dkp2, dan_, dbb_, dr1, dr2, dv1, dv2, dz_):
        _, vjp = jax.vjp(_prep_math, xk, xwd, xad, w0, a0, kkw, kaw, w2v, a2v)
        dxk, dxwd, dxad, dw0, da0, dkk, dka, dw2, da2 = vjp((dlw_, dkp1 + dkp2, dan_, dbb_))
        return [[dr1 + dr2, dxk, dv1 + dv2, dz_], [dxwd, dxad]], [dw0, da0, dkk, dka, dw2, da2]

    cots = [dlw, dkp_s, dkp_p, dan, dbb, dr_s, dr_p, dv_s, dv_p, dzb]
    dxs, dxl, dw0_p, da0_p, dkk_p, dka_p, dw2_p, da2_p = _rowcall(
        "rwkv_prep_bwd", T, tbh, prep_rows + prep_par + [_rows(c_, tbh, 1024) for c_ in cots], prep_bwd_body,
        [(4096, F32), (256, F32)], [(1, 1024)] * 4 + [(LANE, 1024)] * 2)

    def shift_bwd_body(i, dm_, hm, dl_, hl, um, pm, ul, pl_, mm_, ml):
        last = i == T // tbh - 1
        outs, accs = [], []
        for dv_, hv, uv, pv, mv in ((dm_, hm, um, pm, mm_), (dl_, hl, ul, pl_, ml)):
            hv = jnp.where(last, 0.0, hv)
            nxt = pltpu.roll(jnp.concatenate([dv_, hv], axis=0), tbh + SUB - 1, 0)[:tbh]
            pv = jnp.where(i == 0, 0.0, pv)
            prev = pltpu.roll(jnp.concatenate([pv, uv], axis=0), 1, 0)[SUB:]
            outs.append(dv_ * (1.0 - mv) + nxt * mv)
            accs.append(jnp.sum(dv_ * (prev - uv), axis=0, keepdims=True))
        return outs, accs

    def halo_next(arr, w, cb):
        last_blk = T // SUB - 1
        return (arr, (SUB, w), lambda i: (jnp.minimum((i + 1) * (tbh // SUB), last_blk), cb))

    du_b, du_l, dmu_main_p, dmu_lora_p = _rowcall(
        "rwkv_shift_bwd", T, tbh,
        [_rows(dxs, tbh, 4096), halo_next(dxs, 4096, 0), _rows(dxl, tbh, 256), halo_next(dxl, 256, 0),
         _rows(u, tbh, 4096, 1), halo_prev(u, 4096, 1), _rows(u, tbh, 256, LORA_CB), halo_prev(u, 256, LORA_CB),
         _whole(mu_main), _whole(mu_lora)],
        shift_bwd_body, [(4096, BF16), (256, BF16)], [(1, 4096), (1, 256)])

    du = jnp.concatenate([dq.astype(BF16), dk.astype(BF16), dv.astype(BF16), dza.astype(BF16), du_b, dga, dgb,
                          dfl.astype(BF16), jnp.zeros((T, LANE), BF16), du_l], axis=1)
    dw_pad_t = _mm(du, h, ta=True, out_dtype=BF16, tm=1280, tn=1024, name="mm_in_dw")

    ga = _split_rows(dw_pad_t, [(d, lo, pad_col, n) for pad_col, d, lo, n in pieces], NI)
    lora_g = jnp.concatenate([dw2_p[:96].reshape(96, N_DEV, 128).transpose(1, 0, 2),
                              da2_p[:96].reshape(96, N_DEV, 128).transpose(1, 0, 2)], axis=2).astype(BF16)
    gb = jnp.concatenate([dwpf.reshape(1024, N_DEV, 256).transpose(1, 0, 2),
                          dwpr.reshape(1024, N_DEV, 256).transpose(1, 0, 2),
                          dwo.reshape(N_DEV, 2048, 256), lora_g, jnp.zeros((N_DEV, RB - 4192, 256), BF16)], axis=1)

    xx, yy, cc = _place()
    chip_slots = jnp.stack([_slot((*chip, cc)) for chip in _other_chips(xx, yy)]).astype(jnp.int32)
    r1a, r1b = _exchange_pair(ga, gb)
    sa = _pair_add("pair_add_w_in", ga, r1a, chip_slots, 256, by_cols=True)
    sb = _pair_add("pair_add_packed", gb, r1b, chip_slots, 384)
    ga_own = lax.dynamic_index_in_dim(ga, me, 0, keepdims=True)
    gb_own = lax.dynamic_index_in_dim(gb, me, 0, keepdims=True)
    dh, t1a, t1b = _mm(du, w_pad, tb=True, tm=1024, tn=1024, tk=NP // 10, name="mm_in_dx", side=_exchange_first_side(sa, sb))
    core = jnp.stack([cc]).astype(jnp.int32)
    second = _exchange_second_side(_axis_add("axis_add_w_in", sa, t1a, core, 256, by_cols=True),
                                   _axis_add("axis_add_packed", sb, t1b, core, 384))

    def rms_bwd_body(i, xv, g, dhv, doutv):
        _, vjp = jax.vjp(_rms_math, xv, g)
        dx_, dg_ = vjp(dhv)
        return [dx_ + doutv], [dg_]

    grad_x2, dng_p, t2a, t2b = _rowcall(
        "rms_bwd", T, tb, [_rows(x2, tb, D), _whole(norm_gain), _rows(dh, tb, D), _rows(dout, tb, D)],
        rms_bwd_body, [(D, F32)], [(1, D)], side=second)

    dmu = jnp.concatenate([dmu_main_p, dmu_lora_p[:, 0:96], dmu_lora_p[:, 128:224]], axis=1)
    small_parts = [dng_p, dbias_p[:, 0:8], dmu, dw0_p, da0_p, dkk_p, dka_p, drk_p, dlnw_p, dlnb_p, dgf_p, loss_p]
    SR = 128
    small = _pad_cols(jnp.concatenate(small_parts, axis=1), SR * LANE).reshape(SR, LANE)
    rs = _allgather(small, "allgather_small")

    w_in_outs = _adamw("adamw_w_in", w_in[0].T, m_w_in[0].T, v_w_in[0].T, [(ga_own, 1), (r1a, 1), (t1a, 1), (t2a, 1)], 256, by_cols=True)
    g_in, d_in, m_in, v_in = [o.T for o in w_in_outs]
    pk = lambda pf, pr, wo_, w2_, a2_: _pack_b(pf[0], pr[0], wo_[0], w2_[0], a2_[0], RB)
    outs_b = _adamw("adamw_packed", pk(w_proj_fox, w_proj_rwkv, w_out, rwkv_w2, rwkv_a2),
                    pk(m_w_proj_fox, m_w_proj_rwkv, m_w_out, m_rwkv_w2, m_rwkv_a2),
                    pk(v_w_proj_fox, v_w_proj_rwkv, v_w_out, v_rwkv_w2, v_rwkv_a2), [(gb_own, 1), (r1b, 1), (t1b, 1), (t2b, 1)], 384)

    def pack_small(ng, fb, sm, w0, a0, kk_, ka_, rk_, lnw, lnb, fg):
        parts = [ng, fb, sm, w0, a0, kk_, ka_, rk_.reshape(1, 1024), lnw, lnb, fg.reshape(1, D), jnp.zeros((1, 1), F32)]
        return _pad_cols(jnp.concatenate(parts, axis=1), SR * LANE).reshape(SR, LANE)

    outs_s = _adamw("adamw_small",
                    pack_small(norm_gain, fox_forget_bias, rwkv_shift_mix, rwkv_w0, rwkv_a0, rwkv_k_k, rwkv_k_a, rwkv_r_k,
                               rwkv_ln_w, rwkv_ln_b, final_norm_gain),
                    pack_small(m_norm_gain, m_fox_forget_bias, m_rwkv_shift_mix, m_rwkv_w0, m_rwkv_a0, m_rwkv_k_k, m_rwkv_k_a,
                               m_rwkv_r_k, m_rwkv_ln_w, m_rwkv_ln_b, m_final_norm_gain),
                    pack_small(v_norm_gain, v_fox_forget_bias, v_rwkv_shift_mix, v_rwkv_w0, v_rwkv_a0, v_rwkv_k_k, v_rwkv_k_a,
                               v_rwkv_r_k, v_rwkv_ln_w, v_rwkv_ln_b, v_final_norm_gain),
                    [(rs, N_DEV)], SR)

    def unpack_b(pkd):
        return dict(w_proj_fox=pkd[0:1024][None], w_proj_rwkv=pkd[1024:2048][None], w_out=pkd[2048:4096].reshape(1, 256, D),
                    rwkv_w2=pkd[4096:4192, 0:128][None], rwkv_a2=pkd[4096:4192, 128:256][None])

    def unpack_s(pkd):
        flat = pkd.reshape(1, SR * LANE)
        names = [("norm_gain", D), ("fox_forget_bias", 8), ("rwkv_shift_mix", 4288), ("rwkv_w0", 1024), ("rwkv_a0", 1024),
                 ("rwkv_k_k", 1024), ("rwkv_k_a", 1024), ("rwkv_r_k", 1024), ("rwkv_ln_w", 1024), ("rwkv_ln_b", 1024),
                 ("final_norm_gain", D), ("loss", 1)]
        out, off = {}, 0
        for nm, n in names:
            out[nm] = flat[:, off:off + n]
            off += n
        out["rwkv_r_k"] = out["rwkv_r_k"].reshape(1, 16, 64)
        out["final_norm_gain"] = out["final_norm_gain"].reshape(D)
        return out

    order = ["norm_gain", "w_in", "fox_forget_bias", "rwkv_shift_mix", "rwkv_w0", "rwkv_w2", "rwkv_a0", "rwkv_a2", "rwkv_k_k",
             "rwkv_k_a", "rwkv_r_k", "rwkv_ln_w", "rwkv_ln_b", "w_proj_fox", "w_proj_rwkv", "w_out", "final_norm_gain"]
    result = []
    loss = None
    for kind, big in enumerate((g_in, d_in, m_in, v_in)):
        d = {**unpack_b(outs_b[kind]), **unpack_s(outs_s[kind]), "w_in": big[None]}
        if kind == 0:
            loss = d["loss"].reshape(())
        result += [d[n] for n in order]
    return (loss, grad_x2[None], *result)
```

```python
import functools

import jax
import jax.numpy as jnp
from jax import lax
from jax.experimental import pallas as pl
from jax.experimental.pallas import tpu as pltpu

F32 = jnp.float32
BF16 = jnp.bfloat16
HI = lax.Precision.HIGHEST
H3 = lax.Precision.HIGH
MESH = pl.DeviceIdType.MESH

FOX_HD = 128
RW_HD = 64
RMS_EPS = 1e-6
GN_EPS = 64e-5
L2_EPS = 1e-12
ADAM_LR = 0.001
ADAM_B1 = 0.9
ADAM_B2 = 0.999
ADAM_EPS = 1e-08
ADAM_WD = 0.01
ADAM_STEP = 10

LANE = 128
SUB = 8
VMEM_LIMIT = 56 * 1024 * 1024
N_DEV = 8
CHUNK = 128
SCAN_GROUP = 2
SCAN_PAIRS = 2
PS = None
NEG = -1e30


def _scan_shape(T):
    c = min(CHUNK, T)
    return c, min(SCAN_GROUP, T // c)


def _cp(sem=None):
    return pltpu.CompilerParams(dimension_semantics=sem, vmem_limit_bytes=VMEM_LIMIT)


def _sigmoid(x):
    return jax.nn.sigmoid(x)


def _softplus(x):
    return jnp.maximum(x, 0.0) + jnp.log(1.0 + jnp.exp(-jnp.abs(x)))


def _nn(a, b, prec=None):
    return lax.dot_general(a, b, (((1,), (0,)), ((), ())), precision=prec, preferred_element_type=F32)


def _nt(a, b, prec=None):
    return lax.dot_general(a, b, (((1,), (1,)), ((), ())), precision=prec, preferred_element_type=F32)


def _tn(a, b, prec=None):
    return lax.dot_general(a, b, (((0,), (0,)), ((), ())), precision=prec, preferred_element_type=F32)


def _iota2(shape, dim):
    return lax.broadcasted_iota(jnp.int32, shape, dim)


def _seg_sum(x):
    r = _iota2((LANE, LANE), 0) // RW_HD
    c = _iota2((LANE, LANE), 1) // RW_HD
    bd = (r == c).astype(F32)
    parts = [_nn(x[:, j * LANE:(j + 1) * LANE], bd, H3) for j in range(x.shape[1] // LANE)]
    return parts[0] if len(parts) == 1 else jnp.concatenate(parts, axis=1)


def _mm(a, b, *, ta=False, tb=False, out_dtype=F32, tm=1024, tn=1024, tk=None, name, side=None):
    assert not (ta and tb)
    K, M = a.shape if ta else a.shape[::-1]
    N = b.shape[0] if tb else b.shape[1]
    tm, tn = min(tm, M), min(tn, N)
    tk = K if tk is None else tk
    nk = K // tk
    assert M % tm == 0 and N % tn == 0 and K % tk == 0
    gi, gj = M // tm, N // tn
    a_spec = pl.BlockSpec((tk, tm), lambda i, j, k: (k, i)) if ta else pl.BlockSpec((tm, tk), lambda i, j, k: (i, k))
    b_spec = pl.BlockSpec((tn, tk), lambda i, j, k: (j, k)) if tb else pl.BlockSpec((tk, tn), lambda i, j, k: (k, j))
    side_ins, side_outs, side_scr, side_make = side if side is not None else ((), (), (), None)
    n_si, n_so = len(side_ins), len(side_outs)
    n_acc = 0 if nk == 1 else 1

    def body(*refs):
        a_ref, b_ref = refs[:2]
        o_ref = refs[2 + n_si]
        scr = refs[3 + n_si + n_so:]
        k = pl.program_id(2)
        if side_make is not None:
            start, finish = side_make(refs[2:2 + n_si], refs[3 + n_si:3 + n_si + n_so], scr[n_acc:])
            first = jnp.logical_and(jnp.logical_and(pl.program_id(0) == 0, pl.program_id(1) == 0), k == 0)
            last = jnp.logical_and(jnp.logical_and(pl.program_id(0) == gi - 1, pl.program_id(1) == gj - 1), k == nk - 1)
            pl.when(first)(start)
        av = a_ref[...].astype(BF16)
        bv = b_ref[...].astype(BF16)
        p = _tn(av, bv) if ta else _nt(av, bv) if tb else _nn(av, bv)
        if nk == 1:
            o_ref[...] = p.astype(out_dtype)
        else:
            acc_ref = scr[0]

            @pl.when(k == 0)
            def _():
                acc_ref[...] = p

            @pl.when(k > 0)
            def _():
                acc_ref[...] += p

            @pl.when(k == nk - 1)
            def _():
                o_ref[...] = acc_ref[...].astype(out_dtype)
        if side_make is not None:
            pl.when(last)(finish)

    any_spec = pl.BlockSpec(memory_space=pl.ANY)
    res = pl.pallas_call(
        body, name=name,
        out_shape=[jax.ShapeDtypeStruct((M, N), out_dtype)] + list(side_outs),
        grid=(gi, gj, nk),
        in_specs=[a_spec, b_spec] + [any_spec] * n_si,
        out_specs=[pl.BlockSpec((tm, tn), lambda i, j, k: (i, j))] + [any_spec] * n_so,
        scratch_shapes=([] if nk == 1 else [pltpu.VMEM((tm, tn), F32)]) + list(side_scr),
        compiler_params=_cp(("arbitrary",) * 3 if side is not None else ("parallel", "parallel", "arbitrary")),
    )(a, b, *side_ins)
    return res if side is not None else res[0]


def _rows(arr, tb, w, cb=0):
    return (arr, (tb, w), lambda i: (i, cb))


def _whole(arr):
    nd = arr.ndim
    return (arr, arr.shape, lambda i: (0,) * nd)


def _rowcall(name, T, tb, ins, body, outs, accs=(), side=None):
    n_in, n_out, n_acc = len(ins), len(outs), len(accs)
    side_ins, side_outs, side_scr, side_make = side if side is not None else ((), (), (), None)
    n_si = len(side_ins)

    def kern(*refs):
        i = pl.program_id(0)
        if side_make is not None:
            side_refs = refs[n_in + n_si + n_out + n_acc:]
            start, finish = side_make(refs[n_in:n_in + n_si], side_refs[:len(side_outs)], side_refs[len(side_outs):])
            pl.when(i == 0)(start)
            refs = refs[:n_in] + refs[n_in + n_si:]
        vals = [r[...] for r in refs[:n_in]]
        ro, ao = body(i, *vals)
        for r, v in zip(refs[n_in:n_in + n_out], ro):
            if isinstance(v, (list, tuple)):
                off = 0
                for piece in v:
                    w = piece.shape[1]
                    r[:, off:off + w] = piece.astype(r.dtype)
                    off += w
            else:
                r[...] = v.astype(r.dtype)
        if accs:
            acc_refs = refs[n_in + n_out:n_in + n_out + n_acc]

            @pl.when(i == 0)
            def _():
                for r in acc_refs:
                    r[...] = jnp.zeros(r.shape, F32)

            for r, v in zip(acc_refs, ao):
                r[...] += v
        if side_make is not None:
            pl.when(i == T // tb - 1)(finish)

    any_spec = pl.BlockSpec(memory_space=pl.ANY)
    out_shape = [jax.ShapeDtypeStruct((T, w), dt) for (w, dt) in outs] + [jax.ShapeDtypeStruct(s, F32) for s in accs]
    out_specs = [pl.BlockSpec((tb, w), lambda i: (i, 0)) for (w, dt) in outs] + [pl.BlockSpec(s, lambda i: (0, 0)) for s in accs]
    res = pl.pallas_call(
        kern, name=name,
        out_shape=out_shape + list(side_outs),
        grid=(T // tb,),
        in_specs=[pl.BlockSpec(bs, im) for (_, bs, im) in ins] + [any_spec] * n_si,
        out_specs=out_specs + [any_spec] * len(side_outs),
        scratch_shapes=list(side_scr),
        compiler_params=_cp(("arbitrary",)),
    )(*[a for (a, _, _) in ins], *side_ins)
    return res


def _rms_math(x, g):
    r = lax.rsqrt(jnp.mean(x * x, axis=-1, keepdims=True) + RMS_EPS)
    return x * r * g


def _merge_math(ga, gb, pa, pb):
    return _sigmoid(ga) * pa + _sigmoid(gb) * pb


def _prep_math(xk, xwd, xad, w0, a0, kk_w, ka_w, w2p, a2p):
    z = w0 + _nn(jnp.tanh(xwd), w2p, H3)
    w = -_softplus(-z) - 0.5
    lw = -jnp.exp(w)
    ag = _sigmoid(a0 + _nn(xad, a2p, H3))
    p = xk * kk_w
    n = jnp.maximum(jnp.sqrt(_seg_sum(p * p)), L2_EPS)
    kk = p / n
    kp = xk * (1.0 + (ag - 1.0) * ka_w)
    return lw, kp, -kk, kk * ag


def _post_math(y, r, kp, v, z, lnw, lnb, rk):
    inv = 1.0 / RW_HD
    mu = _seg_sum(y) * inv
    d = y - mu
    var = _seg_sum(d * d) * inv
    yn = d * lax.rsqrt(var + GN_EPS) * lnw + lnb
    bonus = _seg_sum(r * kp * rk) * v
    return (yn + bonus) * (z * _sigmoid(z))


@functools.partial(jax.custom_vjp, nondiff_argnums=(1,))
def _unit_inverses(ms, depth):
    eye = (_iota2(ms[0].shape, 0) == _iota2(ms[0].shape, 1)).astype(F32)
    half = ms[0].shape[1]
    mp = [_nn(m, m, PS) for m in ms]
    inv = [eye + m for m in ms]
    n = 2
    while n < depth:
        last = 2 * n >= depth
        for i in range(len(ms)):
            if last:
                inv[i] = inv[i] + _nn(mp[i], inv[i], PS)
            else:
                z = _nn(mp[i], jnp.concatenate([mp[i], inv[i]], axis=1), PS)
                mp[i], inv[i] = z[:, :half], inv[i] + z[:, half:]
        n *= 2
    return tuple(inv)


def _unit_inverses_fwd(ms, depth):
    inv = _unit_inverses(ms, depth)
    return inv, inv


def _unit_inverses_bwd(depth, inv, cts):
    left = [_tn(t, g, PS) for t, g in zip(inv, cts)]
    return (tuple(_nt(l, t, PS) for l, t in zip(left, inv)),)


_unit_inverses.defvjp(_unit_inverses_fwd, _unit_inverses_bwd)


@jax.custom_vjp
def _known_inverses(ms, inv):
    return inv


def _known_inverses_fwd(ms, inv):
    return inv, inv


def _known_inverses_bwd(inv, cts):
    dms = _unit_inverses_bwd(None, inv, cts)[0]
    return dms, tuple(jnp.zeros_like(t) for t in inv)


_known_inverses.defvjp(_known_inverses_fwd, _known_inverses_bwd)


def _scan_group(s0s, *flat, known_inv=None, with_inv=False):
    P = len(s0s)
    G = len(flat) // (6 * P)
    ch = [flat[6 * i:6 * i + 6] for i in range(P * G)]
    C = ch[0][0].shape[0]
    C2 = 2 * C
    cat = jnp.concatenate
    m0 = _iota2((1, LANE), 1) < RW_HD
    mask0 = m0.astype(F32)
    mask1 = 1.0 - mask0
    r2 = _iota2((C2, C2), 0)
    c2 = _iota2((C2, C2), 1)
    dist = r2 - c2
    in_head = dist <= r2 % C
    lower = (_iota2((C, C), 0) >= _iota2((C, C), 1)).astype(F32)
    bd = (_iota2((LANE, LANE), 0) // RW_HD) == (_iota2((LANE, LANE), 1) // RW_HD)

    def tri(m, strict):
        return jnp.where(dist > 0 if strict else dist >= 0, jnp.where(in_head, m, 0.0), 0.0)

    def sel(z):
        return jnp.where(m0, z[:C], z[C:])

    gs = [_nn(lower, c[1], HI) for c in ch]
    pre = []
    for (r, lw, k, v, a, b), g in zip(ch, gs):
        g_end = jnp.sum(lw, axis=0, keepdims=True)
        gm = g - jnp.sum(lw[:C // 2], axis=0, keepdims=True)
        en = jnp.exp(-gm)
        ec = jnp.exp(g_end - g)
        pre.append(dict(at=a * jnp.exp(g - lw), rt=r * jnp.exp(g), am=a * jnp.exp(gm - lw), rm=r * jnp.exp(gm),
                        bt=b * en, kt=k * en, bh=b * ec, kh=k * ec, dec=jnp.exp(g_end), v=v))
    grams = [_nt(cat([p["am"] * mask0, p["am"] * mask1, p["rm"] * mask0, p["rm"] * mask1], axis=0),
                 cat([p["bt"], p["bt"], p["kt"], p["kt"]], axis=0), PS) for p in pre]
    mab = tuple(tri(gm[:C2, :C2], True) for gm in grams)
    tinv = _unit_inverses(mab, C) if known_inv is None else _known_inverses(mab, known_inv)
    xv =[sel(_nn(tri(gm[:C2, C2:], True), cat([p["v"], p["v"]], axis=0), PS)) for gm, p in zip(grams, pre)]
    ys, s = [None] * (P * G), list(s0s)
    for i in range(G):
        for q in range(P):
            n = q * G + i
            p, gm = pre[n], grams[n]
            sx = _nt(cat([p["at"], p["rt"]], axis=0), s[q], PS)
            x = sx[:C] + xv[n]
            u = sel(_nn(tinv[n], cat([x, x], axis=0), PS))
            v = p["v"]
            ys[n] = sx[C:] + sel(_nn(cat([tri(gm[C2:, :C2], False), tri(gm[C2:, C2:], False)], axis=1),
                                     cat([u, u, v, v], axis=0), PS))
            s[q] = s[q] * p["dec"] + jnp.where(bd, _tn(cat([u, v], axis=0), cat([p["bh"], p["kh"]], axis=0), PS), 0.0)
    return (tuple(ys), tuple(s), tinv) if with_inv else (tuple(ys), tuple(s))


def _scan_fwd(xs, lw, kp, an, bb, T):
    C, G = _scan_shape(T)
    P = SCAN_PAIRS
    nc = T // (C * G)
    npair = 1024 // LANE

    def kern(r_ref, lw_ref, k_ref, v_ref, a_ref, b_ref, y_ref, st_ref, inv_ref, s_scr):
        n = pl.program_id(1)

        @pl.when(n == 0)
        def _():
            s_scr[...] = jnp.zeros(s_scr.shape, F32)

        st_ref[0] = s_scr[...]
        ins = (r_ref, lw_ref, k_ref, v_ref, a_ref, b_ref)
        ys, s1, inv = _scan_group(
            tuple(s_scr[q] for q in range(P)),
            *[ref[i * C:(i + 1) * C, q * LANE:(q + 1) * LANE] for q in range(P) for i in range(G) for ref in ins], with_inv=True)
        for q in range(P):
            for i in range(G):
                y_ref[i * C:(i + 1) * C, q * LANE:(q + 1) * LANE] = ys[q * G + i]
                inv_ref[0, 0, q * G + i] = inv[q * G + i]
            s_scr[q] = s1[q]

    def col(off):
        return pl.BlockSpec((C * G, P * LANE), lambda p, n: (n, off // P + p))

    return pl.pallas_call(
        kern, name="rwkv_scan_fwd",
        out_shape=[jax.ShapeDtypeStruct((T, 1024), F32), jax.ShapeDtypeStruct((nc, npair, LANE, LANE), F32),
                   jax.ShapeDtypeStruct((nc, npair // P, P * G, 2 * C, 2 * C), F32)],
        grid=(npair // P, nc),
        in_specs=[col(0), col(0), col(0), col(16), col(0), col(0)],
        out_specs=[col(0), pl.BlockSpec((1, P, LANE, LANE), lambda p, n: (n, p, 0, 0)),
                   pl.BlockSpec((1, 1, P * G, 2 * C, 2 * C), lambda p, n: (n, p, 0, 0, 0))],
        scratch_shapes=[pltpu.VMEM((P, LANE, LANE), F32)],
        compiler_params=_cp(("parallel", "arbitrary")),
    )(xs, lw, kp, xs, an, bb)


def _scan_bwd(xs, lw, kp, an, bb, states, invs, dy, T):
    C, G = _scan_shape(T)
    P = SCAN_PAIRS
    nc = T // (C * G)
    npair = 1024 // LANE

    def kern(r_ref, lw_ref, k_ref, v_ref, a_ref, b_ref, st_ref, inv_ref, dy_ref, dr_ref, dlw_ref, dk_ref, dv_ref, da_ref, db_ref,
             ds_scr):
        n = pl.program_id(1)

        @pl.when(n == 0)
        def _():
            ds_scr[...] = jnp.zeros(ds_scr.shape, F32)

        ins = (r_ref, lw_ref, k_ref, v_ref, a_ref, b_ref)
        units = [(q, i) for q in range(P) for i in range(G)]
        known = tuple(inv_ref[0, 0, q * G + i] for q, i in units)
        _, vjp = jax.vjp(functools.partial(_scan_group, known_inv=known), tuple(st_ref[0, q] for q in range(P)),
                         *[ref[i * C:(i + 1) * C, q * LANE:(q + 1) * LANE] for q, i in units for ref in ins])
        grads = vjp((tuple(dy_ref[i * C:(i + 1) * C, q * LANE:(q + 1) * LANE] for q, i in units),
                     tuple(ds_scr[q] for q in range(P))))
        for q in range(P):
            ds_scr[q] = grads[0][q]
        outs = (dr_ref, dlw_ref, dk_ref, dv_ref, da_ref, db_ref)
        for n_, (q, i) in enumerate(units):
            for t, ref in enumerate(outs):
                ref[i * C:(i + 1) * C, q * LANE:(q + 1) * LANE] = grads[1 + 6 * n_ + t]

    def col(off):
        return pl.BlockSpec((C * G, P * LANE), lambda p, n: (nc - 1 - n, off // P + p))

    return pl.pallas_call(
        kern, name="rwkv_scan_bwd",
        out_shape=[jax.ShapeDtypeStruct((T, 1024), F32)] * 6,
        grid=(npair // P, nc),
        in_specs=[col(0), col(0), col(0), col(16), col(0), col(0),
                  pl.BlockSpec((1, P, LANE, LANE), lambda p, n: (nc - 1 - n, p, 0, 0)),
                  pl.BlockSpec((1, 1, P * G, 2 * C, 2 * C), lambda p, n: (nc - 1 - n, p, 0, 0, 0)), col(0)],
        out_specs=[col(0)] * 6,
        scratch_shapes=[pltpu.VMEM((P, LANE, LANE), F32)],
        compiler_params=_cp(("parallel", "arbitrary")),
    )(xs, lw, kp, xs, an, bb, states, invs, dy)


def _gates_fwd(u, bias_pad, T, f_cb):
    nb = T // LANE

    def kern(f_ref, b_ref, c_ref):
        x = f_ref[...] + b_ref[...]
        lf = jnp.minimum(x, 0.0) - jnp.log(1.0 + jnp.exp(-jnp.abs(x)))
        lft = lf.T
        ut = (_iota2((LANE, LANE), 0) <= _iota2((LANE, LANE), 1)).astype(F32)
        carry = jnp.zeros((LANE, 1), F32)
        for blk in range(nb):
            seg = lft[:, blk * LANE:(blk + 1) * LANE]
            cs = _nn(seg, ut, HI) + carry
            c_ref[:, blk * LANE:(blk + 1) * LANE] = cs[:SUB, :]
            carry = carry + jnp.sum(seg, axis=1, keepdims=True)

    return pl.pallas_call(
        kern, name="fox_gates_fwd",
        out_shape=jax.ShapeDtypeStruct((SUB, T), F32),
        grid=(1,),
        in_specs=[pl.BlockSpec((T, LANE), lambda i: (0, f_cb)), pl.BlockSpec((1, LANE), lambda i: (0, 0))],
        out_specs=pl.BlockSpec((SUB, T), lambda i: (0, 0)),
        compiler_params=_cp(("arbitrary",)),
    )(u, bias_pad)


def _gates_bwd(dc, u, bias_pad, T, f_cb):
    nb = T // LANE

    def kern(dc_ref, f_ref, b_ref, dfl_ref, db_ref):
        dcv = jnp.concatenate([dc_ref[...], jnp.zeros((LANE - SUB, T), F32)], axis=0)
        lt = (_iota2((LANE, LANE), 0) >= _iota2((LANE, LANE), 1)).astype(F32)
        carry = jnp.zeros((LANE, 1), F32)
        pieces = [None] * nb
        for blk in range(nb - 1, -1, -1):
            seg = dcv[:, blk * LANE:(blk + 1) * LANE]
            pieces[blk] = _nn(seg, lt, HI) + carry
            carry = carry + jnp.sum(seg, axis=1, keepdims=True)
        dlf = (pieces[0] if nb == 1 else jnp.concatenate(pieces, axis=1)).T
        x = f_ref[...] + b_ref[...]
        dfl = dlf * _sigmoid(-x)
        dfl_ref[...] = dfl
        db_ref[...] = jnp.sum(dfl, axis=0, keepdims=True)

    return pl.pallas_call(
        kern, name="fox_gates_bwd",
        out_shape=[jax.ShapeDtypeStruct((T, LANE), F32), jax.ShapeDtypeStruct((1, LANE), F32)],
        grid=(1,),
        in_specs=[pl.BlockSpec((SUB, T), lambda i: (0, 0)), pl.BlockSpec((T, LANE), lambda i: (0, f_cb)),
                  pl.BlockSpec((1, LANE), lambda i: (0, 0))],
        out_specs=[pl.BlockSpec((T, LANE), lambda i: (0, 0)), pl.BlockSpec((1, LANE), lambda i: (0, 0))],
        compiler_params=_cp(("arbitrary",)),
    )(dc, u, bias_pad)


ATTN_HEADS = 4


def _attn_block(T):
    return 256 if T % 256 == 0 and T >= 512 else 128


def _attn_fwd(u, c3, T):
    H, HP = 8, ATTN_HEADS
    bq = _attn_block(T)
    nq = T // bq
    scale = FOX_HD ** -0.5
    lanes = [slice(h * LANE, (h + 1) * LANE) for h in range(HP)]

    def kern(q_ref, k_ref, v_ref, z_ref, cq_ref, ck_ref, o_ref, oa_ref, lse_ref):
        i = pl.program_id(1)
        q = [(q_ref[:, ln] * scale).astype(BF16) for ln in lanes]
        c0 = [cq_ref[h][:, 0:1] for h in range(HP)]

        def step(j, carry, diagonal=False):
            off = pl.multiple_of(j * bq, bq)
            s = [_nt(q[h], k_ref[pl.ds(off, bq), lanes[h]].astype(BF16)) + (c0[h] - ck_ref[h, :, pl.ds(off, bq)])
                 for h in range(HP)]
            ps, out = [], []
            for h in range(HP):
                m, l, acc = carry[h]
                sh = s[h]
                if diagonal:
                    sh = jnp.where(_iota2((bq, bq), 1) <= _iota2((bq, bq), 0), sh, NEG)
                m_new = jnp.maximum(m, jnp.max(sh, axis=1, keepdims=True))
                p = jnp.exp(sh - m_new)
                alpha = jnp.exp(m - m_new)
                ps.append(p.astype(BF16))
                out.append((m_new, alpha * l + jnp.sum(p, axis=1, keepdims=True), alpha * acc))
            return tuple((m, l, acc + _nn(ps[h], v_ref[pl.ds(off, bq), lanes[h]].astype(BF16)))
                         for h, (m, l, acc) in enumerate(out))

        init = tuple((jnp.full((bq, 1), NEG, F32), jnp.zeros((bq, 1), F32), jnp.zeros((bq, FOX_HD), F32)) for _ in range(HP))
        res = step(i, lax.fori_loop(0, i, step, init), diagonal=True)
        for h, (m, l, acc) in enumerate(res):
            o = acc / l
            z = z_ref[:, lanes[h]]
            o_ref[:, lanes[h]] = o
            oa_ref[:, lanes[h]] = (o * z * _sigmoid(z)).astype(BF16)
            lse_ref[h] = m + jnp.log(l)

    W = HP * LANE
    return pl.pallas_call(
        kern, name="fox_attn_fwd",
        out_shape=[jax.ShapeDtypeStruct((T, 1024), F32), jax.ShapeDtypeStruct((T, 1024), BF16),
                   jax.ShapeDtypeStruct((H, T, 1), F32)],
        grid=(H // HP, nq),
        in_specs=[pl.BlockSpec((bq, W), lambda g, i: (i, g)),
                  pl.BlockSpec((T, W), lambda g, i: (0, 8 // HP + g)),
                  pl.BlockSpec((T, W), lambda g, i: (0, 16 // HP + g)),
                  pl.BlockSpec((bq, W), lambda g, i: (i, 24 // HP + g)),
                  pl.BlockSpec((HP, 1, bq), lambda g, i: (g, 0, i)),
                  pl.BlockSpec((HP, 1, T), lambda g, i: (g, 0, 0))],
        out_specs=[pl.BlockSpec((bq, W), lambda g, i: (i, g)),
                   pl.BlockSpec((bq, W), lambda g, i: (i, g)),
                   pl.BlockSpec((HP, bq, 1), lambda g, i: (g, i, 0))],
        compiler_params=_cp(("parallel", "arbitrary")),
    )(u, u, u, u, c3, c3)


def _attn_probs(s, lse_i, diagonal):
    if not diagonal:
        return jnp.exp(s - lse_i)
    keep = _iota2(s.shape, 1) <= _iota2(s.shape, 0)
    return jnp.where(keep, jnp.exp(jnp.where(keep, s, NEG) - lse_i), 0.0)


def _attn_delta(u, c3, o, lse, doa, T):
    H, HP = 8, ATTN_HEADS
    bq = _attn_block(T)
    nq = T // bq
    scale = FOX_HD ** -0.5
    lanes = [slice(h * LANE, (h + 1) * LANE) for h in range(HP)]

    def kern(q_ref, k_ref, v_ref, z_ref, c_ref, o_ref, lse_ref, doa_ref, do_ref, dz_ref, dl_ref):
        i = pl.program_id(1)
        z = z_ref[...]
        sg = _sigmoid(z)
        dov = doa_ref[...]
        do_all = (dov * z * sg).astype(BF16)
        do_ref[...] = do_all
        dz_ref[...] = dov * o_ref[...] * (sg * (1.0 + z * (1.0 - sg)))
        qs = [(q_ref[:, ln] * scale).astype(BF16) for ln in lanes]
        dob = [do_all[:, ln] for ln in lanes]
        ioff = pl.multiple_of(i * bq, bq)
        c0 = [c_ref[h, :, pl.ds(ioff, bq)][:, 0:1] for h in range(HP)]
        lse_i = [lse_ref[h] for h in range(HP)]

        def step(j, acc, diagonal=False):
            off = pl.multiple_of(j * bq, bq)
            s = [_nt(qs[h], k_ref[pl.ds(off, bq), lanes[h]].astype(BF16)) + (c0[h] - c_ref[h, :, pl.ds(off, bq)])
                 for h in range(HP)]
            dp = [_nt(dob[h], v_ref[pl.ds(off, bq), lanes[h]].astype(BF16)) for h in range(HP)]
            return tuple(acc[h] + jnp.sum(_attn_probs(s[h], lse_i[h], diagonal) * dp[h], axis=1, keepdims=True)
                         for h in range(HP))

        init = tuple(jnp.zeros((bq, 1), F32) for _ in range(HP))
        res = step(i, lax.fori_loop(0, i, step, init), diagonal=True)
        for h in range(HP):
            dl_ref[h] = res[h]

    W = HP * LANE
    full = lambda cb: pl.BlockSpec((T, W), lambda g, i: (0, cb // HP + g))
    blk = lambda cb: pl.BlockSpec((bq, W), lambda g, i: (i, cb // HP + g))
    return pl.pallas_call(
        kern, name="fox_attn_delta",
        out_shape=[jax.ShapeDtypeStruct((T, 1024), BF16), jax.ShapeDtypeStruct((T, 1024), F32), jax.ShapeDtypeStruct((H, T, 1), F32)],
        grid=(H // HP, nq),
        in_specs=[blk(0), full(8), full(16), blk(24),
                  pl.BlockSpec((HP, 1, T), lambda g, i: (g, 0, 0)),
                  blk(0),
                  pl.BlockSpec((HP, bq, 1), lambda g, i: (g, i, 0)),
                  blk(0)],
        out_specs=[blk(0), blk(0), pl.BlockSpec((HP, bq, 1), lambda g, i: (g, i, 0))],
        compiler_params=_cp(("parallel", "arbitrary")),
    )(u, u, u, u, c3, o, lse, doa)


def _attn_bwd(u, c3, lse, do, delta, T):
    H, HP = 8, ATTN_HEADS
    bq = _attn_block(T)
    nq = T // bq
    scale = FOX_HD ** -0.5
    lanes = [slice(h * LANE, (h + 1) * LANE) for h in range(HP)]

    def kern(q_ref, k_ref, v_ref, c_ref, lse_ref, do_ref, dl_ref, dq_ref, dk_ref, dv_ref, dc_ref):
        j = pl.program_id(1)

        @pl.when(j == 0)
        def _():
            dq_ref[...] = jnp.zeros(dq_ref.shape, F32)

        kj = [k_ref[:, ln].astype(BF16) for ln in lanes]
        vj = [v_ref[:, ln].astype(BF16) for ln in lanes]
        joff = pl.multiple_of(j * bq, bq)
        ck = [c_ref[h, :, pl.ds(joff, bq)] for h in range(HP)]

        def step(i, carry, diagonal=False):
            off = pl.multiple_of(i * bq, bq)
            qs = [(q_ref[pl.ds(off, bq), ln] * scale).astype(BF16) for ln in lanes]
            dob = [do_ref[pl.ds(off, bq), ln] for ln in lanes]
            s = [_nt(qs[h], kj[h]) + (c_ref[h, :, pl.ds(off, bq)][:, 0:1] - ck[h]) for h in range(HP)]
            dp = [_nt(dob[h], vj[h]) for h in range(HP)]
            pb, dsb, dcs = [], [], []
            for h in range(HP):
                p = _attn_probs(s[h], lse_ref[h, pl.ds(off, bq), :], diagonal)
                ds = p * (dp[h] - dl_ref[h, pl.ds(off, bq), :])
                pb.append(p.astype(BF16))
                dsb.append(ds.astype(BF16))
                dcs.append(jnp.sum(ds, axis=0, keepdims=True))
            out = []
            for h, (dk, dv, dc) in enumerate(carry):
                dq_ref[pl.ds(off, bq), lanes[h]] += _nn(dsb[h], kj[h]) * scale
                out.append((dk + _tn(dsb[h], qs[h]), dv + _tn(pb[h], dob[h]), dc - dcs[h]))
            return tuple(out)

        init = tuple((jnp.zeros((bq, FOX_HD), F32), jnp.zeros((bq, FOX_HD), F32), jnp.zeros((1, bq), F32)) for _ in range(HP))
        res = lax.fori_loop(j + 1, nq, step, step(j, init, diagonal=True))
        for h, (dk, dv, dc) in enumerate(res):
            dk_ref[:, lanes[h]] = dk
            dv_ref[:, lanes[h]] = dv
            dc_ref[h] = dc

    W = HP * LANE
    full = lambda cb: pl.BlockSpec((T, W), lambda g, j: (0, cb // HP + g))
    blk = lambda cb: pl.BlockSpec((bq, W), lambda g, j: (j, cb // HP + g))
    col1 = pl.BlockSpec((HP, T, 1), lambda g, j: (g, 0, 0))
    return pl.pallas_call(
        kern, name="fox_attn_bwd",
        out_shape=[jax.ShapeDtypeStruct((T, 1024), F32)] * 3 + [jax.ShapeDtypeStruct((H, 1, T), F32)],
        grid=(H // HP, nq),
        in_specs=[full(0), blk(8), blk(16), pl.BlockSpec((HP, 1, T), lambda g, j: (g, 0, 0)), col1, full(0), col1],
        out_specs=[full(0), blk(0), blk(0), pl.BlockSpec((HP, 1, bq), lambda g, j: (g, 0, j))],
        compiler_params=_cp(("parallel", "arbitrary")),
    )(u, u, u, c3, lse, do, delta)


def _place():
    return lax.axis_index("x"), lax.axis_index("y"), lax.axis_index("c")


def _slot(p):
    return 4 * p[0] + 2 * p[1] + p[2]


def _other_chips(x, y):
    return [(1 - x, y), (x, 1 - y), (1 - x, 1 - y)]


def _allgather_steps(in_refs, out_refs, scratch):
    (src,), (dst,) = in_refs, out_refs
    send_sems, recv_sems, local_sem = scratch
    x, y, c = _place()
    me, sibling = (x, y, c), (x, y, 1 - c)
    chips = _other_chips(x, y)

    def copy(k, block, to, from_input=False):
        d = dst.at[_slot(block)]
        return pltpu.make_async_remote_copy(
            src_ref=src if from_input else d, dst_ref=d, send_sem=send_sems.at[k], recv_sem=recv_sems.at[k],
            device_id=to, device_id_type=MESH)

    def first_copies():
        return [copy(0, me, sibling, True)] + [copy(1 + j, me, (*chip, c), True) for j, chip in enumerate(chips)]

    def start():
        pltpu.make_async_copy(src, dst.at[_slot(me)], local_sem).start()
        for cp in first_copies():
            cp.start()

    def finish():
        passed = []
        for j, chip in enumerate(chips):
            copy(1 + j, (*chip, c), me).wait_recv()
            passed.append(copy(4 + j, (*chip, c), sibling))
            passed[-1].start()
        copy(0, sibling, me).wait_recv()
        for j, chip in enumerate(chips):
            copy(4 + j, (*chip, 1 - c), me).wait_recv()
        for cp in first_copies() + passed:
            cp.wait_send()
        pltpu.make_async_copy(src, dst.at[_slot(me)], local_sem).wait()

    return start, finish


def _allgather_relay_steps(in_refs, out_refs, scratch):
    (src,), (dst,) = in_refs, out_refs
    send_sems, recv_sems, local_sem = scratch
    x, y, c = _place()
    me, sibling = (x, y, c), (x, y, 1 - c)
    x_nbr, y_nbr, diag = (1 - x, y, c), (x, 1 - y, c), (1 - x, 1 - y, c)
    flip = lambda a, bit: a + bit - 2 * a * bit
    relay_from = (flip(x, 1 - c), flip(y, c), c)
    relay_to = (flip(x, c), flip(y, 1 - c), c)

    def copy(k, block, to, from_input=False):
        d = dst.at[_slot(block)]
        return pltpu.make_async_remote_copy(
            src_ref=src if from_input else d, dst_ref=d, send_sem=send_sems.at[k], recv_sem=recv_sems.at[k],
            device_id=to, device_id_type=MESH)

    def first_copies():
        return [copy(0, me, sibling, True), copy(1, me, x_nbr, True), copy(2, me, y_nbr, True)]

    def other(block):
        return block[:2] + (1 - c,)

    def start():
        pltpu.make_async_copy(src, dst.at[_slot(me)], local_sem).start()
        for cp in first_copies():
            cp.start()

    def finish():
        copy(1, x_nbr, me).wait_recv()
        copy(2, y_nbr, me).wait_recv()
        later = [copy(3, relay_from, relay_to), copy(4, x_nbr, sibling), copy(5, y_nbr, sibling)]
        for cp in later:
            cp.start()
        copy(3, diag, me).wait_recv()
        later.append(copy(6, diag, sibling))
        later[-1].start()
        copy(0, sibling, me).wait_recv()
        for k, block in ((4, x_nbr), (5, y_nbr), (6, diag)):
            copy(k, other(block), me).wait_recv()
        for cp in first_copies() + later:
            cp.wait_send()
        pltpu.make_async_copy(src, dst.at[_slot(me)], local_sem).wait()

    return start, finish


def _allgather_side(a, relay=False):
    return ((a,), (jax.ShapeDtypeStruct((N_DEV,) + a.shape, a.dtype),),
            (pltpu.SemaphoreType.DMA((7,)), pltpu.SemaphoreType.DMA((7,)), pltpu.SemaphoreType.DMA),
            _allgather_relay_steps if relay else _allgather_steps)


def _allgather(a, name, relay=False):
    ins, outs, scratch, make = _allgather_side(a, relay)

    def body(a_ref, o_ref, *scr):
        start, finish = make((a_ref,), (o_ref,), scr)
        start()
        finish()

    any_spec = pl.BlockSpec(memory_space=pl.ANY)
    return pl.pallas_call(body, name=name, out_shape=outs[0], in_specs=[any_spec], out_specs=any_spec,
                          scratch_shapes=list(scratch))(a)


def _exchange_pair(ga, gb):
    def body(ga_ref, gb_ref, ra_ref, rb_ref, send_sems, recv_sems):
        x, y, c = _place()
        sibling = (x, y, 1 - c)
        slots = [_slot(sibling)] + [_slot((*chip, 1 - c)) for chip in _other_chips(x, y)]
        cps = []
        for t, (src, dst) in enumerate(((ga_ref, ra_ref), (gb_ref, rb_ref))):
            for k, ps in enumerate(slots):
                cps.append(pltpu.make_async_remote_copy(
                    src_ref=src.at[ps], dst_ref=dst.at[k], send_sem=send_sems.at[t, k], recv_sem=recv_sems.at[t, k],
                    device_id=sibling, device_id_type=MESH))
        for cp in cps:
            cp.start()
        for cp in cps:
            cp.wait()

    any_spec = pl.BlockSpec(memory_space=pl.ANY)
    return pl.pallas_call(
        body, name="exchange_pair",
        out_shape=[jax.ShapeDtypeStruct((4,) + ga.shape[1:], ga.dtype), jax.ShapeDtypeStruct((4,) + gb.shape[1:], gb.dtype)],
        in_specs=[any_spec] * 2,
        out_specs=[any_spec] * 2,
        scratch_shapes=[pltpu.SemaphoreType.DMA((2, 4)), pltpu.SemaphoreType.DMA((2, 4))],
    )(ga, gb)


def _tiling(R, Cc, tile, by_cols):
    if by_cols:
        assert Cc % tile == 0
        return Cc // tile, (R, tile), lambda lead, i: (lead, 0, i)
    assert R % tile == 0
    return R // tile, (tile, Cc), lambda lead, i: (lead, i, 0)


def _pair_add(name, g, r1, slots, tile, by_cols=False):
    _, R, Cc = g.shape
    steps, blk, at = _tiling(R, Cc, tile, by_cols)

    def kern(s_ref, a_ref, b_ref, o_ref):
        o_ref[...] = (a_ref[...].astype(F32) + b_ref[...].astype(F32)).astype(o_ref.dtype)

    return pl.pallas_call(
        kern, name=name,
        out_shape=jax.ShapeDtypeStruct((3, R, Cc), BF16),
        grid_spec=pltpu.PrefetchScalarGridSpec(
            num_scalar_prefetch=1, grid=(3, steps),
            in_specs=[pl.BlockSpec((1,) + blk, lambda j, i, s: at(s[j], i)),
                      pl.BlockSpec((1,) + blk, lambda j, i, s: at(1 + j, i))],
            out_specs=pl.BlockSpec((1,) + blk, lambda j, i, s: at(j, i))),
        compiler_params=_cp(("arbitrary", "arbitrary")),
    )(slots, g, r1)


def _axis_neighbours():
    x, y, c = _place()
    flip = lambda a, bit: a + bit - 2 * a * bit
    return (flip(x, c), flip(y, 1 - c), c), (flip(x, 1 - c), flip(y, c), c), c


def _exchange_first_steps(in_refs, out_refs, scratch):
    pairs = list(zip(in_refs, out_refs))
    send_sems, recv_sems = scratch
    first, _, c = _axis_neighbours()

    def copies():
        return [pltpu.make_async_remote_copy(
            src_ref=src.at[j], dst_ref=dst.at[k], send_sem=send_sems.at[t, k], recv_sem=recv_sems.at[t, k],
            device_id=first, device_id_type=MESH)
            for t, (src, dst) in enumerate(pairs) for k, j in enumerate((1 - c, 2))]

    def start():
        for cp in copies():
            cp.start()

    def finish():
        for cp in copies():
            cp.wait()

    return start, finish


def _exchange_first_side(sa, sb):
    return ((sa, sb), (jax.ShapeDtypeStruct((2,) + sa.shape[1:], sa.dtype), jax.ShapeDtypeStruct((2,) + sb.shape[1:], sb.dtype)),
            (pltpu.SemaphoreType.DMA((2, 2)), pltpu.SemaphoreType.DMA((2, 2))), _exchange_first_steps)


def _axis_add(name, s, t1, core, tile, by_cols=False):
    _, R, Cc = s.shape
    steps, blk, at = _tiling(R, Cc, tile, by_cols)

    def kern(c_ref, a_ref, b_ref, o_ref):
        o_ref[...] = (a_ref[...].astype(F32) + b_ref[...].astype(F32)).astype(o_ref.dtype)

    return pl.pallas_call(
        kern, name=name,
        out_shape=jax.ShapeDtypeStruct((1, R, Cc), BF16),
        grid_spec=pltpu.PrefetchScalarGridSpec(
            num_scalar_prefetch=1, grid=(steps,),
            in_specs=[pl.BlockSpec((1,) + blk, lambda i, cr: at(cr[0], i)),
                      pl.BlockSpec((1,) + blk, lambda i, cr: at(1, i))],
            out_specs=pl.BlockSpec((1,) + blk, lambda i, cr: at(0, i))),
        compiler_params=_cp(("arbitrary",)),
    )(core, s, t1)


def _exchange_second_steps(in_refs, out_refs, scratch):
    send_sems, recv_sems = scratch
    _, second, _ = _axis_neighbours()

    def copies():
        return [pltpu.make_async_remote_copy(src_ref=src, dst_ref=dst, send_sem=send_sems.at[t], recv_sem=recv_sems.at[t],
                                             device_id=second, device_id_type=MESH)
                for t, (src, dst) in enumerate(zip(in_refs, out_refs))]

    def start():
        for cp in copies():
            cp.start()

    def finish():
        for cp in copies():
            cp.wait()

    return start, finish


def _exchange_second_side(pa, pb):
    return ((pa, pb), (jax.ShapeDtypeStruct(pa.shape, pa.dtype), jax.ShapeDtypeStruct(pb.shape, pb.dtype)),
            (pltpu.SemaphoreType.DMA((2,)), pltpu.SemaphoreType.DMA((2,))), _exchange_second_steps)


def _adamw(name, w, m, v, parts, tile, by_cols=False):
    R, Cc = w.shape
    steps, blk_shape, at = _tiling(R, Cc, tile, by_cols)
    n_parts = len(parts)

    def kern(*refs):
        w_ref, m_ref, v_ref = refs[:3]
        g = None
        for r_ref, (_, n) in zip(refs[3:3 + n_parts], parts):
            for s in range(n):
                term = r_ref[s].astype(F32)
                g = term if g is None else g + term
        g_out, d_out, m_out, v_out = refs[3 + n_parts:]
        mn = ADAM_B1 * m_ref[...] + (1.0 - ADAM_B1) * g
        vn = ADAM_B2 * v_ref[...] + (1.0 - ADAM_B2) * (g * g)
        m_hat = mn / (1.0 - ADAM_B1 ** ADAM_STEP)
        v_hat = vn / (1.0 - ADAM_B2 ** ADAM_STEP)
        g_out[...] = g
        d_out[...] = -ADAM_LR * (m_hat / (jnp.sqrt(v_hat) + ADAM_EPS) + ADAM_WD * w_ref[...])
        m_out[...] = mn
        v_out[...] = vn

    blk = pl.BlockSpec(blk_shape, lambda i: at(0, i)[1:])
    return pl.pallas_call(
        kern, name=name,
        out_shape=[jax.ShapeDtypeStruct((R, Cc), F32)] * 4,
        grid=(steps,),
        in_specs=[blk] * 3 + [pl.BlockSpec((n,) + blk_shape, lambda i: at(0, i)) for (_, n) in parts],
        out_specs=[blk] * 4,
        compiler_params=_cp(("arbitrary",)),
    )(w, m, v, *[a for (a, _) in parts])


def _assemble_columns(blocks, pieces, zeros, width):
    _, R, Cc = blocks.shape
    tr = min(256, R)

    def kern(b_ref, o_ref):
        for col, n in zeros:
            o_ref[:, col:col + n] = jnp.zeros((tr, n), o_ref.dtype)
        for col, d, lo, n in pieces:
            o_ref[:, col:col + n] = b_ref[d, :, lo:lo + n]

    return pl.pallas_call(
        kern, name="assemble_w_in",
        out_shape=jax.ShapeDtypeStruct((R, width), blocks.dtype),
        grid=(R // tr,),
        in_specs=[pl.BlockSpec((N_DEV, tr, Cc), lambda i: (0, i, 0))],
        out_specs=pl.BlockSpec((tr, width), lambda i: (i, 0)),
        compiler_params=_cp(("parallel",)),
    )(blocks)


def _split_rows(x, pieces, rows):
    _, Cc = x.shape
    tc = min(256, Cc)

    def kern(x_ref, o_ref):
        for d, lo, row, n in pieces:
            o_ref[d, lo:lo + n, :] = x_ref[row:row + n, :]

    return pl.pallas_call(
        kern, name="split_w_in_grad",
        out_shape=jax.ShapeDtypeStruct((N_DEV, rows, Cc), x.dtype),
        grid=(Cc // tc,),
        in_specs=[pl.BlockSpec((x.shape[0], tc), lambda i: (0, i))],
        out_specs=pl.BlockSpec((N_DEV, rows, tc), lambda i: (0, 0, i)),
        compiler_params=_cp(("parallel",)),
    )(x)


def _pad_cols(a, w):
    return jnp.pad(a, ((0, 0), (0, w - a.shape[1])))


def _pad_rows(a, r):
    return jnp.pad(a, ((0, r - a.shape[0]), (0, 0)))


def _pack_b(pf, pr, wo, w2, a2, rows):
    body = jnp.concatenate([pf, pr, wo.reshape(2048, 256), jnp.concatenate([w2, a2], axis=1)], axis=0)
    return _pad_rows(body, rows)


def kernel(x, norm_gain, w_in, fox_forget_bias, rwkv_shift_mix, rwkv_w0, rwkv_w2, rwkv_a0, rwkv_a2, rwkv_k_k, rwkv_k_a, rwkv_r_k, rwkv_ln_w, rwkv_ln_b, w_proj_fox, w_proj_rwkv, w_out, final_norm_gain, loss_target, m_norm_gain, m_w_in, m_fox_forget_bias, m_rwkv_shift_mix, m_rwkv_w0, m_rwkv_w2, m_rwkv_a0, m_rwkv_a2, m_rwkv_k_k, m_rwkv_k_a, m_rwkv_r_k, m_rwkv_ln_w, m_rwkv_ln_b, m_w_proj_fox, m_w_proj_rwkv, m_w_out, m_final_norm_gain, v_norm_gain, v_w_in, v_fox_forget_bias, v_rwkv_shift_mix, v_rwkv_w0, v_rwkv_w2, v_rwkv_a0, v_rwkv_a2, v_rwkv_k_k, v_rwkv_k_a, v_rwkv_r_k, v_rwkv_ln_w, v_rwkv_ln_b, v_w_proj_fox, v_w_proj_rwkv, v_w_out, v_final_norm_gain):
    T, D = x.shape[1], x.shape[2]
    assert D == 2048 and T % LANE == 0
    NI = w_in.shape[2]
    IN = N_DEV * NI
    RB = 4224
    x2 = x[0]
    lt2 = loss_target[0]
    me = _slot(_place())

    wa = _allgather(w_in[0].astype(BF16), "allgather_w_in", relay=True)
    packed_own = _pack_b(w_proj_fox[0], w_proj_rwkv[0], w_out[0], rwkv_w2[0], rwkv_a2[0], RB).astype(BF16)
    sections = [(0, 4096, 0), (4104, 4096, 4096), (8392, 4096, 8192), (4096, 8, 12288), (8200, 96, 12544), (8296, 96, 12672)]
    NP = 12800
    pieces, zeros, at_col = [], [], 0
    for lo, width, pad_lo in sections:
        if pad_lo > at_col:
            zeros.append((at_col, pad_lo - at_col))
        col = lo
        while col < lo + width:
            d = col // NI
            stop = min(lo + width, (d + 1) * NI)
            pieces.append((pad_lo + col - lo, d, col - d * NI, stop - col))
            col = stop
        at_col = pad_lo + width
    zeros.append((at_col, NP - at_col))
    w_pad = _assemble_columns(wa, pieces, zeros, NP)
    F_CB, LORA_CB = 96, 49

    mu = rwkv_shift_mix
    mu_main = mu[:, 0:4096]
    mu_lora = jnp.concatenate([_pad_cols(mu[:, 4096:4192], LANE), _pad_cols(mu[:, 4192:4288], LANE)], axis=1)
    bias_pad = _pad_cols(fox_forget_bias, LANE)
    rk_flat = rwkv_r_k.reshape(1, 1024)
    gf = final_norm_gain.reshape(1, D)

    tb = min(256, T)
    tbh = min(128, T)
    (h,) = _rowcall("rms_fwd", T, tb, [_rows(x2, tb, D), _whole(norm_gain)],
                    lambda i, xv, g: ([_rms_math(xv, g)], []), [(D, BF16)])
    u, wb = _mm(h, w_pad, tm=1024, tn=1280, name="mm_in", side=_allgather_side(packed_own))
    wpf = wb[:, 0:1024, :].transpose(1, 0, 2).reshape(1024, D)
    wpr = wb[:, 1024:2048, :].transpose(1, 0, 2).reshape(1024, D)
    wo = wb[:, 2048:4096, :].reshape(N_DEV * 256, D)
    w2p = _pad_rows(wb[:, 4096:4192, 0:128].transpose(1, 0, 2).reshape(96, 1024).astype(F32), LANE)
    a2p = _pad_rows(wb[:, 4096:4192, 128:256].transpose(1, 0, 2).reshape(96, 1024).astype(F32), LANE)

    c8 = _gates_fwd(u, bias_pad, T, F_CB)
    c3 = c8.reshape(8, 1, T)
    o_raw, o_a, lse = _attn_fwd(u, c3, T)

    def shift_body(i, um, hm, ul, hl, mm_, ml):
        outs = []
        for uv, hv, mv in ((um, hm, mm_), (ul, hl, ml)):
            hv = jnp.where(i == 0, 0.0, hv)
            prev = pltpu.roll(jnp.concatenate([hv, uv], axis=0), 1, 0)[SUB:]
            outs.append(uv + (prev - uv) * mv)
        return outs, []

    def halo_prev(arr, w, cb):
        return (arr, (SUB, w), lambda i: (jnp.maximum(i * (tbh // SUB) - 1, 0), cb))

    xs, xl = _rowcall("rwkv_shift_fwd", T, tbh,
                      [_rows(u, tbh, 4096, 1), halo_prev(u, 4096, 1), _rows(u, tbh, 256, LORA_CB), halo_prev(u, 256, LORA_CB),
                       _whole(mu_main), _whole(mu_lora)],
                      shift_body, [(4096, F32), (256, F32)])

    prep_par = [_whole(rwkv_w0), _whole(rwkv_a0), _whole(rwkv_k_k), _whole(rwkv_k_a), _whole(w2p), _whole(a2p)]
    prep_rows = [_rows(xs, tbh, 1024, 1), _rows(xl, tbh, LANE, 0), _rows(xl, tbh, LANE, 1)]
    lw, kp, an, bb = _rowcall("rwkv_prep_fwd", T, tbh, prep_rows + prep_par,
                              lambda i, *a: (list(_prep_math(*a)), []), [(1024, F32)] * 4)
    y, states, invs = _scan_fwd(xs, lw, kp, an, bb, T)
    post_rows = [_rows(y, tbh, 1024), _rows(xs, tbh, 1024, 0), _rows(kp, tbh, 1024), _rows(xs, tbh, 1024, 2), _rows(xs, tbh, 1024, 3)]
    post_par = [_whole(rwkv_ln_w), _whole(rwkv_ln_b), _whole(rk_flat)]
    (o_b,) = _rowcall("rwkv_post_fwd", T, tbh, post_rows + post_par,
                      lambda i, *a: ([_post_math(*a)], []), [(1024, BF16)])

    pa = _mm(o_a, wpf, name="mm_proj_fox")
    pb = _mm(o_b, wpr, name="mm_proj_rwkv")
    merge_rows = [_rows(u, tb, D, 4), _rows(u, tb, D, 5), _rows(pa, tb, D), _rows(pb, tb, D)]
    (mg,) = _rowcall("merge_fwd", T, tb, merge_rows, lambda i, *a: ([_merge_math(*a)], []), [(D, BF16)])
    mo = _mm(mg, wo, name="mm_out")

    def head_body(i, xv, mov, ltv, g):
        out = xv + mov
        r = lax.rsqrt(jnp.mean(out * out, axis=-1, keepdims=True) + RMS_EPS)
        yn = out * r
        err = yn * g - ltv
        loss = 0.5 * jnp.sum(jnp.sum(err * err, axis=-1, keepdims=True), axis=0, keepdims=True) / D
        dyv = err / D
        dyn = dyv * g
        dout = r * (dyn - yn * jnp.mean(dyn * yn, axis=-1, keepdims=True))
        return [dout], [loss, jnp.sum(dyv * yn, axis=0, keepdims=True)]

    dout, loss_p, dgf_p = _rowcall("loss_head", T, tb, [_rows(x2, tb, D), _rows(mo, tb, D), _rows(lt2, tb, D), _whole(gf)],
                                   head_body, [(D, F32)], [(1, 1), (1, D)])

    dm = _mm(dout, wo, tb=True, name="mm_out_dx")
    dwo = _mm(mg, dout, ta=True, out_dtype=BF16, name="mm_out_dw")

    def merge_bwd_body(i, ga, gb, pav, pbv, dmv):
        _, vjp = jax.vjp(_merge_math, ga, gb, pav, pbv)
        dga, dgb, dpa, dpb = vjp(dmv)
        return [dga, dgb, dpa, dpb], []

    dga, dgb, dpa, dpb = _rowcall("merge_bwd", T, tb, merge_rows + [_rows(dm, tb, D)], merge_bwd_body,
                                  [(D, BF16), (D, BF16), (D, BF16), (D, BF16)])
    doa = _mm(dpa, wpf, tb=True, name="mm_proj_fox_dx")
    dwpf = _mm(o_a, dpa, ta=True, out_dtype=BF16, name="mm_proj_fox_dw")
    dob = _mm(dpb, wpr, tb=True, name="mm_proj_rwkv_dx")
    dwpr = _mm(o_b, dpb, ta=True, out_dtype=BF16, name="mm_proj_rwkv_dw")

    do_b, dza, delta = _attn_delta(u, c3, o_raw, lse, doa, T)
    dq, dk, dv, dc3 = _attn_bwd(u, c3, lse, do_b, delta, T)
    dfl, dbias_p = _gates_bwd(dc3.reshape(8, T), u, bias_pad, T, F_CB)

    def post_bwd_body(i, yv, rv, kpv, vv, zv, lnw, lnb, rkv, dobv):
        _, vjp = jax.vjp(_post_math, yv, rv, kpv, vv, zv, lnw, lnb, rkv)
        dy_, dr_, dkp_, dv_, dz_, dlnw, dlnb, drk = vjp(dobv)
        return [dy_, dr_, dkp_, dv_, dz_], [dlnw, dlnb, drk]

    dy_s, dr_p, dkp_p, dv_p, dzb, dlnw_p, dlnb_p, drk_p = _rowcall(
        "rwkv_post_bwd", T, tbh, post_rows + post_par + [_rows(dob, tbh, 1024)], post_bwd_body,
        [(1024, F32)] * 5, [(1, 1024)] * 3)
    dr_s, dlw, dkp_s, dv_s, dan, dbb = _scan_bwd(xs, lw, kp, an, bb, states, invs, dy_s, T)

    def prep_bwd_body(i, xk, xwd, xad, w0, a0, kkw, kaw, w2v, a2v, dlw_, dkp1, dkp2, dan_, dbb_, dr1, dr2, dv1, dv2, dz_):
        _, vjp = jax.vjp(_prep_math, xk, xwd, xad, w0, a0, kkw, kaw, w2v, a2v)
        dxk, dxwd, dxad, dw0, da0, dkk, dka, dw2, da2 = vjp((dlw_, dkp1 + dkp2, dan_, dbb_))
        return [[dr1 + dr2, dxk, dv1 + dv2, dz_], [dxwd, dxad]], [dw0, da0, dkk, dka, dw2, da2]

    cots = [dlw, dkp_s, dkp_p, dan, dbb, dr_s, dr_p, dv_s, dv_p, dzb]
    dxs, dxl, dw0_p, da0_p, dkk_p, dka_p, dw2_p, da2_p = _rowcall(
        "rwkv_prep_bwd", T, tbh, prep_rows + prep_par + [_rows(c_, tbh, 1024) for c_ in cots], prep_bwd_body,
        [(4096, F32), (256, F32)], [(1, 1024)] * 4 + [(LANE, 1024)] * 2)

    def shift_bwd_body(i, dm_, hm, dl_, hl, um, pm, ul, pl_, mm_, ml):
        last = i == T // tbh - 1
        outs, accs = [], []
        for dv_, hv, uv, pv, mv in ((dm_, hm, um, pm, mm_), (dl_, hl, ul, pl_, ml)):
            hv = jnp.where(last, 0.0, hv)
            nxt = pltpu.roll(jnp.concatenate([dv_, hv], axis=0), tbh + SUB - 1, 0)[:tbh]
            pv = jnp.where(i == 0, 0.0, pv)
            prev = pltpu.roll(jnp.concatenate([pv, uv], axis=0), 1, 0)[SUB:]
            outs.append(dv_ * (1.0 - mv) + nxt * mv)
            accs.append(jnp.sum(dv_ * (prev - uv), axis=0, keepdims=True))
        return outs, accs

    def halo_next(arr, w, cb):
        last_blk = T // SUB - 1
        return (arr, (SUB, w), lambda i: (jnp.minimum((i + 1) * (tbh // SUB), last_blk), cb))

    du_b, du_l, dmu_main_p, dmu_lora_p = _rowcall(
        "rwkv_shift_bwd", T, tbh,
        [_rows(dxs, tbh, 4096), halo_next(dxs, 4096, 0), _rows(dxl, tbh, 256), halo_next(dxl, 256, 0),
         _rows(u, tbh, 4096, 1), halo_prev(u, 4096, 1), _rows(u, tbh, 256, LORA_CB), halo_prev(u, 256, LORA_CB),
         _whole(mu_main), _whole(mu_lora)],
        shift_bwd_body, [(4096, BF16), (256, BF16)], [(1, 4096), (1, 256)])

    du = jnp.concatenate([dq.astype(BF16), dk.astype(BF16), dv.astype(BF16), dza.astype(BF16), du_b, dga, dgb,
                          dfl.astype(BF16), jnp.zeros((T, LANE), BF16), du_l], axis=1)
    dw_pad_t = _mm(du, h, ta=True, out_dtype=BF16, tm=1280, tn=1024, name="mm_in_dw")

    ga = _split_rows(dw_pad_t, [(d, lo, pad_col, n) for pad_col, d, lo, n in pieces], NI)
    lora_g = jnp.concatenate([dw2_p[:96].reshape(96, N_DEV, 128).transpose(1, 0, 2),
                              da2_p[:96].reshape(96, N_DEV, 128).transpose(1, 0, 2)], axis=2).astype(BF16)
    gb = jnp.concatenate([dwpf.reshape(1024, N_DEV, 256).transpose(1, 0, 2),
                          dwpr.reshape(1024, N_DEV, 256).transpose(1, 0, 2),
                          dwo.reshape(N_DEV, 2048, 256), lora_g, jnp.zeros((N_DEV, RB - 4192, 256), BF16)], axis=1)

    xx, yy, cc = _place()
    chip_slots = jnp.stack([_slot((*chip, cc)) for chip in _other_chips(xx, yy)]).astype(jnp.int32)
    r1a, r1b = _exchange_pair(ga, gb)
    sa = _pair_add("pair_add_w_in", ga, r1a, chip_slots, 256, by_cols=True)
    sb = _pair_add("pair_add_packed", gb, r1b, chip_slots, 384)
    ga_own = lax.dynamic_index_in_dim(ga, me, 0, keepdims=True)
    gb_own = lax.dynamic_index_in_dim(gb, me, 0, keepdims=True)
    dh, t1a, t1b = _mm(du, w_pad, tb=True, tm=1024, tn=1024, tk=NP // 10, name="mm_in_dx", side=_exchange_first_side(sa, sb))
    core = jnp.stack([cc]).astype(jnp.int32)
    second = _exchange_second_side(_axis_add("axis_add_w_in", sa, t1a, core, 256, by_cols=True),
                                   _axis_add("axis_add_packed", sb, t1b, core, 384))

    def rms_bwd_body(i, xv, g, dhv, doutv):
        _, vjp = jax.vjp(_rms_math, xv, g)
        dx_, dg_ = vjp(dhv)
        return [dx_ + doutv], [dg_]

    grad_x2, dng_p, t2a, t2b = _rowcall(
        "rms_bwd", T, tb, [_rows(x2, tb, D), _whole(norm_gain), _rows(dh, tb, D), _rows(dout, tb, D)],
        rms_bwd_body, [(D, F32)], [(1, D)], side=second)

    dmu = jnp.concatenate([dmu_main_p, dmu_lora_p[:, 0:96], dmu_lora_p[:, 128:224]], axis=1)
    small_parts = [dng_p, dbias_p[:, 0:8], dmu, dw0_p, da0_p, dkk_p, dka_p, drk_p, dlnw_p, dlnb_p, dgf_p, loss_p]
    SR = 128
    small = _pad_cols(jnp.concatenate(small_parts, axis=1), SR * LANE).reshape(SR, LANE)
    rs = _allgather(small, "allgather_small")

    w_in_outs = _adamw("adamw_w_in", w_in[0].T, m_w_in[0].T, v_w_in[0].T, [(ga_own, 1), (r1a, 1), (t1a, 1), (t2a, 1)], 256, by_cols=True)
    g_in, d_in, m_in, v_in = [o.T for o in w_in_outs]
    pk = lambda pf, pr, wo_, w2_, a2_: _pack_b(pf[0], pr[0], wo_[0], w2_[0], a2_[0], RB)
    outs_b = _adamw("adamw_packed", pk(w_proj_fox, w_proj_rwkv, w_out, rwkv_w2, rwkv_a2),
                    pk(m_w_proj_fox, m_w_proj_rwkv, m_w_out, m_rwkv_w2, m_rwkv_a2),
                    pk(v_w_proj_fox, v_w_proj_rwkv, v_w_out, v_rwkv_w2, v_rwkv_a2), [(gb_own, 1), (r1b, 1), (t1b, 1), (t2b, 1)], 384)

    def pack_small(ng, fb, sm, w0, a0, kk_, ka_, rk_, lnw, lnb, fg):
        parts = [ng, fb, sm, w0, a0, kk_, ka_, rk_.reshape(1, 1024), lnw, lnb, fg.reshape(1, D), jnp.zeros((1, 1), F32)]
        return _pad_cols(jnp.concatenate(parts, axis=1), SR * LANE).reshape(SR, LANE)

    outs_s = _adamw("adamw_small",
                    pack_small(norm_gain, fox_forget_bias, rwkv_shift_mix, rwkv_w0, rwkv_a0, rwkv_k_k, rwkv_k_a, rwkv_r_k,
                               rwkv_ln_w, rwkv_ln_b, final_norm_gain),
                    pack_small(m_norm_gain, m_fox_forget_bias, m_rwkv_shift_mix, m_rwkv_w0, m_rwkv_a0, m_rwkv_k_k, m_rwkv_k_a,
                               m_rwkv_r_k, m_rwkv_ln_w, m_rwkv_ln_b, m_final_norm_gain),
                    pack_small(v_norm_gain, v_fox_forget_bias, v_rwkv_shift_mix, v_rwkv_w0, v_rwkv_a0, v_rwkv_k_k, v_rwkv_k_a,
                               v_rwkv_r_k, v_rwkv_ln_w, v_rwkv_ln_b, v_final_norm_gain),
                    [(rs, N_DEV)], SR)

    def unpack_b(pkd):
        return dict(w_proj_fox=pkd[0:1024][None], w_proj_rwkv=pkd[1024:2048][None], w_out=pkd[2048:4096].reshape(1, 256, D),
                    rwkv_w2=pkd[4096:4192, 0:128][None], rwkv_a2=pkd[4096:4192, 128:256][None])

    def unpack_s(pkd):
        flat = pkd.reshape(1, SR * LANE)
        names = [("norm_gain", D), ("fox_forget_bias", 8), ("rwkv_shift_mix", 4288), ("rwkv_w0", 1024), ("rwkv_a0", 1024),
                 ("rwkv_k_k", 1024), ("rwkv_k_a", 1024), ("rwkv_r_k", 1024), ("rwkv_ln_w", 1024), ("rwkv_ln_b", 1024),
                 ("final_norm_gain", D), ("loss", 1)]
        out, off = {}, 0
        for nm, n in names:
            out[nm] = flat[:, off:off + n]
            off += n
        out["rwkv_r_k"] = out["rwkv_r_k"].reshape(1, 16, 64)
        out["final_norm_gain"] = out["final_norm_gain"].reshape(D)
        return out

    order = ["norm_gain", "w_in", "fox_forget_bias", "rwkv_shift_mix", "rwkv_w0", "rwkv_w2", "rwkv_a0", "rwkv_a2", "rwkv_k_k",
             "rwkv_k_a", "rwkv_r_k", "rwkv_ln_w", "rwkv_ln_b", "w_proj_fox", "w_proj_rwkv", "w_out", "final_norm_gain"]
    result = []
    loss = None
    for kind, big in enumerate((g_in, d_in, m_in, v_in)):
        d = {**unpack_b(outs_b[kind]), **unpack_s(outs_s[kind]), "w_in": big[None]}
        if kind == 0:
            loss = d["loss"].reshape(())
        result += [d[n] for n in order]
    return (loss, grad_x2[None], *result)
```

```python
import functools

import jax
import jax.numpy as jnp
from jax import lax
from jax.experimental import pallas as pl
from jax.experimental.pallas import tpu as pltpu

F32 = jnp.float32
BF16 = jnp.bfloat16
HI = lax.Precision.HIGHEST
H3 = lax.Precision.HIGH
MESH = pl.DeviceIdType.MESH

FOX_HD = 128
RW_HD = 64
RMS_EPS = 1e-6
GN_EPS = 64e-5
L2_EPS = 1e-12
ADAM_LR = 0.001
ADAM_B1 = 0.9
ADAM_B2 = 0.999
ADAM_EPS = 1e-08
ADAM_WD = 0.01
ADAM_STEP = 10

LANE = 128
SUB = 8
VMEM_LIMIT = 56 * 1024 * 1024
N_DEV = 8
CHUNK = 128
SCAN_GROUP = 2
SCAN_PAIRS = 2
PS = None
NEG = -1e30


def _scan_shape(T):
    c = min(CHUNK, T)
    return c, min(SCAN_GROUP, T // c)


def _cp(sem=None):
    return pltpu.CompilerParams(dimension_semantics=sem, vmem_limit_bytes=VMEM_LIMIT)


def _sigmoid(x):
    return jax.nn.sigmoid(x)


def _softplus(x):
    return jnp.maximum(x, 0.0) + jnp.log(1.0 + jnp.exp(-jnp.abs(x)))


def _nn(a, b, prec=None):
    return lax.dot_general(a, b, (((1,), (0,)), ((), ())), precision=prec, preferred_element_type=F32)


def _nt(a, b, prec=None):
    return lax.dot_general(a, b, (((1,), (1,)), ((), ())), precision=prec, preferred_element_type=F32)


def _tn(a, b, prec=None):
    return lax.dot_general(a, b, (((0,), (0,)), ((), ())), precision=prec, preferred_element_type=F32)


def _iota2(shape, dim):
    return lax.broadcasted_iota(jnp.int32, shape, dim)


def _seg_sum(x):
    r = _iota2((LANE, LANE), 0) // RW_HD
    c = _iota2((LANE, LANE), 1) // RW_HD
    bd = (r == c).astype(F32)
    parts = [_nn(x[:, j * LANE:(j + 1) * LANE], bd, H3) for j in range(x.shape[1] // LANE)]
    return parts[0] if len(parts) == 1 else jnp.concatenate(parts, axis=1)


def _mm(a, b, *, ta=False, tb=False, out_dtype=F32, tm=1024, tn=1024, tk=None, name, side=None):
    assert not (ta and tb)
    K, M = a.shape if ta else a.shape[::-1]
    N = b.shape[0] if tb else b.shape[1]
    tm, tn = min(tm, M), min(tn, N)
    tk = K if tk is None else tk
    nk = K // tk
    assert M % tm == 0 and N % tn == 0 and K % tk == 0
    gi, gj = M // tm, N // tn
    a_spec = pl.BlockSpec((tk, tm), lambda i, j, k: (k, i)) if ta else pl.BlockSpec((tm, tk), lambda i, j, k: (i, k))
    b_spec = pl.BlockSpec((tn, tk), lambda i, j, k: (j, k)) if tb else pl.BlockSpec((tk, tn), lambda i, j, k: (k, j))
    side_ins, side_outs, side_scr, side_make = side if side is not None else ((), (), (), None)
    n_si, n_so = len(side_ins), len(side_outs)
    n_acc = 0 if nk == 1 else 1

    def body(*refs):
        a_ref, b_ref = refs[:2]
        o_ref = refs[2 + n_si]
        scr = refs[3 + n_si + n_so:]
        k = pl.program_id(2)
        if side_make is not None:
            start, finish = side_make(refs[2:2 + n_si], refs[3 + n_si:3 + n_si + n_so], scr[n_acc:])
            first = jnp.logical_and(jnp.logical_and(pl.program_id(0) == 0, pl.program_id(1) == 0), k == 0)
            last = jnp.logical_and(jnp.logical_and(pl.program_id(0) == gi - 1, pl.program_id(1) == gj - 1), k == nk - 1)
            pl.when(first)(start)
        av = a_ref[...].astype(BF16)
        bv = b_ref[...].astype(BF16)
        p = _tn(av, bv) if ta else _nt(av, bv) if tb else _nn(av, bv)
        if nk == 1:
            o_ref[...] = p.astype(out_dtype)
        else:
            acc_ref = scr[0]

            @pl.when(k == 0)
            def _():
                acc_ref[...] = p

            @pl.when(k > 0)
            def _():
                acc_ref[...] += p

            @pl.when(k == nk - 1)
            def _():
                o_ref[...] = acc_ref[...].astype(out_dtype)
        if side_make is not None:
            pl.when(last)(finish)

    any_spec = pl.BlockSpec(memory_space=pl.ANY)
    res = pl.pallas_call(
        body, name=name,
        out_shape=[jax.ShapeDtypeStruct((M, N), out_dtype)] + list(side_outs),
        grid=(gi, gj, nk),
        in_specs=[a_spec, b_spec] + [any_spec] * n_si,
        out_specs=[pl.BlockSpec((tm, tn), lambda i, j, k: (i, j))] + [any_spec] * n_so,
        scratch_shapes=([] if nk == 1 else [pltpu.VMEM((tm, tn), F32)]) + list(side_scr),
        compiler_params=_cp(("arbitrary",) * 3 if side is not None else ("parallel", "parallel", "arbitrary")),
    )(a, b, *side_ins)
    return res if side is not None else res[0]


def _rows(arr, tb, w, cb=0):
    return (arr, (tb, w), lambda i: (i, cb))


def _whole(arr):
    nd = arr.ndim
    return (arr, arr.shape, lambda i: (0,) * nd)


def _rowcall(name, T, tb, ins, body, outs, accs=(), side=None):
    n_in, n_out, n_acc = len(ins), len(outs), len(accs)
    side_ins, side_outs, side_scr, side_make = side if side is not None else ((), (), (), None)
    n_si = len(side_ins)

    def kern(*refs):
        i = pl.program_id(0)
        if side_make is not None:
            side_refs = refs[n_in + n_si + n_out + n_acc:]
            start, finish = side_make(refs[n_in:n_in + n_si], side_refs[:len(side_outs)], side_refs[len(side_outs):])
            pl.when(i == 0)(start)
            refs = refs[:n_in] + refs[n_in + n_si:]
        vals = [r[...] for r in refs[:n_in]]
        ro, ao = body(i, *vals)
        for r, v in zip(refs[n_in:n_in + n_out], ro):
            if isinstance(v, (list, tuple)):
                off = 0
                for piece in v:
                    w = piece.shape[1]
                    r[:, off:off + w] = piece.astype(r.dtype)
                    off += w
            else:
                r[...] = v.astype(r.dtype)
        if accs:
            acc_refs = refs[n_in + n_out:n_in + n_out + n_acc]

            @pl.when(i == 0)
            def _():
                for r in acc_refs:
                    r[...] = jnp.zeros(r.shape, F32)

            for r, v in zip(acc_refs, ao):
                r[...] += v
        if side_make is not None:
            pl.when(i == T // tb - 1)(finish)

    any_spec = pl.BlockSpec(memory_space=pl.ANY)
    out_shape = [jax.ShapeDtypeStruct((T, w), dt) for (w, dt) in outs] + [jax.ShapeDtypeStruct(s, F32) for s in accs]
    out_specs = [pl.BlockSpec((tb, w), lambda i: (i, 0)) for (w, dt) in outs] + [pl.BlockSpec(s, lambda i: (0, 0)) for s in accs]
    res = pl.pallas_call(
        kern, name=name,
        out_shape=out_shape + list(side_outs),
        grid=(T // tb,),
        in_specs=[pl.BlockSpec(bs, im) for (_, bs, im) in ins] + [any_spec] * n_si,
        out_specs=out_specs + [any_spec] * len(side_outs),
        scratch_shapes=list(side_scr),
        compiler_params=_cp(("arbitrary",)),
    )(*[a for (a, _, _) in ins], *side_ins)
    return res


def _rms_math(x, g):
    r = lax.rsqrt(jnp.mean(x * x, axis=-1, keepdims=True) + RMS_EPS)
    return x * r * g


def _merge_math(ga, gb, pa, pb):
    return _sigmoid(ga) * pa + _sigmoid(gb) * pb


def _prep_math(xk, xwd, xad, w0, a0, kk_w, ka_w, w2p, a2p):
    z = w0 + _nn(jnp.tanh(xwd), w2p, H3)
    w = -_softplus(-z) - 0.5
    lw = -jnp.exp(w)
    ag = _sigmoid(a0 + _nn(xad, a2p, H3))
    p = xk * kk_w
    n = jnp.maximum(jnp.sqrt(_seg_sum(p * p)), L2_EPS)
    kk = p / n
    kp = xk * (1.0 + (ag - 1.0) * ka_w)
    return lw, kp, -kk, kk * ag


def _post_math(y, r, kp, v, z, lnw, lnb, rk):
    inv = 1.0 / RW_HD
    mu = _seg_sum(y) * inv
    d = y - mu
    var = _seg_sum(d * d) * inv
    yn = d * lax.rsqrt(var + GN_EPS) * lnw + lnb
    bonus = _seg_sum(r * kp * rk) * v
    return (yn + bonus) * (z * _sigmoid(z))


@functools.partial(jax.custom_vjp, nondiff_argnums=(1,))
def _unit_inverses(ms, depth):
    eye = (_iota2(ms[0].shape, 0) == _iota2(ms[0].shape, 1)).astype(F32)
    half = ms[0].shape[1]
    mp = [_nn(m, m, PS) for m in ms]
    inv = [eye + m for m in ms]
    n = 2
    while n < depth:
        last = 2 * n >= depth
        for i in range(len(ms)):
            if last:
                inv[i] = inv[i] + _nn(mp[i], inv[i], PS)
            else:
                z = _nn(mp[i], jnp.concatenate([mp[i], inv[i]], axis=1), PS)
                mp[i], inv[i] = z[:, :half], inv[i] + z[:, half:]
        n *= 2
    return tuple(inv)


def _unit_inverses_fwd(ms, depth):
    inv = _unit_inverses(ms, depth)
    return inv, inv


def _unit_inverses_bwd(depth, inv, cts):
    left = [_tn(t, g, PS) for t, g in zip(inv, cts)]
    return (tuple(_nt(l, t, PS) for l, t in zip(left, inv)),)


_unit_inverses.defvjp(_unit_inverses_fwd, _unit_inverses_bwd)


@jax.custom_vjp
def _known_inverses(ms, inv):
    return inv


def _known_inverses_fwd(ms, inv):
    return inv, inv


def _known_inverses_bwd(inv, cts):
    dms = _unit_inverses_bwd(None, inv, cts)[0]
    return dms, tuple(jnp.zeros_like(t) for t in inv)


_known_inverses.defvjp(_known_inverses_fwd, _known_inverses_bwd)


def _scan_group(s0s, *flat, known_inv=None, with_inv=False):
    P = len(s0s)
    G = len(flat) // (6 * P)
    ch = [flat[6 * i:6 * i + 6] for i in range(P * G)]
    C = ch[0][0].shape[0]
    C2 = 2 * C
    cat = jnp.concatenate
    m0 = _iota2((1, LANE), 1) < RW_HD
    mask0 = m0.astype(F32)
    mask1 = 1.0 - mask0
    r2 = _iota2((C2, C2), 0)
    c2 = _iota2((C2, C2), 1)
    dist = r2 - c2
    in_head = dist <= r2 % C
    lower = (_iota2((C, C), 0) >= _iota2((C, C), 1)).astype(F32)
    bd = (_iota2((LANE, LANE), 0) // RW_HD) == (_iota2((LANE, LANE), 1) // RW_HD)

    def tri(m, strict):
        return jnp.where(dist > 0 if strict else dist >= 0, jnp.where(in_head, m, 0.0), 0.0)

    def sel(z):
        return jnp.where(m0, z[:C], z[C:])

    gs = [_nn(lower, c[1], HI) for c in ch]
    pre = []
    for (r, lw, k, v, a, b), g in zip(ch, gs):
        g_end = jnp.sum(lw, axis=0, keepdims=True)
        gm = g - jnp.sum(lw[:C // 2], axis=0, keepdims=True)
        en = jnp.exp(-gm)
        ec = jnp.exp(g_end - g)
        pre.append(dict(at=a * jnp.exp(g - lw), rt=r * jnp.exp(g), am=a * jnp.exp(gm - lw), rm=r * jnp.exp(gm),
                        bt=b * en, kt=k * en, bh=b * ec, kh=k * ec, dec=jnp.exp(g_end), v=v))
    grams = [_nt(cat([p["am"] * mask0, p["am"] * mask1, p["rm"] * mask0, p["rm"] * mask1], axis=0),
                 cat([p["bt"], p["bt"], p["kt"], p["kt"]], axis=0), PS) for p in pre]
    mab = tuple(tri(gm[:C2, :C2], True) for gm in grams)
    tinv = _unit_inverses(mab, C) if known_inv is None else _known_inverses(mab, known_inv)
    xv =[sel(_nn(tri(gm[:C2, C2:], True), cat([p["v"], p["v"]], axis=0), PS)) for gm, p in zip(grams, pre)]
    ys, s = [None] * (P * G), list(s0s)
    for i in range(G):
        for q in range(P):
            n = q * G + i
            p, gm = pre[n], grams[n]
            sx = _nt(cat([p["at"], p["rt"]], axis=0), s[q], PS)
            x = sx[:C] + xv[n]
            u = sel(_nn(tinv[n], cat([x, x], axis=0), PS))
            v = p["v"]
            ys[n] = sx[C:] + sel(_nn(cat([tri(gm[C2:, :C2], False), tri(gm[C2:, C2:], False)], axis=1),
                                     cat([u, u, v, v], axis=0), PS))
            s[q] = s[q] * p["dec"] + jnp.where(bd, _tn(cat([u, v], axis=0), cat([p["bh"], p["kh"]], axis=0), PS), 0.0)
    return (tuple(ys), tuple(s), tinv) if with_inv else (tuple(ys), tuple(s))


def _scan_fwd(xs, lw, kp, an, bb, T):
    C, G = _scan_shape(T)
    P = SCAN_PAIRS
    nc = T // (C * G)
    npair = 1024 // LANE

    def kern(r_ref, lw_ref, k_ref, v_ref, a_ref, b_ref, y_ref, st_ref, inv_ref, s_scr):
        n = pl.program_id(1)

        @pl.when(n == 0)
        def _():
            s_scr[...] = jnp.zeros(s_scr.shape, F32)

        st_ref[0] = s_scr[...]
        ins = (r_ref, lw_ref, k_ref, v_ref, a_ref, b_ref)
        ys, s1, inv = _scan_group(
            tuple(s_scr[q] for q in range(P)),
            *[ref[i * C:(i + 1) * C, q * LANE:(q + 1) * LANE] for q in range(P) for i in range(G) for ref in ins], with_inv=True)
        for q in range(P):
            for i in range(G):
                y_ref[i * C:(i + 1) * C, q * LANE:(q + 1) * LANE] = ys[q * G + i]
                inv_ref[0, 0, q * G + i] = inv[q * G + i]
            s_scr[q] = s1[q]

    def col(off):
        return pl.BlockSpec((C * G, P * LANE), lambda p, n: (n, off // P + p))

    return pl.pallas_call(
        kern, name="rwkv_scan_fwd",
        out_shape=[jax.ShapeDtypeStruct((T, 1024), F32), jax.ShapeDtypeStruct((nc, npair, LANE, LANE), F32),
                   jax.ShapeDtypeStruct((nc, npair // P, P * G, 2 * C, 2 * C), F32)],
        grid=(npair // P, nc),
        in_specs=[col(0), col(0), col(0), col(16), col(0), col(0)],
        out_specs=[col(0), pl.BlockSpec((1, P, LANE, LANE), lambda p, n: (n, p, 0, 0)),
                   pl.BlockSpec((1, 1, P * G, 2 * C, 2 * C), lambda p, n: (n, p, 0, 0, 0))],
        scratch_shapes=[pltpu.VMEM((P, LANE, LANE), F32)],
        compiler_params=_cp(("parallel", "arbitrary")),
    )(xs, lw, kp, xs, an, bb)


def _scan_bwd(xs, lw, kp, an, bb, states, invs, dy, T):
    C, G = _scan_shape(T)
    P = SCAN_PAIRS
    nc = T // (C * G)
    npair = 1024 // LANE

    def kern(r_ref, lw_ref, k_ref, v_ref, a_ref, b_ref, st_ref, inv_ref, dy_ref, dr_ref, dlw_ref, dk_ref, dv_ref, da_ref, db_ref,
             ds_scr):
        n = pl.program_id(1)

        @pl.when(n == 0)
        def _():
            ds_scr[...] = jnp.zeros(ds_scr.shape, F32)

        ins = (r_ref, lw_ref, k_ref, v_ref, a_ref, b_ref)
        units = [(q, i) for q in range(P) for i in range(G)]
        known = tuple(inv_ref[0, 0, q * G + i] for q, i in units)
        _, vjp = jax.vjp(functools.partial(_scan_group, known_inv=known), tuple(st_ref[0, q] for q in range(P)),
                         *[ref[i * C:(i + 1) * C, q * LANE:(q + 1) * LANE] for q, i in units for ref in ins])
        grads = vjp((tuple(dy_ref[i * C:(i + 1) * C, q * LANE:(q + 1) * LANE] for q, i in units),
                     tuple(ds_scr[q] for q in range(P))))
        for q in range(P):
            ds_scr[q] = grads[0][q]
        outs = (dr_ref, dlw_ref, dk_ref, dv_ref, da_ref, db_ref)
        for n_, (q, i) in enumerate(units):
            for t, ref in enumerate(outs):
                ref[i * C:(i + 1) * C, q * LANE:(q + 1) * LANE] = grads[1 + 6 * n_ + t]

    def col(off):
        return pl.BlockSpec((C * G, P * LANE), lambda p, n: (nc - 1 - n, off // P + p))

    return pl.pallas_call(
        kern, name="rwkv_scan_bwd",
        out_shape=[jax.ShapeDtypeStruct((T, 1024), F32)] * 6,
        grid=(npair // P, nc),
        in_specs=[col(0), col(0), col(0), col(16), col(0), col(0),
                  pl.BlockSpec((1, P, LANE, LANE), lambda p, n: (nc - 1 - n, p, 0, 0)),
                  pl.BlockSpec((1, 1, P * G, 2 * C, 2 * C), lambda p, n: (nc - 1 - n, p, 0, 0, 0)), col(0)],
        out_specs=[col(0)] * 6,
        scratch_shapes=[pltpu.VMEM((P, LANE, LANE), F32)],
        compiler_params=_cp(("parallel", "arbitrary")),
    )(xs, lw, kp, xs, an, bb, states, invs, dy)


def _gates_fwd(u, bias_pad, T, f_cb):
    nb = T // LANE

    def kern(f_ref, b_ref, c_ref):
        x = f_ref[...] + b_ref[...]
        lf = jnp.minimum(x, 0.0) - jnp.log(1.0 + jnp.exp(-jnp.abs(x)))
        lft = lf.T
        ut = (_iota2((LANE, LANE), 0) <= _iota2((LANE, LANE), 1)).astype(F32)
        carry = jnp.zeros((LANE, 1), F32)
        for blk in range(nb):
            seg = lft[:, blk * LANE:(blk + 1) * LANE]
            cs = _nn(seg, ut, HI) + carry
            c_ref[:, blk * LANE:(blk + 1) * LANE] = cs[:SUB, :]
            carry = carry + jnp.sum(seg, axis=1, keepdims=True)

    return pl.pallas_call(
        kern, name="fox_gates_fwd",
        out_shape=jax.ShapeDtypeStruct((SUB, T), F32),
        grid=(1,),
        in_specs=[pl.BlockSpec((T, LANE), lambda i: (0, f_cb)), pl.BlockSpec((1, LANE), lambda i: (0, 0))],
        out_specs=pl.BlockSpec((SUB, T), lambda i: (0, 0)),
        compiler_params=_cp(("arbitrary",)),
    )(u, bias_pad)


def _gates_bwd(dc, u, bias_pad, T, f_cb):
    nb = T // LANE

    def kern(dc_ref, f_ref, b_ref, dfl_ref, db_ref):
        dcv = jnp.concatenate([dc_ref[...], jnp.zeros((LANE - SUB, T), F32)], axis=0)
        lt = (_iota2((LANE, LANE), 0) >= _iota2((LANE, LANE), 1)).astype(F32)
        carry = jnp.zeros((LANE, 1), F32)
        pieces = [None] * nb
        for blk in range(nb - 1, -1, -1):
            seg = dcv[:, blk * LANE:(blk + 1) * LANE]
            pieces[blk] = _nn(seg, lt, HI) + carry
            carry = carry + jnp.sum(seg, axis=1, keepdims=True)
        dlf = (pieces[0] if nb == 1 else jnp.concatenate(pieces, axis=1)).T
        x = f_ref[...] + b_ref[...]
        dfl = dlf * _sigmoid(-x)
        dfl_ref[...] = dfl
        db_ref[...] = jnp.sum(dfl, axis=0, keepdims=True)

    return pl.pallas_call(
        kern, name="fox_gates_bwd",
        out_shape=[jax.ShapeDtypeStruct((T, LANE), F32), jax.ShapeDtypeStruct((1, LANE), F32)],
        grid=(1,),
        in_specs=[pl.BlockSpec((SUB, T), lambda i: (0, 0)), pl.BlockSpec((T, LANE), lambda i: (0, f_cb)),
                  pl.BlockSpec((1, LANE), lambda i: (0, 0))],
        out_specs=[pl.BlockSpec((T, LANE), lambda i: (0, 0)), pl.BlockSpec((1, LANE), lambda i: (0, 0))],
        compiler_params=_cp(("arbitrary",)),
    )(dc, u, bias_pad)


ATTN_HEADS = 4


def _attn_block(T):
    return 256 if T % 256 == 0 and T >= 512 else 128


def _attn_fwd(u, c3, T):
    H, HP = 8, ATTN_HEADS
    bq = _attn_block(T)
    nq = T // bq
    scale = FOX_HD ** -0.5
    lanes = [slice(h * LANE, (h + 1) * LANE) for h in range(HP)]

    def kern(q_ref, k_ref, v_ref, z_ref, cq_ref, ck_ref, o_ref, oa_ref, lse_ref):
        i = pl.program_id(1)
        q = [(q_ref[:, ln] * scale).astype(BF16) for ln in lanes]
        c0 = [cq_ref[h][:, 0:1] for h in range(HP)]

        def step(j, carry, diagonal=False):
            off = pl.multiple_of(j * bq, bq)
            s = [_nt(q[h], k_ref[pl.ds(off, bq), lanes[h]].astype(BF16)) + (c0[h] - ck_ref[h, :, pl.ds(off, bq)])
                 for h in range(HP)]
            ps, out = [], []
            for h in range(HP):
                m, l, acc = carry[h]
                sh = s[h]
                if diagonal:
                    sh = jnp.where(_iota2((bq, bq), 1) <= _iota2((bq, bq), 0), sh, NEG)
                m_new = jnp.maximum(m, jnp.max(sh, axis=1, keepdims=True))
                p = jnp.exp(sh - m_new)
                alpha = jnp.exp(m - m_new)
                p_hi = p.astype(BF16)
                ps.append((p_hi, (p - p_hi.astype(F32)).astype(BF16)))
                out.append((m_new, alpha * l + jnp.sum(p, axis=1, keepdims=True), alpha * acc))
            res = []
            for h, (m, l, acc) in enumerate(out):
                vj = v_ref[pl.ds(off, bq), lanes[h]].astype(BF16)
                res.append((m, l, acc + _nn(ps[h][0], vj) + _nn(ps[h][1], vj)))
            return tuple(res)

        init = tuple((jnp.full((bq, 1), NEG, F32), jnp.zeros((bq, 1), F32), jnp.zeros((bq, FOX_HD), F32)) for _ in range(HP))
        res = step(i, lax.fori_loop(0, i, step, init), diagonal=True)
        for h, (m, l, acc) in enumerate(res):
            o = acc / l
            z = z_ref[:, lanes[h]]
            o_ref[:, lanes[h]] = o
            oa_ref[:, lanes[h]] = (o * z * _sigmoid(z)).astype(BF16)
            lse_ref[h] = m + jnp.log(l)

    W = HP * LANE
    return pl.pallas_call(
        kern, name="fox_attn_fwd",
        out_shape=[jax.ShapeDtypeStruct((T, 1024), F32), jax.ShapeDtypeStruct((T, 1024), BF16),
                   jax.ShapeDtypeStruct((H, T, 1), F32)],
        grid=(H // HP, nq),
        in_specs=[pl.BlockSpec((bq, W), lambda g, i: (i, g)),
                  pl.BlockSpec((T, W), lambda g, i: (0, 8 // HP + g)),
                  pl.BlockSpec((T, W), lambda g, i: (0, 16 // HP + g)),
                  pl.BlockSpec((bq, W), lambda g, i: (i, 24 // HP + g)),
                  pl.BlockSpec((HP, 1, bq), lambda g, i: (g, 0, i)),
                  pl.BlockSpec((HP, 1, T), lambda g, i: (g, 0, 0))],
        out_specs=[pl.BlockSpec((bq, W), lambda g, i: (i, g)),
                   pl.BlockSpec((bq, W), lambda g, i: (i, g)),
                   pl.BlockSpec((HP, bq, 1), lambda g, i: (g, i, 0))],
        compiler_params=_cp(("parallel", "arbitrary")),
    )(u, u, u, u, c3, c3)


def _attn_probs(s, lse_i, diagonal):
    if not diagonal:
        return jnp.exp(s - lse_i)
    keep = _iota2(s.shape, 1) <= _iota2(s.shape, 0)
    return jnp.where(keep, jnp.exp(jnp.where(keep, s, NEG) - lse_i), 0.0)


def _attn_pre_math(doa, z, o):
    sg = _sigmoid(z)
    do = (doa * z * sg).astype(BF16)
    dz = doa * o * (sg * (1.0 + z * (1.0 - sg)))
    head_of = (_iota2((o.shape[1], LANE), 0) // FOX_HD == _iota2((o.shape[1], LANE), 1)).astype(F32)
    return do, dz, _nn(do.astype(F32) * o, head_of, HI)


def _attn_bwd(u, c3, lse, do, delta, T):
    H, HP = 8, ATTN_HEADS
    bq = _attn_block(T)
    nq = T // bq
    scale = FOX_HD ** -0.5
    lanes = [slice(h * LANE, (h + 1) * LANE) for h in range(HP)]

    def kern(q_ref, k_ref, v_ref, c_ref, lse_ref, do_ref, dl_ref, dq_ref, dk_ref, dv_ref, dc_ref):
        j = pl.program_id(1)

        @pl.when(j == 0)
        def _():
            dq_ref[...] = jnp.zeros(dq_ref.shape, F32)

        kj = [k_ref[:, ln].astype(BF16) for ln in lanes]
        vj = [v_ref[:, ln].astype(BF16) for ln in lanes]
        joff = pl.multiple_of(j * bq, bq)
        ck = [c_ref[h, :, pl.ds(joff, bq)] for h in range(HP)]

        def step(i, carry, diagonal=False):
            off = pl.multiple_of(i * bq, bq)
            qs = [(q_ref[pl.ds(off, bq), ln] * scale).astype(BF16) for ln in lanes]
            dob = [do_ref[pl.ds(off, bq), ln] for ln in lanes]
            s = [_nt(qs[h], kj[h]) + (c_ref[h, :, pl.ds(off, bq)][:, 0:1] - ck[h]) for h in range(HP)]
            dp = [_nt(dob[h], vj[h]) for h in range(HP)]
            pb, dsb, dcs = [], [], []
            for h in range(HP):
                p = _attn_probs(s[h], lse_ref[h, pl.ds(off, bq), :], diagonal)
                ds = p * (dp[h] - dl_ref[h, pl.ds(off, bq), :])
                pb.append(p.astype(BF16))
                dsb.append(ds.astype(BF16))
                dcs.append(jnp.sum(ds, axis=0, keepdims=True))
            out = []
            for h, (dk, dv, dc) in enumerate(carry):
                dq_ref[pl.ds(off, bq), lanes[h]] += _nn(dsb[h], kj[h]) * scale
                out.append((dk + _tn(dsb[h], qs[h]), dv + _tn(pb[h], dob[h]), dc - dcs[h]))
            return tuple(out)

        init = tuple((jnp.zeros((bq, FOX_HD), F32), jnp.zeros((bq, FOX_HD), F32), jnp.zeros((1, bq), F32)) for _ in range(HP))
        res = lax.fori_loop(j + 1, nq, step, step(j, init, diagonal=True))
        for h, (dk, dv, dc) in enumerate(res):
            dk_ref[:, lanes[h]] = dk
            dv_ref[:, lanes[h]] = dv
            dc_ref[h] = dc

    W = HP * LANE
    full = lambda cb: pl.BlockSpec((T, W), lambda g, j: (0, cb // HP + g))
    blk = lambda cb: pl.BlockSpec((bq, W), lambda g, j: (j, cb // HP + g))
    col1 = pl.BlockSpec((HP, T, 1), lambda g, j: (g, 0, 0))
    return pl.pallas_call(
        kern, name="fox_attn_bwd",
        out_shape=[jax.ShapeDtypeStruct((T, 1024), F32)] * 3 + [jax.ShapeDtypeStruct((H, 1, T), F32)],
        grid=(H // HP, nq),
        in_specs=[full(0), blk(8), blk(16), pl.BlockSpec((HP, 1, T), lambda g, j: (g, 0, 0)), col1, full(0), col1],
        out_specs=[full(0), blk(0), blk(0), pl.BlockSpec((HP, 1, bq), lambda g, j: (g, 0, j))],
        compiler_params=_cp(("parallel", "arbitrary")),
    )(u, u, u, c3, lse, do, delta)


def _place():
    return lax.axis_index("x"), lax.axis_index("y"), lax.axis_index("c")


def _slot(p):
    return 4 * p[0] + 2 * p[1] + p[2]


def _other_chips(x, y):
    return [(1 - x, y), (x, 1 - y), (1 - x, 1 - y)]


def _allgather_steps(in_refs, out_refs, scratch):
    (src,), (dst,) = in_refs, out_refs
    send_sems, recv_sems, local_sem = scratch
    x, y, c = _place()
    me, sibling = (x, y, c), (x, y, 1 - c)
    chips = _other_chips(x, y)

    def copy(k, block, to, from_input=False):
        d = dst.at[_slot(block)]
        return pltpu.make_async_remote_copy(
            src_ref=src if from_input else d, dst_ref=d, send_sem=send_sems.at[k], recv_sem=recv_sems.at[k],
            device_id=to, device_id_type=MESH)

    def first_copies():
        return [copy(0, me, sibling, True)] + [copy(1 + j, me, (*chip, c), True) for j, chip in enumerate(chips)]

    def start():
        pltpu.make_async_copy(src, dst.at[_slot(me)], local_sem).start()
        for cp in first_copies():
            cp.start()

    def finish():
        passed = []
        for j, chip in enumerate(chips):
            copy(1 + j, (*chip, c), me).wait_recv()
            passed.append(copy(4 + j, (*chip, c), sibling))
            passed[-1].start()
        copy(0, sibling, me).wait_recv()
        for j, chip in enumerate(chips):
            copy(4 + j, (*chip, 1 - c), me).wait_recv()
        for cp in first_copies() + passed:
            cp.wait_send()
        pltpu.make_async_copy(src, dst.at[_slot(me)], local_sem).wait()

    return start, finish


def _allgather_relay_steps(in_refs, out_refs, scratch):
    (src,), (dst,) = in_refs, out_refs
    send_sems, recv_sems, local_sem = scratch
    x, y, c = _place()
    me, sibling = (x, y, c), (x, y, 1 - c)
    x_nbr, y_nbr, diag = (1 - x, y, c), (x, 1 - y, c), (1 - x, 1 - y, c)
    flip = lambda a, bit: a + bit - 2 * a * bit
    relay_from = (flip(x, 1 - c), flip(y, c), c)
    relay_to = (flip(x, c), flip(y, 1 - c), c)

    def copy(k, block, to, from_input=False):
        d = dst.at[_slot(block)]
        return pltpu.make_async_remote_copy(
            src_ref=src if from_input else d, dst_ref=d, send_sem=send_sems.at[k], recv_sem=recv_sems.at[k],
            device_id=to, device_id_type=MESH)

    def first_copies():
        return [copy(0, me, sibling, True), copy(1, me, x_nbr, True), copy(2, me, y_nbr, True)]

    def other(block):
        return block[:2] + (1 - c,)

    def start():
        pltpu.make_async_copy(src, dst.at[_slot(me)], local_sem).start()
        for cp in first_copies():
            cp.start()

    def finish():
        copy(1, x_nbr, me).wait_recv()
        copy(2, y_nbr, me).wait_recv()
        later = [copy(3, relay_from, relay_to), copy(4, x_nbr, sibling), copy(5, y_nbr, sibling)]
        for cp in later:
            cp.start()
        copy(3, diag, me).wait_recv()
        later.append(copy(6, diag, sibling))
        later[-1].start()
        copy(0, sibling, me).wait_recv()
        for k, block in ((4, x_nbr), (5, y_nbr), (6, diag)):
            copy(k, other(block), me).wait_recv()
        for cp in first_copies() + later:
            cp.wait_send()
        pltpu.make_async_copy(src, dst.at[_slot(me)], local_sem).wait()

    return start, finish


def _allgather_side(a, relay=False):
    return ((a,), (jax.ShapeDtypeStruct((N_DEV,) + a.shape, a.dtype),),
            (pltpu.SemaphoreType.DMA((7,)), pltpu.SemaphoreType.DMA((7,)), pltpu.SemaphoreType.DMA),
            _allgather_relay_steps if relay else _allgather_steps)


def _allgather(a, name, relay=False):
    ins, outs, scratch, make = _allgather_side(a, relay)

    def body(a_ref, o_ref, *scr):
        start, finish = make((a_ref,), (o_ref,), scr)
        start()
        finish()

    any_spec = pl.BlockSpec(memory_space=pl.ANY)
    return pl.pallas_call(body, name=name, out_shape=outs[0], in_specs=[any_spec], out_specs=any_spec,
                          scratch_shapes=list(scratch))(a)


def _exchange_pair(ga, gb):
    def body(ga_ref, gb_ref, ra_ref, rb_ref, send_sems, recv_sems):
        x, y, c = _place()
        sibling = (x, y, 1 - c)
        slots = [_slot(sibling)] + [_slot((*chip, 1 - c)) for chip in _other_chips(x, y)]
        cps = []
        for t, (src, dst) in enumerate(((ga_ref, ra_ref), (gb_ref, rb_ref))):
            for k, ps in enumerate(slots):
                cps.append(pltpu.make_async_remote_copy(
                    src_ref=src.at[ps], dst_ref=dst.at[k], send_sem=send_sems.at[t, k], recv_sem=recv_sems.at[t, k],
                    device_id=sibling, device_id_type=MESH))
        for cp in cps:
            cp.start()
        for cp in cps:
            cp.wait()

    any_spec = pl.BlockSpec(memory_space=pl.ANY)
    return pl.pallas_call(
        body, name="exchange_pair",
        out_shape=[jax.ShapeDtypeStruct((4,) + ga.shape[1:], ga.dtype), jax.ShapeDtypeStruct((4,) + gb.shape[1:], gb.dtype)],
        in_specs=[any_spec] * 2,
        out_specs=[any_spec] * 2,
        scratch_shapes=[pltpu.SemaphoreType.DMA((2, 4)), pltpu.SemaphoreType.DMA((2, 4))],
    )(ga, gb)


def _tiling(R, Cc, tile, by_cols):
    if by_cols:
        assert Cc % tile == 0
        return Cc // tile, (R, tile), lambda lead, i: (lead, 0, i)
    assert R % tile == 0
    return R // tile, (tile, Cc), lambda lead, i: (lead, i, 0)


def _pair_add(name, g, r1, slots, tile, by_cols=False):
    _, R, Cc = g.shape
    steps, blk, at = _tiling(R, Cc, tile, by_cols)

    def kern(s_ref, a_ref, b_ref, o_ref):
        o_ref[...] = (a_ref[...].astype(F32) + b_ref[...].astype(F32)).astype(o_ref.dtype)

    return pl.pallas_call(
        kern, name=name,
        out_shape=jax.ShapeDtypeStruct((3, R, Cc), BF16),
        grid_spec=pltpu.PrefetchScalarGridSpec(
            num_scalar_prefetch=1, grid=(3, steps),
            in_specs=[pl.BlockSpec((1,) + blk, lambda j, i, s: at(s[j], i)),
                      pl.BlockSpec((1,) + blk, lambda j, i, s: at(1 + j, i))],
            out_specs=pl.BlockSpec((1,) + blk, lambda j, i, s: at(j, i))),
        compiler_params=_cp(("arbitrary", "arbitrary")),
    )(slots, g, r1)


def _axis_neighbours():
    x, y, c = _place()
    flip = lambda a, bit: a + bit - 2 * a * bit
    return (flip(x, c), flip(y, 1 - c), c), (flip(x, 1 - c), flip(y, c), c), c


def _exchange_first_steps(in_refs, out_refs, scratch):
    pairs = list(zip(in_refs, out_refs))
    send_sems, recv_sems = scratch
    first, _, c = _axis_neighbours()

    def copies():
        return [pltpu.make_async_remote_copy(
            src_ref=src.at[j], dst_ref=dst.at[k], send_sem=send_sems.at[t, k], recv_sem=recv_sems.at[t, k],
            device_id=first, device_id_type=MESH)
            for t, (src, dst) in enumerate(pairs) for k, j in enumerate((1 - c, 2))]

    def start():
        for cp in copies():
            cp.start()

    def finish():
        for cp in copies():
            cp.wait()

    return start, finish


def _exchange_first_side(sa, sb):
    return ((sa, sb), (jax.ShapeDtypeStruct((2,) + sa.shape[1:], sa.dtype), jax.ShapeDtypeStruct((2,) + sb.shape[1:], sb.dtype)),
            (pltpu.SemaphoreType.DMA((2, 2)), pltpu.SemaphoreType.DMA((2, 2))), _exchange_first_steps)


def _axis_add(name, s, t1, core, tile, by_cols=False):
    _, R, Cc = s.shape
    steps, blk, at = _tiling(R, Cc, tile, by_cols)

    def kern(c_ref, a_ref, b_ref, o_ref):
        o_ref[...] = (a_ref[...].astype(F32) + b_ref[...].astype(F32)).astype(o_ref.dtype)

    return pl.pallas_call(
        kern, name=name,
        out_shape=jax.ShapeDtypeStruct((1, R, Cc), BF16),
        grid_spec=pltpu.PrefetchScalarGridSpec(
            num_scalar_prefetch=1, grid=(steps,),
            in_specs=[pl.BlockSpec((1,) + blk, lambda i, cr: at(cr[0], i)),
                      pl.BlockSpec((1,) + blk, lambda i, cr: at(1, i))],
            out_specs=pl.BlockSpec((1,) + blk, lambda i, cr: at(0, i))),
        compiler_params=_cp(("arbitrary",)),
    )(core, s, t1)


def _exchange_second_steps(in_refs, out_refs, scratch):
    send_sems, recv_sems = scratch
    _, second, _ = _axis_neighbours()

    def copies():
        return [pltpu.make_async_remote_copy(src_ref=src, dst_ref=dst, send_sem=send_sems.at[t], recv_sem=recv_sems.at[t],
                                             device_id=second, device_id_type=MESH)
                for t, (src, dst) in enumerate(zip(in_refs, out_refs))]

    def start():
        for cp in copies():
            cp.start()

    def finish():
        for cp in copies():
            cp.wait()

    return start, finish


def _exchange_second_side(pa, pb):
    return ((pa, pb), (jax.ShapeDtypeStruct(pa.shape, pa.dtype), jax.ShapeDtypeStruct(pb.shape, pb.dtype)),
            (pltpu.SemaphoreType.DMA((2,)), pltpu.SemaphoreType.DMA((2,))), _exchange_second_steps)


def _adamw(name, w, m, v, parts, tile, by_cols=False):
    R, Cc = w.shape
    steps, blk_shape, at = _tiling(R, Cc, tile, by_cols)
    n_parts = len(parts)

    def kern(*refs):
        w_ref, m_ref, v_ref = refs[:3]
        g = None
        for r_ref, (_, n) in zip(refs[3:3 + n_parts], parts):
            for s in range(n):
                term = r_ref[s].astype(F32)
                g = term if g is None else g + term
        g_out, d_out, m_out, v_out = refs[3 + n_parts:]
        mn = ADAM_B1 * m_ref[...] + (1.0 - ADAM_B1) * g
        vn = ADAM_B2 * v_ref[...] + (1.0 - ADAM_B2) * (g * g)
        m_hat = mn / (1.0 - ADAM_B1 ** ADAM_STEP)
        v_hat = vn / (1.0 - ADAM_B2 ** ADAM_STEP)
        g_out[...] = g
        d_out[...] = -ADAM_LR * (m_hat / (jnp.sqrt(v_hat) + ADAM_EPS) + ADAM_WD * w_ref[...])
        m_out[...] = mn
        v_out[...] = vn

    blk = pl.BlockSpec(blk_shape, lambda i: at(0, i)[1:])
    return pl.pallas_call(
        kern, name=name,
        out_shape=[jax.ShapeDtypeStruct((R, Cc), F32)] * 4,
        grid=(steps,),
        in_specs=[blk] * 3 + [pl.BlockSpec((n,) + blk_shape, lambda i: at(0, i)) for (_, n) in parts],
        out_specs=[blk] * 4,
        compiler_params=_cp(("arbitrary",)),
    )(w, m, v, *[a for (a, _) in parts])


def _assemble_columns(blocks, pieces, zeros, width):
    _, R, Cc = blocks.shape
    tr = min(256, R)

    def kern(b_ref, o_ref):
        for col, n in zeros:
            o_ref[:, col:col + n] = jnp.zeros((tr, n), o_ref.dtype)
        for col, d, lo, n in pieces:
            o_ref[:, col:col + n] = b_ref[d, :, lo:lo + n]

    return pl.pallas_call(
        kern, name="assemble_w_in",
        out_shape=jax.ShapeDtypeStruct((R, width), blocks.dtype),
        grid=(R // tr,),
        in_specs=[pl.BlockSpec((N_DEV, tr, Cc), lambda i: (0, i, 0))],
        out_specs=pl.BlockSpec((tr, width), lambda i: (i, 0)),
        compiler_params=_cp(("parallel",)),
    )(blocks)


def _split_rows(x, pieces, rows):
    _, Cc = x.shape
    tc = min(256, Cc)

    def kern(x_ref, o_ref):
        for d, lo, row, n in pieces:
            o_ref[d, lo:lo + n, :] = x_ref[row:row + n, :]

    return pl.pallas_call(
        kern, name="split_w_in_grad",
        out_shape=jax.ShapeDtypeStruct((N_DEV, rows, Cc), x.dtype),
        grid=(Cc // tc,),
        in_specs=[pl.BlockSpec((x.shape[0], tc), lambda i: (0, i))],
        out_specs=pl.BlockSpec((N_DEV, rows, tc), lambda i: (0, 0, i)),
        compiler_params=_cp(("parallel",)),
    )(x)


def _pad_cols(a, w):
    return jnp.pad(a, ((0, 0), (0, w - a.shape[1])))


def _pad_rows(a, r):
    return jnp.pad(a, ((0, r - a.shape[0]), (0, 0)))


def _pack_b(pf, pr, wo, w2, a2, rows):
    body = jnp.concatenate([pf, pr, wo.reshape(2048, 256), jnp.concatenate([w2, a2], axis=1)], axis=0)
    return _pad_rows(body, rows)


def kernel(x, norm_gain, w_in, fox_forget_bias, rwkv_shift_mix, rwkv_w0, rwkv_w2, rwkv_a0, rwkv_a2, rwkv_k_k, rwkv_k_a, rwkv_r_k, rwkv_ln_w, rwkv_ln_b, w_proj_fox, w_proj_rwkv, w_out, final_norm_gain, loss_target, m_norm_gain, m_w_in, m_fox_forget_bias, m_rwkv_shift_mix, m_rwkv_w0, m_rwkv_w2, m_rwkv_a0, m_rwkv_a2, m_rwkv_k_k, m_rwkv_k_a, m_rwkv_r_k, m_rwkv_ln_w, m_rwkv_ln_b, m_w_proj_fox, m_w_proj_rwkv, m_w_out, m_final_norm_gain, v_norm_gain, v_w_in, v_fox_forget_bias, v_rwkv_shift_mix, v_rwkv_w0, v_rwkv_w2, v_rwkv_a0, v_rwkv_a2, v_rwkv_k_k, v_rwkv_k_a, v_rwkv_r_k, v_rwkv_ln_w, v_rwkv_ln_b, v_w_proj_fox, v_w_proj_rwkv, v_w_out, v_final_norm_gain):
    T, D = x.shape[1], x.shape[2]
    assert D == 2048 and T % LANE == 0
    NI = w_in.shape[2]
    IN = N_DEV * NI
    RB = 4224
    x2 = x[0]
    lt2 = loss_target[0]
    me = _slot(_place())

    wa = _allgather(w_in[0].astype(BF16), "allgather_w_in", relay=True)
    packed_own = _pack_b(w_proj_fox[0], w_proj_rwkv[0], w_out[0], rwkv_w2[0], rwkv_a2[0], RB).astype(BF16)
    sections = [(0, 4096, 0), (4104, 4096, 4096), (8392, 4096, 8192), (4096, 8, 12288), (8200, 96, 12544), (8296, 96, 12672)]
    NP = 12800
    pieces, zeros, at_col = [], [], 0
    for lo, width, pad_lo in sections:
        if pad_lo > at_col:
            zeros.append((at_col, pad_lo - at_col))
        col = lo
        while col < lo + width:
            d = col // NI
            stop = min(lo + width, (d + 1) * NI)
            pieces.append((pad_lo + col - lo, d, col - d * NI, stop - col))
            col = stop
        at_col = pad_lo + width
    zeros.append((at_col, NP - at_col))
    w_pad = _assemble_columns(wa, pieces, zeros, NP)
    F_CB, LORA_CB = 96, 49

    mu = rwkv_shift_mix
    mu_main = mu[:, 0:4096]
    mu_lora = jnp.concatenate([_pad_cols(mu[:, 4096:4192], LANE), _pad_cols(mu[:, 4192:4288], LANE)], axis=1)
    bias_pad = _pad_cols(fox_forget_bias, LANE)
    rk_flat = rwkv_r_k.reshape(1, 1024)
    gf = final_norm_gain.reshape(1, D)

    tb = min(256, T)
    tbh = min(128, T)
    (h,) = _rowcall("rms_fwd", T, tb, [_rows(x2, tb, D), _whole(norm_gain)],
                    lambda i, xv, g: ([_rms_math(xv, g)], []), [(D, BF16)])
    u, wb = _mm(h, w_pad, tm=1024, tn=1280, name="mm_in", side=_allgather_side(packed_own))
    wpf = wb[:, 0:1024, :].transpose(1, 0, 2).reshape(1024, D)
    wpr = wb[:, 1024:2048, :].transpose(1, 0, 2).reshape(1024, D)
    wo = wb[:, 2048:4096, :].reshape(N_DEV * 256, D)
    w2p = _pad_rows(wb[:, 4096:4192, 0:128].transpose(1, 0, 2).reshape(96, 1024).astype(F32), LANE)
    a2p = _pad_rows(wb[:, 4096:4192, 128:256].transpose(1, 0, 2).reshape(96, 1024).astype(F32), LANE)

    c8 = _gates_fwd(u, bias_pad, T, F_CB)
    c3 = c8.reshape(8, 1, T)
    o_raw, o_a, lse = _attn_fwd(u, c3, T)

    def shift_body(i, um, hm, ul, hl, mm_, ml):
        outs = []
        for uv, hv, mv in ((um, hm, mm_), (ul, hl, ml)):
            hv = jnp.where(i == 0, 0.0, hv)
            prev = pltpu.roll(jnp.concatenate([hv, uv], axis=0), 1, 0)[SUB:]
            outs.append(uv + (prev - uv) * mv)
        return outs, []

    def halo_prev(arr, w, cb):
        return (arr, (SUB, w), lambda i: (jnp.maximum(i * (tbh // SUB) - 1, 0), cb))

    xs, xl = _rowcall("rwkv_shift_fwd", T, tbh,
                      [_rows(u, tbh, 4096, 1), halo_prev(u, 4096, 1), _rows(u, tbh, 256, LORA_CB), halo_prev(u, 256, LORA_CB),
                       _whole(mu_main), _whole(mu_lora)],
                      shift_body, [(4096, F32), (256, F32)])

    prep_par = [_whole(rwkv_w0), _whole(rwkv_a0), _whole(rwkv_k_k), _whole(rwkv_k_a), _whole(w2p), _whole(a2p)]
    prep_rows = [_rows(xs, tbh, 1024, 1), _rows(xl, tbh, LANE, 0), _rows(xl, tbh, LANE, 1)]
    lw, kp, an, bb = _rowcall("rwkv_prep_fwd", T, tbh, prep_rows + prep_par,
                              lambda i, *a: (list(_prep_math(*a)), []), [(1024, F32)] * 4)
    y, states, invs = _scan_fwd(xs, lw, kp, an, bb, T)
    post_rows = [_rows(y, tbh, 1024), _rows(xs, tbh, 1024, 0), _rows(kp, tbh, 1024), _rows(xs, tbh, 1024, 2), _rows(xs, tbh, 1024, 3)]
    post_par = [_whole(rwkv_ln_w), _whole(rwkv_ln_b), _whole(rk_flat)]
    (o_b,) = _rowcall("rwkv_post_fwd", T, tbh, post_rows + post_par,
                      lambda i, *a: ([_post_math(*a)], []), [(1024, BF16)])

    pa = _mm(o_a, wpf, name="mm_proj_fox")
    pb = _mm(o_b, wpr, name="mm_proj_rwkv")
    merge_rows = [_rows(u, tb, D, 4), _rows(u, tb, D, 5), _rows(pa, tb, D), _rows(pb, tb, D)]
    (mg,) = _rowcall("merge_fwd", T, tb, merge_rows, lambda i, *a: ([_merge_math(*a)], []), [(D, BF16)])
    mo = _mm(mg, wo, name="mm_out")

    def head_body(i, xv, mov, ltv, g):
        out = xv + mov
        r = lax.rsqrt(jnp.mean(out * out, axis=-1, keepdims=True) + RMS_EPS)
        yn = out * r
        err = yn * g - ltv
        loss = 0.5 * jnp.sum(jnp.sum(err * err, axis=-1, keepdims=True), axis=0, keepdims=True) / D
        dyv = err / D
        dyn = dyv * g
        dout = r * (dyn - yn * jnp.mean(dyn * yn, axis=-1, keepdims=True))
        return [dout], [loss, jnp.sum(dyv * yn, axis=0, keepdims=True)]

    dout, loss_p, dgf_p = _rowcall("loss_head", T, tb, [_rows(x2, tb, D), _rows(mo, tb, D), _rows(lt2, tb, D), _whole(gf)],
                                   head_body, [(D, F32)], [(1, 1), (1, D)])

    dm = _mm(dout, wo, tb=True, name="mm_out_dx")
    dwo = _mm(mg, dout, ta=True, out_dtype=BF16, name="mm_out_dw")

    def merge_bwd_body(i, ga, gb, pav, pbv, dmv):
        _, vjp = jax.vjp(_merge_math, ga, gb, pav, pbv)
        dga, dgb, dpa, dpb = vjp(dmv)
        return [dga, dgb, dpa, dpb], []

    dga, dgb, dpa, dpb = _rowcall("merge_bwd", T, tb, merge_rows + [_rows(dm, tb, D)], merge_bwd_body,
                                  [(D, BF16), (D, BF16), (D, BF16), (D, BF16)])
    doa = _mm(dpa, wpf, tb=True, name="mm_proj_fox_dx")
    dwpf = _mm(o_a, dpa, ta=True, out_dtype=BF16, name="mm_proj_fox_dw")
    dob = _mm(dpb, wpr, tb=True, name="mm_proj_rwkv_dx")
    dwpr = _mm(o_b, dpb, ta=True, out_dtype=BF16, name="mm_proj_rwkv_dw")

    do_b, dza, delta128 = _rowcall("fox_attn_pre", T, tb, [_rows(doa, tb, 1024), _rows(u, tb, 1024, 3), _rows(o_raw, tb, 1024)],
                                   lambda i, *a: (list(_attn_pre_math(*a)), []), [(1024, BF16), (1024, F32), (LANE, F32)])
    delta = delta128[:, 0:8].T.reshape(8, T, 1)
    dq, dk, dv, dc3 = _attn_bwd(u, c3, lse, do_b, delta, T)
    dfl, dbias_p = _gates_bwd(dc3.reshape(8, T), u, bias_pad, T, F_CB)

    def post_bwd_body(i, yv, rv, kpv, vv, zv, lnw, lnb, rkv, dobv):
        _, vjp = jax.vjp(_post_math, yv, rv, kpv, vv, zv, lnw, lnb, rkv)
        dy_, dr_, dkp_, dv_, dz_, dlnw, dlnb, drk = vjp(dobv)
        return [dy_, dr_, dkp_, dv_, dz_], [dlnw, dlnb, drk]

    dy_s, dr_p, dkp_p, dv_p, dzb, dlnw_p, dlnb_p, drk_p = _rowcall(
        "rwkv_post_bwd", T, tbh, post_rows + post_par + [_rows(dob, tbh, 1024)], post_bwd_body,
        [(1024, F32)] * 5, [(1, 1024)] * 3)
    dr_s, dlw, dkp_s, dv_s, dan, dbb = _scan_bwd(xs, lw, kp, an, bb, states, invs, dy_s, T)

    def prep_bwd_body(i, xk, xwd, xad, w0, a0, kkw, kaw, w2v, a2v, dlw_, dkp1, dkp2, dan_, dbb_, dr1, dr2, dv1, dv2, dz_):
        _, vjp = jax.vjp(_prep_math, xk, xwd, xad, w0, a0, kkw, kaw, w2v, a2v)
        dxk, dxwd, dxad, dw0, da0, dkk, dka, dw2, da2 = vjp((dlw_, dkp1 + dkp2, dan_, dbb_))
        return [[dr1 + dr2, dxk, dv1 + dv2, dz_], [dxwd, dxad]], [dw0, da0, dkk, dka, dw2, da2]

    cots = [dlw, dkp_s, dkp_p, dan, dbb, dr_s, dr_p, dv_s, dv_p, dzb]
    dxs, dxl, dw0_p, da0_p, dkk_p, dka_p, dw2_p, da2_p = _rowcall(
        "rwkv_prep_bwd", T, tbh, prep_rows + prep_par + [_rows(c_, tbh, 1024) for c_ in cots], prep_bwd_body,
        [(4096, F32), (256, F32)], [(1, 1024)] * 4 + [(LANE, 1024)] * 2)

    def shift_bwd_body(i, dm_, hm, dl_, hl, um, pm, ul, pl_, mm_, ml):
        last = i == T // tbh - 1
        outs, accs = [], []
        for dv_, hv, uv, pv, mv in ((dm_, hm, um, pm, mm_), (dl_, hl, ul, pl_, ml)):
            hv = jnp.where(last, 0.0, hv)
            nxt = pltpu.roll(jnp.concatenate([dv_, hv], axis=0), tbh + SUB - 1, 0)[:tbh]
            pv = jnp.where(i == 0, 0.0, pv)
            prev = pltpu.roll(jnp.concatenate([pv, uv], axis=0), 1, 0)[SUB:]
            outs.append(dv_ * (1.0 - mv) + nxt * mv)
            accs.append(jnp.sum(dv_ * (prev - uv), axis=0, keepdims=True))
        return outs, accs

    def halo_next(arr, w, cb):
        last_blk = T // SUB - 1
        return (arr, (SUB, w), lambda i: (jnp.minimum((i + 1) * (tbh // SUB), last_blk), cb))

    du_b, du_l, dmu_main_p, dmu_lora_p = _rowcall(
        "rwkv_shift_bwd", T, tbh,
        [_rows(dxs, tbh, 4096), halo_next(dxs, 4096, 0), _rows(dxl, tbh, 256), halo_next(dxl, 256, 0),
         _rows(u, tbh, 4096, 1), halo_prev(u, 4096, 1), _rows(u, tbh, 256, LORA_CB), halo_prev(u, 256, LORA_CB),
         _whole(mu_main), _whole(mu_lora)],
        shift_bwd_body, [(4096, BF16), (256, BF16)], [(1, 4096), (1, 256)])

    du = jnp.concatenate([dq.astype(BF16), dk.astype(BF16), dv.astype(BF16), dza.astype(BF16), du_b, dga, dgb,
                          dfl.astype(BF16), jnp.zeros((T, LANE), BF16), du_l], axis=1)
    dw_pad_t = _mm(du, h, ta=True, out_dtype=BF16, tm=1280, tn=1024, name="mm_in_dw")

    ga = _split_rows(dw_pad_t, [(d, lo, pad_col, n) for pad_col, d, lo, n in pieces], NI)
    lora_g = jnp.concatenate([dw2_p[:96].reshape(96, N_DEV, 128).transpose(1, 0, 2),
                              da2_p[:96].reshape(96, N_DEV, 128).transpose(1, 0, 2)], axis=2).astype(BF16)
    gb = jnp.concatenate([dwpf.reshape(1024, N_DEV, 256).transpose(1, 0, 2),
                          dwpr.reshape(1024, N_DEV, 256).transpose(1, 0, 2),
                          dwo.reshape(N_DEV, 2048, 256), lora_g, jnp.zeros((N_DEV, RB - 4192, 256), BF16)], axis=1)

    xx, yy, cc = _place()
    chip_slots = jnp.stack([_slot((*chip, cc)) for chip in _other_chips(xx, yy)]).astype(jnp.int32)
    r1a, r1b = _exchange_pair(ga, gb)
    sa = _pair_add("pair_add_w_in", ga, r1a, chip_slots, 512, by_cols=True)
    sb = _pair_add("pair_add_packed", gb, r1b, chip_slots, 384)
    ga_own = lax.dynamic_index_in_dim(ga, me, 0, keepdims=True)
    gb_own = lax.dynamic_index_in_dim(gb, me, 0, keepdims=True)
    dh, t1a, t1b = _mm(du, w_pad, tb=True, tm=1024, tn=1024, tk=NP // 10, name="mm_in_dx", side=_exchange_first_side(sa, sb))
    core = jnp.stack([cc]).astype(jnp.int32)
    second = _exchange_second_side(_axis_add("axis_add_w_in", sa, t1a, core, 512, by_cols=True),
                                   _axis_add("axis_add_packed", sb, t1b, core, 384))

    def rms_bwd_body(i, xv, g, dhv, doutv):
        _, vjp = jax.vjp(_rms_math, xv, g)
        dx_, dg_ = vjp(dhv)
        return [dx_ + doutv], [dg_]

    grad_x2, dng_p, t2a, t2b = _rowcall(
        "rms_bwd", T, tb, [_rows(x2, tb, D), _whole(norm_gain), _rows(dh, tb, D), _rows(dout, tb, D)],
        rms_bwd_body, [(D, F32)], [(1, D)], side=second)

    dmu = jnp.concatenate([dmu_main_p, dmu_lora_p[:, 0:96], dmu_lora_p[:, 128:224]], axis=1)
    small_parts = [dng_p, dbias_p[:, 0:8], dmu, dw0_p, da0_p, dkk_p, dka_p, drk_p, dlnw_p, dlnb_p, dgf_p, loss_p]
    SR = 128
    small = _pad_cols(jnp.concatenate(small_parts, axis=1), SR * LANE).reshape(SR, LANE)
    rs = _allgather(small, "allgather_small")

    w_in_outs = _adamw("adamw_w_in", w_in[0].T, m_w_in[0].T, v_w_in[0].T, [(ga_own, 1), (r1a, 1), (t1a, 1), (t2a, 1)], 256, by_cols=True)
    g_in, d_in, m_in, v_in = [o.T for o in w_in_outs]
    pk = lambda pf, pr, wo_, w2_, a2_: _pack_b(pf[0], pr[0], wo_[0], w2_[0], a2_[0], RB)
    outs_b = _adamw("adamw_packed", pk(w_proj_fox, w_proj_rwkv, w_out, rwkv_w2, rwkv_a2),
                    pk(m_w_proj_fox, m_w_proj_rwkv, m_w_out, m_rwkv_w2, m_rwkv_a2),
                    pk(v_w_proj_fox, v_w_proj_rwkv, v_w_out, v_rwkv_w2, v_rwkv_a2), [(gb_own, 1), (r1b, 1), (t1b, 1), (t2b, 1)], 384)

    def pack_small(ng, fb, sm, w0, a0, kk_, ka_, rk_, lnw, lnb, fg):
        parts = [ng, fb, sm, w0, a0, kk_, ka_, rk_.reshape(1, 1024), lnw, lnb, fg.reshape(1, D), jnp.zeros((1, 1), F32)]
        return _pad_cols(jnp.concatenate(parts, axis=1), SR * LANE).reshape(SR, LANE)

    outs_s = _adamw("adamw_small",
                    pack_small(norm_gain, fox_forget_bias, rwkv_shift_mix, rwkv_w0, rwkv_a0, rwkv_k_k, rwkv_k_a, rwkv_r_k,
                               rwkv_ln_w, rwkv_ln_b, final_norm_gain),
                    pack_small(m_norm_gain, m_fox_forget_bias, m_rwkv_shift_mix, m_rwkv_w0, m_rwkv_a0, m_rwkv_k_k, m_rwkv_k_a,
                               m_rwkv_r_k, m_rwkv_ln_w, m_rwkv_ln_b, m_final_norm_gain),
                    pack_small(v_norm_gain, v_fox_forget_bias, v_rwkv_shift_mix, v_rwkv_w0, v_rwkv_a0, v_rwkv_k_k, v_rwkv_k_a,
                               v_rwkv_r_k, v_rwkv_ln_w, v_rwkv_ln_b, v_final_norm_gain),
                    [(rs, N_DEV)], SR)

    def unpack_b(pkd):
        return dict(w_proj_fox=pkd[0:1024][None], w_proj_rwkv=pkd[1024:2048][None], w_out=pkd[2048:4096].reshape(1, 256, D),
                    rwkv_w2=pkd[4096:4192, 0:128][None], rwkv_a2=pkd[4096:4192, 128:256][None])

    def unpack_s(pkd):
        flat = pkd.reshape(1, SR * LANE)
        names = [("norm_gain", D), ("fox_forget_bias", 8), ("rwkv_shift_mix", 4288), ("rwkv_w0", 1024), ("rwkv_a0", 1024),
                 ("rwkv_k_k", 1024), ("rwkv_k_a", 1024), ("rwkv_r_k", 1024), ("rwkv_ln_w", 1024), ("rwkv_ln_b", 1024),
                 ("final_norm_gain", D), ("loss", 1)]
        out, off = {}, 0
        for nm, n in names:
            out[nm] = flat[:, off:off + n]
            off += n
        out["rwkv_r_k"] = out["rwkv_r_k"].reshape(1, 16, 64)
        out["final_norm_gain"] = out["final_norm_gain"].reshape(D)
        return out

    order = ["norm_gain", "w_in", "fox_forget_bias", "rwkv_shift_mix", "rwkv_w0", "rwkv_w2", "rwkv_a0", "rwkv_a2", "rwkv_k_k",
             "rwkv_k_a", "rwkv_r_k", "rwkv_ln_w", "rwkv_ln_b", "w_proj_fox", "w_proj_rwkv", "w_out", "final_norm_gain"]
    result = []
    loss = None
    for kind, big in enumerate((g_in, d_in, m_in, v_in)):
        d = {**unpack_b(outs_b[kind]), **unpack_s(outs_s[kind]), "w_in": big[None]}
        if kind == 0:
            loss = d["loss"].reshape(())
        result += [d[n] for n in order]
    return (loss, grad_x2[None], *result)
```

```python
import functools

import jax
import jax.numpy as jnp
from jax import lax
from jax.experimental import pallas as pl
from jax.experimental.pallas import tpu as pltpu

F32 = jnp.float32
BF16 = jnp.bfloat16
HI = lax.Precision.HIGHEST
H3 = lax.Precision.HIGH
MESH = pl.DeviceIdType.MESH

FOX_HD = 128
RW_HD = 64
RMS_EPS = 1e-6
GN_EPS = 64e-5
L2_EPS = 1e-12
ADAM_LR = 0.001
ADAM_B1 = 0.9
ADAM_B2 = 0.999
ADAM_EPS = 1e-08
ADAM_WD = 0.01
ADAM_STEP = 10

LANE = 128
SUB = 8
VMEM_LIMIT = 56 * 1024 * 1024
N_DEV = 8
CHUNK = 128
SCAN_GROUP = 2
SCAN_PAIRS = 2
PS = None
NEG = -1e30


def _scan_shape(T):
    c = min(CHUNK, T)
    return c, min(SCAN_GROUP, T // c)


def _cp(sem=None):
    return pltpu.CompilerParams(dimension_semantics=sem, vmem_limit_bytes=VMEM_LIMIT)


def _sigmoid(x):
    return jax.nn.sigmoid(x)


def _softplus(x):
    return jnp.maximum(x, 0.0) + jnp.log(1.0 + jnp.exp(-jnp.abs(x)))


def _nn(a, b, prec=None):
    return lax.dot_general(a, b, (((1,), (0,)), ((), ())), precision=prec, preferred_element_type=F32)


def _nt(a, b, prec=None):
    return lax.dot_general(a, b, (((1,), (1,)), ((), ())), precision=prec, preferred_element_type=F32)


def _tn(a, b, prec=None):
    return lax.dot_general(a, b, (((0,), (0,)), ((), ())), precision=prec, preferred_element_type=F32)


def _iota2(shape, dim):
    return lax.broadcasted_iota(jnp.int32, shape, dim)


def _seg_sum(x):
    r = _iota2((LANE, LANE), 0) // RW_HD
    c = _iota2((LANE, LANE), 1) // RW_HD
    bd = (r == c).astype(F32)
    parts = [_nn(x[:, j * LANE:(j + 1) * LANE], bd, H3) for j in range(x.shape[1] // LANE)]
    return parts[0] if len(parts) == 1 else jnp.concatenate(parts, axis=1)


def _mm(a, b, *, ta=False, tb=False, out_dtype=F32, tm=1024, tn=1024, tk=None, name, side=None):
    assert not (ta and tb)
    K, M = a.shape if ta else a.shape[::-1]
    N = b.shape[0] if tb else b.shape[1]
    tm, tn = min(tm, M), min(tn, N)
    tk = K if tk is None else tk
    nk = K // tk
    assert M % tm == 0 and N % tn == 0 and K % tk == 0
    gi, gj = M // tm, N // tn
    a_spec = pl.BlockSpec((tk, tm), lambda i, j, k: (k, i)) if ta else pl.BlockSpec((tm, tk), lambda i, j, k: (i, k))
    b_spec = pl.BlockSpec((tn, tk), lambda i, j, k: (j, k)) if tb else pl.BlockSpec((tk, tn), lambda i, j, k: (k, j))
    side_ins, side_outs, side_scr, side_make = side if side is not None else ((), (), (), None)
    n_si, n_so = len(side_ins), len(side_outs)
    n_acc = 0 if nk == 1 else 1

    def body(*refs):
        a_ref, b_ref = refs[:2]
        o_ref = refs[2 + n_si]
        scr = refs[3 + n_si + n_so:]
        k = pl.program_id(2)
        if side_make is not None:
            start, finish = side_make(refs[2:2 + n_si], refs[3 + n_si:3 + n_si + n_so], scr[n_acc:])
            first = jnp.logical_and(jnp.logical_and(pl.program_id(0) == 0, pl.program_id(1) == 0), k == 0)
            last = jnp.logical_and(jnp.logical_and(pl.program_id(0) == gi - 1, pl.program_id(1) == gj - 1), k == nk - 1)
            pl.when(first)(start)
        av = a_ref[...].astype(BF16)
        bv = b_ref[...].astype(BF16)
        p = _tn(av, bv) if ta else _nt(av, bv) if tb else _nn(av, bv)
        if nk == 1:
            o_ref[...] = p.astype(out_dtype)
        else:
            acc_ref = scr[0]

            @pl.when(k == 0)
            def _():
                acc_ref[...] = p

            @pl.when(k > 0)
            def _():
                acc_ref[...] += p

            @pl.when(k == nk - 1)
            def _():
                o_ref[...] = acc_ref[...].astype(out_dtype)
        if side_make is not None:
            pl.when(last)(finish)

    any_spec = pl.BlockSpec(memory_space=pl.ANY)
    res = pl.pallas_call(
        body, name=name,
        out_shape=[jax.ShapeDtypeStruct((M, N), out_dtype)] + list(side_outs),
        grid=(gi, gj, nk),
        in_specs=[a_spec, b_spec] + [any_spec] * n_si,
        out_specs=[pl.BlockSpec((tm, tn), lambda i, j, k: (i, j))] + [any_spec] * n_so,
        scratch_shapes=([] if nk == 1 else [pltpu.VMEM((tm, tn), F32)]) + list(side_scr),
        compiler_params=_cp(("arbitrary",) * 3 if side is not None else ("parallel", "parallel", "arbitrary")),
    )(a, b, *side_ins)
    return res if side is not None else res[0]


def _rows(arr, tb, w, cb=0):
    return (arr, (tb, w), lambda i: (i, cb))


def _whole(arr):
    nd = arr.ndim
    return (arr, arr.shape, lambda i: (0,) * nd)


def _rowcall(name, T, tb, ins, body, outs, accs=(), side=None):
    n_in, n_out, n_acc = len(ins), len(outs), len(accs)
    side_ins, side_outs, side_scr, side_make = side if side is not None else ((), (), (), None)
    n_si = len(side_ins)

    def kern(*refs):
        i = pl.program_id(0)
        if side_make is not None:
            side_refs = refs[n_in + n_si + n_out + n_acc:]
            start, finish = side_make(refs[n_in:n_in + n_si], side_refs[:len(side_outs)], side_refs[len(side_outs):])
            pl.when(i == 0)(start)
            refs = refs[:n_in] + refs[n_in + n_si:]
        vals = [r[...] for r in refs[:n_in]]
        ro, ao = body(i, *vals)
        for r, v in zip(refs[n_in:n_in + n_out], ro):
            if isinstance(v, (list, tuple)):
                off = 0
                for piece in v:
                    w = piece.shape[1]
                    r[:, off:off + w] = piece.astype(r.dtype)
                    off += w
            else:
                r[...] = v.astype(r.dtype)
        if accs:
            acc_refs = refs[n_in + n_out:n_in + n_out + n_acc]

            @pl.when(i == 0)
            def _():
                for r in acc_refs:
                    r[...] = jnp.zeros(r.shape, F32)

            for r, v in zip(acc_refs, ao):
                r[...] += v
        if side_make is not None:
            pl.when(i == T // tb - 1)(finish)

    any_spec = pl.BlockSpec(memory_space=pl.ANY)
    out_shape = [jax.ShapeDtypeStruct((T, w), dt) for (w, dt) in outs] + [jax.ShapeDtypeStruct(s, F32) for s in accs]
    out_specs = [pl.BlockSpec((tb, w), lambda i: (i, 0)) for (w, dt) in outs] + [pl.BlockSpec(s, lambda i: (0, 0)) for s in accs]
    res = pl.pallas_call(
        kern, name=name,
        out_shape=out_shape + list(side_outs),
        grid=(T // tb,),
        in_specs=[pl.BlockSpec(bs, im) for (_, bs, im) in ins] + [any_spec] * n_si,
        out_specs=out_specs + [any_spec] * len(side_outs),
        scratch_shapes=list(side_scr),
        compiler_params=_cp(("arbitrary",)),
    )(*[a for (a, _, _) in ins], *side_ins)
    return res


def _rms_math(x, g):
    r = lax.rsqrt(jnp.mean(x * x, axis=-1, keepdims=True) + RMS_EPS)
    return x * r * g


def _merge_math(ga, gb, pa, pb):
    return _sigmoid(ga) * pa + _sigmoid(gb) * pb


def _prep_math(xk, xwd, xad, w0, a0, kk_w, ka_w, w2p, a2p):
    z = w0 + _nn(jnp.tanh(xwd), w2p, H3)
    w = -_softplus(-z) - 0.5
    lw = -jnp.exp(w)
    ag = _sigmoid(a0 + _nn(xad, a2p, H3))
    p = xk * kk_w
    n = jnp.maximum(jnp.sqrt(_seg_sum(p * p)), L2_EPS)
    kk = p / n
    kp = xk * (1.0 + (ag - 1.0) * ka_w)
    return lw, kp, -kk, kk * ag


def _post_math(y, r, kp, v, z, lnw, lnb, rk):
    inv = 1.0 / RW_HD
    mu = _seg_sum(y) * inv
    d = y - mu
    var = _seg_sum(d * d) * inv
    yn = d * lax.rsqrt(var + GN_EPS) * lnw + lnb
    bonus = _seg_sum(r * kp * rk) * v
    return (yn + bonus) * (z * _sigmoid(z))


def _neumann(ms, depth):
    eye = (_iota2(ms[0].shape, 0) == _iota2(ms[0].shape, 1)).astype(F32)
    width = ms[0].shape[1]
    mp = [_nn(m, m, PS) for m in ms]
    inv = [eye + m for m in ms]
    n = 2
    while n < depth:
        last = 2 * n >= depth
        for i in range(len(ms)):
            if last:
                inv[i] = inv[i] + _nn(mp[i], inv[i], PS)
            else:
                z = _nn(mp[i], jnp.concatenate([mp[i], inv[i]], axis=1), PS)
                mp[i], inv[i] = z[:, :width], inv[i] + z[:, width:]
        n *= 2
    return inv


@functools.partial(jax.custom_vjp, nondiff_argnums=(1,))
def _unit_inverses(ms, depth):
    if depth <= RW_HD:
        return tuple(_neumann(list(ms), depth))
    h, n = depth // 2, len(ms)
    cat = jnp.concatenate
    z = jnp.zeros((h, h), F32)

    def heads(m, r, c):
        b0 = m[r * h:(r + 1) * h, c * h:(c + 1) * h]
        b1 = m[depth + r * h:depth + (r + 1) * h, depth + c * h:depth + (c + 1) * h]
        return cat([cat([b0, z], axis=1), cat([z, b1], axis=1)], axis=0)

    diag = _neumann([heads(m, 0, 0) for m in ms] + [heads(m, 1, 1) for m in ms], h)
    ta, td = diag[:n], diag[n:]
    low = [_nn(heads(m, 1, 0), a, PS) for m, a in zip(ms, ta)]
    low = [_nn(d, x, PS) for d, x in zip(td, low)]
    out = []
    for a, x, d in zip(ta, low, td):
        rows = []
        for hd in (0, 1):
            sl = slice(hd * h, (hd + 1) * h)
            top, bot = [a[sl, sl], z], [x[sl, sl], d[sl, sl]]
            pad = [z, z]
            rows.append(cat(top + pad if hd == 0 else pad + top, axis=1))
            rows.append(cat(bot + pad if hd == 0 else pad + bot, axis=1))
        out.append(cat(rows, axis=0))
    return tuple(out)


def _unit_inverses_fwd(ms, depth):
    inv = _unit_inverses(ms, depth)
    return inv, inv


def _unit_inverses_bwd(depth, inv, cts):
    left = [_tn(t, g, PS) for t, g in zip(inv, cts)]
    return (tuple(_nt(l, t, PS) for l, t in zip(left, inv)),)


_unit_inverses.defvjp(_unit_inverses_fwd, _unit_inverses_bwd)


@jax.custom_vjp
def _known_inverses(ms, inv):
    return inv


def _known_inverses_fwd(ms, inv):
    return inv, inv


def _known_inverses_bwd(inv, cts):
    dms = _unit_inverses_bwd(None, inv, cts)[0]
    return dms, tuple(jnp.zeros_like(t) for t in inv)


_known_inverses.defvjp(_known_inverses_fwd, _known_inverses_bwd)


def _scan_group(s0s, *flat, known_inv=None, with_inv=False):
    P = len(s0s)
    G = len(flat) // (6 * P)
    ch = [flat[6 * i:6 * i + 6] for i in range(P * G)]
    C = ch[0][0].shape[0]
    C2 = 2 * C
    cat = jnp.concatenate
    m0 = _iota2((1, LANE), 1) < RW_HD
    mask0 = m0.astype(F32)
    mask1 = 1.0 - mask0
    r2 = _iota2((C2, C2), 0)
    c2 = _iota2((C2, C2), 1)
    dist = r2 - c2
    in_head = dist <= r2 % C
    lower = (_iota2((C, C), 0) >= _iota2((C, C), 1)).astype(F32)
    bd = (_iota2((LANE, LANE), 0) // RW_HD) == (_iota2((LANE, LANE), 1) // RW_HD)

    def tri(m, strict):
        return jnp.where(dist > 0 if strict else dist >= 0, jnp.where(in_head, m, 0.0), 0.0)

    def sel(z):
        return jnp.where(m0, z[:C], z[C:])

    gs = [_nn(lower, c[1], HI) for c in ch]
    pre = []
    for (r, lw, k, v, a, b), g in zip(ch, gs):
        g_end = jnp.sum(lw, axis=0, keepdims=True)
        gm = g - jnp.sum(lw[:C // 2], axis=0, keepdims=True)
        en = jnp.exp(-gm)
        ec = jnp.exp(g_end - g)
        pre.append(dict(at=a * jnp.exp(g - lw), rt=r * jnp.exp(g), am=a * jnp.exp(gm - lw), rm=r * jnp.exp(gm),
                        bt=b * en, kt=k * en, bh=b * ec, kh=k * ec, dec=jnp.exp(g_end), v=v))
    grams = [_nt(cat([p["am"] * mask0, p["am"] * mask1, p["rm"] * mask0, p["rm"] * mask1], axis=0),
                 cat([p["bt"], p["bt"], p["kt"], p["kt"]], axis=0), PS) for p in pre]
    mab = tuple(tri(gm[:C2, :C2], True) for gm in grams)
    tinv = _unit_inverses(mab, C) if known_inv is None else _known_inverses(mab, known_inv)
    xv =[sel(_nn(tri(gm[:C2, C2:], True), cat([p["v"], p["v"]], axis=0), PS)) for gm, p in zip(grams, pre)]
    ys, s = [None] * (P * G), list(s0s)
    for i in range(G):
        for q in range(P):
            n = q * G + i
            p, gm = pre[n], grams[n]
            sx = _nt(cat([p["at"], p["rt"]], axis=0), s[q], PS)
            x = sx[:C] + xv[n]
            u = sel(_nn(tinv[n], cat([x, x], axis=0), PS))
            v = p["v"]
            ys[n] = sx[C:] + sel(_nn(cat([tri(gm[C2:, :C2], False), tri(gm[C2:, C2:], False)], axis=1),
                                     cat([u, u, v, v], axis=0), PS))
            s[q] = s[q] * p["dec"] + jnp.where(bd, _tn(cat([u, v], axis=0), cat([p["bh"], p["kh"]], axis=0), PS), 0.0)
    return (tuple(ys), tuple(s), tinv) if with_inv else (tuple(ys), tuple(s))


def _scan_fwd(xs, lw, kp, an, bb, T):
    C, G = _scan_shape(T)
    P = SCAN_PAIRS
    nc = T // (C * G)
    npair = 1024 // LANE

    def kern(r_ref, lw_ref, k_ref, v_ref, a_ref, b_ref, y_ref, st_ref, inv_ref, s_scr):
        n = pl.program_id(1)

        @pl.when(n == 0)
        def _():
            s_scr[...] = jnp.zeros(s_scr.shape, F32)

        st_ref[0] = s_scr[...]
        ins = (r_ref, lw_ref, k_ref, v_ref, a_ref, b_ref)
        ys, s1, inv = _scan_group(
            tuple(s_scr[q] for q in range(P)),
            *[ref[i * C:(i + 1) * C, q * LANE:(q + 1) * LANE] for q in range(P) for i in range(G) for ref in ins], with_inv=True)
        for q in range(P):
            for i in range(G):
                y_ref[i * C:(i + 1) * C, q * LANE:(q + 1) * LANE] = ys[q * G + i]
                inv_ref[0, 0, q * G + i] = inv[q * G + i]
            s_scr[q] = s1[q]

    def col(off):
        return pl.BlockSpec((C * G, P * LANE), lambda p, n: (n, off // P + p))

    return pl.pallas_call(
        kern, name="rwkv_scan_fwd",
        out_shape=[jax.ShapeDtypeStruct((T, 1024), F32), jax.ShapeDtypeStruct((nc, npair, LANE, LANE), F32),
                   jax.ShapeDtypeStruct((nc, npair // P, P * G, 2 * C, 2 * C), F32)],
        grid=(npair // P, nc),
        in_specs=[col(0), col(0), col(0), col(16), col(0), col(0)],
        out_specs=[col(0), pl.BlockSpec((1, P, LANE, LANE), lambda p, n: (n, p, 0, 0)),
                   pl.BlockSpec((1, 1, P * G, 2 * C, 2 * C), lambda p, n: (n, p, 0, 0, 0))],
        scratch_shapes=[pltpu.VMEM((P, LANE, LANE), F32)],
        compiler_params=_cp(("parallel", "arbitrary")),
    )(xs, lw, kp, xs, an, bb)


def _scan_bwd(xs, lw, kp, an, bb, states, invs, dy, T):
    C, G = _scan_shape(T)
    P = SCAN_PAIRS
    nc = T // (C * G)
    npair = 1024 // LANE

    def kern(r_ref, lw_ref, k_ref, v_ref, a_ref, b_ref, st_ref, inv_ref, dy_ref, dr_ref, dlw_ref, dk_ref, dv_ref, da_ref, db_ref,
             ds_scr):
        n = pl.program_id(1)

        @pl.when(n == 0)
        def _():
            ds_scr[...] = jnp.zeros(ds_scr.shape, F32)

        ins = (r_ref, lw_ref, k_ref, v_ref, a_ref, b_ref)
        units = [(q, i) for q in range(P) for i in range(G)]
        known = tuple(inv_ref[0, 0, q * G + i] for q, i in units)
        _, vjp = jax.vjp(functools.partial(_scan_group, known_inv=known), tuple(st_ref[0, q] for q in range(P)),
                         *[ref[i * C:(i + 1) * C, q * LANE:(q + 1) * LANE] for q, i in units for ref in ins])
        grads = vjp((tuple(dy_ref[i * C:(i + 1) * C, q * LANE:(q + 1) * LANE] for q, i in units),
                     tuple(ds_scr[q] for q in range(P))))
        for q in range(P):
            ds_scr[q] = grads[0][q]
        outs = (dr_ref, dlw_ref, dk_ref, dv_ref, da_ref, db_ref)
        for n_, (q, i) in enumerate(units):
            for t, ref in enumerate(outs):
                ref[i * C:(i + 1) * C, q * LANE:(q + 1) * LANE] = grads[1 + 6 * n_ + t]

    def col(off):
        return pl.BlockSpec((C * G, P * LANE), lambda p, n: (nc - 1 - n, off // P + p))

    return pl.pallas_call(
        kern, name="rwkv_scan_bwd",
        out_shape=[jax.ShapeDtypeStruct((T, 1024), F32)] * 6,
        grid=(npair // P, nc),
        in_specs=[col(0), col(0), col(0), col(16), col(0), col(0),
                  pl.BlockSpec((1, P, LANE, LANE), lambda p, n: (nc - 1 - n, p, 0, 0)),
                  pl.BlockSpec((1, 1, P * G, 2 * C, 2 * C), lambda p, n: (nc - 1 - n, p, 0, 0, 0)), col(0)],
        out_specs=[col(0)] * 6,
        scratch_shapes=[pltpu.VMEM((P, LANE, LANE), F32)],
        compiler_params=_cp(("parallel", "arbitrary")),
    )(xs, lw, kp, xs, an, bb, states, invs, dy)


def _gates_fwd(u, bias_pad, T, f_cb):
    nb = T // LANE

    def kern(f_ref, b_ref, c_ref):
        x = f_ref[...] + b_ref[...]
        lf = jnp.minimum(x, 0.0) - jnp.log(1.0 + jnp.exp(-jnp.abs(x)))
        lft = lf.T
        ut = (_iota2((LANE, LANE), 0) <= _iota2((LANE, LANE), 1)).astype(F32)
        carry = jnp.zeros((LANE, 1), F32)
        for blk in range(nb):
            seg = lft[:, blk * LANE:(blk + 1) * LANE]
            cs = _nn(seg, ut, HI) + carry
            c_ref[:, blk * LANE:(blk + 1) * LANE] = cs[:SUB, :]
            carry = carry + jnp.sum(seg, axis=1, keepdims=True)

    return pl.pallas_call(
        kern, name="fox_gates_fwd",
        out_shape=jax.ShapeDtypeStruct((SUB, T), F32),
        grid=(1,),
        in_specs=[pl.BlockSpec((T, LANE), lambda i: (0, f_cb)), pl.BlockSpec((1, LANE), lambda i: (0, 0))],
        out_specs=pl.BlockSpec((SUB, T), lambda i: (0, 0)),
        compiler_params=_cp(("arbitrary",)),
    )(u, bias_pad)


def _gates_bwd(dc, u, bias_pad, T, f_cb):
    nb = T // LANE

    def kern(dc_ref, f_ref, b_ref, dfl_ref, db_ref):
        dcv = jnp.concatenate([dc_ref[...], jnp.zeros((LANE - SUB, T), F32)], axis=0)
        lt = (_iota2((LANE, LANE), 0) >= _iota2((LANE, LANE), 1)).astype(F32)
        carry = jnp.zeros((LANE, 1), F32)
        pieces = [None] * nb
        for blk in range(nb - 1, -1, -1):
            seg = dcv[:, blk * LANE:(blk + 1) * LANE]
            pieces[blk] = _nn(seg, lt, HI) + carry
            carry = carry + jnp.sum(seg, axis=1, keepdims=True)
        dlf = (pieces[0] if nb == 1 else jnp.concatenate(pieces, axis=1)).T
        x = f_ref[...] + b_ref[...]
        dfl = dlf * _sigmoid(-x)
        dfl_ref[...] = dfl
        db_ref[...] = jnp.sum(dfl, axis=0, keepdims=True)

    return pl.pallas_call(
        kern, name="fox_gates_bwd",
        out_shape=[jax.ShapeDtypeStruct((T, LANE), F32), jax.ShapeDtypeStruct((1, LANE), F32)],
        grid=(1,),
        in_specs=[pl.BlockSpec((SUB, T), lambda i: (0, 0)), pl.BlockSpec((T, LANE), lambda i: (0, f_cb)),
                  pl.BlockSpec((1, LANE), lambda i: (0, 0))],
        out_specs=[pl.BlockSpec((T, LANE), lambda i: (0, 0)), pl.BlockSpec((1, LANE), lambda i: (0, 0))],
        compiler_params=_cp(("arbitrary",)),
    )(dc, u, bias_pad)


ATTN_HEADS = 4


def _attn_block(T):
    return 256 if T % 256 == 0 and T >= 512 else 128


def _attn_fwd(u, c3, T):
    H, HP = 8, ATTN_HEADS
    bq = _attn_block(T)
    nq = T // bq
    scale = FOX_HD ** -0.5
    lanes = [slice(h * LANE, (h + 1) * LANE) for h in range(HP)]

    def kern(q_ref, k_ref, v_ref, z_ref, cq_ref, ck_ref, o_ref, oa_ref, lse_ref):
        i = pl.program_id(1)
        q = [(q_ref[:, ln] * scale).astype(BF16) for ln in lanes]
        c0 = [cq_ref[h][:, 0:1] for h in range(HP)]

        def step(j, carry, diagonal=False):
            off = pl.multiple_of(j * bq, bq)
            s = [_nt(q[h], k_ref[pl.ds(off, bq), lanes[h]].astype(BF16)) + (c0[h] - ck_ref[h, :, pl.ds(off, bq)])
                 for h in range(HP)]
            ps, out = [], []
            for h in range(HP):
                m, l, acc = carry[h]
                sh = s[h]
                if diagonal:
                    sh = jnp.where(_iota2((bq, bq), 1) <= _iota2((bq, bq), 0), sh, NEG)
                m_new = jnp.maximum(m, jnp.max(sh, axis=1, keepdims=True))
                p = jnp.exp(sh - m_new)
                alpha = jnp.exp(m - m_new)
                p_hi = p.astype(BF16)
                ps.append((p_hi, (p - p_hi.astype(F32)).astype(BF16)))
                out.append((m_new, alpha * l + jnp.sum(p, axis=1, keepdims=True), alpha * acc))
            res = []
            for h, (m, l, acc) in enumerate(out):
                vj = v_ref[pl.ds(off, bq), lanes[h]].astype(BF16)
                res.append((m, l, acc + _nn(ps[h][0], vj) + _nn(ps[h][1], vj)))
            return tuple(res)

        init = tuple((jnp.full((bq, 1), NEG, F32), jnp.zeros((bq, 1), F32), jnp.zeros((bq, FOX_HD), F32)) for _ in range(HP))
        res = step(i, lax.fori_loop(0, i, step, init), diagonal=True)
        for h, (m, l, acc) in enumerate(res):
            o = acc / l
            z = z_ref[:, lanes[h]]
            o_ref[:, lanes[h]] = o
            oa_ref[:, lanes[h]] = (o * z * _sigmoid(z)).astype(BF16)
            lse_ref[h] = m + jnp.log(l)

    W = HP * LANE
    return pl.pallas_call(
        kern, name="fox_attn_fwd",
        out_shape=[jax.ShapeDtypeStruct((T, 1024), F32), jax.ShapeDtypeStruct((T, 1024), BF16),
                   jax.ShapeDtypeStruct((H, T, 1), F32)],
        grid=(H // HP, nq),
        in_specs=[pl.BlockSpec((bq, W), lambda g, i: (i, g)),
                  pl.BlockSpec((T, W), lambda g, i: (0, 8 // HP + g)),
                  pl.BlockSpec((T, W), lambda g, i: (0, 16 // HP + g)),
                  pl.BlockSpec((bq, W), lambda g, i: (i, 24 // HP + g)),
                  pl.BlockSpec((HP, 1, bq), lambda g, i: (g, 0, i)),
                  pl.BlockSpec((HP, 1, T), lambda g, i: (g, 0, 0))],
        out_specs=[pl.BlockSpec((bq, W), lambda g, i: (i, g)),
                   pl.BlockSpec((bq, W), lambda g, i: (i, g)),
                   pl.BlockSpec((HP, bq, 1), lambda g, i: (g, i, 0))],
        compiler_params=_cp(("parallel", "arbitrary")),
    )(u, u, u, u, c3, c3)


def _attn_probs(s, lse_i, diagonal):
    if not diagonal:
        return jnp.exp(s - lse_i)
    keep = _iota2(s.shape, 1) <= _iota2(s.shape, 0)
    return jnp.where(keep, jnp.exp(jnp.where(keep, s, NEG) - lse_i), 0.0)


def _attn_pre_math(doa, z, o):
    sg = _sigmoid(z)
    do = (doa * z * sg).astype(BF16)
    dz = doa * o * (sg * (1.0 + z * (1.0 - sg)))
    head_of = (_iota2((o.shape[1], LANE), 0) // FOX_HD == _iota2((o.shape[1], LANE), 1)).astype(F32)
    return do, dz, _nn(do.astype(F32) * o, head_of, HI)


def _attn_bwd(u, c3, lse, do, delta, T):
    H, HP = 8, ATTN_HEADS
    bq = _attn_block(T)
    nq = T // bq
    scale = FOX_HD ** -0.5
    lanes = [slice(h * LANE, (h + 1) * LANE) for h in range(HP)]

    def kern(q_ref, k_ref, v_ref, c_ref, lse_ref, do_ref, dl_ref, dq_ref, dk_ref, dv_ref, dc_ref):
        j = pl.program_id(1)

        @pl.when(j == 0)
        def _():
            dq_ref[...] = jnp.zeros(dq_ref.shape, F32)

        kj = [k_ref[:, ln].astype(BF16) for ln in lanes]
        vj = [v_ref[:, ln].astype(BF16) for ln in lanes]
        joff = pl.multiple_of(j * bq, bq)
        ck = [c_ref[h, :, pl.ds(joff, bq)] for h in range(HP)]

        def step(i, carry, diagonal=False):
            off = pl.multiple_of(i * bq, bq)
            qs = [(q_ref[pl.ds(off, bq), ln] * scale).astype(BF16) for ln in lanes]
            dob = [do_ref[pl.ds(off, bq), ln] for ln in lanes]
            s = [_nt(qs[h], kj[h]) + (c_ref[h, :, pl.ds(off, bq)][:, 0:1] - ck[h]) for h in range(HP)]
            dp = [_nt(dob[h], vj[h]) for h in range(HP)]
            pb, dsb, dcs = [], [], []
            for h in range(HP):
                p = _attn_probs(s[h], lse_ref[h, pl.ds(off, bq), :], diagonal)
                ds = p * (dp[h] - dl_ref[h, pl.ds(off, bq), :])
                pb.append(p.astype(BF16))
                dsb.append(ds.astype(BF16))
                dcs.append(jnp.sum(ds, axis=0, keepdims=True))
            out = []
            for h, (dk, dv, dc) in enumerate(carry):
                dq_ref[pl.ds(off, bq), lanes[h]] += _nn(dsb[h], kj[h]) * scale
                out.append((dk + _tn(dsb[h], qs[h]), dv + _tn(pb[h], dob[h]), dc - dcs[h]))
            return tuple(out)

        init = tuple((jnp.zeros((bq, FOX_HD), F32), jnp.zeros((bq, FOX_HD), F32), jnp.zeros((1, bq), F32)) for _ in range(HP))
        res = lax.fori_loop(j + 1, nq, step, step(j, init, diagonal=True))
        for h, (dk, dv, dc) in enumerate(res):
            dk_ref[:, lanes[h]] = dk
            dv_ref[:, lanes[h]] = dv
            dc_ref[h] = dc

    W = HP * LANE
    full = lambda cb: pl.BlockSpec((T, W), lambda g, j: (0, cb // HP + g))
    blk = lambda cb: pl.BlockSpec((bq, W), lambda g, j: (j, cb // HP + g))
    col1 = pl.BlockSpec((HP, T, 1), lambda g, j: (g, 0, 0))
    return pl.pallas_call(
        kern, name="fox_attn_bwd",
        out_shape=[jax.ShapeDtypeStruct((T, 1024), F32)] * 3 + [jax.ShapeDtypeStruct((H, 1, T), F32)],
        grid=(H // HP, nq),
        in_specs=[full(0), blk(8), blk(16), pl.BlockSpec((HP, 1, T), lambda g, j: (g, 0, 0)), col1, full(0), col1],
        out_specs=[full(0), blk(0), blk(0), pl.BlockSpec((HP, 1, bq), lambda g, j: (g, 0, j))],
        compiler_params=_cp(("parallel", "arbitrary")),
    )(u, u, u, c3, lse, do, delta)


def _place():
    return lax.axis_index("x"), lax.axis_index("y"), lax.axis_index("c")


def _slot(p):
    return 4 * p[0] + 2 * p[1] + p[2]


def _other_chips(x, y):
    return [(1 - x, y), (x, 1 - y), (1 - x, 1 - y)]


def _allgather_steps(in_refs, out_refs, scratch):
    (src,), (dst,) = in_refs, out_refs
    send_sems, recv_sems, local_sem = scratch
    x, y, c = _place()
    me, sibling = (x, y, c), (x, y, 1 - c)
    chips = _other_chips(x, y)

    def copy(k, block, to, from_input=False):
        d = dst.at[_slot(block)]
        return pltpu.make_async_remote_copy(
            src_ref=src if from_input else d, dst_ref=d, send_sem=send_sems.at[k], recv_sem=recv_sems.at[k],
            device_id=to, device_id_type=MESH)

    def first_copies():
        return [copy(0, me, sibling, True)] + [copy(1 + j, me, (*chip, c), True) for j, chip in enumerate(chips)]

    def start():
        pltpu.make_async_copy(src, dst.at[_slot(me)], local_sem).start()
        for cp in first_copies():
            cp.start()

    def finish():
        passed = []
        for j, chip in enumerate(chips):
            copy(1 + j, (*chip, c), me).wait_recv()
            passed.append(copy(4 + j, (*chip, c), sibling))
            passed[-1].start()
        copy(0, sibling, me).wait_recv()
        for j, chip in enumerate(chips):
            copy(4 + j, (*chip, 1 - c), me).wait_recv()
        for cp in first_copies() + passed:
            cp.wait_send()
        pltpu.make_async_copy(src, dst.at[_slot(me)], local_sem).wait()

    return start, finish


def _allgather_relay_steps(in_refs, out_refs, scratch):
    (src,), (dst,) = in_refs, out_refs
    send_sems, recv_sems, local_sem = scratch
    x, y, c = _place()
    me, sibling = (x, y, c), (x, y, 1 - c)
    x_nbr, y_nbr, diag = (1 - x, y, c), (x, 1 - y, c), (1 - x, 1 - y, c)
    flip = lambda a, bit: a + bit - 2 * a * bit
    relay_from = (flip(x, 1 - c), flip(y, c), c)
    relay_to = (flip(x, c), flip(y, 1 - c), c)

    def copy(k, block, to, from_input=False):
        d = dst.at[_slot(block)]
        return pltpu.make_async_remote_copy(
            src_ref=src if from_input else d, dst_ref=d, send_sem=send_sems.at[k], recv_sem=recv_sems.at[k],
            device_id=to, device_id_type=MESH)

    def first_copies():
        return [copy(0, me, sibling, True), copy(1, me, x_nbr, True), copy(2, me, y_nbr, True)]

    def other(block):
        return block[:2] + (1 - c,)

    def start():
        pltpu.make_async_copy(src, dst.at[_slot(me)], local_sem).start()
        for cp in first_copies():
            cp.start()

    def finish():
        copy(1, x_nbr, me).wait_recv()
        copy(2, y_nbr, me).wait_recv()
        later = [copy(3, relay_from, relay_to), copy(4, x_nbr, sibling), copy(5, y_nbr, sibling)]
        for cp in later:
            cp.start()
        copy(3, diag, me).wait_recv()
        later.append(copy(6, diag, sibling))
        later[-1].start()
        copy(0, sibling, me).wait_recv()
        for k, block in ((4, x_nbr), (5, y_nbr), (6, diag)):
            copy(k, other(block), me).wait_recv()
        for cp in first_copies() + later:
            cp.wait_send()
        pltpu.make_async_copy(src, dst.at[_slot(me)], local_sem).wait()

    return start, finish


def _allgather_side(a, relay=False):
    return ((a,), (jax.ShapeDtypeStruct((N_DEV,) + a.shape, a.dtype),),
            (pltpu.SemaphoreType.DMA((7,)), pltpu.SemaphoreType.DMA((7,)), pltpu.SemaphoreType.DMA),
            _allgather_relay_steps if relay else _allgather_steps)


def _allgather(a, name, relay=False):
    ins, outs, scratch, make = _allgather_side(a, relay)

    def body(a_ref, o_ref, *scr):
        start, finish = make((a_ref,), (o_ref,), scr)
        start()
        finish()

    any_spec = pl.BlockSpec(memory_space=pl.ANY)
    return pl.pallas_call(body, name=name, out_shape=outs[0], in_specs=[any_spec], out_specs=any_spec,
                          scratch_shapes=list(scratch))(a)


def _exchange_pair_steps(in_refs, out_refs, scratch):
    (src,), (dst,) = in_refs, out_refs
    send_sems, recv_sems = scratch
    x, y, c = _place()
    sibling = (x, y, 1 - c)
    slots = [_slot(sibling)] + [_slot((*chip, 1 - c)) for chip in _other_chips(x, y)]

    def copies():
        return [pltpu.make_async_remote_copy(
            src_ref=src.at[ps], dst_ref=dst.at[k], send_sem=send_sems.at[k], recv_sem=recv_sems.at[k],
            device_id=sibling, device_id_type=MESH) for k, ps in enumerate(slots)]

    def start():
        for cp in copies():
            cp.start()

    def finish():
        for cp in copies():
            cp.wait()

    return start, finish


def _exchange_pair_side(g):
    return ((g,), (jax.ShapeDtypeStruct((4,) + g.shape[1:], g.dtype),),
            (pltpu.SemaphoreType.DMA((4,)), pltpu.SemaphoreType.DMA((4,))), _exchange_pair_steps)


def _exchange_pair(g):
    ins, outs, scratch, make = _exchange_pair_side(g)

    def body(g_ref, r_ref, *scr):
        start, finish = make((g_ref,), (r_ref,), scr)
        start()
        finish()

    any_spec = pl.BlockSpec(memory_space=pl.ANY)
    return pl.pallas_call(body, name="exchange_pair", out_shape=outs[0], in_specs=[any_spec], out_specs=any_spec,
                          scratch_shapes=list(scratch))(g)


def _tiling(R, Cc, tile, by_cols):
    if by_cols:
        assert Cc % tile == 0
        return Cc // tile, (R, tile), lambda lead, i: (lead, 0, i)
    assert R % tile == 0
    return R // tile, (tile, Cc), lambda lead, i: (lead, i, 0)


def _pair_add(name, g, r1, slots, tile, by_cols=False):
    _, R, Cc = g.shape
    steps, blk, at = _tiling(R, Cc, tile, by_cols)

    def kern(s_ref, a_ref, b_ref, o_ref):
        o_ref[...] = (a_ref[...].astype(F32) + b_ref[...].astype(F32)).astype(o_ref.dtype)

    return pl.pallas_call(
        kern, name=name,
        out_shape=jax.ShapeDtypeStruct((3, R, Cc), BF16),
        grid_spec=pltpu.PrefetchScalarGridSpec(
            num_scalar_prefetch=1, grid=(3, steps),
            in_specs=[pl.BlockSpec((1,) + blk, lambda j, i, s: at(s[j], i)),
                      pl.BlockSpec((1,) + blk, lambda j, i, s: at(1 + j, i))],
            out_specs=pl.BlockSpec((1,) + blk, lambda j, i, s: at(j, i))),
        compiler_params=_cp(("arbitrary", "arbitrary")),
    )(slots, g, r1)


def _axis_neighbours():
    x, y, c = _place()
    flip = lambda a, bit: a + bit - 2 * a * bit
    return (flip(x, c), flip(y, 1 - c), c), (flip(x, 1 - c), flip(y, c), c), c


def _exchange_first_steps(in_refs, out_refs, scratch):
    pairs = list(zip(in_refs, out_refs))
    send_sems, recv_sems = scratch
    first, _, c = _axis_neighbours()

    def copies():
        return [pltpu.make_async_remote_copy(
            src_ref=src.at[j], dst_ref=dst.at[k], send_sem=send_sems.at[t, k], recv_sem=recv_sems.at[t, k],
            device_id=first, device_id_type=MESH)
            for t, (src, dst) in enumerate(pairs) for k, j in enumerate((1 - c, 2))]

    def start():
        for cp in copies():
            cp.start()

    def finish():
        for cp in copies():
            cp.wait()

    return start, finish


def _exchange_first_side(sa, sb):
    return ((sa, sb), (jax.ShapeDtypeStruct((2,) + sa.shape[1:], sa.dtype), jax.ShapeDtypeStruct((2,) + sb.shape[1:], sb.dtype)),
            (pltpu.SemaphoreType.DMA((2, 2)), pltpu.SemaphoreType.DMA((2, 2))), _exchange_first_steps)


def _axis_add(name, s, t1, core, tile, by_cols=False):
    _, R, Cc = s.shape
    steps, blk, at = _tiling(R, Cc, tile, by_cols)

    def kern(c_ref, a_ref, b_ref, o_ref):
        o_ref[...] = (a_ref[...].astype(F32) + b_ref[...].astype(F32)).astype(o_ref.dtype)

    return pl.pallas_call(
        kern, name=name,
        out_shape=jax.ShapeDtypeStruct((1, R, Cc), BF16),
        grid_spec=pltpu.PrefetchScalarGridSpec(
            num_scalar_prefetch=1, grid=(steps,),
            in_specs=[pl.BlockSpec((1,) + blk, lambda i, cr: at(cr[0], i)),
                      pl.BlockSpec((1,) + blk, lambda i, cr: at(1, i))],
            out_specs=pl.BlockSpec((1,) + blk, lambda i, cr: at(0, i))),
        compiler_params=_cp(("arbitrary",)),
    )(core, s, t1)


def _exchange_second_steps(in_refs, out_refs, scratch):
    send_sems, recv_sems = scratch
    _, second, _ = _axis_neighbours()

    def copies():
        return [pltpu.make_async_remote_copy(src_ref=src, dst_ref=dst, send_sem=send_sems.at[t], recv_sem=recv_sems.at[t],
                                             device_id=second, device_id_type=MESH)
                for t, (src, dst) in enumerate(zip(in_refs, out_refs))]

    def start():
        for cp in copies():
            cp.start()

    def finish():
        for cp in copies():
            cp.wait()

    return start, finish


def _exchange_second_side(pa, pb):
    return ((pa, pb), (jax.ShapeDtypeStruct(pa.shape, pa.dtype), jax.ShapeDtypeStruct(pb.shape, pb.dtype)),
            (pltpu.SemaphoreType.DMA((2,)), pltpu.SemaphoreType.DMA((2,))), _exchange_second_steps)


def _adamw(name, w, m, v, parts, tile, by_cols=False):
    R, Cc = w.shape
    steps, blk_shape, at = _tiling(R, Cc, tile, by_cols)
    n_parts = len(parts)

    def kern(*refs):
        w_ref, m_ref, v_ref = refs[:3]
        g = None
        for r_ref, (_, n) in zip(refs[3:3 + n_parts], parts):
            for s in range(n):
                term = r_ref[s].astype(F32)
                g = term if g is None else g + term
        g_out, d_out, m_out, v_out = refs[3 + n_parts:]
        mn = ADAM_B1 * m_ref[...] + (1.0 - ADAM_B1) * g
        vn = ADAM_B2 * v_ref[...] + (1.0 - ADAM_B2) * (g * g)
        m_hat = mn / (1.0 - ADAM_B1 ** ADAM_STEP)
        v_hat = vn / (1.0 - ADAM_B2 ** ADAM_STEP)
        g_out[...] = g
        d_out[...] = -ADAM_LR * (m_hat / (jnp.sqrt(v_hat) + ADAM_EPS) + ADAM_WD * w_ref[...])
        m_out[...] = mn
        v_out[...] = vn

    blk = pl.BlockSpec(blk_shape, lambda i: at(0, i)[1:])
    return pl.pallas_call(
        kern, name=name,
        out_shape=[jax.ShapeDtypeStruct((R, Cc), F32)] * 4,
        grid=(steps,),
        in_specs=[blk] * 3 + [pl.BlockSpec((n,) + blk_shape, lambda i: at(0, i)) for (_, n) in parts],
        out_specs=[blk] * 4,
        compiler_params=_cp(("arbitrary",)),
    )(w, m, v, *[a for (a, _) in parts])


def _assemble_columns(blocks, pieces, zeros, width):
    _, R, Cc = blocks.shape
    tr = min(256, R)

    def kern(b_ref, o_ref):
        for col, n in zeros:
            o_ref[:, col:col + n] = jnp.zeros((tr, n), o_ref.dtype)
        for col, d, lo, n in pieces:
            o_ref[:, col:col + n] = b_ref[d, :, lo:lo + n]

    return pl.pallas_call(
        kern, name="assemble_w_in",
        out_shape=jax.ShapeDtypeStruct((R, width), blocks.dtype),
        grid=(R // tr,),
        in_specs=[pl.BlockSpec((N_DEV, tr, Cc), lambda i: (0, i, 0))],
        out_specs=pl.BlockSpec((tr, width), lambda i: (i, 0)),
        compiler_params=_cp(("parallel",)),
    )(blocks)


def _split_rows(x, pieces, rows):
    _, Cc = x.shape
    tc = min(256, Cc)

    def kern(x_ref, o_ref):
        for d, lo, row, n in pieces:
            o_ref[d, lo:lo + n, :] = x_ref[row:row + n, :]

    return pl.pallas_call(
        kern, name="split_w_in_grad",
        out_shape=jax.ShapeDtypeStruct((N_DEV, rows, Cc), x.dtype),
        grid=(Cc // tc,),
        in_specs=[pl.BlockSpec((x.shape[0], tc), lambda i: (0, i))],
        out_specs=pl.BlockSpec((N_DEV, rows, tc), lambda i: (0, 0, i)),
        compiler_params=_cp(("parallel",)),
    )(x)


def _pad_cols(a, w):
    return jnp.pad(a, ((0, 0), (0, w - a.shape[1])))


def _pad_rows(a, r):
    return jnp.pad(a, ((0, r - a.shape[0]), (0, 0)))


def _pack_b(pf, pr, wo, w2, a2, rows):
    body = jnp.concatenate([pf, pr, wo.reshape(2048, 256), jnp.concatenate([w2, a2], axis=1)], axis=0)
    return _pad_rows(body, rows)


def kernel(x, norm_gain, w_in, fox_forget_bias, rwkv_shift_mix, rwkv_w0, rwkv_w2, rwkv_a0, rwkv_a2, rwkv_k_k, rwkv_k_a, rwkv_r_k, rwkv_ln_w, rwkv_ln_b, w_proj_fox, w_proj_rwkv, w_out, final_norm_gain, loss_target, m_norm_gain, m_w_in, m_fox_forget_bias, m_rwkv_shift_mix, m_rwkv_w0, m_rwkv_w2, m_rwkv_a0, m_rwkv_a2, m_rwkv_k_k, m_rwkv_k_a, m_rwkv_r_k, m_rwkv_ln_w, m_rwkv_ln_b, m_w_proj_fox, m_w_proj_rwkv, m_w_out, m_final_norm_gain, v_norm_gain, v_w_in, v_fox_forget_bias, v_rwkv_shift_mix, v_rwkv_w0, v_rwkv_w2, v_rwkv_a0, v_rwkv_a2, v_rwkv_k_k, v_rwkv_k_a, v_rwkv_r_k, v_rwkv_ln_w, v_rwkv_ln_b, v_w_proj_fox, v_w_proj_rwkv, v_w_out, v_final_norm_gain):
    T, D = x.shape[1], x.shape[2]
    assert D == 2048 and T % LANE == 0
    NI = w_in.shape[2]
    IN = N_DEV * NI
    RB = 4224
    x2 = x[0]
    lt2 = loss_target[0]
    me = _slot(_place())

    tb = min(256, T)
    tbh = min(128, T)
    h, wa = _rowcall("rms_fwd", T, tb, [_rows(x2, tb, D), _whole(norm_gain)],
                     lambda i, xv, g: ([_rms_math(xv, g)], []), [(D, BF16)],
                     side=_allgather_side(w_in[0].astype(BF16), relay=True))
    packed_own = _pack_b(w_proj_fox[0], w_proj_rwkv[0], w_out[0], rwkv_w2[0], rwkv_a2[0], RB).astype(BF16)
    sections = [(0, 4096, 0), (4104, 4096, 4096), (8392, 4096, 8192), (4096, 8, 12288), (8200, 96, 12544), (8296, 96, 12672)]
    NP = 12800
    pieces, zeros, at_col = [], [], 0
    for lo, width, pad_lo in sections:
        if pad_lo > at_col:
            zeros.append((at_col, pad_lo - at_col))
        col = lo
        while col < lo + width:
            d = col // NI
            stop = min(lo + width, (d + 1) * NI)
            pieces.append((pad_lo + col - lo, d, col - d * NI, stop - col))
            col = stop
        at_col = pad_lo + width
    zeros.append((at_col, NP - at_col))
    w_pad = _assemble_columns(wa, pieces, zeros, NP)
    F_CB, LORA_CB = 96, 49

    mu = rwkv_shift_mix
    mu_main = mu[:, 0:4096]
    mu_lora = jnp.concatenate([_pad_cols(mu[:, 4096:4192], LANE), _pad_cols(mu[:, 4192:4288], LANE)], axis=1)
    bias_pad = _pad_cols(fox_forget_bias, LANE)
    rk_flat = rwkv_r_k.reshape(1, 1024)
    gf = final_norm_gain.reshape(1, D)

    u, wb = _mm(h, w_pad, tm=1024, tn=1280, name="mm_in", side=_allgather_side(packed_own))
    wpf = wb[:, 0:1024, :].transpose(1, 0, 2).reshape(1024, D)
    wpr = wb[:, 1024:2048, :].transpose(1, 0, 2).reshape(1024, D)
    wo = wb[:, 2048:4096, :].reshape(N_DEV * 256, D)
    w2p = _pad_rows(wb[:, 4096:4192, 0:128].transpose(1, 0, 2).reshape(96, 1024).astype(F32), LANE)
    a2p = _pad_rows(wb[:, 4096:4192, 128:256].transpose(1, 0, 2).reshape(96, 1024).astype(F32), LANE)

    c8 = _gates_fwd(u, bias_pad, T, F_CB)
    c3 = c8.reshape(8, 1, T)
    o_raw, o_a, lse = _attn_fwd(u, c3, T)

    def shift_body(i, um, hm, ul, hl, mm_, ml):
        outs = []
        for uv, hv, mv in ((um, hm, mm_), (ul, hl, ml)):
            hv = jnp.where(i == 0, 0.0, hv)
            prev = pltpu.roll(jnp.concatenate([hv, uv], axis=0), 1, 0)[SUB:]
            outs.append(uv + (prev - uv) * mv)
        return outs, []

    def halo_prev(arr, w, cb):
        return (arr, (SUB, w), lambda i: (jnp.maximum(i * (tbh // SUB) - 1, 0), cb))

    xs, xl = _rowcall("rwkv_shift_fwd", T, tbh,
                      [_rows(u, tbh, 4096, 1), halo_prev(u, 4096, 1), _rows(u, tbh, 256, LORA_CB), halo_prev(u, 256, LORA_CB),
                       _whole(mu_main), _whole(mu_lora)],
                      shift_body, [(4096, F32), (256, F32)])

    prep_par = [_whole(rwkv_w0), _whole(rwkv_a0), _whole(rwkv_k_k), _whole(rwkv_k_a), _whole(w2p), _whole(a2p)]
    prep_rows = [_rows(xs, tbh, 1024, 1), _rows(xl, tbh, LANE, 0), _rows(xl, tbh, LANE, 1)]
    lw, kp, an, bb = _rowcall("rwkv_prep_fwd", T, tbh, prep_rows + prep_par,
                              lambda i, *a: (list(_prep_math(*a)), []), [(1024, F32)] * 4)
    y, states, invs = _scan_fwd(xs, lw, kp, an, bb, T)
    post_rows = [_rows(y, tbh, 1024), _rows(xs, tbh, 1024, 0), _rows(kp, tbh, 1024), _rows(xs, tbh, 1024, 2), _rows(xs, tbh, 1024, 3)]
    post_par = [_whole(rwkv_ln_w), _whole(rwkv_ln_b), _whole(rk_flat)]
    (o_b,) = _rowcall("rwkv_post_fwd", T, tbh, post_rows + post_par,
                      lambda i, *a: ([_post_math(*a)], []), [(1024, BF16)])

    pa = _mm(o_a, wpf, name="mm_proj_fox")
    pb = _mm(o_b, wpr, name="mm_proj_rwkv")
    merge_rows = [_rows(u, tb, D, 4), _rows(u, tb, D, 5), _rows(pa, tb, D), _rows(pb, tb, D)]
    (mg,) = _rowcall("merge_fwd", T, tb, merge_rows, lambda i, *a: ([_merge_math(*a)], []), [(D, BF16)])
    mo = _mm(mg, wo, name="mm_out")

    def head_body(i, xv, mov, ltv, g):
        out = xv + mov
        r = lax.rsqrt(jnp.mean(out * out, axis=-1, keepdims=True) + RMS_EPS)
        yn = out * r
        err = yn * g - ltv
        loss = 0.5 * jnp.sum(jnp.sum(err * err, axis=-1, keepdims=True), axis=0, keepdims=True) / D
        dyv = err / D
        dyn = dyv * g
        dout = r * (dyn - yn * jnp.mean(dyn * yn, axis=-1, keepdims=True))
        return [dout], [loss, jnp.sum(dyv * yn, axis=0, keepdims=True)]

    dout, loss_p, dgf_p = _rowcall("loss_head", T, tb, [_rows(x2, tb, D), _rows(mo, tb, D), _rows(lt2, tb, D), _whole(gf)],
                                   head_body, [(D, F32)], [(1, 1), (1, D)])

    dm = _mm(dout, wo, tb=True, name="mm_out_dx")
    dwo = _mm(mg, dout, ta=True, out_dtype=BF16, name="mm_out_dw")

    def merge_bwd_body(i, ga, gb, pav, pbv, dmv):
        _, vjp = jax.vjp(_merge_math, ga, gb, pav, pbv)
        dga, dgb, dpa, dpb = vjp(dmv)
        return [dga, dgb, dpa, dpb], []

    dga, dgb, dpa, dpb = _rowcall("merge_bwd", T, tb, merge_rows + [_rows(dm, tb, D)], merge_bwd_body,
                                  [(D, BF16), (D, BF16), (D, BF16), (D, BF16)])
    doa = _mm(dpa, wpf, tb=True, name="mm_proj_fox_dx")
    dwpf = _mm(o_a, dpa, ta=True, out_dtype=BF16, name="mm_proj_fox_dw")
    dob = _mm(dpb, wpr, tb=True, name="mm_proj_rwkv_dx")
    dwpr = _mm(o_b, dpb, ta=True, out_dtype=BF16, name="mm_proj_rwkv_dw")

    do_b, dza, delta128 = _rowcall("fox_attn_pre", T, tb, [_rows(doa, tb, 1024), _rows(u, tb, 1024, 3), _rows(o_raw, tb, 1024)],
                                   lambda i, *a: (list(_attn_pre_math(*a)), []), [(1024, BF16), (1024, F32), (LANE, F32)])
    delta = delta128[:, 0:8].T.reshape(8, T, 1)
    dq, dk, dv, dc3 = _attn_bwd(u, c3, lse, do_b, delta, T)
    dfl, dbias_p = _gates_bwd(dc3.reshape(8, T), u, bias_pad, T, F_CB)

    def post_bwd_body(i, yv, rv, kpv, vv, zv, lnw, lnb, rkv, dobv):
        _, vjp = jax.vjp(_post_math, yv, rv, kpv, vv, zv, lnw, lnb, rkv)
        dy_, dr_, dkp_, dv_, dz_, dlnw, dlnb, drk = vjp(dobv)
        return [dy_, dr_, dkp_, dv_, dz_], [dlnw, dlnb, drk]

    dy_s, dr_p, dkp_p, dv_p, dzb, dlnw_p, dlnb_p, drk_p = _rowcall(
        "rwkv_post_bwd", T, tbh, post_rows + post_par + [_rows(dob, tbh, 1024)], post_bwd_body,
        [(1024, F32)] * 5, [(1, 1024)] * 3)
    dr_s, dlw, dkp_s, dv_s, dan, dbb = _scan_bwd(xs, lw, kp, an, bb, states, invs, dy_s, T)

    def prep_bwd_body(i, xk, xwd, xad, w0, a0, kkw, kaw, w2v, a2v, dlw_, dkp1, dkp2, dan_, dbb_, dr1, dr2, dv1, dv2, dz_):
        _, vjp = jax.vjp(_prep_math, xk, xwd, xad, w0, a0, kkw, kaw, w2v, a2v)
        dxk, dxwd, dxad, dw0, da0, dkk, dka, dw2, da2 = vjp((dlw_, dkp1 + dkp2, dan_, dbb_))
        return [[dr1 + dr2, dxk, dv1 + dv2, dz_], [dxwd, dxad]], [dw0, da0, dkk, dka, dw2, da2]

    cots = [dlw, dkp_s, dkp_p, dan, dbb, dr_s, dr_p, dv_s, dv_p, dzb]
    dxs, dxl, dw0_p, da0_p, dkk_p, dka_p, dw2_p, da2_p = _rowcall(
        "rwkv_prep_bwd", T, tbh, prep_rows + prep_par + [_rows(c_, tbh, 1024) for c_ in cots], prep_bwd_body,
        [(4096, F32), (256, F32)], [(1, 1024)] * 4 + [(LANE, 1024)] * 2)

    def shift_bwd_body(i, dm_, hm, dl_, hl, um, pm, ul, pl_, mm_, ml):
        last = i == T // tbh - 1
        outs, accs = [], []
        for dv_, hv, uv, pv, mv in ((dm_, hm, um, pm, mm_), (dl_, hl, ul, pl_, ml)):
            hv = jnp.where(last, 0.0, hv)
            nxt = pltpu.roll(jnp.concatenate([dv_, hv], axis=0), tbh + SUB - 1, 0)[:tbh]
            pv = jnp.where(i == 0, 0.0, pv)
            prev = pltpu.roll(jnp.concatenate([pv, uv], axis=0), 1, 0)[SUB:]
            outs.append(dv_ * (1.0 - mv) + nxt * mv)
            accs.append(jnp.sum(dv_ * (prev - uv), axis=0, keepdims=True))
        return outs, accs

    def halo_next(arr, w, cb):
        last_blk = T // SUB - 1
        return (arr, (SUB, w), lambda i: (jnp.minimum((i + 1) * (tbh // SUB), last_blk), cb))

    du_b, du_l, dmu_main_p, dmu_lora_p = _rowcall(
        "rwkv_shift_bwd", T, tbh,
        [_rows(dxs, tbh, 4096), halo_next(dxs, 4096, 0), _rows(dxl, tbh, 256), halo_next(dxl, 256, 0),
         _rows(u, tbh, 4096, 1), halo_prev(u, 4096, 1), _rows(u, tbh, 256, LORA_CB), halo_prev(u, 256, LORA_CB),
         _whole(mu_main), _whole(mu_lora)],
        shift_bwd_body, [(4096, BF16), (256, BF16)], [(1, 4096), (1, 256)])

    du = jnp.concatenate([dq.astype(BF16), dk.astype(BF16), dv.astype(BF16), dza.astype(BF16), du_b, dga, dgb,
                          dfl.astype(BF16), jnp.zeros((T, LANE), BF16), du_l], axis=1)
    lora_g = jnp.concatenate([dw2_p[:96].reshape(96, N_DEV, 128).transpose(1, 0, 2),
                              da2_p[:96].reshape(96, N_DEV, 128).transpose(1, 0, 2)], axis=2).astype(BF16)
    gb = jnp.concatenate([dwpf.reshape(1024, N_DEV, 256).transpose(1, 0, 2),
                          dwpr.reshape(1024, N_DEV, 256).transpose(1, 0, 2),
                          dwo.reshape(N_DEV, 2048, 256), lora_g, jnp.zeros((N_DEV, RB - 4192, 256), BF16)], axis=1)
    dw_pad_t, r1b = _mm(du, h, ta=True, out_dtype=BF16, tm=1280, tn=1024, name="mm_in_dw",
                        side=_exchange_pair_side(gb))
    ga = _split_rows(dw_pad_t, [(d, lo, pad_col, n) for pad_col, d, lo, n in pieces], NI)
    xx, yy, cc = _place()
    chip_slots = jnp.stack([_slot((*chip, cc)) for chip in _other_chips(xx, yy)]).astype(jnp.int32)
    r1a = _exchange_pair(ga)
    sa = _pair_add("pair_add_w_in", ga, r1a, chip_slots, 512, by_cols=True)
    sb = _pair_add("pair_add_packed", gb, r1b, chip_slots, 384)
    ga_own = lax.dynamic_index_in_dim(ga, me, 0, keepdims=True)
    gb_own = lax.dynamic_index_in_dim(gb, me, 0, keepdims=True)
    dh, t1a, t1b = _mm(du, w_pad, tb=True, tm=1024, tn=1024, tk=NP // 10, name="mm_in_dx", side=_exchange_first_side(sa, sb))
    core = jnp.stack([cc]).astype(jnp.int32)
    second = _exchange_second_side(_axis_add("axis_add_w_in", sa, t1a, core, 512, by_cols=True),
                                   _axis_add("axis_add_packed", sb, t1b, core, 384))

    def rms_bwd_body(i, xv, g, dhv, doutv):
        _, vjp = jax.vjp(_rms_math, xv, g)
        dx_, dg_ = vjp(dhv)
        return [dx_ + doutv], [dg_]

    grad_x2, dng_p, t2a, t2b = _rowcall(
        "rms_bwd", T, tb, [_rows(x2, tb, D), _whole(norm_gain), _rows(dh, tb, D), _rows(dout, tb, D)],
        rms_bwd_body, [(D, F32)], [(1, D)], side=second)

    dmu = jnp.concatenate([dmu_main_p, dmu_lora_p[:, 0:96], dmu_lora_p[:, 128:224]], axis=1)
    small_parts = [dng_p, dbias_p[:, 0:8], dmu, dw0_p, da0_p, dkk_p, dka_p, drk_p, dlnw_p, dlnb_p, dgf_p, loss_p]
    SR = 128
    small = _pad_cols(jnp.concatenate(small_parts, axis=1), SR * LANE).reshape(SR, LANE)
    rs = _allgather(small, "allgather_small")

    w_in_outs = _adamw("adamw_w_in", w_in[0].T, m_w_in[0].T, v_w_in[0].T, [(ga_own, 1), (r1a, 1), (t1a, 1), (t2a, 1)], 256, by_cols=True)
    g_in, d_in, m_in, v_in = [o.T for o in w_in_outs]
    pk = lambda pf, pr, wo_, w2_, a2_: _pack_b(pf[0], pr[0], wo_[0], w2_[0], a2_[0], RB)
    outs_b = _adamw("adamw_packed", pk(w_proj_fox, w_proj_rwkv, w_out, rwkv_w2, rwkv_a2),
                    pk(m_w_proj_fox, m_w_proj_rwkv, m_w_out, m_rwkv_w2, m_rwkv_a2),
                    pk(v_w_proj_fox, v_w_proj_rwkv, v_w_out, v_rwkv_w2, v_rwkv_a2), [(gb_own, 1), (r1b, 1), (t1b, 1), (t2b, 1)], 384)

    def pack_small(ng, fb, sm, w0, a0, kk_, ka_, rk_, lnw, lnb, fg):
        parts = [ng, fb, sm, w0, a0, kk_, ka_, rk_.reshape(1, 1024), lnw, lnb, fg.reshape(1, D), jnp.zeros((1, 1), F32)]
        return _pad_cols(jnp.concatenate(parts, axis=1), SR * LANE).reshape(SR, LANE)

    outs_s = _adamw("adamw_small",
                    pack_small(norm_gain, fox_forget_bias, rwkv_shift_mix, rwkv_w0, rwkv_a0, rwkv_k_k, rwkv_k_a, rwkv_r_k,
                               rwkv_ln_w, rwkv_ln_b, final_norm_gain),
                    pack_small(m_norm_gain, m_fox_forget_bias, m_rwkv_shift_mix, m_rwkv_w0, m_rwkv_a0, m_rwkv_k_k, m_rwkv_k_a,
                               m_rwkv_r_k, m_rwkv_ln_w, m_rwkv_ln_b, m_final_norm_gain),
                    pack_small(v_norm_gain, v_fox_forget_bias, v_rwkv_shift_mix, v_rwkv_w0, v_rwkv_a0, v_rwkv_k_k, v_rwkv_k_a,
                               v_rwkv_r_k, v_rwkv_ln_w, v_rwkv_ln_b, v_final_norm_gain),
                    [(rs, N_DEV)], SR)

    def unpack_b(pkd):
        return dict(w_proj_fox=pkd[0:1024][None], w_proj_rwkv=pkd[1024:2048][None], w_out=pkd[2048:4096].reshape(1, 256, D),
                    rwkv_w2=pkd[4096:4192, 0:128][None], rwkv_a2=pkd[4096:4192, 128:256][None])

    def unpack_s(pkd):
        flat = pkd.reshape(1, SR * LANE)
        names = [("norm_gain", D), ("fox_forget_bias", 8), ("rwkv_shift_mix", 4288), ("rwkv_w0", 1024), ("rwkv_a0", 1024),
                 ("rwkv_k_k", 1024), ("rwkv_k_a", 1024), ("rwkv_r_k", 1024), ("rwkv_ln_w", 1024), ("rwkv_ln_b", 1024),
                 ("final_norm_gain", D), ("loss", 1)]
        out, off = {}, 0
        for nm, n in names:
            out[nm] = flat[:, off:off + n]
            off += n
        out["rwkv_r_k"] = out["rwkv_r_k"].reshape(1, 16, 64)
        out["final_norm_gain"] = out["final_norm_gain"].reshape(D)
        return out

    order = ["norm_gain", "w_in", "fox_forget_bias", "rwkv_shift_mix", "rwkv_w0", "rwkv_w2", "rwkv_a0", "rwkv_a2", "rwkv_k_k",
             "rwkv_k_a", "rwkv_r_k", "rwkv_ln_w", "rwkv_ln_b", "w_proj_fox", "w_proj_rwkv", "w_out", "final_norm_gain"]
    result = []
    loss = None
    for kind, big in enumerate((g_in, d_in, m_in, v_in)):
        d = {**unpack_b(outs_b[kind]), **unpack_s(outs_s[kind]), "w_in": big[None]}
        if kind == 0:
            loss = d["loss"].reshape(())
        result += [d[n] for n in order]
    return (loss, grad_x2[None], *result)
```

```python
import functools

import jax
import jax.numpy as jnp
from jax import lax
from jax.experimental import pallas as pl
from jax.experimental.pallas import tpu as pltpu

F32 = jnp.float32
BF16 = jnp.bfloat16
HI = lax.Precision.HIGHEST
H3 = lax.Precision.HIGH
MESH = pl.DeviceIdType.MESH

FOX_HD = 128
RW_HD = 64
RMS_EPS = 1e-6
GN_EPS = 64e-5
L2_EPS = 1e-12
ADAM_LR = 0.001
ADAM_B1 = 0.9
ADAM_B2 = 0.999
ADAM_EPS = 1e-08
ADAM_WD = 0.01
ADAM_STEP = 10

LANE = 128
SUB = 8
VMEM_LIMIT = 56 * 1024 * 1024
N_DEV = 8
CHUNK = 128
SCAN_GROUP = 2
SCAN_PAIRS = 2
PS = None
NEG = -1e30


def _scan_shape(T):
    c = min(CHUNK, T)
    return c, min(SCAN_GROUP, T // c)


def _cp(sem=None):
    return pltpu.CompilerParams(dimension_semantics=sem, vmem_limit_bytes=VMEM_LIMIT)


def _sigmoid(x):
    return jax.nn.sigmoid(x)


def _softplus(x):
    return jnp.maximum(x, 0.0) + jnp.log(1.0 + jnp.exp(-jnp.abs(x)))


def _nn(a, b, prec=None):
    return lax.dot_general(a, b, (((1,), (0,)), ((), ())), precision=prec, preferred_element_type=F32)


def _nt(a, b, prec=None):
    return lax.dot_general(a, b, (((1,), (1,)), ((), ())), precision=prec, preferred_element_type=F32)


def _tn(a, b, prec=None):
    return lax.dot_general(a, b, (((0,), (0,)), ((), ())), precision=prec, preferred_element_type=F32)


def _iota2(shape, dim):
    return lax.broadcasted_iota(jnp.int32, shape, dim)


def _seg_sum(x):
    r = _iota2((LANE, LANE), 0) // RW_HD
    c = _iota2((LANE, LANE), 1) // RW_HD
    bd = (r == c).astype(F32)
    parts = [_nn(x[:, j * LANE:(j + 1) * LANE], bd, H3) for j in range(x.shape[1] // LANE)]
    return parts[0] if len(parts) == 1 else jnp.concatenate(parts, axis=1)


def _mm(a, b, *, ta=False, tb=False, out_dtype=F32, tm=1024, tn=1024, tk=None, name, side=None):
    assert not (ta and tb)
    K, M = a.shape if ta else a.shape[::-1]
    N = b.shape[0] if tb else b.shape[1]
    tm, tn = min(tm, M), min(tn, N)
    tk = K if tk is None else tk
    nk = K // tk
    assert M % tm == 0 and N % tn == 0 and K % tk == 0
    gi, gj = M // tm, N // tn
    a_spec = pl.BlockSpec((tk, tm), lambda i, j, k: (k, i)) if ta else pl.BlockSpec((tm, tk), lambda i, j, k: (i, k))
    b_spec = pl.BlockSpec((tn, tk), lambda i, j, k: (j, k)) if tb else pl.BlockSpec((tk, tn), lambda i, j, k: (k, j))
    side_ins, side_outs, side_scr, side_make = side if side is not None else ((), (), (), None)
    n_si, n_so = len(side_ins), len(side_outs)
    n_acc = 0 if nk == 1 else 1

    def body(*refs):
        a_ref, b_ref = refs[:2]
        o_ref = refs[2 + n_si]
        scr = refs[3 + n_si + n_so:]
        k = pl.program_id(2)
        if side_make is not None:
            start, finish = side_make(refs[2:2 + n_si], refs[3 + n_si:3 + n_si + n_so], scr[n_acc:])
            first = jnp.logical_and(jnp.logical_and(pl.program_id(0) == 0, pl.program_id(1) == 0), k == 0)
            last = jnp.logical_and(jnp.logical_and(pl.program_id(0) == gi - 1, pl.program_id(1) == gj - 1), k == nk - 1)
            pl.when(first)(start)
        av = a_ref[...].astype(BF16)
        bv = b_ref[...].astype(BF16)
        p = _tn(av, bv) if ta else _nt(av, bv) if tb else _nn(av, bv)
        if nk == 1:
            o_ref[...] = p.astype(out_dtype)
        else:
            acc_ref = scr[0]

            @pl.when(k == 0)
            def _():
                acc_ref[...] = p

            @pl.when(k > 0)
            def _():
                acc_ref[...] += p

            @pl.when(k == nk - 1)
            def _():
                o_ref[...] = acc_ref[...].astype(out_dtype)
        if side_make is not None:
            pl.when(last)(finish)

    any_spec = pl.BlockSpec(memory_space=pl.ANY)
    res = pl.pallas_call(
        body, name=name,
        out_shape=[jax.ShapeDtypeStruct((M, N), out_dtype)] + list(side_outs),
        grid=(gi, gj, nk),
        in_specs=[a_spec, b_spec] + [any_spec] * n_si,
        out_specs=[pl.BlockSpec((tm, tn), lambda i, j, k: (i, j))] + [any_spec] * n_so,
        scratch_shapes=([] if nk == 1 else [pltpu.VMEM((tm, tn), F32)]) + list(side_scr),
        compiler_params=_cp(("arbitrary",) * 3 if side is not None else ("parallel", "parallel", "arbitrary")),
    )(a, b, *side_ins)
    return res if side is not None else res[0]


def _rows(arr, tb, w, cb=0):
    return (arr, (tb, w), lambda i: (i, cb))


def _whole(arr):
    nd = arr.ndim
    return (arr, arr.shape, lambda i: (0,) * nd)


def _rowcall(name, T, tb, ins, body, outs, accs=(), side=None):
    n_in, n_out, n_acc = len(ins), len(outs), len(accs)
    side_ins, side_outs, side_scr, side_make = side if side is not None else ((), (), (), None)
    n_si = len(side_ins)

    def kern(*refs):
        i = pl.program_id(0)
        if side_make is not None:
            side_refs = refs[n_in + n_si + n_out + n_acc:]
            start, finish = side_make(refs[n_in:n_in + n_si], side_refs[:len(side_outs)], side_refs[len(side_outs):])
            pl.when(i == 0)(start)
            refs = refs[:n_in] + refs[n_in + n_si:]
        vals = [r[...] for r in refs[:n_in]]
        ro, ao = body(i, *vals)
        for r, v in zip(refs[n_in:n_in + n_out], ro):
            if isinstance(v, (list, tuple)):
                off = 0
                for piece in v:
                    w = piece.shape[1]
                    r[:, off:off + w] = piece.astype(r.dtype)
                    off += w
            else:
                r[...] = v.astype(r.dtype)
        if accs:
            acc_refs = refs[n_in + n_out:n_in + n_out + n_acc]

            @pl.when(i == 0)
            def _():
                for r in acc_refs:
                    r[...] = jnp.zeros(r.shape, F32)

            for r, v in zip(acc_refs, ao):
                r[...] += v
        if side_make is not None:
            pl.when(i == T // tb - 1)(finish)

    any_spec = pl.BlockSpec(memory_space=pl.ANY)
    out_shape = [jax.ShapeDtypeStruct((T, w), dt) for (w, dt) in outs] + [jax.ShapeDtypeStruct(s, F32) for s in accs]
    out_specs = [pl.BlockSpec((tb, w), lambda i: (i, 0)) for (w, dt) in outs] + [pl.BlockSpec(s, lambda i: (0, 0)) for s in accs]
    res = pl.pallas_call(
        kern, name=name,
        out_shape=out_shape + list(side_outs),
        grid=(T // tb,),
        in_specs=[pl.BlockSpec(bs, im) for (_, bs, im) in ins] + [any_spec] * n_si,
        out_specs=out_specs + [any_spec] * len(side_outs),
        scratch_shapes=list(side_scr),
        compiler_params=_cp(("arbitrary",)),
    )(*[a for (a, _, _) in ins], *side_ins)
    return res


def _rms_math(x, g):
    r = lax.rsqrt(jnp.mean(x * x, axis=-1, keepdims=True) + RMS_EPS)
    return x * r * g


def _merge_math(ga, gb, pa, pb):
    return _sigmoid(ga) * pa + _sigmoid(gb) * pb


def _prep_math(xk, xwd, xad, w0, a0, kk_w, ka_w, w2p, a2p):
    z = w0 + _nn(jnp.tanh(xwd), w2p, H3)
    w = -_softplus(-z) - 0.5
    lw = -jnp.exp(w)
    ag = _sigmoid(a0 + _nn(xad, a2p, H3))
    p = xk * kk_w
    n = jnp.maximum(jnp.sqrt(_seg_sum(p * p)), L2_EPS)
    kk = p / n
    kp = xk * (1.0 + (ag - 1.0) * ka_w)
    return lw, kp, -kk, kk * ag


def _post_math(y, r, kp, v, z, lnw, lnb, rk):
    inv = 1.0 / RW_HD
    mu = _seg_sum(y) * inv
    d = y - mu
    var = _seg_sum(d * d) * inv
    yn = d * lax.rsqrt(var + GN_EPS) * lnw + lnb
    bonus = _seg_sum(r * kp * rk) * v
    return (yn + bonus) * (z * _sigmoid(z))


def _neumann(ms, depth):
    eye = (_iota2(ms[0].shape, 0) == _iota2(ms[0].shape, 1)).astype(F32)
    width = ms[0].shape[1]
    mp = [_nn(m, m, PS) for m in ms]
    inv = [eye + m for m in ms]
    n = 2
    while n < depth:
        last = 2 * n >= depth
        for i in range(len(ms)):
            if last:
                inv[i] = inv[i] + _nn(mp[i], inv[i], PS)
            else:
                z = _nn(mp[i], jnp.concatenate([mp[i], inv[i]], axis=1), PS)
                mp[i], inv[i] = z[:, :width], inv[i] + z[:, width:]
        n *= 2
    return inv


@functools.partial(jax.custom_vjp, nondiff_argnums=(1,))
def _unit_inverses(ms, depth):
    if depth <= RW_HD:
        return tuple(_neumann(list(ms), depth))
    h, n = depth // 2, len(ms)
    cat = jnp.concatenate
    z = jnp.zeros((h, h), F32)

    def heads(m, r, c):
        b0 = m[r * h:(r + 1) * h, c * h:(c + 1) * h]
        b1 = m[depth + r * h:depth + (r + 1) * h, depth + c * h:depth + (c + 1) * h]
        return cat([cat([b0, z], axis=1), cat([z, b1], axis=1)], axis=0)

    diag = _neumann([heads(m, 0, 0) for m in ms] + [heads(m, 1, 1) for m in ms], h)
    ta, td = diag[:n], diag[n:]
    low = [_nn(heads(m, 1, 0), a, PS) for m, a in zip(ms, ta)]
    low = [_nn(d, x, PS) for d, x in zip(td, low)]
    out = []
    for a, x, d in zip(ta, low, td):
        rows = []
        for hd in (0, 1):
            sl = slice(hd * h, (hd + 1) * h)
            top, bot = [a[sl, sl], z], [x[sl, sl], d[sl, sl]]
            pad = [z, z]
            rows.append(cat(top + pad if hd == 0 else pad + top, axis=1))
            rows.append(cat(bot + pad if hd == 0 else pad + bot, axis=1))
        out.append(cat(rows, axis=0))
    return tuple(out)


def _unit_inverses_fwd(ms, depth):
    inv = _unit_inverses(ms, depth)
    return inv, inv


def _unit_inverses_bwd(depth, inv, cts):
    left = [_tn(t, g, PS) for t, g in zip(inv, cts)]
    return (tuple(_nt(l, t, PS) for l, t in zip(left, inv)),)


_unit_inverses.defvjp(_unit_inverses_fwd, _unit_inverses_bwd)


@jax.custom_vjp
def _known_inverses(ms, inv):
    return inv


def _known_inverses_fwd(ms, inv):
    return inv, inv


def _known_inverses_bwd(inv, cts):
    dms = _unit_inverses_bwd(None, inv, cts)[0]
    return dms, tuple(jnp.zeros_like(t) for t in inv)


_known_inverses.defvjp(_known_inverses_fwd, _known_inverses_bwd)


def _scan_group(s0s, *flat, known_inv=None, with_inv=False):
    P = len(s0s)
    G = len(flat) // (6 * P)
    ch = [flat[6 * i:6 * i + 6] for i in range(P * G)]
    C = ch[0][0].shape[0]
    C2 = 2 * C
    cat = jnp.concatenate
    m0 = _iota2((1, LANE), 1) < RW_HD
    mask0 = m0.astype(F32)
    mask1 = 1.0 - mask0
    r2 = _iota2((C2, C2), 0)
    c2 = _iota2((C2, C2), 1)
    dist = r2 - c2
    in_head = dist <= r2 % C
    lower = (_iota2((C, C), 0) >= _iota2((C, C), 1)).astype(F32)
    bd = (_iota2((LANE, LANE), 0) // RW_HD) == (_iota2((LANE, LANE), 1) // RW_HD)

    def tri(m, strict):
        return jnp.where(dist > 0 if strict else dist >= 0, jnp.where(in_head, m, 0.0), 0.0)

    def sel(z):
        return jnp.where(m0, z[:C], z[C:])

    gs = [_nn(lower, c[1], HI) for c in ch]
    pre = []
    for (r, lw, k, v, a, b), g in zip(ch, gs):
        g_end = jnp.sum(lw, axis=0, keepdims=True)
        gm = g - jnp.sum(lw[:C // 2], axis=0, keepdims=True)
        en = jnp.exp(-gm)
        ec = jnp.exp(g_end - g)
        pre.append(dict(at=a * jnp.exp(g - lw), rt=r * jnp.exp(g), am=a * jnp.exp(gm - lw), rm=r * jnp.exp(gm),
                        bt=b * en, kt=k * en, bh=b * ec, kh=k * ec, dec=jnp.exp(g_end), v=v))
    grams = [_nt(cat([p["am"] * mask0, p["am"] * mask1, p["rm"] * mask0, p["rm"] * mask1], axis=0),
                 cat([p["bt"], p["bt"], p["kt"], p["kt"]], axis=0), PS) for p in pre]
    mab = tuple(tri(gm[:C2, :C2], True) for gm in grams)
    tinv = _unit_inverses(mab, C) if known_inv is None else _known_inverses(mab, known_inv)
    xv =[sel(_nn(tri(gm[:C2, C2:], True), cat([p["v"], p["v"]], axis=0), PS)) for gm, p in zip(grams, pre)]
    ys, s = [None] * (P * G), list(s0s)
    for i in range(G):
        for q in range(P):
            n = q * G + i
            p, gm = pre[n], grams[n]
            sx = _nt(cat([p["at"], p["rt"]], axis=0), s[q], PS)
            x = sx[:C] + xv[n]
            u = sel(_nn(tinv[n], cat([x, x], axis=0), PS))
            v = p["v"]
            ys[n] = sx[C:] + sel(_nn(cat([tri(gm[C2:, :C2], False), tri(gm[C2:, C2:], False)], axis=1),
                                     cat([u, u, v, v], axis=0), PS))
            s[q] = s[q] * p["dec"] + jnp.where(bd, _tn(cat([u, v], axis=0), cat([p["bh"], p["kh"]], axis=0), PS), 0.0)
    return (tuple(ys), tuple(s), tinv) if with_inv else (tuple(ys), tuple(s))


def _scan_fwd(xs, lw, kp, an, bb, T):
    C, G = _scan_shape(T)
    P = SCAN_PAIRS
    nc = T // (C * G)
    npair = 1024 // LANE

    def kern(r_ref, lw_ref, k_ref, v_ref, a_ref, b_ref, y_ref, st_ref, inv_ref, s_scr):
        n = pl.program_id(1)

        @pl.when(n == 0)
        def _():
            s_scr[...] = jnp.zeros(s_scr.shape, F32)

        st_ref[0] = s_scr[...]
        ins = (r_ref, lw_ref, k_ref, v_ref, a_ref, b_ref)
        ys, s1, inv = _scan_group(
            tuple(s_scr[q] for q in range(P)),
            *[ref[i * C:(i + 1) * C, q * LANE:(q + 1) * LANE] for q in range(P) for i in range(G) for ref in ins], with_inv=True)
        for q in range(P):
            for i in range(G):
                y_ref[i * C:(i + 1) * C, q * LANE:(q + 1) * LANE] = ys[q * G + i]
                inv_ref[0, 0, q * G + i] = inv[q * G + i]
            s_scr[q] = s1[q]

    def col(off):
        return pl.BlockSpec((C * G, P * LANE), lambda p, n: (n, off // P + p))

    return pl.pallas_call(
        kern, name="rwkv_scan_fwd",
        out_shape=[jax.ShapeDtypeStruct((T, 1024), F32), jax.ShapeDtypeStruct((nc, npair, LANE, LANE), F32),
                   jax.ShapeDtypeStruct((nc, npair // P, P * G, 2 * C, 2 * C), F32)],
        grid=(npair // P, nc),
        in_specs=[col(0), col(0), col(0), col(16), col(0), col(0)],
        out_specs=[col(0), pl.BlockSpec((1, P, LANE, LANE), lambda p, n: (n, p, 0, 0)),
                   pl.BlockSpec((1, 1, P * G, 2 * C, 2 * C), lambda p, n: (n, p, 0, 0, 0))],
        scratch_shapes=[pltpu.VMEM((P, LANE, LANE), F32)],
        compiler_params=_cp(("parallel", "arbitrary")),
    )(xs, lw, kp, xs, an, bb)


def _scan_bwd(xs, lw, kp, an, bb, states, invs, dy, T):
    C, G = _scan_shape(T)
    P = SCAN_PAIRS
    nc = T // (C * G)
    npair = 1024 // LANE

    def kern(r_ref, lw_ref, k_ref, v_ref, a_ref, b_ref, st_ref, inv_ref, dy_ref, dr_ref, dlw_ref, dk_ref, dv_ref, da_ref, db_ref,
             ds_scr):
        n = pl.program_id(1)

        @pl.when(n == 0)
        def _():
            ds_scr[...] = jnp.zeros(ds_scr.shape, F32)

        ins = (r_ref, lw_ref, k_ref, v_ref, a_ref, b_ref)
        units = [(q, i) for q in range(P) for i in range(G)]
        known = tuple(inv_ref[0, 0, q * G + i] for q, i in units)
        _, vjp = jax.vjp(functools.partial(_scan_group, known_inv=known), tuple(st_ref[0, q] for q in range(P)),
                         *[ref[i * C:(i + 1) * C, q * LANE:(q + 1) * LANE] for q, i in units for ref in ins])
        grads = vjp((tuple(dy_ref[i * C:(i + 1) * C, q * LANE:(q + 1) * LANE] for q, i in units),
                     tuple(ds_scr[q] for q in range(P))))
        for q in range(P):
            ds_scr[q] = grads[0][q]
        outs = (dr_ref, dlw_ref, dk_ref, dv_ref, da_ref, db_ref)
        for n_, (q, i) in enumerate(units):
            for t, ref in enumerate(outs):
                ref[i * C:(i + 1) * C, q * LANE:(q + 1) * LANE] = grads[1 + 6 * n_ + t]

    def col(off):
        return pl.BlockSpec((C * G, P * LANE), lambda p, n: (nc - 1 - n, off // P + p))

    return pl.pallas_call(
        kern, name="rwkv_scan_bwd",
        out_shape=[jax.ShapeDtypeStruct((T, 1024), F32)] * 6,
        grid=(npair // P, nc),
        in_specs=[col(0), col(0), col(0), col(16), col(0), col(0),
                  pl.BlockSpec((1, P, LANE, LANE), lambda p, n: (nc - 1 - n, p, 0, 0)),
                  pl.BlockSpec((1, 1, P * G, 2 * C, 2 * C), lambda p, n: (nc - 1 - n, p, 0, 0, 0)), col(0)],
        out_specs=[col(0)] * 6,
        scratch_shapes=[pltpu.VMEM((P, LANE, LANE), F32)],
        compiler_params=_cp(("parallel", "arbitrary")),
    )(xs, lw, kp, xs, an, bb, states, invs, dy)


def _gates_fwd(u, bias_pad, T, f_cb):
    nb = T // LANE

    def kern(f_ref, b_ref, c_ref):
        x = f_ref[...] + b_ref[...]
        lf = jnp.minimum(x, 0.0) - jnp.log(1.0 + jnp.exp(-jnp.abs(x)))
        lft = lf.T
        ut = (_iota2((LANE, LANE), 0) <= _iota2((LANE, LANE), 1)).astype(F32)
        carry = jnp.zeros((LANE, 1), F32)
        for blk in range(nb):
            seg = lft[:, blk * LANE:(blk + 1) * LANE]
            cs = _nn(seg, ut, HI) + carry
            c_ref[:, blk * LANE:(blk + 1) * LANE] = cs[:SUB, :]
            carry = carry + jnp.sum(seg, axis=1, keepdims=True)

    return pl.pallas_call(
        kern, name="fox_gates_fwd",
        out_shape=jax.ShapeDtypeStruct((SUB, T), F32),
        grid=(1,),
        in_specs=[pl.BlockSpec((T, LANE), lambda i: (0, f_cb)), pl.BlockSpec((1, LANE), lambda i: (0, 0))],
        out_specs=pl.BlockSpec((SUB, T), lambda i: (0, 0)),
        compiler_params=_cp(("arbitrary",)),
    )(u, bias_pad)


def _gates_bwd(dc, u, bias_pad, T, f_cb):
    nb = T // LANE

    def kern(dc_ref, f_ref, b_ref, dfl_ref, db_ref):
        dcv = jnp.concatenate([dc_ref[...], jnp.zeros((LANE - SUB, T), F32)], axis=0)
        lt = (_iota2((LANE, LANE), 0) >= _iota2((LANE, LANE), 1)).astype(F32)
        carry = jnp.zeros((LANE, 1), F32)
        pieces = [None] * nb
        for blk in range(nb - 1, -1, -1):
            seg = dcv[:, blk * LANE:(blk + 1) * LANE]
            pieces[blk] = _nn(seg, lt, HI) + carry
            carry = carry + jnp.sum(seg, axis=1, keepdims=True)
        dlf = (pieces[0] if nb == 1 else jnp.concatenate(pieces, axis=1)).T
        x = f_ref[...] + b_ref[...]
        dfl = dlf * _sigmoid(-x)
        dfl_ref[...] = dfl
        db_ref[...] = jnp.sum(dfl, axis=0, keepdims=True)

    return pl.pallas_call(
        kern, name="fox_gates_bwd",
        out_shape=[jax.ShapeDtypeStruct((T, LANE), F32), jax.ShapeDtypeStruct((1, LANE), F32)],
        grid=(1,),
        in_specs=[pl.BlockSpec((SUB, T), lambda i: (0, 0)), pl.BlockSpec((T, LANE), lambda i: (0, f_cb)),
                  pl.BlockSpec((1, LANE), lambda i: (0, 0))],
        out_specs=[pl.BlockSpec((T, LANE), lambda i: (0, 0)), pl.BlockSpec((1, LANE), lambda i: (0, 0))],
        compiler_params=_cp(("arbitrary",)),
    )(dc, u, bias_pad)


ATTN_HEADS = 4


def _attn_block(T):
    return 256 if T % 256 == 0 and T >= 512 else 128


def _attn_fwd(u, c3, T):
    H, HP = 8, ATTN_HEADS
    bq = _attn_block(T)
    nq = T // bq
    scale = FOX_HD ** -0.5
    lanes = [slice(h * LANE, (h + 1) * LANE) for h in range(HP)]

    def kern(q_ref, k_ref, v_ref, z_ref, cq_ref, ck_ref, o_ref, oa_ref, lse_ref):
        i = pl.program_id(1)
        q = [(q_ref[:, ln] * scale).astype(BF16) for ln in lanes]
        c0 = [cq_ref[h][:, 0:1] for h in range(HP)]

        def step(j, carry, diagonal=False):
            off = pl.multiple_of(j * bq, bq)
            s = [_nt(q[h], k_ref[pl.ds(off, bq), lanes[h]].astype(BF16)) + (c0[h] - ck_ref[h, :, pl.ds(off, bq)])
                 for h in range(HP)]
            ps, out = [], []
            for h in range(HP):
                m, l, acc = carry[h]
                sh = s[h]
                if diagonal:
                    sh = jnp.where(_iota2((bq, bq), 1) <= _iota2((bq, bq), 0), sh, NEG)
                m_new = jnp.maximum(m, jnp.max(sh, axis=1, keepdims=True))
                p = jnp.exp(sh - m_new)
                alpha = jnp.exp(m - m_new)
                p_hi = p.astype(BF16)
                ps.append((p_hi, (p - p_hi.astype(F32)).astype(BF16)))
                out.append((m_new, alpha * l + jnp.sum(p, axis=1, keepdims=True), alpha * acc))
            res = []
            for h, (m, l, acc) in enumerate(out):
                vj = v_ref[pl.ds(off, bq), lanes[h]].astype(BF16)
                res.append((m, l, acc + _nn(ps[h][0], vj) + _nn(ps[h][1], vj)))
            return tuple(res)

        init = tuple((jnp.full((bq, 1), NEG, F32), jnp.zeros((bq, 1), F32), jnp.zeros((bq, FOX_HD), F32)) for _ in range(HP))
        res = step(i, lax.fori_loop(0, i, step, init), diagonal=True)
        for h, (m, l, acc) in enumerate(res):
            o = acc / l
            z = z_ref[:, lanes[h]]
            o_ref[:, lanes[h]] = o
            oa_ref[:, lanes[h]] = (o * z * _sigmoid(z)).astype(BF16)
            lse_ref[h] = m + jnp.log(l)

    W = HP * LANE
    return pl.pallas_call(
        kern, name="fox_attn_fwd",
        out_shape=[jax.ShapeDtypeStruct((T, 1024), F32), jax.ShapeDtypeStruct((T, 1024), BF16),
                   jax.ShapeDtypeStruct((H, T, 1), F32)],
        grid=(H // HP, nq),
        in_specs=[pl.BlockSpec((bq, W), lambda g, i: (i, g)),
                  pl.BlockSpec((T, W), lambda g, i: (0, 8 // HP + g)),
                  pl.BlockSpec((T, W), lambda g, i: (0, 16 // HP + g)),
                  pl.BlockSpec((bq, W), lambda g, i: (i, 24 // HP + g)),
                  pl.BlockSpec((HP, 1, bq), lambda g, i: (g, 0, i)),
                  pl.BlockSpec((HP, 1, T), lambda g, i: (g, 0, 0))],
        out_specs=[pl.BlockSpec((bq, W), lambda g, i: (i, g)),
                   pl.BlockSpec((bq, W), lambda g, i: (i, g)),
                   pl.BlockSpec((HP, bq, 1), lambda g, i: (g, i, 0))],
        compiler_params=_cp(("parallel", "arbitrary")),
    )(u, u, u, u, c3, c3)


def _attn_probs(s, lse_i, diagonal):
    if not diagonal:
        return jnp.exp(s - lse_i)
    keep = _iota2(s.shape, 1) <= _iota2(s.shape, 0)
    return jnp.where(keep, jnp.exp(jnp.where(keep, s, NEG) - lse_i), 0.0)


def _attn_pre_math(doa, z, o):
    sg = _sigmoid(z)
    do = (doa * z * sg).astype(BF16)
    dz = doa * o * (sg * (1.0 + z * (1.0 - sg)))
    head_of = (_iota2((o.shape[1], LANE), 0) // FOX_HD == _iota2((o.shape[1], LANE), 1)).astype(F32)
    return do, dz, _nn(do.astype(F32) * o, head_of, HI)


def _attn_bwd(u, c3, lse, do, delta, T):
    H, HP = 8, ATTN_HEADS
    bq = _attn_block(T)
    nq = T // bq
    scale = FOX_HD ** -0.5
    lanes = [slice(h * LANE, (h + 1) * LANE) for h in range(HP)]

    def kern(q_ref, k_ref, v_ref, c_ref, lse_ref, do_ref, dl_ref, dq_ref, dk_ref, dv_ref, dc_ref):
        j = pl.program_id(1)

        @pl.when(j == 0)
        def _():
            dq_ref[...] = jnp.zeros(dq_ref.shape, F32)

        kj = [k_ref[:, ln].astype(BF16) for ln in lanes]
        vj = [v_ref[:, ln].astype(BF16) for ln in lanes]
        joff = pl.multiple_of(j * bq, bq)
        ck = [c_ref[h, :, pl.ds(joff, bq)] for h in range(HP)]

        def step(i, carry, diagonal=False):
            off = pl.multiple_of(i * bq, bq)
            qs = [(q_ref[pl.ds(off, bq), ln] * scale).astype(BF16) for ln in lanes]
            dob = [do_ref[pl.ds(off, bq), ln] for ln in lanes]
            s = [_nt(qs[h], kj[h]) + (c_ref[h, :, pl.ds(off, bq)][:, 0:1] - ck[h]) for h in range(HP)]
            dp = [_nt(dob[h], vj[h]) for h in range(HP)]
            pb, dsb, dcs = [], [], []
            for h in range(HP):
                p = _attn_probs(s[h], lse_ref[h, pl.ds(off, bq), :], diagonal)
                ds = p * (dp[h] - dl_ref[h, pl.ds(off, bq), :])
                pb.append(p.astype(BF16))
                dsb.append(ds.astype(BF16))
                dcs.append(jnp.sum(ds, axis=0, keepdims=True))
            out = []
            for h, (dk, dv, dc) in enumerate(carry):
                dq_ref[pl.ds(off, bq), lanes[h]] += _nn(dsb[h], kj[h]) * scale
                out.append((dk + _tn(dsb[h], qs[h]), dv + _tn(pb[h], dob[h]), dc - dcs[h]))
            return tuple(out)

        init = tuple((jnp.zeros((bq, FOX_HD), F32), jnp.zeros((bq, FOX_HD), F32), jnp.zeros((1, bq), F32)) for _ in range(HP))
        res = lax.fori_loop(j + 1, nq, step, step(j, init, diagonal=True))
        for h, (dk, dv, dc) in enumerate(res):
            dk_ref[:, lanes[h]] = dk
            dv_ref[:, lanes[h]] = dv
            dc_ref[h] = dc

    W = HP * LANE
    full = lambda cb: pl.BlockSpec((T, W), lambda g, j: (0, cb // HP + g))
    blk = lambda cb: pl.BlockSpec((bq, W), lambda g, j: (j, cb // HP + g))
    col1 = pl.BlockSpec((HP, T, 1), lambda g, j: (g, 0, 0))
    return pl.pallas_call(
        kern, name="fox_attn_bwd",
        out_shape=[jax.ShapeDtypeStruct((T, 1024), F32)] * 3 + [jax.ShapeDtypeStruct((H, 1, T), F32)],
        grid=(H // HP, nq),
        in_specs=[full(0), blk(8), blk(16), pl.BlockSpec((HP, 1, T), lambda g, j: (g, 0, 0)), col1, full(0), col1],
        out_specs=[full(0), blk(0), blk(0), pl.BlockSpec((HP, 1, bq), lambda g, j: (g, 0, j))],
        compiler_params=_cp(("parallel", "arbitrary")),
    )(u, u, u, c3, lse, do, delta)


def _place():
    return lax.axis_index("x"), lax.axis_index("y"), lax.axis_index("c")


def _slot(p):
    return 4 * p[0] + 2 * p[1] + p[2]


def _other_chips(x, y):
    return [(1 - x, y), (x, 1 - y), (1 - x, 1 - y)]


def _allgather_steps(in_refs, out_refs, scratch):
    (src,), (dst,) = in_refs, out_refs
    send_sems, recv_sems, local_sem = scratch
    x, y, c = _place()
    me, sibling = (x, y, c), (x, y, 1 - c)
    chips = _other_chips(x, y)

    def copy(k, block, to, from_input=False):
        d = dst.at[_slot(block)]
        return pltpu.make_async_remote_copy(
            src_ref=src if from_input else d, dst_ref=d, send_sem=send_sems.at[k], recv_sem=recv_sems.at[k],
            device_id=to, device_id_type=MESH)

    def first_copies():
        return [copy(0, me, sibling, True)] + [copy(1 + j, me, (*chip, c), True) for j, chip in enumerate(chips)]

    def start():
        pltpu.make_async_copy(src, dst.at[_slot(me)], local_sem).start()
        for cp in first_copies():
            cp.start()

    def finish():
        passed = []
        for j, chip in enumerate(chips):
            copy(1 + j, (*chip, c), me).wait_recv()
            passed.append(copy(4 + j, (*chip, c), sibling))
            passed[-1].start()
        copy(0, sibling, me).wait_recv()
        for j, chip in enumerate(chips):
            copy(4 + j, (*chip, 1 - c), me).wait_recv()
        for cp in first_copies() + passed:
            cp.wait_send()
        pltpu.make_async_copy(src, dst.at[_slot(me)], local_sem).wait()

    return start, finish


def _allgather_relay_steps(in_refs, out_refs, scratch):
    (src,), (dst,) = in_refs, out_refs
    send_sems, recv_sems, local_sem = scratch
    x, y, c = _place()
    me, sibling = (x, y, c), (x, y, 1 - c)
    x_nbr, y_nbr, diag = (1 - x, y, c), (x, 1 - y, c), (1 - x, 1 - y, c)
    flip = lambda a, bit: a + bit - 2 * a * bit
    relay_from = (flip(x, 1 - c), flip(y, c), c)
    relay_to = (flip(x, c), flip(y, 1 - c), c)

    def copy(k, block, to, from_input=False):
        d = dst.at[_slot(block)]
        return pltpu.make_async_remote_copy(
            src_ref=src if from_input else d, dst_ref=d, send_sem=send_sems.at[k], recv_sem=recv_sems.at[k],
            device_id=to, device_id_type=MESH)

    def first_copies():
        return [copy(0, me, sibling, True), copy(1, me, x_nbr, True), copy(2, me, y_nbr, True)]

    def other(block):
        return block[:2] + (1 - c,)

    def start():
        pltpu.make_async_copy(src, dst.at[_slot(me)], local_sem).start()
        for cp in first_copies():
            cp.start()

    def finish():
        copy(1, x_nbr, me).wait_recv()
        copy(2, y_nbr, me).wait_recv()
        later = [copy(3, relay_from, relay_to), copy(4, x_nbr, sibling), copy(5, y_nbr, sibling)]
        for cp in later:
            cp.start()
        copy(3, diag, me).wait_recv()
        later.append(copy(6, diag, sibling))
        later[-1].start()
        copy(0, sibling, me).wait_recv()
        for k, block in ((4, x_nbr), (5, y_nbr), (6, diag)):
            copy(k, other(block), me).wait_recv()
        for cp in first_copies() + later:
            cp.wait_send()
        pltpu.make_async_copy(src, dst.at[_slot(me)], local_sem).wait()

    return start, finish


def _allgather_side(a, relay=False):
    return ((a,), (jax.ShapeDtypeStruct((N_DEV,) + a.shape, a.dtype),),
            (pltpu.SemaphoreType.DMA((7,)), pltpu.SemaphoreType.DMA((7,)), pltpu.SemaphoreType.DMA),
            _allgather_relay_steps if relay else _allgather_steps)


def _allgather(a, name, relay=False):
    ins, outs, scratch, make = _allgather_side(a, relay)

    def body(a_ref, o_ref, *scr):
        start, finish = make((a_ref,), (o_ref,), scr)
        start()
        finish()

    any_spec = pl.BlockSpec(memory_space=pl.ANY)
    return pl.pallas_call(body, name=name, out_shape=outs[0], in_specs=[any_spec], out_specs=any_spec,
                          scratch_shapes=list(scratch))(a)


def _exchange_pair_steps(in_refs, out_refs, scratch):
    (src,), (dst,) = in_refs, out_refs
    send_sems, recv_sems = scratch
    x, y, c = _place()
    sibling = (x, y, 1 - c)
    slots = [_slot(sibling)] + [_slot((*chip, 1 - c)) for chip in _other_chips(x, y)]

    def copies():
        return [pltpu.make_async_remote_copy(
            src_ref=src.at[ps], dst_ref=dst.at[k], send_sem=send_sems.at[k], recv_sem=recv_sems.at[k],
            device_id=sibling, device_id_type=MESH) for k, ps in enumerate(slots)]

    def start():
        for cp in copies():
            cp.start()

    def finish():
        for cp in copies():
            cp.wait()

    return start, finish


def _exchange_pair_side(g):
    return ((g,), (jax.ShapeDtypeStruct((4,) + g.shape[1:], g.dtype),),
            (pltpu.SemaphoreType.DMA((4,)), pltpu.SemaphoreType.DMA((4,))), _exchange_pair_steps)


def _exchange_pair(g):
    ins, outs, scratch, make = _exchange_pair_side(g)

    def body(g_ref, r_ref, *scr):
        start, finish = make((g_ref,), (r_ref,), scr)
        start()
        finish()

    any_spec = pl.BlockSpec(memory_space=pl.ANY)
    return pl.pallas_call(body, name="exchange_pair", out_shape=outs[0], in_specs=[any_spec], out_specs=any_spec,
                          scratch_shapes=list(scratch))(g)


def _tiling(R, Cc, tile, by_cols):
    if by_cols:
        assert Cc % tile == 0
        return Cc // tile, (R, tile), lambda lead, i: (lead, 0, i)
    assert R % tile == 0
    return R // tile, (tile, Cc), lambda lead, i: (lead, i, 0)


def _pair_add(name, g, r1, slots, tile, by_cols=False):
    _, R, Cc = g.shape
    steps, blk, at = _tiling(R, Cc, tile, by_cols)

    def kern(s_ref, a_ref, b_ref, o_ref):
        o_ref[...] = (a_ref[...].astype(F32) + b_ref[...].astype(F32)).astype(o_ref.dtype)

    return pl.pallas_call(
        kern, name=name,
        out_shape=jax.ShapeDtypeStruct((3, R, Cc), BF16),
        grid_spec=pltpu.PrefetchScalarGridSpec(
            num_scalar_prefetch=1, grid=(3, steps),
            in_specs=[pl.BlockSpec((1,) + blk, lambda j, i, s: at(s[j], i)),
                      pl.BlockSpec((1,) + blk, lambda j, i, s: at(1 + j, i))],
            out_specs=pl.BlockSpec((1,) + blk, lambda j, i, s: at(j, i))),
        compiler_params=_cp(("arbitrary", "arbitrary")),
    )(slots, g, r1)


def _axis_neighbours():
    x, y, c = _place()
    flip = lambda a, bit: a + bit - 2 * a * bit
    return (flip(x, c), flip(y, 1 - c), c), (flip(x, 1 - c), flip(y, c), c), c


def _exchange_first_steps(in_refs, out_refs, scratch):
    pairs = list(zip(in_refs, out_refs))
    send_sems, recv_sems = scratch
    first, _, c = _axis_neighbours()

    def copies():
        return [pltpu.make_async_remote_copy(
            src_ref=src.at[j], dst_ref=dst.at[k], send_sem=send_sems.at[t, k], recv_sem=recv_sems.at[t, k],
            device_id=first, device_id_type=MESH)
            for t, (src, dst) in enumerate(pairs) for k, j in enumerate((1 - c, 2))]

    def start():
        for cp in copies():
            cp.start()

    def finish():
        for cp in copies():
            cp.wait()

    return start, finish


def _exchange_first_side(sa, sb):
    return ((sa, sb), (jax.ShapeDtypeStruct((2,) + sa.shape[1:], sa.dtype), jax.ShapeDtypeStruct((2,) + sb.shape[1:], sb.dtype)),
            (pltpu.SemaphoreType.DMA((2, 2)), pltpu.SemaphoreType.DMA((2, 2))), _exchange_first_steps)


def _axis_add(name, s, t1, core, tile, by_cols=False):
    _, R, Cc = s.shape
    steps, blk, at = _tiling(R, Cc, tile, by_cols)

    def kern(c_ref, a_ref, b_ref, o_ref):
        o_ref[...] = (a_ref[...].astype(F32) + b_ref[...].astype(F32)).astype(o_ref.dtype)

    return pl.pallas_call(
        kern, name=name,
        out_shape=jax.ShapeDtypeStruct((1, R, Cc), BF16),
        grid_spec=pltpu.PrefetchScalarGridSpec(
            num_scalar_prefetch=1, grid=(steps,),
            in_specs=[pl.BlockSpec((1,) + blk, lambda i, cr: at(cr[0], i)),
                      pl.BlockSpec((1,) + blk, lambda i, cr: at(1, i))],
            out_specs=pl.BlockSpec((1,) + blk, lambda i, cr: at(0, i))),
        compiler_params=_cp(("arbitrary",)),
    )(core, s, t1)


def _exchange_second_steps(in_refs, out_refs, scratch):
    send_sems, recv_sems = scratch
    _, second, _ = _axis_neighbours()

    def copies():
        return [pltpu.make_async_remote_copy(src_ref=src, dst_ref=dst, send_sem=send_sems.at[t], recv_sem=recv_sems.at[t],
                                             device_id=second, device_id_type=MESH)
                for t, (src, dst) in enumerate(zip(in_refs, out_refs))]

    def start():
        for cp in copies():
            cp.start()

    def finish():
        for cp in copies():
            cp.wait()

    return start, finish


def _exchange_second_side(pa, pb):
    return ((pa, pb), (jax.ShapeDtypeStruct(pa.shape, pa.dtype), jax.ShapeDtypeStruct(pb.shape, pb.dtype)),
            (pltpu.SemaphoreType.DMA((2,)), pltpu.SemaphoreType.DMA((2,))), _exchange_second_steps)


def _adamw(name, w, m, v, parts, tile, by_cols=False):
    R, Cc = w.shape
    steps, blk_shape, at = _tiling(R, Cc, tile, by_cols)
    n_parts = len(parts)

    def kern(*refs):
        w_ref, m_ref, v_ref = refs[:3]
        g = None
        for r_ref, (_, n) in zip(refs[3:3 + n_parts], parts):
            for s in range(n):
                term = r_ref[s].astype(F32)
                g = term if g is None else g + term
        g_out, d_out, m_out, v_out = refs[3 + n_parts:]
        mn = ADAM_B1 * m_ref[...] + (1.0 - ADAM_B1) * g
        vn = ADAM_B2 * v_ref[...] + (1.0 - ADAM_B2) * (g * g)
        m_hat = mn / (1.0 - ADAM_B1 ** ADAM_STEP)
        v_hat = vn / (1.0 - ADAM_B2 ** ADAM_STEP)
        g_out[...] = g
        d_out[...] = -ADAM_LR * (m_hat / (jnp.sqrt(v_hat) + ADAM_EPS) + ADAM_WD * w_ref[...])
        m_out[...] = mn
        v_out[...] = vn

    blk = pl.BlockSpec(blk_shape, lambda i: at(0, i)[1:])
    return pl.pallas_call(
        kern, name=name,
        out_shape=[jax.ShapeDtypeStruct((R, Cc), F32)] * 4,
        grid=(steps,),
        in_specs=[blk] * 3 + [pl.BlockSpec((n,) + blk_shape, lambda i: at(0, i)) for (_, n) in parts],
        out_specs=[blk] * 4,
        compiler_params=_cp(("arbitrary",)),
    )(w, m, v, *[a for (a, _) in parts])


def _assemble_columns(blocks, pieces, zeros, width):
    _, R, Cc = blocks.shape
    tr = min(256, R)

    def kern(b_ref, o_ref):
        for col, n in zeros:
            o_ref[:, col:col + n] = jnp.zeros((tr, n), o_ref.dtype)
        for col, d, lo, n in pieces:
            o_ref[:, col:col + n] = b_ref[d, :, lo:lo + n]

    return pl.pallas_call(
        kern, name="assemble_w_in",
        out_shape=jax.ShapeDtypeStruct((R, width), blocks.dtype),
        grid=(R // tr,),
        in_specs=[pl.BlockSpec((N_DEV, tr, Cc), lambda i: (0, i, 0))],
        out_specs=pl.BlockSpec((tr, width), lambda i: (i, 0)),
        compiler_params=_cp(("parallel",)),
    )(blocks)


def _split_rows(x, pieces, rows):
    _, Cc = x.shape
    tc = min(256, Cc)

    def kern(x_ref, o_ref):
        for d, lo, row, n in pieces:
            o_ref[d, lo:lo + n, :] = x_ref[row:row + n, :]

    return pl.pallas_call(
        kern, name="split_w_in_grad",
        out_shape=jax.ShapeDtypeStruct((N_DEV, rows, Cc), x.dtype),
        grid=(Cc // tc,),
        in_specs=[pl.BlockSpec((x.shape[0], tc), lambda i: (0, i))],
        out_specs=pl.BlockSpec((N_DEV, rows, tc), lambda i: (0, 0, i)),
        compiler_params=_cp(("parallel",)),
    )(x)


def _pad_cols(a, w):
    return jnp.pad(a, ((0, 0), (0, w - a.shape[1])))


def _pad_rows(a, r):
    return jnp.pad(a, ((0, r - a.shape[0]), (0, 0)))


def _pack_b(pf, pr, wo, w2, a2, rows):
    body = jnp.concatenate([pf, pr, wo.reshape(2048, 256), jnp.concatenate([w2, a2], axis=1)], axis=0)
    return _pad_rows(body, rows)


def kernel(x, norm_gain, w_in, fox_forget_bias, rwkv_shift_mix, rwkv_w0, rwkv_w2, rwkv_a0, rwkv_a2, rwkv_k_k, rwkv_k_a, rwkv_r_k, rwkv_ln_w, rwkv_ln_b, w_proj_fox, w_proj_rwkv, w_out, final_norm_gain, loss_target, m_norm_gain, m_w_in, m_fox_forget_bias, m_rwkv_shift_mix, m_rwkv_w0, m_rwkv_w2, m_rwkv_a0, m_rwkv_a2, m_rwkv_k_k, m_rwkv_k_a, m_rwkv_r_k, m_rwkv_ln_w, m_rwkv_ln_b, m_w_proj_fox, m_w_proj_rwkv, m_w_out, m_final_norm_gain, v_norm_gain, v_w_in, v_fox_forget_bias, v_rwkv_shift_mix, v_rwkv_w0, v_rwkv_w2, v_rwkv_a0, v_rwkv_a2, v_rwkv_k_k, v_rwkv_k_a, v_rwkv_r_k, v_rwkv_ln_w, v_rwkv_ln_b, v_w_proj_fox, v_w_proj_rwkv, v_w_out, v_final_norm_gain):
    T, D = x.shape[1], x.shape[2]
    assert D == 2048 and T % LANE == 0
    NI = w_in.shape[2]
    IN = N_DEV * NI
    RB = 4224
    x2 = x[0]
    lt2 = loss_target[0]
    me = _slot(_place())

    tb = min(256, T)
    tbh = min(128, T)
    h, wa = _rowcall("rms_fwd", T, tb, [_rows(x2, tb, D), _whole(norm_gain)],
                     lambda i, xv, g: ([_rms_math(xv, g)], []), [(D, BF16)],
                     side=_allgather_side(w_in[0].astype(BF16), relay=True))
    packed_own = _pack_b(w_proj_fox[0], w_proj_rwkv[0], w_out[0], rwkv_w2[0], rwkv_a2[0], RB).astype(BF16)
    sections = [(0, 4096, 0), (4104, 4096, 4096), (8392, 4096, 8192), (4096, 8, 12288), (8200, 96, 12544), (8296, 96, 12672)]
    NP = 12800
    pieces, zeros, at_col = [], [], 0
    for lo, width, pad_lo in sections:
        if pad_lo > at_col:
            zeros.append((at_col, pad_lo - at_col))
        col = lo
        while col < lo + width:
            d = col // NI
            stop = min(lo + width, (d + 1) * NI)
            pieces.append((pad_lo + col - lo, d, col - d * NI, stop - col))
            col = stop
        at_col = pad_lo + width
    zeros.append((at_col, NP - at_col))
    w_pad = _assemble_columns(wa, pieces, zeros, NP)
    F_CB, LORA_CB = 96, 49

    mu = rwkv_shift_mix
    mu_main = mu[:, 0:4096]
    mu_lora = jnp.concatenate([_pad_cols(mu[:, 4096:4192], LANE), _pad_cols(mu[:, 4192:4288], LANE)], axis=1)
    bias_pad = _pad_cols(fox_forget_bias, LANE)
    rk_flat = rwkv_r_k.reshape(1, 1024)
    gf = final_norm_gain.reshape(1, D)

    u, wb = _mm(h, w_pad, tm=1024, tn=1280, name="mm_in", side=_allgather_side(packed_own))
    wpf = wb[:, 0:1024, :].transpose(1, 0, 2).reshape(1024, D)
    wpr = wb[:, 1024:2048, :].transpose(1, 0, 2).reshape(1024, D)
    wo = wb[:, 2048:4096, :].reshape(N_DEV * 256, D)
    w2p = _pad_rows(wb[:, 4096:4192, 0:128].transpose(1, 0, 2).reshape(96, 1024).astype(F32), LANE)
    a2p = _pad_rows(wb[:, 4096:4192, 128:256].transpose(1, 0, 2).reshape(96, 1024).astype(F32), LANE)

    c8 = _gates_fwd(u, bias_pad, T, F_CB)
    c3 = c8.reshape(8, 1, T)
    o_raw, o_a, lse = _attn_fwd(u, c3, T)

    def shift_body(i, um, hm, ul, hl, mm_, ml):
        outs = []
        for uv, hv, mv in ((um, hm, mm_), (ul, hl, ml)):
            hv = jnp.where(i == 0, 0.0, hv)
            prev = pltpu.roll(jnp.concatenate([hv, uv], axis=0), 1, 0)[SUB:]
            outs.append(uv + (prev - uv) * mv)
        return outs, []

    def halo_prev(arr, w, cb):
        return (arr, (SUB, w), lambda i: (jnp.maximum(i * (tbh // SUB) - 1, 0), cb))

    xs, xl = _rowcall("rwkv_shift_fwd", T, tbh,
                      [_rows(u, tbh, 4096, 1), halo_prev(u, 4096, 1), _rows(u, tbh, 256, LORA_CB), halo_prev(u, 256, LORA_CB),
                       _whole(mu_main), _whole(mu_lora)],
                      shift_body, [(4096, F32), (256, F32)])

    prep_par = [_whole(rwkv_w0), _whole(rwkv_a0), _whole(rwkv_k_k), _whole(rwkv_k_a), _whole(w2p), _whole(a2p)]
    prep_rows = [_rows(xs, tbh, 1024, 1), _rows(xl, tbh, LANE, 0), _rows(xl, tbh, LANE, 1)]
    lw, kp, an, bb = _rowcall("rwkv_prep_fwd", T, tbh, prep_rows + prep_par,
                              lambda i, *a: (list(_prep_math(*a)), []), [(1024, F32)] * 4)
    y, states, invs = _scan_fwd(xs, lw, kp, an, bb, T)
    post_rows = [_rows(y, tbh, 1024), _rows(xs, tbh, 1024, 0), _rows(kp, tbh, 1024), _rows(xs, tbh, 1024, 2), _rows(xs, tbh, 1024, 3)]
    post_par = [_whole(rwkv_ln_w), _whole(rwkv_ln_b), _whole(rk_flat)]
    (o_b,) = _rowcall("rwkv_post_fwd", T, tbh, post_rows + post_par,
                      lambda i, *a: ([_post_math(*a)], []), [(1024, BF16)])

    pa = _mm(o_a, wpf, name="mm_proj_fox")
    pb = _mm(o_b, wpr, name="mm_proj_rwkv")
    merge_rows = [_rows(u, tb, D, 4), _rows(u, tb, D, 5), _rows(pa, tb, D), _rows(pb, tb, D)]
    (mg,) = _rowcall("merge_fwd", T, tb, merge_rows, lambda i, *a: ([_merge_math(*a)], []), [(D, BF16)])
    mo = _mm(mg, wo, name="mm_out")

    def head_body(i, xv, mov, ltv, g):
        out = xv + mov
        r = lax.rsqrt(jnp.mean(out * out, axis=-1, keepdims=True) + RMS_EPS)
        yn = out * r
        err = yn * g - ltv
        loss = 0.5 * jnp.sum(jnp.sum(err * err, axis=-1, keepdims=True), axis=0, keepdims=True) / D
        dyv = err / D
        dyn = dyv * g
        dout = r * (dyn - yn * jnp.mean(dyn * yn, axis=-1, keepdims=True))
        return [dout], [loss, jnp.sum(dyv * yn, axis=0, keepdims=True)]

    dout, loss_p, dgf_p = _rowcall("loss_head", T, tb, [_rows(x2, tb, D), _rows(mo, tb, D), _rows(lt2, tb, D), _whole(gf)],
                                   head_body, [(D, F32)], [(1, 1), (1, D)])

    dm = _mm(dout, wo, tb=True, name="mm_out_dx")
    dwo = _mm(mg, dout, ta=True, out_dtype=BF16, name="mm_out_dw")

    def merge_bwd_body(i, ga, gb, pav, pbv, dmv):
        _, vjp = jax.vjp(_merge_math, ga, gb, pav, pbv)
        dga, dgb, dpa, dpb = vjp(dmv)
        return [dga, dgb, dpa, dpb], []

    dga, dgb, dpa, dpb = _rowcall("merge_bwd", T, tb, merge_rows + [_rows(dm, tb, D)], merge_bwd_body,
                                  [(D, BF16), (D, BF16), (D, BF16), (D, BF16)])
    doa = _mm(dpa, wpf, tb=True, name="mm_proj_fox_dx")
    dwpf = _mm(o_a, dpa, ta=True, out_dtype=BF16, name="mm_proj_fox_dw")
    dob = _mm(dpb, wpr, tb=True, name="mm_proj_rwkv_dx")
    dwpr = _mm(o_b, dpb, ta=True, out_dtype=BF16, name="mm_proj_rwkv_dw")

    do_b, dza, delta128 = _rowcall("fox_attn_pre", T, tb, [_rows(doa, tb, 1024), _rows(u, tb, 1024, 3), _rows(o_raw, tb, 1024)],
                                   lambda i, *a: (list(_attn_pre_math(*a)), []), [(1024, BF16), (1024, F32), (LANE, F32)])
    delta = delta128[:, 0:8].T.reshape(8, T, 1)
    dq, dk, dv, dc3 = _attn_bwd(u, c3, lse, do_b, delta, T)
    dfl, dbias_p = _gates_bwd(dc3.reshape(8, T), u, bias_pad, T, F_CB)

    def post_bwd_body(i, yv, rv, kpv, vv, zv, lnw, lnb, rkv, dobv):
        _, vjp = jax.vjp(_post_math, yv, rv, kpv, vv, zv, lnw, lnb, rkv)
        dy_, dr_, dkp_, dv_, dz_, dlnw, dlnb, drk = vjp(dobv)
        return [dy_, dr_, dkp_, dv_, dz_], [dlnw, dlnb, drk]

    dy_s, dr_p, dkp_p, dv_p, dzb, dlnw_p, dlnb_p, drk_p = _rowcall(
        "rwkv_post_bwd", T, tbh, post_rows + post_par + [_rows(dob, tbh, 1024)], post_bwd_body,
        [(1024, F32)] * 5, [(1, 1024)] * 3)
    dr_s, dlw, dkp_s, dv_s, dan, dbb = _scan_bwd(xs, lw, kp, an, bb, states, invs, dy_s, T)

    def prep_bwd_body(i, xk, xwd, xad, w0, a0, kkw, kaw, w2v, a2v, dlw_, dkp1, dkp2, dan_, dbb_, dr1, dr2, dv1, dv2, dz_):
        _, vjp = jax.vjp(_prep_math, xk, xwd, xad, w0, a0, kkw, kaw, w2v, a2v)
        dxk, dxwd, dxad, dw0, da0, dkk, dka, dw2, da2 = vjp((dlw_, dkp1 + dkp2, dan_, dbb_))
        return [[dr1 + dr2, dxk, dv1 + dv2, dz_], [dxwd, dxad]], [dw0, da0, dkk, dka, dw2, da2]

    cots = [dlw, dkp_s, dkp_p, dan, dbb, dr_s, dr_p, dv_s, dv_p, dzb]
    dxs, dxl, dw0_p, da0_p, dkk_p, dka_p, dw2_p, da2_p = _rowcall(
        "rwkv_prep_bwd", T, tbh, prep_rows + prep_par + [_rows(c_, tbh, 1024) for c_ in cots], prep_bwd_body,
        [(4096, F32), (256, F32)], [(1, 1024)] * 4 + [(LANE, 1024)] * 2)

    def shift_bwd_body(i, dm_, hm, dl_, hl, um, pm, ul, pl_, mm_, ml):
        last = i == T // tbh - 1
        outs, accs = [], []
        for dv_, hv, uv, pv, mv in ((dm_, hm, um, pm, mm_), (dl_, hl, ul, pl_, ml)):
            hv = jnp.where(last, 0.0, hv)
            nxt = pltpu.roll(jnp.concatenate([dv_, hv], axis=0), tbh + SUB - 1, 0)[:tbh]
            pv = jnp.where(i == 0, 0.0, pv)
            prev = pltpu.roll(jnp.concatenate([pv, uv], axis=0), 1, 0)[SUB:]
            outs.append(dv_ * (1.0 - mv) + nxt * mv)
            accs.append(jnp.sum(dv_ * (prev - uv), axis=0, keepdims=True))
        return outs, accs

    def halo_next(arr, w, cb):
        last_blk = T // SUB - 1
        return (arr, (SUB, w), lambda i: (jnp.minimum((i + 1) * (tbh // SUB), last_blk), cb))

    du_b, du_l, dmu_main_p, dmu_lora_p = _rowcall(
        "rwkv_shift_bwd", T, tbh,
        [_rows(dxs, tbh, 4096), halo_next(dxs, 4096, 0), _rows(dxl, tbh, 256), halo_next(dxl, 256, 0),
         _rows(u, tbh, 4096, 1), halo_prev(u, 4096, 1), _rows(u, tbh, 256, LORA_CB), halo_prev(u, 256, LORA_CB),
         _whole(mu_main), _whole(mu_lora)],
        shift_bwd_body, [(4096, BF16), (256, BF16)], [(1, 4096), (1, 256)])

    (du,) = _rowcall("assemble_du", T, tb,
                     [_rows(a_, tb, a_.shape[1]) for a_ in (dq, dk, dv, dza, du_b, dga, dgb, dfl, du_l)],
                     lambda i, *a: ([list(a[:8]) + [jnp.zeros((tb, LANE), BF16), a[8]]], []), [(NP, BF16)])
    lora_g = jnp.concatenate([dw2_p[:96].reshape(96, N_DEV, 128).transpose(1, 0, 2),
                              da2_p[:96].reshape(96, N_DEV, 128).transpose(1, 0, 2)], axis=2).astype(BF16)
    gb = jnp.concatenate([dwpf.reshape(1024, N_DEV, 256).transpose(1, 0, 2),
                          dwpr.reshape(1024, N_DEV, 256).transpose(1, 0, 2),
                          dwo.reshape(N_DEV, 2048, 256), lora_g, jnp.zeros((N_DEV, RB - 4192, 256), BF16)], axis=1)
    dw_pad_t, r1b = _mm(du, h, ta=True, out_dtype=BF16, tm=1280, tn=1024, name="mm_in_dw",
                        side=_exchange_pair_side(gb))
    ga = _split_rows(dw_pad_t, [(d, lo, pad_col, n) for pad_col, d, lo, n in pieces], NI)
    xx, yy, cc = _place()
    chip_slots = jnp.stack([_slot((*chip, cc)) for chip in _other_chips(xx, yy)]).astype(jnp.int32)
    r1a = _exchange_pair(ga)
    sa = _pair_add("pair_add_w_in", ga, r1a, chip_slots, 512, by_cols=True)
    sb = _pair_add("pair_add_packed", gb, r1b, chip_slots, 384)
    ga_own = lax.dynamic_index_in_dim(ga, me, 0, keepdims=True)
    gb_own = lax.dynamic_index_in_dim(gb, me, 0, keepdims=True)
    dh, t1a, t1b = _mm(du, w_pad, tb=True, tm=1024, tn=1024, tk=NP // 10, name="mm_in_dx", side=_exchange_first_side(sa, sb))
    core = jnp.stack([cc]).astype(jnp.int32)
    second = _exchange_second_side(_axis_add("axis_add_w_in", sa, t1a, core, 512, by_cols=True),
                                   _axis_add("axis_add_packed", sb, t1b, core, 384))

    def rms_bwd_body(i, xv, g, dhv, doutv):
        _, vjp = jax.vjp(_rms_math, xv, g)
        dx_, dg_ = vjp(dhv)
        return [dx_ + doutv], [dg_]

    grad_x2, dng_p, t2a, t2b = _rowcall(
        "rms_bwd", T, tb, [_rows(x2, tb, D), _whole(norm_gain), _rows(dh, tb, D), _rows(dout, tb, D)],
        rms_bwd_body, [(D, F32)], [(1, D)], side=second)

    w_in_outs = _adamw("adamw_w_in", w_in[0].T, m_w_in[0].T, v_w_in[0].T, [(ga_own, 1), (r1a, 1), (t1a, 1), (t2a, 1)], 256, by_cols=True)
    g_in, d_in, m_in, v_in = [o.T for o in w_in_outs]

    dmu = jnp.concatenate([dmu_main_p, dmu_lora_p[:, 0:96], dmu_lora_p[:, 128:224]], axis=1)
    small_parts = [dng_p, dbias_p[:, 0:8], dmu, dw0_p, da0_p, dkk_p, dka_p, drk_p, dlnw_p, dlnb_p, dgf_p, loss_p]
    SR = 128
    small = _pad_cols(jnp.concatenate(small_parts, axis=1), SR * LANE).reshape(SR, LANE)
    rs = _allgather(small, "allgather_small")
    pk = lambda pf, pr, wo_, w2_, a2_: _pack_b(pf[0], pr[0], wo_[0], w2_[0], a2_[0], RB)
    outs_b = _adamw("adamw_packed", pk(w_proj_fox, w_proj_rwkv, w_out, rwkv_w2, rwkv_a2),
                    pk(m_w_proj_fox, m_w_proj_rwkv, m_w_out, m_rwkv_w2, m_rwkv_a2),
                    pk(v_w_proj_fox, v_w_proj_rwkv, v_w_out, v_rwkv_w2, v_rwkv_a2), [(gb_own, 1), (r1b, 1), (t1b, 1), (t2b, 1)], 384)

    def pack_small(ng, fb, sm, w0, a0, kk_, ka_, rk_, lnw, lnb, fg):
        parts = [ng, fb, sm, w0, a0, kk_, ka_, rk_.reshape(1, 1024), lnw, lnb, fg.reshape(1, D), jnp.zeros((1, 1), F32)]
        return _pad_cols(jnp.concatenate(parts, axis=1), SR * LANE).reshape(SR, LANE)

    outs_s = _adamw("adamw_small",
                    pack_small(norm_gain, fox_forget_bias, rwkv_shift_mix, rwkv_w0, rwkv_a0, rwkv_k_k, rwkv_k_a, rwkv_r_k,
                               rwkv_ln_w, rwkv_ln_b, final_norm_gain),
                    pack_small(m_norm_gain, m_fox_forget_bias, m_rwkv_shift_mix, m_rwkv_w0, m_rwkv_a0, m_rwkv_k_k, m_rwkv_k_a,
                               m_rwkv_r_k, m_rwkv_ln_w, m_rwkv_ln_b, m_final_norm_gain),
                    pack_small(v_norm_gain, v_fox_forget_bias, v_rwkv_shift_mix, v_rwkv_w0, v_rwkv_a0, v_rwkv_k_k, v_rwkv_k_a,
                               v_rwkv_r_k, v_rwkv_ln_w, v_rwkv_ln_b, v_final_norm_gain),
                    [(rs, N_DEV)], SR)

    def unpack_b(pkd):
        return dict(w_proj_fox=pkd[0:1024][None], w_proj_rwkv=pkd[1024:2048][None], w_out=pkd[2048:4096].reshape(1, 256, D),
                    rwkv_w2=pkd[4096:4192, 0:128][None], rwkv_a2=pkd[4096:4192, 128:256][None])

    def unpack_s(pkd):
        flat = pkd.reshape(1, SR * LANE)
        names = [("norm_gain", D), ("fox_forget_bias", 8), ("rwkv_shift_mix", 4288), ("rwkv_w0", 1024), ("rwkv_a0", 1024),
                 ("rwkv_k_k", 1024), ("rwkv_k_a", 1024), ("rwkv_r_k", 1024), ("rwkv_ln_w", 1024), ("rwkv_ln_b", 1024),
                 ("final_norm_gain", D), ("loss", 1)]
        out, off = {}, 0
        for nm, n in names:
            out[nm] = flat[:, off:off + n]
            off += n
        out["rwkv_r_k"] = out["rwkv_r_k"].reshape(1, 16, 64)
        out["final_norm_gain"] = out["final_norm_gain"].reshape(D)
        return out

    order = ["norm_gain", "w_in", "fox_forget_bias", "rwkv_shift_mix", "rwkv_w0", "rwkv_w2", "rwkv_a0", "rwkv_a2", "rwkv_k_k",
             "rwkv_k_a", "rwkv_r_k", "rwkv_ln_w", "rwkv_ln_b", "w_proj_fox", "w_proj_rwkv", "w_out", "final_norm_gain"]
    result = []
    loss = None
    for kind, big in enumerate((g_in, d_in, m_in, v_in)):
        d = {**unpack_b(outs_b[kind]), **unpack_s(outs_s[kind]), "w_in": big[None]}
        if kind == 0:
            loss = d["loss"].reshape(())
        result += [d[n] for n in order]
    return (loss, grad_x2[None], *result)
```

```python
import functools

import jax
import jax.numpy as jnp
from jax import lax
from jax.experimental import pallas as pl
from jax.experimental.pallas import tpu as pltpu

F32 = jnp.float32
BF16 = jnp.bfloat16
HI = lax.Precision.HIGHEST
H3 = lax.Precision.HIGH
MESH = pl.DeviceIdType.MESH

FOX_HD = 128
RW_HD = 64
RMS_EPS = 1e-6
GN_EPS = 64e-5
L2_EPS = 1e-12
ADAM_LR = 0.001
ADAM_B1 = 0.9
ADAM_B2 = 0.999
ADAM_EPS = 1e-08
ADAM_WD = 0.01
ADAM_STEP = 10

LANE = 128
SUB = 8
VMEM_LIMIT = 56 * 1024 * 1024
N_DEV = 8
CHUNK = 128
SCAN_GROUP = 2
SCAN_PAIRS = 2
PS = None
NEG = -1e30


def _scan_shape(T):
    c = min(CHUNK, T)
    return c, min(SCAN_GROUP, T // c)


def _cp(sem=None):
    return pltpu.CompilerParams(dimension_semantics=sem, vmem_limit_bytes=VMEM_LIMIT)


def _sigmoid(x):
    return jax.nn.sigmoid(x)


def _softplus(x):
    return jnp.maximum(x, 0.0) + jnp.log(1.0 + jnp.exp(-jnp.abs(x)))


def _nn(a, b, prec=None):
    return lax.dot_general(a, b, (((1,), (0,)), ((), ())), precision=prec, preferred_element_type=F32)


def _nt(a, b, prec=None):
    return lax.dot_general(a, b, (((1,), (1,)), ((), ())), precision=prec, preferred_element_type=F32)


def _tn(a, b, prec=None):
    return lax.dot_general(a, b, (((0,), (0,)), ((), ())), precision=prec, preferred_element_type=F32)


def _iota2(shape, dim):
    return lax.broadcasted_iota(jnp.int32, shape, dim)


def _seg_sum(x):
    r = _iota2((LANE, LANE), 0) // RW_HD
    c = _iota2((LANE, LANE), 1) // RW_HD
    bd = (r == c).astype(F32)
    parts = [_nn(x[:, j * LANE:(j + 1) * LANE], bd, H3) for j in range(x.shape[1] // LANE)]
    return parts[0] if len(parts) == 1 else jnp.concatenate(parts, axis=1)


def _mm(a, b, *, ta=False, tb=False, out_dtype=F32, tm=1024, tn=1024, tk=None, name, side=None):
    assert not (ta and tb)
    K, M = a.shape if ta else a.shape[::-1]
    N = b.shape[0] if tb else b.shape[1]
    tm, tn = min(tm, M), min(tn, N)
    tk = K if tk is None else tk
    nk = K // tk
    assert M % tm == 0 and N % tn == 0 and K % tk == 0
    gi, gj = M // tm, N // tn
    a_spec = pl.BlockSpec((tk, tm), lambda i, j, k: (k, i)) if ta else pl.BlockSpec((tm, tk), lambda i, j, k: (i, k))
    b_spec = pl.BlockSpec((tn, tk), lambda i, j, k: (j, k)) if tb else pl.BlockSpec((tk, tn), lambda i, j, k: (k, j))
    side_ins, side_outs, side_scr, side_make = side if side is not None else ((), (), (), None)
    n_si, n_so = len(side_ins), len(side_outs)
    n_acc = 0 if nk == 1 else 1

    def body(*refs):
        a_ref, b_ref = refs[:2]
        o_ref = refs[2 + n_si]
        scr = refs[3 + n_si + n_so:]
        k = pl.program_id(2)
        if side_make is not None:
            start, finish = side_make(refs[2:2 + n_si], refs[3 + n_si:3 + n_si + n_so], scr[n_acc:])
            first = jnp.logical_and(jnp.logical_and(pl.program_id(0) == 0, pl.program_id(1) == 0), k == 0)
            last = jnp.logical_and(jnp.logical_and(pl.program_id(0) == gi - 1, pl.program_id(1) == gj - 1), k == nk - 1)
            pl.when(first)(start)
        av = a_ref[...].astype(BF16)
        bv = b_ref[...].astype(BF16)
        p = _tn(av, bv) if ta else _nt(av, bv) if tb else _nn(av, bv)
        if nk == 1:
            o_ref[...] = p.astype(out_dtype)
        else:
            acc_ref = scr[0]

            @pl.when(k == 0)
            def _():
                acc_ref[...] = p

            @pl.when(k > 0)
            def _():
                acc_ref[...] += p

            @pl.when(k == nk - 1)
            def _():
                o_ref[...] = acc_ref[...].astype(out_dtype)
        if side_make is not None:
            pl.when(last)(finish)

    any_spec = pl.BlockSpec(memory_space=pl.ANY)
    res = pl.pallas_call(
        body, name=name,
        out_shape=[jax.ShapeDtypeStruct((M, N), out_dtype)] + list(side_outs),
        grid=(gi, gj, nk),
        in_specs=[a_spec, b_spec] + [any_spec] * n_si,
        out_specs=[pl.BlockSpec((tm, tn), lambda i, j, k: (i, j))] + [any_spec] * n_so,
        scratch_shapes=([] if nk == 1 else [pltpu.VMEM((tm, tn), F32)]) + list(side_scr),
        compiler_params=_cp(("arbitrary",) * 3 if side is not None else ("parallel", "parallel", "arbitrary")),
    )(a, b, *side_ins)
    return res if side is not None else res[0]


def _rows(arr, tb, w, cb=0):
    return (arr, (tb, w), lambda i: (i, cb))


def _whole(arr):
    nd = arr.ndim
    return (arr, arr.shape, lambda i: (0,) * nd)


def _rowcall(name, T, tb, ins, body, outs, accs=(), side=None):
    n_in, n_out, n_acc = len(ins), len(outs), len(accs)
    side_ins, side_outs, side_scr, side_make = side if side is not None else ((), (), (), None)
    n_si = len(side_ins)

    def kern(*refs):
        i = pl.program_id(0)
        if side_make is not None:
            side_refs = refs[n_in + n_si + n_out + n_acc:]
            start, finish = side_make(refs[n_in:n_in + n_si], side_refs[:len(side_outs)], side_refs[len(side_outs):])
            pl.when(i == 0)(start)
            refs = refs[:n_in] + refs[n_in + n_si:]
        vals = [r[...] for r in refs[:n_in]]
        ro, ao = body(i, *vals)
        for r, v in zip(refs[n_in:n_in + n_out], ro):
            if isinstance(v, (list, tuple)):
                off = 0
                for piece in v:
                    w = piece.shape[1]
                    r[:, off:off + w] = piece.astype(r.dtype)
                    off += w
            else:
                r[...] = v.astype(r.dtype)
        if accs:
            acc_refs = refs[n_in + n_out:n_in + n_out + n_acc]

            @pl.when(i == 0)
            def _():
                for r in acc_refs:
                    r[...] = jnp.zeros(r.shape, F32)

            for r, v in zip(acc_refs, ao):
                r[...] += v
        if side_make is not None:
            pl.when(i == T // tb - 1)(finish)

    any_spec = pl.BlockSpec(memory_space=pl.ANY)
    out_shape = [jax.ShapeDtypeStruct((T, w), dt) for (w, dt) in outs] + [jax.ShapeDtypeStruct(s, F32) for s in accs]
    out_specs = [pl.BlockSpec((tb, w), lambda i: (i, 0)) for (w, dt) in outs] + [pl.BlockSpec(s, lambda i: (0, 0)) for s in accs]
    res = pl.pallas_call(
        kern, name=name,
        out_shape=out_shape + list(side_outs),
        grid=(T // tb,),
        in_specs=[pl.BlockSpec(bs, im) for (_, bs, im) in ins] + [any_spec] * n_si,
        out_specs=out_specs + [any_spec] * len(side_outs),
        scratch_shapes=list(side_scr),
        compiler_params=_cp(("arbitrary",)),
    )(*[a for (a, _, _) in ins], *side_ins)
    return res


def _rms_math(x, g):
    r = lax.rsqrt(jnp.mean(x * x, axis=-1, keepdims=True) + RMS_EPS)
    return x * r * g


def _merge_math(ga, gb, pa, pb):
    return _sigmoid(ga) * pa + _sigmoid(gb) * pb


def _prep_math(xk, xwd, xad, w0, a0, kk_w, ka_w, w2p, a2p):
    z = w0 + _nn(jnp.tanh(xwd), w2p, H3)
    w = -_softplus(-z) - 0.5
    lw = -jnp.exp(w)
    ag = _sigmoid(a0 + _nn(xad, a2p, H3))
    p = xk * kk_w
    n = jnp.maximum(jnp.sqrt(_seg_sum(p * p)), L2_EPS)
    kk = p / n
    kp = xk * (1.0 + (ag - 1.0) * ka_w)
    return lw, kp, -kk, kk * ag


def _post_math(y, r, kp, v, z, lnw, lnb, rk):
    inv = 1.0 / RW_HD
    mu = _seg_sum(y) * inv
    d = y - mu
    var = _seg_sum(d * d) * inv
    yn = d * lax.rsqrt(var + GN_EPS) * lnw + lnb
    bonus = _seg_sum(r * kp * rk) * v
    return (yn + bonus) * (z * _sigmoid(z))


def _neumann(ms, depth):
    eye = (_iota2(ms[0].shape, 0) == _iota2(ms[0].shape, 1)).astype(F32)
    width = ms[0].shape[1]
    mp = [_nn(m, m, PS) for m in ms]
    inv = [eye + m for m in ms]
    n = 2
    while n < depth:
        last = 2 * n >= depth
        for i in range(len(ms)):
            if last:
                inv[i] = inv[i] + _nn(mp[i], inv[i], PS)
            else:
                z = _nn(mp[i], jnp.concatenate([mp[i], inv[i]], axis=1), PS)
                mp[i], inv[i] = z[:, :width], inv[i] + z[:, width:]
        n *= 2
    return inv


@functools.partial(jax.custom_vjp, nondiff_argnums=(1,))
def _unit_inverses(ms, depth):
    if depth <= RW_HD:
        return tuple(_neumann(list(ms), depth))
    h, n = depth // 2, len(ms)
    cat = jnp.concatenate
    z = jnp.zeros((h, h), F32)

    def heads(m, r, c):
        b0 = m[r * h:(r + 1) * h, c * h:(c + 1) * h]
        b1 = m[depth + r * h:depth + (r + 1) * h, depth + c * h:depth + (c + 1) * h]
        return cat([cat([b0, z], axis=1), cat([z, b1], axis=1)], axis=0)

    diag = _neumann([heads(m, 0, 0) for m in ms] + [heads(m, 1, 1) for m in ms], h)
    ta, td = diag[:n], diag[n:]
    low = [_nn(heads(m, 1, 0), a, PS) for m, a in zip(ms, ta)]
    low = [_nn(d, x, PS) for d, x in zip(td, low)]
    out = []
    for a, x, d in zip(ta, low, td):
        rows = []
        for hd in (0, 1):
            sl = slice(hd * h, (hd + 1) * h)
            top, bot = [a[sl, sl], z], [x[sl, sl], d[sl, sl]]
            pad = [z, z]
            rows.append(cat(top + pad if hd == 0 else pad + top, axis=1))
            rows.append(cat(bot + pad if hd == 0 else pad + bot, axis=1))
        out.append(cat(rows, axis=0))
    return tuple(out)


def _unit_inverses_fwd(ms, depth):
    inv = _unit_inverses(ms, depth)
    return inv, inv


def _unit_inverses_bwd(depth, inv, cts):
    left = [_tn(t, g, PS) for t, g in zip(inv, cts)]
    return (tuple(_nt(l, t, PS) for l, t in zip(left, inv)),)


_unit_inverses.defvjp(_unit_inverses_fwd, _unit_inverses_bwd)


@jax.custom_vjp
def _known_inverses(ms, inv):
    return inv


def _known_inverses_fwd(ms, inv):
    return inv, inv


def _known_inverses_bwd(inv, cts):
    dms = _unit_inverses_bwd(None, inv, cts)[0]
    return dms, tuple(jnp.zeros_like(t) for t in inv)


_known_inverses.defvjp(_known_inverses_fwd, _known_inverses_bwd)


def _scan_group(s0s, *flat, known_inv=None, with_inv=False):
    P = len(s0s)
    G = len(flat) // (6 * P)
    ch = [flat[6 * i:6 * i + 6] for i in range(P * G)]
    C = ch[0][0].shape[0]
    C2 = 2 * C
    cat = jnp.concatenate
    m0 = _iota2((1, LANE), 1) < RW_HD
    mask0 = m0.astype(F32)
    mask1 = 1.0 - mask0
    r2 = _iota2((C2, C2), 0)
    c2 = _iota2((C2, C2), 1)
    dist = r2 - c2
    in_head = dist <= r2 % C
    lower = (_iota2((C, C), 0) >= _iota2((C, C), 1)).astype(F32)
    bd = (_iota2((LANE, LANE), 0) // RW_HD) == (_iota2((LANE, LANE), 1) // RW_HD)

    def tri(m, strict):
        return jnp.where(dist > 0 if strict else dist >= 0, jnp.where(in_head, m, 0.0), 0.0)

    def sel(z):
        return jnp.where(m0, z[:C], z[C:])

    gs = [_nn(lower, c[1], HI) for c in ch]
    pre = []
    for (r, lw, k, v, a, b), g in zip(ch, gs):
        g_end = jnp.sum(lw, axis=0, keepdims=True)
        gm = g - jnp.sum(lw[:C // 2], axis=0, keepdims=True)
        en = jnp.exp(-gm)
        ec = jnp.exp(g_end - g)
        pre.append(dict(at=a * jnp.exp(g - lw), rt=r * jnp.exp(g), am=a * jnp.exp(gm - lw), rm=r * jnp.exp(gm),
                        bt=b * en, kt=k * en, bh=b * ec, kh=k * ec, dec=jnp.exp(g_end), v=v))
    grams = [_nt(cat([p["am"] * mask0, p["am"] * mask1, p["rm"] * mask0, p["rm"] * mask1], axis=0),
                 cat([p["bt"], p["bt"], p["kt"], p["kt"]], axis=0), PS) for p in pre]
    mab = tuple(tri(gm[:C2, :C2], True) for gm in grams)
    tinv = _unit_inverses(mab, C) if known_inv is None else _known_inverses(mab, known_inv)
    xv =[sel(_nn(tri(gm[:C2, C2:], True), cat([p["v"], p["v"]], axis=0), PS)) for gm, p in zip(grams, pre)]
    ys, s = [None] * (P * G), list(s0s)
    for i in range(G):
        for q in range(P):
            n = q * G + i
            p, gm = pre[n], grams[n]
            sx = _nt(cat([p["at"], p["rt"]], axis=0), s[q], PS)
            x = sx[:C] + xv[n]
            u = sel(_nn(tinv[n], cat([x, x], axis=0), PS))
            v = p["v"]
            ys[n] = sx[C:] + sel(_nn(cat([tri(gm[C2:, :C2], False), tri(gm[C2:, C2:], False)], axis=1),
                                     cat([u, u, v, v], axis=0), PS))
            s[q] = s[q] * p["dec"] + jnp.where(bd, _tn(cat([u, v], axis=0), cat([p["bh"], p["kh"]], axis=0), PS), 0.0)
    return (tuple(ys), tuple(s), tinv) if with_inv else (tuple(ys), tuple(s))


def _scan_fwd(xs, lw, kp, an, bb, T):
    C, G = _scan_shape(T)
    P = SCAN_PAIRS
    nc = T // (C * G)
    npair = 1024 // LANE

    def kern(r_ref, lw_ref, k_ref, v_ref, a_ref, b_ref, y_ref, st_ref, inv_ref, s_scr):
        n = pl.program_id(1)

        @pl.when(n == 0)
        def _():
            s_scr[...] = jnp.zeros(s_scr.shape, F32)

        st_ref[0] = s_scr[...]
        ins = (r_ref, lw_ref, k_ref, v_ref, a_ref, b_ref)
        ys, s1, inv = _scan_group(
            tuple(s_scr[q] for q in range(P)),
            *[ref[i * C:(i + 1) * C, q * LANE:(q + 1) * LANE] for q in range(P) for i in range(G) for ref in ins], with_inv=True)
        for q in range(P):
            for i in range(G):
                y_ref[i * C:(i + 1) * C, q * LANE:(q + 1) * LANE] = ys[q * G + i]
                inv_ref[0, 0, q * G + i] = inv[q * G + i]
            s_scr[q] = s1[q]

    def col(off):
        return pl.BlockSpec((C * G, P * LANE), lambda p, n: (n, off // P + p))

    return pl.pallas_call(
        kern, name="rwkv_scan_fwd",
        out_shape=[jax.ShapeDtypeStruct((T, 1024), F32), jax.ShapeDtypeStruct((nc, npair, LANE, LANE), F32),
                   jax.ShapeDtypeStruct((nc, npair // P, P * G, 2 * C, 2 * C), F32)],
        grid=(npair // P, nc),
        in_specs=[col(0), col(0), col(0), col(16), col(0), col(0)],
        out_specs=[col(0), pl.BlockSpec((1, P, LANE, LANE), lambda p, n: (n, p, 0, 0)),
                   pl.BlockSpec((1, 1, P * G, 2 * C, 2 * C), lambda p, n: (n, p, 0, 0, 0))],
        scratch_shapes=[pltpu.VMEM((P, LANE, LANE), F32)],
        compiler_params=_cp(("parallel", "arbitrary")),
    )(xs, lw, kp, xs, an, bb)


def _scan_bwd(xs, lw, kp, an, bb, states, invs, dy, T):
    C, G = _scan_shape(T)
    P = SCAN_PAIRS
    nc = T // (C * G)
    npair = 1024 // LANE

    def kern(r_ref, lw_ref, k_ref, v_ref, a_ref, b_ref, st_ref, inv_ref, dy_ref, dr_ref, dlw_ref, dk_ref, dv_ref, da_ref, db_ref,
             ds_scr):
        n = pl.program_id(1)

        @pl.when(n == 0)
        def _():
            ds_scr[...] = jnp.zeros(ds_scr.shape, F32)

        ins = (r_ref, lw_ref, k_ref, v_ref, a_ref, b_ref)
        units = [(q, i) for q in range(P) for i in range(G)]
        known = tuple(inv_ref[0, 0, q * G + i] for q, i in units)
        _, vjp = jax.vjp(functools.partial(_scan_group, known_inv=known), tuple(st_ref[0, q] for q in range(P)),
                         *[ref[i * C:(i + 1) * C, q * LANE:(q + 1) * LANE] for q, i in units for ref in ins])
        grads = vjp((tuple(dy_ref[i * C:(i + 1) * C, q * LANE:(q + 1) * LANE] for q, i in units),
                     tuple(ds_scr[q] for q in range(P))))
        for q in range(P):
            ds_scr[q] = grads[0][q]
        outs = (dr_ref, dlw_ref, dk_ref, dv_ref, da_ref, db_ref)
        for n_, (q, i) in enumerate(units):
            for t, ref in enumerate(outs):
                ref[i * C:(i + 1) * C, q * LANE:(q + 1) * LANE] = grads[1 + 6 * n_ + t]

    def col(off):
        return pl.BlockSpec((C * G, P * LANE), lambda p, n: (nc - 1 - n, off // P + p))

    return pl.pallas_call(
        kern, name="rwkv_scan_bwd",
        out_shape=[jax.ShapeDtypeStruct((T, 1024), F32)] * 6,
        grid=(npair // P, nc),
        in_specs=[col(0), col(0), col(0), col(16), col(0), col(0),
                  pl.BlockSpec((1, P, LANE, LANE), lambda p, n: (nc - 1 - n, p, 0, 0)),
                  pl.BlockSpec((1, 1, P * G, 2 * C, 2 * C), lambda p, n: (nc - 1 - n, p, 0, 0, 0)), col(0)],
        out_specs=[col(0)] * 6,
        scratch_shapes=[pltpu.VMEM((P, LANE, LANE), F32)],
        compiler_params=_cp(("parallel", "arbitrary")),
    )(xs, lw, kp, xs, an, bb, states, invs, dy)


def _gates_fwd(u, bias_pad, T, f_cb):
    nb = T // LANE

    def kern(f_ref, b_ref, c_ref):
        x = f_ref[...] + b_ref[...]
        lf = jnp.minimum(x, 0.0) - jnp.log(1.0 + jnp.exp(-jnp.abs(x)))
        lft = lf.T
        ut = (_iota2((LANE, LANE), 0) <= _iota2((LANE, LANE), 1)).astype(F32)
        carry = jnp.zeros((LANE, 1), F32)
        for blk in range(nb):
            seg = lft[:, blk * LANE:(blk + 1) * LANE]
            cs = _nn(seg, ut, HI) + carry
            c_ref[:, blk * LANE:(blk + 1) * LANE] = cs[:SUB, :]
            carry = carry + jnp.sum(seg, axis=1, keepdims=True)

    return pl.pallas_call(
        kern, name="fox_gates_fwd",
        out_shape=jax.ShapeDtypeStruct((SUB, T), F32),
        grid=(1,),
        in_specs=[pl.BlockSpec((T, LANE), lambda i: (0, f_cb)), pl.BlockSpec((1, LANE), lambda i: (0, 0))],
        out_specs=pl.BlockSpec((SUB, T), lambda i: (0, 0)),
        compiler_params=_cp(("arbitrary",)),
    )(u, bias_pad)


def _gates_bwd(dc, u, bias_pad, T, f_cb):
    nb = T // LANE

    def kern(dc_ref, f_ref, b_ref, dfl_ref, db_ref):
        dcv = jnp.concatenate([dc_ref[...], jnp.zeros((LANE - SUB, T), F32)], axis=0)
        lt = (_iota2((LANE, LANE), 0) >= _iota2((LANE, LANE), 1)).astype(F32)
        carry = jnp.zeros((LANE, 1), F32)
        pieces = [None] * nb
        for blk in range(nb - 1, -1, -1):
            seg = dcv[:, blk * LANE:(blk + 1) * LANE]
            pieces[blk] = _nn(seg, lt, HI) + carry
            carry = carry + jnp.sum(seg, axis=1, keepdims=True)
        dlf = (pieces[0] if nb == 1 else jnp.concatenate(pieces, axis=1)).T
        x = f_ref[...] + b_ref[...]
        dfl = dlf * _sigmoid(-x)
        dfl_ref[...] = dfl
        db_ref[...] = jnp.sum(dfl, axis=0, keepdims=True)

    return pl.pallas_call(
        kern, name="fox_gates_bwd",
        out_shape=[jax.ShapeDtypeStruct((T, LANE), F32), jax.ShapeDtypeStruct((1, LANE), F32)],
        grid=(1,),
        in_specs=[pl.BlockSpec((SUB, T), lambda i: (0, 0)), pl.BlockSpec((T, LANE), lambda i: (0, f_cb)),
                  pl.BlockSpec((1, LANE), lambda i: (0, 0))],
        out_specs=[pl.BlockSpec((T, LANE), lambda i: (0, 0)), pl.BlockSpec((1, LANE), lambda i: (0, 0))],
        compiler_params=_cp(("arbitrary",)),
    )(dc, u, bias_pad)


ATTN_HEADS = 4


def _attn_block(T):
    return 256 if T % 256 == 0 and T >= 512 else 128


def _attn_fwd(u, c3, T):
    H, HP = 8, ATTN_HEADS
    bq = _attn_block(T)
    nq = T // bq
    scale = FOX_HD ** -0.5
    lanes = [slice(h * LANE, (h + 1) * LANE) for h in range(HP)]

    def kern(q_ref, k_ref, v_ref, z_ref, cq_ref, ck_ref, o_ref, oa_ref, lse_ref):
        i = pl.program_id(1)
        q = [(q_ref[:, ln] * scale).astype(BF16) for ln in lanes]
        c0 = [cq_ref[h][:, 0:1] for h in range(HP)]

        def step(j, carry, diagonal=False):
            off = pl.multiple_of(j * bq, bq)
            s = [_nt(q[h], k_ref[pl.ds(off, bq), lanes[h]].astype(BF16)) + (c0[h] - ck_ref[h, :, pl.ds(off, bq)])
                 for h in range(HP)]
            ps, out = [], []
            for h in range(HP):
                m, l, acc = carry[h]
                sh = s[h]
                if diagonal:
                    sh = jnp.where(_iota2((bq, bq), 1) <= _iota2((bq, bq), 0), sh, NEG)
                m_new = jnp.maximum(m, jnp.max(sh, axis=1, keepdims=True))
                p = jnp.exp(sh - m_new)
                alpha = jnp.exp(m - m_new)
                p_hi = p.astype(BF16)
                ps.append((p_hi, (p - p_hi.astype(F32)).astype(BF16)))
                out.append((m_new, alpha * l + jnp.sum(p, axis=1, keepdims=True), alpha * acc))
            res = []
            for h, (m, l, acc) in enumerate(out):
                vj = v_ref[pl.ds(off, bq), lanes[h]].astype(BF16)
                res.append((m, l, acc + _nn(ps[h][0], vj) + _nn(ps[h][1], vj)))
            return tuple(res)

        init = tuple((jnp.full((bq, 1), NEG, F32), jnp.zeros((bq, 1), F32), jnp.zeros((bq, FOX_HD), F32)) for _ in range(HP))
        res = step(i, lax.fori_loop(0, i, step, init), diagonal=True)
        for h, (m, l, acc) in enumerate(res):
            o = acc / l
            z = z_ref[:, lanes[h]]
            o_ref[:, lanes[h]] = o
            oa_ref[:, lanes[h]] = (o * z * _sigmoid(z)).astype(BF16)
            lse_ref[h] = m + jnp.log(l)

    W = HP * LANE
    return pl.pallas_call(
        kern, name="fox_attn_fwd",
        out_shape=[jax.ShapeDtypeStruct((T, 1024), F32), jax.ShapeDtypeStruct((T, 1024), BF16),
                   jax.ShapeDtypeStruct((H, T, 1), F32)],
        grid=(H // HP, nq),
        in_specs=[pl.BlockSpec((bq, W), lambda g, i: (i, g)),
                  pl.BlockSpec((T, W), lambda g, i: (0, 8 // HP + g)),
                  pl.BlockSpec((T, W), lambda g, i: (0, 16 // HP + g)),
                  pl.BlockSpec((bq, W), lambda g, i: (i, 24 // HP + g)),
                  pl.BlockSpec((HP, 1, bq), lambda g, i: (g, 0, i)),
                  pl.BlockSpec((HP, 1, T), lambda g, i: (g, 0, 0))],
        out_specs=[pl.BlockSpec((bq, W), lambda g, i: (i, g)),
                   pl.BlockSpec((bq, W), lambda g, i: (i, g)),
                   pl.BlockSpec((HP, bq, 1), lambda g, i: (g, i, 0))],
        compiler_params=_cp(("parallel", "arbitrary")),
    )(u, u, u, u, c3, c3)


def _attn_probs(s, lse_i, diagonal):
    if not diagonal:
        return jnp.exp(s - lse_i)
    keep = _iota2(s.shape, 1) <= _iota2(s.shape, 0)
    return jnp.where(keep, jnp.exp(jnp.where(keep, s, NEG) - lse_i), 0.0)


def _attn_pre_math(doa, z, o):
    sg = _sigmoid(z)
    do = (doa * z * sg).astype(BF16)
    dz = doa * o * (sg * (1.0 + z * (1.0 - sg)))
    head_of = (_iota2((o.shape[1], LANE), 0) // FOX_HD == _iota2((o.shape[1], LANE), 1)).astype(F32)
    return do, dz, _nn(do.astype(F32) * o, head_of, HI)


def _attn_bwd(u, c3, lse, do, delta, T):
    H, HP = 8, ATTN_HEADS
    bq = _attn_block(T)
    nq = T // bq
    scale = FOX_HD ** -0.5
    lanes = [slice(h * LANE, (h + 1) * LANE) for h in range(HP)]

    def kern(q_ref, k_ref, v_ref, c_ref, lse_ref, do_ref, dl_ref, dq_ref, dk_ref, dv_ref, dc_ref):
        j = pl.program_id(1)

        @pl.when(j == 0)
        def _():
            dq_ref[...] = jnp.zeros(dq_ref.shape, F32)

        kj = [k_ref[:, ln].astype(BF16) for ln in lanes]
        vj = [v_ref[:, ln].astype(BF16) for ln in lanes]
        joff = pl.multiple_of(j * bq, bq)
        ck = [c_ref[h, :, pl.ds(joff, bq)] for h in range(HP)]

        def step(i, carry, diagonal=False):
            off = pl.multiple_of(i * bq, bq)
            qs = [(q_ref[pl.ds(off, bq), ln] * scale).astype(BF16) for ln in lanes]
            dob = [do_ref[pl.ds(off, bq), ln] for ln in lanes]
            s = [_nt(qs[h], kj[h]) + (c_ref[h, :, pl.ds(off, bq)][:, 0:1] - ck[h]) for h in range(HP)]
            dp = [_nt(dob[h], vj[h]) for h in range(HP)]
            pb, dsb, dcs = [], [], []
            for h in range(HP):
                p = _attn_probs(s[h], lse_ref[h, pl.ds(off, bq), :], diagonal)
                ds = p * (dp[h] - dl_ref[h, pl.ds(off, bq), :])
                pb.append(p.astype(BF16))
                dsb.append(ds.astype(BF16))
                dcs.append(jnp.sum(ds, axis=0, keepdims=True))
            out = []
            for h, (dk, dv, dc) in enumerate(carry):
                dq_ref[pl.ds(off, bq), lanes[h]] += _nn(dsb[h], kj[h]) * scale
                out.append((dk + _tn(dsb[h], qs[h]), dv + _tn(pb[h], dob[h]), dc - dcs[h]))
            return tuple(out)

        init = tuple((jnp.zeros((bq, FOX_HD), F32), jnp.zeros((bq, FOX_HD), F32), jnp.zeros((1, bq), F32)) for _ in range(HP))
        res = lax.fori_loop(j + 1, nq, step, step(j, init, diagonal=True))
        for h, (dk, dv, dc) in enumerate(res):
            dk_ref[:, lanes[h]] = dk
            dv_ref[:, lanes[h]] = dv
            dc_ref[h] = dc

    W = HP * LANE
    full = lambda cb: pl.BlockSpec((T, W), lambda g, j: (0, cb // HP + g))
    blk = lambda cb: pl.BlockSpec((bq, W), lambda g, j: (j, cb // HP + g))
    col1 = pl.BlockSpec((HP, T, 1), lambda g, j: (g, 0, 0))
    return pl.pallas_call(
        kern, name="fox_attn_bwd",
        out_shape=[jax.ShapeDtypeStruct((T, 1024), F32)] * 3 + [jax.ShapeDtypeStruct((H, 1, T), F32)],
        grid=(H // HP, nq),
        in_specs=[full(0), blk(8), blk(16), pl.BlockSpec((HP, 1, T), lambda g, j: (g, 0, 0)), col1, full(0), col1],
        out_specs=[full(0), blk(0), blk(0), pl.BlockSpec((HP, 1, bq), lambda g, j: (g, 0, j))],
        compiler_params=_cp(("parallel", "arbitrary")),
    )(u, u, u, c3, lse, do, delta)


def _place():
    return lax.axis_index("x"), lax.axis_index("y"), lax.axis_index("c")


def _slot(p):
    return 4 * p[0] + 2 * p[1] + p[2]


def _other_chips(x, y):
    return [(1 - x, y), (x, 1 - y), (1 - x, 1 - y)]


def _allgather_steps(in_refs, out_refs, scratch):
    (src,), (dst,) = in_refs, out_refs
    send_sems, recv_sems, local_sem = scratch
    x, y, c = _place()
    me, sibling = (x, y, c), (x, y, 1 - c)
    chips = _other_chips(x, y)

    def copy(k, block, to, from_input=False):
        d = dst.at[_slot(block)]
        return pltpu.make_async_remote_copy(
            src_ref=src if from_input else d, dst_ref=d, send_sem=send_sems.at[k], recv_sem=recv_sems.at[k],
            device_id=to, device_id_type=MESH)

    def first_copies():
        return [copy(0, me, sibling, True)] + [copy(1 + j, me, (*chip, c), True) for j, chip in enumerate(chips)]

    def start():
        pltpu.make_async_copy(src, dst.at[_slot(me)], local_sem).start()
        for cp in first_copies():
            cp.start()

    def finish():
        passed = []
        for j, chip in enumerate(chips):
            copy(1 + j, (*chip, c), me).wait_recv()
            passed.append(copy(4 + j, (*chip, c), sibling))
            passed[-1].start()
        copy(0, sibling, me).wait_recv()
        for j, chip in enumerate(chips):
            copy(4 + j, (*chip, 1 - c), me).wait_recv()
        for cp in first_copies() + passed:
            cp.wait_send()
        pltpu.make_async_copy(src, dst.at[_slot(me)], local_sem).wait()

    return start, finish


def _allgather_relay_steps(in_refs, out_refs, scratch):
    (src,), (dst,) = in_refs, out_refs
    send_sems, recv_sems, local_sem = scratch
    x, y, c = _place()
    me, sibling = (x, y, c), (x, y, 1 - c)
    x_nbr, y_nbr, diag = (1 - x, y, c), (x, 1 - y, c), (1 - x, 1 - y, c)
    flip = lambda a, bit: a + bit - 2 * a * bit
    relay_from = (flip(x, 1 - c), flip(y, c), c)
    relay_to = (flip(x, c), flip(y, 1 - c), c)

    def copy(k, block, to, from_input=False):
        d = dst.at[_slot(block)]
        return pltpu.make_async_remote_copy(
            src_ref=src if from_input else d, dst_ref=d, send_sem=send_sems.at[k], recv_sem=recv_sems.at[k],
            device_id=to, device_id_type=MESH)

    def first_copies():
        return [copy(0, me, sibling, True), copy(1, me, x_nbr, True), copy(2, me, y_nbr, True)]

    def other(block):
        return block[:2] + (1 - c,)

    def start():
        pltpu.make_async_copy(src, dst.at[_slot(me)], local_sem).start()
        for cp in first_copies():
            cp.start()

    def finish():
        copy(1, x_nbr, me).wait_recv()
        copy(2, y_nbr, me).wait_recv()
        later = [copy(3, relay_from, relay_to), copy(4, x_nbr, sibling), copy(5, y_nbr, sibling)]
        for cp in later:
            cp.start()
        copy(3, diag, me).wait_recv()
        later.append(copy(6, diag, sibling))
        later[-1].start()
        copy(0, sibling, me).wait_recv()
        for k, block in ((4, x_nbr), (5, y_nbr), (6, diag)):
            copy(k, other(block), me).wait_recv()
        for cp in first_copies() + later:
            cp.wait_send()
        pltpu.make_async_copy(src, dst.at[_slot(me)], local_sem).wait()

    return start, finish


def _allgather_side(a, relay=False):
    return ((a,), (jax.ShapeDtypeStruct((N_DEV,) + a.shape, a.dtype),),
            (pltpu.SemaphoreType.DMA((7,)), pltpu.SemaphoreType.DMA((7,)), pltpu.SemaphoreType.DMA),
            _allgather_relay_steps if relay else _allgather_steps)


def _allgather(a, name, relay=False):
    ins, outs, scratch, make = _allgather_side(a, relay)

    def body(a_ref, o_ref, *scr):
        start, finish = make((a_ref,), (o_ref,), scr)
        start()
        finish()

    any_spec = pl.BlockSpec(memory_space=pl.ANY)
    return pl.pallas_call(body, name=name, out_shape=outs[0], in_specs=[any_spec], out_specs=any_spec,
                          scratch_shapes=list(scratch))(a)


def _exchange_pair_steps(in_refs, out_refs, scratch):
    (src,), (dst,) = in_refs, out_refs
    send_sems, recv_sems = scratch
    x, y, c = _place()
    sibling = (x, y, 1 - c)
    slots = [_slot(sibling)] + [_slot((*chip, 1 - c)) for chip in _other_chips(x, y)]

    def copies():
        return [pltpu.make_async_remote_copy(
            src_ref=src.at[ps], dst_ref=dst.at[k], send_sem=send_sems.at[k], recv_sem=recv_sems.at[k],
            device_id=sibling, device_id_type=MESH) for k, ps in enumerate(slots)]

    def start():
        for cp in copies():
            cp.start()

    def finish():
        for cp in copies():
            cp.wait()

    return start, finish


def _exchange_pair_side(g):
    return ((g,), (jax.ShapeDtypeStruct((4,) + g.shape[1:], g.dtype),),
            (pltpu.SemaphoreType.DMA((4,)), pltpu.SemaphoreType.DMA((4,))), _exchange_pair_steps)


def _exchange_pair(g):
    ins, outs, scratch, make = _exchange_pair_side(g)

    def body(g_ref, r_ref, *scr):
        start, finish = make((g_ref,), (r_ref,), scr)
        start()
        finish()

    any_spec = pl.BlockSpec(memory_space=pl.ANY)
    return pl.pallas_call(body, name="exchange_pair", out_shape=outs[0], in_specs=[any_spec], out_specs=any_spec,
                          scratch_shapes=list(scratch))(g)


def _tiling(R, Cc, tile, by_cols):
    if by_cols:
        assert Cc % tile == 0
        return Cc // tile, (R, tile), lambda lead, i: (lead, 0, i)
    assert R % tile == 0
    return R // tile, (tile, Cc), lambda lead, i: (lead, i, 0)


def _pair_add(name, g, r1, slots, tile, by_cols=False, side=None):
    _, R, Cc = g.shape
    steps, blk, at = _tiling(R, Cc, tile, by_cols)
    side_ins, side_outs, side_scr, side_make = side if side is not None else ((), (), (), None)
    n_si, n_so = len(side_ins), len(side_outs)

    def kern(s_ref, a_ref, b_ref, *rest):
        o_ref = rest[n_si]
        if side_make is not None:
            start, finish = side_make(rest[:n_si], rest[n_si + 1:n_si + 1 + n_so], rest[n_si + 1 + n_so:])
            j, i = pl.program_id(0), pl.program_id(1)
            pl.when(jnp.logical_and(j == 0, i == 0))(start)
        o_ref[...] = (a_ref[...].astype(F32) + b_ref[...].astype(F32)).astype(o_ref.dtype)
        if side_make is not None:
            pl.when(jnp.logical_and(j == 2, i == steps - 1))(finish)

    any_spec = pl.BlockSpec(memory_space=pl.ANY)
    res = pl.pallas_call(
        kern, name=name,
        out_shape=[jax.ShapeDtypeStruct((3, R, Cc), BF16)] + list(side_outs),
        grid_spec=pltpu.PrefetchScalarGridSpec(
            num_scalar_prefetch=1, grid=(3, steps),
            in_specs=[pl.BlockSpec((1,) + blk, lambda j, i, s: at(s[j], i)),
                      pl.BlockSpec((1,) + blk, lambda j, i, s: at(1 + j, i))] + [any_spec] * n_si,
            out_specs=[pl.BlockSpec((1,) + blk, lambda j, i, s: at(j, i))] + [any_spec] * n_so,
            scratch_shapes=list(side_scr)),
        compiler_params=_cp(("arbitrary", "arbitrary")),
    )(slots, g, r1, *side_ins)
    return res if side is not None else res[0]


def _axis_neighbours():
    x, y, c = _place()
    flip = lambda a, bit: a + bit - 2 * a * bit
    return (flip(x, c), flip(y, 1 - c), c), (flip(x, 1 - c), flip(y, c), c), c


def _exchange_first_steps(in_refs, out_refs, scratch):
    pairs = list(zip(in_refs, out_refs))
    send_sems, recv_sems = scratch
    first, _, c = _axis_neighbours()

    def copies():
        return [pltpu.make_async_remote_copy(
            src_ref=src.at[j], dst_ref=dst.at[k], send_sem=send_sems.at[t, k], recv_sem=recv_sems.at[t, k],
            device_id=first, device_id_type=MESH)
            for t, (src, dst) in enumerate(pairs) for k, j in enumerate((1 - c, 2))]

    def start():
        for cp in copies():
            cp.start()

    def finish():
        for cp in copies():
            cp.wait()

    return start, finish


def _exchange_first_side(*ss):
    n = len(ss)
    return (ss, tuple(jax.ShapeDtypeStruct((2,) + s.shape[1:], s.dtype) for s in ss),
            (pltpu.SemaphoreType.DMA((n, 2)), pltpu.SemaphoreType.DMA((n, 2))), _exchange_first_steps)


def _axis_add(name, s, t1, core, tile, by_cols=False):
    _, R, Cc = s.shape
    steps, blk, at = _tiling(R, Cc, tile, by_cols)

    def kern(c_ref, a_ref, b_ref, o_ref):
        o_ref[...] = (a_ref[...].astype(F32) + b_ref[...].astype(F32)).astype(o_ref.dtype)

    return pl.pallas_call(
        kern, name=name,
        out_shape=jax.ShapeDtypeStruct((1, R, Cc), BF16),
        grid_spec=pltpu.PrefetchScalarGridSpec(
            num_scalar_prefetch=1, grid=(steps,),
            in_specs=[pl.BlockSpec((1,) + blk, lambda i, cr: at(cr[0], i)),
                      pl.BlockSpec((1,) + blk, lambda i, cr: at(1, i))],
            out_specs=pl.BlockSpec((1,) + blk, lambda i, cr: at(0, i))),
        compiler_params=_cp(("arbitrary",)),
    )(core, s, t1)


def _exchange_second_steps(in_refs, out_refs, scratch):
    send_sems, recv_sems = scratch
    _, second, _ = _axis_neighbours()

    def copies():
        return [pltpu.make_async_remote_copy(src_ref=src, dst_ref=dst, send_sem=send_sems.at[t], recv_sem=recv_sems.at[t],
                                             device_id=second, device_id_type=MESH)
                for t, (src, dst) in enumerate(zip(in_refs, out_refs))]

    def start():
        for cp in copies():
            cp.start()

    def finish():
        for cp in copies():
            cp.wait()

    return start, finish


def _exchange_second_side(*ps):
    n = len(ps)
    return (ps, tuple(jax.ShapeDtypeStruct(p.shape, p.dtype) for p in ps),
            (pltpu.SemaphoreType.DMA((n,)), pltpu.SemaphoreType.DMA((n,))), _exchange_second_steps)


def _adamw(name, w, m, v, parts, tile, by_cols=False):
    R, Cc = w.shape
    steps, blk_shape, at = _tiling(R, Cc, tile, by_cols)
    n_parts = len(parts)

    def kern(*refs):
        w_ref, m_ref, v_ref = refs[:3]
        g = None
        for r_ref, (_, n) in zip(refs[3:3 + n_parts], parts):
            for s in range(n):
                term = r_ref[s].astype(F32)
                g = term if g is None else g + term
        g_out, d_out, m_out, v_out = refs[3 + n_parts:]
        mn = ADAM_B1 * m_ref[...] + (1.0 - ADAM_B1) * g
        vn = ADAM_B2 * v_ref[...] + (1.0 - ADAM_B2) * (g * g)
        m_hat = mn / (1.0 - ADAM_B1 ** ADAM_STEP)
        v_hat = vn / (1.0 - ADAM_B2 ** ADAM_STEP)
        g_out[...] = g
        d_out[...] = -ADAM_LR * (m_hat / (jnp.sqrt(v_hat) + ADAM_EPS) + ADAM_WD * w_ref[...])
        m_out[...] = mn
        v_out[...] = vn

    blk = pl.BlockSpec(blk_shape, lambda i: at(0, i)[1:])
    return pl.pallas_call(
        kern, name=name,
        out_shape=[jax.ShapeDtypeStruct((R, Cc), F32)] * 4,
        grid=(steps,),
        in_specs=[blk] * 3 + [pl.BlockSpec((n,) + blk_shape, lambda i: at(0, i)) for (_, n) in parts],
        out_specs=[blk] * 4,
        compiler_params=_cp(("arbitrary",)),
    )(w, m, v, *[a for (a, _) in parts])


def _assemble_columns(blocks, pieces, zeros, width):
    _, R, Cc = blocks.shape
    tr = min(256, R)

    def kern(b_ref, o_ref):
        for col, n in zeros:
            o_ref[:, col:col + n] = jnp.zeros((tr, n), o_ref.dtype)
        for col, d, lo, n in pieces:
            o_ref[:, col:col + n] = b_ref[d, :, lo:lo + n]

    return pl.pallas_call(
        kern, name="assemble_w_in",
        out_shape=jax.ShapeDtypeStruct((R, width), blocks.dtype),
        grid=(R // tr,),
        in_specs=[pl.BlockSpec((N_DEV, tr, Cc), lambda i: (0, i, 0))],
        out_specs=pl.BlockSpec((tr, width), lambda i: (i, 0)),
        compiler_params=_cp(("parallel",)),
    )(blocks)


def _split_rows(x, pieces, rows, side):
    _, Cc = x.shape
    tc = min(256, Cc)
    side_ins, side_outs, side_scr, side_make = side
    n_si, n_so = len(side_ins), len(side_outs)

    def kern(x_ref, *rest):
        o_ref = rest[n_si]
        start, finish = side_make(rest[:n_si], rest[n_si + 1:n_si + 1 + n_so], rest[n_si + 1 + n_so:])
        pl.when(pl.program_id(0) == 0)(start)
        for d, lo, row, n in pieces:
            o_ref[d, lo:lo + n, :] = x_ref[row:row + n, :]
        pl.when(pl.program_id(0) == Cc // tc - 1)(finish)

    any_spec = pl.BlockSpec(memory_space=pl.ANY)
    return pl.pallas_call(
        kern, name="split_w_in_grad",
        out_shape=[jax.ShapeDtypeStruct((N_DEV, rows, Cc), x.dtype)] + list(side_outs),
        grid=(Cc // tc,),
        in_specs=[pl.BlockSpec((x.shape[0], tc), lambda i: (0, i))] + [any_spec] * n_si,
        out_specs=[pl.BlockSpec((N_DEV, rows, tc), lambda i: (0, 0, i))] + [any_spec] * n_so,
        scratch_shapes=list(side_scr),
        compiler_params=_cp(("arbitrary",)),
    )(x, *side_ins)


def _pad_cols(a, w):
    return jnp.pad(a, ((0, 0), (0, w - a.shape[1])))


def _pad_rows(a, r):
    return jnp.pad(a, ((0, r - a.shape[0]), (0, 0)))


def _pack_b(pf, pr, wo, w2, a2, rows):
    body = jnp.concatenate([pf, pr, wo.reshape(2048, 256), jnp.concatenate([w2, a2], axis=1)], axis=0)
    return _pad_rows(body, rows)


def kernel(x, norm_gain, w_in, fox_forget_bias, rwkv_shift_mix, rwkv_w0, rwkv_w2, rwkv_a0, rwkv_a2, rwkv_k_k, rwkv_k_a, rwkv_r_k, rwkv_ln_w, rwkv_ln_b, w_proj_fox, w_proj_rwkv, w_out, final_norm_gain, loss_target, m_norm_gain, m_w_in, m_fox_forget_bias, m_rwkv_shift_mix, m_rwkv_w0, m_rwkv_w2, m_rwkv_a0, m_rwkv_a2, m_rwkv_k_k, m_rwkv_k_a, m_rwkv_r_k, m_rwkv_ln_w, m_rwkv_ln_b, m_w_proj_fox, m_w_proj_rwkv, m_w_out, m_final_norm_gain, v_norm_gain, v_w_in, v_fox_forget_bias, v_rwkv_shift_mix, v_rwkv_w0, v_rwkv_w2, v_rwkv_a0, v_rwkv_a2, v_rwkv_k_k, v_rwkv_k_a, v_rwkv_r_k, v_rwkv_ln_w, v_rwkv_ln_b, v_w_proj_fox, v_w_proj_rwkv, v_w_out, v_final_norm_gain):
    T, D = x.shape[1], x.shape[2]
    assert D == 2048 and T % LANE == 0
    NI = w_in.shape[2]
    IN = N_DEV * NI
    RB = 4224
    x2 = x[0]
    lt2 = loss_target[0]
    me = _slot(_place())

    tb = min(256, T)
    tbh = min(128, T)
    h, wa = _rowcall("rms_fwd", T, tb, [_rows(x2, tb, D), _whole(norm_gain)],
                     lambda i, xv, g: ([_rms_math(xv, g)], []), [(D, BF16)],
                     side=_allgather_side(w_in[0].astype(BF16), relay=True))
    packed_own = _pack_b(w_proj_fox[0], w_proj_rwkv[0], w_out[0], rwkv_w2[0], rwkv_a2[0], RB).astype(BF16)
    sections = [(0, 4096, 0), (4104, 4096, 4096), (8392, 4096, 8192), (4096, 8, 12288), (8200, 96, 12544), (8296, 96, 12672)]
    NP = 12800
    pieces, zeros, at_col = [], [], 0
    for lo, width, pad_lo in sections:
        if pad_lo > at_col:
            zeros.append((at_col, pad_lo - at_col))
        col = lo
        while col < lo + width:
            d = col // NI
            stop = min(lo + width, (d + 1) * NI)
            pieces.append((pad_lo + col - lo, d, col - d * NI, stop - col))
            col = stop
        at_col = pad_lo + width
    zeros.append((at_col, NP - at_col))
    w_pad = _assemble_columns(wa, pieces, zeros, NP)
    F_CB, LORA_CB = 96, 49

    mu = rwkv_shift_mix
    mu_main = mu[:, 0:4096]
    mu_lora = jnp.concatenate([_pad_cols(mu[:, 4096:4192], LANE), _pad_cols(mu[:, 4192:4288], LANE)], axis=1)
    bias_pad = _pad_cols(fox_forget_bias, LANE)
    rk_flat = rwkv_r_k.reshape(1, 1024)
    gf = final_norm_gain.reshape(1, D)

    u, wb = _mm(h, w_pad, tm=1024, tn=1280, name="mm_in", side=_allgather_side(packed_own))
    wpf = wb[:, 0:1024, :].transpose(1, 0, 2).reshape(1024, D)
    wpr = wb[:, 1024:2048, :].transpose(1, 0, 2).reshape(1024, D)
    wo = wb[:, 2048:4096, :].reshape(N_DEV * 256, D)
    w2p = _pad_rows(wb[:, 4096:4192, 0:128].transpose(1, 0, 2).reshape(96, 1024).astype(F32), LANE)
    a2p = _pad_rows(wb[:, 4096:4192, 128:256].transpose(1, 0, 2).reshape(96, 1024).astype(F32), LANE)

    c8 = _gates_fwd(u, bias_pad, T, F_CB)
    c3 = c8.reshape(8, 1, T)
    o_raw, o_a, lse = _attn_fwd(u, c3, T)

    def shift_body(i, um, hm, ul, hl, mm_, ml):
        outs = []
        for uv, hv, mv in ((um, hm, mm_), (ul, hl, ml)):
            hv = jnp.where(i == 0, 0.0, hv)
            prev = pltpu.roll(jnp.concatenate([hv, uv], axis=0), 1, 0)[SUB:]
            outs.append(uv + (prev - uv) * mv)
        return outs, []

    def halo_prev(arr, w, cb):
        return (arr, (SUB, w), lambda i: (jnp.maximum(i * (tbh // SUB) - 1, 0), cb))

    xs, xl = _rowcall("rwkv_shift_fwd", T, tbh,
                      [_rows(u, tbh, 4096, 1), halo_prev(u, 4096, 1), _rows(u, tbh, 256, LORA_CB), halo_prev(u, 256, LORA_CB),
                       _whole(mu_main), _whole(mu_lora)],
                      shift_body, [(4096, F32), (256, F32)])

    prep_par = [_whole(rwkv_w0), _whole(rwkv_a0), _whole(rwkv_k_k), _whole(rwkv_k_a), _whole(w2p), _whole(a2p)]
    prep_rows = [_rows(xs, tbh, 1024, 1), _rows(xl, tbh, LANE, 0), _rows(xl, tbh, LANE, 1)]
    lw, kp, an, bb = _rowcall("rwkv_prep_fwd", T, tbh, prep_rows + prep_par,
                              lambda i, *a: (list(_prep_math(*a)), []), [(1024, F32)] * 4)
    y, states, invs = _scan_fwd(xs, lw, kp, an, bb, T)
    post_rows = [_rows(y, tbh, 1024), _rows(xs, tbh, 1024, 0), _rows(kp, tbh, 1024), _rows(xs, tbh, 1024, 2), _rows(xs, tbh, 1024, 3)]
    post_par = [_whole(rwkv_ln_w), _whole(rwkv_ln_b), _whole(rk_flat)]
    (o_b,) = _rowcall("rwkv_post_fwd", T, tbh, post_rows + post_par,
                      lambda i, *a: ([_post_math(*a)], []), [(1024, BF16)])

    pa = _mm(o_a, wpf, name="mm_proj_fox")
    pb = _mm(o_b, wpr, name="mm_proj_rwkv")
    merge_rows = [_rows(u, tb, D, 4), _rows(u, tb, D, 5), _rows(pa, tb, D), _rows(pb, tb, D)]
    (mg,) = _rowcall("merge_fwd", T, tb, merge_rows, lambda i, *a: ([_merge_math(*a)], []), [(D, BF16)])
    mo = _mm(mg, wo, name="mm_out")

    def head_body(i, xv, mov, ltv, g):
        out = xv + mov
        r = lax.rsqrt(jnp.mean(out * out, axis=-1, keepdims=True) + RMS_EPS)
        yn = out * r
        err = yn * g - ltv
        loss = 0.5 * jnp.sum(jnp.sum(err * err, axis=-1, keepdims=True), axis=0, keepdims=True) / D
        dyv = err / D
        dyn = dyv * g
        dout = r * (dyn - yn * jnp.mean(dyn * yn, axis=-1, keepdims=True))
        return [dout], [loss, jnp.sum(dyv * yn, axis=0, keepdims=True)]

    dout, loss_p, dgf_p = _rowcall("loss_head", T, tb, [_rows(x2, tb, D), _rows(mo, tb, D), _rows(lt2, tb, D), _whole(gf)],
                                   head_body, [(D, F32)], [(1, 1), (1, D)])

    dm = _mm(dout, wo, tb=True, name="mm_out_dx")
    dwo = _mm(mg, dout, ta=True, out_dtype=BF16, name="mm_out_dw")

    def merge_bwd_body(i, ga, gb, pav, pbv, dmv):
        _, vjp = jax.vjp(_merge_math, ga, gb, pav, pbv)
        dga, dgb, dpa, dpb = vjp(dmv)
        return [dga, dgb, dpa, dpb], []

    dga, dgb, dpa, dpb = _rowcall("merge_bwd", T, tb, merge_rows + [_rows(dm, tb, D)], merge_bwd_body,
                                  [(D, BF16), (D, BF16), (D, BF16), (D, BF16)])
    doa = _mm(dpa, wpf, tb=True, name="mm_proj_fox_dx")
    dwpf = _mm(o_a, dpa, ta=True, out_dtype=BF16, name="mm_proj_fox_dw")
    dob = _mm(dpb, wpr, tb=True, name="mm_proj_rwkv_dx")
    dwpr = _mm(o_b, dpb, ta=True, out_dtype=BF16, name="mm_proj_rwkv_dw")

    do_b, dza, delta128 = _rowcall("fox_attn_pre", T, tb, [_rows(doa, tb, 1024), _rows(u, tb, 1024, 3), _rows(o_raw, tb, 1024)],
                                   lambda i, *a: (list(_attn_pre_math(*a)), []), [(1024, BF16), (1024, F32), (LANE, F32)])
    delta = delta128[:, 0:8].T.reshape(8, T, 1)
    dq, dk, dv, dc3 = _attn_bwd(u, c3, lse, do_b, delta, T)
    dfl, dbias_p = _gates_bwd(dc3.reshape(8, T), u, bias_pad, T, F_CB)

    def post_bwd_body(i, yv, rv, kpv, vv, zv, lnw, lnb, rkv, dobv):
        _, vjp = jax.vjp(_post_math, yv, rv, kpv, vv, zv, lnw, lnb, rkv)
        dy_, dr_, dkp_, dv_, dz_, dlnw, dlnb, drk = vjp(dobv)
        return [dy_, dr_, dkp_, dv_, dz_], [dlnw, dlnb, drk]

    dy_s, dr_p, dkp_p, dv_p, dzb, dlnw_p, dlnb_p, drk_p = _rowcall(
        "rwkv_post_bwd", T, tbh, post_rows + post_par + [_rows(dob, tbh, 1024)], post_bwd_body,
        [(1024, F32)] * 5, [(1, 1024)] * 3)
    dr_s, dlw, dkp_s, dv_s, dan, dbb = _scan_bwd(xs, lw, kp, an, bb, states, invs, dy_s, T)

    def prep_bwd_body(i, xk, xwd, xad, w0, a0, kkw, kaw, w2v, a2v, dlw_, dkp1, dkp2, dan_, dbb_, dr1, dr2, dv1, dv2, dz_):
        _, vjp = jax.vjp(_prep_math, xk, xwd, xad, w0, a0, kkw, kaw, w2v, a2v)
        dxk, dxwd, dxad, dw0, da0, dkk, dka, dw2, da2 = vjp((dlw_, dkp1 + dkp2, dan_, dbb_))
        return [[dr1 + dr2, dxk, dv1 + dv2, dz_], [dxwd, dxad]], [dw0, da0, dkk, dka, dw2, da2]

    cots = [dlw, dkp_s, dkp_p, dan, dbb, dr_s, dr_p, dv_s, dv_p, dzb]
    dxs, dxl, dw0_p, da0_p, dkk_p, dka_p, dw2_p, da2_p = _rowcall(
        "rwkv_prep_bwd", T, tbh, prep_rows + prep_par + [_rows(c_, tbh, 1024) for c_ in cots], prep_bwd_body,
        [(4096, F32), (256, F32)], [(1, 1024)] * 4 + [(LANE, 1024)] * 2)

    def shift_bwd_body(i, dm_, hm, dl_, hl, um, pm, ul, pl_, mm_, ml):
        last = i == T // tbh - 1
        outs, accs = [], []
        for dv_, hv, uv, pv, mv in ((dm_, hm, um, pm, mm_), (dl_, hl, ul, pl_, ml)):
            hv = jnp.where(last, 0.0, hv)
            nxt = pltpu.roll(jnp.concatenate([dv_, hv], axis=0), tbh + SUB - 1, 0)[:tbh]
            pv = jnp.where(i == 0, 0.0, pv)
            prev = pltpu.roll(jnp.concatenate([pv, uv], axis=0), 1, 0)[SUB:]
            outs.append(dv_ * (1.0 - mv) + nxt * mv)
            accs.append(jnp.sum(dv_ * (prev - uv), axis=0, keepdims=True))
        return outs, accs

    def halo_next(arr, w, cb):
        last_blk = T // SUB - 1
        return (arr, (SUB, w), lambda i: (jnp.minimum((i + 1) * (tbh // SUB), last_blk), cb))

    du_b, du_l, dmu_main_p, dmu_lora_p = _rowcall(
        "rwkv_shift_bwd", T, tbh,
        [_rows(dxs, tbh, 4096), halo_next(dxs, 4096, 0), _rows(dxl, tbh, 256), halo_next(dxl, 256, 0),
         _rows(u, tbh, 4096, 1), halo_prev(u, 4096, 1), _rows(u, tbh, 256, LORA_CB), halo_prev(u, 256, LORA_CB),
         _whole(mu_main), _whole(mu_lora)],
        shift_bwd_body, [(4096, BF16), (256, BF16)], [(1, 4096), (1, 256)])

    (du,) = _rowcall("assemble_du", T, tb,
                     [_rows(a_, tb, a_.shape[1]) for a_ in (dq, dk, dv, dza, du_b, dga, dgb, dfl, du_l)],
                     lambda i, *a: ([list(a[:8]) + [jnp.zeros((tb, LANE), BF16), a[8]]], []), [(NP, BF16)])
    lora_g = jnp.concatenate([dw2_p[:96].reshape(96, N_DEV, 128).transpose(1, 0, 2),
                              da2_p[:96].reshape(96, N_DEV, 128).transpose(1, 0, 2)], axis=2).astype(BF16)
    gb = jnp.concatenate([dwpf.reshape(1024, N_DEV, 256).transpose(1, 0, 2),
                          dwpr.reshape(1024, N_DEV, 256).transpose(1, 0, 2),
                          dwo.reshape(N_DEV, 2048, 256), lora_g, jnp.zeros((N_DEV, RB - 4192, 256), BF16)], axis=1)
    dw_pad_t, r1b = _mm(du, h, ta=True, out_dtype=BF16, tm=1280, tn=1024, name="mm_in_dw",
                        side=_exchange_pair_side(gb))
    xx, yy, cc = _place()
    chip_slots = jnp.stack([_slot((*chip, cc)) for chip in _other_chips(xx, yy)]).astype(jnp.int32)
    core = jnp.stack([cc]).astype(jnp.int32)
    sb = _pair_add("pair_add_packed", gb, r1b, chip_slots, 384)
    ga, t1b = _split_rows(dw_pad_t, [(d, lo, pad_col, n) for pad_col, d, lo, n in pieces], NI,
                          _exchange_first_side(sb))
    pb = _axis_add("axis_add_packed", sb, t1b, core, 384)
    r1a = _exchange_pair(ga)
    sa, t2b = _pair_add("pair_add_w_in", ga, r1a, chip_slots, 512, by_cols=True, side=_exchange_second_side(pb))
    ga_own = lax.dynamic_index_in_dim(ga, me, 0, keepdims=True)
    gb_own = lax.dynamic_index_in_dim(gb, me, 0, keepdims=True)
    dh, t1a = _mm(du, w_pad, tb=True, tm=1024, tn=1024, tk=NP // 10, name="mm_in_dx", side=_exchange_first_side(sa))
    pa = _axis_add("axis_add_w_in", sa, t1a, core, 512, by_cols=True)

    def rms_bwd_body(i, xv, g, dhv, doutv):
        _, vjp = jax.vjp(_rms_math, xv, g)
        dx_, dg_ = vjp(dhv)
        return [dx_ + doutv], [dg_]

    grad_x2, dng_p, t2a = _rowcall(
        "rms_bwd", T, tb, [_rows(x2, tb, D), _whole(norm_gain), _rows(dh, tb, D), _rows(dout, tb, D)],
        rms_bwd_body, [(D, F32)], [(1, D)], side=_exchange_second_side(pa))

    w_in_outs = _adamw("adamw_w_in", w_in[0].T, m_w_in[0].T, v_w_in[0].T, [(ga_own, 1), (r1a, 1), (t1a, 1), (t2a, 1)], 256, by_cols=True)
    g_in, d_in, m_in, v_in = [o.T for o in w_in_outs]

    dmu = jnp.concatenate([dmu_main_p, dmu_lora_p[:, 0:96], dmu_lora_p[:, 128:224]], axis=1)
    small_parts = [dng_p, dbias_p[:, 0:8], dmu, dw0_p, da0_p, dkk_p, dka_p, drk_p, dlnw_p, dlnb_p, dgf_p, loss_p]
    SR = 128
    small = _pad_cols(jnp.concatenate(small_parts, axis=1), SR * LANE).reshape(SR, LANE)
    rs = _allgather(small, "allgather_small")
    pk = lambda pf, pr, wo_, w2_, a2_: _pack_b(pf[0], pr[0], wo_[0], w2_[0], a2_[0], RB)
    outs_b = _adamw("adamw_packed", pk(w_proj_fox, w_proj_rwkv, w_out, rwkv_w2, rwkv_a2),
                    pk(m_w_proj_fox, m_w_proj_rwkv, m_w_out, m_rwkv_w2, m_rwkv_a2),
                    pk(v_w_proj_fox, v_w_proj_rwkv, v_w_out, v_rwkv_w2, v_rwkv_a2), [(gb_own, 1), (r1b, 1), (t1b, 1), (t2b, 1)], 384)

    def pack_small(ng, fb, sm, w0, a0, kk_, ka_, rk_, lnw, lnb, fg):
        parts = [ng, fb, sm, w0, a0, kk_, ka_, rk_.reshape(1, 1024), lnw, lnb, fg.reshape(1, D), jnp.zeros((1, 1), F32)]
        return _pad_cols(jnp.concatenate(parts, axis=1), SR * LANE).reshape(SR, LANE)

    outs_s = _adamw("adamw_small",
                    pack_small(norm_gain, fox_forget_bias, rwkv_shift_mix, rwkv_w0, rwkv_a0, rwkv_k_k, rwkv_k_a, rwkv_r_k,
                               rwkv_ln_w, rwkv_ln_b, final_norm_gain),
                    pack_small(m_norm_gain, m_fox_forget_bias, m_rwkv_shift_mix, m_rwkv_w0, m_rwkv_a0, m_rwkv_k_k, m_rwkv_k_a,
                               m_rwkv_r_k, m_rwkv_ln_w, m_rwkv_ln_b, m_final_norm_gain),
                    pack_small(v_norm_gain, v_fox_forget_bias, v_rwkv_shift_mix, v_rwkv_w0, v_rwkv_a0, v_rwkv_k_k, v_rwkv_k_a,
                               v_rwkv_r_k, v_rwkv_ln_w, v_rwkv_ln_b, v_final_norm_gain),
                    [(rs, N_DEV)], SR)

    def unpack_b(pkd):
        return dict(w_proj_fox=pkd[0:1024][None], w_proj_rwkv=pkd[1024:2048][None], w_out=pkd[2048:4096].reshape(1, 256, D),
                    rwkv_w2=pkd[4096:4192, 0:128][None], rwkv_a2=pkd[4096:4192, 128:256][None])

    def unpack_s(pkd):
        flat = pkd.reshape(1, SR * LANE)
        names = [("norm_gain", D), ("fox_forget_bias", 8), ("rwkv_shift_mix", 4288), ("rwkv_w0", 1024), ("rwkv_a0", 1024),
                 ("rwkv_k_k", 1024), ("rwkv_k_a", 1024), ("rwkv_r_k", 1024), ("rwkv_ln_w", 1024), ("rwkv_ln_b", 1024),
                 ("final_norm_gain", D), ("loss", 1)]
        out, off = {}, 0
        for nm, n in names:
            out[nm] = flat[:, off:off + n]
            off += n
        out["rwkv_r_k"] = out["rwkv_r_k"].reshape(1, 16, 64)
        out["final_norm_gain"] = out["final_norm_gain"].reshape(D)
        return out

    order = ["norm_gain", "w_in", "fox_forget_bias", "rwkv_shift_mix", "rwkv_w0", "rwkv_w2", "rwkv_a0", "rwkv_a2", "rwkv_k_k",
             "rwkv_k_a", "rwkv_r_k", "rwkv_ln_w", "rwkv_ln_b", "w_proj_fox", "w_proj_rwkv", "w_out", "final_norm_gain"]
    result = []
    loss = None
    for kind, big in enumerate((g_in, d_in, m_in, v_in)):
        d = {**unpack_b(outs_b[kind]), **unpack_s(outs_s[kind]), "w_in": big[None]}
        if kind == 0:
            loss = d["loss"].reshape(())
        result += [d[n] for n in order]
    return (loss, grad_x2[None], *result)
```

```python
import functools

import jax
import jax.numpy as jnp
from jax import lax
from jax.experimental import pallas as pl
from jax.experimental.pallas import tpu as pltpu

F32 = jnp.float32
BF16 = jnp.bfloat16
HI = lax.Precision.HIGHEST
H3 = lax.Precision.HIGH
MESH = pl.DeviceIdType.MESH

FOX_HD = 128
RW_HD = 64
RMS_EPS = 1e-6
GN_EPS = 64e-5
L2_EPS = 1e-12
ADAM_LR = 0.001
ADAM_B1 = 0.9
ADAM_B2 = 0.999
ADAM_EPS = 1e-08
ADAM_WD = 0.01
ADAM_STEP = 10

LANE = 128
SUB = 8
VMEM_LIMIT = 56 * 1024 * 1024
N_DEV = 8
CHUNK = 128
SCAN_GROUP = 2
SCAN_PAIRS = 2
PS = None
NEG = -1e30


def _scan_shape(T):
    c = min(CHUNK, T)
    return c, min(SCAN_GROUP, T // c)


def _cp(sem=None):
    return pltpu.CompilerParams(dimension_semantics=sem, vmem_limit_bytes=VMEM_LIMIT)


def _sigmoid(x):
    return jax.nn.sigmoid(x)


def _softplus(x):
    return jnp.maximum(x, 0.0) + jnp.log(1.0 + jnp.exp(-jnp.abs(x)))


def _nn(a, b, prec=None):
    return lax.dot_general(a, b, (((1,), (0,)), ((), ())), precision=prec, preferred_element_type=F32)


def _nt(a, b, prec=None):
    return lax.dot_general(a, b, (((1,), (1,)), ((), ())), precision=prec, preferred_element_type=F32)


def _tn(a, b, prec=None):
    return lax.dot_general(a, b, (((0,), (0,)), ((), ())), precision=prec, preferred_element_type=F32)


def _iota2(shape, dim):
    return lax.broadcasted_iota(jnp.int32, shape, dim)


def _seg_sum(x):
    r = _iota2((LANE, LANE), 0) // RW_HD
    c = _iota2((LANE, LANE), 1) // RW_HD
    bd = (r == c).astype(F32)
    parts = [_nn(x[:, j * LANE:(j + 1) * LANE], bd, H3) for j in range(x.shape[1] // LANE)]
    return parts[0] if len(parts) == 1 else jnp.concatenate(parts, axis=1)


def _mm(a, b, *, ta=False, tb=False, out_dtype=F32, tm=1024, tn=1024, tk=None, name, side=None):
    assert not (ta and tb)
    K, M = a.shape if ta else a.shape[::-1]
    N = b.shape[0] if tb else b.shape[1]
    tm, tn = min(tm, M), min(tn, N)
    tk = K if tk is None else tk
    nk = K // tk
    assert M % tm == 0 and N % tn == 0 and K % tk == 0
    gi, gj = M // tm, N // tn
    a_spec = pl.BlockSpec((tk, tm), lambda i, j, k: (k, i)) if ta else pl.BlockSpec((tm, tk), lambda i, j, k: (i, k))
    b_spec = pl.BlockSpec((tn, tk), lambda i, j, k: (j, k)) if tb else pl.BlockSpec((tk, tn), lambda i, j, k: (k, j))
    side_ins, side_outs, side_scr, side_make = side if side is not None else ((), (), (), None)
    n_si, n_so = len(side_ins), len(side_outs)
    n_acc = 0 if nk == 1 else 1

    def body(*refs):
        a_ref, b_ref = refs[:2]
        o_ref = refs[2 + n_si]
        scr = refs[3 + n_si + n_so:]
        k = pl.program_id(2)
        if side_make is not None:
            start, finish = side_make(refs[2:2 + n_si], refs[3 + n_si:3 + n_si + n_so], scr[n_acc:])
            first = jnp.logical_and(jnp.logical_and(pl.program_id(0) == 0, pl.program_id(1) == 0), k == 0)
            last = jnp.logical_and(jnp.logical_and(pl.program_id(0) == gi - 1, pl.program_id(1) == gj - 1), k == nk - 1)
            pl.when(first)(start)
        av = a_ref[...].astype(BF16)
        bv = b_ref[...].astype(BF16)
        p = _tn(av, bv) if ta else _nt(av, bv) if tb else _nn(av, bv)
        if nk == 1:
            o_ref[...] = p.astype(out_dtype)
        else:
            acc_ref = scr[0]

            @pl.when(k == 0)
            def _():
                acc_ref[...] = p

            @pl.when(k > 0)
            def _():
                acc_ref[...] += p

            @pl.when(k == nk - 1)
            def _():
                o_ref[...] = acc_ref[...].astype(out_dtype)
        if side_make is not None:
            pl.when(last)(finish)

    any_spec = pl.BlockSpec(memory_space=pl.ANY)
    res = pl.pallas_call(
        body, name=name,
        out_shape=[jax.ShapeDtypeStruct((M, N), out_dtype)] + list(side_outs),
        grid=(gi, gj, nk),
        in_specs=[a_spec, b_spec] + [any_spec] * n_si,
        out_specs=[pl.BlockSpec((tm, tn), lambda i, j, k: (i, j))] + [any_spec] * n_so,
        scratch_shapes=([] if nk == 1 else [pltpu.VMEM((tm, tn), F32)]) + list(side_scr),
        compiler_params=_cp(("arbitrary",) * 3 if side is not None else ("parallel", "parallel", "arbitrary")),
    )(a, b, *side_ins)
    return res if side is not None else res[0]


def _rows(arr, tb, w, cb=0):
    return (arr, (tb, w), lambda i: (i, cb))


def _whole(arr):
    nd = arr.ndim
    return (arr, arr.shape, lambda i: (0,) * nd)


def _rowcall(name, T, tb, ins, body, outs, accs=(), side=None):
    n_in, n_out, n_acc = len(ins), len(outs), len(accs)
    side_ins, side_outs, side_scr, side_make = side if side is not None else ((), (), (), None)
    n_si = len(side_ins)

    def kern(*refs):
        i = pl.program_id(0)
        if side_make is not None:
            side_refs = refs[n_in + n_si + n_out + n_acc:]
            start, finish = side_make(refs[n_in:n_in + n_si], side_refs[:len(side_outs)], side_refs[len(side_outs):])
            pl.when(i == 0)(start)
            refs = refs[:n_in] + refs[n_in + n_si:]
        vals = [r[...] for r in refs[:n_in]]
        ro, ao = body(i, *vals)
        for r, v in zip(refs[n_in:n_in + n_out], ro):
            if isinstance(v, (list, tuple)):
                off = 0
                for piece in v:
                    w = piece.shape[1]
                    r[:, off:off + w] = piece.astype(r.dtype)
                    off += w
            else:
                r[...] = v.astype(r.dtype)
        if accs:
            acc_refs = refs[n_in + n_out:n_in + n_out + n_acc]

            @pl.when(i == 0)
            def _():
                for r in acc_refs:
                    r[...] = jnp.zeros(r.shape, F32)

            for r, v in zip(acc_refs, ao):
                r[...] += v
        if side_make is not None:
            pl.when(i == T // tb - 1)(finish)

    any_spec = pl.BlockSpec(memory_space=pl.ANY)
    out_shape = [jax.ShapeDtypeStruct((T, w), dt) for (w, dt) in outs] + [jax.ShapeDtypeStruct(s, F32) for s in accs]
    out_specs = [pl.BlockSpec((tb, w), lambda i: (i, 0)) for (w, dt) in outs] + [pl.BlockSpec(s, lambda i: (0, 0)) for s in accs]
    res = pl.pallas_call(
        kern, name=name,
        out_shape=out_shape + list(side_outs),
        grid=(T // tb,),
        in_specs=[pl.BlockSpec(bs, im) for (_, bs, im) in ins] + [any_spec] * n_si,
        out_specs=out_specs + [any_spec] * len(side_outs),
        scratch_shapes=list(side_scr),
        compiler_params=_cp(("arbitrary",)),
    )(*[a for (a, _, _) in ins], *side_ins)
    return res


def _rms_math(x, g):
    r = lax.rsqrt(jnp.mean(x * x, axis=-1, keepdims=True) + RMS_EPS)
    return x * r * g


def _merge_math(ga, gb, pa, pb):
    return _sigmoid(ga) * pa + _sigmoid(gb) * pb


def _prep_math(xk, xwd, xad, w0, a0, kk_w, ka_w, w2p, a2p):
    z = w0 + _nn(jnp.tanh(xwd), w2p, H3)
    w = -_softplus(-z) - 0.5
    lw = -jnp.exp(w)
    ag = _sigmoid(a0 + _nn(xad, a2p, H3))
    p = xk * kk_w
    n = jnp.maximum(jnp.sqrt(_seg_sum(p * p)), L2_EPS)
    kk = p / n
    kp = xk * (1.0 + (ag - 1.0) * ka_w)
    return lw, kp, -kk, kk * ag


def _post_math(y, r, kp, v, z, lnw, lnb, rk):
    inv = 1.0 / RW_HD
    mu = _seg_sum(y) * inv
    d = y - mu
    var = _seg_sum(d * d) * inv
    yn = d * lax.rsqrt(var + GN_EPS) * lnw + lnb
    bonus = _seg_sum(r * kp * rk) * v
    return (yn + bonus) * (z * _sigmoid(z))


def _neumann(ms, depth):
    eye = (_iota2(ms[0].shape, 0) == _iota2(ms[0].shape, 1)).astype(F32)
    width = ms[0].shape[1]
    mp = [_nn(m, m, PS) for m in ms]
    inv = [eye + m for m in ms]
    n = 2
    while n < depth:
        last = 2 * n >= depth
        for i in range(len(ms)):
            if last:
                inv[i] = inv[i] + _nn(mp[i], inv[i], PS)
            else:
                z = _nn(mp[i], jnp.concatenate([mp[i], inv[i]], axis=1), PS)
                mp[i], inv[i] = z[:, :width], inv[i] + z[:, width:]
        n *= 2
    return inv


@functools.partial(jax.custom_vjp, nondiff_argnums=(1,))
def _unit_inverses(ms, depth):
    if depth <= RW_HD:
        return tuple(_neumann(list(ms), depth))
    h, n = depth // 2, len(ms)
    cat = jnp.concatenate
    z = jnp.zeros((h, h), F32)

    def heads(m, r, c):
        b0 = m[r * h:(r + 1) * h, c * h:(c + 1) * h]
        b1 = m[depth + r * h:depth + (r + 1) * h, depth + c * h:depth + (c + 1) * h]
        return cat([cat([b0, z], axis=1), cat([z, b1], axis=1)], axis=0)

    diag = _neumann([heads(m, 0, 0) for m in ms] + [heads(m, 1, 1) for m in ms], h)
    ta, td = diag[:n], diag[n:]
    low = [_nn(heads(m, 1, 0), a, PS) for m, a in zip(ms, ta)]
    low = [_nn(d, x, PS) for d, x in zip(td, low)]
    out = []
    for a, x, d in zip(ta, low, td):
        rows = []
        for hd in (0, 1):
            sl = slice(hd * h, (hd + 1) * h)
            top, bot = [a[sl, sl], z], [x[sl, sl], d[sl, sl]]
            pad = [z, z]
            rows.append(cat(top + pad if hd == 0 else pad + top, axis=1))
            rows.append(cat(bot + pad if hd == 0 else pad + bot, axis=1))
        out.append(cat(rows, axis=0))
    return tuple(out)


def _unit_inverses_fwd(ms, depth):
    inv = _unit_inverses(ms, depth)
    return inv, inv


def _unit_inverses_bwd(depth, inv, cts):
    left = [_tn(t, g, PS) for t, g in zip(inv, cts)]
    return (tuple(_nt(l, t, PS) for l, t in zip(left, inv)),)


_unit_inverses.defvjp(_unit_inverses_fwd, _unit_inverses_bwd)


@jax.custom_vjp
def _known_inverses(ms, inv):
    return inv


def _known_inverses_fwd(ms, inv):
    return inv, inv


def _known_inverses_bwd(inv, cts):
    dms = _unit_inverses_bwd(None, inv, cts)[0]
    return dms, tuple(jnp.zeros_like(t) for t in inv)


_known_inverses.defvjp(_known_inverses_fwd, _known_inverses_bwd)


def _scan_group(s0s, *flat, known_inv=None, with_inv=False):
    P = len(s0s)
    G = len(flat) // (6 * P)
    ch = [flat[6 * i:6 * i + 6] for i in range(P * G)]
    C = ch[0][0].shape[0]
    C2 = 2 * C
    cat = jnp.concatenate
    m0 = _iota2((1, LANE), 1) < RW_HD
    mask0 = m0.astype(F32)
    mask1 = 1.0 - mask0
    r2 = _iota2((C2, C2), 0)
    c2 = _iota2((C2, C2), 1)
    dist = r2 - c2
    in_head = dist <= r2 % C
    lower = (_iota2((C, C), 0) >= _iota2((C, C), 1)).astype(F32)
    bd = (_iota2((LANE, LANE), 0) // RW_HD) == (_iota2((LANE, LANE), 1) // RW_HD)

    def tri(m, strict):
        return jnp.where(dist > 0 if strict else dist >= 0, jnp.where(in_head, m, 0.0), 0.0)

    def sel(z):
        return jnp.where(m0, z[:C], z[C:])

    gs = [_nn(lower, c[1], HI) for c in ch]
    pre = []
    for (r, lw, k, v, a, b), g in zip(ch, gs):
        g_end = jnp.sum(lw, axis=0, keepdims=True)
        gm = g - jnp.sum(lw[:C // 2], axis=0, keepdims=True)
        en = jnp.exp(-gm)
        ec = jnp.exp(g_end - g)
        pre.append(dict(at=a * jnp.exp(g - lw), rt=r * jnp.exp(g), am=a * jnp.exp(gm - lw), rm=r * jnp.exp(gm),
                        bt=b * en, kt=k * en, bh=b * ec, kh=k * ec, dec=jnp.exp(g_end), v=v))
    grams = [_nt(cat([p["am"] * mask0, p["am"] * mask1, p["rm"] * mask0, p["rm"] * mask1], axis=0),
                 cat([p["bt"], p["bt"], p["kt"], p["kt"]], axis=0), PS) for p in pre]
    mab = tuple(tri(gm[:C2, :C2], True) for gm in grams)
    tinv = _unit_inverses(mab, C) if known_inv is None else _known_inverses(mab, known_inv)
    xv =[sel(_nn(tri(gm[:C2, C2:], True), cat([p["v"], p["v"]], axis=0), PS)) for gm, p in zip(grams, pre)]
    ys, s = [None] * (P * G), list(s0s)
    for i in range(G):
        for q in range(P):
            n = q * G + i
            p, gm = pre[n], grams[n]
            sx = _nt(cat([p["at"], p["rt"]], axis=0), s[q], PS)
            x = sx[:C] + xv[n]
            u = sel(_nn(tinv[n], cat([x, x], axis=0), PS))
            v = p["v"]
            ys[n] = sx[C:] + sel(_nn(cat([tri(gm[C2:, :C2], False), tri(gm[C2:, C2:], False)], axis=1),
                                     cat([u, u, v, v], axis=0), PS))
            s[q] = s[q] * p["dec"] + jnp.where(bd, _tn(cat([u, v], axis=0), cat([p["bh"], p["kh"]], axis=0), PS), 0.0)
    return (tuple(ys), tuple(s), tinv) if with_inv else (tuple(ys), tuple(s))


def _scan_fwd(xs, lw, kp, an, bb, T):
    C, G = _scan_shape(T)
    P = SCAN_PAIRS
    nc = T // (C * G)
    npair = 1024 // LANE

    def kern(r_ref, lw_ref, k_ref, v_ref, a_ref, b_ref, y_ref, st_ref, inv_ref, s_scr):
        n = pl.program_id(1)

        @pl.when(n == 0)
        def _():
            s_scr[...] = jnp.zeros(s_scr.shape, F32)

        st_ref[0] = s_scr[...]
        ins = (r_ref, lw_ref, k_ref, v_ref, a_ref, b_ref)
        ys, s1, inv = _scan_group(
            tuple(s_scr[q] for q in range(P)),
            *[ref[i * C:(i + 1) * C, q * LANE:(q + 1) * LANE] for q in range(P) for i in range(G) for ref in ins], with_inv=True)
        for q in range(P):
            for i in range(G):
                y_ref[i * C:(i + 1) * C, q * LANE:(q + 1) * LANE] = ys[q * G + i]
                inv_ref[0, 0, q * G + i] = inv[q * G + i]
            s_scr[q] = s1[q]

    def col(off):
        return pl.BlockSpec((C * G, P * LANE), lambda p, n: (n, off // P + p))

    return pl.pallas_call(
        kern, name="rwkv_scan_fwd",
        out_shape=[jax.ShapeDtypeStruct((T, 1024), F32), jax.ShapeDtypeStruct((nc, npair, LANE, LANE), F32),
                   jax.ShapeDtypeStruct((nc, npair // P, P * G, 2 * C, 2 * C), F32)],
        grid=(npair // P, nc),
        in_specs=[col(0), col(0), col(0), col(16), col(0), col(0)],
        out_specs=[col(0), pl.BlockSpec((1, P, LANE, LANE), lambda p, n: (n, p, 0, 0)),
                   pl.BlockSpec((1, 1, P * G, 2 * C, 2 * C), lambda p, n: (n, p, 0, 0, 0))],
        scratch_shapes=[pltpu.VMEM((P, LANE, LANE), F32)],
        compiler_params=_cp(("parallel", "arbitrary")),
    )(xs, lw, kp, xs, an, bb)


def _scan_bwd(xs, lw, kp, an, bb, states, invs, dy, T):
    C, G = _scan_shape(T)
    P = SCAN_PAIRS
    nc = T // (C * G)
    npair = 1024 // LANE

    def kern(r_ref, lw_ref, k_ref, v_ref, a_ref, b_ref, st_ref, inv_ref, dy_ref, dr_ref, dlw_ref, dk_ref, dv_ref, da_ref, db_ref,
             ds_scr):
        n = pl.program_id(1)

        @pl.when(n == 0)
        def _():
            ds_scr[...] = jnp.zeros(ds_scr.shape, F32)

        ins = (r_ref, lw_ref, k_ref, v_ref, a_ref, b_ref)
        units = [(q, i) for q in range(P) for i in range(G)]
        known = tuple(inv_ref[0, 0, q * G + i] for q, i in units)
        _, vjp = jax.vjp(functools.partial(_scan_group, known_inv=known), tuple(st_ref[0, q] for q in range(P)),
                         *[ref[i * C:(i + 1) * C, q * LANE:(q + 1) * LANE] for q, i in units for ref in ins])
        grads = vjp((tuple(dy_ref[i * C:(i + 1) * C, q * LANE:(q + 1) * LANE] for q, i in units),
                     tuple(ds_scr[q] for q in range(P))))
        for q in range(P):
            ds_scr[q] = grads[0][q]
        outs = (dr_ref, dlw_ref, dk_ref, dv_ref, da_ref, db_ref)
        for n_, (q, i) in enumerate(units):
            for t, ref in enumerate(outs):
                ref[i * C:(i + 1) * C, q * LANE:(q + 1) * LANE] = grads[1 + 6 * n_ + t]

    def col(off):
        return pl.BlockSpec((C * G, P * LANE), lambda p, n: (nc - 1 - n, off // P + p))

    return pl.pallas_call(
        kern, name="rwkv_scan_bwd",
        out_shape=[jax.ShapeDtypeStruct((T, 1024), F32)] * 6,
        grid=(npair // P, nc),
        in_specs=[col(0), col(0), col(0), col(16), col(0), col(0),
                  pl.BlockSpec((1, P, LANE, LANE), lambda p, n: (nc - 1 - n, p, 0, 0)),
                  pl.BlockSpec((1, 1, P * G, 2 * C, 2 * C), lambda p, n: (nc - 1 - n, p, 0, 0, 0)), col(0)],
        out_specs=[col(0)] * 6,
        scratch_shapes=[pltpu.VMEM((P, LANE, LANE), F32)],
        compiler_params=_cp(("parallel", "arbitrary")),
    )(xs, lw, kp, xs, an, bb, states, invs, dy)


def _gates_fwd(u, bias_pad, T, f_cb):
    nb = T // LANE

    def kern(f_ref, b_ref, c_ref):
        x = f_ref[...] + b_ref[...]
        lf = jnp.minimum(x, 0.0) - jnp.log(1.0 + jnp.exp(-jnp.abs(x)))
        lft = lf.T
        ut = (_iota2((LANE, LANE), 0) <= _iota2((LANE, LANE), 1)).astype(F32)
        carry = jnp.zeros((LANE, 1), F32)
        for blk in range(nb):
            seg = lft[:, blk * LANE:(blk + 1) * LANE]
            cs = _nn(seg, ut, HI) + carry
            c_ref[:, blk * LANE:(blk + 1) * LANE] = cs[:SUB, :]
            carry = carry + jnp.sum(seg, axis=1, keepdims=True)

    return pl.pallas_call(
        kern, name="fox_gates_fwd",
        out_shape=jax.ShapeDtypeStruct((SUB, T), F32),
        grid=(1,),
        in_specs=[pl.BlockSpec((T, LANE), lambda i: (0, f_cb)), pl.BlockSpec((1, LANE), lambda i: (0, 0))],
        out_specs=pl.BlockSpec((SUB, T), lambda i: (0, 0)),
        compiler_params=_cp(("arbitrary",)),
    )(u, bias_pad)


def _gates_bwd(dc, u, bias_pad, T, f_cb):
    nb = T // LANE

    def kern(dc_ref, f_ref, b_ref, dfl_ref, db_ref):
        dcv = jnp.concatenate([dc_ref[...], jnp.zeros((LANE - SUB, T), F32)], axis=0)
        lt = (_iota2((LANE, LANE), 0) >= _iota2((LANE, LANE), 1)).astype(F32)
        carry = jnp.zeros((LANE, 1), F32)
        pieces = [None] * nb
        for blk in range(nb - 1, -1, -1):
            seg = dcv[:, blk * LANE:(blk + 1) * LANE]
            pieces[blk] = _nn(seg, lt, HI) + carry
            carry = carry + jnp.sum(seg, axis=1, keepdims=True)
        dlf = (pieces[0] if nb == 1 else jnp.concatenate(pieces, axis=1)).T
        x = f_ref[...] + b_ref[...]
        dfl = dlf * _sigmoid(-x)
        dfl_ref[...] = dfl
        db_ref[...] = jnp.sum(dfl, axis=0, keepdims=True)

    return pl.pallas_call(
        kern, name="fox_gates_bwd",
        out_shape=[jax.ShapeDtypeStruct((T, LANE), F32), jax.ShapeDtypeStruct((1, LANE), F32)],
        grid=(1,),
        in_specs=[pl.BlockSpec((SUB, T), lambda i: (0, 0)), pl.BlockSpec((T, LANE), lambda i: (0, f_cb)),
                  pl.BlockSpec((1, LANE), lambda i: (0, 0))],
        out_specs=[pl.BlockSpec((T, LANE), lambda i: (0, 0)), pl.BlockSpec((1, LANE), lambda i: (0, 0))],
        compiler_params=_cp(("arbitrary",)),
    )(dc, u, bias_pad)


ATTN_HEADS = 4


def _attn_block(T):
    return 256 if T % 256 == 0 and T >= 512 else 128


def _attn_fwd(u, c3, T):
    H, HP = 8, ATTN_HEADS
    bq = _attn_block(T)
    nq = T // bq
    scale = FOX_HD ** -0.5
    lanes = [slice(h * LANE, (h + 1) * LANE) for h in range(HP)]

    def kern(q_ref, k_ref, v_ref, z_ref, cq_ref, ck_ref, o_ref, oa_ref, lse_ref):
        i = pl.program_id(1)
        q = [(q_ref[:, ln] * scale).astype(BF16) for ln in lanes]
        c0 = [cq_ref[h][:, 0:1] for h in range(HP)]

        def step(j, carry, diagonal=False):
            off = pl.multiple_of(j * bq, bq)
            s = [_nt(q[h], k_ref[pl.ds(off, bq), lanes[h]].astype(BF16)) + (c0[h] - ck_ref[h, :, pl.ds(off, bq)])
                 for h in range(HP)]
            ps, out = [], []
            for h in range(HP):
                m, l, acc = carry[h]
                sh = s[h]
                if diagonal:
                    sh = jnp.where(_iota2((bq, bq), 1) <= _iota2((bq, bq), 0), sh, NEG)
                m_new = jnp.maximum(m, jnp.max(sh, axis=1, keepdims=True))
                p = jnp.exp(sh - m_new)
                alpha = jnp.exp(m - m_new)
                p_hi = p.astype(BF16)
                ps.append((p_hi, (p - p_hi.astype(F32)).astype(BF16)))
                out.append((m_new, alpha * l + jnp.sum(p, axis=1, keepdims=True), alpha * acc))
            res = []
            for h, (m, l, acc) in enumerate(out):
                vj = v_ref[pl.ds(off, bq), lanes[h]].astype(BF16)
                res.append((m, l, acc + _nn(ps[h][0], vj) + _nn(ps[h][1], vj)))
            return tuple(res)

        init = tuple((jnp.full((bq, 1), NEG, F32), jnp.zeros((bq, 1), F32), jnp.zeros((bq, FOX_HD), F32)) for _ in range(HP))
        res = step(i, lax.fori_loop(0, i, step, init), diagonal=True)
        for h, (m, l, acc) in enumerate(res):
            o = acc / l
            z = z_ref[:, lanes[h]]
            o_ref[:, lanes[h]] = o
            oa_ref[:, lanes[h]] = (o * z * _sigmoid(z)).astype(BF16)
            lse_ref[h] = m + jnp.log(l)

    W = HP * LANE
    return pl.pallas_call(
        kern, name="fox_attn_fwd",
        out_shape=[jax.ShapeDtypeStruct((T, 1024), F32), jax.ShapeDtypeStruct((T, 1024), BF16),
                   jax.ShapeDtypeStruct((H, T, 1), F32)],
        grid=(H // HP, nq),
        in_specs=[pl.BlockSpec((bq, W), lambda g, i: (i, g)),
                  pl.BlockSpec((T, W), lambda g, i: (0, 8 // HP + g)),
                  pl.BlockSpec((T, W), lambda g, i: (0, 16 // HP + g)),
                  pl.BlockSpec((bq, W), lambda g, i: (i, 24 // HP + g)),
                  pl.BlockSpec((HP, 1, bq), lambda g, i: (g, 0, i)),
                  pl.BlockSpec((HP, 1, T), lambda g, i: (g, 0, 0))],
        out_specs=[pl.BlockSpec((bq, W), lambda g, i: (i, g)),
                   pl.BlockSpec((bq, W), lambda g, i: (i, g)),
                   pl.BlockSpec((HP, bq, 1), lambda g, i: (g, i, 0))],
        compiler_params=_cp(("parallel", "arbitrary")),
    )(u, u, u, u, c3, c3)


def _attn_probs(s, lse_i, diagonal):
    if not diagonal:
        return jnp.exp(s - lse_i)
    keep = _iota2(s.shape, 1) <= _iota2(s.shape, 0)
    return jnp.where(keep, jnp.exp(jnp.where(keep, s, NEG) - lse_i), 0.0)


def _attn_pre_math(doa, z, o):
    sg = _sigmoid(z)
    do = (doa * z * sg).astype(BF16)
    dz = doa * o * (sg * (1.0 + z * (1.0 - sg)))
    head_of = (_iota2((o.shape[1], LANE), 0) // FOX_HD == _iota2((o.shape[1], LANE), 1)).astype(F32)
    return do, dz, _nn(do.astype(F32) * o, head_of, HI)


def _attn_bwd(u, c3, lse, do, delta, T):
    H, HP = 8, ATTN_HEADS
    bq = _attn_block(T)
    nq = T // bq
    scale = FOX_HD ** -0.5
    lanes = [slice(h * LANE, (h + 1) * LANE) for h in range(HP)]

    def kern(q_ref, k_ref, v_ref, c_ref, lse_ref, do_ref, dl_ref, dq_ref, dk_ref, dv_ref, dc_ref):
        j = pl.program_id(1)

        @pl.when(j == 0)
        def _():
            dq_ref[...] = jnp.zeros(dq_ref.shape, F32)

        kj = [k_ref[:, ln].astype(BF16) for ln in lanes]
        vj = [v_ref[:, ln].astype(BF16) for ln in lanes]
        joff = pl.multiple_of(j * bq, bq)
        ck = [c_ref[h, :, pl.ds(joff, bq)] for h in range(HP)]

        def step(i, carry, diagonal=False):
            off = pl.multiple_of(i * bq, bq)
            qs = [(q_ref[pl.ds(off, bq), ln] * scale).astype(BF16) for ln in lanes]
            dob = [do_ref[pl.ds(off, bq), ln] for ln in lanes]
            s = [_nt(qs[h], kj[h]) + (c_ref[h, :, pl.ds(off, bq)][:, 0:1] - ck[h]) for h in range(HP)]
            dp = [_nt(dob[h], vj[h]) for h in range(HP)]
            pb, dsb, dcs = [], [], []
            for h in range(HP):
                p = _attn_probs(s[h], lse_ref[h, pl.ds(off, bq), :], diagonal)
                ds = p * (dp[h] - dl_ref[h, pl.ds(off, bq), :])
                pb.append(p.astype(BF16))
                dsb.append(ds.astype(BF16))
                dcs.append(jnp.sum(ds, axis=0, keepdims=True))
            out = []
            for h, (dk, dv, dc) in enumerate(carry):
                dq_ref[pl.ds(off, bq), lanes[h]] += _nn(dsb[h], kj[h]) * scale
                out.append((dk + _tn(dsb[h], qs[h]), dv + _tn(pb[h], dob[h]), dc - dcs[h]))
            return tuple(out)

        init = tuple((jnp.zeros((bq, FOX_HD), F32), jnp.zeros((bq, FOX_HD), F32), jnp.zeros((1, bq), F32)) for _ in range(HP))
        res = lax.fori_loop(j + 1, nq, step, step(j, init, diagonal=True))
        for h, (dk, dv, dc) in enumerate(res):
            dk_ref[:, lanes[h]] = dk
            dv_ref[:, lanes[h]] = dv
            dc_ref[h] = dc

    W = HP * LANE
    full = lambda cb: pl.BlockSpec((T, W), lambda g, j: (0, cb // HP + g))
    blk = lambda cb: pl.BlockSpec((bq, W), lambda g, j: (j, cb // HP + g))
    col1 = pl.BlockSpec((HP, T, 1), lambda g, j: (g, 0, 0))
    return pl.pallas_call(
        kern, name="fox_attn_bwd",
        out_shape=[jax.ShapeDtypeStruct((T, 1024), F32)] * 3 + [jax.ShapeDtypeStruct((H, 1, T), F32)],
        grid=(H // HP, nq),
        in_specs=[full(0), blk(8), blk(16), pl.BlockSpec((HP, 1, T), lambda g, j: (g, 0, 0)), col1, full(0), col1],
        out_specs=[full(0), blk(0), blk(0), pl.BlockSpec((HP, 1, bq), lambda g, j: (g, 0, j))],
        compiler_params=_cp(("parallel", "arbitrary")),
    )(u, u, u, c3, lse, do, delta)


def _place():
    return lax.axis_index("x"), lax.axis_index("y"), lax.axis_index("c")


def _slot(p):
    return 4 * p[0] + 2 * p[1] + p[2]


def _other_chips(x, y):
    return [(1 - x, y), (x, 1 - y), (1 - x, 1 - y)]


def _allgather_steps(in_refs, out_refs, scratch):
    (src,), (dst,) = in_refs, out_refs
    send_sems, recv_sems, local_sem = scratch
    x, y, c = _place()
    me, sibling = (x, y, c), (x, y, 1 - c)
    chips = _other_chips(x, y)

    def copy(k, block, to, from_input=False):
        d = dst.at[_slot(block)]
        return pltpu.make_async_remote_copy(
            src_ref=src if from_input else d, dst_ref=d, send_sem=send_sems.at[k], recv_sem=recv_sems.at[k],
            device_id=to, device_id_type=MESH)

    def first_copies():
        return [copy(0, me, sibling, True)] + [copy(1 + j, me, (*chip, c), True) for j, chip in enumerate(chips)]

    def start():
        pltpu.make_async_copy(src, dst.at[_slot(me)], local_sem).start()
        for cp in first_copies():
            cp.start()

    def finish():
        passed = []
        for j, chip in enumerate(chips):
            copy(1 + j, (*chip, c), me).wait_recv()
            passed.append(copy(4 + j, (*chip, c), sibling))
            passed[-1].start()
        copy(0, sibling, me).wait_recv()
        for j, chip in enumerate(chips):
            copy(4 + j, (*chip, 1 - c), me).wait_recv()
        for cp in first_copies() + passed:
            cp.wait_send()
        pltpu.make_async_copy(src, dst.at[_slot(me)], local_sem).wait()

    return start, finish


def _allgather_relay_steps(in_refs, out_refs, scratch):
    (src,), (dst,) = in_refs, out_refs
    send_sems, recv_sems, local_sem = scratch
    x, y, c = _place()
    me, sibling = (x, y, c), (x, y, 1 - c)
    x_nbr, y_nbr, diag = (1 - x, y, c), (x, 1 - y, c), (1 - x, 1 - y, c)
    flip = lambda a, bit: a + bit - 2 * a * bit
    relay_from = (flip(x, 1 - c), flip(y, c), c)
    relay_to = (flip(x, c), flip(y, 1 - c), c)

    def copy(k, block, to, from_input=False):
        d = dst.at[_slot(block)]
        return pltpu.make_async_remote_copy(
            src_ref=src if from_input else d, dst_ref=d, send_sem=send_sems.at[k], recv_sem=recv_sems.at[k],
            device_id=to, device_id_type=MESH)

    def first_copies():
        return [copy(0, me, sibling, True), copy(1, me, x_nbr, True), copy(2, me, y_nbr, True)]

    def other(block):
        return block[:2] + (1 - c,)

    def start():
        pltpu.make_async_copy(src, dst.at[_slot(me)], local_sem).start()
        for cp in first_copies():
            cp.start()

    def finish():
        copy(1, x_nbr, me).wait_recv()
        copy(2, y_nbr, me).wait_recv()
        later = [copy(3, relay_from, relay_to), copy(4, x_nbr, sibling), copy(5, y_nbr, sibling)]
        for cp in later:
            cp.start()
        copy(3, diag, me).wait_recv()
        later.append(copy(6, diag, sibling))
        later[-1].start()
        copy(0, sibling, me).wait_recv()
        for k, block in ((4, x_nbr), (5, y_nbr), (6, diag)):
            copy(k, other(block), me).wait_recv()
        for cp in first_copies() + later:
            cp.wait_send()
        pltpu.make_async_copy(src, dst.at[_slot(me)], local_sem).wait()

    return start, finish


def _allgather_side(a, relay=False):
    return ((a,), (jax.ShapeDtypeStruct((N_DEV,) + a.shape, a.dtype),),
            (pltpu.SemaphoreType.DMA((7,)), pltpu.SemaphoreType.DMA((7,)), pltpu.SemaphoreType.DMA),
            _allgather_relay_steps if relay else _allgather_steps)


def _allgather(a, name, relay=False):
    ins, outs, scratch, make = _allgather_side(a, relay)

    def body(a_ref, o_ref, *scr):
        start, finish = make((a_ref,), (o_ref,), scr)
        start()
        finish()

    any_spec = pl.BlockSpec(memory_space=pl.ANY)
    return pl.pallas_call(body, name=name, out_shape=outs[0], in_specs=[any_spec], out_specs=any_spec,
                          scratch_shapes=list(scratch))(a)


def _exchange_pair_steps(in_refs, out_refs, scratch):
    (src,), (dst,) = in_refs, out_refs
    send_sems, recv_sems = scratch
    x, y, c = _place()
    sibling = (x, y, 1 - c)
    slots = [_slot(sibling)] + [_slot((*chip, 1 - c)) for chip in _other_chips(x, y)]

    def copies():
        return [pltpu.make_async_remote_copy(
            src_ref=src.at[ps], dst_ref=dst.at[k], send_sem=send_sems.at[k], recv_sem=recv_sems.at[k],
            device_id=sibling, device_id_type=MESH) for k, ps in enumerate(slots)]

    def start():
        for cp in copies():
            cp.start()

    def finish():
        for cp in copies():
            cp.wait()

    return start, finish


def _exchange_pair_side(g):
    return ((g,), (jax.ShapeDtypeStruct((4,) + g.shape[1:], g.dtype),),
            (pltpu.SemaphoreType.DMA((4,)), pltpu.SemaphoreType.DMA((4,))), _exchange_pair_steps)


def _exchange_pair(g):
    ins, outs, scratch, make = _exchange_pair_side(g)

    def body(g_ref, r_ref, *scr):
        start, finish = make((g_ref,), (r_ref,), scr)
        start()
        finish()

    any_spec = pl.BlockSpec(memory_space=pl.ANY)
    return pl.pallas_call(body, name="exchange_pair", out_shape=outs[0], in_specs=[any_spec], out_specs=any_spec,
                          scratch_shapes=list(scratch))(g)


def _tiling(R, Cc, tile, by_cols):
    if by_cols:
        assert Cc % tile == 0
        return Cc // tile, (R, tile), lambda lead, i: (lead, 0, i)
    assert R % tile == 0
    return R // tile, (tile, Cc), lambda lead, i: (lead, i, 0)


def _pair_add(name, g, r1, slots, tile, by_cols=False, side=None):
    _, R, Cc = g.shape
    steps, blk, at = _tiling(R, Cc, tile, by_cols)
    side_ins, side_outs, side_scr, side_make = side if side is not None else ((), (), (), None)
    n_si, n_so = len(side_ins), len(side_outs)

    def kern(s_ref, a_ref, b_ref, *rest):
        o_ref = rest[n_si]
        if side_make is not None:
            start, finish = side_make(rest[:n_si], rest[n_si + 1:n_si + 1 + n_so], rest[n_si + 1 + n_so:])
            j, i = pl.program_id(0), pl.program_id(1)
            pl.when(jnp.logical_and(j == 0, i == 0))(start)
        o_ref[...] = (a_ref[...].astype(F32) + b_ref[...].astype(F32)).astype(o_ref.dtype)
        if side_make is not None:
            pl.when(jnp.logical_and(j == 2, i == steps - 1))(finish)

    any_spec = pl.BlockSpec(memory_space=pl.ANY)
    res = pl.pallas_call(
        kern, name=name,
        out_shape=[jax.ShapeDtypeStruct((3, R, Cc), BF16)] + list(side_outs),
        grid_spec=pltpu.PrefetchScalarGridSpec(
            num_scalar_prefetch=1, grid=(3, steps),
            in_specs=[pl.BlockSpec((1,) + blk, lambda j, i, s: at(s[j], i)),
                      pl.BlockSpec((1,) + blk, lambda j, i, s: at(1 + j, i))] + [any_spec] * n_si,
            out_specs=[pl.BlockSpec((1,) + blk, lambda j, i, s: at(j, i))] + [any_spec] * n_so,
            scratch_shapes=list(side_scr)),
        compiler_params=_cp(("arbitrary", "arbitrary")),
    )(slots, g, r1, *side_ins)
    return res if side is not None else res[0]


def _axis_neighbours():
    x, y, c = _place()
    flip = lambda a, bit: a + bit - 2 * a * bit
    return (flip(x, c), flip(y, 1 - c), c), (flip(x, 1 - c), flip(y, c), c), c


def _exchange_first_steps(in_refs, out_refs, scratch):
    pairs = list(zip(in_refs, out_refs))
    send_sems, recv_sems = scratch
    first, _, c = _axis_neighbours()

    def copies():
        return [pltpu.make_async_remote_copy(
            src_ref=src.at[j], dst_ref=dst.at[k], send_sem=send_sems.at[t, k], recv_sem=recv_sems.at[t, k],
            device_id=first, device_id_type=MESH)
            for t, (src, dst) in enumerate(pairs) for k, j in enumerate((1 - c, 2))]

    def start():
        for cp in copies():
            cp.start()

    def finish():
        for cp in copies():
            cp.wait()

    return start, finish


def _exchange_first_side(*ss):
    n = len(ss)
    return (ss, tuple(jax.ShapeDtypeStruct((2,) + s.shape[1:], s.dtype) for s in ss),
            (pltpu.SemaphoreType.DMA((n, 2)), pltpu.SemaphoreType.DMA((n, 2))), _exchange_first_steps)


def _axis_add(name, s, t1, core, tile, by_cols=False):
    _, R, Cc = s.shape
    steps, blk, at = _tiling(R, Cc, tile, by_cols)

    def kern(c_ref, a_ref, b_ref, o_ref):
        o_ref[...] = (a_ref[...].astype(F32) + b_ref[...].astype(F32)).astype(o_ref.dtype)

    return pl.pallas_call(
        kern, name=name,
        out_shape=jax.ShapeDtypeStruct((1, R, Cc), BF16),
        grid_spec=pltpu.PrefetchScalarGridSpec(
            num_scalar_prefetch=1, grid=(steps,),
            in_specs=[pl.BlockSpec((1,) + blk, lambda i, cr: at(cr[0], i)),
                      pl.BlockSpec((1,) + blk, lambda i, cr: at(1, i))],
            out_specs=pl.BlockSpec((1,) + blk, lambda i, cr: at(0, i))),
        compiler_params=_cp(("arbitrary",)),
    )(core, s, t1)


def _exchange_second_steps(in_refs, out_refs, scratch):
    send_sems, recv_sems = scratch
    _, second, _ = _axis_neighbours()

    def copies():
        return [pltpu.make_async_remote_copy(src_ref=src, dst_ref=dst, send_sem=send_sems.at[t], recv_sem=recv_sems.at[t],
                                             device_id=second, device_id_type=MESH)
                for t, (src, dst) in enumerate(zip(in_refs, out_refs))]

    def start():
        for cp in copies():
            cp.start()

    def finish():
        for cp in copies():
            cp.wait()

    return start, finish


def _exchange_second_side(*ps):
    n = len(ps)
    return (ps, tuple(jax.ShapeDtypeStruct(p.shape, p.dtype) for p in ps),
            (pltpu.SemaphoreType.DMA((n,)), pltpu.SemaphoreType.DMA((n,))), _exchange_second_steps)


def _adamw(name, w, m, v, parts, tile, by_cols=False):
    R, Cc = w.shape
    steps, blk_shape, at = _tiling(R, Cc, tile, by_cols)
    n_parts = len(parts)

    def kern(*refs):
        w_ref, m_ref, v_ref = refs[:3]
        g = None
        for r_ref, (_, n) in zip(refs[3:3 + n_parts], parts):
            for s in range(n):
                term = r_ref[s].astype(F32)
                g = term if g is None else g + term
        g_out, d_out, m_out, v_out = refs[3 + n_parts:]
        mn = ADAM_B1 * m_ref[...] + (1.0 - ADAM_B1) * g
        vn = ADAM_B2 * v_ref[...] + (1.0 - ADAM_B2) * (g * g)
        m_hat = mn / (1.0 - ADAM_B1 ** ADAM_STEP)
        v_hat = vn / (1.0 - ADAM_B2 ** ADAM_STEP)
        g_out[...] = g
        d_out[...] = -ADAM_LR * (m_hat / (jnp.sqrt(v_hat) + ADAM_EPS) + ADAM_WD * w_ref[...])
        m_out[...] = mn
        v_out[...] = vn

    blk = pl.BlockSpec(blk_shape, lambda i: at(0, i)[1:])
    return pl.pallas_call(
        kern, name=name,
        out_shape=[jax.ShapeDtypeStruct((R, Cc), F32)] * 4,
        grid=(steps,),
        in_specs=[blk] * 3 + [pl.BlockSpec((n,) + blk_shape, lambda i: at(0, i)) for (_, n) in parts],
        out_specs=[blk] * 4,
        compiler_params=_cp(("arbitrary",)),
    )(w, m, v, *[a for (a, _) in parts])


def _assemble_columns(blocks, pieces, zeros, width):
    _, R, Cc = blocks.shape
    tr = min(256, R)

    def kern(b_ref, o_ref):
        for col, n in zeros:
            o_ref[:, col:col + n] = jnp.zeros((tr, n), o_ref.dtype)
        for col, d, lo, n in pieces:
            o_ref[:, col:col + n] = b_ref[d, :, lo:lo + n]

    return pl.pallas_call(
        kern, name="assemble_w_in",
        out_shape=jax.ShapeDtypeStruct((R, width), blocks.dtype),
        grid=(R // tr,),
        in_specs=[pl.BlockSpec((N_DEV, tr, Cc), lambda i: (0, i, 0))],
        out_specs=pl.BlockSpec((tr, width), lambda i: (i, 0)),
        compiler_params=_cp(("parallel",)),
    )(blocks)


def _split_rows(x, pieces, rows, side):
    _, Cc = x.shape
    tc = min(256, Cc)
    side_ins, side_outs, side_scr, side_make = side
    n_si, n_so = len(side_ins), len(side_outs)

    def kern(x_ref, *rest):
        o_ref = rest[n_si]
        start, finish = side_make(rest[:n_si], rest[n_si + 1:n_si + 1 + n_so], rest[n_si + 1 + n_so:])
        pl.when(pl.program_id(0) == 0)(start)
        for d, lo, row, n in pieces:
            o_ref[d, lo:lo + n, :] = x_ref[row:row + n, :]
        pl.when(pl.program_id(0) == Cc // tc - 1)(finish)

    any_spec = pl.BlockSpec(memory_space=pl.ANY)
    return pl.pallas_call(
        kern, name="split_w_in_grad",
        out_shape=[jax.ShapeDtypeStruct((N_DEV, rows, Cc), x.dtype)] + list(side_outs),
        grid=(Cc // tc,),
        in_specs=[pl.BlockSpec((x.shape[0], tc), lambda i: (0, i))] + [any_spec] * n_si,
        out_specs=[pl.BlockSpec((N_DEV, rows, tc), lambda i: (0, 0, i))] + [any_spec] * n_so,
        scratch_shapes=list(side_scr),
        compiler_params=_cp(("arbitrary",)),
    )(x, *side_ins)


def _pad_cols(a, w):
    return jnp.pad(a, ((0, 0), (0, w - a.shape[1])))


def _pad_rows(a, r):
    return jnp.pad(a, ((0, r - a.shape[0]), (0, 0)))


def _pack_b(pf, pr, wo, w2, a2, rows):
    body = jnp.concatenate([pf, pr, wo.reshape(2048, 256), jnp.concatenate([w2, a2], axis=1)], axis=0)
    return _pad_rows(body, rows)


def kernel(x, norm_gain, w_in, fox_forget_bias, rwkv_shift_mix, rwkv_w0, rwkv_w2, rwkv_a0, rwkv_a2, rwkv_k_k, rwkv_k_a, rwkv_r_k, rwkv_ln_w, rwkv_ln_b, w_proj_fox, w_proj_rwkv, w_out, final_norm_gain, loss_target, m_norm_gain, m_w_in, m_fox_forget_bias, m_rwkv_shift_mix, m_rwkv_w0, m_rwkv_w2, m_rwkv_a0, m_rwkv_a2, m_rwkv_k_k, m_rwkv_k_a, m_rwkv_r_k, m_rwkv_ln_w, m_rwkv_ln_b, m_w_proj_fox, m_w_proj_rwkv, m_w_out, m_final_norm_gain, v_norm_gain, v_w_in, v_fox_forget_bias, v_rwkv_shift_mix, v_rwkv_w0, v_rwkv_w2, v_rwkv_a0, v_rwkv_a2, v_rwkv_k_k, v_rwkv_k_a, v_rwkv_r_k, v_rwkv_ln_w, v_rwkv_ln_b, v_w_proj_fox, v_w_proj_rwkv, v_w_out, v_final_norm_gain):
    T, D = x.shape[1], x.shape[2]
    assert D == 2048 and T % LANE == 0
    NI = w_in.shape[2]
    IN = N_DEV * NI
    RB = 4224
    x2 = x[0]
    lt2 = loss_target[0]
    me = _slot(_place())

    tb = min(256, T)
    tbh = min(128, T)
    h, wa = _rowcall("rms_fwd", T, tb, [_rows(x2, tb, D), _whole(norm_gain)],
                     lambda i, xv, g: ([_rms_math(xv, g)], []), [(D, BF16)],
                     side=_allgather_side(w_in[0].astype(BF16), relay=True))
    packed_own = _pack_b(w_proj_fox[0], w_proj_rwkv[0], w_out[0], rwkv_w2[0], rwkv_a2[0], RB).astype(BF16)
    sections = [(0, 4096, 0), (4104, 4096, 4096), (8392, 4096, 8192), (4096, 8, 12288), (8200, 96, 12544), (8296, 96, 12672)]
    NP = 12800
    pieces, zeros, at_col = [], [], 0
    for lo, width, pad_lo in sections:
        if pad_lo > at_col:
            zeros.append((at_col, pad_lo - at_col))
        col = lo
        while col < lo + width:
            d = col // NI
            stop = min(lo + width, (d + 1) * NI)
            pieces.append((pad_lo + col - lo, d, col - d * NI, stop - col))
            col = stop
        at_col = pad_lo + width
    zeros.append((at_col, NP - at_col))
    w_pad = _assemble_columns(wa, pieces, zeros, NP)
    F_CB, LORA_CB = 96, 49

    mu = rwkv_shift_mix
    mu_main = mu[:, 0:4096]
    mu_lora = jnp.concatenate([_pad_cols(mu[:, 4096:4192], LANE), _pad_cols(mu[:, 4192:4288], LANE)], axis=1)
    bias_pad = _pad_cols(fox_forget_bias, LANE)
    rk_flat = rwkv_r_k.reshape(1, 1024)
    gf = final_norm_gain.reshape(1, D)

    u, wb = _mm(h, w_pad, tm=1024, tn=1280, name="mm_in", side=_allgather_side(packed_own))
    wpf = wb[:, 0:1024, :].transpose(1, 0, 2).reshape(1024, D)
    wpr = wb[:, 1024:2048, :].transpose(1, 0, 2).reshape(1024, D)
    wo = wb[:, 2048:4096, :].reshape(N_DEV * 256, D)
    w2p = _pad_rows(wb[:, 4096:4192, 0:128].transpose(1, 0, 2).reshape(96, 1024).astype(F32), LANE)
    a2p = _pad_rows(wb[:, 4096:4192, 128:256].transpose(1, 0, 2).reshape(96, 1024).astype(F32), LANE)

    c8 = _gates_fwd(u, bias_pad, T, F_CB)
    c3 = c8.reshape(8, 1, T)
    o_raw, o_a, lse = _attn_fwd(u, c3, T)

    def shift_body(i, um, hm, ul, hl, mm_, ml):
        outs = []
        for uv, hv, mv in ((um, hm, mm_), (ul, hl, ml)):
            hv = jnp.where(i == 0, 0.0, hv)
            prev = pltpu.roll(jnp.concatenate([hv, uv], axis=0), 1, 0)[SUB:]
            outs.append(uv + (prev - uv) * mv)
        return outs, []

    def halo_prev(arr, w, cb):
        return (arr, (SUB, w), lambda i: (jnp.maximum(i * (tbh // SUB) - 1, 0), cb))

    xs, xl = _rowcall("rwkv_shift_fwd", T, tbh,
                      [_rows(u, tbh, 4096, 1), halo_prev(u, 4096, 1), _rows(u, tbh, 256, LORA_CB), halo_prev(u, 256, LORA_CB),
                       _whole(mu_main), _whole(mu_lora)],
                      shift_body, [(4096, F32), (256, F32)])

    prep_par = [_whole(rwkv_w0), _whole(rwkv_a0), _whole(rwkv_k_k), _whole(rwkv_k_a), _whole(w2p), _whole(a2p)]
    prep_rows = [_rows(xs, tbh, 1024, 1), _rows(xl, tbh, LANE, 0), _rows(xl, tbh, LANE, 1)]
    lw, kp, an, bb = _rowcall("rwkv_prep_fwd", T, tbh, prep_rows + prep_par,
                              lambda i, *a: (list(_prep_math(*a)), []), [(1024, F32)] * 4)
    y, states, invs = _scan_fwd(xs, lw, kp, an, bb, T)
    post_rows = [_rows(y, tbh, 1024), _rows(xs, tbh, 1024, 0), _rows(kp, tbh, 1024), _rows(xs, tbh, 1024, 2), _rows(xs, tbh, 1024, 3)]
    post_par = [_whole(rwkv_ln_w), _whole(rwkv_ln_b), _whole(rk_flat)]
    (o_b,) = _rowcall("rwkv_post_fwd", T, tbh, post_rows + post_par,
                      lambda i, *a: ([_post_math(*a)], []), [(1024, BF16)])

    pa = _mm(o_a, wpf, name="mm_proj_fox")
    pb = _mm(o_b, wpr, name="mm_proj_rwkv")
    merge_rows = [_rows(u, tb, D, 4), _rows(u, tb, D, 5), _rows(pa, tb, D), _rows(pb, tb, D)]
    (mg,) = _rowcall("merge_fwd", T, tb, merge_rows, lambda i, *a: ([_merge_math(*a)], []), [(D, BF16)])
    mo = _mm(mg, wo, name="mm_out")

    def head_body(i, xv, mov, ltv, g):
        out = xv + mov
        r = lax.rsqrt(jnp.mean(out * out, axis=-1, keepdims=True) + RMS_EPS)
        yn = out * r
        err = yn * g - ltv
        loss = 0.5 * jnp.sum(jnp.sum(err * err, axis=-1, keepdims=True), axis=0, keepdims=True) / D
        dyv = err / D
        dyn = dyv * g
        dout = r * (dyn - yn * jnp.mean(dyn * yn, axis=-1, keepdims=True))
        return [dout], [loss, jnp.sum(dyv * yn, axis=0, keepdims=True)]

    dout, loss_p, dgf_p = _rowcall("loss_head", T, tb, [_rows(x2, tb, D), _rows(mo, tb, D), _rows(lt2, tb, D), _whole(gf)],
                                   head_body, [(D, F32)], [(1, 1), (1, D)])

    dm = _mm(dout, wo, tb=True, name="mm_out_dx")
    dwo = _mm(mg, dout, ta=True, out_dtype=BF16, name="mm_out_dw")

    def merge_bwd_body(i, ga, gb, pav, pbv, dmv):
        _, vjp = jax.vjp(_merge_math, ga, gb, pav, pbv)
        dga, dgb, dpa, dpb = vjp(dmv)
        return [dga, dgb, dpa, dpb], []

    dga, dgb, dpa, dpb = _rowcall("merge_bwd", T, tb, merge_rows + [_rows(dm, tb, D)], merge_bwd_body,
                                  [(D, BF16), (D, BF16), (D, BF16), (D, BF16)])
    doa = _mm(dpa, wpf, tb=True, name="mm_proj_fox_dx")
    dwpf = _mm(o_a, dpa, ta=True, out_dtype=BF16, name="mm_proj_fox_dw")
    dob = _mm(dpb, wpr, tb=True, name="mm_proj_rwkv_dx")
    dwpr = _mm(o_b, dpb, ta=True, out_dtype=BF16, name="mm_proj_rwkv_dw")

    do_b, dza, delta128 = _rowcall("fox_attn_pre", T, tb, [_rows(doa, tb, 1024), _rows(u, tb, 1024, 3), _rows(o_raw, tb, 1024)],
                                   lambda i, *a: (list(_attn_pre_math(*a)), []), [(1024, BF16), (1024, F32), (LANE, F32)])
    delta = delta128[:, 0:8].T.reshape(8, T, 1)
    dq, dk, dv, dc3 = _attn_bwd(u, c3, lse, do_b, delta, T)
    dfl, dbias_p = _gates_bwd(dc3.reshape(8, T), u, bias_pad, T, F_CB)

    def post_bwd_body(i, yv, rv, kpv, vv, zv, lnw, lnb, rkv, dobv):
        _, vjp = jax.vjp(_post_math, yv, rv, kpv, vv, zv, lnw, lnb, rkv)
        dy_, dr_, dkp_, dv_, dz_, dlnw, dlnb, drk = vjp(dobv)
        return [dy_, dr_, dkp_, dv_, dz_], [dlnw, dlnb, drk]

    dy_s, dr_p, dkp_p, dv_p, dzb, dlnw_p, dlnb_p, drk_p = _rowcall(
        "rwkv_post_bwd", T, tbh, post_rows + post_par + [_rows(dob, tbh, 1024)], post_bwd_body,
        [(1024, F32)] * 5, [(1, 1024)] * 3)
    dr_s, dlw, dkp_s, dv_s, dan, dbb = _scan_bwd(xs, lw, kp, an, bb, states, invs, dy_s, T)

    def prep_bwd_body(i, xk, xwd, xad, w0, a0, kkw, kaw, w2v, a2v, dlw_, dkp1, dkp2, dan_, dbb_, dr1, dr2, dv1, dv2, dz_):
        _, vjp = jax.vjp(_prep_math, xk, xwd, xad, w0, a0, kkw, kaw, w2v, a2v)
        dxk, dxwd, dxad, dw0, da0, dkk, dka, dw2, da2 = vjp((dlw_, dkp1 + dkp2, dan_, dbb_))
        return [[dr1 + dr2, dxk, dv1 + dv2, dz_], [dxwd, dxad]], [dw0, da0, dkk, dka, dw2, da2]

    cots = [dlw, dkp_s, dkp_p, dan, dbb, dr_s, dr_p, dv_s, dv_p, dzb]
    dxs, dxl, dw0_p, da0_p, dkk_p, dka_p, dw2_p, da2_p = _rowcall(
        "rwkv_prep_bwd", T, tbh, prep_rows + prep_par + [_rows(c_, tbh, 1024) for c_ in cots], prep_bwd_body,
        [(4096, F32), (256, F32)], [(1, 1024)] * 4 + [(LANE, 1024)] * 2)

    def shift_bwd_body(i, dm_, hm, dl_, hl, um, pm, ul, pl_, mm_, ml):
        last = i == T // tbh - 1
        outs, accs = [], []
        for dv_, hv, uv, pv, mv in ((dm_, hm, um, pm, mm_), (dl_, hl, ul, pl_, ml)):
            hv = jnp.where(last, 0.0, hv)
            nxt = pltpu.roll(jnp.concatenate([dv_, hv], axis=0), tbh + SUB - 1, 0)[:tbh]
            pv = jnp.where(i == 0, 0.0, pv)
            prev = pltpu.roll(jnp.concatenate([pv, uv], axis=0), 1, 0)[SUB:]
            outs.append(dv_ * (1.0 - mv) + nxt * mv)
            accs.append(jnp.sum(dv_ * (prev - uv), axis=0, keepdims=True))
        return outs, accs

    def halo_next(arr, w, cb):
        last_blk = T // SUB - 1
        return (arr, (SUB, w), lambda i: (jnp.minimum((i + 1) * (tbh // SUB), last_blk), cb))

    du_b, du_l, dmu_main_p, dmu_lora_p = _rowcall(
        "rwkv_shift_bwd", T, tbh,
        [_rows(dxs, tbh, 4096), halo_next(dxs, 4096, 0), _rows(dxl, tbh, 256), halo_next(dxl, 256, 0),
         _rows(u, tbh, 4096, 1), halo_prev(u, 4096, 1), _rows(u, tbh, 256, LORA_CB), halo_prev(u, 256, LORA_CB),
         _whole(mu_main), _whole(mu_lora)],
        shift_bwd_body, [(4096, BF16), (256, BF16)], [(1, 4096), (1, 256)])

    lora_g = jnp.concatenate([dw2_p[:96].reshape(96, N_DEV, 128).transpose(1, 0, 2),
                              da2_p[:96].reshape(96, N_DEV, 128).transpose(1, 0, 2)], axis=2).astype(BF16)
    gb = jnp.concatenate([dwpf.reshape(1024, N_DEV, 256).transpose(1, 0, 2),
                          dwpr.reshape(1024, N_DEV, 256).transpose(1, 0, 2),
                          dwo.reshape(N_DEV, 2048, 256), lora_g, jnp.zeros((N_DEV, RB - 4192, 256), BF16)], axis=1)
    xx, yy, cc = _place()
    chip_slots = jnp.stack([_slot((*chip, cc)) for chip in _other_chips(xx, yy)]).astype(jnp.int32)
    core = jnp.stack([cc]).astype(jnp.int32)
    du, r1b = _rowcall("assemble_du", T, tb,
                       [_rows(a_, tb, a_.shape[1]) for a_ in (dq, dk, dv, dza, du_b, dga, dgb, dfl, du_l)],
                       lambda i, *a: ([list(a[:8]) + [jnp.zeros((tb, LANE), BF16), a[8]]], []), [(NP, BF16)],
                       side=_exchange_pair_side(gb))
    sb = _pair_add("pair_add_packed", gb, r1b, chip_slots, 384)
    dw_pad_t, t1b = _mm(du, h, ta=True, out_dtype=BF16, tm=1280, tn=1024, name="mm_in_dw",
                        side=_exchange_first_side(sb))
    pb = _axis_add("axis_add_packed", sb, t1b, core, 384)
    ga, t2b = _split_rows(dw_pad_t, [(d, lo, pad_col, n) for pad_col, d, lo, n in pieces], NI,
                          _exchange_second_side(pb))
    r1a = _exchange_pair(ga)
    sa = _pair_add("pair_add_w_in", ga, r1a, chip_slots, 512, by_cols=True)
    ga_own = lax.dynamic_index_in_dim(ga, me, 0, keepdims=True)
    gb_own = lax.dynamic_index_in_dim(gb, me, 0, keepdims=True)
    dh, t1a = _mm(du, w_pad, tb=True, tm=1024, tn=1024, tk=NP // 10, name="mm_in_dx", side=_exchange_first_side(sa))
    pa = _axis_add("axis_add_w_in", sa, t1a, core, 512, by_cols=True)

    def rms_bwd_body(i, xv, g, dhv, doutv):
        _, vjp = jax.vjp(_rms_math, xv, g)
        dx_, dg_ = vjp(dhv)
        return [dx_ + doutv], [dg_]

    grad_x2, dng_p, t2a = _rowcall(
        "rms_bwd", T, tb, [_rows(x2, tb, D), _whole(norm_gain), _rows(dh, tb, D), _rows(dout, tb, D)],
        rms_bwd_body, [(D, F32)], [(1, D)], side=_exchange_second_side(pa))

    w_in_outs = _adamw("adamw_w_in", w_in[0].T, m_w_in[0].T, v_w_in[0].T, [(ga_own, 1), (r1a, 1), (t1a, 1), (t2a, 1)], 256, by_cols=True)
    g_in, d_in, m_in, v_in = [o.T for o in w_in_outs]

    dmu = jnp.concatenate([dmu_main_p, dmu_lora_p[:, 0:96], dmu_lora_p[:, 128:224]], axis=1)
    small_parts = [dng_p, dbias_p[:, 0:8], dmu, dw0_p, da0_p, dkk_p, dka_p, drk_p, dlnw_p, dlnb_p, dgf_p, loss_p]
    SR = 128
    small = _pad_cols(jnp.concatenate(small_parts, axis=1), SR * LANE).reshape(SR, LANE)
    rs = _allgather(small, "allgather_small")
    pk = lambda pf, pr, wo_, w2_, a2_: _pack_b(pf[0], pr[0], wo_[0], w2_[0], a2_[0], RB)
    outs_b = _adamw("adamw_packed", pk(w_proj_fox, w_proj_rwkv, w_out, rwkv_w2, rwkv_a2),
                    pk(m_w_proj_fox, m_w_proj_rwkv, m_w_out, m_rwkv_w2, m_rwkv_a2),
                    pk(v_w_proj_fox, v_w_proj_rwkv, v_w_out, v_rwkv_w2, v_rwkv_a2), [(gb_own, 1), (r1b, 1), (t1b, 1), (t2b, 1)], 384)

    def pack_small(ng, fb, sm, w0, a0, kk_, ka_, rk_, lnw, lnb, fg):
        parts = [ng, fb, sm, w0, a0, kk_, ka_, rk_.reshape(1, 1024), lnw, lnb, fg.reshape(1, D), jnp.zeros((1, 1), F32)]
        return _pad_cols(jnp.concatenate(parts, axis=1), SR * LANE).reshape(SR, LANE)

    outs_s = _adamw("adamw_small",
                    pack_small(norm_gain, fox_forget_bias, rwkv_shift_mix, rwkv_w0, rwkv_a0, rwkv_k_k, rwkv_k_a, rwkv_r_k,
                               rwkv_ln_w, rwkv_ln_b, final_norm_gain),
                    pack_small(m_norm_gain, m_fox_forget_bias, m_rwkv_shift_mix, m_rwkv_w0, m_rwkv_a0, m_rwkv_k_k, m_rwkv_k_a,
                               m_rwkv_r_k, m_rwkv_ln_w, m_rwkv_ln_b, m_final_norm_gain),
                    pack_small(v_norm_gain, v_fox_forget_bias, v_rwkv_shift_mix, v_rwkv_w0, v_rwkv_a0, v_rwkv_k_k, v_rwkv_k_a,
                               v_rwkv_r_k, v_rwkv_ln_w, v_rwkv_ln_b, v_final_norm_gain),
                    [(rs, N_DEV)], SR)

    def unpack_b(pkd):
        return dict(w_proj_fox=pkd[0:1024][None], w_proj_rwkv=pkd[1024:2048][None], w_out=pkd[2048:4096].reshape(1, 256, D),
                    rwkv_w2=pkd[4096:4192, 0:128][None], rwkv_a2=pkd[4096:4192, 128:256][None])

    def unpack_s(pkd):
        flat = pkd.reshape(1, SR * LANE)
        names = [("norm_gain", D), ("fox_forget_bias", 8), ("rwkv_shift_mix", 4288), ("rwkv_w0", 1024), ("rwkv_a0", 1024),
                 ("rwkv_k_k", 1024), ("rwkv_k_a", 1024), ("rwkv_r_k", 1024), ("rwkv_ln_w", 1024), ("rwkv_ln_b", 1024),
                 ("final_norm_gain", D), ("loss", 1)]
        out, off = {}, 0
        for nm, n in names:
            out[nm] = flat[:, off:off + n]
            off += n
        out["rwkv_r_k"] = out["rwkv_r_k"].reshape(1, 16, 64)
        out["final_norm_gain"] = out["final_norm_gain"].reshape(D)
        return out

    order = ["norm_gain", "w_in", "fox_forget_bias", "rwkv_shift_mix", "rwkv_w0", "rwkv_w2", "rwkv_a0", "rwkv_a2", "rwkv_k_k",
             "rwkv_k_a", "rwkv_r_k", "rwkv_ln_w", "rwkv_ln_b", "w_proj_fox", "w_proj_rwkv", "w_out", "final_norm_gain"]
    result = []
    loss = None
    for kind, big in enumerate((g_in, d_in, m_in, v_in)):
        d = {**unpack_b(outs_b[kind]), **unpack_s(outs_s[kind]), "w_in": big[None]}
        if kind == 0:
            loss = d["loss"].reshape(())
        result += [d[n] for n in order]
    return (loss, grad_x2[None], *result)
```

```python
import functools

import jax
import jax.numpy as jnp
from jax import lax
from jax.experimental import pallas as pl
from jax.experimental.pallas import tpu as pltpu

F32 = jnp.float32
BF16 = jnp.bfloat16
HI = lax.Precision.HIGHEST
H3 = lax.Precision.HIGH
MESH = pl.DeviceIdType.MESH

FOX_HD = 128
RW_HD = 64
RMS_EPS = 1e-6
GN_EPS = 64e-5
L2_EPS = 1e-12
ADAM_LR = 0.001
ADAM_B1 = 0.9
ADAM_B2 = 0.999
ADAM_EPS = 1e-08
ADAM_WD = 0.01
ADAM_STEP = 10

LANE = 128
SUB = 8
VMEM_LIMIT = 56 * 1024 * 1024
N_DEV = 8
CHUNK = 128
SCAN_GROUP = 2
SCAN_PAIRS = 2
PS = None
NEG = -1e30


def _scan_shape(T):
    c = min(CHUNK, T)
    return c, min(SCAN_GROUP, T // c)


def _cp(sem=None):
    return pltpu.CompilerParams(dimension_semantics=sem, vmem_limit_bytes=VMEM_LIMIT)


def _sigmoid(x):
    return jax.nn.sigmoid(x)


def _softplus(x):
    return jnp.maximum(x, 0.0) + jnp.log(1.0 + jnp.exp(-jnp.abs(x)))


def _nn(a, b, prec=None):
    return lax.dot_general(a, b, (((1,), (0,)), ((), ())), precision=prec, preferred_element_type=F32)


def _nt(a, b, prec=None):
    return lax.dot_general(a, b, (((1,), (1,)), ((), ())), precision=prec, preferred_element_type=F32)


def _tn(a, b, prec=None):
    return lax.dot_general(a, b, (((0,), (0,)), ((), ())), precision=prec, preferred_element_type=F32)


def _iota2(shape, dim):
    return lax.broadcasted_iota(jnp.int32, shape, dim)


def _seg_sum(x):
    r = _iota2((LANE, LANE), 0) // RW_HD
    c = _iota2((LANE, LANE), 1) // RW_HD
    bd = (r == c).astype(F32)
    parts = [_nn(x[:, j * LANE:(j + 1) * LANE], bd, H3) for j in range(x.shape[1] // LANE)]
    return parts[0] if len(parts) == 1 else jnp.concatenate(parts, axis=1)


def _mm(a, b, *, ta=False, tb=False, out_dtype=F32, tm=1024, tn=1024, tk=None, name, side=None):
    assert not (ta and tb)
    K, M = a.shape if ta else a.shape[::-1]
    N = b.shape[0] if tb else b.shape[1]
    tm, tn = min(tm, M), min(tn, N)
    tk = K if tk is None else tk
    nk = K // tk
    assert M % tm == 0 and N % tn == 0 and K % tk == 0
    gi, gj = M // tm, N // tn
    a_spec = pl.BlockSpec((tk, tm), lambda i, j, k: (k, i)) if ta else pl.BlockSpec((tm, tk), lambda i, j, k: (i, k))
    b_spec = pl.BlockSpec((tn, tk), lambda i, j, k: (j, k)) if tb else pl.BlockSpec((tk, tn), lambda i, j, k: (k, j))
    side_ins, side_outs, side_scr, side_make = side if side is not None else ((), (), (), None)
    n_si, n_so = len(side_ins), len(side_outs)
    n_acc = 0 if nk == 1 else 1

    def body(*refs):
        a_ref, b_ref = refs[:2]
        o_ref = refs[2 + n_si]
        scr = refs[3 + n_si + n_so:]
        k = pl.program_id(2)
        if side_make is not None:
            start, finish = side_make(refs[2:2 + n_si], refs[3 + n_si:3 + n_si + n_so], scr[n_acc:])
            first = jnp.logical_and(jnp.logical_and(pl.program_id(0) == 0, pl.program_id(1) == 0), k == 0)
            last = jnp.logical_and(jnp.logical_and(pl.program_id(0) == gi - 1, pl.program_id(1) == gj - 1), k == nk - 1)
            pl.when(first)(start)
        av = a_ref[...].astype(BF16)
        bv = b_ref[...].astype(BF16)
        p = _tn(av, bv) if ta else _nt(av, bv) if tb else _nn(av, bv)
        if nk == 1:
            o_ref[...] = p.astype(out_dtype)
        else:
            acc_ref = scr[0]

            @pl.when(k == 0)
            def _():
                acc_ref[...] = p

            @pl.when(k > 0)
            def _():
                acc_ref[...] += p

            @pl.when(k == nk - 1)
            def _():
                o_ref[...] = acc_ref[...].astype(out_dtype)
        if side_make is not None:
            pl.when(last)(finish)

    any_spec = pl.BlockSpec(memory_space=pl.ANY)
    res = pl.pallas_call(
        body, name=name,
        out_shape=[jax.ShapeDtypeStruct((M, N), out_dtype)] + list(side_outs),
        grid=(gi, gj, nk),
        in_specs=[a_spec, b_spec] + [any_spec] * n_si,
        out_specs=[pl.BlockSpec((tm, tn), lambda i, j, k: (i, j))] + [any_spec] * n_so,
        scratch_shapes=([] if nk == 1 else [pltpu.VMEM((tm, tn), F32)]) + list(side_scr),
        compiler_params=_cp(("arbitrary",) * 3 if side is not None else ("parallel", "parallel", "arbitrary")),
    )(a, b, *side_ins)
    return res if side is not None else res[0]


def _rows(arr, tb, w, cb=0):
    return (arr, (tb, w), lambda i: (i, cb))


def _whole(arr):
    nd = arr.ndim
    return (arr, arr.shape, lambda i: (0,) * nd)


def _rowcall(name, T, tb, ins, body, outs, accs=(), side=None):
    n_in, n_out, n_acc = len(ins), len(outs), len(accs)
    side_ins, side_outs, side_scr, side_make = side if side is not None else ((), (), (), None)
    n_si = len(side_ins)

    def kern(*refs):
        i = pl.program_id(0)
        if side_make is not None:
            side_refs = refs[n_in + n_si + n_out + n_acc:]
            start, finish = side_make(refs[n_in:n_in + n_si], side_refs[:len(side_outs)], side_refs[len(side_outs):])
            pl.when(i == 0)(start)
            refs = refs[:n_in] + refs[n_in + n_si:]
        vals = [r[...] for r in refs[:n_in]]
        ro, ao = body(i, *vals)
        for r, v in zip(refs[n_in:n_in + n_out], ro):
            if isinstance(v, (list, tuple)):
                off = 0
                for piece in v:
                    w = piece.shape[1]
                    r[:, off:off + w] = piece.astype(r.dtype)
                    off += w
            else:
                r[...] = v.astype(r.dtype)
        if accs:
            acc_refs = refs[n_in + n_out:n_in + n_out + n_acc]

            @pl.when(i == 0)
            def _():
                for r in acc_refs:
                    r[...] = jnp.zeros(r.shape, F32)

            for r, v in zip(acc_refs, ao):
                r[...] += v
        if side_make is not None:
            pl.when(i == T // tb - 1)(finish)

    any_spec = pl.BlockSpec(memory_space=pl.ANY)
    out_shape = [jax.ShapeDtypeStruct((T, w), dt) for (w, dt) in outs] + [jax.ShapeDtypeStruct(s, F32) for s in accs]
    out_specs = [pl.BlockSpec((tb, w), lambda i: (i, 0)) for (w, dt) in outs] + [pl.BlockSpec(s, lambda i: (0, 0)) for s in accs]
    res = pl.pallas_call(
        kern, name=name,
        out_shape=out_shape + list(side_outs),
        grid=(T // tb,),
        in_specs=[pl.BlockSpec(bs, im) for (_, bs, im) in ins] + [any_spec] * n_si,
        out_specs=out_specs + [any_spec] * len(side_outs),
        scratch_shapes=list(side_scr),
        compiler_params=_cp(("arbitrary",)),
    )(*[a for (a, _, _) in ins], *side_ins)
    return res


def _rms_math(x, g):
    r = lax.rsqrt(jnp.mean(x * x, axis=-1, keepdims=True) + RMS_EPS)
    return x * r * g


def _merge_math(ga, gb, pa, pb):
    return _sigmoid(ga) * pa + _sigmoid(gb) * pb


def _prep_math(xk, xwd, xad, w0, a0, kk_w, ka_w, w2p, a2p):
    z = w0 + _nn(jnp.tanh(xwd), w2p, H3)
    w = -_softplus(-z) - 0.5
    lw = -jnp.exp(w)
    ag = _sigmoid(a0 + _nn(xad, a2p, H3))
    p = xk * kk_w
    n = jnp.maximum(jnp.sqrt(_seg_sum(p * p)), L2_EPS)
    kk = p / n
    kp = xk * (1.0 + (ag - 1.0) * ka_w)
    return lw, kp, -kk, kk * ag


def _post_math(y, r, kp, v, z, lnw, lnb, rk):
    inv = 1.0 / RW_HD
    mu = _seg_sum(y) * inv
    d = y - mu
    var = _seg_sum(d * d) * inv
    yn = d * lax.rsqrt(var + GN_EPS) * lnw + lnb
    bonus = _seg_sum(r * kp * rk) * v
    return (yn + bonus) * (z * _sigmoid(z))


def _neumann(ms, depth):
    eye = (_iota2(ms[0].shape, 0) == _iota2(ms[0].shape, 1)).astype(F32)
    width = ms[0].shape[1]
    mp = [_nn(m, m, PS) for m in ms]
    inv = [eye + m for m in ms]
    n = 2
    while n < depth:
        last = 2 * n >= depth
        for i in range(len(ms)):
            if last:
                inv[i] = inv[i] + _nn(mp[i], inv[i], PS)
            else:
                z = _nn(mp[i], jnp.concatenate([mp[i], inv[i]], axis=1), PS)
                mp[i], inv[i] = z[:, :width], inv[i] + z[:, width:]
        n *= 2
    return inv


@functools.partial(jax.custom_vjp, nondiff_argnums=(1,))
def _unit_inverses(ms, depth):
    if depth <= RW_HD:
        return tuple(_neumann(list(ms), depth))
    h, n = depth // 2, len(ms)
    cat = jnp.concatenate
    z = jnp.zeros((h, h), F32)

    def heads(m, r, c):
        b0 = m[r * h:(r + 1) * h, c * h:(c + 1) * h]
        b1 = m[depth + r * h:depth + (r + 1) * h, depth + c * h:depth + (c + 1) * h]
        return cat([cat([b0, z], axis=1), cat([z, b1], axis=1)], axis=0)

    diag = _neumann([heads(m, 0, 0) for m in ms] + [heads(m, 1, 1) for m in ms], h)
    ta, td = diag[:n], diag[n:]
    low = [_nn(heads(m, 1, 0), a, PS) for m, a in zip(ms, ta)]
    low = [_nn(d, x, PS) for d, x in zip(td, low)]
    out = []
    for a, x, d in zip(ta, low, td):
        rows = []
        for hd in (0, 1):
            sl = slice(hd * h, (hd + 1) * h)
            top, bot = [a[sl, sl], z], [x[sl, sl], d[sl, sl]]
            pad = [z, z]
            rows.append(cat(top + pad if hd == 0 else pad + top, axis=1))
            rows.append(cat(bot + pad if hd == 0 else pad + bot, axis=1))
        out.append(cat(rows, axis=0))
    return tuple(out)


def _unit_inverses_fwd(ms, depth):
    inv = _unit_inverses(ms, depth)
    return inv, inv


def _unit_inverses_bwd(depth, inv, cts):
    left = [_tn(t, g, PS) for t, g in zip(inv, cts)]
    return (tuple(_nt(l, t, PS) for l, t in zip(left, inv)),)


_unit_inverses.defvjp(_unit_inverses_fwd, _unit_inverses_bwd)


@jax.custom_vjp
def _known_inverses(ms, inv):
    return inv


def _known_inverses_fwd(ms, inv):
    return inv, inv


def _known_inverses_bwd(inv, cts):
    dms = _unit_inverses_bwd(None, inv, cts)[0]
    return dms, tuple(jnp.zeros_like(t) for t in inv)


_known_inverses.defvjp(_known_inverses_fwd, _known_inverses_bwd)


def _scan_group(s0s, *flat, known_inv=None, with_inv=False):
    P = len(s0s)
    G = len(flat) // (6 * P)
    ch = [flat[6 * i:6 * i + 6] for i in range(P * G)]
    C = ch[0][0].shape[0]
    C2 = 2 * C
    cat = jnp.concatenate
    m0 = _iota2((1, LANE), 1) < RW_HD
    mask0 = m0.astype(F32)
    mask1 = 1.0 - mask0
    r2 = _iota2((C2, C2), 0)
    c2 = _iota2((C2, C2), 1)
    dist = r2 - c2
    in_head = dist <= r2 % C
    lower = (_iota2((C, C), 0) >= _iota2((C, C), 1)).astype(F32)
    bd = (_iota2((LANE, LANE), 0) // RW_HD) == (_iota2((LANE, LANE), 1) // RW_HD)

    def tri(m, strict):
        return jnp.where(dist > 0 if strict else dist >= 0, jnp.where(in_head, m, 0.0), 0.0)

    def sel(z):
        return jnp.where(m0, z[:C], z[C:])

    gs = [_nn(lower, c[1], HI) for c in ch]
    pre = []
    for (r, lw, k, v, a, b), g in zip(ch, gs):
        g_end = jnp.sum(lw, axis=0, keepdims=True)
        gm = g - jnp.sum(lw[:C // 2], axis=0, keepdims=True)
        en = jnp.exp(-gm)
        ec = jnp.exp(g_end - g)
        pre.append(dict(at=a * jnp.exp(g - lw), rt=r * jnp.exp(g), am=a * jnp.exp(gm - lw), rm=r * jnp.exp(gm),
                        bt=b * en, kt=k * en, bh=b * ec, kh=k * ec, dec=jnp.exp(g_end), v=v))
    grams = [_nt(cat([p["am"] * mask0, p["am"] * mask1, p["rm"] * mask0, p["rm"] * mask1], axis=0),
                 cat([p["bt"], p["bt"], p["kt"], p["kt"]], axis=0), PS) for p in pre]
    mab = tuple(tri(gm[:C2, :C2], True) for gm in grams)
    tinv = _unit_inverses(mab, C) if known_inv is None else _known_inverses(mab, known_inv)
    xv =[sel(_nn(tri(gm[:C2, C2:], True), cat([p["v"], p["v"]], axis=0), PS)) for gm, p in zip(grams, pre)]
    ys, s = [None] * (P * G), list(s0s)
    for i in range(G):
        for q in range(P):
            n = q * G + i
            p, gm = pre[n], grams[n]
            sx = _nt(cat([p["at"], p["rt"]], axis=0), s[q], PS)
            x = sx[:C] + xv[n]
            u = sel(_nn(tinv[n], cat([x, x], axis=0), PS))
            v = p["v"]
            ys[n] = sx[C:] + sel(_nn(cat([tri(gm[C2:, :C2], False), tri(gm[C2:, C2:], False)], axis=1),
                                     cat([u, u, v, v], axis=0), PS))
            s[q] = s[q] * p["dec"] + jnp.where(bd, _tn(cat([u, v], axis=0), cat([p["bh"], p["kh"]], axis=0), PS), 0.0)
    return (tuple(ys), tuple(s), tinv) if with_inv else (tuple(ys), tuple(s))


def _scan_fwd(xs, lw, kp, an, bb, T):
    C, G = _scan_shape(T)
    P = SCAN_PAIRS
    nc = T // (C * G)
    npair = 1024 // LANE

    def kern(r_ref, lw_ref, k_ref, v_ref, a_ref, b_ref, y_ref, st_ref, inv_ref, s_scr):
        n = pl.program_id(1)

        @pl.when(n == 0)
        def _():
            s_scr[...] = jnp.zeros(s_scr.shape, F32)

        st_ref[0] = s_scr[...]
        ins = (r_ref, lw_ref, k_ref, v_ref, a_ref, b_ref)
        ys, s1, inv = _scan_group(
            tuple(s_scr[q] for q in range(P)),
            *[ref[i * C:(i + 1) * C, q * LANE:(q + 1) * LANE] for q in range(P) for i in range(G) for ref in ins], with_inv=True)
        for q in range(P):
            for i in range(G):
                y_ref[i * C:(i + 1) * C, q * LANE:(q + 1) * LANE] = ys[q * G + i]
                inv_ref[0, 0, q * G + i] = inv[q * G + i]
            s_scr[q] = s1[q]

    def col(off):
        return pl.BlockSpec((C * G, P * LANE), lambda p, n: (n, off // P + p))

    return pl.pallas_call(
        kern, name="rwkv_scan_fwd",
        out_shape=[jax.ShapeDtypeStruct((T, 1024), F32), jax.ShapeDtypeStruct((nc, npair, LANE, LANE), F32),
                   jax.ShapeDtypeStruct((nc, npair // P, P * G, 2 * C, 2 * C), F32)],
        grid=(npair // P, nc),
        in_specs=[col(0), col(0), col(0), col(16), col(0), col(0)],
        out_specs=[col(0), pl.BlockSpec((1, P, LANE, LANE), lambda p, n: (n, p, 0, 0)),
                   pl.BlockSpec((1, 1, P * G, 2 * C, 2 * C), lambda p, n: (n, p, 0, 0, 0))],
        scratch_shapes=[pltpu.VMEM((P, LANE, LANE), F32)],
        compiler_params=_cp(("parallel", "arbitrary")),
    )(xs, lw, kp, xs, an, bb)


def _scan_bwd(xs, lw, kp, an, bb, states, invs, dy, T):
    C, G = _scan_shape(T)
    P = SCAN_PAIRS
    nc = T // (C * G)
    npair = 1024 // LANE

    def kern(r_ref, lw_ref, k_ref, v_ref, a_ref, b_ref, st_ref, inv_ref, dy_ref, dr_ref, dlw_ref, dk_ref, dv_ref, da_ref, db_ref,
             ds_scr):
        n = pl.program_id(1)

        @pl.when(n == 0)
        def _():
            ds_scr[...] = jnp.zeros(ds_scr.shape, F32)

        ins = (r_ref, lw_ref, k_ref, v_ref, a_ref, b_ref)
        units = [(q, i) for q in range(P) for i in range(G)]
        known = tuple(inv_ref[0, 0, q * G + i] for q, i in units)
        _, vjp = jax.vjp(functools.partial(_scan_group, known_inv=known), tuple(st_ref[0, q] for q in range(P)),
                         *[ref[i * C:(i + 1) * C, q * LANE:(q + 1) * LANE] for q, i in units for ref in ins])
        grads = vjp((tuple(dy_ref[i * C:(i + 1) * C, q * LANE:(q + 1) * LANE] for q, i in units),
                     tuple(ds_scr[q] for q in range(P))))
        for q in range(P):
            ds_scr[q] = grads[0][q]
        outs = (dr_ref, dlw_ref, dk_ref, dv_ref, da_ref, db_ref)
        for n_, (q, i) in enumerate(units):
            for t, ref in enumerate(outs):
                ref[i * C:(i + 1) * C, q * LANE:(q + 1) * LANE] = grads[1 + 6 * n_ + t]

    def col(off):
        return pl.BlockSpec((C * G, P * LANE), lambda p, n: (nc - 1 - n, off // P + p))

    return pl.pallas_call(
        kern, name="rwkv_scan_bwd",
        out_shape=[jax.ShapeDtypeStruct((T, 1024), F32)] * 6,
        grid=(npair // P, nc),
        in_specs=[col(0), col(0), col(0), col(16), col(0), col(0),
                  pl.BlockSpec((1, P, LANE, LANE), lambda p, n: (nc - 1 - n, p, 0, 0)),
                  pl.BlockSpec((1, 1, P * G, 2 * C, 2 * C), lambda p, n: (nc - 1 - n, p, 0, 0, 0)), col(0)],
        out_specs=[col(0)] * 6,
        scratch_shapes=[pltpu.VMEM((P, LANE, LANE), F32)],
        compiler_params=_cp(("parallel", "arbitrary")),
    )(xs, lw, kp, xs, an, bb, states, invs, dy)


def _gates_fwd(u, bias_pad, T, f_cb):
    nb = T // LANE

    def kern(f_ref, b_ref, c_ref):
        x = f_ref[...] + b_ref[...]
        lf = jnp.minimum(x, 0.0) - jnp.log(1.0 + jnp.exp(-jnp.abs(x)))
        lft = lf.T
        ut = (_iota2((LANE, LANE), 0) <= _iota2((LANE, LANE), 1)).astype(F32)
        carry = jnp.zeros((LANE, 1), F32)
        for blk in range(nb):
            seg = lft[:, blk * LANE:(blk + 1) * LANE]
            cs = _nn(seg, ut, HI) + carry
            c_ref[:, blk * LANE:(blk + 1) * LANE] = cs[:SUB, :]
            carry = carry + jnp.sum(seg, axis=1, keepdims=True)

    return pl.pallas_call(
        kern, name="fox_gates_fwd",
        out_shape=jax.ShapeDtypeStruct((SUB, T), F32),
        grid=(1,),
        in_specs=[pl.BlockSpec((T, LANE), lambda i: (0, f_cb)), pl.BlockSpec((1, LANE), lambda i: (0, 0))],
        out_specs=pl.BlockSpec((SUB, T), lambda i: (0, 0)),
        compiler_params=_cp(("arbitrary",)),
    )(u, bias_pad)


def _gates_bwd(dc, u, bias_pad, T, f_cb):
    nb = T // LANE

    def kern(dc_ref, f_ref, b_ref, dfl_ref, db_ref):
        dcv = jnp.concatenate([dc_ref[...], jnp.zeros((LANE - SUB, T), F32)], axis=0)
        lt = (_iota2((LANE, LANE), 0) >= _iota2((LANE, LANE), 1)).astype(F32)
        carry = jnp.zeros((LANE, 1), F32)
        pieces = [None] * nb
        for blk in range(nb - 1, -1, -1):
            seg = dcv[:, blk * LANE:(blk + 1) * LANE]
            pieces[blk] = _nn(seg, lt, HI) + carry
            carry = carry + jnp.sum(seg, axis=1, keepdims=True)
        dlf = (pieces[0] if nb == 1 else jnp.concatenate(pieces, axis=1)).T
        x = f_ref[...] + b_ref[...]
        dfl = dlf * _sigmoid(-x)
        dfl_ref[...] = dfl
        db_ref[...] = jnp.sum(dfl, axis=0, keepdims=True)

    return pl.pallas_call(
        kern, name="fox_gates_bwd",
        out_shape=[jax.ShapeDtypeStruct((T, LANE), F32), jax.ShapeDtypeStruct((1, LANE), F32)],
        grid=(1,),
        in_specs=[pl.BlockSpec((SUB, T), lambda i: (0, 0)), pl.BlockSpec((T, LANE), lambda i: (0, f_cb)),
                  pl.BlockSpec((1, LANE), lambda i: (0, 0))],
        out_specs=[pl.BlockSpec((T, LANE), lambda i: (0, 0)), pl.BlockSpec((1, LANE), lambda i: (0, 0))],
        compiler_params=_cp(("arbitrary",)),
    )(dc, u, bias_pad)


ATTN_HEADS = 4


def _attn_block(T):
    return 256 if T % 256 == 0 and T >= 512 else 128


def _attn_fwd(u, c3, T):
    H, HP = 8, ATTN_HEADS
    bq = _attn_block(T)
    nq = T // bq
    scale = FOX_HD ** -0.5
    lanes = [slice(h * LANE, (h + 1) * LANE) for h in range(HP)]

    def kern(q_ref, k_ref, v_ref, z_ref, cq_ref, ck_ref, o_ref, oa_ref, lse_ref):
        i = pl.program_id(1)
        q = [(q_ref[:, ln] * scale).astype(BF16) for ln in lanes]
        c0 = [cq_ref[h][:, 0:1] for h in range(HP)]

        def step(j, carry, diagonal=False):
            off = pl.multiple_of(j * bq, bq)
            s = [_nt(q[h], k_ref[pl.ds(off, bq), lanes[h]].astype(BF16)) + (c0[h] - ck_ref[h, :, pl.ds(off, bq)])
                 for h in range(HP)]
            ps, out = [], []
            for h in range(HP):
                m, l, acc = carry[h]
                sh = s[h]
                if diagonal:
                    sh = jnp.where(_iota2((bq, bq), 1) <= _iota2((bq, bq), 0), sh, NEG)
                m_new = jnp.maximum(m, jnp.max(sh, axis=1, keepdims=True))
                p = jnp.exp(sh - m_new)
                alpha = jnp.exp(m - m_new)
                p_hi = p.astype(BF16)
                ps.append((p_hi, (p - p_hi.astype(F32)).astype(BF16)))
                out.append((m_new, alpha * l + jnp.sum(p, axis=1, keepdims=True), alpha * acc))
            res = []
            for h, (m, l, acc) in enumerate(out):
                vj = v_ref[pl.ds(off, bq), lanes[h]].astype(BF16)
                res.append((m, l, acc + _nn(ps[h][0], vj) + _nn(ps[h][1], vj)))
            return tuple(res)

        init = tuple((jnp.full((bq, 1), NEG, F32), jnp.zeros((bq, 1), F32), jnp.zeros((bq, FOX_HD), F32)) for _ in range(HP))
        res = step(i, lax.fori_loop(0, i, step, init), diagonal=True)
        for h, (m, l, acc) in enumerate(res):
            o = acc / l
            z = z_ref[:, lanes[h]]
            o_ref[:, lanes[h]] = o
            oa_ref[:, lanes[h]] = (o * z * _sigmoid(z)).astype(BF16)
            lse_ref[h] = m + jnp.log(l)

    W = HP * LANE
    return pl.pallas_call(
        kern, name="fox_attn_fwd",
        out_shape=[jax.ShapeDtypeStruct((T, 1024), F32), jax.ShapeDtypeStruct((T, 1024), BF16),
                   jax.ShapeDtypeStruct((H, T, 1), F32)],
        grid=(H // HP, nq),
        in_specs=[pl.BlockSpec((bq, W), lambda g, i: (i, g)),
                  pl.BlockSpec((T, W), lambda g, i: (0, 8 // HP + g)),
                  pl.BlockSpec((T, W), lambda g, i: (0, 16 // HP + g)),
                  pl.BlockSpec((bq, W), lambda g, i: (i, 24 // HP + g)),
                  pl.BlockSpec((HP, 1, bq), lambda g, i: (g, 0, i)),
                  pl.BlockSpec((HP, 1, T), lambda g, i: (g, 0, 0))],
        out_specs=[pl.BlockSpec((bq, W), lambda g, i: (i, g)),
                   pl.BlockSpec((bq, W), lambda g, i: (i, g)),
                   pl.BlockSpec((HP, bq, 1), lambda g, i: (g, i, 0))],
        compiler_params=_cp(("parallel", "arbitrary")),
    )(u, u, u, u, c3, c3)


def _attn_probs(s, lse_i, diagonal):
    if not diagonal:
        return jnp.exp(s - lse_i)
    keep = _iota2(s.shape, 1) <= _iota2(s.shape, 0)
    return jnp.where(keep, jnp.exp(jnp.where(keep, s, NEG) - lse_i), 0.0)


def _attn_pre_math(doa, z, o):
    sg = _sigmoid(z)
    do = (doa * z * sg).astype(BF16)
    dz = doa * o * (sg * (1.0 + z * (1.0 - sg)))
    head_of = (_iota2((o.shape[1], LANE), 0) // FOX_HD == _iota2((o.shape[1], LANE), 1)).astype(F32)
    return do, dz, _nn(do.astype(F32) * o, head_of, HI)


def _attn_bwd(u, c3, lse, do, delta, T):
    H, HP = 8, ATTN_HEADS
    bq = _attn_block(T)
    nq = T // bq
    scale = FOX_HD ** -0.5
    lanes = [slice(h * LANE, (h + 1) * LANE) for h in range(HP)]

    def kern(q_ref, k_ref, v_ref, c_ref, lse_ref, do_ref, dl_ref, dq_ref, dk_ref, dv_ref, dc_ref):
        j = pl.program_id(1)

        @pl.when(j == 0)
        def _():
            dq_ref[...] = jnp.zeros(dq_ref.shape, F32)

        kj = [k_ref[:, ln].astype(BF16) for ln in lanes]
        vj = [v_ref[:, ln].astype(BF16) for ln in lanes]
        joff = pl.multiple_of(j * bq, bq)
        ck = [c_ref[h, :, pl.ds(joff, bq)] for h in range(HP)]

        def step(i, carry, diagonal=False):
            off = pl.multiple_of(i * bq, bq)
            qs = [(q_ref[pl.ds(off, bq), ln] * scale).astype(BF16) for ln in lanes]
            dob = [do_ref[pl.ds(off, bq), ln] for ln in lanes]
            s = [_nt(qs[h], kj[h]) + (c_ref[h, :, pl.ds(off, bq)][:, 0:1] - ck[h]) for h in range(HP)]
            dp = [_nt(dob[h], vj[h]) for h in range(HP)]
            pb, dsb, dcs = [], [], []
            for h in range(HP):
                p = _attn_probs(s[h], lse_ref[h, pl.ds(off, bq), :], diagonal)
                ds = p * (dp[h] - dl_ref[h, pl.ds(off, bq), :])
                pb.append(p.astype(BF16))
                dsb.append(ds.astype(BF16))
                dcs.append(jnp.sum(ds, axis=0, keepdims=True))
            out = []
            for h, (dk, dv, dc) in enumerate(carry):
                dq_ref[pl.ds(off, bq), lanes[h]] += _nn(dsb[h], kj[h]) * scale
                out.append((dk + _tn(dsb[h], qs[h]), dv + _tn(pb[h], dob[h]), dc - dcs[h]))
            return tuple(out)

        init = tuple((jnp.zeros((bq, FOX_HD), F32), jnp.zeros((bq, FOX_HD), F32), jnp.zeros((1, bq), F32)) for _ in range(HP))
        res = lax.fori_loop(j + 1, nq, step, step(j, init, diagonal=True))
        for h, (dk, dv, dc) in enumerate(res):
            dk_ref[:, lanes[h]] = dk
            dv_ref[:, lanes[h]] = dv
            dc_ref[h] = dc

    W = HP * LANE
    full = lambda cb: pl.BlockSpec((T, W), lambda g, j: (0, cb // HP + g))
    blk = lambda cb: pl.BlockSpec((bq, W), lambda g, j: (j, cb // HP + g))
    col1 = pl.BlockSpec((HP, T, 1), lambda g, j: (g, 0, 0))
    return pl.pallas_call(
        kern, name="fox_attn_bwd",
        out_shape=[jax.ShapeDtypeStruct((T, 1024), F32)] * 3 + [jax.ShapeDtypeStruct((H, 1, T), F32)],
        grid=(H // HP, nq),
        in_specs=[full(0), blk(8), blk(16), pl.BlockSpec((HP, 1, T), lambda g, j: (g, 0, 0)), col1, full(0), col1],
        out_specs=[full(0), blk(0), blk(0), pl.BlockSpec((HP, 1, bq), lambda g, j: (g, 0, j))],
        compiler_params=_cp(("parallel", "arbitrary")),
    )(u, u, u, c3, lse, do, delta)


def _place():
    return lax.axis_index("x"), lax.axis_index("y"), lax.axis_index("c")


def _slot(p):
    return 4 * p[0] + 2 * p[1] + p[2]


def _other_chips(x, y):
    return [(1 - x, y), (x, 1 - y), (1 - x, 1 - y)]


def _allgather_steps(in_refs, out_refs, scratch):
    (src,), (dst,) = in_refs, out_refs
    send_sems, recv_sems, local_sem = scratch
    x, y, c = _place()
    me, sibling = (x, y, c), (x, y, 1 - c)
    chips = _other_chips(x, y)

    def copy(k, block, to, from_input=False):
        d = dst.at[_slot(block)]
        return pltpu.make_async_remote_copy(
            src_ref=src if from_input else d, dst_ref=d, send_sem=send_sems.at[k], recv_sem=recv_sems.at[k],
            device_id=to, device_id_type=MESH)

    def first_copies():
        return [copy(0, me, sibling, True)] + [copy(1 + j, me, (*chip, c), True) for j, chip in enumerate(chips)]

    def start():
        pltpu.make_async_copy(src, dst.at[_slot(me)], local_sem).start()
        for cp in first_copies():
            cp.start()

    def finish():
        passed = []
        for j, chip in enumerate(chips):
            copy(1 + j, (*chip, c), me).wait_recv()
            passed.append(copy(4 + j, (*chip, c), sibling))
            passed[-1].start()
        copy(0, sibling, me).wait_recv()
        for j, chip in enumerate(chips):
            copy(4 + j, (*chip, 1 - c), me).wait_recv()
        for cp in first_copies() + passed:
            cp.wait_send()
        pltpu.make_async_copy(src, dst.at[_slot(me)], local_sem).wait()

    return start, finish


def _allgather_relay_steps(in_refs, out_refs, scratch):
    (src,), (dst,) = in_refs, out_refs
    send_sems, recv_sems, local_sem = scratch
    x, y, c = _place()
    me, sibling = (x, y, c), (x, y, 1 - c)
    x_nbr, y_nbr, diag = (1 - x, y, c), (x, 1 - y, c), (1 - x, 1 - y, c)
    flip = lambda a, bit: a + bit - 2 * a * bit
    relay_from = (flip(x, 1 - c), flip(y, c), c)
    relay_to = (flip(x, c), flip(y, 1 - c), c)

    def copy(k, block, to, from_input=False):
        d = dst.at[_slot(block)]
        return pltpu.make_async_remote_copy(
            src_ref=src if from_input else d, dst_ref=d, send_sem=send_sems.at[k], recv_sem=recv_sems.at[k],
            device_id=to, device_id_type=MESH)

    def first_copies():
        return [copy(0, me, sibling, True), copy(1, me, x_nbr, True), copy(2, me, y_nbr, True)]

    def other(block):
        return block[:2] + (1 - c,)

    def start():
        pltpu.make_async_copy(src, dst.at[_slot(me)], local_sem).start()
        for cp in first_copies():
            cp.start()

    def finish():
        copy(1, x_nbr, me).wait_recv()
        copy(2, y_nbr, me).wait_recv()
        later = [copy(3, relay_from, relay_to), copy(4, x_nbr, sibling), copy(5, y_nbr, sibling)]
        for cp in later:
            cp.start()
        copy(3, diag, me).wait_recv()
        later.append(copy(6, diag, sibling))
        later[-1].start()
        copy(0, sibling, me).wait_recv()
        for k, block in ((4, x_nbr), (5, y_nbr), (6, diag)):
            copy(k, other(block), me).wait_recv()
        for cp in first_copies() + later:
            cp.wait_send()
        pltpu.make_async_copy(src, dst.at[_slot(me)], local_sem).wait()

    return start, finish


def _allgather_side(a, relay=False):
    return ((a,), (jax.ShapeDtypeStruct((N_DEV,) + a.shape, a.dtype),),
            (pltpu.SemaphoreType.DMA((7,)), pltpu.SemaphoreType.DMA((7,)), pltpu.SemaphoreType.DMA),
            _allgather_relay_steps if relay else _allgather_steps)


def _allgather(a, name, relay=False):
    ins, outs, scratch, make = _allgather_side(a, relay)

    def body(a_ref, o_ref, *scr):
        start, finish = make((a_ref,), (o_ref,), scr)
        start()
        finish()

    any_spec = pl.BlockSpec(memory_space=pl.ANY)
    return pl.pallas_call(body, name=name, out_shape=outs[0], in_specs=[any_spec], out_specs=any_spec,
                          scratch_shapes=list(scratch))(a)


def _exchange_pair_steps(in_refs, out_refs, scratch):
    (src,), (dst,) = in_refs, out_refs
    send_sems, recv_sems = scratch
    x, y, c = _place()
    sibling = (x, y, 1 - c)
    slots = [_slot(sibling)] + [_slot((*chip, 1 - c)) for chip in _other_chips(x, y)]

    def copies():
        return [pltpu.make_async_remote_copy(
            src_ref=src.at[ps], dst_ref=dst.at[k], send_sem=send_sems.at[k], recv_sem=recv_sems.at[k],
            device_id=sibling, device_id_type=MESH) for k, ps in enumerate(slots)]

    def start():
        for cp in copies():
            cp.start()

    def finish():
        for cp in copies():
            cp.wait()

    return start, finish


def _exchange_pair_side(g):
    return ((g,), (jax.ShapeDtypeStruct((4,) + g.shape[1:], g.dtype),),
            (pltpu.SemaphoreType.DMA((4,)), pltpu.SemaphoreType.DMA((4,))), _exchange_pair_steps)


def _exchange_pair(g):
    ins, outs, scratch, make = _exchange_pair_side(g)

    def body(g_ref, r_ref, *scr):
        start, finish = make((g_ref,), (r_ref,), scr)
        start()
        finish()

    any_spec = pl.BlockSpec(memory_space=pl.ANY)
    return pl.pallas_call(body, name="exchange_pair", out_shape=outs[0], in_specs=[any_spec], out_specs=any_spec,
                          scratch_shapes=list(scratch))(g)


def _tiling(R, Cc, tile, by_cols):
    if by_cols:
        assert Cc % tile == 0
        return Cc // tile, (R, tile), lambda lead, i: (lead, 0, i)
    assert R % tile == 0
    return R // tile, (tile, Cc), lambda lead, i: (lead, i, 0)


def _pair_add(name, g, r1, slots, tile, by_cols=False, side=None):
    _, R, Cc = g.shape
    steps, blk, at = _tiling(R, Cc, tile, by_cols)
    side_ins, side_outs, side_scr, side_make = side if side is not None else ((), (), (), None)
    n_si, n_so = len(side_ins), len(side_outs)

    def kern(s_ref, a_ref, b_ref, *rest):
        o_ref = rest[n_si]
        if side_make is not None:
            start, finish = side_make(rest[:n_si], rest[n_si + 1:n_si + 1 + n_so], rest[n_si + 1 + n_so:])
            j, i = pl.program_id(0), pl.program_id(1)
            pl.when(jnp.logical_and(j == 0, i == 0))(start)
        o_ref[...] = (a_ref[...].astype(F32) + b_ref[...].astype(F32)).astype(o_ref.dtype)
        if side_make is not None:
            pl.when(jnp.logical_and(j == 2, i == steps - 1))(finish)

    any_spec = pl.BlockSpec(memory_space=pl.ANY)
    res = pl.pallas_call(
        kern, name=name,
        out_shape=[jax.ShapeDtypeStruct((3, R, Cc), BF16)] + list(side_outs),
        grid_spec=pltpu.PrefetchScalarGridSpec(
            num_scalar_prefetch=1, grid=(3, steps),
            in_specs=[pl.BlockSpec((1,) + blk, lambda j, i, s: at(s[j], i)),
                      pl.BlockSpec((1,) + blk, lambda j, i, s: at(1 + j, i))] + [any_spec] * n_si,
            out_specs=[pl.BlockSpec((1,) + blk, lambda j, i, s: at(j, i))] + [any_spec] * n_so,
            scratch_shapes=list(side_scr)),
        compiler_params=_cp(("arbitrary", "arbitrary")),
    )(slots, g, r1, *side_ins)
    return res if side is not None else res[0]


def _axis_neighbours():
    x, y, c = _place()
    flip = lambda a, bit: a + bit - 2 * a * bit
    return (flip(x, c), flip(y, 1 - c), c), (flip(x, 1 - c), flip(y, c), c), c


def _exchange_first_steps(in_refs, out_refs, scratch):
    pairs = list(zip(in_refs, out_refs))
    send_sems, recv_sems = scratch
    first, _, c = _axis_neighbours()

    def copies():
        return [pltpu.make_async_remote_copy(
            src_ref=src.at[j], dst_ref=dst.at[k], send_sem=send_sems.at[t, k], recv_sem=recv_sems.at[t, k],
            device_id=first, device_id_type=MESH)
            for t, (src, dst) in enumerate(pairs) for k, j in enumerate((1 - c, 2))]

    def start():
        for cp in copies():
            cp.start()

    def finish():
        for cp in copies():
            cp.wait()

    return start, finish


def _exchange_first_side(*ss):
    n = len(ss)
    return (ss, tuple(jax.ShapeDtypeStruct((2,) + s.shape[1:], s.dtype) for s in ss),
            (pltpu.SemaphoreType.DMA((n, 2)), pltpu.SemaphoreType.DMA((n, 2))), _exchange_first_steps)


def _axis_add(name, s, t1, core, tile, by_cols=False):
    _, R, Cc = s.shape
    steps, blk, at = _tiling(R, Cc, tile, by_cols)

    def kern(c_ref, a_ref, b_ref, o_ref):
        o_ref[...] = (a_ref[...].astype(F32) + b_ref[...].astype(F32)).astype(o_ref.dtype)

    return pl.pallas_call(
        kern, name=name,
        out_shape=jax.ShapeDtypeStruct((1, R, Cc), BF16),
        grid_spec=pltpu.PrefetchScalarGridSpec(
            num_scalar_prefetch=1, grid=(steps,),
            in_specs=[pl.BlockSpec((1,) + blk, lambda i, cr: at(cr[0], i)),
                      pl.BlockSpec((1,) + blk, lambda i, cr: at(1, i))],
            out_specs=pl.BlockSpec((1,) + blk, lambda i, cr: at(0, i))),
        compiler_params=_cp(("arbitrary",)),
    )(core, s, t1)


def _exchange_second_steps(in_refs, out_refs, scratch):
    send_sems, recv_sems = scratch
    _, second, _ = _axis_neighbours()

    def copies():
        return [pltpu.make_async_remote_copy(src_ref=src, dst_ref=dst, send_sem=send_sems.at[t], recv_sem=recv_sems.at[t],
                                             device_id=second, device_id_type=MESH)
                for t, (src, dst) in enumerate(zip(in_refs, out_refs))]

    def start():
        for cp in copies():
            cp.start()

    def finish():
        for cp in copies():
            cp.wait()

    return start, finish


def _exchange_second_side(*ps):
    n = len(ps)
    return (ps, tuple(jax.ShapeDtypeStruct(p.shape, p.dtype) for p in ps),
            (pltpu.SemaphoreType.DMA((n,)), pltpu.SemaphoreType.DMA((n,))), _exchange_second_steps)


def _adamw(name, w, m, v, parts, tile, by_cols=False):
    R, Cc = w.shape
    steps, blk_shape, at = _tiling(R, Cc, tile, by_cols)
    n_parts = len(parts)

    def kern(*refs):
        w_ref, m_ref, v_ref = refs[:3]
        g = None
        for r_ref, (_, n) in zip(refs[3:3 + n_parts], parts):
            for s in range(n):
                term = r_ref[s].astype(F32)
                g = term if g is None else g + term
        g_out, d_out, m_out, v_out = refs[3 + n_parts:]
        mn = ADAM_B1 * m_ref[...] + (1.0 - ADAM_B1) * g
        vn = ADAM_B2 * v_ref[...] + (1.0 - ADAM_B2) * (g * g)
        m_hat = mn / (1.0 - ADAM_B1 ** ADAM_STEP)
        v_hat = vn / (1.0 - ADAM_B2 ** ADAM_STEP)
        g_out[...] = g
        d_out[...] = -ADAM_LR * (m_hat / (jnp.sqrt(v_hat) + ADAM_EPS) + ADAM_WD * w_ref[...])
        m_out[...] = mn
        v_out[...] = vn

    blk = pl.BlockSpec(blk_shape, lambda i: at(0, i)[1:])
    return pl.pallas_call(
        kern, name=name,
        out_shape=[jax.ShapeDtypeStruct((R, Cc), F32)] * 4,
        grid=(steps,),
        in_specs=[blk] * 3 + [pl.BlockSpec((n,) + blk_shape, lambda i: at(0, i)) for (_, n) in parts],
        out_specs=[blk] * 4,
        compiler_params=_cp(("arbitrary",)),
    )(w, m, v, *[a for (a, _) in parts])


def _assemble_columns(blocks, pieces, zeros, width):
    _, R, Cc = blocks.shape
    tr = min(256, R)

    def kern(b_ref, o_ref):
        for col, n in zeros:
            o_ref[:, col:col + n] = jnp.zeros((tr, n), o_ref.dtype)
        for col, d, lo, n in pieces:
            o_ref[:, col:col + n] = b_ref[d, :, lo:lo + n]

    return pl.pallas_call(
        kern, name="assemble_w_in",
        out_shape=jax.ShapeDtypeStruct((R, width), blocks.dtype),
        grid=(R // tr,),
        in_specs=[pl.BlockSpec((N_DEV, tr, Cc), lambda i: (0, i, 0))],
        out_specs=pl.BlockSpec((tr, width), lambda i: (i, 0)),
        compiler_params=_cp(("parallel",)),
    )(blocks)


def _split_rows(x, pieces, rows, side):
    _, Cc = x.shape
    tc = min(256, Cc)
    side_ins, side_outs, side_scr, side_make = side
    n_si, n_so = len(side_ins), len(side_outs)

    def kern(x_ref, *rest):
        o_ref = rest[n_si]
        start, finish = side_make(rest[:n_si], rest[n_si + 1:n_si + 1 + n_so], rest[n_si + 1 + n_so:])
        pl.when(pl.program_id(0) == 0)(start)
        for d, lo, row, n in pieces:
            o_ref[d, lo:lo + n, :] = x_ref[row:row + n, :]
        pl.when(pl.program_id(0) == Cc // tc - 1)(finish)

    any_spec = pl.BlockSpec(memory_space=pl.ANY)
    return pl.pallas_call(
        kern, name="split_w_in_grad",
        out_shape=[jax.ShapeDtypeStruct((N_DEV, rows, Cc), x.dtype)] + list(side_outs),
        grid=(Cc // tc,),
        in_specs=[pl.BlockSpec((x.shape[0], tc), lambda i: (0, i))] + [any_spec] * n_si,
        out_specs=[pl.BlockSpec((N_DEV, rows, tc), lambda i: (0, 0, i))] + [any_spec] * n_so,
        scratch_shapes=list(side_scr),
        compiler_params=_cp(("arbitrary",)),
    )(x, *side_ins)


def _pad_cols(a, w):
    return jnp.pad(a, ((0, 0), (0, w - a.shape[1])))


def _pad_rows(a, r):
    return jnp.pad(a, ((0, r - a.shape[0]), (0, 0)))


def _pack_b(pf, pr, wo, w2, a2, rows):
    body = jnp.concatenate([pf, pr, wo.reshape(2048, 256), jnp.concatenate([w2, a2], axis=1)], axis=0)
    return _pad_rows(body, rows)


def kernel(x, norm_gain, w_in, fox_forget_bias, rwkv_shift_mix, rwkv_w0, rwkv_w2, rwkv_a0, rwkv_a2, rwkv_k_k, rwkv_k_a, rwkv_r_k, rwkv_ln_w, rwkv_ln_b, w_proj_fox, w_proj_rwkv, w_out, final_norm_gain, loss_target, m_norm_gain, m_w_in, m_fox_forget_bias, m_rwkv_shift_mix, m_rwkv_w0, m_rwkv_w2, m_rwkv_a0, m_rwkv_a2, m_rwkv_k_k, m_rwkv_k_a, m_rwkv_r_k, m_rwkv_ln_w, m_rwkv_ln_b, m_w_proj_fox, m_w_proj_rwkv, m_w_out, m_final_norm_gain, v_norm_gain, v_w_in, v_fox_forget_bias, v_rwkv_shift_mix, v_rwkv_w0, v_rwkv_w2, v_rwkv_a0, v_rwkv_a2, v_rwkv_k_k, v_rwkv_k_a, v_rwkv_r_k, v_rwkv_ln_w, v_rwkv_ln_b, v_w_proj_fox, v_w_proj_rwkv, v_w_out, v_final_norm_gain):
    T, D = x.shape[1], x.shape[2]
    assert D == 2048 and T % LANE == 0
    NI = w_in.shape[2]
    IN = N_DEV * NI
    RB = 4224
    x2 = x[0]
    lt2 = loss_target[0]
    me = _slot(_place())

    tb = min(256, T)
    tbh = min(128, T)
    h, wa = _rowcall("rms_fwd", T, tb, [_rows(x2, tb, D), _whole(norm_gain)],
                     lambda i, xv, g: ([_rms_math(xv, g)], []), [(D, BF16)],
                     side=_allgather_side(w_in[0].astype(BF16), relay=True))
    packed_own = _pack_b(w_proj_fox[0], w_proj_rwkv[0], w_out[0], rwkv_w2[0], rwkv_a2[0], RB).astype(BF16)
    sections = [(0, 4096, 0), (4104, 4096, 4096), (8392, 4096, 8192), (4096, 8, 12288), (8200, 96, 12544), (8296, 96, 12672)]
    NP = 12800
    pieces, zeros, at_col = [], [], 0
    for lo, width, pad_lo in sections:
        if pad_lo > at_col:
            zeros.append((at_col, pad_lo - at_col))
        col = lo
        while col < lo + width:
            d = col // NI
            stop = min(lo + width, (d + 1) * NI)
            pieces.append((pad_lo + col - lo, d, col - d * NI, stop - col))
            col = stop
        at_col = pad_lo + width
    zeros.append((at_col, NP - at_col))
    w_pad = _assemble_columns(wa, pieces, zeros, NP)
    F_CB, LORA_CB = 96, 49

    mu = rwkv_shift_mix
    mu_main = mu[:, 0:4096]
    mu_lora = jnp.concatenate([_pad_cols(mu[:, 4096:4192], LANE), _pad_cols(mu[:, 4192:4288], LANE)], axis=1)
    bias_pad = _pad_cols(fox_forget_bias, LANE)
    rk_flat = rwkv_r_k.reshape(1, 1024)
    gf = final_norm_gain.reshape(1, D)

    u, wb = _mm(h, w_pad, tm=1024, tn=1280, name="mm_in", side=_allgather_side(packed_own))
    wpf = wb[:, 0:1024, :].transpose(1, 0, 2).reshape(1024, D)
    wpr = wb[:, 1024:2048, :].transpose(1, 0, 2).reshape(1024, D)
    wo = wb[:, 2048:4096, :].reshape(N_DEV * 256, D)
    w2p = _pad_rows(wb[:, 4096:4192, 0:128].transpose(1, 0, 2).reshape(96, 1024).astype(F32), LANE)
    a2p = _pad_rows(wb[:, 4096:4192, 128:256].transpose(1, 0, 2).reshape(96, 1024).astype(F32), LANE)

    c8 = _gates_fwd(u, bias_pad, T, F_CB)
    c3 = c8.reshape(8, 1, T)
    o_raw, o_a, lse = _attn_fwd(u, c3, T)

    def shift_body(i, um, hm, ul, hl, mm_, ml):
        outs = []
        for uv, hv, mv in ((um, hm, mm_), (ul, hl, ml)):
            hv = jnp.where(i == 0, 0.0, hv)
            prev = pltpu.roll(jnp.concatenate([hv, uv], axis=0), 1, 0)[SUB:]
            outs.append(uv + (prev - uv) * mv)
        return outs, []

    def halo_prev(arr, w, cb):
        return (arr, (SUB, w), lambda i: (jnp.maximum(i * (tbh // SUB) - 1, 0), cb))

    xs, xl = _rowcall("rwkv_shift_fwd", T, tbh,
                      [_rows(u, tbh, 4096, 1), halo_prev(u, 4096, 1), _rows(u, tbh, 256, LORA_CB), halo_prev(u, 256, LORA_CB),
                       _whole(mu_main), _whole(mu_lora)],
                      shift_body, [(4096, F32), (256, F32)])

    prep_par = [_whole(rwkv_w0), _whole(rwkv_a0), _whole(rwkv_k_k), _whole(rwkv_k_a), _whole(w2p), _whole(a2p)]
    prep_rows = [_rows(xs, tbh, 1024, 1), _rows(xl, tbh, LANE, 0), _rows(xl, tbh, LANE, 1)]
    lw, kp, an, bb = _rowcall("rwkv_prep_fwd", T, tbh, prep_rows + prep_par,
                              lambda i, *a: (list(_prep_math(*a)), []), [(1024, F32)] * 4)
    y, states, invs = _scan_fwd(xs, lw, kp, an, bb, T)
    post_rows = [_rows(y, tbh, 1024), _rows(xs, tbh, 1024, 0), _rows(kp, tbh, 1024), _rows(xs, tbh, 1024, 2), _rows(xs, tbh, 1024, 3)]
    post_par = [_whole(rwkv_ln_w), _whole(rwkv_ln_b), _whole(rk_flat)]
    (o_b,) = _rowcall("rwkv_post_fwd", T, tbh, post_rows + post_par,
                      lambda i, *a: ([_post_math(*a)], []), [(1024, BF16)])

    pa = _mm(o_a, wpf, name="mm_proj_fox")
    pb = _mm(o_b, wpr, name="mm_proj_rwkv")
    merge_rows = [_rows(u, tb, D, 4), _rows(u, tb, D, 5), _rows(pa, tb, D), _rows(pb, tb, D)]
    (mg,) = _rowcall("merge_fwd", T, tb, merge_rows, lambda i, *a: ([_merge_math(*a)], []), [(D, BF16)])
    mo = _mm(mg, wo, name="mm_out")

    def head_body(i, xv, mov, ltv, g):
        out = xv + mov
        r = lax.rsqrt(jnp.mean(out * out, axis=-1, keepdims=True) + RMS_EPS)
        yn = out * r
        err = yn * g - ltv
        loss = 0.5 * jnp.sum(jnp.sum(err * err, axis=-1, keepdims=True), axis=0, keepdims=True) / D
        dyv = err / D
        dyn = dyv * g
        dout = r * (dyn - yn * jnp.mean(dyn * yn, axis=-1, keepdims=True))
        return [dout], [loss, jnp.sum(dyv * yn, axis=0, keepdims=True)]

    dout, loss_p, dgf_p = _rowcall("loss_head", T, tb, [_rows(x2, tb, D), _rows(mo, tb, D), _rows(lt2, tb, D), _whole(gf)],
                                   head_body, [(D, F32)], [(1, 1), (1, D)])

    dm = _mm(dout, wo, tb=True, name="mm_out_dx")
    dwo = _mm(mg, dout, ta=True, out_dtype=BF16, name="mm_out_dw")

    def merge_bwd_body(i, ga, gb, pav, pbv, dmv):
        _, vjp = jax.vjp(_merge_math, ga, gb, pav, pbv)
        dga, dgb, dpa, dpb = vjp(dmv)
        return [dga, dgb, dpa, dpb], []

    dga, dgb, dpa, dpb = _rowcall("merge_bwd", T, tb, merge_rows + [_rows(dm, tb, D)], merge_bwd_body,
                                  [(D, BF16), (D, BF16), (D, BF16), (D, BF16)])
    doa = _mm(dpa, wpf, tb=True, name="mm_proj_fox_dx")
    dwpf = _mm(o_a, dpa, ta=True, out_dtype=BF16, name="mm_proj_fox_dw")
    dob = _mm(dpb, wpr, tb=True, name="mm_proj_rwkv_dx")
    dwpr = _mm(o_b, dpb, ta=True, out_dtype=BF16, name="mm_proj_rwkv_dw")

    do_b, dza, delta128 = _rowcall("fox_attn_pre", T, tb, [_rows(doa, tb, 1024), _rows(u, tb, 1024, 3), _rows(o_raw, tb, 1024)],
                                   lambda i, *a: (list(_attn_pre_math(*a)), []), [(1024, BF16), (1024, F32), (LANE, F32)])
    delta = delta128[:, 0:8].T.reshape(8, T, 1)
    dq, dk, dv, dc3 = _attn_bwd(u, c3, lse, do_b, delta, T)
    dfl, dbias_p = _gates_bwd(dc3.reshape(8, T), u, bias_pad, T, F_CB)

    def post_bwd_body(i, yv, rv, kpv, vv, zv, lnw, lnb, rkv, dobv):
        _, vjp = jax.vjp(_post_math, yv, rv, kpv, vv, zv, lnw, lnb, rkv)
        dy_, dr_, dkp_, dv_, dz_, dlnw, dlnb, drk = vjp(dobv)
        return [dy_, dr_, dkp_, dv_, dz_], [dlnw, dlnb, drk]

    dy_s, dr_p, dkp_p, dv_p, dzb, dlnw_p, dlnb_p, drk_p = _rowcall(
        "rwkv_post_bwd", T, tbh, post_rows + post_par + [_rows(dob, tbh, 1024)], post_bwd_body,
        [(1024, F32)] * 5, [(1, 1024)] * 3)
    dr_s, dlw, dkp_s, dv_s, dan, dbb = _scan_bwd(xs, lw, kp, an, bb, states, invs, dy_s, T)

    def prep_bwd_body(i, xk, xwd, xad, w0, a0, kkw, kaw, w2v, a2v, dlw_, dkp1, dkp2, dan_, dbb_, dr1, dr2, dv1, dv2, dz_):
        _, vjp = jax.vjp(_prep_math, xk, xwd, xad, w0, a0, kkw, kaw, w2v, a2v)
        dxk, dxwd, dxad, dw0, da0, dkk, dka, dw2, da2 = vjp((dlw_, dkp1 + dkp2, dan_, dbb_))
        return [[dr1 + dr2, dxk, dv1 + dv2, dz_], [dxwd, dxad]], [dw0, da0, dkk, dka, dw2, da2]

    cots = [dlw, dkp_s, dkp_p, dan, dbb, dr_s, dr_p, dv_s, dv_p, dzb]
    dxs, dxl, dw0_p, da0_p, dkk_p, dka_p, dw2_p, da2_p = _rowcall(
        "rwkv_prep_bwd", T, tbh, prep_rows + prep_par + [_rows(c_, tbh, 1024) for c_ in cots], prep_bwd_body,
        [(4096, F32), (256, F32)], [(1, 1024)] * 4 + [(LANE, 1024)] * 2)

    def shift_bwd_body(i, dm_, hm, dl_, hl, um, pm, ul, pl_, mm_, ml):
        last = i == T // tbh - 1
        outs, accs = [], []
        for dv_, hv, uv, pv, mv in ((dm_, hm, um, pm, mm_), (dl_, hl, ul, pl_, ml)):
            hv = jnp.where(last, 0.0, hv)
            nxt = pltpu.roll(jnp.concatenate([dv_, hv], axis=0), tbh + SUB - 1, 0)[:tbh]
            pv = jnp.where(i == 0, 0.0, pv)
            prev = pltpu.roll(jnp.concatenate([pv, uv], axis=0), 1, 0)[SUB:]
            outs.append(dv_ * (1.0 - mv) + nxt * mv)
            accs.append(jnp.sum(dv_ * (prev - uv), axis=0, keepdims=True))
        return outs, accs

    def halo_next(arr, w, cb):
        last_blk = T // SUB - 1
        return (arr, (SUB, w), lambda i: (jnp.minimum((i + 1) * (tbh // SUB), last_blk), cb))

    du_b, du_l, dmu_main_p, dmu_lora_p = _rowcall(
        "rwkv_shift_bwd", T, tbh,
        [_rows(dxs, tbh, 4096), halo_next(dxs, 4096, 0), _rows(dxl, tbh, 256), halo_next(dxl, 256, 0),
         _rows(u, tbh, 4096, 1), halo_prev(u, 4096, 1), _rows(u, tbh, 256, LORA_CB), halo_prev(u, 256, LORA_CB),
         _whole(mu_main), _whole(mu_lora)],
        shift_bwd_body, [(4096, BF16), (256, BF16)], [(1, 4096), (1, 256)])

    lora_g = jnp.concatenate([dw2_p[:96].reshape(96, N_DEV, 128).transpose(1, 0, 2),
                              da2_p[:96].reshape(96, N_DEV, 128).transpose(1, 0, 2)], axis=2).astype(BF16)
    gb = jnp.concatenate([dwpf.reshape(1024, N_DEV, 256).transpose(1, 0, 2),
                          dwpr.reshape(1024, N_DEV, 256).transpose(1, 0, 2),
                          dwo.reshape(N_DEV, 2048, 256), lora_g, jnp.zeros((N_DEV, RB - 4192, 256), BF16)], axis=1)
    xx, yy, cc = _place()
    chip_slots = jnp.stack([_slot((*chip, cc)) for chip in _other_chips(xx, yy)]).astype(jnp.int32)
    core = jnp.stack([cc]).astype(jnp.int32)
    du, r1b = _rowcall("assemble_du", T, tb,
                       [_rows(a_, tb, a_.shape[1]) for a_ in (dq, dk, dv, dza, du_b, dga, dgb, dfl, du_l)],
                       lambda i, *a: ([list(a[:8]) + [jnp.zeros((tb, LANE), BF16), a[8]]], []), [(NP, BF16)],
                       side=_exchange_pair_side(gb))
    sb = _pair_add("pair_add_packed", gb, r1b, chip_slots, RB // 3)
    dw_pad_t, t1b = _mm(du, h, ta=True, out_dtype=BF16, tm=1280, tn=1024, name="mm_in_dw",
                        side=_exchange_first_side(sb))
    pb = _axis_add("axis_add_packed", sb, t1b, core, RB // 3)
    ga, t2b = _split_rows(dw_pad_t, [(d, lo, pad_col, n) for pad_col, d, lo, n in pieces], NI,
                          _exchange_second_side(pb))
    r1a = _exchange_pair(ga)
    sa = _pair_add("pair_add_w_in", ga, r1a, chip_slots, 512, by_cols=True)
    ga_own = lax.dynamic_index_in_dim(ga, me, 0, keepdims=True)
    gb_own = lax.dynamic_index_in_dim(gb, me, 0, keepdims=True)
    dh, t1a = _mm(du, w_pad, tb=True, tm=1024, tn=1024, tk=NP // 10, name="mm_in_dx", side=_exchange_first_side(sa))
    pa = _axis_add("axis_add_w_in", sa, t1a, core, 512, by_cols=True)

    def rms_bwd_body(i, xv, g, dhv, doutv):
        _, vjp = jax.vjp(_rms_math, xv, g)
        dx_, dg_ = vjp(dhv)
        return [dx_ + doutv], [dg_]

    grad_x2, dng_p, t2a = _rowcall(
        "rms_bwd", T, tb, [_rows(x2, tb, D), _whole(norm_gain), _rows(dh, tb, D), _rows(dout, tb, D)],
        rms_bwd_body, [(D, F32)], [(1, D)], side=_exchange_second_side(pa))

    w_in_outs = _adamw("adamw_w_in", w_in[0].T, m_w_in[0].T, v_w_in[0].T, [(ga_own, 1), (r1a, 1), (t1a, 1), (t2a, 1)], 256, by_cols=True)
    g_in, d_in, m_in, v_in = [o.T for o in w_in_outs]

    dmu = jnp.concatenate([dmu_main_p, dmu_lora_p[:, 0:96], dmu_lora_p[:, 128:224]], axis=1)
    small_parts = [dng_p, dbias_p[:, 0:8], dmu, dw0_p, da0_p, dkk_p, dka_p, drk_p, dlnw_p, dlnb_p, dgf_p, loss_p]
    SR = 128
    small = _pad_cols(jnp.concatenate(small_parts, axis=1), SR * LANE).reshape(SR, LANE)
    rs = _allgather(small, "allgather_small")
    pk = lambda pf, pr, wo_, w2_, a2_: _pack_b(pf[0], pr[0], wo_[0], w2_[0], a2_[0], RB)
    outs_b = _adamw("adamw_packed", pk(w_proj_fox, w_proj_rwkv, w_out, rwkv_w2, rwkv_a2),
                    pk(m_w_proj_fox, m_w_proj_rwkv, m_w_out, m_rwkv_w2, m_rwkv_a2),
                    pk(v_w_proj_fox, v_w_proj_rwkv, v_w_out, v_rwkv_w2, v_rwkv_a2), [(gb_own, 1), (r1b, 1), (t1b, 1), (t2b, 1)], RB // 3)

    def pack_small(ng, fb, sm, w0, a0, kk_, ka_, rk_, lnw, lnb, fg):
        parts = [ng, fb, sm, w0, a0, kk_, ka_, rk_.reshape(1, 1024), lnw, lnb, fg.reshape(1, D), jnp.zeros((1, 1), F32)]
        return _pad_cols(jnp.concatenate(parts, axis=1), SR * LANE).reshape(SR, LANE)

    outs_s = _adamw("adamw_small",
                    pack_small(norm_gain, fox_forget_bias, rwkv_shift_mix, rwkv_w0, rwkv_a0, rwkv_k_k, rwkv_k_a, rwkv_r_k,
                               rwkv_ln_w, rwkv_ln_b, final_norm_gain),
                    pack_small(m_norm_gain, m_fox_forget_bias, m_rwkv_shift_mix, m_rwkv_w0, m_rwkv_a0, m_rwkv_k_k, m_rwkv_k_a,
                               m_rwkv_r_k, m_rwkv_ln_w, m_rwkv_ln_b, m_final_norm_gain),
                    pack_small(v_norm_gain, v_fox_forget_bias, v_rwkv_shift_mix, v_rwkv_w0, v_rwkv_a0, v_rwkv_k_k, v_rwkv_k_a,
                               v_rwkv_r_k, v_rwkv_ln_w, v_rwkv_ln_b, v_final_norm_gain),
                    [(rs, N_DEV)], SR)

    def unpack_b(pkd):
        return dict(w_proj_fox=pkd[0:1024][None], w_proj_rwkv=pkd[1024:2048][None], w_out=pkd[2048:4096].reshape(1, 256, D),
                    rwkv_w2=pkd[4096:4192, 0:128][None], rwkv_a2=pkd[4096:4192, 128:256][None])

    def unpack_s(pkd):
        flat = pkd.reshape(1, SR * LANE)
        names = [("norm_gain", D), ("fox_forget_bias", 8), ("rwkv_shift_mix", 4288), ("rwkv_w0", 1024), ("rwkv_a0", 1024),
                 ("rwkv_k_k", 1024), ("rwkv_k_a", 1024), ("rwkv_r_k", 1024), ("rwkv_ln_w", 1024), ("rwkv_ln_b", 1024),
                 ("final_norm_gain", D), ("loss", 1)]
        out, off = {}, 0
        for nm, n in names:
            out[nm] = flat[:, off:off + n]
            off += n
        out["rwkv_r_k"] = out["rwkv_r_k"].reshape(1, 16, 64)
        out["final_norm_gain"] = out["final_norm_gain"].reshape(D)
        return out

    order = ["norm_gain", "w_in", "fox_forget_bias", "rwkv_shift_mix", "rwkv_w0", "rwkv_w2", "rwkv_a0", "rwkv_a2", "rwkv_k_k",
             "rwkv_k_a", "rwkv_r_k", "rwkv_ln_w", "rwkv_ln_b", "w_proj_fox", "w_proj_rwkv", "w_out", "final_norm_gain"]
    result = []
    loss = None
    for kind, big in enumerate((g_in, d_in, m_in, v_in)):
        d = {**unpack_b(outs_b[kind]), **unpack_s(outs_s[kind]), "w_in": big[None]}
        if kind == 0:
            loss = d["loss"].reshape(())
        result += [d[n] for n in order]
    return (loss, grad_x2[None], *result)
```

```python
import functools

import jax
import jax.numpy as jnp
from jax import lax
from jax.experimental import pallas as pl
from jax.experimental.pallas import tpu as pltpu

F32 = jnp.float32
BF16 = jnp.bfloat16
HI = lax.Precision.HIGHEST
H3 = lax.Precision.HIGH
MESH = pl.DeviceIdType.MESH

FOX_HD = 128
RW_HD = 64
RMS_EPS = 1e-6
GN_EPS = 64e-5
L2_EPS = 1e-12
ADAM_LR = 0.001
ADAM_B1 = 0.9
ADAM_B2 = 0.999
ADAM_EPS = 1e-08
ADAM_WD = 0.01
ADAM_STEP = 10

LANE = 128
SUB = 8
VMEM_LIMIT = 56 * 1024 * 1024
N_DEV = 8
CHUNK = 128
SCAN_GROUP = 1
SCAN_PAIRS = 8
PS = None
NEG = -1e30


def _scan_shape(T):
    c = min(CHUNK, T)
    return c, min(SCAN_GROUP, T // c)


def _cp(sem=None):
    return pltpu.CompilerParams(dimension_semantics=sem, vmem_limit_bytes=VMEM_LIMIT)


def _sigmoid(x):
    return jax.nn.sigmoid(x)


def _softplus(x):
    return jnp.maximum(x, 0.0) + jnp.log(1.0 + jnp.exp(-jnp.abs(x)))


def _nn(a, b, prec=None):
    return lax.dot_general(a, b, (((1,), (0,)), ((), ())), precision=prec, preferred_element_type=F32)


def _nt(a, b, prec=None):
    return lax.dot_general(a, b, (((1,), (1,)), ((), ())), precision=prec, preferred_element_type=F32)


def _tn(a, b, prec=None):
    return lax.dot_general(a, b, (((0,), (0,)), ((), ())), precision=prec, preferred_element_type=F32)


def _iota2(shape, dim):
    return lax.broadcasted_iota(jnp.int32, shape, dim)


def _seg_sum(x):
    r = _iota2((LANE, LANE), 0) // RW_HD
    c = _iota2((LANE, LANE), 1) // RW_HD
    bd = (r == c).astype(F32)
    parts = [_nn(x[:, j * LANE:(j + 1) * LANE], bd, H3) for j in range(x.shape[1] // LANE)]
    return parts[0] if len(parts) == 1 else jnp.concatenate(parts, axis=1)


def _mm(a, b, *, ta=False, tb=False, out_dtype=F32, tm=1024, tn=1024, tk=None, name, side=None):
    assert not (ta and tb)
    K, M = a.shape if ta else a.shape[::-1]
    N = b.shape[0] if tb else b.shape[1]
    tm, tn = min(tm, M), min(tn, N)
    tk = K if tk is None else tk
    nk = K // tk
    assert M % tm == 0 and N % tn == 0 and K % tk == 0
    gi, gj = M // tm, N // tn
    a_spec = pl.BlockSpec((tk, tm), lambda i, j, k: (k, i)) if ta else pl.BlockSpec((tm, tk), lambda i, j, k: (i, k))
    b_spec = pl.BlockSpec((tn, tk), lambda i, j, k: (j, k)) if tb else pl.BlockSpec((tk, tn), lambda i, j, k: (k, j))
    side_ins, side_outs, side_scr, side_make = side if side is not None else ((), (), (), None)
    n_si, n_so = len(side_ins), len(side_outs)
    n_acc = 0 if nk == 1 else 1

    def body(*refs):
        a_ref, b_ref = refs[:2]
        o_ref = refs[2 + n_si]
        scr = refs[3 + n_si + n_so:]
        k = pl.program_id(2)
        if side_make is not None:
            start, finish = side_make(refs[2:2 + n_si], refs[3 + n_si:3 + n_si + n_so], scr[n_acc:])
            first = jnp.logical_and(jnp.logical_and(pl.program_id(0) == 0, pl.program_id(1) == 0), k == 0)
            last = jnp.logical_and(jnp.logical_and(pl.program_id(0) == gi - 1, pl.program_id(1) == gj - 1), k == nk - 1)
            pl.when(first)(start)
        av = a_ref[...].astype(BF16)
        bv = b_ref[...].astype(BF16)
        p = _tn(av, bv) if ta else _nt(av, bv) if tb else _nn(av, bv)
        if nk == 1:
            o_ref[...] = p.astype(out_dtype)
        else:
            acc_ref = scr[0]

            @pl.when(k == 0)
            def _():
                acc_ref[...] = p

            @pl.when(k > 0)
            def _():
                acc_ref[...] += p

            @pl.when(k == nk - 1)
            def _():
                o_ref[...] = acc_ref[...].astype(out_dtype)
        if side_make is not None:
            pl.when(last)(finish)

    any_spec = pl.BlockSpec(memory_space=pl.ANY)
    res = pl.pallas_call(
        body, name=name,
        out_shape=[jax.ShapeDtypeStruct((M, N), out_dtype)] + list(side_outs),
        grid=(gi, gj, nk),
        in_specs=[a_spec, b_spec] + [any_spec] * n_si,
        out_specs=[pl.BlockSpec((tm, tn), lambda i, j, k: (i, j))] + [any_spec] * n_so,
        scratch_shapes=([] if nk == 1 else [pltpu.VMEM((tm, tn), F32)]) + list(side_scr),
        compiler_params=_cp(("arbitrary",) * 3 if side is not None else ("parallel", "parallel", "arbitrary")),
    )(a, b, *side_ins)
    return res if side is not None else res[0]


def _rows(arr, tb, w, cb=0):
    return (arr, (tb, w), lambda i: (i, cb))


def _whole(arr):
    nd = arr.ndim
    return (arr, arr.shape, lambda i: (0,) * nd)


def _rowcall(name, T, tb, ins, body, outs, accs=(), side=None):
    n_in, n_out, n_acc = len(ins), len(outs), len(accs)
    side_ins, side_outs, side_scr, side_make = side if side is not None else ((), (), (), None)
    n_si = len(side_ins)

    def kern(*refs):
        i = pl.program_id(0)
        if side_make is not None:
            side_refs = refs[n_in + n_si + n_out + n_acc:]
            start, finish = side_make(refs[n_in:n_in + n_si], side_refs[:len(side_outs)], side_refs[len(side_outs):])
            pl.when(i == 0)(start)
            refs = refs[:n_in] + refs[n_in + n_si:]
        vals = [r[...] for r in refs[:n_in]]
        ro, ao = body(i, *vals)
        for r, v in zip(refs[n_in:n_in + n_out], ro):
            if isinstance(v, (list, tuple)):
                off = 0
                for piece in v:
                    w = piece.shape[1]
                    r[:, off:off + w] = piece.astype(r.dtype)
                    off += w
            else:
                r[...] = v.astype(r.dtype)
        if accs:
            acc_refs = refs[n_in + n_out:n_in + n_out + n_acc]

            @pl.when(i == 0)
            def _():
                for r in acc_refs:
                    r[...] = jnp.zeros(r.shape, F32)

            for r, v in zip(acc_refs, ao):
                r[...] += v
        if side_make is not None:
            pl.when(i == T // tb - 1)(finish)

    any_spec = pl.BlockSpec(memory_space=pl.ANY)
    out_shape = [jax.ShapeDtypeStruct((T, w), dt) for (w, dt) in outs] + [jax.ShapeDtypeStruct(s, F32) for s in accs]
    out_specs = [pl.BlockSpec((tb, w), lambda i: (i, 0)) for (w, dt) in outs] + [pl.BlockSpec(s, lambda i: (0, 0)) for s in accs]
    res = pl.pallas_call(
        kern, name=name,
        out_shape=out_shape + list(side_outs),
        grid=(T // tb,),
        in_specs=[pl.BlockSpec(bs, im) for (_, bs, im) in ins] + [any_spec] * n_si,
        out_specs=out_specs + [any_spec] * len(side_outs),
        scratch_shapes=list(side_scr),
        compiler_params=_cp(("arbitrary",)),
    )(*[a for (a, _, _) in ins], *side_ins)
    return res


def _rms_math(x, g):
    r = lax.rsqrt(jnp.mean(x * x, axis=-1, keepdims=True) + RMS_EPS)
    return x * r * g


def _merge_math(ga, gb, pa, pb):
    return _sigmoid(ga) * pa + _sigmoid(gb) * pb


def _prep_math(xk, xwd, xad, w0, a0, kk_w, ka_w, w2p, a2p):
    z = w0 + _nn(jnp.tanh(xwd), w2p, H3)
    w = -_softplus(-z) - 0.5
    lw = -jnp.exp(w)
    ag = _sigmoid(a0 + _nn(xad, a2p, H3))
    p = xk * kk_w
    n = jnp.maximum(jnp.sqrt(_seg_sum(p * p)), L2_EPS)
    kk = p / n
    kp = xk * (1.0 + (ag - 1.0) * ka_w)
    return lw, kp, -kk, kk * ag


def _post_math(y, r, kp, v, z, lnw, lnb, rk):
    inv = 1.0 / RW_HD
    mu = _seg_sum(y) * inv
    d = y - mu
    var = _seg_sum(d * d) * inv
    yn = d * lax.rsqrt(var + GN_EPS) * lnw + lnb
    bonus = _seg_sum(r * kp * rk) * v
    return (yn + bonus) * (z * _sigmoid(z))


def _neumann(ms, depth):
    eye = (_iota2(ms[0].shape, 0) == _iota2(ms[0].shape, 1)).astype(F32)
    width = ms[0].shape[1]
    mp = [_nn(m, m, PS) for m in ms]
    inv = [eye + m for m in ms]
    n = 2
    while n < depth:
        last = 2 * n >= depth
        for i in range(len(ms)):
            if last:
                inv[i] = inv[i] + _nn(mp[i], inv[i], PS)
            else:
                z = _nn(mp[i], jnp.concatenate([mp[i], inv[i]], axis=1), PS)
                mp[i], inv[i] = z[:, :width], inv[i] + z[:, width:]
        n *= 2
    return inv


@functools.partial(jax.custom_vjp, nondiff_argnums=(1,))
def _unit_inverses(ms, depth):
    if depth <= RW_HD:
        return tuple(_neumann(list(ms), depth))
    h, n = depth // 2, len(ms)
    cat = jnp.concatenate
    z = jnp.zeros((h, h), F32)

    def heads(m, r, c):
        b0 = m[r * h:(r + 1) * h, c * h:(c + 1) * h]
        b1 = m[depth + r * h:depth + (r + 1) * h, depth + c * h:depth + (c + 1) * h]
        return cat([cat([b0, z], axis=1), cat([z, b1], axis=1)], axis=0)

    diag = _neumann([heads(m, 0, 0) for m in ms] + [heads(m, 1, 1) for m in ms], h)
    ta, td = diag[:n], diag[n:]
    low = [_nn(heads(m, 1, 0), a, PS) for m, a in zip(ms, ta)]
    low = [_nn(d, x, PS) for d, x in zip(td, low)]
    out = []
    for a, x, d in zip(ta, low, td):
        rows = []
        for hd in (0, 1):
            sl = slice(hd * h, (hd + 1) * h)
            top, bot = [a[sl, sl], z], [x[sl, sl], d[sl, sl]]
            pad = [z, z]
            rows.append(cat(top + pad if hd == 0 else pad + top, axis=1))
            rows.append(cat(bot + pad if hd == 0 else pad + bot, axis=1))
        out.append(cat(rows, axis=0))
    return tuple(out)


def _unit_inverses_fwd(ms, depth):
    inv = _unit_inverses(ms, depth)
    return inv, inv


def _unit_inverses_bwd(depth, inv, cts):
    left = [_tn(t, g, PS) for t, g in zip(inv, cts)]
    return (tuple(_nt(l, t, PS) for l, t in zip(left, inv)),)


_unit_inverses.defvjp(_unit_inverses_fwd, _unit_inverses_bwd)


@jax.custom_vjp
def _known_inverses(ms, inv):
    return inv


def _known_inverses_fwd(ms, inv):
    return inv, inv


def _known_inverses_bwd(inv, cts):
    dms = _unit_inverses_bwd(None, inv, cts)[0]
    return dms, tuple(jnp.zeros_like(t) for t in inv)


_known_inverses.defvjp(_known_inverses_fwd, _known_inverses_bwd)


def _scan_group(s0s, *flat, known_inv=None, with_inv=False):
    P = len(s0s)
    G = len(flat) // (6 * P)
    ch = [flat[6 * i:6 * i + 6] for i in range(P * G)]
    C = ch[0][0].shape[0]
    C2 = 2 * C
    cat = jnp.concatenate
    m0 = _iota2((1, LANE), 1) < RW_HD
    mask0 = m0.astype(F32)
    mask1 = 1.0 - mask0
    r2 = _iota2((C2, C2), 0)
    c2 = _iota2((C2, C2), 1)
    dist = r2 - c2
    in_head = dist <= r2 % C
    lower = (_iota2((C, C), 0) >= _iota2((C, C), 1)).astype(F32)
    bd = (_iota2((LANE, LANE), 0) // RW_HD) == (_iota2((LANE, LANE), 1) // RW_HD)

    def tri(m, strict):
        return jnp.where(dist > 0 if strict else dist >= 0, jnp.where(in_head, m, 0.0), 0.0)

    def sel(z):
        return jnp.where(m0, z[:C], z[C:])

    gs = [_nn(lower, c[1], HI) for c in ch]
    pre = []
    for (r, lw, k, v, a, b), g in zip(ch, gs):
        g_end = jnp.sum(lw, axis=0, keepdims=True)
        gm = g - jnp.sum(lw[:C // 2], axis=0, keepdims=True)
        en = jnp.exp(-gm)
        ec = jnp.exp(g_end - g)
        pre.append(dict(at=a * jnp.exp(g - lw), rt=r * jnp.exp(g), am=a * jnp.exp(gm - lw), rm=r * jnp.exp(gm),
                        bt=b * en, kt=k * en, bh=b * ec, kh=k * ec, dec=jnp.exp(g_end), v=v))
    grams = [_nt(cat([p["am"] * mask0, p["am"] * mask1, p["rm"] * mask0, p["rm"] * mask1], axis=0),
                 cat([p["bt"], p["bt"], p["kt"], p["kt"]], axis=0), PS) for p in pre]
    mab = tuple(tri(gm[:C2, :C2], True) for gm in grams)
    tinv = _unit_inverses(mab, C) if known_inv is None else _known_inverses(mab, known_inv)
    xv =[sel(_nn(tri(gm[:C2, C2:], True), cat([p["v"], p["v"]], axis=0), PS)) for gm, p in zip(grams, pre)]
    ys, s = [None] * (P * G), list(s0s)
    for i in range(G):
        for q in range(P):
            n = q * G + i
            p, gm = pre[n], grams[n]
            sx = _nt(cat([p["at"], p["rt"]], axis=0), s[q], PS)
            x = sx[:C] + xv[n]
            u = sel(_nn(tinv[n], cat([x, x], axis=0), PS))
            v = p["v"]
            ys[n] = sx[C:] + sel(_nn(cat([tri(gm[C2:, :C2], False), tri(gm[C2:, C2:], False)], axis=1),
                                     cat([u, u, v, v], axis=0), PS))
            s[q] = s[q] * p["dec"] + jnp.where(bd, _tn(cat([u, v], axis=0), cat([p["bh"], p["kh"]], axis=0), PS), 0.0)
    return (tuple(ys), tuple(s), tinv) if with_inv else (tuple(ys), tuple(s))


def _scan_fwd(xs, lw, kp, an, bb, T):
    C, G = _scan_shape(T)
    P = SCAN_PAIRS
    nc = T // (C * G)
    npair = 1024 // LANE

    def kern(r_ref, lw_ref, k_ref, v_ref, a_ref, b_ref, y_ref, st_ref, inv_ref, s_scr):
        n = pl.program_id(1)

        @pl.when(n == 0)
        def _():
            s_scr[...] = jnp.zeros(s_scr.shape, F32)

        st_ref[0] = s_scr[...]
        ins = (r_ref, lw_ref, k_ref, v_ref, a_ref, b_ref)
        ys, s1, inv = _scan_group(
            tuple(s_scr[q] for q in range(P)),
            *[ref[i * C:(i + 1) * C, q * LANE:(q + 1) * LANE] for q in range(P) for i in range(G) for ref in ins], with_inv=True)
        for q in range(P):
            for i in range(G):
                y_ref[i * C:(i + 1) * C, q * LANE:(q + 1) * LANE] = ys[q * G + i]
                inv_ref[0, 0, q * G + i] = inv[q * G + i]
            s_scr[q] = s1[q]

    def col(off):
        return pl.BlockSpec((C * G, P * LANE), lambda p, n: (n, off // P + p))

    return pl.pallas_call(
        kern, name="rwkv_scan_fwd",
        out_shape=[jax.ShapeDtypeStruct((T, 1024), F32), jax.ShapeDtypeStruct((nc, npair, LANE, LANE), F32),
                   jax.ShapeDtypeStruct((nc, npair // P, P * G, 2 * C, 2 * C), F32)],
        grid=(npair // P, nc),
        in_specs=[col(0), col(0), col(0), col(16), col(0), col(0)],
        out_specs=[col(0), pl.BlockSpec((1, P, LANE, LANE), lambda p, n: (n, p, 0, 0)),
                   pl.BlockSpec((1, 1, P * G, 2 * C, 2 * C), lambda p, n: (n, p, 0, 0, 0))],
        scratch_shapes=[pltpu.VMEM((P, LANE, LANE), F32)],
        compiler_params=_cp(("parallel", "arbitrary")),
    )(xs, lw, kp, xs, an, bb)


def _scan_bwd(xs, lw, kp, an, bb, states, invs, dy, T):
    C, G = _scan_shape(T)
    P = SCAN_PAIRS
    nc = T // (C * G)
    npair = 1024 // LANE

    def kern(r_ref, lw_ref, k_ref, v_ref, a_ref, b_ref, st_ref, inv_ref, dy_ref, dr_ref, dlw_ref, dk_ref, dv_ref, da_ref, db_ref,
             ds_scr):
        n = pl.program_id(1)

        @pl.when(n == 0)
        def _():
            ds_scr[...] = jnp.zeros(ds_scr.shape, F32)

        ins = (r_ref, lw_ref, k_ref, v_ref, a_ref, b_ref)
        units = [(q, i) for q in range(P) for i in range(G)]
        known = tuple(inv_ref[0, 0, q * G + i] for q, i in units)
        _, vjp = jax.vjp(functools.partial(_scan_group, known_inv=known), tuple(st_ref[0, q] for q in range(P)),
                         *[ref[i * C:(i + 1) * C, q * LANE:(q + 1) * LANE] for q, i in units for ref in ins])
        grads = vjp((tuple(dy_ref[i * C:(i + 1) * C, q * LANE:(q + 1) * LANE] for q, i in units),
                     tuple(ds_scr[q] for q in range(P))))
        for q in range(P):
            ds_scr[q] = grads[0][q]
        outs = (dr_ref, dlw_ref, dk_ref, dv_ref, da_ref, db_ref)
        for n_, (q, i) in enumerate(units):
            for t, ref in enumerate(outs):
                ref[i * C:(i + 1) * C, q * LANE:(q + 1) * LANE] = grads[1 + 6 * n_ + t]

    def col(off):
        return pl.BlockSpec((C * G, P * LANE), lambda p, n: (nc - 1 - n, off // P + p))

    return pl.pallas_call(
        kern, name="rwkv_scan_bwd",
        out_shape=[jax.ShapeDtypeStruct((T, 1024), F32)] * 6,
        grid=(npair // P, nc),
        in_specs=[col(0), col(0), col(0), col(16), col(0), col(0),
                  pl.BlockSpec((1, P, LANE, LANE), lambda p, n: (nc - 1 - n, p, 0, 0)),
                  pl.BlockSpec((1, 1, P * G, 2 * C, 2 * C), lambda p, n: (nc - 1 - n, p, 0, 0, 0)), col(0)],
        out_specs=[col(0)] * 6,
        scratch_shapes=[pltpu.VMEM((P, LANE, LANE), F32)],
        compiler_params=_cp(("parallel", "arbitrary")),
    )(xs, lw, kp, xs, an, bb, states, invs, dy)


def _gates_fwd(u, bias_pad, T, f_cb):
    nb = T // LANE

    def kern(f_ref, b_ref, c_ref):
        x = f_ref[...] + b_ref[...]
        lf = jnp.minimum(x, 0.0) - jnp.log(1.0 + jnp.exp(-jnp.abs(x)))
        lft = lf.T
        ut = (_iota2((LANE, LANE), 0) <= _iota2((LANE, LANE), 1)).astype(F32)
        carry = jnp.zeros((LANE, 1), F32)
        for blk in range(nb):
            seg = lft[:, blk * LANE:(blk + 1) * LANE]
            cs = _nn(seg, ut, HI) + carry
            c_ref[:, blk * LANE:(blk + 1) * LANE] = cs[:SUB, :]
            carry = carry + jnp.sum(seg, axis=1, keepdims=True)

    return pl.pallas_call(
        kern, name="fox_gates_fwd",
        out_shape=jax.ShapeDtypeStruct((SUB, T), F32),
        grid=(1,),
        in_specs=[pl.BlockSpec((T, LANE), lambda i: (0, f_cb)), pl.BlockSpec((1, LANE), lambda i: (0, 0))],
        out_specs=pl.BlockSpec((SUB, T), lambda i: (0, 0)),
        compiler_params=_cp(("arbitrary",)),
    )(u, bias_pad)


def _gates_bwd(dc, u, bias_pad, T, f_cb):
    nb = T // LANE

    def kern(dc_ref, f_ref, b_ref, dfl_ref, db_ref):
        dcv = jnp.concatenate([dc_ref[...], jnp.zeros((LANE - SUB, T), F32)], axis=0)
        lt = (_iota2((LANE, LANE), 0) >= _iota2((LANE, LANE), 1)).astype(F32)
        carry = jnp.zeros((LANE, 1), F32)
        pieces = [None] * nb
        for blk in range(nb - 1, -1, -1):
            seg = dcv[:, blk * LANE:(blk + 1) * LANE]
            pieces[blk] = _nn(seg, lt, HI) + carry
            carry = carry + jnp.sum(seg, axis=1, keepdims=True)
        dlf = (pieces[0] if nb == 1 else jnp.concatenate(pieces, axis=1)).T
        x = f_ref[...] + b_ref[...]
        dfl = dlf * _sigmoid(-x)
        dfl_ref[...] = dfl
        db_ref[...] = jnp.sum(dfl, axis=0, keepdims=True)

    return pl.pallas_call(
        kern, name="fox_gates_bwd",
        out_shape=[jax.ShapeDtypeStruct((T, LANE), F32), jax.ShapeDtypeStruct((1, LANE), F32)],
        grid=(1,),
        in_specs=[pl.BlockSpec((SUB, T), lambda i: (0, 0)), pl.BlockSpec((T, LANE), lambda i: (0, f_cb)),
                  pl.BlockSpec((1, LANE), lambda i: (0, 0))],
        out_specs=[pl.BlockSpec((T, LANE), lambda i: (0, 0)), pl.BlockSpec((1, LANE), lambda i: (0, 0))],
        compiler_params=_cp(("arbitrary",)),
    )(dc, u, bias_pad)


ATTN_HEADS = 4


def _attn_block(T):
    return 256 if T % 256 == 0 and T >= 512 else 128


def _attn_fwd(u, c3, T):
    H, HP = 8, ATTN_HEADS
    bq = _attn_block(T)
    nq = T // bq
    scale = FOX_HD ** -0.5
    lanes = [slice(h * LANE, (h + 1) * LANE) for h in range(HP)]

    def kern(q_ref, k_ref, v_ref, z_ref, cq_ref, ck_ref, o_ref, oa_ref, lse_ref):
        i = pl.program_id(1)
        q = [(q_ref[:, ln] * scale).astype(BF16) for ln in lanes]
        c0 = [cq_ref[h][:, 0:1] for h in range(HP)]

        def step(j, carry, diagonal=False):
            off = pl.multiple_of(j * bq, bq)
            s = [_nt(q[h], k_ref[pl.ds(off, bq), lanes[h]].astype(BF16)) + (c0[h] - ck_ref[h, :, pl.ds(off, bq)])
                 for h in range(HP)]
            ps, out = [], []
            for h in range(HP):
                m, l, acc = carry[h]
                sh = s[h]
                if diagonal:
                    sh = jnp.where(_iota2((bq, bq), 1) <= _iota2((bq, bq), 0), sh, NEG)
                m_new = jnp.maximum(m, jnp.max(sh, axis=1, keepdims=True))
                p = jnp.exp(sh - m_new)
                alpha = jnp.exp(m - m_new)
                p_hi = p.astype(BF16)
                ps.append((p_hi, (p - p_hi.astype(F32)).astype(BF16)))
                out.append((m_new, alpha * l + jnp.sum(p, axis=1, keepdims=True), alpha * acc))
            res = []
            for h, (m, l, acc) in enumerate(out):
                vj = v_ref[pl.ds(off, bq), lanes[h]].astype(BF16)
                res.append((m, l, acc + _nn(ps[h][0], vj) + _nn(ps[h][1], vj)))
            return tuple(res)

        init = tuple((jnp.full((bq, 1), NEG, F32), jnp.zeros((bq, 1), F32), jnp.zeros((bq, FOX_HD), F32)) for _ in range(HP))
        res = step(i, lax.fori_loop(0, i, step, init), diagonal=True)
        for h, (m, l, acc) in enumerate(res):
            o = acc / l
            z = z_ref[:, lanes[h]]
            o_ref[:, lanes[h]] = o
            oa_ref[:, lanes[h]] = (o * z * _sigmoid(z)).astype(BF16)
            lse_ref[h] = m + jnp.log(l)

    W = HP * LANE
    return pl.pallas_call(
        kern, name="fox_attn_fwd",
        out_shape=[jax.ShapeDtypeStruct((T, 1024), F32), jax.ShapeDtypeStruct((T, 1024), BF16),
                   jax.ShapeDtypeStruct((H, T, 1), F32)],
        grid=(H // HP, nq),
        in_specs=[pl.BlockSpec((bq, W), lambda g, i: (i, g)),
                  pl.BlockSpec((T, W), lambda g, i: (0, 8 // HP + g)),
                  pl.BlockSpec((T, W), lambda g, i: (0, 16 // HP + g)),
                  pl.BlockSpec((bq, W), lambda g, i: (i, 24 // HP + g)),
                  pl.BlockSpec((HP, 1, bq), lambda g, i: (g, 0, i)),
                  pl.BlockSpec((HP, 1, T), lambda g, i: (g, 0, 0))],
        out_specs=[pl.BlockSpec((bq, W), lambda g, i: (i, g)),
                   pl.BlockSpec((bq, W), lambda g, i: (i, g)),
                   pl.BlockSpec((HP, bq, 1), lambda g, i: (g, i, 0))],
        compiler_params=_cp(("parallel", "arbitrary")),
    )(u, u, u, u, c3, c3)


def _attn_probs(s, lse_i, diagonal):
    if not diagonal:
        return jnp.exp(s - lse_i)
    keep = _iota2(s.shape, 1) <= _iota2(s.shape, 0)
    return jnp.where(keep, jnp.exp(jnp.where(keep, s, NEG) - lse_i), 0.0)


def _attn_pre_math(doa, z, o):
    sg = _sigmoid(z)
    do = (doa * z * sg).astype(BF16)
    dz = doa * o * (sg * (1.0 + z * (1.0 - sg)))
    head_of = (_iota2((o.shape[1], LANE), 0) // FOX_HD == _iota2((o.shape[1], LANE), 1)).astype(F32)
    return do, dz, _nn(do.astype(F32) * o, head_of, HI)


def _attn_bwd(u, c3, lse, do, delta, T):
    H, HP = 8, ATTN_HEADS
    bq = _attn_block(T)
    nq = T // bq
    scale = FOX_HD ** -0.5
    lanes = [slice(h * LANE, (h + 1) * LANE) for h in range(HP)]

    def kern(q_ref, k_ref, v_ref, c_ref, lse_ref, do_ref, dl_ref, dq_ref, dk_ref, dv_ref, dc_ref):
        j = pl.program_id(1)

        @pl.when(j == 0)
        def _():
            dq_ref[...] = jnp.zeros(dq_ref.shape, F32)

        kj = [k_ref[:, ln].astype(BF16) for ln in lanes]
        vj = [v_ref[:, ln].astype(BF16) for ln in lanes]
        joff = pl.multiple_of(j * bq, bq)
        ck = [c_ref[h, :, pl.ds(joff, bq)] for h in range(HP)]

        def step(i, carry, diagonal=False):
            off = pl.multiple_of(i * bq, bq)
            qs = [(q_ref[pl.ds(off, bq), ln] * scale).astype(BF16) for ln in lanes]
            dob = [do_ref[pl.ds(off, bq), ln] for ln in lanes]
            s = [_nt(qs[h], kj[h]) + (c_ref[h, :, pl.ds(off, bq)][:, 0:1] - ck[h]) for h in range(HP)]
            dp = [_nt(dob[h], vj[h]) for h in range(HP)]
            pb, dsb, dcs = [], [], []
            for h in range(HP):
                p = _attn_probs(s[h], lse_ref[h, pl.ds(off, bq), :], diagonal)
                ds = p * (dp[h] - dl_ref[h, pl.ds(off, bq), :])
                pb.append(p.astype(BF16))
                dsb.append(ds.astype(BF16))
                dcs.append(jnp.sum(ds, axis=0, keepdims=True))
            out = []
            for h, (dk, dv, dc) in enumerate(carry):
                dq_ref[pl.ds(off, bq), lanes[h]] += _nn(dsb[h], kj[h]) * scale
                out.append((dk + _tn(dsb[h], qs[h]), dv + _tn(pb[h], dob[h]), dc - dcs[h]))
            return tuple(out)

        init = tuple((jnp.zeros((bq, FOX_HD), F32), jnp.zeros((bq, FOX_HD), F32), jnp.zeros((1, bq), F32)) for _ in range(HP))
        res = lax.fori_loop(j + 1, nq, step, step(j, init, diagonal=True))
        for h, (dk, dv, dc) in enumerate(res):
            dk_ref[:, lanes[h]] = dk
            dv_ref[:, lanes[h]] = dv
            dc_ref[h] = dc

    W = HP * LANE
    full = lambda cb: pl.BlockSpec((T, W), lambda g, j: (0, cb // HP + g))
    blk = lambda cb: pl.BlockSpec((bq, W), lambda g, j: (j, cb // HP + g))
    col1 = pl.BlockSpec((HP, T, 1), lambda g, j: (g, 0, 0))
    return pl.pallas_call(
        kern, name="fox_attn_bwd",
        out_shape=[jax.ShapeDtypeStruct((T, 1024), F32)] * 3 + [jax.ShapeDtypeStruct((H, 1, T), F32)],
        grid=(H // HP, nq),
        in_specs=[full(0), blk(8), blk(16), pl.BlockSpec((HP, 1, T), lambda g, j: (g, 0, 0)), col1, full(0), col1],
        out_specs=[full(0), blk(0), blk(0), pl.BlockSpec((HP, 1, bq), lambda g, j: (g, 0, j))],
        compiler_params=_cp(("parallel", "arbitrary")),
    )(u, u, u, c3, lse, do, delta)


def _place():
    return lax.axis_index("x"), lax.axis_index("y"), lax.axis_index("c")


def _slot(p):
    return 4 * p[0] + 2 * p[1] + p[2]


def _other_chips(x, y):
    return [(1 - x, y), (x, 1 - y), (1 - x, 1 - y)]


def _allgather_steps(in_refs, out_refs, scratch):
    (src,), (dst,) = in_refs, out_refs
    send_sems, recv_sems, local_sem = scratch
    x, y, c = _place()
    me, sibling = (x, y, c), (x, y, 1 - c)
    chips = _other_chips(x, y)

    def copy(k, block, to, from_input=False):
        d = dst.at[_slot(block)]
        return pltpu.make_async_remote_copy(
            src_ref=src if from_input else d, dst_ref=d, send_sem=send_sems.at[k], recv_sem=recv_sems.at[k],
            device_id=to, device_id_type=MESH)

    def first_copies():
        return [copy(0, me, sibling, True)] + [copy(1 + j, me, (*chip, c), True) for j, chip in enumerate(chips)]

    def start():
        pltpu.make_async_copy(src, dst.at[_slot(me)], local_sem).start()
        for cp in first_copies():
            cp.start()

    def finish():
        passed = []
        for j, chip in enumerate(chips):
            copy(1 + j, (*chip, c), me).wait_recv()
            passed.append(copy(4 + j, (*chip, c), sibling))
            passed[-1].start()
        copy(0, sibling, me).wait_recv()
        for j, chip in enumerate(chips):
            copy(4 + j, (*chip, 1 - c), me).wait_recv()
        for cp in first_copies() + passed:
            cp.wait_send()
        pltpu.make_async_copy(src, dst.at[_slot(me)], local_sem).wait()

    return start, finish


def _allgather_relay_steps(in_refs, out_refs, scratch):
    (src,), (dst,) = in_refs, out_refs
    send_sems, recv_sems, local_sem = scratch
    x, y, c = _place()
    me, sibling = (x, y, c), (x, y, 1 - c)
    x_nbr, y_nbr, diag = (1 - x, y, c), (x, 1 - y, c), (1 - x, 1 - y, c)
    flip = lambda a, bit: a + bit - 2 * a * bit
    relay_from = (flip(x, 1 - c), flip(y, c), c)
    relay_to = (flip(x, c), flip(y, 1 - c), c)

    def copy(k, block, to, from_input=False):
        d = dst.at[_slot(block)]
        return pltpu.make_async_remote_copy(
            src_ref=src if from_input else d, dst_ref=d, send_sem=send_sems.at[k], recv_sem=recv_sems.at[k],
            device_id=to, device_id_type=MESH)

    def first_copies():
        return [copy(0, me, sibling, True), copy(1, me, x_nbr, True), copy(2, me, y_nbr, True)]

    def other(block):
        return block[:2] + (1 - c,)

    def start():
        pltpu.make_async_copy(src, dst.at[_slot(me)], local_sem).start()
        for cp in first_copies():
            cp.start()

    def finish():
        copy(1, x_nbr, me).wait_recv()
        copy(2, y_nbr, me).wait_recv()
        later = [copy(3, relay_from, relay_to), copy(4, x_nbr, sibling), copy(5, y_nbr, sibling)]
        for cp in later:
            cp.start()
        copy(3, diag, me).wait_recv()
        later.append(copy(6, diag, sibling))
        later[-1].start()
        copy(0, sibling, me).wait_recv()
        for k, block in ((4, x_nbr), (5, y_nbr), (6, diag)):
            copy(k, other(block), me).wait_recv()
        for cp in first_copies() + later:
            cp.wait_send()
        pltpu.make_async_copy(src, dst.at[_slot(me)], local_sem).wait()

    return start, finish


def _allgather_side(a, relay=False):
    return ((a,), (jax.ShapeDtypeStruct((N_DEV,) + a.shape, a.dtype),),
            (pltpu.SemaphoreType.DMA((7,)), pltpu.SemaphoreType.DMA((7,)), pltpu.SemaphoreType.DMA),
            _allgather_relay_steps if relay else _allgather_steps)


def _allgather(a, name, relay=False):
    ins, outs, scratch, make = _allgather_side(a, relay)

    def body(a_ref, o_ref, *scr):
        start, finish = make((a_ref,), (o_ref,), scr)
        start()
        finish()

    any_spec = pl.BlockSpec(memory_space=pl.ANY)
    return pl.pallas_call(body, name=name, out_shape=outs[0], in_specs=[any_spec], out_specs=any_spec,
                          scratch_shapes=list(scratch))(a)


def _exchange_pair_steps(in_refs, out_refs, scratch):
    (src,), (dst,) = in_refs, out_refs
    send_sems, recv_sems = scratch
    x, y, c = _place()
    sibling = (x, y, 1 - c)
    slots = [_slot(sibling)] + [_slot((*chip, 1 - c)) for chip in _other_chips(x, y)]

    def copies():
        return [pltpu.make_async_remote_copy(
            src_ref=src.at[ps], dst_ref=dst.at[k], send_sem=send_sems.at[k], recv_sem=recv_sems.at[k],
            device_id=sibling, device_id_type=MESH) for k, ps in enumerate(slots)]

    def start():
        for cp in copies():
            cp.start()

    def finish():
        for cp in copies():
            cp.wait()

    return start, finish


def _exchange_pair_side(g):
    return ((g,), (jax.ShapeDtypeStruct((4,) + g.shape[1:], g.dtype),),
            (pltpu.SemaphoreType.DMA((4,)), pltpu.SemaphoreType.DMA((4,))), _exchange_pair_steps)


def _exchange_pair(g):
    ins, outs, scratch, make = _exchange_pair_side(g)

    def body(g_ref, r_ref, *scr):
        start, finish = make((g_ref,), (r_ref,), scr)
        start()
        finish()

    any_spec = pl.BlockSpec(memory_space=pl.ANY)
    return pl.pallas_call(body, name="exchange_pair", out_shape=outs[0], in_specs=[any_spec], out_specs=any_spec,
                          scratch_shapes=list(scratch))(g)


def _tiling(R, Cc, tile, by_cols):
    if by_cols:
        assert Cc % tile == 0
        return Cc // tile, (R, tile), lambda lead, i: (lead, 0, i)
    assert R % tile == 0
    return R // tile, (tile, Cc), lambda lead, i: (lead, i, 0)


def _pair_add(name, g, r1, slots, tile, by_cols=False, side=None):
    _, R, Cc = g.shape
    steps, blk, at = _tiling(R, Cc, tile, by_cols)
    side_ins, side_outs, side_scr, side_make = side if side is not None else ((), (), (), None)
    n_si, n_so = len(side_ins), len(side_outs)

    def kern(s_ref, a_ref, b_ref, *rest):
        o_ref = rest[n_si]
        if side_make is not None:
            start, finish = side_make(rest[:n_si], rest[n_si + 1:n_si + 1 + n_so], rest[n_si + 1 + n_so:])
            j, i = pl.program_id(0), pl.program_id(1)
            pl.when(jnp.logical_and(j == 0, i == 0))(start)
        o_ref[...] = (a_ref[...].astype(F32) + b_ref[...].astype(F32)).astype(o_ref.dtype)
        if side_make is not None:
            pl.when(jnp.logical_and(j == 2, i == steps - 1))(finish)

    any_spec = pl.BlockSpec(memory_space=pl.ANY)
    res = pl.pallas_call(
        kern, name=name,
        out_shape=[jax.ShapeDtypeStruct((3, R, Cc), BF16)] + list(side_outs),
        grid_spec=pltpu.PrefetchScalarGridSpec(
            num_scalar_prefetch=1, grid=(3, steps),
            in_specs=[pl.BlockSpec((1,) + blk, lambda j, i, s: at(s[j], i)),
                      pl.BlockSpec((1,) + blk, lambda j, i, s: at(1 + j, i))] + [any_spec] * n_si,
            out_specs=[pl.BlockSpec((1,) + blk, lambda j, i, s: at(j, i))] + [any_spec] * n_so,
            scratch_shapes=list(side_scr)),
        compiler_params=_cp(("arbitrary", "arbitrary")),
    )(slots, g, r1, *side_ins)
    return res if side is not None else res[0]


def _axis_neighbours():
    x, y, c = _place()
    flip = lambda a, bit: a + bit - 2 * a * bit
    return (flip(x, c), flip(y, 1 - c), c), (flip(x, 1 - c), flip(y, c), c), c


def _exchange_first_steps(in_refs, out_refs, scratch):
    pairs = list(zip(in_refs, out_refs))
    send_sems, recv_sems = scratch
    first, _, c = _axis_neighbours()

    def copies():
        return [pltpu.make_async_remote_copy(
            src_ref=src.at[j], dst_ref=dst.at[k], send_sem=send_sems.at[t, k], recv_sem=recv_sems.at[t, k],
            device_id=first, device_id_type=MESH)
            for t, (src, dst) in enumerate(pairs) for k, j in enumerate((1 - c, 2))]

    def start():
        for cp in copies():
            cp.start()

    def finish():
        for cp in copies():
            cp.wait()

    return start, finish


def _exchange_first_side(*ss):
    n = len(ss)
    return (ss, tuple(jax.ShapeDtypeStruct((2,) + s.shape[1:], s.dtype) for s in ss),
            (pltpu.SemaphoreType.DMA((n, 2)), pltpu.SemaphoreType.DMA((n, 2))), _exchange_first_steps)


def _axis_add(name, s, t1, core, tile, by_cols=False):
    _, R, Cc = s.shape
    steps, blk, at = _tiling(R, Cc, tile, by_cols)

    def kern(c_ref, a_ref, b_ref, o_ref):
        o_ref[...] = (a_ref[...].astype(F32) + b_ref[...].astype(F32)).astype(o_ref.dtype)

    return pl.pallas_call(
        kern, name=name,
        out_shape=jax.ShapeDtypeStruct((1, R, Cc), BF16),
        grid_spec=pltpu.PrefetchScalarGridSpec(
            num_scalar_prefetch=1, grid=(steps,),
            in_specs=[pl.BlockSpec((1,) + blk, lambda i, cr: at(cr[0], i)),
                      pl.BlockSpec((1,) + blk, lambda i, cr: at(1, i))],
            out_specs=pl.BlockSpec((1,) + blk, lambda i, cr: at(0, i))),
        compiler_params=_cp(("arbitrary",)),
    )(core, s, t1)


def _exchange_second_steps(in_refs, out_refs, scratch):
    send_sems, recv_sems = scratch
    _, second, _ = _axis_neighbours()

    def copies():
        return [pltpu.make_async_remote_copy(src_ref=src, dst_ref=dst, send_sem=send_sems.at[t], recv_sem=recv_sems.at[t],
                                             device_id=second, device_id_type=MESH)
                for t, (src, dst) in enumerate(zip(in_refs, out_refs))]

    def start():
        for cp in copies():
            cp.start()

    def finish():
        for cp in copies():
            cp.wait()

    return start, finish


def _exchange_second_side(*ps):
    n = len(ps)
    return (ps, tuple(jax.ShapeDtypeStruct(p.shape, p.dtype) for p in ps),
            (pltpu.SemaphoreType.DMA((n,)), pltpu.SemaphoreType.DMA((n,))), _exchange_second_steps)


def _adamw(name, w, m, v, parts, tile, by_cols=False):
    R, Cc = w.shape
    steps, blk_shape, at = _tiling(R, Cc, tile, by_cols)
    n_parts = len(parts)

    def kern(*refs):
        w_ref, m_ref, v_ref = refs[:3]
        g = None
        for r_ref, (_, n) in zip(refs[3:3 + n_parts], parts):
            for s in range(n):
                term = r_ref[s].astype(F32)
                g = term if g is None else g + term
        g_out, d_out, m_out, v_out = refs[3 + n_parts:]
        mn = ADAM_B1 * m_ref[...] + (1.0 - ADAM_B1) * g
        vn = ADAM_B2 * v_ref[...] + (1.0 - ADAM_B2) * (g * g)
        m_hat = mn / (1.0 - ADAM_B1 ** ADAM_STEP)
        v_hat = vn / (1.0 - ADAM_B2 ** ADAM_STEP)
        g_out[...] = g
        d_out[...] = -ADAM_LR * (m_hat / (jnp.sqrt(v_hat) + ADAM_EPS) + ADAM_WD * w_ref[...])
        m_out[...] = mn
        v_out[...] = vn

    blk = pl.BlockSpec(blk_shape, lambda i: at(0, i)[1:])
    return pl.pallas_call(
        kern, name=name,
        out_shape=[jax.ShapeDtypeStruct((R, Cc), F32)] * 4,
        grid=(steps,),
        in_specs=[blk] * 3 + [pl.BlockSpec((n,) + blk_shape, lambda i: at(0, i)) for (_, n) in parts],
        out_specs=[blk] * 4,
        compiler_params=_cp(("arbitrary",)),
    )(w, m, v, *[a for (a, _) in parts])


def _assemble_columns(blocks, pieces, zeros, width):
    _, R, Cc = blocks.shape
    tr = min(256, R)

    def kern(b_ref, o_ref):
        for col, n in zeros:
            o_ref[:, col:col + n] = jnp.zeros((tr, n), o_ref.dtype)
        for col, d, lo, n in pieces:
            o_ref[:, col:col + n] = b_ref[d, :, lo:lo + n]

    return pl.pallas_call(
        kern, name="assemble_w_in",
        out_shape=jax.ShapeDtypeStruct((R, width), blocks.dtype),
        grid=(R // tr,),
        in_specs=[pl.BlockSpec((N_DEV, tr, Cc), lambda i: (0, i, 0))],
        out_specs=pl.BlockSpec((tr, width), lambda i: (i, 0)),
        compiler_params=_cp(("parallel",)),
    )(blocks)


def _split_rows(x, pieces, rows, side):
    _, Cc = x.shape
    tc = min(256, Cc)
    side_ins, side_outs, side_scr, side_make = side
    n_si, n_so = len(side_ins), len(side_outs)

    def kern(x_ref, *rest):
        o_ref = rest[n_si]
        start, finish = side_make(rest[:n_si], rest[n_si + 1:n_si + 1 + n_so], rest[n_si + 1 + n_so:])
        pl.when(pl.program_id(0) == 0)(start)
        for d, lo, row, n in pieces:
            o_ref[d, lo:lo + n, :] = x_ref[row:row + n, :]
        pl.when(pl.program_id(0) == Cc // tc - 1)(finish)

    any_spec = pl.BlockSpec(memory_space=pl.ANY)
    return pl.pallas_call(
        kern, name="split_w_in_grad",
        out_shape=[jax.ShapeDtypeStruct((N_DEV, rows, Cc), x.dtype)] + list(side_outs),
        grid=(Cc // tc,),
        in_specs=[pl.BlockSpec((x.shape[0], tc), lambda i: (0, i))] + [any_spec] * n_si,
        out_specs=[pl.BlockSpec((N_DEV, rows, tc), lambda i: (0, 0, i))] + [any_spec] * n_so,
        scratch_shapes=list(side_scr),
        compiler_params=_cp(("arbitrary",)),
    )(x, *side_ins)


def _pad_cols(a, w):
    return jnp.pad(a, ((0, 0), (0, w - a.shape[1])))


def _pad_rows(a, r):
    return jnp.pad(a, ((0, r - a.shape[0]), (0, 0)))


def _pack_b(pf, pr, wo, w2, a2, rows):
    body = jnp.concatenate([pf, pr, wo.reshape(2048, 256), jnp.concatenate([w2, a2], axis=1)], axis=0)
    return _pad_rows(body, rows)


def kernel(x, norm_gain, w_in, fox_forget_bias, rwkv_shift_mix, rwkv_w0, rwkv_w2, rwkv_a0, rwkv_a2, rwkv_k_k, rwkv_k_a, rwkv_r_k, rwkv_ln_w, rwkv_ln_b, w_proj_fox, w_proj_rwkv, w_out, final_norm_gain, loss_target, m_norm_gain, m_w_in, m_fox_forget_bias, m_rwkv_shift_mix, m_rwkv_w0, m_rwkv_w2, m_rwkv_a0, m_rwkv_a2, m_rwkv_k_k, m_rwkv_k_a, m_rwkv_r_k, m_rwkv_ln_w, m_rwkv_ln_b, m_w_proj_fox, m_w_proj_rwkv, m_w_out, m_final_norm_gain, v_norm_gain, v_w_in, v_fox_forget_bias, v_rwkv_shift_mix, v_rwkv_w0, v_rwkv_w2, v_rwkv_a0, v_rwkv_a2, v_rwkv_k_k, v_rwkv_k_a, v_rwkv_r_k, v_rwkv_ln_w, v_rwkv_ln_b, v_w_proj_fox, v_w_proj_rwkv, v_w_out, v_final_norm_gain):
    T, D = x.shape[1], x.shape[2]
    assert D == 2048 and T % LANE == 0
    NI = w_in.shape[2]
    IN = N_DEV * NI
    RB = 4224
    x2 = x[0]
    lt2 = loss_target[0]
    me = _slot(_place())

    tb = min(256, T)
    tbh = min(128, T)
    h, wa = _rowcall("rms_fwd", T, tb, [_rows(x2, tb, D), _whole(norm_gain)],
                     lambda i, xv, g: ([_rms_math(xv, g)], []), [(D, BF16)],
                     side=_allgather_side(w_in[0].astype(BF16), relay=True))
    packed_own = _pack_b(w_proj_fox[0], w_proj_rwkv[0], w_out[0], rwkv_w2[0], rwkv_a2[0], RB).astype(BF16)
    sections = [(0, 4096, 0), (4104, 4096, 4096), (8392, 4096, 8192), (4096, 8, 12288), (8200, 96, 12544), (8296, 96, 12672)]
    NP = 12800
    pieces, zeros, at_col = [], [], 0
    for lo, width, pad_lo in sections:
        if pad_lo > at_col:
            zeros.append((at_col, pad_lo - at_col))
        col = lo
        while col < lo + width:
            d = col // NI
            stop = min(lo + width, (d + 1) * NI)
            pieces.append((pad_lo + col - lo, d, col - d * NI, stop - col))
            col = stop
        at_col = pad_lo + width
    zeros.append((at_col, NP - at_col))
    w_pad = _assemble_columns(wa, pieces, zeros, NP)
    F_CB, LORA_CB = 96, 49

    mu = rwkv_shift_mix
    mu_main = mu[:, 0:4096]
    mu_lora = jnp.concatenate([_pad_cols(mu[:, 4096:4192], LANE), _pad_cols(mu[:, 4192:4288], LANE)], axis=1)
    bias_pad = _pad_cols(fox_forget_bias, LANE)
    rk_flat = rwkv_r_k.reshape(1, 1024)
    gf = final_norm_gain.reshape(1, D)

    u, wb = _mm(h, w_pad, tm=1024, tn=1280, name="mm_in", side=_allgather_side(packed_own))
    wpf = wb[:, 0:1024, :].transpose(1, 0, 2).reshape(1024, D)
    wpr = wb[:, 1024:2048, :].transpose(1, 0, 2).reshape(1024, D)
    wo = wb[:, 2048:4096, :].reshape(N_DEV * 256, D)
    w2p = _pad_rows(wb[:, 4096:4192, 0:128].transpose(1, 0, 2).reshape(96, 1024).astype(F32), LANE)
    a2p = _pad_rows(wb[:, 4096:4192, 128:256].transpose(1, 0, 2).reshape(96, 1024).astype(F32), LANE)

    c8 = _gates_fwd(u, bias_pad, T, F_CB)
    c3 = c8.reshape(8, 1, T)
    o_raw, o_a, lse = _attn_fwd(u, c3, T)

    def shift_body(i, um, hm, ul, hl, mm_, ml):
        outs = []
        for uv, hv, mv in ((um, hm, mm_), (ul, hl, ml)):
            hv = jnp.where(i == 0, 0.0, hv)
            prev = pltpu.roll(jnp.concatenate([hv, uv], axis=0), 1, 0)[SUB:]
            outs.append(uv + (prev - uv) * mv)
        return outs, []

    def halo_prev(arr, w, cb):
        return (arr, (SUB, w), lambda i: (jnp.maximum(i * (tbh // SUB) - 1, 0), cb))

    xs, xl = _rowcall("rwkv_shift_fwd", T, tbh,
                      [_rows(u, tbh, 4096, 1), halo_prev(u, 4096, 1), _rows(u, tbh, 256, LORA_CB), halo_prev(u, 256, LORA_CB),
                       _whole(mu_main), _whole(mu_lora)],
                      shift_body, [(4096, F32), (256, F32)])

    prep_par = [_whole(rwkv_w0), _whole(rwkv_a0), _whole(rwkv_k_k), _whole(rwkv_k_a), _whole(w2p), _whole(a2p)]
    prep_rows = [_rows(xs, tbh, 1024, 1), _rows(xl, tbh, LANE, 0), _rows(xl, tbh, LANE, 1)]
    lw, kp, an, bb = _rowcall("rwkv_prep_fwd", T, tbh, prep_rows + prep_par,
                              lambda i, *a: (list(_prep_math(*a)), []), [(1024, F32)] * 4)
    y, states, invs = _scan_fwd(xs, lw, kp, an, bb, T)
    post_rows = [_rows(y, tbh, 1024), _rows(xs, tbh, 1024, 0), _rows(kp, tbh, 1024), _rows(xs, tbh, 1024, 2), _rows(xs, tbh, 1024, 3)]
    post_par = [_whole(rwkv_ln_w), _whole(rwkv_ln_b), _whole(rk_flat)]
    (o_b,) = _rowcall("rwkv_post_fwd", T, tbh, post_rows + post_par,
                      lambda i, *a: ([_post_math(*a)], []), [(1024, BF16)])

    pa = _mm(o_a, wpf, name="mm_proj_fox")
    pb = _mm(o_b, wpr, name="mm_proj_rwkv")
    merge_rows = [_rows(u, tb, D, 4), _rows(u, tb, D, 5), _rows(pa, tb, D), _rows(pb, tb, D)]
    (mg,) = _rowcall("merge_fwd", T, tb, merge_rows, lambda i, *a: ([_merge_math(*a)], []), [(D, BF16)])
    mo = _mm(mg, wo, name="mm_out")

    def head_body(i, xv, mov, ltv, g):
        out = xv + mov
        r = lax.rsqrt(jnp.mean(out * out, axis=-1, keepdims=True) + RMS_EPS)
        yn = out * r
        err = yn * g - ltv
        loss = 0.5 * jnp.sum(jnp.sum(err * err, axis=-1, keepdims=True), axis=0, keepdims=True) / D
        dyv = err / D
        dyn = dyv * g
        dout = r * (dyn - yn * jnp.mean(dyn * yn, axis=-1, keepdims=True))
        return [dout], [loss, jnp.sum(dyv * yn, axis=0, keepdims=True)]

    dout, loss_p, dgf_p = _rowcall("loss_head", T, tb, [_rows(x2, tb, D), _rows(mo, tb, D), _rows(lt2, tb, D), _whole(gf)],
                                   head_body, [(D, F32)], [(1, 1), (1, D)])

    dm = _mm(dout, wo, tb=True, name="mm_out_dx")
    dwo = _mm(mg, dout, ta=True, out_dtype=BF16, name="mm_out_dw")

    def merge_bwd_body(i, ga, gb, pav, pbv, dmv):
        _, vjp = jax.vjp(_merge_math, ga, gb, pav, pbv)
        dga, dgb, dpa, dpb = vjp(dmv)
        return [dga, dgb, dpa, dpb], []

    dga, dgb, dpa, dpb = _rowcall("merge_bwd", T, tb, merge_rows + [_rows(dm, tb, D)], merge_bwd_body,
                                  [(D, BF16), (D, BF16), (D, BF16), (D, BF16)])
    doa = _mm(dpa, wpf, tb=True, name="mm_proj_fox_dx")
    dwpf = _mm(o_a, dpa, ta=True, out_dtype=BF16, name="mm_proj_fox_dw")
    dob = _mm(dpb, wpr, tb=True, name="mm_proj_rwkv_dx")
    dwpr = _mm(o_b, dpb, ta=True, out_dtype=BF16, name="mm_proj_rwkv_dw")

    do_b, dza, delta128 = _rowcall("fox_attn_pre", T, tb, [_rows(doa, tb, 1024), _rows(u, tb, 1024, 3), _rows(o_raw, tb, 1024)],
                                   lambda i, *a: (list(_attn_pre_math(*a)), []), [(1024, BF16), (1024, F32), (LANE, F32)])
    delta = delta128[:, 0:8].T.reshape(8, T, 1)
    dq, dk, dv, dc3 = _attn_bwd(u, c3, lse, do_b, delta, T)
    dfl, dbias_p = _gates_bwd(dc3.reshape(8, T), u, bias_pad, T, F_CB)

    def post_bwd_body(i, yv, rv, kpv, vv, zv, lnw, lnb, rkv, dobv):
        _, vjp = jax.vjp(_post_math, yv, rv, kpv, vv, zv, lnw, lnb, rkv)
        dy_, dr_, dkp_, dv_, dz_, dlnw, dlnb, drk = vjp(dobv)
        return [dy_, dr_, dkp_, dv_, dz_], [dlnw, dlnb, drk]

    dy_s, dr_p, dkp_p, dv_p, dzb, dlnw_p, dlnb_p, drk_p = _rowcall(
        "rwkv_post_bwd", T, tbh, post_rows + post_par + [_rows(dob, tbh, 1024)], post_bwd_body,
        [(1024, F32)] * 5, [(1, 1024)] * 3)
    dr_s, dlw, dkp_s, dv_s, dan, dbb = _scan_bwd(xs, lw, kp, an, bb, states, invs, dy_s, T)

    def prep_bwd_body(i, xk, xwd, xad, w0, a0, kkw, kaw, w2v, a2v, dlw_, dkp1, dkp2, dan_, dbb_, dr1, dr2, dv1, dv2, dz_):
        _, vjp = jax.vjp(_prep_math, xk, xwd, xad, w0, a0, kkw, kaw, w2v, a2v)
        dxk, dxwd, dxad, dw0, da0, dkk, dka, dw2, da2 = vjp((dlw_, dkp1 + dkp2, dan_, dbb_))
        return [[dr1 + dr2, dxk, dv1 + dv2, dz_], [dxwd, dxad]], [dw0, da0, dkk, dka, dw2, da2]

    cots = [dlw, dkp_s, dkp_p, dan, dbb, dr_s, dr_p, dv_s, dv_p, dzb]
    dxs, dxl, dw0_p, da0_p, dkk_p, dka_p, dw2_p, da2_p = _rowcall(
        "rwkv_prep_bwd", T, tbh, prep_rows + prep_par + [_rows(c_, tbh, 1024) for c_ in cots], prep_bwd_body,
        [(4096, F32), (256, F32)], [(1, 1024)] * 4 + [(LANE, 1024)] * 2)

    def shift_bwd_body(i, dm_, hm, dl_, hl, um, pm, ul, pl_, mm_, ml):
        last = i == T // tbh - 1
        outs, accs = [], []
        for dv_, hv, uv, pv, mv in ((dm_, hm, um, pm, mm_), (dl_, hl, ul, pl_, ml)):
            hv = jnp.where(last, 0.0, hv)
            nxt = pltpu.roll(jnp.concatenate([dv_, hv], axis=0), tbh + SUB - 1, 0)[:tbh]
            pv = jnp.where(i == 0, 0.0, pv)
            prev = pltpu.roll(jnp.concatenate([pv, uv], axis=0), 1, 0)[SUB:]
            outs.append(dv_ * (1.0 - mv) + nxt * mv)
            accs.append(jnp.sum(dv_ * (prev - uv), axis=0, keepdims=True))
        return outs, accs

    def halo_next(arr, w, cb):
        last_blk = T // SUB - 1
        return (arr, (SUB, w), lambda i: (jnp.minimum((i + 1) * (tbh // SUB), last_blk), cb))

    du_b, du_l, dmu_main_p, dmu_lora_p = _rowcall(
        "rwkv_shift_bwd", T, tbh,
        [_rows(dxs, tbh, 4096), halo_next(dxs, 4096, 0), _rows(dxl, tbh, 256), halo_next(dxl, 256, 0),
         _rows(u, tbh, 4096, 1), halo_prev(u, 4096, 1), _rows(u, tbh, 256, LORA_CB), halo_prev(u, 256, LORA_CB),
         _whole(mu_main), _whole(mu_lora)],
        shift_bwd_body, [(4096, BF16), (256, BF16)], [(1, 4096), (1, 256)])

    lora_g = jnp.concatenate([dw2_p[:96].reshape(96, N_DEV, 128).transpose(1, 0, 2),
                              da2_p[:96].reshape(96, N_DEV, 128).transpose(1, 0, 2)], axis=2).astype(BF16)
    gb = jnp.concatenate([dwpf.reshape(1024, N_DEV, 256).transpose(1, 0, 2),
                          dwpr.reshape(1024, N_DEV, 256).transpose(1, 0, 2),
                          dwo.reshape(N_DEV, 2048, 256), lora_g, jnp.zeros((N_DEV, RB - 4192, 256), BF16)], axis=1)
    xx, yy, cc = _place()
    chip_slots = jnp.stack([_slot((*chip, cc)) for chip in _other_chips(xx, yy)]).astype(jnp.int32)
    core = jnp.stack([cc]).astype(jnp.int32)
    du, r1b = _rowcall("assemble_du", T, tb,
                       [_rows(a_, tb, a_.shape[1]) for a_ in (dq, dk, dv, dza, du_b, dga, dgb, dfl, du_l)],
                       lambda i, *a: ([list(a[:8]) + [jnp.zeros((tb, LANE), BF16), a[8]]], []), [(NP, BF16)],
                       side=_exchange_pair_side(gb))
    sb = _pair_add("pair_add_packed", gb, r1b, chip_slots, RB // 3)
    dw_pad_t, t1b = _mm(du, h, ta=True, out_dtype=BF16, tm=1280, tn=1024, name="mm_in_dw",
                        side=_exchange_first_side(sb))
    pb = _axis_add("axis_add_packed", sb, t1b, core, RB // 3)
    ga, t2b = _split_rows(dw_pad_t, [(d, lo, pad_col, n) for pad_col, d, lo, n in pieces], NI,
                          _exchange_second_side(pb))
    r1a = _exchange_pair(ga)
    sa = _pair_add("pair_add_w_in", ga, r1a, chip_slots, 512, by_cols=True)
    ga_own = lax.dynamic_index_in_dim(ga, me, 0, keepdims=True)
    gb_own = lax.dynamic_index_in_dim(gb, me, 0, keepdims=True)
    dh, t1a = _mm(du, w_pad, tb=True, tm=1024, tn=1024, tk=NP // 10, name="mm_in_dx", side=_exchange_first_side(sa))
    pa = _axis_add("axis_add_w_in", sa, t1a, core, 512, by_cols=True)

    def rms_bwd_body(i, xv, g, dhv, doutv):
        _, vjp = jax.vjp(_rms_math, xv, g)
        dx_, dg_ = vjp(dhv)
        return [dx_ + doutv], [dg_]

    grad_x2, dng_p, t2a = _rowcall(
        "rms_bwd", T, tb, [_rows(x2, tb, D), _whole(norm_gain), _rows(dh, tb, D), _rows(dout, tb, D)],
        rms_bwd_body, [(D, F32)], [(1, D)], side=_exchange_second_side(pa))

    w_in_outs = _adamw("adamw_w_in", w_in[0].T, m_w_in[0].T, v_w_in[0].T, [(ga_own, 1), (r1a, 1), (t1a, 1), (t2a, 1)], 256, by_cols=True)
    g_in, d_in, m_in, v_in = [o.T for o in w_in_outs]

    dmu = jnp.concatenate([dmu_main_p, dmu_lora_p[:, 0:96], dmu_lora_p[:, 128:224]], axis=1)
    small_parts = [dng_p, dbias_p[:, 0:8], dmu, dw0_p, da0_p, dkk_p, dka_p, drk_p, dlnw_p, dlnb_p, dgf_p, loss_p]
    SR = 128
    small = _pad_cols(jnp.concatenate(small_parts, axis=1), SR * LANE).reshape(SR, LANE)
    rs = _allgather(small, "allgather_small")
    pk = lambda pf, pr, wo_, w2_, a2_: _pack_b(pf[0], pr[0], wo_[0], w2_[0], a2_[0], RB)
    outs_b = _adamw("adamw_packed", pk(w_proj_fox, w_proj_rwkv, w_out, rwkv_w2, rwkv_a2),
                    pk(m_w_proj_fox, m_w_proj_rwkv, m_w_out, m_rwkv_w2, m_rwkv_a2),
                    pk(v_w_proj_fox, v_w_proj_rwkv, v_w_out, v_rwkv_w2, v_rwkv_a2), [(gb_own, 1), (r1b, 1), (t1b, 1), (t2b, 1)], RB // 3)

    def pack_small(ng, fb, sm, w0, a0, kk_, ka_, rk_, lnw, lnb, fg):
        parts = [ng, fb, sm, w0, a0, kk_, ka_, rk_.reshape(1, 1024), lnw, lnb, fg.reshape(1, D), jnp.zeros((1, 1), F32)]
        return _pad_cols(jnp.concatenate(parts, axis=1), SR * LANE).reshape(SR, LANE)

    outs_s = _adamw("adamw_small",
                    pack_small(norm_gain, fox_forget_bias, rwkv_shift_mix, rwkv_w0, rwkv_a0, rwkv_k_k, rwkv_k_a, rwkv_r_k,
                               rwkv_ln_w, rwkv_ln_b, final_norm_gain),
                    pack_small(m_norm_gain, m_fox_forget_bias, m_rwkv_shift_mix, m_rwkv_w0, m_rwkv_a0, m_rwkv_k_k, m_rwkv_k_a,
                               m_rwkv_r_k, m_rwkv_ln_w, m_rwkv_ln_b, m_final_norm_gain),
                    pack_small(v_norm_gain, v_fox_forget_bias, v_rwkv_shift_mix, v_rwkv_w0, v_rwkv_a0, v_rwkv_k_k, v_rwkv_k_a,
                               v_rwkv_r_k, v_rwkv_ln_w, v_rwkv_ln_b, v_final_norm_gain),
                    [(rs, N_DEV)], SR)

    def unpack_b(pkd):
        return dict(w_proj_fox=pkd[0:1024][None], w_proj_rwkv=pkd[1024:2048][None], w_out=pkd[2048:4096].reshape(1, 256, D),
                    rwkv_w2=pkd[4096:4192, 0:128][None], rwkv_a2=pkd[4096:4192, 128:256][None])

    def unpack_s(pkd):
        flat = pkd.reshape(1, SR * LANE)
        names = [("norm_gain", D), ("fox_forget_bias", 8), ("rwkv_shift_mix", 4288), ("rwkv_w0", 1024), ("rwkv_a0", 1024),
                 ("rwkv_k_k", 1024), ("rwkv_k_a", 1024), ("rwkv_r_k", 1024), ("rwkv_ln_w", 1024), ("rwkv_ln_b", 1024),
                 ("final_norm_gain", D), ("loss", 1)]
        out, off = {}, 0
        for nm, n in names:
            out[nm] = flat[:, off:off + n]
            off += n
        out["rwkv_r_k"] = out["rwkv_r_k"].reshape(1, 16, 64)
        out["final_norm_gain"] = out["final_norm_gain"].reshape(D)
        return out

    order = ["norm_gain", "w_in", "fox_forget_bias", "rwkv_shift_mix", "rwkv_w0", "rwkv_w2", "rwkv_a0", "rwkv_a2", "rwkv_k_k",
             "rwkv_k_a", "rwkv_r_k", "rwkv_ln_w", "rwkv_ln_b", "w_proj_fox", "w_proj_rwkv", "w_out", "final_norm_gain"]
    result = []
    loss = None
    for kind, big in enumerate((g_in, d_in, m_in, v_in)):
        d = {**unpack_b(outs_b[kind]), **unpack_s(outs_s[kind]), "w_in": big[None]}
        if kind == 0:
            loss = d["loss"].reshape(())
        result += [d[n] for n in order]
    return (loss, grad_x2[None], *result)
```

```python
import functools

import jax
import jax.numpy as jnp
from jax import lax
from jax.experimental import pallas as pl
from jax.experimental.pallas import tpu as pltpu

F32 = jnp.float32
BF16 = jnp.bfloat16
HI = lax.Precision.HIGHEST
H3 = lax.Precision.HIGH
MESH = pl.DeviceIdType.MESH

FOX_HD = 128
RW_HD = 64
RMS_EPS = 1e-6
GN_EPS = 64e-5
L2_EPS = 1e-12
ADAM_LR = 0.001
ADAM_B1 = 0.9
ADAM_B2 = 0.999
ADAM_EPS = 1e-08
ADAM_WD = 0.01
ADAM_STEP = 10

LANE = 128
SUB = 8
VMEM_LIMIT = 56 * 1024 * 1024
N_DEV = 8
CHUNK = 128
SCAN_GROUP = 1
SCAN_PAIRS = 8
PS = None
NEG = -1e30


def _scan_shape(T):
    c = min(CHUNK, T)
    return c, min(SCAN_GROUP, T // c)


def _cp(sem=None):
    return pltpu.CompilerParams(dimension_semantics=sem, vmem_limit_bytes=VMEM_LIMIT)


def _sigmoid(x):
    return jax.nn.sigmoid(x)


def _softplus(x):
    return jnp.maximum(x, 0.0) + jnp.log(1.0 + jnp.exp(-jnp.abs(x)))


def _nn(a, b, prec=None):
    return lax.dot_general(a, b, (((1,), (0,)), ((), ())), precision=prec, preferred_element_type=F32)


def _nt(a, b, prec=None):
    return lax.dot_general(a, b, (((1,), (1,)), ((), ())), precision=prec, preferred_element_type=F32)


def _tn(a, b, prec=None):
    return lax.dot_general(a, b, (((0,), (0,)), ((), ())), precision=prec, preferred_element_type=F32)


def _iota2(shape, dim):
    return lax.broadcasted_iota(jnp.int32, shape, dim)


def _seg_sum(x):
    r = _iota2((LANE, LANE), 0) // RW_HD
    c = _iota2((LANE, LANE), 1) // RW_HD
    bd = (r == c).astype(F32)
    parts = [_nn(x[:, j * LANE:(j + 1) * LANE], bd, H3) for j in range(x.shape[1] // LANE)]
    return parts[0] if len(parts) == 1 else jnp.concatenate(parts, axis=1)


def _mm(a, b, *, ta=False, tb=False, out_dtype=F32, tm=1024, tn=1024, tk=None, name, side=None):
    assert not (ta and tb)
    K, M = a.shape if ta else a.shape[::-1]
    N = b.shape[0] if tb else b.shape[1]
    tm, tn = min(tm, M), min(tn, N)
    tk = K if tk is None else tk
    nk = K // tk
    assert M % tm == 0 and N % tn == 0 and K % tk == 0
    gi, gj = M // tm, N // tn
    a_spec = pl.BlockSpec((tk, tm), lambda i, j, k: (k, i)) if ta else pl.BlockSpec((tm, tk), lambda i, j, k: (i, k))
    b_spec = pl.BlockSpec((tn, tk), lambda i, j, k: (j, k)) if tb else pl.BlockSpec((tk, tn), lambda i, j, k: (k, j))
    side_ins, side_outs, side_scr, side_make = side if side is not None else ((), (), (), None)
    n_si, n_so = len(side_ins), len(side_outs)
    n_acc = 0 if nk == 1 else 1

    def body(*refs):
        a_ref, b_ref = refs[:2]
        o_ref = refs[2 + n_si]
        scr = refs[3 + n_si + n_so:]
        k = pl.program_id(2)
        if side_make is not None:
            start, finish = side_make(refs[2:2 + n_si], refs[3 + n_si:3 + n_si + n_so], scr[n_acc:])
            first = jnp.logical_and(jnp.logical_and(pl.program_id(0) == 0, pl.program_id(1) == 0), k == 0)
            last = jnp.logical_and(jnp.logical_and(pl.program_id(0) == gi - 1, pl.program_id(1) == gj - 1), k == nk - 1)
            pl.when(first)(start)
        av = a_ref[...].astype(BF16)
        bv = b_ref[...].astype(BF16)
        p = _tn(av, bv) if ta else _nt(av, bv) if tb else _nn(av, bv)
        if nk == 1:
            o_ref[...] = p.astype(out_dtype)
        else:
            acc_ref = scr[0]

            @pl.when(k == 0)
            def _():
                acc_ref[...] = p

            @pl.when(k > 0)
            def _():
                acc_ref[...] += p

            @pl.when(k == nk - 1)
            def _():
                o_ref[...] = acc_ref[...].astype(out_dtype)
        if side_make is not None:
            pl.when(last)(finish)

    any_spec = pl.BlockSpec(memory_space=pl.ANY)
    res = pl.pallas_call(
        body, name=name,
        out_shape=[jax.ShapeDtypeStruct((M, N), out_dtype)] + list(side_outs),
        grid=(gi, gj, nk),
        in_specs=[a_spec, b_spec] + [any_spec] * n_si,
        out_specs=[pl.BlockSpec((tm, tn), lambda i, j, k: (i, j))] + [any_spec] * n_so,
        scratch_shapes=([] if nk == 1 else [pltpu.VMEM((tm, tn), F32)]) + list(side_scr),
        compiler_params=_cp(("arbitrary",) * 3 if side is not None else ("parallel", "parallel", "arbitrary")),
    )(a, b, *side_ins)
    return res if side is not None else res[0]


def _rows(arr, tb, w, cb=0):
    return (arr, (tb, w), lambda i: (i, cb))


def _whole(arr):
    nd = arr.ndim
    return (arr, arr.shape, lambda i: (0,) * nd)


def _rowcall(name, T, tb, ins, body, outs, accs=(), side=None):
    n_in, n_out, n_acc = len(ins), len(outs), len(accs)
    side_ins, side_outs, side_scr, side_make = side if side is not None else ((), (), (), None)
    n_si = len(side_ins)

    def kern(*refs):
        i = pl.program_id(0)
        if side_make is not None:
            side_refs = refs[n_in + n_si + n_out + n_acc:]
            start, finish = side_make(refs[n_in:n_in + n_si], side_refs[:len(side_outs)], side_refs[len(side_outs):])
            pl.when(i == 0)(start)
            refs = refs[:n_in] + refs[n_in + n_si:]
        vals = [r[...] for r in refs[:n_in]]
        ro, ao = body(i, *vals)
        for r, v in zip(refs[n_in:n_in + n_out], ro):
            if isinstance(v, (list, tuple)):
                off = 0
                for piece in v:
                    w = piece.shape[1]
                    r[:, off:off + w] = piece.astype(r.dtype)
                    off += w
            else:
                r[...] = v.astype(r.dtype)
        if accs:
            acc_refs = refs[n_in + n_out:n_in + n_out + n_acc]

            @pl.when(i == 0)
            def _():
                for r in acc_refs:
                    r[...] = jnp.zeros(r.shape, F32)

            for r, v in zip(acc_refs, ao):
                r[...] += v
        if side_make is not None:
            pl.when(i == T // tb - 1)(finish)

    any_spec = pl.BlockSpec(memory_space=pl.ANY)
    out_shape = [jax.ShapeDtypeStruct((T, w), dt) for (w, dt) in outs] + [jax.ShapeDtypeStruct(s, F32) for s in accs]
    out_specs = [pl.BlockSpec((tb, w), lambda i: (i, 0)) for (w, dt) in outs] + [pl.BlockSpec(s, lambda i: (0, 0)) for s in accs]
    res = pl.pallas_call(
        kern, name=name,
        out_shape=out_shape + list(side_outs),
        grid=(T // tb,),
        in_specs=[pl.BlockSpec(bs, im) for (_, bs, im) in ins] + [any_spec] * n_si,
        out_specs=out_specs + [any_spec] * len(side_outs),
        scratch_shapes=list(side_scr),
        compiler_params=_cp(("arbitrary",)),
    )(*[a for (a, _, _) in ins], *side_ins)
    return res


def _rms_math(x, g):
    r = lax.rsqrt(jnp.mean(x * x, axis=-1, keepdims=True) + RMS_EPS)
    return x * r * g


def _merge_math(ga, gb, pa, pb):
    return _sigmoid(ga) * pa + _sigmoid(gb) * pb


def _prep_math(xk, xwd, xad, w0, a0, kk_w, ka_w, w2p, a2p):
    z = w0 + _nn(jnp.tanh(xwd), w2p, H3)
    w = -_softplus(-z) - 0.5
    lw = -jnp.exp(w)
    ag = _sigmoid(a0 + _nn(xad, a2p, H3))
    p = xk * kk_w
    n = jnp.maximum(jnp.sqrt(_seg_sum(p * p)), L2_EPS)
    kk = p / n
    kp = xk * (1.0 + (ag - 1.0) * ka_w)
    return lw, kp, -kk, kk * ag


def _post_math(y, r, kp, v, z, lnw, lnb, rk):
    inv = 1.0 / RW_HD
    mu = _seg_sum(y) * inv
    d = y - mu
    var = _seg_sum(d * d) * inv
    yn = d * lax.rsqrt(var + GN_EPS) * lnw + lnb
    bonus = _seg_sum(r * kp * rk) * v
    return (yn + bonus) * (z * _sigmoid(z))


def _neumann(ms, depth):
    eye = (_iota2(ms[0].shape, 0) == _iota2(ms[0].shape, 1)).astype(F32)
    width = ms[0].shape[1]
    mp = [_nn(m, m, PS) for m in ms]
    inv = [eye + m for m in ms]
    n = 2
    while n < depth:
        last = 2 * n >= depth
        for i in range(len(ms)):
            if last:
                inv[i] = inv[i] + _nn(mp[i], inv[i], PS)
            else:
                z = _nn(mp[i], jnp.concatenate([mp[i], inv[i]], axis=1), PS)
                mp[i], inv[i] = z[:, :width], inv[i] + z[:, width:]
        n *= 2
    return inv


@functools.partial(jax.custom_vjp, nondiff_argnums=(1,))
def _unit_inverses(ms, depth):
    if depth <= RW_HD:
        return tuple(_neumann(list(ms), depth))
    h, n = depth // 2, len(ms)
    cat = jnp.concatenate
    z = jnp.zeros((h, h), F32)

    def heads(m, r, c):
        b0 = m[r * h:(r + 1) * h, c * h:(c + 1) * h]
        b1 = m[depth + r * h:depth + (r + 1) * h, depth + c * h:depth + (c + 1) * h]
        return cat([cat([b0, z], axis=1), cat([z, b1], axis=1)], axis=0)

    diag = _neumann([heads(m, 0, 0) for m in ms] + [heads(m, 1, 1) for m in ms], h)
    ta, td = diag[:n], diag[n:]
    low = [_nn(heads(m, 1, 0), a, PS) for m, a in zip(ms, ta)]
    low = [_nn(d, x, PS) for d, x in zip(td, low)]
    out = []
    for a, x, d in zip(ta, low, td):
        rows = []
        for hd in (0, 1):
            sl = slice(hd * h, (hd + 1) * h)
            top, bot = [a[sl, sl], z], [x[sl, sl], d[sl, sl]]
            pad = [z, z]
            rows.append(cat(top + pad if hd == 0 else pad + top, axis=1))
            rows.append(cat(bot + pad if hd == 0 else pad + bot, axis=1))
        out.append(cat(rows, axis=0))
    return tuple(out)


def _unit_inverses_fwd(ms, depth):
    inv = _unit_inverses(ms, depth)
    return inv, inv


def _unit_inverses_bwd(depth, inv, cts):
    left = [_tn(t, g, PS) for t, g in zip(inv, cts)]
    return (tuple(_nt(l, t, PS) for l, t in zip(left, inv)),)


_unit_inverses.defvjp(_unit_inverses_fwd, _unit_inverses_bwd)


@jax.custom_vjp
def _known_inverses(ms, inv):
    return inv


def _known_inverses_fwd(ms, inv):
    return inv, inv


def _known_inverses_bwd(inv, cts):
    dms = _unit_inverses_bwd(None, inv, cts)[0]
    return dms, tuple(jnp.zeros_like(t) for t in inv)


_known_inverses.defvjp(_known_inverses_fwd, _known_inverses_bwd)


def _scan_group(s0s, *flat, known_inv=None, with_inv=False):
    P = len(s0s)
    G = len(flat) // (6 * P)
    ch = [flat[6 * i:6 * i + 6] for i in range(P * G)]
    C = ch[0][0].shape[0]
    C2 = 2 * C
    cat = jnp.concatenate
    m0 = _iota2((1, LANE), 1) < RW_HD
    mask0 = m0.astype(F32)
    mask1 = 1.0 - mask0
    r2 = _iota2((C2, C2), 0)
    c2 = _iota2((C2, C2), 1)
    dist = r2 - c2
    in_head = dist <= r2 % C
    lower = (_iota2((C, C), 0) >= _iota2((C, C), 1)).astype(F32)
    bd = (_iota2((LANE, LANE), 0) // RW_HD) == (_iota2((LANE, LANE), 1) // RW_HD)

    def tri(m, strict):
        return jnp.where(dist > 0 if strict else dist >= 0, jnp.where(in_head, m, 0.0), 0.0)

    def sel(z):
        return jnp.where(m0, z[:C], z[C:])

    gs = [_nn(lower, c[1], HI) for c in ch]
    pre = []
    for (r, lw, k, v, a, b), g in zip(ch, gs):
        g_end = jnp.sum(lw, axis=0, keepdims=True)
        gm = g - jnp.sum(lw[:C // 2], axis=0, keepdims=True)
        en = jnp.exp(-gm)
        ec = jnp.exp(g_end - g)
        pre.append(dict(at=a * jnp.exp(g - lw), rt=r * jnp.exp(g), am=a * jnp.exp(gm - lw), rm=r * jnp.exp(gm),
                        bt=b * en, kt=k * en, bh=b * ec, kh=k * ec, dec=jnp.exp(g_end), v=v))
    grams = [_nt(cat([p["am"] * mask0, p["am"] * mask1, p["rm"] * mask0, p["rm"] * mask1], axis=0),
                 cat([p["bt"], p["bt"], p["kt"], p["kt"]], axis=0), PS) for p in pre]
    mab = tuple(tri(gm[:C2, :C2], True) for gm in grams)
    tinv = _unit_inverses(mab, C) if known_inv is None else _known_inverses(mab, known_inv)
    xv =[sel(_nn(tri(gm[:C2, C2:], True), cat([p["v"], p["v"]], axis=0), PS)) for gm, p in zip(grams, pre)]
    ys, s = [None] * (P * G), list(s0s)
    for i in range(G):
        for q in range(P):
            n = q * G + i
            p, gm = pre[n], grams[n]
            sx = _nt(cat([p["at"], p["rt"]], axis=0), s[q], PS)
            x = sx[:C] + xv[n]
            u = sel(_nn(tinv[n], cat([x, x], axis=0), PS))
            v = p["v"]
            ys[n] = sx[C:] + sel(_nn(cat([tri(gm[C2:, :C2], False), tri(gm[C2:, C2:], False)], axis=1),
                                     cat([u, u, v, v], axis=0), PS))
            s[q] = s[q] * p["dec"] + jnp.where(bd, _tn(cat([u, v], axis=0), cat([p["bh"], p["kh"]], axis=0), PS), 0.0)
    return (tuple(ys), tuple(s), tinv) if with_inv else (tuple(ys), tuple(s))


def _scan_fwd(xs, lw, kp, an, bb, T):
    C, G = _scan_shape(T)
    P = SCAN_PAIRS
    nc = T // (C * G)
    npair = 1024 // LANE

    def kern(r_ref, lw_ref, k_ref, v_ref, a_ref, b_ref, y_ref, st_ref, inv_ref, s_scr):
        n = pl.program_id(1)

        @pl.when(n == 0)
        def _():
            s_scr[...] = jnp.zeros(s_scr.shape, F32)

        st_ref[0] = s_scr[...]
        ins = (r_ref, lw_ref, k_ref, v_ref, a_ref, b_ref)
        ys, s1, inv = _scan_group(
            tuple(s_scr[q] for q in range(P)),
            *[ref[i * C:(i + 1) * C, q * LANE:(q + 1) * LANE] for q in range(P) for i in range(G) for ref in ins], with_inv=True)
        for q in range(P):
            for i in range(G):
                y_ref[i * C:(i + 1) * C, q * LANE:(q + 1) * LANE] = ys[q * G + i]
                inv_ref[0, 0, q * G + i] = inv[q * G + i]
            s_scr[q] = s1[q]

    def col(off):
        return pl.BlockSpec((C * G, P * LANE), lambda p, n: (n, off // P + p))

    return pl.pallas_call(
        kern, name="rwkv_scan_fwd",
        out_shape=[jax.ShapeDtypeStruct((T, 1024), F32), jax.ShapeDtypeStruct((nc, npair, LANE, LANE), F32),
                   jax.ShapeDtypeStruct((nc, npair // P, P * G, 2 * C, 2 * C), F32)],
        grid=(npair // P, nc),
        in_specs=[col(0), col(0), col(0), col(16), col(0), col(0)],
        out_specs=[col(0), pl.BlockSpec((1, P, LANE, LANE), lambda p, n: (n, p, 0, 0)),
                   pl.BlockSpec((1, 1, P * G, 2 * C, 2 * C), lambda p, n: (n, p, 0, 0, 0))],
        scratch_shapes=[pltpu.VMEM((P, LANE, LANE), F32)],
        compiler_params=_cp(("parallel", "arbitrary")),
    )(xs, lw, kp, xs, an, bb)


def _scan_bwd(xs, lw, kp, an, bb, states, invs, dy, T):
    C, G = _scan_shape(T)
    P = SCAN_PAIRS
    nc = T // (C * G)
    npair = 1024 // LANE

    def kern(r_ref, lw_ref, k_ref, v_ref, a_ref, b_ref, st_ref, inv_ref, dy_ref, dr_ref, dlw_ref, dk_ref, dv_ref, da_ref, db_ref,
             ds_scr):
        n = pl.program_id(1)

        @pl.when(n == 0)
        def _():
            ds_scr[...] = jnp.zeros(ds_scr.shape, F32)

        ins = (r_ref, lw_ref, k_ref, v_ref, a_ref, b_ref)
        units = [(q, i) for q in range(P) for i in range(G)]
        known = tuple(inv_ref[0, 0, q * G + i] for q, i in units)
        _, vjp = jax.vjp(functools.partial(_scan_group, known_inv=known), tuple(st_ref[0, q] for q in range(P)),
                         *[ref[i * C:(i + 1) * C, q * LANE:(q + 1) * LANE] for q, i in units for ref in ins])
        grads = vjp((tuple(dy_ref[i * C:(i + 1) * C, q * LANE:(q + 1) * LANE] for q, i in units),
                     tuple(ds_scr[q] for q in range(P))))
        for q in range(P):
            ds_scr[q] = grads[0][q]
        outs = (dr_ref, dlw_ref, dk_ref, dv_ref, da_ref, db_ref)
        for n_, (q, i) in enumerate(units):
            for t, ref in enumerate(outs):
                ref[i * C:(i + 1) * C, q * LANE:(q + 1) * LANE] = grads[1 + 6 * n_ + t]

    def col(off):
        return pl.BlockSpec((C * G, P * LANE), lambda p, n: (nc - 1 - n, off // P + p))

    return pl.pallas_call(
        kern, name="rwkv_scan_bwd",
        out_shape=[jax.ShapeDtypeStruct((T, 1024), F32)] * 6,
        grid=(npair // P, nc),
        in_specs=[col(0), col(0), col(0), col(16), col(0), col(0),
                  pl.BlockSpec((1, P, LANE, LANE), lambda p, n: (nc - 1 - n, p, 0, 0)),
                  pl.BlockSpec((1, 1, P * G, 2 * C, 2 * C), lambda p, n: (nc - 1 - n, p, 0, 0, 0)), col(0)],
        out_specs=[col(0)] * 6,
        scratch_shapes=[pltpu.VMEM((P, LANE, LANE), F32)],
        compiler_params=_cp(("parallel", "arbitrary")),
    )(xs, lw, kp, xs, an, bb, states, invs, dy)


def _gates_fwd(u, bias_pad, T, f_cb):
    nb = T // LANE

    def kern(f_ref, b_ref, c_ref):
        x = f_ref[...] + b_ref[...]
        lf = jnp.minimum(x, 0.0) - jnp.log(1.0 + jnp.exp(-jnp.abs(x)))
        lft = lf.T
        ut = (_iota2((LANE, LANE), 0) <= _iota2((LANE, LANE), 1)).astype(F32)
        carry = jnp.zeros((LANE, 1), F32)
        for blk in range(nb):
            seg = lft[:, blk * LANE:(blk + 1) * LANE]
            cs = _nn(seg, ut, HI) + carry
            c_ref[:, blk * LANE:(blk + 1) * LANE] = cs[:SUB, :]
            carry = carry + jnp.sum(seg, axis=1, keepdims=True)

    return pl.pallas_call(
        kern, name="fox_gates_fwd",
        out_shape=jax.ShapeDtypeStruct((SUB, T), F32),
        grid=(1,),
        in_specs=[pl.BlockSpec((T, LANE), lambda i: (0, f_cb)), pl.BlockSpec((1, LANE), lambda i: (0, 0))],
        out_specs=pl.BlockSpec((SUB, T), lambda i: (0, 0)),
        compiler_params=_cp(("arbitrary",)),
    )(u, bias_pad)


def _gates_bwd(dc, u, bias_pad, T, f_cb):
    nb = T // LANE

    def kern(dc_ref, f_ref, b_ref, dfl_ref, db_ref):
        dcv = jnp.concatenate([dc_ref[...], jnp.zeros((LANE - SUB, T), F32)], axis=0)
        lt = (_iota2((LANE, LANE), 0) >= _iota2((LANE, LANE), 1)).astype(F32)
        carry = jnp.zeros((LANE, 1), F32)
        pieces = [None] * nb
        for blk in range(nb - 1, -1, -1):
            seg = dcv[:, blk * LANE:(blk + 1) * LANE]
            pieces[blk] = _nn(seg, lt, HI) + carry
            carry = carry + jnp.sum(seg, axis=1, keepdims=True)
        dlf = (pieces[0] if nb == 1 else jnp.concatenate(pieces, axis=1)).T
        x = f_ref[...] + b_ref[...]
        dfl = dlf * _sigmoid(-x)
        dfl_ref[...] = dfl
        db_ref[...] = jnp.sum(dfl, axis=0, keepdims=True)

    return pl.pallas_call(
        kern, name="fox_gates_bwd",
        out_shape=[jax.ShapeDtypeStruct((T, LANE), F32), jax.ShapeDtypeStruct((1, LANE), F32)],
        grid=(1,),
        in_specs=[pl.BlockSpec((SUB, T), lambda i: (0, 0)), pl.BlockSpec((T, LANE), lambda i: (0, f_cb)),
                  pl.BlockSpec((1, LANE), lambda i: (0, 0))],
        out_specs=[pl.BlockSpec((T, LANE), lambda i: (0, 0)), pl.BlockSpec((1, LANE), lambda i: (0, 0))],
        compiler_params=_cp(("arbitrary",)),
    )(dc, u, bias_pad)


ATTN_HEADS = 2


def _attn_block(T):
    return 512 if T % 512 == 0 and T >= 1024 else 128


def _attn_fwd(u, c3, T):
    H, HP = 8, ATTN_HEADS
    bq = _attn_block(T)
    nq = T // bq
    scale = FOX_HD ** -0.5
    lanes = [slice(h * LANE, (h + 1) * LANE) for h in range(HP)]

    def kern(q_ref, k_ref, v_ref, z_ref, cq_ref, ck_ref, o_ref, oa_ref, lse_ref):
        i = pl.program_id(1)
        q = [(q_ref[:, ln] * scale).astype(BF16) for ln in lanes]
        c0 = [cq_ref[h][:, 0:1] for h in range(HP)]

        def step(j, carry, diagonal=False):
            off = pl.multiple_of(j * bq, bq)
            s = [_nt(q[h], k_ref[pl.ds(off, bq), lanes[h]].astype(BF16)) + (c0[h] - ck_ref[h, :, pl.ds(off, bq)])
                 for h in range(HP)]
            ps, out = [], []
            for h in range(HP):
                m, l, acc = carry[h]
                sh = s[h]
                if diagonal:
                    sh = jnp.where(_iota2((bq, bq), 1) <= _iota2((bq, bq), 0), sh, NEG)
                m_new = jnp.maximum(m, jnp.max(sh, axis=1, keepdims=True))
                p = jnp.exp(sh - m_new)
                alpha = jnp.exp(m - m_new)
                p_hi = p.astype(BF16)
                ps.append((p_hi, (p - p_hi.astype(F32)).astype(BF16)))
                out.append((m_new, alpha * l + jnp.sum(p, axis=1, keepdims=True), alpha * acc))
            res = []
            for h, (m, l, acc) in enumerate(out):
                vj = v_ref[pl.ds(off, bq), lanes[h]].astype(BF16)
                res.append((m, l, acc + _nn(ps[h][0], vj) + _nn(ps[h][1], vj)))
            return tuple(res)

        init = tuple((jnp.full((bq, 1), NEG, F32), jnp.zeros((bq, 1), F32), jnp.zeros((bq, FOX_HD), F32)) for _ in range(HP))
        res = step(i, lax.fori_loop(0, i, step, init), diagonal=True)
        for h, (m, l, acc) in enumerate(res):
            o = acc / l
            z = z_ref[:, lanes[h]]
            o_ref[:, lanes[h]] = o
            oa_ref[:, lanes[h]] = (o * z * _sigmoid(z)).astype(BF16)
            lse_ref[h] = m + jnp.log(l)

    W = HP * LANE
    return pl.pallas_call(
        kern, name="fox_attn_fwd",
        out_shape=[jax.ShapeDtypeStruct((T, 1024), F32), jax.ShapeDtypeStruct((T, 1024), BF16),
                   jax.ShapeDtypeStruct((H, T, 1), F32)],
        grid=(H // HP, nq),
        in_specs=[pl.BlockSpec((bq, W), lambda g, i: (i, g)),
                  pl.BlockSpec((T, W), lambda g, i: (0, 8 // HP + g)),
                  pl.BlockSpec((T, W), lambda g, i: (0, 16 // HP + g)),
                  pl.BlockSpec((bq, W), lambda g, i: (i, 24 // HP + g)),
                  pl.BlockSpec((HP, 1, bq), lambda g, i: (g, 0, i)),
                  pl.BlockSpec((HP, 1, T), lambda g, i: (g, 0, 0))],
        out_specs=[pl.BlockSpec((bq, W), lambda g, i: (i, g)),
                   pl.BlockSpec((bq, W), lambda g, i: (i, g)),
                   pl.BlockSpec((HP, bq, 1), lambda g, i: (g, i, 0))],
        compiler_params=_cp(("parallel", "arbitrary")),
    )(u, u, u, u, c3, c3)


def _attn_probs(s, lse_i, diagonal):
    if not diagonal:
        return jnp.exp(s - lse_i)
    keep = _iota2(s.shape, 1) <= _iota2(s.shape, 0)
    return jnp.where(keep, jnp.exp(jnp.where(keep, s, NEG) - lse_i), 0.0)


def _attn_pre_math(doa, z, o):
    sg = _sigmoid(z)
    do = (doa * z * sg).astype(BF16)
    dz = doa * o * (sg * (1.0 + z * (1.0 - sg)))
    head_of = (_iota2((o.shape[1], LANE), 0) // FOX_HD == _iota2((o.shape[1], LANE), 1)).astype(F32)
    return do, dz, _nn(do.astype(F32) * o, head_of, HI)


def _attn_bwd(u, c3, lse, do, delta, T):
    H, HP = 8, ATTN_HEADS
    bq = _attn_block(T)
    nq = T // bq
    scale = FOX_HD ** -0.5
    lanes = [slice(h * LANE, (h + 1) * LANE) for h in range(HP)]

    def kern(q_ref, k_ref, v_ref, c_ref, lse_ref, do_ref, dl_ref, dq_ref, dk_ref, dv_ref, dc_ref):
        j = pl.program_id(1)

        @pl.when(j == 0)
        def _():
            dq_ref[...] = jnp.zeros(dq_ref.shape, F32)

        kj = [k_ref[:, ln].astype(BF16) for ln in lanes]
        vj = [v_ref[:, ln].astype(BF16) for ln in lanes]
        joff = pl.multiple_of(j * bq, bq)
        ck = [c_ref[h, :, pl.ds(joff, bq)] for h in range(HP)]

        def step(i, carry, diagonal=False):
            off = pl.multiple_of(i * bq, bq)
            qs = [(q_ref[pl.ds(off, bq), ln] * scale).astype(BF16) for ln in lanes]
            dob = [do_ref[pl.ds(off, bq), ln] for ln in lanes]
            s = [_nt(qs[h], kj[h]) + (c_ref[h, :, pl.ds(off, bq)][:, 0:1] - ck[h]) for h in range(HP)]
            dp = [_nt(dob[h], vj[h]) for h in range(HP)]
            pb, dsb, dcs = [], [], []
            for h in range(HP):
                p = _attn_probs(s[h], lse_ref[h, pl.ds(off, bq), :], diagonal)
                ds = p * (dp[h] - dl_ref[h, pl.ds(off, bq), :])
                pb.append(p.astype(BF16))
                dsb.append(ds.astype(BF16))
                dcs.append(jnp.sum(ds, axis=0, keepdims=True))
            out = []
            for h, (dk, dv, dc) in enumerate(carry):
                dq_ref[pl.ds(off, bq), lanes[h]] += _nn(dsb[h], kj[h]) * scale
                out.append((dk + _tn(dsb[h], qs[h]), dv + _tn(pb[h], dob[h]), dc - dcs[h]))
            return tuple(out)

        init = tuple((jnp.zeros((bq, FOX_HD), F32), jnp.zeros((bq, FOX_HD), F32), jnp.zeros((1, bq), F32)) for _ in range(HP))
        res = lax.fori_loop(j + 1, nq, step, step(j, init, diagonal=True))
        for h, (dk, dv, dc) in enumerate(res):
            dk_ref[:, lanes[h]] = dk
            dv_ref[:, lanes[h]] = dv
            dc_ref[h] = dc

    W = HP * LANE
    full = lambda cb: pl.BlockSpec((T, W), lambda g, j: (0, cb // HP + g))
    blk = lambda cb: pl.BlockSpec((bq, W), lambda g, j: (j, cb // HP + g))
    col1 = pl.BlockSpec((HP, T, 1), lambda g, j: (g, 0, 0))
    return pl.pallas_call(
        kern, name="fox_attn_bwd",
        out_shape=[jax.ShapeDtypeStruct((T, 1024), F32)] * 3 + [jax.ShapeDtypeStruct((H, 1, T), F32)],
        grid=(H // HP, nq),
        in_specs=[full(0), blk(8), blk(16), pl.BlockSpec((HP, 1, T), lambda g, j: (g, 0, 0)), col1, full(0), col1],
        out_specs=[full(0), blk(0), blk(0), pl.BlockSpec((HP, 1, bq), lambda g, j: (g, 0, j))],
        compiler_params=_cp(("parallel", "arbitrary")),
    )(u, u, u, c3, lse, do, delta)


def _place():
    return lax.axis_index("x"), lax.axis_index("y"), lax.axis_index("c")


def _slot(p):
    return 4 * p[0] + 2 * p[1] + p[2]


def _other_chips(x, y):
    return [(1 - x, y), (x, 1 - y), (1 - x, 1 - y)]


def _allgather_steps(in_refs, out_refs, scratch):
    (src,), (dst,) = in_refs, out_refs
    send_sems, recv_sems, local_sem = scratch
    x, y, c = _place()
    me, sibling = (x, y, c), (x, y, 1 - c)
    chips = _other_chips(x, y)

    def copy(k, block, to, from_input=False):
        d = dst.at[_slot(block)]
        return pltpu.make_async_remote_copy(
            src_ref=src if from_input else d, dst_ref=d, send_sem=send_sems.at[k], recv_sem=recv_sems.at[k],
            device_id=to, device_id_type=MESH)

    def first_copies():
        return [copy(0, me, sibling, True)] + [copy(1 + j, me, (*chip, c), True) for j, chip in enumerate(chips)]

    def start():
        pltpu.make_async_copy(src, dst.at[_slot(me)], local_sem).start()
        for cp in first_copies():
            cp.start()

    def finish():
        passed = []
        for j, chip in enumerate(chips):
            copy(1 + j, (*chip, c), me).wait_recv()
            passed.append(copy(4 + j, (*chip, c), sibling))
            passed[-1].start()
        copy(0, sibling, me).wait_recv()
        for j, chip in enumerate(chips):
            copy(4 + j, (*chip, 1 - c), me).wait_recv()
        for cp in first_copies() + passed:
            cp.wait_send()
        pltpu.make_async_copy(src, dst.at[_slot(me)], local_sem).wait()

    return start, finish


def _allgather_relay_steps(in_refs, out_refs, scratch):
    (src,), (dst,) = in_refs, out_refs
    send_sems, recv_sems, local_sem = scratch
    x, y, c = _place()
    me, sibling = (x, y, c), (x, y, 1 - c)
    x_nbr, y_nbr, diag = (1 - x, y, c), (x, 1 - y, c), (1 - x, 1 - y, c)
    flip = lambda a, bit: a + bit - 2 * a * bit
    relay_from = (flip(x, 1 - c), flip(y, c), c)
    relay_to = (flip(x, c), flip(y, 1 - c), c)

    def copy(k, block, to, from_input=False):
        d = dst.at[_slot(block)]
        return pltpu.make_async_remote_copy(
            src_ref=src if from_input else d, dst_ref=d, send_sem=send_sems.at[k], recv_sem=recv_sems.at[k],
            device_id=to, device_id_type=MESH)

    def first_copies():
        return [copy(0, me, sibling, True), copy(1, me, x_nbr, True), copy(2, me, y_nbr, True)]

    def other(block):
        return block[:2] + (1 - c,)

    def start():
        pltpu.make_async_copy(src, dst.at[_slot(me)], local_sem).start()
        for cp in first_copies():
            cp.start()

    def finish():
        copy(1, x_nbr, me).wait_recv()
        copy(2, y_nbr, me).wait_recv()
        later = [copy(3, relay_from, relay_to), copy(4, x_nbr, sibling), copy(5, y_nbr, sibling)]
        for cp in later:
            cp.start()
        copy(3, diag, me).wait_recv()
        later.append(copy(6, diag, sibling))
        later[-1].start()
        copy(0, sibling, me).wait_recv()
        for k, block in ((4, x_nbr), (5, y_nbr), (6, diag)):
            copy(k, other(block), me).wait_recv()
        for cp in first_copies() + later:
            cp.wait_send()
        pltpu.make_async_copy(src, dst.at[_slot(me)], local_sem).wait()

    return start, finish


def _allgather_side(a, relay=False):
    return ((a,), (jax.ShapeDtypeStruct((N_DEV,) + a.shape, a.dtype),),
            (pltpu.SemaphoreType.DMA((7,)), pltpu.SemaphoreType.DMA((7,)), pltpu.SemaphoreType.DMA),
            _allgather_relay_steps if relay else _allgather_steps)


def _allgather(a, name, relay=False):
    ins, outs, scratch, make = _allgather_side(a, relay)

    def body(a_ref, o_ref, *scr):
        start, finish = make((a_ref,), (o_ref,), scr)
        start()
        finish()

    any_spec = pl.BlockSpec(memory_space=pl.ANY)
    return pl.pallas_call(body, name=name, out_shape=outs[0], in_specs=[any_spec], out_specs=any_spec,
                          scratch_shapes=list(scratch))(a)


def _exchange_pair_steps(in_refs, out_refs, scratch):
    (src,), (dst,) = in_refs, out_refs
    send_sems, recv_sems = scratch
    x, y, c = _place()
    sibling = (x, y, 1 - c)
    slots = [_slot(sibling)] + [_slot((*chip, 1 - c)) for chip in _other_chips(x, y)]

    def copies():
        return [pltpu.make_async_remote_copy(
            src_ref=src.at[ps], dst_ref=dst.at[k], send_sem=send_sems.at[k], recv_sem=recv_sems.at[k],
            device_id=sibling, device_id_type=MESH) for k, ps in enumerate(slots)]

    def start():
        for cp in copies():
            cp.start()

    def finish():
        for cp in copies():
            cp.wait()

    return start, finish


def _exchange_pair_side(g):
    return ((g,), (jax.ShapeDtypeStruct((4,) + g.shape[1:], g.dtype),),
            (pltpu.SemaphoreType.DMA((4,)), pltpu.SemaphoreType.DMA((4,))), _exchange_pair_steps)


def _exchange_pair(g):
    ins, outs, scratch, make = _exchange_pair_side(g)

    def body(g_ref, r_ref, *scr):
        start, finish = make((g_ref,), (r_ref,), scr)
        start()
        finish()

    any_spec = pl.BlockSpec(memory_space=pl.ANY)
    return pl.pallas_call(body, name="exchange_pair", out_shape=outs[0], in_specs=[any_spec], out_specs=any_spec,
                          scratch_shapes=list(scratch))(g)


def _tiling(R, Cc, tile, by_cols):
    if by_cols:
        assert Cc % tile == 0
        return Cc // tile, (R, tile), lambda lead, i: (lead, 0, i)
    assert R % tile == 0
    return R // tile, (tile, Cc), lambda lead, i: (lead, i, 0)


def _pair_add(name, g, r1, slots, tile, by_cols=False, side=None):
    _, R, Cc = g.shape
    steps, blk, at = _tiling(R, Cc, tile, by_cols)
    side_ins, side_outs, side_scr, side_make = side if side is not None else ((), (), (), None)
    n_si, n_so = len(side_ins), len(side_outs)

    def kern(s_ref, a_ref, b_ref, *rest):
        o_ref = rest[n_si]
        if side_make is not None:
            start, finish = side_make(rest[:n_si], rest[n_si + 1:n_si + 1 + n_so], rest[n_si + 1 + n_so:])
            j, i = pl.program_id(0), pl.program_id(1)
            pl.when(jnp.logical_and(j == 0, i == 0))(start)
        o_ref[...] = (a_ref[...].astype(F32) + b_ref[...].astype(F32)).astype(o_ref.dtype)
        if side_make is not None:
            pl.when(jnp.logical_and(j == 2, i == steps - 1))(finish)

    any_spec = pl.BlockSpec(memory_space=pl.ANY)
    res = pl.pallas_call(
        kern, name=name,
        out_shape=[jax.ShapeDtypeStruct((3, R, Cc), BF16)] + list(side_outs),
        grid_spec=pltpu.PrefetchScalarGridSpec(
            num_scalar_prefetch=1, grid=(3, steps),
            in_specs=[pl.BlockSpec((1,) + blk, lambda j, i, s: at(s[j], i)),
                      pl.BlockSpec((1,) + blk, lambda j, i, s: at(1 + j, i))] + [any_spec] * n_si,
            out_specs=[pl.BlockSpec((1,) + blk, lambda j, i, s: at(j, i))] + [any_spec] * n_so,
            scratch_shapes=list(side_scr)),
        compiler_params=_cp(("arbitrary", "arbitrary")),
    )(slots, g, r1, *side_ins)
    return res if side is not None else res[0]


def _axis_neighbours():
    x, y, c = _place()
    flip = lambda a, bit: a + bit - 2 * a * bit
    return (flip(x, c), flip(y, 1 - c), c), (flip(x, 1 - c), flip(y, c), c), c


def _exchange_first_steps(in_refs, out_refs, scratch):
    pairs = list(zip(in_refs, out_refs))
    send_sems, recv_sems = scratch
    first, _, c = _axis_neighbours()

    def copies():
        return [pltpu.make_async_remote_copy(
            src_ref=src.at[j], dst_ref=dst.at[k], send_sem=send_sems.at[t, k], recv_sem=recv_sems.at[t, k],
            device_id=first, device_id_type=MESH)
            for t, (src, dst) in enumerate(pairs) for k, j in enumerate((1 - c, 2))]

    def start():
        for cp in copies():
            cp.start()

    def finish():
        for cp in copies():
            cp.wait()

    return start, finish


def _exchange_first_side(*ss):
    n = len(ss)
    return (ss, tuple(jax.ShapeDtypeStruct((2,) + s.shape[1:], s.dtype) for s in ss),
            (pltpu.SemaphoreType.DMA((n, 2)), pltpu.SemaphoreType.DMA((n, 2))), _exchange_first_steps)


def _axis_add(name, s, t1, core, tile, by_cols=False):
    _, R, Cc = s.shape
    steps, blk, at = _tiling(R, Cc, tile, by_cols)

    def kern(c_ref, a_ref, b_ref, o_ref):
        o_ref[...] = (a_ref[...].astype(F32) + b_ref[...].astype(F32)).astype(o_ref.dtype)

    return pl.pallas_call(
        kern, name=name,
        out_shape=jax.ShapeDtypeStruct((1, R, Cc), BF16),
        grid_spec=pltpu.PrefetchScalarGridSpec(
            num_scalar_prefetch=1, grid=(steps,),
            in_specs=[pl.BlockSpec((1,) + blk, lambda i, cr: at(cr[0], i)),
                      pl.BlockSpec((1,) + blk, lambda i, cr: at(1, i))],
            out_specs=pl.BlockSpec((1,) + blk, lambda i, cr: at(0, i))),
        compiler_params=_cp(("arbitrary",)),
    )(core, s, t1)


def _exchange_second_steps(in_refs, out_refs, scratch):
    send_sems, recv_sems = scratch
    _, second, _ = _axis_neighbours()

    def copies():
        return [pltpu.make_async_remote_copy(src_ref=src, dst_ref=dst, send_sem=send_sems.at[t], recv_sem=recv_sems.at[t],
                                             device_id=second, device_id_type=MESH)
                for t, (src, dst) in enumerate(zip(in_refs, out_refs))]

    def start():
        for cp in copies():
            cp.start()

    def finish():
        for cp in copies():
            cp.wait()

    return start, finish


def _exchange_second_side(*ps):
    n = len(ps)
    return (ps, tuple(jax.ShapeDtypeStruct(p.shape, p.dtype) for p in ps),
            (pltpu.SemaphoreType.DMA((n,)), pltpu.SemaphoreType.DMA((n,))), _exchange_second_steps)


def _adamw(name, w, m, v, parts, tile, by_cols=False):
    R, Cc = w.shape
    steps, blk_shape, at = _tiling(R, Cc, tile, by_cols)
    n_parts = len(parts)

    def kern(*refs):
        w_ref, m_ref, v_ref = refs[:3]
        g = None
        for r_ref, (_, n) in zip(refs[3:3 + n_parts], parts):
            for s in range(n):
                term = r_ref[s].astype(F32)
                g = term if g is None else g + term
        g_out, d_out, m_out, v_out = refs[3 + n_parts:]
        mn = ADAM_B1 * m_ref[...] + (1.0 - ADAM_B1) * g
        vn = ADAM_B2 * v_ref[...] + (1.0 - ADAM_B2) * (g * g)
        m_hat = mn / (1.0 - ADAM_B1 ** ADAM_STEP)
        v_hat = vn / (1.0 - ADAM_B2 ** ADAM_STEP)
        g_out[...] = g
        d_out[...] = -ADAM_LR * (m_hat / (jnp.sqrt(v_hat) + ADAM_EPS) + ADAM_WD * w_ref[...])
        m_out[...] = mn
        v_out[...] = vn

    blk = pl.BlockSpec(blk_shape, lambda i: at(0, i)[1:])
    return pl.pallas_call(
        kern, name=name,
        out_shape=[jax.ShapeDtypeStruct((R, Cc), F32)] * 4,
        grid=(steps,),
        in_specs=[blk] * 3 + [pl.BlockSpec((n,) + blk_shape, lambda i: at(0, i)) for (_, n) in parts],
        out_specs=[blk] * 4,
        compiler_params=_cp(("arbitrary",)),
    )(w, m, v, *[a for (a, _) in parts])


def _assemble_columns(blocks, pieces, zeros, width):
    _, R, Cc = blocks.shape
    tr = min(256, R)

    def kern(b_ref, o_ref):
        for col, n in zeros:
            o_ref[:, col:col + n] = jnp.zeros((tr, n), o_ref.dtype)
        for col, d, lo, n in pieces:
            o_ref[:, col:col + n] = b_ref[d, :, lo:lo + n]

    return pl.pallas_call(
        kern, name="assemble_w_in",
        out_shape=jax.ShapeDtypeStruct((R, width), blocks.dtype),
        grid=(R // tr,),
        in_specs=[pl.BlockSpec((N_DEV, tr, Cc), lambda i: (0, i, 0))],
        out_specs=pl.BlockSpec((tr, width), lambda i: (i, 0)),
        compiler_params=_cp(("parallel",)),
    )(blocks)


def _split_rows(x, pieces, rows, side):
    _, Cc = x.shape
    tc = min(256, Cc)
    side_ins, side_outs, side_scr, side_make = side
    n_si, n_so = len(side_ins), len(side_outs)

    def kern(x_ref, *rest):
        o_ref = rest[n_si]
        start, finish = side_make(rest[:n_si], rest[n_si + 1:n_si + 1 + n_so], rest[n_si + 1 + n_so:])
        pl.when(pl.program_id(0) == 0)(start)
        for d, lo, row, n in pieces:
            o_ref[d, lo:lo + n, :] = x_ref[row:row + n, :]
        pl.when(pl.program_id(0) == Cc // tc - 1)(finish)

    any_spec = pl.BlockSpec(memory_space=pl.ANY)
    return pl.pallas_call(
        kern, name="split_w_in_grad",
        out_shape=[jax.ShapeDtypeStruct((N_DEV, rows, Cc), x.dtype)] + list(side_outs),
        grid=(Cc // tc,),
        in_specs=[pl.BlockSpec((x.shape[0], tc), lambda i: (0, i))] + [any_spec] * n_si,
        out_specs=[pl.BlockSpec((N_DEV, rows, tc), lambda i: (0, 0, i))] + [any_spec] * n_so,
        scratch_shapes=list(side_scr),
        compiler_params=_cp(("arbitrary",)),
    )(x, *side_ins)


def _pad_cols(a, w):
    return jnp.pad(a, ((0, 0), (0, w - a.shape[1])))


def _pad_rows(a, r):
    return jnp.pad(a, ((0, r - a.shape[0]), (0, 0)))


def _pack_b(pf, pr, wo, w2, a2, rows):
    body = jnp.concatenate([pf, pr, wo.reshape(2048, 256), jnp.concatenate([w2, a2], axis=1)], axis=0)
    return _pad_rows(body, rows)


def kernel(x, norm_gain, w_in, fox_forget_bias, rwkv_shift_mix, rwkv_w0, rwkv_w2, rwkv_a0, rwkv_a2, rwkv_k_k, rwkv_k_a, rwkv_r_k, rwkv_ln_w, rwkv_ln_b, w_proj_fox, w_proj_rwkv, w_out, final_norm_gain, loss_target, m_norm_gain, m_w_in, m_fox_forget_bias, m_rwkv_shift_mix, m_rwkv_w0, m_rwkv_w2, m_rwkv_a0, m_rwkv_a2, m_rwkv_k_k, m_rwkv_k_a, m_rwkv_r_k, m_rwkv_ln_w, m_rwkv_ln_b, m_w_proj_fox, m_w_proj_rwkv, m_w_out, m_final_norm_gain, v_norm_gain, v_w_in, v_fox_forget_bias, v_rwkv_shift_mix, v_rwkv_w0, v_rwkv_w2, v_rwkv_a0, v_rwkv_a2, v_rwkv_k_k, v_rwkv_k_a, v_rwkv_r_k, v_rwkv_ln_w, v_rwkv_ln_b, v_w_proj_fox, v_w_proj_rwkv, v_w_out, v_final_norm_gain):
    T, D = x.shape[1], x.shape[2]
    assert D == 2048 and T % LANE == 0
    NI = w_in.shape[2]
    IN = N_DEV * NI
    RB = 4224
    x2 = x[0]
    lt2 = loss_target[0]
    me = _slot(_place())

    tb = min(256, T)
    tbh = min(128, T)
    h, wa = _rowcall("rms_fwd", T, tb, [_rows(x2, tb, D), _whole(norm_gain)],
                     lambda i, xv, g: ([_rms_math(xv, g)], []), [(D, BF16)],
                     side=_allgather_side(w_in[0].astype(BF16), relay=True))
    packed_own = _pack_b(w_proj_fox[0], w_proj_rwkv[0], w_out[0], rwkv_w2[0], rwkv_a2[0], RB).astype(BF16)
    sections = [(0, 4096, 0), (4104, 4096, 4096), (8392, 4096, 8192), (4096, 8, 12288), (8200, 96, 12544), (8296, 96, 12672)]
    NP = 12800
    pieces, zeros, at_col = [], [], 0
    for lo, width, pad_lo in sections:
        if pad_lo > at_col:
            zeros.append((at_col, pad_lo - at_col))
        col = lo
        while col < lo + width:
            d = col // NI
            stop = min(lo + width, (d + 1) * NI)
            pieces.append((pad_lo + col - lo, d, col - d * NI, stop - col))
            col = stop
        at_col = pad_lo + width
    zeros.append((at_col, NP - at_col))
    w_pad = _assemble_columns(wa, pieces, zeros, NP)
    F_CB, LORA_CB = 96, 49

    mu = rwkv_shift_mix
    mu_main = mu[:, 0:4096]
    mu_lora = jnp.concatenate([_pad_cols(mu[:, 4096:4192], LANE), _pad_cols(mu[:, 4192:4288], LANE)], axis=1)
    bias_pad = _pad_cols(fox_forget_bias, LANE)
    rk_flat = rwkv_r_k.reshape(1, 1024)
    gf = final_norm_gain.reshape(1, D)

    u, wb = _mm(h, w_pad, tm=1024, tn=1280, name="mm_in", side=_allgather_side(packed_own))
    wpf = wb[:, 0:1024, :].transpose(1, 0, 2).reshape(1024, D)
    wpr = wb[:, 1024:2048, :].transpose(1, 0, 2).reshape(1024, D)
    wo = wb[:, 2048:4096, :].reshape(N_DEV * 256, D)
    w2p = _pad_rows(wb[:, 4096:4192, 0:128].transpose(1, 0, 2).reshape(96, 1024).astype(F32), LANE)
    a2p = _pad_rows(wb[:, 4096:4192, 128:256].transpose(1, 0, 2).reshape(96, 1024).astype(F32), LANE)

    c8 = _gates_fwd(u, bias_pad, T, F_CB)
    c3 = c8.reshape(8, 1, T)
    o_raw, o_a, lse = _attn_fwd(u, c3, T)

    def shift_body(i, um, hm, ul, hl, mm_, ml):
        outs = []
        for uv, hv, mv in ((um, hm, mm_), (ul, hl, ml)):
            hv = jnp.where(i == 0, 0.0, hv)
            prev = pltpu.roll(jnp.concatenate([hv, uv], axis=0), 1, 0)[SUB:]
            outs.append(uv + (prev - uv) * mv)
        return outs, []

    def halo_prev(arr, w, cb):
        return (arr, (SUB, w), lambda i: (jnp.maximum(i * (tbh // SUB) - 1, 0), cb))

    xs, xl = _rowcall("rwkv_shift_fwd", T, tbh,
                      [_rows(u, tbh, 4096, 1), halo_prev(u, 4096, 1), _rows(u, tbh, 256, LORA_CB), halo_prev(u, 256, LORA_CB),
                       _whole(mu_main), _whole(mu_lora)],
                      shift_body, [(4096, F32), (256, F32)])

    prep_par = [_whole(rwkv_w0), _whole(rwkv_a0), _whole(rwkv_k_k), _whole(rwkv_k_a), _whole(w2p), _whole(a2p)]
    prep_rows = [_rows(xs, tbh, 1024, 1), _rows(xl, tbh, LANE, 0), _rows(xl, tbh, LANE, 1)]
    lw, kp, an, bb = _rowcall("rwkv_prep_fwd", T, tbh, prep_rows + prep_par,
                              lambda i, *a: (list(_prep_math(*a)), []), [(1024, F32)] * 4)
    y, states, invs = _scan_fwd(xs, lw, kp, an, bb, T)
    post_rows = [_rows(y, tbh, 1024), _rows(xs, tbh, 1024, 0), _rows(kp, tbh, 1024), _rows(xs, tbh, 1024, 2), _rows(xs, tbh, 1024, 3)]
    post_par = [_whole(rwkv_ln_w), _whole(rwkv_ln_b), _whole(rk_flat)]
    (o_b,) = _rowcall("rwkv_post_fwd", T, tbh, post_rows + post_par,
                      lambda i, *a: ([_post_math(*a)], []), [(1024, BF16)])

    pa = _mm(o_a, wpf, name="mm_proj_fox")
    pb = _mm(o_b, wpr, name="mm_proj_rwkv")
    merge_rows = [_rows(u, tb, D, 4), _rows(u, tb, D, 5), _rows(pa, tb, D), _rows(pb, tb, D)]
    (mg,) = _rowcall("merge_fwd", T, tb, merge_rows, lambda i, *a: ([_merge_math(*a)], []), [(D, BF16)])
    mo = _mm(mg, wo, name="mm_out")

    def head_body(i, xv, mov, ltv, g):
        out = xv + mov
        r = lax.rsqrt(jnp.mean(out * out, axis=-1, keepdims=True) + RMS_EPS)
        yn = out * r
        err = yn * g - ltv
        loss = 0.5 * jnp.sum(jnp.sum(err * err, axis=-1, keepdims=True), axis=0, keepdims=True) / D
        dyv = err / D
        dyn = dyv * g
        dout = r * (dyn - yn * jnp.mean(dyn * yn, axis=-1, keepdims=True))
        return [dout], [loss, jnp.sum(dyv * yn, axis=0, keepdims=True)]

    dout, loss_p, dgf_p = _rowcall("loss_head", T, tb, [_rows(x2, tb, D), _rows(mo, tb, D), _rows(lt2, tb, D), _whole(gf)],
                                   head_body, [(D, F32)], [(1, 1), (1, D)])

    dm = _mm(dout, wo, tb=True, name="mm_out_dx")
    dwo = _mm(mg, dout, ta=True, out_dtype=BF16, name="mm_out_dw")

    def merge_bwd_body(i, ga, gb, pav, pbv, dmv):
        _, vjp = jax.vjp(_merge_math, ga, gb, pav, pbv)
        dga, dgb, dpa, dpb = vjp(dmv)
        return [dga, dgb, dpa, dpb], []

    dga, dgb, dpa, dpb = _rowcall("merge_bwd", T, tb, merge_rows + [_rows(dm, tb, D)], merge_bwd_body,
                                  [(D, BF16), (D, BF16), (D, BF16), (D, BF16)])
    doa = _mm(dpa, wpf, tb=True, name="mm_proj_fox_dx")
    dwpf = _mm(o_a, dpa, ta=True, out_dtype=BF16, name="mm_proj_fox_dw")
    dob = _mm(dpb, wpr, tb=True, name="mm_proj_rwkv_dx")
    dwpr = _mm(o_b, dpb, ta=True, out_dtype=BF16, name="mm_proj_rwkv_dw")

    do_b, dza, delta128 = _rowcall("fox_attn_pre", T, tb, [_rows(doa, tb, 1024), _rows(u, tb, 1024, 3), _rows(o_raw, tb, 1024)],
                                   lambda i, *a: (list(_attn_pre_math(*a)), []), [(1024, BF16), (1024, F32), (LANE, F32)])
    delta = delta128[:, 0:8].T.reshape(8, T, 1)
    dq, dk, dv, dc3 = _attn_bwd(u, c3, lse, do_b, delta, T)
    dfl, dbias_p = _gates_bwd(dc3.reshape(8, T), u, bias_pad, T, F_CB)

    def post_bwd_body(i, yv, rv, kpv, vv, zv, lnw, lnb, rkv, dobv):
        _, vjp = jax.vjp(_post_math, yv, rv, kpv, vv, zv, lnw, lnb, rkv)
        dy_, dr_, dkp_, dv_, dz_, dlnw, dlnb, drk = vjp(dobv)
        return [dy_, dr_, dkp_, dv_, dz_], [dlnw, dlnb, drk]

    dy_s, dr_p, dkp_p, dv_p, dzb, dlnw_p, dlnb_p, drk_p = _rowcall(
        "rwkv_post_bwd", T, tbh, post_rows + post_par + [_rows(dob, tbh, 1024)], post_bwd_body,
        [(1024, F32)] * 5, [(1, 1024)] * 3)
    dr_s, dlw, dkp_s, dv_s, dan, dbb = _scan_bwd(xs, lw, kp, an, bb, states, invs, dy_s, T)

    def prep_bwd_body(i, xk, xwd, xad, w0, a0, kkw, kaw, w2v, a2v, dlw_, dkp1, dkp2, dan_, dbb_, dr1, dr2, dv1, dv2, dz_):
        _, vjp = jax.vjp(_prep_math, xk, xwd, xad, w0, a0, kkw, kaw, w2v, a2v)
        dxk, dxwd, dxad, dw0, da0, dkk, dka, dw2, da2 = vjp((dlw_, dkp1 + dkp2, dan_, dbb_))
        return [[dr1 + dr2, dxk, dv1 + dv2, dz_], [dxwd, dxad]], [dw0, da0, dkk, dka, dw2, da2]

    cots = [dlw, dkp_s, dkp_p, dan, dbb, dr_s, dr_p, dv_s, dv_p, dzb]
    dxs, dxl, dw0_p, da0_p, dkk_p, dka_p, dw2_p, da2_p = _rowcall(
        "rwkv_prep_bwd", T, tbh, prep_rows + prep_par + [_rows(c_, tbh, 1024) for c_ in cots], prep_bwd_body,
        [(4096, F32), (256, F32)], [(1, 1024)] * 4 + [(LANE, 1024)] * 2)

    def shift_bwd_body(i, dm_, hm, dl_, hl, um, pm, ul, pl_, mm_, ml):
        last = i == T // tbh - 1
        outs, accs = [], []
        for dv_, hv, uv, pv, mv in ((dm_, hm, um, pm, mm_), (dl_, hl, ul, pl_, ml)):
            hv = jnp.where(last, 0.0, hv)
            nxt = pltpu.roll(jnp.concatenate([dv_, hv], axis=0), tbh + SUB - 1, 0)[:tbh]
            pv = jnp.where(i == 0, 0.0, pv)
            prev = pltpu.roll(jnp.concatenate([pv, uv], axis=0), 1, 0)[SUB:]
            outs.append(dv_ * (1.0 - mv) + nxt * mv)
            accs.append(jnp.sum(dv_ * (prev - uv), axis=0, keepdims=True))
        return outs, accs

    def halo_next(arr, w, cb):
        last_blk = T // SUB - 1
        return (arr, (SUB, w), lambda i: (jnp.minimum((i + 1) * (tbh // SUB), last_blk), cb))

    du_b, du_l, dmu_main_p, dmu_lora_p = _rowcall(
        "rwkv_shift_bwd", T, tbh,
        [_rows(dxs, tbh, 4096), halo_next(dxs, 4096, 0), _rows(dxl, tbh, 256), halo_next(dxl, 256, 0),
         _rows(u, tbh, 4096, 1), halo_prev(u, 4096, 1), _rows(u, tbh, 256, LORA_CB), halo_prev(u, 256, LORA_CB),
         _whole(mu_main), _whole(mu_lora)],
        shift_bwd_body, [(4096, BF16), (256, BF16)], [(1, 4096), (1, 256)])

    lora_g = jnp.concatenate([dw2_p[:96].reshape(96, N_DEV, 128).transpose(1, 0, 2),
                              da2_p[:96].reshape(96, N_DEV, 128).transpose(1, 0, 2)], axis=2).astype(BF16)
    gb = jnp.concatenate([dwpf.reshape(1024, N_DEV, 256).transpose(1, 0, 2),
                          dwpr.reshape(1024, N_DEV, 256).transpose(1, 0, 2),
                          dwo.reshape(N_DEV, 2048, 256), lora_g, jnp.zeros((N_DEV, RB - 4192, 256), BF16)], axis=1)
    xx, yy, cc = _place()
    chip_slots = jnp.stack([_slot((*chip, cc)) for chip in _other_chips(xx, yy)]).astype(jnp.int32)
    core = jnp.stack([cc]).astype(jnp.int32)
    du, r1b = _rowcall("assemble_du", T, tb,
                       [_rows(a_, tb, a_.shape[1]) for a_ in (dq, dk, dv, dza, du_b, dga, dgb, dfl, du_l)],
                       lambda i, *a: ([list(a[:8]) + [jnp.zeros((tb, LANE), BF16), a[8]]], []), [(NP, BF16)],
                       side=_exchange_pair_side(gb))
    sb = _pair_add("pair_add_packed", gb, r1b, chip_slots, RB // 3)
    dw_pad_t, t1b = _mm(du, h, ta=True, out_dtype=BF16, tm=1280, tn=1024, name="mm_in_dw",
                        side=_exchange_first_side(sb))
    pb = _axis_add("axis_add_packed", sb, t1b, core, RB // 3)
    ga, t2b = _split_rows(dw_pad_t, [(d, lo, pad_col, n) for pad_col, d, lo, n in pieces], NI,
                          _exchange_second_side(pb))
    r1a = _exchange_pair(ga)
    sa = _pair_add("pair_add_w_in", ga, r1a, chip_slots, 512, by_cols=True)
    ga_own = lax.dynamic_index_in_dim(ga, me, 0, keepdims=True)
    gb_own = lax.dynamic_index_in_dim(gb, me, 0, keepdims=True)
    dh, t1a = _mm(du, w_pad, tb=True, tm=1024, tn=1024, tk=NP // 10, name="mm_in_dx", side=_exchange_first_side(sa))
    pa = _axis_add("axis_add_w_in", sa, t1a, core, 512, by_cols=True)

    def rms_bwd_body(i, xv, g, dhv, doutv):
        _, vjp = jax.vjp(_rms_math, xv, g)
        dx_, dg_ = vjp(dhv)
        return [dx_ + doutv], [dg_]

    grad_x2, dng_p, t2a = _rowcall(
        "rms_bwd", T, tb, [_rows(x2, tb, D), _whole(norm_gain), _rows(dh, tb, D), _rows(dout, tb, D)],
        rms_bwd_body, [(D, F32)], [(1, D)], side=_exchange_second_side(pa))

    w_in_outs = _adamw("adamw_w_in", w_in[0].T, m_w_in[0].T, v_w_in[0].T, [(ga_own, 1), (r1a, 1), (t1a, 1), (t2a, 1)], 256, by_cols=True)
    g_in, d_in, m_in, v_in = [o.T for o in w_in_outs]

    dmu = jnp.concatenate([dmu_main_p, dmu_lora_p[:, 0:96], dmu_lora_p[:, 128:224]], axis=1)
    small_parts = [dng_p, dbias_p[:, 0:8], dmu, dw0_p, da0_p, dkk_p, dka_p, drk_p, dlnw_p, dlnb_p, dgf_p, loss_p]
    SR = 128
    small = _pad_cols(jnp.concatenate(small_parts, axis=1), SR * LANE).reshape(SR, LANE)
    rs = _allgather(small, "allgather_small")
    pk = lambda pf, pr, wo_, w2_, a2_: _pack_b(pf[0], pr[0], wo_[0], w2_[0], a2_[0], RB)
    outs_b = _adamw("adamw_packed", pk(w_proj_fox, w_proj_rwkv, w_out, rwkv_w2, rwkv_a2),
                    pk(m_w_proj_fox, m_w_proj_rwkv, m_w_out, m_rwkv_w2, m_rwkv_a2),
                    pk(v_w_proj_fox, v_w_proj_rwkv, v_w_out, v_rwkv_w2, v_rwkv_a2), [(gb_own, 1), (r1b, 1), (t1b, 1), (t2b, 1)], RB // 3)

    def pack_small(ng, fb, sm, w0, a0, kk_, ka_, rk_, lnw, lnb, fg):
        parts = [ng, fb, sm, w0, a0, kk_, ka_, rk_.reshape(1, 1024), lnw, lnb, fg.reshape(1, D), jnp.zeros((1, 1), F32)]
        return _pad_cols(jnp.concatenate(parts, axis=1), SR * LANE).reshape(SR, LANE)

    outs_s = _adamw("adamw_small",
                    pack_small(norm_gain, fox_forget_bias, rwkv_shift_mix, rwkv_w0, rwkv_a0, rwkv_k_k, rwkv_k_a, rwkv_r_k,
                               rwkv_ln_w, rwkv_ln_b, final_norm_gain),
                    pack_small(m_norm_gain, m_fox_forget_bias, m_rwkv_shift_mix, m_rwkv_w0, m_rwkv_a0, m_rwkv_k_k, m_rwkv_k_a,
                               m_rwkv_r_k, m_rwkv_ln_w, m_rwkv_ln_b, m_final_norm_gain),
                    pack_small(v_norm_gain, v_fox_forget_bias, v_rwkv_shift_mix, v_rwkv_w0, v_rwkv_a0, v_rwkv_k_k, v_rwkv_k_a,
                               v_rwkv_r_k, v_rwkv_ln_w, v_rwkv_ln_b, v_final_norm_gain),
                    [(rs, N_DEV)], SR)

    def unpack_b(pkd):
        return dict(w_proj_fox=pkd[0:1024][None], w_proj_rwkv=pkd[1024:2048][None], w_out=pkd[2048:4096].reshape(1, 256, D),
                    rwkv_w2=pkd[4096:4192, 0:128][None], rwkv_a2=pkd[4096:4192, 128:256][None])

    def unpack_s(pkd):
        flat = pkd.reshape(1, SR * LANE)
        names = [("norm_gain", D), ("fox_forget_bias", 8), ("rwkv_shift_mix", 4288), ("rwkv_w0", 1024), ("rwkv_a0", 1024),
                 ("rwkv_k_k", 1024), ("rwkv_k_a", 1024), ("rwkv_r_k", 1024), ("rwkv_ln_w", 1024), ("rwkv_ln_b", 1024),
                 ("final_norm_gain", D), ("loss", 1)]
        out, off = {}, 0
        for nm, n in names:
            out[nm] = flat[:, off:off + n]
            off += n
        out["rwkv_r_k"] = out["rwkv_r_k"].reshape(1, 16, 64)
        out["final_norm_gain"] = out["final_norm_gain"].reshape(D)
        return out

    order = ["norm_gain", "w_in", "fox_forget_bias", "rwkv_shift_mix", "rwkv_w0", "rwkv_w2", "rwkv_a0", "rwkv_a2", "rwkv_k_k",
             "rwkv_k_a", "rwkv_r_k", "rwkv_ln_w", "rwkv_ln_b", "w_proj_fox", "w_proj_rwkv", "w_out", "final_norm_gain"]
    result = []
    loss = None
    for kind, big in enumerate((g_in, d_in, m_in, v_in)):
        d = {**unpack_b(outs_b[kind]), **unpack_s(outs_s[kind]), "w_in": big[None]}
        if kind == 0:
            loss = d["loss"].reshape(())
        result += [d[n] for n in order]
    return (loss, grad_x2[None], *result)
```

```python
import functools

import jax
import jax.numpy as jnp
from jax import lax
from jax.experimental import pallas as pl
from jax.experimental.pallas import tpu as pltpu

F32 = jnp.float32
BF16 = jnp.bfloat16
HI = lax.Precision.HIGHEST
H3 = lax.Precision.HIGH
MESH = pl.DeviceIdType.MESH

FOX_HD = 128
RW_HD = 64
RMS_EPS = 1e-6
GN_EPS = 64e-5
L2_EPS = 1e-12
ADAM_LR = 0.001
ADAM_B1 = 0.9
ADAM_B2 = 0.999
ADAM_EPS = 1e-08
ADAM_WD = 0.01
ADAM_STEP = 10

LANE = 128
SUB = 8
VMEM_LIMIT = 56 * 1024 * 1024
N_DEV = 8
CHUNK = 128
SCAN_GROUP = 1
SCAN_PAIRS = 8
PS = None
NEG = -1e30


def _scan_shape(T):
    c = min(CHUNK, T)
    return c, min(SCAN_GROUP, T // c)


def _cp(sem=None):
    return pltpu.CompilerParams(dimension_semantics=sem, vmem_limit_bytes=VMEM_LIMIT)


def _sigmoid(x):
    return jax.nn.sigmoid(x)


def _softplus(x):
    return jnp.maximum(x, 0.0) + jnp.log(1.0 + jnp.exp(-jnp.abs(x)))


def _nn(a, b, prec=None):
    return lax.dot_general(a, b, (((1,), (0,)), ((), ())), precision=prec, preferred_element_type=F32)


def _nt(a, b, prec=None):
    return lax.dot_general(a, b, (((1,), (1,)), ((), ())), precision=prec, preferred_element_type=F32)


def _tn(a, b, prec=None):
    return lax.dot_general(a, b, (((0,), (0,)), ((), ())), precision=prec, preferred_element_type=F32)


def _iota2(shape, dim):
    return lax.broadcasted_iota(jnp.int32, shape, dim)


def _seg_sum(x):
    r = _iota2((LANE, LANE), 0) // RW_HD
    c = _iota2((LANE, LANE), 1) // RW_HD
    bd = (r == c).astype(F32)
    parts = [_nn(x[:, j * LANE:(j + 1) * LANE], bd, H3) for j in range(x.shape[1] // LANE)]
    return parts[0] if len(parts) == 1 else jnp.concatenate(parts, axis=1)


def _mm(a, b, *, ta=False, tb=False, out_dtype=F32, tm=1024, tn=1024, tk=None, name, side=None):
    assert not (ta and tb)
    K, M = a.shape if ta else a.shape[::-1]
    N = b.shape[0] if tb else b.shape[1]
    tm, tn = min(tm, M), min(tn, N)
    tk = K if tk is None else tk
    nk = K // tk
    assert M % tm == 0 and N % tn == 0 and K % tk == 0
    gi, gj = M // tm, N // tn
    a_spec = pl.BlockSpec((tk, tm), lambda i, j, k: (k, i)) if ta else pl.BlockSpec((tm, tk), lambda i, j, k: (i, k))
    b_spec = pl.BlockSpec((tn, tk), lambda i, j, k: (j, k)) if tb else pl.BlockSpec((tk, tn), lambda i, j, k: (k, j))
    side_ins, side_outs, side_scr, side_make = side if side is not None else ((), (), (), None)
    n_si, n_so = len(side_ins), len(side_outs)
    n_acc = 0 if nk == 1 else 1

    def body(*refs):
        a_ref, b_ref = refs[:2]
        o_ref = refs[2 + n_si]
        scr = refs[3 + n_si + n_so:]
        k = pl.program_id(2)
        if side_make is not None:
            start, finish = side_make(refs[2:2 + n_si], refs[3 + n_si:3 + n_si + n_so], scr[n_acc:])
            first = jnp.logical_and(jnp.logical_and(pl.program_id(0) == 0, pl.program_id(1) == 0), k == 0)
            last = jnp.logical_and(jnp.logical_and(pl.program_id(0) == gi - 1, pl.program_id(1) == gj - 1), k == nk - 1)
            pl.when(first)(start)
        av = a_ref[...].astype(BF16)
        bv = b_ref[...].astype(BF16)
        p = _tn(av, bv) if ta else _nt(av, bv) if tb else _nn(av, bv)
        if nk == 1:
            o_ref[...] = p.astype(out_dtype)
        else:
            acc_ref = scr[0]

            @pl.when(k == 0)
            def _():
                acc_ref[...] = p

            @pl.when(k > 0)
            def _():
                acc_ref[...] += p

            @pl.when(k == nk - 1)
            def _():
                o_ref[...] = acc_ref[...].astype(out_dtype)
        if side_make is not None:
            pl.when(last)(finish)

    any_spec = pl.BlockSpec(memory_space=pl.ANY)
    res = pl.pallas_call(
        body, name=name,
        out_shape=[jax.ShapeDtypeStruct((M, N), out_dtype)] + list(side_outs),
        grid=(gi, gj, nk),
        in_specs=[a_spec, b_spec] + [any_spec] * n_si,
        out_specs=[pl.BlockSpec((tm, tn), lambda i, j, k: (i, j))] + [any_spec] * n_so,
        scratch_shapes=([] if nk == 1 else [pltpu.VMEM((tm, tn), F32)]) + list(side_scr),
        compiler_params=_cp(("arbitrary",) * 3 if side is not None else ("parallel", "parallel", "arbitrary")),
    )(a, b, *side_ins)
    return res if side is not None else res[0]


def _rows(arr, tb, w, cb=0):
    return (arr, (tb, w), lambda i: (i, cb))


def _whole(arr):
    nd = arr.ndim
    return (arr, arr.shape, lambda i: (0,) * nd)


def _rowcall(name, T, tb, ins, body, outs, accs=(), side=None):
    n_in, n_out, n_acc = len(ins), len(outs), len(accs)
    side_ins, side_outs, side_scr, side_make = side if side is not None else ((), (), (), None)
    n_si = len(side_ins)

    def kern(*refs):
        i = pl.program_id(0)
        if side_make is not None:
            side_refs = refs[n_in + n_si + n_out + n_acc:]
            start, finish = side_make(refs[n_in:n_in + n_si], side_refs[:len(side_outs)], side_refs[len(side_outs):])
            pl.when(i == 0)(start)
            refs = refs[:n_in] + refs[n_in + n_si:]
        vals = [r[...] for r in refs[:n_in]]
        ro, ao = body(i, *vals)
        for r, v in zip(refs[n_in:n_in + n_out], ro):
            if isinstance(v, (list, tuple)):
                off = 0
                for piece in v:
                    w = piece.shape[1]
                    r[:, off:off + w] = piece.astype(r.dtype)
                    off += w
            else:
                r[...] = v.astype(r.dtype)
        if accs:
            acc_refs = refs[n_in + n_out:n_in + n_out + n_acc]

            @pl.when(i == 0)
            def _():
                for r in acc_refs:
                    r[...] = jnp.zeros(r.shape, F32)

            for r, v in zip(acc_refs, ao):
                r[...] += v
        if side_make is not None:
            pl.when(i == T // tb - 1)(finish)

    any_spec = pl.BlockSpec(memory_space=pl.ANY)
    out_shape = [jax.ShapeDtypeStruct((T, w), dt) for (w, dt) in outs] + [jax.ShapeDtypeStruct(s, F32) for s in accs]
    out_specs = [pl.BlockSpec((tb, w), lambda i: (i, 0)) for (w, dt) in outs] + [pl.BlockSpec(s, lambda i: (0, 0)) for s in accs]
    res = pl.pallas_call(
        kern, name=name,
        out_shape=out_shape + list(side_outs),
        grid=(T // tb,),
        in_specs=[pl.BlockSpec(bs, im) for (_, bs, im) in ins] + [any_spec] * n_si,
        out_specs=out_specs + [any_spec] * len(side_outs),
        scratch_shapes=list(side_scr),
        compiler_params=_cp(("arbitrary",)),
    )(*[a for (a, _, _) in ins], *side_ins)
    return res


def _rms_math(x, g):
    r = lax.rsqrt(jnp.mean(x * x, axis=-1, keepdims=True) + RMS_EPS)
    return x * r * g


def _merge_math(ga, gb, pa, pb):
    return _sigmoid(ga) * pa + _sigmoid(gb) * pb


def _prep_math(xk, xwd, xad, w0, a0, kk_w, ka_w, w2p, a2p):
    z = w0 + _nn(jnp.tanh(xwd), w2p, H3)
    w = -_softplus(-z) - 0.5
    lw = -jnp.exp(w)
    ag = _sigmoid(a0 + _nn(xad, a2p, H3))
    p = xk * kk_w
    n = jnp.maximum(jnp.sqrt(_seg_sum(p * p)), L2_EPS)
    kk = p / n
    kp = xk * (1.0 + (ag - 1.0) * ka_w)
    return lw, kp, -kk, kk * ag


def _post_math(y, r, kp, v, z, lnw, lnb, rk):
    inv = 1.0 / RW_HD
    mu = _seg_sum(y) * inv
    d = y - mu
    var = _seg_sum(d * d) * inv
    yn = d * lax.rsqrt(var + GN_EPS) * lnw + lnb
    bonus = _seg_sum(r * kp * rk) * v
    return (yn + bonus) * (z * _sigmoid(z))


def _neumann(ms, depth):
    eye = (_iota2(ms[0].shape, 0) == _iota2(ms[0].shape, 1)).astype(F32)
    width = ms[0].shape[1]
    mp = [_nn(m, m, PS) for m in ms]
    inv = [eye + m for m in ms]
    n = 2
    while n < depth:
        last = 2 * n >= depth
        for i in range(len(ms)):
            if last:
                inv[i] = inv[i] + _nn(mp[i], inv[i], PS)
            else:
                z = _nn(mp[i], jnp.concatenate([mp[i], inv[i]], axis=1), PS)
                mp[i], inv[i] = z[:, :width], inv[i] + z[:, width:]
        n *= 2
    return inv


@functools.partial(jax.custom_vjp, nondiff_argnums=(1,))
def _unit_inverses(ms, depth):
    if depth <= RW_HD:
        return tuple(_neumann(list(ms), depth))
    h, n = depth // 2, len(ms)
    cat = jnp.concatenate
    z = jnp.zeros((h, h), F32)

    def heads(m, r, c):
        b0 = m[r * h:(r + 1) * h, c * h:(c + 1) * h]
        b1 = m[depth + r * h:depth + (r + 1) * h, depth + c * h:depth + (c + 1) * h]
        return cat([cat([b0, z], axis=1), cat([z, b1], axis=1)], axis=0)

    diag = _neumann([heads(m, 0, 0) for m in ms] + [heads(m, 1, 1) for m in ms], h)
    ta, td = diag[:n], diag[n:]
    low = [_nn(heads(m, 1, 0), a, PS) for m, a in zip(ms, ta)]
    low = [_nn(d, x, PS) for d, x in zip(td, low)]
    out = []
    for a, x, d in zip(ta, low, td):
        rows = []
        for hd in (0, 1):
            sl = slice(hd * h, (hd + 1) * h)
            top, bot = [a[sl, sl], z], [x[sl, sl], d[sl, sl]]
            pad = [z, z]
            rows.append(cat(top + pad if hd == 0 else pad + top, axis=1))
            rows.append(cat(bot + pad if hd == 0 else pad + bot, axis=1))
        out.append(cat(rows, axis=0))
    return tuple(out)


def _unit_inverses_fwd(ms, depth):
    inv = _unit_inverses(ms, depth)
    return inv, inv


def _unit_inverses_bwd(depth, inv, cts):
    left = [_tn(t, g, PS) for t, g in zip(inv, cts)]
    return (tuple(_nt(l, t, PS) for l, t in zip(left, inv)),)


_unit_inverses.defvjp(_unit_inverses_fwd, _unit_inverses_bwd)


@jax.custom_vjp
def _known_inverses(ms, inv):
    return inv


def _known_inverses_fwd(ms, inv):
    return inv, inv


def _known_inverses_bwd(inv, cts):
    dms = _unit_inverses_bwd(None, inv, cts)[0]
    return dms, tuple(jnp.zeros_like(t) for t in inv)


_known_inverses.defvjp(_known_inverses_fwd, _known_inverses_bwd)


def _scan_group(s0s, *flat, known_inv=None, with_inv=False):
    P = len(s0s)
    G = len(flat) // (6 * P)
    ch = [flat[6 * i:6 * i + 6] for i in range(P * G)]
    C = ch[0][0].shape[0]
    C2 = 2 * C
    cat = jnp.concatenate
    m0 = _iota2((1, LANE), 1) < RW_HD
    mask0 = m0.astype(F32)
    mask1 = 1.0 - mask0
    r2 = _iota2((C2, C2), 0)
    c2 = _iota2((C2, C2), 1)
    dist = r2 - c2
    in_head = dist <= r2 % C
    lower = (_iota2((C, C), 0) >= _iota2((C, C), 1)).astype(F32)
    bd = (_iota2((LANE, LANE), 0) // RW_HD) == (_iota2((LANE, LANE), 1) // RW_HD)

    def tri(m, strict):
        return jnp.where(dist > 0 if strict else dist >= 0, jnp.where(in_head, m, 0.0), 0.0)

    def sel(z):
        return jnp.where(m0, z[:C], z[C:])

    gs = [_nn(lower, c[1], HI) for c in ch]
    pre = []
    for (r, lw, k, v, a, b), g in zip(ch, gs):
        g_end = jnp.sum(lw, axis=0, keepdims=True)
        gm = g - jnp.sum(lw[:C // 2], axis=0, keepdims=True)
        en = jnp.exp(-gm)
        ec = jnp.exp(g_end - g)
        pre.append(dict(at=a * jnp.exp(g - lw), rt=r * jnp.exp(g), am=a * jnp.exp(gm - lw), rm=r * jnp.exp(gm),
                        bt=b * en, kt=k * en, bh=b * ec, kh=k * ec, dec=jnp.exp(g_end), v=v))
    grams = [_nt(cat([p["am"] * mask0, p["am"] * mask1, p["rm"] * mask0, p["rm"] * mask1], axis=0),
                 cat([p["bt"], p["bt"], p["kt"], p["kt"]], axis=0), PS) for p in pre]
    mab = tuple(tri(gm[:C2, :C2], True) for gm in grams)
    tinv = _unit_inverses(mab, C) if known_inv is None else _known_inverses(mab, known_inv)
    xv =[sel(_nn(tri(gm[:C2, C2:], True), cat([p["v"], p["v"]], axis=0), PS)) for gm, p in zip(grams, pre)]
    ys, s = [None] * (P * G), list(s0s)
    for i in range(G):
        for q in range(P):
            n = q * G + i
            p, gm = pre[n], grams[n]
            sx = _nt(cat([p["at"], p["rt"]], axis=0), s[q], PS)
            x = sx[:C] + xv[n]
            u = sel(_nn(tinv[n], cat([x, x], axis=0), PS))
            v = p["v"]
            ys[n] = sx[C:] + sel(_nn(cat([tri(gm[C2:, :C2], False), tri(gm[C2:, C2:], False)], axis=1),
                                     cat([u, u, v, v], axis=0), PS))
            s[q] = s[q] * p["dec"] + jnp.where(bd, _tn(cat([u, v], axis=0), cat([p["bh"], p["kh"]], axis=0), PS), 0.0)
    return (tuple(ys), tuple(s), tinv) if with_inv else (tuple(ys), tuple(s))


def _scan_fwd(xs, lw, kp, an, bb, T):
    C, G = _scan_shape(T)
    P = SCAN_PAIRS
    nc = T // (C * G)
    npair = 1024 // LANE

    def kern(r_ref, lw_ref, k_ref, v_ref, a_ref, b_ref, y_ref, st_ref, inv_ref, s_scr):
        n = pl.program_id(1)

        @pl.when(n == 0)
        def _():
            s_scr[...] = jnp.zeros(s_scr.shape, F32)

        st_ref[0] = s_scr[...]
        ins = (r_ref, lw_ref, k_ref, v_ref, a_ref, b_ref)
        ys, s1, inv = _scan_group(
            tuple(s_scr[q] for q in range(P)),
            *[ref[i * C:(i + 1) * C, q * LANE:(q + 1) * LANE] for q in range(P) for i in range(G) for ref in ins], with_inv=True)
        for q in range(P):
            for i in range(G):
                y_ref[i * C:(i + 1) * C, q * LANE:(q + 1) * LANE] = ys[q * G + i]
                inv_ref[0, 0, q * G + i] = inv[q * G + i]
            s_scr[q] = s1[q]

    def col(off):
        return pl.BlockSpec((C * G, P * LANE), lambda p, n: (n, off // P + p))

    return pl.pallas_call(
        kern, name="rwkv_scan_fwd",
        out_shape=[jax.ShapeDtypeStruct((T, 1024), F32), jax.ShapeDtypeStruct((nc, npair, LANE, LANE), F32),
                   jax.ShapeDtypeStruct((nc, npair // P, P * G, 2 * C, 2 * C), F32)],
        grid=(npair // P, nc),
        in_specs=[col(0), col(0), col(0), col(16), col(0), col(0)],
        out_specs=[col(0), pl.BlockSpec((1, P, LANE, LANE), lambda p, n: (n, p, 0, 0)),
                   pl.BlockSpec((1, 1, P * G, 2 * C, 2 * C), lambda p, n: (n, p, 0, 0, 0))],
        scratch_shapes=[pltpu.VMEM((P, LANE, LANE), F32)],
        compiler_params=_cp(("parallel", "arbitrary")),
    )(xs, lw, kp, xs, an, bb)


def _scan_bwd(xs, lw, kp, an, bb, states, invs, dy, T):
    C, G = _scan_shape(T)
    P = SCAN_PAIRS
    nc = T // (C * G)
    npair = 1024 // LANE

    def kern(r_ref, lw_ref, k_ref, v_ref, a_ref, b_ref, st_ref, inv_ref, dy_ref, dr_ref, dlw_ref, dk_ref, dv_ref, da_ref, db_ref,
             ds_scr):
        n = pl.program_id(1)

        @pl.when(n == 0)
        def _():
            ds_scr[...] = jnp.zeros(ds_scr.shape, F32)

        ins = (r_ref, lw_ref, k_ref, v_ref, a_ref, b_ref)
        units = [(q, i) for q in range(P) for i in range(G)]
        known = tuple(inv_ref[0, 0, q * G + i] for q, i in units)
        _, vjp = jax.vjp(functools.partial(_scan_group, known_inv=known), tuple(st_ref[0, q] for q in range(P)),
                         *[ref[i * C:(i + 1) * C, q * LANE:(q + 1) * LANE] for q, i in units for ref in ins])
        grads = vjp((tuple(dy_ref[i * C:(i + 1) * C, q * LANE:(q + 1) * LANE] for q, i in units),
                     tuple(ds_scr[q] for q in range(P))))
        for q in range(P):
            ds_scr[q] = grads[0][q]
        outs = (dr_ref, dlw_ref, dk_ref, dv_ref, da_ref, db_ref)
        for n_, (q, i) in enumerate(units):
            for t, ref in enumerate(outs):
                ref[i * C:(i + 1) * C, q * LANE:(q + 1) * LANE] = grads[1 + 6 * n_ + t]

    def col(off):
        return pl.BlockSpec((C * G, P * LANE), lambda p, n: (nc - 1 - n, off // P + p))

    return pl.pallas_call(
        kern, name="rwkv_scan_bwd",
        out_shape=[jax.ShapeDtypeStruct((T, 1024), F32)] * 6,
        grid=(npair // P, nc),
        in_specs=[col(0), col(0), col(0), col(16), col(0), col(0),
                  pl.BlockSpec((1, P, LANE, LANE), lambda p, n: (nc - 1 - n, p, 0, 0)),
                  pl.BlockSpec((1, 1, P * G, 2 * C, 2 * C), lambda p, n: (nc - 1 - n, p, 0, 0, 0)), col(0)],
        out_specs=[col(0)] * 6,
        scratch_shapes=[pltpu.VMEM((P, LANE, LANE), F32)],
        compiler_params=_cp(("parallel", "arbitrary")),
    )(xs, lw, kp, xs, an, bb, states, invs, dy)


def _gates_fwd(u, bias_pad, T, f_cb):
    nb = T // LANE

    def kern(f_ref, b_ref, c_ref):
        x = f_ref[...] + b_ref[...]
        lf = jnp.minimum(x, 0.0) - jnp.log(1.0 + jnp.exp(-jnp.abs(x)))
        lft = lf.T
        ut = (_iota2((LANE, LANE), 0) <= _iota2((LANE, LANE), 1)).astype(F32)
        carry = jnp.zeros((LANE, 1), F32)
        for blk in range(nb):
            seg = lft[:, blk * LANE:(blk + 1) * LANE]
            cs = _nn(seg, ut, HI) + carry
            c_ref[:, blk * LANE:(blk + 1) * LANE] = cs[:SUB, :]
            carry = carry + jnp.sum(seg, axis=1, keepdims=True)

    return pl.pallas_call(
        kern, name="fox_gates_fwd",
        out_shape=jax.ShapeDtypeStruct((SUB, T), F32),
        grid=(1,),
        in_specs=[pl.BlockSpec((T, LANE), lambda i: (0, f_cb)), pl.BlockSpec((1, LANE), lambda i: (0, 0))],
        out_specs=pl.BlockSpec((SUB, T), lambda i: (0, 0)),
        compiler_params=_cp(("arbitrary",)),
    )(u, bias_pad)


def _gates_bwd(dc, u, bias_pad, T, f_cb):
    nb = T // LANE

    def kern(dc_ref, f_ref, b_ref, dfl_ref, db_ref):
        dcv = jnp.concatenate([dc_ref[...], jnp.zeros((LANE - SUB, T), F32)], axis=0)
        lt = (_iota2((LANE, LANE), 0) >= _iota2((LANE, LANE), 1)).astype(F32)
        carry = jnp.zeros((LANE, 1), F32)
        pieces = [None] * nb
        for blk in range(nb - 1, -1, -1):
            seg = dcv[:, blk * LANE:(blk + 1) * LANE]
            pieces[blk] = _nn(seg, lt, HI) + carry
            carry = carry + jnp.sum(seg, axis=1, keepdims=True)
        dlf = (pieces[0] if nb == 1 else jnp.concatenate(pieces, axis=1)).T
        x = f_ref[...] + b_ref[...]
        dfl = dlf * _sigmoid(-x)
        dfl_ref[...] = dfl
        db_ref[...] = jnp.sum(dfl, axis=0, keepdims=True)

    return pl.pallas_call(
        kern, name="fox_gates_bwd",
        out_shape=[jax.ShapeDtypeStruct((T, LANE), F32), jax.ShapeDtypeStruct((1, LANE), F32)],
        grid=(1,),
        in_specs=[pl.BlockSpec((SUB, T), lambda i: (0, 0)), pl.BlockSpec((T, LANE), lambda i: (0, f_cb)),
                  pl.BlockSpec((1, LANE), lambda i: (0, 0))],
        out_specs=[pl.BlockSpec((T, LANE), lambda i: (0, 0)), pl.BlockSpec((1, LANE), lambda i: (0, 0))],
        compiler_params=_cp(("arbitrary",)),
    )(dc, u, bias_pad)


ATTN_HEADS = 2


def _attn_block(T):
    return 512 if T % 512 == 0 and T >= 1024 else 128


def _attn_fwd(u, c3, T):
    H, HP = 8, ATTN_HEADS
    bq = _attn_block(T)
    nq = T // bq
    scale = FOX_HD ** -0.5
    lanes = [slice(h * LANE, (h + 1) * LANE) for h in range(HP)]

    def kern(q_ref, k_ref, v_ref, z_ref, cq_ref, ck_ref, o_ref, oa_ref, lse_ref):
        i = pl.program_id(1)
        q = [(q_ref[:, ln] * scale).astype(BF16) for ln in lanes]
        c0 = [cq_ref[h][:, 0:1] for h in range(HP)]

        def step(j, carry, diagonal=False):
            off = pl.multiple_of(j * bq, bq)
            s = [_nt(q[h], k_ref[pl.ds(off, bq), lanes[h]].astype(BF16)) + (c0[h] - ck_ref[h, :, pl.ds(off, bq)])
                 for h in range(HP)]
            ps, out = [], []
            for h in range(HP):
                m, l, acc = carry[h]
                sh = s[h]
                if diagonal:
                    sh = jnp.where(_iota2((bq, bq), 1) <= _iota2((bq, bq), 0), sh, NEG)
                m_new = jnp.maximum(m, jnp.max(sh, axis=1, keepdims=True))
                p = jnp.exp(sh - m_new)
                alpha = jnp.exp(m - m_new)
                p_hi = p.astype(BF16)
                ps.append((p_hi, (p - p_hi.astype(F32)).astype(BF16)))
                out.append((m_new, alpha * l + jnp.sum(p, axis=1, keepdims=True), alpha * acc))
            res = []
            for h, (m, l, acc) in enumerate(out):
                vj = v_ref[pl.ds(off, bq), lanes[h]].astype(BF16)
                res.append((m, l, acc + _nn(ps[h][0], vj) + _nn(ps[h][1], vj)))
            return tuple(res)

        init = tuple((jnp.full((bq, 1), NEG, F32), jnp.zeros((bq, 1), F32), jnp.zeros((bq, FOX_HD), F32)) for _ in range(HP))
        res = step(i, lax.fori_loop(0, i, step, init), diagonal=True)
        for h, (m, l, acc) in enumerate(res):
            o = acc / l
            z = z_ref[:, lanes[h]]
            o_ref[:, lanes[h]] = o
            oa_ref[:, lanes[h]] = (o * z * _sigmoid(z)).astype(BF16)
            lse_ref[h] = m + jnp.log(l)

    W = HP * LANE
    return pl.pallas_call(
        kern, name="fox_attn_fwd",
        out_shape=[jax.ShapeDtypeStruct((T, 1024), F32), jax.ShapeDtypeStruct((T, 1024), BF16),
                   jax.ShapeDtypeStruct((H, T, 1), F32)],
        grid=(H // HP, nq),
        in_specs=[pl.BlockSpec((bq, W), lambda g, i: (i, g)),
                  pl.BlockSpec((T, W), lambda g, i: (0, 8 // HP + g)),
                  pl.BlockSpec((T, W), lambda g, i: (0, 16 // HP + g)),
                  pl.BlockSpec((bq, W), lambda g, i: (i, 24 // HP + g)),
                  pl.BlockSpec((HP, 1, bq), lambda g, i: (g, 0, i)),
                  pl.BlockSpec((HP, 1, T), lambda g, i: (g, 0, 0))],
        out_specs=[pl.BlockSpec((bq, W), lambda g, i: (i, g)),
                   pl.BlockSpec((bq, W), lambda g, i: (i, g)),
                   pl.BlockSpec((HP, bq, 1), lambda g, i: (g, i, 0))],
        compiler_params=_cp(("parallel", "arbitrary")),
    )(u, u, u, u, c3, c3)


def _attn_probs(s, lse_i, diagonal):
    if not diagonal:
        return jnp.exp(s - lse_i)
    keep = _iota2(s.shape, 1) <= _iota2(s.shape, 0)
    return jnp.where(keep, jnp.exp(jnp.where(keep, s, NEG) - lse_i), 0.0)


def _attn_pre_math(doa, z, o):
    sg = _sigmoid(z)
    do = (doa * z * sg).astype(BF16)
    dz = doa * o * (sg * (1.0 + z * (1.0 - sg)))
    head_of = (_iota2((o.shape[1], LANE), 0) // FOX_HD == _iota2((o.shape[1], LANE), 1)).astype(F32)
    return do, dz, _nn(do.astype(F32) * o, head_of, HI)


def _attn_bwd(u, c3, lse, do, delta, T):
    H, HP = 8, ATTN_HEADS
    bq = _attn_block(T)
    nq = T // bq
    scale = FOX_HD ** -0.5
    lanes = [slice(h * LANE, (h + 1) * LANE) for h in range(HP)]

    def kern(q_ref, k_ref, v_ref, c_ref, lse_ref, do_ref, dl_ref, dq_ref, dk_ref, dv_ref, dc_ref):
        j = pl.program_id(1)

        @pl.when(j == 0)
        def _():
            dq_ref[...] = jnp.zeros(dq_ref.shape, F32)

        kj = [k_ref[:, ln].astype(BF16) for ln in lanes]
        vj = [v_ref[:, ln].astype(BF16) for ln in lanes]
        joff = pl.multiple_of(j * bq, bq)
        ck = [c_ref[h, :, pl.ds(joff, bq)] for h in range(HP)]

        def step(i, carry, diagonal=False):
            off = pl.multiple_of(i * bq, bq)
            qs = [(q_ref[pl.ds(off, bq), ln] * scale).astype(BF16) for ln in lanes]
            dob = [do_ref[pl.ds(off, bq), ln] for ln in lanes]
            s = [_nt(qs[h], kj[h]) + (c_ref[h, :, pl.ds(off, bq)][:, 0:1] - ck[h]) for h in range(HP)]
            dp = [_nt(dob[h], vj[h]) for h in range(HP)]
            pb, dsb, dcs = [], [], []
            for h in range(HP):
                p = _attn_probs(s[h], lse_ref[h, pl.ds(off, bq), :], diagonal)
                ds = p * (dp[h] - dl_ref[h, pl.ds(off, bq), :])
                pb.append(p.astype(BF16))
                dsb.append(ds.astype(BF16))
                dcs.append(jnp.sum(ds, axis=0, keepdims=True))
            out = []
            for h, (dk, dv, dc) in enumerate(carry):
                dq_ref[pl.ds(off, bq), lanes[h]] += _nn(dsb[h], kj[h]) * scale
                out.append((dk + _tn(dsb[h], qs[h]), dv + _tn(pb[h], dob[h]), dc - dcs[h]))
            return tuple(out)

        init = tuple((jnp.zeros((bq, FOX_HD), F32), jnp.zeros((bq, FOX_HD), F32), jnp.zeros((1, bq), F32)) for _ in range(HP))
        res = lax.fori_loop(j + 1, nq, step, step(j, init, diagonal=True))
        for h, (dk, dv, dc) in enumerate(res):
            dk_ref[:, lanes[h]] = dk
            dv_ref[:, lanes[h]] = dv
            dc_ref[h] = dc

    W = HP * LANE
    full = lambda cb: pl.BlockSpec((T, W), lambda g, j: (0, cb // HP + g))
    blk = lambda cb: pl.BlockSpec((bq, W), lambda g, j: (j, cb // HP + g))
    col1 = pl.BlockSpec((HP, T, 1), lambda g, j: (g, 0, 0))
    return pl.pallas_call(
        kern, name="fox_attn_bwd",
        out_shape=[jax.ShapeDtypeStruct((T, 1024), F32)] * 3 + [jax.ShapeDtypeStruct((H, 1, T), F32)],
        grid=(H // HP, nq),
        in_specs=[full(0), blk(8), blk(16), pl.BlockSpec((HP, 1, T), lambda g, j: (g, 0, 0)), col1, full(0), col1],
        out_specs=[full(0), blk(0), blk(0), pl.BlockSpec((HP, 1, bq), lambda g, j: (g, 0, j))],
        compiler_params=_cp(("parallel", "arbitrary")),
    )(u, u, u, c3, lse, do, delta)


def _place():
    return lax.axis_index("x"), lax.axis_index("y"), lax.axis_index("c")


def _slot(p):
    return 4 * p[0] + 2 * p[1] + p[2]


def _other_chips(x, y):
    return [(1 - x, y), (x, 1 - y), (1 - x, 1 - y)]


def _allgather_steps(in_refs, out_refs, scratch):
    (src,), (dst,) = in_refs, out_refs
    send_sems, recv_sems, local_sem = scratch
    x, y, c = _place()
    me, sibling = (x, y, c), (x, y, 1 - c)
    chips = _other_chips(x, y)

    def copy(k, block, to, from_input=False):
        d = dst.at[_slot(block)]
        return pltpu.make_async_remote_copy(
            src_ref=src if from_input else d, dst_ref=d, send_sem=send_sems.at[k], recv_sem=recv_sems.at[k],
            device_id=to, device_id_type=MESH)

    def first_copies():
        return [copy(0, me, sibling, True)] + [copy(1 + j, me, (*chip, c), True) for j, chip in enumerate(chips)]

    def start():
        pltpu.make_async_copy(src, dst.at[_slot(me)], local_sem).start()
        for cp in first_copies():
            cp.start()

    def finish():
        passed = []
        for j, chip in enumerate(chips):
            copy(1 + j, (*chip, c), me).wait_recv()
            passed.append(copy(4 + j, (*chip, c), sibling))
            passed[-1].start()
        copy(0, sibling, me).wait_recv()
        for j, chip in enumerate(chips):
            copy(4 + j, (*chip, 1 - c), me).wait_recv()
        for cp in first_copies() + passed:
            cp.wait_send()
        pltpu.make_async_copy(src, dst.at[_slot(me)], local_sem).wait()

    return start, finish


def _allgather_relay_steps(in_refs, out_refs, scratch):
    (src,), (dst,) = in_refs, out_refs
    send_sems, recv_sems, local_sem = scratch
    x, y, c = _place()
    me, sibling = (x, y, c), (x, y, 1 - c)
    x_nbr, y_nbr, diag = (1 - x, y, c), (x, 1 - y, c), (1 - x, 1 - y, c)
    flip = lambda a, bit: a + bit - 2 * a * bit
    relay_from = (flip(x, 1 - c), flip(y, c), c)
    relay_to = (flip(x, c), flip(y, 1 - c), c)

    def copy(k, block, to, from_input=False):
        d = dst.at[_slot(block)]
        return pltpu.make_async_remote_copy(
            src_ref=src if from_input else d, dst_ref=d, send_sem=send_sems.at[k], recv_sem=recv_sems.at[k],
            device_id=to, device_id_type=MESH)

    def first_copies():
        return [copy(0, me, sibling, True), copy(1, me, x_nbr, True), copy(2, me, y_nbr, True)]

    def other(block):
        return block[:2] + (1 - c,)

    def start():
        pltpu.make_async_copy(src, dst.at[_slot(me)], local_sem).start()
        for cp in first_copies():
            cp.start()

    def finish():
        copy(1, x_nbr, me).wait_recv()
        copy(2, y_nbr, me).wait_recv()
        later = [copy(3, relay_from, relay_to), copy(4, x_nbr, sibling), copy(5, y_nbr, sibling)]
        for cp in later:
            cp.start()
        copy(3, diag, me).wait_recv()
        later.append(copy(6, diag, sibling))
        later[-1].start()
        copy(0, sibling, me).wait_recv()
        for k, block in ((4, x_nbr), (5, y_nbr), (6, diag)):
            copy(k, other(block), me).wait_recv()
        for cp in first_copies() + later:
            cp.wait_send()
        pltpu.make_async_copy(src, dst.at[_slot(me)], local_sem).wait()

    return start, finish


def _allgather_side(a, relay=False):
    return ((a,), (jax.ShapeDtypeStruct((N_DEV,) + a.shape, a.dtype),),
            (pltpu.SemaphoreType.DMA((7,)), pltpu.SemaphoreType.DMA((7,)), pltpu.SemaphoreType.DMA),
            _allgather_relay_steps if relay else _allgather_steps)


def _allgather(a, name, relay=False):
    ins, outs, scratch, make = _allgather_side(a, relay)

    def body(a_ref, o_ref, *scr):
        start, finish = make((a_ref,), (o_ref,), scr)
        start()
        finish()

    any_spec = pl.BlockSpec(memory_space=pl.ANY)
    return pl.pallas_call(body, name=name, out_shape=outs[0], in_specs=[any_spec], out_specs=any_spec,
                          scratch_shapes=list(scratch))(a)


def _exchange_pair_steps(in_refs, out_refs, scratch):
    (src,), (dst,) = in_refs, out_refs
    send_sems, recv_sems = scratch
    x, y, c = _place()
    sibling = (x, y, 1 - c)
    slots = [_slot(sibling)] + [_slot((*chip, 1 - c)) for chip in _other_chips(x, y)]

    def copies():
        return [pltpu.make_async_remote_copy(
            src_ref=src.at[ps], dst_ref=dst.at[k], send_sem=send_sems.at[k], recv_sem=recv_sems.at[k],
            device_id=sibling, device_id_type=MESH) for k, ps in enumerate(slots)]

    def start():
        for cp in copies():
            cp.start()

    def finish():
        for cp in copies():
            cp.wait()

    return start, finish


def _exchange_pair_side(g):
    return ((g,), (jax.ShapeDtypeStruct((4,) + g.shape[1:], g.dtype),),
            (pltpu.SemaphoreType.DMA((4,)), pltpu.SemaphoreType.DMA((4,))), _exchange_pair_steps)


def _exchange_pair(g):
    ins, outs, scratch, make = _exchange_pair_side(g)

    def body(g_ref, r_ref, *scr):
        start, finish = make((g_ref,), (r_ref,), scr)
        start()
        finish()

    any_spec = pl.BlockSpec(memory_space=pl.ANY)
    return pl.pallas_call(body, name="exchange_pair", out_shape=outs[0], in_specs=[any_spec], out_specs=any_spec,
                          scratch_shapes=list(scratch))(g)


def _tiling(R, Cc, tile, by_cols):
    if by_cols:
        assert Cc % tile == 0
        return Cc // tile, (R, tile), lambda lead, i: (lead, 0, i)
    assert R % tile == 0
    return R // tile, (tile, Cc), lambda lead, i: (lead, i, 0)


def _pair_add(name, g, r1, slots, tile, by_cols=False, side=None):
    _, R, Cc = g.shape
    steps, blk, at = _tiling(R, Cc, tile, by_cols)
    side_ins, side_outs, side_scr, side_make = side if side is not None else ((), (), (), None)
    n_si, n_so = len(side_ins), len(side_outs)

    def kern(s_ref, a_ref, b_ref, *rest):
        o_ref = rest[n_si]
        if side_make is not None:
            start, finish = side_make(rest[:n_si], rest[n_si + 1:n_si + 1 + n_so], rest[n_si + 1 + n_so:])
            j, i = pl.program_id(0), pl.program_id(1)
            pl.when(jnp.logical_and(j == 0, i == 0))(start)
        o_ref[...] = (a_ref[...].astype(F32) + b_ref[...].astype(F32)).astype(o_ref.dtype)
        if side_make is not None:
            pl.when(jnp.logical_and(j == 2, i == steps - 1))(finish)

    any_spec = pl.BlockSpec(memory_space=pl.ANY)
    res = pl.pallas_call(
        kern, name=name,
        out_shape=[jax.ShapeDtypeStruct((3, R, Cc), BF16)] + list(side_outs),
        grid_spec=pltpu.PrefetchScalarGridSpec(
            num_scalar_prefetch=1, grid=(3, steps),
            in_specs=[pl.BlockSpec((1,) + blk, lambda j, i, s: at(s[j], i)),
                      pl.BlockSpec((1,) + blk, lambda j, i, s: at(1 + j, i))] + [any_spec] * n_si,
            out_specs=[pl.BlockSpec((1,) + blk, lambda j, i, s: at(j, i))] + [any_spec] * n_so,
            scratch_shapes=list(side_scr)),
        compiler_params=_cp(("arbitrary", "arbitrary")),
    )(slots, g, r1, *side_ins)
    return res if side is not None else res[0]


def _axis_neighbours():
    x, y, c = _place()
    flip = lambda a, bit: a + bit - 2 * a * bit
    return (flip(x, c), flip(y, 1 - c), c), (flip(x, 1 - c), flip(y, c), c), c


def _exchange_first_steps(in_refs, out_refs, scratch):
    pairs = list(zip(in_refs, out_refs))
    send_sems, recv_sems = scratch
    first, _, c = _axis_neighbours()

    def copies():
        return [pltpu.make_async_remote_copy(
            src_ref=src.at[j], dst_ref=dst.at[k], send_sem=send_sems.at[t, k], recv_sem=recv_sems.at[t, k],
            device_id=first, device_id_type=MESH)
            for t, (src, dst) in enumerate(pairs) for k, j in enumerate((1 - c, 2))]

    def start():
        for cp in copies():
            cp.start()

    def finish():
        for cp in copies():
            cp.wait()

    return start, finish


def _exchange_first_side(*ss):
    n = len(ss)
    return (ss, tuple(jax.ShapeDtypeStruct((2,) + s.shape[1:], s.dtype) for s in ss),
            (pltpu.SemaphoreType.DMA((n, 2)), pltpu.SemaphoreType.DMA((n, 2))), _exchange_first_steps)


def _axis_add(name, s, t1, core, tile, by_cols=False):
    _, R, Cc = s.shape
    steps, blk, at = _tiling(R, Cc, tile, by_cols)

    def kern(c_ref, a_ref, b_ref, o_ref):
        o_ref[...] = (a_ref[...].astype(F32) + b_ref[...].astype(F32)).astype(o_ref.dtype)

    return pl.pallas_call(
        kern, name=name,
        out_shape=jax.ShapeDtypeStruct((1, R, Cc), BF16),
        grid_spec=pltpu.PrefetchScalarGridSpec(
            num_scalar_prefetch=1, grid=(steps,),
            in_specs=[pl.BlockSpec((1,) + blk, lambda i, cr: at(cr[0], i)),
                      pl.BlockSpec((1,) + blk, lambda i, cr: at(1, i))],
            out_specs=pl.BlockSpec((1,) + blk, lambda i, cr: at(0, i))),
        compiler_params=_cp(("arbitrary",)),
    )(core, s, t1)


def _exchange_second_steps(in_refs, out_refs, scratch):
    send_sems, recv_sems = scratch
    _, second, _ = _axis_neighbours()

    def copies():
        return [pltpu.make_async_remote_copy(src_ref=src, dst_ref=dst, send_sem=send_sems.at[t], recv_sem=recv_sems.at[t],
                                             device_id=second, device_id_type=MESH)
                for t, (src, dst) in enumerate(zip(in_refs, out_refs))]

    def start():
        for cp in copies():
            cp.start()

    def finish():
        for cp in copies():
            cp.wait()

    return start, finish


def _exchange_second_side(*ps):
    n = len(ps)
    return (ps, tuple(jax.ShapeDtypeStruct(p.shape, p.dtype) for p in ps),
            (pltpu.SemaphoreType.DMA((n,)), pltpu.SemaphoreType.DMA((n,))), _exchange_second_steps)


def _adamw(name, w, m, v, parts, tile, by_cols=False):
    R, Cc = w.shape
    steps, blk_shape, at = _tiling(R, Cc, tile, by_cols)
    n_parts = len(parts)

    def kern(*refs):
        w_ref, m_ref, v_ref = refs[:3]
        g = None
        for r_ref, (_, n) in zip(refs[3:3 + n_parts], parts):
            for s in range(n):
                term = r_ref[s].astype(F32)
                g = term if g is None else g + term
        g_out, d_out, m_out, v_out = refs[3 + n_parts:]
        mn = ADAM_B1 * m_ref[...] + (1.0 - ADAM_B1) * g
        vn = ADAM_B2 * v_ref[...] + (1.0 - ADAM_B2) * (g * g)
        m_hat = mn / (1.0 - ADAM_B1 ** ADAM_STEP)
        v_hat = vn / (1.0 - ADAM_B2 ** ADAM_STEP)
        g_out[...] = g
        d_out[...] = -ADAM_LR * (m_hat / (jnp.sqrt(v_hat) + ADAM_EPS) + ADAM_WD * w_ref[...])
        m_out[...] = mn
        v_out[...] = vn

    blk = pl.BlockSpec(blk_shape, lambda i: at(0, i)[1:])
    return pl.pallas_call(
        kern, name=name,
        out_shape=[jax.ShapeDtypeStruct((R, Cc), F32)] * 4,
        grid=(steps,),
        in_specs=[blk] * 3 + [pl.BlockSpec((n,) + blk_shape, lambda i: at(0, i)) for (_, n) in parts],
        out_specs=[blk] * 4,
        compiler_params=_cp(("arbitrary",)),
    )(w, m, v, *[a for (a, _) in parts])


def _assemble_columns(blocks, pieces, zeros, width):
    _, R, Cc = blocks.shape
    tr = min(256, R)

    def kern(b_ref, o_ref):
        for col, n in zeros:
            o_ref[:, col:col + n] = jnp.zeros((tr, n), o_ref.dtype)
        for col, d, lo, n in pieces:
            o_ref[:, col:col + n] = b_ref[d, :, lo:lo + n]

    return pl.pallas_call(
        kern, name="assemble_w_in",
        out_shape=jax.ShapeDtypeStruct((R, width), blocks.dtype),
        grid=(R // tr,),
        in_specs=[pl.BlockSpec((N_DEV, tr, Cc), lambda i: (0, i, 0))],
        out_specs=pl.BlockSpec((tr, width), lambda i: (i, 0)),
        compiler_params=_cp(("parallel",)),
    )(blocks)


def _split_rows(x, pieces, rows, side):
    _, Cc = x.shape
    tc = min(256, Cc)
    side_ins, side_outs, side_scr, side_make = side
    n_si, n_so = len(side_ins), len(side_outs)

    def kern(x_ref, *rest):
        o_ref = rest[n_si]
        start, finish = side_make(rest[:n_si], rest[n_si + 1:n_si + 1 + n_so], rest[n_si + 1 + n_so:])
        pl.when(pl.program_id(0) == 0)(start)
        for d, lo, row, n in pieces:
            o_ref[d, lo:lo + n, :] = x_ref[row:row + n, :]
        pl.when(pl.program_id(0) == Cc // tc - 1)(finish)

    any_spec = pl.BlockSpec(memory_space=pl.ANY)
    return pl.pallas_call(
        kern, name="split_w_in_grad",
        out_shape=[jax.ShapeDtypeStruct((N_DEV, rows, Cc), x.dtype)] + list(side_outs),
        grid=(Cc // tc,),
        in_specs=[pl.BlockSpec((x.shape[0], tc), lambda i: (0, i))] + [any_spec] * n_si,
        out_specs=[pl.BlockSpec((N_DEV, rows, tc), lambda i: (0, 0, i))] + [any_spec] * n_so,
        scratch_shapes=list(side_scr),
        compiler_params=_cp(("arbitrary",)),
    )(x, *side_ins)


def _pad_cols(a, w):
    return jnp.pad(a, ((0, 0), (0, w - a.shape[1])))


def _pad_rows(a, r):
    return jnp.pad(a, ((0, r - a.shape[0]), (0, 0)))


def _pack_b(pf, pr, wo, w2, a2, rows):
    body = jnp.concatenate([pf, pr, wo.reshape(2048, 256), jnp.concatenate([w2, a2], axis=1)], axis=0)
    return _pad_rows(body, rows)


def kernel(x, norm_gain, w_in, fox_forget_bias, rwkv_shift_mix, rwkv_w0, rwkv_w2, rwkv_a0, rwkv_a2, rwkv_k_k, rwkv_k_a, rwkv_r_k, rwkv_ln_w, rwkv_ln_b, w_proj_fox, w_proj_rwkv, w_out, final_norm_gain, loss_target, m_norm_gain, m_w_in, m_fox_forget_bias, m_rwkv_shift_mix, m_rwkv_w0, m_rwkv_w2, m_rwkv_a0, m_rwkv_a2, m_rwkv_k_k, m_rwkv_k_a, m_rwkv_r_k, m_rwkv_ln_w, m_rwkv_ln_b, m_w_proj_fox, m_w_proj_rwkv, m_w_out, m_final_norm_gain, v_norm_gain, v_w_in, v_fox_forget_bias, v_rwkv_shift_mix, v_rwkv_w0, v_rwkv_w2, v_rwkv_a0, v_rwkv_a2, v_rwkv_k_k, v_rwkv_k_a, v_rwkv_r_k, v_rwkv_ln_w, v_rwkv_ln_b, v_w_proj_fox, v_w_proj_rwkv, v_w_out, v_final_norm_gain):
    T, D = x.shape[1], x.shape[2]
    assert D == 2048 and T % LANE == 0
    NI = w_in.shape[2]
    IN = N_DEV * NI
    RB = 4224
    x2 = x[0]
    lt2 = loss_target[0]
    me = _slot(_place())

    tb = min(256, T)
    tbh = min(128, T)
    h, wa = _rowcall("rms_fwd", T, tb, [_rows(x2, tb, D), _whole(norm_gain)],
                     lambda i, xv, g: ([_rms_math(xv, g)], []), [(D, BF16)],
                     side=_allgather_side(w_in[0].astype(BF16), relay=True))
    packed_own = _pack_b(w_proj_fox[0], w_proj_rwkv[0], w_out[0], rwkv_w2[0], rwkv_a2[0], RB).astype(BF16)
    sections = [(0, 4096, 0), (4104, 4096, 4096), (8392, 4096, 8192), (4096, 8, 12288), (8200, 96, 12544), (8296, 96, 12672)]
    NP = 12800
    pieces, zeros, at_col = [], [], 0
    for lo, width, pad_lo in sections:
        if pad_lo > at_col:
            zeros.append((at_col, pad_lo - at_col))
        col = lo
        while col < lo + width:
            d = col // NI
            stop = min(lo + width, (d + 1) * NI)
            pieces.append((pad_lo + col - lo, d, col - d * NI, stop - col))
            col = stop
        at_col = pad_lo + width
    zeros.append((at_col, NP - at_col))
    w_pad = _assemble_columns(wa, pieces, zeros, NP)
    F_CB, LORA_CB = 96, 49

    mu = rwkv_shift_mix
    mu_main = mu[:, 0:4096]
    mu_lora = jnp.concatenate([_pad_cols(mu[:, 4096:4192], LANE), _pad_cols(mu[:, 4192:4288], LANE)], axis=1)
    bias_pad = _pad_cols(fox_forget_bias, LANE)
    rk_flat = rwkv_r_k.reshape(1, 1024)
    gf = final_norm_gain.reshape(1, D)

    u, wb = _mm(h, w_pad, tm=1024, tn=1280, name="mm_in", side=_allgather_side(packed_own))
    wpf = wb[:, 0:1024, :].transpose(1, 0, 2).reshape(1024, D)
    wpr = wb[:, 1024:2048, :].transpose(1, 0, 2).reshape(1024, D)
    wo = wb[:, 2048:4096, :].reshape(N_DEV * 256, D)
    w2p = _pad_rows(wb[:, 4096:4192, 0:128].transpose(1, 0, 2).reshape(96, 1024).astype(F32), LANE)
    a2p = _pad_rows(wb[:, 4096:4192, 128:256].transpose(1, 0, 2).reshape(96, 1024).astype(F32), LANE)

    c8 = _gates_fwd(u, bias_pad, T, F_CB)
    c3 = c8.reshape(8, 1, T)
    o_raw, o_a, lse = _attn_fwd(u, c3, T)

    def shift_body(i, um, hm, ul, hl, mm_, ml):
        outs = []
        for uv, hv, mv in ((um, hm, mm_), (ul, hl, ml)):
            hv = jnp.where(i == 0, 0.0, hv)
            prev = pltpu.roll(jnp.concatenate([hv, uv], axis=0), 1, 0)[SUB:]
            outs.append(uv + (prev - uv) * mv)
        return outs, []

    def halo_prev(arr, w, cb):
        return (arr, (SUB, w), lambda i: (jnp.maximum(i * (tbh // SUB) - 1, 0), cb))

    xs, xl = _rowcall("rwkv_shift_fwd", T, tbh,
                      [_rows(u, tbh, 4096, 1), halo_prev(u, 4096, 1), _rows(u, tbh, 256, LORA_CB), halo_prev(u, 256, LORA_CB),
                       _whole(mu_main), _whole(mu_lora)],
                      shift_body, [(4096, F32), (256, F32)])

    prep_par = [_whole(rwkv_w0), _whole(rwkv_a0), _whole(rwkv_k_k), _whole(rwkv_k_a), _whole(w2p), _whole(a2p)]
    prep_rows = [_rows(xs, tbh, 1024, 1), _rows(xl, tbh, LANE, 0), _rows(xl, tbh, LANE, 1)]
    lw, kp, an, bb = _rowcall("rwkv_prep_fwd", T, tbh, prep_rows + prep_par,
                              lambda i, *a: (list(_prep_math(*a)), []), [(1024, F32)] * 4)
    y, states, invs = _scan_fwd(xs, lw, kp, an, bb, T)
    post_rows = [_rows(y, tbh, 1024), _rows(xs, tbh, 1024, 0), _rows(kp, tbh, 1024), _rows(xs, tbh, 1024, 2), _rows(xs, tbh, 1024, 3)]
    post_par = [_whole(rwkv_ln_w), _whole(rwkv_ln_b), _whole(rk_flat)]
    (o_b,) = _rowcall("rwkv_post_fwd", T, tbh, post_rows + post_par,
                      lambda i, *a: ([_post_math(*a)], []), [(1024, BF16)])

    pa = _mm(o_a, wpf, out_dtype=BF16, name="mm_proj_fox")
    pb = _mm(o_b, wpr, out_dtype=BF16, name="mm_proj_rwkv")
    merge_rows = [_rows(u, tb, D, 4), _rows(u, tb, D, 5), _rows(pa, tb, D), _rows(pb, tb, D)]
    (mg,) = _rowcall("merge_fwd", T, tb, merge_rows, lambda i, *a: ([_merge_math(*a)], []), [(D, BF16)])
    mo = _mm(mg, wo, name="mm_out")

    def head_body(i, xv, mov, ltv, g):
        out = xv + mov
        r = lax.rsqrt(jnp.mean(out * out, axis=-1, keepdims=True) + RMS_EPS)
        yn = out * r
        err = yn * g - ltv
        loss = 0.5 * jnp.sum(jnp.sum(err * err, axis=-1, keepdims=True), axis=0, keepdims=True) / D
        dyv = err / D
        dyn = dyv * g
        dout = r * (dyn - yn * jnp.mean(dyn * yn, axis=-1, keepdims=True))
        return [dout], [loss, jnp.sum(dyv * yn, axis=0, keepdims=True)]

    dout, loss_p, dgf_p = _rowcall("loss_head", T, tb, [_rows(x2, tb, D), _rows(mo, tb, D), _rows(lt2, tb, D), _whole(gf)],
                                   head_body, [(D, F32)], [(1, 1), (1, D)])

    dm = _mm(dout, wo, tb=True, out_dtype=BF16, name="mm_out_dx")
    dwo = _mm(mg, dout, ta=True, out_dtype=BF16, name="mm_out_dw")

    def merge_bwd_body(i, ga, gb, pav, pbv, dmv):
        _, vjp = jax.vjp(_merge_math, ga, gb, pav.astype(F32), pbv.astype(F32))
        dga, dgb, dpa, dpb = vjp(dmv.astype(F32))
        return [dga, dgb, dpa, dpb], []

    dga, dgb, dpa, dpb = _rowcall("merge_bwd", T, tb, merge_rows + [_rows(dm, tb, D)], merge_bwd_body,
                                  [(D, BF16), (D, BF16), (D, BF16), (D, BF16)])
    doa = _mm(dpa, wpf, tb=True, name="mm_proj_fox_dx")
    dwpf = _mm(o_a, dpa, ta=True, out_dtype=BF16, name="mm_proj_fox_dw")
    dob = _mm(dpb, wpr, tb=True, name="mm_proj_rwkv_dx")
    dwpr = _mm(o_b, dpb, ta=True, out_dtype=BF16, name="mm_proj_rwkv_dw")

    do_b, dza, delta128 = _rowcall("fox_attn_pre", T, tb, [_rows(doa, tb, 1024), _rows(u, tb, 1024, 3), _rows(o_raw, tb, 1024)],
                                   lambda i, *a: (list(_attn_pre_math(*a)), []), [(1024, BF16), (1024, F32), (LANE, F32)])
    delta = delta128[:, 0:8].T.reshape(8, T, 1)
    dq, dk, dv, dc3 = _attn_bwd(u, c3, lse, do_b, delta, T)
    dfl, dbias_p = _gates_bwd(dc3.reshape(8, T), u, bias_pad, T, F_CB)

    def post_bwd_body(i, yv, rv, kpv, vv, zv, lnw, lnb, rkv, dobv):
        _, vjp = jax.vjp(_post_math, yv, rv, kpv, vv, zv, lnw, lnb, rkv)
        dy_, dr_, dkp_, dv_, dz_, dlnw, dlnb, drk = vjp(dobv)
        return [dy_, dr_, dkp_, dv_, dz_], [dlnw, dlnb, drk]

    dy_s, dr_p, dkp_p, dv_p, dzb, dlnw_p, dlnb_p, drk_p = _rowcall(
        "rwkv_post_bwd", T, tbh, post_rows + post_par + [_rows(dob, tbh, 1024)], post_bwd_body,
        [(1024, F32)] * 5, [(1, 1024)] * 3)
    dr_s, dlw, dkp_s, dv_s, dan, dbb = _scan_bwd(xs, lw, kp, an, bb, states, invs, dy_s, T)

    def prep_bwd_body(i, xk, xwd, xad, w0, a0, kkw, kaw, w2v, a2v, dlw_, dkp1, dkp2, dan_, dbb_, dr1, dr2, dv1, dv2, dz_):
        _, vjp = jax.vjp(_prep_math, xk, xwd, xad, w0, a0, kkw, kaw, w2v, a2v)
        dxk, dxwd, dxad, dw0, da0, dkk, dka, dw2, da2 = vjp((dlw_, dkp1 + dkp2, dan_, dbb_))
        return [[dr1 + dr2, dxk, dv1 + dv2, dz_], [dxwd, dxad]], [dw0, da0, dkk, dka, dw2, da2]

    cots = [dlw, dkp_s, dkp_p, dan, dbb, dr_s, dr_p, dv_s, dv_p, dzb]
    dxs, dxl, dw0_p, da0_p, dkk_p, dka_p, dw2_p, da2_p = _rowcall(
        "rwkv_prep_bwd", T, tbh, prep_rows + prep_par + [_rows(c_, tbh, 1024) for c_ in cots], prep_bwd_body,
        [(4096, F32), (256, F32)], [(1, 1024)] * 4 + [(LANE, 1024)] * 2)

    def shift_bwd_body(i, dm_, hm, dl_, hl, um, pm, ul, pl_, mm_, ml):
        last = i == T // tbh - 1
        outs, accs = [], []
        for dv_, hv, uv, pv, mv in ((dm_, hm, um, pm, mm_), (dl_, hl, ul, pl_, ml)):
            hv = jnp.where(last, 0.0, hv)
            nxt = pltpu.roll(jnp.concatenate([dv_, hv], axis=0), tbh + SUB - 1, 0)[:tbh]
            pv = jnp.where(i == 0, 0.0, pv)
            prev = pltpu.roll(jnp.concatenate([pv, uv], axis=0), 1, 0)[SUB:]
            outs.append(dv_ * (1.0 - mv) + nxt * mv)
            accs.append(jnp.sum(dv_ * (prev - uv), axis=0, keepdims=True))
        return outs, accs

    def halo_next(arr, w, cb):
        last_blk = T // SUB - 1
        return (arr, (SUB, w), lambda i: (jnp.minimum((i + 1) * (tbh // SUB), last_blk), cb))

    du_b, du_l, dmu_main_p, dmu_lora_p = _rowcall(
        "rwkv_shift_bwd", T, tbh,
        [_rows(dxs, tbh, 4096), halo_next(dxs, 4096, 0), _rows(dxl, tbh, 256), halo_next(dxl, 256, 0),
         _rows(u, tbh, 4096, 1), halo_prev(u, 4096, 1), _rows(u, tbh, 256, LORA_CB), halo_prev(u, 256, LORA_CB),
         _whole(mu_main), _whole(mu_lora)],
        shift_bwd_body, [(4096, BF16), (256, BF16)], [(1, 4096), (1, 256)])

    lora_g = jnp.concatenate([dw2_p[:96].reshape(96, N_DEV, 128).transpose(1, 0, 2),
                              da2_p[:96].reshape(96, N_DEV, 128).transpose(1, 0, 2)], axis=2).astype(BF16)
    gb = jnp.concatenate([dwpf.reshape(1024, N_DEV, 256).transpose(1, 0, 2),
                          dwpr.reshape(1024, N_DEV, 256).transpose(1, 0, 2),
                          dwo.reshape(N_DEV, 2048, 256), lora_g, jnp.zeros((N_DEV, RB - 4192, 256), BF16)], axis=1)
    xx, yy, cc = _place()
    chip_slots = jnp.stack([_slot((*chip, cc)) for chip in _other_chips(xx, yy)]).astype(jnp.int32)
    core = jnp.stack([cc]).astype(jnp.int32)
    du, r1b = _rowcall("assemble_du", T, tb,
                       [_rows(a_, tb, a_.shape[1]) for a_ in (dq, dk, dv, dza, du_b, dga, dgb, dfl, du_l)],
                       lambda i, *a: ([list(a[:8]) + [jnp.zeros((tb, LANE), BF16), a[8]]], []), [(NP, BF16)],
                       side=_exchange_pair_side(gb))
    sb = _pair_add("pair_add_packed", gb, r1b, chip_slots, RB // 3)
    dw_pad_t, t1b = _mm(du, h, ta=True, out_dtype=BF16, tm=1280, tn=1024, name="mm_in_dw",
                        side=_exchange_first_side(sb))
    pb = _axis_add("axis_add_packed", sb, t1b, core, RB // 3)
    ga, t2b = _split_rows(dw_pad_t, [(d, lo, pad_col, n) for pad_col, d, lo, n in pieces], NI,
                          _exchange_second_side(pb))
    r1a = _exchange_pair(ga)
    sa = _pair_add("pair_add_w_in", ga, r1a, chip_slots, 512, by_cols=True)
    ga_own = lax.dynamic_index_in_dim(ga, me, 0, keepdims=True)
    gb_own = lax.dynamic_index_in_dim(gb, me, 0, keepdims=True)
    dh, t1a = _mm(du, w_pad, tb=True, tm=1024, tn=1024, tk=NP // 10, name="mm_in_dx", side=_exchange_first_side(sa))
    pa = _axis_add("axis_add_w_in", sa, t1a, core, 512, by_cols=True)

    def rms_bwd_body(i, xv, g, dhv, doutv):
        _, vjp = jax.vjp(_rms_math, xv, g)
        dx_, dg_ = vjp(dhv)
        return [dx_ + doutv], [dg_]

    grad_x2, dng_p, t2a = _rowcall(
        "rms_bwd", T, tb, [_rows(x2, tb, D), _whole(norm_gain), _rows(dh, tb, D), _rows(dout, tb, D)],
        rms_bwd_body, [(D, F32)], [(1, D)], side=_exchange_second_side(pa))

    w_in_outs = _adamw("adamw_w_in", w_in[0].T, m_w_in[0].T, v_w_in[0].T, [(ga_own, 1), (r1a, 1), (t1a, 1), (t2a, 1)], 256, by_cols=True)
    g_in, d_in, m_in, v_in = [o.T for o in w_in_outs]

    dmu = jnp.concatenate([dmu_main_p, dmu_lora_p[:, 0:96], dmu_lora_p[:, 128:224]], axis=1)
    small_parts = [dng_p, dbias_p[:, 0:8], dmu, dw0_p, da0_p, dkk_p, dka_p, drk_p, dlnw_p, dlnb_p, dgf_p, loss_p]
    SR = 128
    small = _pad_cols(jnp.concatenate(small_parts, axis=1), SR * LANE).reshape(SR, LANE)
    rs = _allgather(small, "allgather_small")
    pk = lambda pf, pr, wo_, w2_, a2_: _pack_b(pf[0], pr[0], wo_[0], w2_[0], a2_[0], RB)
    outs_b = _adamw("adamw_packed", pk(w_proj_fox, w_proj_rwkv, w_out, rwkv_w2, rwkv_a2),
                    pk(m_w_proj_fox, m_w_proj_rwkv, m_w_out, m_rwkv_w2, m_rwkv_a2),
                    pk(v_w_proj_fox, v_w_proj_rwkv, v_w_out, v_rwkv_w2, v_rwkv_a2), [(gb_own, 1), (r1b, 1), (t1b, 1), (t2b, 1)], RB // 3)

    def pack_small(ng, fb, sm, w0, a0, kk_, ka_, rk_, lnw, lnb, fg):
        parts = [ng, fb, sm, w0, a0, kk_, ka_, rk_.reshape(1, 1024), lnw, lnb, fg.reshape(1, D), jnp.zeros((1, 1), F32)]
        return _pad_cols(jnp.concatenate(parts, axis=1), SR * LANE).reshape(SR, LANE)

    outs_s = _adamw("adamw_small",
                    pack_small(norm_gain, fox_forget_bias, rwkv_shift_mix, rwkv_w0, rwkv_a0, rwkv_k_k, rwkv_k_a, rwkv_r_k,
                               rwkv_ln_w, rwkv_ln_b, final_norm_gain),
                    pack_small(m_norm_gain, m_fox_forget_bias, m_rwkv_shift_mix, m_rwkv_w0, m_rwkv_a0, m_rwkv_k_k, m_rwkv_k_a,
                               m_rwkv_r_k, m_rwkv_ln_w, m_rwkv_ln_b, m_final_norm_gain),
                    pack_small(v_norm_gain, v_fox_forget_bias, v_rwkv_shift_mix, v_rwkv_w0, v_rwkv_a0, v_rwkv_k_k, v_rwkv_k_a,
                               v_rwkv_r_k, v_rwkv_ln_w, v_rwkv_ln_b, v_final_norm_gain),
                    [(rs, N_DEV)], SR)

    def unpack_b(pkd):
        return dict(w_proj_fox=pkd[0:1024][None], w_proj_rwkv=pkd[1024:2048][None], w_out=pkd[2048:4096].reshape(1, 256, D),
                    rwkv_w2=pkd[4096:4192, 0:128][None], rwkv_a2=pkd[4096:4192, 128:256][None])

    def unpack_s(pkd):
        flat = pkd.reshape(1, SR * LANE)
        names = [("norm_gain", D), ("fox_forget_bias", 8), ("rwkv_shift_mix", 4288), ("rwkv_w0", 1024), ("rwkv_a0", 1024),
                 ("rwkv_k_k", 1024), ("rwkv_k_a", 1024), ("rwkv_r_k", 1024), ("rwkv_ln_w", 1024), ("rwkv_ln_b", 1024),
                 ("final_norm_gain", D), ("loss", 1)]
        out, off = {}, 0
        for nm, n in names:
            out[nm] = flat[:, off:off + n]
            off += n
        out["rwkv_r_k"] = out["rwkv_r_k"].reshape(1, 16, 64)
        out["final_norm_gain"] = out["final_norm_gain"].reshape(D)
        return out

    order = ["norm_gain", "w_in", "fox_forget_bias", "rwkv_shift_mix", "rwkv_w0", "rwkv_w2", "rwkv_a0", "rwkv_a2", "rwkv_k_k",
             "rwkv_k_a", "rwkv_r_k", "rwkv_ln_w", "rwkv_ln_b", "w_proj_fox", "w_proj_rwkv", "w_out", "final_norm_gain"]
    result = []
    loss = None
    for kind, big in enumerate((g_in, d_in, m_in, v_in)):
        d = {**unpack_b(outs_b[kind]), **unpack_s(outs_s[kind]), "w_in": big[None]}
        if kind == 0:
            loss = d["loss"].reshape(())
        result += [d[n] for n in order]
    return (loss, grad_x2[None], *result)
```

```python
import functools

import jax
import jax.numpy as jnp
from jax import lax
from jax.experimental import pallas as pl
from jax.experimental.pallas import tpu as pltpu

F32 = jnp.float32
BF16 = jnp.bfloat16
HI = lax.Precision.HIGHEST
H3 = lax.Precision.HIGH
MESH = pl.DeviceIdType.MESH

FOX_HD = 128
RW_HD = 64
RMS_EPS = 1e-6
GN_EPS = 64e-5
L2_EPS = 1e-12
ADAM_LR = 0.001
ADAM_B1 = 0.9
ADAM_B2 = 0.999
ADAM_EPS = 1e-08
ADAM_WD = 0.01
ADAM_STEP = 10

LANE = 128
SUB = 8
VMEM_LIMIT = 56 * 1024 * 1024
N_DEV = 8
CHUNK = 128
SCAN_GROUP = 1
SCAN_PAIRS = 8
PS = None
NEG = -1e30


def _scan_shape(T):
    c = min(CHUNK, T)
    return c, min(SCAN_GROUP, T // c)


def _cp(sem=None):
    return pltpu.CompilerParams(dimension_semantics=sem, vmem_limit_bytes=VMEM_LIMIT)


def _sigmoid(x):
    return jax.nn.sigmoid(x)


def _softplus(x):
    return jnp.maximum(x, 0.0) + jnp.log(1.0 + jnp.exp(-jnp.abs(x)))


def _nn(a, b, prec=None):
    return lax.dot_general(a, b, (((1,), (0,)), ((), ())), precision=prec, preferred_element_type=F32)


def _nt(a, b, prec=None):
    return lax.dot_general(a, b, (((1,), (1,)), ((), ())), precision=prec, preferred_element_type=F32)


def _tn(a, b, prec=None):
    return lax.dot_general(a, b, (((0,), (0,)), ((), ())), precision=prec, preferred_element_type=F32)


def _iota2(shape, dim):
    return lax.broadcasted_iota(jnp.int32, shape, dim)


def _seg_sum(x):
    r = _iota2((LANE, LANE), 0) // RW_HD
    c = _iota2((LANE, LANE), 1) // RW_HD
    bd = (r == c).astype(F32)
    parts = [_nn(x[:, j * LANE:(j + 1) * LANE], bd, H3) for j in range(x.shape[1] // LANE)]
    return parts[0] if len(parts) == 1 else jnp.concatenate(parts, axis=1)


def _mm(a, b, *, ta=False, tb=False, out_dtype=F32, tm=1024, tn=1024, tk=None, name, side=None):
    assert not (ta and tb)
    K, M = a.shape if ta else a.shape[::-1]
    N = b.shape[0] if tb else b.shape[1]
    tm, tn = min(tm, M), min(tn, N)
    tk = K if tk is None else tk
    nk = K // tk
    assert M % tm == 0 and N % tn == 0 and K % tk == 0
    gi, gj = M // tm, N // tn
    a_spec = pl.BlockSpec((tk, tm), lambda i, j, k: (k, i)) if ta else pl.BlockSpec((tm, tk), lambda i, j, k: (i, k))
    b_spec = pl.BlockSpec((tn, tk), lambda i, j, k: (j, k)) if tb else pl.BlockSpec((tk, tn), lambda i, j, k: (k, j))
    side_ins, side_outs, side_scr, side_make = side if side is not None else ((), (), (), None)
    n_si, n_so = len(side_ins), len(side_outs)
    n_acc = 0 if nk == 1 else 1

    def body(*refs):
        a_ref, b_ref = refs[:2]
        o_ref = refs[2 + n_si]
        scr = refs[3 + n_si + n_so:]
        k = pl.program_id(2)
        if side_make is not None:
            start, finish = side_make(refs[2:2 + n_si], refs[3 + n_si:3 + n_si + n_so], scr[n_acc:])
            first = jnp.logical_and(jnp.logical_and(pl.program_id(0) == 0, pl.program_id(1) == 0), k == 0)
            last = jnp.logical_and(jnp.logical_and(pl.program_id(0) == gi - 1, pl.program_id(1) == gj - 1), k == nk - 1)
            pl.when(first)(start)
        av = a_ref[...].astype(BF16)
        bv = b_ref[...].astype(BF16)
        p = _tn(av, bv) if ta else _nt(av, bv) if tb else _nn(av, bv)
        if nk == 1:
            o_ref[...] = p.astype(out_dtype)
        else:
            acc_ref = scr[0]

            @pl.when(k == 0)
            def _():
                acc_ref[...] = p

            @pl.when(k > 0)
            def _():
                acc_ref[...] += p

            @pl.when(k == nk - 1)
            def _():
                o_ref[...] = acc_ref[...].astype(out_dtype)
        if side_make is not None:
            pl.when(last)(finish)

    any_spec = pl.BlockSpec(memory_space=pl.ANY)
    res = pl.pallas_call(
        body, name=name,
        out_shape=[jax.ShapeDtypeStruct((M, N), out_dtype)] + list(side_outs),
        grid=(gi, gj, nk),
        in_specs=[a_spec, b_spec] + [any_spec] * n_si,
        out_specs=[pl.BlockSpec((tm, tn), lambda i, j, k: (i, j))] + [any_spec] * n_so,
        scratch_shapes=([] if nk == 1 else [pltpu.VMEM((tm, tn), F32)]) + list(side_scr),
        compiler_params=_cp(("arbitrary",) * 3 if side is not None else ("parallel", "parallel", "arbitrary")),
    )(a, b, *side_ins)
    return res if side is not None else res[0]


def _rows(arr, tb, w, cb=0):
    return (arr, (tb, w), lambda i: (i, cb))


def _whole(arr):
    nd = arr.ndim
    return (arr, arr.shape, lambda i: (0,) * nd)


def _rowcall(name, T, tb, ins, body, outs, accs=(), side=None):
    n_in, n_out, n_acc = len(ins), len(outs), len(accs)
    side_ins, side_outs, side_scr, side_make = side if side is not None else ((), (), (), None)
    n_si = len(side_ins)

    def kern(*refs):
        i = pl.program_id(0)
        if side_make is not None:
            side_refs = refs[n_in + n_si + n_out + n_acc:]
            start, finish = side_make(refs[n_in:n_in + n_si], side_refs[:len(side_outs)], side_refs[len(side_outs):])
            pl.when(i == 0)(start)
            refs = refs[:n_in] + refs[n_in + n_si:]
        vals = [r[...] for r in refs[:n_in]]
        ro, ao = body(i, *vals)
        for r, v in zip(refs[n_in:n_in + n_out], ro):
            if isinstance(v, (list, tuple)):
                off = 0
                for piece in v:
                    w = piece.shape[1]
                    r[:, off:off + w] = piece.astype(r.dtype)
                    off += w
            else:
                r[...] = v.astype(r.dtype)
        if accs:
            acc_refs = refs[n_in + n_out:n_in + n_out + n_acc]

            @pl.when(i == 0)
            def _():
                for r in acc_refs:
                    r[...] = jnp.zeros(r.shape, F32)

            for r, v in zip(acc_refs, ao):
                r[...] += v
        if side_make is not None:
            pl.when(i == T // tb - 1)(finish)

    any_spec = pl.BlockSpec(memory_space=pl.ANY)
    out_shape = [jax.ShapeDtypeStruct((T, w), dt) for (w, dt) in outs] + [jax.ShapeDtypeStruct(s, F32) for s in accs]
    out_specs = [pl.BlockSpec((tb, w), lambda i: (i, 0)) for (w, dt) in outs] + [pl.BlockSpec(s, lambda i: (0, 0)) for s in accs]
    res = pl.pallas_call(
        kern, name=name,
        out_shape=out_shape + list(side_outs),
        grid=(T // tb,),
        in_specs=[pl.BlockSpec(bs, im) for (_, bs, im) in ins] + [any_spec] * n_si,
        out_specs=out_specs + [any_spec] * len(side_outs),
        scratch_shapes=list(side_scr),
        compiler_params=_cp(("arbitrary",)),
    )(*[a for (a, _, _) in ins], *side_ins)
    return res


def _rms_math(x, g):
    r = lax.rsqrt(jnp.mean(x * x, axis=-1, keepdims=True) + RMS_EPS)
    return x * r * g


def _merge_math(ga, gb, pa, pb):
    return _sigmoid(ga) * pa + _sigmoid(gb) * pb


def _prep_math(xk, xwd, xad, w0, a0, kk_w, ka_w, w2p, a2p):
    z = w0 + _nn(jnp.tanh(xwd), w2p, H3)
    w = -_softplus(-z) - 0.5
    lw = -jnp.exp(w)
    ag = _sigmoid(a0 + _nn(xad, a2p, H3))
    p = xk * kk_w
    n = jnp.maximum(jnp.sqrt(_seg_sum(p * p)), L2_EPS)
    kk = p / n
    kp = xk * (1.0 + (ag - 1.0) * ka_w)
    return lw, kp, -kk, kk * ag


def _post_math(y, r, kp, v, z, lnw, lnb, rk):
    inv = 1.0 / RW_HD
    mu = _seg_sum(y) * inv
    d = y - mu
    var = _seg_sum(d * d) * inv
    yn = d * lax.rsqrt(var + GN_EPS) * lnw + lnb
    bonus = _seg_sum(r * kp * rk) * v
    return (yn + bonus) * (z * _sigmoid(z))


def _neumann(ms, depth):
    eye = (_iota2(ms[0].shape, 0) == _iota2(ms[0].shape, 1)).astype(F32)
    width = ms[0].shape[1]
    mp = [_nn(m, m, PS) for m in ms]
    inv = [eye + m for m in ms]
    n = 2
    while n < depth:
        last = 2 * n >= depth
        for i in range(len(ms)):
            if last:
                inv[i] = inv[i] + _nn(mp[i], inv[i], PS)
            else:
                z = _nn(mp[i], jnp.concatenate([mp[i], inv[i]], axis=1), PS)
                mp[i], inv[i] = z[:, :width], inv[i] + z[:, width:]
        n *= 2
    return inv


@functools.partial(jax.custom_vjp, nondiff_argnums=(1,))
def _unit_inverses(ms, depth):
    if depth <= RW_HD:
        return tuple(_neumann(list(ms), depth))
    h, n = depth // 2, len(ms)
    cat = jnp.concatenate
    z = jnp.zeros((h, h), F32)

    def heads(m, r, c):
        b0 = m[r * h:(r + 1) * h, c * h:(c + 1) * h]
        b1 = m[depth + r * h:depth + (r + 1) * h, depth + c * h:depth + (c + 1) * h]
        return cat([cat([b0, z], axis=1), cat([z, b1], axis=1)], axis=0)

    diag = _neumann([heads(m, 0, 0) for m in ms] + [heads(m, 1, 1) for m in ms], h)
    ta, td = diag[:n], diag[n:]
    low = [_nn(heads(m, 1, 0), a, PS) for m, a in zip(ms, ta)]
    low = [_nn(d, x, PS) for d, x in zip(td, low)]
    out = []
    for a, x, d in zip(ta, low, td):
        rows = []
        for hd in (0, 1):
            sl = slice(hd * h, (hd + 1) * h)
            top, bot = [a[sl, sl], z], [x[sl, sl], d[sl, sl]]
            pad = [z, z]
            rows.append(cat(top + pad if hd == 0 else pad + top, axis=1))
            rows.append(cat(bot + pad if hd == 0 else pad + bot, axis=1))
        out.append(cat(rows, axis=0))
    return tuple(out)


def _unit_inverses_fwd(ms, depth):
    inv = _unit_inverses(ms, depth)
    return inv, inv


def _unit_inverses_bwd(depth, inv, cts):
    left = [_tn(t, g, PS) for t, g in zip(inv, cts)]
    return (tuple(_nt(l, t, PS) for l, t in zip(left, inv)),)


_unit_inverses.defvjp(_unit_inverses_fwd, _unit_inverses_bwd)


@jax.custom_vjp
def _known_inverses(ms, inv):
    return inv


def _known_inverses_fwd(ms, inv):
    return inv, inv


def _known_inverses_bwd(inv, cts):
    dms = _unit_inverses_bwd(None, inv, cts)[0]
    return dms, tuple(jnp.zeros_like(t) for t in inv)


_known_inverses.defvjp(_known_inverses_fwd, _known_inverses_bwd)


def _scan_group(s0s, *flat, known_inv=None, with_inv=False):
    P = len(s0s)
    G = len(flat) // (6 * P)
    ch = [flat[6 * i:6 * i + 6] for i in range(P * G)]
    C = ch[0][0].shape[0]
    C2 = 2 * C
    cat = jnp.concatenate
    m0 = _iota2((1, LANE), 1) < RW_HD
    mask0 = m0.astype(F32)
    mask1 = 1.0 - mask0
    r2 = _iota2((C2, C2), 0)
    c2 = _iota2((C2, C2), 1)
    dist = r2 - c2
    in_head = dist <= r2 % C
    lower = (_iota2((C, C), 0) >= _iota2((C, C), 1)).astype(F32)
    bd = (_iota2((LANE, LANE), 0) // RW_HD) == (_iota2((LANE, LANE), 1) // RW_HD)

    def tri(m, strict):
        return jnp.where(dist > 0 if strict else dist >= 0, jnp.where(in_head, m, 0.0), 0.0)

    def sel(z):
        return jnp.where(m0, z[:C], z[C:])

    gs = [_nn(lower, c[1], HI) for c in ch]
    pre = []
    for (r, lw, k, v, a, b), g in zip(ch, gs):
        g_end = jnp.sum(lw, axis=0, keepdims=True)
        gm = g - jnp.sum(lw[:C // 2], axis=0, keepdims=True)
        en = jnp.exp(-gm)
        ec = jnp.exp(g_end - g)
        pre.append(dict(at=a * jnp.exp(g - lw), rt=r * jnp.exp(g), am=a * jnp.exp(gm - lw), rm=r * jnp.exp(gm),
                        bt=b * en, kt=k * en, bh=b * ec, kh=k * ec, dec=jnp.exp(g_end), v=v))
    grams = [_nt(cat([p["am"] * mask0, p["am"] * mask1, p["rm"] * mask0, p["rm"] * mask1], axis=0),
                 cat([p["bt"], p["bt"], p["kt"], p["kt"]], axis=0), PS) for p in pre]
    mab = tuple(tri(gm[:C2, :C2], True) for gm in grams)
    tinv = _unit_inverses(mab, C) if known_inv is None else _known_inverses(mab, known_inv)
    xv =[sel(_nn(tri(gm[:C2, C2:], True), cat([p["v"], p["v"]], axis=0), PS)) for gm, p in zip(grams, pre)]
    ys, s = [None] * (P * G), list(s0s)
    for i in range(G):
        for q in range(P):
            n = q * G + i
            p, gm = pre[n], grams[n]
            sx = _nt(cat([p["at"], p["rt"]], axis=0), s[q], PS)
            x = sx[:C] + xv[n]
            u = sel(_nn(tinv[n], cat([x, x], axis=0), PS))
            v = p["v"]
            ys[n] = sx[C:] + sel(_nn(cat([tri(gm[C2:, :C2], False), tri(gm[C2:, C2:], False)], axis=1),
                                     cat([u, u, v, v], axis=0), PS))
            s[q] = s[q] * p["dec"] + jnp.where(bd, _tn(cat([u, v], axis=0), cat([p["bh"], p["kh"]], axis=0), PS), 0.0)
    return (tuple(ys), tuple(s), tinv) if with_inv else (tuple(ys), tuple(s))


def _scan_fwd(xs, lw, kp, an, bb, T):
    C, G = _scan_shape(T)
    P = SCAN_PAIRS
    nc = T // (C * G)
    npair = 1024 // LANE

    def kern(r_ref, lw_ref, k_ref, v_ref, a_ref, b_ref, y_ref, st_ref, inv_ref, s_scr):
        n = pl.program_id(1)

        @pl.when(n == 0)
        def _():
            s_scr[...] = jnp.zeros(s_scr.shape, F32)

        st_ref[0] = s_scr[...]
        ins = (r_ref, lw_ref, k_ref, v_ref, a_ref, b_ref)
        ys, s1, inv = _scan_group(
            tuple(s_scr[q] for q in range(P)),
            *[ref[i * C:(i + 1) * C, q * LANE:(q + 1) * LANE] for q in range(P) for i in range(G) for ref in ins], with_inv=True)
        for q in range(P):
            for i in range(G):
                y_ref[i * C:(i + 1) * C, q * LANE:(q + 1) * LANE] = ys[q * G + i]
                inv_ref[0, 0, q * G + i] = inv[q * G + i]
            s_scr[q] = s1[q]

    def col(off):
        return pl.BlockSpec((C * G, P * LANE), lambda p, n: (n, off // P + p))

    return pl.pallas_call(
        kern, name="rwkv_scan_fwd",
        out_shape=[jax.ShapeDtypeStruct((T, 1024), F32), jax.ShapeDtypeStruct((nc, npair, LANE, LANE), F32),
                   jax.ShapeDtypeStruct((nc, npair // P, P * G, 2 * C, 2 * C), F32)],
        grid=(npair // P, nc),
        in_specs=[col(0), col(0), col(0), col(16), col(0), col(0)],
        out_specs=[col(0), pl.BlockSpec((1, P, LANE, LANE), lambda p, n: (n, p, 0, 0)),
                   pl.BlockSpec((1, 1, P * G, 2 * C, 2 * C), lambda p, n: (n, p, 0, 0, 0))],
        scratch_shapes=[pltpu.VMEM((P, LANE, LANE), F32)],
        compiler_params=_cp(("parallel", "arbitrary")),
    )(xs, lw, kp, xs, an, bb)


def _scan_bwd(xs, lw, kp, an, bb, states, invs, dy, T):
    C, G = _scan_shape(T)
    P = SCAN_PAIRS
    nc = T // (C * G)
    npair = 1024 // LANE

    def kern(r_ref, lw_ref, k_ref, v_ref, a_ref, b_ref, st_ref, inv_ref, dy_ref, dr_ref, dlw_ref, dk_ref, dv_ref, da_ref, db_ref,
             ds_scr):
        n = pl.program_id(1)

        @pl.when(n == 0)
        def _():
            ds_scr[...] = jnp.zeros(ds_scr.shape, F32)

        ins = (r_ref, lw_ref, k_ref, v_ref, a_ref, b_ref)
        units = [(q, i) for q in range(P) for i in range(G)]
        known = tuple(inv_ref[0, 0, q * G + i] for q, i in units)
        _, vjp = jax.vjp(functools.partial(_scan_group, known_inv=known), tuple(st_ref[0, q] for q in range(P)),
                         *[ref[i * C:(i + 1) * C, q * LANE:(q + 1) * LANE] for q, i in units for ref in ins])
        grads = vjp((tuple(dy_ref[i * C:(i + 1) * C, q * LANE:(q + 1) * LANE] for q, i in units),
                     tuple(ds_scr[q] for q in range(P))))
        for q in range(P):
            ds_scr[q] = grads[0][q]
        outs = (dr_ref, dlw_ref, dk_ref, dv_ref, da_ref, db_ref)
        for n_, (q, i) in enumerate(units):
            for t, ref in enumerate(outs):
                ref[i * C:(i + 1) * C, q * LANE:(q + 1) * LANE] = grads[1 + 6 * n_ + t]

    def col(off):
        return pl.BlockSpec((C * G, P * LANE), lambda p, n: (nc - 1 - n, off // P + p))

    return pl.pallas_call(
        kern, name="rwkv_scan_bwd",
        out_shape=[jax.ShapeDtypeStruct((T, 1024), F32)] * 6,
        grid=(npair // P, nc),
        in_specs=[col(0), col(0), col(0), col(16), col(0), col(0),
                  pl.BlockSpec((1, P, LANE, LANE), lambda p, n: (nc - 1 - n, p, 0, 0)),
                  pl.BlockSpec((1, 1, P * G, 2 * C, 2 * C), lambda p, n: (nc - 1 - n, p, 0, 0, 0)), col(0)],
        out_specs=[col(0)] * 6,
        scratch_shapes=[pltpu.VMEM((P, LANE, LANE), F32)],
        compiler_params=_cp(("parallel", "arbitrary")),
    )(xs, lw, kp, xs, an, bb, states, invs, dy)


def _gates_fwd(u, bias_pad, T, f_cb):
    nb = T // LANE

    def kern(f_ref, b_ref, c_ref):
        x = f_ref[...] + b_ref[...]
        lf = jnp.minimum(x, 0.0) - jnp.log(1.0 + jnp.exp(-jnp.abs(x)))
        lft = lf.T
        ut = (_iota2((LANE, LANE), 0) <= _iota2((LANE, LANE), 1)).astype(F32)
        carry = jnp.zeros((LANE, 1), F32)
        for blk in range(nb):
            seg = lft[:, blk * LANE:(blk + 1) * LANE]
            cs = _nn(seg, ut, HI) + carry
            c_ref[:, blk * LANE:(blk + 1) * LANE] = cs[:SUB, :]
            carry = carry + jnp.sum(seg, axis=1, keepdims=True)

    return pl.pallas_call(
        kern, name="fox_gates_fwd",
        out_shape=jax.ShapeDtypeStruct((SUB, T), F32),
        grid=(1,),
        in_specs=[pl.BlockSpec((T, LANE), lambda i: (0, f_cb)), pl.BlockSpec((1, LANE), lambda i: (0, 0))],
        out_specs=pl.BlockSpec((SUB, T), lambda i: (0, 0)),
        compiler_params=_cp(("arbitrary",)),
    )(u, bias_pad)


def _gates_bwd(dc, u, bias_pad, T, f_cb):
    nb = T // LANE

    def kern(dc_ref, f_ref, b_ref, dfl_ref, db_ref):
        dcv = jnp.concatenate([dc_ref[...], jnp.zeros((LANE - SUB, T), F32)], axis=0)
        lt = (_iota2((LANE, LANE), 0) >= _iota2((LANE, LANE), 1)).astype(F32)
        carry = jnp.zeros((LANE, 1), F32)
        pieces = [None] * nb
        for blk in range(nb - 1, -1, -1):
            seg = dcv[:, blk * LANE:(blk + 1) * LANE]
            pieces[blk] = _nn(seg, lt, HI) + carry
            carry = carry + jnp.sum(seg, axis=1, keepdims=True)
        dlf = (pieces[0] if nb == 1 else jnp.concatenate(pieces, axis=1)).T
        x = f_ref[...] + b_ref[...]
        dfl = dlf * _sigmoid(-x)
        dfl_ref[...] = dfl
        db_ref[...] = jnp.sum(dfl, axis=0, keepdims=True)

    return pl.pallas_call(
        kern, name="fox_gates_bwd",
        out_shape=[jax.ShapeDtypeStruct((T, LANE), F32), jax.ShapeDtypeStruct((1, LANE), F32)],
        grid=(1,),
        in_specs=[pl.BlockSpec((SUB, T), lambda i: (0, 0)), pl.BlockSpec((T, LANE), lambda i: (0, f_cb)),
                  pl.BlockSpec((1, LANE), lambda i: (0, 0))],
        out_specs=[pl.BlockSpec((T, LANE), lambda i: (0, 0)), pl.BlockSpec((1, LANE), lambda i: (0, 0))],
        compiler_params=_cp(("arbitrary",)),
    )(dc, u, bias_pad)


ATTN_HEADS = 2


def _attn_block(T):
    return 512 if T % 512 == 0 and T >= 1024 else 128


def _attn_fwd(u, c3, T):
    H, HP = 8, ATTN_HEADS
    bq = _attn_block(T)
    nq = T // bq
    scale = FOX_HD ** -0.5
    lanes = [slice(h * LANE, (h + 1) * LANE) for h in range(HP)]

    def kern(q_ref, k_ref, v_ref, z_ref, cq_ref, ck_ref, o_ref, oa_ref, lse_ref):
        i = pl.program_id(1)
        q = [(q_ref[:, ln] * scale).astype(BF16) for ln in lanes]
        c0 = [cq_ref[h][:, 0:1] for h in range(HP)]

        def step(j, carry, diagonal=False):
            off = pl.multiple_of(j * bq, bq)
            s = [_nt(q[h], k_ref[pl.ds(off, bq), lanes[h]].astype(BF16)) + (c0[h] - ck_ref[h, :, pl.ds(off, bq)])
                 for h in range(HP)]
            ps, out = [], []
            for h in range(HP):
                m, l, acc = carry[h]
                sh = s[h]
                if diagonal:
                    sh = jnp.where(_iota2((bq, bq), 1) <= _iota2((bq, bq), 0), sh, NEG)
                m_new = jnp.maximum(m, jnp.max(sh, axis=1, keepdims=True))
                p = jnp.exp(sh - m_new)
                alpha = jnp.exp(m - m_new)
                p_hi = p.astype(BF16)
                ps.append((p_hi, (p - p_hi.astype(F32)).astype(BF16)))
                out.append((m_new, alpha * l + jnp.sum(p, axis=1, keepdims=True), alpha * acc))
            res = []
            for h, (m, l, acc) in enumerate(out):
                vj = v_ref[pl.ds(off, bq), lanes[h]].astype(BF16)
                res.append((m, l, acc + _nn(ps[h][0], vj) + _nn(ps[h][1], vj)))
            return tuple(res)

        init = tuple((jnp.full((bq, 1), NEG, F32), jnp.zeros((bq, 1), F32), jnp.zeros((bq, FOX_HD), F32)) for _ in range(HP))
        res = step(i, lax.fori_loop(0, i, step, init), diagonal=True)
        for h, (m, l, acc) in enumerate(res):
            o = acc / l
            z = z_ref[:, lanes[h]]
            o_ref[:, lanes[h]] = o
            oa_ref[:, lanes[h]] = (o * z * _sigmoid(z)).astype(BF16)
            lse_ref[h] = m + jnp.log(l)

    W = HP * LANE
    return pl.pallas_call(
        kern, name="fox_attn_fwd",
        out_shape=[jax.ShapeDtypeStruct((T, 1024), F32), jax.ShapeDtypeStruct((T, 1024), BF16),
                   jax.ShapeDtypeStruct((H, T, 1), F32)],
        grid=(H // HP, nq),
        in_specs=[pl.BlockSpec((bq, W), lambda g, i: (i, g)),
                  pl.BlockSpec((T, W), lambda g, i: (0, 8 // HP + g)),
                  pl.BlockSpec((T, W), lambda g, i: (0, 16 // HP + g)),
                  pl.BlockSpec((bq, W), lambda g, i: (i, 24 // HP + g)),
                  pl.BlockSpec((HP, 1, bq), lambda g, i: (g, 0, i)),
                  pl.BlockSpec((HP, 1, T), lambda g, i: (g, 0, 0))],
        out_specs=[pl.BlockSpec((bq, W), lambda g, i: (i, g)),
                   pl.BlockSpec((bq, W), lambda g, i: (i, g)),
                   pl.BlockSpec((HP, bq, 1), lambda g, i: (g, i, 0))],
        compiler_params=_cp(("parallel", "arbitrary")),
    )(u, u, u, u, c3, c3)


def _attn_probs(s, lse_i, diagonal):
    if not diagonal:
        return jnp.exp(s - lse_i)
    keep = _iota2(s.shape, 1) <= _iota2(s.shape, 0)
    return jnp.where(keep, jnp.exp(jnp.where(keep, s, NEG) - lse_i), 0.0)


def _attn_pre_math(doa, z, o):
    sg = _sigmoid(z)
    do = (doa * z * sg).astype(BF16)
    dz = doa * o * (sg * (1.0 + z * (1.0 - sg)))
    head_of = (_iota2((o.shape[1], LANE), 0) // FOX_HD == _iota2((o.shape[1], LANE), 1)).astype(F32)
    return do, dz, _nn(do.astype(F32) * o, head_of, HI)


def _attn_bwd(u, c3, lse, do, delta, T):
    H, HP = 8, ATTN_HEADS
    bq = _attn_block(T)
    nq = T // bq
    scale = FOX_HD ** -0.5
    lanes = [slice(h * LANE, (h + 1) * LANE) for h in range(HP)]

    def kern(q_ref, k_ref, v_ref, c_ref, lse_ref, do_ref, dl_ref, dq_ref, dk_ref, dv_ref, dc_ref):
        j = pl.program_id(1)

        @pl.when(j == 0)
        def _():
            dq_ref[...] = jnp.zeros(dq_ref.shape, F32)

        kj = [k_ref[:, ln].astype(BF16) for ln in lanes]
        vj = [v_ref[:, ln].astype(BF16) for ln in lanes]
        joff = pl.multiple_of(j * bq, bq)
        ck = [c_ref[h, :, pl.ds(joff, bq)] for h in range(HP)]

        def step(i, carry, diagonal=False):
            off = pl.multiple_of(i * bq, bq)
            qs = [(q_ref[pl.ds(off, bq), ln] * scale).astype(BF16) for ln in lanes]
            dob = [do_ref[pl.ds(off, bq), ln] for ln in lanes]
            s = [_nt(qs[h], kj[h]) + (c_ref[h, :, pl.ds(off, bq)][:, 0:1] - ck[h]) for h in range(HP)]
            dp = [_nt(dob[h], vj[h]) for h in range(HP)]
            pb, dsb, dcs = [], [], []
            for h in range(HP):
                p = _attn_probs(s[h], lse_ref[h, pl.ds(off, bq), :], diagonal)
                ds = p * (dp[h] - dl_ref[h, pl.ds(off, bq), :])
                pb.append(p.astype(BF16))
                dsb.append(ds.astype(BF16))
                dcs.append(jnp.sum(ds, axis=0, keepdims=True))
            out = []
            for h, (dk, dv, dc) in enumerate(carry):
                dq_ref[pl.ds(off, bq), lanes[h]] += _nn(dsb[h], kj[h]) * scale
                out.append((dk + _tn(dsb[h], qs[h]), dv + _tn(pb[h], dob[h]), dc - dcs[h]))
            return tuple(out)

        init = tuple((jnp.zeros((bq, FOX_HD), F32), jnp.zeros((bq, FOX_HD), F32), jnp.zeros((1, bq), F32)) for _ in range(HP))
        res = lax.fori_loop(j + 1, nq, step, step(j, init, diagonal=True))
        for h, (dk, dv, dc) in enumerate(res):
            dk_ref[:, lanes[h]] = dk
            dv_ref[:, lanes[h]] = dv
            dc_ref[h] = dc

    W = HP * LANE
    full = lambda cb: pl.BlockSpec((T, W), lambda g, j: (0, cb // HP + g))
    blk = lambda cb: pl.BlockSpec((bq, W), lambda g, j: (j, cb // HP + g))
    col1 = pl.BlockSpec((HP, T, 1), lambda g, j: (g, 0, 0))
    return pl.pallas_call(
        kern, name="fox_attn_bwd",
        out_shape=[jax.ShapeDtypeStruct((T, 1024), F32)] * 3 + [jax.ShapeDtypeStruct((H, 1, T), F32)],
        grid=(H // HP, nq),
        in_specs=[full(0), blk(8), blk(16), pl.BlockSpec((HP, 1, T), lambda g, j: (g, 0, 0)), col1, full(0), col1],
        out_specs=[full(0), blk(0), blk(0), pl.BlockSpec((HP, 1, bq), lambda g, j: (g, 0, j))],
        compiler_params=_cp(("parallel", "arbitrary")),
    )(u, u, u, c3, lse, do, delta)


def _place():
    return lax.axis_index("x"), lax.axis_index("y"), lax.axis_index("c")


def _slot(p):
    return 4 * p[0] + 2 * p[1] + p[2]


def _other_chips(x, y):
    return [(1 - x, y), (x, 1 - y), (1 - x, 1 - y)]


def _allgather_steps(in_refs, out_refs, scratch):
    (src,), (dst,) = in_refs, out_refs
    send_sems, recv_sems, local_sem = scratch
    x, y, c = _place()
    me, sibling = (x, y, c), (x, y, 1 - c)
    chips = _other_chips(x, y)

    def copy(k, block, to, from_input=False):
        d = dst.at[_slot(block)]
        return pltpu.make_async_remote_copy(
            src_ref=src if from_input else d, dst_ref=d, send_sem=send_sems.at[k], recv_sem=recv_sems.at[k],
            device_id=to, device_id_type=MESH)

    def first_copies():
        return [copy(0, me, sibling, True)] + [copy(1 + j, me, (*chip, c), True) for j, chip in enumerate(chips)]

    def start():
        pltpu.make_async_copy(src, dst.at[_slot(me)], local_sem).start()
        for cp in first_copies():
            cp.start()

    def finish():
        passed = []
        for j, chip in enumerate(chips):
            copy(1 + j, (*chip, c), me).wait_recv()
            passed.append(copy(4 + j, (*chip, c), sibling))
            passed[-1].start()
        copy(0, sibling, me).wait_recv()
        for j, chip in enumerate(chips):
            copy(4 + j, (*chip, 1 - c), me).wait_recv()
        for cp in first_copies() + passed:
            cp.wait_send()
        pltpu.make_async_copy(src, dst.at[_slot(me)], local_sem).wait()

    return start, finish


def _allgather_relay_steps(in_refs, out_refs, scratch):
    (src,), (dst,) = in_refs, out_refs
    send_sems, recv_sems, local_sem = scratch
    x, y, c = _place()
    me, sibling = (x, y, c), (x, y, 1 - c)
    x_nbr, y_nbr, diag = (1 - x, y, c), (x, 1 - y, c), (1 - x, 1 - y, c)
    flip = lambda a, bit: a + bit - 2 * a * bit
    relay_from = (flip(x, 1 - c), flip(y, c), c)
    relay_to = (flip(x, c), flip(y, 1 - c), c)

    def copy(k, block, to, from_input=False):
        d = dst.at[_slot(block)]
        return pltpu.make_async_remote_copy(
            src_ref=src if from_input else d, dst_ref=d, send_sem=send_sems.at[k], recv_sem=recv_sems.at[k],
            device_id=to, device_id_type=MESH)

    def first_copies():
        return [copy(0, me, sibling, True), copy(1, me, x_nbr, True), copy(2, me, y_nbr, True)]

    def other(block):
        return block[:2] + (1 - c,)

    def start():
        pltpu.make_async_copy(src, dst.at[_slot(me)], local_sem).start()
        for cp in first_copies():
            cp.start()

    def finish():
        copy(1, x_nbr, me).wait_recv()
        copy(2, y_nbr, me).wait_recv()
        later = [copy(3, relay_from, relay_to), copy(4, x_nbr, sibling), copy(5, y_nbr, sibling)]
        for cp in later:
            cp.start()
        copy(3, diag, me).wait_recv()
        later.append(copy(6, diag, sibling))
        later[-1].start()
        copy(0, sibling, me).wait_recv()
        for k, block in ((4, x_nbr), (5, y_nbr), (6, diag)):
            copy(k, other(block), me).wait_recv()
        for cp in first_copies() + later:
            cp.wait_send()
        pltpu.make_async_copy(src, dst.at[_slot(me)], local_sem).wait()

    return start, finish


def _allgather_side(a, relay=False):
    return ((a,), (jax.ShapeDtypeStruct((N_DEV,) + a.shape, a.dtype),),
            (pltpu.SemaphoreType.DMA((7,)), pltpu.SemaphoreType.DMA((7,)), pltpu.SemaphoreType.DMA),
            _allgather_relay_steps if relay else _allgather_steps)


def _allgather(a, name, relay=False):
    ins, outs, scratch, make = _allgather_side(a, relay)

    def body(a_ref, o_ref, *scr):
        start, finish = make((a_ref,), (o_ref,), scr)
        start()
        finish()

    any_spec = pl.BlockSpec(memory_space=pl.ANY)
    return pl.pallas_call(body, name=name, out_shape=outs[0], in_specs=[any_spec], out_specs=any_spec,
                          scratch_shapes=list(scratch))(a)


def _exchange_pair_steps(in_refs, out_refs, scratch):
    (src,), (dst,) = in_refs, out_refs
    send_sems, recv_sems = scratch
    x, y, c = _place()
    sibling = (x, y, 1 - c)
    slots = [_slot(sibling)] + [_slot((*chip, 1 - c)) for chip in _other_chips(x, y)]

    def copies():
        return [pltpu.make_async_remote_copy(
            src_ref=src.at[ps], dst_ref=dst.at[k], send_sem=send_sems.at[k], recv_sem=recv_sems.at[k],
            device_id=sibling, device_id_type=MESH) for k, ps in enumerate(slots)]

    def start():
        for cp in copies():
            cp.start()

    def finish():
        for cp in copies():
            cp.wait()

    return start, finish


def _exchange_pair_side(g):
    return ((g,), (jax.ShapeDtypeStruct((4,) + g.shape[1:], g.dtype),),
            (pltpu.SemaphoreType.DMA((4,)), pltpu.SemaphoreType.DMA((4,))), _exchange_pair_steps)


def _exchange_pair(g):
    ins, outs, scratch, make = _exchange_pair_side(g)

    def body(g_ref, r_ref, *scr):
        start, finish = make((g_ref,), (r_ref,), scr)
        start()
        finish()

    any_spec = pl.BlockSpec(memory_space=pl.ANY)
    return pl.pallas_call(body, name="exchange_pair", out_shape=outs[0], in_specs=[any_spec], out_specs=any_spec,
                          scratch_shapes=list(scratch))(g)


def _tiling(R, Cc, tile, by_cols):
    if by_cols:
        assert Cc % tile == 0
        return Cc // tile, (R, tile), lambda lead, i: (lead, 0, i)
    assert R % tile == 0
    return R // tile, (tile, Cc), lambda lead, i: (lead, i, 0)


def _pair_add(name, g, r1, slots, tile, by_cols=False, side=None):
    _, R, Cc = g.shape
    steps, blk, at = _tiling(R, Cc, tile, by_cols)
    side_ins, side_outs, side_scr, side_make = side if side is not None else ((), (), (), None)
    n_si, n_so = len(side_ins), len(side_outs)

    def kern(s_ref, a_ref, b_ref, *rest):
        o_ref = rest[n_si]
        if side_make is not None:
            start, finish = side_make(rest[:n_si], rest[n_si + 1:n_si + 1 + n_so], rest[n_si + 1 + n_so:])
            j, i = pl.program_id(0), pl.program_id(1)
            pl.when(jnp.logical_and(j == 0, i == 0))(start)
        o_ref[...] = (a_ref[...].astype(F32) + b_ref[...].astype(F32)).astype(o_ref.dtype)
        if side_make is not None:
            pl.when(jnp.logical_and(j == 2, i == steps - 1))(finish)

    any_spec = pl.BlockSpec(memory_space=pl.ANY)
    res = pl.pallas_call(
        kern, name=name,
        out_shape=[jax.ShapeDtypeStruct((3, R, Cc), BF16)] + list(side_outs),
        grid_spec=pltpu.PrefetchScalarGridSpec(
            num_scalar_prefetch=1, grid=(3, steps),
            in_specs=[pl.BlockSpec((1,) + blk, lambda j, i, s: at(s[j], i)),
                      pl.BlockSpec((1,) + blk, lambda j, i, s: at(1 + j, i))] + [any_spec] * n_si,
            out_specs=[pl.BlockSpec((1,) + blk, lambda j, i, s: at(j, i))] + [any_spec] * n_so,
            scratch_shapes=list(side_scr)),
        compiler_params=_cp(("arbitrary", "arbitrary")),
    )(slots, g, r1, *side_ins)
    return res if side is not None else res[0]


def _axis_neighbours():
    x, y, c = _place()
    flip = lambda a, bit: a + bit - 2 * a * bit
    return (flip(x, c), flip(y, 1 - c), c), (flip(x, 1 - c), flip(y, c), c), c


def _exchange_first_steps(in_refs, out_refs, scratch):
    pairs = list(zip(in_refs, out_refs))
    send_sems, recv_sems = scratch
    first, _, c = _axis_neighbours()

    def copies():
        return [pltpu.make_async_remote_copy(
            src_ref=src.at[j], dst_ref=dst.at[k], send_sem=send_sems.at[t, k], recv_sem=recv_sems.at[t, k],
            device_id=first, device_id_type=MESH)
            for t, (src, dst) in enumerate(pairs) for k, j in enumerate((1 - c, 2))]

    def start():
        for cp in copies():
            cp.start()

    def finish():
        for cp in copies():
            cp.wait()

    return start, finish


def _exchange_first_side(*ss):
    n = len(ss)
    return (ss, tuple(jax.ShapeDtypeStruct((2,) + s.shape[1:], s.dtype) for s in ss),
            (pltpu.SemaphoreType.DMA((n, 2)), pltpu.SemaphoreType.DMA((n, 2))), _exchange_first_steps)


def _axis_add(name, s, t1, core, tile, by_cols=False):
    _, R, Cc = s.shape
    steps, blk, at = _tiling(R, Cc, tile, by_cols)

    def kern(c_ref, a_ref, b_ref, o_ref):
        o_ref[...] = (a_ref[...].astype(F32) + b_ref[...].astype(F32)).astype(o_ref.dtype)

    return pl.pallas_call(
        kern, name=name,
        out_shape=jax.ShapeDtypeStruct((1, R, Cc), BF16),
        grid_spec=pltpu.PrefetchScalarGridSpec(
            num_scalar_prefetch=1, grid=(steps,),
            in_specs=[pl.BlockSpec((1,) + blk, lambda i, cr: at(cr[0], i)),
                      pl.BlockSpec((1,) + blk, lambda i, cr: at(1, i))],
            out_specs=pl.BlockSpec((1,) + blk, lambda i, cr: at(0, i))),
        compiler_params=_cp(("arbitrary",)),
    )(core, s, t1)


def _exchange_second_steps(in_refs, out_refs, scratch):
    send_sems, recv_sems = scratch
    _, second, _ = _axis_neighbours()

    def copies():
        return [pltpu.make_async_remote_copy(src_ref=src, dst_ref=dst, send_sem=send_sems.at[t], recv_sem=recv_sems.at[t],
                                             device_id=second, device_id_type=MESH)
                for t, (src, dst) in enumerate(zip(in_refs, out_refs))]

    def start():
        for cp in copies():
            cp.start()

    def finish():
        for cp in copies():
            cp.wait()

    return start, finish


def _exchange_second_side(*ps):
    n = len(ps)
    return (ps, tuple(jax.ShapeDtypeStruct(p.shape, p.dtype) for p in ps),
            (pltpu.SemaphoreType.DMA((n,)), pltpu.SemaphoreType.DMA((n,))), _exchange_second_steps)


def _adamw(name, w, m, v, parts, tile, by_cols=False):
    R, Cc = w.shape
    steps, blk_shape, at = _tiling(R, Cc, tile, by_cols)
    n_parts = len(parts)

    def kern(*refs):
        w_ref, m_ref, v_ref = refs[:3]
        g = None
        for r_ref, (_, n) in zip(refs[3:3 + n_parts], parts):
            for s in range(n):
                term = r_ref[s].astype(F32)
                g = term if g is None else g + term
        g_out, d_out, m_out, v_out = refs[3 + n_parts:]
        mn = ADAM_B1 * m_ref[...] + (1.0 - ADAM_B1) * g
        vn = ADAM_B2 * v_ref[...] + (1.0 - ADAM_B2) * (g * g)
        m_hat = mn / (1.0 - ADAM_B1 ** ADAM_STEP)
        v_hat = vn / (1.0 - ADAM_B2 ** ADAM_STEP)
        g_out[...] = g
        d_out[...] = -ADAM_LR * (m_hat / (jnp.sqrt(v_hat) + ADAM_EPS) + ADAM_WD * w_ref[...])
        m_out[...] = mn
        v_out[...] = vn

    blk = pl.BlockSpec(blk_shape, lambda i: at(0, i)[1:])
    return pl.pallas_call(
        kern, name=name,
        out_shape=[jax.ShapeDtypeStruct((R, Cc), F32)] * 4,
        grid=(steps,),
        in_specs=[blk] * 3 + [pl.BlockSpec((n,) + blk_shape, lambda i: at(0, i)) for (_, n) in parts],
        out_specs=[blk] * 4,
        compiler_params=_cp(("arbitrary",)),
    )(w, m, v, *[a for (a, _) in parts])


def _assemble_columns(blocks, pieces, zeros, width):
    _, R, Cc = blocks.shape
    tr = min(256, R)

    def kern(b_ref, o_ref):
        for col, n in zeros:
            o_ref[:, col:col + n] = jnp.zeros((tr, n), o_ref.dtype)
        for col, d, lo, n in pieces:
            o_ref[:, col:col + n] = b_ref[d, :, lo:lo + n]

    return pl.pallas_call(
        kern, name="assemble_w_in",
        out_shape=jax.ShapeDtypeStruct((R, width), blocks.dtype),
        grid=(R // tr,),
        in_specs=[pl.BlockSpec((N_DEV, tr, Cc), lambda i: (0, i, 0))],
        out_specs=pl.BlockSpec((tr, width), lambda i: (i, 0)),
        compiler_params=_cp(("parallel",)),
    )(blocks)


def _split_rows(x, pieces, rows, side):
    _, Cc = x.shape
    tc = min(256, Cc)
    side_ins, side_outs, side_scr, side_make = side
    n_si, n_so = len(side_ins), len(side_outs)

    def kern(x_ref, *rest):
        o_ref = rest[n_si]
        start, finish = side_make(rest[:n_si], rest[n_si + 1:n_si + 1 + n_so], rest[n_si + 1 + n_so:])
        pl.when(pl.program_id(0) == 0)(start)
        for d, lo, row, n in pieces:
            o_ref[d, lo:lo + n, :] = x_ref[row:row + n, :]
        pl.when(pl.program_id(0) == Cc // tc - 1)(finish)

    any_spec = pl.BlockSpec(memory_space=pl.ANY)
    return pl.pallas_call(
        kern, name="split_w_in_grad",
        out_shape=[jax.ShapeDtypeStruct((N_DEV, rows, Cc), x.dtype)] + list(side_outs),
        grid=(Cc // tc,),
        in_specs=[pl.BlockSpec((x.shape[0], tc), lambda i: (0, i))] + [any_spec] * n_si,
        out_specs=[pl.BlockSpec((N_DEV, rows, tc), lambda i: (0, 0, i))] + [any_spec] * n_so,
        scratch_shapes=list(side_scr),
        compiler_params=_cp(("arbitrary",)),
    )(x, *side_ins)


def _pad_cols(a, w):
    return jnp.pad(a, ((0, 0), (0, w - a.shape[1])))


def _pad_rows(a, r):
    return jnp.pad(a, ((0, r - a.shape[0]), (0, 0)))


def _pack_b(pf, pr, wo, w2, a2, rows):
    body = jnp.concatenate([pf, pr, wo.reshape(2048, 256), jnp.concatenate([w2, a2], axis=1)], axis=0)
    return _pad_rows(body, rows)


def kernel(x, norm_gain, w_in, fox_forget_bias, rwkv_shift_mix, rwkv_w0, rwkv_w2, rwkv_a0, rwkv_a2, rwkv_k_k, rwkv_k_a, rwkv_r_k, rwkv_ln_w, rwkv_ln_b, w_proj_fox, w_proj_rwkv, w_out, final_norm_gain, loss_target, m_norm_gain, m_w_in, m_fox_forget_bias, m_rwkv_shift_mix, m_rwkv_w0, m_rwkv_w2, m_rwkv_a0, m_rwkv_a2, m_rwkv_k_k, m_rwkv_k_a, m_rwkv_r_k, m_rwkv_ln_w, m_rwkv_ln_b, m_w_proj_fox, m_w_proj_rwkv, m_w_out, m_final_norm_gain, v_norm_gain, v_w_in, v_fox_forget_bias, v_rwkv_shift_mix, v_rwkv_w0, v_rwkv_w2, v_rwkv_a0, v_rwkv_a2, v_rwkv_k_k, v_rwkv_k_a, v_rwkv_r_k, v_rwkv_ln_w, v_rwkv_ln_b, v_w_proj_fox, v_w_proj_rwkv, v_w_out, v_final_norm_gain):
    T, D = x.shape[1], x.shape[2]
    assert D == 2048 and T % LANE == 0
    NI = w_in.shape[2]
    IN = N_DEV * NI
    RB = 4224
    x2 = x[0]
    lt2 = loss_target[0]
    me = _slot(_place())

    tb = min(256, T)
    tbh = min(128, T)
    h, wa = _rowcall("rms_fwd", T, tb, [_rows(x2, tb, D), _whole(norm_gain)],
                     lambda i, xv, g: ([_rms_math(xv, g)], []), [(D, BF16)],
                     side=_allgather_side(w_in[0].astype(BF16), relay=True))
    packed_own = _pack_b(w_proj_fox[0], w_proj_rwkv[0], w_out[0], rwkv_w2[0], rwkv_a2[0], RB).astype(BF16)
    sections = [(0, 4096, 0), (4104, 4096, 4096), (8392, 4096, 8192), (4096, 8, 12288), (8200, 96, 12544), (8296, 96, 12672)]
    NP = 12800
    pieces, zeros, at_col = [], [], 0
    for lo, width, pad_lo in sections:
        if pad_lo > at_col:
            zeros.append((at_col, pad_lo - at_col))
        col = lo
        while col < lo + width:
            d = col // NI
            stop = min(lo + width, (d + 1) * NI)
            pieces.append((pad_lo + col - lo, d, col - d * NI, stop - col))
            col = stop
        at_col = pad_lo + width
    zeros.append((at_col, NP - at_col))
    w_pad = _assemble_columns(wa, pieces, zeros, NP)
    F_CB, LORA_CB = 96, 49

    mu = rwkv_shift_mix
    mu_main = mu[:, 0:4096]
    mu_lora = jnp.concatenate([_pad_cols(mu[:, 4096:4192], LANE), _pad_cols(mu[:, 4192:4288], LANE)], axis=1)
    bias_pad = _pad_cols(fox_forget_bias, LANE)
    rk_flat = rwkv_r_k.reshape(1, 1024)
    gf = final_norm_gain.reshape(1, D)

    u, wb = _mm(h, w_pad, tm=1024, tn=1280, name="mm_in", side=_allgather_side(packed_own))
    wpf = wb[:, 0:1024, :].transpose(1, 0, 2).reshape(1024, D)
    wpr = wb[:, 1024:2048, :].transpose(1, 0, 2).reshape(1024, D)
    wo = wb[:, 2048:4096, :].reshape(N_DEV * 256, D)
    w2p = _pad_rows(wb[:, 4096:4192, 0:128].transpose(1, 0, 2).reshape(96, 1024).astype(F32), LANE)
    a2p = _pad_rows(wb[:, 4096:4192, 128:256].transpose(1, 0, 2).reshape(96, 1024).astype(F32), LANE)

    c8 = _gates_fwd(u, bias_pad, T, F_CB)
    c3 = c8.reshape(8, 1, T)
    o_raw, o_a, lse = _attn_fwd(u, c3, T)

    def shift_body(i, um, hm, ul, hl, mm_, ml):
        outs = []
        for uv, hv, mv in ((um, hm, mm_), (ul, hl, ml)):
            hv = jnp.where(i == 0, 0.0, hv)
            prev = pltpu.roll(jnp.concatenate([hv, uv], axis=0), 1, 0)[SUB:]
            outs.append(uv + (prev - uv) * mv)
        return outs, []

    def halo_prev(arr, w, cb):
        return (arr, (SUB, w), lambda i: (jnp.maximum(i * (tbh // SUB) - 1, 0), cb))

    xs, xl = _rowcall("rwkv_shift_fwd", T, tbh,
                      [_rows(u, tbh, 4096, 1), halo_prev(u, 4096, 1), _rows(u, tbh, 256, LORA_CB), halo_prev(u, 256, LORA_CB),
                       _whole(mu_main), _whole(mu_lora)],
                      shift_body, [(4096, F32), (256, F32)])

    prep_par = [_whole(rwkv_w0), _whole(rwkv_a0), _whole(rwkv_k_k), _whole(rwkv_k_a), _whole(w2p), _whole(a2p)]
    prep_rows = [_rows(xs, tbh, 1024, 1), _rows(xl, tbh, LANE, 0), _rows(xl, tbh, LANE, 1)]
    lw, kp, an, bb = _rowcall("rwkv_prep_fwd", T, tbh, prep_rows + prep_par,
                              lambda i, *a: (list(_prep_math(*a)), []), [(1024, F32)] * 4)
    y, states, invs = _scan_fwd(xs, lw, kp, an, bb, T)
    post_rows = [_rows(y, tbh, 1024), _rows(xs, tbh, 1024, 0), _rows(kp, tbh, 1024), _rows(xs, tbh, 1024, 2), _rows(xs, tbh, 1024, 3)]
    post_par = [_whole(rwkv_ln_w), _whole(rwkv_ln_b), _whole(rk_flat)]
    (o_b,) = _rowcall("rwkv_post_fwd", T, tbh, post_rows + post_par,
                      lambda i, *a: ([_post_math(*a)], []), [(1024, BF16)])

    pa = _mm(o_a, wpf, out_dtype=BF16, name="mm_proj_fox")
    pb = _mm(o_b, wpr, out_dtype=BF16, name="mm_proj_rwkv")
    merge_rows = [_rows(u, tb, D, 4), _rows(u, tb, D, 5), _rows(pa, tb, D), _rows(pb, tb, D)]
    (mg,) = _rowcall("merge_fwd", T, tb, merge_rows, lambda i, *a: ([_merge_math(*a)], []), [(D, BF16)])
    mo = _mm(mg, wo, name="mm_out")

    def head_body(i, xv, mov, ltv, g):
        out = xv + mov
        r = lax.rsqrt(jnp.mean(out * out, axis=-1, keepdims=True) + RMS_EPS)
        yn = out * r
        err = yn * g - ltv
        loss = 0.5 * jnp.sum(jnp.sum(err * err, axis=-1, keepdims=True), axis=0, keepdims=True) / D
        dyv = err / D
        dyn = dyv * g
        dout = r * (dyn - yn * jnp.mean(dyn * yn, axis=-1, keepdims=True))
        return [dout], [loss, jnp.sum(dyv * yn, axis=0, keepdims=True)]

    dout, loss_p, dgf_p = _rowcall("loss_head", T, tb, [_rows(x2, tb, D), _rows(mo, tb, D), _rows(lt2, tb, D), _whole(gf)],
                                   head_body, [(D, F32)], [(1, 1), (1, D)])

    dm = _mm(dout, wo, tb=True, name="mm_out_dx")
    dwo = _mm(mg, dout, ta=True, out_dtype=BF16, name="mm_out_dw")

    def merge_bwd_body(i, ga, gb, pav, pbv, dmv):
        _, vjp = jax.vjp(_merge_math, ga, gb, pav.astype(F32), pbv.astype(F32))
        dga, dgb, dpa, dpb = vjp(dmv)
        return [dga, dgb, dpa, dpb], []

    dga, dgb, dpa, dpb = _rowcall("merge_bwd", T, tb, merge_rows + [_rows(dm, tb, D)], merge_bwd_body,
                                  [(D, BF16), (D, BF16), (D, BF16), (D, BF16)])
    doa = _mm(dpa, wpf, tb=True, name="mm_proj_fox_dx")
    dwpf = _mm(o_a, dpa, ta=True, out_dtype=BF16, name="mm_proj_fox_dw")
    dob = _mm(dpb, wpr, tb=True, name="mm_proj_rwkv_dx")
    dwpr = _mm(o_b, dpb, ta=True, out_dtype=BF16, name="mm_proj_rwkv_dw")

    do_b, dza, delta128 = _rowcall("fox_attn_pre", T, tb, [_rows(doa, tb, 1024), _rows(u, tb, 1024, 3), _rows(o_raw, tb, 1024)],
                                   lambda i, *a: (list(_attn_pre_math(*a)), []), [(1024, BF16), (1024, F32), (LANE, F32)])
    delta = delta128[:, 0:8].T.reshape(8, T, 1)
    dq, dk, dv, dc3 = _attn_bwd(u, c3, lse, do_b, delta, T)
    dfl, dbias_p = _gates_bwd(dc3.reshape(8, T), u, bias_pad, T, F_CB)

    def post_bwd_body(i, yv, rv, kpv, vv, zv, lnw, lnb, rkv, dobv):
        _, vjp = jax.vjp(_post_math, yv, rv, kpv, vv, zv, lnw, lnb, rkv)
        dy_, dr_, dkp_, dv_, dz_, dlnw, dlnb, drk = vjp(dobv)
        return [dy_, dr_, dkp_, dv_, dz_], [dlnw, dlnb, drk]

    dy_s, dr_p, dkp_p, dv_p, dzb, dlnw_p, dlnb_p, drk_p = _rowcall(
        "rwkv_post_bwd", T, tbh, post_rows + post_par + [_rows(dob, tbh, 1024)], post_bwd_body,
        [(1024, F32)] * 5, [(1, 1024)] * 3)
    dr_s, dlw, dkp_s, dv_s, dan, dbb = _scan_bwd(xs, lw, kp, an, bb, states, invs, dy_s, T)

    def prep_bwd_body(i, xk, xwd, xad, w0, a0, kkw, kaw, w2v, a2v, dlw_, dkp1, dkp2, dan_, dbb_, dr1, dr2, dv1, dv2, dz_):
        _, vjp = jax.vjp(_prep_math, xk, xwd, xad, w0, a0, kkw, kaw, w2v, a2v)
        dxk, dxwd, dxad, dw0, da0, dkk, dka, dw2, da2 = vjp((dlw_, dkp1 + dkp2, dan_, dbb_))
        return [[dr1 + dr2, dxk, dv1 + dv2, dz_], [dxwd, dxad]], [dw0, da0, dkk, dka, dw2, da2]

    cots = [dlw, dkp_s, dkp_p, dan, dbb, dr_s, dr_p, dv_s, dv_p, dzb]
    dxs, dxl, dw0_p, da0_p, dkk_p, dka_p, dw2_p, da2_p = _rowcall(
        "rwkv_prep_bwd", T, tbh, prep_rows + prep_par + [_rows(c_, tbh, 1024) for c_ in cots], prep_bwd_body,
        [(4096, F32), (256, F32)], [(1, 1024)] * 4 + [(LANE, 1024)] * 2)

    def shift_bwd_body(i, dm_, hm, dl_, hl, um, pm, ul, pl_, mm_, ml):
        last = i == T // tbh - 1
        outs, accs = [], []
        for dv_, hv, uv, pv, mv in ((dm_, hm, um, pm, mm_), (dl_, hl, ul, pl_, ml)):
            hv = jnp.where(last, 0.0, hv)
            nxt = pltpu.roll(jnp.concatenate([dv_, hv], axis=0), tbh + SUB - 1, 0)[:tbh]
            pv = jnp.where(i == 0, 0.0, pv)
            prev = pltpu.roll(jnp.concatenate([pv, uv], axis=0), 1, 0)[SUB:]
            outs.append(dv_ * (1.0 - mv) + nxt * mv)
            accs.append(jnp.sum(dv_ * (prev - uv), axis=0, keepdims=True))
        return outs, accs

    def halo_next(arr, w, cb):
        last_blk = T // SUB - 1
        return (arr, (SUB, w), lambda i: (jnp.minimum((i + 1) * (tbh // SUB), last_blk), cb))

    du_b, du_l, dmu_main_p, dmu_lora_p = _rowcall(
        "rwkv_shift_bwd", T, tbh,
        [_rows(dxs, tbh, 4096), halo_next(dxs, 4096, 0), _rows(dxl, tbh, 256), halo_next(dxl, 256, 0),
         _rows(u, tbh, 4096, 1), halo_prev(u, 4096, 1), _rows(u, tbh, 256, LORA_CB), halo_prev(u, 256, LORA_CB),
         _whole(mu_main), _whole(mu_lora)],
        shift_bwd_body, [(4096, BF16), (256, BF16)], [(1, 4096), (1, 256)])

    lora_g = jnp.concatenate([dw2_p[:96].reshape(96, N_DEV, 128).transpose(1, 0, 2),
                              da2_p[:96].reshape(96, N_DEV, 128).transpose(1, 0, 2)], axis=2).astype(BF16)
    gb = jnp.concatenate([dwpf.reshape(1024, N_DEV, 256).transpose(1, 0, 2),
                          dwpr.reshape(1024, N_DEV, 256).transpose(1, 0, 2),
                          dwo.reshape(N_DEV, 2048, 256), lora_g, jnp.zeros((N_DEV, RB - 4192, 256), BF16)], axis=1)
    xx, yy, cc = _place()
    chip_slots = jnp.stack([_slot((*chip, cc)) for chip in _other_chips(xx, yy)]).astype(jnp.int32)
    core = jnp.stack([cc]).astype(jnp.int32)
    du, r1b = _rowcall("assemble_du", T, tb,
                       [_rows(a_, tb, a_.shape[1]) for a_ in (dq, dk, dv, dza, du_b, dga, dgb, dfl, du_l)],
                       lambda i, *a: ([list(a[:8]) + [jnp.zeros((tb, LANE), BF16), a[8]]], []), [(NP, BF16)],
                       side=_exchange_pair_side(gb))
    sb = _pair_add("pair_add_packed", gb, r1b, chip_slots, RB // 3)
    dw_pad_t, t1b = _mm(du, h, ta=True, out_dtype=BF16, tm=1280, tn=1024, name="mm_in_dw",
                        side=_exchange_first_side(sb))
    pb = _axis_add("axis_add_packed", sb, t1b, core, RB // 3)
    ga, t2b = _split_rows(dw_pad_t, [(d, lo, pad_col, n) for pad_col, d, lo, n in pieces], NI,
                          _exchange_second_side(pb))
    r1a = _exchange_pair(ga)
    sa = _pair_add("pair_add_w_in", ga, r1a, chip_slots, 512, by_cols=True)
    ga_own = lax.dynamic_index_in_dim(ga, me, 0, keepdims=True)
    gb_own = lax.dynamic_index_in_dim(gb, me, 0, keepdims=True)
    dh, t1a = _mm(du, w_pad, tb=True, tm=1024, tn=1024, tk=NP // 10, name="mm_in_dx", side=_exchange_first_side(sa))
    pa = _axis_add("axis_add_w_in", sa, t1a, core, 512, by_cols=True)

    def rms_bwd_body(i, xv, g, dhv, doutv):
        _, vjp = jax.vjp(_rms_math, xv, g)
        dx_, dg_ = vjp(dhv)
        return [dx_ + doutv], [dg_]

    grad_x2, dng_p, t2a = _rowcall(
        "rms_bwd", T, tb, [_rows(x2, tb, D), _whole(norm_gain), _rows(dh, tb, D), _rows(dout, tb, D)],
        rms_bwd_body, [(D, F32)], [(1, D)], side=_exchange_second_side(pa))

    w_in_outs = _adamw("adamw_w_in", w_in[0].T, m_w_in[0].T, v_w_in[0].T, [(ga_own, 1), (r1a, 1), (t1a, 1), (t2a, 1)], 256, by_cols=True)
    g_in, d_in, m_in, v_in = [o.T for o in w_in_outs]

    dmu = jnp.concatenate([dmu_main_p, dmu_lora_p[:, 0:96], dmu_lora_p[:, 128:224]], axis=1)
    small_parts = [dng_p, dbias_p[:, 0:8], dmu, dw0_p, da0_p, dkk_p, dka_p, drk_p, dlnw_p, dlnb_p, dgf_p, loss_p]
    SR = 128
    small = _pad_cols(jnp.concatenate(small_parts, axis=1), SR * LANE).reshape(SR, LANE)
    rs = _allgather(small, "allgather_small")
    pk = lambda pf, pr, wo_, w2_, a2_: _pack_b(pf[0], pr[0], wo_[0], w2_[0], a2_[0], RB)
    outs_b = _adamw("adamw_packed", pk(w_proj_fox, w_proj_rwkv, w_out, rwkv_w2, rwkv_a2),
                    pk(m_w_proj_fox, m_w_proj_rwkv, m_w_out, m_rwkv_w2, m_rwkv_a2),
                    pk(v_w_proj_fox, v_w_proj_rwkv, v_w_out, v_rwkv_w2, v_rwkv_a2), [(gb_own, 1), (r1b, 1), (t1b, 1), (t2b, 1)], RB // 3)

    def pack_small(ng, fb, sm, w0, a0, kk_, ka_, rk_, lnw, lnb, fg):
        parts = [ng, fb, sm, w0, a0, kk_, ka_, rk_.reshape(1, 1024), lnw, lnb, fg.reshape(1, D), jnp.zeros((1, 1), F32)]
        return _pad_cols(jnp.concatenate(parts, axis=1), SR * LANE).reshape(SR, LANE)

    outs_s = _adamw("adamw_small",
                    pack_small(norm_gain, fox_forget_bias, rwkv_shift_mix, rwkv_w0, rwkv_a0, rwkv_k_k, rwkv_k_a, rwkv_r_k,
                               rwkv_ln_w, rwkv_ln_b, final_norm_gain),
                    pack_small(m_norm_gain, m_fox_forget_bias, m_rwkv_shift_mix, m_rwkv_w0, m_rwkv_a0, m_rwkv_k_k, m_rwkv_k_a,
                               m_rwkv_r_k, m_rwkv_ln_w, m_rwkv_ln_b, m_final_norm_gain),
                    pack_small(v_norm_gain, v_fox_forget_bias, v_rwkv_shift_mix, v_rwkv_w0, v_rwkv_a0, v_rwkv_k_k, v_rwkv_k_a,
                               v_rwkv_r_k, v_rwkv_ln_w, v_rwkv_ln_b, v_final_norm_gain),
                    [(rs, N_DEV)], SR)

    def unpack_b(pkd):
        return dict(w_proj_fox=pkd[0:1024][None], w_proj_rwkv=pkd[1024:2048][None], w_out=pkd[2048:4096].reshape(1, 256, D),
                    rwkv_w2=pkd[4096:4192, 0:128][None], rwkv_a2=pkd[4096:4192, 128:256][None])

    def unpack_s(pkd):
        flat = pkd.reshape(1, SR * LANE)
        names = [("norm_gain", D), ("fox_forget_bias", 8), ("rwkv_shift_mix", 4288), ("rwkv_w0", 1024), ("rwkv_a0", 1024),
                 ("rwkv_k_k", 1024), ("rwkv_k_a", 1024), ("rwkv_r_k", 1024), ("rwkv_ln_w", 1024), ("rwkv_ln_b", 1024),
                 ("final_norm_gain", D), ("loss", 1)]
        out, off = {}, 0
        for nm, n in names:
            out[nm] = flat[:, off:off + n]
            off += n
        out["rwkv_r_k"] = out["rwkv_r_k"].reshape(1, 16, 64)
        out["final_norm_gain"] = out["final_norm_gain"].reshape(D)
        return out

    order = ["norm_gain", "w_in", "fox_forget_bias", "rwkv_shift_mix", "rwkv_w0", "rwkv_w2", "rwkv_a0", "rwkv_a2", "rwkv_k_k",
             "rwkv_k_a", "rwkv_r_k", "rwkv_ln_w", "rwkv_ln_b", "w_proj_fox", "w_proj_rwkv", "w_out", "final_norm_gain"]
    result = []
    loss = None
    for kind, big in enumerate((g_in, d_in, m_in, v_in)):
        d = {**unpack_b(outs_b[kind]), **unpack_s(outs_s[kind]), "w_in": big[None]}
        if kind == 0:
            loss = d["loss"].reshape(())
        result += [d[n] for n in order]
    return (loss, grad_x2[None], *result)
```

```python
import functools

import jax
import jax.numpy as jnp
from jax import lax
from jax.experimental import pallas as pl
from jax.experimental.pallas import tpu as pltpu

F32 = jnp.float32
BF16 = jnp.bfloat16
HI = lax.Precision.HIGHEST
H3 = lax.Precision.HIGH
MESH = pl.DeviceIdType.MESH

FOX_HD = 128
RW_HD = 64
RMS_EPS = 1e-6
GN_EPS = 64e-5
L2_EPS = 1e-12
ADAM_LR = 0.001
ADAM_B1 = 0.9
ADAM_B2 = 0.999
ADAM_EPS = 1e-08
ADAM_WD = 0.01
ADAM_STEP = 10

LANE = 128
SUB = 8
VMEM_LIMIT = 56 * 1024 * 1024
N_DEV = 8
CHUNK = 128
SCAN_GROUP = 1
SCAN_PAIRS = 8
PS = None
NEG = -1e30


def _scan_shape(T):
    c = min(CHUNK, T)
    return c, min(SCAN_GROUP, T // c)


def _cp(sem=None):
    return pltpu.CompilerParams(dimension_semantics=sem, vmem_limit_bytes=VMEM_LIMIT)


def _sigmoid(x):
    return jax.nn.sigmoid(x)


def _softplus(x):
    return jnp.maximum(x, 0.0) + jnp.log(1.0 + jnp.exp(-jnp.abs(x)))


def _nn(a, b, prec=None):
    return lax.dot_general(a, b, (((1,), (0,)), ((), ())), precision=prec, preferred_element_type=F32)


def _nt(a, b, prec=None):
    return lax.dot_general(a, b, (((1,), (1,)), ((), ())), precision=prec, preferred_element_type=F32)


def _tn(a, b, prec=None):
    return lax.dot_general(a, b, (((0,), (0,)), ((), ())), precision=prec, preferred_element_type=F32)


def _iota2(shape, dim):
    return lax.broadcasted_iota(jnp.int32, shape, dim)


def _seg_sum(x):
    r = _iota2((LANE, LANE), 0) // RW_HD
    c = _iota2((LANE, LANE), 1) // RW_HD
    bd = (r == c).astype(F32)
    parts = [_nn(x[:, j * LANE:(j + 1) * LANE], bd, H3) for j in range(x.shape[1] // LANE)]
    return parts[0] if len(parts) == 1 else jnp.concatenate(parts, axis=1)


def _mm(a, b, *, ta=False, tb=False, out_dtype=F32, tm=1024, tn=1024, tk=None, name, side=None):
    assert not (ta and tb)
    K, M = a.shape if ta else a.shape[::-1]
    N = b.shape[0] if tb else b.shape[1]
    tm, tn = min(tm, M), min(tn, N)
    tk = K if tk is None else tk
    nk = K // tk
    assert M % tm == 0 and N % tn == 0 and K % tk == 0
    gi, gj = M // tm, N // tn
    a_spec = pl.BlockSpec((tk, tm), lambda i, j, k: (k, i)) if ta else pl.BlockSpec((tm, tk), lambda i, j, k: (i, k))
    b_spec = pl.BlockSpec((tn, tk), lambda i, j, k: (j, k)) if tb else pl.BlockSpec((tk, tn), lambda i, j, k: (k, j))
    side_ins, side_outs, side_scr, side_make = side if side is not None else ((), (), (), None)
    n_si, n_so = len(side_ins), len(side_outs)
    n_acc = 0 if nk == 1 else 1

    def body(*refs):
        a_ref, b_ref = refs[:2]
        o_ref = refs[2 + n_si]
        scr = refs[3 + n_si + n_so:]
        k = pl.program_id(2)
        if side_make is not None:
            start, finish = side_make(refs[2:2 + n_si], refs[3 + n_si:3 + n_si + n_so], scr[n_acc:])
            first = jnp.logical_and(jnp.logical_and(pl.program_id(0) == 0, pl.program_id(1) == 0), k == 0)
            last = jnp.logical_and(jnp.logical_and(pl.program_id(0) == gi - 1, pl.program_id(1) == gj - 1), k == nk - 1)
            pl.when(first)(start)
        av = a_ref[...].astype(BF16)
        bv = b_ref[...].astype(BF16)
        p = _tn(av, bv) if ta else _nt(av, bv) if tb else _nn(av, bv)
        if nk == 1:
            o_ref[...] = p.astype(out_dtype)
        else:
            acc_ref = scr[0]

            @pl.when(k == 0)
            def _():
                acc_ref[...] = p

            @pl.when(k > 0)
            def _():
                acc_ref[...] += p

            @pl.when(k == nk - 1)
            def _():
                o_ref[...] = acc_ref[...].astype(out_dtype)
        if side_make is not None:
            pl.when(last)(finish)

    any_spec = pl.BlockSpec(memory_space=pl.ANY)
    res = pl.pallas_call(
        body, name=name,
        out_shape=[jax.ShapeDtypeStruct((M, N), out_dtype)] + list(side_outs),
        grid=(gi, gj, nk),
        in_specs=[a_spec, b_spec] + [any_spec] * n_si,
        out_specs=[pl.BlockSpec((tm, tn), lambda i, j, k: (i, j))] + [any_spec] * n_so,
        scratch_shapes=([] if nk == 1 else [pltpu.VMEM((tm, tn), F32)]) + list(side_scr),
        compiler_params=_cp(("arbitrary",) * 3 if side is not None else ("parallel", "parallel", "arbitrary")),
    )(a, b, *side_ins)
    return res if side is not None else res[0]


def _rows(arr, tb, w, cb=0):
    return (arr, (tb, w), lambda i: (i, cb))


def _whole(arr):
    nd = arr.ndim
    return (arr, arr.shape, lambda i: (0,) * nd)


def _rowcall(name, T, tb, ins, body, outs, accs=(), side=None):
    n_in, n_out, n_acc = len(ins), len(outs), len(accs)
    side_ins, side_outs, side_scr, side_make = side if side is not None else ((), (), (), None)
    n_si = len(side_ins)

    def kern(*refs):
        i = pl.program_id(0)
        if side_make is not None:
            side_refs = refs[n_in + n_si + n_out + n_acc:]
            start, finish = side_make(refs[n_in:n_in + n_si], side_refs[:len(side_outs)], side_refs[len(side_outs):])
            pl.when(i == 0)(start)
            refs = refs[:n_in] + refs[n_in + n_si:]
        vals = [r[...] for r in refs[:n_in]]
        ro, ao = body(i, *vals)
        for r, v in zip(refs[n_in:n_in + n_out], ro):
            if isinstance(v, (list, tuple)):
                off = 0
                for piece in v:
                    w = piece.shape[1]
                    r[:, off:off + w] = piece.astype(r.dtype)
                    off += w
            else:
                r[...] = v.astype(r.dtype)
        if accs:
            acc_refs = refs[n_in + n_out:n_in + n_out + n_acc]

            @pl.when(i == 0)
            def _():
                for r in acc_refs:
                    r[...] = jnp.zeros(r.shape, F32)

            for r, v in zip(acc_refs, ao):
                r[...] += v
        if side_make is not None:
            pl.when(i == T // tb - 1)(finish)

    any_spec = pl.BlockSpec(memory_space=pl.ANY)
    out_shape = [jax.ShapeDtypeStruct((T, w), dt) for (w, dt) in outs] + [jax.ShapeDtypeStruct(s, F32) for s in accs]
    out_specs = [pl.BlockSpec((tb, w), lambda i: (i, 0)) for (w, dt) in outs] + [pl.BlockSpec(s, lambda i: (0, 0)) for s in accs]
    res = pl.pallas_call(
        kern, name=name,
        out_shape=out_shape + list(side_outs),
        grid=(T // tb,),
        in_specs=[pl.BlockSpec(bs, im) for (_, bs, im) in ins] + [any_spec] * n_si,
        out_specs=out_specs + [any_spec] * len(side_outs),
        scratch_shapes=list(side_scr),
        compiler_params=_cp(("arbitrary",)),
    )(*[a for (a, _, _) in ins], *side_ins)
    return res


def _rms_math(x, g):
    r = lax.rsqrt(jnp.mean(x * x, axis=-1, keepdims=True) + RMS_EPS)
    return x * r * g


def _merge_math(ga, gb, pa, pb):
    return _sigmoid(ga) * pa + _sigmoid(gb) * pb


def _prep_math(xk, xwd, xad, w0, a0, kk_w, ka_w, w2p, a2p):
    z = w0 + _nn(jnp.tanh(xwd), w2p, H3)
    w = -_softplus(-z) - 0.5
    lw = -jnp.exp(w)
    ag = _sigmoid(a0 + _nn(xad, a2p, H3))
    p = xk * kk_w
    n = jnp.maximum(jnp.sqrt(_seg_sum(p * p)), L2_EPS)
    kk = p / n
    kp = xk * (1.0 + (ag - 1.0) * ka_w)
    return lw, kp, -kk, kk * ag


def _post_math(y, r, kp, v, z, lnw, lnb, rk):
    inv = 1.0 / RW_HD
    mu = _seg_sum(y) * inv
    d = y - mu
    var = _seg_sum(d * d) * inv
    yn = d * lax.rsqrt(var + GN_EPS) * lnw + lnb
    bonus = _seg_sum(r * kp * rk) * v
    return (yn + bonus) * (z * _sigmoid(z))


def _neumann(ms, depth):
    eye = (_iota2(ms[0].shape, 0) == _iota2(ms[0].shape, 1)).astype(F32)
    width = ms[0].shape[1]
    mp = [_nn(m, m, PS) for m in ms]
    inv = [eye + m for m in ms]
    n = 2
    while n < depth:
        last = 2 * n >= depth
        for i in range(len(ms)):
            if last:
                inv[i] = inv[i] + _nn(mp[i], inv[i], PS)
            else:
                z = _nn(mp[i], jnp.concatenate([mp[i], inv[i]], axis=1), PS)
                mp[i], inv[i] = z[:, :width], inv[i] + z[:, width:]
        n *= 2
    return inv


@functools.partial(jax.custom_vjp, nondiff_argnums=(1,))
def _unit_inverses(ms, depth):
    if depth <= RW_HD:
        return tuple(_neumann(list(ms), depth))
    h, n = depth // 2, len(ms)
    cat = jnp.concatenate
    z = jnp.zeros((h, h), F32)

    def heads(m, r, c):
        b0 = m[r * h:(r + 1) * h, c * h:(c + 1) * h]
        b1 = m[depth + r * h:depth + (r + 1) * h, depth + c * h:depth + (c + 1) * h]
        return cat([cat([b0, z], axis=1), cat([z, b1], axis=1)], axis=0)

    diag = _neumann([heads(m, 0, 0) for m in ms] + [heads(m, 1, 1) for m in ms], h)
    ta, td = diag[:n], diag[n:]
    low = [_nn(heads(m, 1, 0), a, PS) for m, a in zip(ms, ta)]
    low = [_nn(d, x, PS) for d, x in zip(td, low)]
    out = []
    for a, x, d in zip(ta, low, td):
        rows = []
        for hd in (0, 1):
            sl = slice(hd * h, (hd + 1) * h)
            top, bot = [a[sl, sl], z], [x[sl, sl], d[sl, sl]]
            pad = [z, z]
            rows.append(cat(top + pad if hd == 0 else pad + top, axis=1))
            rows.append(cat(bot + pad if hd == 0 else pad + bot, axis=1))
        out.append(cat(rows, axis=0))
    return tuple(out)


def _unit_inverses_fwd(ms, depth):
    inv = _unit_inverses(ms, depth)
    return inv, inv


def _unit_inverses_bwd(depth, inv, cts):
    left = [_tn(t, g, PS) for t, g in zip(inv, cts)]
    return (tuple(_nt(l, t, PS) for l, t in zip(left, inv)),)


_unit_inverses.defvjp(_unit_inverses_fwd, _unit_inverses_bwd)


@jax.custom_vjp
def _known_inverses(ms, inv):
    return inv


def _known_inverses_fwd(ms, inv):
    return inv, inv


def _known_inverses_bwd(inv, cts):
    dms = _unit_inverses_bwd(None, inv, cts)[0]
    return dms, tuple(jnp.zeros_like(t) for t in inv)


_known_inverses.defvjp(_known_inverses_fwd, _known_inverses_bwd)


def _scan_group(s0s, *flat, known_inv=None, with_inv=False):
    P = len(s0s)
    G = len(flat) // (6 * P)
    ch = [flat[6 * i:6 * i + 6] for i in range(P * G)]
    C = ch[0][0].shape[0]
    C2 = 2 * C
    cat = jnp.concatenate
    m0 = _iota2((1, LANE), 1) < RW_HD
    mask0 = m0.astype(F32)
    mask1 = 1.0 - mask0
    r2 = _iota2((C2, C2), 0)
    c2 = _iota2((C2, C2), 1)
    dist = r2 - c2
    in_head = dist <= r2 % C
    lower = (_iota2((C, C), 0) >= _iota2((C, C), 1)).astype(F32)
    bd = (_iota2((LANE, LANE), 0) // RW_HD) == (_iota2((LANE, LANE), 1) // RW_HD)

    def tri(m, strict):
        return jnp.where(dist > 0 if strict else dist >= 0, jnp.where(in_head, m, 0.0), 0.0)

    def sel(z):
        return jnp.where(m0, z[:C], z[C:])

    gs = [_nn(lower, c[1], HI) for c in ch]
    pre = []
    for (r, lw, k, v, a, b), g in zip(ch, gs):
        g_end = jnp.sum(lw, axis=0, keepdims=True)
        gm = g - jnp.sum(lw[:C // 2], axis=0, keepdims=True)
        en = jnp.exp(-gm)
        ec = jnp.exp(g_end - g)
        pre.append(dict(at=a * jnp.exp(g - lw), rt=r * jnp.exp(g), am=a * jnp.exp(gm - lw), rm=r * jnp.exp(gm),
                        bt=b * en, kt=k * en, bh=b * ec, kh=k * ec, dec=jnp.exp(g_end), v=v))
    grams = [_nt(cat([p["am"] * mask0, p["am"] * mask1, p["rm"] * mask0, p["rm"] * mask1], axis=0),
                 cat([p["bt"], p["bt"], p["kt"], p["kt"]], axis=0), PS) for p in pre]
    mab = tuple(tri(gm[:C2, :C2], True) for gm in grams)
    tinv = _unit_inverses(mab, C) if known_inv is None else _known_inverses(mab, known_inv)
    xv =[sel(_nn(tri(gm[:C2, C2:], True), cat([p["v"], p["v"]], axis=0), PS)) for gm, p in zip(grams, pre)]
    ys, s = [None] * (P * G), list(s0s)
    for i in range(G):
        for q in range(P):
            n = q * G + i
            p, gm = pre[n], grams[n]
            sx = _nt(cat([p["at"], p["rt"]], axis=0), s[q], PS)
            x = sx[:C] + xv[n]
            u = sel(_nn(tinv[n], cat([x, x], axis=0), PS))
            v = p["v"]
            ys[n] = sx[C:] + sel(_nn(cat([tri(gm[C2:, :C2], False), tri(gm[C2:, C2:], False)], axis=1),
                                     cat([u, u, v, v], axis=0), PS))
            s[q] = s[q] * p["dec"] + jnp.where(bd, _tn(cat([u, v], axis=0), cat([p["bh"], p["kh"]], axis=0), PS), 0.0)
    return (tuple(ys), tuple(s), tinv) if with_inv else (tuple(ys), tuple(s))


def _scan_fwd(xs, lw, kp, an, bb, T):
    C, G = _scan_shape(T)
    P = SCAN_PAIRS
    nc = T // (C * G)
    npair = 1024 // LANE

    def kern(r_ref, lw_ref, k_ref, v_ref, a_ref, b_ref, y_ref, st_ref, inv_ref, s_scr):
        n = pl.program_id(1)

        @pl.when(n == 0)
        def _():
            s_scr[...] = jnp.zeros(s_scr.shape, F32)

        st_ref[0] = s_scr[...]
        ins = (r_ref, lw_ref, k_ref, v_ref, a_ref, b_ref)
        ys, s1, inv = _scan_group(
            tuple(s_scr[q] for q in range(P)),
            *[ref[i * C:(i + 1) * C, q * LANE:(q + 1) * LANE] for q in range(P) for i in range(G) for ref in ins], with_inv=True)
        for q in range(P):
            for i in range(G):
                y_ref[i * C:(i + 1) * C, q * LANE:(q + 1) * LANE] = ys[q * G + i]
                inv_ref[0, 0, q * G + i] = inv[q * G + i]
            s_scr[q] = s1[q]

    def col(off):
        return pl.BlockSpec((C * G, P * LANE), lambda p, n: (n, off // P + p))

    return pl.pallas_call(
        kern, name="rwkv_scan_fwd",
        out_shape=[jax.ShapeDtypeStruct((T, 1024), F32), jax.ShapeDtypeStruct((nc, npair, LANE, LANE), F32),
                   jax.ShapeDtypeStruct((nc, npair // P, P * G, 2 * C, 2 * C), F32)],
        grid=(npair // P, nc),
        in_specs=[col(0), col(0), col(0), col(16), col(0), col(0)],
        out_specs=[col(0), pl.BlockSpec((1, P, LANE, LANE), lambda p, n: (n, p, 0, 0)),
                   pl.BlockSpec((1, 1, P * G, 2 * C, 2 * C), lambda p, n: (n, p, 0, 0, 0))],
        scratch_shapes=[pltpu.VMEM((P, LANE, LANE), F32)],
        compiler_params=_cp(("parallel", "arbitrary")),
    )(xs, lw, kp, xs, an, bb)


def _scan_bwd(xs, lw, kp, an, bb, states, invs, dy, T):
    C, G = _scan_shape(T)
    P = SCAN_PAIRS
    nc = T // (C * G)
    npair = 1024 // LANE

    def kern(r_ref, lw_ref, k_ref, v_ref, a_ref, b_ref, st_ref, inv_ref, dy_ref, dr_ref, dlw_ref, dk_ref, dv_ref, da_ref, db_ref,
             ds_scr):
        n = pl.program_id(1)

        @pl.when(n == 0)
        def _():
            ds_scr[...] = jnp.zeros(ds_scr.shape, F32)

        ins = (r_ref, lw_ref, k_ref, v_ref, a_ref, b_ref)
        units = [(q, i) for q in range(P) for i in range(G)]
        known = tuple(inv_ref[0, 0, q * G + i] for q, i in units)
        _, vjp = jax.vjp(functools.partial(_scan_group, known_inv=known), tuple(st_ref[0, q] for q in range(P)),
                         *[ref[i * C:(i + 1) * C, q * LANE:(q + 1) * LANE] for q, i in units for ref in ins])
        grads = vjp((tuple(dy_ref[i * C:(i + 1) * C, q * LANE:(q + 1) * LANE] for q, i in units),
                     tuple(ds_scr[q] for q in range(P))))
        for q in range(P):
            ds_scr[q] = grads[0][q]
        outs = (dr_ref, dlw_ref, dk_ref, dv_ref, da_ref, db_ref)
        for n_, (q, i) in enumerate(units):
            for t, ref in enumerate(outs):
                ref[i * C:(i + 1) * C, q * LANE:(q + 1) * LANE] = grads[1 + 6 * n_ + t]

    def col(off):
        return pl.BlockSpec((C * G, P * LANE), lambda p, n: (nc - 1 - n, off // P + p))

    return pl.pallas_call(
        kern, name="rwkv_scan_bwd",
        out_shape=[jax.ShapeDtypeStruct((T, 1024), F32)] * 6,
        grid=(npair // P, nc),
        in_specs=[col(0), col(0), col(0), col(16), col(0), col(0),
                  pl.BlockSpec((1, P, LANE, LANE), lambda p, n: (nc - 1 - n, p, 0, 0)),
                  pl.BlockSpec((1, 1, P * G, 2 * C, 2 * C), lambda p, n: (nc - 1 - n, p, 0, 0, 0)), col(0)],
        out_specs=[col(0)] * 6,
        scratch_shapes=[pltpu.VMEM((P, LANE, LANE), F32)],
        compiler_params=_cp(("parallel", "arbitrary")),
    )(xs, lw, kp, xs, an, bb, states, invs, dy)


def _gates_fwd(u, bias_pad, T, f_cb):
    nb = T // LANE

    def kern(f_ref, b_ref, c_ref):
        x = f_ref[...] + b_ref[...]
        lf = jnp.minimum(x, 0.0) - jnp.log(1.0 + jnp.exp(-jnp.abs(x)))
        lft = lf.T
        ut = (_iota2((LANE, LANE), 0) <= _iota2((LANE, LANE), 1)).astype(F32)
        carry = jnp.zeros((LANE, 1), F32)
        for blk in range(nb):
            seg = lft[:, blk * LANE:(blk + 1) * LANE]
            cs = _nn(seg, ut, HI) + carry
            c_ref[:, blk * LANE:(blk + 1) * LANE] = cs[:SUB, :]
            carry = carry + jnp.sum(seg, axis=1, keepdims=True)

    return pl.pallas_call(
        kern, name="fox_gates_fwd",
        out_shape=jax.ShapeDtypeStruct((SUB, T), F32),
        grid=(1,),
        in_specs=[pl.BlockSpec((T, LANE), lambda i: (0, f_cb)), pl.BlockSpec((1, LANE), lambda i: (0, 0))],
        out_specs=pl.BlockSpec((SUB, T), lambda i: (0, 0)),
        compiler_params=_cp(("arbitrary",)),
    )(u, bias_pad)


def _gates_bwd(dc, u, bias_pad, T, f_cb):
    nb = T // LANE

    def kern(dc_ref, f_ref, b_ref, dfl_ref, db_ref):
        dcv = jnp.concatenate([dc_ref[...], jnp.zeros((LANE - SUB, T), F32)], axis=0)
        lt = (_iota2((LANE, LANE), 0) >= _iota2((LANE, LANE), 1)).astype(F32)
        carry = jnp.zeros((LANE, 1), F32)
        pieces = [None] * nb
        for blk in range(nb - 1, -1, -1):
            seg = dcv[:, blk * LANE:(blk + 1) * LANE]
            pieces[blk] = _nn(seg, lt, HI) + carry
            carry = carry + jnp.sum(seg, axis=1, keepdims=True)
        dlf = (pieces[0] if nb == 1 else jnp.concatenate(pieces, axis=1)).T
        x = f_ref[...] + b_ref[...]
        dfl = dlf * _sigmoid(-x)
        dfl_ref[...] = dfl
        db_ref[...] = jnp.sum(dfl, axis=0, keepdims=True)

    return pl.pallas_call(
        kern, name="fox_gates_bwd",
        out_shape=[jax.ShapeDtypeStruct((T, LANE), F32), jax.ShapeDtypeStruct((1, LANE), F32)],
        grid=(1,),
        in_specs=[pl.BlockSpec((SUB, T), lambda i: (0, 0)), pl.BlockSpec((T, LANE), lambda i: (0, f_cb)),
                  pl.BlockSpec((1, LANE), lambda i: (0, 0))],
        out_specs=[pl.BlockSpec((T, LANE), lambda i: (0, 0)), pl.BlockSpec((1, LANE), lambda i: (0, 0))],
        compiler_params=_cp(("arbitrary",)),
    )(dc, u, bias_pad)


ATTN_HEADS = 2


def _attn_block(T):
    return 512 if T % 512 == 0 and T >= 1024 else 128


def _attn_fwd(u, c3, T):
    H, HP = 8, ATTN_HEADS
    bq = _attn_block(T)
    nq = T // bq
    scale = FOX_HD ** -0.5
    lanes = [slice(h * LANE, (h + 1) * LANE) for h in range(HP)]

    def kern(q_ref, k_ref, v_ref, z_ref, cq_ref, ck_ref, o_ref, oa_ref, lse_ref):
        i = pl.program_id(1)
        q = [(q_ref[:, ln] * scale).astype(BF16) for ln in lanes]
        c0 = [cq_ref[h][:, 0:1] for h in range(HP)]

        def step(j, carry, diagonal=False):
            off = pl.multiple_of(j * bq, bq)
            s = [_nt(q[h], k_ref[pl.ds(off, bq), lanes[h]].astype(BF16)) + (c0[h] - ck_ref[h, :, pl.ds(off, bq)])
                 for h in range(HP)]
            ps, out = [], []
            for h in range(HP):
                m, l, acc = carry[h]
                sh = s[h]
                if diagonal:
                    sh = jnp.where(_iota2((bq, bq), 1) <= _iota2((bq, bq), 0), sh, NEG)
                m_new = jnp.maximum(m, jnp.max(sh, axis=1, keepdims=True))
                p = jnp.exp(sh - m_new)
                alpha = jnp.exp(m - m_new)
                p_hi = p.astype(BF16)
                ps.append((p_hi, (p - p_hi.astype(F32)).astype(BF16)))
                out.append((m_new, alpha * l + jnp.sum(p, axis=1, keepdims=True), alpha * acc))
            res = []
            for h, (m, l, acc) in enumerate(out):
                vj = v_ref[pl.ds(off, bq), lanes[h]].astype(BF16)
                res.append((m, l, acc + _nn(ps[h][0], vj) + _nn(ps[h][1], vj)))
            return tuple(res)

        init = tuple((jnp.full((bq, 1), NEG, F32), jnp.zeros((bq, 1), F32), jnp.zeros((bq, FOX_HD), F32)) for _ in range(HP))
        res = step(i, lax.fori_loop(0, i, step, init), diagonal=True)
        for h, (m, l, acc) in enumerate(res):
            o = acc / l
            z = z_ref[:, lanes[h]]
            o_ref[:, lanes[h]] = o
            oa_ref[:, lanes[h]] = (o * z * _sigmoid(z)).astype(BF16)
            lse_ref[h] = m + jnp.log(l)

    W = HP * LANE
    return pl.pallas_call(
        kern, name="fox_attn_fwd",
        out_shape=[jax.ShapeDtypeStruct((T, 1024), F32), jax.ShapeDtypeStruct((T, 1024), BF16),
                   jax.ShapeDtypeStruct((H, T, 1), F32)],
        grid=(H // HP, nq),
        in_specs=[pl.BlockSpec((bq, W), lambda g, i: (i, g)),
                  pl.BlockSpec((T, W), lambda g, i: (0, 8 // HP + g)),
                  pl.BlockSpec((T, W), lambda g, i: (0, 16 // HP + g)),
                  pl.BlockSpec((bq, W), lambda g, i: (i, 24 // HP + g)),
                  pl.BlockSpec((HP, 1, bq), lambda g, i: (g, 0, i)),
                  pl.BlockSpec((HP, 1, T), lambda g, i: (g, 0, 0))],
        out_specs=[pl.BlockSpec((bq, W), lambda g, i: (i, g)),
                   pl.BlockSpec((bq, W), lambda g, i: (i, g)),
                   pl.BlockSpec((HP, bq, 1), lambda g, i: (g, i, 0))],
        compiler_params=_cp(("parallel", "arbitrary")),
    )(u, u, u, u, c3, c3)


def _attn_probs(s, lse_i, diagonal):
    if not diagonal:
        return jnp.exp(s - lse_i)
    keep = _iota2(s.shape, 1) <= _iota2(s.shape, 0)
    return jnp.where(keep, jnp.exp(jnp.where(keep, s, NEG) - lse_i), 0.0)


def _attn_pre_math(doa, z, o):
    sg = _sigmoid(z)
    do = (doa * z * sg).astype(BF16)
    dz = doa * o * (sg * (1.0 + z * (1.0 - sg)))
    head_of = (_iota2((o.shape[1], LANE), 0) // FOX_HD == _iota2((o.shape[1], LANE), 1)).astype(F32)
    return do, dz, _nn(do.astype(F32) * o, head_of, HI)


def _attn_bwd(u, c3, lse, do, delta, T):
    H, HP = 8, ATTN_HEADS
    bq = _attn_block(T)
    nq = T // bq
    scale = FOX_HD ** -0.5
    lanes = [slice(h * LANE, (h + 1) * LANE) for h in range(HP)]

    def kern(q_ref, k_ref, v_ref, c_ref, lse_ref, do_ref, dl_ref, dq_ref, dk_ref, dv_ref, dc_ref):
        j = pl.program_id(1)

        @pl.when(j == 0)
        def _():
            dq_ref[...] = jnp.zeros(dq_ref.shape, F32)

        kj = [k_ref[:, ln].astype(BF16) for ln in lanes]
        vj = [v_ref[:, ln].astype(BF16) for ln in lanes]
        joff = pl.multiple_of(j * bq, bq)
        ck = [c_ref[h, :, pl.ds(joff, bq)] for h in range(HP)]

        def step(i, carry, diagonal=False):
            off = pl.multiple_of(i * bq, bq)
            qs = [(q_ref[pl.ds(off, bq), ln] * scale).astype(BF16) for ln in lanes]
            dob = [do_ref[pl.ds(off, bq), ln] for ln in lanes]
            s = [_nt(qs[h], kj[h]) + (c_ref[h, :, pl.ds(off, bq)][:, 0:1] - ck[h]) for h in range(HP)]
            dp = [_nt(dob[h], vj[h]) for h in range(HP)]
            pb, dsb, dcs = [], [], []
            for h in range(HP):
                p = _attn_probs(s[h], lse_ref[h, pl.ds(off, bq), :], diagonal)
                ds = p * (dp[h] - dl_ref[h, pl.ds(off, bq), :])
                pb.append(p.astype(BF16))
                dsb.append(ds.astype(BF16))
                dcs.append(jnp.sum(ds, axis=0, keepdims=True))
            out = []
            for h, (dk, dv, dc) in enumerate(carry):
                dq_ref[pl.ds(off, bq), lanes[h]] += _nn(dsb[h], kj[h]) * scale
                out.append((dk + _tn(dsb[h], qs[h]), dv + _tn(pb[h], dob[h]), dc - dcs[h]))
            return tuple(out)

        init = tuple((jnp.zeros((bq, FOX_HD), F32), jnp.zeros((bq, FOX_HD), F32), jnp.zeros((1, bq), F32)) for _ in range(HP))
        res = lax.fori_loop(j + 1, nq, step, step(j, init, diagonal=True))
        for h, (dk, dv, dc) in enumerate(res):
            dk_ref[:, lanes[h]] = dk
            dv_ref[:, lanes[h]] = dv
            dc_ref[h] = dc

    W = HP * LANE
    full = lambda cb: pl.BlockSpec((T, W), lambda g, j: (0, cb // HP + g))
    blk = lambda cb: pl.BlockSpec((bq, W), lambda g, j: (j, cb // HP + g))
    col1 = pl.BlockSpec((HP, T, 1), lambda g, j: (g, 0, 0))
    return pl.pallas_call(
        kern, name="fox_attn_bwd",
        out_shape=[jax.ShapeDtypeStruct((T, 1024), F32)] * 3 + [jax.ShapeDtypeStruct((H, 1, T), F32)],
        grid=(H // HP, nq),
        in_specs=[full(0), blk(8), blk(16), pl.BlockSpec((HP, 1, T), lambda g, j: (g, 0, 0)), col1, full(0), col1],
        out_specs=[full(0), blk(0), blk(0), pl.BlockSpec((HP, 1, bq), lambda g, j: (g, 0, j))],
        compiler_params=_cp(("parallel", "arbitrary")),
    )(u, u, u, c3, lse, do, delta)


def _place():
    return lax.axis_index("x"), lax.axis_index("y"), lax.axis_index("c")


def _slot(p):
    return 4 * p[0] + 2 * p[1] + p[2]


def _other_chips(x, y):
    return [(1 - x, y), (x, 1 - y), (1 - x, 1 - y)]


def _allgather_steps(in_refs, out_refs, scratch):
    (src,), (dst,) = in_refs, out_refs
    send_sems, recv_sems, local_sem = scratch
    x, y, c = _place()
    me, sibling = (x, y, c), (x, y, 1 - c)
    chips = _other_chips(x, y)

    def copy(k, block, to, from_input=False):
        d = dst.at[_slot(block)]
        return pltpu.make_async_remote_copy(
            src_ref=src if from_input else d, dst_ref=d, send_sem=send_sems.at[k], recv_sem=recv_sems.at[k],
            device_id=to, device_id_type=MESH)

    def first_copies():
        return [copy(0, me, sibling, True)] + [copy(1 + j, me, (*chip, c), True) for j, chip in enumerate(chips)]

    def start():
        pltpu.make_async_copy(src, dst.at[_slot(me)], local_sem).start()
        for cp in first_copies():
            cp.start()

    def finish():
        passed = []
        for j, chip in enumerate(chips):
            copy(1 + j, (*chip, c), me).wait_recv()
            passed.append(copy(4 + j, (*chip, c), sibling))
            passed[-1].start()
        copy(0, sibling, me).wait_recv()
        for j, chip in enumerate(chips):
            copy(4 + j, (*chip, 1 - c), me).wait_recv()
        for cp in first_copies() + passed:
            cp.wait_send()
        pltpu.make_async_copy(src, dst.at[_slot(me)], local_sem).wait()

    return start, finish


def _allgather_relay_steps(in_refs, out_refs, scratch):
    (src,), (dst,) = in_refs, out_refs
    send_sems, recv_sems, local_sem = scratch
    x, y, c = _place()
    me, sibling = (x, y, c), (x, y, 1 - c)
    x_nbr, y_nbr, diag = (1 - x, y, c), (x, 1 - y, c), (1 - x, 1 - y, c)
    flip = lambda a, bit: a + bit - 2 * a * bit
    relay_from = (flip(x, 1 - c), flip(y, c), c)
    relay_to = (flip(x, c), flip(y, 1 - c), c)

    def copy(k, block, to, from_input=False):
        d = dst.at[_slot(block)]
        return pltpu.make_async_remote_copy(
            src_ref=src if from_input else d, dst_ref=d, send_sem=send_sems.at[k], recv_sem=recv_sems.at[k],
            device_id=to, device_id_type=MESH)

    def first_copies():
        return [copy(0, me, sibling, True), copy(1, me, x_nbr, True), copy(2, me, y_nbr, True)]

    def other(block):
        return block[:2] + (1 - c,)

    def start():
        pltpu.make_async_copy(src, dst.at[_slot(me)], local_sem).start()
        for cp in first_copies():
            cp.start()

    def finish():
        copy(1, x_nbr, me).wait_recv()
        copy(2, y_nbr, me).wait_recv()
        later = [copy(3, relay_from, relay_to), copy(4, x_nbr, sibling), copy(5, y_nbr, sibling)]
        for cp in later:
            cp.start()
        copy(3, diag, me).wait_recv()
        later.append(copy(6, diag, sibling))
        later[-1].start()
        copy(0, sibling, me).wait_recv()
        for k, block in ((4, x_nbr), (5, y_nbr), (6, diag)):
            copy(k, other(block), me).wait_recv()
        for cp in first_copies() + later:
            cp.wait_send()
        pltpu.make_async_copy(src, dst.at[_slot(me)], local_sem).wait()

    return start, finish


def _allgather_side(a, relay=False):
    return ((a,), (jax.ShapeDtypeStruct((N_DEV,) + a.shape, a.dtype),),
            (pltpu.SemaphoreType.DMA((7,)), pltpu.SemaphoreType.DMA((7,)), pltpu.SemaphoreType.DMA),
            _allgather_relay_steps if relay else _allgather_steps)


def _allgather(a, name, relay=False):
    ins, outs, scratch, make = _allgather_side(a, relay)

    def body(a_ref, o_ref, *scr):
        start, finish = make((a_ref,), (o_ref,), scr)
        start()
        finish()

    any_spec = pl.BlockSpec(memory_space=pl.ANY)
    return pl.pallas_call(body, name=name, out_shape=outs[0], in_specs=[any_spec], out_specs=any_spec,
                          scratch_shapes=list(scratch))(a)


def _exchange_pair_steps(in_refs, out_refs, scratch):
    (src,), (dst,) = in_refs, out_refs
    send_sems, recv_sems = scratch
    x, y, c = _place()
    sibling = (x, y, 1 - c)
    slots = [_slot(sibling)] + [_slot((*chip, 1 - c)) for chip in _other_chips(x, y)]

    def copies():
        return [pltpu.make_async_remote_copy(
            src_ref=src.at[ps], dst_ref=dst.at[k], send_sem=send_sems.at[k], recv_sem=recv_sems.at[k],
            device_id=sibling, device_id_type=MESH) for k, ps in enumerate(slots)]

    def start():
        for cp in copies():
            cp.start()

    def finish():
        for cp in copies():
            cp.wait()

    return start, finish


def _exchange_pair_side(g):
    return ((g,), (jax.ShapeDtypeStruct((4,) + g.shape[1:], g.dtype),),
            (pltpu.SemaphoreType.DMA((4,)), pltpu.SemaphoreType.DMA((4,))), _exchange_pair_steps)


def _exchange_pair(g):
    ins, outs, scratch, make = _exchange_pair_side(g)

    def body(g_ref, r_ref, *scr):
        start, finish = make((g_ref,), (r_ref,), scr)
        start()
        finish()

    any_spec = pl.BlockSpec(memory_space=pl.ANY)
    return pl.pallas_call(body, name="exchange_pair", out_shape=outs[0], in_specs=[any_spec], out_specs=any_spec,
                          scratch_shapes=list(scratch))(g)


def _tiling(R, Cc, tile, by_cols):
    if by_cols:
        assert Cc % tile == 0
        return Cc // tile, (R, tile), lambda lead, i: (lead, 0, i)
    assert R % tile == 0
    return R // tile, (tile, Cc), lambda lead, i: (lead, i, 0)


def _pair_add(name, g, r1, slots, tile, by_cols=False, side=None):
    _, R, Cc = g.shape
    steps, blk, at = _tiling(R, Cc, tile, by_cols)
    side_ins, side_outs, side_scr, side_make = side if side is not None else ((), (), (), None)
    n_si, n_so = len(side_ins), len(side_outs)

    def kern(s_ref, a_ref, b_ref, *rest):
        o_ref = rest[n_si]
        if side_make is not None:
            start, finish = side_make(rest[:n_si], rest[n_si + 1:n_si + 1 + n_so], rest[n_si + 1 + n_so:])
            j, i = pl.program_id(0), pl.program_id(1)
            pl.when(jnp.logical_and(j == 0, i == 0))(start)
        o_ref[...] = (a_ref[...].astype(F32) + b_ref[...].astype(F32)).astype(o_ref.dtype)
        if side_make is not None:
            pl.when(jnp.logical_and(j == 2, i == steps - 1))(finish)

    any_spec = pl.BlockSpec(memory_space=pl.ANY)
    res = pl.pallas_call(
        kern, name=name,
        out_shape=[jax.ShapeDtypeStruct((3, R, Cc), BF16)] + list(side_outs),
        grid_spec=pltpu.PrefetchScalarGridSpec(
            num_scalar_prefetch=1, grid=(3, steps),
            in_specs=[pl.BlockSpec((1,) + blk, lambda j, i, s: at(s[j], i)),
                      pl.BlockSpec((1,) + blk, lambda j, i, s: at(1 + j, i))] + [any_spec] * n_si,
            out_specs=[pl.BlockSpec((1,) + blk, lambda j, i, s: at(j, i))] + [any_spec] * n_so,
            scratch_shapes=list(side_scr)),
        compiler_params=_cp(("arbitrary", "arbitrary")),
    )(slots, g, r1, *side_ins)
    return res if side is not None else res[0]


def _axis_neighbours():
    x, y, c = _place()
    flip = lambda a, bit: a + bit - 2 * a * bit
    return (flip(x, c), flip(y, 1 - c), c), (flip(x, 1 - c), flip(y, c), c), c


def _exchange_first_steps(in_refs, out_refs, scratch):
    pairs = list(zip(in_refs, out_refs))
    send_sems, recv_sems = scratch
    first, _, c = _axis_neighbours()

    def copies():
        return [pltpu.make_async_remote_copy(
            src_ref=src.at[j], dst_ref=dst.at[k], send_sem=send_sems.at[t, k], recv_sem=recv_sems.at[t, k],
            device_id=first, device_id_type=MESH)
            for t, (src, dst) in enumerate(pairs) for k, j in enumerate((1 - c, 2))]

    def start():
        for cp in copies():
            cp.start()

    def finish():
        for cp in copies():
            cp.wait()

    return start, finish


def _exchange_first_side(*ss):
    n = len(ss)
    return (ss, tuple(jax.ShapeDtypeStruct((2,) + s.shape[1:], s.dtype) for s in ss),
            (pltpu.SemaphoreType.DMA((n, 2)), pltpu.SemaphoreType.DMA((n, 2))), _exchange_first_steps)


def _axis_add(name, s, t1, core, tile, by_cols=False):
    _, R, Cc = s.shape
    steps, blk, at = _tiling(R, Cc, tile, by_cols)

    def kern(c_ref, a_ref, b_ref, o_ref):
        o_ref[...] = (a_ref[...].astype(F32) + b_ref[...].astype(F32)).astype(o_ref.dtype)

    return pl.pallas_call(
        kern, name=name,
        out_shape=jax.ShapeDtypeStruct((1, R, Cc), BF16),
        grid_spec=pltpu.PrefetchScalarGridSpec(
            num_scalar_prefetch=1, grid=(steps,),
            in_specs=[pl.BlockSpec((1,) + blk, lambda i, cr: at(cr[0], i)),
                      pl.BlockSpec((1,) + blk, lambda i, cr: at(1, i))],
            out_specs=pl.BlockSpec((1,) + blk, lambda i, cr: at(0, i))),
        compiler_params=_cp(("arbitrary",)),
    )(core, s, t1)


def _exchange_second_steps(in_refs, out_refs, scratch):
    send_sems, recv_sems = scratch
    _, second, _ = _axis_neighbours()

    def copies():
        return [pltpu.make_async_remote_copy(src_ref=src, dst_ref=dst, send_sem=send_sems.at[t], recv_sem=recv_sems.at[t],
                                             device_id=second, device_id_type=MESH)
                for t, (src, dst) in enumerate(zip(in_refs, out_refs))]

    def start():
        for cp in copies():
            cp.start()

    def finish():
        for cp in copies():
            cp.wait()

    return start, finish


def _exchange_second_side(*ps):
    n = len(ps)
    return (ps, tuple(jax.ShapeDtypeStruct(p.shape, p.dtype) for p in ps),
            (pltpu.SemaphoreType.DMA((n,)), pltpu.SemaphoreType.DMA((n,))), _exchange_second_steps)


def _adamw(name, w, m, v, parts, tile, by_cols=False):
    R, Cc = w.shape
    steps, blk_shape, at = _tiling(R, Cc, tile, by_cols)
    n_parts = len(parts)

    def kern(*refs):
        w_ref, m_ref, v_ref = refs[:3]
        g = None
        for r_ref, (_, n) in zip(refs[3:3 + n_parts], parts):
            for s in range(n):
                term = r_ref[s].astype(F32)
                g = term if g is None else g + term
        g_out, d_out, m_out, v_out = refs[3 + n_parts:]
        mn = ADAM_B1 * m_ref[...] + (1.0 - ADAM_B1) * g
        vn = ADAM_B2 * v_ref[...] + (1.0 - ADAM_B2) * (g * g)
        m_hat = mn / (1.0 - ADAM_B1 ** ADAM_STEP)
        v_hat = vn / (1.0 - ADAM_B2 ** ADAM_STEP)
        g_out[...] = g
        d_out[...] = -ADAM_LR * (m_hat / (jnp.sqrt(v_hat) + ADAM_EPS) + ADAM_WD * w_ref[...])
        m_out[...] = mn
        v_out[...] = vn

    blk = pl.BlockSpec(blk_shape, lambda i: at(0, i)[1:])
    return pl.pallas_call(
        kern, name=name,
        out_shape=[jax.ShapeDtypeStruct((R, Cc), F32)] * 4,
        grid=(steps,),
        in_specs=[blk] * 3 + [pl.BlockSpec((n,) + blk_shape, lambda i: at(0, i)) for (_, n) in parts],
        out_specs=[blk] * 4,
        compiler_params=_cp(("arbitrary",)),
    )(w, m, v, *[a for (a, _) in parts])


def _assemble_columns(blocks, pieces, zeros, width):
    _, R, Cc = blocks.shape
    tr = min(256, R)

    def kern(b_ref, o_ref):
        for col, n in zeros:
            o_ref[:, col:col + n] = jnp.zeros((tr, n), o_ref.dtype)
        for col, d, lo, n in pieces:
            o_ref[:, col:col + n] = b_ref[d, :, lo:lo + n]

    return pl.pallas_call(
        kern, name="assemble_w_in",
        out_shape=jax.ShapeDtypeStruct((R, width), blocks.dtype),
        grid=(R // tr,),
        in_specs=[pl.BlockSpec((N_DEV, tr, Cc), lambda i: (0, i, 0))],
        out_specs=pl.BlockSpec((tr, width), lambda i: (i, 0)),
        compiler_params=_cp(("parallel",)),
    )(blocks)


def _split_rows(x, pieces, rows, side):
    _, Cc = x.shape
    tc = min(256, Cc)
    side_ins, side_outs, side_scr, side_make = side
    n_si, n_so = len(side_ins), len(side_outs)

    def kern(x_ref, *rest):
        o_ref = rest[n_si]
        start, finish = side_make(rest[:n_si], rest[n_si + 1:n_si + 1 + n_so], rest[n_si + 1 + n_so:])
        pl.when(pl.program_id(0) == 0)(start)
        for d, lo, row, n in pieces:
            o_ref[d, lo:lo + n, :] = x_ref[row:row + n, :]
        pl.when(pl.program_id(0) == Cc // tc - 1)(finish)

    any_spec = pl.BlockSpec(memory_space=pl.ANY)
    return pl.pallas_call(
        kern, name="split_w_in_grad",
        out_shape=[jax.ShapeDtypeStruct((N_DEV, rows, Cc), x.dtype)] + list(side_outs),
        grid=(Cc // tc,),
        in_specs=[pl.BlockSpec((x.shape[0], tc), lambda i: (0, i))] + [any_spec] * n_si,
        out_specs=[pl.BlockSpec((N_DEV, rows, tc), lambda i: (0, 0, i))] + [any_spec] * n_so,
        scratch_shapes=list(side_scr),
        compiler_params=_cp(("arbitrary",)),
    )(x, *side_ins)


def _pad_cols(a, w):
    return jnp.pad(a, ((0, 0), (0, w - a.shape[1])))


def _pad_rows(a, r):
    return jnp.pad(a, ((0, r - a.shape[0]), (0, 0)))


def _pack_b(pf, pr, wo, w2, a2, rows):
    body = jnp.concatenate([pf, pr, wo.reshape(2048, 256), jnp.concatenate([w2, a2], axis=1)], axis=0)
    return _pad_rows(body, rows)


def kernel(x, norm_gain, w_in, fox_forget_bias, rwkv_shift_mix, rwkv_w0, rwkv_w2, rwkv_a0, rwkv_a2, rwkv_k_k, rwkv_k_a, rwkv_r_k, rwkv_ln_w, rwkv_ln_b, w_proj_fox, w_proj_rwkv, w_out, final_norm_gain, loss_target, m_norm_gain, m_w_in, m_fox_forget_bias, m_rwkv_shift_mix, m_rwkv_w0, m_rwkv_w2, m_rwkv_a0, m_rwkv_a2, m_rwkv_k_k, m_rwkv_k_a, m_rwkv_r_k, m_rwkv_ln_w, m_rwkv_ln_b, m_w_proj_fox, m_w_proj_rwkv, m_w_out, m_final_norm_gain, v_norm_gain, v_w_in, v_fox_forget_bias, v_rwkv_shift_mix, v_rwkv_w0, v_rwkv_w2, v_rwkv_a0, v_rwkv_a2, v_rwkv_k_k, v_rwkv_k_a, v_rwkv_r_k, v_rwkv_ln_w, v_rwkv_ln_b, v_w_proj_fox, v_w_proj_rwkv, v_w_out, v_final_norm_gain):
    T, D = x.shape[1], x.shape[2]
    assert D == 2048 and T % LANE == 0
    NI = w_in.shape[2]
    IN = N_DEV * NI
    RB = 4224
    x2 = x[0]
    lt2 = loss_target[0]
    me = _slot(_place())

    tb = min(256, T)
    tbh = min(128, T)
    h, wa = _rowcall("rms_fwd", T, tb, [_rows(x2, tb, D), _whole(norm_gain)],
                     lambda i, xv, g: ([_rms_math(xv, g)], []), [(D, BF16)],
                     side=_allgather_side(w_in[0].astype(BF16), relay=True))
    packed_own = _pack_b(w_proj_fox[0], w_proj_rwkv[0], w_out[0], rwkv_w2[0], rwkv_a2[0], RB).astype(BF16)
    sections = [(0, 4096, 0), (4104, 4096, 4096), (8392, 4096, 8192), (4096, 8, 12288), (8200, 96, 12544), (8296, 96, 12672)]
    NP = 12800
    pieces, zeros, at_col = [], [], 0
    for lo, width, pad_lo in sections:
        if pad_lo > at_col:
            zeros.append((at_col, pad_lo - at_col))
        col = lo
        while col < lo + width:
            d = col // NI
            stop = min(lo + width, (d + 1) * NI)
            pieces.append((pad_lo + col - lo, d, col - d * NI, stop - col))
            col = stop
        at_col = pad_lo + width
    zeros.append((at_col, NP - at_col))
    w_pad = _assemble_columns(wa, pieces, zeros, NP)
    F_CB, LORA_CB = 96, 49

    mu = rwkv_shift_mix
    mu_main = mu[:, 0:4096]
    mu_lora = jnp.concatenate([_pad_cols(mu[:, 4096:4192], LANE), _pad_cols(mu[:, 4192:4288], LANE)], axis=1)
    bias_pad = _pad_cols(fox_forget_bias, LANE)
    rk_flat = rwkv_r_k.reshape(1, 1024)
    gf = final_norm_gain.reshape(1, D)

    u, wb = _mm(h, w_pad, tm=1024, tn=1280, name="mm_in", side=_allgather_side(packed_own))
    wpf = wb[:, 0:1024, :].transpose(1, 0, 2).reshape(1024, D)
    wpr = wb[:, 1024:2048, :].transpose(1, 0, 2).reshape(1024, D)
    wo = wb[:, 2048:4096, :].reshape(N_DEV * 256, D)
    w2p = _pad_rows(wb[:, 4096:4192, 0:128].transpose(1, 0, 2).reshape(96, 1024).astype(F32), LANE)
    a2p = _pad_rows(wb[:, 4096:4192, 128:256].transpose(1, 0, 2).reshape(96, 1024).astype(F32), LANE)

    c8 = _gates_fwd(u, bias_pad, T, F_CB)
    c3 = c8.reshape(8, 1, T)
    o_raw, o_a, lse = _attn_fwd(u, c3, T)

    def shift_body(i, um, hm, ul, hl, mm_, ml):
        outs = []
        for uv, hv, mv in ((um, hm, mm_), (ul, hl, ml)):
            hv = jnp.where(i == 0, 0.0, hv)
            prev = pltpu.roll(jnp.concatenate([hv, uv], axis=0), 1, 0)[SUB:]
            outs.append(uv + (prev - uv) * mv)
        return outs, []

    def halo_prev(arr, w, cb):
        return (arr, (SUB, w), lambda i: (jnp.maximum(i * (tbh // SUB) - 1, 0), cb))

    xs, xl = _rowcall("rwkv_shift_fwd", T, tbh,
                      [_rows(u, tbh, 4096, 1), halo_prev(u, 4096, 1), _rows(u, tbh, 256, LORA_CB), halo_prev(u, 256, LORA_CB),
                       _whole(mu_main), _whole(mu_lora)],
                      shift_body, [(4096, F32), (256, F32)])

    prep_par = [_whole(rwkv_w0), _whole(rwkv_a0), _whole(rwkv_k_k), _whole(rwkv_k_a), _whole(w2p), _whole(a2p)]
    prep_rows = [_rows(xs, tbh, 1024, 1), _rows(xl, tbh, LANE, 0), _rows(xl, tbh, LANE, 1)]
    lw, kp, an, bb = _rowcall("rwkv_prep_fwd", T, tbh, prep_rows + prep_par,
                              lambda i, *a: (list(_prep_math(*a)), []), [(1024, F32)] * 4)
    y, states, invs = _scan_fwd(xs, lw, kp, an, bb, T)
    post_rows = [_rows(y, tbh, 1024), _rows(xs, tbh, 1024, 0), _rows(kp, tbh, 1024), _rows(xs, tbh, 1024, 2), _rows(xs, tbh, 1024, 3)]
    post_par = [_whole(rwkv_ln_w), _whole(rwkv_ln_b), _whole(rk_flat)]
    (o_b,) = _rowcall("rwkv_post_fwd", T, tbh, post_rows + post_par,
                      lambda i, *a: ([_post_math(*a)], []), [(1024, BF16)])

    pa = _mm(o_a, wpf, out_dtype=BF16, name="mm_proj_fox")
    pb = _mm(o_b, wpr, out_dtype=BF16, name="mm_proj_rwkv")
    merge_rows = [_rows(u, tb, D, 4), _rows(u, tb, D, 5), _rows(pa, tb, D), _rows(pb, tb, D)]
    (mg,) = _rowcall("merge_fwd", T, tb, merge_rows, lambda i, *a: ([_merge_math(*a)], []), [(D, BF16)])

    def head_body(i, xv, mgv, wov, ltv, g):
        out = xv + _nn(mgv, wov)
        r = lax.rsqrt(jnp.mean(out * out, axis=-1, keepdims=True) + RMS_EPS)
        yn = out * r
        err = yn * g - ltv
        loss = 0.5 * jnp.sum(jnp.sum(err * err, axis=-1, keepdims=True), axis=0, keepdims=True) / D
        dyv = err / D
        dyn = dyv * g
        dout = r * (dyn - yn * jnp.mean(dyn * yn, axis=-1, keepdims=True))
        return [dout], [loss, jnp.sum(dyv * yn, axis=0, keepdims=True)]

    dout, loss_p, dgf_p = _rowcall("mm_out_loss_head", T, tb,
                                   [_rows(x2, tb, D), _rows(mg, tb, D), _whole(wo), _rows(lt2, tb, D), _whole(gf)],
                                   head_body, [(D, F32)], [(1, 1), (1, D)])

    dm = _mm(dout, wo, tb=True, name="mm_out_dx")
    dwo = _mm(mg, dout, ta=True, out_dtype=BF16, name="mm_out_dw")

    def merge_bwd_body(i, ga, gb, pav, pbv, dmv):
        _, vjp = jax.vjp(_merge_math, ga, gb, pav.astype(F32), pbv.astype(F32))
        dga, dgb, dpa, dpb = vjp(dmv)
        return [dga, dgb, dpa, dpb], []

    dga, dgb, dpa, dpb = _rowcall("merge_bwd", T, tb, merge_rows + [_rows(dm, tb, D)], merge_bwd_body,
                                  [(D, BF16), (D, BF16), (D, BF16), (D, BF16)])
    doa = _mm(dpa, wpf, tb=True, name="mm_proj_fox_dx")
    dwpf = _mm(o_a, dpa, ta=True, out_dtype=BF16, name="mm_proj_fox_dw")
    dob = _mm(dpb, wpr, tb=True, name="mm_proj_rwkv_dx")
    dwpr = _mm(o_b, dpb, ta=True, out_dtype=BF16, name="mm_proj_rwkv_dw")

    do_b, dza, delta128 = _rowcall("fox_attn_pre", T, tb, [_rows(doa, tb, 1024), _rows(u, tb, 1024, 3), _rows(o_raw, tb, 1024)],
                                   lambda i, *a: (list(_attn_pre_math(*a)), []), [(1024, BF16), (1024, F32), (LANE, F32)])
    delta = delta128[:, 0:8].T.reshape(8, T, 1)
    dq, dk, dv, dc3 = _attn_bwd(u, c3, lse, do_b, delta, T)
    dfl, dbias_p = _gates_bwd(dc3.reshape(8, T), u, bias_pad, T, F_CB)

    def post_bwd_body(i, yv, rv, kpv, vv, zv, lnw, lnb, rkv, dobv):
        _, vjp = jax.vjp(_post_math, yv, rv, kpv, vv, zv, lnw, lnb, rkv)
        dy_, dr_, dkp_, dv_, dz_, dlnw, dlnb, drk = vjp(dobv)
        return [dy_, dr_, dkp_, dv_, dz_], [dlnw, dlnb, drk]

    dy_s, dr_p, dkp_p, dv_p, dzb, dlnw_p, dlnb_p, drk_p = _rowcall(
        "rwkv_post_bwd", T, tbh, post_rows + post_par + [_rows(dob, tbh, 1024)], post_bwd_body,
        [(1024, F32)] * 5, [(1, 1024)] * 3)
    dr_s, dlw, dkp_s, dv_s, dan, dbb = _scan_bwd(xs, lw, kp, an, bb, states, invs, dy_s, T)

    def prep_bwd_body(i, xk, xwd, xad, w0, a0, kkw, kaw, w2v, a2v, dlw_, dkp1, dkp2, dan_, dbb_, dr1, dr2, dv1, dv2, dz_):
        _, vjp = jax.vjp(_prep_math, xk, xwd, xad, w0, a0, kkw, kaw, w2v, a2v)
        dxk, dxwd, dxad, dw0, da0, dkk, dka, dw2, da2 = vjp((dlw_, dkp1 + dkp2, dan_, dbb_))
        return [[dr1 + dr2, dxk, dv1 + dv2, dz_], [dxwd, dxad]], [dw0, da0, dkk, dka, dw2, da2]

    cots = [dlw, dkp_s, dkp_p, dan, dbb, dr_s, dr_p, dv_s, dv_p, dzb]
    dxs, dxl, dw0_p, da0_p, dkk_p, dka_p, dw2_p, da2_p = _rowcall(
        "rwkv_prep_bwd", T, tbh, prep_rows + prep_par + [_rows(c_, tbh, 1024) for c_ in cots], prep_bwd_body,
        [(4096, F32), (256, F32)], [(1, 1024)] * 4 + [(LANE, 1024)] * 2)

    def shift_bwd_body(i, dm_, hm, dl_, hl, um, pm, ul, pl_, mm_, ml):
        last = i == T // tbh - 1
        outs, accs = [], []
        for dv_, hv, uv, pv, mv in ((dm_, hm, um, pm, mm_), (dl_, hl, ul, pl_, ml)):
            hv = jnp.where(last, 0.0, hv)
            nxt = pltpu.roll(jnp.concatenate([dv_, hv], axis=0), tbh + SUB - 1, 0)[:tbh]
            pv = jnp.where(i == 0, 0.0, pv)
            prev = pltpu.roll(jnp.concatenate([pv, uv], axis=0), 1, 0)[SUB:]
            outs.append(dv_ * (1.0 - mv) + nxt * mv)
            accs.append(jnp.sum(dv_ * (prev - uv), axis=0, keepdims=True))
        return outs, accs

    def halo_next(arr, w, cb):
        last_blk = T // SUB - 1
        return (arr, (SUB, w), lambda i: (jnp.minimum((i + 1) * (tbh // SUB), last_blk), cb))

    du_b, du_l, dmu_main_p, dmu_lora_p = _rowcall(
        "rwkv_shift_bwd", T, tbh,
        [_rows(dxs, tbh, 4096), halo_next(dxs, 4096, 0), _rows(dxl, tbh, 256), halo_next(dxl, 256, 0),
         _rows(u, tbh, 4096, 1), halo_prev(u, 4096, 1), _rows(u, tbh, 256, LORA_CB), halo_prev(u, 256, LORA_CB),
         _whole(mu_main), _whole(mu_lora)],
        shift_bwd_body, [(4096, BF16), (256, BF16)], [(1, 4096), (1, 256)])

    lora_g = jnp.concatenate([dw2_p[:96].reshape(96, N_DEV, 128).transpose(1, 0, 2),
                              da2_p[:96].reshape(96, N_DEV, 128).transpose(1, 0, 2)], axis=2).astype(BF16)
    gb = jnp.concatenate([dwpf.reshape(1024, N_DEV, 256).transpose(1, 0, 2),
                          dwpr.reshape(1024, N_DEV, 256).transpose(1, 0, 2),
                          dwo.reshape(N_DEV, 2048, 256), lora_g, jnp.zeros((N_DEV, RB - 4192, 256), BF16)], axis=1)
    xx, yy, cc = _place()
    chip_slots = jnp.stack([_slot((*chip, cc)) for chip in _other_chips(xx, yy)]).astype(jnp.int32)
    core = jnp.stack([cc]).astype(jnp.int32)
    du, r1b = _rowcall("assemble_du", T, tb,
                       [_rows(a_, tb, a_.shape[1]) for a_ in (dq, dk, dv, dza, du_b, dga, dgb, dfl, du_l)],
                       lambda i, *a: ([list(a[:8]) + [jnp.zeros((tb, LANE), BF16), a[8]]], []), [(NP, BF16)],
                       side=_exchange_pair_side(gb))
    sb = _pair_add("pair_add_packed", gb, r1b, chip_slots, RB // 3)
    dw_pad_t, t1b = _mm(du, h, ta=True, out_dtype=BF16, tm=1280, tn=1024, name="mm_in_dw",
                        side=_exchange_first_side(sb))
    pb = _axis_add("axis_add_packed", sb, t1b, core, RB // 3)
    ga, t2b = _split_rows(dw_pad_t, [(d, lo, pad_col, n) for pad_col, d, lo, n in pieces], NI,
                          _exchange_second_side(pb))
    r1a = _exchange_pair(ga)
    sa = _pair_add("pair_add_w_in", ga, r1a, chip_slots, 512, by_cols=True)
    ga_own = lax.dynamic_index_in_dim(ga, me, 0, keepdims=True)
    gb_own = lax.dynamic_index_in_dim(gb, me, 0, keepdims=True)
    dh, t1a = _mm(du, w_pad, tb=True, tm=1024, tn=1024, tk=NP // 10, name="mm_in_dx", side=_exchange_first_side(sa))
    pa = _axis_add("axis_add_w_in", sa, t1a, core, 512, by_cols=True)

    def rms_bwd_body(i, xv, g, dhv, doutv):
        _, vjp = jax.vjp(_rms_math, xv, g)
        dx_, dg_ = vjp(dhv)
        return [dx_ + doutv], [dg_]

    grad_x2, dng_p, t2a = _rowcall(
        "rms_bwd", T, tb, [_rows(x2, tb, D), _whole(norm_gain), _rows(dh, tb, D), _rows(dout, tb, D)],
        rms_bwd_body, [(D, F32)], [(1, D)], side=_exchange_second_side(pa))

    w_in_outs = _adamw("adamw_w_in", w_in[0].T, m_w_in[0].T, v_w_in[0].T, [(ga_own, 1), (r1a, 1), (t1a, 1), (t2a, 1)], 256, by_cols=True)
    g_in, d_in, m_in, v_in = [o.T for o in w_in_outs]

    dmu = jnp.concatenate([dmu_main_p, dmu_lora_p[:, 0:96], dmu_lora_p[:, 128:224]], axis=1)
    small_parts = [dng_p, dbias_p[:, 0:8], dmu, dw0_p, da0_p, dkk_p, dka_p, drk_p, dlnw_p, dlnb_p, dgf_p, loss_p]
    SR = 128
    small = _pad_cols(jnp.concatenate(small_parts, axis=1), SR * LANE).reshape(SR, LANE)
    rs = _allgather(small, "allgather_small")
    pk = lambda pf, pr, wo_, w2_, a2_: _pack_b(pf[0], pr[0], wo_[0], w2_[0], a2_[0], RB)
    outs_b = _adamw("adamw_packed", pk(w_proj_fox, w_proj_rwkv, w_out, rwkv_w2, rwkv_a2),
                    pk(m_w_proj_fox, m_w_proj_rwkv, m_w_out, m_rwkv_w2, m_rwkv_a2),
                    pk(v_w_proj_fox, v_w_proj_rwkv, v_w_out, v_rwkv_w2, v_rwkv_a2), [(gb_own, 1), (r1b, 1), (t1b, 1), (t2b, 1)], RB // 3)

    def pack_small(ng, fb, sm, w0, a0, kk_, ka_, rk_, lnw, lnb, fg):
        parts = [ng, fb, sm, w0, a0, kk_, ka_, rk_.reshape(1, 1024), lnw, lnb, fg.reshape(1, D), jnp.zeros((1, 1), F32)]
        return _pad_cols(jnp.concatenate(parts, axis=1), SR * LANE).reshape(SR, LANE)

    outs_s = _adamw("adamw_small",
                    pack_small(norm_gain, fox_forget_bias, rwkv_shift_mix, rwkv_w0, rwkv_a0, rwkv_k_k, rwkv_k_a, rwkv_r_k,
                               rwkv_ln_w, rwkv_ln_b, final_norm_gain),
                    pack_small(m_norm_gain, m_fox_forget_bias, m_rwkv_shift_mix, m_rwkv_w0, m_rwkv_a0, m_rwkv_k_k, m_rwkv_k_a,
                               m_rwkv_r_k, m_rwkv_ln_w, m_rwkv_ln_b, m_final_norm_gain),
                    pack_small(v_norm_gain, v_fox_forget_bias, v_rwkv_shift_mix, v_rwkv_w0, v_rwkv_a0, v_rwkv_k_k, v_rwkv_k_a,
                               v_rwkv_r_k, v_rwkv_ln_w, v_rwkv_ln_b, v_final_norm_gain),
                    [(rs, N_DEV)], SR)

    def unpack_b(pkd):
        return dict(w_proj_fox=pkd[0:1024][None], w_proj_rwkv=pkd[1024:2048][None], w_out=pkd[2048:4096].reshape(1, 256, D),
                    rwkv_w2=pkd[4096:4192, 0:128][None], rwkv_a2=pkd[4096:4192, 128:256][None])

    def unpack_s(pkd):
        flat = pkd.reshape(1, SR * LANE)
        names = [("norm_gain", D), ("fox_forget_bias", 8), ("rwkv_shift_mix", 4288), ("rwkv_w0", 1024), ("rwkv_a0", 1024),
                 ("rwkv_k_k", 1024), ("rwkv_k_a", 1024), ("rwkv_r_k", 1024), ("rwkv_ln_w", 1024), ("rwkv_ln_b", 1024),
                 ("final_norm_gain", D), ("loss", 1)]
        out, off = {}, 0
        for nm, n in names:
            out[nm] = flat[:, off:off + n]
            off += n
        out["rwkv_r_k"] = out["rwkv_r_k"].reshape(1, 16, 64)
        out["final_norm_gain"] = out["final_norm_gain"].reshape(D)
        return out

    order = ["norm_gain", "w_in", "fox_forget_bias", "rwkv_shift_mix", "rwkv_w0", "rwkv_w2", "rwkv_a0", "rwkv_a2", "rwkv_k_k",
             "rwkv_k_a", "rwkv_r_k", "rwkv_ln_w", "rwkv_ln_b", "w_proj_fox", "w_proj_rwkv", "w_out", "final_norm_gain"]
    result = []
    loss = None
    for kind, big in enumerate((g_in, d_in, m_in, v_in)):
        d = {**unpack_b(outs_b[kind]), **unpack_s(outs_s[kind]), "w_in": big[None]}
        if kind == 0:
            loss = d["loss"].reshape(())
        result += [d[n] for n in order]
    return (loss, grad_x2[None], *result)
```

```python
import functools

import jax
import jax.numpy as jnp
from jax import lax
from jax.experimental import pallas as pl
from jax.experimental.pallas import tpu as pltpu

F32 = jnp.float32
BF16 = jnp.bfloat16
HI = lax.Precision.HIGHEST
H3 = lax.Precision.HIGH
MESH = pl.DeviceIdType.MESH

FOX_HD = 128
RW_HD = 64
RMS_EPS = 1e-6
GN_EPS = 64e-5
L2_EPS = 1e-12
ADAM_LR = 0.001
ADAM_B1 = 0.9
ADAM_B2 = 0.999
ADAM_EPS = 1e-08
ADAM_WD = 0.01
ADAM_STEP = 10

LANE = 128
SUB = 8
VMEM_LIMIT = 56 * 1024 * 1024
N_DEV = 8
CHUNK = 128
SCAN_GROUP = 1
SCAN_PAIRS = 8
PS = None
NEG = -1e30


def _scan_shape(T):
    c = min(CHUNK, T)
    return c, min(SCAN_GROUP, T // c)


def _cp(sem=None):
    return pltpu.CompilerParams(dimension_semantics=sem, vmem_limit_bytes=VMEM_LIMIT)


def _sigmoid(x):
    return jax.nn.sigmoid(x)


def _softplus(x):
    return jnp.maximum(x, 0.0) + jnp.log(1.0 + jnp.exp(-jnp.abs(x)))


def _nn(a, b, prec=None):
    return lax.dot_general(a, b, (((1,), (0,)), ((), ())), precision=prec, preferred_element_type=F32)


def _nt(a, b, prec=None):
    return lax.dot_general(a, b, (((1,), (1,)), ((), ())), precision=prec, preferred_element_type=F32)


def _tn(a, b, prec=None):
    return lax.dot_general(a, b, (((0,), (0,)), ((), ())), precision=prec, preferred_element_type=F32)


def _iota2(shape, dim):
    return lax.broadcasted_iota(jnp.int32, shape, dim)


def _seg_sum(x):
    r = _iota2((LANE, LANE), 0) // RW_HD
    c = _iota2((LANE, LANE), 1) // RW_HD
    bd = (r == c).astype(F32)
    parts = [_nn(x[:, j * LANE:(j + 1) * LANE], bd, H3) for j in range(x.shape[1] // LANE)]
    return parts[0] if len(parts) == 1 else jnp.concatenate(parts, axis=1)


def _mm(a, b, *, ta=False, tb=False, out_dtype=F32, tm=1024, tn=1024, tk=None, name, side=None):
    assert not (ta and tb)
    K, M = a.shape if ta else a.shape[::-1]
    N = b.shape[0] if tb else b.shape[1]
    tm, tn = min(tm, M), min(tn, N)
    tk = K if tk is None else tk
    nk = K // tk
    assert M % tm == 0 and N % tn == 0 and K % tk == 0
    gi, gj = M // tm, N // tn
    a_spec = pl.BlockSpec((tk, tm), lambda i, j, k: (k, i)) if ta else pl.BlockSpec((tm, tk), lambda i, j, k: (i, k))
    b_spec = pl.BlockSpec((tn, tk), lambda i, j, k: (j, k)) if tb else pl.BlockSpec((tk, tn), lambda i, j, k: (k, j))
    side_ins, side_outs, side_scr, side_make = side if side is not None else ((), (), (), None)
    n_si, n_so = len(side_ins), len(side_outs)
    n_acc = 0 if nk == 1 else 1

    def body(*refs):
        a_ref, b_ref = refs[:2]
        o_ref = refs[2 + n_si]
        scr = refs[3 + n_si + n_so:]
        k = pl.program_id(2)
        if side_make is not None:
            start, finish = side_make(refs[2:2 + n_si], refs[3 + n_si:3 + n_si + n_so], scr[n_acc:])
            first = jnp.logical_and(jnp.logical_and(pl.program_id(0) == 0, pl.program_id(1) == 0), k == 0)
            last = jnp.logical_and(jnp.logical_and(pl.program_id(0) == gi - 1, pl.program_id(1) == gj - 1), k == nk - 1)
            pl.when(first)(start)
        av = a_ref[...].astype(BF16)
        bv = b_ref[...].astype(BF16)
        p = _tn(av, bv) if ta else _nt(av, bv) if tb else _nn(av, bv)
        if nk == 1:
            o_ref[...] = p.astype(out_dtype)
        else:
            acc_ref = scr[0]

            @pl.when(k == 0)
            def _():
                acc_ref[...] = p

            @pl.when(k > 0)
            def _():
                acc_ref[...] += p

            @pl.when(k == nk - 1)
            def _():
                o_ref[...] = acc_ref[...].astype(out_dtype)
        if side_make is not None:
            pl.when(last)(finish)

    any_spec = pl.BlockSpec(memory_space=pl.ANY)
    res = pl.pallas_call(
        body, name=name,
        out_shape=[jax.ShapeDtypeStruct((M, N), out_dtype)] + list(side_outs),
        grid=(gi, gj, nk),
        in_specs=[a_spec, b_spec] + [any_spec] * n_si,
        out_specs=[pl.BlockSpec((tm, tn), lambda i, j, k: (i, j))] + [any_spec] * n_so,
        scratch_shapes=([] if nk == 1 else [pltpu.VMEM((tm, tn), F32)]) + list(side_scr),
        compiler_params=_cp(("arbitrary",) * 3 if side is not None else ("parallel", "parallel", "arbitrary")),
    )(a, b, *side_ins)
    return res if side is not None else res[0]


def _rows(arr, tb, w, cb=0):
    return (arr, (tb, w), lambda i: (i, cb))


def _whole(arr):
    nd = arr.ndim
    return (arr, arr.shape, lambda i: (0,) * nd)


def _rowcall(name, T, tb, ins, body, outs, accs=(), side=None):
    n_in, n_out, n_acc = len(ins), len(outs), len(accs)
    side_ins, side_outs, side_scr, side_make = side if side is not None else ((), (), (), None)
    n_si = len(side_ins)

    def kern(*refs):
        i = pl.program_id(0)
        if side_make is not None:
            side_refs = refs[n_in + n_si + n_out + n_acc:]
            start, finish = side_make(refs[n_in:n_in + n_si], side_refs[:len(side_outs)], side_refs[len(side_outs):])
            pl.when(i == 0)(start)
            refs = refs[:n_in] + refs[n_in + n_si:]
        vals = [r[...] for r in refs[:n_in]]
        ro, ao = body(i, *vals)
        for r, v in zip(refs[n_in:n_in + n_out], ro):
            if isinstance(v, (list, tuple)):
                off = 0
                for piece in v:
                    w = piece.shape[1]
                    r[:, off:off + w] = piece.astype(r.dtype)
                    off += w
            else:
                r[...] = v.astype(r.dtype)
        if accs:
            acc_refs = refs[n_in + n_out:n_in + n_out + n_acc]

            @pl.when(i == 0)
            def _():
                for r in acc_refs:
                    r[...] = jnp.zeros(r.shape, F32)

            for r, v in zip(acc_refs, ao):
                r[...] += v
        if side_make is not None:
            pl.when(i == T // tb - 1)(finish)

    any_spec = pl.BlockSpec(memory_space=pl.ANY)
    out_shape = [jax.ShapeDtypeStruct((T, w), dt) for (w, dt) in outs] + [jax.ShapeDtypeStruct(s, F32) for s in accs]
    out_specs = [pl.BlockSpec((tb, w), lambda i: (i, 0)) for (w, dt) in outs] + [pl.BlockSpec(s, lambda i: (0, 0)) for s in accs]
    res = pl.pallas_call(
        kern, name=name,
        out_shape=out_shape + list(side_outs),
        grid=(T // tb,),
        in_specs=[pl.BlockSpec(bs, im) for (_, bs, im) in ins] + [any_spec] * n_si,
        out_specs=out_specs + [any_spec] * len(side_outs),
        scratch_shapes=list(side_scr),
        compiler_params=_cp(("arbitrary",)),
    )(*[a for (a, _, _) in ins], *side_ins)
    return res


def _rms_math(x, g):
    r = lax.rsqrt(jnp.mean(x * x, axis=-1, keepdims=True) + RMS_EPS)
    return x * r * g


def _merge_math(ga, gb, pa, pb):
    return _sigmoid(ga) * pa + _sigmoid(gb) * pb


def _prep_math(xk, xwd, xad, w0, a0, kk_w, ka_w, w2p, a2p):
    z = w0 + _nn(jnp.tanh(xwd), w2p, H3)
    w = -_softplus(-z) - 0.5
    lw = -jnp.exp(w)
    ag = _sigmoid(a0 + _nn(xad, a2p, H3))
    p = xk * kk_w
    n = jnp.maximum(jnp.sqrt(_seg_sum(p * p)), L2_EPS)
    kk = p / n
    kp = xk * (1.0 + (ag - 1.0) * ka_w)
    return lw, kp, -kk, kk * ag


def _post_math(y, r, kp, v, z, lnw, lnb, rk):
    inv = 1.0 / RW_HD
    mu = _seg_sum(y) * inv
    d = y - mu
    var = _seg_sum(d * d) * inv
    yn = d * lax.rsqrt(var + GN_EPS) * lnw + lnb
    bonus = _seg_sum(r * kp * rk) * v
    return (yn + bonus) * (z * _sigmoid(z))


def _neumann(ms, depth):
    eye = (_iota2(ms[0].shape, 0) == _iota2(ms[0].shape, 1)).astype(F32)
    width = ms[0].shape[1]
    mp = [_nn(m, m, PS) for m in ms]
    inv = [eye + m for m in ms]
    n = 2
    while n < depth:
        last = 2 * n >= depth
        for i in range(len(ms)):
            if last:
                inv[i] = inv[i] + _nn(mp[i], inv[i], PS)
            else:
                z = _nn(mp[i], jnp.concatenate([mp[i], inv[i]], axis=1), PS)
                mp[i], inv[i] = z[:, :width], inv[i] + z[:, width:]
        n *= 2
    return inv


@functools.partial(jax.custom_vjp, nondiff_argnums=(1,))
def _unit_inverses(ms, depth):
    if depth <= RW_HD:
        return tuple(_neumann(list(ms), depth))
    h, n = depth // 2, len(ms)
    cat = jnp.concatenate
    z = jnp.zeros((h, h), F32)

    def heads(m, r, c):
        b0 = m[r * h:(r + 1) * h, c * h:(c + 1) * h]
        b1 = m[depth + r * h:depth + (r + 1) * h, depth + c * h:depth + (c + 1) * h]
        return cat([cat([b0, z], axis=1), cat([z, b1], axis=1)], axis=0)

    diag = _neumann([heads(m, 0, 0) for m in ms] + [heads(m, 1, 1) for m in ms], h)
    ta, td = diag[:n], diag[n:]
    low = [_nn(heads(m, 1, 0), a, PS) for m, a in zip(ms, ta)]
    low = [_nn(d, x, PS) for d, x in zip(td, low)]
    out = []
    for a, x, d in zip(ta, low, td):
        rows = []
        for hd in (0, 1):
            sl = slice(hd * h, (hd + 1) * h)
            top, bot = [a[sl, sl], z], [x[sl, sl], d[sl, sl]]
            pad = [z, z]
            rows.append(cat(top + pad if hd == 0 else pad + top, axis=1))
            rows.append(cat(bot + pad if hd == 0 else pad + bot, axis=1))
        out.append(cat(rows, axis=0))
    return tuple(out)


def _unit_inverses_fwd(ms, depth):
    inv = _unit_inverses(ms, depth)
    return inv, inv


def _unit_inverses_bwd(depth, inv, cts):
    left = [_tn(t, g, PS) for t, g in zip(inv, cts)]
    return (tuple(_nt(l, t, PS) for l, t in zip(left, inv)),)


_unit_inverses.defvjp(_unit_inverses_fwd, _unit_inverses_bwd)


@jax.custom_vjp
def _known_inverses(ms, inv):
    return inv


def _known_inverses_fwd(ms, inv):
    return inv, inv


def _known_inverses_bwd(inv, cts):
    dms = _unit_inverses_bwd(None, inv, cts)[0]
    return dms, tuple(jnp.zeros_like(t) for t in inv)


_known_inverses.defvjp(_known_inverses_fwd, _known_inverses_bwd)


def _scan_group(s0s, *flat, known_inv=None, with_inv=False):
    P = len(s0s)
    G = len(flat) // (6 * P)
    ch = [flat[6 * i:6 * i + 6] for i in range(P * G)]
    C = ch[0][0].shape[0]
    C2 = 2 * C
    cat = jnp.concatenate
    m0 = _iota2((1, LANE), 1) < RW_HD
    mask0 = m0.astype(F32)
    mask1 = 1.0 - mask0
    r2 = _iota2((C2, C2), 0)
    c2 = _iota2((C2, C2), 1)
    dist = r2 - c2
    in_head = dist <= r2 % C
    lower = (_iota2((C, C), 0) >= _iota2((C, C), 1)).astype(F32)
    bd = (_iota2((LANE, LANE), 0) // RW_HD) == (_iota2((LANE, LANE), 1) // RW_HD)

    def tri(m, strict):
        return jnp.where(dist > 0 if strict else dist >= 0, jnp.where(in_head, m, 0.0), 0.0)

    def sel(z):
        return jnp.where(m0, z[:C], z[C:])

    gs = [_nn(lower, c[1], HI) for c in ch]
    pre = []
    for (r, lw, k, v, a, b), g in zip(ch, gs):
        g_end = jnp.sum(lw, axis=0, keepdims=True)
        gm = g - jnp.sum(lw[:C // 2], axis=0, keepdims=True)
        en = jnp.exp(-gm)
        ec = jnp.exp(g_end - g)
        pre.append(dict(at=a * jnp.exp(g - lw), rt=r * jnp.exp(g), am=a * jnp.exp(gm - lw), rm=r * jnp.exp(gm),
                        bt=b * en, kt=k * en, bh=b * ec, kh=k * ec, dec=jnp.exp(g_end), v=v))
    grams = [_nt(cat([p["am"] * mask0, p["am"] * mask1, p["rm"] * mask0, p["rm"] * mask1], axis=0),
                 cat([p["bt"], p["bt"], p["kt"], p["kt"]], axis=0), PS) for p in pre]
    mab = tuple(tri(gm[:C2, :C2], True) for gm in grams)
    tinv = _unit_inverses(mab, C) if known_inv is None else _known_inverses(mab, known_inv)
    xv =[sel(_nn(tri(gm[:C2, C2:], True), cat([p["v"], p["v"]], axis=0), PS)) for gm, p in zip(grams, pre)]
    ys, s = [None] * (P * G), list(s0s)
    for i in range(G):
        for q in range(P):
            n = q * G + i
            p, gm = pre[n], grams[n]
            sx = _nt(cat([p["at"], p["rt"]], axis=0), s[q], PS)
            x = sx[:C] + xv[n]
            u = sel(_nn(tinv[n], cat([x, x], axis=0), PS))
            v = p["v"]
            ys[n] = sx[C:] + sel(_nn(cat([tri(gm[C2:, :C2], False), tri(gm[C2:, C2:], False)], axis=1),
                                     cat([u, u, v, v], axis=0), PS))
            s[q] = s[q] * p["dec"] + jnp.where(bd, _tn(cat([u, v], axis=0), cat([p["bh"], p["kh"]], axis=0), PS), 0.0)
    return (tuple(ys), tuple(s), tinv) if with_inv else (tuple(ys), tuple(s))


def _scan_fwd(xs, lw, kp, an, bb, T):
    C, G = _scan_shape(T)
    P = SCAN_PAIRS
    nc = T // (C * G)
    npair = 1024 // LANE

    def kern(r_ref, lw_ref, k_ref, v_ref, a_ref, b_ref, y_ref, st_ref, inv_ref, s_scr):
        n = pl.program_id(1)

        @pl.when(n == 0)
        def _():
            s_scr[...] = jnp.zeros(s_scr.shape, F32)

        st_ref[0] = s_scr[...]
        ins = (r_ref, lw_ref, k_ref, v_ref, a_ref, b_ref)
        ys, s1, inv = _scan_group(
            tuple(s_scr[q] for q in range(P)),
            *[ref[i * C:(i + 1) * C, q * LANE:(q + 1) * LANE] for q in range(P) for i in range(G) for ref in ins], with_inv=True)
        for q in range(P):
            for i in range(G):
                y_ref[i * C:(i + 1) * C, q * LANE:(q + 1) * LANE] = ys[q * G + i]
                inv_ref[0, 0, q * G + i] = inv[q * G + i]
            s_scr[q] = s1[q]

    def col(off):
        return pl.BlockSpec((C * G, P * LANE), lambda p, n: (n, off // P + p))

    return pl.pallas_call(
        kern, name="rwkv_scan_fwd",
        out_shape=[jax.ShapeDtypeStruct((T, 1024), F32), jax.ShapeDtypeStruct((nc, npair, LANE, LANE), F32),
                   jax.ShapeDtypeStruct((nc, npair // P, P * G, 2 * C, 2 * C), F32)],
        grid=(npair // P, nc),
        in_specs=[col(0), col(0), col(0), col(16), col(0), col(0)],
        out_specs=[col(0), pl.BlockSpec((1, P, LANE, LANE), lambda p, n: (n, p, 0, 0)),
                   pl.BlockSpec((1, 1, P * G, 2 * C, 2 * C), lambda p, n: (n, p, 0, 0, 0))],
        scratch_shapes=[pltpu.VMEM((P, LANE, LANE), F32)],
        compiler_params=_cp(("parallel", "arbitrary")),
    )(xs, lw, kp, xs, an, bb)


def _scan_bwd(xs, lw, kp, an, bb, states, invs, dy, T):
    C, G = _scan_shape(T)
    P = SCAN_PAIRS
    nc = T // (C * G)
    npair = 1024 // LANE

    def kern(r_ref, lw_ref, k_ref, v_ref, a_ref, b_ref, st_ref, inv_ref, dy_ref, dr_ref, dlw_ref, dk_ref, dv_ref, da_ref, db_ref,
             ds_scr):
        n = pl.program_id(1)

        @pl.when(n == 0)
        def _():
            ds_scr[...] = jnp.zeros(ds_scr.shape, F32)

        ins = (r_ref, lw_ref, k_ref, v_ref, a_ref, b_ref)
        units = [(q, i) for q in range(P) for i in range(G)]
        known = tuple(inv_ref[0, 0, q * G + i] for q, i in units)
        _, vjp = jax.vjp(functools.partial(_scan_group, known_inv=known), tuple(st_ref[0, q] for q in range(P)),
                         *[ref[i * C:(i + 1) * C, q * LANE:(q + 1) * LANE] for q, i in units for ref in ins])
        grads = vjp((tuple(dy_ref[i * C:(i + 1) * C, q * LANE:(q + 1) * LANE] for q, i in units),
                     tuple(ds_scr[q] for q in range(P))))
        for q in range(P):
            ds_scr[q] = grads[0][q]
        outs = (dr_ref, dlw_ref, dk_ref, dv_ref, da_ref, db_ref)
        for n_, (q, i) in enumerate(units):
            for t, ref in enumerate(outs):
                ref[i * C:(i + 1) * C, q * LANE:(q + 1) * LANE] = grads[1 + 6 * n_ + t]

    def col(off):
        return pl.BlockSpec((C * G, P * LANE), lambda p, n: (nc - 1 - n, off // P + p))

    return pl.pallas_call(
        kern, name="rwkv_scan_bwd",
        out_shape=[jax.ShapeDtypeStruct((T, 1024), F32)] * 6,
        grid=(npair // P, nc),
        in_specs=[col(0), col(0), col(0), col(16), col(0), col(0),
                  pl.BlockSpec((1, P, LANE, LANE), lambda p, n: (nc - 1 - n, p, 0, 0)),
                  pl.BlockSpec((1, 1, P * G, 2 * C, 2 * C), lambda p, n: (nc - 1 - n, p, 0, 0, 0)), col(0)],
        out_specs=[col(0)] * 6,
        scratch_shapes=[pltpu.VMEM((P, LANE, LANE), F32)],
        compiler_params=_cp(("parallel", "arbitrary")),
    )(xs, lw, kp, xs, an, bb, states, invs, dy)


def _gates_fwd(u, bias_pad, T, f_cb):
    nb = T // LANE

    def kern(f_ref, b_ref, c_ref):
        x = f_ref[...] + b_ref[...]
        lf = jnp.minimum(x, 0.0) - jnp.log(1.0 + jnp.exp(-jnp.abs(x)))
        lft = lf.T
        ut = (_iota2((LANE, LANE), 0) <= _iota2((LANE, LANE), 1)).astype(F32)
        carry = jnp.zeros((LANE, 1), F32)
        for blk in range(nb):
            seg = lft[:, blk * LANE:(blk + 1) * LANE]
            cs = _nn(seg, ut, HI) + carry
            c_ref[:, blk * LANE:(blk + 1) * LANE] = cs[:SUB, :]
            carry = carry + jnp.sum(seg, axis=1, keepdims=True)

    return pl.pallas_call(
        kern, name="fox_gates_fwd",
        out_shape=jax.ShapeDtypeStruct((SUB, T), F32),
        grid=(1,),
        in_specs=[pl.BlockSpec((T, LANE), lambda i: (0, f_cb)), pl.BlockSpec((1, LANE), lambda i: (0, 0))],
        out_specs=pl.BlockSpec((SUB, T), lambda i: (0, 0)),
        compiler_params=_cp(("arbitrary",)),
    )(u, bias_pad)


def _gates_bwd(dc, u, bias_pad, T, f_cb):
    nb = T // LANE

    def kern(dc_ref, f_ref, b_ref, dfl_ref, db_ref):
        dcv = jnp.concatenate([dc_ref[...], jnp.zeros((LANE - SUB, T), F32)], axis=0)
        lt = (_iota2((LANE, LANE), 0) >= _iota2((LANE, LANE), 1)).astype(F32)
        carry = jnp.zeros((LANE, 1), F32)
        pieces = [None] * nb
        for blk in range(nb - 1, -1, -1):
            seg = dcv[:, blk * LANE:(blk + 1) * LANE]
            pieces[blk] = _nn(seg, lt, HI) + carry
            carry = carry + jnp.sum(seg, axis=1, keepdims=True)
        dlf = (pieces[0] if nb == 1 else jnp.concatenate(pieces, axis=1)).T
        x = f_ref[...] + b_ref[...]
        dfl = dlf * _sigmoid(-x)
        dfl_ref[...] = dfl
        db_ref[...] = jnp.sum(dfl, axis=0, keepdims=True)

    return pl.pallas_call(
        kern, name="fox_gates_bwd",
        out_shape=[jax.ShapeDtypeStruct((T, LANE), F32), jax.ShapeDtypeStruct((1, LANE), F32)],
        grid=(1,),
        in_specs=[pl.BlockSpec((SUB, T), lambda i: (0, 0)), pl.BlockSpec((T, LANE), lambda i: (0, f_cb)),
                  pl.BlockSpec((1, LANE), lambda i: (0, 0))],
        out_specs=[pl.BlockSpec((T, LANE), lambda i: (0, 0)), pl.BlockSpec((1, LANE), lambda i: (0, 0))],
        compiler_params=_cp(("arbitrary",)),
    )(dc, u, bias_pad)


ATTN_HEADS = 2


def _attn_block(T):
    return 512 if T % 512 == 0 and T >= 1024 else 128


def _attn_fwd(u, c3, T):
    H, HP = 8, ATTN_HEADS
    bq = _attn_block(T)
    nq = T // bq
    scale = FOX_HD ** -0.5
    lanes = [slice(h * LANE, (h + 1) * LANE) for h in range(HP)]

    def kern(q_ref, k_ref, v_ref, z_ref, cq_ref, ck_ref, o_ref, oa_ref, lse_ref):
        i = pl.program_id(1)
        q = [(q_ref[:, ln] * scale).astype(BF16) for ln in lanes]
        c0 = [cq_ref[h][:, 0:1] for h in range(HP)]

        def step(j, carry, diagonal=False):
            off = pl.multiple_of(j * bq, bq)
            s = [_nt(q[h], k_ref[pl.ds(off, bq), lanes[h]].astype(BF16)) + (c0[h] - ck_ref[h, :, pl.ds(off, bq)])
                 for h in range(HP)]
            ps, out = [], []
            for h in range(HP):
                m, l, acc = carry[h]
                sh = s[h]
                if diagonal:
                    sh = jnp.where(_iota2((bq, bq), 1) <= _iota2((bq, bq), 0), sh, NEG)
                m_new = jnp.maximum(m, jnp.max(sh, axis=1, keepdims=True))
                p = jnp.exp(sh - m_new)
                alpha = jnp.exp(m - m_new)
                p_hi = p.astype(BF16)
                ps.append((p_hi, (p - p_hi.astype(F32)).astype(BF16)))
                out.append((m_new, alpha * l + jnp.sum(p, axis=1, keepdims=True), alpha * acc))
            res = []
            for h, (m, l, acc) in enumerate(out):
                vj = v_ref[pl.ds(off, bq), lanes[h]].astype(BF16)
                res.append((m, l, acc + _nn(ps[h][0], vj) + _nn(ps[h][1], vj)))
            return tuple(res)

        init = tuple((jnp.full((bq, 1), NEG, F32), jnp.zeros((bq, 1), F32), jnp.zeros((bq, FOX_HD), F32)) for _ in range(HP))
        res = step(i, lax.fori_loop(0, i, step, init), diagonal=True)
        for h, (m, l, acc) in enumerate(res):
            o = acc / l
            z = z_ref[:, lanes[h]]
            o_ref[:, lanes[h]] = o
            oa_ref[:, lanes[h]] = (o * z * _sigmoid(z)).astype(BF16)
            lse_ref[h] = m + jnp.log(l)

    W = HP * LANE
    return pl.pallas_call(
        kern, name="fox_attn_fwd",
        out_shape=[jax.ShapeDtypeStruct((T, 1024), F32), jax.ShapeDtypeStruct((T, 1024), BF16),
                   jax.ShapeDtypeStruct((H, T, 1), F32)],
        grid=(H // HP, nq),
        in_specs=[pl.BlockSpec((bq, W), lambda g, i: (i, g)),
                  pl.BlockSpec((T, W), lambda g, i: (0, 8 // HP + g)),
                  pl.BlockSpec((T, W), lambda g, i: (0, 16 // HP + g)),
                  pl.BlockSpec((bq, W), lambda g, i: (i, 24 // HP + g)),
                  pl.BlockSpec((HP, 1, bq), lambda g, i: (g, 0, i)),
                  pl.BlockSpec((HP, 1, T), lambda g, i: (g, 0, 0))],
        out_specs=[pl.BlockSpec((bq, W), lambda g, i: (i, g)),
                   pl.BlockSpec((bq, W), lambda g, i: (i, g)),
                   pl.BlockSpec((HP, bq, 1), lambda g, i: (g, i, 0))],
        compiler_params=_cp(("parallel", "arbitrary")),
    )(u, u, u, u, c3, c3)


def _attn_probs(s, lse_i, diagonal):
    if not diagonal:
        return jnp.exp(s - lse_i)
    keep = _iota2(s.shape, 1) <= _iota2(s.shape, 0)
    return jnp.where(keep, jnp.exp(jnp.where(keep, s, NEG) - lse_i), 0.0)


def _attn_pre_math(doa, z, o):
    sg = _sigmoid(z)
    do = (doa * z * sg).astype(BF16)
    dz = doa * o * (sg * (1.0 + z * (1.0 - sg)))
    head_of = (_iota2((o.shape[1], LANE), 0) // FOX_HD == _iota2((o.shape[1], LANE), 1)).astype(F32)
    return do, dz, _nn(do.astype(F32) * o, head_of, HI)


def _attn_bwd(u, c3, lse, do, delta, T):
    H, HP = 8, ATTN_HEADS
    bq = _attn_block(T)
    nq = T // bq
    scale = FOX_HD ** -0.5
    lanes = [slice(h * LANE, (h + 1) * LANE) for h in range(HP)]

    def kern(q_ref, k_ref, v_ref, c_ref, lse_ref, do_ref, dl_ref, dq_ref, dk_ref, dv_ref, dc_ref):
        j = pl.program_id(1)

        @pl.when(j == 0)
        def _():
            dq_ref[...] = jnp.zeros(dq_ref.shape, F32)

        kj = [k_ref[:, ln].astype(BF16) for ln in lanes]
        vj = [v_ref[:, ln].astype(BF16) for ln in lanes]
        joff = pl.multiple_of(j * bq, bq)
        ck = [c_ref[h, :, pl.ds(joff, bq)] for h in range(HP)]

        def step(i, carry, diagonal=False):
            off = pl.multiple_of(i * bq, bq)
            qs = [(q_ref[pl.ds(off, bq), ln] * scale).astype(BF16) for ln in lanes]
            dob = [do_ref[pl.ds(off, bq), ln] for ln in lanes]
            s = [_nt(qs[h], kj[h]) + (c_ref[h, :, pl.ds(off, bq)][:, 0:1] - ck[h]) for h in range(HP)]
            dp = [_nt(dob[h], vj[h]) for h in range(HP)]
            pb, dsb, dcs = [], [], []
            for h in range(HP):
                p = _attn_probs(s[h], lse_ref[h, pl.ds(off, bq), :], diagonal)
                ds = p * (dp[h] - dl_ref[h, pl.ds(off, bq), :])
                pb.append(p.astype(BF16))
                dsb.append(ds.astype(BF16))
                dcs.append(jnp.sum(ds, axis=0, keepdims=True))
            out = []
            for h, (dk, dv, dc) in enumerate(carry):
                dq_ref[pl.ds(off, bq), lanes[h]] += _nn(dsb[h], kj[h]) * scale
                out.append((dk + _tn(dsb[h], qs[h]), dv + _tn(pb[h], dob[h]), dc - dcs[h]))
            return tuple(out)

        init = tuple((jnp.zeros((bq, FOX_HD), F32), jnp.zeros((bq, FOX_HD), F32), jnp.zeros((1, bq), F32)) for _ in range(HP))
        res = lax.fori_loop(j + 1, nq, step, step(j, init, diagonal=True))
        for h, (dk, dv, dc) in enumerate(res):
            dk_ref[:, lanes[h]] = dk
            dv_ref[:, lanes[h]] = dv
            dc_ref[h] = dc

    W = HP * LANE
    full = lambda cb: pl.BlockSpec((T, W), lambda g, j: (0, cb // HP + g))
    blk = lambda cb: pl.BlockSpec((bq, W), lambda g, j: (j, cb // HP + g))
    col1 = pl.BlockSpec((HP, T, 1), lambda g, j: (g, 0, 0))
    return pl.pallas_call(
        kern, name="fox_attn_bwd",
        out_shape=[jax.ShapeDtypeStruct((T, 1024), F32)] * 3 + [jax.ShapeDtypeStruct((H, 1, T), F32)],
        grid=(H // HP, nq),
        in_specs=[full(0), blk(8), blk(16), pl.BlockSpec((HP, 1, T), lambda g, j: (g, 0, 0)), col1, full(0), col1],
        out_specs=[full(0), blk(0), blk(0), pl.BlockSpec((HP, 1, bq), lambda g, j: (g, 0, j))],
        compiler_params=_cp(("parallel", "arbitrary")),
    )(u, u, u, c3, lse, do, delta)


def _place():
    return lax.axis_index("x"), lax.axis_index("y"), lax.axis_index("c")


def _slot(p):
    return 4 * p[0] + 2 * p[1] + p[2]


def _other_chips(x, y):
    return [(1 - x, y), (x, 1 - y), (1 - x, 1 - y)]


def _allgather_steps(in_refs, out_refs, scratch):
    (src,), (dst,) = in_refs, out_refs
    send_sems, recv_sems, local_sem = scratch
    x, y, c = _place()
    me, sibling = (x, y, c), (x, y, 1 - c)
    chips = _other_chips(x, y)

    def copy(k, block, to, from_input=False):
        d = dst.at[_slot(block)]
        return pltpu.make_async_remote_copy(
            src_ref=src if from_input else d, dst_ref=d, send_sem=send_sems.at[k], recv_sem=recv_sems.at[k],
            device_id=to, device_id_type=MESH)

    def first_copies():
        return [copy(0, me, sibling, True)] + [copy(1 + j, me, (*chip, c), True) for j, chip in enumerate(chips)]

    def start():
        pltpu.make_async_copy(src, dst.at[_slot(me)], local_sem).start()
        for cp in first_copies():
            cp.start()

    def finish():
        passed = []
        for j, chip in enumerate(chips):
            copy(1 + j, (*chip, c), me).wait_recv()
            passed.append(copy(4 + j, (*chip, c), sibling))
            passed[-1].start()
        copy(0, sibling, me).wait_recv()
        for j, chip in enumerate(chips):
            copy(4 + j, (*chip, 1 - c), me).wait_recv()
        for cp in first_copies() + passed:
            cp.wait_send()
        pltpu.make_async_copy(src, dst.at[_slot(me)], local_sem).wait()

    return start, finish


def _allgather_relay_steps(in_refs, out_refs, scratch):
    (src,), (dst,) = in_refs, out_refs
    send_sems, recv_sems, local_sem = scratch
    x, y, c = _place()
    me, sibling = (x, y, c), (x, y, 1 - c)
    x_nbr, y_nbr, diag = (1 - x, y, c), (x, 1 - y, c), (1 - x, 1 - y, c)
    flip = lambda a, bit: a + bit - 2 * a * bit
    relay_from = (flip(x, 1 - c), flip(y, c), c)
    relay_to = (flip(x, c), flip(y, 1 - c), c)

    def copy(k, block, to, from_input=False):
        d = dst.at[_slot(block)]
        return pltpu.make_async_remote_copy(
            src_ref=src if from_input else d, dst_ref=d, send_sem=send_sems.at[k], recv_sem=recv_sems.at[k],
            device_id=to, device_id_type=MESH)

    def first_copies():
        return [copy(0, me, sibling, True), copy(1, me, x_nbr, True), copy(2, me, y_nbr, True)]

    def other(block):
        return block[:2] + (1 - c,)

    def start():
        pltpu.make_async_copy(src, dst.at[_slot(me)], local_sem).start()
        for cp in first_copies():
            cp.start()

    def finish():
        copy(1, x_nbr, me).wait_recv()
        copy(2, y_nbr, me).wait_recv()
        later = [copy(3, relay_from, relay_to), copy(4, x_nbr, sibling), copy(5, y_nbr, sibling)]
        for cp in later:
            cp.start()
        copy(3, diag, me).wait_recv()
        later.append(copy(6, diag, sibling))
        later[-1].start()
        copy(0, sibling, me).wait_recv()
        for k, block in ((4, x_nbr), (5, y_nbr), (6, diag)):
            copy(k, other(block), me).wait_recv()
        for cp in first_copies() + later:
            cp.wait_send()
        pltpu.make_async_copy(src, dst.at[_slot(me)], local_sem).wait()

    return start, finish


def _allgather_side(a, relay=False):
    return ((a,), (jax.ShapeDtypeStruct((N_DEV,) + a.shape, a.dtype),),
            (pltpu.SemaphoreType.DMA((7,)), pltpu.SemaphoreType.DMA((7,)), pltpu.SemaphoreType.DMA),
            _allgather_relay_steps if relay else _allgather_steps)


def _allgather(a, name, relay=False):
    ins, outs, scratch, make = _allgather_side(a, relay)

    def body(a_ref, o_ref, *scr):
        start, finish = make((a_ref,), (o_ref,), scr)
        start()
        finish()

    any_spec = pl.BlockSpec(memory_space=pl.ANY)
    return pl.pallas_call(body, name=name, out_shape=outs[0], in_specs=[any_spec], out_specs=any_spec,
                          scratch_shapes=list(scratch))(a)


def _exchange_pair_steps(in_refs, out_refs, scratch):
    (src,), (dst,) = in_refs, out_refs
    send_sems, recv_sems = scratch
    x, y, c = _place()
    sibling = (x, y, 1 - c)
    slots = [_slot(sibling)] + [_slot((*chip, 1 - c)) for chip in _other_chips(x, y)]

    def copies():
        return [pltpu.make_async_remote_copy(
            src_ref=src.at[ps], dst_ref=dst.at[k], send_sem=send_sems.at[k], recv_sem=recv_sems.at[k],
            device_id=sibling, device_id_type=MESH) for k, ps in enumerate(slots)]

    def start():
        for cp in copies():
            cp.start()

    def finish():
        for cp in copies():
            cp.wait()

    return start, finish


def _exchange_pair_side(g):
    return ((g,), (jax.ShapeDtypeStruct((4,) + g.shape[1:], g.dtype),),
            (pltpu.SemaphoreType.DMA((4,)), pltpu.SemaphoreType.DMA((4,))), _exchange_pair_steps)


def _exchange_pair(g):
    ins, outs, scratch, make = _exchange_pair_side(g)

    def body(g_ref, r_ref, *scr):
        start, finish = make((g_ref,), (r_ref,), scr)
        start()
        finish()

    any_spec = pl.BlockSpec(memory_space=pl.ANY)
    return pl.pallas_call(body, name="exchange_pair", out_shape=outs[0], in_specs=[any_spec], out_specs=any_spec,
                          scratch_shapes=list(scratch))(g)


def _tiling(R, Cc, tile, by_cols):
    if by_cols:
        assert Cc % tile == 0
        return Cc // tile, (R, tile), lambda lead, i: (lead, 0, i)
    assert R % tile == 0
    return R // tile, (tile, Cc), lambda lead, i: (lead, i, 0)


def _pair_add(name, g, r1, slots, tile, by_cols=False, side=None):
    _, R, Cc = g.shape
    steps, blk, at = _tiling(R, Cc, tile, by_cols)
    side_ins, side_outs, side_scr, side_make = side if side is not None else ((), (), (), None)
    n_si, n_so = len(side_ins), len(side_outs)

    def kern(s_ref, a_ref, b_ref, *rest):
        o_ref = rest[n_si]
        if side_make is not None:
            start, finish = side_make(rest[:n_si], rest[n_si + 1:n_si + 1 + n_so], rest[n_si + 1 + n_so:])
            j, i = pl.program_id(0), pl.program_id(1)
            pl.when(jnp.logical_and(j == 0, i == 0))(start)
        o_ref[...] = (a_ref[...].astype(F32) + b_ref[...].astype(F32)).astype(o_ref.dtype)
        if side_make is not None:
            pl.when(jnp.logical_and(j == 2, i == steps - 1))(finish)

    any_spec = pl.BlockSpec(memory_space=pl.ANY)
    res = pl.pallas_call(
        kern, name=name,
        out_shape=[jax.ShapeDtypeStruct((3, R, Cc), BF16)] + list(side_outs),
        grid_spec=pltpu.PrefetchScalarGridSpec(
            num_scalar_prefetch=1, grid=(3, steps),
            in_specs=[pl.BlockSpec((1,) + blk, lambda j, i, s: at(s[j], i)),
                      pl.BlockSpec((1,) + blk, lambda j, i, s: at(1 + j, i))] + [any_spec] * n_si,
            out_specs=[pl.BlockSpec((1,) + blk, lambda j, i, s: at(j, i))] + [any_spec] * n_so,
            scratch_shapes=list(side_scr)),
        compiler_params=_cp(("arbitrary", "arbitrary")),
    )(slots, g, r1, *side_ins)
    return res if side is not None else res[0]


def _axis_neighbours():
    x, y, c = _place()
    flip = lambda a, bit: a + bit - 2 * a * bit
    return (flip(x, c), flip(y, 1 - c), c), (flip(x, 1 - c), flip(y, c), c), c


def _exchange_first_steps(in_refs, out_refs, scratch):
    pairs = list(zip(in_refs, out_refs))
    send_sems, recv_sems = scratch
    first, _, c = _axis_neighbours()

    def copies():
        return [pltpu.make_async_remote_copy(
            src_ref=src.at[j], dst_ref=dst.at[k], send_sem=send_sems.at[t, k], recv_sem=recv_sems.at[t, k],
            device_id=first, device_id_type=MESH)
            for t, (src, dst) in enumerate(pairs) for k, j in enumerate((1 - c, 2))]

    def start():
        for cp in copies():
            cp.start()

    def finish():
        for cp in copies():
            cp.wait()

    return start, finish


def _exchange_first_side(*ss):
    n = len(ss)
    return (ss, tuple(jax.ShapeDtypeStruct((2,) + s.shape[1:], s.dtype) for s in ss),
            (pltpu.SemaphoreType.DMA((n, 2)), pltpu.SemaphoreType.DMA((n, 2))), _exchange_first_steps)


def _axis_add(name, s, t1, core, tile, by_cols=False):
    _, R, Cc = s.shape
    steps, blk, at = _tiling(R, Cc, tile, by_cols)

    def kern(c_ref, a_ref, b_ref, o_ref):
        o_ref[...] = (a_ref[...].astype(F32) + b_ref[...].astype(F32)).astype(o_ref.dtype)

    return pl.pallas_call(
        kern, name=name,
        out_shape=jax.ShapeDtypeStruct((1, R, Cc), BF16),
        grid_spec=pltpu.PrefetchScalarGridSpec(
            num_scalar_prefetch=1, grid=(steps,),
            in_specs=[pl.BlockSpec((1,) + blk, lambda i, cr: at(cr[0], i)),
                      pl.BlockSpec((1,) + blk, lambda i, cr: at(1, i))],
            out_specs=pl.BlockSpec((1,) + blk, lambda i, cr: at(0, i))),
        compiler_params=_cp(("arbitrary",)),
    )(core, s, t1)


def _exchange_second_steps(in_refs, out_refs, scratch):
    send_sems, recv_sems = scratch
    _, second, _ = _axis_neighbours()

    def copies():
        return [pltpu.make_async_remote_copy(src_ref=src, dst_ref=dst, send_sem=send_sems.at[t], recv_sem=recv_sems.at[t],
                                             device_id=second, device_id_type=MESH)
                for t, (src, dst) in enumerate(zip(in_refs, out_refs))]

    def start():
        for cp in copies():
            cp.start()

    def finish():
        for cp in copies():
            cp.wait()

    return start, finish


def _exchange_second_side(*ps):
    n = len(ps)
    return (ps, tuple(jax.ShapeDtypeStruct(p.shape, p.dtype) for p in ps),
            (pltpu.SemaphoreType.DMA((n,)), pltpu.SemaphoreType.DMA((n,))), _exchange_second_steps)


def _adamw(name, w, m, v, parts, tile, by_cols=False):
    R, Cc = w.shape
    steps, blk_shape, at = _tiling(R, Cc, tile, by_cols)
    n_parts = len(parts)

    def kern(*refs):
        w_ref, m_ref, v_ref = refs[:3]
        g = None
        for r_ref, (_, n) in zip(refs[3:3 + n_parts], parts):
            for s in range(n):
                term = r_ref[s].astype(F32)
                g = term if g is None else g + term
        g_out, d_out, m_out, v_out = refs[3 + n_parts:]
        mn = ADAM_B1 * m_ref[...] + (1.0 - ADAM_B1) * g
        vn = ADAM_B2 * v_ref[...] + (1.0 - ADAM_B2) * (g * g)
        m_hat = mn / (1.0 - ADAM_B1 ** ADAM_STEP)
        v_hat = vn / (1.0 - ADAM_B2 ** ADAM_STEP)
        g_out[...] = g
        d_out[...] = -ADAM_LR * (m_hat / (jnp.sqrt(v_hat) + ADAM_EPS) + ADAM_WD * w_ref[...])
        m_out[...] = mn
        v_out[...] = vn

    blk = pl.BlockSpec(blk_shape, lambda i: at(0, i)[1:])
    return pl.pallas_call(
        kern, name=name,
        out_shape=[jax.ShapeDtypeStruct((R, Cc), F32)] * 4,
        grid=(steps,),
        in_specs=[blk] * 3 + [pl.BlockSpec((n,) + blk_shape, lambda i: at(0, i)) for (_, n) in parts],
        out_specs=[blk] * 4,
        compiler_params=_cp(("arbitrary",)),
    )(w, m, v, *[a for (a, _) in parts])


def _assemble_columns(blocks, pieces, zeros, width):
    _, R, Cc = blocks.shape
    tr = min(256, R)

    def kern(b_ref, o_ref):
        for col, n in zeros:
            o_ref[:, col:col + n] = jnp.zeros((tr, n), o_ref.dtype)
        for col, d, lo, n in pieces:
            o_ref[:, col:col + n] = b_ref[d, :, lo:lo + n]

    return pl.pallas_call(
        kern, name="assemble_w_in",
        out_shape=jax.ShapeDtypeStruct((R, width), blocks.dtype),
        grid=(R // tr,),
        in_specs=[pl.BlockSpec((N_DEV, tr, Cc), lambda i: (0, i, 0))],
        out_specs=pl.BlockSpec((tr, width), lambda i: (i, 0)),
        compiler_params=_cp(("parallel",)),
    )(blocks)


def _split_rows(x, pieces, rows, side):
    _, Cc = x.shape
    tc = min(256, Cc)
    side_ins, side_outs, side_scr, side_make = side
    n_si, n_so = len(side_ins), len(side_outs)

    def kern(x_ref, *rest):
        o_ref = rest[n_si]
        start, finish = side_make(rest[:n_si], rest[n_si + 1:n_si + 1 + n_so], rest[n_si + 1 + n_so:])
        pl.when(pl.program_id(0) == 0)(start)
        for d, lo, row, n in pieces:
            o_ref[d, lo:lo + n, :] = x_ref[row:row + n, :]
        pl.when(pl.program_id(0) == Cc // tc - 1)(finish)

    any_spec = pl.BlockSpec(memory_space=pl.ANY)
    return pl.pallas_call(
        kern, name="split_w_in_grad",
        out_shape=[jax.ShapeDtypeStruct((N_DEV, rows, Cc), x.dtype)] + list(side_outs),
        grid=(Cc // tc,),
        in_specs=[pl.BlockSpec((x.shape[0], tc), lambda i: (0, i))] + [any_spec] * n_si,
        out_specs=[pl.BlockSpec((N_DEV, rows, tc), lambda i: (0, 0, i))] + [any_spec] * n_so,
        scratch_shapes=list(side_scr),
        compiler_params=_cp(("arbitrary",)),
    )(x, *side_ins)


def _pad_cols(a, w):
    return jnp.pad(a, ((0, 0), (0, w - a.shape[1])))


def _pad_rows(a, r):
    return jnp.pad(a, ((0, r - a.shape[0]), (0, 0)))


def _pack_b(pf, pr, wo, w2, a2, rows):
    body = jnp.concatenate([pf, pr, wo.reshape(2048, 256), jnp.concatenate([w2, a2], axis=1)], axis=0)
    return _pad_rows(body, rows)


def kernel(x, norm_gain, w_in, fox_forget_bias, rwkv_shift_mix, rwkv_w0, rwkv_w2, rwkv_a0, rwkv_a2, rwkv_k_k, rwkv_k_a, rwkv_r_k, rwkv_ln_w, rwkv_ln_b, w_proj_fox, w_proj_rwkv, w_out, final_norm_gain, loss_target, m_norm_gain, m_w_in, m_fox_forget_bias, m_rwkv_shift_mix, m_rwkv_w0, m_rwkv_w2, m_rwkv_a0, m_rwkv_a2, m_rwkv_k_k, m_rwkv_k_a, m_rwkv_r_k, m_rwkv_ln_w, m_rwkv_ln_b, m_w_proj_fox, m_w_proj_rwkv, m_w_out, m_final_norm_gain, v_norm_gain, v_w_in, v_fox_forget_bias, v_rwkv_shift_mix, v_rwkv_w0, v_rwkv_w2, v_rwkv_a0, v_rwkv_a2, v_rwkv_k_k, v_rwkv_k_a, v_rwkv_r_k, v_rwkv_ln_w, v_rwkv_ln_b, v_w_proj_fox, v_w_proj_rwkv, v_w_out, v_final_norm_gain):
    T, D = x.shape[1], x.shape[2]
    assert D == 2048 and T % LANE == 0
    NI = w_in.shape[2]
    IN = N_DEV * NI
    RB = 4224
    x2 = x[0]
    lt2 = loss_target[0]
    me = _slot(_place())

    tb = min(256, T)
    tbh = min(128, T)
    h, wa = _rowcall("rms_fwd", T, tb, [_rows(x2, tb, D), _whole(norm_gain)],
                     lambda i, xv, g: ([_rms_math(xv, g)], []), [(D, BF16)],
                     side=_allgather_side(w_in[0].astype(BF16), relay=True))
    packed_own = _pack_b(w_proj_fox[0], w_proj_rwkv[0], w_out[0], rwkv_w2[0], rwkv_a2[0], RB).astype(BF16)
    sections = [(0, 4096, 0), (4104, 4096, 4096), (8392, 4096, 8192), (4096, 8, 12288), (8200, 96, 12544), (8296, 96, 12672)]
    NP = 12800
    pieces, zeros, at_col = [], [], 0
    for lo, width, pad_lo in sections:
        if pad_lo > at_col:
            zeros.append((at_col, pad_lo - at_col))
        col = lo
        while col < lo + width:
            d = col // NI
            stop = min(lo + width, (d + 1) * NI)
            pieces.append((pad_lo + col - lo, d, col - d * NI, stop - col))
            col = stop
        at_col = pad_lo + width
    zeros.append((at_col, NP - at_col))
    w_pad = _assemble_columns(wa, pieces, zeros, NP)
    F_CB, LORA_CB = 96, 49

    mu = rwkv_shift_mix
    mu_main = mu[:, 0:4096]
    mu_lora = jnp.concatenate([_pad_cols(mu[:, 4096:4192], LANE), _pad_cols(mu[:, 4192:4288], LANE)], axis=1)
    bias_pad = _pad_cols(fox_forget_bias, LANE)
    rk_flat = rwkv_r_k.reshape(1, 1024)
    gf = final_norm_gain.reshape(1, D)

    u, wb = _mm(h, w_pad, tm=1024, tn=1280, name="mm_in", side=_allgather_side(packed_own))
    wpf = wb[:, 0:1024, :].transpose(1, 0, 2).reshape(1024, D)
    wpr = wb[:, 1024:2048, :].transpose(1, 0, 2).reshape(1024, D)
    wo = wb[:, 2048:4096, :].reshape(N_DEV * 256, D)
    w2p = _pad_rows(wb[:, 4096:4192, 0:128].transpose(1, 0, 2).reshape(96, 1024).astype(F32), LANE)
    a2p = _pad_rows(wb[:, 4096:4192, 128:256].transpose(1, 0, 2).reshape(96, 1024).astype(F32), LANE)

    c8 = _gates_fwd(u, bias_pad, T, F_CB)
    c3 = c8.reshape(8, 1, T)
    o_raw, o_a, lse = _attn_fwd(u, c3, T)

    def shift_body(i, um, hm, ul, hl, mm_, ml):
        outs = []
        for uv, hv, mv in ((um, hm, mm_), (ul, hl, ml)):
            hv = jnp.where(i == 0, 0.0, hv)
            prev = pltpu.roll(jnp.concatenate([hv, uv], axis=0), 1, 0)[SUB:]
            outs.append(uv + (prev - uv) * mv)
        return outs, []

    def halo_prev(arr, w, cb):
        return (arr, (SUB, w), lambda i: (jnp.maximum(i * (tbh // SUB) - 1, 0), cb))

    xs, xl = _rowcall("rwkv_shift_fwd", T, tbh,
                      [_rows(u, tbh, 4096, 1), halo_prev(u, 4096, 1), _rows(u, tbh, 256, LORA_CB), halo_prev(u, 256, LORA_CB),
                       _whole(mu_main), _whole(mu_lora)],
                      shift_body, [(4096, F32), (256, F32)])

    prep_par = [_whole(rwkv_w0), _whole(rwkv_a0), _whole(rwkv_k_k), _whole(rwkv_k_a), _whole(w2p), _whole(a2p)]
    prep_rows = [_rows(xs, tbh, 1024, 1), _rows(xl, tbh, LANE, 0), _rows(xl, tbh, LANE, 1)]
    lw, kp, an, bb = _rowcall("rwkv_prep_fwd", T, tbh, prep_rows + prep_par,
                              lambda i, *a: (list(_prep_math(*a)), []), [(1024, F32)] * 4)
    y, states, invs = _scan_fwd(xs, lw, kp, an, bb, T)
    post_rows = [_rows(y, tbh, 1024), _rows(xs, tbh, 1024, 0), _rows(kp, tbh, 1024), _rows(xs, tbh, 1024, 2), _rows(xs, tbh, 1024, 3)]
    post_par = [_whole(rwkv_ln_w), _whole(rwkv_ln_b), _whole(rk_flat)]
    (o_b,) = _rowcall("rwkv_post_fwd", T, tbh, post_rows + post_par,
                      lambda i, *a: ([_post_math(*a)], []), [(1024, BF16)])

    pa = _mm(o_a, wpf, out_dtype=BF16, name="mm_proj_fox")
    pb = _mm(o_b, wpr, out_dtype=BF16, name="mm_proj_rwkv")
    merge_rows = [_rows(u, tb, D, 4), _rows(u, tb, D, 5), _rows(pa, tb, D), _rows(pb, tb, D)]
    (mg,) = _rowcall("merge_fwd", T, tb, merge_rows, lambda i, *a: ([_merge_math(*a)], []), [(D, BF16)])

    def head_body(i, xv, mgv, wov, ltv, g):
        out = xv + _nn(mgv, wov)
        r = lax.rsqrt(jnp.mean(out * out, axis=-1, keepdims=True) + RMS_EPS)
        yn = out * r
        err = yn * g - ltv
        loss = 0.5 * jnp.sum(jnp.sum(err * err, axis=-1, keepdims=True), axis=0, keepdims=True) / D
        dyv = err / D
        dyn = dyv * g
        dout = r * (dyn - yn * jnp.mean(dyn * yn, axis=-1, keepdims=True))
        return [dout], [loss, jnp.sum(dyv * yn, axis=0, keepdims=True)]

    dout, loss_p, dgf_p = _rowcall("mm_out_loss_head", T, tb,
                                   [_rows(x2, tb, D), _rows(mg, tb, D), _whole(wo), _rows(lt2, tb, D), _whole(gf)],
                                   head_body, [(D, F32)], [(1, 1), (1, D)])

    dwo = _mm(mg, dout, ta=True, out_dtype=BF16, name="mm_out_dw")

    def merge_bwd_body(i, ga, gb, pav, pbv, doutv, wov):
        _, vjp = jax.vjp(_merge_math, ga, gb, pav.astype(F32), pbv.astype(F32))
        dga, dgb, dpa, dpb = vjp(_nt(doutv.astype(BF16), wov))
        return [dga, dgb, dpa, dpb], []

    dga, dgb, dpa, dpb = _rowcall("mm_out_dx_merge_bwd", T, tb, merge_rows + [_rows(dout, tb, D), _whole(wo)], merge_bwd_body,
                                  [(D, BF16), (D, BF16), (D, BF16), (D, BF16)])
    doa = _mm(dpa, wpf, tb=True, name="mm_proj_fox_dx")
    dwpf = _mm(o_a, dpa, ta=True, out_dtype=BF16, name="mm_proj_fox_dw")
    dob = _mm(dpb, wpr, tb=True, name="mm_proj_rwkv_dx")
    dwpr = _mm(o_b, dpb, ta=True, out_dtype=BF16, name="mm_proj_rwkv_dw")

    do_b, dza, delta128 = _rowcall("fox_attn_pre", T, tb, [_rows(doa, tb, 1024), _rows(u, tb, 1024, 3), _rows(o_raw, tb, 1024)],
                                   lambda i, *a: (list(_attn_pre_math(*a)), []), [(1024, BF16), (1024, F32), (LANE, F32)])
    delta = delta128[:, 0:8].T.reshape(8, T, 1)
    dq, dk, dv, dc3 = _attn_bwd(u, c3, lse, do_b, delta, T)
    dfl, dbias_p = _gates_bwd(dc3.reshape(8, T), u, bias_pad, T, F_CB)

    def post_bwd_body(i, yv, rv, kpv, vv, zv, lnw, lnb, rkv, dobv):
        _, vjp = jax.vjp(_post_math, yv, rv, kpv, vv, zv, lnw, lnb, rkv)
        dy_, dr_, dkp_, dv_, dz_, dlnw, dlnb, drk = vjp(dobv)
        return [dy_, dr_, dkp_, dv_, dz_], [dlnw, dlnb, drk]

    dy_s, dr_p, dkp_p, dv_p, dzb, dlnw_p, dlnb_p, drk_p = _rowcall(
        "rwkv_post_bwd", T, tbh, post_rows + post_par + [_rows(dob, tbh, 1024)], post_bwd_body,
        [(1024, F32)] * 5, [(1, 1024)] * 3)
    dr_s, dlw, dkp_s, dv_s, dan, dbb = _scan_bwd(xs, lw, kp, an, bb, states, invs, dy_s, T)

    def prep_bwd_body(i, xk, xwd, xad, w0, a0, kkw, kaw, w2v, a2v, dlw_, dkp1, dkp2, dan_, dbb_, dr1, dr2, dv1, dv2, dz_):
        _, vjp = jax.vjp(_prep_math, xk, xwd, xad, w0, a0, kkw, kaw, w2v, a2v)
        dxk, dxwd, dxad, dw0, da0, dkk, dka, dw2, da2 = vjp((dlw_, dkp1 + dkp2, dan_, dbb_))
        return [[dr1 + dr2, dxk, dv1 + dv2, dz_], [dxwd, dxad]], [dw0, da0, dkk, dka, dw2, da2]

    cots = [dlw, dkp_s, dkp_p, dan, dbb, dr_s, dr_p, dv_s, dv_p, dzb]
    dxs, dxl, dw0_p, da0_p, dkk_p, dka_p, dw2_p, da2_p = _rowcall(
        "rwkv_prep_bwd", T, tbh, prep_rows + prep_par + [_rows(c_, tbh, 1024) for c_ in cots], prep_bwd_body,
        [(4096, F32), (256, F32)], [(1, 1024)] * 4 + [(LANE, 1024)] * 2)

    def shift_bwd_body(i, dm_, hm, dl_, hl, um, pm, ul, pl_, mm_, ml):
        last = i == T // tbh - 1
        outs, accs = [], []
        for dv_, hv, uv, pv, mv in ((dm_, hm, um, pm, mm_), (dl_, hl, ul, pl_, ml)):
            hv = jnp.where(last, 0.0, hv)
            nxt = pltpu.roll(jnp.concatenate([dv_, hv], axis=0), tbh + SUB - 1, 0)[:tbh]
            pv = jnp.where(i == 0, 0.0, pv)
            prev = pltpu.roll(jnp.concatenate([pv, uv], axis=0), 1, 0)[SUB:]
            outs.append(dv_ * (1.0 - mv) + nxt * mv)
            accs.append(jnp.sum(dv_ * (prev - uv), axis=0, keepdims=True))
        return outs, accs

    def halo_next(arr, w, cb):
        last_blk = T // SUB - 1
        return (arr, (SUB, w), lambda i: (jnp.minimum((i + 1) * (tbh // SUB), last_blk), cb))

    du_b, du_l, dmu_main_p, dmu_lora_p = _rowcall(
        "rwkv_shift_bwd", T, tbh,
        [_rows(dxs, tbh, 4096), halo_next(dxs, 4096, 0), _rows(dxl, tbh, 256), halo_next(dxl, 256, 0),
         _rows(u, tbh, 4096, 1), halo_prev(u, 4096, 1), _rows(u, tbh, 256, LORA_CB), halo_prev(u, 256, LORA_CB),
         _whole(mu_main), _whole(mu_lora)],
        shift_bwd_body, [(4096, BF16), (256, BF16)], [(1, 4096), (1, 256)])

    lora_g = jnp.concatenate([dw2_p[:96].reshape(96, N_DEV, 128).transpose(1, 0, 2),
                              da2_p[:96].reshape(96, N_DEV, 128).transpose(1, 0, 2)], axis=2).astype(BF16)
    gb = jnp.concatenate([dwpf.reshape(1024, N_DEV, 256).transpose(1, 0, 2),
                          dwpr.reshape(1024, N_DEV, 256).transpose(1, 0, 2),
                          dwo.reshape(N_DEV, 2048, 256), lora_g, jnp.zeros((N_DEV, RB - 4192, 256), BF16)], axis=1)
    xx, yy, cc = _place()
    chip_slots = jnp.stack([_slot((*chip, cc)) for chip in _other_chips(xx, yy)]).astype(jnp.int32)
    core = jnp.stack([cc]).astype(jnp.int32)
    du, r1b = _rowcall("assemble_du", T, tb,
                       [_rows(a_, tb, a_.shape[1]) for a_ in (dq, dk, dv, dza, du_b, dga, dgb, dfl, du_l)],
                       lambda i, *a: ([list(a[:8]) + [jnp.zeros((tb, LANE), BF16), a[8]]], []), [(NP, BF16)],
                       side=_exchange_pair_side(gb))
    sb = _pair_add("pair_add_packed", gb, r1b, chip_slots, RB // 3)
    dw_pad_t, t1b = _mm(du, h, ta=True, out_dtype=BF16, tm=1280, tn=1024, name="mm_in_dw",
                        side=_exchange_first_side(sb))
    pb = _axis_add("axis_add_packed", sb, t1b, core, RB // 3)
    ga, t2b = _split_rows(dw_pad_t, [(d, lo, pad_col, n) for pad_col, d, lo, n in pieces], NI,
                          _exchange_second_side(pb))
    r1a = _exchange_pair(ga)
    sa = _pair_add("pair_add_w_in", ga, r1a, chip_slots, 512, by_cols=True)
    ga_own = lax.dynamic_index_in_dim(ga, me, 0, keepdims=True)
    gb_own = lax.dynamic_index_in_dim(gb, me, 0, keepdims=True)
    dh, t1a = _mm(du, w_pad, tb=True, tm=1024, tn=1024, tk=NP // 10, name="mm_in_dx", side=_exchange_first_side(sa))
    pa = _axis_add("axis_add_w_in", sa, t1a, core, 512, by_cols=True)

    def rms_bwd_body(i, xv, g, dhv, doutv):
        _, vjp = jax.vjp(_rms_math, xv, g)
        dx_, dg_ = vjp(dhv)
        return [dx_ + doutv], [dg_]

    grad_x2, dng_p, t2a = _rowcall(
        "rms_bwd", T, tb, [_rows(x2, tb, D), _whole(norm_gain), _rows(dh, tb, D), _rows(dout, tb, D)],
        rms_bwd_body, [(D, F32)], [(1, D)], side=_exchange_second_side(pa))

    w_in_outs = _adamw("adamw_w_in", w_in[0].T, m_w_in[0].T, v_w_in[0].T, [(ga_own, 1), (r1a, 1), (t1a, 1), (t2a, 1)], 256, by_cols=True)
    g_in, d_in, m_in, v_in = [o.T for o in w_in_outs]

    dmu = jnp.concatenate([dmu_main_p, dmu_lora_p[:, 0:96], dmu_lora_p[:, 128:224]], axis=1)
    small_parts = [dng_p, dbias_p[:, 0:8], dmu, dw0_p, da0_p, dkk_p, dka_p, drk_p, dlnw_p, dlnb_p, dgf_p, loss_p]
    SR = 128
    small = _pad_cols(jnp.concatenate(small_parts, axis=1), SR * LANE).reshape(SR, LANE)
    rs = _allgather(small, "allgather_small")
    pk = lambda pf, pr, wo_, w2_, a2_: _pack_b(pf[0], pr[0], wo_[0], w2_[0], a2_[0], RB)
    outs_b = _adamw("adamw_packed", pk(w_proj_fox, w_proj_rwkv, w_out, rwkv_w2, rwkv_a2),
                    pk(m_w_proj_fox, m_w_proj_rwkv, m_w_out, m_rwkv_w2, m_rwkv_a2),
                    pk(v_w_proj_fox, v_w_proj_rwkv, v_w_out, v_rwkv_w2, v_rwkv_a2), [(gb_own, 1), (r1b, 1), (t1b, 1), (t2b, 1)], RB // 3)

    def pack_small(ng, fb, sm, w0, a0, kk_, ka_, rk_, lnw, lnb, fg):
        parts = [ng, fb, sm, w0, a0, kk_, ka_, rk_.reshape(1, 1024), lnw, lnb, fg.reshape(1, D), jnp.zeros((1, 1), F32)]
        return _pad_cols(jnp.concatenate(parts, axis=1), SR * LANE).reshape(SR, LANE)

    outs_s = _adamw("adamw_small",
                    pack_small(norm_gain, fox_forget_bias, rwkv_shift_mix, rwkv_w0, rwkv_a0, rwkv_k_k, rwkv_k_a, rwkv_r_k,
                               rwkv_ln_w, rwkv_ln_b, final_norm_gain),
                    pack_small(m_norm_gain, m_fox_forget_bias, m_rwkv_shift_mix, m_rwkv_w0, m_rwkv_a0, m_rwkv_k_k, m_rwkv_k_a,
                               m_rwkv_r_k, m_rwkv_ln_w, m_rwkv_ln_b, m_final_norm_gain),
                    pack_small(v_norm_gain, v_fox_forget_bias, v_rwkv_shift_mix, v_rwkv_w0, v_rwkv_a0, v_rwkv_k_k, v_rwkv_k_a,
                               v_rwkv_r_k, v_rwkv_ln_w, v_rwkv_ln_b, v_final_norm_gain),
                    [(rs, N_DEV)], SR)

    def unpack_b(pkd):
        return dict(w_proj_fox=pkd[0:1024][None], w_proj_rwkv=pkd[1024:2048][None], w_out=pkd[2048:4096].reshape(1, 256, D),
                    rwkv_w2=pkd[4096:4192, 0:128][None], rwkv_a2=pkd[4096:4192, 128:256][None])

    def unpack_s(pkd):
        flat = pkd.reshape(1, SR * LANE)
        names = [("norm_gain", D), ("fox_forget_bias", 8), ("rwkv_shift_mix", 4288), ("rwkv_w0", 1024), ("rwkv_a0", 1024),
                 ("rwkv_k_k", 1024), ("rwkv_k_a", 1024), ("rwkv_r_k", 1024), ("rwkv_ln_w", 1024), ("rwkv_ln_b", 1024),
                 ("final_norm_gain", D), ("loss", 1)]
        out, off = {}, 0
        for nm, n in names:
            out[nm] = flat[:, off:off + n]
            off += n
        out["rwkv_r_k"] = out["rwkv_r_k"].reshape(1, 16, 64)
        out["final_norm_gain"] = out["final_norm_gain"].reshape(D)
        return out

    order = ["norm_gain", "w_in", "fox_forget_bias", "rwkv_shift_mix", "rwkv_w0", "rwkv_w2", "rwkv_a0", "rwkv_a2", "rwkv_k_k",
             "rwkv_k_a", "rwkv_r_k", "rwkv_ln_w", "rwkv_ln_b", "w_proj_fox", "w_proj_rwkv", "w_out", "final_norm_gain"]
    result = []
    loss = None
    for kind, big in enumerate((g_in, d_in, m_in, v_in)):
        d = {**unpack_b(outs_b[kind]), **unpack_s(outs_s[kind]), "w_in": big[None]}
        if kind == 0:
            loss = d["loss"].reshape(())
        result += [d[n] for n in order]
    return (loss, grad_x2[None], *result)
```

```python
import functools

import jax
import jax.numpy as jnp
from jax import lax
from jax.experimental import pallas as pl
from jax.experimental.pallas import tpu as pltpu

F32 = jnp.float32
BF16 = jnp.bfloat16
HI = lax.Precision.HIGHEST
H3 = lax.Precision.HIGH
MESH = pl.DeviceIdType.MESH

FOX_HD = 128
RW_HD = 64
RMS_EPS = 1e-6
GN_EPS = 64e-5
L2_EPS = 1e-12
ADAM_LR = 0.001
ADAM_B1 = 0.9
ADAM_B2 = 0.999
ADAM_EPS = 1e-08
ADAM_WD = 0.01
ADAM_STEP = 10

LANE = 128
SUB = 8
VMEM_LIMIT = 56 * 1024 * 1024
N_DEV = 8
CHUNK = 128
SCAN_GROUP = 1
SCAN_PAIRS = 8
PS = None
NEG = -1e30


def _scan_shape(T):
    c = min(CHUNK, T)
    return c, min(SCAN_GROUP, T // c)


def _cp(sem=None):
    return pltpu.CompilerParams(dimension_semantics=sem, vmem_limit_bytes=VMEM_LIMIT)


def _sigmoid(x):
    return jax.nn.sigmoid(x)


def _softplus(x):
    return jnp.maximum(x, 0.0) + jnp.log(1.0 + jnp.exp(-jnp.abs(x)))


def _nn(a, b, prec=None):
    return lax.dot_general(a, b, (((1,), (0,)), ((), ())), precision=prec, preferred_element_type=F32)


def _nt(a, b, prec=None):
    return lax.dot_general(a, b, (((1,), (1,)), ((), ())), precision=prec, preferred_element_type=F32)


def _tn(a, b, prec=None):
    return lax.dot_general(a, b, (((0,), (0,)), ((), ())), precision=prec, preferred_element_type=F32)


def _iota2(shape, dim):
    return lax.broadcasted_iota(jnp.int32, shape, dim)


def _seg_sum(x):
    r = _iota2((LANE, LANE), 0) // RW_HD
    c = _iota2((LANE, LANE), 1) // RW_HD
    bd = (r == c).astype(F32)
    parts = [_nn(x[:, j * LANE:(j + 1) * LANE], bd, H3) for j in range(x.shape[1] // LANE)]
    return parts[0] if len(parts) == 1 else jnp.concatenate(parts, axis=1)


def _mm(a, b, *, ta=False, tb=False, out_dtype=F32, tm=1024, tn=1024, tk=None, name, side=None):
    assert not (ta and tb)
    K, M = a.shape if ta else a.shape[::-1]
    N = b.shape[0] if tb else b.shape[1]
    tm, tn = min(tm, M), min(tn, N)
    tk = K if tk is None else tk
    nk = K // tk
    assert M % tm == 0 and N % tn == 0 and K % tk == 0
    gi, gj = M // tm, N // tn
    a_spec = pl.BlockSpec((tk, tm), lambda i, j, k: (k, i)) if ta else pl.BlockSpec((tm, tk), lambda i, j, k: (i, k))
    b_spec = pl.BlockSpec((tn, tk), lambda i, j, k: (j, k)) if tb else pl.BlockSpec((tk, tn), lambda i, j, k: (k, j))
    side_ins, side_outs, side_scr, side_make = side if side is not None else ((), (), (), None)
    n_si, n_so = len(side_ins), len(side_outs)
    n_acc = 0 if nk == 1 else 1

    def body(*refs):
        a_ref, b_ref = refs[:2]
        o_ref = refs[2 + n_si]
        scr = refs[3 + n_si + n_so:]
        k = pl.program_id(2)
        if side_make is not None:
            start, finish = side_make(refs[2:2 + n_si], refs[3 + n_si:3 + n_si + n_so], scr[n_acc:])
            first = jnp.logical_and(jnp.logical_and(pl.program_id(0) == 0, pl.program_id(1) == 0), k == 0)
            last = jnp.logical_and(jnp.logical_and(pl.program_id(0) == gi - 1, pl.program_id(1) == gj - 1), k == nk - 1)
            pl.when(first)(start)
        av = a_ref[...].astype(BF16)
        bv = b_ref[...].astype(BF16)
        p = _tn(av, bv) if ta else _nt(av, bv) if tb else _nn(av, bv)
        if nk == 1:
            o_ref[...] = p.astype(out_dtype)
        else:
            acc_ref = scr[0]

            @pl.when(k == 0)
            def _():
                acc_ref[...] = p

            @pl.when(k > 0)
            def _():
                acc_ref[...] += p

            @pl.when(k == nk - 1)
            def _():
                o_ref[...] = acc_ref[...].astype(out_dtype)
        if side_make is not None:
            pl.when(last)(finish)

    any_spec = pl.BlockSpec(memory_space=pl.ANY)
    res = pl.pallas_call(
        body, name=name,
        out_shape=[jax.ShapeDtypeStruct((M, N), out_dtype)] + list(side_outs),
        grid=(gi, gj, nk),
        in_specs=[a_spec, b_spec] + [any_spec] * n_si,
        out_specs=[pl.BlockSpec((tm, tn), lambda i, j, k: (i, j))] + [any_spec] * n_so,
        scratch_shapes=([] if nk == 1 else [pltpu.VMEM((tm, tn), F32)]) + list(side_scr),
        compiler_params=_cp(("arbitrary",) * 3 if side is not None else ("parallel", "parallel", "arbitrary")),
    )(a, b, *side_ins)
    return res if side is not None else res[0]


def _rows(arr, tb, w, cb=0):
    return (arr, (tb, w), lambda i: (i, cb))


def _whole(arr):
    nd = arr.ndim
    return (arr, arr.shape, lambda i: (0,) * nd)


def _rowcall(name, T, tb, ins, body, outs, accs=(), side=None):
    n_in, n_out, n_acc = len(ins), len(outs), len(accs)
    side_ins, side_outs, side_scr, side_make = side if side is not None else ((), (), (), None)
    n_si = len(side_ins)

    def kern(*refs):
        i = pl.program_id(0)
        if side_make is not None:
            side_refs = refs[n_in + n_si + n_out + n_acc:]
            start, finish = side_make(refs[n_in:n_in + n_si], side_refs[:len(side_outs)], side_refs[len(side_outs):])
            pl.when(i == 0)(start)
            refs = refs[:n_in] + refs[n_in + n_si:]
        vals = [r[...] for r in refs[:n_in]]
        ro, ao = body(i, *vals)
        for r, v in zip(refs[n_in:n_in + n_out], ro):
            if isinstance(v, (list, tuple)):
                off = 0
                for piece in v:
                    w = piece.shape[1]
                    r[:, off:off + w] = piece.astype(r.dtype)
                    off += w
            else:
                r[...] = v.astype(r.dtype)
        if accs:
            acc_refs = refs[n_in + n_out:n_in + n_out + n_acc]

            @pl.when(i == 0)
            def _():
                for r in acc_refs:
                    r[...] = jnp.zeros(r.shape, F32)

            for r, v in zip(acc_refs, ao):
                r[...] += v
        if side_make is not None:
            pl.when(i == T // tb - 1)(finish)

    any_spec = pl.BlockSpec(memory_space=pl.ANY)
    out_shape = [jax.ShapeDtypeStruct((T, w), dt) for (w, dt) in outs] + [jax.ShapeDtypeStruct(s, F32) for s in accs]
    out_specs = [pl.BlockSpec((tb, w), lambda i: (i, 0)) for (w, dt) in outs] + [pl.BlockSpec(s, lambda i: (0, 0)) for s in accs]
    res = pl.pallas_call(
        kern, name=name,
        out_shape=out_shape + list(side_outs),
        grid=(T // tb,),
        in_specs=[pl.BlockSpec(bs, im) for (_, bs, im) in ins] + [any_spec] * n_si,
        out_specs=out_specs + [any_spec] * len(side_outs),
        scratch_shapes=list(side_scr),
        compiler_params=_cp(("arbitrary",)),
    )(*[a for (a, _, _) in ins], *side_ins)
    return res


def _rms_math(x, g):
    r = lax.rsqrt(jnp.mean(x * x, axis=-1, keepdims=True) + RMS_EPS)
    return x * r * g


def _merge_math(ga, gb, pa, pb):
    return _sigmoid(ga) * pa + _sigmoid(gb) * pb


def _prep_math(xk, xwd, xad, w0, a0, kk_w, ka_w, w2p, a2p):
    z = w0 + _nn(jnp.tanh(xwd), w2p, H3)
    w = -_softplus(-z) - 0.5
    lw = -jnp.exp(w)
    ag = _sigmoid(a0 + _nn(xad, a2p, H3))
    p = xk * kk_w
    n = jnp.maximum(jnp.sqrt(_seg_sum(p * p)), L2_EPS)
    kk = p / n
    kp = xk * (1.0 + (ag - 1.0) * ka_w)
    return lw, kp, -kk, kk * ag


def _post_math(y, r, kp, v, z, lnw, lnb, rk):
    inv = 1.0 / RW_HD
    mu = _seg_sum(y) * inv
    d = y - mu
    var = _seg_sum(d * d) * inv
    yn = d * lax.rsqrt(var + GN_EPS) * lnw + lnb
    bonus = _seg_sum(r * kp * rk) * v
    return (yn + bonus) * (z * _sigmoid(z))


def _neumann(ms, depth):
    eye = (_iota2(ms[0].shape, 0) == _iota2(ms[0].shape, 1)).astype(F32)
    width = ms[0].shape[1]
    mp = [_nn(m, m, PS) for m in ms]
    inv = [eye + m for m in ms]
    n = 2
    while n < depth:
        last = 2 * n >= depth
        for i in range(len(ms)):
            if last:
                inv[i] = inv[i] + _nn(mp[i], inv[i], PS)
            else:
                z = _nn(mp[i], jnp.concatenate([mp[i], inv[i]], axis=1), PS)
                mp[i], inv[i] = z[:, :width], inv[i] + z[:, width:]
        n *= 2
    return inv


@functools.partial(jax.custom_vjp, nondiff_argnums=(1,))
def _unit_inverses(ms, depth):
    if depth <= RW_HD:
        return tuple(_neumann(list(ms), depth))
    h, n = depth // 2, len(ms)
    cat = jnp.concatenate
    z = jnp.zeros((h, h), F32)

    def heads(m, r, c):
        b0 = m[r * h:(r + 1) * h, c * h:(c + 1) * h]
        b1 = m[depth + r * h:depth + (r + 1) * h, depth + c * h:depth + (c + 1) * h]
        return cat([cat([b0, z], axis=1), cat([z, b1], axis=1)], axis=0)

    diag = _neumann([heads(m, 0, 0) for m in ms] + [heads(m, 1, 1) for m in ms], h)
    ta, td = diag[:n], diag[n:]
    low = [_nn(heads(m, 1, 0), a, PS) for m, a in zip(ms, ta)]
    low = [_nn(d, x, PS) for d, x in zip(td, low)]
    out = []
    for a, x, d in zip(ta, low, td):
        rows = []
        for hd in (0, 1):
            sl = slice(hd * h, (hd + 1) * h)
            top, bot = [a[sl, sl], z], [x[sl, sl], d[sl, sl]]
            pad = [z, z]
            rows.append(cat(top + pad if hd == 0 else pad + top, axis=1))
            rows.append(cat(bot + pad if hd == 0 else pad + bot, axis=1))
        out.append(cat(rows, axis=0))
    return tuple(out)


def _unit_inverses_fwd(ms, depth):
    inv = _unit_inverses(ms, depth)
    return inv, inv


def _unit_inverses_bwd(depth, inv, cts):
    left = [_tn(t, g, PS) for t, g in zip(inv, cts)]
    return (tuple(_nt(l, t, PS) for l, t in zip(left, inv)),)


_unit_inverses.defvjp(_unit_inverses_fwd, _unit_inverses_bwd)


@jax.custom_vjp
def _known_inverses(ms, inv):
    return inv


def _known_inverses_fwd(ms, inv):
    return inv, inv


def _known_inverses_bwd(inv, cts):
    dms = _unit_inverses_bwd(None, inv, cts)[0]
    return dms, tuple(jnp.zeros_like(t) for t in inv)


_known_inverses.defvjp(_known_inverses_fwd, _known_inverses_bwd)


def _scan_group(s0s, *flat, known_inv=None, with_inv=False):
    P = len(s0s)
    G = len(flat) // (6 * P)
    ch = [flat[6 * i:6 * i + 6] for i in range(P * G)]
    C = ch[0][0].shape[0]
    C2 = 2 * C
    cat = jnp.concatenate
    m0 = _iota2((1, LANE), 1) < RW_HD
    mask0 = m0.astype(F32)
    mask1 = 1.0 - mask0
    r2 = _iota2((C2, C2), 0)
    c2 = _iota2((C2, C2), 1)
    dist = r2 - c2
    in_head = dist <= r2 % C
    lower = (_iota2((C, C), 0) >= _iota2((C, C), 1)).astype(F32)
    bd = (_iota2((LANE, LANE), 0) // RW_HD) == (_iota2((LANE, LANE), 1) // RW_HD)

    def tri(m, strict):
        return jnp.where(dist > 0 if strict else dist >= 0, jnp.where(in_head, m, 0.0), 0.0)

    def sel(z):
        return jnp.where(m0, z[:C], z[C:])

    gs = [_nn(lower, c[1], HI) for c in ch]
    pre = []
    for (r, lw, k, v, a, b), g in zip(ch, gs):
        g_end = jnp.sum(lw, axis=0, keepdims=True)
        gm = g - jnp.sum(lw[:C // 2], axis=0, keepdims=True)
        en = jnp.exp(-gm)
        ec = jnp.exp(g_end - g)
        pre.append(dict(at=a * jnp.exp(g - lw), rt=r * jnp.exp(g), am=a * jnp.exp(gm - lw), rm=r * jnp.exp(gm),
                        bt=b * en, kt=k * en, bh=b * ec, kh=k * ec, dec=jnp.exp(g_end), v=v))
    grams = [_nt(cat([p["am"] * mask0, p["am"] * mask1, p["rm"] * mask0, p["rm"] * mask1], axis=0),
                 cat([p["bt"], p["bt"], p["kt"], p["kt"]], axis=0), PS) for p in pre]
    mab = tuple(tri(gm[:C2, :C2], True) for gm in grams)
    tinv = _unit_inverses(mab, C) if known_inv is None else _known_inverses(mab, known_inv)
    xv =[sel(_nn(tri(gm[:C2, C2:], True), cat([p["v"], p["v"]], axis=0), PS)) for gm, p in zip(grams, pre)]
    ys, s = [None] * (P * G), list(s0s)
    for i in range(G):
        for q in range(P):
            n = q * G + i
            p, gm = pre[n], grams[n]
            sx = _nt(cat([p["at"], p["rt"]], axis=0), s[q], PS)
            x = sx[:C] + xv[n]
            u = sel(_nn(tinv[n], cat([x, x], axis=0), PS))
            v = p["v"]
            ys[n] = sx[C:] + sel(_nn(cat([tri(gm[C2:, :C2], False), tri(gm[C2:, C2:], False)], axis=1),
                                     cat([u, u, v, v], axis=0), PS))
            s[q] = s[q] * p["dec"] + jnp.where(bd, _tn(cat([u, v], axis=0), cat([p["bh"], p["kh"]], axis=0), PS), 0.0)
    return (tuple(ys), tuple(s), tinv) if with_inv else (tuple(ys), tuple(s))


def _scan_fwd(xs, lw, kp, an, bb, T):
    C, G = _scan_shape(T)
    P = SCAN_PAIRS
    nc = T // (C * G)
    npair = 1024 // LANE

    def kern(r_ref, lw_ref, k_ref, v_ref, a_ref, b_ref, y_ref, st_ref, inv_ref, s_scr):
        n = pl.program_id(1)

        @pl.when(n == 0)
        def _():
            s_scr[...] = jnp.zeros(s_scr.shape, F32)

        st_ref[0] = s_scr[...]
        ins = (r_ref, lw_ref, k_ref, v_ref, a_ref, b_ref)
        ys, s1, inv = _scan_group(
            tuple(s_scr[q] for q in range(P)),
            *[ref[i * C:(i + 1) * C, q * LANE:(q + 1) * LANE] for q in range(P) for i in range(G) for ref in ins], with_inv=True)
        for q in range(P):
            for i in range(G):
                y_ref[i * C:(i + 1) * C, q * LANE:(q + 1) * LANE] = ys[q * G + i]
                inv_ref[0, 0, q * G + i] = inv[q * G + i]
            s_scr[q] = s1[q]

    def col(off):
        return pl.BlockSpec((C * G, P * LANE), lambda p, n: (n, off // P + p))

    return pl.pallas_call(
        kern, name="rwkv_scan_fwd",
        out_shape=[jax.ShapeDtypeStruct((T, 1024), F32), jax.ShapeDtypeStruct((nc, npair, LANE, LANE), F32),
                   jax.ShapeDtypeStruct((nc, npair // P, P * G, 2 * C, 2 * C), F32)],
        grid=(npair // P, nc),
        in_specs=[col(0), col(0), col(0), col(16), col(0), col(0)],
        out_specs=[col(0), pl.BlockSpec((1, P, LANE, LANE), lambda p, n: (n, p, 0, 0)),
                   pl.BlockSpec((1, 1, P * G, 2 * C, 2 * C), lambda p, n: (n, p, 0, 0, 0))],
        scratch_shapes=[pltpu.VMEM((P, LANE, LANE), F32)],
        compiler_params=_cp(("parallel", "arbitrary")),
    )(xs, lw, kp, xs, an, bb)


def _scan_bwd(xs, lw, kp, an, bb, states, invs, dy, T):
    C, G = _scan_shape(T)
    P = SCAN_PAIRS
    nc = T // (C * G)
    npair = 1024 // LANE

    def kern(r_ref, lw_ref, k_ref, v_ref, a_ref, b_ref, st_ref, inv_ref, dy_ref, dr_ref, dlw_ref, dk_ref, dv_ref, da_ref, db_ref,
             ds_scr):
        n = pl.program_id(1)

        @pl.when(n == 0)
        def _():
            ds_scr[...] = jnp.zeros(ds_scr.shape, F32)

        ins = (r_ref, lw_ref, k_ref, v_ref, a_ref, b_ref)
        units = [(q, i) for q in range(P) for i in range(G)]
        known = tuple(inv_ref[0, 0, q * G + i] for q, i in units)
        _, vjp = jax.vjp(functools.partial(_scan_group, known_inv=known), tuple(st_ref[0, q] for q in range(P)),
                         *[ref[i * C:(i + 1) * C, q * LANE:(q + 1) * LANE] for q, i in units for ref in ins])
        grads = vjp((tuple(dy_ref[i * C:(i + 1) * C, q * LANE:(q + 1) * LANE] for q, i in units),
                     tuple(ds_scr[q] for q in range(P))))
        for q in range(P):
            ds_scr[q] = grads[0][q]
        outs = (dr_ref, dlw_ref, dk_ref, dv_ref, da_ref, db_ref)
        for n_, (q, i) in enumerate(units):
            for t, ref in enumerate(outs):
                ref[i * C:(i + 1) * C, q * LANE:(q + 1) * LANE] = grads[1 + 6 * n_ + t]

    def col(off):
        return pl.BlockSpec((C * G, P * LANE), lambda p, n: (nc - 1 - n, off // P + p))

    return pl.pallas_call(
        kern, name="rwkv_scan_bwd",
        out_shape=[jax.ShapeDtypeStruct((T, 1024), F32)] * 6,
        grid=(npair // P, nc),
        in_specs=[col(0), col(0), col(0), col(16), col(0), col(0),
                  pl.BlockSpec((1, P, LANE, LANE), lambda p, n: (nc - 1 - n, p, 0, 0)),
                  pl.BlockSpec((1, 1, P * G, 2 * C, 2 * C), lambda p, n: (nc - 1 - n, p, 0, 0, 0)), col(0)],
        out_specs=[col(0)] * 6,
        scratch_shapes=[pltpu.VMEM((P, LANE, LANE), F32)],
        compiler_params=_cp(("parallel", "arbitrary")),
    )(xs, lw, kp, xs, an, bb, states, invs, dy)


def _gates_fwd(u, bias_pad, T, f_cb):
    nb = T // LANE

    def kern(f_ref, b_ref, c_ref):
        x = f_ref[...] + b_ref[...]
        lf = jnp.minimum(x, 0.0) - jnp.log(1.0 + jnp.exp(-jnp.abs(x)))
        lft = lf.T
        ut = (_iota2((LANE, LANE), 0) <= _iota2((LANE, LANE), 1)).astype(F32)
        carry = jnp.zeros((LANE, 1), F32)
        for blk in range(nb):
            seg = lft[:, blk * LANE:(blk + 1) * LANE]
            cs = _nn(seg, ut, HI) + carry
            c_ref[:, blk * LANE:(blk + 1) * LANE] = cs[:SUB, :]
            carry = carry + jnp.sum(seg, axis=1, keepdims=True)

    return pl.pallas_call(
        kern, name="fox_gates_fwd",
        out_shape=jax.ShapeDtypeStruct((SUB, T), F32),
        grid=(1,),
        in_specs=[pl.BlockSpec((T, LANE), lambda i: (0, f_cb)), pl.BlockSpec((1, LANE), lambda i: (0, 0))],
        out_specs=pl.BlockSpec((SUB, T), lambda i: (0, 0)),
        compiler_params=_cp(("arbitrary",)),
    )(u, bias_pad)


def _gates_bwd(dc, u, bias_pad, T, f_cb):
    nb = T // LANE

    def kern(dc_ref, f_ref, b_ref, dfl_ref, db_ref):
        dcv = jnp.concatenate([dc_ref[...], jnp.zeros((LANE - SUB, T), F32)], axis=0)
        lt = (_iota2((LANE, LANE), 0) >= _iota2((LANE, LANE), 1)).astype(F32)
        carry = jnp.zeros((LANE, 1), F32)
        pieces = [None] * nb
        for blk in range(nb - 1, -1, -1):
            seg = dcv[:, blk * LANE:(blk + 1) * LANE]
            pieces[blk] = _nn(seg, lt, HI) + carry
            carry = carry + jnp.sum(seg, axis=1, keepdims=True)
        dlf = (pieces[0] if nb == 1 else jnp.concatenate(pieces, axis=1)).T
        x = f_ref[...] + b_ref[...]
        dfl = dlf * _sigmoid(-x)
        dfl_ref[...] = dfl
        db_ref[...] = jnp.sum(dfl, axis=0, keepdims=True)

    return pl.pallas_call(
        kern, name="fox_gates_bwd",
        out_shape=[jax.ShapeDtypeStruct((T, LANE), F32), jax.ShapeDtypeStruct((1, LANE), F32)],
        grid=(1,),
        in_specs=[pl.BlockSpec((SUB, T), lambda i: (0, 0)), pl.BlockSpec((T, LANE), lambda i: (0, f_cb)),
                  pl.BlockSpec((1, LANE), lambda i: (0, 0))],
        out_specs=[pl.BlockSpec((T, LANE), lambda i: (0, 0)), pl.BlockSpec((1, LANE), lambda i: (0, 0))],
        compiler_params=_cp(("arbitrary",)),
    )(dc, u, bias_pad)


ATTN_HEADS = 2


def _attn_block(T):
    return 512 if T % 512 == 0 and T >= 1024 else 128


def _attn_fwd(u, c3, T):
    H, HP = 8, ATTN_HEADS
    bq = _attn_block(T)
    nq = T // bq
    scale = FOX_HD ** -0.5
    lanes = [slice(h * LANE, (h + 1) * LANE) for h in range(HP)]

    def kern(q_ref, k_ref, v_ref, z_ref, cq_ref, ck_ref, o_ref, oa_ref, lse_ref):
        i = pl.program_id(1)
        q = [(q_ref[:, ln] * scale).astype(BF16) for ln in lanes]
        c0 = [cq_ref[h][:, 0:1] for h in range(HP)]

        def step(j, carry, diagonal=False):
            off = pl.multiple_of(j * bq, bq)
            s = [_nt(q[h], k_ref[pl.ds(off, bq), lanes[h]].astype(BF16)) + (c0[h] - ck_ref[h, :, pl.ds(off, bq)])
                 for h in range(HP)]
            ps, out = [], []
            for h in range(HP):
                m, l, acc = carry[h]
                sh = s[h]
                if diagonal:
                    sh = jnp.where(_iota2((bq, bq), 1) <= _iota2((bq, bq), 0), sh, NEG)
                m_new = jnp.maximum(m, jnp.max(sh, axis=1, keepdims=True))
                p = jnp.exp(sh - m_new)
                alpha = jnp.exp(m - m_new)
                p_hi = p.astype(BF16)
                ps.append((p_hi, (p - p_hi.astype(F32)).astype(BF16)))
                out.append((m_new, alpha * l + jnp.sum(p, axis=1, keepdims=True), alpha * acc))
            res = []
            for h, (m, l, acc) in enumerate(out):
                vj = v_ref[pl.ds(off, bq), lanes[h]].astype(BF16)
                res.append((m, l, acc + _nn(ps[h][0], vj) + _nn(ps[h][1], vj)))
            return tuple(res)

        init = tuple((jnp.full((bq, 1), NEG, F32), jnp.zeros((bq, 1), F32), jnp.zeros((bq, FOX_HD), F32)) for _ in range(HP))
        res = step(i, lax.fori_loop(0, i, step, init), diagonal=True)
        for h, (m, l, acc) in enumerate(res):
            o = acc / l
            z = z_ref[:, lanes[h]]
            o_ref[:, lanes[h]] = o
            oa_ref[:, lanes[h]] = (o * z * _sigmoid(z)).astype(BF16)
            lse_ref[h] = m + jnp.log(l)

    W = HP * LANE
    return pl.pallas_call(
        kern, name="fox_attn_fwd",
        out_shape=[jax.ShapeDtypeStruct((T, 1024), F32), jax.ShapeDtypeStruct((T, 1024), BF16),
                   jax.ShapeDtypeStruct((H, T, 1), F32)],
        grid=(H // HP, nq),
        in_specs=[pl.BlockSpec((bq, W), lambda g, i: (i, g)),
                  pl.BlockSpec((T, W), lambda g, i: (0, 8 // HP + g)),
                  pl.BlockSpec((T, W), lambda g, i: (0, 16 // HP + g)),
                  pl.BlockSpec((bq, W), lambda g, i: (i, 24 // HP + g)),
                  pl.BlockSpec((HP, 1, bq), lambda g, i: (g, 0, i)),
                  pl.BlockSpec((HP, 1, T), lambda g, i: (g, 0, 0))],
        out_specs=[pl.BlockSpec((bq, W), lambda g, i: (i, g)),
                   pl.BlockSpec((bq, W), lambda g, i: (i, g)),
                   pl.BlockSpec((HP, bq, 1), lambda g, i: (g, i, 0))],
        compiler_params=_cp(("parallel", "arbitrary")),
    )(u, u, u, u, c3, c3)


def _attn_probs(s, lse_i, diagonal):
    if not diagonal:
        return jnp.exp(s - lse_i)
    keep = _iota2(s.shape, 1) <= _iota2(s.shape, 0)
    return jnp.where(keep, jnp.exp(jnp.where(keep, s, NEG) - lse_i), 0.0)


def _attn_pre_math(doa, z, o):
    sg = _sigmoid(z)
    do = (doa * z * sg).astype(BF16)
    dz = doa * o * (sg * (1.0 + z * (1.0 - sg)))
    head_of = (_iota2((o.shape[1], LANE), 0) // FOX_HD == _iota2((o.shape[1], LANE), 1)).astype(F32)
    return do, dz, _nn(do.astype(F32) * o, head_of, HI)


def _attn_bwd(u, c3, lse, do, delta, T):
    H, HP = 8, ATTN_HEADS
    bq = _attn_block(T)
    nq = T // bq
    scale = FOX_HD ** -0.5
    lanes = [slice(h * LANE, (h + 1) * LANE) for h in range(HP)]

    def kern(q_ref, k_ref, v_ref, c_ref, lse_ref, do_ref, dl_ref, dq_ref, dk_ref, dv_ref, dc_ref):
        j = pl.program_id(1)

        @pl.when(j == 0)
        def _():
            dq_ref[...] = jnp.zeros(dq_ref.shape, F32)

        kj = [k_ref[:, ln].astype(BF16) for ln in lanes]
        vj = [v_ref[:, ln].astype(BF16) for ln in lanes]
        joff = pl.multiple_of(j * bq, bq)
        ck = [c_ref[h, :, pl.ds(joff, bq)] for h in range(HP)]

        def step(i, carry, diagonal=False):
            off = pl.multiple_of(i * bq, bq)
            qs = [(q_ref[pl.ds(off, bq), ln] * scale).astype(BF16) for ln in lanes]
            dob = [do_ref[pl.ds(off, bq), ln] for ln in lanes]
            s = [_nt(qs[h], kj[h]) + (c_ref[h, :, pl.ds(off, bq)][:, 0:1] - ck[h]) for h in range(HP)]
            dp = [_nt(dob[h], vj[h]) for h in range(HP)]
            pb, dsb, dcs = [], [], []
            for h in range(HP):
                p = _attn_probs(s[h], lse_ref[h, pl.ds(off, bq), :], diagonal)
                ds = p * (dp[h] - dl_ref[h, pl.ds(off, bq), :])
                pb.append(p.astype(BF16))
                dsb.append(ds.astype(BF16))
                dcs.append(jnp.sum(ds, axis=0, keepdims=True))
            out = []
            for h, (dk, dv, dc) in enumerate(carry):
                dq_ref[pl.ds(off, bq), lanes[h]] += _nn(dsb[h], kj[h]) * scale
                out.append((dk + _tn(dsb[h], qs[h]), dv + _tn(pb[h], dob[h]), dc - dcs[h]))
            return tuple(out)

        init = tuple((jnp.zeros((bq, FOX_HD), F32), jnp.zeros((bq, FOX_HD), F32), jnp.zeros((1, bq), F32)) for _ in range(HP))
        res = lax.fori_loop(j + 1, nq, step, step(j, init, diagonal=True))
        for h, (dk, dv, dc) in enumerate(res):
            dk_ref[:, lanes[h]] = dk
            dv_ref[:, lanes[h]] = dv
            dc_ref[h] = dc

    W = HP * LANE
    full = lambda cb: pl.BlockSpec((T, W), lambda g, j: (0, cb // HP + g))
    blk = lambda cb: pl.BlockSpec((bq, W), lambda g, j: (j, cb // HP + g))
    col1 = pl.BlockSpec((HP, T, 1), lambda g, j: (g, 0, 0))
    return pl.pallas_call(
        kern, name="fox_attn_bwd",
        out_shape=[jax.ShapeDtypeStruct((T, 1024), F32)] * 3 + [jax.ShapeDtypeStruct((H, 1, T), F32)],
        grid=(H // HP, nq),
        in_specs=[full(0), blk(8), blk(16), pl.BlockSpec((HP, 1, T), lambda g, j: (g, 0, 0)), col1, full(0), col1],
        out_specs=[full(0), blk(0), blk(0), pl.BlockSpec((HP, 1, bq), lambda g, j: (g, 0, j))],
        compiler_params=_cp(("parallel", "arbitrary")),
    )(u, u, u, c3, lse, do, delta)


def _place():
    return lax.axis_index("x"), lax.axis_index("y"), lax.axis_index("c")


def _slot(p):
    return 4 * p[0] + 2 * p[1] + p[2]


def _other_chips(x, y):
    return [(1 - x, y), (x, 1 - y), (1 - x, 1 - y)]


def _allgather_steps(in_refs, out_refs, scratch):
    (src,), (dst,) = in_refs, out_refs
    send_sems, recv_sems, local_sem = scratch
    x, y, c = _place()
    me, sibling = (x, y, c), (x, y, 1 - c)
    chips = _other_chips(x, y)

    def copy(k, block, to, from_input=False):
        d = dst.at[_slot(block)]
        return pltpu.make_async_remote_copy(
            src_ref=src if from_input else d, dst_ref=d, send_sem=send_sems.at[k], recv_sem=recv_sems.at[k],
            device_id=to, device_id_type=MESH)

    def first_copies():
        return [copy(0, me, sibling, True)] + [copy(1 + j, me, (*chip, c), True) for j, chip in enumerate(chips)]

    def start():
        pltpu.make_async_copy(src, dst.at[_slot(me)], local_sem).start()
        for cp in first_copies():
            cp.start()

    def finish():
        passed = []
        for j, chip in enumerate(chips):
            copy(1 + j, (*chip, c), me).wait_recv()
            passed.append(copy(4 + j, (*chip, c), sibling))
            passed[-1].start()
        copy(0, sibling, me).wait_recv()
        for j, chip in enumerate(chips):
            copy(4 + j, (*chip, 1 - c), me).wait_recv()
        for cp in first_copies() + passed:
            cp.wait_send()
        pltpu.make_async_copy(src, dst.at[_slot(me)], local_sem).wait()

    return start, finish


def _allgather_relay_steps(in_refs, out_refs, scratch):
    (src,), (dst,) = in_refs, out_refs
    send_sems, recv_sems, local_sem = scratch
    x, y, c = _place()
    me, sibling = (x, y, c), (x, y, 1 - c)
    x_nbr, y_nbr, diag = (1 - x, y, c), (x, 1 - y, c), (1 - x, 1 - y, c)
    flip = lambda a, bit: a + bit - 2 * a * bit
    relay_from = (flip(x, 1 - c), flip(y, c), c)
    relay_to = (flip(x, c), flip(y, 1 - c), c)

    def copy(k, block, to, from_input=False):
        d = dst.at[_slot(block)]
        return pltpu.make_async_remote_copy(
            src_ref=src if from_input else d, dst_ref=d, send_sem=send_sems.at[k], recv_sem=recv_sems.at[k],
            device_id=to, device_id_type=MESH)

    def first_copies():
        return [copy(0, me, sibling, True), copy(1, me, x_nbr, True), copy(2, me, y_nbr, True)]

    def other(block):
        return block[:2] + (1 - c,)

    def start():
        pltpu.make_async_copy(src, dst.at[_slot(me)], local_sem).start()
        for cp in first_copies():
            cp.start()

    def finish():
        copy(1, x_nbr, me).wait_recv()
        copy(2, y_nbr, me).wait_recv()
        later = [copy(3, relay_from, relay_to), copy(4, x_nbr, sibling), copy(5, y_nbr, sibling)]
        for cp in later:
            cp.start()
        copy(3, diag, me).wait_recv()
        later.append(copy(6, diag, sibling))
        later[-1].start()
        copy(0, sibling, me).wait_recv()
        for k, block in ((4, x_nbr), (5, y_nbr), (6, diag)):
            copy(k, other(block), me).wait_recv()
        for cp in first_copies() + later:
            cp.wait_send()
        pltpu.make_async_copy(src, dst.at[_slot(me)], local_sem).wait()

    return start, finish


def _allgather_side(a, relay=False):
    return ((a,), (jax.ShapeDtypeStruct((N_DEV,) + a.shape, a.dtype),),
            (pltpu.SemaphoreType.DMA((7,)), pltpu.SemaphoreType.DMA((7,)), pltpu.SemaphoreType.DMA),
            _allgather_relay_steps if relay else _allgather_steps)


def _allgather(a, name, relay=False):
    ins, outs, scratch, make = _allgather_side(a, relay)

    def body(a_ref, o_ref, *scr):
        start, finish = make((a_ref,), (o_ref,), scr)
        start()
        finish()

    any_spec = pl.BlockSpec(memory_space=pl.ANY)
    return pl.pallas_call(body, name=name, out_shape=outs[0], in_specs=[any_spec], out_specs=any_spec,
                          scratch_shapes=list(scratch))(a)


def _exchange_pair_steps(in_refs, out_refs, scratch):
    (src,), (dst,) = in_refs, out_refs
    send_sems, recv_sems = scratch
    x, y, c = _place()
    sibling = (x, y, 1 - c)
    slots = [_slot(sibling)] + [_slot((*chip, 1 - c)) for chip in _other_chips(x, y)]

    def copies():
        return [pltpu.make_async_remote_copy(
            src_ref=src.at[ps], dst_ref=dst.at[k], send_sem=send_sems.at[k], recv_sem=recv_sems.at[k],
            device_id=sibling, device_id_type=MESH) for k, ps in enumerate(slots)]

    def start():
        for cp in copies():
            cp.start()

    def finish():
        for cp in copies():
            cp.wait()

    return start, finish


def _exchange_pair_side(g):
    return ((g,), (jax.ShapeDtypeStruct((4,) + g.shape[1:], g.dtype),),
            (pltpu.SemaphoreType.DMA((4,)), pltpu.SemaphoreType.DMA((4,))), _exchange_pair_steps)


def _exchange_pair(g):
    ins, outs, scratch, make = _exchange_pair_side(g)

    def body(g_ref, r_ref, *scr):
        start, finish = make((g_ref,), (r_ref,), scr)
        start()
        finish()

    any_spec = pl.BlockSpec(memory_space=pl.ANY)
    return pl.pallas_call(body, name="exchange_pair", out_shape=outs[0], in_specs=[any_spec], out_specs=any_spec,
                          scratch_shapes=list(scratch))(g)


def _tiling(R, Cc, tile, by_cols):
    if by_cols:
        assert Cc % tile == 0
        return Cc // tile, (R, tile), lambda lead, i: (lead, 0, i)
    assert R % tile == 0
    return R // tile, (tile, Cc), lambda lead, i: (lead, i, 0)


def _pair_add(name, g, r1, slots, tile, by_cols=False, side=None):
    _, R, Cc = g.shape
    steps, blk, at = _tiling(R, Cc, tile, by_cols)
    side_ins, side_outs, side_scr, side_make = side if side is not None else ((), (), (), None)
    n_si, n_so = len(side_ins), len(side_outs)

    def kern(s_ref, a_ref, b_ref, *rest):
        o_ref = rest[n_si]
        if side_make is not None:
            start, finish = side_make(rest[:n_si], rest[n_si + 1:n_si + 1 + n_so], rest[n_si + 1 + n_so:])
            j, i = pl.program_id(0), pl.program_id(1)
            pl.when(jnp.logical_and(j == 0, i == 0))(start)
        o_ref[...] = (a_ref[...].astype(F32) + b_ref[...].astype(F32)).astype(o_ref.dtype)
        if side_make is not None:
            pl.when(jnp.logical_and(j == 2, i == steps - 1))(finish)

    any_spec = pl.BlockSpec(memory_space=pl.ANY)
    res = pl.pallas_call(
        kern, name=name,
        out_shape=[jax.ShapeDtypeStruct((3, R, Cc), BF16)] + list(side_outs),
        grid_spec=pltpu.PrefetchScalarGridSpec(
            num_scalar_prefetch=1, grid=(3, steps),
            in_specs=[pl.BlockSpec((1,) + blk, lambda j, i, s: at(s[j], i)),
                      pl.BlockSpec((1,) + blk, lambda j, i, s: at(1 + j, i))] + [any_spec] * n_si,
            out_specs=[pl.BlockSpec((1,) + blk, lambda j, i, s: at(j, i))] + [any_spec] * n_so,
            scratch_shapes=list(side_scr)),
        compiler_params=_cp(("arbitrary", "arbitrary")),
    )(slots, g, r1, *side_ins)
    return res if side is not None else res[0]


def _axis_neighbours():
    x, y, c = _place()
    flip = lambda a, bit: a + bit - 2 * a * bit
    return (flip(x, c), flip(y, 1 - c), c), (flip(x, 1 - c), flip(y, c), c), c


def _exchange_first_steps(in_refs, out_refs, scratch):
    pairs = list(zip(in_refs, out_refs))
    send_sems, recv_sems = scratch
    first, _, c = _axis_neighbours()

    def copies():
        return [pltpu.make_async_remote_copy(
            src_ref=src.at[j], dst_ref=dst.at[k], send_sem=send_sems.at[t, k], recv_sem=recv_sems.at[t, k],
            device_id=first, device_id_type=MESH)
            for t, (src, dst) in enumerate(pairs) for k, j in enumerate((1 - c, 2))]

    def start():
        for cp in copies():
            cp.start()

    def finish():
        for cp in copies():
            cp.wait()

    return start, finish


def _exchange_first_side(*ss):
    n = len(ss)
    return (ss, tuple(jax.ShapeDtypeStruct((2,) + s.shape[1:], s.dtype) for s in ss),
            (pltpu.SemaphoreType.DMA((n, 2)), pltpu.SemaphoreType.DMA((n, 2))), _exchange_first_steps)


def _axis_add(name, s, t1, core, tile, by_cols=False):
    _, R, Cc = s.shape
    steps, blk, at = _tiling(R, Cc, tile, by_cols)

    def kern(c_ref, a_ref, b_ref, o_ref):
        o_ref[...] = (a_ref[...].astype(F32) + b_ref[...].astype(F32)).astype(o_ref.dtype)

    return pl.pallas_call(
        kern, name=name,
        out_shape=jax.ShapeDtypeStruct((1, R, Cc), BF16),
        grid_spec=pltpu.PrefetchScalarGridSpec(
            num_scalar_prefetch=1, grid=(steps,),
            in_specs=[pl.BlockSpec((1,) + blk, lambda i, cr: at(cr[0], i)),
                      pl.BlockSpec((1,) + blk, lambda i, cr: at(1, i))],
            out_specs=pl.BlockSpec((1,) + blk, lambda i, cr: at(0, i))),
        compiler_params=_cp(("arbitrary",)),
    )(core, s, t1)


def _exchange_second_steps(in_refs, out_refs, scratch):
    send_sems, recv_sems = scratch
    _, second, _ = _axis_neighbours()

    def copies():
        return [pltpu.make_async_remote_copy(src_ref=src, dst_ref=dst, send_sem=send_sems.at[t], recv_sem=recv_sems.at[t],
                                             device_id=second, device_id_type=MESH)
                for t, (src, dst) in enumerate(zip(in_refs, out_refs))]

    def start():
        for cp in copies():
            cp.start()

    def finish():
        for cp in copies():
            cp.wait()

    return start, finish


def _exchange_second_side(*ps):
    n = len(ps)
    return (ps, tuple(jax.ShapeDtypeStruct(p.shape, p.dtype) for p in ps),
            (pltpu.SemaphoreType.DMA((n,)), pltpu.SemaphoreType.DMA((n,))), _exchange_second_steps)


def _adamw(name, w, m, v, parts, tile, by_cols=False):
    R, Cc = w.shape
    steps, blk_shape, at = _tiling(R, Cc, tile, by_cols)
    n_parts = len(parts)

    def kern(*refs):
        w_ref, m_ref, v_ref = refs[:3]
        g = None
        for r_ref, (_, n) in zip(refs[3:3 + n_parts], parts):
            for s in range(n):
                term = r_ref[s].astype(F32)
                g = term if g is None else g + term
        g_out, d_out, m_out, v_out = refs[3 + n_parts:]
        mn = ADAM_B1 * m_ref[...] + (1.0 - ADAM_B1) * g
        vn = ADAM_B2 * v_ref[...] + (1.0 - ADAM_B2) * (g * g)
        m_hat = mn / (1.0 - ADAM_B1 ** ADAM_STEP)
        v_hat = vn / (1.0 - ADAM_B2 ** ADAM_STEP)
        g_out[...] = g
        d_out[...] = -ADAM_LR * (m_hat / (jnp.sqrt(v_hat) + ADAM_EPS) + ADAM_WD * w_ref[...])
        m_out[...] = mn
        v_out[...] = vn

    blk = pl.BlockSpec(blk_shape, lambda i: at(0, i)[1:])
    return pl.pallas_call(
        kern, name=name,
        out_shape=[jax.ShapeDtypeStruct((R, Cc), F32)] * 4,
        grid=(steps,),
        in_specs=[blk] * 3 + [pl.BlockSpec((n,) + blk_shape, lambda i: at(0, i)) for (_, n) in parts],
        out_specs=[blk] * 4,
        compiler_params=_cp(("arbitrary",)),
    )(w, m, v, *[a for (a, _) in parts])


def _assemble_columns(blocks, pieces, zeros, width):
    _, R, Cc = blocks.shape
    tr = min(256, R)

    def kern(b_ref, o_ref):
        for col, n in zeros:
            o_ref[:, col:col + n] = jnp.zeros((tr, n), o_ref.dtype)
        for col, d, lo, n in pieces:
            o_ref[:, col:col + n] = b_ref[d, :, lo:lo + n]

    return pl.pallas_call(
        kern, name="assemble_w_in",
        out_shape=jax.ShapeDtypeStruct((R, width), blocks.dtype),
        grid=(R // tr,),
        in_specs=[pl.BlockSpec((N_DEV, tr, Cc), lambda i: (0, i, 0))],
        out_specs=pl.BlockSpec((tr, width), lambda i: (i, 0)),
        compiler_params=_cp(("parallel",)),
    )(blocks)


def _split_rows(x, pieces, rows, side):
    _, Cc = x.shape
    tc = min(256, Cc)
    side_ins, side_outs, side_scr, side_make = side
    n_si, n_so = len(side_ins), len(side_outs)

    def kern(x_ref, *rest):
        o_ref = rest[n_si]
        start, finish = side_make(rest[:n_si], rest[n_si + 1:n_si + 1 + n_so], rest[n_si + 1 + n_so:])
        pl.when(pl.program_id(0) == 0)(start)
        for d, lo, row, n in pieces:
            o_ref[d, lo:lo + n, :] = x_ref[row:row + n, :]
        pl.when(pl.program_id(0) == Cc // tc - 1)(finish)

    any_spec = pl.BlockSpec(memory_space=pl.ANY)
    return pl.pallas_call(
        kern, name="split_w_in_grad",
        out_shape=[jax.ShapeDtypeStruct((N_DEV, rows, Cc), x.dtype)] + list(side_outs),
        grid=(Cc // tc,),
        in_specs=[pl.BlockSpec((x.shape[0], tc), lambda i: (0, i))] + [any_spec] * n_si,
        out_specs=[pl.BlockSpec((N_DEV, rows, tc), lambda i: (0, 0, i))] + [any_spec] * n_so,
        scratch_shapes=list(side_scr),
        compiler_params=_cp(("arbitrary",)),
    )(x, *side_ins)


def _pad_cols(a, w):
    return jnp.pad(a, ((0, 0), (0, w - a.shape[1])))


def _pad_rows(a, r):
    return jnp.pad(a, ((0, r - a.shape[0]), (0, 0)))


def _pack_b(pf, pr, wo, w2, a2, rows):
    body = jnp.concatenate([pf, pr, wo.reshape(2048, 256), jnp.concatenate([w2, a2], axis=1)], axis=0)
    return _pad_rows(body, rows)


def kernel(x, norm_gain, w_in, fox_forget_bias, rwkv_shift_mix, rwkv_w0, rwkv_w2, rwkv_a0, rwkv_a2, rwkv_k_k, rwkv_k_a, rwkv_r_k, rwkv_ln_w, rwkv_ln_b, w_proj_fox, w_proj_rwkv, w_out, final_norm_gain, loss_target, m_norm_gain, m_w_in, m_fox_forget_bias, m_rwkv_shift_mix, m_rwkv_w0, m_rwkv_w2, m_rwkv_a0, m_rwkv_a2, m_rwkv_k_k, m_rwkv_k_a, m_rwkv_r_k, m_rwkv_ln_w, m_rwkv_ln_b, m_w_proj_fox, m_w_proj_rwkv, m_w_out, m_final_norm_gain, v_norm_gain, v_w_in, v_fox_forget_bias, v_rwkv_shift_mix, v_rwkv_w0, v_rwkv_w2, v_rwkv_a0, v_rwkv_a2, v_rwkv_k_k, v_rwkv_k_a, v_rwkv_r_k, v_rwkv_ln_w, v_rwkv_ln_b, v_w_proj_fox, v_w_proj_rwkv, v_w_out, v_final_norm_gain):
    T, D = x.shape[1], x.shape[2]
    assert D == 2048 and T % LANE == 0
    NI = w_in.shape[2]
    IN = N_DEV * NI
    RB = 4224
    x2 = x[0]
    lt2 = loss_target[0]
    me = _slot(_place())

    tb = min(256, T)
    tbh = min(128, T)
    h, wa = _rowcall("rms_fwd", T, tb, [_rows(x2, tb, D), _whole(norm_gain)],
                     lambda i, xv, g: ([_rms_math(xv, g)], []), [(D, BF16)],
                     side=_allgather_side(w_in[0].astype(BF16), relay=True))
    packed_own = _pack_b(w_proj_fox[0], w_proj_rwkv[0], w_out[0], rwkv_w2[0], rwkv_a2[0], RB).astype(BF16)
    sections = [(0, 4096, 0), (4104, 4096, 4096), (8392, 4096, 8192), (4096, 8, 12288), (8200, 96, 12544), (8296, 96, 12672)]
    NP = 12800
    pieces, zeros, at_col = [], [], 0
    for lo, width, pad_lo in sections:
        if pad_lo > at_col:
            zeros.append((at_col, pad_lo - at_col))
        col = lo
        while col < lo + width:
            d = col // NI
            stop = min(lo + width, (d + 1) * NI)
            pieces.append((pad_lo + col - lo, d, col - d * NI, stop - col))
            col = stop
        at_col = pad_lo + width
    zeros.append((at_col, NP - at_col))
    w_pad = _assemble_columns(wa, pieces, zeros, NP)
    F_CB, LORA_CB = 96, 49

    mu = rwkv_shift_mix
    mu_main = mu[:, 0:4096]
    mu_lora = jnp.concatenate([_pad_cols(mu[:, 4096:4192], LANE), _pad_cols(mu[:, 4192:4288], LANE)], axis=1)
    bias_pad = _pad_cols(fox_forget_bias, LANE)
    rk_flat = rwkv_r_k.reshape(1, 1024)
    gf = final_norm_gain.reshape(1, D)

    u, wb = _mm(h, w_pad, tm=1024, tn=1280, name="mm_in", side=_allgather_side(packed_own))
    wpf = wb[:, 0:1024, :].transpose(1, 0, 2).reshape(1024, D)
    wpr = wb[:, 1024:2048, :].transpose(1, 0, 2).reshape(1024, D)
    wo = wb[:, 2048:4096, :].reshape(N_DEV * 256, D)
    w2p = _pad_rows(wb[:, 4096:4192, 0:128].transpose(1, 0, 2).reshape(96, 1024).astype(F32), LANE)
    a2p = _pad_rows(wb[:, 4096:4192, 128:256].transpose(1, 0, 2).reshape(96, 1024).astype(F32), LANE)

    c8 = _gates_fwd(u, bias_pad, T, F_CB)
    c3 = c8.reshape(8, 1, T)
    o_raw, o_a, lse = _attn_fwd(u, c3, T)

    def shift_body(i, um, hm, ul, hl, mm_, ml):
        outs = []
        for uv, hv, mv in ((um, hm, mm_), (ul, hl, ml)):
            hv = jnp.where(i == 0, 0.0, hv)
            prev = pltpu.roll(jnp.concatenate([hv, uv], axis=0), 1, 0)[SUB:]
            outs.append(uv + (prev - uv) * mv)
        return outs, []

    def halo_prev(arr, w, cb):
        return (arr, (SUB, w), lambda i: (jnp.maximum(i * (tbh // SUB) - 1, 0), cb))

    xs, xl = _rowcall("rwkv_shift_fwd", T, tbh,
                      [_rows(u, tbh, 4096, 1), halo_prev(u, 4096, 1), _rows(u, tbh, 256, LORA_CB), halo_prev(u, 256, LORA_CB),
                       _whole(mu_main), _whole(mu_lora)],
                      shift_body, [(4096, F32), (256, F32)])

    prep_par = [_whole(rwkv_w0), _whole(rwkv_a0), _whole(rwkv_k_k), _whole(rwkv_k_a), _whole(w2p), _whole(a2p)]
    prep_rows = [_rows(xs, tbh, 1024, 1), _rows(xl, tbh, LANE, 0), _rows(xl, tbh, LANE, 1)]
    lw, kp, an, bb = _rowcall("rwkv_prep_fwd", T, tbh, prep_rows + prep_par,
                              lambda i, *a: (list(_prep_math(*a)), []), [(1024, F32)] * 4)
    y, states, invs = _scan_fwd(xs, lw, kp, an, bb, T)
    post_rows = [_rows(y, tbh, 1024), _rows(xs, tbh, 1024, 0), _rows(kp, tbh, 1024), _rows(xs, tbh, 1024, 2), _rows(xs, tbh, 1024, 3)]
    post_par = [_whole(rwkv_ln_w), _whole(rwkv_ln_b), _whole(rk_flat)]
    (o_b,) = _rowcall("rwkv_post_fwd", T, tbh, post_rows + post_par,
                      lambda i, *a: ([_post_math(*a)], []), [(1024, BF16)])

    def merge_fwd_body(i, ga, gb, oav, obv, wpfv, wprv):
        pav, pbv = _nn(oav, wpfv).astype(BF16), _nn(obv, wprv).astype(BF16)
        return [pav, pbv, _merge_math(ga, gb, pav.astype(F32), pbv.astype(F32))], []

    pa, pb, mg = _rowcall("mm_proj_merge_fwd", T, tb,
                          [_rows(u, tb, D, 4), _rows(u, tb, D, 5), _rows(o_a, tb, 1024), _rows(o_b, tb, 1024), _whole(wpf), _whole(wpr)],
                          merge_fwd_body, [(D, BF16)] * 3)
    merge_rows = [_rows(u, tb, D, 4), _rows(u, tb, D, 5), _rows(pa, tb, D), _rows(pb, tb, D)]

    def head_body(i, xv, mgv, wov, ltv, g):
        out = xv + _nn(mgv, wov)
        r = lax.rsqrt(jnp.mean(out * out, axis=-1, keepdims=True) + RMS_EPS)
        yn = out * r
        err = yn * g - ltv
        loss = 0.5 * jnp.sum(jnp.sum(err * err, axis=-1, keepdims=True), axis=0, keepdims=True) / D
        dyv = err / D
        dyn = dyv * g
        dout = r * (dyn - yn * jnp.mean(dyn * yn, axis=-1, keepdims=True))
        return [dout], [loss, jnp.sum(dyv * yn, axis=0, keepdims=True)]

    dout, loss_p, dgf_p = _rowcall("mm_out_loss_head", T, tb,
                                   [_rows(x2, tb, D), _rows(mg, tb, D), _whole(wo), _rows(lt2, tb, D), _whole(gf)],
                                   head_body, [(D, F32)], [(1, 1), (1, D)])

    dwo = _mm(mg, dout, ta=True, out_dtype=BF16, name="mm_out_dw")

    def merge_bwd_body(i, ga, gb, pav, pbv, doutv, wov):
        _, vjp = jax.vjp(_merge_math, ga, gb, pav.astype(F32), pbv.astype(F32))
        dga, dgb, dpa, dpb = vjp(_nt(doutv.astype(BF16), wov))
        return [dga, dgb, dpa, dpb], []

    dga, dgb, dpa, dpb = _rowcall("mm_out_dx_merge_bwd", T, tb, merge_rows + [_rows(dout, tb, D), _whole(wo)], merge_bwd_body,
                                  [(D, BF16), (D, BF16), (D, BF16), (D, BF16)])
    dwpf = _mm(o_a, dpa, ta=True, out_dtype=BF16, name="mm_proj_fox_dw")
    dob = _mm(dpb, wpr, tb=True, name="mm_proj_rwkv_dx")
    dwpr = _mm(o_b, dpb, ta=True, out_dtype=BF16, name="mm_proj_rwkv_dw")

    do_b, dza, delta128 = _rowcall("mm_proj_fox_dx_attn_pre", T, tb,
                                   [_rows(dpa, tb, D), _whole(wpf), _rows(u, tb, 1024, 3), _rows(o_raw, tb, 1024)],
                                   lambda i, dpav, wv, z, o: (list(_attn_pre_math(_nt(dpav, wv), z, o)), []),
                                   [(1024, BF16), (1024, F32), (LANE, F32)])
    delta = delta128[:, 0:8].T.reshape(8, T, 1)
    dq, dk, dv, dc3 = _attn_bwd(u, c3, lse, do_b, delta, T)
    dfl, dbias_p = _gates_bwd(dc3.reshape(8, T), u, bias_pad, T, F_CB)

    def post_bwd_body(i, yv, rv, kpv, vv, zv, lnw, lnb, rkv, dobv):
        _, vjp = jax.vjp(_post_math, yv, rv, kpv, vv, zv, lnw, lnb, rkv)
        dy_, dr_, dkp_, dv_, dz_, dlnw, dlnb, drk = vjp(dobv)
        return [dy_, dr_, dkp_, dv_, dz_], [dlnw, dlnb, drk]

    dy_s, dr_p, dkp_p, dv_p, dzb, dlnw_p, dlnb_p, drk_p = _rowcall(
        "rwkv_post_bwd", T, tbh, post_rows + post_par + [_rows(dob, tbh, 1024)], post_bwd_body,
        [(1024, F32)] * 5, [(1, 1024)] * 3)
    dr_s, dlw, dkp_s, dv_s, dan, dbb = _scan_bwd(xs, lw, kp, an, bb, states, invs, dy_s, T)

    def prep_bwd_body(i, xk, xwd, xad, w0, a0, kkw, kaw, w2v, a2v, dlw_, dkp1, dkp2, dan_, dbb_, dr1, dr2, dv1, dv2, dz_):
        _, vjp = jax.vjp(_prep_math, xk, xwd, xad, w0, a0, kkw, kaw, w2v, a2v)
        dxk, dxwd, dxad, dw0, da0, dkk, dka, dw2, da2 = vjp((dlw_, dkp1 + dkp2, dan_, dbb_))
        return [[dr1 + dr2, dxk, dv1 + dv2, dz_], [dxwd, dxad]], [dw0, da0, dkk, dka, dw2, da2]

    cots = [dlw, dkp_s, dkp_p, dan, dbb, dr_s, dr_p, dv_s, dv_p, dzb]
    dxs, dxl, dw0_p, da0_p, dkk_p, dka_p, dw2_p, da2_p = _rowcall(
        "rwkv_prep_bwd", T, tbh, prep_rows + prep_par + [_rows(c_, tbh, 1024) for c_ in cots], prep_bwd_body,
        [(4096, F32), (256, F32)], [(1, 1024)] * 4 + [(LANE, 1024)] * 2)

    def shift_bwd_body(i, dm_, hm, dl_, hl, um, pm, ul, pl_, mm_, ml):
        last = i == T // tbh - 1
        outs, accs = [], []
        for dv_, hv, uv, pv, mv in ((dm_, hm, um, pm, mm_), (dl_, hl, ul, pl_, ml)):
            hv = jnp.where(last, 0.0, hv)
            nxt = pltpu.roll(jnp.concatenate([dv_, hv], axis=0), tbh + SUB - 1, 0)[:tbh]
            pv = jnp.where(i == 0, 0.0, pv)
            prev = pltpu.roll(jnp.concatenate([pv, uv], axis=0), 1, 0)[SUB:]
            outs.append(dv_ * (1.0 - mv) + nxt * mv)
            accs.append(jnp.sum(dv_ * (prev - uv), axis=0, keepdims=True))
        return outs, accs

    def halo_next(arr, w, cb):
        last_blk = T // SUB - 1
        return (arr, (SUB, w), lambda i: (jnp.minimum((i + 1) * (tbh // SUB), last_blk), cb))

    du_b, du_l, dmu_main_p, dmu_lora_p = _rowcall(
        "rwkv_shift_bwd", T, tbh,
        [_rows(dxs, tbh, 4096), halo_next(dxs, 4096, 0), _rows(dxl, tbh, 256), halo_next(dxl, 256, 0),
         _rows(u, tbh, 4096, 1), halo_prev(u, 4096, 1), _rows(u, tbh, 256, LORA_CB), halo_prev(u, 256, LORA_CB),
         _whole(mu_main), _whole(mu_lora)],
        shift_bwd_body, [(4096, BF16), (256, BF16)], [(1, 4096), (1, 256)])

    lora_g = jnp.concatenate([dw2_p[:96].reshape(96, N_DEV, 128).transpose(1, 0, 2),
                              da2_p[:96].reshape(96, N_DEV, 128).transpose(1, 0, 2)], axis=2).astype(BF16)
    gb = jnp.concatenate([dwpf.reshape(1024, N_DEV, 256).transpose(1, 0, 2),
                          dwpr.reshape(1024, N_DEV, 256).transpose(1, 0, 2),
                          dwo.reshape(N_DEV, 2048, 256), lora_g, jnp.zeros((N_DEV, RB - 4192, 256), BF16)], axis=1)
    xx, yy, cc = _place()
    chip_slots = jnp.stack([_slot((*chip, cc)) for chip in _other_chips(xx, yy)]).astype(jnp.int32)
    core = jnp.stack([cc]).astype(jnp.int32)
    du, r1b = _rowcall("assemble_du", T, tb,
                       [_rows(a_, tb, a_.shape[1]) for a_ in (dq, dk, dv, dza, du_b, dga, dgb, dfl, du_l)],
                       lambda i, *a: ([list(a[:8]) + [jnp.zeros((tb, LANE), BF16), a[8]]], []), [(NP, BF16)],
                       side=_exchange_pair_side(gb))
    sb = _pair_add("pair_add_packed", gb, r1b, chip_slots, RB // 3)
    dw_pad_t, t1b = _mm(du, h, ta=True, out_dtype=BF16, tm=1280, tn=1024, name="mm_in_dw",
                        side=_exchange_first_side(sb))
    pb = _axis_add("axis_add_packed", sb, t1b, core, RB // 3)
    ga, t2b = _split_rows(dw_pad_t, [(d, lo, pad_col, n) for pad_col, d, lo, n in pieces], NI,
                          _exchange_second_side(pb))
    r1a = _exchange_pair(ga)
    sa = _pair_add("pair_add_w_in", ga, r1a, chip_slots, 512, by_cols=True)
    ga_own = lax.dynamic_index_in_dim(ga, me, 0, keepdims=True)
    gb_own = lax.dynamic_index_in_dim(gb, me, 0, keepdims=True)
    dh, t1a = _mm(du, w_pad, tb=True, tm=1024, tn=1024, tk=NP // 10, name="mm_in_dx", side=_exchange_first_side(sa))
    pa = _axis_add("axis_add_w_in", sa, t1a, core, 512, by_cols=True)

    def rms_bwd_body(i, xv, g, dhv, doutv):
        _, vjp = jax.vjp(_rms_math, xv, g)
        dx_, dg_ = vjp(dhv)
        return [dx_ + doutv], [dg_]

    grad_x2, dng_p, t2a = _rowcall(
        "rms_bwd", T, tb, [_rows(x2, tb, D), _whole(norm_gain), _rows(dh, tb, D), _rows(dout, tb, D)],
        rms_bwd_body, [(D, F32)], [(1, D)], side=_exchange_second_side(pa))

    w_in_outs = _adamw("adamw_w_in", w_in[0].T, m_w_in[0].T, v_w_in[0].T, [(ga_own, 1), (r1a, 1), (t1a, 1), (t2a, 1)], 256, by_cols=True)
    g_in, d_in, m_in, v_in = [o.T for o in w_in_outs]

    dmu = jnp.concatenate([dmu_main_p, dmu_lora_p[:, 0:96], dmu_lora_p[:, 128:224]], axis=1)
    small_parts = [dng_p, dbias_p[:, 0:8], dmu, dw0_p, da0_p, dkk_p, dka_p, drk_p, dlnw_p, dlnb_p, dgf_p, loss_p]
    SR = 128
    small = _pad_cols(jnp.concatenate(small_parts, axis=1), SR * LANE).reshape(SR, LANE)
    rs = _allgather(small, "allgather_small")
    pk = lambda pf, pr, wo_, w2_, a2_: _pack_b(pf[0], pr[0], wo_[0], w2_[0], a2_[0], RB)
    outs_b = _adamw("adamw_packed", pk(w_proj_fox, w_proj_rwkv, w_out, rwkv_w2, rwkv_a2),
                    pk(m_w_proj_fox, m_w_proj_rwkv, m_w_out, m_rwkv_w2, m_rwkv_a2),
                    pk(v_w_proj_fox, v_w_proj_rwkv, v_w_out, v_rwkv_w2, v_rwkv_a2), [(gb_own, 1), (r1b, 1), (t1b, 1), (t2b, 1)], RB // 3)

    def pack_small(ng, fb, sm, w0, a0, kk_, ka_, rk_, lnw, lnb, fg):
        parts = [ng, fb, sm, w0, a0, kk_, ka_, rk_.reshape(1, 1024), lnw, lnb, fg.reshape(1, D), jnp.zeros((1, 1), F32)]
        return _pad_cols(jnp.concatenate(parts, axis=1), SR * LANE).reshape(SR, LANE)

    outs_s = _adamw("adamw_small",
                    pack_small(norm_gain, fox_forget_bias, rwkv_shift_mix, rwkv_w0, rwkv_a0, rwkv_k_k, rwkv_k_a, rwkv_r_k,
                               rwkv_ln_w, rwkv_ln_b, final_norm_gain),
                    pack_small(m_norm_gain, m_fox_forget_bias, m_rwkv_shift_mix, m_rwkv_w0, m_rwkv_a0, m_rwkv_k_k, m_rwkv_k_a,
                               m_rwkv_r_k, m_rwkv_ln_w, m_rwkv_ln_b, m_final_norm_gain),
                    pack_small(v_norm_gain, v_fox_forget_bias, v_rwkv_shift_mix, v_rwkv_w0, v_rwkv_a0, v_rwkv_k_k, v_rwkv_k_a,
                               v_rwkv_r_k, v_rwkv_ln_w, v_rwkv_ln_b, v_final_norm_gain),
                    [(rs, N_DEV)], SR)

    def unpack_b(pkd):
        return dict(w_proj_fox=pkd[0:1024][None], w_proj_rwkv=pkd[1024:2048][None], w_out=pkd[2048:4096].reshape(1, 256, D),
                    rwkv_w2=pkd[4096:4192, 0:128][None], rwkv_a2=pkd[4096:4192, 128:256][None])

    def unpack_s(pkd):
        flat = pkd.reshape(1, SR * LANE)
        names = [("norm_gain", D), ("fox_forget_bias", 8), ("rwkv_shift_mix", 4288), ("rwkv_w0", 1024), ("rwkv_a0", 1024),
                 ("rwkv_k_k", 1024), ("rwkv_k_a", 1024), ("rwkv_r_k", 1024), ("rwkv_ln_w", 1024), ("rwkv_ln_b", 1024),
                 ("final_norm_gain", D), ("loss", 1)]
        out, off = {}, 0
        for nm, n in names:
            out[nm] = flat[:, off:off + n]
            off += n
        out["rwkv_r_k"] = out["rwkv_r_k"].reshape(1, 16, 64)
        out["final_norm_gain"] = out["final_norm_gain"].reshape(D)
        return out

    order = ["norm_gain", "w_in", "fox_forget_bias", "rwkv_shift_mix", "rwkv_w0", "rwkv_w2", "rwkv_a0", "rwkv_a2", "rwkv_k_k",
             "rwkv_k_a", "rwkv_r_k", "rwkv_ln_w", "rwkv_ln_b", "w_proj_fox", "w_proj_rwkv", "w_out", "final_norm_gain"]
    result = []
    loss = None
    for kind, big in enumerate((g_in, d_in, m_in, v_in)):
        d = {**unpack_b(outs_b[kind]), **unpack_s(outs_s[kind]), "w_in": big[None]}
        if kind == 0:
            loss = d["loss"].reshape(())
        result += [d[n] for n in order]
    return (loss, grad_x2[None], *result)
```

```python
import functools

import jax
import jax.numpy as jnp
from jax import lax
from jax.experimental import pallas as pl
from jax.experimental.pallas import tpu as pltpu

F32 = jnp.float32
BF16 = jnp.bfloat16
HI = lax.Precision.HIGHEST
H3 = lax.Precision.HIGH
MESH = pl.DeviceIdType.MESH

FOX_HD = 128
RW_HD = 64
RMS_EPS = 1e-6
GN_EPS = 64e-5
L2_EPS = 1e-12
ADAM_LR = 0.001
ADAM_B1 = 0.9
ADAM_B2 = 0.999
ADAM_EPS = 1e-08
ADAM_WD = 0.01
ADAM_STEP = 10

LANE = 128
SUB = 8
VMEM_LIMIT = 56 * 1024 * 1024
N_DEV = 8
CHUNK = 128
SCAN_GROUP = 1
SCAN_PAIRS = 8
PS = None
NEG = -1e30


def _scan_shape(T):
    c = min(CHUNK, T)
    return c, min(SCAN_GROUP, T // c)


def _cp(sem=None):
    return pltpu.CompilerParams(dimension_semantics=sem, vmem_limit_bytes=VMEM_LIMIT)


def _sigmoid(x):
    return jax.nn.sigmoid(x)


def _softplus(x):
    return jnp.maximum(x, 0.0) + jnp.log(1.0 + jnp.exp(-jnp.abs(x)))


def _nn(a, b, prec=None):
    return lax.dot_general(a, b, (((1,), (0,)), ((), ())), precision=prec, preferred_element_type=F32)


def _nt(a, b, prec=None):
    return lax.dot_general(a, b, (((1,), (1,)), ((), ())), precision=prec, preferred_element_type=F32)


def _tn(a, b, prec=None):
    return lax.dot_general(a, b, (((0,), (0,)), ((), ())), precision=prec, preferred_element_type=F32)


def _iota2(shape, dim):
    return lax.broadcasted_iota(jnp.int32, shape, dim)


def _seg_sum(x):
    r = _iota2((LANE, LANE), 0) // RW_HD
    c = _iota2((LANE, LANE), 1) // RW_HD
    bd = (r == c).astype(F32)
    parts = [_nn(x[:, j * LANE:(j + 1) * LANE], bd, H3) for j in range(x.shape[1] // LANE)]
    return parts[0] if len(parts) == 1 else jnp.concatenate(parts, axis=1)


def _mm(a, b, *, ta=False, tb=False, out_dtype=F32, tm=1024, tn=1024, tk=None, name, side=None):
    assert not (ta and tb)
    K, M = a.shape if ta else a.shape[::-1]
    N = b.shape[0] if tb else b.shape[1]
    tm, tn = min(tm, M), min(tn, N)
    tk = K if tk is None else tk
    nk = K // tk
    assert M % tm == 0 and N % tn == 0 and K % tk == 0
    gi, gj = M // tm, N // tn
    a_spec = pl.BlockSpec((tk, tm), lambda i, j, k: (k, i)) if ta else pl.BlockSpec((tm, tk), lambda i, j, k: (i, k))
    b_spec = pl.BlockSpec((tn, tk), lambda i, j, k: (j, k)) if tb else pl.BlockSpec((tk, tn), lambda i, j, k: (k, j))
    side_ins, side_outs, side_scr, side_make = side if side is not None else ((), (), (), None)
    n_si, n_so = len(side_ins), len(side_outs)
    n_acc = 0 if nk == 1 else 1

    def body(*refs):
        a_ref, b_ref = refs[:2]
        o_ref = refs[2 + n_si]
        scr = refs[3 + n_si + n_so:]
        k = pl.program_id(2)
        if side_make is not None:
            start, finish = side_make(refs[2:2 + n_si], refs[3 + n_si:3 + n_si + n_so], scr[n_acc:])
            first = jnp.logical_and(jnp.logical_and(pl.program_id(0) == 0, pl.program_id(1) == 0), k == 0)
            last = jnp.logical_and(jnp.logical_and(pl.program_id(0) == gi - 1, pl.program_id(1) == gj - 1), k == nk - 1)
            pl.when(first)(start)
        av = a_ref[...].astype(BF16)
        bv = b_ref[...].astype(BF16)
        p = _tn(av, bv) if ta else _nt(av, bv) if tb else _nn(av, bv)
        if nk == 1:
            o_ref[...] = p.astype(out_dtype)
        else:
            acc_ref = scr[0]

            @pl.when(k == 0)
            def _():
                acc_ref[...] = p

            @pl.when(k > 0)
            def _():
                acc_ref[...] += p

            @pl.when(k == nk - 1)
            def _():
                o_ref[...] = acc_ref[...].astype(out_dtype)
        if side_make is not None:
            pl.when(last)(finish)

    any_spec = pl.BlockSpec(memory_space=pl.ANY)
    res = pl.pallas_call(
        body, name=name,
        out_shape=[jax.ShapeDtypeStruct((M, N), out_dtype)] + list(side_outs),
        grid=(gi, gj, nk),
        in_specs=[a_spec, b_spec] + [any_spec] * n_si,
        out_specs=[pl.BlockSpec((tm, tn), lambda i, j, k: (i, j))] + [any_spec] * n_so,
        scratch_shapes=([] if nk == 1 else [pltpu.VMEM((tm, tn), F32)]) + list(side_scr),
        compiler_params=_cp(("arbitrary",) * 3 if side is not None else ("parallel", "parallel", "arbitrary")),
    )(a, b, *side_ins)
    return res if side is not None else res[0]


def _rows(arr, tb, w, cb=0):
    return (arr, (tb, w), lambda i: (i, cb))


def _whole(arr):
    nd = arr.ndim
    return (arr, arr.shape, lambda i: (0,) * nd)


def _rowcall(name, T, tb, ins, body, outs, accs=(), side=None):
    n_in, n_out, n_acc = len(ins), len(outs), len(accs)
    side_ins, side_outs, side_scr, side_make = side if side is not None else ((), (), (), None)
    n_si = len(side_ins)

    def kern(*refs):
        i = pl.program_id(0)
        if side_make is not None:
            side_refs = refs[n_in + n_si + n_out + n_acc:]
            start, finish = side_make(refs[n_in:n_in + n_si], side_refs[:len(side_outs)], side_refs[len(side_outs):])
            pl.when(i == 0)(start)
            refs = refs[:n_in] + refs[n_in + n_si:]
        vals = [r[...] for r in refs[:n_in]]
        ro, ao = body(i, *vals)
        for r, v in zip(refs[n_in:n_in + n_out], ro):
            if isinstance(v, (list, tuple)):
                off = 0
                for piece in v:
                    w = piece.shape[1]
                    r[:, off:off + w] = piece.astype(r.dtype)
                    off += w
            else:
                r[...] = v.astype(r.dtype)
        if accs:
            acc_refs = refs[n_in + n_out:n_in + n_out + n_acc]

            @pl.when(i == 0)
            def _():
                for r in acc_refs:
                    r[...] = jnp.zeros(r.shape, F32)

            for r, v in zip(acc_refs, ao):
                r[...] += v
        if side_make is not None:
            pl.when(i == T // tb - 1)(finish)

    any_spec = pl.BlockSpec(memory_space=pl.ANY)
    out_shape = [jax.ShapeDtypeStruct((T, w), dt) for (w, dt) in outs] + [jax.ShapeDtypeStruct(s, F32) for s in accs]
    out_specs = [pl.BlockSpec((tb, w), lambda i: (i, 0)) for (w, dt) in outs] + [pl.BlockSpec(s, lambda i: (0, 0)) for s in accs]
    res = pl.pallas_call(
        kern, name=name,
        out_shape=out_shape + list(side_outs),
        grid=(T // tb,),
        in_specs=[pl.BlockSpec(bs, im) for (_, bs, im) in ins] + [any_spec] * n_si,
        out_specs=out_specs + [any_spec] * len(side_outs),
        scratch_shapes=list(side_scr),
        compiler_params=_cp(("arbitrary",)),
    )(*[a for (a, _, _) in ins], *side_ins)
    return res


def _rms_math(x, g):
    r = lax.rsqrt(jnp.mean(x * x, axis=-1, keepdims=True) + RMS_EPS)
    return x * r * g


def _merge_math(ga, gb, pa, pb):
    return _sigmoid(ga) * pa + _sigmoid(gb) * pb


def _prep_math(xk, xwd, xad, w0, a0, kk_w, ka_w, w2p, a2p):
    z = w0 + _nn(jnp.tanh(xwd), w2p, H3)
    w = -_softplus(-z) - 0.5
    lw = -jnp.exp(w)
    ag = _sigmoid(a0 + _nn(xad, a2p, H3))
    p = xk * kk_w
    n = jnp.maximum(jnp.sqrt(_seg_sum(p * p)), L2_EPS)
    kk = p / n
    kp = xk * (1.0 + (ag - 1.0) * ka_w)
    return lw, kp, -kk, kk * ag


def _post_math(y, r, kp, v, z, lnw, lnb, rk):
    inv = 1.0 / RW_HD
    mu = _seg_sum(y) * inv
    d = y - mu
    var = _seg_sum(d * d) * inv
    yn = d * lax.rsqrt(var + GN_EPS) * lnw + lnb
    bonus = _seg_sum(r * kp * rk) * v
    return (yn + bonus) * (z * _sigmoid(z))


def _neumann(ms, depth):
    eye = (_iota2(ms[0].shape, 0) == _iota2(ms[0].shape, 1)).astype(F32)
    width = ms[0].shape[1]
    mp = [_nn(m, m, PS) for m in ms]
    inv = [eye + m for m in ms]
    n = 2
    while n < depth:
        last = 2 * n >= depth
        for i in range(len(ms)):
            if last:
                inv[i] = inv[i] + _nn(mp[i], inv[i], PS)
            else:
                z = _nn(mp[i], jnp.concatenate([mp[i], inv[i]], axis=1), PS)
                mp[i], inv[i] = z[:, :width], inv[i] + z[:, width:]
        n *= 2
    return inv


@functools.partial(jax.custom_vjp, nondiff_argnums=(1,))
def _unit_inverses(ms, depth):
    if depth <= RW_HD:
        return tuple(_neumann(list(ms), depth))
    h, n = depth // 2, len(ms)
    cat = jnp.concatenate
    z = jnp.zeros((h, h), F32)

    def heads(m, r, c):
        b0 = m[r * h:(r + 1) * h, c * h:(c + 1) * h]
        b1 = m[depth + r * h:depth + (r + 1) * h, depth + c * h:depth + (c + 1) * h]
        return cat([cat([b0, z], axis=1), cat([z, b1], axis=1)], axis=0)

    diag = _neumann([heads(m, 0, 0) for m in ms] + [heads(m, 1, 1) for m in ms], h)
    ta, td = diag[:n], diag[n:]
    low = [_nn(heads(m, 1, 0), a, PS) for m, a in zip(ms, ta)]
    low = [_nn(d, x, PS) for d, x in zip(td, low)]
    out = []
    for a, x, d in zip(ta, low, td):
        rows = []
        for hd in (0, 1):
            sl = slice(hd * h, (hd + 1) * h)
            top, bot = [a[sl, sl], z], [x[sl, sl], d[sl, sl]]
            pad = [z, z]
            rows.append(cat(top + pad if hd == 0 else pad + top, axis=1))
            rows.append(cat(bot + pad if hd == 0 else pad + bot, axis=1))
        out.append(cat(rows, axis=0))
    return tuple(out)


def _unit_inverses_fwd(ms, depth):
    inv = _unit_inverses(ms, depth)
    return inv, inv


def _unit_inverses_bwd(depth, inv, cts):
    left = [_tn(t, g, PS) for t, g in zip(inv, cts)]
    return (tuple(_nt(l, t, PS) for l, t in zip(left, inv)),)


_unit_inverses.defvjp(_unit_inverses_fwd, _unit_inverses_bwd)


@jax.custom_vjp
def _known_inverses(ms, inv):
    return inv


def _known_inverses_fwd(ms, inv):
    return inv, inv


def _known_inverses_bwd(inv, cts):
    dms = _unit_inverses_bwd(None, inv, cts)[0]
    return dms, tuple(jnp.zeros_like(t) for t in inv)


_known_inverses.defvjp(_known_inverses_fwd, _known_inverses_bwd)


def _scan_group(s0s, *flat, known_inv=None, with_inv=False):
    P = len(s0s)
    G = len(flat) // (6 * P)
    ch = [flat[6 * i:6 * i + 6] for i in range(P * G)]
    C = ch[0][0].shape[0]
    C2 = 2 * C
    cat = jnp.concatenate
    m0 = _iota2((1, LANE), 1) < RW_HD
    mask0 = m0.astype(F32)
    mask1 = 1.0 - mask0
    r2 = _iota2((C2, C2), 0)
    c2 = _iota2((C2, C2), 1)
    dist = r2 - c2
    in_head = dist <= r2 % C
    lower = (_iota2((C, C), 0) >= _iota2((C, C), 1)).astype(F32)
    bd = (_iota2((LANE, LANE), 0) // RW_HD) == (_iota2((LANE, LANE), 1) // RW_HD)

    def tri(m, strict):
        return jnp.where(dist > 0 if strict else dist >= 0, jnp.where(in_head, m, 0.0), 0.0)

    def sel(z):
        return jnp.where(m0, z[:C], z[C:])

    gs = [_nn(lower, c[1], HI) for c in ch]
    pre = []
    for (r, lw, k, v, a, b), g in zip(ch, gs):
        g_end = jnp.sum(lw, axis=0, keepdims=True)
        gm = g - jnp.sum(lw[:C // 2], axis=0, keepdims=True)
        en = jnp.exp(-gm)
        ec = jnp.exp(g_end - g)
        pre.append(dict(at=a * jnp.exp(g - lw), rt=r * jnp.exp(g), am=a * jnp.exp(gm - lw), rm=r * jnp.exp(gm),
                        bt=b * en, kt=k * en, bh=b * ec, kh=k * ec, dec=jnp.exp(g_end), v=v))
    grams = [_nt(cat([p["am"] * mask0, p["am"] * mask1, p["rm"] * mask0, p["rm"] * mask1], axis=0),
                 cat([p["bt"], p["bt"], p["kt"], p["kt"]], axis=0), PS) for p in pre]
    mab = tuple(tri(gm[:C2, :C2], True) for gm in grams)
    tinv = _unit_inverses(mab, C) if known_inv is None else _known_inverses(mab, known_inv)
    xv =[sel(_nn(tri(gm[:C2, C2:], True), cat([p["v"], p["v"]], axis=0), PS)) for gm, p in zip(grams, pre)]
    ys, s = [None] * (P * G), list(s0s)
    for i in range(G):
        for q in range(P):
            n = q * G + i
            p, gm = pre[n], grams[n]
            sx = _nt(cat([p["at"], p["rt"]], axis=0), s[q], PS)
            x = sx[:C] + xv[n]
            u = sel(_nn(tinv[n], cat([x, x], axis=0), PS))
            v = p["v"]
            ys[n] = sx[C:] + sel(_nn(cat([tri(gm[C2:, :C2], False), tri(gm[C2:, C2:], False)], axis=1),
                                     cat([u, u, v, v], axis=0), PS))
            s[q] = s[q] * p["dec"] + jnp.where(bd, _tn(cat([u, v], axis=0), cat([p["bh"], p["kh"]], axis=0), PS), 0.0)
    return (tuple(ys), tuple(s), tinv) if with_inv else (tuple(ys), tuple(s))


def _scan_fwd(xs, lw, kp, an, bb, T):
    C, G = _scan_shape(T)
    P = SCAN_PAIRS
    nc = T // (C * G)
    npair = 1024 // LANE

    def kern(r_ref, lw_ref, k_ref, v_ref, a_ref, b_ref, y_ref, st_ref, inv_ref, s_scr):
        n = pl.program_id(1)

        @pl.when(n == 0)
        def _():
            s_scr[...] = jnp.zeros(s_scr.shape, F32)

        st_ref[0] = s_scr[...]
        ins = (r_ref, lw_ref, k_ref, v_ref, a_ref, b_ref)
        ys, s1, inv = _scan_group(
            tuple(s_scr[q] for q in range(P)),
            *[ref[i * C:(i + 1) * C, q * LANE:(q + 1) * LANE] for q in range(P) for i in range(G) for ref in ins], with_inv=True)
        for q in range(P):
            for i in range(G):
                y_ref[i * C:(i + 1) * C, q * LANE:(q + 1) * LANE] = ys[q * G + i]
                inv_ref[0, 0, q * G + i] = inv[q * G + i]
            s_scr[q] = s1[q]

    def col(off):
        return pl.BlockSpec((C * G, P * LANE), lambda p, n: (n, off // P + p))

    return pl.pallas_call(
        kern, name="rwkv_scan_fwd",
        out_shape=[jax.ShapeDtypeStruct((T, 1024), F32), jax.ShapeDtypeStruct((nc, npair, LANE, LANE), F32),
                   jax.ShapeDtypeStruct((nc, npair // P, P * G, 2 * C, 2 * C), F32)],
        grid=(npair // P, nc),
        in_specs=[col(0), col(0), col(0), col(16), col(0), col(0)],
        out_specs=[col(0), pl.BlockSpec((1, P, LANE, LANE), lambda p, n: (n, p, 0, 0)),
                   pl.BlockSpec((1, 1, P * G, 2 * C, 2 * C), lambda p, n: (n, p, 0, 0, 0))],
        scratch_shapes=[pltpu.VMEM((P, LANE, LANE), F32)],
        compiler_params=_cp(("parallel", "arbitrary")),
    )(xs, lw, kp, xs, an, bb)


def _scan_bwd(xs, lw, kp, an, bb, states, invs, dy, T):
    C, G = _scan_shape(T)
    P = SCAN_PAIRS
    nc = T // (C * G)
    npair = 1024 // LANE

    def kern(r_ref, lw_ref, k_ref, v_ref, a_ref, b_ref, st_ref, inv_ref, dy_ref, dr_ref, dlw_ref, dk_ref, dv_ref, da_ref, db_ref,
             ds_scr):
        n = pl.program_id(1)

        @pl.when(n == 0)
        def _():
            ds_scr[...] = jnp.zeros(ds_scr.shape, F32)

        ins = (r_ref, lw_ref, k_ref, v_ref, a_ref, b_ref)
        units = [(q, i) for q in range(P) for i in range(G)]
        known = tuple(inv_ref[0, 0, q * G + i] for q, i in units)
        _, vjp = jax.vjp(functools.partial(_scan_group, known_inv=known), tuple(st_ref[0, q] for q in range(P)),
                         *[ref[i * C:(i + 1) * C, q * LANE:(q + 1) * LANE] for q, i in units for ref in ins])
        grads = vjp((tuple(dy_ref[i * C:(i + 1) * C, q * LANE:(q + 1) * LANE] for q, i in units),
                     tuple(ds_scr[q] for q in range(P))))
        for q in range(P):
            ds_scr[q] = grads[0][q]
        outs = (dr_ref, dlw_ref, dk_ref, dv_ref, da_ref, db_ref)
        for n_, (q, i) in enumerate(units):
            for t, ref in enumerate(outs):
                ref[i * C:(i + 1) * C, q * LANE:(q + 1) * LANE] = grads[1 + 6 * n_ + t]

    def col(off):
        return pl.BlockSpec((C * G, P * LANE), lambda p, n: (nc - 1 - n, off // P + p))

    return pl.pallas_call(
        kern, name="rwkv_scan_bwd",
        out_shape=[jax.ShapeDtypeStruct((T, 1024), F32)] * 6,
        grid=(npair // P, nc),
        in_specs=[col(0), col(0), col(0), col(16), col(0), col(0),
                  pl.BlockSpec((1, P, LANE, LANE), lambda p, n: (nc - 1 - n, p, 0, 0)),
                  pl.BlockSpec((1, 1, P * G, 2 * C, 2 * C), lambda p, n: (nc - 1 - n, p, 0, 0, 0)), col(0)],
        out_specs=[col(0)] * 6,
        scratch_shapes=[pltpu.VMEM((P, LANE, LANE), F32)],
        compiler_params=_cp(("parallel", "arbitrary")),
    )(xs, lw, kp, xs, an, bb, states, invs, dy)


def _gates_fwd(u, bias_pad, T, f_cb):
    nb = T // LANE

    def kern(f_ref, b_ref, c_ref):
        x = f_ref[...] + b_ref[...]
        lf = jnp.minimum(x, 0.0) - jnp.log(1.0 + jnp.exp(-jnp.abs(x)))
        lft = lf.T
        ut = (_iota2((LANE, LANE), 0) <= _iota2((LANE, LANE), 1)).astype(F32)
        carry = jnp.zeros((LANE, 1), F32)
        for blk in range(nb):
            seg = lft[:, blk * LANE:(blk + 1) * LANE]
            cs = _nn(seg, ut, HI) + carry
            c_ref[:, blk * LANE:(blk + 1) * LANE] = cs[:SUB, :]
            carry = carry + jnp.sum(seg, axis=1, keepdims=True)

    return pl.pallas_call(
        kern, name="fox_gates_fwd",
        out_shape=jax.ShapeDtypeStruct((SUB, T), F32),
        grid=(1,),
        in_specs=[pl.BlockSpec((T, LANE), lambda i: (0, f_cb)), pl.BlockSpec((1, LANE), lambda i: (0, 0))],
        out_specs=pl.BlockSpec((SUB, T), lambda i: (0, 0)),
        compiler_params=_cp(("arbitrary",)),
    )(u, bias_pad)


def _gates_bwd(dc, u, bias_pad, T, f_cb):
    nb = T // LANE

    def kern(dc_ref, f_ref, b_ref, dfl_ref, db_ref):
        dcv = jnp.concatenate([dc_ref[...], jnp.zeros((LANE - SUB, T), F32)], axis=0)
        lt = (_iota2((LANE, LANE), 0) >= _iota2((LANE, LANE), 1)).astype(F32)
        carry = jnp.zeros((LANE, 1), F32)
        pieces = [None] * nb
        for blk in range(nb - 1, -1, -1):
            seg = dcv[:, blk * LANE:(blk + 1) * LANE]
            pieces[blk] = _nn(seg, lt, HI) + carry
            carry = carry + jnp.sum(seg, axis=1, keepdims=True)
        dlf = (pieces[0] if nb == 1 else jnp.concatenate(pieces, axis=1)).T
        x = f_ref[...] + b_ref[...]
        dfl = dlf * _sigmoid(-x)
        dfl_ref[...] = dfl
        db_ref[...] = jnp.sum(dfl, axis=0, keepdims=True)

    return pl.pallas_call(
        kern, name="fox_gates_bwd",
        out_shape=[jax.ShapeDtypeStruct((T, LANE), F32), jax.ShapeDtypeStruct((1, LANE), F32)],
        grid=(1,),
        in_specs=[pl.BlockSpec((SUB, T), lambda i: (0, 0)), pl.BlockSpec((T, LANE), lambda i: (0, f_cb)),
                  pl.BlockSpec((1, LANE), lambda i: (0, 0))],
        out_specs=[pl.BlockSpec((T, LANE), lambda i: (0, 0)), pl.BlockSpec((1, LANE), lambda i: (0, 0))],
        compiler_params=_cp(("arbitrary",)),
    )(dc, u, bias_pad)


ATTN_HEADS = 2


def _attn_block(T):
    return 512 if T % 512 == 0 and T >= 1024 else 128


def _attn_fwd(u, c3, T):
    H, HP = 8, ATTN_HEADS
    bq = _attn_block(T)
    nq = T // bq
    scale = FOX_HD ** -0.5
    lanes = [slice(h * LANE, (h + 1) * LANE) for h in range(HP)]

    def kern(q_ref, k_ref, v_ref, z_ref, cq_ref, ck_ref, o_ref, oa_ref, lse_ref):
        i = pl.program_id(1)
        q = [(q_ref[:, ln] * scale).astype(BF16) for ln in lanes]
        c0 = [cq_ref[h][:, 0:1] for h in range(HP)]

        def step(j, carry, diagonal=False):
            off = pl.multiple_of(j * bq, bq)
            s = [_nt(q[h], k_ref[pl.ds(off, bq), lanes[h]].astype(BF16)) + (c0[h] - ck_ref[h, :, pl.ds(off, bq)])
                 for h in range(HP)]
            ps, out = [], []
            for h in range(HP):
                m, l, acc = carry[h]
                sh = s[h]
                if diagonal:
                    sh = jnp.where(_iota2((bq, bq), 1) <= _iota2((bq, bq), 0), sh, NEG)
                m_new = jnp.maximum(m, jnp.max(sh, axis=1, keepdims=True))
                p = jnp.exp(sh - m_new)
                alpha = jnp.exp(m - m_new)
                p_hi = p.astype(BF16)
                ps.append((p_hi, (p - p_hi.astype(F32)).astype(BF16)))
                out.append((m_new, alpha * l + jnp.sum(p, axis=1, keepdims=True), alpha * acc))
            res = []
            for h, (m, l, acc) in enumerate(out):
                vj = v_ref[pl.ds(off, bq), lanes[h]].astype(BF16)
                res.append((m, l, acc + _nn(ps[h][0], vj) + _nn(ps[h][1], vj)))
            return tuple(res)

        init = tuple((jnp.full((bq, 1), NEG, F32), jnp.zeros((bq, 1), F32), jnp.zeros((bq, FOX_HD), F32)) for _ in range(HP))
        res = step(i, lax.fori_loop(0, i, step, init), diagonal=True)
        for h, (m, l, acc) in enumerate(res):
            o = acc / l
            z = z_ref[:, lanes[h]]
            o_ref[:, lanes[h]] = o
            oa_ref[:, lanes[h]] = (o * z * _sigmoid(z)).astype(BF16)
            lse_ref[h] = m + jnp.log(l)

    W = HP * LANE
    return pl.pallas_call(
        kern, name="fox_attn_fwd",
        out_shape=[jax.ShapeDtypeStruct((T, 1024), F32), jax.ShapeDtypeStruct((T, 1024), BF16),
                   jax.ShapeDtypeStruct((H, T, 1), F32)],
        grid=(H // HP, nq),
        in_specs=[pl.BlockSpec((bq, W), lambda g, i: (i, g)),
                  pl.BlockSpec((T, W), lambda g, i: (0, 8 // HP + g)),
                  pl.BlockSpec((T, W), lambda g, i: (0, 16 // HP + g)),
                  pl.BlockSpec((bq, W), lambda g, i: (i, 24 // HP + g)),
                  pl.BlockSpec((HP, 1, bq), lambda g, i: (g, 0, i)),
                  pl.BlockSpec((HP, 1, T), lambda g, i: (g, 0, 0))],
        out_specs=[pl.BlockSpec((bq, W), lambda g, i: (i, g)),
                   pl.BlockSpec((bq, W), lambda g, i: (i, g)),
                   pl.BlockSpec((HP, bq, 1), lambda g, i: (g, i, 0))],
        compiler_params=_cp(("parallel", "arbitrary")),
    )(u, u, u, u, c3, c3)


def _attn_probs(s, lse_i, diagonal):
    if not diagonal:
        return jnp.exp(s - lse_i)
    keep = _iota2(s.shape, 1) <= _iota2(s.shape, 0)
    return jnp.where(keep, jnp.exp(jnp.where(keep, s, NEG) - lse_i), 0.0)


def _attn_pre_math(doa, z, o):
    sg = _sigmoid(z)
    do = (doa * z * sg).astype(BF16)
    dz = doa * o * (sg * (1.0 + z * (1.0 - sg)))
    head_of = (_iota2((o.shape[1], LANE), 0) // FOX_HD == _iota2((o.shape[1], LANE), 1)).astype(F32)
    return do, dz, _nn(do.astype(F32) * o, head_of, HI)


def _attn_bwd(u, c3, lse, do, delta, T):
    H, HP = 8, ATTN_HEADS
    bq = _attn_block(T)
    nq = T // bq
    scale = FOX_HD ** -0.5
    lanes = [slice(h * LANE, (h + 1) * LANE) for h in range(HP)]

    def kern(q_ref, k_ref, v_ref, c_ref, lse_ref, do_ref, dl_ref, dq_ref, dk_ref, dv_ref, dc_ref):
        j = pl.program_id(1)

        @pl.when(j == 0)
        def _():
            dq_ref[...] = jnp.zeros(dq_ref.shape, F32)

        kj = [k_ref[:, ln].astype(BF16) for ln in lanes]
        vj = [v_ref[:, ln].astype(BF16) for ln in lanes]
        joff = pl.multiple_of(j * bq, bq)
        ck = [c_ref[h, :, pl.ds(joff, bq)] for h in range(HP)]

        def step(i, carry, diagonal=False):
            off = pl.multiple_of(i * bq, bq)
            qs = [(q_ref[pl.ds(off, bq), ln] * scale).astype(BF16) for ln in lanes]
            dob = [do_ref[pl.ds(off, bq), ln] for ln in lanes]
            s = [_nt(qs[h], kj[h]) + (c_ref[h, :, pl.ds(off, bq)][:, 0:1] - ck[h]) for h in range(HP)]
            dp = [_nt(dob[h], vj[h]) for h in range(HP)]
            pb, dsb, dcs = [], [], []
            for h in range(HP):
                p = _attn_probs(s[h], lse_ref[h, pl.ds(off, bq), :], diagonal)
                ds = p * (dp[h] - dl_ref[h, pl.ds(off, bq), :])
                pb.append(p.astype(BF16))
                dsb.append(ds.astype(BF16))
                dcs.append(jnp.sum(ds, axis=0, keepdims=True))
            out = []
            for h, (dk, dv, dc) in enumerate(carry):
                dq_ref[pl.ds(off, bq), lanes[h]] += _nn(dsb[h], kj[h]) * scale
                out.append((dk + _tn(dsb[h], qs[h]), dv + _tn(pb[h], dob[h]), dc - dcs[h]))
            return tuple(out)

        init = tuple((jnp.zeros((bq, FOX_HD), F32), jnp.zeros((bq, FOX_HD), F32), jnp.zeros((1, bq), F32)) for _ in range(HP))
        res = lax.fori_loop(j + 1, nq, step, step(j, init, diagonal=True))
        for h, (dk, dv, dc) in enumerate(res):
            dk_ref[:, lanes[h]] = dk
            dv_ref[:, lanes[h]] = dv
            dc_ref[h] = dc

    W = HP * LANE
    full = lambda cb: pl.BlockSpec((T, W), lambda g, j: (0, cb // HP + g))
    blk = lambda cb: pl.BlockSpec((bq, W), lambda g, j: (j, cb // HP + g))
    col1 = pl.BlockSpec((HP, T, 1), lambda g, j: (g, 0, 0))
    return pl.pallas_call(
        kern, name="fox_attn_bwd",
        out_shape=[jax.ShapeDtypeStruct((T, 1024), F32)] * 3 + [jax.ShapeDtypeStruct((H, 1, T), F32)],
        grid=(H // HP, nq),
        in_specs=[full(0), blk(8), blk(16), pl.BlockSpec((HP, 1, T), lambda g, j: (g, 0, 0)), col1, full(0), col1],
        out_specs=[full(0), blk(0), blk(0), pl.BlockSpec((HP, 1, bq), lambda g, j: (g, 0, j))],
        compiler_params=_cp(("parallel", "arbitrary")),
    )(u, u, u, c3, lse, do, delta)


def _place():
    return lax.axis_index("x"), lax.axis_index("y"), lax.axis_index("c")


def _slot(p):
    return 4 * p[0] + 2 * p[1] + p[2]


def _other_chips(x, y):
    return [(1 - x, y), (x, 1 - y), (1 - x, 1 - y)]


def _allgather_steps(in_refs, out_refs, scratch):
    (src,), (dst,) = in_refs, out_refs
    send_sems, recv_sems, local_sem = scratch
    x, y, c = _place()
    me, sibling = (x, y, c), (x, y, 1 - c)
    chips = _other_chips(x, y)

    def copy(k, block, to, from_input=False):
        d = dst.at[_slot(block)]
        return pltpu.make_async_remote_copy(
            src_ref=src if from_input else d, dst_ref=d, send_sem=send_sems.at[k], recv_sem=recv_sems.at[k],
            device_id=to, device_id_type=MESH)

    def first_copies():
        return [copy(0, me, sibling, True)] + [copy(1 + j, me, (*chip, c), True) for j, chip in enumerate(chips)]

    def start():
        pltpu.make_async_copy(src, dst.at[_slot(me)], local_sem).start()
        for cp in first_copies():
            cp.start()

    def finish():
        passed = []
        for j, chip in enumerate(chips):
            copy(1 + j, (*chip, c), me).wait_recv()
            passed.append(copy(4 + j, (*chip, c), sibling))
            passed[-1].start()
        copy(0, sibling, me).wait_recv()
        for j, chip in enumerate(chips):
            copy(4 + j, (*chip, 1 - c), me).wait_recv()
        for cp in first_copies() + passed:
            cp.wait_send()
        pltpu.make_async_copy(src, dst.at[_slot(me)], local_sem).wait()

    return start, finish


def _allgather_relay_steps(in_refs, out_refs, scratch):
    (src,), (dst,) = in_refs, out_refs
    send_sems, recv_sems, local_sem = scratch
    x, y, c = _place()
    me, sibling = (x, y, c), (x, y, 1 - c)
    x_nbr, y_nbr, diag = (1 - x, y, c), (x, 1 - y, c), (1 - x, 1 - y, c)
    flip = lambda a, bit: a + bit - 2 * a * bit
    relay_from = (flip(x, 1 - c), flip(y, c), c)
    relay_to = (flip(x, c), flip(y, 1 - c), c)

    def copy(k, block, to, from_input=False):
        d = dst.at[_slot(block)]
        return pltpu.make_async_remote_copy(
            src_ref=src if from_input else d, dst_ref=d, send_sem=send_sems.at[k], recv_sem=recv_sems.at[k],
            device_id=to, device_id_type=MESH)

    def first_copies():
        return [copy(0, me, sibling, True), copy(1, me, x_nbr, True), copy(2, me, y_nbr, True)]

    def other(block):
        return block[:2] + (1 - c,)

    def start():
        pltpu.make_async_copy(src, dst.at[_slot(me)], local_sem).start()
        for cp in first_copies():
            cp.start()

    def finish():
        copy(1, x_nbr, me).wait_recv()
        copy(2, y_nbr, me).wait_recv()
        later = [copy(3, relay_from, relay_to), copy(4, x_nbr, sibling), copy(5, y_nbr, sibling)]
        for cp in later:
            cp.start()
        copy(3, diag, me).wait_recv()
        later.append(copy(6, diag, sibling))
        later[-1].start()
        copy(0, sibling, me).wait_recv()
        for k, block in ((4, x_nbr), (5, y_nbr), (6, diag)):
            copy(k, other(block), me).wait_recv()
        for cp in first_copies() + later:
            cp.wait_send()
        pltpu.make_async_copy(src, dst.at[_slot(me)], local_sem).wait()

    return start, finish


def _allgather_side(a, relay=False):
    return ((a,), (jax.ShapeDtypeStruct((N_DEV,) + a.shape, a.dtype),),
            (pltpu.SemaphoreType.DMA((7,)), pltpu.SemaphoreType.DMA((7,)), pltpu.SemaphoreType.DMA),
            _allgather_relay_steps if relay else _allgather_steps)


def _allgather(a, name, relay=False):
    ins, outs, scratch, make = _allgather_side(a, relay)

    def body(a_ref, o_ref, *scr):
        start, finish = make((a_ref,), (o_ref,), scr)
        start()
        finish()

    any_spec = pl.BlockSpec(memory_space=pl.ANY)
    return pl.pallas_call(body, name=name, out_shape=outs[0], in_specs=[any_spec], out_specs=any_spec,
                          scratch_shapes=list(scratch))(a)


def _exchange_pair_steps(in_refs, out_refs, scratch):
    (src,), (dst,) = in_refs, out_refs
    send_sems, recv_sems = scratch
    x, y, c = _place()
    sibling = (x, y, 1 - c)
    slots = [_slot(sibling)] + [_slot((*chip, 1 - c)) for chip in _other_chips(x, y)]

    def copies():
        return [pltpu.make_async_remote_copy(
            src_ref=src.at[ps], dst_ref=dst.at[k], send_sem=send_sems.at[k], recv_sem=recv_sems.at[k],
            device_id=sibling, device_id_type=MESH) for k, ps in enumerate(slots)]

    def start():
        for cp in copies():
            cp.start()

    def finish():
        for cp in copies():
            cp.wait()

    return start, finish


def _exchange_pair_side(g):
    return ((g,), (jax.ShapeDtypeStruct((4,) + g.shape[1:], g.dtype),),
            (pltpu.SemaphoreType.DMA((4,)), pltpu.SemaphoreType.DMA((4,))), _exchange_pair_steps)


def _exchange_pair(g):
    ins, outs, scratch, make = _exchange_pair_side(g)

    def body(g_ref, r_ref, *scr):
        start, finish = make((g_ref,), (r_ref,), scr)
        start()
        finish()

    any_spec = pl.BlockSpec(memory_space=pl.ANY)
    return pl.pallas_call(body, name="exchange_pair", out_shape=outs[0], in_specs=[any_spec], out_specs=any_spec,
                          scratch_shapes=list(scratch))(g)


def _tiling(R, Cc, tile, by_cols):
    if by_cols:
        assert Cc % tile == 0
        return Cc // tile, (R, tile), lambda lead, i: (lead, 0, i)
    assert R % tile == 0
    return R // tile, (tile, Cc), lambda lead, i: (lead, i, 0)


def _pair_add(name, g, r1, slots, tile, by_cols=False, side=None):
    _, R, Cc = g.shape
    steps, blk, at = _tiling(R, Cc, tile, by_cols)
    side_ins, side_outs, side_scr, side_make = side if side is not None else ((), (), (), None)
    n_si, n_so = len(side_ins), len(side_outs)

    def kern(s_ref, a_ref, b_ref, *rest):
        o_ref = rest[n_si]
        if side_make is not None:
            start, finish = side_make(rest[:n_si], rest[n_si + 1:n_si + 1 + n_so], rest[n_si + 1 + n_so:])
            j, i = pl.program_id(0), pl.program_id(1)
            pl.when(jnp.logical_and(j == 0, i == 0))(start)
        o_ref[...] = (a_ref[...].astype(F32) + b_ref[...].astype(F32)).astype(o_ref.dtype)
        if side_make is not None:
            pl.when(jnp.logical_and(j == 2, i == steps - 1))(finish)

    any_spec = pl.BlockSpec(memory_space=pl.ANY)
    res = pl.pallas_call(
        kern, name=name,
        out_shape=[jax.ShapeDtypeStruct((3, R, Cc), BF16)] + list(side_outs),
        grid_spec=pltpu.PrefetchScalarGridSpec(
            num_scalar_prefetch=1, grid=(3, steps),
            in_specs=[pl.BlockSpec((1,) + blk, lambda j, i, s: at(s[j], i)),
                      pl.BlockSpec((1,) + blk, lambda j, i, s: at(1 + j, i))] + [any_spec] * n_si,
            out_specs=[pl.BlockSpec((1,) + blk, lambda j, i, s: at(j, i))] + [any_spec] * n_so,
            scratch_shapes=list(side_scr)),
        compiler_params=_cp(("arbitrary", "arbitrary")),
    )(slots, g, r1, *side_ins)
    return res if side is not None else res[0]


def _axis_neighbours():
    x, y, c = _place()
    flip = lambda a, bit: a + bit - 2 * a * bit
    return (flip(x, c), flip(y, 1 - c), c), (flip(x, 1 - c), flip(y, c), c), c


def _exchange_first_steps(in_refs, out_refs, scratch):
    pairs = list(zip(in_refs, out_refs))
    send_sems, recv_sems = scratch
    first, _, c = _axis_neighbours()

    def copies():
        return [pltpu.make_async_remote_copy(
            src_ref=src.at[j], dst_ref=dst.at[k], send_sem=send_sems.at[t, k], recv_sem=recv_sems.at[t, k],
            device_id=first, device_id_type=MESH)
            for t, (src, dst) in enumerate(pairs) for k, j in enumerate((1 - c, 2))]

    def start():
        for cp in copies():
            cp.start()

    def finish():
        for cp in copies():
            cp.wait()

    return start, finish


def _exchange_first_side(*ss):
    n = len(ss)
    return (ss, tuple(jax.ShapeDtypeStruct((2,) + s.shape[1:], s.dtype) for s in ss),
            (pltpu.SemaphoreType.DMA((n, 2)), pltpu.SemaphoreType.DMA((n, 2))), _exchange_first_steps)


def _axis_add(name, s, t1, core, tile, by_cols=False):
    _, R, Cc = s.shape
    steps, blk, at = _tiling(R, Cc, tile, by_cols)

    def kern(c_ref, a_ref, b_ref, o_ref):
        o_ref[...] = (a_ref[...].astype(F32) + b_ref[...].astype(F32)).astype(o_ref.dtype)

    return pl.pallas_call(
        kern, name=name,
        out_shape=jax.ShapeDtypeStruct((1, R, Cc), BF16),
        grid_spec=pltpu.PrefetchScalarGridSpec(
            num_scalar_prefetch=1, grid=(steps,),
            in_specs=[pl.BlockSpec((1,) + blk, lambda i, cr: at(cr[0], i)),
                      pl.BlockSpec((1,) + blk, lambda i, cr: at(1, i))],
            out_specs=pl.BlockSpec((1,) + blk, lambda i, cr: at(0, i))),
        compiler_params=_cp(("arbitrary",)),
    )(core, s, t1)


def _exchange_second_steps(in_refs, out_refs, scratch):
    send_sems, recv_sems = scratch
    _, second, _ = _axis_neighbours()

    def copies():
        return [pltpu.make_async_remote_copy(src_ref=src, dst_ref=dst, send_sem=send_sems.at[t], recv_sem=recv_sems.at[t],
                                             device_id=second, device_id_type=MESH)
                for t, (src, dst) in enumerate(zip(in_refs, out_refs))]

    def start():
        for cp in copies():
            cp.start()

    def finish():
        for cp in copies():
            cp.wait()

    return start, finish


def _exchange_second_side(*ps):
    n = len(ps)
    return (ps, tuple(jax.ShapeDtypeStruct(p.shape, p.dtype) for p in ps),
            (pltpu.SemaphoreType.DMA((n,)), pltpu.SemaphoreType.DMA((n,))), _exchange_second_steps)


def _adamw(name, w, m, v, parts, tile, by_cols=False):
    R, Cc = w.shape
    steps, blk_shape, at = _tiling(R, Cc, tile, by_cols)
    n_parts = len(parts)

    def kern(*refs):
        w_ref, m_ref, v_ref = refs[:3]
        g = None
        for r_ref, (_, n) in zip(refs[3:3 + n_parts], parts):
            for s in range(n):
                term = r_ref[s].astype(F32)
                g = term if g is None else g + term
        g_out, d_out, m_out, v_out = refs[3 + n_parts:]
        mn = ADAM_B1 * m_ref[...] + (1.0 - ADAM_B1) * g
        vn = ADAM_B2 * v_ref[...] + (1.0 - ADAM_B2) * (g * g)
        m_hat = mn / (1.0 - ADAM_B1 ** ADAM_STEP)
        v_hat = vn / (1.0 - ADAM_B2 ** ADAM_STEP)
        g_out[...] = g
        d_out[...] = -ADAM_LR * (m_hat / (jnp.sqrt(v_hat) + ADAM_EPS) + ADAM_WD * w_ref[...])
        m_out[...] = mn
        v_out[...] = vn

    blk = pl.BlockSpec(blk_shape, lambda i: at(0, i)[1:])
    return pl.pallas_call(
        kern, name=name,
        out_shape=[jax.ShapeDtypeStruct((R, Cc), F32)] * 4,
        grid=(steps,),
        in_specs=[blk] * 3 + [pl.BlockSpec((n,) + blk_shape, lambda i: at(0, i)) for (_, n) in parts],
        out_specs=[blk] * 4,
        compiler_params=_cp(("arbitrary",)),
    )(w, m, v, *[a for (a, _) in parts])


def _assemble_columns(blocks, pieces, zeros, width):
    _, R, Cc = blocks.shape
    tr = min(256, R)

    def kern(b_ref, o_ref):
        for col, n in zeros:
            o_ref[:, col:col + n] = jnp.zeros((tr, n), o_ref.dtype)
        for col, d, lo, n in pieces:
            o_ref[:, col:col + n] = b_ref[d, :, lo:lo + n]

    return pl.pallas_call(
        kern, name="assemble_w_in",
        out_shape=jax.ShapeDtypeStruct((R, width), blocks.dtype),
        grid=(R // tr,),
        in_specs=[pl.BlockSpec((N_DEV, tr, Cc), lambda i: (0, i, 0))],
        out_specs=pl.BlockSpec((tr, width), lambda i: (i, 0)),
        compiler_params=_cp(("parallel",)),
    )(blocks)


def _split_rows(x, pieces, rows, side):
    _, Cc = x.shape
    tc = min(256, Cc)
    side_ins, side_outs, side_scr, side_make = side
    n_si, n_so = len(side_ins), len(side_outs)

    def kern(x_ref, *rest):
        o_ref = rest[n_si]
        start, finish = side_make(rest[:n_si], rest[n_si + 1:n_si + 1 + n_so], rest[n_si + 1 + n_so:])
        pl.when(pl.program_id(0) == 0)(start)
        for d, lo, row, n in pieces:
            o_ref[d, lo:lo + n, :] = x_ref[row:row + n, :]
        pl.when(pl.program_id(0) == Cc // tc - 1)(finish)

    any_spec = pl.BlockSpec(memory_space=pl.ANY)
    return pl.pallas_call(
        kern, name="split_w_in_grad",
        out_shape=[jax.ShapeDtypeStruct((N_DEV, rows, Cc), x.dtype)] + list(side_outs),
        grid=(Cc // tc,),
        in_specs=[pl.BlockSpec((x.shape[0], tc), lambda i: (0, i))] + [any_spec] * n_si,
        out_specs=[pl.BlockSpec((N_DEV, rows, tc), lambda i: (0, 0, i))] + [any_spec] * n_so,
        scratch_shapes=list(side_scr),
        compiler_params=_cp(("arbitrary",)),
    )(x, *side_ins)


def _pad_cols(a, w):
    return jnp.pad(a, ((0, 0), (0, w - a.shape[1])))


def _pad_rows(a, r):
    return jnp.pad(a, ((0, r - a.shape[0]), (0, 0)))


def _pack_b(pf, pr, wo, w2, a2, rows):
    body = jnp.concatenate([pf, pr, wo.reshape(2048, 256), jnp.concatenate([w2, a2], axis=1)], axis=0)
    return _pad_rows(body, rows)


def kernel(x, norm_gain, w_in, fox_forget_bias, rwkv_shift_mix, rwkv_w0, rwkv_w2, rwkv_a0, rwkv_a2, rwkv_k_k, rwkv_k_a, rwkv_r_k, rwkv_ln_w, rwkv_ln_b, w_proj_fox, w_proj_rwkv, w_out, final_norm_gain, loss_target, m_norm_gain, m_w_in, m_fox_forget_bias, m_rwkv_shift_mix, m_rwkv_w0, m_rwkv_w2, m_rwkv_a0, m_rwkv_a2, m_rwkv_k_k, m_rwkv_k_a, m_rwkv_r_k, m_rwkv_ln_w, m_rwkv_ln_b, m_w_proj_fox, m_w_proj_rwkv, m_w_out, m_final_norm_gain, v_norm_gain, v_w_in, v_fox_forget_bias, v_rwkv_shift_mix, v_rwkv_w0, v_rwkv_w2, v_rwkv_a0, v_rwkv_a2, v_rwkv_k_k, v_rwkv_k_a, v_rwkv_r_k, v_rwkv_ln_w, v_rwkv_ln_b, v_w_proj_fox, v_w_proj_rwkv, v_w_out, v_final_norm_gain):
    T, D = x.shape[1], x.shape[2]
    assert D == 2048 and T % LANE == 0
    NI = w_in.shape[2]
    IN = N_DEV * NI
    RB = 4224
    x2 = x[0]
    lt2 = loss_target[0]
    me = _slot(_place())

    tb = min(256, T)
    tbh = min(128, T)
    h, wa = _rowcall("rms_fwd", T, tb, [_rows(x2, tb, D), _whole(norm_gain)],
                     lambda i, xv, g: ([_rms_math(xv, g)], []), [(D, BF16)],
                     side=_allgather_side(w_in[0].astype(BF16), relay=True))
    packed_own = _pack_b(w_proj_fox[0], w_proj_rwkv[0], w_out[0], rwkv_w2[0], rwkv_a2[0], RB).astype(BF16)
    sections = [(0, 4096, 0), (4104, 4096, 4096), (8392, 4096, 8192), (4096, 8, 12288), (8200, 96, 12544), (8296, 96, 12672)]
    NP = 12800
    pieces, zeros, at_col = [], [], 0
    for lo, width, pad_lo in sections:
        if pad_lo > at_col:
            zeros.append((at_col, pad_lo - at_col))
        col = lo
        while col < lo + width:
            d = col // NI
            stop = min(lo + width, (d + 1) * NI)
            pieces.append((pad_lo + col - lo, d, col - d * NI, stop - col))
            col = stop
        at_col = pad_lo + width
    zeros.append((at_col, NP - at_col))
    w_pad = _assemble_columns(wa, pieces, zeros, NP)
    F_CB, LORA_CB = 96, 49

    mu = rwkv_shift_mix
    mu_main = mu[:, 0:4096]
    mu_lora = jnp.concatenate([_pad_cols(mu[:, 4096:4192], LANE), _pad_cols(mu[:, 4192:4288], LANE)], axis=1)
    bias_pad = _pad_cols(fox_forget_bias, LANE)
    rk_flat = rwkv_r_k.reshape(1, 1024)
    gf = final_norm_gain.reshape(1, D)

    u, wb = _mm(h, w_pad, tm=1024, tn=1280, name="mm_in", side=_allgather_side(packed_own))
    wpf = wb[:, 0:1024, :].transpose(1, 0, 2).reshape(1024, D)
    wpr = wb[:, 1024:2048, :].transpose(1, 0, 2).reshape(1024, D)
    wo = wb[:, 2048:4096, :].reshape(N_DEV * 256, D)
    w2p = _pad_rows(wb[:, 4096:4192, 0:128].transpose(1, 0, 2).reshape(96, 1024).astype(F32), LANE)
    a2p = _pad_rows(wb[:, 4096:4192, 128:256].transpose(1, 0, 2).reshape(96, 1024).astype(F32), LANE)

    c8 = _gates_fwd(u, bias_pad, T, F_CB)
    c3 = c8.reshape(8, 1, T)
    o_raw, o_a, lse = _attn_fwd(u, c3, T)

    def shift_body(i, um, hm, ul, hl, mm_, ml):
        outs = []
        for uv, hv, mv in ((um, hm, mm_), (ul, hl, ml)):
            hv = jnp.where(i == 0, 0.0, hv)
            prev = pltpu.roll(jnp.concatenate([hv, uv], axis=0), 1, 0)[SUB:]
            outs.append(uv + (prev - uv) * mv)
        return outs, []

    def halo_prev(arr, w, cb):
        return (arr, (SUB, w), lambda i: (jnp.maximum(i * (tbh // SUB) - 1, 0), cb))

    xs, xl = _rowcall("rwkv_shift_fwd", T, tbh,
                      [_rows(u, tbh, 4096, 1), halo_prev(u, 4096, 1), _rows(u, tbh, 256, LORA_CB), halo_prev(u, 256, LORA_CB),
                       _whole(mu_main), _whole(mu_lora)],
                      shift_body, [(4096, F32), (256, F32)])

    prep_par = [_whole(rwkv_w0), _whole(rwkv_a0), _whole(rwkv_k_k), _whole(rwkv_k_a), _whole(w2p), _whole(a2p)]
    prep_rows = [_rows(xs, tbh, 1024, 1), _rows(xl, tbh, LANE, 0), _rows(xl, tbh, LANE, 1)]
    lw, kp, an, bb = _rowcall("rwkv_prep_fwd", T, tbh, prep_rows + prep_par,
                              lambda i, *a: (list(_prep_math(*a)), []), [(1024, F32)] * 4)
    y, states, invs = _scan_fwd(xs, lw, kp, an, bb, T)
    post_rows = [_rows(y, tbh, 1024), _rows(xs, tbh, 1024, 0), _rows(kp, tbh, 1024), _rows(xs, tbh, 1024, 2), _rows(xs, tbh, 1024, 3)]
    post_par = [_whole(rwkv_ln_w), _whole(rwkv_ln_b), _whole(rk_flat)]
    (o_b,) = _rowcall("rwkv_post_fwd", T, tbh, post_rows + post_par,
                      lambda i, *a: ([_post_math(*a)], []), [(1024, BF16)])

    def merge_fwd_body(i, ga, gb, oav, obv, wpfv, wprv):
        pav, pbv = _nn(oav, wpfv).astype(BF16), _nn(obv, wprv).astype(BF16)
        return [pav, pbv, _merge_math(ga, gb, pav.astype(F32), pbv.astype(F32))], []

    pa, pb, mg = _rowcall("mm_proj_merge_fwd", T, tb,
                          [_rows(u, tb, D, 4), _rows(u, tb, D, 5), _rows(o_a, tb, 1024), _rows(o_b, tb, 1024), _whole(wpf), _whole(wpr)],
                          merge_fwd_body, [(D, BF16)] * 3)
    merge_rows = [_rows(u, tb, D, 4), _rows(u, tb, D, 5), _rows(pa, tb, D), _rows(pb, tb, D)]

    def head_body(i, xv, mgv, wov, ltv, g):
        out = xv + _nn(mgv, wov)
        r = lax.rsqrt(jnp.mean(out * out, axis=-1, keepdims=True) + RMS_EPS)
        yn = out * r
        err = yn * g - ltv
        loss = 0.5 * jnp.sum(jnp.sum(err * err, axis=-1, keepdims=True), axis=0, keepdims=True) / D
        dyv = err / D
        dyn = dyv * g
        dout = r * (dyn - yn * jnp.mean(dyn * yn, axis=-1, keepdims=True))
        return [dout], [loss, jnp.sum(dyv * yn, axis=0, keepdims=True)]

    dout, loss_p, dgf_p = _rowcall("mm_out_loss_head", T, tb,
                                   [_rows(x2, tb, D), _rows(mg, tb, D), _whole(wo), _rows(lt2, tb, D), _whole(gf)],
                                   head_body, [(D, F32)], [(1, 1), (1, D)])

    dwo = _mm(mg, dout, ta=True, out_dtype=BF16, name="mm_out_dw")

    def merge_bwd_body(i, ga, gb, pav, pbv, doutv, wov):
        _, vjp = jax.vjp(_merge_math, ga, gb, pav.astype(F32), pbv.astype(F32))
        dga, dgb, dpa, dpb = vjp(_nt(doutv.astype(BF16), wov))
        return [dga, dgb, dpa, dpb], []

    dga, dgb, dpa, dpb = _rowcall("mm_out_dx_merge_bwd", T, tb, merge_rows + [_rows(dout, tb, D), _whole(wo)], merge_bwd_body,
                                  [(D, BF16), (D, BF16), (D, BF16), (D, BF16)])
    dwpf = _mm(o_a, dpa, ta=True, out_dtype=BF16, name="mm_proj_fox_dw")
    dwpr = _mm(o_b, dpb, ta=True, out_dtype=BF16, name="mm_proj_rwkv_dw")

    do_b, dza, delta128 = _rowcall("mm_proj_fox_dx_attn_pre", T, tb,
                                   [_rows(dpa, tb, D), _whole(wpf), _rows(u, tb, 1024, 3), _rows(o_raw, tb, 1024)],
                                   lambda i, dpav, wv, z, o: (list(_attn_pre_math(_nt(dpav, wv), z, o)), []),
                                   [(1024, BF16), (1024, F32), (LANE, F32)])
    delta = delta128[:, 0:8].T.reshape(8, T, 1)
    dq, dk, dv, dc3 = _attn_bwd(u, c3, lse, do_b, delta, T)
    dfl, dbias_p = _gates_bwd(dc3.reshape(8, T), u, bias_pad, T, F_CB)

    def post_bwd_body(i, yv, rv, kpv, vv, zv, lnw, lnb, rkv, dpbv, wv):
        _, vjp = jax.vjp(_post_math, yv, rv, kpv, vv, zv, lnw, lnb, rkv)
        dy_, dr_, dkp_, dv_, dz_, dlnw, dlnb, drk = vjp(_nt(dpbv, wv))
        return [dy_, dr_, dkp_, dv_, dz_], [dlnw, dlnb, drk]

    dy_s, dr_p, dkp_p, dv_p, dzb, dlnw_p, dlnb_p, drk_p = _rowcall(
        "mm_proj_rwkv_dx_post_bwd", T, tbh, post_rows + post_par + [_rows(dpb, tbh, D), _whole(wpr)], post_bwd_body,
        [(1024, F32)] * 5, [(1, 1024)] * 3)
    dr_s, dlw, dkp_s, dv_s, dan, dbb = _scan_bwd(xs, lw, kp, an, bb, states, invs, dy_s, T)

    def prep_bwd_body(i, xk, xwd, xad, w0, a0, kkw, kaw, w2v, a2v, dlw_, dkp1, dkp2, dan_, dbb_, dr1, dr2, dv1, dv2, dz_):
        _, vjp = jax.vjp(_prep_math, xk, xwd, xad, w0, a0, kkw, kaw, w2v, a2v)
        dxk, dxwd, dxad, dw0, da0, dkk, dka, dw2, da2 = vjp((dlw_, dkp1 + dkp2, dan_, dbb_))
        return [[dr1 + dr2, dxk, dv1 + dv2, dz_], [dxwd, dxad]], [dw0, da0, dkk, dka, dw2, da2]

    cots = [dlw, dkp_s, dkp_p, dan, dbb, dr_s, dr_p, dv_s, dv_p, dzb]
    dxs, dxl, dw0_p, da0_p, dkk_p, dka_p, dw2_p, da2_p = _rowcall(
        "rwkv_prep_bwd", T, tbh, prep_rows + prep_par + [_rows(c_, tbh, 1024) for c_ in cots], prep_bwd_body,
        [(4096, F32), (256, F32)], [(1, 1024)] * 4 + [(LANE, 1024)] * 2)

    def shift_bwd_body(i, dm_, hm, dl_, hl, um, pm, ul, pl_, mm_, ml):
        last = i == T // tbh - 1
        outs, accs = [], []
        for dv_, hv, uv, pv, mv in ((dm_, hm, um, pm, mm_), (dl_, hl, ul, pl_, ml)):
            hv = jnp.where(last, 0.0, hv)
            nxt = pltpu.roll(jnp.concatenate([dv_, hv], axis=0), tbh + SUB - 1, 0)[:tbh]
            pv = jnp.where(i == 0, 0.0, pv)
            prev = pltpu.roll(jnp.concatenate([pv, uv], axis=0), 1, 0)[SUB:]
            outs.append(dv_ * (1.0 - mv) + nxt * mv)
            accs.append(jnp.sum(dv_ * (prev - uv), axis=0, keepdims=True))
        return outs, accs

    def halo_next(arr, w, cb):
        last_blk = T // SUB - 1
        return (arr, (SUB, w), lambda i: (jnp.minimum((i + 1) * (tbh // SUB), last_blk), cb))

    du_b, du_l, dmu_main_p, dmu_lora_p = _rowcall(
        "rwkv_shift_bwd", T, tbh,
        [_rows(dxs, tbh, 4096), halo_next(dxs, 4096, 0), _rows(dxl, tbh, 256), halo_next(dxl, 256, 0),
         _rows(u, tbh, 4096, 1), halo_prev(u, 4096, 1), _rows(u, tbh, 256, LORA_CB), halo_prev(u, 256, LORA_CB),
         _whole(mu_main), _whole(mu_lora)],
        shift_bwd_body, [(4096, BF16), (256, BF16)], [(1, 4096), (1, 256)])

    lora_g = jnp.concatenate([dw2_p[:96].reshape(96, N_DEV, 128).transpose(1, 0, 2),
                              da2_p[:96].reshape(96, N_DEV, 128).transpose(1, 0, 2)], axis=2).astype(BF16)
    gb = jnp.concatenate([dwpf.reshape(1024, N_DEV, 256).transpose(1, 0, 2),
                          dwpr.reshape(1024, N_DEV, 256).transpose(1, 0, 2),
                          dwo.reshape(N_DEV, 2048, 256), lora_g, jnp.zeros((N_DEV, RB - 4192, 256), BF16)], axis=1)
    xx, yy, cc = _place()
    chip_slots = jnp.stack([_slot((*chip, cc)) for chip in _other_chips(xx, yy)]).astype(jnp.int32)
    core = jnp.stack([cc]).astype(jnp.int32)
    du, r1b = _rowcall("assemble_du", T, tb,
                       [_rows(a_, tb, a_.shape[1]) for a_ in (dq, dk, dv, dza, du_b, dga, dgb, dfl, du_l)],
                       lambda i, *a: ([list(a[:8]) + [jnp.zeros((tb, LANE), BF16), a[8]]], []), [(NP, BF16)],
                       side=_exchange_pair_side(gb))
    sb = _pair_add("pair_add_packed", gb, r1b, chip_slots, RB // 3)
    dw_pad_t, t1b = _mm(du, h, ta=True, out_dtype=BF16, tm=1280, tn=1024, name="mm_in_dw",
                        side=_exchange_first_side(sb))
    pb = _axis_add("axis_add_packed", sb, t1b, core, RB // 3)
    ga, t2b = _split_rows(dw_pad_t, [(d, lo, pad_col, n) for pad_col, d, lo, n in pieces], NI,
                          _exchange_second_side(pb))
    r1a = _exchange_pair(ga)
    sa = _pair_add("pair_add_w_in", ga, r1a, chip_slots, 512, by_cols=True)
    ga_own = lax.dynamic_index_in_dim(ga, me, 0, keepdims=True)
    gb_own = lax.dynamic_index_in_dim(gb, me, 0, keepdims=True)
    dh, t1a = _mm(du, w_pad, tb=True, tm=1024, tn=1024, tk=NP // 10, name="mm_in_dx", side=_exchange_first_side(sa))
    pa = _axis_add("axis_add_w_in", sa, t1a, core, 512, by_cols=True)

    def rms_bwd_body(i, xv, g, dhv, doutv):
        _, vjp = jax.vjp(_rms_math, xv, g)
        dx_, dg_ = vjp(dhv)
        return [dx_ + doutv], [dg_]

    grad_x2, dng_p, t2a = _rowcall(
        "rms_bwd", T, tb, [_rows(x2, tb, D), _whole(norm_gain), _rows(dh, tb, D), _rows(dout, tb, D)],
        rms_bwd_body, [(D, F32)], [(1, D)], side=_exchange_second_side(pa))

    w_in_outs = _adamw("adamw_w_in", w_in[0].T, m_w_in[0].T, v_w_in[0].T, [(ga_own, 1), (r1a, 1), (t1a, 1), (t2a, 1)], 256, by_cols=True)
    g_in, d_in, m_in, v_in = [o.T for o in w_in_outs]

    dmu = jnp.concatenate([dmu_main_p, dmu_lora_p[:, 0:96], dmu_lora_p[:, 128:224]], axis=1)
    small_parts = [dng_p, dbias_p[:, 0:8], dmu, dw0_p, da0_p, dkk_p, dka_p, drk_p, dlnw_p, dlnb_p, dgf_p, loss_p]
    SR = 128
    small = _pad_cols(jnp.concatenate(small_parts, axis=1), SR * LANE).reshape(SR, LANE)
    rs = _allgather(small, "allgather_small")
    pk = lambda pf, pr, wo_, w2_, a2_: _pack_b(pf[0], pr[0], wo_[0], w2_[0], a2_[0], RB)
    outs_b = _adamw("adamw_packed", pk(w_proj_fox, w_proj_rwkv, w_out, rwkv_w2, rwkv_a2),
                    pk(m_w_proj_fox, m_w_proj_rwkv, m_w_out, m_rwkv_w2, m_rwkv_a2),
                    pk(v_w_proj_fox, v_w_proj_rwkv, v_w_out, v_rwkv_w2, v_rwkv_a2), [(gb_own, 1), (r1b, 1), (t1b, 1), (t2b, 1)], RB // 3)

    def pack_small(ng, fb, sm, w0, a0, kk_, ka_, rk_, lnw, lnb, fg):
        parts = [ng, fb, sm, w0, a0, kk_, ka_, rk_.reshape(1, 1024), lnw, lnb, fg.reshape(1, D), jnp.zeros((1, 1), F32)]
        return _pad_cols(jnp.concatenate(parts, axis=1), SR * LANE).reshape(SR, LANE)

    outs_s = _adamw("adamw_small",
                    pack_small(norm_gain, fox_forget_bias, rwkv_shift_mix, rwkv_w0, rwkv_a0, rwkv_k_k, rwkv_k_a, rwkv_r_k,
                               rwkv_ln_w, rwkv_ln_b, final_norm_gain),
                    pack_small(m_norm_gain, m_fox_forget_bias, m_rwkv_shift_mix, m_rwkv_w0, m_rwkv_a0, m_rwkv_k_k, m_rwkv_k_a,
                               m_rwkv_r_k, m_rwkv_ln_w, m_rwkv_ln_b, m_final_norm_gain),
                    pack_small(v_norm_gain, v_fox_forget_bias, v_rwkv_shift_mix, v_rwkv_w0, v_rwkv_a0, v_rwkv_k_k, v_rwkv_k_a,
                               v_rwkv_r_k, v_rwkv_ln_w, v_rwkv_ln_b, v_final_norm_gain),
                    [(rs, N_DEV)], SR)

    def unpack_b(pkd):
        return dict(w_proj_fox=pkd[0:1024][None], w_proj_rwkv=pkd[1024:2048][None], w_out=pkd[2048:4096].reshape(1, 256, D),
                    rwkv_w2=pkd[4096:4192, 0:128][None], rwkv_a2=pkd[4096:4192, 128:256][None])

    def unpack_s(pkd):
        flat = pkd.reshape(1, SR * LANE)
        names = [("norm_gain", D), ("fox_forget_bias", 8), ("rwkv_shift_mix", 4288), ("rwkv_w0", 1024), ("rwkv_a0", 1024),
                 ("rwkv_k_k", 1024), ("rwkv_k_a", 1024), ("rwkv_r_k", 1024), ("rwkv_ln_w", 1024), ("rwkv_ln_b", 1024),
                 ("final_norm_gain", D), ("loss", 1)]
        out, off = {}, 0
        for nm, n in names:
            out[nm] = flat[:, off:off + n]
            off += n
        out["rwkv_r_k"] = out["rwkv_r_k"].reshape(1, 16, 64)
        out["final_norm_gain"] = out["final_norm_gain"].reshape(D)
        return out

    order = ["norm_gain", "w_in", "fox_forget_bias", "rwkv_shift_mix", "rwkv_w0", "rwkv_w2", "rwkv_a0", "rwkv_a2", "rwkv_k_k",
             "rwkv_k_a", "rwkv_r_k", "rwkv_ln_w", "rwkv_ln_b", "w_proj_fox", "w_proj_rwkv", "w_out", "final_norm_gain"]
    result = []
    loss = None
    for kind, big in enumerate((g_in, d_in, m_in, v_in)):
        d = {**unpack_b(outs_b[kind]), **unpack_s(outs_s[kind]), "w_in": big[None]}
        if kind == 0:
            loss = d["loss"].reshape(())
        result += [d[n] for n in order]
    return (loss, grad_x2[None], *result)
```
